```python
import jax, jax.numpy as jnp
from jax import lax
import numpy as np

D_MODEL = 1024
BATCH = 8
SEQ = 2048
DEPTH = 1

CHUNK = 64
PLE_DIM = 256
EPS = 1e-6
GMLP_GROUPS = 8
GMLP_GROUP_DIM = 128
GMLP_WIDTH = GMLP_GROUPS * GMLP_GROUP_DIM
GMLP_BLOCK = 128
FOX_HEADS = 16
FOX_HEAD_DIM = 64
FOX_WIDTH = FOX_HEADS * FOX_HEAD_DIM
Q_BLOCK = 128
D_FF = 2816
CONV_WIDTH = 3
N_BRANCH = 2
IN_COLS = 2 * GMLP_WIDTH + 3 * FOX_WIDTH + FOX_HEADS + N_BRANCH * D_MODEL

kernel_name = "hybrid_gmlp_fox_convffn_block"


def rmsnorm(x, g):
    x32 = x.astype(jnp.float32)
    y = x32 * lax.rsqrt(jnp.mean(x32 * x32, axis=-1, keepdims=True) + EPS)
    return (y * g.astype(jnp.float32)).astype(x.dtype)


def layernorm(x, g, b):
    x32 = x.astype(jnp.float32)
    mu = jnp.mean(x32, axis=-1, keepdims=True)
    xc = x32 - mu
    y = xc * lax.rsqrt(jnp.mean(xc * xc, axis=-1, keepdims=True) + EPS)
    return (y * g.astype(jnp.float32) + b.astype(jnp.float32)).astype(x.dtype)


def gmlp_spatial_gating(z_u, z_v, ln_g, ln_b, w_s, b_s):
    B, S, _ = z_u.shape
    v = layernorm(z_v, ln_g, ln_b)
    v = v.reshape(B, S // GMLP_BLOCK, GMLP_BLOCK, GMLP_GROUPS, GMLP_GROUP_DIM)
    pos = jnp.arange(GMLP_BLOCK)
    mask = (pos[None, :] // CHUNK) <= (pos[:, None] // CHUNK)
    w = jnp.where(mask[None], w_s, jnp.zeros_like(w_s))
    mixed = jnp.einsum('gts,bnsgc->bntgc', w, v) + b_s.T[None, None, :, :, None]
    return z_u * mixed.reshape(B, S, GMLP_WIDTH)


def forgetting_attention(q, k, v, f_logit, b_f):
    B, S, _ = q.shape
    def heads(t):
        return t.reshape(B, S, FOX_HEADS, FOX_HEAD_DIM).transpose(0, 2, 1, 3)
    q, k, v = heads(q), heads(k), heads(v)
    log_f = jax.nn.log_sigmoid(f_logit.astype(jnp.float32) + b_f.astype(jnp.float32))
    cum = jnp.cumsum(log_f, axis=1).transpose(0, 2, 1)
    scale = FOX_HEAD_DIM ** -0.5
    outs = []
    for i in range(S // Q_BLOCK):
        lo, hi = i * Q_BLOCK, (i + 1) * Q_BLOCK
        s = jnp.einsum('bhqd,bhkd->bhqk', q[:, :, lo:hi], k[:, :, :hi]).astype(jnp.float32) * scale
        s = s + cum[:, :, lo:hi, None] - cum[:, :, None, :hi]
        qpos = jnp.arange(lo, hi)
        kpos = jnp.arange(hi)
        s = jnp.where(kpos[None, :] <= qpos[:, None], s, -1e30)
        prob = jax.nn.softmax(s, axis=-1).astype(v.dtype)
        outs.append(jnp.einsum('bhqk,bhkd->bhqd', prob, v[:, :, :hi]))
    o = jnp.concatenate(outs, axis=2)
    return o.transpose(0, 2, 1, 3).reshape(B, S, FOX_WIDTH)


def causal_depthwise_conv(u, w, b):
    S = u.shape[1]
    up = jnp.pad(u, ((0, 0), (CONV_WIDTH - 1, 0), (0, 0)))
    out = b + w[0] * up[:, 0:S]
    for j in range(1, CONV_WIDTH):
        out = out + w[j] * up[:, j:j + S]
    return out


def _fwd_setup_inputs(seed: int = 0) -> dict:
    key = jax.random.key(seed)
    ks = jax.random.split(key, 24)
    f32 = jnp.float32
    def nrm(k, shape, scale):
        return jax.random.normal(k, shape, f32) * scale
    L = DEPTH
    return {
        "x": nrm(ks[0], (BATCH, SEQ, D_MODEL), 1.0),
        "p": nrm(ks[1], (DEPTH, BATCH, SEQ, PLE_DIM), 1.0),
        "norm_mix_g": 1.0 + nrm(ks[2], (L, D_MODEL), 0.02),
        "w_in": nrm(ks[3], (L, D_MODEL, IN_COLS), D_MODEL ** -0.5),
        "b_f": 2.0 + nrm(ks[4], (L, FOX_HEADS), 0.5),
        "gmlp_ln_g": 1.0 + nrm(ks[5], (L, GMLP_WIDTH), 0.02),
        "gmlp_ln_b": nrm(ks[6], (L, GMLP_WIDTH), 0.02),
        "gmlp_w_s": nrm(ks[7], (L, GMLP_GROUPS, GMLP_BLOCK, GMLP_BLOCK), GMLP_BLOCK ** -0.5),
        "gmlp_b_s": 1.0 + nrm(ks[8], (L, GMLP_GROUPS, GMLP_BLOCK), 0.1),
        "w_branch_a": nrm(ks[9], (L, GMLP_WIDTH, D_MODEL), GMLP_WIDTH ** -0.5),
        "w_branch_b": nrm(ks[10], (L, FOX_WIDTH, D_MODEL), FOX_WIDTH ** -0.5),
        "w_out": nrm(ks[11], (L, D_MODEL, D_MODEL), D_MODEL ** -0.5),
        "norm_ffn_g": 1.0 + nrm(ks[12], (L, D_MODEL), 0.02),
        "w_up": nrm(ks[13], (L, D_MODEL, 2 * D_FF), D_MODEL ** -0.5),
        "conv_w": nrm(ks[14], (L, CONV_WIDTH, 2 * D_FF), CONV_WIDTH ** -0.5),
        "conv_b": nrm(ks[15], (L, 2 * D_FF), 0.02),
        "w_down": nrm(ks[16], (L, D_FF, D_MODEL), D_FF ** -0.5),
        "norm_ple_g": 1.0 + nrm(ks[17], (L, D_MODEL), 0.02),
        "w_ple": nrm(ks[18], (L, PLE_DIM, D_MODEL), PLE_DIM ** -0.5),
        "w_ple_gate": nrm(ks[19], (L, D_MODEL, D_MODEL), D_MODEL ** -0.5),
        "norm_final_g": 1.0 + nrm(ks[20], (D_MODEL,), 0.02),
    }


def _fwd_reference(x, p, norm_mix_g, w_in, b_f, gmlp_ln_g, gmlp_ln_b, gmlp_w_s, gmlp_b_s,
              w_branch_a, w_branch_b, w_out, norm_ffn_g, w_up, conv_w, conv_b, w_down,
              norm_ple_g, w_ple, w_ple_gate, norm_final_g):
    o1 = 2 * GMLP_WIDTH
    o2 = o1 + 3 * FOX_WIDTH
    o3 = o2 + FOX_HEADS
    for i in range(DEPTH):
        h = rmsnorm(x, norm_mix_g[i])
        z = jnp.einsum('bsd,dc->bsc', h, w_in[i])
        uv = jax.nn.gelu(z[..., :o1])
        z_u, z_v = uv[..., :GMLP_WIDTH], uv[..., GMLP_WIDTH:]
        q = z[..., o1:o1 + FOX_WIDTH]
        k = z[..., o1 + FOX_WIDTH:o1 + 2 * FOX_WIDTH]
        v = z[..., o1 + 2 * FOX_WIDTH:o2]
        f_logit = z[..., o2:o3]
        gates = jax.nn.sigmoid(z[..., o3:])
        gate_a, gate_b = gates[..., :D_MODEL], gates[..., D_MODEL:]

        a = gmlp_spatial_gating(z_u, z_v, gmlp_ln_g[i], gmlp_ln_b[i], gmlp_w_s[i], gmlp_b_s[i])
        b = forgetting_attention(q, k, v, f_logit, b_f[i])
        y_a = jnp.einsum('bsc,cd->bsd', a, w_branch_a[i])
        y_b = jnp.einsum('bsc,cd->bsd', b, w_branch_b[i])
        merged = gate_a * y_a + gate_b * y_b
        x = x + jnp.einsum('bsd,de->bse', merged, w_out[i])

        h2 = rmsnorm(x, norm_ffn_g[i])
        up = jnp.einsum('bsd,df->bsf', h2, w_up[i])
        up = causal_depthwise_conv(up, conv_w[i], conv_b[i])
        act = jax.nn.gelu(up[..., :D_FF]) * up[..., D_FF:]
        x = x + jnp.einsum('bsf,fd->bsd', act, w_down[i])

        h3 = rmsnorm(x, norm_ple_g[i])
        ple = jnp.einsum('bse,ed->bsd', p[i], w_ple[i])
        x = x + ple * jax.nn.sigmoid(jnp.einsum('bsd,de->bse', h3, w_ple_gate[i]))
    return rmsnorm(x, norm_final_g)


import jax as _jax
import jax.numpy as _jnp

TWIN_FORMAT = 'train_step'
FWD_PARAMS = ['x', 'p', 'norm_mix_g', 'w_in', 'b_f', 'gmlp_ln_g', 'gmlp_ln_b', 'gmlp_w_s', 'gmlp_b_s', 'w_branch_a', 'w_branch_b', 'w_out', 'norm_ffn_g', 'w_up', 'conv_w', 'conv_b', 'w_down', 'norm_ple_g', 'w_ple', 'w_ple_gate', 'norm_final_g']
TWIN_WEIGHTS = ['norm_mix_g', 'w_in', 'b_f', 'gmlp_ln_g', 'gmlp_ln_b', 'gmlp_w_s', 'gmlp_b_s', 'w_branch_a', 'w_branch_b', 'w_out', 'norm_ffn_g', 'w_up', 'conv_w', 'conv_b', 'w_down', 'norm_ple_g', 'w_ple', 'w_ple_gate', 'norm_final_g']
TWIN_DIFF_INPUT = 'x'
TWIN_INPUTS = ['x', 'p', 'norm_mix_g', 'w_in', 'b_f', 'gmlp_ln_g', 'gmlp_ln_b', 'gmlp_w_s', 'gmlp_b_s', 'w_branch_a', 'w_branch_b', 'w_out', 'norm_ffn_g', 'w_up', 'conv_w', 'conv_b', 'w_down', 'norm_ple_g', 'w_ple', 'w_ple_gate', 'norm_final_g', 'loss_target', 'm_norm_mix_g', 'm_w_in', 'm_b_f', 'm_gmlp_ln_g', 'm_gmlp_ln_b', 'm_gmlp_w_s', 'm_gmlp_b_s', 'm_w_branch_a', 'm_w_branch_b', 'm_w_out', 'm_norm_ffn_g', 'm_w_up', 'm_conv_w', 'm_conv_b', 'm_w_down', 'm_norm_ple_g', 'm_w_ple', 'm_w_ple_gate', 'm_norm_final_g', 'v_norm_mix_g', 'v_w_in', 'v_b_f', 'v_gmlp_ln_g', 'v_gmlp_ln_b', 'v_gmlp_w_s', 'v_gmlp_b_s', 'v_w_branch_a', 'v_w_branch_b', 'v_w_out', 'v_norm_ffn_g', 'v_w_up', 'v_conv_w', 'v_conv_b', 'v_w_down', 'v_norm_ple_g', 'v_w_ple', 'v_w_ple_gate', 'v_norm_final_g']
TWIN_OUTPUTS = ['loss', 'grad_x', 'grad_norm_mix_g', 'grad_w_in', 'grad_b_f', 'grad_gmlp_ln_g', 'grad_gmlp_ln_b', 'grad_gmlp_w_s', 'grad_gmlp_b_s', 'grad_w_branch_a', 'grad_w_branch_b', 'grad_w_out', 'grad_norm_ffn_g', 'grad_w_up', 'grad_conv_w', 'grad_conv_b', 'grad_w_down', 'grad_norm_ple_g', 'grad_w_ple', 'grad_w_ple_gate', 'grad_norm_final_g', 'delta_norm_mix_g', 'delta_w_in', 'delta_b_f', 'delta_gmlp_ln_g', 'delta_gmlp_ln_b', 'delta_gmlp_w_s', 'delta_gmlp_b_s', 'delta_w_branch_a', 'delta_w_branch_b', 'delta_w_out', 'delta_norm_ffn_g', 'delta_w_up', 'delta_conv_w', 'delta_conv_b', 'delta_w_down', 'delta_norm_ple_g', 'delta_w_ple', 'delta_w_ple_gate', 'delta_norm_final_g', 'new_m_norm_mix_g', 'new_m_w_in', 'new_m_b_f', 'new_m_gmlp_ln_g', 'new_m_gmlp_ln_b', 'new_m_gmlp_w_s', 'new_m_gmlp_b_s', 'new_m_w_branch_a', 'new_m_w_branch_b', 'new_m_w_out', 'new_m_norm_ffn_g', 'new_m_w_up', 'new_m_conv_w', 'new_m_conv_b', 'new_m_w_down', 'new_m_norm_ple_g', 'new_m_w_ple', 'new_m_w_ple_gate', 'new_m_norm_final_g', 'new_v_norm_mix_g', 'new_v_w_in', 'new_v_b_f', 'new_v_gmlp_ln_g', 'new_v_gmlp_ln_b', 'new_v_gmlp_w_s', 'new_v_gmlp_b_s', 'new_v_w_branch_a', 'new_v_w_branch_b', 'new_v_w_out', 'new_v_norm_ffn_g', 'new_v_w_up', 'new_v_conv_w', 'new_v_conv_b', 'new_v_w_down', 'new_v_norm_ple_g', 'new_v_w_ple', 'new_v_w_ple_gate', 'new_v_norm_final_g']
TWIN_LEAF_KINDS = {'loss': 'loss', 'grad_x': 'grad_x', 'grad_norm_mix_g': 'grad_w', 'grad_w_in': 'grad_w', 'grad_b_f': 'grad_w', 'grad_gmlp_ln_g': 'grad_w', 'grad_gmlp_ln_b': 'grad_w', 'grad_gmlp_w_s': 'grad_w', 'grad_gmlp_b_s': 'grad_w', 'grad_w_branch_a': 'grad_w', 'grad_w_branch_b': 'grad_w', 'grad_w_out': 'grad_w', 'grad_norm_ffn_g': 'grad_w', 'grad_w_up': 'grad_w', 'grad_conv_w': 'grad_w', 'grad_conv_b': 'grad_w', 'grad_w_down': 'grad_w', 'grad_norm_ple_g': 'grad_w', 'grad_w_ple': 'grad_w', 'grad_w_ple_gate': 'grad_w', 'grad_norm_final_g': 'grad_w', 'delta_norm_mix_g': 'delta_w', 'delta_w_in': 'delta_w', 'delta_b_f': 'delta_w', 'delta_gmlp_ln_g': 'delta_w', 'delta_gmlp_ln_b': 'delta_w', 'delta_gmlp_w_s': 'delta_w', 'delta_gmlp_b_s': 'delta_w', 'delta_w_branch_a': 'delta_w', 'delta_w_branch_b': 'delta_w', 'delta_w_out': 'delta_w', 'delta_norm_ffn_g': 'delta_w', 'delta_w_up': 'delta_w', 'delta_conv_w': 'delta_w', 'delta_conv_b': 'delta_w', 'delta_w_down': 'delta_w', 'delta_norm_ple_g': 'delta_w', 'delta_w_ple': 'delta_w', 'delta_w_ple_gate': 'delta_w', 'delta_norm_final_g': 'delta_w', 'new_m_norm_mix_g': 'new_m', 'new_m_w_in': 'new_m', 'new_m_b_f': 'new_m', 'new_m_gmlp_ln_g': 'new_m', 'new_m_gmlp_ln_b': 'new_m', 'new_m_gmlp_w_s': 'new_m', 'new_m_gmlp_b_s': 'new_m', 'new_m_w_branch_a': 'new_m', 'new_m_w_branch_b': 'new_m', 'new_m_w_out': 'new_m', 'new_m_norm_ffn_g': 'new_m', 'new_m_w_up': 'new_m', 'new_m_conv_w': 'new_m', 'new_m_conv_b': 'new_m', 'new_m_w_down': 'new_m', 'new_m_norm_ple_g': 'new_m', 'new_m_w_ple': 'new_m', 'new_m_w_ple_gate': 'new_m', 'new_m_norm_final_g': 'new_m', 'new_v_norm_mix_g': 'new_v', 'new_v_w_in': 'new_v', 'new_v_b_f': 'new_v', 'new_v_gmlp_ln_g': 'new_v', 'new_v_gmlp_ln_b': 'new_v', 'new_v_gmlp_w_s': 'new_v', 'new_v_gmlp_b_s': 'new_v', 'new_v_w_branch_a': 'new_v', 'new_v_w_branch_b': 'new_v', 'new_v_w_out': 'new_v', 'new_v_norm_ffn_g': 'new_v', 'new_v_w_up': 'new_v', 'new_v_conv_w': 'new_v', 'new_v_conv_b': 'new_v', 'new_v_w_down': 'new_v', 'new_v_norm_ple_g': 'new_v', 'new_v_w_ple': 'new_v', 'new_v_w_ple_gate': 'new_v', 'new_v_norm_final_g': 'new_v'}


def _forward(args):
    return _fwd_reference(*[args[k] for k in FWD_PARAMS])


def _output_shape():
    out = _jax.eval_shape(lambda: _forward(_fwd_setup_inputs(0)))
    return out.shape, out.dtype

N_MICROBATCH = 1
ADAM_LR = 0.001
ADAM_B1 = 0.9
ADAM_B2 = 0.999
ADAM_EPS = 1e-08
ADAM_WD = 0.01
ADAM_STEP = 10
PER_EXAMPLE_BATCH_AXIS = {'x': 0, 'p': 1, 'loss_target': 0}
SHARED_INPUTS = []
_WEIGHT_DTYPES = {'norm_mix_g': _jnp.float32, 'w_in': _jnp.float32, 'b_f': _jnp.float32, 'gmlp_ln_g': _jnp.float32, 'gmlp_ln_b': _jnp.float32, 'gmlp_w_s': _jnp.float32, 'gmlp_b_s': _jnp.float32, 'w_branch_a': _jnp.float32, 'w_branch_b': _jnp.float32, 'w_out': _jnp.float32, 'norm_ffn_g': _jnp.float32, 'w_up': _jnp.float32, 'conv_w': _jnp.float32, 'conv_b': _jnp.float32, 'w_down': _jnp.float32, 'norm_ple_g': _jnp.float32, 'w_ple': _jnp.float32, 'w_ple_gate': _jnp.float32, 'norm_final_g': _jnp.float32}
MOMENT_SCALE = {'norm_mix_g': 8.633929e-02, 'w_in': 3.178509e-02, 'b_f': 1.817029e-01, 'gmlp_ln_g': 3.543209e-02, 'gmlp_ln_b': 3.653301e-02, 'gmlp_w_s': 3.588814e-02, 'gmlp_b_s': 4.178612e-02, 'w_branch_a': 5.531593e-02, 'w_branch_b': 2.755894e-02, 'w_out': 6.164639e-02, 'norm_ffn_g': 8.486139e-02, 'w_up': 3.625565e-02, 'conv_w': 3.842585e-02, 'conv_b': 3.642360e-02, 'w_down': 5.913411e-02, 'norm_ple_g': 2.008144e-02, 'w_ple': 4.827149e-02, 'w_ple_gate': 1.896445e-02, 'norm_final_g': 1.601485e+01}


def _to_microbatches(a, axis):
    t = _jnp.moveaxis(a, axis, 0)
    t = t.reshape((N_MICROBATCH, t.shape[0] // N_MICROBATCH) + t.shape[1:])
    return _jnp.moveaxis(t, 1, axis + 1)


def setup_inputs(seed: int = 0) -> dict:
    inp = _fwd_setup_inputs(seed)
    key = _jax.random.fold_in(_jax.random.key(seed), 7919)
    shape, _ = _output_shape()
    out = dict(inp)
    out["loss_target"] = _jax.random.normal(_jax.random.fold_in(key, 0), shape, _jnp.float32)
    for i, name in enumerate(TWIN_WEIGHTS):
        w = inp[name].astype(_jnp.float32)
        if MOMENT_SCALE is None:
            s = _jnp.sqrt(_jnp.mean(_jnp.square(w)) + 1e-30)
        else:
            s = MOMENT_SCALE[name]
        km, kv = _jax.random.split(_jax.random.fold_in(key, i + 1))
        out[name] = w
        out["m_" + name] = s * _jax.random.normal(km, w.shape, _jnp.float32)
        out["v_" + name] = (s * s) * _jax.random.uniform(kv, w.shape, _jnp.float32, 0.5, 1.5)
    if N_MICROBATCH > 1:
        for name, axis in PER_EXAMPLE_BATCH_AXIS.items():
            out[name] = _to_microbatches(out[name], axis)
    return {'x': out['x'], 'p': out['p'], 'norm_mix_g': out['norm_mix_g'], 'w_in': out['w_in'], 'b_f': out['b_f'], 'gmlp_ln_g': out['gmlp_ln_g'], 'gmlp_ln_b': out['gmlp_ln_b'], 'gmlp_w_s': out['gmlp_w_s'], 'gmlp_b_s': out['gmlp_b_s'], 'w_branch_a': out['w_branch_a'], 'w_branch_b': out['w_branch_b'], 'w_out': out['w_out'], 'norm_ffn_g': out['norm_ffn_g'], 'w_up': out['w_up'], 'conv_w': out['conv_w'], 'conv_b': out['conv_b'], 'w_down': out['w_down'], 'norm_ple_g': out['norm_ple_g'], 'w_ple': out['w_ple'], 'w_ple_gate': out['w_ple_gate'], 'norm_final_g': out['norm_final_g'], 'loss_target': out['loss_target'], 'm_norm_mix_g': out['m_norm_mix_g'], 'm_w_in': out['m_w_in'], 'm_b_f': out['m_b_f'], 'm_gmlp_ln_g': out['m_gmlp_ln_g'], 'm_gmlp_ln_b': out['m_gmlp_ln_b'], 'm_gmlp_w_s': out['m_gmlp_w_s'], 'm_gmlp_b_s': out['m_gmlp_b_s'], 'm_w_branch_a': out['m_w_branch_a'], 'm_w_branch_b': out['m_w_branch_b'], 'm_w_out': out['m_w_out'], 'm_norm_ffn_g': out['m_norm_ffn_g'], 'm_w_up': out['m_w_up'], 'm_conv_w': out['m_conv_w'], 'm_conv_b': out['m_conv_b'], 'm_w_down': out['m_w_down'], 'm_norm_ple_g': out['m_norm_ple_g'], 'm_w_ple': out['m_w_ple'], 'm_w_ple_gate': out['m_w_ple_gate'], 'm_norm_final_g': out['m_norm_final_g'], 'v_norm_mix_g': out['v_norm_mix_g'], 'v_w_in': out['v_w_in'], 'v_b_f': out['v_b_f'], 'v_gmlp_ln_g': out['v_gmlp_ln_g'], 'v_gmlp_ln_b': out['v_gmlp_ln_b'], 'v_gmlp_w_s': out['v_gmlp_w_s'], 'v_gmlp_b_s': out['v_gmlp_b_s'], 'v_w_branch_a': out['v_w_branch_a'], 'v_w_branch_b': out['v_w_branch_b'], 'v_w_out': out['v_w_out'], 'v_norm_ffn_g': out['v_norm_ffn_g'], 'v_w_up': out['v_w_up'], 'v_conv_w': out['v_conv_w'], 'v_conv_b': out['v_conv_b'], 'v_w_down': out['v_w_down'], 'v_norm_ple_g': out['v_norm_ple_g'], 'v_w_ple': out['v_w_ple'], 'v_w_ple_gate': out['v_w_ple_gate'], 'v_norm_final_g': out['v_norm_final_g']}


def _loss(weights, diff, rest, loss_target):
    with _jax.named_scope("forward"):
        args = {**rest, TWIN_DIFF_INPUT: diff, **{k: w.astype(_WEIGHT_DTYPES[k]) for k, w in weights.items()}}
        y = _forward(args)
    with _jax.named_scope("loss_head"):
        err = _jnp.square(y.astype(_jnp.float32) - loss_target)
        return 0.5 * _jnp.sum(_jnp.mean(err, axis=-1)) if err.ndim else 0.5 * err


def _adamw(w, g, m, v):
    m = ADAM_B1 * m + (1.0 - ADAM_B1) * g
    v = ADAM_B2 * v + (1.0 - ADAM_B2) * _jnp.square(g)
    m_hat = m / (1.0 - ADAM_B1 ** ADAM_STEP)
    v_hat = v / (1.0 - ADAM_B2 ** ADAM_STEP)
    delta = -ADAM_LR * (m_hat / (_jnp.sqrt(v_hat) + ADAM_EPS) + ADAM_WD * w)
    return delta, m, v


def reference(x, p, norm_mix_g, w_in, b_f, gmlp_ln_g, gmlp_ln_b, gmlp_w_s, gmlp_b_s, w_branch_a, w_branch_b, w_out, norm_ffn_g, w_up, conv_w, conv_b, w_down, norm_ple_g, w_ple, w_ple_gate, norm_final_g, loss_target, m_norm_mix_g, m_w_in, m_b_f, m_gmlp_ln_g, m_gmlp_ln_b, m_gmlp_w_s, m_gmlp_b_s, m_w_branch_a, m_w_branch_b, m_w_out, m_norm_ffn_g, m_w_up, m_conv_w, m_conv_b, m_w_down, m_norm_ple_g, m_w_ple, m_w_ple_gate, m_norm_final_g, v_norm_mix_g, v_w_in, v_b_f, v_gmlp_ln_g, v_gmlp_ln_b, v_gmlp_w_s, v_gmlp_b_s, v_w_branch_a, v_w_branch_b, v_w_out, v_norm_ffn_g, v_w_up, v_conv_w, v_conv_b, v_w_down, v_norm_ple_g, v_w_ple, v_w_ple_gate, v_norm_final_g):
    given = dict(x=x, p=p, norm_mix_g=norm_mix_g, w_in=w_in, b_f=b_f, gmlp_ln_g=gmlp_ln_g, gmlp_ln_b=gmlp_ln_b, gmlp_w_s=gmlp_w_s, gmlp_b_s=gmlp_b_s, w_branch_a=w_branch_a, w_branch_b=w_branch_b, w_out=w_out, norm_ffn_g=norm_ffn_g, w_up=w_up, conv_w=conv_w, conv_b=conv_b, w_down=w_down, norm_ple_g=norm_ple_g, w_ple=w_ple, w_ple_gate=w_ple_gate, norm_final_g=norm_final_g, loss_target=loss_target, m_norm_mix_g=m_norm_mix_g, m_w_in=m_w_in, m_b_f=m_b_f, m_gmlp_ln_g=m_gmlp_ln_g, m_gmlp_ln_b=m_gmlp_ln_b, m_gmlp_w_s=m_gmlp_w_s, m_gmlp_b_s=m_gmlp_b_s, m_w_branch_a=m_w_branch_a, m_w_branch_b=m_w_branch_b, m_w_out=m_w_out, m_norm_ffn_g=m_norm_ffn_g, m_w_up=m_w_up, m_conv_w=m_conv_w, m_conv_b=m_conv_b, m_w_down=m_w_down, m_norm_ple_g=m_norm_ple_g, m_w_ple=m_w_ple, m_w_ple_gate=m_w_ple_gate, m_norm_final_g=m_norm_final_g, v_norm_mix_g=v_norm_mix_g, v_w_in=v_w_in, v_b_f=v_b_f, v_gmlp_ln_g=v_gmlp_ln_g, v_gmlp_ln_b=v_gmlp_ln_b, v_gmlp_w_s=v_gmlp_w_s, v_gmlp_b_s=v_gmlp_b_s, v_w_branch_a=v_w_branch_a, v_w_branch_b=v_w_branch_b, v_w_out=v_w_out, v_norm_ffn_g=v_norm_ffn_g, v_w_up=v_w_up, v_conv_w=v_conv_w, v_conv_b=v_conv_b, v_w_down=v_w_down, v_norm_ple_g=v_norm_ple_g, v_w_ple=v_w_ple, v_w_ple_gate=v_w_ple_gate, v_norm_final_g=v_norm_final_g)
    weights = {n: given[n] for n in TWIN_WEIGHTS}
    shared = {n: given[n] for n in SHARED_INPUTS}
    per_example = {n: given[n] for n in ['x', 'p']}
    grad_fn = _jax.value_and_grad(_loss, argnums=(0, 1))

    def one_microbatch(ex, loss_target):
        ex = dict(ex)
        diff = ex.pop(TWIN_DIFF_INPUT)
        return grad_fn(weights, diff, {**shared, **ex}, loss_target)

    if N_MICROBATCH == 1:
        loss, (grad_w, grad_x) = one_microbatch(per_example, given["loss_target"])
    else:
        def body(carry, xs):
            loss_sum, grad_sum = carry
            l_k, (gw_k, gx_k) = one_microbatch(xs[0], xs[1])
            with _jax.named_scope("update"):
                return (loss_sum + l_k, _jax.tree.map(_jnp.add, grad_sum, gw_k)), gx_k

        init = (_jnp.zeros((), _jnp.float32), _jax.tree.map(_jnp.zeros_like, weights))
        (loss, grad_w), grad_x = _jax.lax.scan(body, init, (per_example, given["loss_target"]))
    with _jax.named_scope("update"):
        delta_w, new_m, new_v = {}, {}, {}
        for n in TWIN_WEIGHTS:
            delta_w[n], new_m[n], new_v[n] = _adamw(weights[n], grad_w[n], given["m_" + n], given["v_" + n])
    return (loss, grad_x, *[grad_w[n] for n in TWIN_WEIGHTS], *[delta_w[n] for n in TWIN_WEIGHTS],
            *[new_m[n] for n in TWIN_WEIGHTS], *[new_v[n] for n in TWIN_WEIGHTS])
```

```python
import functools
import math

import jax
import jax.numpy as jnp
from jax import lax
from jax.experimental import pallas as pl
from jax.experimental.pallas import tpu as pltpu

F32 = jnp.float32
BF16 = jnp.bfloat16
I32 = jnp.int32

D_MODEL = 1024
GROUPS = 8
GDIM = 128
GBLOCK = 128
CHUNK = 64
HEADS = 16
HEAD_DIM = 64
D_FF = 2816
PLE_DIM = 256
EPS = 1e-6
N_DEV = 8
ATT_SCALE = HEAD_DIM ** -0.5
NEG = -1e30

ADAM_LR = 0.001
ADAM_B1 = 0.9
ADAM_B2 = 0.999
ADAM_EPS = 1e-08
ADAM_WD = 0.01
ADAM_STEP = 10

V7X_VMEM_LIMIT = 48 * 1024 * 1024
MESH = pl.DeviceIdType.MESH

O_F = 2 * 1024 + 3 * 1024
O_G = O_F + HEADS
IN_COLS = O_G + 2 * D_MODEL
MAIN_COLS = IN_COLS - HEADS

SEGS = (("w_in", 912), ("w_a", 128), ("w_b", 128), ("w_out", 128), ("w_up", 704), ("conv_w", 16),
        ("w_down", 352), ("w_ple", 32), ("w_pg", 128), ("pad", 32))
SEG_OFF = {}
_o = 0
for _n, _r in SEGS:
    SEG_OFF[_n] = (_o, _r)
    _o += _r
PACK_ROWS = _o

SMALL = (("norm_mix_g", 1), ("b_f", 1), ("gmlp_ln_g", 1), ("gmlp_ln_b", 1), ("gmlp_w_s", 128), ("gmlp_b_s", 1),
         ("norm_ffn_g", 1), ("conv_b", 6), ("norm_ple_g", 1), ("norm_final_g", 1))
SMALL_OFF = {}
_o = 0
for _n, _r in SMALL:
    SMALL_OFF[_n] = (_o, _r)
    _o += _r
SMALL_ROWS = 144


def _cparams(sem):
    return pltpu.CompilerParams(dimension_semantics=sem, vmem_limit_bytes=V7X_VMEM_LIMIT)


def _gelu(x):
    c = math.sqrt(2.0 / math.pi)
    return 0.5 * x * (1.0 + jnp.tanh(c * (x + 0.044715 * x * x * x)))


def _gelu_and_grad(x):
    c = math.sqrt(2.0 / math.pi)
    t = jnp.tanh(c * (x + 0.044715 * x * x * x))
    g = 0.5 * x * (1.0 + t)
    dg = 0.5 * (1.0 + t) + 0.5 * x * (1.0 - t * t) * (c * (1.0 + 3.0 * 0.044715 * x * x))
    return g, dg


def _sigmoid(x):
    return 1.0 / (1.0 + jnp.exp(-x))


def _dot(a, b, dims):
    return lax.dot_general(a, b, (dims, ((), ())), preferred_element_type=F32)


NN = ((1,), (0,))
NT = ((1,), (1,))
TN = ((0,), (0,))


def _matmul(a, b, *, mode, out_dtype, name, tm=512, tn=512, tk=512, add=None):
    if mode == "tn":
        kdim, m = a.shape
    else:
        m, kdim = a.shape
    n = b.shape[0] if mode == "nt" else b.shape[1]
    tm, tn, tk = min(tm, m), min(tn, n), min(tk, kdim)
    assert m % tm == 0 and n % tn == 0 and kdim % tk == 0, (name, m, n, kdim, tm, tn, tk)
    nk = kdim // tk
    dims = {"nn": NN, "nt": NT, "tn": TN}[mode]

    def body(*refs):
        if add is None:
            a_ref, b_ref, o_ref, acc_ref = refs
            add_ref = None
        else:
            a_ref, b_ref, add_ref, o_ref, acc_ref = refs
        k = pl.program_id(2)
        part = _dot(a_ref[...].astype(BF16), b_ref[...].astype(BF16), dims)

        @pl.when(k == 0)
        def _():
            acc_ref[...] = part

        @pl.when(k > 0)
        def _():
            acc_ref[...] += part

        @pl.when(k == nk - 1)
        def _():
            r = acc_ref[...]
            if add_ref is not None:
                r = add_ref[...].astype(F32) + r
            o_ref[...] = r.astype(out_dtype)

    a_spec = pl.BlockSpec((tk, tm), lambda i, j, k: (k, i)) if mode == "tn" else pl.BlockSpec((tm, tk), lambda i, j, k: (i, k))
    b_spec = pl.BlockSpec((tn, tk), lambda i, j, k: (j, k)) if mode == "nt" else pl.BlockSpec((tk, tn), lambda i, j, k: (k, j))
    o_spec = pl.BlockSpec((tm, tn), lambda i, j, k: (i, j))
    in_specs = [a_spec, b_spec] + ([o_spec] if add is not None else [])
    args = (a, b) + ((add,) if add is not None else ())
    return pl.pallas_call(
        body, name=name, grid=(m // tm, n // tn, nk),
        in_specs=in_specs, out_specs=o_spec,
        out_shape=jax.ShapeDtypeStruct((m, n), out_dtype),
        scratch_shapes=[pltpu.VMEM((tm, tn), F32)],
        compiler_params=_cparams(("parallel", "parallel", "arbitrary")),
    )(*args)


def _row_spec(tr, width):
    return pl.BlockSpec((tr, width), lambda i: (i, 0))


def _full_spec(shape):
    return pl.BlockSpec(shape, lambda i: tuple(0 for _ in shape))


def _rmsnorm_fwd(x, g, *, name, tr=256):
    s, d = x.shape

    def body(x_ref, g_ref, o_ref):
        xv = x_ref[...]
        r = lax.rsqrt(jnp.mean(xv * xv, axis=-1, keepdims=True) + EPS)
        o_ref[...] = ((xv * r) * g_ref[...]).astype(BF16)

    return pl.pallas_call(
        body, name=name, grid=(s // tr,),
        in_specs=[_row_spec(tr, d), _full_spec((1, d))], out_specs=_row_spec(tr, d),
        out_shape=jax.ShapeDtypeStruct((s, d), BF16), compiler_params=_cparams(("parallel",)),
    )(x, g)


def _rmsnorm_bwd(dres, dh, x, g, *, name, tr=256):
    s, d = x.shape

    def body(dres_ref, dh_ref, x_ref, g_ref, dx_ref, dg_ref):
        i = pl.program_id(0)
        xv = x_ref[...]
        r = lax.rsqrt(jnp.mean(xv * xv, axis=-1, keepdims=True) + EPS)
        xhat = xv * r
        dhv = dh_ref[...].astype(F32)
        dxhat = dhv * g_ref[...]
        dx = r * (dxhat - xhat * jnp.mean(dxhat * xhat, axis=-1, keepdims=True))
        dx_ref[...] = dres_ref[...] + dx
        dgp = jnp.sum(dhv * xhat, axis=0, keepdims=True)

        @pl.when(i == 0)
        def _():
            dg_ref[...] = dgp

        @pl.when(i > 0)
        def _():
            dg_ref[...] += dgp

    return pl.pallas_call(
        body, name=name, grid=(s // tr,),
        in_specs=[_row_spec(tr, d), _row_spec(tr, d), _row_spec(tr, d), _full_spec((1, d))],
        out_specs=[_row_spec(tr, d), _full_spec((1, d))],
        out_shape=[jax.ShapeDtypeStruct((s, d), F32), jax.ShapeDtypeStruct((1, d), F32)],
        compiler_params=_cparams(("arbitrary",)),
    )(dres, dh, x, g)


def _final_loss_bwd(x3, target, g, *, name, tr=256):
    s, d = x3.shape

    def body(x_ref, t_ref, g_ref, loss_ref, dx_ref, dg_ref):
        i = pl.program_id(0)
        xv = x_ref[...]
        r = lax.rsqrt(jnp.mean(xv * xv, axis=-1, keepdims=True) + EPS)
        xhat = xv * r
        diff = xhat * g_ref[...] - t_ref[...]
        lp = jnp.zeros((1, 128), F32) + (0.5 / d) * jnp.sum(diff * diff)
        dy = diff * (1.0 / d)
        dxhat = dy * g_ref[...]
        dx_ref[...] = r * (dxhat - xhat * jnp.mean(dxhat * xhat, axis=-1, keepdims=True))
        dgp = jnp.sum(dy * xhat, axis=0, keepdims=True)

        @pl.when(i == 0)
        def _():
            dg_ref[...] = dgp
            loss_ref[...] = lp

        @pl.when(i > 0)
        def _():
            dg_ref[...] += dgp
            loss_ref[...] += lp

    return pl.pallas_call(
        body, name=name, grid=(s // tr,),
        in_specs=[_row_spec(tr, d), _row_spec(tr, d), _full_spec((1, d))],
        out_specs=[_full_spec((1, 128)), _row_spec(tr, d), _full_spec((1, d))],
        out_shape=[jax.ShapeDtypeStruct((1, 128), F32), jax.ShapeDtypeStruct((s, d), F32),
                   jax.ShapeDtypeStruct((1, d), F32)],
        compiler_params=_cparams(("arbitrary",)),
    )(x3, target, g)


def _merge_fwd(ya, yb, ga, gb, *, name, tr=256):
    s, d = ya.shape

    def body(ya_ref, yb_ref, ga_ref, gb_ref, o_ref):
        o_ref[...] = (_sigmoid(ga_ref[...]) * ya_ref[...] + _sigmoid(gb_ref[...]) * yb_ref[...]).astype(BF16)

    return pl.pallas_call(
        body, name=name, grid=(s // tr,),
        in_specs=[_row_spec(tr, d)] * 4, out_specs=_row_spec(tr, d),
        out_shape=jax.ShapeDtypeStruct((s, d), BF16), compiler_params=_cparams(("parallel",)),
    )(ya, yb, ga, gb)


def _merge_bwd(dm, ya, yb, ga, gb, *, name, tr=256):
    s, d = ya.shape

    def body(dm_ref, ya_ref, yb_ref, ga_ref, gb_ref, dya_ref, dyb_ref, dga_ref, dgb_ref):
        dmv = dm_ref[...]
        sa = _sigmoid(ga_ref[...])
        sb = _sigmoid(gb_ref[...])
        dya_ref[...] = (dmv * sa).astype(BF16)
        dyb_ref[...] = (dmv * sb).astype(BF16)
        dga_ref[...] = (dmv * ya_ref[...] * (sa * (1.0 - sa))).astype(BF16)
        dgb_ref[...] = (dmv * yb_ref[...] * (sb * (1.0 - sb))).astype(BF16)

    o = jax.ShapeDtypeStruct((s, d), BF16)
    return pl.pallas_call(
        body, name=name, grid=(s // tr,),
        in_specs=[_row_spec(tr, d)] * 5, out_specs=[_row_spec(tr, d)] * 4,
        out_shape=[o, o, o, o], compiler_params=_cparams(("parallel",)),
    )(dm, ya, yb, ga, gb)


def _ple_fwd(x2, ple, gp, *, name, tr=256):
    s, d = x2.shape

    def body(x_ref, ple_ref, gp_ref, o_ref):
        o_ref[...] = x_ref[...] + ple_ref[...] * _sigmoid(gp_ref[...])

    return pl.pallas_call(
        body, name=name, grid=(s // tr,),
        in_specs=[_row_spec(tr, d)] * 3, out_specs=_row_spec(tr, d),
        out_shape=jax.ShapeDtypeStruct((s, d), F32), compiler_params=_cparams(("parallel",)),
    )(x2, ple, gp)


def _ple_bwd(dx3, ple, gp, *, name, tr=256):
    s, d = dx3.shape

    def body(dx_ref, ple_ref, gp_ref, dple_ref, dgp_ref):
        sg = _sigmoid(gp_ref[...])
        dxv = dx_ref[...]
        dple_ref[...] = (dxv * sg).astype(BF16)
        dgp_ref[...] = (dxv * ple_ref[...] * (sg * (1.0 - sg))).astype(BF16)

    o = jax.ShapeDtypeStruct((s, d), BF16)
    return pl.pallas_call(
        body, name=name, grid=(s // tr,),
        in_specs=[_row_spec(tr, d)] * 3, out_specs=[_row_spec(tr, d)] * 2,
        out_shape=[o, o], compiler_params=_cparams(("parallel",)),
    )(dx3, ple, gp)


def _masked_ws(ws_ref, g):
    row = lax.broadcasted_iota(I32, (GBLOCK, GBLOCK), 0)
    col = lax.broadcasted_iota(I32, (GBLOCK, GBLOCK), 1)
    keep = (col // CHUNK) <= (row // CHUNK)
    return jnp.where(keep, ws_ref[g], 0.0), keep


def _layernorm_parts(zv):
    mu = jnp.mean(zv, axis=-1, keepdims=True)
    xc = zv - mu
    rs = lax.rsqrt(jnp.mean(xc * xc, axis=-1, keepdims=True) + EPS)
    return xc * rs, rs


def _gmlp_fwd(zu_pre, zv_pre, ln_g, ln_b, w_s, bs_t, *, name):
    s, w = zu_pre.shape

    def body(zu_ref, zv_ref, lng_ref, lnb_ref, ws_ref, bs_ref, a_ref):
        zu = _gelu(zu_ref[...])
        zv = _gelu(zv_ref[...])
        xhat, _ = _layernorm_parts(zv)
        vln = (xhat * lng_ref[...] + lnb_ref[...]).astype(BF16)
        for g in range(GROUPS):
            wm, _ = _masked_ws(ws_ref, g)
            mixed = _dot(wm.astype(BF16), vln[:, g * GDIM:(g + 1) * GDIM], NN) + bs_ref[:, g:g + 1]
            a_ref[:, g * GDIM:(g + 1) * GDIM] = (zu[:, g * GDIM:(g + 1) * GDIM] * mixed).astype(BF16)

    return pl.pallas_call(
        body, name=name, grid=(s // GBLOCK,),
        in_specs=[_row_spec(GBLOCK, w), _row_spec(GBLOCK, w), _full_spec((1, w)), _full_spec((1, w)),
                  _full_spec((GROUPS, GBLOCK, GBLOCK)), _full_spec((GBLOCK, 128))],
        out_specs=_row_spec(GBLOCK, w),
        out_shape=jax.ShapeDtypeStruct((s, w), BF16), compiler_params=_cparams(("parallel",)),
    )(zu_pre, zv_pre, ln_g, ln_b, w_s, bs_t)


def _gmlp_bwd(da, zu_pre, zv_pre, ln_g, ln_b, w_s, bs_t, *, name):
    s, w = zu_pre.shape

    def body(da_ref, zu_ref, zv_ref, lng_ref, lnb_ref, ws_ref, bs_ref,
             dzu_ref, dzv_ref, dws_ref, dbs_ref, dlng_ref, dlnb_ref, dvln_ref):
        i = pl.program_id(0)
        zu, dzu_g = _gelu_and_grad(zu_ref[...])
        zv, dzv_g = _gelu_and_grad(zv_ref[...])
        xhat, rs = _layernorm_parts(zv)
        vln = (xhat * lng_ref[...] + lnb_ref[...]).astype(BF16)
        dav = da_ref[...].astype(F32)
        lane = lax.broadcasted_iota(I32, (GBLOCK, 128), 1)
        dbs = jnp.zeros((GBLOCK, 128), F32)

        @pl.when(i == 0)
        def _():
            dws_ref[...] = jnp.zeros_like(dws_ref)

        for g in range(GROUPS):
            sl = slice(g * GDIM, (g + 1) * GDIM)
            wm, keep = _masked_ws(ws_ref, g)
            wmb = wm.astype(BF16)
            vg = vln[:, sl]
            mixed = _dot(wmb, vg, NN) + bs_ref[:, g:g + 1]
            dag = dav[:, sl]
            dzu_ref[:, sl] = (dag * mixed * dzu_g[:, sl]).astype(BF16)
            dmix = dag * zu[:, sl]
            dmb = dmix.astype(BF16)
            dws_ref[g] += jnp.where(keep, _dot(dmb, vg, NT), 0.0)
            dbs = jnp.where(lane == g, jnp.sum(dmix, axis=1, keepdims=True), dbs)
            dvln_ref[:, sl] = _dot(wmb, dmb, TN)
        dvln = dvln_ref[...]
        dxhat = dvln * lng_ref[...]
        dzv = rs * (dxhat - jnp.mean(dxhat, axis=-1, keepdims=True)
                    - xhat * jnp.mean(dxhat * xhat, axis=-1, keepdims=True))
        dzv_ref[...] = (dzv * dzv_g).astype(BF16)
        dlng = jnp.sum(dvln * xhat, axis=0, keepdims=True)
        dlnb = jnp.sum(dvln, axis=0, keepdims=True)

        @pl.when(i == 0)
        def _():
            dbs_ref[...] = dbs
            dlng_ref[...] = dlng
            dlnb_ref[...] = dlnb

        @pl.when(i > 0)
        def _():
            dbs_ref[...] += dbs
            dlng_ref[...] += dlng
            dlnb_ref[...] += dlnb

    return pl.pallas_call(
        body, name=name, grid=(s // GBLOCK,),
        in_specs=[_row_spec(GBLOCK, w)] * 3 + [_full_spec((1, w)), _full_spec((1, w)),
                                               _full_spec((GROUPS, GBLOCK, GBLOCK)), _full_spec((GBLOCK, 128))],
        out_specs=[_row_spec(GBLOCK, w), _row_spec(GBLOCK, w), _full_spec((GROUPS, GBLOCK, GBLOCK)),
                   _full_spec((GBLOCK, 128)), _full_spec((1, w)), _full_spec((1, w))],
        out_shape=[jax.ShapeDtypeStruct((s, w), BF16), jax.ShapeDtypeStruct((s, w), BF16),
                   jax.ShapeDtypeStruct((GROUPS, GBLOCK, GBLOCK), F32), jax.ShapeDtypeStruct((GBLOCK, 128), F32),
                   jax.ShapeDtypeStruct((1, w), F32), jax.ShapeDtypeStruct((1, w), F32)],
        scratch_shapes=[pltpu.VMEM((GBLOCK, w), F32)],
        compiler_params=_cparams(("arbitrary",)),
    )(da, zu_pre, zv_pre, ln_g, ln_b, w_s, bs_t)


def _shift_down(u, k):
    row = lax.broadcasted_iota(I32, u.shape, 0)
    return jnp.where(row >= k, pltpu.roll(u, k, 0), 0.0)


def _shift_up(u, k):
    s = u.shape[0]
    row = lax.broadcasted_iota(I32, u.shape, 0)
    return jnp.where(row < s - k, pltpu.roll(u, s - k, 0), 0.0)


def _conv(u, w_ref, b_ref):
    return b_ref[...] + w_ref[0:1, :] * _shift_down(u, 2) + w_ref[1:2, :] * _shift_down(u, 1) + w_ref[2:3, :] * u


def _convglu_fwd(up_a, up_g, cw_a, cw_g, cb_a, cb_g, *, name, tc=256):
    s, f = up_a.shape

    def body(ua_ref, ug_ref, wa_ref, wg_ref, ba_ref, bg_ref, o_ref):
        ca = _conv(ua_ref[...], wa_ref, ba_ref)
        cg = _conv(ug_ref[...], wg_ref, bg_ref)
        o_ref[...] = (_gelu(ca) * cg).astype(BF16)

    col = pl.BlockSpec((s, tc), lambda j: (0, j))
    w3 = pl.BlockSpec((3, tc), lambda j: (0, j))
    b1 = pl.BlockSpec((1, tc), lambda j: (0, j))
    return pl.pallas_call(
        body, name=name, grid=(f // tc,),
        in_specs=[col, col, w3, w3, b1, b1], out_specs=col,
        out_shape=jax.ShapeDtypeStruct((s, f), BF16), compiler_params=_cparams(("parallel",)),
    )(up_a, up_g, cw_a, cw_g, cb_a, cb_g)


def _convglu_bwd(dact, up_a, up_g, cw_a, cw_g, cb_a, cb_g, *, name, tc=256):
    s, f = up_a.shape

    def half(dc, u, w_ref, du_ref, dw_ref, db_ref):
        db_ref[...] = jnp.sum(dc, axis=0, keepdims=True)
        dw_ref[0:1, :] = jnp.sum(dc * _shift_down(u, 2), axis=0, keepdims=True)
        dw_ref[1:2, :] = jnp.sum(dc * _shift_down(u, 1), axis=0, keepdims=True)
        dw_ref[2:3, :] = jnp.sum(dc * u, axis=0, keepdims=True)
        du = w_ref[2:3, :] * dc + w_ref[1:2, :] * _shift_up(dc, 1) + w_ref[0:1, :] * _shift_up(dc, 2)
        du_ref[...] = du.astype(BF16)

    def body(d_ref, ua_ref, ug_ref, wa_ref, wg_ref, ba_ref, bg_ref,
             dua_ref, dug_ref, dwa_ref, dwg_ref, dba_ref, dbg_ref):
        ua = ua_ref[...]
        ug = ug_ref[...]
        ca = _conv(ua, wa_ref, ba_ref)
        cg = _conv(ug, wg_ref, bg_ref)
        ga, dga = _gelu_and_grad(ca)
        dv = d_ref[...].astype(F32)
        half(dv * cg * dga, ua, wa_ref, dua_ref, dwa_ref, dba_ref)
        half(dv * ga, ug, wg_ref, dug_ref, dwg_ref, dbg_ref)

    col = pl.BlockSpec((s, tc), lambda j: (0, j))
    w3 = pl.BlockSpec((3, tc), lambda j: (0, j))
    b1 = pl.BlockSpec((1, tc), lambda j: (0, j))
    return pl.pallas_call(
        body, name=name, grid=(f // tc,),
        in_specs=[col, col, col, w3, w3, b1, b1], out_specs=[col, col, w3, w3, b1, b1],
        out_shape=[jax.ShapeDtypeStruct((s, f), BF16), jax.ShapeDtypeStruct((s, f), BF16),
                   jax.ShapeDtypeStruct((3, f), F32), jax.ShapeDtypeStruct((3, f), F32),
                   jax.ShapeDtypeStruct((1, f), F32), jax.ShapeDtypeStruct((1, f), F32)],
        compiler_params=_cparams(("parallel",)),
    )(dact, up_a, up_g, cw_a, cw_g, cb_a, cb_g)


def _tri_dot(tri, x):
    b0 = x.astype(BF16)
    r1 = x - b0.astype(F32)
    b1 = r1.astype(BF16)
    b2 = (r1 - b1.astype(F32)).astype(BF16)
    return _dot(tri, b0, NN) + _dot(tri, b1, NN) + _dot(tri, b2, NN)


def _log_sigmoid(x):
    return jnp.minimum(x, 0.0) - jnp.log(1.0 + jnp.exp(-jnp.abs(x)))


def _expand_heads(col16, rows):
    head_of_lane = lax.broadcasted_iota(I32, (rows, HEADS * HEAD_DIM), 1) // HEAD_DIM
    out = jnp.zeros((rows, HEADS * HEAD_DIM), F32)
    for h in range(HEADS):
        out = jnp.where(head_of_lane == h, col16[:, h:h + 1], out)
    return out


def _forget_cumsum(f_logit, b_f, *, name):
    s = f_logit.shape[0]
    nb = s // 128

    def body(f_ref, b_ref, cqe_ref, ckt_ref):
        row = lax.broadcasted_iota(I32, (128, 128), 0)
        col = lax.broadcasted_iota(I32, (128, 128), 1)
        tri = (col <= row).astype(BF16)

        def step(n, carry):
            r0 = pl.multiple_of(n * 128, 128)
            lf = _log_sigmoid(f_ref[pl.ds(r0, 128), :] + b_ref[...])
            cum = _tri_dot(tri, lf) + carry
            cqe_ref[pl.ds(r0, 128), :] = _expand_heads(cum, 128)
            ckt_ref[:, pl.ds(r0, 128)] = cum.T
            return cum[127:128, :]

        lax.fori_loop(0, nb, step, jnp.zeros((1, 128), F32))

    return pl.pallas_call(
        body, name=name, grid=(1,),
        in_specs=[_full_spec((s, 128)), _full_spec((1, 128))],
        out_specs=[_full_spec((s, HEADS * HEAD_DIM)), _full_spec((128, s))],
        out_shape=[jax.ShapeDtypeStruct((s, HEADS * HEAD_DIM), F32), jax.ShapeDtypeStruct((128, s), F32)],
        compiler_params=_cparams(("arbitrary",)),
    )(f_logit, b_f)


def _forget_bwd(dcq16, dck16, f_logit, b_f, *, name):
    s = f_logit.shape[0]
    nb = s // 128

    def body(a_ref, k_ref, f_ref, b_ref, df_ref, db_ref):
        row = lax.broadcasted_iota(I32, (128, 128), 0)
        col = lax.broadcasted_iota(I32, (128, 128), 1)
        tri_rev = (col >= row).astype(BF16)

        def step(m, carry):
            suffix, dbsum = carry
            n = nb - 1 - m
            r0 = pl.multiple_of(n * 128, 128)
            dcum = a_ref[pl.ds(r0, 128), :] + k_ref[pl.ds(r0, 128), :]
            dlf = _tri_dot(tri_rev, dcum) + suffix
            df = dlf * _sigmoid(-(f_ref[pl.ds(r0, 128), :] + b_ref[...]))
            df_ref[pl.ds(r0, 128), :] = df.astype(BF16)
            return dlf[0:1, :], dbsum + jnp.sum(df, axis=0, keepdims=True)

        _, dbsum = lax.fori_loop(0, nb, step, (jnp.zeros((1, 128), F32), jnp.zeros((1, 128), F32)))
        db_ref[...] = dbsum

    return pl.pallas_call(
        body, name=name, grid=(1,),
        in_specs=[_full_spec((s, 128))] * 3 + [_full_spec((1, 128))],
        out_specs=[_full_spec((s, 128)), _full_spec((1, 128))],
        out_shape=[jax.ShapeDtypeStruct((s, 128), BF16), jax.ShapeDtypeStruct((1, 128), F32)],
        compiler_params=_cparams(("arbitrary",)),
    )(dcq16, dck16, f_logit, b_f)


ATT_T = 256


def _head_lanes(rows):
    return lax.broadcasted_iota(I32, (rows, 128), 1) < HEAD_DIM


def _attn_fwd(q, k, v, cqe, ck3, *, name):
    s = q.shape[0]
    t = ATT_T
    nq = s // t

    def body(q_ref, k_ref, v_ref, cq_ref, ck_ref, o_ref, lse_ref):
        i = pl.program_id(1)
        first = _head_lanes(t)
        q2 = q_ref[...]
        row = lax.broadcasted_iota(I32, (t, t), 0)
        col = lax.broadcasted_iota(I32, (t, t), 1)
        o_pair = jnp.zeros((t, 128), F32)
        lse_pair = jnp.zeros((t, 128), F32)
        for e in range(2):
            mine = first if e == 0 else jnp.logical_not(first)
            qh = jnp.where(mine, q2, jnp.zeros_like(q2))
            cq = cq_ref[:, e * HEAD_DIM:e * HEAD_DIM + 1]

            def step(j, carry, diag):
                m, l, acc = carry
                c0 = pl.multiple_of(j * t, t)
                kb = k_ref[pl.ds(c0, t), :]
                vb = v_ref[pl.ds(c0, t), :]
                sc = _dot(qh, kb, NT) * ATT_SCALE + cq - ck_ref[0, e:e + 1, pl.ds(c0, t)]
                if diag:
                    sc = jnp.where(col <= row, sc, NEG)
                m_new = jnp.maximum(m, jnp.max(sc, axis=-1, keepdims=True))
                alpha = jnp.exp(m - m_new)
                p = jnp.exp(sc - m_new)
                l = alpha * l + jnp.sum(p, axis=-1, keepdims=True)
                acc = alpha * acc + _dot(p.astype(BF16), vb, NN)
                return m_new, l, acc

            carry = (jnp.full((t, 1), NEG, F32), jnp.zeros((t, 1), F32), jnp.zeros((t, 128), F32))
            carry = lax.fori_loop(0, i, functools.partial(step, diag=False), carry)
            m, l, acc = step(i, carry, True)
            o_pair = jnp.where(mine, acc / l, o_pair)
            lse_pair = jnp.where(mine, m + jnp.log(l), lse_pair)
        o_ref[...] = o_pair.astype(BF16)
        lse_ref[...] = lse_pair

    blk = pl.BlockSpec((t, 128), lambda hp, i: (i, hp))
    full = pl.BlockSpec((s, 128), lambda hp, i: (0, hp))
    return pl.pallas_call(
        body, name=name, grid=(HEADS // 2, nq),
        in_specs=[blk, full, full, blk, pl.BlockSpec((1, 8, s), lambda hp, i: (hp, 0, 0))],
        out_specs=[blk, blk],
        out_shape=[jax.ShapeDtypeStruct((s, HEADS * HEAD_DIM), BF16), jax.ShapeDtypeStruct((s, HEADS * HEAD_DIM), F32)],
        compiler_params=_cparams(("parallel", "arbitrary")),
    )(q, k, v, cqe, ck3)


def _attn_bwd(q, k, v, o, do, lse, cqe, ck3, *, name):
    s = q.shape[0]
    t = ATT_T
    nb = s // t

    def body(q_ref, k_ref, v_ref, o_ref, do_ref, lse_ref, cq_ref, ck_ref,
             dq_ref, dk_ref, dv_ref, dcq_ref, dck_ref, dq_acc, dcq_acc):
        first = _head_lanes(t)
        dq_acc[...] = jnp.zeros_like(dq_acc)
        dcq_acc[...] = jnp.zeros_like(dcq_acc)
        dck_ref[...] = jnp.zeros_like(dck_ref)
        row = lax.broadcasted_iota(I32, (t, t), 0)
        col = lax.broadcasted_iota(I32, (t, t), 1)

        def key_block(j, _):
            c0 = pl.multiple_of(j * t, t)
            kb = k_ref[pl.ds(c0, t), :]
            vb = v_ref[pl.ds(c0, t), :]
            dk_pair = jnp.zeros((t, 128), F32)
            dv_pair = jnp.zeros((t, 128), F32)
            for e in range(2):
                mine = first if e == 0 else jnp.logical_not(first)
                ck = ck_ref[0, e:e + 1, pl.ds(c0, t)]

                def query_block(i, carry):
                    dk_a, dv_a, dck_a = carry
                    r0 = pl.multiple_of(i * t, t)
                    qb = q_ref[pl.ds(r0, t), :]
                    qh = jnp.where(mine, qb, jnp.zeros_like(qb))
                    dob = do_ref[pl.ds(r0, t), :]
                    doh = jnp.where(mine, dob, jnp.zeros_like(dob))
                    delta = jnp.sum(doh.astype(F32) * o_ref[pl.ds(r0, t), :].astype(F32), axis=-1, keepdims=True)
                    cq = cq_ref[pl.ds(r0, t), e * HEAD_DIM:e * HEAD_DIM + 1]
                    lse_q = lse_ref[pl.ds(r0, t), e * HEAD_DIM:e * HEAD_DIM + 1]
                    sc = _dot(qh, kb, NT) * ATT_SCALE + cq - ck
                    sc = jnp.where(col + c0 <= row + r0, sc, NEG)
                    p = jnp.exp(sc - lse_q)
                    dp = _dot(doh, vb, NT)
                    ds = p * (dp - delta)
                    dsb = ds.astype(BF16)
                    dv_a = dv_a + _dot(p.astype(BF16), dob, TN)
                    dk_a = dk_a + _dot(dsb, qb, TN)
                    dqp = _dot(dsb, kb, NN) * ATT_SCALE
                    dq_acc[pl.ds(r0, t), :] += jnp.where(mine, dqp, 0.0)
                    dcq_acc[pl.ds(r0, t), :] += jnp.where(mine, jnp.sum(ds, axis=-1, keepdims=True), 0.0)
                    dck_a = dck_a - jnp.sum(ds, axis=0, keepdims=True)
                    return dk_a, dv_a, dck_a

                zero = jnp.zeros((t, 128), F32)
                dk_a, dv_a, dck_a = lax.fori_loop(j, nb, query_block, (zero, zero, jnp.zeros((1, t), F32)))
                dk_pair = jnp.where(mine, dk_a * ATT_SCALE, dk_pair)
                dv_pair = jnp.where(mine, dv_a, dv_pair)
                dck_ref[0, e:e + 1, pl.ds(c0, t)] = dck_a
            dk_ref[pl.ds(c0, t), :] = dk_pair.astype(BF16)
            dv_ref[pl.ds(c0, t), :] = dv_pair.astype(BF16)
            return 0

        lax.fori_loop(0, nb, key_block, 0)
        dq_ref[...] = dq_acc[...].astype(BF16)
        dcq_ref[...] = dcq_acc[...]

    full = pl.BlockSpec((s, 128), lambda hp: (0, hp))
    ck_spec = pl.BlockSpec((1, 8, s), lambda hp: (hp, 0, 0))
    wide = jax.ShapeDtypeStruct((s, HEADS * HEAD_DIM), BF16)
    return pl.pallas_call(
        body, name=name, grid=(HEADS // 2,),
        in_specs=[full] * 7 + [ck_spec],
        out_specs=[full, full, full, full, ck_spec],
        out_shape=[wide, wide, wide, jax.ShapeDtypeStruct((s, HEADS * HEAD_DIM), F32),
                   jax.ShapeDtypeStruct((HEADS // 2, 8, s), F32)],
        scratch_shapes=[pltpu.VMEM((s, 128), F32), pltpu.VMEM((s, 128), F32)],
        compiler_params=_cparams(("parallel",)),
    )(q, k, v, o, do, lse, cqe, ck3)


def _adam_math(w, g, m, v):
    m = ADAM_B1 * m + (1.0 - ADAM_B1) * g
    v = ADAM_B2 * v + (1.0 - ADAM_B2) * (g * g)
    m_hat = m / (1.0 - ADAM_B1 ** ADAM_STEP)
    v_hat = v / (1.0 - ADAM_B2 ** ADAM_STEP)
    delta = -ADAM_LR * (m_hat / (jnp.sqrt(v_hat) + ADAM_EPS) + ADAM_WD * w)
    return delta, m, v


def _sum_pairs(keep, recv, pos, *, name, tr=512):
    r = recv.shape[1]

    def body(pos_ref, a_ref, b_ref, o_ref):
        o_ref[...] = a_ref[...].astype(F32) + b_ref[...].astype(F32)

    grid_spec = pltpu.PrefetchScalarGridSpec(
        num_scalar_prefetch=1, grid=(4, r // tr),
        in_specs=[pl.BlockSpec((1, tr, 1024), lambda q, i, pos: (2 * q + pos[2], i, 0)),
                  pl.BlockSpec((1, tr, 1024), lambda q, i, pos: (q, i, 0))],
        out_specs=pl.BlockSpec((1, tr, 1024), lambda q, i, pos: (q, i, 0)))
    return pl.pallas_call(
        body, name=name, grid_spec=grid_spec, out_shape=jax.ShapeDtypeStruct((4, r, 1024), F32),
        compiler_params=_cparams(("parallel", "parallel")),
    )(pos, keep, recv)


def _adam_sharded(psum, recv, w, m, v, pos, *, name, tr=256):
    r = w.shape[0]

    def body(pos_ref, p_ref, r_ref, w_ref, m_ref, v_ref, g_ref, d_ref, mo_ref, vo_ref):
        g = p_ref[0] + r_ref[0].astype(F32) + r_ref[1].astype(F32) + r_ref[2].astype(F32)
        delta, mn, vn = _adam_math(w_ref[...], g, m_ref[...], v_ref[...])
        g_ref[...] = g
        d_ref[...] = delta
        mo_ref[...] = mn
        vo_ref[...] = vn

    row = pl.BlockSpec((tr, 1024), lambda i, pos: (i, 0))
    grid_spec = pltpu.PrefetchScalarGridSpec(
        num_scalar_prefetch=1, grid=(r // tr,),
        in_specs=[pl.BlockSpec((1, tr, 1024), lambda i, pos: (2 * pos[0] + pos[1], i, 0)),
                  pl.BlockSpec((3, tr, 1024), lambda i, pos: (0, i, 0)), row, row, row],
        out_specs=[row, row, row, row])
    o = jax.ShapeDtypeStruct((r, 1024), F32)
    return pl.pallas_call(
        body, name=name, grid_spec=grid_spec, out_shape=[o, o, o, o],
        compiler_params=_cparams(("parallel",)),
    )(pos, psum, recv, w, m, v)


def _adam_replicated(chip_sums, w, m, v, *, name):
    r = w.shape[0]

    def body(s_ref, w_ref, m_ref, v_ref, g_ref, d_ref, mo_ref, vo_ref):
        g = ((s_ref[0] + s_ref[1]) + s_ref[2]) + s_ref[3]
        delta, mn, vn = _adam_math(w_ref[...], g, m_ref[...], v_ref[...])
        g_ref[...] = g
        d_ref[...] = delta
        mo_ref[...] = mn
        vo_ref[...] = vn

    o = jax.ShapeDtypeStruct((r, 1024), F32)
    full = _full_spec((r, 1024))
    return pl.pallas_call(
        body, name=name, grid=(1,),
        in_specs=[_full_spec((4, r, 1024)), full, full, full], out_specs=[full] * 4, out_shape=[o] * 4,
        compiler_params=_cparams(("arbitrary",)),
    )(chip_sums, w, m, v)


def _pair_sum_small(mine, theirs, *, name):
    def body(a_ref, b_ref, o_ref):
        o_ref[...] = a_ref[...] + b_ref[...]

    full = _full_spec(mine.shape)
    return pl.pallas_call(
        body, name=name, grid=(1,), in_specs=[full, full], out_specs=full,
        out_shape=jax.ShapeDtypeStruct(mine.shape, F32), compiler_params=_cparams(("arbitrary",)),
    )(mine, theirs)


ANY = pl.BlockSpec(memory_space=pl.ANY)
OTHER_CHIPS = ((1, 0), (0, 1), (1, 1))


def _allgather_weights(shard, *, name):
    r = shard.shape[0]

    def body(x_ref, out_ref, send_sems, recv_sems, local_sem):
        x, y, c = lax.axis_index("x"), lax.axis_index("y"), lax.axis_index("c")
        me, sibling = (x, y, c), (x, y, 1 - c)
        chips = [(x ^ fx, y ^ fy) for fx, fy in OTHER_CHIPS]

        def slab(px, py, pc):
            return out_ref.at[4 * px + 2 * py + pc]

        def copy(k, block, to, src=None):
            return pltpu.make_async_remote_copy(
                src_ref=slab(*block) if src is None else src, dst_ref=slab(*block),
                send_sem=send_sems.at[k], recv_sem=recv_sems.at[k], device_id=to, device_id_type=MESH)

        mine = pltpu.make_async_copy(x_ref, slab(*me), local_sem)
        mine.start()
        first = [copy(0, me, sibling, src=x_ref)]
        first += [copy(1 + j, me, (*chip, c), src=x_ref) for j, chip in enumerate(chips)]
        for cp in first:
            cp.start()
        passed = [copy(4 + j, (*chip, c), sibling) for j, chip in enumerate(chips)]
        for j, chip in enumerate(chips):
            copy(1 + j, (*chip, c), me).wait_recv()
            passed[j].start()
        copy(0, sibling, me).wait_recv()
        for j, chip in enumerate(chips):
            copy(4 + j, (*chip, 1 - c), me).wait_recv()
        for cp in first + passed:
            cp.wait_send()
        mine.wait()

    return pl.pallas_call(
        body, name=name, out_shape=jax.ShapeDtypeStruct((N_DEV, r, 1024), shard.dtype),
        in_specs=[ANY], out_specs=ANY,
        scratch_shapes=[pltpu.SemaphoreType.DMA((7,)), pltpu.SemaphoreType.DMA((7,)), pltpu.SemaphoreType.DMA],
    )(shard)


def _exchange_sibling(grads, small, *, name):
    r = grads.shape[1]

    def body(g_ref, s_ref, rg_ref, rs_ref, send_sems, recv_sems):
        x, y, c = lax.axis_index("x"), lax.axis_index("y"), lax.axis_index("c")
        sibling = (x, y, 1 - c)
        copies = []
        for q in range(4):
            copies.append(pltpu.make_async_remote_copy(
                src_ref=g_ref.at[2 * q + (1 - c)], dst_ref=rg_ref.at[q],
                send_sem=send_sems.at[q], recv_sem=recv_sems.at[q], device_id=sibling, device_id_type=MESH))
        copies.append(pltpu.make_async_remote_copy(
            src_ref=s_ref, dst_ref=rs_ref, send_sem=send_sems.at[4], recv_sem=recv_sems.at[4],
            device_id=sibling, device_id_type=MESH))
        for cp in copies:
            cp.start()
        for cp in copies:
            cp.wait()

    return pl.pallas_call(
        body, name=name,
        out_shape=[jax.ShapeDtypeStruct((4, r, 1024), grads.dtype), jax.ShapeDtypeStruct(small.shape, small.dtype)],
        in_specs=[ANY, ANY], out_specs=[ANY, ANY],
        scratch_shapes=[pltpu.SemaphoreType.DMA((5,)), pltpu.SemaphoreType.DMA((5,))],
    )(grads, small)


def _exchange_chips(psums, small_sum, *, name):
    r = psums.shape[1]
    rs = small_sum.shape[0]

    def body(p_ref, s_ref, rp_ref, tab_ref, send_sems, recv_sems, local_sem):
        x, y, c = lax.axis_index("x"), lax.axis_index("y"), lax.axis_index("c")
        mine = pltpu.make_async_copy(s_ref, tab_ref.at[2 * x + y], local_sem)
        mine.start()
        copies = []
        for k, (fx, fy) in enumerate(OTHER_CHIPS):
            px, py = x ^ fx, y ^ fy
            copies.append(pltpu.make_async_remote_copy(
                src_ref=p_ref.at[2 * px + py], dst_ref=rp_ref.at[k],
                send_sem=send_sems.at[k], recv_sem=recv_sems.at[k], device_id=(px, py, c), device_id_type=MESH))
            copies.append(pltpu.make_async_remote_copy(
                src_ref=s_ref, dst_ref=tab_ref.at[2 * x + y],
                send_sem=send_sems.at[3 + k], recv_sem=recv_sems.at[3 + k], device_id=(px, py, c), device_id_type=MESH))
        for cp in copies:
            cp.start()
        for k, (fx, fy) in enumerate(OTHER_CHIPS):
            px, py = x ^ fx, y ^ fy
            copies[2 * k].wait()
            pltpu.make_async_remote_copy(
                src_ref=s_ref, dst_ref=tab_ref.at[2 * px + py],
                send_sem=send_sems.at[3 + k], recv_sem=recv_sems.at[3 + k], device_id=(px, py, c),
                device_id_type=MESH).wait()
        mine.wait()

    return pl.pallas_call(
        body, name=name,
        out_shape=[jax.ShapeDtypeStruct((3, r, 1024), psums.dtype), jax.ShapeDtypeStruct((4, rs, 1024), F32)],
        in_specs=[ANY, ANY], out_specs=[ANY, ANY],
        scratch_shapes=[pltpu.SemaphoreType.DMA((6,)), pltpu.SemaphoreType.DMA((6,)), pltpu.SemaphoreType.DMA],
    )(psums, small_sum)


def _rows(a, rows):
    flat = a.reshape(-1)
    return jnp.pad(flat, (0, rows * 1024 - flat.shape[0])).reshape(rows, 1024)


def _pack_shard(parts, dtype):
    def one(n, r):
        if n == "pad":
            return jnp.zeros((r, 1024), dtype)
        if n == "conv_w" and dtype == BF16:
            return _rows(lax.bitcast_convert_type(parts[n], BF16), r)
        return _rows(parts[n].astype(dtype), r)

    return jnp.concatenate([one(n, r) for n, r in SEGS], axis=0)


def _seg(packed, name):
    off, r = SEG_OFF[name]
    return packed[..., off:off + r, :]


def _unpack_gathered(g):
    def cols(name, rows, width):
        n = rows * width
        a = _seg(g, name).reshape(N_DEV, -1)[:, :n].reshape(N_DEV, rows, width)
        return a.transpose(1, 0, 2).reshape(rows, N_DEV * width)

    def rows_(name, r):
        return _seg(g, name)[:, :r].reshape(N_DEV * r, 1024)

    w_in = cols("w_in", 1024, IN_COLS // N_DEV)
    cw = 2 * D_FF // N_DEV
    conv_bits = _seg(g, "conv_w").reshape(N_DEV, -1)[:, :3 * cw * 2].reshape(N_DEV, 3, cw, 2)
    conv_w = lax.bitcast_convert_type(conv_bits, F32).transpose(1, 0, 2).reshape(3, N_DEV * cw)
    return dict(
        w_main=jnp.concatenate([w_in[:, :O_F], w_in[:, O_G:]], axis=1),
        w_f=jnp.pad(w_in[:, O_F:O_G], ((0, 0), (0, 128 - HEADS))),
        w_a=rows_("w_a", 128), w_b=rows_("w_b", 128), w_out=rows_("w_out", 128),
        w_up=cols("w_up", 1024, 2 * D_FF // N_DEV), conv_w=conv_w,
        w_down=rows_("w_down", D_FF // N_DEV), w_ple=cols("w_ple", PLE_DIM, D_MODEL // N_DEV),
        w_pg=rows_("w_pg", 128))


def _pack_grads(gr):
    def cols(a, r):
        rows, tot = a.shape
        width = tot // N_DEV
        b = a.reshape(rows, N_DEV, width).transpose(1, 0, 2).reshape(N_DEV, -1)
        return jnp.pad(b, ((0, 0), (0, r * 1024 - rows * width))).reshape(N_DEV, r, 1024)

    def rows_(a):
        return a.reshape(N_DEV, -1, 1024)

    segs = dict(w_in=cols(gr["w_in"], 912), w_a=rows_(gr["w_a"]), w_b=rows_(gr["w_b"]), w_out=rows_(gr["w_out"]),
                w_up=cols(gr["w_up"], 704), conv_w=cols(gr["conv_w"], 16), w_down=rows_(gr["w_down"]),
                w_ple=cols(gr["w_ple"], 32), w_pg=rows_(gr["w_pg"]), pad=jnp.zeros((N_DEV, 32, 1024), F32))
    return jnp.concatenate([segs[n].astype(BF16) for n, _ in SEGS], axis=1)


def _pack_small(parts):
    return jnp.concatenate(
        [_rows(parts[n].astype(F32), r) for n, r in SMALL] + [jnp.zeros((SMALL_ROWS - 142, 1024), F32)], axis=0)


def _small(packed, name, shape):
    off, r = SMALL_OFF[name]
    n = math.prod(shape)
    return packed[off:off + r].reshape(-1)[:n].reshape(shape)


def _local_step(x, p, target, w, sm):
    s = x.shape[0]
    mm = _matmul
    w_main = w["w_main"]
    w_u, w_v = w_main[:, 0:1024], w_main[:, 1024:2048]
    w_qkv = w_main[:, 2048:5120]
    w_gates = w_main[:, 5120:7168]
    w_up_a, w_up_g = w["w_up"][:, :D_FF], w["w_up"][:, D_FF:]
    cw_a, cw_g = sm["conv_w"][:, :D_FF], sm["conv_w"][:, D_FF:]
    cb_a, cb_g = sm["conv_b"][:, :D_FF], sm["conv_b"][:, D_FF:]
    bs_t = jnp.pad(sm["gmlp_b_s"].T, ((0, 0), (0, 128 - GROUPS)))
    b_f = jnp.pad(sm["b_f"], ((0, 0), (0, 128 - HEADS)))

    h = _rmsnorm_fwd(x, sm["norm_mix_g"], name="norm_mix")
    zu = mm(h, w_u, mode="nn", out_dtype=F32, name="in_u", tn=1024, tk=1024)
    zv = mm(h, w_v, mode="nn", out_dtype=F32, name="in_v", tn=1024, tk=1024)
    qkv = mm(h, w_qkv, mode="nn", out_dtype=BF16, name="in_qkv", tn=1024, tk=1024)
    gates = mm(h, w_gates, mode="nn", out_dtype=F32, name="in_gates", tn=1024, tk=1024)
    f_logit = mm(h, w["w_f"], mode="nn", out_dtype=F32, name="in_f", tk=1024)
    q, k, v = qkv[:, :1024], qkv[:, 1024:2048], qkv[:, 2048:]
    ga, gb = gates[:, :1024], gates[:, 1024:]
    a = _gmlp_fwd(zu, zv, sm["gmlp_ln_g"], sm["gmlp_ln_b"], sm["gmlp_w_s"], bs_t, name="gmlp_fwd")
    cqe, ckt = _forget_cumsum(f_logit, b_f, name="forget_cumsum")
    ck3 = jnp.pad(ckt[:HEADS].reshape(HEADS // 2, 2, s), ((0, 0), (0, 6), (0, 0)))
    b, lse = _attn_fwd(q, k, v, cqe, ck3, name="attn_fwd")
    ya = mm(a, w["w_a"], mode="nn", out_dtype=F32, name="branch_a", tn=1024, tk=1024)
    yb = mm(b, w["w_b"], mode="nn", out_dtype=F32, name="branch_b", tn=1024, tk=1024)
    merged = _merge_fwd(ya, yb, ga, gb, name="merge_fwd")
    x1 = mm(merged, w["w_out"], mode="nn", out_dtype=F32, name="out_proj", tn=1024, tk=1024, add=x)
    h2 = _rmsnorm_fwd(x1, sm["norm_ffn_g"], name="norm_ffn")
    up_a = mm(h2, w_up_a, mode="nn", out_dtype=F32, name="up_a", tn=256, tk=1024)
    up_g = mm(h2, w_up_g, mode="nn", out_dtype=F32, name="up_g", tn=256, tk=1024)
    act = _convglu_fwd(up_a, up_g, cw_a, cw_g, cb_a, cb_g, name="convglu_fwd")
    x2 = mm(act, w["w_down"], mode="nn", out_dtype=F32, name="down", tn=1024, tk=256, add=x1)
    h3 = _rmsnorm_fwd(x2, sm["norm_ple_g"], name="norm_ple")
    ple = mm(p, w["w_ple"], mode="nn", out_dtype=F32, name="ple", tn=1024, tk=256)
    gp = mm(h3, w["w_pg"], mode="nn", out_dtype=F32, name="ple_gate", tn=1024, tk=1024)
    x3 = _ple_fwd(x2, ple, gp, name="ple_fwd")

    loss, dx3, d_norm_final = _final_loss_bwd(x3, target, sm["norm_final_g"], name="loss_bwd")
    dple, dgp = _ple_bwd(dx3, ple, gp, name="ple_bwd")
    g_w_ple = mm(p, dple, mode="tn", out_dtype=F32, name="d_w_ple", tm=256, tn=1024, tk=512)
    g_w_pg = mm(h3, dgp, mode="tn", out_dtype=F32, name="d_w_pg", tn=1024, tk=512)
    dh3 = mm(dgp, w["w_pg"], mode="nt", out_dtype=F32, name="d_h3", tn=1024, tk=1024)
    dx2, d_norm_ple = _rmsnorm_bwd(dx3, dh3, x2, sm["norm_ple_g"], name="norm_ple_bwd")
    g_w_down = mm(act, dx2, mode="tn", out_dtype=F32, name="d_w_down", tm=256, tn=1024, tk=512)
    dact = mm(dx2, w["w_down"], mode="nt", out_dtype=BF16, name="d_act", tn=256, tk=1024)
    dup_a, dup_g, dcw_a, dcw_g, dcb_a, dcb_g = _convglu_bwd(dact, up_a, up_g, cw_a, cw_g, cb_a, cb_g, name="convglu_bwd")
    g_w_up = jnp.concatenate(
        [mm(h2, dup_a, mode="tn", out_dtype=F32, name="d_w_up_a", tn=256, tk=512),
         mm(h2, dup_g, mode="tn", out_dtype=F32, name="d_w_up_g", tn=256, tk=512)], axis=1)
    dh2 = mm(dup_a, w_up_a, mode="nt", out_dtype=F32, name="d_h2_a", tn=1024, tk=256)
    dh2 = mm(dup_g, w_up_g, mode="nt", out_dtype=F32, name="d_h2_g", tn=1024, tk=256, add=dh2)
    dx1, d_norm_ffn = _rmsnorm_bwd(dx2, dh2, x1, sm["norm_ffn_g"], name="norm_ffn_bwd")
    g_w_out = mm(merged, dx1, mode="tn", out_dtype=F32, name="d_w_out", tn=1024, tk=512)
    dmerged = mm(dx1, w["w_out"], mode="nt", out_dtype=F32, name="d_merged", tn=1024, tk=1024)
    dya, dyb, dga, dgb = _merge_bwd(dmerged, ya, yb, ga, gb, name="merge_bwd")
    g_w_a = mm(a, dya, mode="tn", out_dtype=F32, name="d_w_a", tn=1024, tk=512)
    g_w_b = mm(b, dyb, mode="tn", out_dtype=F32, name="d_w_b", tn=1024, tk=512)
    da = mm(dya, w["w_a"], mode="nt", out_dtype=BF16, name="d_a", tn=1024, tk=1024)
    db = mm(dyb, w["w_b"], mode="nt", out_dtype=BF16, name="d_b", tn=1024, tk=1024)
    dzu, dzv, d_w_s, d_bs_t, d_ln_g, d_ln_b = _gmlp_bwd(
        da, zu, zv, sm["gmlp_ln_g"], sm["gmlp_ln_b"], sm["gmlp_w_s"], bs_t, name="gmlp_bwd")
    dq, dk, dv, dcqe, dck3 = _attn_bwd(q, k, v, b, db, lse, cqe, ck3, name="attn_bwd")
    dcq16 = jnp.pad(dcqe[:, ::HEAD_DIM], ((0, 0), (0, 128 - HEADS)))
    dck16 = jnp.pad(dck3[:, :2, :].reshape(HEADS, s).T, ((0, 0), (0, 128 - HEADS)))
    dzf, d_b_f = _forget_bwd(dcq16, dck16, f_logit, b_f, name="forget_bwd")
    dz = jnp.concatenate([dzu, dzv, dq, dk, dv, dga, dgb], axis=1)
    g_w_main = mm(h, dz, mode="tn", out_dtype=F32, name="d_w_main", tn=1024, tk=512)
    g_w_f = mm(h, dzf, mode="tn", out_dtype=F32, name="d_w_f", tk=512)
    dh = mm(dz, w_main, mode="nt", out_dtype=F32, name="d_h_main", tn=1024, tk=1024)
    dh = mm(dzf, w["w_f"], mode="nt", out_dtype=F32, name="d_h_f", tn=1024, add=dh)
    dx0, d_norm_mix = _rmsnorm_bwd(dx1, dh, x, sm["norm_mix_g"], name="norm_mix_bwd")

    g_w_in = jnp.concatenate([g_w_main[:, :O_F], g_w_f[:, :HEADS], g_w_main[:, O_F:]], axis=1)
    grads = dict(w_in=g_w_in, w_a=g_w_a, w_b=g_w_b, w_out=g_w_out, w_up=g_w_up,
                 conv_w=jnp.concatenate([dcw_a, dcw_g], axis=1), w_down=g_w_down, w_ple=g_w_ple, w_pg=g_w_pg)
    small = dict(norm_mix_g=d_norm_mix, b_f=d_b_f[:, :HEADS], gmlp_ln_g=d_ln_g, gmlp_ln_b=d_ln_b, gmlp_w_s=d_w_s,
                 gmlp_b_s=d_bs_t[:, :GROUPS].T, norm_ffn_g=d_norm_ffn,
                 conv_b=jnp.concatenate([dcb_a, dcb_g], axis=1), norm_ple_g=d_norm_ple, norm_final_g=d_norm_final)
    return loss, dx0, grads, small


SHARDED = (("w_in", "w_in"), ("w_branch_a", "w_a"), ("w_branch_b", "w_b"), ("w_out", "w_out"), ("w_up", "w_up"),
           ("conv_w", "conv_w"), ("w_down", "w_down"), ("w_ple", "w_ple"), ("w_ple_gate", "w_pg"))
REPLICATED = tuple(n for n, _ in SMALL)
WEIGHT_ORDER = ("norm_mix_g", "w_in", "b_f", "gmlp_ln_g", "gmlp_ln_b", "gmlp_w_s", "gmlp_b_s", "w_branch_a",
                "w_branch_b", "w_out", "norm_ffn_g", "w_up", "conv_w", "conv_b", "w_down", "norm_ple_g", "w_ple",
                "w_ple_gate", "norm_final_g")


def kernel(x, p, norm_mix_g, w_in, b_f, gmlp_ln_g, gmlp_ln_b, gmlp_w_s, gmlp_b_s, w_branch_a, w_branch_b, w_out, norm_ffn_g, w_up, conv_w, conv_b, w_down, norm_ple_g, w_ple, w_ple_gate, norm_final_g, loss_target, m_norm_mix_g, m_w_in, m_b_f, m_gmlp_ln_g, m_gmlp_ln_b, m_gmlp_w_s, m_gmlp_b_s, m_w_branch_a, m_w_branch_b, m_w_out, m_norm_ffn_g, m_w_up, m_conv_w, m_conv_b, m_w_down, m_norm_ple_g, m_w_ple, m_w_ple_gate, m_norm_final_g, v_norm_mix_g, v_w_in, v_b_f, v_gmlp_ln_g, v_gmlp_ln_b, v_gmlp_w_s, v_gmlp_b_s, v_w_branch_a, v_w_branch_b, v_w_out, v_norm_ffn_g, v_w_up, v_conv_w, v_conv_b, v_w_down, v_norm_ple_g, v_w_ple, v_w_ple_gate, v_norm_final_g):
    given = dict(locals())
    weights = {n: given[n] for n in WEIGHT_ORDER}
    mom_m = {n: given["m_" + n] for n in WEIGHT_ORDER}
    mom_v = {n: given["v_" + n] for n in WEIGHT_ORDER}
    pos = jnp.stack([lax.axis_index("x"), lax.axis_index("y"), lax.axis_index("c")]).astype(I32)

    def sharded(src):
        return {short: src[long] for long, short in SHARDED}

    gathered = _allgather_weights(_pack_shard(sharded(weights), BF16), name="allgather_weights")
    full = _unpack_gathered(gathered)

    sm = dict(norm_mix_g=norm_mix_g, b_f=b_f, gmlp_ln_g=gmlp_ln_g, gmlp_ln_b=gmlp_ln_b, gmlp_w_s=gmlp_w_s[0],
              gmlp_b_s=gmlp_b_s[0], norm_ffn_g=norm_ffn_g, conv_b=conv_b, norm_ple_g=norm_ple_g,
              norm_final_g=norm_final_g.reshape(1, D_MODEL), conv_w=full["conv_w"].astype(F32))
    loss_part, dx0, grads, small = _local_step(x[0], p[0, 0], loss_target[0], full, sm)

    packed_g = _pack_grads(grads)
    small_g = _pack_small(small)
    from_sib, small_sib = _exchange_sibling(packed_g, small_g, name="exchange_sibling")
    chip_sums = _sum_pairs(packed_g, from_sib, pos, name="sum_sibling")
    small_chip = _pair_sum_small(small_g, small_sib, name="sum_sibling_small")
    from_chips, small_tab = _exchange_chips(chip_sums.astype(BF16), small_chip, name="exchange_chips")

    g_sh, d_sh, m_sh, v_sh = _adam_sharded(
        chip_sums, from_chips, _pack_shard(sharded(weights), F32), _pack_shard(sharded(mom_m), F32),
        _pack_shard(sharded(mom_v), F32), pos, name="adam_sharded")
    rep = lambda src: _pack_small({n: src[n] for n in REPLICATED})
    g_rp, d_rp, m_rp, v_rp = _adam_replicated(small_tab, rep(weights), rep(mom_m), rep(mom_v), name="adam_replicated")

    def unpack(sh, rp):
        out = {}
        for long, short in SHARDED:
            shape = weights[long].shape
            out[long] = _seg(sh, short).reshape(-1)[:math.prod(shape)].reshape(shape)
        for n in REPLICATED:
            out[n] = _small(rp, n, weights[n].shape)
        return [out[n] for n in WEIGHT_ORDER]

    loss = lax.psum(loss_part[0, 0], ("x", "y", "c"))
    return (loss, dx0[None], *unpack(g_sh, g_rp), *unpack(d_sh, d_rp), *unpack(m_sh, m_rp), *unpack(v_sh, v_rp))
```

```python
import functools
import math

import jax
import jax.numpy as jnp
from jax import lax
from jax.experimental import pallas as pl
from jax.experimental.pallas import tpu as pltpu

F32 = jnp.float32
BF16 = jnp.bfloat16
I32 = jnp.int32

D_MODEL = 1024
GROUPS = 8
GDIM = 128
GBLOCK = 128
CHUNK = 64
HEADS = 16
HEAD_DIM = 64
D_FF = 2816
PLE_DIM = 256
EPS = 1e-6
N_DEV = 8
ATT_SCALE = HEAD_DIM ** -0.5
NEG = -1e30

ADAM_LR = 0.001
ADAM_B1 = 0.9
ADAM_B2 = 0.999
ADAM_EPS = 1e-08
ADAM_WD = 0.01
ADAM_STEP = 10

V7X_VMEM_LIMIT = 48 * 1024 * 1024
MESH = pl.DeviceIdType.MESH

O_F = 2 * 1024 + 3 * 1024
O_G = O_F + HEADS
IN_COLS = O_G + 2 * D_MODEL
MAIN_COLS = IN_COLS - HEADS
IN_SHARD = IN_COLS // N_DEV
IN_SHARD_PAD = 912

SHARDED = (("w_in", "cols"), ("w_branch_a", "rows"), ("w_branch_b", "rows"), ("w_out", "rows"), ("w_up", "cols"),
           ("conv_w", "f32"), ("w_down", "rows"), ("w_ple", "cols"), ("w_ple_gate", "rows"))

SMALL = (("norm_mix_g", 8), ("b_f", 8), ("gmlp_ln_g", 8), ("gmlp_ln_b", 8), ("gmlp_w_s", 128), ("gmlp_b_s", 8),
         ("norm_ffn_g", 8), ("conv_b", 8), ("norm_ple_g", 8), ("norm_final_g", 8))
SMALL_OFF = {}
_o = 0
for _n, _r in SMALL:
    SMALL_OFF[_n] = (_o, _r)
    _o += _r
SMALL_ROWS = _o

WEIGHT_ORDER = ("norm_mix_g", "w_in", "b_f", "gmlp_ln_g", "gmlp_ln_b", "gmlp_w_s", "gmlp_b_s", "w_branch_a",
                "w_branch_b", "w_out", "norm_ffn_g", "w_up", "conv_w", "conv_b", "w_down", "norm_ple_g", "w_ple",
                "w_ple_gate", "norm_final_g")


def _cparams(sem):
    return pltpu.CompilerParams(dimension_semantics=sem, vmem_limit_bytes=V7X_VMEM_LIMIT)


def _gelu(x):
    c = math.sqrt(2.0 / math.pi)
    return 0.5 * x * (1.0 + jnp.tanh(c * (x + 0.044715 * x * x * x)))


def _gelu_and_grad(x):
    c = math.sqrt(2.0 / math.pi)
    t = jnp.tanh(c * (x + 0.044715 * x * x * x))
    g = 0.5 * x * (1.0 + t)
    dg = 0.5 * (1.0 + t) + 0.5 * x * (1.0 - t * t) * (c * (1.0 + 3.0 * 0.044715 * x * x))
    return g, dg


def _sigmoid(x):
    return 1.0 / (1.0 + jnp.exp(-x))


def _dot(a, b, dims):
    return lax.dot_general(a, b, (dims, ((), ())), preferred_element_type=F32)


NN = ((1,), (0,))
NT = ((1,), (1,))
TN = ((0,), (0,))


def _row_tile(rows, most):
    best = None
    for t in range(16, min(rows, most) + 1, 16):
        if rows % t == 0:
            best = t
    return best if best is not None else rows


def _matmul(a, b, *, mode, out_dtype, name, tm=512, tn=512, tk=512, add=None):
    if mode == "tn":
        kdim, m = a.shape
    else:
        m, kdim = a.shape
    n = b.shape[0] if mode == "nt" else b.shape[1]
    tm, tn, tk = min(tm, m), min(tn, n), min(tk, kdim)
    assert m % tm == 0 and n % tn == 0 and kdim % tk == 0, (name, m, n, kdim, tm, tn, tk)
    nk = kdim // tk
    dims = {"nn": NN, "nt": NT, "tn": TN}[mode]

    def body(*refs):
        if add is None:
            a_ref, b_ref, o_ref, acc_ref = refs
            add_ref = None
        else:
            a_ref, b_ref, add_ref, o_ref, acc_ref = refs
        k = pl.program_id(2)
        part = _dot(a_ref[...].astype(BF16), b_ref[...].astype(BF16), dims)

        @pl.when(k == 0)
        def _():
            acc_ref[...] = part

        @pl.when(k > 0)
        def _():
            acc_ref[...] += part

        @pl.when(k == nk - 1)
        def _():
            r = acc_ref[...]
            if add_ref is not None:
                r = add_ref[...].astype(F32) + r
            o_ref[...] = r.astype(out_dtype)

    a_spec = pl.BlockSpec((tk, tm), lambda i, j, k: (k, i)) if mode == "tn" else pl.BlockSpec((tm, tk), lambda i, j, k: (i, k))
    b_spec = pl.BlockSpec((tn, tk), lambda i, j, k: (j, k)) if mode == "nt" else pl.BlockSpec((tk, tn), lambda i, j, k: (k, j))
    o_spec = pl.BlockSpec((tm, tn), lambda i, j, k: (i, j))
    in_specs = [a_spec, b_spec] + ([o_spec] if add is not None else [])
    args = (a, b) + ((add,) if add is not None else ())
    return pl.pallas_call(
        body, name=name, grid=(m // tm, n // tn, nk),
        in_specs=in_specs, out_specs=o_spec,
        out_shape=jax.ShapeDtypeStruct((m, n), out_dtype),
        scratch_shapes=[pltpu.VMEM((tm, tn), F32)],
        compiler_params=_cparams(("parallel", "parallel", "arbitrary")),
    )(*args)


def _row_spec(tr, width):
    return pl.BlockSpec((tr, width), lambda i: (i, 0))


def _full_spec(shape):
    return pl.BlockSpec(shape, lambda i: tuple(0 for _ in shape))


def _rmsnorm_fwd(x, g, *, name, tr=256):
    s, d = x.shape

    def body(x_ref, g_ref, o_ref):
        xv = x_ref[...]
        r = lax.rsqrt(jnp.mean(xv * xv, axis=-1, keepdims=True) + EPS)
        o_ref[...] = ((xv * r) * g_ref[...]).astype(BF16)

    return pl.pallas_call(
        body, name=name, grid=(s // tr,),
        in_specs=[_row_spec(tr, d), _full_spec((1, d))], out_specs=_row_spec(tr, d),
        out_shape=jax.ShapeDtypeStruct((s, d), BF16), compiler_params=_cparams(("parallel",)),
    )(x, g)


def _rmsnorm_bwd(dres, dh, x, g, *, name, tr=256):
    s, d = x.shape

    def body(dres_ref, dh_ref, x_ref, g_ref, dx_ref, dg_ref):
        i = pl.program_id(0)
        xv = x_ref[...]
        r = lax.rsqrt(jnp.mean(xv * xv, axis=-1, keepdims=True) + EPS)
        xhat = xv * r
        dhv = dh_ref[...].astype(F32)
        dxhat = dhv * g_ref[...]
        dx = r * (dxhat - xhat * jnp.mean(dxhat * xhat, axis=-1, keepdims=True))
        dx_ref[...] = dres_ref[...] + dx
        dgp = jnp.sum(dhv * xhat, axis=0, keepdims=True)

        @pl.when(i == 0)
        def _():
            dg_ref[...] = dgp

        @pl.when(i > 0)
        def _():
            dg_ref[...] += dgp

    return pl.pallas_call(
        body, name=name, grid=(s // tr,),
        in_specs=[_row_spec(tr, d), _row_spec(tr, d), _row_spec(tr, d), _full_spec((1, d))],
        out_specs=[_row_spec(tr, d), _full_spec((1, d))],
        out_shape=[jax.ShapeDtypeStruct((s, d), F32), jax.ShapeDtypeStruct((1, d), F32)],
        compiler_params=_cparams(("arbitrary",)),
    )(dres, dh, x, g)


def _final_loss_bwd(x3, target, g, *, name, tr=256):
    s, d = x3.shape

    def body(x_ref, t_ref, g_ref, loss_ref, dx_ref, dg_ref):
        i = pl.program_id(0)
        xv = x_ref[...]
        r = lax.rsqrt(jnp.mean(xv * xv, axis=-1, keepdims=True) + EPS)
        xhat = xv * r
        diff = xhat * g_ref[...] - t_ref[...]
        lp = jnp.zeros((1, 128), F32) + (0.5 / d) * jnp.sum(diff * diff)
        dy = diff * (1.0 / d)
        dxhat = dy * g_ref[...]
        dx_ref[...] = r * (dxhat - xhat * jnp.mean(dxhat * xhat, axis=-1, keepdims=True))
        dgp = jnp.sum(dy * xhat, axis=0, keepdims=True)

        @pl.when(i == 0)
        def _():
            dg_ref[...] = dgp
            loss_ref[...] = lp

        @pl.when(i > 0)
        def _():
            dg_ref[...] += dgp
            loss_ref[...] += lp

    return pl.pallas_call(
        body, name=name, grid=(s // tr,),
        in_specs=[_row_spec(tr, d), _row_spec(tr, d), _full_spec((1, d))],
        out_specs=[_full_spec((1, 128)), _row_spec(tr, d), _full_spec((1, d))],
        out_shape=[jax.ShapeDtypeStruct((1, 128), F32), jax.ShapeDtypeStruct((s, d), F32),
                   jax.ShapeDtypeStruct((1, d), F32)],
        compiler_params=_cparams(("arbitrary",)),
    )(x3, target, g)


def _merge_fwd(ya, yb, ga, gb, *, name, tr=256):
    s, d = ya.shape

    def body(ya_ref, yb_ref, ga_ref, gb_ref, o_ref):
        o_ref[...] = (_sigmoid(ga_ref[...]) * ya_ref[...] + _sigmoid(gb_ref[...]) * yb_ref[...]).astype(BF16)

    return pl.pallas_call(
        body, name=name, grid=(s // tr,),
        in_specs=[_row_spec(tr, d)] * 4, out_specs=_row_spec(tr, d),
        out_shape=jax.ShapeDtypeStruct((s, d), BF16), compiler_params=_cparams(("parallel",)),
    )(ya, yb, ga, gb)


def _merge_bwd(dm, ya, yb, ga, gb, *, name, tr=256):
    s, d = ya.shape

    def body(dm_ref, ya_ref, yb_ref, ga_ref, gb_ref, dya_ref, dyb_ref, dga_ref, dgb_ref):
        dmv = dm_ref[...]
        sa = _sigmoid(ga_ref[...])
        sb = _sigmoid(gb_ref[...])
        dya_ref[...] = (dmv * sa).astype(BF16)
        dyb_ref[...] = (dmv * sb).astype(BF16)
        dga_ref[...] = (dmv * ya_ref[...] * (sa * (1.0 - sa))).astype(BF16)
        dgb_ref[...] = (dmv * yb_ref[...] * (sb * (1.0 - sb))).astype(BF16)

    o = jax.ShapeDtypeStruct((s, d), BF16)
    return pl.pallas_call(
        body, name=name, grid=(s // tr,),
        in_specs=[_row_spec(tr, d)] * 5, out_specs=[_row_spec(tr, d)] * 4,
        out_shape=[o, o, o, o], compiler_params=_cparams(("parallel",)),
    )(dm, ya, yb, ga, gb)


def _ple_fwd(x2, ple, gp, *, name, tr=256):
    s, d = x2.shape

    def body(x_ref, ple_ref, gp_ref, o_ref):
        o_ref[...] = x_ref[...] + ple_ref[...] * _sigmoid(gp_ref[...])

    return pl.pallas_call(
        body, name=name, grid=(s // tr,),
        in_specs=[_row_spec(tr, d)] * 3, out_specs=_row_spec(tr, d),
        out_shape=jax.ShapeDtypeStruct((s, d), F32), compiler_params=_cparams(("parallel",)),
    )(x2, ple, gp)


def _ple_bwd(dx3, ple, gp, *, name, tr=256):
    s, d = dx3.shape

    def body(dx_ref, ple_ref, gp_ref, dple_ref, dgp_ref):
        sg = _sigmoid(gp_ref[...])
        dxv = dx_ref[...]
        dple_ref[...] = (dxv * sg).astype(BF16)
        dgp_ref[...] = (dxv * ple_ref[...] * (sg * (1.0 - sg))).astype(BF16)

    o = jax.ShapeDtypeStruct((s, d), BF16)
    return pl.pallas_call(
        body, name=name, grid=(s // tr,),
        in_specs=[_row_spec(tr, d)] * 3, out_specs=[_row_spec(tr, d)] * 2,
        out_shape=[o, o], compiler_params=_cparams(("parallel",)),
    )(dx3, ple, gp)


def _masked_ws(ws_ref, g):
    row = lax.broadcasted_iota(I32, (GBLOCK, GBLOCK), 0)
    col = lax.broadcasted_iota(I32, (GBLOCK, GBLOCK), 1)
    keep = (col // CHUNK) <= (row // CHUNK)
    return jnp.where(keep, ws_ref[g], 0.0), keep


def _layernorm_parts(zv):
    mu = jnp.mean(zv, axis=-1, keepdims=True)
    xc = zv - mu
    rs = lax.rsqrt(jnp.mean(xc * xc, axis=-1, keepdims=True) + EPS)
    return xc * rs, rs


def _gmlp_fwd(zu_pre, zv_pre, ln_g, ln_b, w_s, bs_t, *, name):
    s, w = zu_pre.shape

    def body(zu_ref, zv_ref, lng_ref, lnb_ref, ws_ref, bs_ref, a_ref):
        zu = _gelu(zu_ref[...])
        zv = _gelu(zv_ref[...])
        xhat, _ = _layernorm_parts(zv)
        vln = (xhat * lng_ref[...] + lnb_ref[...]).astype(BF16)
        for g in range(GROUPS):
            wm, _ = _masked_ws(ws_ref, g)
            mixed = _dot(wm.astype(BF16), vln[:, g * GDIM:(g + 1) * GDIM], NN) + bs_ref[:, g:g + 1]
            a_ref[:, g * GDIM:(g + 1) * GDIM] = (zu[:, g * GDIM:(g + 1) * GDIM] * mixed).astype(BF16)

    return pl.pallas_call(
        body, name=name, grid=(s // GBLOCK,),
        in_specs=[_row_spec(GBLOCK, w), _row_spec(GBLOCK, w), _full_spec((1, w)), _full_spec((1, w)),
                  _full_spec((GROUPS, GBLOCK, GBLOCK)), _full_spec((GBLOCK, 128))],
        out_specs=_row_spec(GBLOCK, w),
        out_shape=jax.ShapeDtypeStruct((s, w), BF16), compiler_params=_cparams(("parallel",)),
    )(zu_pre, zv_pre, ln_g, ln_b, w_s, bs_t)


def _gmlp_bwd(da, zu_pre, zv_pre, ln_g, ln_b, w_s, bs_t, *, name):
    s, w = zu_pre.shape

    def body(da_ref, zu_ref, zv_ref, lng_ref, lnb_ref, ws_ref, bs_ref,
             dzu_ref, dzv_ref, dws_ref, dbs_ref, dlng_ref, dlnb_ref, dvln_ref):
        i = pl.program_id(0)
        zu, dzu_g = _gelu_and_grad(zu_ref[...])
        zv, dzv_g = _gelu_and_grad(zv_ref[...])
        xhat, rs = _layernorm_parts(zv)
        vln = (xhat * lng_ref[...] + lnb_ref[...]).astype(BF16)
        dav = da_ref[...].astype(F32)
        lane = lax.broadcasted_iota(I32, (GBLOCK, 128), 1)
        dbs = jnp.zeros((GBLOCK, 128), F32)

        @pl.when(i == 0)
        def _():
            dws_ref[...] = jnp.zeros_like(dws_ref)

        for g in range(GROUPS):
            sl = slice(g * GDIM, (g + 1) * GDIM)
            wm, keep = _masked_ws(ws_ref, g)
            wmb = wm.astype(BF16)
            vg = vln[:, sl]
            mixed = _dot(wmb, vg, NN) + bs_ref[:, g:g + 1]
            dag = dav[:, sl]
            dzu_ref[:, sl] = (dag * mixed * dzu_g[:, sl]).astype(BF16)
            dmix = dag * zu[:, sl]
            dmb = dmix.astype(BF16)
            dws_ref[g] += jnp.where(keep, _dot(dmb, vg, NT), 0.0)
            dbs = jnp.where(lane == g, jnp.sum(dmix, axis=1, keepdims=True), dbs)
            dvln_ref[:, sl] = _dot(wmb, dmb, TN)
        dvln = dvln_ref[...]
        dxhat = dvln * lng_ref[...]
        dzv = rs * (dxhat - jnp.mean(dxhat, axis=-1, keepdims=True)
                    - xhat * jnp.mean(dxhat * xhat, axis=-1, keepdims=True))
        dzv_ref[...] = (dzv * dzv_g).astype(BF16)
        dlng = jnp.sum(dvln * xhat, axis=0, keepdims=True)
        dlnb = jnp.sum(dvln, axis=0, keepdims=True)

        @pl.when(i == 0)
        def _():
            dbs_ref[...] = dbs
            dlng_ref[...] = dlng
            dlnb_ref[...] = dlnb

        @pl.when(i > 0)
        def _():
            dbs_ref[...] += dbs
            dlng_ref[...] += dlng
            dlnb_ref[...] += dlnb

    return pl.pallas_call(
        body, name=name, grid=(s // GBLOCK,),
        in_specs=[_row_spec(GBLOCK, w)] * 3 + [_full_spec((1, w)), _full_spec((1, w)),
                                               _full_spec((GROUPS, GBLOCK, GBLOCK)), _full_spec((GBLOCK, 128))],
        out_specs=[_row_spec(GBLOCK, w), _row_spec(GBLOCK, w), _full_spec((GROUPS, GBLOCK, GBLOCK)),
                   _full_spec((GBLOCK, 128)), _full_spec((1, w)), _full_spec((1, w))],
        out_shape=[jax.ShapeDtypeStruct((s, w), BF16), jax.ShapeDtypeStruct((s, w), BF16),
                   jax.ShapeDtypeStruct((GROUPS, GBLOCK, GBLOCK), F32), jax.ShapeDtypeStruct((GBLOCK, 128), F32),
                   jax.ShapeDtypeStruct((1, w), F32), jax.ShapeDtypeStruct((1, w), F32)],
        scratch_shapes=[pltpu.VMEM((GBLOCK, w), F32)],
        compiler_params=_cparams(("arbitrary",)),
    )(da, zu_pre, zv_pre, ln_g, ln_b, w_s, bs_t)


def _shift_down(u, k):
    row = lax.broadcasted_iota(I32, u.shape, 0)
    return jnp.where(row >= k, pltpu.roll(u, k, 0), 0.0)


def _shift_up(u, k):
    s = u.shape[0]
    row = lax.broadcasted_iota(I32, u.shape, 0)
    return jnp.where(row < s - k, pltpu.roll(u, s - k, 0), 0.0)


def _conv(u, w_ref, b_ref):
    return b_ref[...] + w_ref[0:1, :] * _shift_down(u, 2) + w_ref[1:2, :] * _shift_down(u, 1) + w_ref[2:3, :] * u


def _convglu_fwd(up_a, up_g, cw_a, cw_g, cb_a, cb_g, *, name, tc=256):
    s, f = up_a.shape

    def body(ua_ref, ug_ref, wa_ref, wg_ref, ba_ref, bg_ref, o_ref):
        ca = _conv(ua_ref[...], wa_ref, ba_ref)
        cg = _conv(ug_ref[...], wg_ref, bg_ref)
        o_ref[...] = (_gelu(ca) * cg).astype(BF16)

    col = pl.BlockSpec((s, tc), lambda j: (0, j))
    w3 = pl.BlockSpec((3, tc), lambda j: (0, j))
    b1 = pl.BlockSpec((1, tc), lambda j: (0, j))
    return pl.pallas_call(
        body, name=name, grid=(f // tc,),
        in_specs=[col, col, w3, w3, b1, b1], out_specs=col,
        out_shape=jax.ShapeDtypeStruct((s, f), BF16), compiler_params=_cparams(("parallel",)),
    )(up_a, up_g, cw_a, cw_g, cb_a, cb_g)


def _convglu_bwd(dact, up_a, up_g, cw_a, cw_g, cb_a, cb_g, *, name, tc=256):
    s, f = up_a.shape

    def half(dc, u, w_ref, du_ref, dw_ref, db_ref):
        db_ref[...] = jnp.sum(dc, axis=0, keepdims=True)
        dw_ref[0:1, :] = jnp.sum(dc * _shift_down(u, 2), axis=0, keepdims=True)
        dw_ref[1:2, :] = jnp.sum(dc * _shift_down(u, 1), axis=0, keepdims=True)
        dw_ref[2:3, :] = jnp.sum(dc * u, axis=0, keepdims=True)
        du = w_ref[2:3, :] * dc + w_ref[1:2, :] * _shift_up(dc, 1) + w_ref[0:1, :] * _shift_up(dc, 2)
        du_ref[...] = du.astype(BF16)

    def body(d_ref, ua_ref, ug_ref, wa_ref, wg_ref, ba_ref, bg_ref,
             dua_ref, dug_ref, dwa_ref, dwg_ref, dba_ref, dbg_ref):
        ua = ua_ref[...]
        ug = ug_ref[...]
        ca = _conv(ua, wa_ref, ba_ref)
        cg = _conv(ug, wg_ref, bg_ref)
        ga, dga = _gelu_and_grad(ca)
        dv = d_ref[...].astype(F32)
        half(dv * cg * dga, ua, wa_ref, dua_ref, dwa_ref, dba_ref)
        half(dv * ga, ug, wg_ref, dug_ref, dwg_ref, dbg_ref)

    col = pl.BlockSpec((s, tc), lambda j: (0, j))
    w3 = pl.BlockSpec((3, tc), lambda j: (0, j))
    b1 = pl.BlockSpec((1, tc), lambda j: (0, j))
    return pl.pallas_call(
        body, name=name, grid=(f // tc,),
        in_specs=[col, col, col, w3, w3, b1, b1], out_specs=[col, col, w3, w3, b1, b1],
        out_shape=[jax.ShapeDtypeStruct((s, f), BF16), jax.ShapeDtypeStruct((s, f), BF16),
                   jax.ShapeDtypeStruct((3, f), F32), jax.ShapeDtypeStruct((3, f), F32),
                   jax.ShapeDtypeStruct((1, f), F32), jax.ShapeDtypeStruct((1, f), F32)],
        compiler_params=_cparams(("parallel",)),
    )(dact, up_a, up_g, cw_a, cw_g, cb_a, cb_g)


def _tri_dot(tri, x):
    b0 = x.astype(BF16)
    r1 = x - b0.astype(F32)
    b1 = r1.astype(BF16)
    b2 = (r1 - b1.astype(F32)).astype(BF16)
    return _dot(tri, b0, NN) + _dot(tri, b1, NN) + _dot(tri, b2, NN)


def _log_sigmoid(x):
    return jnp.minimum(x, 0.0) - jnp.log(1.0 + jnp.exp(-jnp.abs(x)))


def _expand_heads(col16, rows):
    head_of_lane = lax.broadcasted_iota(I32, (rows, HEADS * HEAD_DIM), 1) // HEAD_DIM
    out = jnp.zeros((rows, HEADS * HEAD_DIM), F32)
    for h in range(HEADS):
        out = jnp.where(head_of_lane == h, col16[:, h:h + 1], out)
    return out


def _forget_cumsum(f_logit, b_f, *, name):
    s = f_logit.shape[0]
    nb = s // 128

    def body(f_ref, b_ref, cqe_ref, ckt_ref):
        row = lax.broadcasted_iota(I32, (128, 128), 0)
        col = lax.broadcasted_iota(I32, (128, 128), 1)
        tri = (col <= row).astype(BF16)

        def step(n, carry):
            r0 = pl.multiple_of(n * 128, 128)
            lf = _log_sigmoid(f_ref[pl.ds(r0, 128), :] + b_ref[...])
            cum = _tri_dot(tri, lf) + carry
            cqe_ref[pl.ds(r0, 128), :] = _expand_heads(cum, 128)
            ckt_ref[:, pl.ds(r0, 128)] = cum.T
            return cum[127:128, :]

        lax.fori_loop(0, nb, step, jnp.zeros((1, 128), F32))

    return pl.pallas_call(
        body, name=name, grid=(1,),
        in_specs=[_full_spec((s, 128)), _full_spec((1, 128))],
        out_specs=[_full_spec((s, HEADS * HEAD_DIM)), _full_spec((128, s))],
        out_shape=[jax.ShapeDtypeStruct((s, HEADS * HEAD_DIM), F32), jax.ShapeDtypeStruct((128, s), F32)],
        compiler_params=_cparams(("arbitrary",)),
    )(f_logit, b_f)


def _forget_bwd(dcq16, dck16, f_logit, b_f, *, name):
    s = f_logit.shape[0]
    nb = s // 128

    def body(a_ref, k_ref, f_ref, b_ref, df_ref, db_ref):
        row = lax.broadcasted_iota(I32, (128, 128), 0)
        col = lax.broadcasted_iota(I32, (128, 128), 1)
        tri_rev = (col >= row).astype(BF16)

        def step(m, carry):
            suffix, dbsum = carry
            n = nb - 1 - m
            r0 = pl.multiple_of(n * 128, 128)
            dcum = a_ref[pl.ds(r0, 128), :] + k_ref[pl.ds(r0, 128), :]
            dlf = _tri_dot(tri_rev, dcum) + suffix
            df = dlf * _sigmoid(-(f_ref[pl.ds(r0, 128), :] + b_ref[...]))
            df_ref[pl.ds(r0, 128), :] = df.astype(BF16)
            return dlf[0:1, :], dbsum + jnp.sum(df, axis=0, keepdims=True)

        _, dbsum = lax.fori_loop(0, nb, step, (jnp.zeros((1, 128), F32), jnp.zeros((1, 128), F32)))
        db_ref[...] = dbsum

    return pl.pallas_call(
        body, name=name, grid=(1,),
        in_specs=[_full_spec((s, 128))] * 3 + [_full_spec((1, 128))],
        out_specs=[_full_spec((s, 128)), _full_spec((1, 128))],
        out_shape=[jax.ShapeDtypeStruct((s, 128), BF16), jax.ShapeDtypeStruct((1, 128), F32)],
        compiler_params=_cparams(("arbitrary",)),
    )(dcq16, dck16, f_logit, b_f)


ATT_T = 256


def _head_lanes(rows):
    return lax.broadcasted_iota(I32, (rows, 128), 1) < HEAD_DIM


def _attn_fwd(q, k, v, cqe, ck3, *, name):
    s = q.shape[0]
    t = ATT_T
    nq = s // t

    def body(q_ref, k_ref, v_ref, cq_ref, ck_ref, o_ref, lse_ref):
        i = pl.program_id(1)
        first = _head_lanes(t)
        q2 = q_ref[...]
        row = lax.broadcasted_iota(I32, (t, t), 0)
        col = lax.broadcasted_iota(I32, (t, t), 1)
        o_pair = jnp.zeros((t, 128), F32)
        lse_pair = jnp.zeros((t, 128), F32)
        for e in range(2):
            mine = first if e == 0 else jnp.logical_not(first)
            qh = jnp.where(mine, q2, jnp.zeros_like(q2))
            cq = cq_ref[:, e * HEAD_DIM:e * HEAD_DIM + 1]

            def step(j, carry, diag):
                m, l, acc = carry
                c0 = pl.multiple_of(j * t, t)
                kb = k_ref[pl.ds(c0, t), :]
                vb = v_ref[pl.ds(c0, t), :]
                sc = _dot(qh, kb, NT) * ATT_SCALE + cq - ck_ref[0, e:e + 1, pl.ds(c0, t)]
                if diag:
                    sc = jnp.where(col <= row, sc, NEG)
                m_new = jnp.maximum(m, jnp.max(sc, axis=-1, keepdims=True))
                alpha = jnp.exp(m - m_new)
                p = jnp.exp(sc - m_new)
                l = alpha * l + jnp.sum(p, axis=-1, keepdims=True)
                acc = alpha * acc + _dot(p.astype(BF16), vb, NN)
                return m_new, l, acc

            carry = (jnp.full((t, 1), NEG, F32), jnp.zeros((t, 1), F32), jnp.zeros((t, 128), F32))
            carry = lax.fori_loop(0, i, functools.partial(step, diag=False), carry)
            m, l, acc = step(i, carry, True)
            o_pair = jnp.where(mine, acc / l, o_pair)
            lse_pair = jnp.where(mine, m + jnp.log(l), lse_pair)
        o_ref[...] = o_pair.astype(BF16)
        lse_ref[...] = lse_pair

    blk = pl.BlockSpec((t, 128), lambda hp, i: (i, hp))
    full = pl.BlockSpec((s, 128), lambda hp, i: (0, hp))
    return pl.pallas_call(
        body, name=name, grid=(HEADS // 2, nq),
        in_specs=[blk, full, full, blk, pl.BlockSpec((1, 8, s), lambda hp, i: (hp, 0, 0))],
        out_specs=[blk, blk],
        out_shape=[jax.ShapeDtypeStruct((s, HEADS * HEAD_DIM), BF16), jax.ShapeDtypeStruct((s, HEADS * HEAD_DIM), F32)],
        compiler_params=_cparams(("parallel", "arbitrary")),
    )(q, k, v, cqe, ck3)


def _attn_bwd(q, k, v, o, do, lse, cqe, ck3, *, name):
    s = q.shape[0]
    t = ATT_T
    nb = s // t

    def body(q_ref, k_ref, v_ref, o_ref, do_ref, lse_ref, cq_ref, ck_ref,
             dq_ref, dk_ref, dv_ref, dcq_ref, dck_ref, dq_acc, dcq_acc):
        first = _head_lanes(t)
        dq_acc[...] = jnp.zeros_like(dq_acc)
        dcq_acc[...] = jnp.zeros_like(dcq_acc)
        dck_ref[...] = jnp.zeros_like(dck_ref)
        row = lax.broadcasted_iota(I32, (t, t), 0)
        col = lax.broadcasted_iota(I32, (t, t), 1)

        def key_block(j, _):
            c0 = pl.multiple_of(j * t, t)
            kb = k_ref[pl.ds(c0, t), :]
            vb = v_ref[pl.ds(c0, t), :]
            dk_pair = jnp.zeros((t, 128), F32)
            dv_pair = jnp.zeros((t, 128), F32)
            for e in range(2):
                mine = first if e == 0 else jnp.logical_not(first)
                ck = ck_ref[0, e:e + 1, pl.ds(c0, t)]

                def query_block(i, carry):
                    dk_a, dv_a, dck_a = carry
                    r0 = pl.multiple_of(i * t, t)
                    qb = q_ref[pl.ds(r0, t), :]
                    qh = jnp.where(mine, qb, jnp.zeros_like(qb))
                    dob = do_ref[pl.ds(r0, t), :]
                    doh = jnp.where(mine, dob, jnp.zeros_like(dob))
                    delta = jnp.sum(doh.astype(F32) * o_ref[pl.ds(r0, t), :].astype(F32), axis=-1, keepdims=True)
                    cq = cq_ref[pl.ds(r0, t), e * HEAD_DIM:e * HEAD_DIM + 1]
                    lse_q = lse_ref[pl.ds(r0, t), e * HEAD_DIM:e * HEAD_DIM + 1]
                    sc = _dot(qh, kb, NT) * ATT_SCALE + cq - ck
                    sc = jnp.where(col + c0 <= row + r0, sc, NEG)
                    p = jnp.exp(sc - lse_q)
                    dp = _dot(doh, vb, NT)
                    ds = p * (dp - delta)
                    dsb = ds.astype(BF16)
                    dv_a = dv_a + _dot(p.astype(BF16), dob, TN)
                    dk_a = dk_a + _dot(dsb, qb, TN)
                    dqp = _dot(dsb, kb, NN) * ATT_SCALE
                    dq_acc[pl.ds(r0, t), :] += jnp.where(mine, dqp, 0.0)
                    dcq_acc[pl.ds(r0, t), :] += jnp.where(mine, jnp.sum(ds, axis=-1, keepdims=True), 0.0)
                    dck_a = dck_a - jnp.sum(ds, axis=0, keepdims=True)
                    return dk_a, dv_a, dck_a

                zero = jnp.zeros((t, 128), F32)
                dk_a, dv_a, dck_a = lax.fori_loop(j, nb, query_block, (zero, zero, jnp.zeros((1, t), F32)))
                dk_pair = jnp.where(mine, dk_a * ATT_SCALE, dk_pair)
                dv_pair = jnp.where(mine, dv_a, dv_pair)
                dck_ref[0, e:e + 1, pl.ds(c0, t)] = dck_a
            dk_ref[pl.ds(c0, t), :] = dk_pair.astype(BF16)
            dv_ref[pl.ds(c0, t), :] = dv_pair.astype(BF16)
            return 0

        lax.fori_loop(0, nb, key_block, 0)
        dq_ref[...] = dq_acc[...].astype(BF16)
        dcq_ref[...] = dcq_acc[...]

    full = pl.BlockSpec((s, 128), lambda hp: (0, hp))
    ck_spec = pl.BlockSpec((1, 8, s), lambda hp: (hp, 0, 0))
    wide = jax.ShapeDtypeStruct((s, HEADS * HEAD_DIM), BF16)
    return pl.pallas_call(
        body, name=name, grid=(HEADS // 2,),
        in_specs=[full] * 7 + [ck_spec],
        out_specs=[full, full, full, full, ck_spec],
        out_shape=[wide, wide, wide, jax.ShapeDtypeStruct((s, HEADS * HEAD_DIM), F32),
                   jax.ShapeDtypeStruct((HEADS // 2, 8, s), F32)],
        scratch_shapes=[pltpu.VMEM((s, 128), F32), pltpu.VMEM((s, 128), F32)],
        compiler_params=_cparams(("parallel",)),
    )(q, k, v, o, do, lse, cqe, ck3)


def _adam_math(w, g, m, v):
    m = ADAM_B1 * m + (1.0 - ADAM_B1) * g
    v = ADAM_B2 * v + (1.0 - ADAM_B2) * (g * g)
    m_hat = m / (1.0 - ADAM_B1 ** ADAM_STEP)
    v_hat = v / (1.0 - ADAM_B2 ** ADAM_STEP)
    delta = -ADAM_LR * (m_hat / (jnp.sqrt(v_hat) + ADAM_EPS) + ADAM_WD * w)
    return delta, m, v


def _sum_pairs(keep, recv, pos, *, name):
    _, r, c = recv.shape
    tr = _row_tile(r, 512)

    def body(pos_ref, a_ref, b_ref, o32_ref, o16_ref):
        tot = a_ref[...].astype(F32) + b_ref[...].astype(F32)
        o32_ref[...] = tot
        o16_ref[...] = tot.astype(BF16)

    out = pl.BlockSpec((1, tr, c), lambda q, i, pos: (q, i, 0))
    grid_spec = pltpu.PrefetchScalarGridSpec(
        num_scalar_prefetch=1, grid=(4, r // tr),
        in_specs=[pl.BlockSpec((1, tr, c), lambda q, i, pos: (2 * q + pos[2], i, 0)), out],
        out_specs=[out, out])
    return pl.pallas_call(
        body, name=name, grid_spec=grid_spec,
        out_shape=[jax.ShapeDtypeStruct((4, r, c), F32), jax.ShapeDtypeStruct((4, r, c), BF16)],
        compiler_params=_cparams(("parallel", "parallel")),
    )(pos, keep, recv)


def _sum_chips(psum, recv, pos, *, name):
    _, r, c = recv.shape
    tr = _row_tile(r, 512)

    def body(pos_ref, p_ref, r_ref, g_ref):
        g_ref[...] = p_ref[0] + r_ref[0].astype(F32) + r_ref[1].astype(F32) + r_ref[2].astype(F32)

    grid_spec = pltpu.PrefetchScalarGridSpec(
        num_scalar_prefetch=1, grid=(r // tr,),
        in_specs=[pl.BlockSpec((1, tr, c), lambda i, pos: (2 * pos[0] + pos[1], i, 0)),
                  pl.BlockSpec((3, tr, c), lambda i, pos: (0, i, 0))],
        out_specs=pl.BlockSpec((tr, c), lambda i, pos: (i, 0)))
    return pl.pallas_call(
        body, name=name, grid_spec=grid_spec, out_shape=jax.ShapeDtypeStruct((r, c), F32),
        compiler_params=_cparams(("parallel",)),
    )(pos, psum, recv)


def _adam(g, w, m, v, *, name):
    r, c = w.shape
    tr = _row_tile(r, 256)

    def body(g_ref, w_ref, m_ref, v_ref, d_ref, mo_ref, vo_ref):
        delta, mn, vn = _adam_math(w_ref[...], g_ref[...], m_ref[...], v_ref[...])
        d_ref[...] = delta
        mo_ref[...] = mn
        vo_ref[...] = vn

    blk = _row_spec(tr, c)
    o = jax.ShapeDtypeStruct((r, c), F32)
    return pl.pallas_call(
        body, name=name, grid=(r // tr,), in_specs=[blk] * 4, out_specs=[blk] * 3, out_shape=[o, o, o],
        compiler_params=_cparams(("parallel",)),
    )(g, w, m, v)


def _adam_replicated(chip_sums, w, m, v, *, name):
    r = w.shape[0]

    def body(s_ref, w_ref, m_ref, v_ref, g_ref, d_ref, mo_ref, vo_ref):
        g = ((s_ref[0] + s_ref[1]) + s_ref[2]) + s_ref[3]
        delta, mn, vn = _adam_math(w_ref[...], g, m_ref[...], v_ref[...])
        g_ref[...] = g
        d_ref[...] = delta
        mo_ref[...] = mn
        vo_ref[...] = vn

    o = jax.ShapeDtypeStruct((r, 1024), F32)
    full = _full_spec((r, 1024))
    return pl.pallas_call(
        body, name=name, grid=(1,),
        in_specs=[_full_spec((4, r, 1024)), full, full, full], out_specs=[full] * 4, out_shape=[o] * 4,
        compiler_params=_cparams(("arbitrary",)),
    )(chip_sums, w, m, v)


def _pair_sum_small(mine, theirs, *, name):
    def body(a_ref, b_ref, o_ref):
        o_ref[...] = a_ref[...] + b_ref[...]

    full = _full_spec(mine.shape)
    return pl.pallas_call(
        body, name=name, grid=(1,), in_specs=[full, full], out_specs=full,
        out_shape=jax.ShapeDtypeStruct(mine.shape, F32), compiler_params=_cparams(("arbitrary",)),
    )(mine, theirs)


ANY = pl.BlockSpec(memory_space=pl.ANY)
OTHER_CHIPS = ((1, 0), (0, 1), (1, 1))


def _allgather(shards, *, name):
    n = len(shards)

    def body(*refs):
        x_refs, out_refs = refs[:n], refs[n:2 * n]
        send_sems, recv_sems, local_sems = refs[2 * n:]
        x, y, c = lax.axis_index("x"), lax.axis_index("y"), lax.axis_index("c")
        me, sibling = (x, y, c), (x, y, 1 - c)
        chips = [(x ^ fx, y ^ fy) for fx, fy in OTHER_CHIPS]

        def copy(t, k, block, to, from_input=False):
            px, py, pc = block
            slab = out_refs[t].at[4 * px + 2 * py + pc]
            return pltpu.make_async_remote_copy(
                src_ref=x_refs[t] if from_input else slab, dst_ref=slab,
                send_sem=send_sems.at[7 * t + k], recv_sem=recv_sems.at[7 * t + k], device_id=to, device_id_type=MESH)

        mine = [pltpu.make_async_copy(x_refs[t], out_refs[t].at[4 * x + 2 * y + c], local_sems.at[t]) for t in range(n)]
        for cp in mine:
            cp.start()
        first = []
        for t in range(n):
            first.append(copy(t, 0, me, sibling, from_input=True))
            first += [copy(t, 1 + j, me, (*chip, c), from_input=True) for j, chip in enumerate(chips)]
        for cp in first:
            cp.start()
        passed = []
        for j, chip in enumerate(chips):
            for t in range(n):
                copy(t, 1 + j, (*chip, c), me).wait_recv()
                fwd = copy(t, 4 + j, (*chip, c), sibling)
                fwd.start()
                passed.append(fwd)
        for t in range(n):
            copy(t, 0, sibling, me).wait_recv()
            for j, chip in enumerate(chips):
                copy(t, 4 + j, (*chip, 1 - c), me).wait_recv()
        for cp in first + passed:
            cp.wait_send()
        for cp in mine:
            cp.wait()

    return pl.pallas_call(
        body, name=name, out_shape=[jax.ShapeDtypeStruct((N_DEV,) + a.shape, a.dtype) for a in shards],
        in_specs=[ANY] * n, out_specs=[ANY] * n,
        scratch_shapes=[pltpu.SemaphoreType.DMA((7 * n,)), pltpu.SemaphoreType.DMA((7 * n,)),
                        pltpu.SemaphoreType.DMA((n,))],
    )(*shards)


def _exchange_sibling(slabs, small, *, name):
    n = len(slabs)

    def body(*refs):
        g_refs, s_ref = refs[:n], refs[n]
        rg_refs, rs_ref = refs[n + 1:2 * n + 1], refs[2 * n + 1]
        send_sems, recv_sems = refs[2 * n + 2:]
        x, y, c = lax.axis_index("x"), lax.axis_index("y"), lax.axis_index("c")
        sibling = (x, y, 1 - c)
        copies = []
        for t in range(n):
            for q in range(4):
                copies.append(pltpu.make_async_remote_copy(
                    src_ref=g_refs[t].at[2 * q + (1 - c)], dst_ref=rg_refs[t].at[q],
                    send_sem=send_sems.at[4 * t + q], recv_sem=recv_sems.at[4 * t + q],
                    device_id=sibling, device_id_type=MESH))
        copies.append(pltpu.make_async_remote_copy(
            src_ref=s_ref, dst_ref=rs_ref, send_sem=send_sems.at[4 * n], recv_sem=recv_sems.at[4 * n],
            device_id=sibling, device_id_type=MESH))
        for cp in copies:
            cp.start()
        for cp in copies:
            cp.wait()

    return pl.pallas_call(
        body, name=name,
        out_shape=[jax.ShapeDtypeStruct((4,) + a.shape[1:], a.dtype) for a in slabs]
        + [jax.ShapeDtypeStruct(small.shape, small.dtype)],
        in_specs=[ANY] * (n + 1), out_specs=[ANY] * (n + 1),
        scratch_shapes=[pltpu.SemaphoreType.DMA((4 * n + 1,)), pltpu.SemaphoreType.DMA((4 * n + 1,))],
    )(*slabs, small)


def _exchange_chips(psums, small_sum, *, name):
    n = len(psums)
    rs = small_sum.shape[0]

    def body(*refs):
        p_refs, s_ref = refs[:n], refs[n]
        rp_refs, tab_ref = refs[n + 1:2 * n + 1], refs[2 * n + 1]
        send_sems, recv_sems, local_sem = refs[2 * n + 2:]
        x, y, c = lax.axis_index("x"), lax.axis_index("y"), lax.axis_index("c")
        mine = pltpu.make_async_copy(s_ref, tab_ref.at[2 * x + y], local_sem)
        mine.start()

        def table_copy(k, px, py, slot):
            return pltpu.make_async_remote_copy(
                src_ref=s_ref, dst_ref=tab_ref.at[slot], send_sem=send_sems.at[3 * n + k],
                recv_sem=recv_sems.at[3 * n + k], device_id=(px, py, c), device_id_type=MESH)

        copies = []
        for k, (fx, fy) in enumerate(OTHER_CHIPS):
            px, py = x ^ fx, y ^ fy
            for t in range(n):
                copies.append(pltpu.make_async_remote_copy(
                    src_ref=p_refs[t].at[2 * px + py], dst_ref=rp_refs[t].at[k],
                    send_sem=send_sems.at[3 * t + k], recv_sem=recv_sems.at[3 * t + k],
                    device_id=(px, py, c), device_id_type=MESH))
            copies.append(table_copy(k, px, py, 2 * x + y))
        for cp in copies:
            cp.start()
        for k, (fx, fy) in enumerate(OTHER_CHIPS):
            px, py = x ^ fx, y ^ fy
            for t in range(n):
                copies[k * (n + 1) + t].wait()
            table_copy(k, px, py, 2 * px + py).wait()
        mine.wait()

    return pl.pallas_call(
        body, name=name,
        out_shape=[jax.ShapeDtypeStruct((3,) + a.shape[1:], a.dtype) for a in psums]
        + [jax.ShapeDtypeStruct((4, rs, 1024), F32)],
        in_specs=[ANY] * (n + 1), out_specs=[ANY] * (n + 1),
        scratch_shapes=[pltpu.SemaphoreType.DMA((3 * n + 3,)), pltpu.SemaphoreType.DMA((3 * n + 3,)),
                        pltpu.SemaphoreType.DMA],
    )(*psums, small_sum)


def _to_comm(name, kind, block):
    a = block[0]
    if kind == "cols":
        a = a.T
        if name == "w_in":
            a = jnp.pad(a, ((0, IN_SHARD_PAD - IN_SHARD), (0, 0)))
    return a if kind == "f32" else a.astype(BF16)


def _from_comm(name, kind, a):
    if kind == "cols":
        if name == "w_in":
            a = a[:IN_SHARD]
        a = a.T
    return a[None]


def _assemble_weights(g):
    wt_in = g["w_in"][:, :IN_SHARD].reshape(IN_COLS, D_MODEL)
    return dict(
        wt_main=jnp.concatenate([wt_in[:O_F], wt_in[O_G:]], axis=0),
        wt_f=jnp.pad(wt_in[O_F:O_G], ((0, 128 - HEADS), (0, 0))),
        w_a=g["w_branch_a"].reshape(D_MODEL, D_MODEL), w_b=g["w_branch_b"].reshape(D_MODEL, D_MODEL),
        w_out=g["w_out"].reshape(D_MODEL, D_MODEL), wt_up=g["w_up"].reshape(2 * D_FF, D_MODEL),
        conv_w=g["conv_w"].transpose(1, 0, 2).reshape(3, 2 * D_FF),
        w_down=g["w_down"].reshape(D_FF, D_MODEL), wt_ple=g["w_ple"].reshape(D_MODEL, PLE_DIM),
        w_pg=g["w_ple_gate"].reshape(D_MODEL, D_MODEL))


def _grad_slabs(gr):
    gt_in = jnp.concatenate([gr["wt_main"][:O_F], gr["wt_f"][:HEADS], gr["wt_main"][O_F:]], axis=0)
    gt_in = jnp.pad(gt_in.reshape(N_DEV, IN_SHARD, D_MODEL), ((0, 0), (0, IN_SHARD_PAD - IN_SHARD), (0, 0)))
    cw = gr["conv_w"]
    out = dict(
        w_in=gt_in, w_branch_a=gr["w_a"].reshape(N_DEV, -1, D_MODEL), w_branch_b=gr["w_b"].reshape(N_DEV, -1, D_MODEL),
        w_out=gr["w_out"].reshape(N_DEV, -1, D_MODEL), w_up=gr["wt_up"].reshape(N_DEV, -1, D_MODEL),
        conv_w=cw.reshape(3, N_DEV, -1).transpose(1, 0, 2), w_down=gr["w_down"].reshape(N_DEV, -1, D_MODEL),
        w_ple=gr["wt_ple"].reshape(N_DEV, -1, PLE_DIM), w_ple_gate=gr["w_pg"].reshape(N_DEV, -1, D_MODEL))
    return {k: v.astype(BF16) for k, v in out.items()}


def _rows(a, rows):
    flat = a.reshape(-1)
    return jnp.pad(flat, (0, rows * 1024 - flat.shape[0])).reshape(rows, 1024)


def _pack_small(parts):
    return jnp.concatenate([_rows(parts[n].astype(F32), r) for n, r in SMALL], axis=0)


def _small(packed, name, shape):
    off, r = SMALL_OFF[name]
    n = math.prod(shape)
    return packed[off:off + r].reshape(-1)[:n].reshape(shape)


def _local_step(x, p, target, w, sm):
    s = x.shape[0]
    mm = _matmul
    wt_main = w["wt_main"]
    wt_u, wt_v = wt_main[0:1024], wt_main[1024:2048]
    wt_qkv = wt_main[2048:5120]
    wt_gates = wt_main[5120:7168]
    wt_up_a, wt_up_g = w["wt_up"][:D_FF], w["wt_up"][D_FF:]
    cw_a, cw_g = w["conv_w"][:, :D_FF], w["conv_w"][:, D_FF:]
    cb_a, cb_g = sm["conv_b"][:, :D_FF], sm["conv_b"][:, D_FF:]
    bs_t = jnp.pad(sm["gmlp_b_s"].T, ((0, 0), (0, 128 - GROUPS)))
    b_f = jnp.pad(sm["b_f"], ((0, 0), (0, 128 - HEADS)))

    h = _rmsnorm_fwd(x, sm["norm_mix_g"], name="norm_mix")
    zu = mm(h, wt_u, mode="nt", out_dtype=F32, name="in_u", tn=1024, tk=1024)
    zv = mm(h, wt_v, mode="nt", out_dtype=F32, name="in_v", tn=1024, tk=1024)
    qkv = mm(h, wt_qkv, mode="nt", out_dtype=BF16, name="in_qkv", tn=1024, tk=1024)
    gates = mm(h, wt_gates, mode="nt", out_dtype=F32, name="in_gates", tn=1024, tk=1024)
    f_logit = mm(h, w["wt_f"], mode="nt", out_dtype=F32, name="in_f", tk=1024)
    q, k, v = qkv[:, :1024], qkv[:, 1024:2048], qkv[:, 2048:]
    ga, gb = gates[:, :1024], gates[:, 1024:]
    a = _gmlp_fwd(zu, zv, sm["gmlp_ln_g"], sm["gmlp_ln_b"], sm["gmlp_w_s"], bs_t, name="gmlp_fwd")
    cqe, ckt = _forget_cumsum(f_logit, b_f, name="forget_cumsum")
    ck3 = jnp.pad(ckt[:HEADS].reshape(HEADS // 2, 2, s), ((0, 0), (0, 6), (0, 0)))
    b, lse = _attn_fwd(q, k, v, cqe, ck3, name="attn_fwd")
    ya = mm(a, w["w_a"], mode="nn", out_dtype=F32, name="branch_a", tn=1024, tk=1024)
    yb = mm(b, w["w_b"], mode="nn", out_dtype=F32, name="branch_b", tn=1024, tk=1024)
    merged = _merge_fwd(ya, yb, ga, gb, name="merge_fwd")
    x1 = mm(merged, w["w_out"], mode="nn", out_dtype=F32, name="out_proj", tn=1024, tk=1024, add=x)
    h2 = _rmsnorm_fwd(x1, sm["norm_ffn_g"], name="norm_ffn")
    up_a = mm(h2, wt_up_a, mode="nt", out_dtype=F32, name="up_a", tn=256, tk=1024)
    up_g = mm(h2, wt_up_g, mode="nt", out_dtype=F32, name="up_g", tn=256, tk=1024)
    act = _convglu_fwd(up_a, up_g, cw_a, cw_g, cb_a, cb_g, name="convglu_fwd")
    x2 = mm(act, w["w_down"], mode="nn", out_dtype=F32, name="down", tn=1024, tk=256, add=x1)
    h3 = _rmsnorm_fwd(x2, sm["norm_ple_g"], name="norm_ple")
    ple = mm(p, w["wt_ple"], mode="nt", out_dtype=F32, name="ple", tn=1024, tk=256)
    gp = mm(h3, w["w_pg"], mode="nn", out_dtype=F32, name="ple_gate", tn=1024, tk=1024)
    x3 = _ple_fwd(x2, ple, gp, name="ple_fwd")

    loss, dx3, d_norm_final = _final_loss_bwd(x3, target, sm["norm_final_g"], name="loss_bwd")
    dple, dgp = _ple_bwd(dx3, ple, gp, name="ple_bwd")
    g_wt_ple = mm(dple, p, mode="tn", out_dtype=BF16, name="d_w_ple", tn=256, tk=512)
    g_w_pg = mm(h3, dgp, mode="tn", out_dtype=BF16, name="d_w_pg", tn=1024, tk=512)
    dh3 = mm(dgp, w["w_pg"], mode="nt", out_dtype=F32, name="d_h3", tn=1024, tk=1024)
    dx2, d_norm_ple = _rmsnorm_bwd(dx3, dh3, x2, sm["norm_ple_g"], name="norm_ple_bwd")
    g_w_down = mm(act, dx2, mode="tn", out_dtype=BF16, name="d_w_down", tm=256, tn=1024, tk=512)
    dact = mm(dx2, w["w_down"], mode="nt", out_dtype=BF16, name="d_act", tn=256, tk=1024)
    dup_a, dup_g, dcw_a, dcw_g, dcb_a, dcb_g = _convglu_bwd(dact, up_a, up_g, cw_a, cw_g, cb_a, cb_g, name="convglu_bwd")
    g_wt_up = jnp.concatenate(
        [mm(dup_a, h2, mode="tn", out_dtype=BF16, name="d_w_up_a", tm=256, tn=1024, tk=512),
         mm(dup_g, h2, mode="tn", out_dtype=BF16, name="d_w_up_g", tm=256, tn=1024, tk=512)], axis=0)
    dh2 = mm(dup_a, wt_up_a, mode="nn", out_dtype=F32, name="d_h2_a", tn=1024, tk=256)
    dh2 = mm(dup_g, wt_up_g, mode="nn", out_dtype=F32, name="d_h2_g", tn=1024, tk=256, add=dh2)
    dx1, d_norm_ffn = _rmsnorm_bwd(dx2, dh2, x1, sm["norm_ffn_g"], name="norm_ffn_bwd")
    g_w_out = mm(merged, dx1, mode="tn", out_dtype=BF16, name="d_w_out", tn=1024, tk=512)
    dmerged = mm(dx1, w["w_out"], mode="nt", out_dtype=F32, name="d_merged", tn=1024, tk=1024)
    dya, dyb, dga, dgb = _merge_bwd(dmerged, ya, yb, ga, gb, name="merge_bwd")
    g_w_a = mm(a, dya, mode="tn", out_dtype=BF16, name="d_w_a", tn=1024, tk=512)
    g_w_b = mm(b, dyb, mode="tn", out_dtype=BF16, name="d_w_b", tn=1024, tk=512)
    da = mm(dya, w["w_a"], mode="nt", out_dtype=BF16, name="d_a", tn=1024, tk=1024)
    db = mm(dyb, w["w_b"], mode="nt", out_dtype=BF16, name="d_b", tn=1024, tk=1024)
    dzu, dzv, d_w_s, d_bs_t, d_ln_g, d_ln_b = _gmlp_bwd(
        da, zu, zv, sm["gmlp_ln_g"], sm["gmlp_ln_b"], sm["gmlp_w_s"], bs_t, name="gmlp_bwd")
    dq, dk, dv, dcqe, dck3 = _attn_bwd(q, k, v, b, db, lse, cqe, ck3, name="attn_bwd")
    dcq16 = jnp.pad(dcqe[:, ::HEAD_DIM], ((0, 0), (0, 128 - HEADS)))
    dck16 = jnp.pad(dck3[:, :2, :].reshape(HEADS, s).T, ((0, 0), (0, 128 - HEADS)))
    dzf, d_b_f = _forget_bwd(dcq16, dck16, f_logit, b_f, name="forget_bwd")
    dz = jnp.concatenate([dzu, dzv, dq, dk, dv, dga, dgb], axis=1)
    g_wt_main = mm(dz, h, mode="tn", out_dtype=BF16, name="d_w_main", tn=1024, tk=512)
    g_wt_f = mm(dzf, h, mode="tn", out_dtype=BF16, name="d_w_f", tn=1024, tk=512)
    dh = mm(dz, wt_main, mode="nn", out_dtype=F32, name="d_h_main", tn=1024, tk=1024)
    dh = mm(dzf, w["wt_f"], mode="nn", out_dtype=F32, name="d_h_f", tn=1024, add=dh)
    dx0, d_norm_mix = _rmsnorm_bwd(dx1, dh, x, sm["norm_mix_g"], name="norm_mix_bwd")

    grads = dict(wt_main=g_wt_main, wt_f=g_wt_f, w_a=g_w_a, w_b=g_w_b, w_out=g_w_out, wt_up=g_wt_up,
                 conv_w=jnp.concatenate([dcw_a, dcw_g], axis=1), w_down=g_w_down, wt_ple=g_wt_ple, w_pg=g_w_pg)
    small = dict(norm_mix_g=d_norm_mix, b_f=d_b_f[:, :HEADS], gmlp_ln_g=d_ln_g, gmlp_ln_b=d_ln_b, gmlp_w_s=d_w_s,
                 gmlp_b_s=d_bs_t[:, :GROUPS].T, norm_ffn_g=d_norm_ffn,
                 conv_b=jnp.concatenate([dcb_a, dcb_g], axis=1), norm_ple_g=d_norm_ple, norm_final_g=d_norm_final)
    return loss, dx0, grads, small


def kernel(x, p, norm_mix_g, w_in, b_f, gmlp_ln_g, gmlp_ln_b, gmlp_w_s, gmlp_b_s, w_branch_a, w_branch_b, w_out, norm_ffn_g, w_up, conv_w, conv_b, w_down, norm_ple_g, w_ple, w_ple_gate, norm_final_g, loss_target, m_norm_mix_g, m_w_in, m_b_f, m_gmlp_ln_g, m_gmlp_ln_b, m_gmlp_w_s, m_gmlp_b_s, m_w_branch_a, m_w_branch_b, m_w_out, m_norm_ffn_g, m_w_up, m_conv_w, m_conv_b, m_w_down, m_norm_ple_g, m_w_ple, m_w_ple_gate, m_norm_final_g, v_norm_mix_g, v_w_in, v_b_f, v_gmlp_ln_g, v_gmlp_ln_b, v_gmlp_w_s, v_gmlp_b_s, v_w_branch_a, v_w_branch_b, v_w_out, v_norm_ffn_g, v_w_up, v_conv_w, v_conv_b, v_w_down, v_norm_ple_g, v_w_ple, v_w_ple_gate, v_norm_final_g):
    given = dict(locals())
    weights = {n: given[n] for n in WEIGHT_ORDER}
    mom_m = {n: given["m_" + n] for n in WEIGHT_ORDER}
    mom_v = {n: given["v_" + n] for n in WEIGHT_ORDER}
    pos = jnp.stack([lax.axis_index("x"), lax.axis_index("y"), lax.axis_index("c")]).astype(I32)
    names = [n for n, _ in SHARDED]
    kinds = dict(SHARDED)

    gathered = _allgather([_to_comm(n, kinds[n], weights[n]) for n in names], name="allgather_weights")
    full = _assemble_weights(dict(zip(names, gathered)))

    sm = dict(norm_mix_g=norm_mix_g, b_f=b_f, gmlp_ln_g=gmlp_ln_g, gmlp_ln_b=gmlp_ln_b, gmlp_w_s=gmlp_w_s[0],
              gmlp_b_s=gmlp_b_s[0], norm_ffn_g=norm_ffn_g, conv_b=conv_b, norm_ple_g=norm_ple_g,
              norm_final_g=norm_final_g.reshape(1, D_MODEL))
    loss_part, dx0, grads, small = _local_step(x[0], p[0, 0], loss_target[0], full, sm)

    slabs = _grad_slabs(grads)
    small_g = _pack_small(small)
    *from_sib, small_sib = _exchange_sibling([slabs[n] for n in names], small_g, name="exchange_sibling")
    sums = [_sum_pairs(slabs[n], r, pos, name="sum_sibling_" + n) for n, r in zip(names, from_sib)]
    small_chip = _pair_sum_small(small_g, small_sib, name="sum_sibling_small")
    *from_chips, small_tab = _exchange_chips([s16 for _, s16 in sums], small_chip, name="exchange_chips")

    grad, delta, new_m, new_v = {}, {}, {}, {}
    for n, (s32, _), r in zip(names, sums, from_chips):
        g = _from_comm(n, kinds[n], _sum_chips(s32, r, pos, name="sum_chips_" + n))
        d, mn, vn = _adam(g[0], weights[n][0], mom_m[n][0], mom_v[n][0], name="adam_" + n)
        grad[n], delta[n], new_m[n], new_v[n] = g, d[None], mn[None], vn[None]
    replicated = [n for n, _ in SMALL]
    rep = lambda src: _pack_small({n: src[n] for n in replicated})
    packed = _adam_replicated(small_tab, rep(weights), rep(mom_m), rep(mom_v), name="adam_replicated")
    for out, pk in zip((grad, delta, new_m, new_v), packed):
        for n in replicated:
            out[n] = _small(pk, n, weights[n].shape)

    loss = lax.psum(loss_part[0, 0], ("x", "y", "c"))
    return (loss, dx0[None], *[grad[n] for n in WEIGHT_ORDER], *[delta[n] for n in WEIGHT_ORDER],
            *[new_m[n] for n in WEIGHT_ORDER], *[new_v[n] for n in WEIGHT_ORDER])
```

```python
import functools
import math

import jax
import jax.numpy as jnp
from jax import lax
from jax.experimental import pallas as pl
from jax.experimental.pallas import tpu as pltpu

F32 = jnp.float32
BF16 = jnp.bfloat16
I32 = jnp.int32

D_MODEL = 1024
GROUPS = 8
GDIM = 128
GBLOCK = 128
CHUNK = 64
HEADS = 16
HEAD_DIM = 64
D_FF = 2816
PLE_DIM = 256
EPS = 1e-6
N_DEV = 8
ATT_SCALE = HEAD_DIM ** -0.5
NEG = -1e30

ADAM_LR = 0.001
ADAM_B1 = 0.9
ADAM_B2 = 0.999
ADAM_EPS = 1e-08
ADAM_WD = 0.01
ADAM_STEP = 10

V7X_VMEM_LIMIT = 48 * 1024 * 1024
MESH = pl.DeviceIdType.MESH

O_F = 2 * 1024 + 3 * 1024
O_G = O_F + HEADS
IN_COLS = O_G + 2 * D_MODEL
MAIN_COLS = IN_COLS - HEADS
IN_SHARD = IN_COLS // N_DEV
IN_SHARD_PAD = 912

SHARDED = (("w_in", "cols"), ("w_branch_a", "rows"), ("w_branch_b", "rows"), ("w_out", "rows"), ("w_up", "cols"),
           ("conv_w", "f32"), ("w_down", "rows"), ("w_ple", "cols"), ("w_ple_gate", "rows"))

SMALL = (("norm_mix_g", 8), ("b_f", 8), ("gmlp_ln_g", 8), ("gmlp_ln_b", 8), ("gmlp_w_s", 128), ("gmlp_b_s", 8),
         ("norm_ffn_g", 8), ("conv_b", 8), ("norm_ple_g", 8), ("norm_final_g", 8))
SMALL_OFF = {}
_o = 0
for _n, _r in SMALL:
    SMALL_OFF[_n] = (_o, _r)
    _o += _r
SMALL_ROWS = _o

WEIGHT_ORDER = ("norm_mix_g", "w_in", "b_f", "gmlp_ln_g", "gmlp_ln_b", "gmlp_w_s", "gmlp_b_s", "w_branch_a",
                "w_branch_b", "w_out", "norm_ffn_g", "w_up", "conv_w", "conv_b", "w_down", "norm_ple_g", "w_ple",
                "w_ple_gate", "norm_final_g")


def _cparams(sem):
    return pltpu.CompilerParams(dimension_semantics=sem, vmem_limit_bytes=V7X_VMEM_LIMIT)


def _gelu(x):
    c = math.sqrt(2.0 / math.pi)
    return 0.5 * x * (1.0 + jnp.tanh(c * (x + 0.044715 * x * x * x)))


def _gelu_and_grad(x):
    c = math.sqrt(2.0 / math.pi)
    t = jnp.tanh(c * (x + 0.044715 * x * x * x))
    g = 0.5 * x * (1.0 + t)
    dg = 0.5 * (1.0 + t) + 0.5 * x * (1.0 - t * t) * (c * (1.0 + 3.0 * 0.044715 * x * x))
    return g, dg


def _sigmoid(x):
    return 1.0 / (1.0 + jnp.exp(-x))


def _dot(a, b, dims):
    return lax.dot_general(a, b, (dims, ((), ())), preferred_element_type=F32)


NN = ((1,), (0,))
NT = ((1,), (1,))
TN = ((0,), (0,))


def _row_tile(rows, most):
    best = None
    for t in range(16, min(rows, most) + 1, 16):
        if rows % t == 0:
            best = t
    return best if best is not None else rows


def _matmul(a, b, *, mode, out_dtype, name, tm=512, tn=512, tk=512, add=None):
    if mode == "tn":
        kdim, m = a.shape
    else:
        m, kdim = a.shape
    n = b.shape[0] if mode == "nt" else b.shape[1]
    tm, tn, tk = min(tm, m), min(tn, n), min(tk, kdim)
    assert m % tm == 0 and n % tn == 0 and kdim % tk == 0, (name, m, n, kdim, tm, tn, tk)
    nk = kdim // tk
    dims = {"nn": NN, "nt": NT, "tn": TN}[mode]

    def body(*refs):
        if add is None:
            a_ref, b_ref, o_ref, acc_ref = refs
            add_ref = None
        else:
            a_ref, b_ref, add_ref, o_ref, acc_ref = refs
        k = pl.program_id(2)
        part = _dot(a_ref[...].astype(BF16), b_ref[...].astype(BF16), dims)

        @pl.when(k == 0)
        def _():
            acc_ref[...] = part

        @pl.when(k > 0)
        def _():
            acc_ref[...] += part

        @pl.when(k == nk - 1)
        def _():
            r = acc_ref[...]
            if add_ref is not None:
                r = add_ref[...].astype(F32) + r
            o_ref[...] = r.astype(out_dtype)

    a_spec = pl.BlockSpec((tk, tm), lambda i, j, k: (k, i)) if mode == "tn" else pl.BlockSpec((tm, tk), lambda i, j, k: (i, k))
    b_spec = pl.BlockSpec((tn, tk), lambda i, j, k: (j, k)) if mode == "nt" else pl.BlockSpec((tk, tn), lambda i, j, k: (k, j))
    o_spec = pl.BlockSpec((tm, tn), lambda i, j, k: (i, j))
    in_specs = [a_spec, b_spec] + ([o_spec] if add is not None else [])
    args = (a, b) + ((add,) if add is not None else ())
    return pl.pallas_call(
        body, name=name, grid=(m // tm, n // tn, nk),
        in_specs=in_specs, out_specs=o_spec,
        out_shape=jax.ShapeDtypeStruct((m, n), out_dtype),
        scratch_shapes=[pltpu.VMEM((tm, tn), F32)],
        compiler_params=_cparams(("parallel", "parallel", "arbitrary")),
    )(*args)


def _row_spec(tr, width):
    return pl.BlockSpec((tr, width), lambda i: (i, 0))


def _full_spec(shape):
    return pl.BlockSpec(shape, lambda i: tuple(0 for _ in shape))


def _rmsnorm_fwd(x, g, *, name, tr=256):
    s, d = x.shape

    def body(x_ref, g_ref, o_ref):
        xv = x_ref[...]
        r = lax.rsqrt(jnp.mean(xv * xv, axis=-1, keepdims=True) + EPS)
        o_ref[...] = ((xv * r) * g_ref[...]).astype(BF16)

    return pl.pallas_call(
        body, name=name, grid=(s // tr,),
        in_specs=[_row_spec(tr, d), _full_spec((1, d))], out_specs=_row_spec(tr, d),
        out_shape=jax.ShapeDtypeStruct((s, d), BF16), compiler_params=_cparams(("parallel",)),
    )(x, g)


def _rmsnorm_bwd(dres, dh, x, g, *, name, tr=256):
    s, d = x.shape

    def body(dres_ref, dh_ref, x_ref, g_ref, dx_ref, dg_ref):
        i = pl.program_id(0)
        xv = x_ref[...]
        r = lax.rsqrt(jnp.mean(xv * xv, axis=-1, keepdims=True) + EPS)
        xhat = xv * r
        dhv = dh_ref[...].astype(F32)
        dxhat = dhv * g_ref[...]
        dx = r * (dxhat - xhat * jnp.mean(dxhat * xhat, axis=-1, keepdims=True))
        dx_ref[...] = dres_ref[...] + dx
        dgp = jnp.sum(dhv * xhat, axis=0, keepdims=True)

        @pl.when(i == 0)
        def _():
            dg_ref[...] = dgp

        @pl.when(i > 0)
        def _():
            dg_ref[...] += dgp

    return pl.pallas_call(
        body, name=name, grid=(s // tr,),
        in_specs=[_row_spec(tr, d), _row_spec(tr, d), _row_spec(tr, d), _full_spec((1, d))],
        out_specs=[_row_spec(tr, d), _full_spec((1, d))],
        out_shape=[jax.ShapeDtypeStruct((s, d), F32), jax.ShapeDtypeStruct((1, d), F32)],
        compiler_params=_cparams(("arbitrary",)),
    )(dres, dh, x, g)


def _final_loss_bwd(x3, target, g, *, name, tr=256):
    s, d = x3.shape

    def body(x_ref, t_ref, g_ref, loss_ref, dx_ref, dg_ref):
        i = pl.program_id(0)
        xv = x_ref[...]
        r = lax.rsqrt(jnp.mean(xv * xv, axis=-1, keepdims=True) + EPS)
        xhat = xv * r
        diff = xhat * g_ref[...] - t_ref[...]
        lp = jnp.zeros((1, 128), F32) + (0.5 / d) * jnp.sum(diff * diff)
        dy = diff * (1.0 / d)
        dxhat = dy * g_ref[...]
        dx_ref[...] = r * (dxhat - xhat * jnp.mean(dxhat * xhat, axis=-1, keepdims=True))
        dgp = jnp.sum(dy * xhat, axis=0, keepdims=True)

        @pl.when(i == 0)
        def _():
            dg_ref[...] = dgp
            loss_ref[...] = lp

        @pl.when(i > 0)
        def _():
            dg_ref[...] += dgp
            loss_ref[...] += lp

    return pl.pallas_call(
        body, name=name, grid=(s // tr,),
        in_specs=[_row_spec(tr, d), _row_spec(tr, d), _full_spec((1, d))],
        out_specs=[_full_spec((1, 128)), _row_spec(tr, d), _full_spec((1, d))],
        out_shape=[jax.ShapeDtypeStruct((1, 128), F32), jax.ShapeDtypeStruct((s, d), F32),
                   jax.ShapeDtypeStruct((1, d), F32)],
        compiler_params=_cparams(("arbitrary",)),
    )(x3, target, g)


def _merge_fwd(ya, yb, ga, gb, *, name, tr=256):
    s, d = ya.shape

    def body(ya_ref, yb_ref, ga_ref, gb_ref, o_ref):
        o_ref[...] = (_sigmoid(ga_ref[...]) * ya_ref[...] + _sigmoid(gb_ref[...]) * yb_ref[...]).astype(BF16)

    return pl.pallas_call(
        body, name=name, grid=(s // tr,),
        in_specs=[_row_spec(tr, d)] * 4, out_specs=_row_spec(tr, d),
        out_shape=jax.ShapeDtypeStruct((s, d), BF16), compiler_params=_cparams(("parallel",)),
    )(ya, yb, ga, gb)


def _merge_bwd(dm, ya, yb, ga, gb, *, name, tr=256):
    s, d = ya.shape

    def body(dm_ref, ya_ref, yb_ref, ga_ref, gb_ref, dya_ref, dyb_ref, dga_ref, dgb_ref):
        dmv = dm_ref[...]
        sa = _sigmoid(ga_ref[...])
        sb = _sigmoid(gb_ref[...])
        dya_ref[...] = (dmv * sa).astype(BF16)
        dyb_ref[...] = (dmv * sb).astype(BF16)
        dga_ref[...] = (dmv * ya_ref[...] * (sa * (1.0 - sa))).astype(BF16)
        dgb_ref[...] = (dmv * yb_ref[...] * (sb * (1.0 - sb))).astype(BF16)

    o = jax.ShapeDtypeStruct((s, d), BF16)
    return pl.pallas_call(
        body, name=name, grid=(s // tr,),
        in_specs=[_row_spec(tr, d)] * 5, out_specs=[_row_spec(tr, d)] * 4,
        out_shape=[o, o, o, o], compiler_params=_cparams(("parallel",)),
    )(dm, ya, yb, ga, gb)


def _ple_fwd(x2, ple, gp, *, name, tr=256):
    s, d = x2.shape

    def body(x_ref, ple_ref, gp_ref, o_ref):
        o_ref[...] = x_ref[...] + ple_ref[...] * _sigmoid(gp_ref[...])

    return pl.pallas_call(
        body, name=name, grid=(s // tr,),
        in_specs=[_row_spec(tr, d)] * 3, out_specs=_row_spec(tr, d),
        out_shape=jax.ShapeDtypeStruct((s, d), F32), compiler_params=_cparams(("parallel",)),
    )(x2, ple, gp)


def _ple_bwd(dx3, ple, gp, *, name, tr=256):
    s, d = dx3.shape

    def body(dx_ref, ple_ref, gp_ref, dple_ref, dgp_ref):
        sg = _sigmoid(gp_ref[...])
        dxv = dx_ref[...]
        dple_ref[...] = (dxv * sg).astype(BF16)
        dgp_ref[...] = (dxv * ple_ref[...] * (sg * (1.0 - sg))).astype(BF16)

    o = jax.ShapeDtypeStruct((s, d), BF16)
    return pl.pallas_call(
        body, name=name, grid=(s // tr,),
        in_specs=[_row_spec(tr, d)] * 3, out_specs=[_row_spec(tr, d)] * 2,
        out_shape=[o, o], compiler_params=_cparams(("parallel",)),
    )(dx3, ple, gp)


def _masked_ws(ws_ref, g):
    row = lax.broadcasted_iota(I32, (GBLOCK, GBLOCK), 0)
    col = lax.broadcasted_iota(I32, (GBLOCK, GBLOCK), 1)
    keep = (col // CHUNK) <= (row // CHUNK)
    return jnp.where(keep, ws_ref[g], 0.0), keep


def _layernorm_parts(zv):
    mu = jnp.mean(zv, axis=-1, keepdims=True)
    xc = zv - mu
    rs = lax.rsqrt(jnp.mean(xc * xc, axis=-1, keepdims=True) + EPS)
    return xc * rs, rs


def _gmlp_fwd(zu_pre, zv_pre, ln_g, ln_b, w_s, bs_t, *, name):
    s, w = zu_pre.shape

    def body(zu_ref, zv_ref, lng_ref, lnb_ref, ws_ref, bs_ref, a_ref):
        zu = _gelu(zu_ref[...])
        zv = _gelu(zv_ref[...])
        xhat, _ = _layernorm_parts(zv)
        vln = (xhat * lng_ref[...] + lnb_ref[...]).astype(BF16)
        for g in range(GROUPS):
            wm, _ = _masked_ws(ws_ref, g)
            mixed = _dot(wm.astype(BF16), vln[:, g * GDIM:(g + 1) * GDIM], NN) + bs_ref[:, g:g + 1]
            a_ref[:, g * GDIM:(g + 1) * GDIM] = (zu[:, g * GDIM:(g + 1) * GDIM] * mixed).astype(BF16)

    return pl.pallas_call(
        body, name=name, grid=(s // GBLOCK,),
        in_specs=[_row_spec(GBLOCK, w), _row_spec(GBLOCK, w), _full_spec((1, w)), _full_spec((1, w)),
                  _full_spec((GROUPS, GBLOCK, GBLOCK)), _full_spec((GBLOCK, 128))],
        out_specs=_row_spec(GBLOCK, w),
        out_shape=jax.ShapeDtypeStruct((s, w), BF16), compiler_params=_cparams(("parallel",)),
    )(zu_pre, zv_pre, ln_g, ln_b, w_s, bs_t)


def _gmlp_bwd(da, zu_pre, zv_pre, ln_g, ln_b, w_s, bs_t, *, name):
    s, w = zu_pre.shape

    def body(da_ref, zu_ref, zv_ref, lng_ref, lnb_ref, ws_ref, bs_ref,
             dzu_ref, dzv_ref, dws_ref, dbs_ref, dlng_ref, dlnb_ref, dvln_ref):
        i = pl.program_id(0)
        zu, dzu_g = _gelu_and_grad(zu_ref[...])
        zv, dzv_g = _gelu_and_grad(zv_ref[...])
        xhat, rs = _layernorm_parts(zv)
        vln = (xhat * lng_ref[...] + lnb_ref[...]).astype(BF16)
        dav = da_ref[...].astype(F32)
        lane = lax.broadcasted_iota(I32, (GBLOCK, 128), 1)
        dbs = jnp.zeros((GBLOCK, 128), F32)

        @pl.when(i == 0)
        def _():
            dws_ref[...] = jnp.zeros_like(dws_ref)

        for g in range(GROUPS):
            sl = slice(g * GDIM, (g + 1) * GDIM)
            wm, keep = _masked_ws(ws_ref, g)
            wmb = wm.astype(BF16)
            vg = vln[:, sl]
            mixed = _dot(wmb, vg, NN) + bs_ref[:, g:g + 1]
            dag = dav[:, sl]
            dzu_ref[:, sl] = (dag * mixed * dzu_g[:, sl]).astype(BF16)
            dmix = dag * zu[:, sl]
            dmb = dmix.astype(BF16)
            dws_ref[g] += jnp.where(keep, _dot(dmb, vg, NT), 0.0)
            dbs = jnp.where(lane == g, jnp.sum(dmix, axis=1, keepdims=True), dbs)
            dvln_ref[:, sl] = _dot(wmb, dmb, TN)
        dvln = dvln_ref[...]
        dxhat = dvln * lng_ref[...]
        dzv = rs * (dxhat - jnp.mean(dxhat, axis=-1, keepdims=True)
                    - xhat * jnp.mean(dxhat * xhat, axis=-1, keepdims=True))
        dzv_ref[...] = (dzv * dzv_g).astype(BF16)
        dlng = jnp.sum(dvln * xhat, axis=0, keepdims=True)
        dlnb = jnp.sum(dvln, axis=0, keepdims=True)

        @pl.when(i == 0)
        def _():
            dbs_ref[...] = dbs
            dlng_ref[...] = dlng
            dlnb_ref[...] = dlnb

        @pl.when(i > 0)
        def _():
            dbs_ref[...] += dbs
            dlng_ref[...] += dlng
            dlnb_ref[...] += dlnb

    return pl.pallas_call(
        body, name=name, grid=(s // GBLOCK,),
        in_specs=[_row_spec(GBLOCK, w)] * 3 + [_full_spec((1, w)), _full_spec((1, w)),
                                               _full_spec((GROUPS, GBLOCK, GBLOCK)), _full_spec((GBLOCK, 128))],
        out_specs=[_row_spec(GBLOCK, w), _row_spec(GBLOCK, w), _full_spec((GROUPS, GBLOCK, GBLOCK)),
                   _full_spec((GBLOCK, 128)), _full_spec((1, w)), _full_spec((1, w))],
        out_shape=[jax.ShapeDtypeStruct((s, w), BF16), jax.ShapeDtypeStruct((s, w), BF16),
                   jax.ShapeDtypeStruct((GROUPS, GBLOCK, GBLOCK), F32), jax.ShapeDtypeStruct((GBLOCK, 128), F32),
                   jax.ShapeDtypeStruct((1, w), F32), jax.ShapeDtypeStruct((1, w), F32)],
        scratch_shapes=[pltpu.VMEM((GBLOCK, w), F32)],
        compiler_params=_cparams(("arbitrary",)),
    )(da, zu_pre, zv_pre, ln_g, ln_b, w_s, bs_t)


def _shift_down(u, k):
    row = lax.broadcasted_iota(I32, u.shape, 0)
    return jnp.where(row >= k, pltpu.roll(u, k, 0), 0.0)


def _shift_up(u, k):
    s = u.shape[0]
    row = lax.broadcasted_iota(I32, u.shape, 0)
    return jnp.where(row < s - k, pltpu.roll(u, s - k, 0), 0.0)


def _conv(u, w_ref, b_ref):
    return b_ref[...] + w_ref[0:1, :] * _shift_down(u, 2) + w_ref[1:2, :] * _shift_down(u, 1) + w_ref[2:3, :] * u


def _convglu_fwd(up_a, up_g, cw_a, cw_g, cb_a, cb_g, *, name, tc=256):
    s, f = up_a.shape

    def body(ua_ref, ug_ref, wa_ref, wg_ref, ba_ref, bg_ref, o_ref):
        ca = _conv(ua_ref[...], wa_ref, ba_ref)
        cg = _conv(ug_ref[...], wg_ref, bg_ref)
        o_ref[...] = (_gelu(ca) * cg).astype(BF16)

    col = pl.BlockSpec((s, tc), lambda j: (0, j))
    w3 = pl.BlockSpec((3, tc), lambda j: (0, j))
    b1 = pl.BlockSpec((1, tc), lambda j: (0, j))
    return pl.pallas_call(
        body, name=name, grid=(f // tc,),
        in_specs=[col, col, w3, w3, b1, b1], out_specs=col,
        out_shape=jax.ShapeDtypeStruct((s, f), BF16), compiler_params=_cparams(("parallel",)),
    )(up_a, up_g, cw_a, cw_g, cb_a, cb_g)


def _convglu_bwd(dact, up_a, up_g, cw_a, cw_g, cb_a, cb_g, *, name, tc=256):
    s, f = up_a.shape

    def half(dc, u, w_ref, du_ref, dw_ref, db_ref):
        db_ref[...] = jnp.sum(dc, axis=0, keepdims=True)
        dw_ref[0:1, :] = jnp.sum(dc * _shift_down(u, 2), axis=0, keepdims=True)
        dw_ref[1:2, :] = jnp.sum(dc * _shift_down(u, 1), axis=0, keepdims=True)
        dw_ref[2:3, :] = jnp.sum(dc * u, axis=0, keepdims=True)
        du = w_ref[2:3, :] * dc + w_ref[1:2, :] * _shift_up(dc, 1) + w_ref[0:1, :] * _shift_up(dc, 2)
        du_ref[...] = du.astype(BF16)

    def body(d_ref, ua_ref, ug_ref, wa_ref, wg_ref, ba_ref, bg_ref,
             dua_ref, dug_ref, dwa_ref, dwg_ref, dba_ref, dbg_ref):
        ua = ua_ref[...]
        ug = ug_ref[...]
        ca = _conv(ua, wa_ref, ba_ref)
        cg = _conv(ug, wg_ref, bg_ref)
        ga, dga = _gelu_and_grad(ca)
        dv = d_ref[...].astype(F32)
        half(dv * cg * dga, ua, wa_ref, dua_ref, dwa_ref, dba_ref)
        half(dv * ga, ug, wg_ref, dug_ref, dwg_ref, dbg_ref)

    col = pl.BlockSpec((s, tc), lambda j: (0, j))
    w3 = pl.BlockSpec((3, tc), lambda j: (0, j))
    b1 = pl.BlockSpec((1, tc), lambda j: (0, j))
    return pl.pallas_call(
        body, name=name, grid=(f // tc,),
        in_specs=[col, col, col, w3, w3, b1, b1], out_specs=[col, col, w3, w3, b1, b1],
        out_shape=[jax.ShapeDtypeStruct((s, f), BF16), jax.ShapeDtypeStruct((s, f), BF16),
                   jax.ShapeDtypeStruct((3, f), F32), jax.ShapeDtypeStruct((3, f), F32),
                   jax.ShapeDtypeStruct((1, f), F32), jax.ShapeDtypeStruct((1, f), F32)],
        compiler_params=_cparams(("parallel",)),
    )(dact, up_a, up_g, cw_a, cw_g, cb_a, cb_g)


def _tri_dot(tri, x):
    b0 = x.astype(BF16)
    r1 = x - b0.astype(F32)
    b1 = r1.astype(BF16)
    b2 = (r1 - b1.astype(F32)).astype(BF16)
    return _dot(tri, b0, NN) + _dot(tri, b1, NN) + _dot(tri, b2, NN)


def _log_sigmoid(x):
    return jnp.minimum(x, 0.0) - jnp.log(1.0 + jnp.exp(-jnp.abs(x)))


def _expand_heads(col16, rows):
    head_of_lane = lax.broadcasted_iota(I32, (rows, HEADS * HEAD_DIM), 1) // HEAD_DIM
    out = jnp.zeros((rows, HEADS * HEAD_DIM), F32)
    for h in range(HEADS):
        out = jnp.where(head_of_lane == h, col16[:, h:h + 1], out)
    return out


def _forget_cumsum(f_logit, b_f, *, name):
    s = f_logit.shape[0]
    nb = s // 128

    def body(f_ref, b_ref, cqe_ref):
        row = lax.broadcasted_iota(I32, (128, 128), 0)
        col = lax.broadcasted_iota(I32, (128, 128), 1)
        tri = (col <= row).astype(BF16)

        def step(n, carry):
            r0 = pl.multiple_of(n * 128, 128)
            lf = _log_sigmoid(f_ref[pl.ds(r0, 128), :] + b_ref[...])
            cum = _tri_dot(tri, lf) + carry
            cqe_ref[pl.ds(r0, 128), :] = _expand_heads(cum, 128)
            return cum[127:128, :]

        lax.fori_loop(0, nb, step, jnp.zeros((1, 128), F32))

    return pl.pallas_call(
        body, name=name, grid=(1,),
        in_specs=[_full_spec((s, 128)), _full_spec((1, 128))],
        out_specs=_full_spec((s, HEADS * HEAD_DIM)),
        out_shape=jax.ShapeDtypeStruct((s, HEADS * HEAD_DIM), F32),
        compiler_params=_cparams(("arbitrary",)),
    )(f_logit, b_f)


def _forget_bwd(dcq16, dck16, f_logit, b_f, *, name):
    s = f_logit.shape[0]
    nb = s // 128

    def body(a_ref, k_ref, f_ref, b_ref, df_ref, db_ref):
        row = lax.broadcasted_iota(I32, (128, 128), 0)
        col = lax.broadcasted_iota(I32, (128, 128), 1)
        tri_rev = (col >= row).astype(BF16)

        def step(m, carry):
            suffix, dbsum = carry
            n = nb - 1 - m
            r0 = pl.multiple_of(n * 128, 128)
            dcum = a_ref[pl.ds(r0, 128), :] + k_ref[pl.ds(r0, 128), :]
            dlf = _tri_dot(tri_rev, dcum) + suffix
            df = dlf * _sigmoid(-(f_ref[pl.ds(r0, 128), :] + b_ref[...]))
            df_ref[pl.ds(r0, 128), :] = df.astype(BF16)
            return dlf[0:1, :], dbsum + jnp.sum(df, axis=0, keepdims=True)

        _, dbsum = lax.fori_loop(0, nb, step, (jnp.zeros((1, 128), F32), jnp.zeros((1, 128), F32)))
        db_ref[...] = dbsum

    return pl.pallas_call(
        body, name=name, grid=(1,),
        in_specs=[_full_spec((s, 128))] * 3 + [_full_spec((1, 128))],
        out_specs=[_full_spec((s, 128)), _full_spec((1, 128))],
        out_shape=[jax.ShapeDtypeStruct((s, 128), BF16), jax.ShapeDtypeStruct((1, 128), F32)],
        compiler_params=_cparams(("arbitrary",)),
    )(dcq16, dck16, f_logit, b_f)


ATT_T = 256


def _head_lanes(rows):
    return lax.broadcasted_iota(I32, (rows, 128), 1) < HEAD_DIM


def _bf16_pieces(c):
    p0 = c.astype(BF16).astype(F32)
    r = c - p0
    p1 = r.astype(BF16).astype(F32)
    p2 = (r - p1).astype(BF16).astype(F32)
    return p0, p1, p2


def _col_reduce(x, op):
    rows = x.shape[0]
    while rows > 8:
        rows //= 2
        x = op(x[:rows], x[rows:])
    return jnp.max(x, axis=0, keepdims=True) if op is jnp.maximum else jnp.sum(x, axis=0, keepdims=True)


def _attn_prep(qkv, cqe, *, name):
    s = qkv.shape[0]
    npair = HEADS // 2

    def body(q_ref, k_ref, v_ref, c_ref, qa_ref, ka_ref, vt_ref):
        rows = 128
        lane = lax.broadcasted_iota(I32, (rows, 128), 1)

        def chunk(n, _):
            r0 = pl.multiple_of(n * rows, rows)
            sl = pl.ds(r0, rows)
            qv = q_ref[sl, :].astype(F32) * ATT_SCALE
            kv = k_ref[sl, :].astype(F32)
            for e in range(2):
                mine = (lane < HEAD_DIM) if e == 0 else (lane >= HEAD_DIM)
                base = HEAD_DIM * (1 - e)
                p0, p1, p2 = _bf16_pieces(c_ref[sl, HEAD_DIM * e:HEAD_DIM * e + 1])
                ones_hi = jnp.where((lane >= base + 3) & (lane < base + 6), 1.0, 0.0)
                ones_lo = jnp.where((lane >= base) & (lane < base + 3), 1.0, 0.0)
                qa = jnp.where(mine, qv, jnp.where(lane == base, p0, jnp.where(lane == base + 1, p1,
                               jnp.where(lane == base + 2, p2, ones_hi))))
                ka = jnp.where(mine, kv, jnp.where(lane == base + 3, -p0, jnp.where(lane == base + 4, -p1,
                               jnp.where(lane == base + 5, -p2, ones_lo))))
                qa_ref[e, sl, :] = qa.astype(BF16)
                ka_ref[e, sl, :] = ka.astype(BF16)
            vt_ref[0, :, sl] = v_ref[sl, :].astype(F32).T.astype(BF16)
            return 0

        lax.fori_loop(0, s // rows, chunk, 0)

    pair = pl.BlockSpec((2, s, 128), lambda hp: (hp, 0, 0))
    return pl.pallas_call(
        body, name=name, grid=(npair,),
        in_specs=[pl.BlockSpec((s, 128), lambda hp: (0, hp)), pl.BlockSpec((s, 128), lambda hp: (0, npair + hp)),
                  pl.BlockSpec((s, 128), lambda hp: (0, 2 * npair + hp)), pl.BlockSpec((s, 128), lambda hp: (0, hp))],
        out_specs=[pair, pair, pl.BlockSpec((1, 128, s), lambda hp: (hp, 0, 0))],
        out_shape=[jax.ShapeDtypeStruct((HEADS, s, 128), BF16), jax.ShapeDtypeStruct((HEADS, s, 128), BF16),
                   jax.ShapeDtypeStruct((npair, 128, s), BF16)],
        compiler_params=_cparams(("parallel",)),
    )(qkv, qkv, qkv, cqe)


def _attn_fwd(qa, ka, vt, *, name):
    s = qa.shape[1]
    t = ATT_T
    nq = s // t

    def body(qa_ref, ka_ref, vt_ref, o_ref, lse_ref):
        i = pl.program_id(1)
        krow = lax.broadcasted_iota(I32, (t, t), 0)
        qcol = lax.broadcasted_iota(I32, (t, t), 1)
        sub = lax.broadcasted_iota(I32, (128, t), 0)
        row8 = lax.broadcasted_iota(I32, (8, t), 0)
        qbs = (qa_ref[0], qa_ref[1])
        tk = 2 * t

        def step(j, carry, diag):
            c0 = pl.multiple_of(j * tk, tk)
            vtb = vt_ref[0, :, pl.ds(c0, tk)]
            sts = [_dot(ka_ref[e, pl.ds(c0, tk), :], qbs[e], NT) for e in range(2)]
            if diag:
                keep = (lax.broadcasted_iota(I32, (tk, t), 0) - lax.broadcasted_iota(I32, (tk, t), 1)) <= t * (i % 2)
                sts = [jnp.where(keep, st, NEG) for st in sts]
            pts, stats = [], []
            for e in range(2):
                m, l, _ = carry[e]
                m_new = jnp.maximum(m, _col_reduce(sts[e], jnp.maximum))
                alpha = jnp.exp(m - m_new)
                pt = jnp.exp(sts[e] - m_new)
                stats.append((m_new, alpha, alpha * l + _col_reduce(pt, jnp.add)))
                pts.append(pt.astype(BF16))
            pvs = [_dot(vtb, pts[e], NN) for e in range(2)]
            return tuple((stats[e][0], stats[e][2], stats[e][1] * carry[e][2] + pvs[e]) for e in range(2))

        init = (jnp.full((1, t), NEG, F32), jnp.zeros((1, t), F32), jnp.zeros((128, t), F32))
        carry = lax.fori_loop(0, i // 2, functools.partial(step, diag=False), (init, init))
        (m0, l0, acc0), (m1, l1, acc1) = step(i // 2, carry, True)
        o_pair = jnp.where(sub < HEAD_DIM, acc0 / l0, acc1 / l1)
        o_ref[...] = o_pair.T.astype(BF16)
        lse_ref[0] = jnp.where(row8 == 0, m0 + jnp.log(l0), jnp.where(row8 == 1, m1 + jnp.log(l1), 0.0))

    return pl.pallas_call(
        body, name=name, grid=(HEADS // 2, nq),
        in_specs=[pl.BlockSpec((2, t, 128), lambda hp, i: (hp, i, 0)), pl.BlockSpec((2, s, 128), lambda hp, i: (hp, 0, 0)),
                  pl.BlockSpec((1, 128, s), lambda hp, i: (hp, 0, 0))],
        out_specs=[pl.BlockSpec((t, 128), lambda hp, i: (i, hp)), pl.BlockSpec((1, 8, t), lambda hp, i: (hp, 0, i))],
        out_shape=[jax.ShapeDtypeStruct((s, HEADS * HEAD_DIM), BF16), jax.ShapeDtypeStruct((HEADS // 2, 8, s), F32)],
        compiler_params=_cparams(("parallel", "arbitrary")),
    )(qa, ka, vt)


def _attn_delta(do, o, *, name):
    s = do.shape[0]

    def body(do_ref, o_ref, d_ref):
        prod = do_ref[...].astype(F32) * o_ref[...].astype(F32)
        row = lax.broadcasted_iota(I32, (8, 128), 0)
        lane = lax.broadcasted_iota(I32, (8, 128), 1)
        sel = ((row == 0) & (lane < HEAD_DIM) | (row == 1) & (lane >= HEAD_DIM)).astype(BF16)
        p0, p1, p2 = _bf16_pieces(prod)
        d_ref[0] = (_dot(sel, p0.astype(BF16), NT) + _dot(sel, p1.astype(BF16), NT)) + _dot(sel, p2.astype(BF16), NT)

    pair = pl.BlockSpec((s, 128), lambda hp: (0, hp))
    return pl.pallas_call(
        body, name=name, grid=(HEADS // 2,), in_specs=[pair, pair],
        out_specs=pl.BlockSpec((1, 8, s), lambda hp: (hp, 0, 0)),
        out_shape=jax.ShapeDtypeStruct((HEADS // 2, 8, s), F32), compiler_params=_cparams(("parallel",)),
    )(do, o)


def _attn_bwd(qa, ka, qkv, do, lse3, delta3, *, name):
    s = qa.shape[1]
    t = ATT_T
    nb = s // t
    npair = HEADS // 2

    def body(qa_ref, ka_ref, v_ref, do_ref, lse_ref, delta_ref,
             dq_ref, dk_ref, dv_ref, aux_ref, dcq_ref, dqt):
        first = _head_lanes(t)
        dqt[...] = jnp.zeros_like(dqt)
        krow = lax.broadcasted_iota(I32, (t, t), 0)
        qcol = lax.broadcasted_iota(I32, (t, t), 1)

        def key_block(j, _):
            c0 = pl.multiple_of(j * t, t)
            vb = v_ref[pl.ds(c0, t), :]
            kbs = (ka_ref[0, pl.ds(c0, t), :], ka_ref[1, pl.ds(c0, t), :])
            kbts = tuple(kb.astype(F32).T.astype(BF16) for kb in kbs)
            vhs = (jnp.where(first, vb, jnp.zeros_like(vb)), jnp.where(first, jnp.zeros_like(vb), vb))

            def query_block(i, carry, diag):
                r0 = pl.multiple_of(i * t, t)
                dob = do_ref[pl.ds(r0, t), :]
                qbs = [qa_ref[e, pl.ds(r0, t), :] for e in range(2)]
                sts = [_dot(kbs[e], qbs[e], NT) for e in range(2)]
                dpts = [_dot(vhs[e], dob, NT) for e in range(2)]
                ptbs, dsbs = [], []
                for e in range(2):
                    st = jnp.where(krow <= qcol, sts[e], NEG) if diag else sts[e]
                    pt = jnp.exp(st - lse_ref[0, e:e + 1, pl.ds(r0, t)])
                    dsbs.append((pt * (dpts[e] - delta_ref[0, e:e + 1, pl.ds(r0, t)])).astype(BF16))
                    ptbs.append(pt.astype(BF16))
                out = []
                for e in range(2):
                    dk_a, dv_a = carry[e]
                    dv_a = dv_a + _dot(ptbs[e], dob, NN)
                    dk_a = dk_a + _dot(dsbs[e], qbs[e], NN)
                    dqt[e, :, pl.ds(r0, t)] += _dot(kbts[e], dsbs[e], NN)
                    out.append((dk_a, dv_a))
                return tuple(out)

            zero = jnp.zeros((t, 128), F32)
            carry = query_block(j, ((zero, zero), (zero, zero)), True)
            (dk0, dv0), (dk1, dv1) = lax.fori_loop(j + 1, nb, functools.partial(query_block, diag=False), carry)
            dk_ref[pl.ds(c0, t), :] = jnp.where(first, dk0, dk1).astype(BF16)
            dv_ref[pl.ds(c0, t), :] = jnp.where(first, dv0, dv1).astype(BF16)
            aux_ref[pl.ds(c0, t), :] = jnp.where(first, dk1, dk0)
            return 0

        lax.fori_loop(0, nb, key_block, 0)
        sub = lax.broadcasted_iota(I32, (128, s), 0)
        row8 = lax.broadcasted_iota(I32, (8, s), 0)
        dq_ref[...] = (jnp.where(sub < HEAD_DIM, dqt[0], dqt[1]) * ATT_SCALE).T.astype(BF16)
        dcq_ref[0] = jnp.where(row8 == 0, dqt[0, HEAD_DIM:HEAD_DIM + 1, :], jnp.where(row8 == 1, dqt[1, 0:1, :], 0.0))

    def pair_cols(off):
        return pl.BlockSpec((s, 128), lambda hp: (0, off + hp))

    heads = pl.BlockSpec((2, s, 128), lambda hp: (hp, 0, 0))
    rows = pl.BlockSpec((1, 8, s), lambda hp: (hp, 0, 0))
    wide = jax.ShapeDtypeStruct((s, HEADS * HEAD_DIM), BF16)
    return pl.pallas_call(
        body, name=name, grid=(npair,),
        in_specs=[heads, heads, pair_cols(2 * npair), pair_cols(0), rows, rows],
        out_specs=[pair_cols(0), pair_cols(0), pair_cols(0), pair_cols(0), rows],
        out_shape=[wide, wide, wide, jax.ShapeDtypeStruct((s, HEADS * HEAD_DIM), F32),
                   jax.ShapeDtypeStruct((npair, 8, s), F32)],
        scratch_shapes=[pltpu.VMEM((2, 128, s), F32)],
        compiler_params=_cparams(("parallel",)),
    )(qa, ka, qkv, do, lse3, delta3)


def _adam_math(w, g, m, v):
    m = ADAM_B1 * m + (1.0 - ADAM_B1) * g
    v = ADAM_B2 * v + (1.0 - ADAM_B2) * (g * g)
    m_hat = m / (1.0 - ADAM_B1 ** ADAM_STEP)
    v_hat = v / (1.0 - ADAM_B2 ** ADAM_STEP)
    delta = -ADAM_LR * (m_hat / (jnp.sqrt(v_hat) + ADAM_EPS) + ADAM_WD * w)
    return delta, m, v


def _sum_pairs(keep, recv, pos, *, name):
    _, r, c = recv.shape
    tr = _row_tile(r, 512)

    def body(pos_ref, a_ref, b_ref, o32_ref, o16_ref):
        tot = a_ref[...].astype(F32) + b_ref[...].astype(F32)
        o32_ref[...] = tot
        o16_ref[...] = tot.astype(BF16)

    out = pl.BlockSpec((1, tr, c), lambda q, i, pos: (q, i, 0))
    grid_spec = pltpu.PrefetchScalarGridSpec(
        num_scalar_prefetch=1, grid=(4, r // tr),
        in_specs=[pl.BlockSpec((1, tr, c), lambda q, i, pos: (2 * q + pos[2], i, 0)), out],
        out_specs=[out, out])
    return pl.pallas_call(
        body, name=name, grid_spec=grid_spec,
        out_shape=[jax.ShapeDtypeStruct((4, r, c), F32), jax.ShapeDtypeStruct((4, r, c), BF16)],
        compiler_params=_cparams(("parallel", "parallel")),
    )(pos, keep, recv)


def _sum_chips(psum, recv, pos, *, name):
    _, r, c = recv.shape
    tr = _row_tile(r, 512)

    def body(pos_ref, p_ref, r_ref, g_ref):
        g_ref[...] = p_ref[0] + r_ref[0].astype(F32) + r_ref[1].astype(F32) + r_ref[2].astype(F32)

    grid_spec = pltpu.PrefetchScalarGridSpec(
        num_scalar_prefetch=1, grid=(r // tr,),
        in_specs=[pl.BlockSpec((1, tr, c), lambda i, pos: (2 * pos[0] + pos[1], i, 0)),
                  pl.BlockSpec((3, tr, c), lambda i, pos: (0, i, 0))],
        out_specs=pl.BlockSpec((tr, c), lambda i, pos: (i, 0)))
    return pl.pallas_call(
        body, name=name, grid_spec=grid_spec, out_shape=jax.ShapeDtypeStruct((r, c), F32),
        compiler_params=_cparams(("parallel",)),
    )(pos, psum, recv)


def _adam(g, w, m, v, *, name):
    r, c = w.shape
    tr = _row_tile(r, 256)

    def body(g_ref, w_ref, m_ref, v_ref, d_ref, mo_ref, vo_ref):
        delta, mn, vn = _adam_math(w_ref[...], g_ref[...], m_ref[...], v_ref[...])
        d_ref[...] = delta
        mo_ref[...] = mn
        vo_ref[...] = vn

    blk = _row_spec(tr, c)
    o = jax.ShapeDtypeStruct((r, c), F32)
    return pl.pallas_call(
        body, name=name, grid=(r // tr,), in_specs=[blk] * 4, out_specs=[blk] * 3, out_shape=[o, o, o],
        compiler_params=_cparams(("parallel",)),
    )(g, w, m, v)


def _adam_replicated(chip_sums, w, m, v, *, name):
    r = w.shape[0]

    def body(s_ref, w_ref, m_ref, v_ref, g_ref, d_ref, mo_ref, vo_ref):
        g = ((s_ref[0] + s_ref[1]) + s_ref[2]) + s_ref[3]
        delta, mn, vn = _adam_math(w_ref[...], g, m_ref[...], v_ref[...])
        g_ref[...] = g
        d_ref[...] = delta
        mo_ref[...] = mn
        vo_ref[...] = vn

    o = jax.ShapeDtypeStruct((r, 1024), F32)
    full = _full_spec((r, 1024))
    return pl.pallas_call(
        body, name=name, grid=(1,),
        in_specs=[_full_spec((4, r, 1024)), full, full, full], out_specs=[full] * 4, out_shape=[o] * 4,
        compiler_params=_cparams(("arbitrary",)),
    )(chip_sums, w, m, v)


def _pair_sum_small(mine, theirs, *, name):
    def body(a_ref, b_ref, o_ref):
        o_ref[...] = a_ref[...] + b_ref[...]

    full = _full_spec(mine.shape)
    return pl.pallas_call(
        body, name=name, grid=(1,), in_specs=[full, full], out_specs=full,
        out_shape=jax.ShapeDtypeStruct(mine.shape, F32), compiler_params=_cparams(("arbitrary",)),
    )(mine, theirs)


ANY = pl.BlockSpec(memory_space=pl.ANY)
OTHER_CHIPS = ((1, 0), (0, 1), (1, 1))


def _allgather(shards, *, name):
    n = len(shards)

    def body(*refs):
        x_refs, out_refs = refs[:n], refs[n:2 * n]
        send_sems, recv_sems, local_sems = refs[2 * n:]
        x, y, c = lax.axis_index("x"), lax.axis_index("y"), lax.axis_index("c")
        me, sibling = (x, y, c), (x, y, 1 - c)
        chips = [(x ^ fx, y ^ fy) for fx, fy in OTHER_CHIPS]

        def copy(t, k, block, to, from_input=False):
            px, py, pc = block
            slab = out_refs[t].at[4 * px + 2 * py + pc]
            return pltpu.make_async_remote_copy(
                src_ref=x_refs[t] if from_input else slab, dst_ref=slab,
                send_sem=send_sems.at[7 * t + k], recv_sem=recv_sems.at[7 * t + k], device_id=to, device_id_type=MESH)

        mine = [pltpu.make_async_copy(x_refs[t], out_refs[t].at[4 * x + 2 * y + c], local_sems.at[t]) for t in range(n)]
        for cp in mine:
            cp.start()
        first = []
        for t in range(n):
            first.append(copy(t, 0, me, sibling, from_input=True))
            first += [copy(t, 1 + j, me, (*chip, c), from_input=True) for j, chip in enumerate(chips)]
        for cp in first:
            cp.start()
        passed = []
        for j, chip in enumerate(chips):
            for t in range(n):
                copy(t, 1 + j, (*chip, c), me).wait_recv()
                fwd = copy(t, 4 + j, (*chip, c), sibling)
                fwd.start()
                passed.append(fwd)
        for t in range(n):
            copy(t, 0, sibling, me).wait_recv()
            for j, chip in enumerate(chips):
                copy(t, 4 + j, (*chip, 1 - c), me).wait_recv()
        for cp in first + passed:
            cp.wait_send()
        for cp in mine:
            cp.wait()

    return pl.pallas_call(
        body, name=name, out_shape=[jax.ShapeDtypeStruct((N_DEV,) + a.shape, a.dtype) for a in shards],
        in_specs=[ANY] * n, out_specs=[ANY] * n,
        scratch_shapes=[pltpu.SemaphoreType.DMA((7 * n,)), pltpu.SemaphoreType.DMA((7 * n,)),
                        pltpu.SemaphoreType.DMA((n,))],
    )(*shards)


def _exchange_sibling(slabs, small, *, name):
    n = len(slabs)

    def body(*refs):
        g_refs, s_ref = refs[:n], refs[n]
        rg_refs, rs_ref = refs[n + 1:2 * n + 1], refs[2 * n + 1]
        send_sems, recv_sems = refs[2 * n + 2:]
        x, y, c = lax.axis_index("x"), lax.axis_index("y"), lax.axis_index("c")
        sibling = (x, y, 1 - c)
        copies = []
        for t in range(n):
            for q in range(4):
                copies.append(pltpu.make_async_remote_copy(
                    src_ref=g_refs[t].at[2 * q + (1 - c)], dst_ref=rg_refs[t].at[q],
                    send_sem=send_sems.at[4 * t + q], recv_sem=recv_sems.at[4 * t + q],
                    device_id=sibling, device_id_type=MESH))
        copies.append(pltpu.make_async_remote_copy(
            src_ref=s_ref, dst_ref=rs_ref, send_sem=send_sems.at[4 * n], recv_sem=recv_sems.at[4 * n],
            device_id=sibling, device_id_type=MESH))
        for cp in copies:
            cp.start()
        for cp in copies:
            cp.wait()

    return pl.pallas_call(
        body, name=name,
        out_shape=[jax.ShapeDtypeStruct((4,) + a.shape[1:], a.dtype) for a in slabs]
        + [jax.ShapeDtypeStruct(small.shape, small.dtype)],
        in_specs=[ANY] * (n + 1), out_specs=[ANY] * (n + 1),
        scratch_shapes=[pltpu.SemaphoreType.DMA((4 * n + 1,)), pltpu.SemaphoreType.DMA((4 * n + 1,))],
    )(*slabs, small)


def _exchange_chips(psums, small_sum, *, name):
    n = len(psums)
    rs = small_sum.shape[0]

    def body(*refs):
        p_refs, s_ref = refs[:n], refs[n]
        rp_refs, tab_ref = refs[n + 1:2 * n + 1], refs[2 * n + 1]
        send_sems, recv_sems, local_sem = refs[2 * n + 2:]
        x, y, c = lax.axis_index("x"), lax.axis_index("y"), lax.axis_index("c")
        mine = pltpu.make_async_copy(s_ref, tab_ref.at[2 * x + y], local_sem)
        mine.start()

        def table_copy(k, px, py, slot):
            return pltpu.make_async_remote_copy(
                src_ref=s_ref, dst_ref=tab_ref.at[slot], send_sem=send_sems.at[3 * n + k],
                recv_sem=recv_sems.at[3 * n + k], device_id=(px, py, c), device_id_type=MESH)

        copies = []
        for k, (fx, fy) in enumerate(OTHER_CHIPS):
            px, py = x ^ fx, y ^ fy
            for t in range(n):
                copies.append(pltpu.make_async_remote_copy(
                    src_ref=p_refs[t].at[2 * px + py], dst_ref=rp_refs[t].at[k],
                    send_sem=send_sems.at[3 * t + k], recv_sem=recv_sems.at[3 * t + k],
                    device_id=(px, py, c), device_id_type=MESH))
            copies.append(table_copy(k, px, py, 2 * x + y))
        for cp in copies:
            cp.start()
        for k, (fx, fy) in enumerate(OTHER_CHIPS):
            px, py = x ^ fx, y ^ fy
            for t in range(n):
                copies[k * (n + 1) + t].wait()
            table_copy(k, px, py, 2 * px + py).wait()
        mine.wait()

    return pl.pallas_call(
        body, name=name,
        out_shape=[jax.ShapeDtypeStruct((3,) + a.shape[1:], a.dtype) for a in psums]
        + [jax.ShapeDtypeStruct((4, rs, 1024), F32)],
        in_specs=[ANY] * (n + 1), out_specs=[ANY] * (n + 1),
        scratch_shapes=[pltpu.SemaphoreType.DMA((3 * n + 3,)), pltpu.SemaphoreType.DMA((3 * n + 3,)),
                        pltpu.SemaphoreType.DMA],
    )(*psums, small_sum)


def _to_comm(name, kind, block):
    a = block[0]
    if kind == "cols":
        a = a.T
        if name == "w_in":
            a = jnp.pad(a, ((0, IN_SHARD_PAD - IN_SHARD), (0, 0)))
    return a if kind == "f32" else a.astype(BF16)


def _from_comm(name, kind, a):
    if kind == "cols":
        if name == "w_in":
            a = a[:IN_SHARD]
        a = a.T
    return a[None]


def _assemble_weights(g):
    wt_in = g["w_in"][:, :IN_SHARD].reshape(IN_COLS, D_MODEL)
    return dict(
        wt_main=jnp.concatenate([wt_in[:O_F], wt_in[O_G:]], axis=0),
        wt_f=jnp.pad(wt_in[O_F:O_G], ((0, 128 - HEADS), (0, 0))),
        w_a=g["w_branch_a"].reshape(D_MODEL, D_MODEL), w_b=g["w_branch_b"].reshape(D_MODEL, D_MODEL),
        w_out=g["w_out"].reshape(D_MODEL, D_MODEL), wt_up=g["w_up"].reshape(2 * D_FF, D_MODEL),
        conv_w=g["conv_w"].transpose(1, 0, 2).reshape(3, 2 * D_FF),
        w_down=g["w_down"].reshape(D_FF, D_MODEL), wt_ple=g["w_ple"].reshape(D_MODEL, PLE_DIM),
        w_pg=g["w_ple_gate"].reshape(D_MODEL, D_MODEL))


def _grad_slabs(gr):
    gt_in = jnp.concatenate([gr["wt_main"][:O_F], gr["wt_f"][:HEADS], gr["wt_main"][O_F:]], axis=0)
    gt_in = jnp.pad(gt_in.reshape(N_DEV, IN_SHARD, D_MODEL), ((0, 0), (0, IN_SHARD_PAD - IN_SHARD), (0, 0)))
    cw = gr["conv_w"]
    out = dict(
        w_in=gt_in, w_branch_a=gr["w_a"].reshape(N_DEV, -1, D_MODEL), w_branch_b=gr["w_b"].reshape(N_DEV, -1, D_MODEL),
        w_out=gr["w_out"].reshape(N_DEV, -1, D_MODEL), w_up=gr["wt_up"].reshape(N_DEV, -1, D_MODEL),
        conv_w=cw.reshape(3, N_DEV, -1).transpose(1, 0, 2), w_down=gr["w_down"].reshape(N_DEV, -1, D_MODEL),
        w_ple=gr["wt_ple"].reshape(N_DEV, -1, PLE_DIM), w_ple_gate=gr["w_pg"].reshape(N_DEV, -1, D_MODEL))
    return {k: v.astype(BF16) for k, v in out.items()}


def _rows(a, rows):
    flat = a.reshape(-1)
    return jnp.pad(flat, (0, rows * 1024 - flat.shape[0])).reshape(rows, 1024)


def _pack_small(parts):
    return jnp.concatenate([_rows(parts[n].astype(F32), r) for n, r in SMALL], axis=0)


def _small(packed, name, shape):
    off, r = SMALL_OFF[name]
    n = math.prod(shape)
    return packed[off:off + r].reshape(-1)[:n].reshape(shape)


def _local_step(x, p, target, w, sm):
    s = x.shape[0]
    mm = _matmul
    wt_main = w["wt_main"]
    wt_u, wt_v = wt_main[0:1024], wt_main[1024:2048]
    wt_qkv = wt_main[2048:5120]
    wt_gates = wt_main[5120:7168]
    wt_up_a, wt_up_g = w["wt_up"][:D_FF], w["wt_up"][D_FF:]
    cw_a, cw_g = w["conv_w"][:, :D_FF], w["conv_w"][:, D_FF:]
    cb_a, cb_g = sm["conv_b"][:, :D_FF], sm["conv_b"][:, D_FF:]
    bs_t = jnp.pad(sm["gmlp_b_s"].T, ((0, 0), (0, 128 - GROUPS)))
    b_f = jnp.pad(sm["b_f"], ((0, 0), (0, 128 - HEADS)))

    h = _rmsnorm_fwd(x, sm["norm_mix_g"], name="norm_mix")
    zu = mm(h, wt_u, mode="nt", out_dtype=F32, name="in_u", tn=1024, tk=1024)
    zv = mm(h, wt_v, mode="nt", out_dtype=F32, name="in_v", tn=1024, tk=1024)
    qkv = mm(h, wt_qkv, mode="nt", out_dtype=BF16, name="in_qkv", tn=1024, tk=1024)
    gates = mm(h, wt_gates, mode="nt", out_dtype=F32, name="in_gates", tn=1024, tk=1024)
    f_logit = mm(h, w["wt_f"], mode="nt", out_dtype=F32, name="in_f", tk=1024)
    ga, gb = gates[:, :1024], gates[:, 1024:]
    a = _gmlp_fwd(zu, zv, sm["gmlp_ln_g"], sm["gmlp_ln_b"], sm["gmlp_w_s"], bs_t, name="gmlp_fwd")
    cqe = _forget_cumsum(f_logit, b_f, name="forget_cumsum")
    qa, ka, vt = _attn_prep(qkv, cqe, name="attn_prep")
    b, lse3 = _attn_fwd(qa, ka, vt, name="attn_fwd")
    ya = mm(a, w["w_a"], mode="nn", out_dtype=F32, name="branch_a", tn=1024, tk=1024)
    yb = mm(b, w["w_b"], mode="nn", out_dtype=F32, name="branch_b", tn=1024, tk=1024)
    merged = _merge_fwd(ya, yb, ga, gb, name="merge_fwd")
    x1 = mm(merged, w["w_out"], mode="nn", out_dtype=F32, name="out_proj", tn=1024, tk=1024, add=x)
    h2 = _rmsnorm_fwd(x1, sm["norm_ffn_g"], name="norm_ffn")
    up_a = mm(h2, wt_up_a, mode="nt", out_dtype=F32, name="up_a", tn=256, tk=1024)
    up_g = mm(h2, wt_up_g, mode="nt", out_dtype=F32, name="up_g", tn=256, tk=1024)
    act = _convglu_fwd(up_a, up_g, cw_a, cw_g, cb_a, cb_g, name="convglu_fwd")
    x2 = mm(act, w["w_down"], mode="nn", out_dtype=F32, name="down", tn=1024, tk=256, add=x1)
    h3 = _rmsnorm_fwd(x2, sm["norm_ple_g"], name="norm_ple")
    ple = mm(p, w["wt_ple"], mode="nt", out_dtype=F32, name="ple", tn=1024, tk=256)
    gp = mm(h3, w["w_pg"], mode="nn", out_dtype=F32, name="ple_gate", tn=1024, tk=1024)
    x3 = _ple_fwd(x2, ple, gp, name="ple_fwd")

    loss, dx3, d_norm_final = _final_loss_bwd(x3, target, sm["norm_final_g"], name="loss_bwd")
    dple, dgp = _ple_bwd(dx3, ple, gp, name="ple_bwd")
    g_wt_ple = mm(dple, p, mode="tn", out_dtype=BF16, name="d_w_ple", tn=256, tk=512)
    g_w_pg = mm(h3, dgp, mode="tn", out_dtype=BF16, name="d_w_pg", tn=1024, tk=512)
    dh3 = mm(dgp, w["w_pg"], mode="nt", out_dtype=F32, name="d_h3", tn=1024, tk=1024)
    dx2, d_norm_ple = _rmsnorm_bwd(dx3, dh3, x2, sm["norm_ple_g"], name="norm_ple_bwd")
    g_w_down = mm(act, dx2, mode="tn", out_dtype=BF16, name="d_w_down", tm=256, tn=1024, tk=512)
    dact = mm(dx2, w["w_down"], mode="nt", out_dtype=BF16, name="d_act", tn=256, tk=1024)
    dup_a, dup_g, dcw_a, dcw_g, dcb_a, dcb_g = _convglu_bwd(dact, up_a, up_g, cw_a, cw_g, cb_a, cb_g, name="convglu_bwd")
    g_wt_up = jnp.concatenate(
        [mm(dup_a, h2, mode="tn", out_dtype=BF16, name="d_w_up_a", tm=256, tn=1024, tk=512),
         mm(dup_g, h2, mode="tn", out_dtype=BF16, name="d_w_up_g", tm=256, tn=1024, tk=512)], axis=0)
    dh2 = mm(dup_a, wt_up_a, mode="nn", out_dtype=F32, name="d_h2_a", tn=1024, tk=256)
    dh2 = mm(dup_g, wt_up_g, mode="nn", out_dtype=F32, name="d_h2_g", tn=1024, tk=256, add=dh2)
    dx1, d_norm_ffn = _rmsnorm_bwd(dx2, dh2, x1, sm["norm_ffn_g"], name="norm_ffn_bwd")
    g_w_out = mm(merged, dx1, mode="tn", out_dtype=BF16, name="d_w_out", tn=1024, tk=512)
    dmerged = mm(dx1, w["w_out"], mode="nt", out_dtype=F32, name="d_merged", tn=1024, tk=1024)
    dya, dyb, dga, dgb = _merge_bwd(dmerged, ya, yb, ga, gb, name="merge_bwd")
    g_w_a = mm(a, dya, mode="tn", out_dtype=BF16, name="d_w_a", tn=1024, tk=512)
    g_w_b = mm(b, dyb, mode="tn", out_dtype=BF16, name="d_w_b", tn=1024, tk=512)
    da = mm(dya, w["w_a"], mode="nt", out_dtype=BF16, name="d_a", tn=1024, tk=1024)
    db = mm(dyb, w["w_b"], mode="nt", out_dtype=BF16, name="d_b", tn=1024, tk=1024)
    dzu, dzv, d_w_s, d_bs_t, d_ln_g, d_ln_b = _gmlp_bwd(
        da, zu, zv, sm["gmlp_ln_g"], sm["gmlp_ln_b"], sm["gmlp_w_s"], bs_t, name="gmlp_bwd")
    delta3 = _attn_delta(db, b, name="attn_delta")
    dq, dk, dv, aux, dcq3 = _attn_bwd(qa, ka, qkv, db, lse3, delta3, name="attn_bwd")
    dcq16 = jnp.pad(dcq3[:, :2, :].reshape(HEADS, s).T, ((0, 0), (0, 128 - HEADS)))
    aux = aux.reshape(s, HEADS // 2, 128)
    dck16 = -jnp.stack([aux[:, :, HEAD_DIM + 3], aux[:, :, 3]], axis=-1).reshape(s, HEADS)
    dck16 = jnp.pad(dck16, ((0, 0), (0, 128 - HEADS)))
    dzf, d_b_f = _forget_bwd(dcq16, dck16, f_logit, b_f, name="forget_bwd")
    dz = jnp.concatenate([dzu, dzv, dq, dk, dv, dga, dgb], axis=1)
    g_wt_main = mm(dz, h, mode="tn", out_dtype=BF16, name="d_w_main", tn=1024, tk=512)
    g_wt_f = mm(dzf, h, mode="tn", out_dtype=BF16, name="d_w_f", tn=1024, tk=512)
    dh = mm(dz, wt_main, mode="nn", out_dtype=F32, name="d_h_main", tn=1024, tk=1024)
    dh = mm(dzf, w["wt_f"], mode="nn", out_dtype=F32, name="d_h_f", tn=1024, add=dh)
    dx0, d_norm_mix = _rmsnorm_bwd(dx1, dh, x, sm["norm_mix_g"], name="norm_mix_bwd")

    grads = dict(wt_main=g_wt_main, wt_f=g_wt_f, w_a=g_w_a, w_b=g_w_b, w_out=g_w_out, wt_up=g_wt_up,
                 conv_w=jnp.concatenate([dcw_a, dcw_g], axis=1), w_down=g_w_down, wt_ple=g_wt_ple, w_pg=g_w_pg)
    small = dict(norm_mix_g=d_norm_mix, b_f=d_b_f[:, :HEADS], gmlp_ln_g=d_ln_g, gmlp_ln_b=d_ln_b, gmlp_w_s=d_w_s,
                 gmlp_b_s=d_bs_t[:, :GROUPS].T, norm_ffn_g=d_norm_ffn,
                 conv_b=jnp.concatenate([dcb_a, dcb_g], axis=1), norm_ple_g=d_norm_ple, norm_final_g=d_norm_final)
    return loss, dx0, grads, small


def kernel(x, p, norm_mix_g, w_in, b_f, gmlp_ln_g, gmlp_ln_b, gmlp_w_s, gmlp_b_s, w_branch_a, w_branch_b, w_out, norm_ffn_g, w_up, conv_w, conv_b, w_down, norm_ple_g, w_ple, w_ple_gate, norm_final_g, loss_target, m_norm_mix_g, m_w_in, m_b_f, m_gmlp_ln_g, m_gmlp_ln_b, m_gmlp_w_s, m_gmlp_b_s, m_w_branch_a, m_w_branch_b, m_w_out, m_norm_ffn_g, m_w_up, m_conv_w, m_conv_b, m_w_down, m_norm_ple_g, m_w_ple, m_w_ple_gate, m_norm_final_g, v_norm_mix_g, v_w_in, v_b_f, v_gmlp_ln_g, v_gmlp_ln_b, v_gmlp_w_s, v_gmlp_b_s, v_w_branch_a, v_w_branch_b, v_w_out, v_norm_ffn_g, v_w_up, v_conv_w, v_conv_b, v_w_down, v_norm_ple_g, v_w_ple, v_w_ple_gate, v_norm_final_g):
    given = dict(locals())
    weights = {n: given[n] for n in WEIGHT_ORDER}
    mom_m = {n: given["m_" + n] for n in WEIGHT_ORDER}
    mom_v = {n: given["v_" + n] for n in WEIGHT_ORDER}
    pos = jnp.stack([lax.axis_index("x"), lax.axis_index("y"), lax.axis_index("c")]).astype(I32)
    names = [n for n, _ in SHARDED]
    kinds = dict(SHARDED)

    gathered = _allgather([_to_comm(n, kinds[n], weights[n]) for n in names], name="allgather_weights")
    full = _assemble_weights(dict(zip(names, gathered)))

    sm = dict(norm_mix_g=norm_mix_g, b_f=b_f, gmlp_ln_g=gmlp_ln_g, gmlp_ln_b=gmlp_ln_b, gmlp_w_s=gmlp_w_s[0],
              gmlp_b_s=gmlp_b_s[0], norm_ffn_g=norm_ffn_g, conv_b=conv_b, norm_ple_g=norm_ple_g,
              norm_final_g=norm_final_g.reshape(1, D_MODEL))
    loss_part, dx0, grads, small = _local_step(x[0], p[0, 0], loss_target[0], full, sm)

    slabs = _grad_slabs(grads)
    small_g = _pack_small(small)
    *from_sib, small_sib = _exchange_sibling([slabs[n] for n in names], small_g, name="exchange_sibling")
    sums = [_sum_pairs(slabs[n], r, pos, name="sum_sibling_" + n) for n, r in zip(names, from_sib)]
    small_chip = _pair_sum_small(small_g, small_sib, name="sum_sibling_small")
    *from_chips, small_tab = _exchange_chips([s16 for _, s16 in sums], small_chip, name="exchange_chips")

    grad, delta, new_m, new_v = {}, {}, {}, {}
    for n, (s32, _), r in zip(names, sums, from_chips):
        g = _from_comm(n, kinds[n], _sum_chips(s32, r, pos, name="sum_chips_" + n))
        d, mn, vn = _adam(g[0], weights[n][0], mom_m[n][0], mom_v[n][0], name="adam_" + n)
        grad[n], delta[n], new_m[n], new_v[n] = g, d[None], mn[None], vn[None]
    replicated = [n for n, _ in SMALL]
    rep = lambda src: _pack_small({n: src[n] for n in replicated})
    packed = _adam_replicated(small_tab, rep(weights), rep(mom_m), rep(mom_v), name="adam_replicated")
    for out, pk in zip((grad, delta, new_m, new_v), packed):
        for n in replicated:
            out[n] = _small(pk, n, weights[n].shape)

    loss = lax.psum(loss_part[0, 0], ("x", "y", "c"))
    return (loss, dx0[None], *[grad[n] for n in WEIGHT_ORDER], *[delta[n] for n in WEIGHT_ORDER],
            *[new_m[n] for n in WEIGHT_ORDER], *[new_v[n] for n in WEIGHT_ORDER])
```

```python
import functools
import math

import jax
import jax.numpy as jnp
from jax import lax
from jax.experimental import pallas as pl
from jax.experimental.pallas import tpu as pltpu

F32 = jnp.float32
BF16 = jnp.bfloat16
I32 = jnp.int32

D_MODEL = 1024
GROUPS = 8
GDIM = 128
GBLOCK = 128
CHUNK = 64
HEADS = 16
HEAD_DIM = 64
D_FF = 2816
PLE_DIM = 256
EPS = 1e-6
N_DEV = 8
ATT_SCALE = HEAD_DIM ** -0.5
NEG = -1e30

ADAM_LR = 0.001
ADAM_B1 = 0.9
ADAM_B2 = 0.999
ADAM_EPS = 1e-08
ADAM_WD = 0.01
ADAM_STEP = 10

V7X_VMEM_LIMIT = 48 * 1024 * 1024
MESH = pl.DeviceIdType.MESH

O_F = 2 * 1024 + 3 * 1024
O_G = O_F + HEADS
IN_COLS = O_G + 2 * D_MODEL
MAIN_COLS = IN_COLS - HEADS
IN_SHARD = IN_COLS // N_DEV
IN_SHARD_PAD = 912

SHARDED = (("w_in", "cols"), ("w_branch_a", "rows"), ("w_branch_b", "rows"), ("w_out", "rows"), ("w_up", "cols"),
           ("conv_w", "f32"), ("w_down", "rows"), ("w_ple", "cols"), ("w_ple_gate", "rows"))

SMALL = (("norm_mix_g", 8), ("b_f", 8), ("gmlp_ln_g", 8), ("gmlp_ln_b", 8), ("gmlp_w_s", 128), ("gmlp_b_s", 8),
         ("norm_ffn_g", 8), ("conv_b", 8), ("norm_ple_g", 8), ("norm_final_g", 8))
SMALL_OFF = {}
_o = 0
for _n, _r in SMALL:
    SMALL_OFF[_n] = (_o, _r)
    _o += _r
SMALL_ROWS = _o

WEIGHT_ORDER = ("norm_mix_g", "w_in", "b_f", "gmlp_ln_g", "gmlp_ln_b", "gmlp_w_s", "gmlp_b_s", "w_branch_a",
                "w_branch_b", "w_out", "norm_ffn_g", "w_up", "conv_w", "conv_b", "w_down", "norm_ple_g", "w_ple",
                "w_ple_gate", "norm_final_g")


def _cparams(sem):
    return pltpu.CompilerParams(dimension_semantics=sem, vmem_limit_bytes=V7X_VMEM_LIMIT)


def _gelu(x):
    c = math.sqrt(2.0 / math.pi)
    return 0.5 * x * (1.0 + jnp.tanh(c * (x + 0.044715 * x * x * x)))


def _gelu_and_grad(x):
    c = math.sqrt(2.0 / math.pi)
    t = jnp.tanh(c * (x + 0.044715 * x * x * x))
    g = 0.5 * x * (1.0 + t)
    dg = 0.5 * (1.0 + t) + 0.5 * x * (1.0 - t * t) * (c * (1.0 + 3.0 * 0.044715 * x * x))
    return g, dg


def _sigmoid(x):
    return 1.0 / (1.0 + jnp.exp(-x))


def _dot(a, b, dims):
    return lax.dot_general(a, b, (dims, ((), ())), preferred_element_type=F32)


NN = ((1,), (0,))
NT = ((1,), (1,))
TN = ((0,), (0,))


def _row_tile(rows, most):
    best = None
    for t in range(16, min(rows, most) + 1, 16):
        if rows % t == 0:
            best = t
    return best if best is not None else rows


def _matmul(a, b, *, mode, out_dtype, name, tm=512, tn=512, tk=512, add=None, n=None, b_off=0):
    if mode == "tn":
        kdim, m = a.shape
    else:
        m, kdim = a.shape
    if n is None:
        n = b.shape[0] if mode == "nt" else b.shape[1]
    tm, tn, tk = min(tm, m), min(tn, n), min(tk, kdim)
    assert m % tm == 0 and n % tn == 0 and kdim % tk == 0, (name, m, n, kdim, tm, tn, tk)
    nk = kdim // tk
    dims = {"nn": NN, "nt": NT, "tn": TN}[mode]

    def finish(r, add_ref, o_ref):
        if add_ref is not None:
            r = add_ref[...].astype(F32) + r
        o_ref[...] = r.astype(out_dtype)

    def body(*refs):
        refs = list(refs)
        a_ref, b_ref = refs[:2]
        add_ref = refs[2] if add is not None else None
        o_ref = refs[3] if add is not None else refs[2]
        part = _dot(a_ref[...].astype(BF16), b_ref[...].astype(BF16), dims)
        if nk == 1:
            finish(part, add_ref, o_ref)
            return
        acc_ref = refs[-1]
        k = pl.program_id(2)

        @pl.when(k == 0)
        def _():
            acc_ref[...] = part

        @pl.when((k > 0) & (k < nk - 1))
        def _():
            acc_ref[...] += part

        @pl.when(k == nk - 1)
        def _():
            finish(acc_ref[...] + part, add_ref, o_ref)

    a_spec = pl.BlockSpec((tk, tm), lambda i, j, k: (k, i)) if mode == "tn" else pl.BlockSpec((tm, tk), lambda i, j, k: (i, k))
    if mode == "nt":
        b_spec = pl.BlockSpec((tn, tk), lambda i, j, k: (j + b_off, k))
    else:
        b_spec = pl.BlockSpec((tk, tn), lambda i, j, k: (k + b_off, j))
    o_spec = pl.BlockSpec((tm, tn), lambda i, j, k: (i, j))
    in_specs = [a_spec, b_spec] + ([o_spec] if add is not None else [])
    args = (a, b) + ((add,) if add is not None else ())
    return pl.pallas_call(
        body, name=name, grid=(m // tm, n // tn, nk),
        in_specs=in_specs, out_specs=o_spec,
        out_shape=jax.ShapeDtypeStruct((m, n), out_dtype),
        scratch_shapes=[pltpu.VMEM((tm, tn), F32)] if nk > 1 else [],
        compiler_params=_cparams(("parallel", "parallel", "arbitrary")),
    )(*args)


def _row_spec(tr, width, col_block=0):
    return pl.BlockSpec((tr, width), lambda i: (i, col_block))


def _full_spec(shape):
    return pl.BlockSpec(shape, lambda i: tuple(0 for _ in shape))


def _rmsnorm_fwd(x, g, *, name, tr=256):
    s, d = x.shape

    def body(x_ref, g_ref, o_ref):
        xv = x_ref[...]
        r = lax.rsqrt(jnp.mean(xv * xv, axis=-1, keepdims=True) + EPS)
        o_ref[...] = ((xv * r) * g_ref[...]).astype(BF16)

    return pl.pallas_call(
        body, name=name, grid=(s // tr,),
        in_specs=[_row_spec(tr, d), _full_spec((1, d))], out_specs=_row_spec(tr, d),
        out_shape=jax.ShapeDtypeStruct((s, d), BF16), compiler_params=_cparams(("parallel",)),
    )(x, g)


def _rmsnorm_bwd(dres, dh, x, g, *, name, tr=256):
    s, d = x.shape

    def body(dres_ref, dh_ref, x_ref, g_ref, dx_ref, dxb_ref, dg_ref):
        i = pl.program_id(0)
        xv = x_ref[...]
        r = lax.rsqrt(jnp.mean(xv * xv, axis=-1, keepdims=True) + EPS)
        xhat = xv * r
        dhv = dh_ref[...].astype(F32)
        dxhat = dhv * g_ref[...]
        dx = dres_ref[...] + r * (dxhat - xhat * jnp.mean(dxhat * xhat, axis=-1, keepdims=True))
        dx_ref[...] = dx
        dxb_ref[...] = dx.astype(BF16)
        dgp = jnp.sum(dhv * xhat, axis=0, keepdims=True)

        @pl.when(i == 0)
        def _():
            dg_ref[...] = dgp

        @pl.when(i > 0)
        def _():
            dg_ref[...] += dgp

    return pl.pallas_call(
        body, name=name, grid=(s // tr,),
        in_specs=[_row_spec(tr, d), _row_spec(tr, d), _row_spec(tr, d), _full_spec((1, d))],
        out_specs=[_row_spec(tr, d), _row_spec(tr, d), _full_spec((1, d))],
        out_shape=[jax.ShapeDtypeStruct((s, d), F32), jax.ShapeDtypeStruct((s, d), BF16),
                   jax.ShapeDtypeStruct((1, d), F32)],
        compiler_params=_cparams(("arbitrary",)),
    )(dres, dh, x, g)


def _final_loss_bwd(x3, target, g, *, name, tr=256):
    s, d = x3.shape

    def body(x_ref, t_ref, g_ref, loss_ref, dx_ref, dg_ref):
        i = pl.program_id(0)
        xv = x_ref[...]
        r = lax.rsqrt(jnp.mean(xv * xv, axis=-1, keepdims=True) + EPS)
        xhat = xv * r
        diff = xhat * g_ref[...] - t_ref[...]
        lp = jnp.zeros((1, 128), F32) + (0.5 / d) * jnp.sum(diff * diff)
        dy = diff * (1.0 / d)
        dxhat = dy * g_ref[...]
        dx_ref[...] = r * (dxhat - xhat * jnp.mean(dxhat * xhat, axis=-1, keepdims=True))
        dgp = jnp.sum(dy * xhat, axis=0, keepdims=True)

        @pl.when(i == 0)
        def _():
            dg_ref[...] = dgp
            loss_ref[...] = lp

        @pl.when(i > 0)
        def _():
            dg_ref[...] += dgp
            loss_ref[...] += lp

    return pl.pallas_call(
        body, name=name, grid=(s // tr,),
        in_specs=[_row_spec(tr, d), _row_spec(tr, d), _full_spec((1, d))],
        out_specs=[_full_spec((1, 128)), _row_spec(tr, d), _full_spec((1, d))],
        out_shape=[jax.ShapeDtypeStruct((1, 128), F32), jax.ShapeDtypeStruct((s, d), F32),
                   jax.ShapeDtypeStruct((1, d), F32)],
        compiler_params=_cparams(("arbitrary",)),
    )(x3, target, g)


def _merge_fwd(ya, yb, zuvg, *, name, tr=256):
    s, d = ya.shape

    def body(ya_ref, yb_ref, ga_ref, gb_ref, o_ref):
        o_ref[...] = (_sigmoid(ga_ref[...]) * ya_ref[...] + _sigmoid(gb_ref[...]) * yb_ref[...]).astype(BF16)

    return pl.pallas_call(
        body, name=name, grid=(s // tr,),
        in_specs=[_row_spec(tr, d), _row_spec(tr, d), _row_spec(tr, d, 2), _row_spec(tr, d, 3)],
        out_specs=_row_spec(tr, d),
        out_shape=jax.ShapeDtypeStruct((s, d), BF16), compiler_params=_cparams(("parallel",)),
    )(ya, yb, zuvg, zuvg)


def _merge_bwd(dm, ya, yb, zuvg, *, name, tr=256):
    s, d = ya.shape

    def body(dm_ref, ya_ref, yb_ref, ga_ref, gb_ref, dya_ref, dyb_ref, dga_ref, dgb_ref):
        dmv = dm_ref[...]
        sa = _sigmoid(ga_ref[...])
        sb = _sigmoid(gb_ref[...])
        dya_ref[...] = (dmv * sa).astype(BF16)
        dyb_ref[...] = (dmv * sb).astype(BF16)
        dga_ref[...] = (dmv * ya_ref[...] * (sa * (1.0 - sa))).astype(BF16)
        dgb_ref[...] = (dmv * yb_ref[...] * (sb * (1.0 - sb))).astype(BF16)

    o = jax.ShapeDtypeStruct((s, d), BF16)
    return pl.pallas_call(
        body, name=name, grid=(s // tr,),
        in_specs=[_row_spec(tr, d)] * 3 + [_row_spec(tr, d, 2), _row_spec(tr, d, 3)], out_specs=[_row_spec(tr, d)] * 4,
        out_shape=[o, o, o, o], compiler_params=_cparams(("parallel",)),
    )(dm, ya, yb, zuvg, zuvg)


def _ple_fwd(x2, ple, gp, *, name, tr=256):
    s, d = x2.shape

    def body(x_ref, ple_ref, gp_ref, o_ref):
        o_ref[...] = x_ref[...] + ple_ref[...] * _sigmoid(gp_ref[...])

    return pl.pallas_call(
        body, name=name, grid=(s // tr,),
        in_specs=[_row_spec(tr, d)] * 3, out_specs=_row_spec(tr, d),
        out_shape=jax.ShapeDtypeStruct((s, d), F32), compiler_params=_cparams(("parallel",)),
    )(x2, ple, gp)


def _ple_bwd(dx3, ple, gp, *, name, tr=256):
    s, d = dx3.shape

    def body(dx_ref, ple_ref, gp_ref, dple_ref, dgp_ref):
        sg = _sigmoid(gp_ref[...])
        dxv = dx_ref[...]
        dple_ref[...] = (dxv * sg).astype(BF16)
        dgp_ref[...] = (dxv * ple_ref[...] * (sg * (1.0 - sg))).astype(BF16)

    o = jax.ShapeDtypeStruct((s, d), BF16)
    return pl.pallas_call(
        body, name=name, grid=(s // tr,),
        in_specs=[_row_spec(tr, d)] * 3, out_specs=[_row_spec(tr, d)] * 2,
        out_shape=[o, o], compiler_params=_cparams(("parallel",)),
    )(dx3, ple, gp)


def _masked_ws(ws_ref, g):
    row = lax.broadcasted_iota(I32, (GBLOCK, GBLOCK), 0)
    col = lax.broadcasted_iota(I32, (GBLOCK, GBLOCK), 1)
    keep = (col // CHUNK) <= (row // CHUNK)
    return jnp.where(keep, ws_ref[g], 0.0), keep


def _layernorm_parts(zv):
    mu = jnp.mean(zv, axis=-1, keepdims=True)
    xc = zv - mu
    rs = lax.rsqrt(jnp.mean(xc * xc, axis=-1, keepdims=True) + EPS)
    return xc * rs, rs


def _gmlp_fwd(zuvg, ln_g, ln_b, w_s, bs_t, *, name):
    s, w = zuvg.shape[0], GROUPS * GDIM

    def body(zu_ref, zv_ref, lng_ref, lnb_ref, ws_ref, bs_ref, a_ref):
        zu = _gelu(zu_ref[...])
        zv = _gelu(zv_ref[...])
        xhat, _ = _layernorm_parts(zv)
        vln = (xhat * lng_ref[...] + lnb_ref[...]).astype(BF16)
        for g in range(GROUPS):
            wm, _ = _masked_ws(ws_ref, g)
            mixed = _dot(wm.astype(BF16), vln[:, g * GDIM:(g + 1) * GDIM], NN) + bs_ref[:, g:g + 1]
            a_ref[:, g * GDIM:(g + 1) * GDIM] = (zu[:, g * GDIM:(g + 1) * GDIM] * mixed).astype(BF16)

    return pl.pallas_call(
        body, name=name, grid=(s // GBLOCK,),
        in_specs=[_row_spec(GBLOCK, w, 0), _row_spec(GBLOCK, w, 1), _full_spec((1, w)), _full_spec((1, w)),
                  _full_spec((GROUPS, GBLOCK, GBLOCK)), _full_spec((GBLOCK, 128))],
        out_specs=_row_spec(GBLOCK, w),
        out_shape=jax.ShapeDtypeStruct((s, w), BF16), compiler_params=_cparams(("parallel",)),
    )(zuvg, zuvg, ln_g, ln_b, w_s, bs_t)


def _gmlp_bwd(da, zuvg, ln_g, ln_b, w_s, bs_t, *, name):
    s, w = zuvg.shape[0], GROUPS * GDIM

    def body(da_ref, zu_ref, zv_ref, lng_ref, lnb_ref, ws_ref, bs_ref,
             dzu_ref, dzv_ref, dws_ref, dbs_ref, dlng_ref, dlnb_ref, dvln_ref):
        i = pl.program_id(0)
        zu, dzu_g = _gelu_and_grad(zu_ref[...])
        zv, dzv_g = _gelu_and_grad(zv_ref[...])
        xhat, rs = _layernorm_parts(zv)
        vln = (xhat * lng_ref[...] + lnb_ref[...]).astype(BF16)
        dav = da_ref[...].astype(F32)
        lane = lax.broadcasted_iota(I32, (GBLOCK, 128), 1)
        dbs = jnp.zeros((GBLOCK, 128), F32)

        @pl.when(i == 0)
        def _():
            dws_ref[...] = jnp.zeros_like(dws_ref)

        for g in range(GROUPS):
            sl = slice(g * GDIM, (g + 1) * GDIM)
            wm, keep = _masked_ws(ws_ref, g)
            wmb = wm.astype(BF16)
            vg = vln[:, sl]
            mixed = _dot(wmb, vg, NN) + bs_ref[:, g:g + 1]
            dag = dav[:, sl]
            dzu_ref[:, sl] = (dag * mixed * dzu_g[:, sl]).astype(BF16)
            dmix = dag * zu[:, sl]
            dmb = dmix.astype(BF16)
            dws_ref[g] += jnp.where(keep, _dot(dmb, vg, NT), 0.0)
            dbs = jnp.where(lane == g, jnp.sum(dmix, axis=1, keepdims=True), dbs)
            dvln_ref[:, sl] = _dot(wmb, dmb, TN)
        dvln = dvln_ref[...]
        dxhat = dvln * lng_ref[...]
        dzv = rs * (dxhat - jnp.mean(dxhat, axis=-1, keepdims=True)
                    - xhat * jnp.mean(dxhat * xhat, axis=-1, keepdims=True))
        dzv_ref[...] = (dzv * dzv_g).astype(BF16)
        dlng = jnp.sum(dvln * xhat, axis=0, keepdims=True)
        dlnb = jnp.sum(dvln, axis=0, keepdims=True)

        @pl.when(i == 0)
        def _():
            dbs_ref[...] = dbs
            dlng_ref[...] = dlng
            dlnb_ref[...] = dlnb

        @pl.when(i > 0)
        def _():
            dbs_ref[...] += dbs
            dlng_ref[...] += dlng
            dlnb_ref[...] += dlnb

    return pl.pallas_call(
        body, name=name, grid=(s // GBLOCK,),
        in_specs=[_row_spec(GBLOCK, w), _row_spec(GBLOCK, w, 0), _row_spec(GBLOCK, w, 1), _full_spec((1, w)),
                  _full_spec((1, w)), _full_spec((GROUPS, GBLOCK, GBLOCK)), _full_spec((GBLOCK, 128))],
        out_specs=[_row_spec(GBLOCK, w), _row_spec(GBLOCK, w), _full_spec((GROUPS, GBLOCK, GBLOCK)),
                   _full_spec((GBLOCK, 128)), _full_spec((1, w)), _full_spec((1, w))],
        out_shape=[jax.ShapeDtypeStruct((s, w), BF16), jax.ShapeDtypeStruct((s, w), BF16),
                   jax.ShapeDtypeStruct((GROUPS, GBLOCK, GBLOCK), F32), jax.ShapeDtypeStruct((GBLOCK, 128), F32),
                   jax.ShapeDtypeStruct((1, w), F32), jax.ShapeDtypeStruct((1, w), F32)],
        scratch_shapes=[pltpu.VMEM((GBLOCK, w), F32)],
        compiler_params=_cparams(("arbitrary",)),
    )(da, zuvg, zuvg, ln_g, ln_b, w_s, bs_t)


def _shift_down(u, k):
    row = lax.broadcasted_iota(I32, u.shape, 0)
    return jnp.where(row >= k, pltpu.roll(u, k, 0), 0.0)


def _shift_up(u, k):
    s = u.shape[0]
    row = lax.broadcasted_iota(I32, u.shape, 0)
    return jnp.where(row < s - k, pltpu.roll(u, s - k, 0), 0.0)


def _conv(u, w_ref, b_ref):
    return b_ref[...] + w_ref[0:1, :] * _shift_down(u, 2) + w_ref[1:2, :] * _shift_down(u, 1) + w_ref[2:3, :] * u


def _conv_specs(s, f, tc):
    nc = f // tc
    half = lambda rows: [pl.BlockSpec((rows, tc), lambda j: (0, j)), pl.BlockSpec((rows, tc), lambda j: (0, nc + j))]
    return half(s), half(3), half(1)


def _convglu_fwd(up, conv_w, conv_b, *, name, tc=256):
    s, f = up.shape[0], up.shape[1] // 2
    up_specs, w_specs, b_specs = _conv_specs(s, f, tc)

    def body(ua_ref, ug_ref, wa_ref, wg_ref, ba_ref, bg_ref, o_ref):
        ca = _conv(ua_ref[...], wa_ref, ba_ref)
        cg = _conv(ug_ref[...], wg_ref, bg_ref)
        o_ref[...] = (_gelu(ca) * cg).astype(BF16)

    return pl.pallas_call(
        body, name=name, grid=(f // tc,),
        in_specs=up_specs + w_specs + b_specs, out_specs=up_specs[0],
        out_shape=jax.ShapeDtypeStruct((s, f), BF16), compiler_params=_cparams(("parallel",)),
    )(up, up, conv_w, conv_w, conv_b, conv_b)


def _convglu_bwd(dact, up, conv_w, conv_b, *, name, tc=256):
    s, f = up.shape[0], up.shape[1] // 2
    up_specs, w_specs, b_specs = _conv_specs(s, f, tc)

    def half(dc, u, w_ref, du_ref, dw_ref, db_ref):
        db_ref[...] = jnp.sum(dc, axis=0, keepdims=True)
        dw_ref[0:1, :] = jnp.sum(dc * _shift_down(u, 2), axis=0, keepdims=True)
        dw_ref[1:2, :] = jnp.sum(dc * _shift_down(u, 1), axis=0, keepdims=True)
        dw_ref[2:3, :] = jnp.sum(dc * u, axis=0, keepdims=True)
        du = w_ref[2:3, :] * dc + w_ref[1:2, :] * _shift_up(dc, 1) + w_ref[0:1, :] * _shift_up(dc, 2)
        du_ref[...] = du.astype(BF16)

    def body(d_ref, ua_ref, ug_ref, wa_ref, wg_ref, ba_ref, bg_ref,
             dua_ref, dug_ref, dwa_ref, dwg_ref, dba_ref, dbg_ref):
        ua = ua_ref[...]
        ug = ug_ref[...]
        ca = _conv(ua, wa_ref, ba_ref)
        cg = _conv(ug, wg_ref, bg_ref)
        ga, dga = _gelu_and_grad(ca)
        dv = d_ref[...].astype(F32)
        half(dv * cg * dga, ua, wa_ref, dua_ref, dwa_ref, dba_ref)
        half(dv * ga, ug, wg_ref, dug_ref, dwg_ref, dbg_ref)

    col, w3, b1 = up_specs[0], w_specs[0], b_specs[0]
    return pl.pallas_call(
        body, name=name, grid=(f // tc,),
        in_specs=[col] + up_specs + w_specs + b_specs, out_specs=[col, col, w3, w3, b1, b1],
        out_shape=[jax.ShapeDtypeStruct((s, f), BF16), jax.ShapeDtypeStruct((s, f), BF16),
                   jax.ShapeDtypeStruct((3, f), F32), jax.ShapeDtypeStruct((3, f), F32),
                   jax.ShapeDtypeStruct((1, f), F32), jax.ShapeDtypeStruct((1, f), F32)],
        compiler_params=_cparams(("parallel",)),
    )(dact, up, up, conv_w, conv_w, conv_b, conv_b)


def _tri_dot(tri, x):
    b0 = x.astype(BF16)
    r1 = x - b0.astype(F32)
    b1 = r1.astype(BF16)
    b2 = (r1 - b1.astype(F32)).astype(BF16)
    return _dot(tri, b0, NN) + _dot(tri, b1, NN) + _dot(tri, b2, NN)


def _log_sigmoid(x):
    return jnp.minimum(x, 0.0) - jnp.log(1.0 + jnp.exp(-jnp.abs(x)))


def _expand_heads(col16, rows):
    head_of_lane = lax.broadcasted_iota(I32, (rows, HEADS * HEAD_DIM), 1) // HEAD_DIM
    out = jnp.zeros((rows, HEADS * HEAD_DIM), F32)
    for h in range(HEADS):
        out = jnp.where(head_of_lane == h, col16[:, h:h + 1], out)
    return out


def _forget_cumsum(f_logit, b_f, *, name):
    s = f_logit.shape[0]
    nb = s // 128

    def body(f_ref, b_ref, cqe_ref):
        row = lax.broadcasted_iota(I32, (128, 128), 0)
        col = lax.broadcasted_iota(I32, (128, 128), 1)
        tri = (col <= row).astype(BF16)

        def step(n, carry):
            r0 = pl.multiple_of(n * 128, 128)
            lf = _log_sigmoid(f_ref[pl.ds(r0, 128), :] + b_ref[...])
            cum = _tri_dot(tri, lf) + carry
            cqe_ref[pl.ds(r0, 128), :] = _expand_heads(cum, 128)
            return cum[127:128, :]

        lax.fori_loop(0, nb, step, jnp.zeros((1, 128), F32))

    return pl.pallas_call(
        body, name=name, grid=(1,),
        in_specs=[_full_spec((s, 128)), _full_spec((1, 128))],
        out_specs=_full_spec((s, HEADS * HEAD_DIM)),
        out_shape=jax.ShapeDtypeStruct((s, HEADS * HEAD_DIM), F32),
        compiler_params=_cparams(("arbitrary",)),
    )(f_logit, b_f)


def _forget_bwd(dcq16, dck16, f_logit, b_f, *, name):
    s = f_logit.shape[0]
    nb = s // 128

    def body(a_ref, k_ref, f_ref, b_ref, df_ref, db_ref):
        row = lax.broadcasted_iota(I32, (128, 128), 0)
        col = lax.broadcasted_iota(I32, (128, 128), 1)
        tri_rev = (col >= row).astype(BF16)

        def step(m, carry):
            suffix, dbsum = carry
            n = nb - 1 - m
            r0 = pl.multiple_of(n * 128, 128)
            dcum = a_ref[pl.ds(r0, 128), :] + k_ref[pl.ds(r0, 128), :]
            dlf = _tri_dot(tri_rev, dcum) + suffix
            df = dlf * _sigmoid(-(f_ref[pl.ds(r0, 128), :] + b_ref[...]))
            df_ref[pl.ds(r0, 128), :] = df.astype(BF16)
            return dlf[0:1, :], dbsum + jnp.sum(df, axis=0, keepdims=True)

        _, dbsum = lax.fori_loop(0, nb, step, (jnp.zeros((1, 128), F32), jnp.zeros((1, 128), F32)))
        db_ref[...] = dbsum

    return pl.pallas_call(
        body, name=name, grid=(1,),
        in_specs=[_full_spec((s, 128))] * 3 + [_full_spec((1, 128))],
        out_specs=[_full_spec((s, 128)), _full_spec((1, 128))],
        out_shape=[jax.ShapeDtypeStruct((s, 128), BF16), jax.ShapeDtypeStruct((1, 128), F32)],
        compiler_params=_cparams(("arbitrary",)),
    )(dcq16, dck16, f_logit, b_f)


ATT_T = 256


def _head_lanes(rows):
    return lax.broadcasted_iota(I32, (rows, 128), 1) < HEAD_DIM


def _bf16_pieces(c):
    p0 = c.astype(BF16).astype(F32)
    r = c - p0
    p1 = r.astype(BF16).astype(F32)
    p2 = (r - p1).astype(BF16).astype(F32)
    return p0, p1, p2


def _col_reduce(x, op):
    rows = x.shape[0]
    while rows > 8:
        rows //= 2
        x = op(x[:rows], x[rows:])
    return jnp.max(x, axis=0, keepdims=True) if op is jnp.maximum else jnp.sum(x, axis=0, keepdims=True)


def _attn_prep(qkv, cqe, *, name):
    s = qkv.shape[0]
    npair = HEADS // 2

    def body(q_ref, k_ref, v_ref, c_ref, qa_ref, ka_ref, vt_ref):
        rows = 128
        lane = lax.broadcasted_iota(I32, (rows, 128), 1)

        def chunk(n, _):
            r0 = pl.multiple_of(n * rows, rows)
            sl = pl.ds(r0, rows)
            qv = q_ref[sl, :].astype(F32) * ATT_SCALE
            kv = k_ref[sl, :].astype(F32)
            for e in range(2):
                mine = (lane < HEAD_DIM) if e == 0 else (lane >= HEAD_DIM)
                base = HEAD_DIM * (1 - e)
                p0, p1, p2 = _bf16_pieces(c_ref[sl, HEAD_DIM * e:HEAD_DIM * e + 1])
                ones_hi = jnp.where((lane >= base + 3) & (lane < base + 6), 1.0, 0.0)
                ones_lo = jnp.where((lane >= base) & (lane < base + 3), 1.0, 0.0)
                qa = jnp.where(mine, qv, jnp.where(lane == base, p0, jnp.where(lane == base + 1, p1,
                               jnp.where(lane == base + 2, p2, ones_hi))))
                ka = jnp.where(mine, kv, jnp.where(lane == base + 3, -p0, jnp.where(lane == base + 4, -p1,
                               jnp.where(lane == base + 5, -p2, ones_lo))))
                qa_ref[e, sl, :] = qa.astype(BF16)
                ka_ref[e, sl, :] = ka.astype(BF16)
            vt_ref[0, :, sl] = v_ref[sl, :].astype(F32).T.astype(BF16)
            return 0

        lax.fori_loop(0, s // rows, chunk, 0)

    pair = pl.BlockSpec((2, s, 128), lambda hp: (hp, 0, 0))
    return pl.pallas_call(
        body, name=name, grid=(npair,),
        in_specs=[pl.BlockSpec((s, 128), lambda hp: (0, hp)), pl.BlockSpec((s, 128), lambda hp: (0, npair + hp)),
                  pl.BlockSpec((s, 128), lambda hp: (0, 2 * npair + hp)), pl.BlockSpec((s, 128), lambda hp: (0, hp))],
        out_specs=[pair, pair, pl.BlockSpec((1, 128, s), lambda hp: (hp, 0, 0))],
        out_shape=[jax.ShapeDtypeStruct((HEADS, s, 128), BF16), jax.ShapeDtypeStruct((HEADS, s, 128), BF16),
                   jax.ShapeDtypeStruct((npair, 128, s), BF16)],
        compiler_params=_cparams(("parallel",)),
    )(qkv, qkv, qkv, cqe)


def _attn_fwd(qa, ka, vt, *, name):
    s = qa.shape[1]
    t = ATT_T
    nq = s // t

    def body(qa_ref, ka_ref, vt_ref, o_ref, lse_ref):
        i = pl.program_id(1)
        krow = lax.broadcasted_iota(I32, (t, t), 0)
        qcol = lax.broadcasted_iota(I32, (t, t), 1)
        sub = lax.broadcasted_iota(I32, (128, t), 0)
        row8 = lax.broadcasted_iota(I32, (8, t), 0)
        qbs = (qa_ref[0], qa_ref[1])
        tk = 2 * t

        def step(j, carry, diag):
            c0 = pl.multiple_of(j * tk, tk)
            vtb = vt_ref[0, :, pl.ds(c0, tk)]
            sts = [_dot(ka_ref[e, pl.ds(c0, tk), :], qbs[e], NT) for e in range(2)]
            if diag:
                keep = (lax.broadcasted_iota(I32, (tk, t), 0) - lax.broadcasted_iota(I32, (tk, t), 1)) <= t * (i % 2)
                sts = [jnp.where(keep, st, NEG) for st in sts]
            pts, stats = [], []
            for e in range(2):
                m, l, _ = carry[e]
                m_new = jnp.maximum(m, _col_reduce(sts[e], jnp.maximum))
                alpha = jnp.exp(m - m_new)
                pt = jnp.exp(sts[e] - m_new)
                stats.append((m_new, alpha, alpha * l + _col_reduce(pt, jnp.add)))
                pts.append(pt.astype(BF16))
            pvs = [_dot(vtb, pts[e], NN) for e in range(2)]
            return tuple((stats[e][0], stats[e][2], stats[e][1] * carry[e][2] + pvs[e]) for e in range(2))

        init = (jnp.full((1, t), NEG, F32), jnp.zeros((1, t), F32), jnp.zeros((128, t), F32))
        carry = lax.fori_loop(0, i // 2, functools.partial(step, diag=False), (init, init))
        (m0, l0, acc0), (m1, l1, acc1) = step(i // 2, carry, True)
        o_pair = jnp.where(sub < HEAD_DIM, acc0 / l0, acc1 / l1)
        o_ref[...] = o_pair.T.astype(BF16)
        lse_ref[0] = jnp.where(row8 == 0, m0 + jnp.log(l0), jnp.where(row8 == 1, m1 + jnp.log(l1), 0.0))

    return pl.pallas_call(
        body, name=name, grid=(HEADS // 2, nq),
        in_specs=[pl.BlockSpec((2, t, 128), lambda hp, i: (hp, i, 0)), pl.BlockSpec((2, s, 128), lambda hp, i: (hp, 0, 0)),
                  pl.BlockSpec((1, 128, s), lambda hp, i: (hp, 0, 0))],
        out_specs=[pl.BlockSpec((t, 128), lambda hp, i: (i, hp)), pl.BlockSpec((1, 8, t), lambda hp, i: (hp, 0, i))],
        out_shape=[jax.ShapeDtypeStruct((s, HEADS * HEAD_DIM), BF16), jax.ShapeDtypeStruct((HEADS // 2, 8, s), F32)],
        compiler_params=_cparams(("parallel", "arbitrary")),
    )(qa, ka, vt)


def _attn_delta(do, o, *, name):
    s = do.shape[0]

    def body(do_ref, o_ref, d_ref):
        prod = do_ref[...].astype(F32) * o_ref[...].astype(F32)
        row = lax.broadcasted_iota(I32, (8, 128), 0)
        lane = lax.broadcasted_iota(I32, (8, 128), 1)
        sel = ((row == 0) & (lane < HEAD_DIM) | (row == 1) & (lane >= HEAD_DIM)).astype(BF16)
        p0, p1, p2 = _bf16_pieces(prod)
        d_ref[0] = (_dot(sel, p0.astype(BF16), NT) + _dot(sel, p1.astype(BF16), NT)) + _dot(sel, p2.astype(BF16), NT)

    pair = pl.BlockSpec((s, 128), lambda hp: (0, hp))
    return pl.pallas_call(
        body, name=name, grid=(HEADS // 2,), in_specs=[pair, pair],
        out_specs=pl.BlockSpec((1, 8, s), lambda hp: (hp, 0, 0)),
        out_shape=jax.ShapeDtypeStruct((HEADS // 2, 8, s), F32), compiler_params=_cparams(("parallel",)),
    )(do, o)


def _attn_bwd(qa, ka, qkv, do, lse3, delta3, *, name):
    s = qa.shape[1]
    t = ATT_T
    nb = s // t
    npair = HEADS // 2

    def body(qa_ref, ka_ref, v_ref, do_ref, lse_ref, delta_ref,
             dq_ref, dk_ref, dv_ref, aux_ref, dcq_ref, dqt):
        first = _head_lanes(t)
        dqt[...] = jnp.zeros_like(dqt)
        krow = lax.broadcasted_iota(I32, (t, t), 0)
        qcol = lax.broadcasted_iota(I32, (t, t), 1)

        def key_block(j, _):
            c0 = pl.multiple_of(j * t, t)
            vb = v_ref[pl.ds(c0, t), :]
            kbs = (ka_ref[0, pl.ds(c0, t), :], ka_ref[1, pl.ds(c0, t), :])
            kbts = tuple(kb.astype(F32).T.astype(BF16) for kb in kbs)
            vhs = (jnp.where(first, vb, jnp.zeros_like(vb)), jnp.where(first, jnp.zeros_like(vb), vb))

            def query_block(i, carry, diag):
                r0 = pl.multiple_of(i * t, t)
                dob = do_ref[pl.ds(r0, t), :]
                qbs = [qa_ref[e, pl.ds(r0, t), :] for e in range(2)]
                sts = [_dot(kbs[e], qbs[e], NT) for e in range(2)]
                dpts = [_dot(vhs[e], dob, NT) for e in range(2)]
                ptbs, dsbs = [], []
                for e in range(2):
                    st = jnp.where(krow <= qcol, sts[e], NEG) if diag else sts[e]
                    pt = jnp.exp(st - lse_ref[0, e:e + 1, pl.ds(r0, t)])
                    dsbs.append((pt * (dpts[e] - delta_ref[0, e:e + 1, pl.ds(r0, t)])).astype(BF16))
                    ptbs.append(pt.astype(BF16))
                out = []
                for e in range(2):
                    dk_a, dv_a = carry[e]
                    dv_a = dv_a + _dot(ptbs[e], dob, NN)
                    dk_a = dk_a + _dot(dsbs[e], qbs[e], NN)
                    dqt[e, :, pl.ds(r0, t)] += _dot(kbts[e], dsbs[e], NN)
                    out.append((dk_a, dv_a))
                return tuple(out)

            zero = jnp.zeros((t, 128), F32)
            carry = query_block(j, ((zero, zero), (zero, zero)), True)
            (dk0, dv0), (dk1, dv1) = lax.fori_loop(j + 1, nb, functools.partial(query_block, diag=False), carry)
            dk_ref[pl.ds(c0, t), :] = jnp.where(first, dk0, dk1).astype(BF16)
            dv_ref[pl.ds(c0, t), :] = jnp.where(first, dv0, dv1).astype(BF16)
            aux_ref[pl.ds(c0, t), :] = jnp.where(first, dk1, dk0)
            return 0

        lax.fori_loop(0, nb, key_block, 0)
        sub = lax.broadcasted_iota(I32, (128, s), 0)
        row8 = lax.broadcasted_iota(I32, (8, s), 0)
        dq_ref[...] = (jnp.where(sub < HEAD_DIM, dqt[0], dqt[1]) * ATT_SCALE).T.astype(BF16)
        dcq_ref[0] = jnp.where(row8 == 0, dqt[0, HEAD_DIM:HEAD_DIM + 1, :], jnp.where(row8 == 1, dqt[1, 0:1, :], 0.0))

    def pair_cols(off):
        return pl.BlockSpec((s, 128), lambda hp: (0, off + hp))

    heads = pl.BlockSpec((2, s, 128), lambda hp: (hp, 0, 0))
    rows = pl.BlockSpec((1, 8, s), lambda hp: (hp, 0, 0))
    wide = jax.ShapeDtypeStruct((s, HEADS * HEAD_DIM), BF16)
    return pl.pallas_call(
        body, name=name, grid=(npair,),
        in_specs=[heads, heads, pair_cols(2 * npair), pair_cols(0), rows, rows],
        out_specs=[pair_cols(0), pair_cols(0), pair_cols(0), pair_cols(0), rows],
        out_shape=[wide, wide, wide, jax.ShapeDtypeStruct((s, HEADS * HEAD_DIM), F32),
                   jax.ShapeDtypeStruct((npair, 8, s), F32)],
        scratch_shapes=[pltpu.VMEM((2, 128, s), F32)],
        compiler_params=_cparams(("parallel",)),
    )(qa, ka, qkv, do, lse3, delta3)


def _adam_math(w, g, m, v):
    m = ADAM_B1 * m + (1.0 - ADAM_B1) * g
    v = ADAM_B2 * v + (1.0 - ADAM_B2) * (g * g)
    m_hat = m / (1.0 - ADAM_B1 ** ADAM_STEP)
    v_hat = v / (1.0 - ADAM_B2 ** ADAM_STEP)
    delta = -ADAM_LR * (m_hat / (jnp.sqrt(v_hat) + ADAM_EPS) + ADAM_WD * w)
    return delta, m, v


def _sum_pairs(keep, recv, pos, *, name):
    _, r, c = recv.shape
    tr = _row_tile(r, 512)

    def body(pos_ref, a_ref, b_ref, o32_ref, o16_ref):
        tot = a_ref[...].astype(F32) + b_ref[...].astype(F32)
        o32_ref[...] = tot
        o16_ref[...] = tot.astype(BF16)

    out = pl.BlockSpec((1, tr, c), lambda q, i, pos: (q, i, 0))
    grid_spec = pltpu.PrefetchScalarGridSpec(
        num_scalar_prefetch=1, grid=(4, r // tr),
        in_specs=[pl.BlockSpec((1, tr, c), lambda q, i, pos: (2 * q + pos[2], i, 0)), out],
        out_specs=[out, out])
    return pl.pallas_call(
        body, name=name, grid_spec=grid_spec,
        out_shape=[jax.ShapeDtypeStruct((4, r, c), F32), jax.ShapeDtypeStruct((4, r, c), BF16)],
        compiler_params=_cparams(("parallel", "parallel")),
    )(pos, keep, recv)


def _sum_chips(psum, recv, pos, *, name):
    _, r, c = recv.shape
    tr = _row_tile(r, 512)

    def body(pos_ref, p_ref, r_ref, g_ref):
        g_ref[...] = p_ref[0] + r_ref[0].astype(F32) + r_ref[1].astype(F32) + r_ref[2].astype(F32)

    grid_spec = pltpu.PrefetchScalarGridSpec(
        num_scalar_prefetch=1, grid=(r // tr,),
        in_specs=[pl.BlockSpec((1, tr, c), lambda i, pos: (2 * pos[0] + pos[1], i, 0)),
                  pl.BlockSpec((3, tr, c), lambda i, pos: (0, i, 0))],
        out_specs=pl.BlockSpec((tr, c), lambda i, pos: (i, 0)))
    return pl.pallas_call(
        body, name=name, grid_spec=grid_spec, out_shape=jax.ShapeDtypeStruct((r, c), F32),
        compiler_params=_cparams(("parallel",)),
    )(pos, psum, recv)


def _adam(g, w, m, v, *, name):
    r, c = w.shape
    tr = _row_tile(r, 256)

    def body(g_ref, w_ref, m_ref, v_ref, d_ref, mo_ref, vo_ref):
        delta, mn, vn = _adam_math(w_ref[...], g_ref[...], m_ref[...], v_ref[...])
        d_ref[...] = delta
        mo_ref[...] = mn
        vo_ref[...] = vn

    blk = _row_spec(tr, c)
    o = jax.ShapeDtypeStruct((r, c), F32)
    return pl.pallas_call(
        body, name=name, grid=(r // tr,), in_specs=[blk] * 4, out_specs=[blk] * 3, out_shape=[o, o, o],
        compiler_params=_cparams(("parallel",)),
    )(g, w, m, v)


def _adam_replicated(chip_sums, w, m, v, *, name):
    r = w.shape[0]

    def body(s_ref, w_ref, m_ref, v_ref, g_ref, d_ref, mo_ref, vo_ref):
        g = ((s_ref[0] + s_ref[1]) + s_ref[2]) + s_ref[3]
        delta, mn, vn = _adam_math(w_ref[...], g, m_ref[...], v_ref[...])
        g_ref[...] = g
        d_ref[...] = delta
        mo_ref[...] = mn
        vo_ref[...] = vn

    o = jax.ShapeDtypeStruct((r, 1024), F32)
    full = _full_spec((r, 1024))
    return pl.pallas_call(
        body, name=name, grid=(1,),
        in_specs=[_full_spec((4, r, 1024)), full, full, full], out_specs=[full] * 4, out_shape=[o] * 4,
        compiler_params=_cparams(("arbitrary",)),
    )(chip_sums, w, m, v)


def _pair_sum_small(mine, theirs, *, name):
    def body(a_ref, b_ref, o_ref):
        o_ref[...] = a_ref[...] + b_ref[...]

    full = _full_spec(mine.shape)
    return pl.pallas_call(
        body, name=name, grid=(1,), in_specs=[full, full], out_specs=full,
        out_shape=jax.ShapeDtypeStruct(mine.shape, F32), compiler_params=_cparams(("arbitrary",)),
    )(mine, theirs)


ANY = pl.BlockSpec(memory_space=pl.ANY)
OTHER_CHIPS = ((1, 0), (0, 1), (1, 1))


def _allgather(shards, *, name):
    n = len(shards)

    def body(*refs):
        x_refs, out_refs = refs[:n], refs[n:2 * n]
        send_sems, recv_sems, local_sems = refs[2 * n:]
        x, y, c = lax.axis_index("x"), lax.axis_index("y"), lax.axis_index("c")
        me, sibling = (x, y, c), (x, y, 1 - c)
        chips = [(x ^ fx, y ^ fy) for fx, fy in OTHER_CHIPS]

        def copy(t, k, block, to, from_input=False):
            px, py, pc = block
            slab = out_refs[t].at[4 * px + 2 * py + pc]
            return pltpu.make_async_remote_copy(
                src_ref=x_refs[t] if from_input else slab, dst_ref=slab,
                send_sem=send_sems.at[7 * t + k], recv_sem=recv_sems.at[7 * t + k], device_id=to, device_id_type=MESH)

        mine = [pltpu.make_async_copy(x_refs[t], out_refs[t].at[4 * x + 2 * y + c], local_sems.at[t]) for t in range(n)]
        for cp in mine:
            cp.start()
        first = []
        for t in range(n):
            first.append(copy(t, 0, me, sibling, from_input=True))
            first += [copy(t, 1 + j, me, (*chip, c), from_input=True) for j, chip in enumerate(chips)]
        for cp in first:
            cp.start()
        passed = []
        for j, chip in enumerate(chips):
            for t in range(n):
                copy(t, 1 + j, (*chip, c), me).wait_recv()
                fwd = copy(t, 4 + j, (*chip, c), sibling)
                fwd.start()
                passed.append(fwd)
        for t in range(n):
            copy(t, 0, sibling, me).wait_recv()
            for j, chip in enumerate(chips):
                copy(t, 4 + j, (*chip, 1 - c), me).wait_recv()
        for cp in first + passed:
            cp.wait_send()
        for cp in mine:
            cp.wait()

    return pl.pallas_call(
        body, name=name, out_shape=[jax.ShapeDtypeStruct((N_DEV,) + a.shape, a.dtype) for a in shards],
        in_specs=[ANY] * n, out_specs=[ANY] * n,
        scratch_shapes=[pltpu.SemaphoreType.DMA((7 * n,)), pltpu.SemaphoreType.DMA((7 * n,)),
                        pltpu.SemaphoreType.DMA((n,))],
    )(*shards)


def _exchange_sibling(slabs, small, *, name):
    n = len(slabs)

    def body(*refs):
        g_refs, s_ref = refs[:n], refs[n]
        rg_refs, rs_ref = refs[n + 1:2 * n + 1], refs[2 * n + 1]
        send_sems, recv_sems = refs[2 * n + 2:]
        x, y, c = lax.axis_index("x"), lax.axis_index("y"), lax.axis_index("c")
        sibling = (x, y, 1 - c)
        copies = []
        for t in range(n):
            for q in range(4):
                copies.append(pltpu.make_async_remote_copy(
                    src_ref=g_refs[t].at[2 * q + (1 - c)], dst_ref=rg_refs[t].at[q],
                    send_sem=send_sems.at[4 * t + q], recv_sem=recv_sems.at[4 * t + q],
                    device_id=sibling, device_id_type=MESH))
        copies.append(pltpu.make_async_remote_copy(
            src_ref=s_ref, dst_ref=rs_ref, send_sem=send_sems.at[4 * n], recv_sem=recv_sems.at[4 * n],
            device_id=sibling, device_id_type=MESH))
        for cp in copies:
            cp.start()
        for cp in copies:
            cp.wait()

    return pl.pallas_call(
        body, name=name,
        out_shape=[jax.ShapeDtypeStruct((4,) + a.shape[1:], a.dtype) for a in slabs]
        + [jax.ShapeDtypeStruct(small.shape, small.dtype)],
        in_specs=[ANY] * (n + 1), out_specs=[ANY] * (n + 1),
        scratch_shapes=[pltpu.SemaphoreType.DMA((4 * n + 1,)), pltpu.SemaphoreType.DMA((4 * n + 1,))],
    )(*slabs, small)


def _exchange_chips(psums, small_sum, *, name):
    n = len(psums)
    rs = small_sum.shape[0]

    def body(*refs):
        p_refs, s_ref = refs[:n], refs[n]
        rp_refs, tab_ref = refs[n + 1:2 * n + 1], refs[2 * n + 1]
        send_sems, recv_sems, local_sem = refs[2 * n + 2:]
        x, y, c = lax.axis_index("x"), lax.axis_index("y"), lax.axis_index("c")
        mine = pltpu.make_async_copy(s_ref, tab_ref.at[2 * x + y], local_sem)
        mine.start()

        def table_copy(k, px, py, slot):
            return pltpu.make_async_remote_copy(
                src_ref=s_ref, dst_ref=tab_ref.at[slot], send_sem=send_sems.at[3 * n + k],
                recv_sem=recv_sems.at[3 * n + k], device_id=(px, py, c), device_id_type=MESH)

        copies = []
        for k, (fx, fy) in enumerate(OTHER_CHIPS):
            px, py = x ^ fx, y ^ fy
            for t in range(n):
                copies.append(pltpu.make_async_remote_copy(
                    src_ref=p_refs[t].at[2 * px + py], dst_ref=rp_refs[t].at[k],
                    send_sem=send_sems.at[3 * t + k], recv_sem=recv_sems.at[3 * t + k],
                    device_id=(px, py, c), device_id_type=MESH))
            copies.append(table_copy(k, px, py, 2 * x + y))
        for cp in copies:
            cp.start()
        for k, (fx, fy) in enumerate(OTHER_CHIPS):
            px, py = x ^ fx, y ^ fy
            for t in range(n):
                copies[k * (n + 1) + t].wait()
            table_copy(k, px, py, 2 * px + py).wait()
        mine.wait()

    return pl.pallas_call(
        body, name=name,
        out_shape=[jax.ShapeDtypeStruct((3,) + a.shape[1:], a.dtype) for a in psums]
        + [jax.ShapeDtypeStruct((4, rs, 1024), F32)],
        in_specs=[ANY] * (n + 1), out_specs=[ANY] * (n + 1),
        scratch_shapes=[pltpu.SemaphoreType.DMA((3 * n + 3,)), pltpu.SemaphoreType.DMA((3 * n + 3,)),
                        pltpu.SemaphoreType.DMA],
    )(*psums, small_sum)


def _to_comm(name, kind, block):
    a = block[0]
    if kind == "cols":
        a = a.T
        if name == "w_in":
            a = jnp.pad(a, ((0, IN_SHARD_PAD - IN_SHARD), (0, 0)))
    return a if kind == "f32" else a.astype(BF16)


def _from_comm(name, kind, a):
    if kind == "cols":
        if name == "w_in":
            a = a[:IN_SHARD]
        a = a.T
    return a[None]


def _assemble_weights(g):
    wt_in = g["w_in"][:, :IN_SHARD].reshape(IN_COLS, D_MODEL)
    return dict(
        wt_main=jnp.concatenate([wt_in[:2048], wt_in[O_G:], wt_in[2048:O_F]], axis=0),
        wt_f=jnp.pad(wt_in[O_F:O_G], ((0, 128 - HEADS), (0, 0))),
        w_a=g["w_branch_a"].reshape(D_MODEL, D_MODEL), w_b=g["w_branch_b"].reshape(D_MODEL, D_MODEL),
        w_out=g["w_out"].reshape(D_MODEL, D_MODEL), wt_up=g["w_up"].reshape(2 * D_FF, D_MODEL),
        conv_w=g["conv_w"].transpose(1, 0, 2).reshape(3, 2 * D_FF),
        w_down=g["w_down"].reshape(D_FF, D_MODEL), wt_ple=g["w_ple"].reshape(D_MODEL, PLE_DIM),
        w_pg=g["w_ple_gate"].reshape(D_MODEL, D_MODEL))


def _grad_slabs(gr):
    gm = gr["wt_main"]
    gt_in = jnp.concatenate([gm[:2048], gm[4096:], gr["wt_f"][:HEADS], gm[2048:4096]], axis=0)
    gt_in = jnp.pad(gt_in.reshape(N_DEV, IN_SHARD, D_MODEL), ((0, 0), (0, IN_SHARD_PAD - IN_SHARD), (0, 0)))
    cw = gr["conv_w"]
    out = dict(
        w_in=gt_in, w_branch_a=gr["w_a"].reshape(N_DEV, -1, D_MODEL), w_branch_b=gr["w_b"].reshape(N_DEV, -1, D_MODEL),
        w_out=gr["w_out"].reshape(N_DEV, -1, D_MODEL), w_up=gr["wt_up"].reshape(N_DEV, -1, D_MODEL),
        conv_w=cw.reshape(3, N_DEV, -1).transpose(1, 0, 2), w_down=gr["w_down"].reshape(N_DEV, -1, D_MODEL),
        w_ple=gr["wt_ple"].reshape(N_DEV, -1, PLE_DIM), w_ple_gate=gr["w_pg"].reshape(N_DEV, -1, D_MODEL))
    return {k: v.astype(BF16) for k, v in out.items()}


def _rows(a, rows):
    flat = a.reshape(-1)
    return jnp.pad(flat, (0, rows * 1024 - flat.shape[0])).reshape(rows, 1024)


def _pack_small(parts):
    return jnp.concatenate([_rows(parts[n].astype(F32), r) for n, r in SMALL], axis=0)


def _small(packed, name, shape):
    off, r = SMALL_OFF[name]
    n = math.prod(shape)
    return packed[off:off + r].reshape(-1)[:n].reshape(shape)


def _local_step(x, p, target, w, sm):
    s = x.shape[0]
    mm = _matmul
    wt_main = w["wt_main"]
    wt_up = w["wt_up"]
    conv_w, conv_b = w["conv_w"], sm["conv_b"]
    bs_t = jnp.pad(sm["gmlp_b_s"].T, ((0, 0), (0, 128 - GROUPS)))
    b_f = jnp.pad(sm["b_f"], ((0, 0), (0, 128 - HEADS)))
    big = dict(tm=1024, tn=1024, tk=1024)
    whole_s = dict(tn=1024, tk=s)

    h = _rmsnorm_fwd(x, sm["norm_mix_g"], name="norm_mix")
    zuvg = mm(h, wt_main, mode="nt", out_dtype=F32, name="in_uvg", n=4096, **big)
    qkv = mm(h, wt_main, mode="nt", out_dtype=BF16, name="in_qkv", n=3072, b_off=4, **big)
    f_logit = mm(h, w["wt_f"], mode="nt", out_dtype=F32, name="in_f", tm=1024, tk=1024)
    a = _gmlp_fwd(zuvg, sm["gmlp_ln_g"], sm["gmlp_ln_b"], sm["gmlp_w_s"], bs_t, name="gmlp_fwd")
    cqe = _forget_cumsum(f_logit, b_f, name="forget_cumsum")
    qa, ka, vt = _attn_prep(qkv, cqe, name="attn_prep")
    b, lse3 = _attn_fwd(qa, ka, vt, name="attn_fwd")
    ya = mm(a, w["w_a"], mode="nn", out_dtype=F32, name="branch_a", **big)
    yb = mm(b, w["w_b"], mode="nn", out_dtype=F32, name="branch_b", **big)
    merged = _merge_fwd(ya, yb, zuvg, name="merge_fwd")
    x1 = mm(merged, w["w_out"], mode="nn", out_dtype=F32, name="out_proj", add=x, **big)
    h2 = _rmsnorm_fwd(x1, sm["norm_ffn_g"], name="norm_ffn")
    up = mm(h2, wt_up, mode="nt", out_dtype=F32, name="up", tm=1024, tn=512, tk=1024)
    act = _convglu_fwd(up, conv_w, conv_b, name="convglu_fwd")
    x2 = mm(act, w["w_down"], mode="nn", out_dtype=F32, name="down", tm=1024, tn=1024, tk=1408, add=x1)
    h3 = _rmsnorm_fwd(x2, sm["norm_ple_g"], name="norm_ple")
    ple = mm(p, w["wt_ple"], mode="nt", out_dtype=F32, name="ple", tm=1024, tn=1024, tk=256)
    gp = mm(h3, w["w_pg"], mode="nn", out_dtype=F32, name="ple_gate", **big)
    x3 = _ple_fwd(x2, ple, gp, name="ple_fwd")

    loss, dx3, d_norm_final = _final_loss_bwd(x3, target, sm["norm_final_g"], name="loss_bwd")
    dple, dgp = _ple_bwd(dx3, ple, gp, name="ple_bwd")
    g_wt_ple = mm(dple, p, mode="tn", out_dtype=BF16, name="d_w_ple", tm=512, tn=256, tk=s)
    g_w_pg = mm(h3, dgp, mode="tn", out_dtype=BF16, name="d_w_pg", tm=256, **whole_s)
    dh3 = mm(dgp, w["w_pg"], mode="nt", out_dtype=F32, name="d_h3", **big)
    dx2, dx2b, d_norm_ple = _rmsnorm_bwd(dx3, dh3, x2, sm["norm_ple_g"], name="norm_ple_bwd")
    g_w_down = mm(act, dx2b, mode="tn", out_dtype=BF16, name="d_w_down", tm=256, **whole_s)
    dact = mm(dx2b, w["w_down"], mode="nt", out_dtype=BF16, name="d_act", tm=1024, tn=1408, tk=1024)
    dup_a, dup_g, dcw_a, dcw_g, dcb_a, dcb_g = _convglu_bwd(dact, up, conv_w, conv_b, name="convglu_bwd")
    g_wt_up = jnp.concatenate(
        [mm(dup_a, h2, mode="tn", out_dtype=BF16, name="d_w_up_a", tm=256, **whole_s),
         mm(dup_g, h2, mode="tn", out_dtype=BF16, name="d_w_up_g", tm=256, **whole_s)], axis=0)
    dh2 = mm(dup_a, wt_up, mode="nn", out_dtype=F32, name="d_h2_a", tm=1024, tn=1024, tk=1408)
    dh2 = mm(dup_g, wt_up, mode="nn", out_dtype=F32, name="d_h2_g", tm=1024, tn=1024, tk=1408, b_off=2, add=dh2)
    dx1, dx1b, d_norm_ffn = _rmsnorm_bwd(dx2, dh2, x1, sm["norm_ffn_g"], name="norm_ffn_bwd")
    g_w_out = mm(merged, dx1b, mode="tn", out_dtype=BF16, name="d_w_out", tm=256, **whole_s)
    dmerged = mm(dx1b, w["w_out"], mode="nt", out_dtype=F32, name="d_merged", **big)
    dya, dyb, dga, dgb = _merge_bwd(dmerged, ya, yb, zuvg, name="merge_bwd")
    g_w_a = mm(a, dya, mode="tn", out_dtype=BF16, name="d_w_a", tm=256, **whole_s)
    g_w_b = mm(b, dyb, mode="tn", out_dtype=BF16, name="d_w_b", tm=256, **whole_s)
    da = mm(dya, w["w_a"], mode="nt", out_dtype=BF16, name="d_a", **big)
    db = mm(dyb, w["w_b"], mode="nt", out_dtype=BF16, name="d_b", **big)
    dzu, dzv, d_w_s, d_bs_t, d_ln_g, d_ln_b = _gmlp_bwd(
        da, zuvg, sm["gmlp_ln_g"], sm["gmlp_ln_b"], sm["gmlp_w_s"], bs_t, name="gmlp_bwd")
    delta3 = _attn_delta(db, b, name="attn_delta")
    dq, dk, dv, aux, dcq3 = _attn_bwd(qa, ka, qkv, db, lse3, delta3, name="attn_bwd")
    dcq16 = jnp.pad(dcq3[:, :2, :].reshape(HEADS, s).T, ((0, 0), (0, 128 - HEADS)))
    aux = aux.reshape(s, HEADS // 2, 128)
    dck16 = -jnp.stack([aux[:, :, HEAD_DIM + 3], aux[:, :, 3]], axis=-1).reshape(s, HEADS)
    dck16 = jnp.pad(dck16, ((0, 0), (0, 128 - HEADS)))
    dzf, d_b_f = _forget_bwd(dcq16, dck16, f_logit, b_f, name="forget_bwd")
    dz = jnp.concatenate([dzu, dzv, dga, dgb, dq, dk, dv], axis=1)
    g_wt_main = mm(dz, h, mode="tn", out_dtype=BF16, name="d_w_main", tm=512, **whole_s)
    g_wt_f = mm(dzf, h, mode="tn", out_dtype=BF16, name="d_w_f", **whole_s)
    dh = mm(dz, wt_main, mode="nn", out_dtype=F32, name="d_h_main", **big)
    dh = mm(dzf, w["wt_f"], mode="nn", out_dtype=F32, name="d_h_f", tm=1024, tn=1024, add=dh)
    dx0, _, d_norm_mix = _rmsnorm_bwd(dx1, dh, x, sm["norm_mix_g"], name="norm_mix_bwd")

    grads = dict(wt_main=g_wt_main, wt_f=g_wt_f, w_a=g_w_a, w_b=g_w_b, w_out=g_w_out, wt_up=g_wt_up,
                 conv_w=jnp.concatenate([dcw_a, dcw_g], axis=1), w_down=g_w_down, wt_ple=g_wt_ple, w_pg=g_w_pg)
    small = dict(norm_mix_g=d_norm_mix, b_f=d_b_f[:, :HEADS], gmlp_ln_g=d_ln_g, gmlp_ln_b=d_ln_b, gmlp_w_s=d_w_s,
                 gmlp_b_s=d_bs_t[:, :GROUPS].T, norm_ffn_g=d_norm_ffn,
                 conv_b=jnp.concatenate([dcb_a, dcb_g], axis=1), norm_ple_g=d_norm_ple, norm_final_g=d_norm_final)
    return loss, dx0, grads, small


def kernel(x, p, norm_mix_g, w_in, b_f, gmlp_ln_g, gmlp_ln_b, gmlp_w_s, gmlp_b_s, w_branch_a, w_branch_b, w_out, norm_ffn_g, w_up, conv_w, conv_b, w_down, norm_ple_g, w_ple, w_ple_gate, norm_final_g, loss_target, m_norm_mix_g, m_w_in, m_b_f, m_gmlp_ln_g, m_gmlp_ln_b, m_gmlp_w_s, m_gmlp_b_s, m_w_branch_a, m_w_branch_b, m_w_out, m_norm_ffn_g, m_w_up, m_conv_w, m_conv_b, m_w_down, m_norm_ple_g, m_w_ple, m_w_ple_gate, m_norm_final_g, v_norm_mix_g, v_w_in, v_b_f, v_gmlp_ln_g, v_gmlp_ln_b, v_gmlp_w_s, v_gmlp_b_s, v_w_branch_a, v_w_branch_b, v_w_out, v_norm_ffn_g, v_w_up, v_conv_w, v_conv_b, v_w_down, v_norm_ple_g, v_w_ple, v_w_ple_gate, v_norm_final_g):
    given = dict(locals())
    weights = {n: given[n] for n in WEIGHT_ORDER}
    mom_m = {n: given["m_" + n] for n in WEIGHT_ORDER}
    mom_v = {n: given["v_" + n] for n in WEIGHT_ORDER}
    pos = jnp.stack([lax.axis_index("x"), lax.axis_index("y"), lax.axis_index("c")]).astype(I32)
    names = [n for n, _ in SHARDED]
    kinds = dict(SHARDED)

    gathered = _allgather([_to_comm(n, kinds[n], weights[n]) for n in names], name="allgather_weights")
    full = _assemble_weights(dict(zip(names, gathered)))

    sm = dict(norm_mix_g=norm_mix_g, b_f=b_f, gmlp_ln_g=gmlp_ln_g, gmlp_ln_b=gmlp_ln_b, gmlp_w_s=gmlp_w_s[0],
              gmlp_b_s=gmlp_b_s[0], norm_ffn_g=norm_ffn_g, conv_b=conv_b, norm_ple_g=norm_ple_g,
              norm_final_g=norm_final_g.reshape(1, D_MODEL))
    loss_part, dx0, grads, small = _local_step(x[0], p[0, 0], loss_target[0], full, sm)

    slabs = _grad_slabs(grads)
    small_g = _pack_small(small)
    *from_sib, small_sib = _exchange_sibling([slabs[n] for n in names], small_g, name="exchange_sibling")
    sums = [_sum_pairs(slabs[n], r, pos, name="sum_sibling_" + n) for n, r in zip(names, from_sib)]
    small_chip = _pair_sum_small(small_g, small_sib, name="sum_sibling_small")
    *from_chips, small_tab = _exchange_chips([s16 for _, s16 in sums], small_chip, name="exchange_chips")

    grad, delta, new_m, new_v = {}, {}, {}, {}
    for n, (s32, _), r in zip(names, sums, from_chips):
        g = _from_comm(n, kinds[n], _sum_chips(s32, r, pos, name="sum_chips_" + n))
        d, mn, vn = _adam(g[0], weights[n][0], mom_m[n][0], mom_v[n][0], name="adam_" + n)
        grad[n], delta[n], new_m[n], new_v[n] = g, d[None], mn[None], vn[None]
    replicated = [n for n, _ in SMALL]
    rep = lambda src: _pack_small({n: src[n] for n in replicated})
    packed = _adam_replicated(small_tab, rep(weights), rep(mom_m), rep(mom_v), name="adam_replicated")
    for out, pk in zip((grad, delta, new_m, new_v), packed):
        for n in replicated:
            out[n] = _small(pk, n, weights[n].shape)

    loss = lax.psum(loss_part[0, 0], ("x", "y", "c"))
    return (loss, dx0[None], *[grad[n] for n in WEIGHT_ORDER], *[delta[n] for n in WEIGHT_ORDER],
            *[new_m[n] for n in WEIGHT_ORDER], *[new_v[n] for n in WEIGHT_ORDER])
```

```python
import functools
import math

import jax
import jax.numpy as jnp
from jax import lax
from jax.experimental import pallas as pl
from jax.experimental.pallas import tpu as pltpu

F32 = jnp.float32
BF16 = jnp.bfloat16
I32 = jnp.int32

D_MODEL = 1024
GROUPS = 8
GDIM = 128
GBLOCK = 128
CHUNK = 64
HEADS = 16
HEAD_DIM = 64
D_FF = 2816
PLE_DIM = 256
EPS = 1e-6
N_DEV = 8
ATT_SCALE = HEAD_DIM ** -0.5
NEG = -1e30

ADAM_LR = 0.001
ADAM_B1 = 0.9
ADAM_B2 = 0.999
ADAM_EPS = 1e-08
ADAM_WD = 0.01
ADAM_STEP = 10

V7X_VMEM_LIMIT = 48 * 1024 * 1024
MESH = pl.DeviceIdType.MESH

O_F = 2 * 1024 + 3 * 1024
O_G = O_F + HEADS
IN_COLS = O_G + 2 * D_MODEL
MAIN_COLS = IN_COLS - HEADS
IN_SHARD = IN_COLS // N_DEV
IN_SHARD_PAD = 912

SHARDED = (("w_in", "cols"), ("w_branch_a", "rows"), ("w_branch_b", "rows"), ("w_out", "rows"), ("w_up", "cols"),
           ("conv_w", "f32"), ("w_down", "rows"), ("w_ple", "cols"), ("w_ple_gate", "rows"))

SMALL = (("norm_mix_g", 8), ("b_f", 8), ("gmlp_ln_g", 8), ("gmlp_ln_b", 8), ("gmlp_w_s", 128), ("gmlp_b_s", 8),
         ("norm_ffn_g", 8), ("conv_b", 8), ("norm_ple_g", 8), ("norm_final_g", 8))
SMALL_OFF = {}
_o = 0
for _n, _r in SMALL:
    SMALL_OFF[_n] = (_o, _r)
    _o += _r
SMALL_ROWS = _o

WEIGHT_ORDER = ("norm_mix_g", "w_in", "b_f", "gmlp_ln_g", "gmlp_ln_b", "gmlp_w_s", "gmlp_b_s", "w_branch_a",
                "w_branch_b", "w_out", "norm_ffn_g", "w_up", "conv_w", "conv_b", "w_down", "norm_ple_g", "w_ple",
                "w_ple_gate", "norm_final_g")


def _cparams(sem):
    return pltpu.CompilerParams(dimension_semantics=sem, vmem_limit_bytes=V7X_VMEM_LIMIT)


def _gelu(x):
    c = math.sqrt(2.0 / math.pi)
    return 0.5 * x * (1.0 + jnp.tanh(c * (x + 0.044715 * x * x * x)))


def _gelu_and_grad(x):
    c = math.sqrt(2.0 / math.pi)
    t = jnp.tanh(c * (x + 0.044715 * x * x * x))
    g = 0.5 * x * (1.0 + t)
    dg = 0.5 * (1.0 + t) + 0.5 * x * (1.0 - t * t) * (c * (1.0 + 3.0 * 0.044715 * x * x))
    return g, dg


def _sigmoid(x):
    return 1.0 / (1.0 + jnp.exp(-x))


def _dot(a, b, dims):
    return lax.dot_general(a, b, (dims, ((), ())), preferred_element_type=F32)


NN = ((1,), (0,))
NT = ((1,), (1,))
TN = ((0,), (0,))


def _row_tile(rows, most):
    best = None
    for t in range(16, min(rows, most) + 1, 16):
        if rows % t == 0:
            best = t
    return best if best is not None else rows


def _matmul(a, b, *, mode, out_dtype, name, tm=512, tn=512, tk=512, add=None, n=None, b_off=0):
    if mode == "tn":
        kdim, m = a.shape
    else:
        m, kdim = a.shape
    if n is None:
        n = b.shape[0] if mode == "nt" else b.shape[1]
    tm, tn, tk = min(tm, m), min(tn, n), min(tk, kdim)
    assert m % tm == 0 and n % tn == 0 and kdim % tk == 0, (name, m, n, kdim, tm, tn, tk)
    nk = kdim // tk
    dims = {"nn": NN, "nt": NT, "tn": TN}[mode]

    def finish(r, add_ref, o_ref):
        if add_ref is not None:
            r = add_ref[...].astype(F32) + r
        o_ref[...] = r.astype(out_dtype)

    def body(*refs):
        refs = list(refs)
        a_ref, b_ref = refs[:2]
        add_ref = refs[2] if add is not None else None
        o_ref = refs[3] if add is not None else refs[2]
        part = _dot(a_ref[...].astype(BF16), b_ref[...].astype(BF16), dims)
        if nk == 1:
            finish(part, add_ref, o_ref)
            return
        acc_ref = refs[-1]
        k = pl.program_id(2)

        @pl.when(k == 0)
        def _():
            acc_ref[...] = part

        @pl.when((k > 0) & (k < nk - 1))
        def _():
            acc_ref[...] += part

        @pl.when(k == nk - 1)
        def _():
            finish(acc_ref[...] + part, add_ref, o_ref)

    a_spec = pl.BlockSpec((tk, tm), lambda i, j, k: (k, i)) if mode == "tn" else pl.BlockSpec((tm, tk), lambda i, j, k: (i, k))
    if mode == "nt":
        b_spec = pl.BlockSpec((tn, tk), lambda i, j, k: (j + b_off, k))
    else:
        b_spec = pl.BlockSpec((tk, tn), lambda i, j, k: (k + b_off, j))
    o_spec = pl.BlockSpec((tm, tn), lambda i, j, k: (i, j))
    in_specs = [a_spec, b_spec] + ([o_spec] if add is not None else [])
    args = (a, b) + ((add,) if add is not None else ())
    return pl.pallas_call(
        body, name=name, grid=(m // tm, n // tn, nk),
        in_specs=in_specs, out_specs=o_spec,
        out_shape=jax.ShapeDtypeStruct((m, n), out_dtype),
        scratch_shapes=[pltpu.VMEM((tm, tn), F32)] if nk > 1 else [],
        compiler_params=_cparams(("parallel", "parallel", "arbitrary")),
    )(*args)


def _row_spec(tr, width, col_block=0):
    return pl.BlockSpec((tr, width), lambda i: (i, col_block))


def _full_spec(shape):
    return pl.BlockSpec(shape, lambda i: tuple(0 for _ in shape))


def _rmsnorm_fwd(x, g, *, name, tr=256):
    s, d = x.shape

    def body(x_ref, g_ref, o_ref):
        xv = x_ref[...]
        r = lax.rsqrt(jnp.mean(xv * xv, axis=-1, keepdims=True) + EPS)
        o_ref[...] = ((xv * r) * g_ref[...]).astype(BF16)

    return pl.pallas_call(
        body, name=name, grid=(s // tr,),
        in_specs=[_row_spec(tr, d), _full_spec((1, d))], out_specs=_row_spec(tr, d),
        out_shape=jax.ShapeDtypeStruct((s, d), BF16), compiler_params=_cparams(("parallel",)),
    )(x, g)


def _rmsnorm_bwd(dres, dh, x, g, *, name, tr=256):
    s, d = x.shape

    def body(dres_ref, dh_ref, x_ref, g_ref, dx_ref, dxb_ref, dg_ref):
        i = pl.program_id(0)
        xv = x_ref[...]
        r = lax.rsqrt(jnp.mean(xv * xv, axis=-1, keepdims=True) + EPS)
        xhat = xv * r
        dhv = dh_ref[...].astype(F32)
        dxhat = dhv * g_ref[...]
        dx = dres_ref[...] + r * (dxhat - xhat * jnp.mean(dxhat * xhat, axis=-1, keepdims=True))
        dx_ref[...] = dx
        dxb_ref[...] = dx.astype(BF16)
        dgp = jnp.sum(dhv * xhat, axis=0, keepdims=True)

        @pl.when(i == 0)
        def _():
            dg_ref[...] = dgp

        @pl.when(i > 0)
        def _():
            dg_ref[...] += dgp

    return pl.pallas_call(
        body, name=name, grid=(s // tr,),
        in_specs=[_row_spec(tr, d), _row_spec(tr, d), _row_spec(tr, d), _full_spec((1, d))],
        out_specs=[_row_spec(tr, d), _row_spec(tr, d), _full_spec((1, d))],
        out_shape=[jax.ShapeDtypeStruct((s, d), F32), jax.ShapeDtypeStruct((s, d), BF16),
                   jax.ShapeDtypeStruct((1, d), F32)],
        compiler_params=_cparams(("arbitrary",)),
    )(dres, dh, x, g)


def _final_loss_bwd(x3, target, g, *, name, tr=256):
    s, d = x3.shape

    def body(x_ref, t_ref, g_ref, loss_ref, dx_ref, dg_ref):
        i = pl.program_id(0)
        xv = x_ref[...]
        r = lax.rsqrt(jnp.mean(xv * xv, axis=-1, keepdims=True) + EPS)
        xhat = xv * r
        diff = xhat * g_ref[...] - t_ref[...]
        lp = jnp.zeros((1, 128), F32) + (0.5 / d) * jnp.sum(diff * diff)
        dy = diff * (1.0 / d)
        dxhat = dy * g_ref[...]
        dx_ref[...] = r * (dxhat - xhat * jnp.mean(dxhat * xhat, axis=-1, keepdims=True))
        dgp = jnp.sum(dy * xhat, axis=0, keepdims=True)

        @pl.when(i == 0)
        def _():
            dg_ref[...] = dgp
            loss_ref[...] = lp

        @pl.when(i > 0)
        def _():
            dg_ref[...] += dgp
            loss_ref[...] += lp

    return pl.pallas_call(
        body, name=name, grid=(s // tr,),
        in_specs=[_row_spec(tr, d), _row_spec(tr, d), _full_spec((1, d))],
        out_specs=[_full_spec((1, 128)), _row_spec(tr, d), _full_spec((1, d))],
        out_shape=[jax.ShapeDtypeStruct((1, 128), F32), jax.ShapeDtypeStruct((s, d), F32),
                   jax.ShapeDtypeStruct((1, d), F32)],
        compiler_params=_cparams(("arbitrary",)),
    )(x3, target, g)


def _merge_fwd(ya, yb, zuvg, *, name, tr=256):
    s, d = ya.shape

    def body(ya_ref, yb_ref, ga_ref, gb_ref, o_ref):
        o_ref[...] = (_sigmoid(ga_ref[...]) * ya_ref[...] + _sigmoid(gb_ref[...]) * yb_ref[...]).astype(BF16)

    return pl.pallas_call(
        body, name=name, grid=(s // tr,),
        in_specs=[_row_spec(tr, d), _row_spec(tr, d), _row_spec(tr, d, 2), _row_spec(tr, d, 3)],
        out_specs=_row_spec(tr, d),
        out_shape=jax.ShapeDtypeStruct((s, d), BF16), compiler_params=_cparams(("parallel",)),
    )(ya, yb, zuvg, zuvg)


def _merge_bwd(dm, ya, yb, zuvg, *, name, tr=256):
    s, d = ya.shape

    def body(dm_ref, ya_ref, yb_ref, ga_ref, gb_ref, dya_ref, dyb_ref, dga_ref, dgb_ref):
        dmv = dm_ref[...]
        sa = _sigmoid(ga_ref[...])
        sb = _sigmoid(gb_ref[...])
        dya_ref[...] = (dmv * sa).astype(BF16)
        dyb_ref[...] = (dmv * sb).astype(BF16)
        dga_ref[...] = (dmv * ya_ref[...] * (sa * (1.0 - sa))).astype(BF16)
        dgb_ref[...] = (dmv * yb_ref[...] * (sb * (1.0 - sb))).astype(BF16)

    o = jax.ShapeDtypeStruct((s, d), BF16)
    return pl.pallas_call(
        body, name=name, grid=(s // tr,),
        in_specs=[_row_spec(tr, d)] * 3 + [_row_spec(tr, d, 2), _row_spec(tr, d, 3)], out_specs=[_row_spec(tr, d)] * 4,
        out_shape=[o, o, o, o], compiler_params=_cparams(("parallel",)),
    )(dm, ya, yb, zuvg, zuvg)


def _ple_fwd(x2, ple, gp, *, name, tr=256):
    s, d = x2.shape

    def body(x_ref, ple_ref, gp_ref, o_ref):
        o_ref[...] = x_ref[...] + ple_ref[...] * _sigmoid(gp_ref[...])

    return pl.pallas_call(
        body, name=name, grid=(s // tr,),
        in_specs=[_row_spec(tr, d)] * 3, out_specs=_row_spec(tr, d),
        out_shape=jax.ShapeDtypeStruct((s, d), F32), compiler_params=_cparams(("parallel",)),
    )(x2, ple, gp)


def _ple_bwd(dx3, ple, gp, *, name, tr=256):
    s, d = dx3.shape

    def body(dx_ref, ple_ref, gp_ref, dple_ref, dgp_ref):
        sg = _sigmoid(gp_ref[...])
        dxv = dx_ref[...]
        dple_ref[...] = (dxv * sg).astype(BF16)
        dgp_ref[...] = (dxv * ple_ref[...] * (sg * (1.0 - sg))).astype(BF16)

    o = jax.ShapeDtypeStruct((s, d), BF16)
    return pl.pallas_call(
        body, name=name, grid=(s // tr,),
        in_specs=[_row_spec(tr, d)] * 3, out_specs=[_row_spec(tr, d)] * 2,
        out_shape=[o, o], compiler_params=_cparams(("parallel",)),
    )(dx3, ple, gp)


def _masked_ws(ws_ref, g):
    row = lax.broadcasted_iota(I32, (GBLOCK, GBLOCK), 0)
    col = lax.broadcasted_iota(I32, (GBLOCK, GBLOCK), 1)
    keep = (col // CHUNK) <= (row // CHUNK)
    return jnp.where(keep, ws_ref[g], 0.0), keep


def _layernorm_parts(zv):
    mu = jnp.mean(zv, axis=-1, keepdims=True)
    xc = zv - mu
    rs = lax.rsqrt(jnp.mean(xc * xc, axis=-1, keepdims=True) + EPS)
    return xc * rs, rs


def _gmlp_fwd(zuvg, ln_g, ln_b, w_s, bs_t, *, name):
    s, w = zuvg.shape[0], GROUPS * GDIM

    def body(zu_ref, zv_ref, lng_ref, lnb_ref, ws_ref, bs_ref, a_ref):
        zu = _gelu(zu_ref[...])
        zv = _gelu(zv_ref[...])
        xhat, _ = _layernorm_parts(zv)
        vln = (xhat * lng_ref[...] + lnb_ref[...]).astype(BF16)
        for g in range(GROUPS):
            wm, _ = _masked_ws(ws_ref, g)
            mixed = _dot(wm.astype(BF16), vln[:, g * GDIM:(g + 1) * GDIM], NN) + bs_ref[:, g:g + 1]
            a_ref[:, g * GDIM:(g + 1) * GDIM] = (zu[:, g * GDIM:(g + 1) * GDIM] * mixed).astype(BF16)

    return pl.pallas_call(
        body, name=name, grid=(s // GBLOCK,),
        in_specs=[_row_spec(GBLOCK, w, 0), _row_spec(GBLOCK, w, 1), _full_spec((1, w)), _full_spec((1, w)),
                  _full_spec((GROUPS, GBLOCK, GBLOCK)), _full_spec((GBLOCK, 128))],
        out_specs=_row_spec(GBLOCK, w),
        out_shape=jax.ShapeDtypeStruct((s, w), BF16), compiler_params=_cparams(("parallel",)),
    )(zuvg, zuvg, ln_g, ln_b, w_s, bs_t)


def _gmlp_bwd(da, zuvg, ln_g, ln_b, w_s, bs_t, *, name):
    s, w = zuvg.shape[0], GROUPS * GDIM

    def body(da_ref, zu_ref, zv_ref, lng_ref, lnb_ref, ws_ref, bs_ref,
             dzu_ref, dzv_ref, dws_ref, dbs_ref, dlng_ref, dlnb_ref, dvln_ref):
        i = pl.program_id(0)
        zu, dzu_g = _gelu_and_grad(zu_ref[...])
        zv, dzv_g = _gelu_and_grad(zv_ref[...])
        xhat, rs = _layernorm_parts(zv)
        vln = (xhat * lng_ref[...] + lnb_ref[...]).astype(BF16)
        dav = da_ref[...].astype(F32)
        lane = lax.broadcasted_iota(I32, (GBLOCK, 128), 1)
        dbs = jnp.zeros((GBLOCK, 128), F32)

        @pl.when(i == 0)
        def _():
            dws_ref[...] = jnp.zeros_like(dws_ref)

        for g in range(GROUPS):
            sl = slice(g * GDIM, (g + 1) * GDIM)
            wm, keep = _masked_ws(ws_ref, g)
            wmb = wm.astype(BF16)
            vg = vln[:, sl]
            mixed = _dot(wmb, vg, NN) + bs_ref[:, g:g + 1]
            dag = dav[:, sl]
            dzu_ref[:, sl] = (dag * mixed * dzu_g[:, sl]).astype(BF16)
            dmix = dag * zu[:, sl]
            dmb = dmix.astype(BF16)
            dws_ref[g] += jnp.where(keep, _dot(dmb, vg, NT), 0.0)
            dbs = jnp.where(lane == g, jnp.sum(dmix, axis=1, keepdims=True), dbs)
            dvln_ref[:, sl] = _dot(wmb, dmb, TN)
        dvln = dvln_ref[...]
        dxhat = dvln * lng_ref[...]
        dzv = rs * (dxhat - jnp.mean(dxhat, axis=-1, keepdims=True)
                    - xhat * jnp.mean(dxhat * xhat, axis=-1, keepdims=True))
        dzv_ref[...] = (dzv * dzv_g).astype(BF16)
        dlng = jnp.sum(dvln * xhat, axis=0, keepdims=True)
        dlnb = jnp.sum(dvln, axis=0, keepdims=True)

        @pl.when(i == 0)
        def _():
            dbs_ref[...] = dbs
            dlng_ref[...] = dlng
            dlnb_ref[...] = dlnb

        @pl.when(i > 0)
        def _():
            dbs_ref[...] += dbs
            dlng_ref[...] += dlng
            dlnb_ref[...] += dlnb

    return pl.pallas_call(
        body, name=name, grid=(s // GBLOCK,),
        in_specs=[_row_spec(GBLOCK, w), _row_spec(GBLOCK, w, 0), _row_spec(GBLOCK, w, 1), _full_spec((1, w)),
                  _full_spec((1, w)), _full_spec((GROUPS, GBLOCK, GBLOCK)), _full_spec((GBLOCK, 128))],
        out_specs=[_row_spec(GBLOCK, w), _row_spec(GBLOCK, w), _full_spec((GROUPS, GBLOCK, GBLOCK)),
                   _full_spec((GBLOCK, 128)), _full_spec((1, w)), _full_spec((1, w))],
        out_shape=[jax.ShapeDtypeStruct((s, w), BF16), jax.ShapeDtypeStruct((s, w), BF16),
                   jax.ShapeDtypeStruct((GROUPS, GBLOCK, GBLOCK), F32), jax.ShapeDtypeStruct((GBLOCK, 128), F32),
                   jax.ShapeDtypeStruct((1, w), F32), jax.ShapeDtypeStruct((1, w), F32)],
        scratch_shapes=[pltpu.VMEM((GBLOCK, w), F32)],
        compiler_params=_cparams(("arbitrary",)),
    )(da, zuvg, zuvg, ln_g, ln_b, w_s, bs_t)


def _shift_down(u, k):
    row = lax.broadcasted_iota(I32, u.shape, 0)
    return jnp.where(row >= k, pltpu.roll(u, k, 0), 0.0)


def _shift_up(u, k):
    s = u.shape[0]
    row = lax.broadcasted_iota(I32, u.shape, 0)
    return jnp.where(row < s - k, pltpu.roll(u, s - k, 0), 0.0)


def _conv(u, w_ref, b_ref):
    return b_ref[...] + w_ref[0:1, :] * _shift_down(u, 2) + w_ref[1:2, :] * _shift_down(u, 1) + w_ref[2:3, :] * u


def _conv_specs(s, f, tc):
    nc = f // tc
    half = lambda rows: [pl.BlockSpec((rows, tc), lambda j: (0, j)), pl.BlockSpec((rows, tc), lambda j: (0, nc + j))]
    return half(s), half(3), half(1)


def _convglu_fwd(up, conv_w, conv_b, *, name, tc=256):
    s, f = up.shape[0], up.shape[1] // 2
    up_specs, w_specs, b_specs = _conv_specs(s, f, tc)

    def body(ua_ref, ug_ref, wa_ref, wg_ref, ba_ref, bg_ref, o_ref):
        ca = _conv(ua_ref[...], wa_ref, ba_ref)
        cg = _conv(ug_ref[...], wg_ref, bg_ref)
        o_ref[...] = (_gelu(ca) * cg).astype(BF16)

    return pl.pallas_call(
        body, name=name, grid=(f // tc,),
        in_specs=up_specs + w_specs + b_specs, out_specs=up_specs[0],
        out_shape=jax.ShapeDtypeStruct((s, f), BF16), compiler_params=_cparams(("parallel",)),
    )(up, up, conv_w, conv_w, conv_b, conv_b)


def _convglu_bwd(dact, up, conv_w, conv_b, *, name, tc=256):
    s, f = up.shape[0], up.shape[1] // 2
    up_specs, w_specs, b_specs = _conv_specs(s, f, tc)

    def half(dc, u, w_ref, du_ref, dw_ref, db_ref):
        db_ref[...] = jnp.sum(dc, axis=0, keepdims=True)
        dw_ref[0:1, :] = jnp.sum(dc * _shift_down(u, 2), axis=0, keepdims=True)
        dw_ref[1:2, :] = jnp.sum(dc * _shift_down(u, 1), axis=0, keepdims=True)
        dw_ref[2:3, :] = jnp.sum(dc * u, axis=0, keepdims=True)
        du = w_ref[2:3, :] * dc + w_ref[1:2, :] * _shift_up(dc, 1) + w_ref[0:1, :] * _shift_up(dc, 2)
        du_ref[...] = du.astype(BF16)

    def body(d_ref, ua_ref, ug_ref, wa_ref, wg_ref, ba_ref, bg_ref,
             dua_ref, dug_ref, dwa_ref, dwg_ref, dba_ref, dbg_ref):
        ua = ua_ref[...]
        ug = ug_ref[...]
        ca = _conv(ua, wa_ref, ba_ref)
        cg = _conv(ug, wg_ref, bg_ref)
        ga, dga = _gelu_and_grad(ca)
        dv = d_ref[...].astype(F32)
        half(dv * cg * dga, ua, wa_ref, dua_ref, dwa_ref, dba_ref)
        half(dv * ga, ug, wg_ref, dug_ref, dwg_ref, dbg_ref)

    col, w3, b1 = up_specs[0], w_specs[0], b_specs[0]
    return pl.pallas_call(
        body, name=name, grid=(f // tc,),
        in_specs=[col] + up_specs + w_specs + b_specs, out_specs=[col, col, w3, w3, b1, b1],
        out_shape=[jax.ShapeDtypeStruct((s, f), BF16), jax.ShapeDtypeStruct((s, f), BF16),
                   jax.ShapeDtypeStruct((3, f), F32), jax.ShapeDtypeStruct((3, f), F32),
                   jax.ShapeDtypeStruct((1, f), F32), jax.ShapeDtypeStruct((1, f), F32)],
        compiler_params=_cparams(("parallel",)),
    )(dact, up, up, conv_w, conv_w, conv_b, conv_b)


def _tri_dot(tri, x):
    b0 = x.astype(BF16)
    r1 = x - b0.astype(F32)
    b1 = r1.astype(BF16)
    b2 = (r1 - b1.astype(F32)).astype(BF16)
    return _dot(tri, b0, NN) + _dot(tri, b1, NN) + _dot(tri, b2, NN)


def _log_sigmoid(x):
    return jnp.minimum(x, 0.0) - jnp.log(1.0 + jnp.exp(-jnp.abs(x)))


def _expand_heads(col16, rows):
    head_of_lane = lax.broadcasted_iota(I32, (rows, HEADS * HEAD_DIM), 1) // HEAD_DIM
    out = jnp.zeros((rows, HEADS * HEAD_DIM), F32)
    for h in range(HEADS):
        out = jnp.where(head_of_lane == h, col16[:, h:h + 1], out)
    return out


def _forget_cumsum(f_logit, b_f, *, name):
    s = f_logit.shape[0]
    nb = s // 128

    def body(f_ref, b_ref, cqe_ref):
        row = lax.broadcasted_iota(I32, (128, 128), 0)
        col = lax.broadcasted_iota(I32, (128, 128), 1)
        tri = (col <= row).astype(BF16)

        def step(n, carry):
            r0 = pl.multiple_of(n * 128, 128)
            lf = _log_sigmoid(f_ref[pl.ds(r0, 128), :] + b_ref[...])
            cum = _tri_dot(tri, lf) + carry
            cqe_ref[pl.ds(r0, 128), :] = _expand_heads(cum, 128)
            return cum[127:128, :]

        lax.fori_loop(0, nb, step, jnp.zeros((1, 128), F32))

    return pl.pallas_call(
        body, name=name, grid=(1,),
        in_specs=[_full_spec((s, 128)), _full_spec((1, 128))],
        out_specs=_full_spec((s, HEADS * HEAD_DIM)),
        out_shape=jax.ShapeDtypeStruct((s, HEADS * HEAD_DIM), F32),
        compiler_params=_cparams(("arbitrary",)),
    )(f_logit, b_f)


def _forget_bwd(dcq16, dck16, f_logit, b_f, *, name):
    s = f_logit.shape[0]
    nb = s // 128

    def body(a_ref, k_ref, f_ref, b_ref, df_ref, db_ref):
        row = lax.broadcasted_iota(I32, (128, 128), 0)
        col = lax.broadcasted_iota(I32, (128, 128), 1)
        tri_rev = (col >= row).astype(BF16)

        def step(m, carry):
            suffix, dbsum = carry
            n = nb - 1 - m
            r0 = pl.multiple_of(n * 128, 128)
            dcum = a_ref[pl.ds(r0, 128), :] + k_ref[pl.ds(r0, 128), :]
            dlf = _tri_dot(tri_rev, dcum) + suffix
            df = dlf * _sigmoid(-(f_ref[pl.ds(r0, 128), :] + b_ref[...]))
            df_ref[pl.ds(r0, 128), :] = df.astype(BF16)
            return dlf[0:1, :], dbsum + jnp.sum(df, axis=0, keepdims=True)

        _, dbsum = lax.fori_loop(0, nb, step, (jnp.zeros((1, 128), F32), jnp.zeros((1, 128), F32)))
        db_ref[...] = dbsum

    return pl.pallas_call(
        body, name=name, grid=(1,),
        in_specs=[_full_spec((s, 128))] * 3 + [_full_spec((1, 128))],
        out_specs=[_full_spec((s, 128)), _full_spec((1, 128))],
        out_shape=[jax.ShapeDtypeStruct((s, 128), BF16), jax.ShapeDtypeStruct((1, 128), F32)],
        compiler_params=_cparams(("arbitrary",)),
    )(dcq16, dck16, f_logit, b_f)


ATT_T = 256


def _head_lanes(rows):
    return lax.broadcasted_iota(I32, (rows, 128), 1) < HEAD_DIM


def _bf16_pieces(c):
    p0 = c.astype(BF16).astype(F32)
    r = c - p0
    p1 = r.astype(BF16).astype(F32)
    p2 = (r - p1).astype(BF16).astype(F32)
    return p0, p1, p2


def _col_reduce(x, op):
    rows = x.shape[0]
    while rows > 8:
        rows //= 2
        x = op(x[:rows], x[rows:])
    return jnp.max(x, axis=0, keepdims=True) if op is jnp.maximum else jnp.sum(x, axis=0, keepdims=True)


def _attn_prep(qkv, cqe, *, name):
    s = qkv.shape[0]
    npair = HEADS // 2

    def body(q_ref, k_ref, v_ref, c_ref, qa_ref, ka_ref, vt_ref):
        rows = 128
        lane = lax.broadcasted_iota(I32, (rows, 128), 1)

        def chunk(n, _):
            r0 = pl.multiple_of(n * rows, rows)
            sl = pl.ds(r0, rows)
            qv = q_ref[sl, :].astype(F32) * ATT_SCALE
            kv = k_ref[sl, :].astype(F32)
            for e in range(2):
                mine = (lane < HEAD_DIM) if e == 0 else (lane >= HEAD_DIM)
                base = HEAD_DIM * (1 - e)
                p0, p1, p2 = _bf16_pieces(c_ref[sl, HEAD_DIM * e:HEAD_DIM * e + 1])
                ones_hi = jnp.where((lane >= base + 3) & (lane < base + 6), 1.0, 0.0)
                ones_lo = jnp.where((lane >= base) & (lane < base + 3), 1.0, 0.0)
                qa = jnp.where(mine, qv, jnp.where(lane == base, p0, jnp.where(lane == base + 1, p1,
                               jnp.where(lane == base + 2, p2, ones_hi))))
                ka = jnp.where(mine, kv, jnp.where(lane == base + 3, -p0, jnp.where(lane == base + 4, -p1,
                               jnp.where(lane == base + 5, -p2, ones_lo))))
                qa_ref[e, sl, :] = qa.astype(BF16)
                ka_ref[e, sl, :] = ka.astype(BF16)
            vt_ref[0, :, sl] = v_ref[sl, :].astype(F32).T.astype(BF16)
            return 0

        lax.fori_loop(0, s // rows, chunk, 0)

    pair = pl.BlockSpec((2, s, 128), lambda hp: (hp, 0, 0))
    return pl.pallas_call(
        body, name=name, grid=(npair,),
        in_specs=[pl.BlockSpec((s, 128), lambda hp: (0, hp)), pl.BlockSpec((s, 128), lambda hp: (0, npair + hp)),
                  pl.BlockSpec((s, 128), lambda hp: (0, 2 * npair + hp)), pl.BlockSpec((s, 128), lambda hp: (0, hp))],
        out_specs=[pair, pair, pl.BlockSpec((1, 128, s), lambda hp: (hp, 0, 0))],
        out_shape=[jax.ShapeDtypeStruct((HEADS, s, 128), BF16), jax.ShapeDtypeStruct((HEADS, s, 128), BF16),
                   jax.ShapeDtypeStruct((npair, 128, s), BF16)],
        compiler_params=_cparams(("parallel",)),
    )(qkv, qkv, qkv, cqe)


def _attn_fwd(qa, ka, vt, shards=(), *, name):
    s = qa.shape[1]
    t = ATT_T
    nq = s // t
    n = len(shards)
    npair = HEADS // 2

    def body(*refs):
        qa_ref, ka_ref, vt_ref = refs[:3]
        x_refs = refs[3:3 + n]
        o_ref, lse_ref = refs[3 + n:5 + n]
        out_refs = refs[5 + n:5 + 2 * n]
        sems = refs[5 + 2 * n:]
        hp = pl.program_id(0)
        i = pl.program_id(1)
        if n:
            @pl.when((hp == 0) & (i == 0))
            def _():
                local, sends, _ = _gather1_copies(x_refs, out_refs, *sems, with_arrivals=False)
                for cp in local + sends:
                    cp.start()

        krow = lax.broadcasted_iota(I32, (t, t), 0)
        qcol = lax.broadcasted_iota(I32, (t, t), 1)
        sub = lax.broadcasted_iota(I32, (128, t), 0)
        row8 = lax.broadcasted_iota(I32, (8, t), 0)
        qbs = (qa_ref[0], qa_ref[1])
        tk = 2 * t

        def step(j, carry, diag):
            c0 = pl.multiple_of(j * tk, tk)
            vtb = vt_ref[0, :, pl.ds(c0, tk)]
            sts = [_dot(ka_ref[e, pl.ds(c0, tk), :], qbs[e], NT) for e in range(2)]
            if diag:
                keep = (lax.broadcasted_iota(I32, (tk, t), 0) - lax.broadcasted_iota(I32, (tk, t), 1)) <= t * (i % 2)
                sts = [jnp.where(keep, st, NEG) for st in sts]
            pts, stats = [], []
            for e in range(2):
                m, l, _ = carry[e]
                m_new = jnp.maximum(m, _col_reduce(sts[e], jnp.maximum))
                alpha = jnp.exp(m - m_new)
                pt = jnp.exp(sts[e] - m_new)
                stats.append((m_new, alpha, alpha * l + _col_reduce(pt, jnp.add)))
                pts.append(pt.astype(BF16))
            pvs = [_dot(vtb, pts[e], NN) for e in range(2)]
            return tuple((stats[e][0], stats[e][2], stats[e][1] * carry[e][2] + pvs[e]) for e in range(2))

        init = (jnp.full((1, t), NEG, F32), jnp.zeros((1, t), F32), jnp.zeros((128, t), F32))
        carry = lax.fori_loop(0, i // 2, functools.partial(step, diag=False), (init, init))
        (m0, l0, acc0), (m1, l1, acc1) = step(i // 2, carry, True)
        o_pair = jnp.where(sub < HEAD_DIM, acc0 / l0, acc1 / l1)
        o_ref[...] = o_pair.T.astype(BF16)
        lse_ref[0] = jnp.where(row8 == 0, m0 + jnp.log(l0), jnp.where(row8 == 1, m1 + jnp.log(l1), 0.0))
        if n:
            @pl.when((hp == npair - 1) & (i == nq - 1))
            def _():
                local, sends, arrivals = _gather1_copies(x_refs, out_refs, *sems)
                for cp in arrivals:
                    cp.wait_recv()
                for cp in sends:
                    cp.wait_send()
                for cp in local:
                    cp.wait()

    out = pl.pallas_call(
        body, name=name, grid=(npair, nq),
        in_specs=[pl.BlockSpec((2, t, 128), lambda hp, i: (hp, i, 0)), pl.BlockSpec((2, s, 128), lambda hp, i: (hp, 0, 0)),
                  pl.BlockSpec((1, 128, s), lambda hp, i: (hp, 0, 0))] + [ANY] * n,
        out_specs=[pl.BlockSpec((t, 128), lambda hp, i: (i, hp)), pl.BlockSpec((1, 8, t), lambda hp, i: (hp, 0, i))]
        + [ANY] * n,
        out_shape=[jax.ShapeDtypeStruct((s, HEADS * HEAD_DIM), BF16), jax.ShapeDtypeStruct((npair, 8, s), F32)]
        + [jax.ShapeDtypeStruct((N_DEV,) + a.shape, a.dtype) for a in shards],
        scratch_shapes=_gather1_scratch(n) if n else [],
        compiler_params=_cparams(("arbitrary", "arbitrary")),
    )(qa, ka, vt, *shards)
    return out[0], out[1], list(out[2:])


def _attn_delta(do, o, *, name):
    s = do.shape[0]

    def body(do_ref, o_ref, d_ref):
        prod = do_ref[...].astype(F32) * o_ref[...].astype(F32)
        row = lax.broadcasted_iota(I32, (8, 128), 0)
        lane = lax.broadcasted_iota(I32, (8, 128), 1)
        sel = ((row == 0) & (lane < HEAD_DIM) | (row == 1) & (lane >= HEAD_DIM)).astype(BF16)
        p0, p1, p2 = _bf16_pieces(prod)
        d_ref[0] = (_dot(sel, p0.astype(BF16), NT) + _dot(sel, p1.astype(BF16), NT)) + _dot(sel, p2.astype(BF16), NT)

    pair = pl.BlockSpec((s, 128), lambda hp: (0, hp))
    return pl.pallas_call(
        body, name=name, grid=(HEADS // 2,), in_specs=[pair, pair],
        out_specs=pl.BlockSpec((1, 8, s), lambda hp: (hp, 0, 0)),
        out_shape=jax.ShapeDtypeStruct((HEADS // 2, 8, s), F32), compiler_params=_cparams(("parallel",)),
    )(do, o)


def _attn_bwd(qa, ka, qkv, do, lse3, delta3, psums=(), *, name):
    s = qa.shape[1]
    t = ATT_T
    nb = s // t
    npair = HEADS // 2
    n = len(psums)

    def body(*refs):
        qa_ref, ka_ref, v_ref, do_ref, lse_ref, delta_ref = refs[:6]
        p_refs = refs[6:6 + n]
        dq_ref, dk_ref, dv_ref, aux_ref, dcq_ref = refs[6 + n:11 + n]
        rp_refs = refs[11 + n:11 + 2 * n]
        dqt = refs[11 + 2 * n]
        sems = refs[12 + 2 * n:]
        hp = pl.program_id(0)
        if n:
            @pl.when(hp == 0)
            def _():
                for cp in _chip_scatter_copies(p_refs, rp_refs, *sems):
                    cp.start()

        first = _head_lanes(t)
        dqt[...] = jnp.zeros_like(dqt)
        krow = lax.broadcasted_iota(I32, (t, t), 0)
        qcol = lax.broadcasted_iota(I32, (t, t), 1)

        def key_block(j, _):
            c0 = pl.multiple_of(j * t, t)
            vb = v_ref[pl.ds(c0, t), :]
            kbs = (ka_ref[0, pl.ds(c0, t), :], ka_ref[1, pl.ds(c0, t), :])
            kbts = tuple(kb.astype(F32).T.astype(BF16) for kb in kbs)
            vhs = (jnp.where(first, vb, jnp.zeros_like(vb)), jnp.where(first, jnp.zeros_like(vb), vb))

            def query_block(i, carry, diag):
                r0 = pl.multiple_of(i * t, t)
                dob = do_ref[pl.ds(r0, t), :]
                qbs = [qa_ref[e, pl.ds(r0, t), :] for e in range(2)]
                sts = [_dot(kbs[e], qbs[e], NT) for e in range(2)]
                dpts = [_dot(vhs[e], dob, NT) for e in range(2)]
                ptbs, dsbs = [], []
                for e in range(2):
                    st = jnp.where(krow <= qcol, sts[e], NEG) if diag else sts[e]
                    pt = jnp.exp(st - lse_ref[0, e:e + 1, pl.ds(r0, t)])
                    dsbs.append((pt * (dpts[e] - delta_ref[0, e:e + 1, pl.ds(r0, t)])).astype(BF16))
                    ptbs.append(pt.astype(BF16))
                out = []
                for e in range(2):
                    dk_a, dv_a = carry[e]
                    dv_a = dv_a + _dot(ptbs[e], dob, NN)
                    dk_a = dk_a + _dot(dsbs[e], qbs[e], NN)
                    dqt[e, :, pl.ds(r0, t)] += _dot(kbts[e], dsbs[e], NN)
                    out.append((dk_a, dv_a))
                return tuple(out)

            zero = jnp.zeros((t, 128), F32)
            carry = query_block(j, ((zero, zero), (zero, zero)), True)
            (dk0, dv0), (dk1, dv1) = lax.fori_loop(j + 1, nb, functools.partial(query_block, diag=False), carry)
            dk_ref[pl.ds(c0, t), :] = jnp.where(first, dk0, dk1).astype(BF16)
            dv_ref[pl.ds(c0, t), :] = jnp.where(first, dv0, dv1).astype(BF16)
            aux_ref[pl.ds(c0, t), :] = jnp.where(first, dk1, dk0)
            return 0

        lax.fori_loop(0, nb, key_block, 0)
        sub = lax.broadcasted_iota(I32, (128, s), 0)
        row8 = lax.broadcasted_iota(I32, (8, s), 0)
        dq_ref[...] = (jnp.where(sub < HEAD_DIM, dqt[0], dqt[1]) * ATT_SCALE).T.astype(BF16)
        dcq_ref[0] = jnp.where(row8 == 0, dqt[0, HEAD_DIM:HEAD_DIM + 1, :], jnp.where(row8 == 1, dqt[1, 0:1, :], 0.0))
        if n:
            @pl.when(hp == npair - 1)
            def _():
                for cp in _chip_scatter_copies(p_refs, rp_refs, *sems):
                    cp.wait()

    def pair_cols(off):
        return pl.BlockSpec((s, 128), lambda hp: (0, off + hp))

    heads = pl.BlockSpec((2, s, 128), lambda hp: (hp, 0, 0))
    rows = pl.BlockSpec((1, 8, s), lambda hp: (hp, 0, 0))
    wide = jax.ShapeDtypeStruct((s, HEADS * HEAD_DIM), BF16)
    out = pl.pallas_call(
        body, name=name, grid=(npair,),
        in_specs=[heads, heads, pair_cols(2 * npair), pair_cols(0), rows, rows] + [ANY] * n,
        out_specs=[pair_cols(0), pair_cols(0), pair_cols(0), pair_cols(0), rows] + [ANY] * n,
        out_shape=[wide, wide, wide, jax.ShapeDtypeStruct((s, HEADS * HEAD_DIM), F32),
                   jax.ShapeDtypeStruct((npair, 8, s), F32)]
        + [jax.ShapeDtypeStruct((3,) + a.shape[1:], a.dtype) for a in psums],
        scratch_shapes=[pltpu.VMEM((2, 128, s), F32)]
        + ([pltpu.SemaphoreType.DMA((3 * n,)), pltpu.SemaphoreType.DMA((3 * n,))] if n else []),
        compiler_params=_cparams(("arbitrary",)),
    )(qa, ka, qkv, do, lse3, delta3, *psums)
    return out[:5], list(out[5:])


def _adam_math(w, g, m, v):
    m = ADAM_B1 * m + (1.0 - ADAM_B1) * g
    v = ADAM_B2 * v + (1.0 - ADAM_B2) * (g * g)
    m_hat = m / (1.0 - ADAM_B1 ** ADAM_STEP)
    v_hat = v / (1.0 - ADAM_B2 ** ADAM_STEP)
    delta = -ADAM_LR * (m_hat / (jnp.sqrt(v_hat) + ADAM_EPS) + ADAM_WD * w)
    return delta, m, v


def _sum_pairs(keep, recv, pos, *, name):
    _, r, c = recv.shape
    tr = _row_tile(r, 512)

    def body(pos_ref, a_ref, b_ref, o32_ref, o16_ref):
        tot = a_ref[...].astype(F32) + b_ref[...].astype(F32)
        o32_ref[...] = tot
        o16_ref[...] = tot.astype(BF16)

    out = pl.BlockSpec((1, tr, c), lambda q, i, pos: (q, i, 0))
    grid_spec = pltpu.PrefetchScalarGridSpec(
        num_scalar_prefetch=1, grid=(4, r // tr),
        in_specs=[pl.BlockSpec((1, tr, c), lambda q, i, pos: (2 * q + pos[2], i, 0)), out],
        out_specs=[out, out])
    return pl.pallas_call(
        body, name=name, grid_spec=grid_spec,
        out_shape=[jax.ShapeDtypeStruct((4, r, c), F32), jax.ShapeDtypeStruct((4, r, c), BF16)],
        compiler_params=_cparams(("parallel", "parallel")),
    )(pos, keep, recv)


def _sum_chips(psum, recv, pos, *, name):
    _, r, c = recv.shape
    tr = _row_tile(r, 512)

    def body(pos_ref, p_ref, r_ref, g_ref):
        g_ref[...] = p_ref[0] + r_ref[0].astype(F32) + r_ref[1].astype(F32) + r_ref[2].astype(F32)

    grid_spec = pltpu.PrefetchScalarGridSpec(
        num_scalar_prefetch=1, grid=(r // tr,),
        in_specs=[pl.BlockSpec((1, tr, c), lambda i, pos: (2 * pos[0] + pos[1], i, 0)),
                  pl.BlockSpec((3, tr, c), lambda i, pos: (0, i, 0))],
        out_specs=pl.BlockSpec((tr, c), lambda i, pos: (i, 0)))
    return pl.pallas_call(
        body, name=name, grid_spec=grid_spec, out_shape=jax.ShapeDtypeStruct((r, c), F32),
        compiler_params=_cparams(("parallel",)),
    )(pos, psum, recv)


def _adam(g, w, m, v, *, name):
    r, c = w.shape
    tr = _row_tile(r, 256)

    def body(g_ref, w_ref, m_ref, v_ref, d_ref, mo_ref, vo_ref):
        delta, mn, vn = _adam_math(w_ref[...], g_ref[...], m_ref[...], v_ref[...])
        d_ref[...] = delta
        mo_ref[...] = mn
        vo_ref[...] = vn

    blk = _row_spec(tr, c)
    o = jax.ShapeDtypeStruct((r, c), F32)
    return pl.pallas_call(
        body, name=name, grid=(r // tr,), in_specs=[blk] * 4, out_specs=[blk] * 3, out_shape=[o, o, o],
        compiler_params=_cparams(("parallel",)),
    )(g, w, m, v)


def _adam_replicated(chip_sums, w, m, v, *, name):
    r = w.shape[0]

    def body(s_ref, w_ref, m_ref, v_ref, g_ref, d_ref, mo_ref, vo_ref):
        g = ((s_ref[0] + s_ref[1]) + s_ref[2]) + s_ref[3]
        delta, mn, vn = _adam_math(w_ref[...], g, m_ref[...], v_ref[...])
        g_ref[...] = g
        d_ref[...] = delta
        mo_ref[...] = mn
        vo_ref[...] = vn

    o = jax.ShapeDtypeStruct((r, 1024), F32)
    full = _full_spec((r, 1024))
    return pl.pallas_call(
        body, name=name, grid=(1,),
        in_specs=[_full_spec((4, r, 1024)), full, full, full], out_specs=[full] * 4, out_shape=[o] * 4,
        compiler_params=_cparams(("arbitrary",)),
    )(chip_sums, w, m, v)


def _pair_sum_small(mine, theirs, *, name):
    def body(a_ref, b_ref, o_ref):
        o_ref[...] = a_ref[...] + b_ref[...]

    full = _full_spec(mine.shape)
    return pl.pallas_call(
        body, name=name, grid=(1,), in_specs=[full, full], out_specs=full,
        out_shape=jax.ShapeDtypeStruct(mine.shape, F32), compiler_params=_cparams(("arbitrary",)),
    )(mine, theirs)


ANY = pl.BlockSpec(memory_space=pl.ANY)
OTHER_CHIPS = ((1, 0), (0, 1), (1, 1))


def _gather1_copies(x_refs, out_refs, send_sems, recv_sems, local_sems, with_arrivals=True):
    x, y, c = lax.axis_index("x"), lax.axis_index("y"), lax.axis_index("c")
    peers = [(x, y, 1 - c)] + [(x ^ fx, y ^ fy, c) for fx, fy in OTHER_CHIPS]
    local, sends, arrivals = [], [], []
    for t, (x_ref, out_ref) in enumerate(zip(x_refs, out_refs)):
        local.append(pltpu.make_async_copy(x_ref, out_ref.at[4 * x + 2 * y + c], local_sems.at[t]))
        for k, (px, py, pc) in enumerate(peers):
            sems = dict(send_sem=send_sems.at[4 * t + k], recv_sem=recv_sems.at[4 * t + k],
                        device_id=(px, py, pc), device_id_type=MESH)
            sends.append(pltpu.make_async_remote_copy(src_ref=x_ref, dst_ref=out_ref.at[4 * x + 2 * y + c], **sems))
            if with_arrivals:
                arrivals.append(
                    pltpu.make_async_remote_copy(src_ref=x_ref, dst_ref=out_ref.at[4 * px + 2 * py + pc], **sems))
    return local, sends, arrivals


def _gather1_scratch(n):
    return [pltpu.SemaphoreType.DMA((4 * n,)), pltpu.SemaphoreType.DMA((4 * n,)), pltpu.SemaphoreType.DMA((n,))]


def _gather2(gathered, *, name):
    n = len(gathered)

    def body(*refs):
        in_refs, g_refs = refs[:n], refs[n:2 * n]
        send_sems, recv_sems = refs[2 * n:]
        x, y, c = lax.axis_index("x"), lax.axis_index("y"), lax.axis_index("c")
        copies, arrivals = [], []
        for t in range(n):
            for j, (fx, fy) in enumerate(OTHER_CHIPS):
                px, py = x ^ fx, y ^ fy
                sems = dict(send_sem=send_sems.at[3 * t + j], recv_sem=recv_sems.at[3 * t + j],
                            device_id=(x, y, 1 - c), device_id_type=MESH)
                mine, theirs = 4 * px + 2 * py + c, 4 * px + 2 * py + (1 - c)
                copies.append(pltpu.make_async_remote_copy(src_ref=in_refs[t].at[mine], dst_ref=g_refs[t].at[mine], **sems))
                arrivals.append(pltpu.make_async_remote_copy(src_ref=in_refs[t].at[mine], dst_ref=g_refs[t].at[theirs], **sems))
        for cp in copies:
            cp.start()
        for cp in arrivals:
            cp.wait_recv()
        for cp in copies:
            cp.wait_send()

    return pl.pallas_call(
        body, name=name, out_shape=[jax.ShapeDtypeStruct(a.shape, a.dtype) for a in gathered],
        in_specs=[ANY] * n, out_specs=[ANY] * n, input_output_aliases={t: t for t in range(n)},
        scratch_shapes=[pltpu.SemaphoreType.DMA((3 * n,)), pltpu.SemaphoreType.DMA((3 * n,))],
    )(*gathered)


def _chip_scatter_copies(p_refs, rp_refs, send_sems, recv_sems):
    x, y, c = lax.axis_index("x"), lax.axis_index("y"), lax.axis_index("c")
    copies = []
    for t, (p_ref, rp_ref) in enumerate(zip(p_refs, rp_refs)):
        for k, (fx, fy) in enumerate(OTHER_CHIPS):
            px, py = x ^ fx, y ^ fy
            copies.append(pltpu.make_async_remote_copy(
                src_ref=p_ref.at[2 * px + py], dst_ref=rp_ref.at[k], send_sem=send_sems.at[3 * t + k],
                recv_sem=recv_sems.at[3 * t + k], device_id=(px, py, c), device_id_type=MESH))
    return copies


def _allgather(shards, *, name):
    n = len(shards)

    def body(*refs):
        x_refs, out_refs = refs[:n], refs[n:2 * n]
        send_sems, recv_sems, local_sems = refs[2 * n:]
        x, y, c = lax.axis_index("x"), lax.axis_index("y"), lax.axis_index("c")
        me, sibling = (x, y, c), (x, y, 1 - c)
        chips = [(x ^ fx, y ^ fy) for fx, fy in OTHER_CHIPS]

        def copy(t, k, block, to, from_input=False):
            px, py, pc = block
            slab = out_refs[t].at[4 * px + 2 * py + pc]
            return pltpu.make_async_remote_copy(
                src_ref=x_refs[t] if from_input else slab, dst_ref=slab,
                send_sem=send_sems.at[7 * t + k], recv_sem=recv_sems.at[7 * t + k], device_id=to, device_id_type=MESH)

        mine = [pltpu.make_async_copy(x_refs[t], out_refs[t].at[4 * x + 2 * y + c], local_sems.at[t]) for t in range(n)]
        for cp in mine:
            cp.start()
        first = []
        for t in range(n):
            first.append(copy(t, 0, me, sibling, from_input=True))
            first += [copy(t, 1 + j, me, (*chip, c), from_input=True) for j, chip in enumerate(chips)]
        for cp in first:
            cp.start()
        passed = []
        for j, chip in enumerate(chips):
            for t in range(n):
                copy(t, 1 + j, (*chip, c), me).wait_recv()
                fwd = copy(t, 4 + j, (*chip, c), sibling)
                fwd.start()
                passed.append(fwd)
        for t in range(n):
            copy(t, 0, sibling, me).wait_recv()
            for j, chip in enumerate(chips):
                copy(t, 4 + j, (*chip, 1 - c), me).wait_recv()
        for cp in first + passed:
            cp.wait_send()
        for cp in mine:
            cp.wait()

    return pl.pallas_call(
        body, name=name, out_shape=[jax.ShapeDtypeStruct((N_DEV,) + a.shape, a.dtype) for a in shards],
        in_specs=[ANY] * n, out_specs=[ANY] * n,
        scratch_shapes=[pltpu.SemaphoreType.DMA((7 * n,)), pltpu.SemaphoreType.DMA((7 * n,)),
                        pltpu.SemaphoreType.DMA((n,))],
    )(*shards)


def _exchange_sibling(slabs, small, *, name):
    n = len(slabs)
    extra = [] if small is None else [small]
    ne = n + len(extra)

    def body(*refs):
        in_refs, out_refs = refs[:ne], refs[ne:2 * ne]
        send_sems, recv_sems = refs[2 * ne:]
        x, y, c = lax.axis_index("x"), lax.axis_index("y"), lax.axis_index("c")
        sibling = (x, y, 1 - c)
        copies = []
        for t in range(n):
            for q in range(4):
                copies.append(pltpu.make_async_remote_copy(
                    src_ref=in_refs[t].at[2 * q + (1 - c)], dst_ref=out_refs[t].at[q],
                    send_sem=send_sems.at[4 * t + q], recv_sem=recv_sems.at[4 * t + q],
                    device_id=sibling, device_id_type=MESH))
        if extra:
            copies.append(pltpu.make_async_remote_copy(
                src_ref=in_refs[n], dst_ref=out_refs[n], send_sem=send_sems.at[4 * n], recv_sem=recv_sems.at[4 * n],
                device_id=sibling, device_id_type=MESH))
        for cp in copies:
            cp.start()
        for cp in copies:
            cp.wait()

    return pl.pallas_call(
        body, name=name,
        out_shape=[jax.ShapeDtypeStruct((4,) + a.shape[1:], a.dtype) for a in slabs]
        + [jax.ShapeDtypeStruct(a.shape, a.dtype) for a in extra],
        in_specs=[ANY] * ne, out_specs=[ANY] * ne,
        scratch_shapes=[pltpu.SemaphoreType.DMA((4 * n + 1,)), pltpu.SemaphoreType.DMA((4 * n + 1,))],
    )(*slabs, *extra)


def _exchange_chips(psums, small_sum, *, name):
    n = len(psums)
    rs = small_sum.shape[0]

    def body(*refs):
        p_refs, s_ref = refs[:n], refs[n]
        rp_refs, tab_ref = refs[n + 1:2 * n + 1], refs[2 * n + 1]
        send_sems, recv_sems, local_sem = refs[2 * n + 2:]
        x, y, c = lax.axis_index("x"), lax.axis_index("y"), lax.axis_index("c")
        mine = pltpu.make_async_copy(s_ref, tab_ref.at[2 * x + y], local_sem)
        mine.start()

        def table_copy(k, px, py, slot):
            return pltpu.make_async_remote_copy(
                src_ref=s_ref, dst_ref=tab_ref.at[slot], send_sem=send_sems.at[3 * n + k],
                recv_sem=recv_sems.at[3 * n + k], device_id=(px, py, c), device_id_type=MESH)

        copies = []
        for k, (fx, fy) in enumerate(OTHER_CHIPS):
            px, py = x ^ fx, y ^ fy
            for t in range(n):
                copies.append(pltpu.make_async_remote_copy(
                    src_ref=p_refs[t].at[2 * px + py], dst_ref=rp_refs[t].at[k],
                    send_sem=send_sems.at[3 * t + k], recv_sem=recv_sems.at[3 * t + k],
                    device_id=(px, py, c), device_id_type=MESH))
            copies.append(table_copy(k, px, py, 2 * x + y))
        for cp in copies:
            cp.start()
        for k, (fx, fy) in enumerate(OTHER_CHIPS):
            px, py = x ^ fx, y ^ fy
            for t in range(n):
                copies[k * (n + 1) + t].wait()
            table_copy(k, px, py, 2 * px + py).wait()
        mine.wait()

    return pl.pallas_call(
        body, name=name,
        out_shape=[jax.ShapeDtypeStruct((3,) + a.shape[1:], a.dtype) for a in psums]
        + [jax.ShapeDtypeStruct((4, rs, 1024), F32)],
        in_specs=[ANY] * (n + 1), out_specs=[ANY] * (n + 1),
        scratch_shapes=[pltpu.SemaphoreType.DMA((3 * n + 3,)), pltpu.SemaphoreType.DMA((3 * n + 3,)),
                        pltpu.SemaphoreType.DMA],
    )(*psums, small_sum)


def _to_comm(name, kind, block):
    a = block[0]
    if kind == "cols":
        a = a.T
        if name == "w_in":
            a = jnp.pad(a, ((0, IN_SHARD_PAD - IN_SHARD), (0, 0)))
    return a if kind == "f32" else a.astype(BF16)


def _from_comm(name, kind, a):
    if kind == "cols":
        if name == "w_in":
            a = a[:IN_SHARD]
        a = a.T
    return a[None]


def _assemble_weights(g):
    out = {}
    if "w_in" in g:
        wt_in = g["w_in"][:, :IN_SHARD].reshape(IN_COLS, D_MODEL)
        out["wt_main"] = jnp.concatenate([wt_in[:2048], wt_in[O_G:], wt_in[2048:O_F]], axis=0)
        out["wt_f"] = jnp.pad(wt_in[O_F:O_G], ((0, 128 - HEADS), (0, 0)))
    square = dict(w_branch_a="w_a", w_branch_b="w_b", w_out="w_out", w_ple_gate="w_pg")
    for long, short in square.items():
        if long in g:
            out[short] = g[long].reshape(D_MODEL, D_MODEL)
    if "w_up" in g:
        out["wt_up"] = g["w_up"].reshape(2 * D_FF, D_MODEL)
    if "conv_w" in g:
        out["conv_w"] = g["conv_w"].transpose(1, 0, 2).reshape(3, 2 * D_FF)
    if "w_down" in g:
        out["w_down"] = g["w_down"].reshape(D_FF, D_MODEL)
    if "w_ple" in g:
        out["wt_ple"] = g["w_ple"].reshape(D_MODEL, PLE_DIM)
    return out


def _grad_slabs(gr):
    out = {}
    if "wt_main" in gr:
        gm = gr["wt_main"]
        gt_in = jnp.concatenate([gm[:2048], gm[4096:], gr["wt_f"][:HEADS], gm[2048:4096]], axis=0)
        out["w_in"] = jnp.pad(gt_in.reshape(N_DEV, IN_SHARD, D_MODEL), ((0, 0), (0, IN_SHARD_PAD - IN_SHARD), (0, 0)))
    rows = dict(w_a="w_branch_a", w_b="w_branch_b", w_out="w_out", wt_up="w_up", w_down="w_down", w_pg="w_ple_gate")
    for short, long in rows.items():
        if short in gr:
            out[long] = gr[short].reshape(N_DEV, -1, D_MODEL)
    if "conv_w" in gr:
        out["conv_w"] = gr["conv_w"].reshape(3, N_DEV, -1).transpose(1, 0, 2)
    if "wt_ple" in gr:
        out["w_ple"] = gr["wt_ple"].reshape(N_DEV, -1, PLE_DIM)
    return {k: v.astype(BF16) for k, v in out.items()}


def _rows(a, rows):
    flat = a.reshape(-1)
    return jnp.pad(flat, (0, rows * 1024 - flat.shape[0])).reshape(rows, 1024)


def _pack_small(parts):
    return jnp.concatenate([_rows(parts[n].astype(F32), r) for n, r in SMALL], axis=0)


def _small(packed, name, shape):
    off, r = SMALL_OFF[name]
    n = math.prod(shape)
    return packed[off:off + r].reshape(-1)[:n].reshape(shape)


class _Overlap:
    def __init__(self, shards, finish_gather, start_reduce, finish_reduce):
        self.shards, self.finish_gather = shards, finish_gather
        self.start_reduce, self.finish_reduce = start_reduce, finish_reduce


def _local_step(x, p, target, w, sm, overlap=None):
    s = x.shape[0]
    mm = _matmul
    wt_main = w["wt_main"]
    conv_b = sm["conv_b"]
    bs_t = jnp.pad(sm["gmlp_b_s"].T, ((0, 0), (0, 128 - GROUPS)))
    b_f = jnp.pad(sm["b_f"], ((0, 0), (0, 128 - HEADS)))
    big = dict(tm=1024, tn=1024, tk=1024)
    whole_s = dict(tn=1024, tk=s)

    h = _rmsnorm_fwd(x, sm["norm_mix_g"], name="norm_mix")
    zuvg = mm(h, wt_main, mode="nt", out_dtype=F32, name="in_uvg", n=4096, **big)
    qkv = mm(h, wt_main, mode="nt", out_dtype=BF16, name="in_qkv", n=3072, b_off=4, **big)
    f_logit = mm(h, w["wt_f"], mode="nt", out_dtype=F32, name="in_f", tm=1024, tk=1024)
    a = _gmlp_fwd(zuvg, sm["gmlp_ln_g"], sm["gmlp_ln_b"], sm["gmlp_w_s"], bs_t, name="gmlp_fwd")
    cqe = _forget_cumsum(f_logit, b_f, name="forget_cumsum")
    qa, ka, vt = _attn_prep(qkv, cqe, name="attn_prep")
    if overlap is None:
        b, lse3, _ = _attn_fwd(qa, ka, vt, name="attn_fwd")
    else:
        b, lse3, level1 = _attn_fwd(qa, ka, vt, overlap.shards, name="attn_fwd")
        w = {**w, **overlap.finish_gather(level1)}
    wt_up, conv_w = w["wt_up"], w["conv_w"]
    ya = mm(a, w["w_a"], mode="nn", out_dtype=F32, name="branch_a", **big)
    yb = mm(b, w["w_b"], mode="nn", out_dtype=F32, name="branch_b", **big)
    merged = _merge_fwd(ya, yb, zuvg, name="merge_fwd")
    x1 = mm(merged, w["w_out"], mode="nn", out_dtype=F32, name="out_proj", add=x, **big)
    h2 = _rmsnorm_fwd(x1, sm["norm_ffn_g"], name="norm_ffn")
    up = mm(h2, wt_up, mode="nt", out_dtype=F32, name="up", tm=1024, tn=512, tk=1024)
    act = _convglu_fwd(up, conv_w, conv_b, name="convglu_fwd")
    x2 = mm(act, w["w_down"], mode="nn", out_dtype=F32, name="down", tm=1024, tn=1024, tk=1408, add=x1)
    h3 = _rmsnorm_fwd(x2, sm["norm_ple_g"], name="norm_ple")
    ple = mm(p, w["wt_ple"], mode="nt", out_dtype=F32, name="ple", tm=1024, tn=1024, tk=256)
    gp = mm(h3, w["w_pg"], mode="nn", out_dtype=F32, name="ple_gate", **big)
    x3 = _ple_fwd(x2, ple, gp, name="ple_fwd")

    loss, dx3, d_norm_final = _final_loss_bwd(x3, target, sm["norm_final_g"], name="loss_bwd")
    dple, dgp = _ple_bwd(dx3, ple, gp, name="ple_bwd")
    g_wt_ple = mm(dple, p, mode="tn", out_dtype=BF16, name="d_w_ple", tm=512, tn=256, tk=s)
    g_w_pg = mm(h3, dgp, mode="tn", out_dtype=BF16, name="d_w_pg", tm=256, **whole_s)
    dh3 = mm(dgp, w["w_pg"], mode="nt", out_dtype=F32, name="d_h3", **big)
    dx2, dx2b, d_norm_ple = _rmsnorm_bwd(dx3, dh3, x2, sm["norm_ple_g"], name="norm_ple_bwd")
    g_w_down = mm(act, dx2b, mode="tn", out_dtype=BF16, name="d_w_down", tm=256, **whole_s)
    dact = mm(dx2b, w["w_down"], mode="nt", out_dtype=BF16, name="d_act", tm=1024, tn=1408, tk=1024)
    dup_a, dup_g, dcw_a, dcw_g, dcb_a, dcb_g = _convglu_bwd(dact, up, conv_w, conv_b, name="convglu_bwd")
    g_wt_up = jnp.concatenate(
        [mm(dup_a, h2, mode="tn", out_dtype=BF16, name="d_w_up_a", tm=256, **whole_s),
         mm(dup_g, h2, mode="tn", out_dtype=BF16, name="d_w_up_g", tm=256, **whole_s)], axis=0)
    dh2 = mm(dup_a, wt_up, mode="nn", out_dtype=F32, name="d_h2_a", tm=1024, tn=1024, tk=1408)
    dh2 = mm(dup_g, wt_up, mode="nn", out_dtype=F32, name="d_h2_g", tm=1024, tn=1024, tk=1408, b_off=2, add=dh2)
    dx1, dx1b, d_norm_ffn = _rmsnorm_bwd(dx2, dh2, x1, sm["norm_ffn_g"], name="norm_ffn_bwd")
    g_w_out = mm(merged, dx1b, mode="tn", out_dtype=BF16, name="d_w_out", tm=256, **whole_s)
    dmerged = mm(dx1b, w["w_out"], mode="nt", out_dtype=F32, name="d_merged", **big)
    dya, dyb, dga, dgb = _merge_bwd(dmerged, ya, yb, zuvg, name="merge_bwd")
    g_w_a = mm(a, dya, mode="tn", out_dtype=BF16, name="d_w_a", tm=256, **whole_s)
    g_w_b = mm(b, dyb, mode="tn", out_dtype=BF16, name="d_w_b", tm=256, **whole_s)
    da = mm(dya, w["w_a"], mode="nt", out_dtype=BF16, name="d_a", **big)
    db = mm(dyb, w["w_b"], mode="nt", out_dtype=BF16, name="d_b", **big)
    dzu, dzv, d_w_s, d_bs_t, d_ln_g, d_ln_b = _gmlp_bwd(
        da, zuvg, sm["gmlp_ln_g"], sm["gmlp_ln_b"], sm["gmlp_w_s"], bs_t, name="gmlp_bwd")
    delta3 = _attn_delta(db, b, name="attn_delta")
    grads = dict(w_a=g_w_a, w_b=g_w_b, w_out=g_w_out, wt_up=g_wt_up, conv_w=jnp.concatenate([dcw_a, dcw_g], axis=1),
                 w_down=g_w_down, wt_ple=g_wt_ple, w_pg=g_w_pg)
    if overlap is None:
        (dq, dk, dv, aux, dcq3), _ = _attn_bwd(qa, ka, qkv, db, lse3, delta3, name="attn_bwd")
    else:
        (dq, dk, dv, aux, dcq3), from_chips = _attn_bwd(
            qa, ka, qkv, db, lse3, delta3, overlap.start_reduce(grads), name="attn_bwd")
        overlap.finish_reduce(from_chips)
    dcq16 = jnp.pad(dcq3[:, :2, :].reshape(HEADS, s).T, ((0, 0), (0, 128 - HEADS)))
    aux = aux.reshape(s, HEADS // 2, 128)
    dck16 = -jnp.stack([aux[:, :, HEAD_DIM + 3], aux[:, :, 3]], axis=-1).reshape(s, HEADS)
    dck16 = jnp.pad(dck16, ((0, 0), (0, 128 - HEADS)))
    dzf, d_b_f = _forget_bwd(dcq16, dck16, f_logit, b_f, name="forget_bwd")
    dz = jnp.concatenate([dzu, dzv, dga, dgb, dq, dk, dv], axis=1)
    g_wt_main = mm(dz, h, mode="tn", out_dtype=BF16, name="d_w_main", tm=512, **whole_s)
    g_wt_f = mm(dzf, h, mode="tn", out_dtype=BF16, name="d_w_f", **whole_s)
    dh = mm(dz, wt_main, mode="nn", out_dtype=F32, name="d_h_main", **big)
    dh = mm(dzf, w["wt_f"], mode="nn", out_dtype=F32, name="d_h_f", tm=1024, tn=1024, add=dh)
    dx0, _, d_norm_mix = _rmsnorm_bwd(dx1, dh, x, sm["norm_mix_g"], name="norm_mix_bwd")

    grads = dict(grads, wt_main=g_wt_main, wt_f=g_wt_f)
    small = dict(norm_mix_g=d_norm_mix, b_f=d_b_f[:, :HEADS], gmlp_ln_g=d_ln_g, gmlp_ln_b=d_ln_b, gmlp_w_s=d_w_s,
                 gmlp_b_s=d_bs_t[:, :GROUPS].T, norm_ffn_g=d_norm_ffn,
                 conv_b=jnp.concatenate([dcb_a, dcb_g], axis=1), norm_ple_g=d_norm_ple, norm_final_g=d_norm_final)
    return loss, dx0, grads, small


def kernel(x, p, norm_mix_g, w_in, b_f, gmlp_ln_g, gmlp_ln_b, gmlp_w_s, gmlp_b_s, w_branch_a, w_branch_b, w_out, norm_ffn_g, w_up, conv_w, conv_b, w_down, norm_ple_g, w_ple, w_ple_gate, norm_final_g, loss_target, m_norm_mix_g, m_w_in, m_b_f, m_gmlp_ln_g, m_gmlp_ln_b, m_gmlp_w_s, m_gmlp_b_s, m_w_branch_a, m_w_branch_b, m_w_out, m_norm_ffn_g, m_w_up, m_conv_w, m_conv_b, m_w_down, m_norm_ple_g, m_w_ple, m_w_ple_gate, m_norm_final_g, v_norm_mix_g, v_w_in, v_b_f, v_gmlp_ln_g, v_gmlp_ln_b, v_gmlp_w_s, v_gmlp_b_s, v_w_branch_a, v_w_branch_b, v_w_out, v_norm_ffn_g, v_w_up, v_conv_w, v_conv_b, v_w_down, v_norm_ple_g, v_w_ple, v_w_ple_gate, v_norm_final_g):
    given = dict(locals())
    weights = {n: given[n] for n in WEIGHT_ORDER}
    mom_m = {n: given["m_" + n] for n in WEIGHT_ORDER}
    mom_v = {n: given["v_" + n] for n in WEIGHT_ORDER}
    pos = jnp.stack([lax.axis_index("x"), lax.axis_index("y"), lax.axis_index("c")]).astype(I32)
    names = [n for n, _ in SHARDED]
    kinds = dict(SHARDED)

    later = [n for n in names if n != "w_in"]
    reduced = {}

    first = _allgather([_to_comm("w_in", kinds["w_in"], weights["w_in"])], name="allgather_w_in")

    def finish_gather(level1):
        return _assemble_weights(dict(zip(later, _gather2(level1, name="allgather_forward"))))

    def start_reduce(grads):
        slabs = _grad_slabs(grads)
        from_sib = _exchange_sibling([slabs[n] for n in later], None, name="exchange_sibling")
        sums = [_sum_pairs(slabs[n], r, pos, name="sum_sibling_" + n) for n, r in zip(later, from_sib)]
        reduced.update({n: s32 for n, (s32, _) in zip(later, sums)})
        return [s16 for _, s16 in sums]

    def finish_reduce(from_chips):
        reduced.update({n: (reduced[n], r) for n, r in zip(later, from_chips)})

    overlap = _Overlap([_to_comm(n, kinds[n], weights[n]) for n in later], finish_gather, start_reduce, finish_reduce)

    sm = dict(norm_mix_g=norm_mix_g, b_f=b_f, gmlp_ln_g=gmlp_ln_g, gmlp_ln_b=gmlp_ln_b, gmlp_w_s=gmlp_w_s[0],
              gmlp_b_s=gmlp_b_s[0], norm_ffn_g=norm_ffn_g, conv_b=conv_b, norm_ple_g=norm_ple_g,
              norm_final_g=norm_final_g.reshape(1, D_MODEL))
    loss_part, dx0, grads, small = _local_step(
        x[0], p[0, 0], loss_target[0], _assemble_weights({"w_in": first[0]}), sm, overlap)

    slab_in = _grad_slabs({k: grads[k] for k in ("wt_main", "wt_f")})["w_in"]
    small_g = _pack_small(small)
    from_sib, small_sib = _exchange_sibling([slab_in], small_g, name="exchange_sibling_w_in")
    s32, s16 = _sum_pairs(slab_in, from_sib, pos, name="sum_sibling_w_in")
    small_chip = _pair_sum_small(small_g, small_sib, name="sum_sibling_small")
    from_chips, small_tab = _exchange_chips([s16], small_chip, name="exchange_chips_w_in")
    reduced["w_in"] = (s32, from_chips)

    grad, delta, new_m, new_v = {}, {}, {}, {}
    for n in names:
        s32, r = reduced[n]
        g = _from_comm(n, kinds[n], _sum_chips(s32, r, pos, name="sum_chips_" + n))
        d, mn, vn = _adam(g[0], weights[n][0], mom_m[n][0], mom_v[n][0], name="adam_" + n)
        grad[n], delta[n], new_m[n], new_v[n] = g, d[None], mn[None], vn[None]
    replicated = [n for n, _ in SMALL]
    rep = lambda src: _pack_small({n: src[n] for n in replicated})
    packed = _adam_replicated(small_tab, rep(weights), rep(mom_m), rep(mom_v), name="adam_replicated")
    for out, pk in zip((grad, delta, new_m, new_v), packed):
        for n in replicated:
            out[n] = _small(pk, n, weights[n].shape)

    loss = lax.psum(loss_part[0, 0], ("x", "y", "c"))
    return (loss, dx0[None], *[grad[n] for n in WEIGHT_ORDER], *[delta[n] for n in WEIGHT_ORDER],
            *[new_m[n] for n in WEIGHT_ORDER], *[new_v[n] for n in WEIGHT_ORDER])
```

```python
import functools
import math

import jax
import jax.numpy as jnp
from jax import lax
from jax.experimental import pallas as pl
from jax.experimental.pallas import tpu as pltpu

F32 = jnp.float32
BF16 = jnp.bfloat16
I32 = jnp.int32

D_MODEL = 1024
GROUPS = 8
GDIM = 128
GBLOCK = 128
CHUNK = 64
HEADS = 16
HEAD_DIM = 64
D_FF = 2816
PLE_DIM = 256
EPS = 1e-6
N_DEV = 8
ATT_SCALE = HEAD_DIM ** -0.5
NEG = -1e30

ADAM_LR = 0.001
ADAM_B1 = 0.9
ADAM_B2 = 0.999
ADAM_EPS = 1e-08
ADAM_WD = 0.01
ADAM_STEP = 10

V7X_VMEM_LIMIT = 48 * 1024 * 1024
MESH = pl.DeviceIdType.MESH

O_F = 2 * 1024 + 3 * 1024
O_G = O_F + HEADS
IN_COLS = O_G + 2 * D_MODEL
MAIN_COLS = IN_COLS - HEADS
IN_SHARD = IN_COLS // N_DEV
IN_SHARD_PAD = 912

SHARDED = (("w_in", "cols"), ("w_branch_a", "rows"), ("w_branch_b", "rows"), ("w_out", "rows"), ("w_up", "cols"),
           ("conv_w", "f32"), ("w_down", "rows"), ("w_ple", "cols"), ("w_ple_gate", "rows"))

SMALL = (("norm_mix_g", 8), ("b_f", 8), ("gmlp_ln_g", 8), ("gmlp_ln_b", 8), ("gmlp_w_s", 128), ("gmlp_b_s", 8),
         ("norm_ffn_g", 8), ("conv_b", 8), ("norm_ple_g", 8), ("norm_final_g", 8))
SMALL_OFF = {}
_o = 0
for _n, _r in SMALL:
    SMALL_OFF[_n] = (_o, _r)
    _o += _r
SMALL_ROWS = _o

WEIGHT_ORDER = ("norm_mix_g", "w_in", "b_f", "gmlp_ln_g", "gmlp_ln_b", "gmlp_w_s", "gmlp_b_s", "w_branch_a",
                "w_branch_b", "w_out", "norm_ffn_g", "w_up", "conv_w", "conv_b", "w_down", "norm_ple_g", "w_ple",
                "w_ple_gate", "norm_final_g")


def _cparams(sem):
    return pltpu.CompilerParams(dimension_semantics=sem, vmem_limit_bytes=V7X_VMEM_LIMIT)


def _gelu(x):
    c = math.sqrt(2.0 / math.pi)
    return 0.5 * x * (1.0 + jnp.tanh(c * (x + 0.044715 * x * x * x)))


def _gelu_and_grad(x):
    c = math.sqrt(2.0 / math.pi)
    t = jnp.tanh(c * (x + 0.044715 * x * x * x))
    g = 0.5 * x * (1.0 + t)
    dg = 0.5 * (1.0 + t) + 0.5 * x * (1.0 - t * t) * (c * (1.0 + 3.0 * 0.044715 * x * x))
    return g, dg


def _sigmoid(x):
    return 1.0 / (1.0 + jnp.exp(-x))


def _dot(a, b, dims):
    return lax.dot_general(a, b, (dims, ((), ())), preferred_element_type=F32)


NN = ((1,), (0,))
NT = ((1,), (1,))
TN = ((0,), (0,))


def _row_tile(rows, most):
    best = None
    for t in range(16, min(rows, most) + 1, 16):
        if rows % t == 0:
            best = t
    return best if best is not None else rows


def _matmul(a, b, *, mode, out_dtype, name, tm=512, tn=512, tk=512, add=None, n=None, b_off=0,
            out_rows=None, o_off=0, into=None):
    if mode == "tn":
        kdim, m = a.shape
    else:
        m, kdim = a.shape
    if n is None:
        n = b.shape[0] if mode == "nt" else b.shape[1]
    tm, tn, tk = min(tm, m), min(tn, n), min(tk, kdim)
    assert m % tm == 0 and n % tn == 0 and kdim % tk == 0, (name, m, n, kdim, tm, tn, tk)
    nk = kdim // tk
    dims = {"nn": NN, "nt": NT, "tn": TN}[mode]

    def finish(r, add_ref, o_ref):
        if add_ref is not None:
            r = add_ref[...].astype(F32) + r
        o_ref[...] = r.astype(out_dtype)

    def body(*refs):
        refs = list(refs)
        a_ref, b_ref = refs[:2]
        add_ref = refs[2] if add is not None else None
        o_ref = refs[2 + (add is not None) + (into is not None)]
        part = _dot(a_ref[...].astype(BF16), b_ref[...].astype(BF16), dims)
        if nk == 1:
            finish(part, add_ref, o_ref)
            return
        acc_ref = refs[-1]
        k = pl.program_id(2)

        @pl.when(k == 0)
        def _():
            acc_ref[...] = part

        @pl.when((k > 0) & (k < nk - 1))
        def _():
            acc_ref[...] += part

        @pl.when(k == nk - 1)
        def _():
            finish(acc_ref[...] + part, add_ref, o_ref)

    a_spec = pl.BlockSpec((tk, tm), lambda i, j, k: (k, i)) if mode == "tn" else pl.BlockSpec((tm, tk), lambda i, j, k: (i, k))
    if mode == "nt":
        b_spec = pl.BlockSpec((tn, tk), lambda i, j, k: (j + b_off, k))
    else:
        b_spec = pl.BlockSpec((tk, tn), lambda i, j, k: (k + b_off, j))
    o_spec = pl.BlockSpec((tm, tn), lambda i, j, k: (i + o_off, j))
    in_specs = [a_spec, b_spec] + ([pl.BlockSpec((tm, tn), lambda i, j, k: (i, j))] if add is not None else [])
    args = (a, b) + ((add,) if add is not None else ())
    aliases = {}
    if into is not None:
        aliases = {len(args): 0}
        in_specs.append(pl.BlockSpec(memory_space=pl.ANY))
        args += (into,)
    return pl.pallas_call(
        body, name=name, grid=(m // tm, n // tn, nk),
        in_specs=in_specs, out_specs=o_spec,
        out_shape=jax.ShapeDtypeStruct((m if out_rows is None else out_rows, n), out_dtype),
        scratch_shapes=[pltpu.VMEM((tm, tn), F32)] if nk > 1 else [],
        input_output_aliases=aliases,
        compiler_params=_cparams(("parallel", "parallel", "arbitrary")),
    )(*args)


def _row_spec(tr, width, col_block=0):
    return pl.BlockSpec((tr, width), lambda i: (i, col_block))


def _full_spec(shape):
    return pl.BlockSpec(shape, lambda i: tuple(0 for _ in shape))


def _rmsnorm_fwd(x, g, *, name, tr=256):
    s, d = x.shape

    def body(x_ref, g_ref, o_ref):
        xv = x_ref[...]
        r = lax.rsqrt(jnp.mean(xv * xv, axis=-1, keepdims=True) + EPS)
        o_ref[...] = ((xv * r) * g_ref[...]).astype(BF16)

    return pl.pallas_call(
        body, name=name, grid=(s // tr,),
        in_specs=[_row_spec(tr, d), _full_spec((1, d))], out_specs=_row_spec(tr, d),
        out_shape=jax.ShapeDtypeStruct((s, d), BF16), compiler_params=_cparams(("parallel",)),
    )(x, g)


def _rmsnorm_bwd(dres, dh, x, g, *, name, tr=256):
    s, d = x.shape

    def body(dres_ref, dh_ref, x_ref, g_ref, dx_ref, dxb_ref, dg_ref):
        i = pl.program_id(0)
        xv = x_ref[...]
        r = lax.rsqrt(jnp.mean(xv * xv, axis=-1, keepdims=True) + EPS)
        xhat = xv * r
        dhv = dh_ref[...].astype(F32)
        dxhat = dhv * g_ref[...]
        dx = dres_ref[...] + r * (dxhat - xhat * jnp.mean(dxhat * xhat, axis=-1, keepdims=True))
        dx_ref[...] = dx
        dxb_ref[...] = dx.astype(BF16)
        dgp = jnp.sum(dhv * xhat, axis=0, keepdims=True)

        @pl.when(i == 0)
        def _():
            dg_ref[...] = dgp

        @pl.when(i > 0)
        def _():
            dg_ref[...] += dgp

    return pl.pallas_call(
        body, name=name, grid=(s // tr,),
        in_specs=[_row_spec(tr, d), _row_spec(tr, d), _row_spec(tr, d), _full_spec((1, d))],
        out_specs=[_row_spec(tr, d), _row_spec(tr, d), _full_spec((1, d))],
        out_shape=[jax.ShapeDtypeStruct((s, d), F32), jax.ShapeDtypeStruct((s, d), BF16),
                   jax.ShapeDtypeStruct((1, d), F32)],
        compiler_params=_cparams(("arbitrary",)),
    )(dres, dh, x, g)


def _final_loss_bwd(x3, target, g, *, name, tr=256):
    s, d = x3.shape

    def body(x_ref, t_ref, g_ref, loss_ref, dx_ref, dg_ref):
        i = pl.program_id(0)
        xv = x_ref[...]
        r = lax.rsqrt(jnp.mean(xv * xv, axis=-1, keepdims=True) + EPS)
        xhat = xv * r
        diff = xhat * g_ref[...] - t_ref[...]
        lp = jnp.zeros((1, 128), F32) + (0.5 / d) * jnp.sum(diff * diff)
        dy = diff * (1.0 / d)
        dxhat = dy * g_ref[...]
        dx_ref[...] = r * (dxhat - xhat * jnp.mean(dxhat * xhat, axis=-1, keepdims=True))
        dgp = jnp.sum(dy * xhat, axis=0, keepdims=True)

        @pl.when(i == 0)
        def _():
            dg_ref[...] = dgp
            loss_ref[...] = lp

        @pl.when(i > 0)
        def _():
            dg_ref[...] += dgp
            loss_ref[...] += lp

    return pl.pallas_call(
        body, name=name, grid=(s // tr,),
        in_specs=[_row_spec(tr, d), _row_spec(tr, d), _full_spec((1, d))],
        out_specs=[_full_spec((1, 128)), _row_spec(tr, d), _full_spec((1, d))],
        out_shape=[jax.ShapeDtypeStruct((1, 128), F32), jax.ShapeDtypeStruct((s, d), F32),
                   jax.ShapeDtypeStruct((1, d), F32)],
        compiler_params=_cparams(("arbitrary",)),
    )(x3, target, g)


def _merge_fwd(ya, yb, zuvg, *, name, tr=256):
    s, d = ya.shape

    def body(ya_ref, yb_ref, ga_ref, gb_ref, o_ref):
        o_ref[...] = (_sigmoid(ga_ref[...]) * ya_ref[...] + _sigmoid(gb_ref[...]) * yb_ref[...]).astype(BF16)

    return pl.pallas_call(
        body, name=name, grid=(s // tr,),
        in_specs=[_row_spec(tr, d), _row_spec(tr, d), _row_spec(tr, d, 2), _row_spec(tr, d, 3)],
        out_specs=_row_spec(tr, d),
        out_shape=jax.ShapeDtypeStruct((s, d), BF16), compiler_params=_cparams(("parallel",)),
    )(ya, yb, zuvg, zuvg)


def _merge_bwd(dm, ya, yb, zuvg, *, name, tr=256):
    s, d = ya.shape

    def body(dm_ref, ya_ref, yb_ref, ga_ref, gb_ref, dya_ref, dyb_ref, dga_ref, dgb_ref):
        dmv = dm_ref[...]
        sa = _sigmoid(ga_ref[...])
        sb = _sigmoid(gb_ref[...])
        dya_ref[...] = (dmv * sa).astype(BF16)
        dyb_ref[...] = (dmv * sb).astype(BF16)
        dga_ref[...] = (dmv * ya_ref[...] * (sa * (1.0 - sa))).astype(BF16)
        dgb_ref[...] = (dmv * yb_ref[...] * (sb * (1.0 - sb))).astype(BF16)

    o = jax.ShapeDtypeStruct((s, d), BF16)
    return pl.pallas_call(
        body, name=name, grid=(s // tr,),
        in_specs=[_row_spec(tr, d)] * 3 + [_row_spec(tr, d, 2), _row_spec(tr, d, 3)], out_specs=[_row_spec(tr, d)] * 4,
        out_shape=[o, o, o, o], compiler_params=_cparams(("parallel",)),
    )(dm, ya, yb, zuvg, zuvg)


def _ple_fwd(x2, ple, gp, *, name, tr=256):
    s, d = x2.shape

    def body(x_ref, ple_ref, gp_ref, o_ref):
        o_ref[...] = x_ref[...] + ple_ref[...] * _sigmoid(gp_ref[...])

    return pl.pallas_call(
        body, name=name, grid=(s // tr,),
        in_specs=[_row_spec(tr, d)] * 3, out_specs=_row_spec(tr, d),
        out_shape=jax.ShapeDtypeStruct((s, d), F32), compiler_params=_cparams(("parallel",)),
    )(x2, ple, gp)


def _ple_bwd(dx3, ple, gp, *, name, tr=256):
    s, d = dx3.shape

    def body(dx_ref, ple_ref, gp_ref, dple_ref, dgp_ref):
        sg = _sigmoid(gp_ref[...])
        dxv = dx_ref[...]
        dple_ref[...] = (dxv * sg).astype(BF16)
        dgp_ref[...] = (dxv * ple_ref[...] * (sg * (1.0 - sg))).astype(BF16)

    o = jax.ShapeDtypeStruct((s, d), BF16)
    return pl.pallas_call(
        body, name=name, grid=(s // tr,),
        in_specs=[_row_spec(tr, d)] * 3, out_specs=[_row_spec(tr, d)] * 2,
        out_shape=[o, o], compiler_params=_cparams(("parallel",)),
    )(dx3, ple, gp)


def _masked_ws(ws_ref, g):
    row = lax.broadcasted_iota(I32, (GBLOCK, GBLOCK), 0)
    col = lax.broadcasted_iota(I32, (GBLOCK, GBLOCK), 1)
    keep = (col // CHUNK) <= (row // CHUNK)
    return jnp.where(keep, ws_ref[g], 0.0), keep


def _layernorm_parts(zv):
    mu = jnp.mean(zv, axis=-1, keepdims=True)
    xc = zv - mu
    rs = lax.rsqrt(jnp.mean(xc * xc, axis=-1, keepdims=True) + EPS)
    return xc * rs, rs


def _gmlp_fwd(zuvg, ln_g, ln_b, w_s, bs_t, *, name):
    s, w = zuvg.shape[0], GROUPS * GDIM

    def body(zu_ref, zv_ref, lng_ref, lnb_ref, ws_ref, bs_ref, a_ref):
        zu = _gelu(zu_ref[...])
        zv = _gelu(zv_ref[...])
        xhat, _ = _layernorm_parts(zv)
        vln = (xhat * lng_ref[...] + lnb_ref[...]).astype(BF16)
        for g in range(GROUPS):
            wm, _ = _masked_ws(ws_ref, g)
            mixed = _dot(wm.astype(BF16), vln[:, g * GDIM:(g + 1) * GDIM], NN) + bs_ref[:, g:g + 1]
            a_ref[:, g * GDIM:(g + 1) * GDIM] = (zu[:, g * GDIM:(g + 1) * GDIM] * mixed).astype(BF16)

    return pl.pallas_call(
        body, name=name, grid=(s // GBLOCK,),
        in_specs=[_row_spec(GBLOCK, w, 0), _row_spec(GBLOCK, w, 1), _full_spec((1, w)), _full_spec((1, w)),
                  _full_spec((GROUPS, GBLOCK, GBLOCK)), _full_spec((GBLOCK, 128))],
        out_specs=_row_spec(GBLOCK, w),
        out_shape=jax.ShapeDtypeStruct((s, w), BF16), compiler_params=_cparams(("parallel",)),
    )(zuvg, zuvg, ln_g, ln_b, w_s, bs_t)


def _gmlp_bwd(da, zuvg, ln_g, ln_b, w_s, bs_t, *, name):
    s, w = zuvg.shape[0], GROUPS * GDIM

    def body(da_ref, zu_ref, zv_ref, lng_ref, lnb_ref, ws_ref, bs_ref,
             dzu_ref, dzv_ref, dws_ref, dbs_ref, dlng_ref, dlnb_ref, dvln_ref):
        i = pl.program_id(0)
        zu, dzu_g = _gelu_and_grad(zu_ref[...])
        zv, dzv_g = _gelu_and_grad(zv_ref[...])
        xhat, rs = _layernorm_parts(zv)
        vln = (xhat * lng_ref[...] + lnb_ref[...]).astype(BF16)
        dav = da_ref[...].astype(F32)
        lane = lax.broadcasted_iota(I32, (GBLOCK, 128), 1)
        dbs = jnp.zeros((GBLOCK, 128), F32)

        @pl.when(i == 0)
        def _():
            dws_ref[...] = jnp.zeros_like(dws_ref)

        for g in range(GROUPS):
            sl = slice(g * GDIM, (g + 1) * GDIM)
            wm, keep = _masked_ws(ws_ref, g)
            wmb = wm.astype(BF16)
            vg = vln[:, sl]
            mixed = _dot(wmb, vg, NN) + bs_ref[:, g:g + 1]
            dag = dav[:, sl]
            dzu_ref[:, sl] = (dag * mixed * dzu_g[:, sl]).astype(BF16)
            dmix = dag * zu[:, sl]
            dmb = dmix.astype(BF16)
            dws_ref[g] += jnp.where(keep, _dot(dmb, vg, NT), 0.0)
            dbs = jnp.where(lane == g, jnp.sum(dmix, axis=1, keepdims=True), dbs)
            dvln_ref[:, sl] = _dot(wmb, dmb, TN)
        dvln = dvln_ref[...]
        dxhat = dvln * lng_ref[...]
        dzv = rs * (dxhat - jnp.mean(dxhat, axis=-1, keepdims=True)
                    - xhat * jnp.mean(dxhat * xhat, axis=-1, keepdims=True))
        dzv_ref[...] = (dzv * dzv_g).astype(BF16)
        dlng = jnp.sum(dvln * xhat, axis=0, keepdims=True)
        dlnb = jnp.sum(dvln, axis=0, keepdims=True)

        @pl.when(i == 0)
        def _():
            dbs_ref[...] = dbs
            dlng_ref[...] = dlng
            dlnb_ref[...] = dlnb

        @pl.when(i > 0)
        def _():
            dbs_ref[...] += dbs
            dlng_ref[...] += dlng
            dlnb_ref[...] += dlnb

    return pl.pallas_call(
        body, name=name, grid=(s // GBLOCK,),
        in_specs=[_row_spec(GBLOCK, w), _row_spec(GBLOCK, w, 0), _row_spec(GBLOCK, w, 1), _full_spec((1, w)),
                  _full_spec((1, w)), _full_spec((GROUPS, GBLOCK, GBLOCK)), _full_spec((GBLOCK, 128))],
        out_specs=[_row_spec(GBLOCK, w), _row_spec(GBLOCK, w), _full_spec((GROUPS, GBLOCK, GBLOCK)),
                   _full_spec((GBLOCK, 128)), _full_spec((1, w)), _full_spec((1, w))],
        out_shape=[jax.ShapeDtypeStruct((s, w), BF16), jax.ShapeDtypeStruct((s, w), BF16),
                   jax.ShapeDtypeStruct((GROUPS, GBLOCK, GBLOCK), F32), jax.ShapeDtypeStruct((GBLOCK, 128), F32),
                   jax.ShapeDtypeStruct((1, w), F32), jax.ShapeDtypeStruct((1, w), F32)],
        scratch_shapes=[pltpu.VMEM((GBLOCK, w), F32)],
        compiler_params=_cparams(("arbitrary",)),
    )(da, zuvg, zuvg, ln_g, ln_b, w_s, bs_t)


def _shift_down(u, k):
    row = lax.broadcasted_iota(I32, u.shape, 0)
    return jnp.where(row >= k, pltpu.roll(u, k, 0), 0.0)


def _shift_up(u, k):
    s = u.shape[0]
    row = lax.broadcasted_iota(I32, u.shape, 0)
    return jnp.where(row < s - k, pltpu.roll(u, s - k, 0), 0.0)


def _conv(u, w_ref, b_ref):
    return b_ref[...] + w_ref[0:1, :] * _shift_down(u, 2) + w_ref[1:2, :] * _shift_down(u, 1) + w_ref[2:3, :] * u


def _conv_specs(s, f, tc):
    nc = f // tc
    half = lambda rows: [pl.BlockSpec((rows, tc), lambda j: (0, j)), pl.BlockSpec((rows, tc), lambda j: (0, nc + j))]
    return half(s), half(3), half(1)


def _convglu_fwd(up, conv_w, conv_b, *, name, tc=256):
    s, f = up.shape[0], up.shape[1] // 2
    up_specs, w_specs, b_specs = _conv_specs(s, f, tc)

    def body(ua_ref, ug_ref, wa_ref, wg_ref, ba_ref, bg_ref, o_ref):
        ca = _conv(ua_ref[...], wa_ref, ba_ref)
        cg = _conv(ug_ref[...], wg_ref, bg_ref)
        o_ref[...] = (_gelu(ca) * cg).astype(BF16)

    return pl.pallas_call(
        body, name=name, grid=(f // tc,),
        in_specs=up_specs + w_specs + b_specs, out_specs=up_specs[0],
        out_shape=jax.ShapeDtypeStruct((s, f), BF16), compiler_params=_cparams(("parallel",)),
    )(up, up, conv_w, conv_w, conv_b, conv_b)


def _convglu_bwd(dact, up, conv_w, conv_b, *, name, tc=256):
    s, f = up.shape[0], up.shape[1] // 2
    up_specs, w_specs, b_specs = _conv_specs(s, f, tc)

    def half(dc, u, w_ref, du_ref, dw_ref, db_ref):
        db_ref[...] = jnp.sum(dc, axis=0, keepdims=True)
        dw_ref[0:1, :] = jnp.sum(dc * _shift_down(u, 2), axis=0, keepdims=True)
        dw_ref[1:2, :] = jnp.sum(dc * _shift_down(u, 1), axis=0, keepdims=True)
        dw_ref[2:3, :] = jnp.sum(dc * u, axis=0, keepdims=True)
        du = w_ref[2:3, :] * dc + w_ref[1:2, :] * _shift_up(dc, 1) + w_ref[0:1, :] * _shift_up(dc, 2)
        du_ref[...] = du.astype(BF16)

    def body(d_ref, ua_ref, ug_ref, wa_ref, wg_ref, ba_ref, bg_ref,
             dua_ref, dug_ref, dwa_ref, dwg_ref, dba_ref, dbg_ref):
        ua = ua_ref[...]
        ug = ug_ref[...]
        ca = _conv(ua, wa_ref, ba_ref)
        cg = _conv(ug, wg_ref, bg_ref)
        ga, dga = _gelu_and_grad(ca)
        dv = d_ref[...].astype(F32)
        half(dv * cg * dga, ua, wa_ref, dua_ref, dwa_ref, dba_ref)
        half(dv * ga, ug, wg_ref, dug_ref, dwg_ref, dbg_ref)

    col, w3, b1 = up_specs[0], w_specs[0], b_specs[0]
    return pl.pallas_call(
        body, name=name, grid=(f // tc,),
        in_specs=[col] + up_specs + w_specs + b_specs, out_specs=[col, col, w3, w3, b1, b1],
        out_shape=[jax.ShapeDtypeStruct((s, f), BF16), jax.ShapeDtypeStruct((s, f), BF16),
                   jax.ShapeDtypeStruct((3, f), F32), jax.ShapeDtypeStruct((3, f), F32),
                   jax.ShapeDtypeStruct((1, f), F32), jax.ShapeDtypeStruct((1, f), F32)],
        compiler_params=_cparams(("parallel",)),
    )(dact, up, up, conv_w, conv_w, conv_b, conv_b)


def _tri_dot(tri, x):
    b0 = x.astype(BF16)
    r1 = x - b0.astype(F32)
    b1 = r1.astype(BF16)
    b2 = (r1 - b1.astype(F32)).astype(BF16)
    return _dot(tri, b0, NN) + _dot(tri, b1, NN) + _dot(tri, b2, NN)


def _log_sigmoid(x):
    return jnp.minimum(x, 0.0) - jnp.log(1.0 + jnp.exp(-jnp.abs(x)))


def _expand_heads(col16, rows):
    head_of_lane = lax.broadcasted_iota(I32, (rows, HEADS * HEAD_DIM), 1) // HEAD_DIM
    out = jnp.zeros((rows, HEADS * HEAD_DIM), F32)
    for h in range(HEADS):
        out = jnp.where(head_of_lane == h, col16[:, h:h + 1], out)
    return out


def _forget_cumsum(f_logit, b_f, *, name):
    s = f_logit.shape[0]
    nb = s // 128

    def body(f_ref, b_ref, cqe_ref):
        row = lax.broadcasted_iota(I32, (128, 128), 0)
        col = lax.broadcasted_iota(I32, (128, 128), 1)
        tri = (col <= row).astype(BF16)

        def step(n, carry):
            r0 = pl.multiple_of(n * 128, 128)
            lf = _log_sigmoid(f_ref[pl.ds(r0, 128), :] + b_ref[...])
            cum = _tri_dot(tri, lf) + carry
            cqe_ref[pl.ds(r0, 128), :] = _expand_heads(cum, 128)
            return cum[127:128, :]

        lax.fori_loop(0, nb, step, jnp.zeros((1, 128), F32))

    return pl.pallas_call(
        body, name=name, grid=(1,),
        in_specs=[_full_spec((s, 128)), _full_spec((1, 128))],
        out_specs=_full_spec((s, HEADS * HEAD_DIM)),
        out_shape=jax.ShapeDtypeStruct((s, HEADS * HEAD_DIM), F32),
        compiler_params=_cparams(("arbitrary",)),
    )(f_logit, b_f)


def _forget_bwd(dcq16, sum_q16, f_logit, b_f, *, name):
    s = f_logit.shape[0]
    nb = s // 128

    def body(a_ref, k_ref, f_ref, b_ref, df_ref, db_ref):
        row = lax.broadcasted_iota(I32, (128, 128), 0)
        col = lax.broadcasted_iota(I32, (128, 128), 1)
        tri_rev = (col >= row).astype(BF16)

        def step(m, carry):
            suffix, dbsum = carry
            n = nb - 1 - m
            r0 = pl.multiple_of(n * 128, 128)
            dcum = a_ref[pl.ds(r0, 128), :] - k_ref[pl.ds(r0, 128), :]
            dlf = _tri_dot(tri_rev, dcum) + suffix
            df = dlf * _sigmoid(-(f_ref[pl.ds(r0, 128), :] + b_ref[...]))
            df_ref[pl.ds(r0, 128), :] = df.astype(BF16)
            return dlf[0:1, :], dbsum + jnp.sum(df, axis=0, keepdims=True)

        _, dbsum = lax.fori_loop(0, nb, step, (jnp.zeros((1, 128), F32), jnp.zeros((1, 128), F32)))
        db_ref[...] = dbsum

    return pl.pallas_call(
        body, name=name, grid=(1,),
        in_specs=[_full_spec((s, 128))] * 3 + [_full_spec((1, 128))],
        out_specs=[_full_spec((s, 128)), _full_spec((1, 128))],
        out_shape=[jax.ShapeDtypeStruct((s, 128), BF16), jax.ShapeDtypeStruct((1, 128), F32)],
        compiler_params=_cparams(("arbitrary",)),
    )(dcq16, sum_q16, f_logit, b_f)


ATT_T = 256


def _head_lanes(rows):
    return lax.broadcasted_iota(I32, (rows, 128), 1) < HEAD_DIM


def _bf16_pieces(c):
    p0 = c.astype(BF16).astype(F32)
    r = c - p0
    p1 = r.astype(BF16).astype(F32)
    p2 = (r - p1).astype(BF16).astype(F32)
    return p0, p1, p2


def _col_reduce(x, op):
    rows = x.shape[0]
    while rows > 8:
        rows //= 2
        x = op(x[:rows], x[rows:])
    return jnp.max(x, axis=0, keepdims=True) if op is jnp.maximum else jnp.sum(x, axis=0, keepdims=True)


def _attn_prep(qkv, cqe, *, name):
    s = qkv.shape[0]
    npair = HEADS // 2

    def body(q_ref, k_ref, v_ref, c_ref, qa_ref, ka_ref, vt_ref):
        rows = 128
        lane = lax.broadcasted_iota(I32, (rows, 128), 1)

        def chunk(n, _):
            r0 = pl.multiple_of(n * rows, rows)
            sl = pl.ds(r0, rows)
            qv = q_ref[sl, :].astype(F32) * ATT_SCALE
            kv = k_ref[sl, :].astype(F32)
            for e in range(2):
                mine = (lane < HEAD_DIM) if e == 0 else (lane >= HEAD_DIM)
                base = HEAD_DIM * (1 - e)
                p0, p1, p2 = _bf16_pieces(c_ref[sl, HEAD_DIM * e:HEAD_DIM * e + 1])
                ones_hi = jnp.where((lane >= base + 3) & (lane < base + 6), 1.0, 0.0)
                ones_lo = jnp.where((lane >= base) & (lane < base + 3), 1.0, 0.0)
                qa = jnp.where(mine, qv, jnp.where(lane == base, p0, jnp.where(lane == base + 1, p1,
                               jnp.where(lane == base + 2, p2, ones_hi))))
                ka = jnp.where(mine, kv, jnp.where(lane == base + 3, -p0, jnp.where(lane == base + 4, -p1,
                               jnp.where(lane == base + 5, -p2, ones_lo))))
                qa_ref[e, sl, :] = qa.astype(BF16)
                ka_ref[e, sl, :] = ka.astype(BF16)
            vt_ref[0, :, sl] = v_ref[sl, :].astype(F32).T.astype(BF16)
            return 0

        lax.fori_loop(0, s // rows, chunk, 0)

    pair = pl.BlockSpec((2, s, 128), lambda hp: (hp, 0, 0))
    return pl.pallas_call(
        body, name=name, grid=(npair,),
        in_specs=[pl.BlockSpec((s, 128), lambda hp: (0, hp)), pl.BlockSpec((s, 128), lambda hp: (0, npair + hp)),
                  pl.BlockSpec((s, 128), lambda hp: (0, 2 * npair + hp)), pl.BlockSpec((s, 128), lambda hp: (0, hp))],
        out_specs=[pair, pair, pl.BlockSpec((1, 128, s), lambda hp: (hp, 0, 0))],
        out_shape=[jax.ShapeDtypeStruct((HEADS, s, 128), BF16), jax.ShapeDtypeStruct((HEADS, s, 128), BF16),
                   jax.ShapeDtypeStruct((npair, 128, s), BF16)],
        compiler_params=_cparams(("parallel",)),
    )(qkv, qkv, qkv, cqe)


def _attn_fwd(qa, ka, vt, shards=(), *, name):
    s = qa.shape[1]
    t = ATT_T
    nq = s // t
    n = len(shards)
    npair = HEADS // 2

    def body(*refs):
        qa_ref, ka_ref, vt_ref = refs[:3]
        x_refs = refs[3:3 + n]
        o_ref, lse_ref = refs[3 + n:5 + n]
        out_refs = refs[5 + n:5 + 2 * n]
        sems = refs[5 + 2 * n:]
        hp = pl.program_id(0)
        i = pl.program_id(1)
        if n:
            @pl.when((hp == 0) & (i == 0))
            def _():
                local, sends, _ = _gather1_copies(x_refs, out_refs, *sems, with_arrivals=False)
                for cp in local + sends:
                    cp.start()

        krow = lax.broadcasted_iota(I32, (t, t), 0)
        qcol = lax.broadcasted_iota(I32, (t, t), 1)
        sub = lax.broadcasted_iota(I32, (128, t), 0)
        row8 = lax.broadcasted_iota(I32, (8, t), 0)
        qbs = (qa_ref[0], qa_ref[1])
        tk = 2 * t

        def step(j, carry, diag):
            c0 = pl.multiple_of(j * tk, tk)
            vtb = vt_ref[0, :, pl.ds(c0, tk)]
            sts = [_dot(ka_ref[e, pl.ds(c0, tk), :], qbs[e], NT) for e in range(2)]
            if diag:
                keep = (lax.broadcasted_iota(I32, (tk, t), 0) - lax.broadcasted_iota(I32, (tk, t), 1)) <= t * (i % 2)
                sts = [jnp.where(keep, st, NEG) for st in sts]
            pts, stats = [], []
            for e in range(2):
                m, l, _ = carry[e]
                m_new = jnp.maximum(m, _col_reduce(sts[e], jnp.maximum))
                alpha = jnp.exp(m - m_new)
                pt = jnp.exp(sts[e] - m_new)
                stats.append((m_new, alpha, alpha * l + _col_reduce(pt, jnp.add)))
                pts.append(pt.astype(BF16))
            pvs = [_dot(vtb, pts[e], NN) for e in range(2)]
            return tuple((stats[e][0], stats[e][2], stats[e][1] * carry[e][2] + pvs[e]) for e in range(2))

        init = (jnp.full((1, t), NEG, F32), jnp.zeros((1, t), F32), jnp.zeros((128, t), F32))
        carry = lax.fori_loop(0, i // 2, functools.partial(step, diag=False), (init, init))
        (m0, l0, acc0), (m1, l1, acc1) = step(i // 2, carry, True)
        o_pair = jnp.where(sub < HEAD_DIM, acc0 / l0, acc1 / l1)
        o_ref[...] = o_pair.T.astype(BF16)
        lse_ref[0] = jnp.where(row8 == 0, m0 + jnp.log(l0), jnp.where(row8 == 1, m1 + jnp.log(l1), 0.0))
        if n:
            @pl.when((hp == npair - 1) & (i == nq - 1))
            def _():
                local, sends, arrivals = _gather1_copies(x_refs, out_refs, *sems)
                for cp in arrivals:
                    cp.wait_recv()
                for cp in sends:
                    cp.wait_send()
                for cp in local:
                    cp.wait()

    out = pl.pallas_call(
        body, name=name, grid=(npair, nq),
        in_specs=[pl.BlockSpec((2, t, 128), lambda hp, i: (hp, i, 0)), pl.BlockSpec((2, s, 128), lambda hp, i: (hp, 0, 0)),
                  pl.BlockSpec((1, 128, s), lambda hp, i: (hp, 0, 0))] + [ANY] * n,
        out_specs=[pl.BlockSpec((t, 128), lambda hp, i: (i, hp)), pl.BlockSpec((1, 8, t), lambda hp, i: (hp, 0, i))]
        + [ANY] * n,
        out_shape=[jax.ShapeDtypeStruct((s, HEADS * HEAD_DIM), BF16), jax.ShapeDtypeStruct((npair, 8, s), F32)]
        + [jax.ShapeDtypeStruct((N_DEV,) + a.shape, a.dtype) for a in shards],
        scratch_shapes=_gather1_scratch(n) if n else [],
        compiler_params=_cparams(("arbitrary", "arbitrary")),
    )(qa, ka, vt, *shards)
    return out[0], out[1], list(out[2:])


def _attn_delta(do, o, *, name):
    s = do.shape[0]

    def body(do_ref, o_ref, d_ref):
        prod = do_ref[...].astype(F32) * o_ref[...].astype(F32)
        row = lax.broadcasted_iota(I32, (8, 128), 0)
        lane = lax.broadcasted_iota(I32, (8, 128), 1)
        sel = ((row == 0) & (lane < HEAD_DIM) | (row == 1) & (lane >= HEAD_DIM)).astype(BF16)
        p0, p1, p2 = _bf16_pieces(prod)
        d_ref[0] = (_dot(sel, p0.astype(BF16), NT) + _dot(sel, p1.astype(BF16), NT)) + _dot(sel, p2.astype(BF16), NT)

    pair = pl.BlockSpec((s, 128), lambda hp: (0, hp))
    return pl.pallas_call(
        body, name=name, grid=(HEADS // 2,), in_specs=[pair, pair],
        out_specs=pl.BlockSpec((1, 8, s), lambda hp: (hp, 0, 0)),
        out_shape=jax.ShapeDtypeStruct((HEADS // 2, 8, s), F32), compiler_params=_cparams(("parallel",)),
    )(do, o)


def _attn_bwd(qa, ka, qkv, do, lse3, delta3, psums=(), *, name):
    s = qa.shape[1]
    t = ATT_T
    nb = s // t
    npair = HEADS // 2
    n = len(psums)

    def body(*refs):
        qa_ref, ka_ref, v_ref, do_ref, lse_ref, delta_ref = refs[:6]
        p_refs = refs[6:6 + n]
        dq_ref, dk_ref, dv_ref, aux_ref, dcq_ref = refs[6 + n:11 + n]
        rp_refs = refs[11 + n:11 + 2 * n]
        dqt = refs[11 + 2 * n]
        sems = refs[12 + 2 * n:]
        hp = pl.program_id(0)
        if n:
            @pl.when(hp == 0)
            def _():
                for cp in _chip_scatter_copies(p_refs, rp_refs, *sems):
                    cp.start()

        first = _head_lanes(t)
        lane = lax.broadcasted_iota(I32, (t, 128), 1)
        dqt[...] = jnp.zeros_like(dqt)

        @pl.when(hp == 0)
        def _():
            aux_ref[...] = jnp.zeros_like(aux_ref)

        krow = lax.broadcasted_iota(I32, (t, t), 0)
        qcol = lax.broadcasted_iota(I32, (t, t), 1)

        def key_block(j, _):
            c0 = pl.multiple_of(j * t, t)
            vb = v_ref[pl.ds(c0, t), :]
            kbs = (ka_ref[0, pl.ds(c0, t), :], ka_ref[1, pl.ds(c0, t), :])
            kbts = tuple(kb.astype(F32).T.astype(BF16) for kb in kbs)
            vhs = (jnp.where(first, vb, jnp.zeros_like(vb)), jnp.where(first, jnp.zeros_like(vb), vb))

            def query_block(i, carry, diag):
                r0 = pl.multiple_of(i * t, t)
                dob = do_ref[pl.ds(r0, t), :]
                qbs = [qa_ref[e, pl.ds(r0, t), :] for e in range(2)]
                sts = [_dot(kbs[e], qbs[e], NT) for e in range(2)]
                dpts = [_dot(vhs[e], dob, NT) for e in range(2)]
                ptbs, dsbs = [], []
                for e in range(2):
                    st = jnp.where(krow <= qcol, sts[e], NEG) if diag else sts[e]
                    pt = jnp.exp(st - lse_ref[0, e:e + 1, pl.ds(r0, t)])
                    dsbs.append((pt * (dpts[e] - delta_ref[0, e:e + 1, pl.ds(r0, t)])).astype(BF16))
                    ptbs.append(pt.astype(BF16))
                out = []
                for e in range(2):
                    dk_a, dv_a = carry[e]
                    dv_a = dv_a + _dot(ptbs[e], dob, NN)
                    dk_a = dk_a + _dot(dsbs[e], qbs[e], NN)
                    dqt[e, :, pl.ds(r0, t)] += _dot(kbts[e], dsbs[e], NN)
                    out.append((dk_a, dv_a))
                return tuple(out)

            zero = jnp.zeros((t, 128), F32)
            carry = query_block(j, ((zero, zero), (zero, zero)), True)
            (dk0, dv0), (dk1, dv1) = lax.fori_loop(j + 1, nb, functools.partial(query_block, diag=False), carry)
            dk_ref[pl.ds(c0, t), :] = jnp.where(first, dk0, dk1).astype(BF16)
            dv_ref[pl.ds(c0, t), :] = jnp.where(first, dv0, dv1).astype(BF16)
            sum_q = jnp.where(lane == 2 * hp, dk0[:, HEAD_DIM + 3:HEAD_DIM + 4],
                              jnp.where(lane == 2 * hp + 1, dk1[:, 3:4], aux_ref[pl.ds(c0, t), :]))
            aux_ref[pl.ds(c0, t), :] = sum_q
            return 0

        lax.fori_loop(0, nb, key_block, 0)
        sub = lax.broadcasted_iota(I32, (128, s), 0)
        row8 = lax.broadcasted_iota(I32, (8, s), 0)
        dq_ref[...] = (jnp.where(sub < HEAD_DIM, dqt[0], dqt[1]) * ATT_SCALE).T.astype(BF16)
        dcq_ref[0] = jnp.where(row8 == 0, dqt[0, HEAD_DIM:HEAD_DIM + 1, :], jnp.where(row8 == 1, dqt[1, 0:1, :], 0.0))
        if n:
            @pl.when(hp == npair - 1)
            def _():
                for cp in _chip_scatter_copies(p_refs, rp_refs, *sems):
                    cp.wait()

    def pair_cols(off):
        return pl.BlockSpec((s, 128), lambda hp: (0, off + hp))

    heads = pl.BlockSpec((2, s, 128), lambda hp: (hp, 0, 0))
    rows = pl.BlockSpec((1, 8, s), lambda hp: (hp, 0, 0))
    wide = jax.ShapeDtypeStruct((s, HEADS * HEAD_DIM), BF16)
    out = pl.pallas_call(
        body, name=name, grid=(npair,),
        in_specs=[heads, heads, pair_cols(2 * npair), pair_cols(0), rows, rows] + [ANY] * n,
        out_specs=[pair_cols(0), pair_cols(0), pair_cols(0), pl.BlockSpec((s, 128), lambda hp: (0, 0)), rows] + [ANY] * n,
        out_shape=[wide, wide, wide, jax.ShapeDtypeStruct((s, 128), F32), jax.ShapeDtypeStruct((npair, 8, s), F32)]
        + [jax.ShapeDtypeStruct((3,) + a.shape[1:], a.dtype) for a in psums],
        scratch_shapes=[pltpu.VMEM((2, 128, s), F32)]
        + ([pltpu.SemaphoreType.DMA((3 * n,)), pltpu.SemaphoreType.DMA((3 * n,))] if n else []),
        compiler_params=_cparams(("arbitrary",)),
    )(qa, ka, qkv, do, lse3, delta3, *psums)
    return out[:5], list(out[5:])


def _adam_math(w, g, m, v):
    m = ADAM_B1 * m + (1.0 - ADAM_B1) * g
    v = ADAM_B2 * v + (1.0 - ADAM_B2) * (g * g)
    m_hat = m / (1.0 - ADAM_B1 ** ADAM_STEP)
    v_hat = v / (1.0 - ADAM_B2 ** ADAM_STEP)
    delta = -ADAM_LR * (m_hat / (jnp.sqrt(v_hat) + ADAM_EPS) + ADAM_WD * w)
    return delta, m, v


def _sum_pairs(keep, recv, pos, *, name):
    _, r, c = recv.shape
    tr = _row_tile(r, 512)

    def body(pos_ref, a_ref, b_ref, o32_ref, o16_ref):
        tot = a_ref[...].astype(F32) + b_ref[...].astype(F32)
        o32_ref[...] = tot
        o16_ref[...] = tot.astype(BF16)

    out = pl.BlockSpec((1, tr, c), lambda q, i, pos: (q, i, 0))
    grid_spec = pltpu.PrefetchScalarGridSpec(
        num_scalar_prefetch=1, grid=(4, r // tr),
        in_specs=[pl.BlockSpec((1, tr, c), lambda q, i, pos: (2 * q + pos[2], i, 0)), out],
        out_specs=[out, out])
    return pl.pallas_call(
        body, name=name, grid_spec=grid_spec,
        out_shape=[jax.ShapeDtypeStruct((4, r, c), F32), jax.ShapeDtypeStruct((4, r, c), BF16)],
        compiler_params=_cparams(("parallel", "parallel")),
    )(pos, keep, recv)


def _adam_sharded(psum, recv, w, m, v, pos, *, name):
    r, c = w.shape
    tr = _row_tile(r, 320)

    def body(pos_ref, p_ref, r_ref, w_ref, m_ref, v_ref, g_ref, d_ref, mo_ref, vo_ref):
        g = p_ref[0] + r_ref[0].astype(F32) + r_ref[1].astype(F32) + r_ref[2].astype(F32)
        delta, mn, vn = _adam_math(w_ref[...], g, m_ref[...], v_ref[...])
        g_ref[...] = g
        d_ref[...] = delta
        mo_ref[...] = mn
        vo_ref[...] = vn

    row = pl.BlockSpec((tr, c), lambda i, pos: (i, 0))
    grid_spec = pltpu.PrefetchScalarGridSpec(
        num_scalar_prefetch=1, grid=(r // tr,),
        in_specs=[pl.BlockSpec((1, tr, c), lambda i, pos: (2 * pos[0] + pos[1], i, 0)),
                  pl.BlockSpec((3, tr, c), lambda i, pos: (0, i, 0)), row, row, row],
        out_specs=[row, row, row, row])
    o = jax.ShapeDtypeStruct((r, c), F32)
    return pl.pallas_call(
        body, name=name, grid_spec=grid_spec, out_shape=[o, o, o, o],
        compiler_params=_cparams(("parallel",)),
    )(pos, psum, recv, w, m, v)


def _adam_replicated(chip_sums, w, m, v, *, name):
    r = w.shape[0]

    def body(s_ref, w_ref, m_ref, v_ref, g_ref, d_ref, mo_ref, vo_ref):
        g = ((s_ref[0] + s_ref[1]) + s_ref[2]) + s_ref[3]
        delta, mn, vn = _adam_math(w_ref[...], g, m_ref[...], v_ref[...])
        g_ref[...] = g
        d_ref[...] = delta
        mo_ref[...] = mn
        vo_ref[...] = vn

    o = jax.ShapeDtypeStruct((r, 1024), F32)
    full = _full_spec((r, 1024))
    return pl.pallas_call(
        body, name=name, grid=(1,),
        in_specs=[_full_spec((4, r, 1024)), full, full, full], out_specs=[full] * 4, out_shape=[o] * 4,
        compiler_params=_cparams(("arbitrary",)),
    )(chip_sums, w, m, v)


def _pair_sum_small(mine, theirs, *, name):
    def body(a_ref, b_ref, o_ref):
        o_ref[...] = a_ref[...] + b_ref[...]

    full = _full_spec(mine.shape)
    return pl.pallas_call(
        body, name=name, grid=(1,), in_specs=[full, full], out_specs=full,
        out_shape=jax.ShapeDtypeStruct(mine.shape, F32), compiler_params=_cparams(("arbitrary",)),
    )(mine, theirs)


ANY = pl.BlockSpec(memory_space=pl.ANY)
OTHER_CHIPS = ((1, 0), (0, 1), (1, 1))


def _gather1_copies(x_refs, out_refs, send_sems, recv_sems, local_sems, with_arrivals=True):
    x, y, c = lax.axis_index("x"), lax.axis_index("y"), lax.axis_index("c")
    peers = [(x, y, 1 - c)] + [(x ^ fx, y ^ fy, c) for fx, fy in OTHER_CHIPS]
    local, sends, arrivals = [], [], []
    for t, (x_ref, out_ref) in enumerate(zip(x_refs, out_refs)):
        local.append(pltpu.make_async_copy(x_ref, out_ref.at[4 * x + 2 * y + c], local_sems.at[t]))
        for k, (px, py, pc) in enumerate(peers):
            sems = dict(send_sem=send_sems.at[4 * t + k], recv_sem=recv_sems.at[4 * t + k],
                        device_id=(px, py, pc), device_id_type=MESH)
            sends.append(pltpu.make_async_remote_copy(src_ref=x_ref, dst_ref=out_ref.at[4 * x + 2 * y + c], **sems))
            if with_arrivals:
                arrivals.append(
                    pltpu.make_async_remote_copy(src_ref=x_ref, dst_ref=out_ref.at[4 * px + 2 * py + pc], **sems))
    return local, sends, arrivals


def _gather1_scratch(n):
    return [pltpu.SemaphoreType.DMA((4 * n,)), pltpu.SemaphoreType.DMA((4 * n,)), pltpu.SemaphoreType.DMA((n,))]


def _gather2(gathered, *, name):
    n = len(gathered)

    def body(*refs):
        in_refs, g_refs = refs[:n], refs[n:2 * n]
        send_sems, recv_sems = refs[2 * n:]
        x, y, c = lax.axis_index("x"), lax.axis_index("y"), lax.axis_index("c")
        copies, arrivals = [], []
        for t in range(n):
            for j, (fx, fy) in enumerate(OTHER_CHIPS):
                px, py = x ^ fx, y ^ fy
                sems = dict(send_sem=send_sems.at[3 * t + j], recv_sem=recv_sems.at[3 * t + j],
                            device_id=(x, y, 1 - c), device_id_type=MESH)
                mine, theirs = 4 * px + 2 * py + c, 4 * px + 2 * py + (1 - c)
                copies.append(pltpu.make_async_remote_copy(src_ref=in_refs[t].at[mine], dst_ref=g_refs[t].at[mine], **sems))
                arrivals.append(pltpu.make_async_remote_copy(src_ref=in_refs[t].at[mine], dst_ref=g_refs[t].at[theirs], **sems))
        for cp in copies:
            cp.start()
        for cp in arrivals:
            cp.wait_recv()
        for cp in copies:
            cp.wait_send()

    return pl.pallas_call(
        body, name=name, out_shape=[jax.ShapeDtypeStruct(a.shape, a.dtype) for a in gathered],
        in_specs=[ANY] * n, out_specs=[ANY] * n, input_output_aliases={t: t for t in range(n)},
        scratch_shapes=[pltpu.SemaphoreType.DMA((3 * n,)), pltpu.SemaphoreType.DMA((3 * n,))],
    )(*gathered)


def _chip_scatter_copies(p_refs, rp_refs, send_sems, recv_sems):
    x, y, c = lax.axis_index("x"), lax.axis_index("y"), lax.axis_index("c")
    copies = []
    for t, (p_ref, rp_ref) in enumerate(zip(p_refs, rp_refs)):
        for k, (fx, fy) in enumerate(OTHER_CHIPS):
            px, py = x ^ fx, y ^ fy
            copies.append(pltpu.make_async_remote_copy(
                src_ref=p_ref.at[2 * px + py], dst_ref=rp_ref.at[k], send_sem=send_sems.at[3 * t + k],
                recv_sem=recv_sems.at[3 * t + k], device_id=(px, py, c), device_id_type=MESH))
    return copies


def _allgather(shards, *, name):
    n = len(shards)

    def body(*refs):
        x_refs, out_refs = refs[:n], refs[n:2 * n]
        send_sems, recv_sems, local_sems = refs[2 * n:]
        x, y, c = lax.axis_index("x"), lax.axis_index("y"), lax.axis_index("c")
        me, sibling = (x, y, c), (x, y, 1 - c)
        chips = [(x ^ fx, y ^ fy) for fx, fy in OTHER_CHIPS]

        def copy(t, k, block, to, from_input=False):
            px, py, pc = block
            slab = out_refs[t].at[4 * px + 2 * py + pc]
            return pltpu.make_async_remote_copy(
                src_ref=x_refs[t] if from_input else slab, dst_ref=slab,
                send_sem=send_sems.at[7 * t + k], recv_sem=recv_sems.at[7 * t + k], device_id=to, device_id_type=MESH)

        mine = [pltpu.make_async_copy(x_refs[t], out_refs[t].at[4 * x + 2 * y + c], local_sems.at[t]) for t in range(n)]
        for cp in mine:
            cp.start()
        first = []
        for t in range(n):
            first.append(copy(t, 0, me, sibling, from_input=True))
            first += [copy(t, 1 + j, me, (*chip, c), from_input=True) for j, chip in enumerate(chips)]
        for cp in first:
            cp.start()
        passed = []
        for j, chip in enumerate(chips):
            for t in range(n):
                copy(t, 1 + j, (*chip, c), me).wait_recv()
                fwd = copy(t, 4 + j, (*chip, c), sibling)
                fwd.start()
                passed.append(fwd)
        for t in range(n):
            copy(t, 0, sibling, me).wait_recv()
            for j, chip in enumerate(chips):
                copy(t, 4 + j, (*chip, 1 - c), me).wait_recv()
        for cp in first + passed:
            cp.wait_send()
        for cp in mine:
            cp.wait()

    return pl.pallas_call(
        body, name=name, out_shape=[jax.ShapeDtypeStruct((N_DEV,) + a.shape, a.dtype) for a in shards],
        in_specs=[ANY] * n, out_specs=[ANY] * n,
        scratch_shapes=[pltpu.SemaphoreType.DMA((7 * n,)), pltpu.SemaphoreType.DMA((7 * n,)),
                        pltpu.SemaphoreType.DMA((n,))],
    )(*shards)


def _exchange_sibling(slabs, small, *, name):
    n = len(slabs)
    extra = [] if small is None else [small]
    ne = n + len(extra)

    def body(*refs):
        in_refs, out_refs = refs[:ne], refs[ne:2 * ne]
        send_sems, recv_sems = refs[2 * ne:]
        x, y, c = lax.axis_index("x"), lax.axis_index("y"), lax.axis_index("c")
        sibling = (x, y, 1 - c)
        copies = []
        for t in range(n):
            for q in range(4):
                copies.append(pltpu.make_async_remote_copy(
                    src_ref=in_refs[t].at[2 * q + (1 - c)], dst_ref=out_refs[t].at[q],
                    send_sem=send_sems.at[4 * t + q], recv_sem=recv_sems.at[4 * t + q],
                    device_id=sibling, device_id_type=MESH))
        if extra:
            copies.append(pltpu.make_async_remote_copy(
                src_ref=in_refs[n], dst_ref=out_refs[n], send_sem=send_sems.at[4 * n], recv_sem=recv_sems.at[4 * n],
                device_id=sibling, device_id_type=MESH))
        for cp in copies:
            cp.start()
        for cp in copies:
            cp.wait()

    return pl.pallas_call(
        body, name=name,
        out_shape=[jax.ShapeDtypeStruct((4,) + a.shape[1:], a.dtype) for a in slabs]
        + [jax.ShapeDtypeStruct(a.shape, a.dtype) for a in extra],
        in_specs=[ANY] * ne, out_specs=[ANY] * ne,
        scratch_shapes=[pltpu.SemaphoreType.DMA((4 * n + 1,)), pltpu.SemaphoreType.DMA((4 * n + 1,))],
    )(*slabs, *extra)


def _exchange_chips(psums, small_sum, *, name):
    n = len(psums)
    rs = small_sum.shape[0]

    def body(*refs):
        p_refs, s_ref = refs[:n], refs[n]
        rp_refs, tab_ref = refs[n + 1:2 * n + 1], refs[2 * n + 1]
        send_sems, recv_sems, local_sem = refs[2 * n + 2:]
        x, y, c = lax.axis_index("x"), lax.axis_index("y"), lax.axis_index("c")
        mine = pltpu.make_async_copy(s_ref, tab_ref.at[2 * x + y], local_sem)
        mine.start()

        def table_copy(k, px, py, slot):
            return pltpu.make_async_remote_copy(
                src_ref=s_ref, dst_ref=tab_ref.at[slot], send_sem=send_sems.at[3 * n + k],
                recv_sem=recv_sems.at[3 * n + k], device_id=(px, py, c), device_id_type=MESH)

        copies = []
        for k, (fx, fy) in enumerate(OTHER_CHIPS):
            px, py = x ^ fx, y ^ fy
            for t in range(n):
                copies.append(pltpu.make_async_remote_copy(
                    src_ref=p_refs[t].at[2 * px + py], dst_ref=rp_refs[t].at[k],
                    send_sem=send_sems.at[3 * t + k], recv_sem=recv_sems.at[3 * t + k],
                    device_id=(px, py, c), device_id_type=MESH))
            copies.append(table_copy(k, px, py, 2 * x + y))
        for cp in copies:
            cp.start()
        for k, (fx, fy) in enumerate(OTHER_CHIPS):
            px, py = x ^ fx, y ^ fy
            for t in range(n):
                copies[k * (n + 1) + t].wait()
            table_copy(k, px, py, 2 * px + py).wait()
        mine.wait()

    return pl.pallas_call(
        body, name=name,
        out_shape=[jax.ShapeDtypeStruct((3,) + a.shape[1:], a.dtype) for a in psums]
        + [jax.ShapeDtypeStruct((4, rs, 1024), F32)],
        in_specs=[ANY] * (n + 1), out_specs=[ANY] * (n + 1),
        scratch_shapes=[pltpu.SemaphoreType.DMA((3 * n + 3,)), pltpu.SemaphoreType.DMA((3 * n + 3,)),
                        pltpu.SemaphoreType.DMA],
    )(*psums, small_sum)


def _to_comm(name, kind, block, dtype=BF16):
    a = block[0]
    if kind == "cols":
        a = a.T
        if name == "w_in":
            a = jnp.pad(a, ((0, IN_SHARD_PAD - IN_SHARD), (0, 0)))
    return a if kind == "f32" else a.astype(dtype)


def _from_comm(name, kind, a):
    if kind == "cols":
        if name == "w_in":
            a = a[:IN_SHARD]
        a = a.T
    return a[None]


def _assemble_weights(g):
    out = {}
    if "w_in" in g:
        wt_in = g["w_in"][:, :IN_SHARD].reshape(IN_COLS, D_MODEL)
        out["wt_main"] = jnp.concatenate([wt_in[:2048], wt_in[O_G:], wt_in[2048:O_F]], axis=0)
        out["wt_f"] = jnp.pad(wt_in[O_F:O_G], ((0, 128 - HEADS), (0, 0)))
    square = dict(w_branch_a="w_a", w_branch_b="w_b", w_out="w_out", w_ple_gate="w_pg")
    for long, short in square.items():
        if long in g:
            out[short] = g[long].reshape(D_MODEL, D_MODEL)
    if "w_up" in g:
        out["wt_up"] = g["w_up"].reshape(2 * D_FF, D_MODEL)
    if "conv_w" in g:
        out["conv_w"] = g["conv_w"].transpose(1, 0, 2).reshape(3, 2 * D_FF)
    if "w_down" in g:
        out["w_down"] = g["w_down"].reshape(D_FF, D_MODEL)
    if "w_ple" in g:
        out["wt_ple"] = g["w_ple"].reshape(D_MODEL, PLE_DIM)
    return out


def _grad_slabs(gr):
    out = {}
    if "wt_main" in gr:
        gm = gr["wt_main"]
        gt_in = jnp.concatenate([gm[:2048], gm[4096:], gr["wt_f"][:HEADS], gm[2048:4096]], axis=0)
        out["w_in"] = jnp.pad(gt_in.reshape(N_DEV, IN_SHARD, D_MODEL), ((0, 0), (0, IN_SHARD_PAD - IN_SHARD), (0, 0)))
    rows = dict(w_a="w_branch_a", w_b="w_branch_b", w_out="w_out", wt_up="w_up", w_down="w_down", w_pg="w_ple_gate")
    for short, long in rows.items():
        if short in gr:
            out[long] = gr[short].reshape(N_DEV, -1, D_MODEL)
    if "conv_w" in gr:
        out["conv_w"] = gr["conv_w"].reshape(3, N_DEV, -1).transpose(1, 0, 2)
    if "wt_ple" in gr:
        out["w_ple"] = gr["wt_ple"].reshape(N_DEV, -1, PLE_DIM)
    return {k: v.astype(BF16) for k, v in out.items()}


def _rows(a, rows):
    flat = a.reshape(-1)
    return jnp.pad(flat, (0, rows * 1024 - flat.shape[0])).reshape(rows, 1024)


def _pack_small(parts):
    return jnp.concatenate([_rows(parts[n].astype(F32), r) for n, r in SMALL], axis=0)


def _small(packed, name, shape):
    off, r = SMALL_OFF[name]
    n = math.prod(shape)
    return packed[off:off + r].reshape(-1)[:n].reshape(shape)


class _Overlap:
    def __init__(self, shards, finish_gather, start_reduce, finish_reduce):
        self.shards, self.finish_gather = shards, finish_gather
        self.start_reduce, self.finish_reduce = start_reduce, finish_reduce


def _local_step(x, p, target, w, sm, overlap=None):
    s = x.shape[0]
    mm = _matmul
    wt_main = w["wt_main"]
    conv_b = sm["conv_b"]
    bs_t = jnp.pad(sm["gmlp_b_s"].T, ((0, 0), (0, 128 - GROUPS)))
    b_f = jnp.pad(sm["b_f"], ((0, 0), (0, 128 - HEADS)))
    big = dict(tm=1024, tn=1024, tk=1024)
    whole_s = dict(tn=1024, tk=s)

    h = _rmsnorm_fwd(x, sm["norm_mix_g"], name="norm_mix")
    zuvg = mm(h, wt_main, mode="nt", out_dtype=F32, name="in_uvg", n=4096, **big)
    qkv = mm(h, wt_main, mode="nt", out_dtype=BF16, name="in_qkv", n=3072, b_off=4, **big)
    f_logit = mm(h, w["wt_f"], mode="nt", out_dtype=F32, name="in_f", tm=1024, tk=1024)
    a = _gmlp_fwd(zuvg, sm["gmlp_ln_g"], sm["gmlp_ln_b"], sm["gmlp_w_s"], bs_t, name="gmlp_fwd")
    cqe = _forget_cumsum(f_logit, b_f, name="forget_cumsum")
    qa, ka, vt = _attn_prep(qkv, cqe, name="attn_prep")
    if overlap is None:
        b, lse3, _ = _attn_fwd(qa, ka, vt, name="attn_fwd")
    else:
        b, lse3, level1 = _attn_fwd(qa, ka, vt, overlap.shards, name="attn_fwd")
        w = {**w, **overlap.finish_gather(level1)}
    wt_up, conv_w = w["wt_up"], w["conv_w"]
    ya = mm(a, w["w_a"], mode="nn", out_dtype=F32, name="branch_a", **big)
    yb = mm(b, w["w_b"], mode="nn", out_dtype=F32, name="branch_b", **big)
    merged = _merge_fwd(ya, yb, zuvg, name="merge_fwd")
    x1 = mm(merged, w["w_out"], mode="nn", out_dtype=F32, name="out_proj", add=x, **big)
    h2 = _rmsnorm_fwd(x1, sm["norm_ffn_g"], name="norm_ffn")
    up = mm(h2, wt_up, mode="nt", out_dtype=F32, name="up", tm=1024, tn=512, tk=1024)
    act = _convglu_fwd(up, conv_w, conv_b, name="convglu_fwd")
    x2 = mm(act, w["w_down"], mode="nn", out_dtype=F32, name="down", tm=1024, tn=1024, tk=1408, add=x1)
    h3 = _rmsnorm_fwd(x2, sm["norm_ple_g"], name="norm_ple")
    ple = mm(p, w["wt_ple"], mode="nt", out_dtype=F32, name="ple", tm=1024, tn=1024, tk=256)
    gp = mm(h3, w["w_pg"], mode="nn", out_dtype=F32, name="ple_gate", **big)
    x3 = _ple_fwd(x2, ple, gp, name="ple_fwd")

    loss, dx3, d_norm_final = _final_loss_bwd(x3, target, sm["norm_final_g"], name="loss_bwd")
    dple, dgp = _ple_bwd(dx3, ple, gp, name="ple_bwd")
    g_wt_ple = mm(dple, p, mode="tn", out_dtype=BF16, name="d_w_ple", tm=512, tn=256, tk=s)
    g_w_pg = mm(h3, dgp, mode="tn", out_dtype=BF16, name="d_w_pg", tm=256, **whole_s)
    dh3 = mm(dgp, w["w_pg"], mode="nt", out_dtype=F32, name="d_h3", **big)
    dx2, dx2b, d_norm_ple = _rmsnorm_bwd(dx3, dh3, x2, sm["norm_ple_g"], name="norm_ple_bwd")
    g_w_down = mm(act, dx2b, mode="tn", out_dtype=BF16, name="d_w_down", tm=256, **whole_s)
    dact = mm(dx2b, w["w_down"], mode="nt", out_dtype=BF16, name="d_act", tm=1024, tn=1408, tk=1024)
    dup_a, dup_g, dcw_a, dcw_g, dcb_a, dcb_g = _convglu_bwd(dact, up, conv_w, conv_b, name="convglu_bwd")
    g_wt_up = mm(dup_a, h2, mode="tn", out_dtype=BF16, name="d_w_up_a", tm=256, out_rows=2 * D_FF, **whole_s)
    g_wt_up = mm(dup_g, h2, mode="tn", out_dtype=BF16, name="d_w_up_g", tm=256, out_rows=2 * D_FF,
                 o_off=D_FF // 256, into=g_wt_up, **whole_s)
    dh2 = mm(dup_a, wt_up, mode="nn", out_dtype=F32, name="d_h2_a", tm=1024, tn=1024, tk=1408)
    dh2 = mm(dup_g, wt_up, mode="nn", out_dtype=F32, name="d_h2_g", tm=1024, tn=1024, tk=1408, b_off=2, add=dh2)
    dx1, dx1b, d_norm_ffn = _rmsnorm_bwd(dx2, dh2, x1, sm["norm_ffn_g"], name="norm_ffn_bwd")
    g_w_out = mm(merged, dx1b, mode="tn", out_dtype=BF16, name="d_w_out", tm=256, **whole_s)
    dmerged = mm(dx1b, w["w_out"], mode="nt", out_dtype=F32, name="d_merged", **big)
    dya, dyb, dga, dgb = _merge_bwd(dmerged, ya, yb, zuvg, name="merge_bwd")
    g_w_a = mm(a, dya, mode="tn", out_dtype=BF16, name="d_w_a", tm=256, **whole_s)
    g_w_b = mm(b, dyb, mode="tn", out_dtype=BF16, name="d_w_b", tm=256, **whole_s)
    da = mm(dya, w["w_a"], mode="nt", out_dtype=BF16, name="d_a", **big)
    db = mm(dyb, w["w_b"], mode="nt", out_dtype=BF16, name="d_b", **big)
    dzu, dzv, d_w_s, d_bs_t, d_ln_g, d_ln_b = _gmlp_bwd(
        da, zuvg, sm["gmlp_ln_g"], sm["gmlp_ln_b"], sm["gmlp_w_s"], bs_t, name="gmlp_bwd")
    delta3 = _attn_delta(db, b, name="attn_delta")
    grads = dict(w_a=g_w_a, w_b=g_w_b, w_out=g_w_out, wt_up=g_wt_up, conv_w=jnp.concatenate([dcw_a, dcw_g], axis=1),
                 w_down=g_w_down, wt_ple=g_wt_ple, w_pg=g_w_pg)
    if overlap is None:
        (dq, dk, dv, aux, dcq3), _ = _attn_bwd(qa, ka, qkv, db, lse3, delta3, name="attn_bwd")
    else:
        (dq, dk, dv, aux, dcq3), from_chips = _attn_bwd(
            qa, ka, qkv, db, lse3, delta3, overlap.start_reduce(grads), name="attn_bwd")
        overlap.finish_reduce(from_chips)
    dcq16 = jnp.pad(dcq3[:, :2, :].reshape(HEADS, s).T, ((0, 0), (0, 128 - HEADS)))
    dzf, d_b_f = _forget_bwd(dcq16, aux, f_logit, b_f, name="forget_bwd")
    dz = jnp.concatenate([dzu, dzv, dga, dgb, dq, dk, dv], axis=1)
    g_wt_main = mm(dz, h, mode="tn", out_dtype=BF16, name="d_w_main", tm=512, **whole_s)
    g_wt_f = mm(dzf, h, mode="tn", out_dtype=BF16, name="d_w_f", **whole_s)
    dh = mm(dz, wt_main, mode="nn", out_dtype=F32, name="d_h_main", **big)
    dh = mm(dzf, w["wt_f"], mode="nn", out_dtype=F32, name="d_h_f", tm=1024, tn=1024, add=dh)
    dx0, _, d_norm_mix = _rmsnorm_bwd(dx1, dh, x, sm["norm_mix_g"], name="norm_mix_bwd")

    grads = dict(grads, wt_main=g_wt_main, wt_f=g_wt_f)
    small = dict(norm_mix_g=d_norm_mix, b_f=d_b_f[:, :HEADS], gmlp_ln_g=d_ln_g, gmlp_ln_b=d_ln_b, gmlp_w_s=d_w_s,
                 gmlp_b_s=d_bs_t[:, :GROUPS].T, norm_ffn_g=d_norm_ffn,
                 conv_b=jnp.concatenate([dcb_a, dcb_g], axis=1), norm_ple_g=d_norm_ple, norm_final_g=d_norm_final)
    return loss, dx0, grads, small


def kernel(x, p, norm_mix_g, w_in, b_f, gmlp_ln_g, gmlp_ln_b, gmlp_w_s, gmlp_b_s, w_branch_a, w_branch_b, w_out, norm_ffn_g, w_up, conv_w, conv_b, w_down, norm_ple_g, w_ple, w_ple_gate, norm_final_g, loss_target, m_norm_mix_g, m_w_in, m_b_f, m_gmlp_ln_g, m_gmlp_ln_b, m_gmlp_w_s, m_gmlp_b_s, m_w_branch_a, m_w_branch_b, m_w_out, m_norm_ffn_g, m_w_up, m_conv_w, m_conv_b, m_w_down, m_norm_ple_g, m_w_ple, m_w_ple_gate, m_norm_final_g, v_norm_mix_g, v_w_in, v_b_f, v_gmlp_ln_g, v_gmlp_ln_b, v_gmlp_w_s, v_gmlp_b_s, v_w_branch_a, v_w_branch_b, v_w_out, v_norm_ffn_g, v_w_up, v_conv_w, v_conv_b, v_w_down, v_norm_ple_g, v_w_ple, v_w_ple_gate, v_norm_final_g):
    given = dict(locals())
    weights = {n: given[n] for n in WEIGHT_ORDER}
    mom_m = {n: given["m_" + n] for n in WEIGHT_ORDER}
    mom_v = {n: given["v_" + n] for n in WEIGHT_ORDER}
    pos = jnp.stack([lax.axis_index("x"), lax.axis_index("y"), lax.axis_index("c")]).astype(I32)
    names = [n for n, _ in SHARDED]
    kinds = dict(SHARDED)

    later = [n for n in names if n != "w_in"]
    reduced = {}

    first = _allgather([_to_comm("w_in", kinds["w_in"], weights["w_in"])], name="allgather_w_in")

    def finish_gather(level1):
        return _assemble_weights(dict(zip(later, _gather2(level1, name="allgather_forward"))))

    def start_reduce(grads):
        slabs = _grad_slabs(grads)
        from_sib = _exchange_sibling([slabs[n] for n in later], None, name="exchange_sibling")
        sums = [_sum_pairs(slabs[n], r, pos, name="sum_sibling_" + n) for n, r in zip(later, from_sib)]
        reduced.update({n: s32 for n, (s32, _) in zip(later, sums)})
        return [s16 for _, s16 in sums]

    def finish_reduce(from_chips):
        reduced.update({n: (reduced[n], r) for n, r in zip(later, from_chips)})

    overlap = _Overlap([_to_comm(n, kinds[n], weights[n]) for n in later], finish_gather, start_reduce, finish_reduce)

    sm = dict(norm_mix_g=norm_mix_g, b_f=b_f, gmlp_ln_g=gmlp_ln_g, gmlp_ln_b=gmlp_ln_b, gmlp_w_s=gmlp_w_s[0],
              gmlp_b_s=gmlp_b_s[0], norm_ffn_g=norm_ffn_g, conv_b=conv_b, norm_ple_g=norm_ple_g,
              norm_final_g=norm_final_g.reshape(1, D_MODEL))
    loss_part, dx0, grads, small = _local_step(
        x[0], p[0, 0], loss_target[0], _assemble_weights({"w_in": first[0]}), sm, overlap)

    slab_in = _grad_slabs({k: grads[k] for k in ("wt_main", "wt_f")})["w_in"]
    small_g = _pack_small(small)
    from_sib, small_sib = _exchange_sibling([slab_in], small_g, name="exchange_sibling_w_in")
    s32, s16 = _sum_pairs(slab_in, from_sib, pos, name="sum_sibling_w_in")
    small_chip = _pair_sum_small(small_g, small_sib, name="sum_sibling_small")
    from_chips, small_tab = _exchange_chips([s16], small_chip, name="exchange_chips_w_in")
    reduced["w_in"] = (s32, from_chips)

    grad, delta, new_m, new_v = {}, {}, {}, {}
    for n in names:
        s32, r = reduced[n]
        outs = _adam_sharded(s32, r, *[_to_comm(n, kinds[n], src[n], F32) for src in (weights, mom_m, mom_v)], pos,
                             name="adam_" + n)
        grad[n], delta[n], new_m[n], new_v[n] = [_from_comm(n, kinds[n], o) for o in outs]
    replicated = [n for n, _ in SMALL]
    rep = lambda src: _pack_small({n: src[n] for n in replicated})
    packed = _adam_replicated(small_tab, rep(weights), rep(mom_m), rep(mom_v), name="adam_replicated")
    for out, pk in zip((grad, delta, new_m, new_v), packed):
        for n in replicated:
            out[n] = _small(pk, n, weights[n].shape)

    loss = lax.psum(loss_part[0, 0], ("x", "y", "c"))
    return (loss, dx0[None], *[grad[n] for n in WEIGHT_ORDER], *[delta[n] for n in WEIGHT_ORDER],
            *[new_m[n] for n in WEIGHT_ORDER], *[new_v[n] for n in WEIGHT_ORDER])
```

```python
import functools
import math

import jax
import jax.numpy as jnp
from jax import lax
from jax.experimental import pallas as pl
from jax.experimental.pallas import tpu as pltpu

F32 = jnp.float32
BF16 = jnp.bfloat16
I32 = jnp.int32

D_MODEL = 1024
GROUPS = 8
GDIM = 128
GBLOCK = 128
CHUNK = 64
HEADS = 16
HEAD_DIM = 64
D_FF = 2816
PLE_DIM = 256
EPS = 1e-6
N_DEV = 8
ATT_SCALE = HEAD_DIM ** -0.5
NEG = -1e30

ADAM_LR = 0.001
ADAM_B1 = 0.9
ADAM_B2 = 0.999
ADAM_EPS = 1e-08
ADAM_WD = 0.01
ADAM_STEP = 10

V7X_VMEM_LIMIT = 48 * 1024 * 1024
MESH = pl.DeviceIdType.MESH

O_F = 2 * 1024 + 3 * 1024
O_G = O_F + HEADS
IN_COLS = O_G + 2 * D_MODEL
MAIN_COLS = IN_COLS - HEADS
IN_SHARD = IN_COLS // N_DEV
IN_SHARD_PAD = 912

SHARDED = (("w_in", "cols"), ("w_branch_a", "rows"), ("w_branch_b", "rows"), ("w_out", "rows"), ("w_up", "cols"),
           ("conv_w", "f32"), ("w_down", "rows"), ("w_ple", "cols"), ("w_ple_gate", "rows"))

SMALL = (("norm_mix_g", 8), ("b_f", 8), ("gmlp_ln_g", 8), ("gmlp_ln_b", 8), ("gmlp_w_s", 128), ("gmlp_b_s", 8),
         ("norm_ffn_g", 8), ("conv_b", 8), ("norm_ple_g", 8), ("norm_final_g", 8))
SMALL_OFF = {}
_o = 0
for _n, _r in SMALL:
    SMALL_OFF[_n] = (_o, _r)
    _o += _r
SMALL_ROWS = _o

WEIGHT_ORDER = ("norm_mix_g", "w_in", "b_f", "gmlp_ln_g", "gmlp_ln_b", "gmlp_w_s", "gmlp_b_s", "w_branch_a",
                "w_branch_b", "w_out", "norm_ffn_g", "w_up", "conv_w", "conv_b", "w_down", "norm_ple_g", "w_ple",
                "w_ple_gate", "norm_final_g")


def _cparams(sem):
    return pltpu.CompilerParams(dimension_semantics=sem, vmem_limit_bytes=V7X_VMEM_LIMIT)


def _gelu(x):
    c = math.sqrt(2.0 / math.pi)
    return 0.5 * x * (1.0 + jnp.tanh(c * (x + 0.044715 * x * x * x)))


def _gelu_and_grad(x):
    c = math.sqrt(2.0 / math.pi)
    t = jnp.tanh(c * (x + 0.044715 * x * x * x))
    g = 0.5 * x * (1.0 + t)
    dg = 0.5 * (1.0 + t) + 0.5 * x * (1.0 - t * t) * (c * (1.0 + 3.0 * 0.044715 * x * x))
    return g, dg


def _sigmoid(x):
    return 1.0 / (1.0 + jnp.exp(-x))


def _dot(a, b, dims):
    return lax.dot_general(a, b, (dims, ((), ())), preferred_element_type=F32)


NN = ((1,), (0,))
NT = ((1,), (1,))
TN = ((0,), (0,))


def _row_tile(rows, most):
    best = None
    for t in range(16, min(rows, most) + 1, 16):
        if rows % t == 0:
            best = t
    return best if best is not None else rows


def _matmul(a, b, *, mode, out_dtype, name, tm=512, tn=512, tk=512, add=None, n=None, b_off=0,
            out_rows=None, o_off=0, into=None, carry=None):
    if mode == "tn":
        kdim, m = a.shape
    else:
        m, kdim = a.shape
    if n is None:
        n = b.shape[0] if mode == "nt" else b.shape[1]
    tm, tn, tk = min(tm, m), min(tn, n), min(tk, kdim)
    assert m % tm == 0 and n % tn == 0 and kdim % tk == 0, (name, m, n, kdim, tm, tn, tk)
    nk = kdim // tk
    dims = {"nn": NN, "nt": NT, "tn": TN}[mode]

    def finish(r, add_ref, o_ref):
        if add_ref is not None:
            r = add_ref[...].astype(F32) + r
        o_ref[...] = r.astype(out_dtype)

    def body(*refs):
        refs = list(refs)
        a_ref, b_ref = refs[:2]
        add_ref = refs[2] if add is not None else None
        o_ref = refs[2 + (add is not None) + (into is not None)]
        part = _dot(a_ref[...].astype(BF16), b_ref[...].astype(BF16), dims)
        if nk == 1:
            finish(part, add_ref, o_ref)
            return
        acc_ref = refs[-1]
        k = pl.program_id(2)

        @pl.when(k == 0)
        def _():
            acc_ref[...] = part

        @pl.when((k > 0) & (k < nk - 1))
        def _():
            acc_ref[...] += part

        @pl.when(k == nk - 1)
        def _():
            finish(acc_ref[...] + part, add_ref, o_ref)

    a_spec = pl.BlockSpec((tk, tm), lambda i, j, k: (k, i)) if mode == "tn" else pl.BlockSpec((tm, tk), lambda i, j, k: (i, k))
    if mode == "nt":
        b_spec = pl.BlockSpec((tn, tk), lambda i, j, k: (j + b_off, k))
    else:
        b_spec = pl.BlockSpec((tk, tn), lambda i, j, k: (k + b_off, j))
    o_spec = pl.BlockSpec((tm, tn), lambda i, j, k: (i + o_off, j))
    in_specs = [a_spec, b_spec] + ([pl.BlockSpec((tm, tn), lambda i, j, k: (i, j))] if add is not None else [])
    args = (a, b) + ((add,) if add is not None else ())
    aliases = {}
    if into is not None:
        aliases = {len(args): 0}
        in_specs.append(pl.BlockSpec(memory_space=pl.ANY))
        args += (into,)
    (out,), carried = _carry_call(
        body, carry, name=name, grid=(m // tm, n // tn, nk), in_specs=in_specs, out_specs=[o_spec],
        out_shape=[jax.ShapeDtypeStruct((m if out_rows is None else out_rows, n), out_dtype)],
        scratch_shapes=[pltpu.VMEM((tm, tn), F32)] if nk > 1 else [], args=args, own_aliases=aliases)
    return out if carry is None else (out, carried)


def _row_spec(tr, width, col_block=0):
    return pl.BlockSpec((tr, width), lambda i: (i, col_block))


def _full_spec(shape):
    return pl.BlockSpec(shape, lambda i: tuple(0 for _ in shape))


def _rmsnorm_fwd(x, g, *, name, tr=256):
    s, d = x.shape

    def body(x_ref, g_ref, o_ref):
        xv = x_ref[...]
        r = lax.rsqrt(jnp.mean(xv * xv, axis=-1, keepdims=True) + EPS)
        o_ref[...] = ((xv * r) * g_ref[...]).astype(BF16)

    return pl.pallas_call(
        body, name=name, grid=(s // tr,),
        in_specs=[_row_spec(tr, d), _full_spec((1, d))], out_specs=_row_spec(tr, d),
        out_shape=jax.ShapeDtypeStruct((s, d), BF16), compiler_params=_cparams(("parallel",)),
    )(x, g)


def _rmsnorm_bwd(dres, dh, x, g, *, name, tr=256):
    s, d = x.shape

    def body(dres_ref, dh_ref, x_ref, g_ref, dx_ref, dxb_ref, dg_ref):
        i = pl.program_id(0)
        xv = x_ref[...]
        r = lax.rsqrt(jnp.mean(xv * xv, axis=-1, keepdims=True) + EPS)
        xhat = xv * r
        dhv = dh_ref[...].astype(F32)
        dxhat = dhv * g_ref[...]
        dx = dres_ref[...] + r * (dxhat - xhat * jnp.mean(dxhat * xhat, axis=-1, keepdims=True))
        dx_ref[...] = dx
        dxb_ref[...] = dx.astype(BF16)
        dgp = jnp.sum(dhv * xhat, axis=0, keepdims=True)

        @pl.when(i == 0)
        def _():
            dg_ref[...] = dgp

        @pl.when(i > 0)
        def _():
            dg_ref[...] += dgp

    return pl.pallas_call(
        body, name=name, grid=(s // tr,),
        in_specs=[_row_spec(tr, d), _row_spec(tr, d), _row_spec(tr, d), _full_spec((1, d))],
        out_specs=[_row_spec(tr, d), _row_spec(tr, d), _full_spec((1, d))],
        out_shape=[jax.ShapeDtypeStruct((s, d), F32), jax.ShapeDtypeStruct((s, d), BF16),
                   jax.ShapeDtypeStruct((1, d), F32)],
        compiler_params=_cparams(("arbitrary",)),
    )(dres, dh, x, g)


def _final_loss_bwd(x3, target, g, *, name, tr=256):
    s, d = x3.shape

    def body(x_ref, t_ref, g_ref, loss_ref, dx_ref, dg_ref):
        i = pl.program_id(0)
        xv = x_ref[...]
        r = lax.rsqrt(jnp.mean(xv * xv, axis=-1, keepdims=True) + EPS)
        xhat = xv * r
        diff = xhat * g_ref[...] - t_ref[...]
        lp = jnp.zeros((1, 128), F32) + (0.5 / d) * jnp.sum(diff * diff)
        dy = diff * (1.0 / d)
        dxhat = dy * g_ref[...]
        dx_ref[...] = r * (dxhat - xhat * jnp.mean(dxhat * xhat, axis=-1, keepdims=True))
        dgp = jnp.sum(dy * xhat, axis=0, keepdims=True)

        @pl.when(i == 0)
        def _():
            dg_ref[...] = dgp
            loss_ref[...] = lp

        @pl.when(i > 0)
        def _():
            dg_ref[...] += dgp
            loss_ref[...] += lp

    return pl.pallas_call(
        body, name=name, grid=(s // tr,),
        in_specs=[_row_spec(tr, d), _row_spec(tr, d), _full_spec((1, d))],
        out_specs=[_full_spec((1, 128)), _row_spec(tr, d), _full_spec((1, d))],
        out_shape=[jax.ShapeDtypeStruct((1, 128), F32), jax.ShapeDtypeStruct((s, d), F32),
                   jax.ShapeDtypeStruct((1, d), F32)],
        compiler_params=_cparams(("arbitrary",)),
    )(x3, target, g)


def _merge_fwd(ya, yb, zuvg, *, name, tr=256):
    s, d = ya.shape

    def body(ya_ref, yb_ref, ga_ref, gb_ref, o_ref):
        o_ref[...] = (_sigmoid(ga_ref[...]) * ya_ref[...] + _sigmoid(gb_ref[...]) * yb_ref[...]).astype(BF16)

    return pl.pallas_call(
        body, name=name, grid=(s // tr,),
        in_specs=[_row_spec(tr, d), _row_spec(tr, d), _row_spec(tr, d, 2), _row_spec(tr, d, 3)],
        out_specs=_row_spec(tr, d),
        out_shape=jax.ShapeDtypeStruct((s, d), BF16), compiler_params=_cparams(("parallel",)),
    )(ya, yb, zuvg, zuvg)


def _merge_bwd(dm, ya, yb, zuvg, *, name, tr=256):
    s, d = ya.shape

    def body(dm_ref, ya_ref, yb_ref, ga_ref, gb_ref, dya_ref, dyb_ref, dga_ref, dgb_ref):
        dmv = dm_ref[...]
        sa = _sigmoid(ga_ref[...])
        sb = _sigmoid(gb_ref[...])
        dya_ref[...] = (dmv * sa).astype(BF16)
        dyb_ref[...] = (dmv * sb).astype(BF16)
        dga_ref[...] = (dmv * ya_ref[...] * (sa * (1.0 - sa))).astype(BF16)
        dgb_ref[...] = (dmv * yb_ref[...] * (sb * (1.0 - sb))).astype(BF16)

    o = jax.ShapeDtypeStruct((s, d), BF16)
    return pl.pallas_call(
        body, name=name, grid=(s // tr,),
        in_specs=[_row_spec(tr, d)] * 3 + [_row_spec(tr, d, 2), _row_spec(tr, d, 3)], out_specs=[_row_spec(tr, d)] * 4,
        out_shape=[o, o, o, o], compiler_params=_cparams(("parallel",)),
    )(dm, ya, yb, zuvg, zuvg)


def _ple_fwd(x2, ple, gp, *, name, tr=256):
    s, d = x2.shape

    def body(x_ref, ple_ref, gp_ref, o_ref):
        o_ref[...] = x_ref[...] + ple_ref[...] * _sigmoid(gp_ref[...])

    return pl.pallas_call(
        body, name=name, grid=(s // tr,),
        in_specs=[_row_spec(tr, d)] * 3, out_specs=_row_spec(tr, d),
        out_shape=jax.ShapeDtypeStruct((s, d), F32), compiler_params=_cparams(("parallel",)),
    )(x2, ple, gp)


def _ple_bwd(dx3, ple, gp, *, name, tr=256):
    s, d = dx3.shape

    def body(dx_ref, ple_ref, gp_ref, dple_ref, dgp_ref):
        sg = _sigmoid(gp_ref[...])
        dxv = dx_ref[...]
        dple_ref[...] = (dxv * sg).astype(BF16)
        dgp_ref[...] = (dxv * ple_ref[...] * (sg * (1.0 - sg))).astype(BF16)

    o = jax.ShapeDtypeStruct((s, d), BF16)
    return pl.pallas_call(
        body, name=name, grid=(s // tr,),
        in_specs=[_row_spec(tr, d)] * 3, out_specs=[_row_spec(tr, d)] * 2,
        out_shape=[o, o], compiler_params=_cparams(("parallel",)),
    )(dx3, ple, gp)


def _masked_ws(ws_ref, g):
    row = lax.broadcasted_iota(I32, (GBLOCK, GBLOCK), 0)
    col = lax.broadcasted_iota(I32, (GBLOCK, GBLOCK), 1)
    keep = (col // CHUNK) <= (row // CHUNK)
    return jnp.where(keep, ws_ref[g], 0.0), keep


def _layernorm_parts(zv):
    mu = jnp.mean(zv, axis=-1, keepdims=True)
    xc = zv - mu
    rs = lax.rsqrt(jnp.mean(xc * xc, axis=-1, keepdims=True) + EPS)
    return xc * rs, rs


def _gmlp_fwd(zuvg, ln_g, ln_b, w_s, bs_t, *, name):
    s, w = zuvg.shape[0], GROUPS * GDIM

    def body(zu_ref, zv_ref, lng_ref, lnb_ref, ws_ref, bs_ref, a_ref):
        zu = _gelu(zu_ref[...])
        zv = _gelu(zv_ref[...])
        xhat, _ = _layernorm_parts(zv)
        vln = (xhat * lng_ref[...] + lnb_ref[...]).astype(BF16)
        for g in range(GROUPS):
            wm, _ = _masked_ws(ws_ref, g)
            mixed = _dot(wm.astype(BF16), vln[:, g * GDIM:(g + 1) * GDIM], NN) + bs_ref[:, g:g + 1]
            a_ref[:, g * GDIM:(g + 1) * GDIM] = (zu[:, g * GDIM:(g + 1) * GDIM] * mixed).astype(BF16)

    return pl.pallas_call(
        body, name=name, grid=(s // GBLOCK,),
        in_specs=[_row_spec(GBLOCK, w, 0), _row_spec(GBLOCK, w, 1), _full_spec((1, w)), _full_spec((1, w)),
                  _full_spec((GROUPS, GBLOCK, GBLOCK)), _full_spec((GBLOCK, 128))],
        out_specs=_row_spec(GBLOCK, w),
        out_shape=jax.ShapeDtypeStruct((s, w), BF16), compiler_params=_cparams(("parallel",)),
    )(zuvg, zuvg, ln_g, ln_b, w_s, bs_t)


def _gmlp_bwd(da, zuvg, ln_g, ln_b, w_s, bs_t, carry=None, *, name):
    s, w = zuvg.shape[0], GROUPS * GDIM

    def body(da_ref, zu_ref, zv_ref, lng_ref, lnb_ref, ws_ref, bs_ref,
             dzu_ref, dzv_ref, dws_ref, dbs_ref, dlng_ref, dlnb_ref, dvln_ref):
        i = pl.program_id(0)
        zu, dzu_g = _gelu_and_grad(zu_ref[...])
        zv, dzv_g = _gelu_and_grad(zv_ref[...])
        xhat, rs = _layernorm_parts(zv)
        vln = (xhat * lng_ref[...] + lnb_ref[...]).astype(BF16)
        dav = da_ref[...].astype(F32)
        lane = lax.broadcasted_iota(I32, (GBLOCK, 128), 1)
        dbs = jnp.zeros((GBLOCK, 128), F32)

        @pl.when(i == 0)
        def _():
            dws_ref[...] = jnp.zeros_like(dws_ref)

        for g in range(GROUPS):
            sl = slice(g * GDIM, (g + 1) * GDIM)
            wm, keep = _masked_ws(ws_ref, g)
            wmb = wm.astype(BF16)
            vg = vln[:, sl]
            mixed = _dot(wmb, vg, NN) + bs_ref[:, g:g + 1]
            dag = dav[:, sl]
            dzu_ref[:, sl] = (dag * mixed * dzu_g[:, sl]).astype(BF16)
            dmix = dag * zu[:, sl]
            dmb = dmix.astype(BF16)
            dws_ref[g] += jnp.where(keep, _dot(dmb, vg, NT), 0.0)
            dbs = jnp.where(lane == g, jnp.sum(dmix, axis=1, keepdims=True), dbs)
            dvln_ref[:, sl] = _dot(wmb, dmb, TN)
        dvln = dvln_ref[...]
        dxhat = dvln * lng_ref[...]
        dzv = rs * (dxhat - jnp.mean(dxhat, axis=-1, keepdims=True)
                    - xhat * jnp.mean(dxhat * xhat, axis=-1, keepdims=True))
        dzv_ref[...] = (dzv * dzv_g).astype(BF16)
        dlng = jnp.sum(dvln * xhat, axis=0, keepdims=True)
        dlnb = jnp.sum(dvln, axis=0, keepdims=True)

        @pl.when(i == 0)
        def _():
            dbs_ref[...] = dbs
            dlng_ref[...] = dlng
            dlnb_ref[...] = dlnb

        @pl.when(i > 0)
        def _():
            dbs_ref[...] += dbs
            dlng_ref[...] += dlng
            dlnb_ref[...] += dlnb

    return _carry_call(
        body, carry, name=name, grid=(s // GBLOCK,),
        in_specs=[_row_spec(GBLOCK, w), _row_spec(GBLOCK, w, 0), _row_spec(GBLOCK, w, 1), _full_spec((1, w)),
                  _full_spec((1, w)), _full_spec((GROUPS, GBLOCK, GBLOCK)), _full_spec((GBLOCK, 128))],
        out_specs=[_row_spec(GBLOCK, w), _row_spec(GBLOCK, w), _full_spec((GROUPS, GBLOCK, GBLOCK)),
                   _full_spec((GBLOCK, 128)), _full_spec((1, w)), _full_spec((1, w))],
        out_shape=[jax.ShapeDtypeStruct((s, w), BF16), jax.ShapeDtypeStruct((s, w), BF16),
                   jax.ShapeDtypeStruct((GROUPS, GBLOCK, GBLOCK), F32), jax.ShapeDtypeStruct((GBLOCK, 128), F32),
                   jax.ShapeDtypeStruct((1, w), F32), jax.ShapeDtypeStruct((1, w), F32)],
        scratch_shapes=[pltpu.VMEM((GBLOCK, w), F32)], args=[da, zuvg, zuvg, ln_g, ln_b, w_s, bs_t])


def _shift_down(u, k):
    row = lax.broadcasted_iota(I32, u.shape, 0)
    return jnp.where(row >= k, pltpu.roll(u, k, 0), 0.0)


def _shift_up(u, k):
    s = u.shape[0]
    row = lax.broadcasted_iota(I32, u.shape, 0)
    return jnp.where(row < s - k, pltpu.roll(u, s - k, 0), 0.0)


def _conv(u, w_ref, b_ref):
    return b_ref[...] + w_ref[0:1, :] * _shift_down(u, 2) + w_ref[1:2, :] * _shift_down(u, 1) + w_ref[2:3, :] * u


def _conv_specs(s, f, tc):
    nc = f // tc
    half = lambda rows: [pl.BlockSpec((rows, tc), lambda j: (0, j)), pl.BlockSpec((rows, tc), lambda j: (0, nc + j))]
    return half(s), half(3), half(1)


def _convglu_fwd(up, conv_w, conv_b, *, name, tc=256):
    s, f = up.shape[0], up.shape[1] // 2
    up_specs, w_specs, b_specs = _conv_specs(s, f, tc)

    def body(ua_ref, ug_ref, wa_ref, wg_ref, ba_ref, bg_ref, o_ref):
        ca = _conv(ua_ref[...], wa_ref, ba_ref)
        cg = _conv(ug_ref[...], wg_ref, bg_ref)
        o_ref[...] = (_gelu(ca) * cg).astype(BF16)

    return pl.pallas_call(
        body, name=name, grid=(f // tc,),
        in_specs=up_specs + w_specs + b_specs, out_specs=up_specs[0],
        out_shape=jax.ShapeDtypeStruct((s, f), BF16), compiler_params=_cparams(("parallel",)),
    )(up, up, conv_w, conv_w, conv_b, conv_b)


def _convglu_bwd(dact, up, conv_w, conv_b, *, name, tc=256):
    s, f = up.shape[0], up.shape[1] // 2
    up_specs, w_specs, b_specs = _conv_specs(s, f, tc)

    def half(dc, u, w_ref, du_ref, dw_ref, db_ref):
        db_ref[...] = jnp.sum(dc, axis=0, keepdims=True)
        dw_ref[0:1, :] = jnp.sum(dc * _shift_down(u, 2), axis=0, keepdims=True)
        dw_ref[1:2, :] = jnp.sum(dc * _shift_down(u, 1), axis=0, keepdims=True)
        dw_ref[2:3, :] = jnp.sum(dc * u, axis=0, keepdims=True)
        du = w_ref[2:3, :] * dc + w_ref[1:2, :] * _shift_up(dc, 1) + w_ref[0:1, :] * _shift_up(dc, 2)
        du_ref[...] = du.astype(BF16)

    def body(d_ref, ua_ref, ug_ref, wa_ref, wg_ref, ba_ref, bg_ref,
             dua_ref, dug_ref, dwa_ref, dwg_ref, dba_ref, dbg_ref):
        ua = ua_ref[...]
        ug = ug_ref[...]
        ca = _conv(ua, wa_ref, ba_ref)
        cg = _conv(ug, wg_ref, bg_ref)
        ga, dga = _gelu_and_grad(ca)
        dv = d_ref[...].astype(F32)
        half(dv * cg * dga, ua, wa_ref, dua_ref, dwa_ref, dba_ref)
        half(dv * ga, ug, wg_ref, dug_ref, dwg_ref, dbg_ref)

    col, w3, b1 = up_specs[0], w_specs[0], b_specs[0]
    return pl.pallas_call(
        body, name=name, grid=(f // tc,),
        in_specs=[col] + up_specs + w_specs + b_specs, out_specs=[col, col, w3, w3, b1, b1],
        out_shape=[jax.ShapeDtypeStruct((s, f), BF16), jax.ShapeDtypeStruct((s, f), BF16),
                   jax.ShapeDtypeStruct((3, f), F32), jax.ShapeDtypeStruct((3, f), F32),
                   jax.ShapeDtypeStruct((1, f), F32), jax.ShapeDtypeStruct((1, f), F32)],
        compiler_params=_cparams(("parallel",)),
    )(dact, up, up, conv_w, conv_w, conv_b, conv_b)


def _tri_dot(tri, x):
    b0 = x.astype(BF16)
    r1 = x - b0.astype(F32)
    b1 = r1.astype(BF16)
    b2 = (r1 - b1.astype(F32)).astype(BF16)
    return _dot(tri, b0, NN) + _dot(tri, b1, NN) + _dot(tri, b2, NN)


def _log_sigmoid(x):
    return jnp.minimum(x, 0.0) - jnp.log(1.0 + jnp.exp(-jnp.abs(x)))


def _expand_heads(col16, rows):
    head_of_lane = lax.broadcasted_iota(I32, (rows, HEADS * HEAD_DIM), 1) // HEAD_DIM
    out = jnp.zeros((rows, HEADS * HEAD_DIM), F32)
    for h in range(HEADS):
        out = jnp.where(head_of_lane == h, col16[:, h:h + 1], out)
    return out


def _forget_cumsum(f_logit, b_f, *, name):
    s = f_logit.shape[0]
    nb = s // 128

    def body(f_ref, b_ref, cqe_ref):
        row = lax.broadcasted_iota(I32, (128, 128), 0)
        col = lax.broadcasted_iota(I32, (128, 128), 1)
        tri = (col <= row).astype(BF16)

        def step(n, carry):
            r0 = pl.multiple_of(n * 128, 128)
            lf = _log_sigmoid(f_ref[pl.ds(r0, 128), :] + b_ref[...])
            cum = _tri_dot(tri, lf) + carry
            cqe_ref[pl.ds(r0, 128), :] = _expand_heads(cum, 128)
            return cum[127:128, :]

        lax.fori_loop(0, nb, step, jnp.zeros((1, 128), F32))

    return pl.pallas_call(
        body, name=name, grid=(1,),
        in_specs=[_full_spec((s, 128)), _full_spec((1, 128))],
        out_specs=_full_spec((s, HEADS * HEAD_DIM)),
        out_shape=jax.ShapeDtypeStruct((s, HEADS * HEAD_DIM), F32),
        compiler_params=_cparams(("arbitrary",)),
    )(f_logit, b_f)


def _forget_bwd(dcq16, sum_q16, f_logit, b_f, *, name):
    s = f_logit.shape[0]
    nb = s // 128

    def body(a_ref, k_ref, f_ref, b_ref, df_ref, db_ref):
        row = lax.broadcasted_iota(I32, (128, 128), 0)
        col = lax.broadcasted_iota(I32, (128, 128), 1)
        tri_rev = (col >= row).astype(BF16)

        def step(m, carry):
            suffix, dbsum = carry
            n = nb - 1 - m
            r0 = pl.multiple_of(n * 128, 128)
            dcum = a_ref[pl.ds(r0, 128), :] - k_ref[pl.ds(r0, 128), :]
            dlf = _tri_dot(tri_rev, dcum) + suffix
            df = dlf * _sigmoid(-(f_ref[pl.ds(r0, 128), :] + b_ref[...]))
            df_ref[pl.ds(r0, 128), :] = df.astype(BF16)
            return dlf[0:1, :], dbsum + jnp.sum(df, axis=0, keepdims=True)

        _, dbsum = lax.fori_loop(0, nb, step, (jnp.zeros((1, 128), F32), jnp.zeros((1, 128), F32)))
        db_ref[...] = dbsum

    return pl.pallas_call(
        body, name=name, grid=(1,),
        in_specs=[_full_spec((s, 128))] * 3 + [_full_spec((1, 128))],
        out_specs=[_full_spec((s, 128)), _full_spec((1, 128))],
        out_shape=[jax.ShapeDtypeStruct((s, 128), BF16), jax.ShapeDtypeStruct((1, 128), F32)],
        compiler_params=_cparams(("arbitrary",)),
    )(dcq16, sum_q16, f_logit, b_f)


ATT_T = 256


def _head_lanes(rows):
    return lax.broadcasted_iota(I32, (rows, 128), 1) < HEAD_DIM


def _bf16_pieces(c):
    p0 = c.astype(BF16).astype(F32)
    r = c - p0
    p1 = r.astype(BF16).astype(F32)
    p2 = (r - p1).astype(BF16).astype(F32)
    return p0, p1, p2


def _col_reduce(x, op):
    rows = x.shape[0]
    while rows > 8:
        rows //= 2
        x = op(x[:rows], x[rows:])
    return jnp.max(x, axis=0, keepdims=True) if op is jnp.maximum else jnp.sum(x, axis=0, keepdims=True)


def _attn_prep(qkv, cqe, *, name):
    s = qkv.shape[0]
    npair = HEADS // 2

    def body(q_ref, k_ref, v_ref, c_ref, qa_ref, ka_ref, vt_ref):
        rows = 128
        lane = lax.broadcasted_iota(I32, (rows, 128), 1)

        def chunk(n, _):
            r0 = pl.multiple_of(n * rows, rows)
            sl = pl.ds(r0, rows)
            qv = q_ref[sl, :].astype(F32) * ATT_SCALE
            kv = k_ref[sl, :].astype(F32)
            for e in range(2):
                mine = (lane < HEAD_DIM) if e == 0 else (lane >= HEAD_DIM)
                base = HEAD_DIM * (1 - e)
                p0, p1, p2 = _bf16_pieces(c_ref[sl, HEAD_DIM * e:HEAD_DIM * e + 1])
                ones_hi = jnp.where((lane >= base + 3) & (lane < base + 6), 1.0, 0.0)
                ones_lo = jnp.where((lane >= base) & (lane < base + 3), 1.0, 0.0)
                qa = jnp.where(mine, qv, jnp.where(lane == base, p0, jnp.where(lane == base + 1, p1,
                               jnp.where(lane == base + 2, p2, ones_hi))))
                ka = jnp.where(mine, kv, jnp.where(lane == base + 3, -p0, jnp.where(lane == base + 4, -p1,
                               jnp.where(lane == base + 5, -p2, ones_lo))))
                qa_ref[e, sl, :] = qa.astype(BF16)
                ka_ref[e, sl, :] = ka.astype(BF16)
            vt_ref[0, :, sl] = v_ref[sl, :].astype(F32).T.astype(BF16)
            return 0

        lax.fori_loop(0, s // rows, chunk, 0)

    pair = pl.BlockSpec((2, s, 128), lambda hp: (hp, 0, 0))
    return pl.pallas_call(
        body, name=name, grid=(npair,),
        in_specs=[pl.BlockSpec((s, 128), lambda hp: (0, hp)), pl.BlockSpec((s, 128), lambda hp: (0, npair + hp)),
                  pl.BlockSpec((s, 128), lambda hp: (0, 2 * npair + hp)), pl.BlockSpec((s, 128), lambda hp: (0, hp))],
        out_specs=[pair, pair, pl.BlockSpec((1, 128, s), lambda hp: (hp, 0, 0))],
        out_shape=[jax.ShapeDtypeStruct((HEADS, s, 128), BF16), jax.ShapeDtypeStruct((HEADS, s, 128), BF16),
                   jax.ShapeDtypeStruct((npair, 128, s), BF16)],
        compiler_params=_cparams(("parallel",)),
    )(qkv, qkv, qkv, cqe)


def _attn_fwd(qa, ka, vt, carry=None, *, name):
    s = qa.shape[1]
    t = ATT_T
    nq = s // t
    npair = HEADS // 2

    def body(qa_ref, ka_ref, vt_ref, o_ref, lse_ref):
        i = pl.program_id(1)
        krow = lax.broadcasted_iota(I32, (t, t), 0)
        qcol = lax.broadcasted_iota(I32, (t, t), 1)
        sub = lax.broadcasted_iota(I32, (128, t), 0)
        row8 = lax.broadcasted_iota(I32, (8, t), 0)
        qbs = (qa_ref[0], qa_ref[1])
        tk = 2 * t

        def step(j, carry, diag):
            c0 = pl.multiple_of(j * tk, tk)
            vtb = vt_ref[0, :, pl.ds(c0, tk)]
            sts = [_dot(ka_ref[e, pl.ds(c0, tk), :], qbs[e], NT) for e in range(2)]
            if diag:
                keep = (lax.broadcasted_iota(I32, (tk, t), 0) - lax.broadcasted_iota(I32, (tk, t), 1)) <= t * (i % 2)
                sts = [jnp.where(keep, st, NEG) for st in sts]
            pts, stats = [], []
            for e in range(2):
                m, l, _ = carry[e]
                m_new = jnp.maximum(m, _col_reduce(sts[e], jnp.maximum))
                alpha = jnp.exp(m - m_new)
                pt = jnp.exp(sts[e] - m_new)
                stats.append((m_new, alpha, alpha * l + _col_reduce(pt, jnp.add)))
                pts.append(pt.astype(BF16))
            pvs = [_dot(vtb, pts[e], NN) for e in range(2)]
            return tuple((stats[e][0], stats[e][2], stats[e][1] * carry[e][2] + pvs[e]) for e in range(2))

        init = (jnp.full((1, t), NEG, F32), jnp.zeros((1, t), F32), jnp.zeros((128, t), F32))
        carry = lax.fori_loop(0, i // 2, functools.partial(step, diag=False), (init, init))
        (m0, l0, acc0), (m1, l1, acc1) = step(i // 2, carry, True)
        o_pair = jnp.where(sub < HEAD_DIM, acc0 / l0, acc1 / l1)
        o_ref[...] = o_pair.T.astype(BF16)
        lse_ref[0] = jnp.where(row8 == 0, m0 + jnp.log(l0), jnp.where(row8 == 1, m1 + jnp.log(l1), 0.0))

    return _carry_call(
        body, carry, name=name, grid=(npair, nq),
        in_specs=[pl.BlockSpec((2, t, 128), lambda hp, i: (hp, i, 0)), pl.BlockSpec((2, s, 128), lambda hp, i: (hp, 0, 0)),
                  pl.BlockSpec((1, 128, s), lambda hp, i: (hp, 0, 0))],
        out_specs=[pl.BlockSpec((t, 128), lambda hp, i: (i, hp)), pl.BlockSpec((1, 8, t), lambda hp, i: (hp, 0, i))],
        out_shape=[jax.ShapeDtypeStruct((s, HEADS * HEAD_DIM), BF16), jax.ShapeDtypeStruct((npair, 8, s), F32)],
        scratch_shapes=[], args=[qa, ka, vt])


def _attn_delta(do, o, *, name):
    s = do.shape[0]

    def body(do_ref, o_ref, d_ref):
        prod = do_ref[...].astype(F32) * o_ref[...].astype(F32)
        row = lax.broadcasted_iota(I32, (8, 128), 0)
        lane = lax.broadcasted_iota(I32, (8, 128), 1)
        sel = ((row == 0) & (lane < HEAD_DIM) | (row == 1) & (lane >= HEAD_DIM)).astype(BF16)
        p0, p1, p2 = _bf16_pieces(prod)
        d_ref[0] = (_dot(sel, p0.astype(BF16), NT) + _dot(sel, p1.astype(BF16), NT)) + _dot(sel, p2.astype(BF16), NT)

    pair = pl.BlockSpec((s, 128), lambda hp: (0, hp))
    return pl.pallas_call(
        body, name=name, grid=(HEADS // 2,), in_specs=[pair, pair],
        out_specs=pl.BlockSpec((1, 8, s), lambda hp: (hp, 0, 0)),
        out_shape=jax.ShapeDtypeStruct((HEADS // 2, 8, s), F32), compiler_params=_cparams(("parallel",)),
    )(do, o)


def _attn_bwd(qa, ka, qkv, do, lse3, delta3, carry=None, *, name):
    s = qa.shape[1]
    t = ATT_T
    nb = s // t
    npair = HEADS // 2

    def body(qa_ref, ka_ref, v_ref, do_ref, lse_ref, delta_ref, dq_ref, dk_ref, dv_ref, aux_ref, dcq_ref, dqt):
        hp = pl.program_id(0)
        first = _head_lanes(t)
        lane = lax.broadcasted_iota(I32, (t, 128), 1)
        dqt[...] = jnp.zeros_like(dqt)

        @pl.when(hp == 0)
        def _():
            aux_ref[...] = jnp.zeros_like(aux_ref)

        krow = lax.broadcasted_iota(I32, (t, t), 0)
        qcol = lax.broadcasted_iota(I32, (t, t), 1)

        def key_block(j, _):
            c0 = pl.multiple_of(j * t, t)
            vb = v_ref[pl.ds(c0, t), :]
            kbs = (ka_ref[0, pl.ds(c0, t), :], ka_ref[1, pl.ds(c0, t), :])
            kbts = tuple(kb.astype(F32).T.astype(BF16) for kb in kbs)
            vhs = (jnp.where(first, vb, jnp.zeros_like(vb)), jnp.where(first, jnp.zeros_like(vb), vb))

            def query_block(i, carry, diag):
                r0 = pl.multiple_of(i * t, t)
                dob = do_ref[pl.ds(r0, t), :]
                qbs = [qa_ref[e, pl.ds(r0, t), :] for e in range(2)]
                sts = [_dot(kbs[e], qbs[e], NT) for e in range(2)]
                dpts = [_dot(vhs[e], dob, NT) for e in range(2)]
                ptbs, dsbs = [], []
                for e in range(2):
                    st = jnp.where(krow <= qcol, sts[e], NEG) if diag else sts[e]
                    pt = jnp.exp(st - lse_ref[0, e:e + 1, pl.ds(r0, t)])
                    dsbs.append((pt * (dpts[e] - delta_ref[0, e:e + 1, pl.ds(r0, t)])).astype(BF16))
                    ptbs.append(pt.astype(BF16))
                out = []
                for e in range(2):
                    dk_a, dv_a = carry[e]
                    dv_a = dv_a + _dot(ptbs[e], dob, NN)
                    dk_a = dk_a + _dot(dsbs[e], qbs[e], NN)
                    dqt[e, :, pl.ds(r0, t)] += _dot(kbts[e], dsbs[e], NN)
                    out.append((dk_a, dv_a))
                return tuple(out)

            zero = jnp.zeros((t, 128), F32)
            carry = query_block(j, ((zero, zero), (zero, zero)), True)
            (dk0, dv0), (dk1, dv1) = lax.fori_loop(j + 1, nb, functools.partial(query_block, diag=False), carry)
            dk_ref[pl.ds(c0, t), :] = jnp.where(first, dk0, dk1).astype(BF16)
            dv_ref[pl.ds(c0, t), :] = jnp.where(first, dv0, dv1).astype(BF16)
            sum_q = jnp.where(lane == 2 * hp, dk0[:, HEAD_DIM + 3:HEAD_DIM + 4],
                              jnp.where(lane == 2 * hp + 1, dk1[:, 3:4], aux_ref[pl.ds(c0, t), :]))
            aux_ref[pl.ds(c0, t), :] = sum_q
            return 0

        lax.fori_loop(0, nb, key_block, 0)
        sub = lax.broadcasted_iota(I32, (128, s), 0)
        row8 = lax.broadcasted_iota(I32, (8, s), 0)
        dq_ref[...] = (jnp.where(sub < HEAD_DIM, dqt[0], dqt[1]) * ATT_SCALE).T.astype(BF16)
        dcq_ref[0] = jnp.where(row8 == 0, dqt[0, HEAD_DIM:HEAD_DIM + 1, :], jnp.where(row8 == 1, dqt[1, 0:1, :], 0.0))

    def pair_cols(off):
        return pl.BlockSpec((s, 128), lambda hp: (0, off + hp))

    heads = pl.BlockSpec((2, s, 128), lambda hp: (hp, 0, 0))
    rows = pl.BlockSpec((1, 8, s), lambda hp: (hp, 0, 0))
    wide = jax.ShapeDtypeStruct((s, HEADS * HEAD_DIM), BF16)
    return _carry_call(
        body, carry, name=name, grid=(npair,),
        in_specs=[heads, heads, pair_cols(2 * npair), pair_cols(0), rows, rows],
        out_specs=[pair_cols(0), pair_cols(0), pair_cols(0), pl.BlockSpec((s, 128), lambda hp: (0, 0)), rows],
        out_shape=[wide, wide, wide, jax.ShapeDtypeStruct((s, 128), F32), jax.ShapeDtypeStruct((npair, 8, s), F32)],
        scratch_shapes=[pltpu.VMEM((2, 128, s), F32)], args=[qa, ka, qkv, do, lse3, delta3])


def _adam_math(w, g, m, v):
    m = ADAM_B1 * m + (1.0 - ADAM_B1) * g
    v = ADAM_B2 * v + (1.0 - ADAM_B2) * (g * g)
    m_hat = m / (1.0 - ADAM_B1 ** ADAM_STEP)
    v_hat = v / (1.0 - ADAM_B2 ** ADAM_STEP)
    delta = -ADAM_LR * (m_hat / (jnp.sqrt(v_hat) + ADAM_EPS) + ADAM_WD * w)
    return delta, m, v


def _sum_pairs(keep, recv, pos, *, name):
    _, r, c = recv.shape
    tr = _row_tile(r, 512)

    def body(pos_ref, a_ref, b_ref, o32_ref, o16_ref):
        tot = a_ref[...].astype(F32) + b_ref[...].astype(F32)
        o32_ref[...] = tot
        o16_ref[...] = tot.astype(BF16)

    out = pl.BlockSpec((1, tr, c), lambda q, i, pos: (q, i, 0))
    grid_spec = pltpu.PrefetchScalarGridSpec(
        num_scalar_prefetch=1, grid=(4, r // tr),
        in_specs=[pl.BlockSpec((1, tr, c), lambda q, i, pos: (2 * q + pos[2], i, 0)), out],
        out_specs=[out, out])
    return pl.pallas_call(
        body, name=name, grid_spec=grid_spec,
        out_shape=[jax.ShapeDtypeStruct((4, r, c), F32), jax.ShapeDtypeStruct((4, r, c), BF16)],
        compiler_params=_cparams(("parallel", "parallel")),
    )(pos, keep, recv)


def _adam_sharded(psum, recv, w, m, v, pos, *, name):
    r, c = w.shape
    tr = _row_tile(r, 320)

    def body(pos_ref, p_ref, r_ref, w_ref, m_ref, v_ref, g_ref, d_ref, mo_ref, vo_ref):
        g = p_ref[0] + r_ref[0].astype(F32) + r_ref[1].astype(F32) + r_ref[2].astype(F32)
        delta, mn, vn = _adam_math(w_ref[...], g, m_ref[...], v_ref[...])
        g_ref[...] = g
        d_ref[...] = delta
        mo_ref[...] = mn
        vo_ref[...] = vn

    row = pl.BlockSpec((tr, c), lambda i, pos: (i, 0))
    grid_spec = pltpu.PrefetchScalarGridSpec(
        num_scalar_prefetch=1, grid=(r // tr,),
        in_specs=[pl.BlockSpec((1, tr, c), lambda i, pos: (2 * pos[0] + pos[1], i, 0)),
                  pl.BlockSpec((3, tr, c), lambda i, pos: (0, i, 0)), row, row, row],
        out_specs=[row, row, row, row])
    o = jax.ShapeDtypeStruct((r, c), F32)
    return pl.pallas_call(
        body, name=name, grid_spec=grid_spec, out_shape=[o, o, o, o],
        compiler_params=_cparams(("parallel",)),
    )(pos, psum, recv, w, m, v)


def _adam_replicated(chip_sums, last, w, m, v, *, name):
    r = w.shape[0]

    def body(s_ref, l_ref, w_ref, m_ref, v_ref, g_ref, d_ref, mo_ref, vo_ref):
        g = (((s_ref[0] + s_ref[1]) + s_ref[2]) + s_ref[3]) + l_ref[...]
        delta, mn, vn = _adam_math(w_ref[...], g, m_ref[...], v_ref[...])
        g_ref[...] = g
        d_ref[...] = delta
        mo_ref[...] = mn
        vo_ref[...] = vn

    o = jax.ShapeDtypeStruct((r, 1024), F32)
    full = _full_spec((r, 1024))
    return pl.pallas_call(
        body, name=name, grid=(1,),
        in_specs=[_full_spec((4, r, 1024)), full, full, full, full], out_specs=[full] * 4, out_shape=[o] * 4,
        compiler_params=_cparams(("arbitrary",)),
    )(chip_sums, last, w, m, v)


def _pair_sum_small(mine, theirs, *, name):
    def body(a_ref, b_ref, o_ref):
        o_ref[...] = a_ref[...] + b_ref[...]

    full = _full_spec(mine.shape)
    return pl.pallas_call(
        body, name=name, grid=(1,), in_specs=[full, full], out_specs=full,
        out_shape=jax.ShapeDtypeStruct(mine.shape, F32), compiler_params=_cparams(("arbitrary",)),
    )(mine, theirs)


ANY = pl.BlockSpec(memory_space=pl.ANY)
OTHER_CHIPS = ((1, 0), (0, 1), (1, 1))


class _Carry:
    def __init__(self, inputs, out_shapes, scratch, start, wait, aliases=None):
        self.inputs, self.out_shapes, self.scratch = list(inputs), list(out_shapes), list(scratch)
        self.start, self.wait, self.aliases = start, wait, dict(aliases or {})


def _carried(body, carry, n_in, n_out, grid):
    if carry is None:
        return body
    ci, co, cs = len(carry.inputs), len(carry.out_shapes), len(carry.scratch)

    def wrapped(*refs):
        ins, cins = refs[:n_in], refs[n_in:n_in + ci]
        outs, couts = refs[n_in + ci:n_in + ci + n_out], refs[n_in + ci + n_out:n_in + ci + n_out + co]
        rest = refs[n_in + ci + n_out + co:]
        scratch, cscr = rest[:len(rest) - cs], rest[len(rest) - cs:]
        first, last = None, None
        for axis, size in enumerate(grid):
            f, l = pl.program_id(axis) == 0, pl.program_id(axis) == size - 1
            first = f if first is None else first & f
            last = l if last is None else last & l

        @pl.when(first)
        def _():
            carry.start(cins, couts, cscr)

        body(*ins, *outs, *scratch)

        @pl.when(last)
        def _():
            carry.wait(cins, couts, cscr)

    return wrapped


def _carry_call(body, carry, *, name, grid, in_specs, out_specs, out_shape, scratch_shapes, args, vmem=True,
                own_aliases=None):
    n_in, n_out = len(in_specs), len(out_specs)
    extra_in = [ANY] * len(carry.inputs) if carry else []
    extra_out = [ANY] * len(carry.out_shapes) if carry else []
    aliases = dict(own_aliases or {})
    if carry:
        aliases.update({n_in + i: n_out + o for i, o in carry.aliases.items()})
    out = pl.pallas_call(
        _carried(body, carry, n_in, n_out, grid), name=name, grid=grid,
        in_specs=list(in_specs) + extra_in, out_specs=list(out_specs) + extra_out,
        out_shape=list(out_shape) + (carry.out_shapes if carry else []),
        scratch_shapes=list(scratch_shapes) + (carry.scratch if carry else []),
        input_output_aliases=aliases,
        compiler_params=_cparams(("arbitrary",) * len(grid)) if vmem else None,
    )(*args, *(carry.inputs if carry else []))
    return list(out[:n_out]), list(out[n_out:])


def _run_carry(carry, *, name):
    return _carry_call(lambda: None, carry, name=name, grid=(1,), in_specs=[], out_specs=[], out_shape=[],
                       scratch_shapes=[], args=[], vmem=False)[1]


def _sems(n):
    return [pltpu.SemaphoreType.DMA((n,)), pltpu.SemaphoreType.DMA((n,))]


def _carry_gather1(shards):
    n = len(shards)

    def copies(x_refs, out_refs, scr, with_arrivals):
        send_sems, recv_sems, local_sems = scr
        x, y, c = lax.axis_index("x"), lax.axis_index("y"), lax.axis_index("c")
        peers = [(x, y, 1 - c)] + [(x ^ fx, y ^ fy, c) for fx, fy in OTHER_CHIPS]
        local, sends, arrivals = [], [], []
        for t, (x_ref, out_ref) in enumerate(zip(x_refs, out_refs)):
            local.append(pltpu.make_async_copy(x_ref, out_ref.at[4 * x + 2 * y + c], local_sems.at[t]))
            for k, (px, py, pc) in enumerate(peers):
                sems = dict(send_sem=send_sems.at[4 * t + k], recv_sem=recv_sems.at[4 * t + k],
                            device_id=(px, py, pc), device_id_type=MESH)
                sends.append(pltpu.make_async_remote_copy(src_ref=x_ref, dst_ref=out_ref.at[4 * x + 2 * y + c], **sems))
                if with_arrivals:
                    arrivals.append(
                        pltpu.make_async_remote_copy(src_ref=x_ref, dst_ref=out_ref.at[4 * px + 2 * py + pc], **sems))
        return local, sends, arrivals

    def start(x_refs, out_refs, scr):
        local, sends, _ = copies(x_refs, out_refs, scr, False)
        for cp in local + sends:
            cp.start()

    def wait(x_refs, out_refs, scr):
        local, sends, arrivals = copies(x_refs, out_refs, scr, True)
        for cp in arrivals:
            cp.wait_recv()
        for cp in sends:
            cp.wait_send()
        for cp in local:
            cp.wait()

    return _Carry(shards, [jax.ShapeDtypeStruct((N_DEV,) + a.shape, a.dtype) for a in shards],
                  _sems(4 * n) + [pltpu.SemaphoreType.DMA((n,))], start, wait)


def _carry_gather2(gathered):
    n = len(gathered)

    def copies(in_refs, g_refs, scr, with_arrivals):
        send_sems, recv_sems = scr
        x, y, c = lax.axis_index("x"), lax.axis_index("y"), lax.axis_index("c")
        sends, arrivals = [], []
        for t in range(n):
            for j, (fx, fy) in enumerate(OTHER_CHIPS):
                px, py = x ^ fx, y ^ fy
                sems = dict(send_sem=send_sems.at[3 * t + j], recv_sem=recv_sems.at[3 * t + j],
                            device_id=(x, y, 1 - c), device_id_type=MESH)
                mine, theirs = 4 * px + 2 * py + c, 4 * px + 2 * py + (1 - c)
                sends.append(pltpu.make_async_remote_copy(src_ref=in_refs[t].at[mine], dst_ref=g_refs[t].at[mine], **sems))
                if with_arrivals:
                    arrivals.append(pltpu.make_async_remote_copy(
                        src_ref=in_refs[t].at[mine], dst_ref=g_refs[t].at[theirs], **sems))
        return sends, arrivals

    def start(in_refs, g_refs, scr):
        for cp in copies(in_refs, g_refs, scr, False)[0]:
            cp.start()

    def wait(in_refs, g_refs, scr):
        sends, arrivals = copies(in_refs, g_refs, scr, True)
        for cp in arrivals:
            cp.wait_recv()
        for cp in sends:
            cp.wait_send()

    return _Carry(gathered, [jax.ShapeDtypeStruct(a.shape, a.dtype) for a in gathered], _sems(3 * n), start, wait,
                  aliases={t: t for t in range(n)})


def _allreduce_rows(x, *, name):
    def body(x_ref, o_ref, sib_ref, mine_ref, tab_ref, send_sems, recv_sems):
        x, y, c = lax.axis_index("x"), lax.axis_index("y"), lax.axis_index("c")
        swap = pltpu.make_async_remote_copy(src_ref=x_ref, dst_ref=sib_ref, send_sem=send_sems.at[0],
                                            recv_sem=recv_sems.at[0], device_id=(x, y, 1 - c), device_id_type=MESH)
        swap.start()
        swap.wait()
        mine_ref[...] = x_ref[...] + sib_ref[...]
        tab_ref[pl.ds(2 * x + y, 1)] = mine_ref[...][None]

        def copy(k, slot):
            fx, fy = OTHER_CHIPS[k]
            return pltpu.make_async_remote_copy(
                src_ref=mine_ref, dst_ref=tab_ref.at[slot], send_sem=send_sems.at[1 + k], recv_sem=recv_sems.at[1 + k],
                device_id=(x ^ fx, y ^ fy, c), device_id_type=MESH)

        for k in range(3):
            copy(k, 2 * x + y).start()
        for k, (fx, fy) in enumerate(OTHER_CHIPS):
            copy(k, 2 * (x ^ fx) + (y ^ fy)).wait()
        o_ref[...] = ((tab_ref[0] + tab_ref[1]) + tab_ref[2]) + tab_ref[3]

    vmem = pl.BlockSpec(memory_space=pltpu.VMEM)
    return pl.pallas_call(
        body, name=name, out_shape=jax.ShapeDtypeStruct(x.shape, F32), in_specs=[vmem], out_specs=vmem,
        scratch_shapes=[pltpu.VMEM(x.shape, F32), pltpu.VMEM(x.shape, F32), pltpu.VMEM((4,) + x.shape, F32)] + _sems(4),
    )(x)


def _allgather(shards, *, name):
    n = len(shards)

    def body(*refs):
        x_refs, out_refs = refs[:n], refs[n:2 * n]
        send_sems, recv_sems, local_sems = refs[2 * n:]
        x, y, c = lax.axis_index("x"), lax.axis_index("y"), lax.axis_index("c")
        me, sibling = (x, y, c), (x, y, 1 - c)
        chips = [(x ^ fx, y ^ fy) for fx, fy in OTHER_CHIPS]

        def copy(t, k, block, to, from_input=False):
            px, py, pc = block
            slab = out_refs[t].at[4 * px + 2 * py + pc]
            return pltpu.make_async_remote_copy(
                src_ref=x_refs[t] if from_input else slab, dst_ref=slab,
                send_sem=send_sems.at[7 * t + k], recv_sem=recv_sems.at[7 * t + k], device_id=to, device_id_type=MESH)

        mine = [pltpu.make_async_copy(x_refs[t], out_refs[t].at[4 * x + 2 * y + c], local_sems.at[t]) for t in range(n)]
        for cp in mine:
            cp.start()
        first = []
        for t in range(n):
            first.append(copy(t, 0, me, sibling, from_input=True))
            first += [copy(t, 1 + j, me, (*chip, c), from_input=True) for j, chip in enumerate(chips)]
        for cp in first:
            cp.start()
        passed = []
        for j, chip in enumerate(chips):
            for t in range(n):
                copy(t, 1 + j, (*chip, c), me).wait_recv()
                fwd = copy(t, 4 + j, (*chip, c), sibling)
                fwd.start()
                passed.append(fwd)
        for t in range(n):
            copy(t, 0, sibling, me).wait_recv()
            for j, chip in enumerate(chips):
                copy(t, 4 + j, (*chip, 1 - c), me).wait_recv()
        for cp in first + passed:
            cp.wait_send()
        for cp in mine:
            cp.wait()

    return pl.pallas_call(
        body, name=name, out_shape=[jax.ShapeDtypeStruct((N_DEV,) + a.shape, a.dtype) for a in shards],
        in_specs=[ANY] * n, out_specs=[ANY] * n,
        scratch_shapes=[pltpu.SemaphoreType.DMA((7 * n,)), pltpu.SemaphoreType.DMA((7 * n,)),
                        pltpu.SemaphoreType.DMA((n,))],
    )(*shards)


def _carry_sibling(slabs, small=None):
    n = len(slabs)
    extra = [] if small is None else [small]

    def copies(in_refs, out_refs, scr):
        send_sems, recv_sems = scr
        x, y, c = lax.axis_index("x"), lax.axis_index("y"), lax.axis_index("c")
        sibling = (x, y, 1 - c)
        out = []
        for t in range(n):
            for q in range(4):
                out.append(pltpu.make_async_remote_copy(
                    src_ref=in_refs[t].at[2 * q + (1 - c)], dst_ref=out_refs[t].at[q],
                    send_sem=send_sems.at[4 * t + q], recv_sem=recv_sems.at[4 * t + q],
                    device_id=sibling, device_id_type=MESH))
        if extra:
            out.append(pltpu.make_async_remote_copy(
                src_ref=in_refs[n], dst_ref=out_refs[n], send_sem=send_sems.at[4 * n], recv_sem=recv_sems.at[4 * n],
                device_id=sibling, device_id_type=MESH))
        return out

    def start(*refs):
        for cp in copies(*refs):
            cp.start()

    def wait(*refs):
        for cp in copies(*refs):
            cp.wait()

    return _Carry(list(slabs) + extra,
                  [jax.ShapeDtypeStruct((4,) + a.shape[1:], a.dtype) for a in slabs]
                  + [jax.ShapeDtypeStruct(a.shape, a.dtype) for a in extra], _sems(4 * n + 1), start, wait)


def _carry_chips(psums, small_sum=None):
    n = len(psums)
    table = small_sum is not None

    def copies(in_refs, out_refs, scr, arrivals):
        send_sems, recv_sems = scr[0], scr[1]
        x, y, c = lax.axis_index("x"), lax.axis_index("y"), lax.axis_index("c")
        out = []
        for k, (fx, fy) in enumerate(OTHER_CHIPS):
            px, py = x ^ fx, y ^ fy
            for t in range(n):
                out.append(pltpu.make_async_remote_copy(
                    src_ref=in_refs[t].at[2 * px + py], dst_ref=out_refs[t].at[k],
                    send_sem=send_sems.at[3 * t + k], recv_sem=recv_sems.at[3 * t + k],
                    device_id=(px, py, c), device_id_type=MESH))
            if table:
                slot = 2 * px + py if arrivals else 2 * x + y
                out.append(pltpu.make_async_remote_copy(
                    src_ref=in_refs[n], dst_ref=out_refs[n].at[slot], send_sem=send_sems.at[3 * n + k],
                    recv_sem=recv_sems.at[3 * n + k], device_id=(px, py, c), device_id_type=MESH))
        return out

    def own(in_refs, out_refs, scr):
        x, y = lax.axis_index("x"), lax.axis_index("y")
        return pltpu.make_async_copy(in_refs[n], out_refs[n].at[2 * x + y], scr[2])

    def start(in_refs, out_refs, scr):
        if table:
            own(in_refs, out_refs, scr).start()
        for cp in copies(in_refs, out_refs, scr, False):
            cp.start()

    def wait(in_refs, out_refs, scr):
        for cp in copies(in_refs, out_refs, scr, True):
            cp.wait()
        if table:
            own(in_refs, out_refs, scr).wait()

    out_shapes = [jax.ShapeDtypeStruct((3,) + a.shape[1:], a.dtype) for a in psums]
    if table:
        out_shapes.append(jax.ShapeDtypeStruct((4,) + small_sum.shape, F32))
    return _Carry(list(psums) + ([small_sum] if table else []), out_shapes,
                  _sems(3 * n + 3) + ([pltpu.SemaphoreType.DMA] if table else []), start, wait)


def _to_comm(name, kind, block, dtype=BF16):
    a = block[0]
    if kind == "cols":
        a = a.T
        if name == "w_in":
            a = jnp.pad(a, ((0, IN_SHARD_PAD - IN_SHARD), (0, 0)))
    return a if kind == "f32" else a.astype(dtype)


def _from_comm(name, kind, a):
    if kind == "cols":
        if name == "w_in":
            a = a[:IN_SHARD]
        a = a.T
    return a[None]


def _assemble_weights(g):
    out = {}
    if "w_in" in g:
        wt_in = g["w_in"][:, :IN_SHARD].reshape(IN_COLS, D_MODEL)
        out["wt_main"] = jnp.concatenate([wt_in[:2048], wt_in[O_G:], wt_in[2048:O_F]], axis=0)
        out["wt_f"] = jnp.pad(wt_in[O_F:O_G], ((0, 128 - HEADS), (0, 0)))
    square = dict(w_branch_a="w_a", w_branch_b="w_b", w_out="w_out", w_ple_gate="w_pg")
    for long, short in square.items():
        if long in g:
            out[short] = g[long].reshape(D_MODEL, D_MODEL)
    if "w_up" in g:
        out["wt_up"] = g["w_up"].reshape(2 * D_FF, D_MODEL)
    if "conv_w" in g:
        out["conv_w"] = g["conv_w"].transpose(1, 0, 2).reshape(3, 2 * D_FF)
    if "w_down" in g:
        out["w_down"] = g["w_down"].reshape(D_FF, D_MODEL)
    if "w_ple" in g:
        out["wt_ple"] = g["w_ple"].reshape(D_MODEL, PLE_DIM)
    return out


def _grad_slabs(gr):
    out = {}
    if "wt_main" in gr:
        gm = gr["wt_main"]
        gt_in = jnp.concatenate([gm[:2048], gm[4096:], gr["wt_f"][:HEADS], gm[2048:4096]], axis=0)
        out["w_in"] = jnp.pad(gt_in.reshape(N_DEV, IN_SHARD, D_MODEL), ((0, 0), (0, IN_SHARD_PAD - IN_SHARD), (0, 0)))
    rows = dict(w_a="w_branch_a", w_b="w_branch_b", w_out="w_out", wt_up="w_up", w_down="w_down", w_pg="w_ple_gate")
    for short, long in rows.items():
        if short in gr:
            out[long] = gr[short].reshape(N_DEV, -1, D_MODEL)
    if "conv_w" in gr:
        out["conv_w"] = gr["conv_w"].reshape(3, N_DEV, -1).transpose(1, 0, 2)
    if "wt_ple" in gr:
        out["w_ple"] = gr["wt_ple"].reshape(N_DEV, -1, PLE_DIM)
    return {k: v.astype(BF16) for k, v in out.items()}


def _rows(a, rows):
    flat = a.reshape(-1)
    return jnp.pad(flat, (0, rows * 1024 - flat.shape[0])).reshape(rows, 1024)


def _pack_small(parts):
    return jnp.concatenate([_rows(parts[n].astype(F32), r) for n, r in SMALL], axis=0)


def _small(packed, name, shape):
    off, r = SMALL_OFF[name]
    n = math.prod(shape)
    return packed[off:off + r].reshape(-1)[:n].reshape(shape)


class _Exchanges:
    def __init__(self, later, shards, pos):
        self.later, self.shards, self.pos, self.reduced = later, shards, pos, {}

    def gather1(self):
        return _carry_gather1(self.shards)

    def gather2(self, level1):
        return _carry_gather2(level1)

    def weights(self, full):
        return _assemble_weights(dict(zip(self.later, full)))

    def sibling(self, grads):
        self.slabs = _grad_slabs(grads)
        return _carry_sibling([self.slabs[n] for n in self.later])

    def chips(self, from_sib):
        sums = [_sum_pairs(self.slabs[n], r, self.pos, name="sum_sibling_" + n) for n, r in zip(self.later, from_sib)]
        self.sums32 = [s32 for s32, _ in sums]
        return _carry_chips([s16 for _, s16 in sums])

    def chips_done(self, from_chips):
        self.reduced.update({n: (s32, r) for n, s32, r in zip(self.later, self.sums32, from_chips)})

    def w_in_chips(self, grads_in, small_g):
        slab = _grad_slabs(grads_in)["w_in"]
        from_sib, small_sib = _run_carry(_carry_sibling([slab], small_g), name="exchange_sibling_w_in")
        self.s32_in, s16 = _sum_pairs(slab, from_sib, self.pos, name="sum_sibling_w_in")
        return _carry_chips([s16], _pair_sum_small(small_g, small_sib, name="sum_sibling_small"))

    def w_in_chips_done(self, carried):
        from_chips, self.table = carried
        self.reduced["w_in"] = (self.s32_in, from_chips)


def _local_step(x, p, target, w, sm, ex=None):
    s = x.shape[0]
    mm = _matmul
    wt_main = w["wt_main"]
    conv_b = sm["conv_b"]
    bs_t = jnp.pad(sm["gmlp_b_s"].T, ((0, 0), (0, 128 - GROUPS)))
    b_f = jnp.pad(sm["b_f"], ((0, 0), (0, 128 - HEADS)))
    big = dict(tm=1024, tn=1024, tk=1024)
    whole_s = dict(tn=1024, tk=s)

    h = _rmsnorm_fwd(x, sm["norm_mix_g"], name="norm_mix")
    qkv = mm(h, wt_main, mode="nt", out_dtype=BF16, name="in_qkv", n=3072, b_off=4, **big)
    f_logit = mm(h, w["wt_f"], mode="nt", out_dtype=F32, name="in_f", tm=1024, tk=1024)
    cqe = _forget_cumsum(f_logit, b_f, name="forget_cumsum")
    qa, ka, vt = _attn_prep(qkv, cqe, name="attn_prep")
    uvg = dict(mode="nt", out_dtype=F32, name="in_uvg", n=4096, **big)
    if ex is None:
        (b, lse3), _ = _attn_fwd(qa, ka, vt, name="attn_fwd")
        zuvg = mm(h, wt_main, **uvg)
    else:
        (b, lse3), level1 = _attn_fwd(qa, ka, vt, ex.gather1(), name="attn_fwd")
        zuvg, full = mm(h, wt_main, carry=ex.gather2(level1), **uvg)
        w = {**w, **ex.weights(full)}
    a = _gmlp_fwd(zuvg, sm["gmlp_ln_g"], sm["gmlp_ln_b"], sm["gmlp_w_s"], bs_t, name="gmlp_fwd")
    wt_up, conv_w = w["wt_up"], w["conv_w"]
    ya = mm(a, w["w_a"], mode="nn", out_dtype=F32, name="branch_a", **big)
    yb = mm(b, w["w_b"], mode="nn", out_dtype=F32, name="branch_b", **big)
    merged = _merge_fwd(ya, yb, zuvg, name="merge_fwd")
    x1 = mm(merged, w["w_out"], mode="nn", out_dtype=F32, name="out_proj", add=x, **big)
    h2 = _rmsnorm_fwd(x1, sm["norm_ffn_g"], name="norm_ffn")
    up = mm(h2, wt_up, mode="nt", out_dtype=F32, name="up", tm=1024, tn=512, tk=1024)
    act = _convglu_fwd(up, conv_w, conv_b, name="convglu_fwd")
    x2 = mm(act, w["w_down"], mode="nn", out_dtype=F32, name="down", tm=1024, tn=1024, tk=1408, add=x1)
    h3 = _rmsnorm_fwd(x2, sm["norm_ple_g"], name="norm_ple")
    ple = mm(p, w["wt_ple"], mode="nt", out_dtype=F32, name="ple", tm=1024, tn=1024, tk=256)
    gp = mm(h3, w["w_pg"], mode="nn", out_dtype=F32, name="ple_gate", **big)
    x3 = _ple_fwd(x2, ple, gp, name="ple_fwd")

    loss, dx3, d_norm_final = _final_loss_bwd(x3, target, sm["norm_final_g"], name="loss_bwd")
    dple, dgp = _ple_bwd(dx3, ple, gp, name="ple_bwd")
    g_wt_ple = mm(dple, p, mode="tn", out_dtype=BF16, name="d_w_ple", tm=512, tn=256, tk=s)
    g_w_pg = mm(h3, dgp, mode="tn", out_dtype=BF16, name="d_w_pg", tm=256, **whole_s)
    dh3 = mm(dgp, w["w_pg"], mode="nt", out_dtype=F32, name="d_h3", **big)
    dx2, dx2b, d_norm_ple = _rmsnorm_bwd(dx3, dh3, x2, sm["norm_ple_g"], name="norm_ple_bwd")
    g_w_down = mm(act, dx2b, mode="tn", out_dtype=BF16, name="d_w_down", tm=256, **whole_s)
    dact = mm(dx2b, w["w_down"], mode="nt", out_dtype=BF16, name="d_act", tm=1024, tn=1408, tk=1024)
    dup_a, dup_g, dcw_a, dcw_g, dcb_a, dcb_g = _convglu_bwd(dact, up, conv_w, conv_b, name="convglu_bwd")
    g_wt_up = mm(dup_a, h2, mode="tn", out_dtype=BF16, name="d_w_up_a", tm=256, out_rows=2 * D_FF, **whole_s)
    g_wt_up = mm(dup_g, h2, mode="tn", out_dtype=BF16, name="d_w_up_g", tm=256, out_rows=2 * D_FF,
                 o_off=D_FF // 256, into=g_wt_up, **whole_s)
    dh2 = mm(dup_a, wt_up, mode="nn", out_dtype=F32, name="d_h2_a", tm=1024, tn=1024, tk=1408)
    dh2 = mm(dup_g, wt_up, mode="nn", out_dtype=F32, name="d_h2_g", tm=1024, tn=1024, tk=1408, b_off=2, add=dh2)
    dx1, dx1b, d_norm_ffn = _rmsnorm_bwd(dx2, dh2, x1, sm["norm_ffn_g"], name="norm_ffn_bwd")
    g_w_out = mm(merged, dx1b, mode="tn", out_dtype=BF16, name="d_w_out", tm=256, **whole_s)
    dmerged = mm(dx1b, w["w_out"], mode="nt", out_dtype=F32, name="d_merged", **big)
    dya, dyb, dga, dgb = _merge_bwd(dmerged, ya, yb, zuvg, name="merge_bwd")
    g_w_a = mm(a, dya, mode="tn", out_dtype=BF16, name="d_w_a", tm=256, **whole_s)
    g_w_b = mm(b, dyb, mode="tn", out_dtype=BF16, name="d_w_b", tm=256, **whole_s)
    da = mm(dya, w["w_a"], mode="nt", out_dtype=BF16, name="d_a", **big)
    db = mm(dyb, w["w_b"], mode="nt", out_dtype=BF16, name="d_b", **big)
    grads = dict(w_a=g_w_a, w_b=g_w_b, w_out=g_w_out, wt_up=g_wt_up, conv_w=jnp.concatenate([dcw_a, dcw_g], axis=1),
                 w_down=g_w_down, wt_ple=g_wt_ple, w_pg=g_w_pg)
    gmlp_args = (da, zuvg, sm["gmlp_ln_g"], sm["gmlp_ln_b"], sm["gmlp_w_s"], bs_t)
    delta3 = _attn_delta(db, b, name="attn_delta")
    attn_args = (qa, ka, qkv, db, lse3, delta3)
    if ex is None:
        (dzu, dzv, d_w_s, d_bs_t, d_ln_g, d_ln_b), _ = _gmlp_bwd(*gmlp_args, name="gmlp_bwd")
        (dq, dk, dv, aux, dcq3), _ = _attn_bwd(*attn_args, name="attn_bwd")
    else:
        (dzu, dzv, d_w_s, d_bs_t, d_ln_g, d_ln_b), from_sib = _gmlp_bwd(*gmlp_args, ex.sibling(grads), name="gmlp_bwd")
        (dq, dk, dv, aux, dcq3), from_chips = _attn_bwd(*attn_args, ex.chips(from_sib), name="attn_bwd")
        ex.chips_done(from_chips)
    dcq16 = jnp.pad(dcq3[:, :2, :].reshape(HEADS, s).T, ((0, 0), (0, 128 - HEADS)))
    dzf, d_b_f = _forget_bwd(dcq16, aux, f_logit, b_f, name="forget_bwd")
    dz = jnp.concatenate([dzu, dzv, dga, dgb, dq, dk, dv], axis=1)
    g_wt_main = mm(dz, h, mode="tn", out_dtype=BF16, name="d_w_main", tm=512, **whole_s)
    g_wt_f = mm(dzf, h, mode="tn", out_dtype=BF16, name="d_w_f", **whole_s)
    grads = dict(grads, wt_main=g_wt_main, wt_f=g_wt_f)
    small = dict(norm_mix_g=jnp.zeros((1, D_MODEL), F32), b_f=d_b_f[:, :HEADS], gmlp_ln_g=d_ln_g, gmlp_ln_b=d_ln_b,
                 gmlp_w_s=d_w_s, gmlp_b_s=d_bs_t[:, :GROUPS].T, norm_ffn_g=d_norm_ffn,
                 conv_b=jnp.concatenate([dcb_a, dcb_g], axis=1), norm_ple_g=d_norm_ple, norm_final_g=d_norm_final)
    dh_main = dict(mode="nn", out_dtype=F32, name="d_h_main", **big)
    if ex is None:
        dh = mm(dz, wt_main, **dh_main)
    else:
        dh, carried = mm(dz, wt_main, carry=ex.w_in_chips(dict(wt_main=g_wt_main, wt_f=g_wt_f), _pack_small(small)),
                         **dh_main)
        ex.w_in_chips_done(carried)
    dh = mm(dzf, w["wt_f"], mode="nn", out_dtype=F32, name="d_h_f", tm=1024, tn=1024, add=dh)
    dx0, _, d_norm_mix = _rmsnorm_bwd(dx1, dh, x, sm["norm_mix_g"], name="norm_mix_bwd")
    return loss, dx0, grads, dict(small, norm_mix_g=d_norm_mix)


def kernel(x, p, norm_mix_g, w_in, b_f, gmlp_ln_g, gmlp_ln_b, gmlp_w_s, gmlp_b_s, w_branch_a, w_branch_b, w_out, norm_ffn_g, w_up, conv_w, conv_b, w_down, norm_ple_g, w_ple, w_ple_gate, norm_final_g, loss_target, m_norm_mix_g, m_w_in, m_b_f, m_gmlp_ln_g, m_gmlp_ln_b, m_gmlp_w_s, m_gmlp_b_s, m_w_branch_a, m_w_branch_b, m_w_out, m_norm_ffn_g, m_w_up, m_conv_w, m_conv_b, m_w_down, m_norm_ple_g, m_w_ple, m_w_ple_gate, m_norm_final_g, v_norm_mix_g, v_w_in, v_b_f, v_gmlp_ln_g, v_gmlp_ln_b, v_gmlp_w_s, v_gmlp_b_s, v_w_branch_a, v_w_branch_b, v_w_out, v_norm_ffn_g, v_w_up, v_conv_w, v_conv_b, v_w_down, v_norm_ple_g, v_w_ple, v_w_ple_gate, v_norm_final_g):
    given = dict(locals())
    weights = {n: given[n] for n in WEIGHT_ORDER}
    mom_m = {n: given["m_" + n] for n in WEIGHT_ORDER}
    mom_v = {n: given["v_" + n] for n in WEIGHT_ORDER}
    pos = jnp.stack([lax.axis_index("x"), lax.axis_index("y"), lax.axis_index("c")]).astype(I32)
    names = [n for n, _ in SHARDED]
    kinds = dict(SHARDED)

    later = [n for n in names if n != "w_in"]

    first = _allgather([_to_comm("w_in", kinds["w_in"], weights["w_in"])], name="allgather_w_in")
    ex = _Exchanges(later, [_to_comm(n, kinds[n], weights[n]) for n in later], pos)

    sm = dict(norm_mix_g=norm_mix_g, b_f=b_f, gmlp_ln_g=gmlp_ln_g, gmlp_ln_b=gmlp_ln_b, gmlp_w_s=gmlp_w_s[0],
              gmlp_b_s=gmlp_b_s[0], norm_ffn_g=norm_ffn_g, conv_b=conv_b, norm_ple_g=norm_ple_g,
              norm_final_g=norm_final_g.reshape(1, D_MODEL))
    loss_part, dx0, grads, small = _local_step(
        x[0], p[0, 0], loss_target[0], _assemble_weights({"w_in": first[0]}), sm, ex)

    norm_mix_total = _allreduce_rows(_rows(small["norm_mix_g"], 8), name="allreduce_norm_mix")
    small_last = jnp.pad(norm_mix_total, ((0, SMALL_ROWS - 8), (0, 0)))

    grad, delta, new_m, new_v = {}, {}, {}, {}
    for n in names:
        s32, r = ex.reduced[n]
        outs = _adam_sharded(s32, r, *[_to_comm(n, kinds[n], src[n], F32) for src in (weights, mom_m, mom_v)], pos,
                             name="adam_" + n)
        grad[n], delta[n], new_m[n], new_v[n] = [_from_comm(n, kinds[n], o) for o in outs]
    replicated = [n for n, _ in SMALL]
    rep = lambda src: _pack_small({n: src[n] for n in replicated})
    packed = _adam_replicated(ex.table, small_last, rep(weights), rep(mom_m), rep(mom_v), name="adam_replicated")
    for out, pk in zip((grad, delta, new_m, new_v), packed):
        for n in replicated:
            out[n] = _small(pk, n, weights[n].shape)

    loss = lax.psum(loss_part[0, 0], ("x", "y", "c"))
    return (loss, dx0[None], *[grad[n] for n in WEIGHT_ORDER], *[delta[n] for n in WEIGHT_ORDER],
            *[new_m[n] for n in WEIGHT_ORDER], *[new_v[n] for n in WEIGHT_ORDER])
```

```python
import functools
import math

import jax
import jax.numpy as jnp
from jax import lax
from jax.experimental import pallas as pl
from jax.experimental.pallas import tpu as pltpu

F32 = jnp.float32
BF16 = jnp.bfloat16
I32 = jnp.int32

D_MODEL = 1024
GROUPS = 8
GDIM = 128
GBLOCK = 128
CHUNK = 64
HEADS = 16
HEAD_DIM = 64
D_FF = 2816
PLE_DIM = 256
EPS = 1e-6
N_DEV = 8
ATT_SCALE = HEAD_DIM ** -0.5
NEG = -1e30

ADAM_LR = 0.001
ADAM_B1 = 0.9
ADAM_B2 = 0.999
ADAM_EPS = 1e-08
ADAM_WD = 0.01
ADAM_STEP = 10

V7X_VMEM_LIMIT = 48 * 1024 * 1024
MESH = pl.DeviceIdType.MESH

O_F = 2 * 1024 + 3 * 1024
O_G = O_F + HEADS
IN_COLS = O_G + 2 * D_MODEL
MAIN_COLS = IN_COLS - HEADS
IN_SHARD = IN_COLS // N_DEV
IN_SHARD_PAD = 912

SHARDED = (("w_in", "cols"), ("w_branch_a", "rows"), ("w_branch_b", "rows"), ("w_out", "rows"), ("w_up", "cols"),
           ("conv_w", "f32"), ("w_down", "rows"), ("w_ple", "cols"), ("w_ple_gate", "rows"))

SMALL = (("norm_mix_g", 8), ("b_f", 8), ("gmlp_ln_g", 8), ("gmlp_ln_b", 8), ("gmlp_w_s", 128), ("gmlp_b_s", 8),
         ("norm_ffn_g", 8), ("conv_b", 8), ("norm_ple_g", 8), ("norm_final_g", 8))
SMALL_OFF = {}
_o = 0
for _n, _r in SMALL:
    SMALL_OFF[_n] = (_o, _r)
    _o += _r
SMALL_ROWS = _o

WEIGHT_ORDER = ("norm_mix_g", "w_in", "b_f", "gmlp_ln_g", "gmlp_ln_b", "gmlp_w_s", "gmlp_b_s", "w_branch_a",
                "w_branch_b", "w_out", "norm_ffn_g", "w_up", "conv_w", "conv_b", "w_down", "norm_ple_g", "w_ple",
                "w_ple_gate", "norm_final_g")


def _cparams(sem):
    return pltpu.CompilerParams(dimension_semantics=sem, vmem_limit_bytes=V7X_VMEM_LIMIT)


def _gelu(x):
    c = math.sqrt(2.0 / math.pi)
    return 0.5 * x * (1.0 + jnp.tanh(c * (x + 0.044715 * x * x * x)))


def _gelu_and_grad(x):
    c = math.sqrt(2.0 / math.pi)
    t = jnp.tanh(c * (x + 0.044715 * x * x * x))
    g = 0.5 * x * (1.0 + t)
    dg = 0.5 * (1.0 + t) + 0.5 * x * (1.0 - t * t) * (c * (1.0 + 3.0 * 0.044715 * x * x))
    return g, dg


def _sigmoid(x):
    return 1.0 / (1.0 + jnp.exp(-x))


def _dot(a, b, dims):
    return lax.dot_general(a, b, (dims, ((), ())), preferred_element_type=F32)


NN = ((1,), (0,))
NT = ((1,), (1,))
TN = ((0,), (0,))


def _row_tile(rows, most):
    best = None
    for t in range(16, min(rows, most) + 1, 16):
        if rows % t == 0:
            best = t
    return best if best is not None else rows


def _matmul(a, b, *, mode, out_dtype, name, tm=512, tn=512, tk=512, add=None, n=None, b_off=0,
            out_rows=None, o_off=0, into=None, carry=None):
    if mode == "tn":
        kdim, m = a.shape
    else:
        m, kdim = a.shape
    if n is None:
        n = b.shape[0] if mode == "nt" else b.shape[1]
    tm, tn, tk = min(tm, m), min(tn, n), min(tk, kdim)
    assert m % tm == 0 and n % tn == 0 and kdim % tk == 0, (name, m, n, kdim, tm, tn, tk)
    nk = kdim // tk
    dims = {"nn": NN, "nt": NT, "tn": TN}[mode]

    def finish(r, add_ref, o_ref):
        if add_ref is not None:
            r = add_ref[...].astype(F32) + r
        o_ref[...] = r.astype(out_dtype)

    def body(*refs):
        refs = list(refs)
        a_ref, b_ref = refs[:2]
        add_ref = refs[2] if add is not None else None
        o_ref = refs[2 + (add is not None) + (into is not None)]
        part = _dot(a_ref[...].astype(BF16), b_ref[...].astype(BF16), dims)
        if nk == 1:
            finish(part, add_ref, o_ref)
            return
        acc_ref = refs[-1]
        k = pl.program_id(2)

        @pl.when(k == 0)
        def _():
            acc_ref[...] = part

        @pl.when((k > 0) & (k < nk - 1))
        def _():
            acc_ref[...] += part

        @pl.when(k == nk - 1)
        def _():
            finish(acc_ref[...] + part, add_ref, o_ref)

    a_spec = pl.BlockSpec((tk, tm), lambda i, j, k: (k, i)) if mode == "tn" else pl.BlockSpec((tm, tk), lambda i, j, k: (i, k))
    if mode == "nt":
        b_spec = pl.BlockSpec((tn, tk), lambda i, j, k: (j + b_off, k))
    else:
        b_spec = pl.BlockSpec((tk, tn), lambda i, j, k: (k + b_off, j))
    o_spec = pl.BlockSpec((tm, tn), lambda i, j, k: (i + o_off, j))
    in_specs = [a_spec, b_spec] + ([pl.BlockSpec((tm, tn), lambda i, j, k: (i, j))] if add is not None else [])
    args = (a, b) + ((add,) if add is not None else ())
    aliases = {}
    if into is not None:
        aliases = {len(args): 0}
        in_specs.append(pl.BlockSpec(memory_space=pl.ANY))
        args += (into,)
    (out,), carried = _carry_call(
        body, carry, name=name, grid=(m // tm, n // tn, nk), in_specs=in_specs, out_specs=[o_spec],
        out_shape=[jax.ShapeDtypeStruct((m if out_rows is None else out_rows, n), out_dtype)],
        scratch_shapes=[pltpu.VMEM((tm, tn), F32)] if nk > 1 else [], args=args, own_aliases=aliases)
    return out if carry is None else (out, carried)


def _row_spec(tr, width, col_block=0):
    return pl.BlockSpec((tr, width), lambda i: (i, col_block))


def _full_spec(shape):
    return pl.BlockSpec(shape, lambda i: tuple(0 for _ in shape))


def _rmsnorm_fwd(x, g, *, name, tr=256):
    s, d = x.shape

    def body(x_ref, g_ref, o_ref):
        xv = x_ref[...]
        r = lax.rsqrt(jnp.mean(xv * xv, axis=-1, keepdims=True) + EPS)
        o_ref[...] = ((xv * r) * g_ref[...]).astype(BF16)

    return pl.pallas_call(
        body, name=name, grid=(s // tr,),
        in_specs=[_row_spec(tr, d), _full_spec((1, d))], out_specs=_row_spec(tr, d),
        out_shape=jax.ShapeDtypeStruct((s, d), BF16), compiler_params=_cparams(("parallel",)),
    )(x, g)


def _rmsnorm_bwd(dres, dh, x, g, *, name, tr=256):
    s, d = x.shape

    def body(dres_ref, dh_ref, x_ref, g_ref, dx_ref, dxb_ref, dg_ref):
        i = pl.program_id(0)
        xv = x_ref[...]
        r = lax.rsqrt(jnp.mean(xv * xv, axis=-1, keepdims=True) + EPS)
        xhat = xv * r
        dhv = dh_ref[...].astype(F32)
        dxhat = dhv * g_ref[...]
        dx = dres_ref[...] + r * (dxhat - xhat * jnp.mean(dxhat * xhat, axis=-1, keepdims=True))
        dx_ref[...] = dx
        dxb_ref[...] = dx.astype(BF16)
        dgp = jnp.sum(dhv * xhat, axis=0, keepdims=True)

        @pl.when(i == 0)
        def _():
            dg_ref[...] = dgp

        @pl.when(i > 0)
        def _():
            dg_ref[...] += dgp

    return pl.pallas_call(
        body, name=name, grid=(s // tr,),
        in_specs=[_row_spec(tr, d), _row_spec(tr, d), _row_spec(tr, d), _full_spec((1, d))],
        out_specs=[_row_spec(tr, d), _row_spec(tr, d), _full_spec((1, d))],
        out_shape=[jax.ShapeDtypeStruct((s, d), F32), jax.ShapeDtypeStruct((s, d), BF16),
                   jax.ShapeDtypeStruct((1, d), F32)],
        compiler_params=_cparams(("arbitrary",)),
    )(dres, dh, x, g)


def _final_loss_bwd(x3, target, g, *, name, tr=256):
    s, d = x3.shape

    def body(x_ref, t_ref, g_ref, loss_ref, dx_ref, dg_ref):
        i = pl.program_id(0)
        xv = x_ref[...]
        r = lax.rsqrt(jnp.mean(xv * xv, axis=-1, keepdims=True) + EPS)
        xhat = xv * r
        diff = xhat * g_ref[...] - t_ref[...]
        lp = jnp.zeros((1, 128), F32) + (0.5 / d) * jnp.sum(diff * diff)
        dy = diff * (1.0 / d)
        dxhat = dy * g_ref[...]
        dx_ref[...] = r * (dxhat - xhat * jnp.mean(dxhat * xhat, axis=-1, keepdims=True))
        dgp = jnp.sum(dy * xhat, axis=0, keepdims=True)

        @pl.when(i == 0)
        def _():
            dg_ref[...] = dgp
            loss_ref[...] = lp

        @pl.when(i > 0)
        def _():
            dg_ref[...] += dgp
            loss_ref[...] += lp

    return pl.pallas_call(
        body, name=name, grid=(s // tr,),
        in_specs=[_row_spec(tr, d), _row_spec(tr, d), _full_spec((1, d))],
        out_specs=[_full_spec((1, 128)), _row_spec(tr, d), _full_spec((1, d))],
        out_shape=[jax.ShapeDtypeStruct((1, 128), F32), jax.ShapeDtypeStruct((s, d), F32),
                   jax.ShapeDtypeStruct((1, d), F32)],
        compiler_params=_cparams(("arbitrary",)),
    )(x3, target, g)


def _merge_fwd(ya, yb, zuvg, *, name, tr=256):
    s, d = ya.shape

    def body(ya_ref, yb_ref, ga_ref, gb_ref, o_ref):
        o_ref[...] = (_sigmoid(ga_ref[...]) * ya_ref[...] + _sigmoid(gb_ref[...]) * yb_ref[...]).astype(BF16)

    return pl.pallas_call(
        body, name=name, grid=(s // tr,),
        in_specs=[_row_spec(tr, d), _row_spec(tr, d), _row_spec(tr, d, 2), _row_spec(tr, d, 3)],
        out_specs=_row_spec(tr, d),
        out_shape=jax.ShapeDtypeStruct((s, d), BF16), compiler_params=_cparams(("parallel",)),
    )(ya, yb, zuvg, zuvg)


def _merge_bwd(dm, ya, yb, zuvg, *, name, tr=256):
    s, d = ya.shape

    def body(dm_ref, ya_ref, yb_ref, ga_ref, gb_ref, dya_ref, dyb_ref, dga_ref, dgb_ref):
        dmv = dm_ref[...]
        sa = _sigmoid(ga_ref[...])
        sb = _sigmoid(gb_ref[...])
        dya_ref[...] = (dmv * sa).astype(BF16)
        dyb_ref[...] = (dmv * sb).astype(BF16)
        dga_ref[...] = (dmv * ya_ref[...] * (sa * (1.0 - sa))).astype(BF16)
        dgb_ref[...] = (dmv * yb_ref[...] * (sb * (1.0 - sb))).astype(BF16)

    o = jax.ShapeDtypeStruct((s, d), BF16)
    return pl.pallas_call(
        body, name=name, grid=(s // tr,),
        in_specs=[_row_spec(tr, d)] * 3 + [_row_spec(tr, d, 2), _row_spec(tr, d, 3)], out_specs=[_row_spec(tr, d)] * 4,
        out_shape=[o, o, o, o], compiler_params=_cparams(("parallel",)),
    )(dm, ya, yb, zuvg, zuvg)


def _ple_fwd(x2, ple, gp, *, name, tr=256):
    s, d = x2.shape

    def body(x_ref, ple_ref, gp_ref, o_ref):
        o_ref[...] = x_ref[...] + ple_ref[...] * _sigmoid(gp_ref[...])

    return pl.pallas_call(
        body, name=name, grid=(s // tr,),
        in_specs=[_row_spec(tr, d)] * 3, out_specs=_row_spec(tr, d),
        out_shape=jax.ShapeDtypeStruct((s, d), F32), compiler_params=_cparams(("parallel",)),
    )(x2, ple, gp)


def _ple_bwd(dx3, ple, gp, *, name, tr=256):
    s, d = dx3.shape

    def body(dx_ref, ple_ref, gp_ref, dple_ref, dgp_ref):
        sg = _sigmoid(gp_ref[...])
        dxv = dx_ref[...]
        dple_ref[...] = (dxv * sg).astype(BF16)
        dgp_ref[...] = (dxv * ple_ref[...] * (sg * (1.0 - sg))).astype(BF16)

    o = jax.ShapeDtypeStruct((s, d), BF16)
    return pl.pallas_call(
        body, name=name, grid=(s // tr,),
        in_specs=[_row_spec(tr, d)] * 3, out_specs=[_row_spec(tr, d)] * 2,
        out_shape=[o, o], compiler_params=_cparams(("parallel",)),
    )(dx3, ple, gp)


def _masked_ws(ws_ref, g):
    row = lax.broadcasted_iota(I32, (GBLOCK, GBLOCK), 0)
    col = lax.broadcasted_iota(I32, (GBLOCK, GBLOCK), 1)
    keep = (col // CHUNK) <= (row // CHUNK)
    return jnp.where(keep, ws_ref[g], 0.0), keep


def _layernorm_parts(zv):
    mu = jnp.mean(zv, axis=-1, keepdims=True)
    xc = zv - mu
    rs = lax.rsqrt(jnp.mean(xc * xc, axis=-1, keepdims=True) + EPS)
    return xc * rs, rs


def _gmlp_fwd(zuvg, ln_g, ln_b, w_s, bs_t, *, name):
    s, w = zuvg.shape[0], GROUPS * GDIM

    def body(zu_ref, zv_ref, lng_ref, lnb_ref, ws_ref, bs_ref, a_ref):
        zu = _gelu(zu_ref[...])
        zv = _gelu(zv_ref[...])
        xhat, _ = _layernorm_parts(zv)
        vln = (xhat * lng_ref[...] + lnb_ref[...]).astype(BF16)
        for g in range(GROUPS):
            wm, _ = _masked_ws(ws_ref, g)
            mixed = _dot(wm.astype(BF16), vln[:, g * GDIM:(g + 1) * GDIM], NN) + bs_ref[:, g:g + 1]
            a_ref[:, g * GDIM:(g + 1) * GDIM] = (zu[:, g * GDIM:(g + 1) * GDIM] * mixed).astype(BF16)

    return pl.pallas_call(
        body, name=name, grid=(s // GBLOCK,),
        in_specs=[_row_spec(GBLOCK, w, 0), _row_spec(GBLOCK, w, 1), _full_spec((1, w)), _full_spec((1, w)),
                  _full_spec((GROUPS, GBLOCK, GBLOCK)), _full_spec((GBLOCK, 128))],
        out_specs=_row_spec(GBLOCK, w),
        out_shape=jax.ShapeDtypeStruct((s, w), BF16), compiler_params=_cparams(("parallel",)),
    )(zuvg, zuvg, ln_g, ln_b, w_s, bs_t)


def _gmlp_bwd(da, zuvg, ln_g, ln_b, w_s, bs_t, carry=None, *, name):
    s, w = zuvg.shape[0], GROUPS * GDIM

    def body(da_ref, zu_ref, zv_ref, lng_ref, lnb_ref, ws_ref, bs_ref,
             dzu_ref, dzv_ref, dws_ref, dbs_ref, dlng_ref, dlnb_ref, dvln_ref):
        i = pl.program_id(0)
        zu, dzu_g = _gelu_and_grad(zu_ref[...])
        zv, dzv_g = _gelu_and_grad(zv_ref[...])
        xhat, rs = _layernorm_parts(zv)
        vln = (xhat * lng_ref[...] + lnb_ref[...]).astype(BF16)
        dav = da_ref[...].astype(F32)
        lane = lax.broadcasted_iota(I32, (GBLOCK, 128), 1)
        dbs = jnp.zeros((GBLOCK, 128), F32)

        @pl.when(i == 0)
        def _():
            dws_ref[...] = jnp.zeros_like(dws_ref)

        for g in range(GROUPS):
            sl = slice(g * GDIM, (g + 1) * GDIM)
            wm, keep = _masked_ws(ws_ref, g)
            wmb = wm.astype(BF16)
            vg = vln[:, sl]
            mixed = _dot(wmb, vg, NN) + bs_ref[:, g:g + 1]
            dag = dav[:, sl]
            dzu_ref[:, sl] = (dag * mixed * dzu_g[:, sl]).astype(BF16)
            dmix = dag * zu[:, sl]
            dmb = dmix.astype(BF16)
            dws_ref[g] += jnp.where(keep, _dot(dmb, vg, NT), 0.0)
            dbs = jnp.where(lane == g, jnp.sum(dmix, axis=1, keepdims=True), dbs)
            dvln_ref[:, sl] = _dot(wmb, dmb, TN)
        dvln = dvln_ref[...]
        dxhat = dvln * lng_ref[...]
        dzv = rs * (dxhat - jnp.mean(dxhat, axis=-1, keepdims=True)
                    - xhat * jnp.mean(dxhat * xhat, axis=-1, keepdims=True))
        dzv_ref[...] = (dzv * dzv_g).astype(BF16)
        dlng = jnp.sum(dvln * xhat, axis=0, keepdims=True)
        dlnb = jnp.sum(dvln, axis=0, keepdims=True)

        @pl.when(i == 0)
        def _():
            dbs_ref[...] = dbs
            dlng_ref[...] = dlng
            dlnb_ref[...] = dlnb

        @pl.when(i > 0)
        def _():
            dbs_ref[...] += dbs
            dlng_ref[...] += dlng
            dlnb_ref[...] += dlnb

    return _carry_call(
        body, carry, name=name, grid=(s // GBLOCK,),
        in_specs=[_row_spec(GBLOCK, w), _row_spec(GBLOCK, w, 0), _row_spec(GBLOCK, w, 1), _full_spec((1, w)),
                  _full_spec((1, w)), _full_spec((GROUPS, GBLOCK, GBLOCK)), _full_spec((GBLOCK, 128))],
        out_specs=[_row_spec(GBLOCK, w), _row_spec(GBLOCK, w), _full_spec((GROUPS, GBLOCK, GBLOCK)),
                   _full_spec((GBLOCK, 128)), _full_spec((1, w)), _full_spec((1, w))],
        out_shape=[jax.ShapeDtypeStruct((s, w), BF16), jax.ShapeDtypeStruct((s, w), BF16),
                   jax.ShapeDtypeStruct((GROUPS, GBLOCK, GBLOCK), F32), jax.ShapeDtypeStruct((GBLOCK, 128), F32),
                   jax.ShapeDtypeStruct((1, w), F32), jax.ShapeDtypeStruct((1, w), F32)],
        scratch_shapes=[pltpu.VMEM((GBLOCK, w), F32)], args=[da, zuvg, zuvg, ln_g, ln_b, w_s, bs_t])


def _shift_down(u, k):
    row = lax.broadcasted_iota(I32, u.shape, 0)
    return jnp.where(row >= k, pltpu.roll(u, k, 0), 0.0)


def _shift_up(u, k):
    s = u.shape[0]
    row = lax.broadcasted_iota(I32, u.shape, 0)
    return jnp.where(row < s - k, pltpu.roll(u, s - k, 0), 0.0)


def _conv(u, w_ref, b_ref):
    return b_ref[...] + w_ref[0:1, :] * _shift_down(u, 2) + w_ref[1:2, :] * _shift_down(u, 1) + w_ref[2:3, :] * u


def _conv_specs(s, f, tc):
    nc = f // tc
    half = lambda rows: [pl.BlockSpec((rows, tc), lambda j: (0, j)), pl.BlockSpec((rows, tc), lambda j: (0, nc + j))]
    return half(s), half(3), half(1)


def _convglu_fwd(up, conv_w, conv_b, *, name, tc=256):
    s, f = up.shape[0], up.shape[1] // 2
    up_specs, w_specs, b_specs = _conv_specs(s, f, tc)

    def body(ua_ref, ug_ref, wa_ref, wg_ref, ba_ref, bg_ref, o_ref):
        ca = _conv(ua_ref[...], wa_ref, ba_ref)
        cg = _conv(ug_ref[...], wg_ref, bg_ref)
        o_ref[...] = (_gelu(ca) * cg).astype(BF16)

    return pl.pallas_call(
        body, name=name, grid=(f // tc,),
        in_specs=up_specs + w_specs + b_specs, out_specs=up_specs[0],
        out_shape=jax.ShapeDtypeStruct((s, f), BF16), compiler_params=_cparams(("parallel",)),
    )(up, up, conv_w, conv_w, conv_b, conv_b)


def _convglu_bwd(dact, up, conv_w, conv_b, *, name, tc=256):
    s, f = up.shape[0], up.shape[1] // 2
    up_specs, w_specs, b_specs = _conv_specs(s, f, tc)

    def half(dc, u, w_ref, du_ref, dw_ref, db_ref):
        db_ref[...] = jnp.sum(dc, axis=0, keepdims=True)
        dw_ref[0:1, :] = jnp.sum(dc * _shift_down(u, 2), axis=0, keepdims=True)
        dw_ref[1:2, :] = jnp.sum(dc * _shift_down(u, 1), axis=0, keepdims=True)
        dw_ref[2:3, :] = jnp.sum(dc * u, axis=0, keepdims=True)
        du = w_ref[2:3, :] * dc + w_ref[1:2, :] * _shift_up(dc, 1) + w_ref[0:1, :] * _shift_up(dc, 2)
        du_ref[...] = du.astype(BF16)

    def body(d_ref, ua_ref, ug_ref, wa_ref, wg_ref, ba_ref, bg_ref,
             dua_ref, dug_ref, dwa_ref, dwg_ref, dba_ref, dbg_ref):
        ua = ua_ref[...]
        ug = ug_ref[...]
        ca = _conv(ua, wa_ref, ba_ref)
        cg = _conv(ug, wg_ref, bg_ref)
        ga, dga = _gelu_and_grad(ca)
        dv = d_ref[...].astype(F32)
        half(dv * cg * dga, ua, wa_ref, dua_ref, dwa_ref, dba_ref)
        half(dv * ga, ug, wg_ref, dug_ref, dwg_ref, dbg_ref)

    col, w3, b1 = up_specs[0], w_specs[0], b_specs[0]
    return pl.pallas_call(
        body, name=name, grid=(f // tc,),
        in_specs=[col] + up_specs + w_specs + b_specs, out_specs=[col, col, w3, w3, b1, b1],
        out_shape=[jax.ShapeDtypeStruct((s, f), BF16), jax.ShapeDtypeStruct((s, f), BF16),
                   jax.ShapeDtypeStruct((3, f), F32), jax.ShapeDtypeStruct((3, f), F32),
                   jax.ShapeDtypeStruct((1, f), F32), jax.ShapeDtypeStruct((1, f), F32)],
        compiler_params=_cparams(("parallel",)),
    )(dact, up, up, conv_w, conv_w, conv_b, conv_b)


def _tri_dot(tri, x):
    b0 = x.astype(BF16)
    r1 = x - b0.astype(F32)
    b1 = r1.astype(BF16)
    b2 = (r1 - b1.astype(F32)).astype(BF16)
    return _dot(tri, b0, NN) + _dot(tri, b1, NN) + _dot(tri, b2, NN)


def _log_sigmoid(x):
    return jnp.minimum(x, 0.0) - jnp.log(1.0 + jnp.exp(-jnp.abs(x)))


def _expand_heads(col16, rows):
    head_of_lane = lax.broadcasted_iota(I32, (rows, HEADS * HEAD_DIM), 1) // HEAD_DIM
    out = jnp.zeros((rows, HEADS * HEAD_DIM), F32)
    for h in range(HEADS):
        out = jnp.where(head_of_lane == h, col16[:, h:h + 1], out)
    return out


def _forget_cumsum(f_logit, b_f, *, name):
    s = f_logit.shape[0]
    nb = s // 128

    def body(f_ref, b_ref, cqe_ref):
        row = lax.broadcasted_iota(I32, (128, 128), 0)
        col = lax.broadcasted_iota(I32, (128, 128), 1)
        tri = (col <= row).astype(BF16)

        def step(n, carry):
            r0 = pl.multiple_of(n * 128, 128)
            lf = _log_sigmoid(f_ref[pl.ds(r0, 128), :] + b_ref[...])
            cum = _tri_dot(tri, lf) + carry
            cqe_ref[pl.ds(r0, 128), :] = _expand_heads(cum, 128)
            return cum[127:128, :]

        lax.fori_loop(0, nb, step, jnp.zeros((1, 128), F32))

    return pl.pallas_call(
        body, name=name, grid=(1,),
        in_specs=[_full_spec((s, 128)), _full_spec((1, 128))],
        out_specs=_full_spec((s, HEADS * HEAD_DIM)),
        out_shape=jax.ShapeDtypeStruct((s, HEADS * HEAD_DIM), F32),
        compiler_params=_cparams(("arbitrary",)),
    )(f_logit, b_f)


def _forget_bwd(dcq16, sum_q16, f_logit, b_f, *, name):
    s = f_logit.shape[0]
    nb = s // 128

    def body(a_ref, k_ref, f_ref, b_ref, df_ref, db_ref):
        row = lax.broadcasted_iota(I32, (128, 128), 0)
        col = lax.broadcasted_iota(I32, (128, 128), 1)
        tri_rev = (col >= row).astype(BF16)

        def step(m, carry):
            suffix, dbsum = carry
            n = nb - 1 - m
            r0 = pl.multiple_of(n * 128, 128)
            dcum = a_ref[pl.ds(r0, 128), :] - k_ref[pl.ds(r0, 128), :]
            dlf = _tri_dot(tri_rev, dcum) + suffix
            df = dlf * _sigmoid(-(f_ref[pl.ds(r0, 128), :] + b_ref[...]))
            df_ref[pl.ds(r0, 128), :] = df.astype(BF16)
            return dlf[0:1, :], dbsum + jnp.sum(df, axis=0, keepdims=True)

        _, dbsum = lax.fori_loop(0, nb, step, (jnp.zeros((1, 128), F32), jnp.zeros((1, 128), F32)))
        db_ref[...] = dbsum

    return pl.pallas_call(
        body, name=name, grid=(1,),
        in_specs=[_full_spec((s, 128))] * 3 + [_full_spec((1, 128))],
        out_specs=[_full_spec((s, 128)), _full_spec((1, 128))],
        out_shape=[jax.ShapeDtypeStruct((s, 128), BF16), jax.ShapeDtypeStruct((1, 128), F32)],
        compiler_params=_cparams(("arbitrary",)),
    )(dcq16, sum_q16, f_logit, b_f)


ATT_T = 256


def _head_lanes(rows):
    return lax.broadcasted_iota(I32, (rows, 128), 1) < HEAD_DIM


def _bf16_pieces(c):
    p0 = c.astype(BF16).astype(F32)
    r = c - p0
    p1 = r.astype(BF16).astype(F32)
    p2 = (r - p1).astype(BF16).astype(F32)
    return p0, p1, p2


def _col_reduce(x, op):
    rows = x.shape[0]
    while rows > 8:
        rows //= 2
        x = op(x[:rows], x[rows:])
    return jnp.max(x, axis=0, keepdims=True) if op is jnp.maximum else jnp.sum(x, axis=0, keepdims=True)


def _attn_prep(qkv, cqe, *, name):
    s = qkv.shape[0]
    npair = HEADS // 2

    def body(q_ref, k_ref, v_ref, c_ref, qa_ref, ka_ref, vt_ref):
        rows = 128
        lane = lax.broadcasted_iota(I32, (rows, 128), 1)

        def chunk(n, _):
            r0 = pl.multiple_of(n * rows, rows)
            sl = pl.ds(r0, rows)
            qv = q_ref[sl, :].astype(F32) * ATT_SCALE
            kv = k_ref[sl, :].astype(F32)
            for e in range(2):
                mine = (lane < HEAD_DIM) if e == 0 else (lane >= HEAD_DIM)
                base = HEAD_DIM * (1 - e)
                p0, p1, p2 = _bf16_pieces(c_ref[sl, HEAD_DIM * e:HEAD_DIM * e + 1])
                ones_hi = jnp.where((lane >= base + 3) & (lane < base + 6), 1.0, 0.0)
                ones_lo = jnp.where((lane >= base) & (lane < base + 3), 1.0, 0.0)
                qa = jnp.where(mine, qv, jnp.where(lane == base, p0, jnp.where(lane == base + 1, p1,
                               jnp.where(lane == base + 2, p2, ones_hi))))
                ka = jnp.where(mine, kv, jnp.where(lane == base + 3, -p0, jnp.where(lane == base + 4, -p1,
                               jnp.where(lane == base + 5, -p2, ones_lo))))
                qa_ref[e, sl, :] = qa.astype(BF16)
                ka_ref[e, sl, :] = ka.astype(BF16)
            vt_ref[0, :, sl] = v_ref[sl, :].astype(F32).T.astype(BF16)
            return 0

        lax.fori_loop(0, s // rows, chunk, 0)

    pair = pl.BlockSpec((2, s, 128), lambda hp: (hp, 0, 0))
    return pl.pallas_call(
        body, name=name, grid=(npair,),
        in_specs=[pl.BlockSpec((s, 128), lambda hp: (0, hp)), pl.BlockSpec((s, 128), lambda hp: (0, npair + hp)),
                  pl.BlockSpec((s, 128), lambda hp: (0, 2 * npair + hp)), pl.BlockSpec((s, 128), lambda hp: (0, hp))],
        out_specs=[pair, pair, pl.BlockSpec((1, 128, s), lambda hp: (hp, 0, 0))],
        out_shape=[jax.ShapeDtypeStruct((HEADS, s, 128), BF16), jax.ShapeDtypeStruct((HEADS, s, 128), BF16),
                   jax.ShapeDtypeStruct((npair, 128, s), BF16)],
        compiler_params=_cparams(("parallel",)),
    )(qkv, qkv, qkv, cqe)


def _attn_fwd(qa, ka, vt, carry=None, *, name):
    s = qa.shape[1]
    t = ATT_T
    nq = s // t
    npair = HEADS // 2

    def body(qa_ref, ka_ref, vt_ref, o_ref, lse_ref):
        i = pl.program_id(1)
        krow = lax.broadcasted_iota(I32, (t, t), 0)
        qcol = lax.broadcasted_iota(I32, (t, t), 1)
        sub = lax.broadcasted_iota(I32, (128, t), 0)
        row8 = lax.broadcasted_iota(I32, (8, t), 0)
        qbs = (qa_ref[0], qa_ref[1])
        tk = 2 * t

        def step(j, carry, diag):
            c0 = pl.multiple_of(j * tk, tk)
            vtb = vt_ref[0, :, pl.ds(c0, tk)]
            sts = [_dot(ka_ref[e, pl.ds(c0, tk), :], qbs[e], NT) for e in range(2)]
            if diag:
                keep = (lax.broadcasted_iota(I32, (tk, t), 0) - lax.broadcasted_iota(I32, (tk, t), 1)) <= t * (i % 2)
                sts = [jnp.where(keep, st, NEG) for st in sts]
            pts, stats = [], []
            for e in range(2):
                m, l, _ = carry[e]
                m_new = jnp.maximum(m, _col_reduce(sts[e], jnp.maximum))
                alpha = jnp.exp(m - m_new)
                pt = jnp.exp(sts[e] - m_new)
                stats.append((m_new, alpha, alpha * l + _col_reduce(pt, jnp.add)))
                pts.append(pt.astype(BF16))
            pvs = [_dot(vtb, pts[e], NN) for e in range(2)]
            return tuple((stats[e][0], stats[e][2], stats[e][1] * carry[e][2] + pvs[e]) for e in range(2))

        init = (jnp.full((1, t), NEG, F32), jnp.zeros((1, t), F32), jnp.zeros((128, t), F32))
        carry = lax.fori_loop(0, i // 2, functools.partial(step, diag=False), (init, init))
        (m0, l0, acc0), (m1, l1, acc1) = step(i // 2, carry, True)
        o_pair = jnp.where(sub < HEAD_DIM, acc0 / l0, acc1 / l1)
        o_ref[...] = o_pair.T.astype(BF16)
        lse_ref[0] = jnp.where(row8 == 0, m0 + jnp.log(l0), jnp.where(row8 == 1, m1 + jnp.log(l1), 0.0))

    return _carry_call(
        body, carry, name=name, grid=(npair, nq),
        in_specs=[pl.BlockSpec((2, t, 128), lambda hp, i: (hp, i, 0)), pl.BlockSpec((2, s, 128), lambda hp, i: (hp, 0, 0)),
                  pl.BlockSpec((1, 128, s), lambda hp, i: (hp, 0, 0))],
        out_specs=[pl.BlockSpec((t, 128), lambda hp, i: (i, hp)), pl.BlockSpec((1, 8, t), lambda hp, i: (hp, 0, i))],
        out_shape=[jax.ShapeDtypeStruct((s, HEADS * HEAD_DIM), BF16), jax.ShapeDtypeStruct((npair, 8, s), F32)],
        scratch_shapes=[], args=[qa, ka, vt])


def _attn_delta(do, o, carry=None, *, name):
    s = do.shape[0]

    def body(do_ref, o_ref, d_ref):
        prod = do_ref[...].astype(F32) * o_ref[...].astype(F32)
        row = lax.broadcasted_iota(I32, (8, 128), 0)
        lane = lax.broadcasted_iota(I32, (8, 128), 1)
        sel = ((row == 0) & (lane < HEAD_DIM) | (row == 1) & (lane >= HEAD_DIM)).astype(BF16)
        p0, p1, p2 = _bf16_pieces(prod)
        d_ref[0] = (_dot(sel, p0.astype(BF16), NT) + _dot(sel, p1.astype(BF16), NT)) + _dot(sel, p2.astype(BF16), NT)

    pair = pl.BlockSpec((s, 128), lambda hp: (0, hp))
    (delta3,), carried = _carry_call(
        body, carry, name=name, grid=(HEADS // 2,), in_specs=[pair, pair],
        out_specs=[pl.BlockSpec((1, 8, s), lambda hp: (hp, 0, 0))],
        out_shape=[jax.ShapeDtypeStruct((HEADS // 2, 8, s), F32)], scratch_shapes=[], args=[do, o])
    return delta3, carried


def _attn_bwd(qa, ka, qkv, do, lse3, delta3, carry=None, *, name):
    s = qa.shape[1]
    t = ATT_T
    nb = s // t
    npair = HEADS // 2

    def body(qa_ref, ka_ref, v_ref, do_ref, lse_ref, delta_ref, dq_ref, dk_ref, dv_ref, aux_ref, dcq_ref, dqt):
        hp = pl.program_id(0)
        first = _head_lanes(t)
        lane = lax.broadcasted_iota(I32, (t, 128), 1)
        dqt[...] = jnp.zeros_like(dqt)

        @pl.when(hp == 0)
        def _():
            aux_ref[...] = jnp.zeros_like(aux_ref)

        krow = lax.broadcasted_iota(I32, (t, t), 0)
        qcol = lax.broadcasted_iota(I32, (t, t), 1)

        def key_block(j, _):
            c0 = pl.multiple_of(j * t, t)
            vb = v_ref[pl.ds(c0, t), :]
            kbs = (ka_ref[0, pl.ds(c0, t), :], ka_ref[1, pl.ds(c0, t), :])
            kbts = tuple(kb.astype(F32).T.astype(BF16) for kb in kbs)
            vhs = (jnp.where(first, vb, jnp.zeros_like(vb)), jnp.where(first, jnp.zeros_like(vb), vb))

            def query_block(i, carry, diag):
                r0 = pl.multiple_of(i * t, t)
                dob = do_ref[pl.ds(r0, t), :]
                sts = [_dot(kbs[e], qa_ref[e, pl.ds(r0, t), :], NT) for e in range(2)]
                dpts = [_dot(vhs[e], dob, NT) for e in range(2)]
                ptbs, dsbs = [], []
                for e in range(2):
                    st = jnp.where(krow <= qcol, sts[e], NEG) if diag else sts[e]
                    pt = jnp.exp(st - lse_ref[0, e:e + 1, pl.ds(r0, t)])
                    dsbs.append((pt * (dpts[e] - delta_ref[0, e:e + 1, pl.ds(r0, t)])).astype(BF16))
                    ptbs.append(pt.astype(BF16))
                out = []
                for e in range(2):
                    dk_a, dv_a = carry[e]
                    dv_a = dv_a + _dot(ptbs[e], dob, NN)
                    dk_a = dk_a + _dot(dsbs[e], qa_ref[e, pl.ds(r0, t), :], NN)
                    dqt[e, :, pl.ds(r0, t)] += _dot(kbts[e], dsbs[e], NN)
                    out.append((dk_a, dv_a))
                return tuple(out)

            zero = jnp.zeros((t, 128), F32)
            carry = query_block(j, ((zero, zero), (zero, zero)), True)
            (dk0, dv0), (dk1, dv1) = lax.fori_loop(j + 1, nb, functools.partial(query_block, diag=False), carry)
            dk_ref[pl.ds(c0, t), :] = jnp.where(first, dk0, dk1).astype(BF16)
            dv_ref[pl.ds(c0, t), :] = jnp.where(first, dv0, dv1).astype(BF16)
            sum_q = jnp.where(lane == 2 * hp, dk0[:, HEAD_DIM + 3:HEAD_DIM + 4],
                              jnp.where(lane == 2 * hp + 1, dk1[:, 3:4], aux_ref[pl.ds(c0, t), :]))
            aux_ref[pl.ds(c0, t), :] = sum_q
            return 0

        lax.fori_loop(0, nb, key_block, 0)
        sub = lax.broadcasted_iota(I32, (128, s), 0)
        row8 = lax.broadcasted_iota(I32, (8, s), 0)
        dq_ref[...] = (jnp.where(sub < HEAD_DIM, dqt[0], dqt[1]) * ATT_SCALE).T.astype(BF16)
        dcq_ref[0] = jnp.where(row8 == 0, dqt[0, HEAD_DIM:HEAD_DIM + 1, :], jnp.where(row8 == 1, dqt[1, 0:1, :], 0.0))

    def pair_cols(off):
        return pl.BlockSpec((s, 128), lambda hp: (0, off + hp))

    heads = pl.BlockSpec((2, s, 128), lambda hp: (hp, 0, 0))
    rows = pl.BlockSpec((1, 8, s), lambda hp: (hp, 0, 0))
    wide = jax.ShapeDtypeStruct((s, HEADS * HEAD_DIM), BF16)
    return _carry_call(
        body, carry, name=name, grid=(npair,),
        in_specs=[heads, heads, pair_cols(2 * npair), pair_cols(0), rows, rows],
        out_specs=[pair_cols(0), pair_cols(0), pair_cols(0), pl.BlockSpec((s, 128), lambda hp: (0, 0)), rows],
        out_shape=[wide, wide, wide, jax.ShapeDtypeStruct((s, 128), F32), jax.ShapeDtypeStruct((npair, 8, s), F32)],
        scratch_shapes=[pltpu.VMEM((2, 128, s), F32)], args=[qa, ka, qkv, do, lse3, delta3])


def _adam_math(w, g, m, v):
    m = ADAM_B1 * m + (1.0 - ADAM_B1) * g
    v = ADAM_B2 * v + (1.0 - ADAM_B2) * (g * g)
    m_hat = m / (1.0 - ADAM_B1 ** ADAM_STEP)
    v_hat = v / (1.0 - ADAM_B2 ** ADAM_STEP)
    delta = -ADAM_LR * (m_hat / (jnp.sqrt(v_hat) + ADAM_EPS) + ADAM_WD * w)
    return delta, m, v


def _sum_pairs(keep, recv, pos, *, name):
    _, r, c = recv.shape
    tr = _row_tile(r, 512)

    def body(pos_ref, a_ref, b_ref, o32_ref, o16_ref):
        tot = a_ref[...].astype(F32) + b_ref[...].astype(F32)
        o32_ref[...] = tot
        o16_ref[...] = tot.astype(BF16)

    out = pl.BlockSpec((1, tr, c), lambda q, i, pos: (q, i, 0))
    grid_spec = pltpu.PrefetchScalarGridSpec(
        num_scalar_prefetch=1, grid=(4, r // tr),
        in_specs=[pl.BlockSpec((1, tr, c), lambda q, i, pos: (2 * q + pos[2], i, 0)), out],
        out_specs=[out, out])
    return pl.pallas_call(
        body, name=name, grid_spec=grid_spec,
        out_shape=[jax.ShapeDtypeStruct((4, r, c), F32), jax.ShapeDtypeStruct((4, r, c), BF16)],
        compiler_params=_cparams(("parallel", "parallel")),
    )(pos, keep, recv)


def _adam_sharded(psum, recv, w, m, v, pos, *, name):
    r, c = w.shape
    tr = _row_tile(r, 320)

    def body(pos_ref, p_ref, r_ref, w_ref, m_ref, v_ref, g_ref, d_ref, mo_ref, vo_ref):
        g = p_ref[0] + r_ref[0].astype(F32) + r_ref[1].astype(F32) + r_ref[2].astype(F32)
        delta, mn, vn = _adam_math(w_ref[...], g, m_ref[...], v_ref[...])
        g_ref[...] = g
        d_ref[...] = delta
        mo_ref[...] = mn
        vo_ref[...] = vn

    row = pl.BlockSpec((tr, c), lambda i, pos: (i, 0))
    grid_spec = pltpu.PrefetchScalarGridSpec(
        num_scalar_prefetch=1, grid=(r // tr,),
        in_specs=[pl.BlockSpec((1, tr, c), lambda i, pos: (2 * pos[0] + pos[1], i, 0)),
                  pl.BlockSpec((3, tr, c), lambda i, pos: (0, i, 0)), row, row, row],
        out_specs=[row, row, row, row])
    o = jax.ShapeDtypeStruct((r, c), F32)
    return pl.pallas_call(
        body, name=name, grid_spec=grid_spec, out_shape=[o, o, o, o],
        compiler_params=_cparams(("parallel",)),
    )(pos, psum, recv, w, m, v)


def _adam_replicated(chip_sums, last, w, m, v, *, name):
    r = w.shape[0]

    def body(s_ref, l_ref, w_ref, m_ref, v_ref, g_ref, d_ref, mo_ref, vo_ref):
        g = (((s_ref[0] + s_ref[1]) + s_ref[2]) + s_ref[3]) + l_ref[...]
        delta, mn, vn = _adam_math(w_ref[...], g, m_ref[...], v_ref[...])
        g_ref[...] = g
        d_ref[...] = delta
        mo_ref[...] = mn
        vo_ref[...] = vn

    o = jax.ShapeDtypeStruct((r, 1024), F32)
    full = _full_spec((r, 1024))
    return pl.pallas_call(
        body, name=name, grid=(1,),
        in_specs=[_full_spec((4, r, 1024)), full, full, full, full], out_specs=[full] * 4, out_shape=[o] * 4,
        compiler_params=_cparams(("arbitrary",)),
    )(chip_sums, last, w, m, v)


def _pair_sum_small(mine, theirs, *, name):
    def body(a_ref, b_ref, o_ref):
        o_ref[...] = a_ref[...] + b_ref[...]

    full = _full_spec(mine.shape)
    return pl.pallas_call(
        body, name=name, grid=(1,), in_specs=[full, full], out_specs=full,
        out_shape=jax.ShapeDtypeStruct(mine.shape, F32), compiler_params=_cparams(("arbitrary",)),
    )(mine, theirs)


ANY = pl.BlockSpec(memory_space=pl.ANY)
OTHER_CHIPS = ((1, 0), (0, 1), (1, 1))


class _Carry:
    def __init__(self, inputs, out_shapes, scratch, start, wait, aliases=None):
        self.inputs, self.out_shapes, self.scratch = list(inputs), list(out_shapes), list(scratch)
        self.start, self.wait, self.aliases = start, wait, dict(aliases or {})


def _carried(body, carry, n_in, n_out, grid):
    if carry is None:
        return body
    ci, co, cs = len(carry.inputs), len(carry.out_shapes), len(carry.scratch)

    def wrapped(*refs):
        ins, cins = refs[:n_in], refs[n_in:n_in + ci]
        outs, couts = refs[n_in + ci:n_in + ci + n_out], refs[n_in + ci + n_out:n_in + ci + n_out + co]
        rest = refs[n_in + ci + n_out + co:]
        scratch, cscr = rest[:len(rest) - cs], rest[len(rest) - cs:]
        first, last = None, None
        for axis, size in enumerate(grid):
            f, l = pl.program_id(axis) == 0, pl.program_id(axis) == size - 1
            first = f if first is None else first & f
            last = l if last is None else last & l

        @pl.when(first)
        def _():
            carry.start(cins, couts, cscr)

        body(*ins, *outs, *scratch)

        @pl.when(last)
        def _():
            carry.wait(cins, couts, cscr)

    return wrapped


def _carry_call(body, carry, *, name, grid, in_specs, out_specs, out_shape, scratch_shapes, args, vmem=True,
                own_aliases=None):
    n_in, n_out = len(in_specs), len(out_specs)
    extra_in = [ANY] * len(carry.inputs) if carry else []
    extra_out = [ANY] * len(carry.out_shapes) if carry else []
    aliases = dict(own_aliases or {})
    if carry:
        aliases.update({n_in + i: n_out + o for i, o in carry.aliases.items()})
    out = pl.pallas_call(
        _carried(body, carry, n_in, n_out, grid), name=name, grid=grid,
        in_specs=list(in_specs) + extra_in, out_specs=list(out_specs) + extra_out,
        out_shape=list(out_shape) + (carry.out_shapes if carry else []),
        scratch_shapes=list(scratch_shapes) + (carry.scratch if carry else []),
        input_output_aliases=aliases,
        compiler_params=_cparams(("arbitrary",) * len(grid)) if vmem else None,
    )(*args, *(carry.inputs if carry else []))
    return list(out[:n_out]), list(out[n_out:])


def _run_carry(carry, *, name):
    return _carry_call(lambda: None, carry, name=name, grid=(1,), in_specs=[], out_specs=[], out_shape=[],
                       scratch_shapes=[], args=[], vmem=False)[1]


def _sems(n):
    return [pltpu.SemaphoreType.DMA((n,)), pltpu.SemaphoreType.DMA((n,))]


def _carry_gather1(shards):
    n = len(shards)

    def copies(x_refs, out_refs, scr, with_arrivals):
        send_sems, recv_sems, local_sems = scr
        x, y, c = lax.axis_index("x"), lax.axis_index("y"), lax.axis_index("c")
        peers = [(x, y, 1 - c)] + [(x ^ fx, y ^ fy, c) for fx, fy in OTHER_CHIPS]
        local, sends, arrivals = [], [], []
        for t, (x_ref, out_ref) in enumerate(zip(x_refs, out_refs)):
            local.append(pltpu.make_async_copy(x_ref, out_ref.at[4 * x + 2 * y + c], local_sems.at[t]))
            for k, (px, py, pc) in enumerate(peers):
                sems = dict(send_sem=send_sems.at[4 * t + k], recv_sem=recv_sems.at[4 * t + k],
                            device_id=(px, py, pc), device_id_type=MESH)
                sends.append(pltpu.make_async_remote_copy(src_ref=x_ref, dst_ref=out_ref.at[4 * x + 2 * y + c], **sems))
                if with_arrivals:
                    arrivals.append(
                        pltpu.make_async_remote_copy(src_ref=x_ref, dst_ref=out_ref.at[4 * px + 2 * py + pc], **sems))
        return local, sends, arrivals

    def start(x_refs, out_refs, scr):
        local, sends, _ = copies(x_refs, out_refs, scr, False)
        for cp in local + sends:
            cp.start()

    def wait(x_refs, out_refs, scr):
        local, sends, arrivals = copies(x_refs, out_refs, scr, True)
        for cp in arrivals:
            cp.wait_recv()
        for cp in sends:
            cp.wait_send()
        for cp in local:
            cp.wait()

    return _Carry(shards, [jax.ShapeDtypeStruct((N_DEV,) + a.shape, a.dtype) for a in shards],
                  _sems(4 * n) + [pltpu.SemaphoreType.DMA((n,))], start, wait)


def _carry_gather2(gathered):
    n = len(gathered)

    def copies(in_refs, g_refs, scr, with_arrivals):
        send_sems, recv_sems = scr
        x, y, c = lax.axis_index("x"), lax.axis_index("y"), lax.axis_index("c")
        sends, arrivals = [], []
        for t in range(n):
            for j, (fx, fy) in enumerate(OTHER_CHIPS):
                px, py = x ^ fx, y ^ fy
                sems = dict(send_sem=send_sems.at[3 * t + j], recv_sem=recv_sems.at[3 * t + j],
                            device_id=(x, y, 1 - c), device_id_type=MESH)
                mine, theirs = 4 * px + 2 * py + c, 4 * px + 2 * py + (1 - c)
                sends.append(pltpu.make_async_remote_copy(src_ref=in_refs[t].at[mine], dst_ref=g_refs[t].at[mine], **sems))
                if with_arrivals:
                    arrivals.append(pltpu.make_async_remote_copy(
                        src_ref=in_refs[t].at[mine], dst_ref=g_refs[t].at[theirs], **sems))
        return sends, arrivals

    def start(in_refs, g_refs, scr):
        for cp in copies(in_refs, g_refs, scr, False)[0]:
            cp.start()

    def wait(in_refs, g_refs, scr):
        sends, arrivals = copies(in_refs, g_refs, scr, True)
        for cp in arrivals:
            cp.wait_recv()
        for cp in sends:
            cp.wait_send()

    return _Carry(gathered, [jax.ShapeDtypeStruct(a.shape, a.dtype) for a in gathered], _sems(3 * n), start, wait,
                  aliases={t: t for t in range(n)})


def _allreduce_rows(x, *, name):
    def body(x_ref, o_ref, sib_ref, mine_ref, tab_ref, send_sems, recv_sems):
        x, y, c = lax.axis_index("x"), lax.axis_index("y"), lax.axis_index("c")
        swap = pltpu.make_async_remote_copy(src_ref=x_ref, dst_ref=sib_ref, send_sem=send_sems.at[0],
                                            recv_sem=recv_sems.at[0], device_id=(x, y, 1 - c), device_id_type=MESH)
        swap.start()
        swap.wait()
        mine_ref[...] = x_ref[...] + sib_ref[...]
        tab_ref[pl.ds(2 * x + y, 1)] = mine_ref[...][None]

        def copy(k, slot):
            fx, fy = OTHER_CHIPS[k]
            return pltpu.make_async_remote_copy(
                src_ref=mine_ref, dst_ref=tab_ref.at[slot], send_sem=send_sems.at[1 + k], recv_sem=recv_sems.at[1 + k],
                device_id=(x ^ fx, y ^ fy, c), device_id_type=MESH)

        for k in range(3):
            copy(k, 2 * x + y).start()
        for k, (fx, fy) in enumerate(OTHER_CHIPS):
            copy(k, 2 * (x ^ fx) + (y ^ fy)).wait()
        o_ref[...] = ((tab_ref[0] + tab_ref[1]) + tab_ref[2]) + tab_ref[3]

    vmem = pl.BlockSpec(memory_space=pltpu.VMEM)
    return pl.pallas_call(
        body, name=name, out_shape=jax.ShapeDtypeStruct(x.shape, F32), in_specs=[vmem], out_specs=vmem,
        scratch_shapes=[pltpu.VMEM(x.shape, F32), pltpu.VMEM(x.shape, F32), pltpu.VMEM((4,) + x.shape, F32)] + _sems(4),
    )(x)


def _allgather(shards, *, name):
    n = len(shards)

    def body(*refs):
        x_refs, out_refs = refs[:n], refs[n:2 * n]
        send_sems, recv_sems, local_sems = refs[2 * n:]
        x, y, c = lax.axis_index("x"), lax.axis_index("y"), lax.axis_index("c")
        me, sibling = (x, y, c), (x, y, 1 - c)
        chips = [(x ^ fx, y ^ fy) for fx, fy in OTHER_CHIPS]

        def copy(t, k, block, to, from_input=False):
            px, py, pc = block
            slab = out_refs[t].at[4 * px + 2 * py + pc]
            return pltpu.make_async_remote_copy(
                src_ref=x_refs[t] if from_input else slab, dst_ref=slab,
                send_sem=send_sems.at[7 * t + k], recv_sem=recv_sems.at[7 * t + k], device_id=to, device_id_type=MESH)

        mine = [pltpu.make_async_copy(x_refs[t], out_refs[t].at[4 * x + 2 * y + c], local_sems.at[t]) for t in range(n)]
        for cp in mine:
            cp.start()
        first = []
        for t in range(n):
            first.append(copy(t, 0, me, sibling, from_input=True))
            first += [copy(t, 1 + j, me, (*chip, c), from_input=True) for j, chip in enumerate(chips)]
        for cp in first:
            cp.start()
        passed = []
        for j, chip in enumerate(chips):
            for t in range(n):
                copy(t, 1 + j, (*chip, c), me).wait_recv()
                fwd = copy(t, 4 + j, (*chip, c), sibling)
                fwd.start()
                passed.append(fwd)
        for t in range(n):
            copy(t, 0, sibling, me).wait_recv()
            for j, chip in enumerate(chips):
                copy(t, 4 + j, (*chip, 1 - c), me).wait_recv()
        for cp in first + passed:
            cp.wait_send()
        for cp in mine:
            cp.wait()

    return pl.pallas_call(
        body, name=name, out_shape=[jax.ShapeDtypeStruct((N_DEV,) + a.shape, a.dtype) for a in shards],
        in_specs=[ANY] * n, out_specs=[ANY] * n,
        scratch_shapes=[pltpu.SemaphoreType.DMA((7 * n,)), pltpu.SemaphoreType.DMA((7 * n,)),
                        pltpu.SemaphoreType.DMA((n,))],
    )(*shards)


def _carry_sibling(slabs, small=None):
    n = len(slabs)
    extra = [] if small is None else [small]

    def copies(in_refs, out_refs, scr):
        send_sems, recv_sems = scr
        x, y, c = lax.axis_index("x"), lax.axis_index("y"), lax.axis_index("c")
        sibling = (x, y, 1 - c)
        out = []
        for t in range(n):
            for q in range(4):
                out.append(pltpu.make_async_remote_copy(
                    src_ref=in_refs[t].at[2 * q + (1 - c)], dst_ref=out_refs[t].at[q],
                    send_sem=send_sems.at[4 * t + q], recv_sem=recv_sems.at[4 * t + q],
                    device_id=sibling, device_id_type=MESH))
        if extra:
            out.append(pltpu.make_async_remote_copy(
                src_ref=in_refs[n], dst_ref=out_refs[n], send_sem=send_sems.at[4 * n], recv_sem=recv_sems.at[4 * n],
                device_id=sibling, device_id_type=MESH))
        return out

    def start(*refs):
        for cp in copies(*refs):
            cp.start()

    def wait(*refs):
        for cp in copies(*refs):
            cp.wait()

    return _Carry(list(slabs) + extra,
                  [jax.ShapeDtypeStruct((4,) + a.shape[1:], a.dtype) for a in slabs]
                  + [jax.ShapeDtypeStruct(a.shape, a.dtype) for a in extra], _sems(4 * n + 1), start, wait)


def _carry_chips(psums, small_sum=None):
    n = len(psums)
    table = small_sum is not None

    def copies(in_refs, out_refs, scr, arrivals):
        send_sems, recv_sems = scr[0], scr[1]
        x, y, c = lax.axis_index("x"), lax.axis_index("y"), lax.axis_index("c")
        out = []
        for k, (fx, fy) in enumerate(OTHER_CHIPS):
            px, py = x ^ fx, y ^ fy
            for t in range(n):
                out.append(pltpu.make_async_remote_copy(
                    src_ref=in_refs[t].at[2 * px + py], dst_ref=out_refs[t].at[k],
                    send_sem=send_sems.at[3 * t + k], recv_sem=recv_sems.at[3 * t + k],
                    device_id=(px, py, c), device_id_type=MESH))
            if table:
                slot = 2 * px + py if arrivals else 2 * x + y
                out.append(pltpu.make_async_remote_copy(
                    src_ref=in_refs[n], dst_ref=out_refs[n].at[slot], send_sem=send_sems.at[3 * n + k],
                    recv_sem=recv_sems.at[3 * n + k], device_id=(px, py, c), device_id_type=MESH))
        return out

    def own(in_refs, out_refs, scr):
        x, y = lax.axis_index("x"), lax.axis_index("y")
        return pltpu.make_async_copy(in_refs[n], out_refs[n].at[2 * x + y], scr[2])

    def start(in_refs, out_refs, scr):
        if table:
            own(in_refs, out_refs, scr).start()
        for cp in copies(in_refs, out_refs, scr, False):
            cp.start()

    def wait(in_refs, out_refs, scr):
        for cp in copies(in_refs, out_refs, scr, True):
            cp.wait()
        if table:
            own(in_refs, out_refs, scr).wait()

    out_shapes = [jax.ShapeDtypeStruct((3,) + a.shape[1:], a.dtype) for a in psums]
    if table:
        out_shapes.append(jax.ShapeDtypeStruct((4,) + small_sum.shape, F32))
    return _Carry(list(psums) + ([small_sum] if table else []), out_shapes,
                  _sems(3 * n + 3) + ([pltpu.SemaphoreType.DMA] if table else []), start, wait)


def _to_comm(name, kind, block, dtype=BF16):
    a = block[0]
    if kind == "cols":
        a = a.T
        if name == "w_in":
            a = jnp.pad(a, ((0, IN_SHARD_PAD - IN_SHARD), (0, 0)))
    return a if kind == "f32" else a.astype(dtype)


def _from_comm(name, kind, a):
    if kind == "cols":
        if name == "w_in":
            a = a[:IN_SHARD]
        a = a.T
    return a[None]


def _assemble_weights(g):
    out = {}
    if "w_in" in g:
        wt_in = g["w_in"][:, :IN_SHARD].reshape(IN_COLS, D_MODEL)
        out["wt_main"] = jnp.concatenate([wt_in[:2048], wt_in[O_G:], wt_in[2048:O_F]], axis=0)
        out["wt_f"] = jnp.pad(wt_in[O_F:O_G], ((0, 128 - HEADS), (0, 0)))
    square = dict(w_branch_a="w_a", w_branch_b="w_b", w_out="w_out", w_ple_gate="w_pg")
    for long, short in square.items():
        if long in g:
            out[short] = g[long].reshape(D_MODEL, D_MODEL)
    if "w_up" in g:
        out["wt_up"] = g["w_up"].reshape(2 * D_FF, D_MODEL)
    if "conv_w" in g:
        out["conv_w"] = g["conv_w"].transpose(1, 0, 2).reshape(3, 2 * D_FF)
    if "w_down" in g:
        out["w_down"] = g["w_down"].reshape(D_FF, D_MODEL)
    if "w_ple" in g:
        out["wt_ple"] = g["w_ple"].reshape(D_MODEL, PLE_DIM)
    return out


def _grad_slabs(gr):
    out = {}
    if "wt_main" in gr:
        gm, gf = gr["wt_main"], gr["wt_f"]
        segments = ((0, 2048, gm, 0), (2048, O_F, gm, 2048), (O_F, O_G, gf, -O_F), (O_G, IN_COLS, gm, 2048 - O_G))
        slabs = []
        for j in range(N_DEV):
            lo, hi = j * IN_SHARD, (j + 1) * IN_SHARD
            pieces = [src[max(lo, a) + shift:min(hi, b) + shift] for a, b, src, shift in segments if max(lo, a) < min(hi, b)]
            pieces.append(jnp.zeros((IN_SHARD_PAD - IN_SHARD, D_MODEL), gm.dtype))
            slabs.append(jnp.concatenate(pieces, axis=0))
        out["w_in"] = jnp.stack(slabs)
    rows = dict(w_a="w_branch_a", w_b="w_branch_b", w_out="w_out", wt_up="w_up", w_down="w_down", w_pg="w_ple_gate")
    for short, long in rows.items():
        if short in gr:
            out[long] = gr[short].reshape(N_DEV, -1, D_MODEL)
    if "conv_w" in gr:
        out["conv_w"] = gr["conv_w"].reshape(3, N_DEV, -1).transpose(1, 0, 2)
    if "wt_ple" in gr:
        out["w_ple"] = gr["wt_ple"].reshape(N_DEV, -1, PLE_DIM)
    return {k: v.astype(BF16) for k, v in out.items()}


def _rows(a, rows):
    flat = a.reshape(-1)
    return jnp.pad(flat, (0, rows * 1024 - flat.shape[0])).reshape(rows, 1024)


def _pack_small(parts):
    return jnp.concatenate([_rows(parts[n].astype(F32), r) for n, r in SMALL], axis=0)


def _small(packed, name, shape):
    off, r = SMALL_OFF[name]
    n = math.prod(shape)
    return packed[off:off + r].reshape(-1)[:n].reshape(shape)


class _Exchanges:
    def __init__(self, later, shards, pos):
        self.later, self.shards, self.pos, self.reduced = later, shards, pos, {}

    def gather1(self):
        return _carry_gather1(self.shards)

    def gather2(self, level1):
        return _carry_gather2(level1)

    def weights(self, full):
        return _assemble_weights(dict(zip(self.later, full)))

    def sibling(self, grads):
        self.slabs = _grad_slabs(grads)
        return _carry_sibling([self.slabs[n] for n in self.later])

    def sibling_small(self, small_g):
        self.small_g = small_g
        return _carry_sibling([], small_g)

    def chips(self, from_sib, small_sib):
        sums = [_sum_pairs(self.slabs[n], r, self.pos, name="sum_sibling_" + n) for n, r in zip(self.later, from_sib)]
        self.sums32 = [s32 for s32, _ in sums]
        return _carry_chips([s16 for _, s16 in sums], _pair_sum_small(self.small_g, small_sib, name="sum_sibling_small"))

    def chips_done(self, carried):
        *from_chips, self.table = carried
        self.reduced.update({n: (s32, r) for n, s32, r in zip(self.later, self.sums32, from_chips)})

    def w_in_chips(self, grads_in):
        slab = _grad_slabs(grads_in)["w_in"]
        (from_sib,) = _run_carry(_carry_sibling([slab]), name="exchange_sibling_w_in")
        self.s32_in, s16 = _sum_pairs(slab, from_sib, self.pos, name="sum_sibling_w_in")
        return _carry_chips([s16])

    def w_in_chips_done(self, carried):
        self.reduced["w_in"] = (self.s32_in, carried[0])


def _local_step(x, p, target, w, sm, ex=None):
    s = x.shape[0]
    mm = _matmul
    wt_main = w["wt_main"]
    conv_b = sm["conv_b"]
    bs_t = jnp.pad(sm["gmlp_b_s"].T, ((0, 0), (0, 128 - GROUPS)))
    b_f = jnp.pad(sm["b_f"], ((0, 0), (0, 128 - HEADS)))
    big = dict(tm=1024, tn=1024, tk=1024)
    whole_s = dict(tn=1024, tk=s)

    h = _rmsnorm_fwd(x, sm["norm_mix_g"], name="norm_mix")
    qkv = mm(h, wt_main, mode="nt", out_dtype=BF16, name="in_qkv", n=3072, b_off=4, **big)
    f_logit = mm(h, w["wt_f"], mode="nt", out_dtype=F32, name="in_f", tm=1024, tk=1024)
    cqe = _forget_cumsum(f_logit, b_f, name="forget_cumsum")
    qa, ka, vt = _attn_prep(qkv, cqe, name="attn_prep")
    uvg = dict(mode="nt", out_dtype=F32, name="in_uvg", n=4096, **big)
    if ex is None:
        (b, lse3), _ = _attn_fwd(qa, ka, vt, name="attn_fwd")
        zuvg = mm(h, wt_main, **uvg)
    else:
        (b, lse3), level1 = _attn_fwd(qa, ka, vt, ex.gather1(), name="attn_fwd")
        zuvg, full = mm(h, wt_main, carry=ex.gather2(level1), **uvg)
        w = {**w, **ex.weights(full)}
    a = _gmlp_fwd(zuvg, sm["gmlp_ln_g"], sm["gmlp_ln_b"], sm["gmlp_w_s"], bs_t, name="gmlp_fwd")
    wt_up, conv_w = w["wt_up"], w["conv_w"]
    ya = mm(a, w["w_a"], mode="nn", out_dtype=F32, name="branch_a", **big)
    yb = mm(b, w["w_b"], mode="nn", out_dtype=F32, name="branch_b", **big)
    merged = _merge_fwd(ya, yb, zuvg, name="merge_fwd")
    x1 = mm(merged, w["w_out"], mode="nn", out_dtype=F32, name="out_proj", add=x, **big)
    h2 = _rmsnorm_fwd(x1, sm["norm_ffn_g"], name="norm_ffn")
    up = mm(h2, wt_up, mode="nt", out_dtype=F32, name="up", tm=1024, tn=512, tk=1024)
    act = _convglu_fwd(up, conv_w, conv_b, name="convglu_fwd")
    x2 = mm(act, w["w_down"], mode="nn", out_dtype=F32, name="down", tm=1024, tn=1024, tk=1408, add=x1)
    h3 = _rmsnorm_fwd(x2, sm["norm_ple_g"], name="norm_ple")
    ple = mm(p, w["wt_ple"], mode="nt", out_dtype=F32, name="ple", tm=1024, tn=1024, tk=256)
    gp = mm(h3, w["w_pg"], mode="nn", out_dtype=F32, name="ple_gate", **big)
    x3 = _ple_fwd(x2, ple, gp, name="ple_fwd")

    loss, dx3, d_norm_final = _final_loss_bwd(x3, target, sm["norm_final_g"], name="loss_bwd")
    dple, dgp = _ple_bwd(dx3, ple, gp, name="ple_bwd")
    g_wt_ple = mm(dple, p, mode="tn", out_dtype=BF16, name="d_w_ple", tm=512, tn=256, tk=s)
    g_w_pg = mm(h3, dgp, mode="tn", out_dtype=BF16, name="d_w_pg", tm=256, **whole_s)
    dh3 = mm(dgp, w["w_pg"], mode="nt", out_dtype=F32, name="d_h3", **big)
    dx2, dx2b, d_norm_ple = _rmsnorm_bwd(dx3, dh3, x2, sm["norm_ple_g"], name="norm_ple_bwd")
    g_w_down = mm(act, dx2b, mode="tn", out_dtype=BF16, name="d_w_down", tm=256, **whole_s)
    dact = mm(dx2b, w["w_down"], mode="nt", out_dtype=BF16, name="d_act", tm=1024, tn=1408, tk=1024)
    dup_a, dup_g, dcw_a, dcw_g, dcb_a, dcb_g = _convglu_bwd(dact, up, conv_w, conv_b, name="convglu_bwd")
    g_wt_up = mm(dup_a, h2, mode="tn", out_dtype=BF16, name="d_w_up_a", tm=256, out_rows=2 * D_FF, **whole_s)
    g_wt_up = mm(dup_g, h2, mode="tn", out_dtype=BF16, name="d_w_up_g", tm=256, out_rows=2 * D_FF,
                 o_off=D_FF // 256, into=g_wt_up, **whole_s)
    dh2 = mm(dup_a, wt_up, mode="nn", out_dtype=F32, name="d_h2_a", tm=1024, tn=1024, tk=1408)
    dh2 = mm(dup_g, wt_up, mode="nn", out_dtype=F32, name="d_h2_g", tm=1024, tn=1024, tk=1408, b_off=2, add=dh2)
    dx1, dx1b, d_norm_ffn = _rmsnorm_bwd(dx2, dh2, x1, sm["norm_ffn_g"], name="norm_ffn_bwd")
    g_w_out = mm(merged, dx1b, mode="tn", out_dtype=BF16, name="d_w_out", tm=256, **whole_s)
    dmerged = mm(dx1b, w["w_out"], mode="nt", out_dtype=F32, name="d_merged", **big)
    dya, dyb, dga, dgb = _merge_bwd(dmerged, ya, yb, zuvg, name="merge_bwd")
    g_w_a = mm(a, dya, mode="tn", out_dtype=BF16, name="d_w_a", tm=256, **whole_s)
    g_w_b = mm(b, dyb, mode="tn", out_dtype=BF16, name="d_w_b", tm=256, **whole_s)
    da = mm(dya, w["w_a"], mode="nt", out_dtype=BF16, name="d_a", **big)
    db = mm(dyb, w["w_b"], mode="nt", out_dtype=BF16, name="d_b", **big)
    grads = dict(w_a=g_w_a, w_b=g_w_b, w_out=g_w_out, wt_up=g_wt_up, conv_w=jnp.concatenate([dcw_a, dcw_g], axis=1),
                 w_down=g_w_down, wt_ple=g_wt_ple, w_pg=g_w_pg)
    gmlp_args = (da, zuvg, sm["gmlp_ln_g"], sm["gmlp_ln_b"], sm["gmlp_w_s"], bs_t)
    if ex is None:
        (dzu, dzv, d_w_s, d_bs_t, d_ln_g, d_ln_b), _ = _gmlp_bwd(*gmlp_args, name="gmlp_bwd")
    else:
        (dzu, dzv, d_w_s, d_bs_t, d_ln_g, d_ln_b), from_sib = _gmlp_bwd(*gmlp_args, ex.sibling(grads), name="gmlp_bwd")
    small = dict(norm_mix_g=jnp.zeros((1, D_MODEL), F32), b_f=jnp.zeros((1, HEADS), F32), gmlp_ln_g=d_ln_g,
                 gmlp_ln_b=d_ln_b, gmlp_w_s=d_w_s, gmlp_b_s=d_bs_t[:, :GROUPS].T, norm_ffn_g=d_norm_ffn,
                 conv_b=jnp.concatenate([dcb_a, dcb_g], axis=1), norm_ple_g=d_norm_ple, norm_final_g=d_norm_final)
    if ex is None:
        delta3, _ = _attn_delta(db, b, name="attn_delta")
        (dq, dk, dv, aux, dcq3), _ = _attn_bwd(qa, ka, qkv, db, lse3, delta3, name="attn_bwd")
    else:
        delta3, (small_sib,) = _attn_delta(db, b, ex.sibling_small(_pack_small(small)), name="attn_delta")
        (dq, dk, dv, aux, dcq3), carried = _attn_bwd(qa, ka, qkv, db, lse3, delta3, ex.chips(from_sib, small_sib),
                                                     name="attn_bwd")
        ex.chips_done(carried)
    dcq16 = jnp.pad(dcq3[:, :2, :].reshape(HEADS, s).T, ((0, 0), (0, 128 - HEADS)))
    dzf, d_b_f = _forget_bwd(dcq16, aux, f_logit, b_f, name="forget_bwd")
    dz = jnp.concatenate([dzu, dzv, dga, dgb, dq, dk, dv], axis=1)
    g_wt_main = mm(dz, h, mode="tn", out_dtype=BF16, name="d_w_main", tm=512, **whole_s)
    g_wt_f = mm(dzf, h, mode="tn", out_dtype=BF16, name="d_w_f", **whole_s)
    grads = dict(grads, wt_main=g_wt_main, wt_f=g_wt_f)
    dh_main = dict(mode="nn", out_dtype=F32, name="d_h_main", **big)
    if ex is None:
        dh = mm(dz, wt_main, **dh_main)
    else:
        dh, carried = mm(dz, wt_main, carry=ex.w_in_chips(dict(wt_main=g_wt_main, wt_f=g_wt_f)), **dh_main)
        ex.w_in_chips_done(carried)
    dh = mm(dzf, w["wt_f"], mode="nn", out_dtype=F32, name="d_h_f", tm=1024, tn=1024, add=dh)
    dx0, _, d_norm_mix = _rmsnorm_bwd(dx1, dh, x, sm["norm_mix_g"], name="norm_mix_bwd")
    return loss, dx0, grads, dict(small, norm_mix_g=d_norm_mix, b_f=d_b_f[:, :HEADS])


def kernel(x, p, norm_mix_g, w_in, b_f, gmlp_ln_g, gmlp_ln_b, gmlp_w_s, gmlp_b_s, w_branch_a, w_branch_b, w_out, norm_ffn_g, w_up, conv_w, conv_b, w_down, norm_ple_g, w_ple, w_ple_gate, norm_final_g, loss_target, m_norm_mix_g, m_w_in, m_b_f, m_gmlp_ln_g, m_gmlp_ln_b, m_gmlp_w_s, m_gmlp_b_s, m_w_branch_a, m_w_branch_b, m_w_out, m_norm_ffn_g, m_w_up, m_conv_w, m_conv_b, m_w_down, m_norm_ple_g, m_w_ple, m_w_ple_gate, m_norm_final_g, v_norm_mix_g, v_w_in, v_b_f, v_gmlp_ln_g, v_gmlp_ln_b, v_gmlp_w_s, v_gmlp_b_s, v_w_branch_a, v_w_branch_b, v_w_out, v_norm_ffn_g, v_w_up, v_conv_w, v_conv_b, v_w_down, v_norm_ple_g, v_w_ple, v_w_ple_gate, v_norm_final_g):
    given = dict(locals())
    weights = {n: given[n] for n in WEIGHT_ORDER}
    mom_m = {n: given["m_" + n] for n in WEIGHT_ORDER}
    mom_v = {n: given["v_" + n] for n in WEIGHT_ORDER}
    pos = jnp.stack([lax.axis_index("x"), lax.axis_index("y"), lax.axis_index("c")]).astype(I32)
    names = [n for n, _ in SHARDED]
    kinds = dict(SHARDED)

    later = [n for n in names if n != "w_in"]

    first = _allgather([_to_comm("w_in", kinds["w_in"], weights["w_in"])], name="allgather_w_in")
    ex = _Exchanges(later, [_to_comm(n, kinds[n], weights[n]) for n in later], pos)

    sm = dict(norm_mix_g=norm_mix_g, b_f=b_f, gmlp_ln_g=gmlp_ln_g, gmlp_ln_b=gmlp_ln_b, gmlp_w_s=gmlp_w_s[0],
              gmlp_b_s=gmlp_b_s[0], norm_ffn_g=norm_ffn_g, conv_b=conv_b, norm_ple_g=norm_ple_g,
              norm_final_g=norm_final_g.reshape(1, D_MODEL))
    loss_part, dx0, grads, small = _local_step(
        x[0], p[0, 0], loss_target[0], _assemble_weights({"w_in": first[0]}), sm, ex)

    last = jnp.concatenate([_rows(small["norm_mix_g"], 8), _rows(small["b_f"], 8)], axis=0)
    small_last = jnp.pad(_allreduce_rows(last, name="allreduce_last"), ((0, SMALL_ROWS - 16), (0, 0)))

    grad, delta, new_m, new_v = {}, {}, {}, {}
    for n in names:
        s32, r = ex.reduced[n]
        outs = _adam_sharded(s32, r, *[_to_comm(n, kinds[n], src[n], F32) for src in (weights, mom_m, mom_v)], pos,
                             name="adam_" + n)
        grad[n], delta[n], new_m[n], new_v[n] = [_from_comm(n, kinds[n], o) for o in outs]
    replicated = [n for n, _ in SMALL]
    rep = lambda src: _pack_small({n: src[n] for n in replicated})
    packed = _adam_replicated(ex.table, small_last, rep(weights), rep(mom_m), rep(mom_v), name="adam_replicated")
    for out, pk in zip((grad, delta, new_m, new_v), packed):
        for n in replicated:
            out[n] = _small(pk, n, weights[n].shape)

    loss = lax.psum(loss_part[0, 0], ("x", "y", "c"))
    return (loss, dx0[None], *[grad[n] for n in WEIGHT_ORDER], *[delta[n] for n in WEIGHT_ORDER],
            *[new_m[n] for n in WEIGHT_ORDER], *[new_v[n] for n in WEIGHT_ORDER])
```

```python
import functools
import math

import jax
import jax.numpy as jnp
from jax import lax
from jax.experimental import pallas as pl
from jax.experimental.pallas import tpu as pltpu

F32 = jnp.float32
BF16 = jnp.bfloat16
I32 = jnp.int32

D_MODEL = 1024
GROUPS = 8
GDIM = 128
GBLOCK = 128
CHUNK = 64
HEADS = 16
HEAD_DIM = 64
D_FF = 2816
PLE_DIM = 256
EPS = 1e-6
N_DEV = 8
ATT_SCALE = HEAD_DIM ** -0.5
NEG = -1e30

ADAM_LR = 0.001
ADAM_B1 = 0.9
ADAM_B2 = 0.999
ADAM_EPS = 1e-08
ADAM_WD = 0.01
ADAM_STEP = 10

V7X_VMEM_LIMIT = 48 * 1024 * 1024
MESH = pl.DeviceIdType.MESH

O_F = 2 * 1024 + 3 * 1024
O_G = O_F + HEADS
IN_COLS = O_G + 2 * D_MODEL
MAIN_COLS = IN_COLS - HEADS
IN_SHARD = IN_COLS // N_DEV
IN_SHARD_PAD = 912

SHARDED = (("w_in", "cols"), ("w_branch_a", "rows"), ("w_branch_b", "rows"), ("w_out", "rows"), ("w_up", "cols"),
           ("conv_w", "f32"), ("w_down", "rows"), ("w_ple", "cols"), ("w_ple_gate", "rows"))

SMALL = (("norm_mix_g", 8), ("b_f", 8), ("gmlp_ln_g", 8), ("gmlp_ln_b", 8), ("gmlp_w_s", 128), ("gmlp_b_s", 8),
         ("norm_ffn_g", 8), ("conv_b", 8), ("norm_ple_g", 8), ("norm_final_g", 8))
SMALL_OFF = {}
_o = 0
for _n, _r in SMALL:
    SMALL_OFF[_n] = (_o, _r)
    _o += _r
SMALL_ROWS = _o

WEIGHT_ORDER = ("norm_mix_g", "w_in", "b_f", "gmlp_ln_g", "gmlp_ln_b", "gmlp_w_s", "gmlp_b_s", "w_branch_a",
                "w_branch_b", "w_out", "norm_ffn_g", "w_up", "conv_w", "conv_b", "w_down", "norm_ple_g", "w_ple",
                "w_ple_gate", "norm_final_g")


def _cparams(sem):
    return pltpu.CompilerParams(dimension_semantics=sem, vmem_limit_bytes=V7X_VMEM_LIMIT)


def _gelu(x):
    c = math.sqrt(2.0 / math.pi)
    return 0.5 * x * (1.0 + jnp.tanh(c * (x + 0.044715 * x * x * x)))


def _gelu_and_grad(x):
    c = math.sqrt(2.0 / math.pi)
    t = jnp.tanh(c * (x + 0.044715 * x * x * x))
    g = 0.5 * x * (1.0 + t)
    dg = 0.5 * (1.0 + t) + 0.5 * x * (1.0 - t * t) * (c * (1.0 + 3.0 * 0.044715 * x * x))
    return g, dg


def _sigmoid(x):
    return 1.0 / (1.0 + jnp.exp(-x))


def _dot(a, b, dims):
    return lax.dot_general(a, b, (dims, ((), ())), preferred_element_type=F32)


NN = ((1,), (0,))
NT = ((1,), (1,))
TN = ((0,), (0,))


def _row_tile(rows, most):
    best = None
    for t in range(16, min(rows, most) + 1, 16):
        if rows % t == 0:
            best = t
    return best if best is not None else rows


def _matmul(a, b, *, mode, out_dtype, name, tm=512, tn=512, tk=512, add=None, n=None, b_off=0,
            out_rows=None, o_off=0, into=None, carry=None):
    if mode == "tn":
        kdim, m = a.shape
    else:
        m, kdim = a.shape
    if n is None:
        n = b.shape[0] if mode == "nt" else b.shape[1]
    tm, tn, tk = min(tm, m), min(tn, n), min(tk, kdim)
    assert m % tm == 0 and n % tn == 0 and kdim % tk == 0, (name, m, n, kdim, tm, tn, tk)
    nk = kdim // tk
    dims = {"nn": NN, "nt": NT, "tn": TN}[mode]

    def finish(r, add_ref, o_ref):
        if add_ref is not None:
            r = add_ref[...].astype(F32) + r
        o_ref[...] = r.astype(out_dtype)

    def body(*refs):
        refs = list(refs)
        a_ref, b_ref = refs[:2]
        add_ref = refs[2] if add is not None else None
        o_ref = refs[2 + (add is not None) + (into is not None)]
        part = _dot(a_ref[...].astype(BF16), b_ref[...].astype(BF16), dims)
        if nk == 1:
            finish(part, add_ref, o_ref)
            return
        acc_ref = refs[-1]
        k = pl.program_id(2)

        @pl.when(k == 0)
        def _():
            acc_ref[...] = part

        @pl.when((k > 0) & (k < nk - 1))
        def _():
            acc_ref[...] += part

        @pl.when(k == nk - 1)
        def _():
            finish(acc_ref[...] + part, add_ref, o_ref)

    a_spec = pl.BlockSpec((tk, tm), lambda i, j, k: (k, i)) if mode == "tn" else pl.BlockSpec((tm, tk), lambda i, j, k: (i, k))
    if mode == "nt":
        b_spec = pl.BlockSpec((tn, tk), lambda i, j, k: (j + b_off, k))
    else:
        b_spec = pl.BlockSpec((tk, tn), lambda i, j, k: (k + b_off, j))
    o_spec = pl.BlockSpec((tm, tn), lambda i, j, k: (i + o_off, j))
    in_specs = [a_spec, b_spec] + ([pl.BlockSpec((tm, tn), lambda i, j, k: (i, j))] if add is not None else [])
    args = (a, b) + ((add,) if add is not None else ())
    aliases = {}
    if into is not None:
        aliases = {len(args): 0}
        in_specs.append(pl.BlockSpec(memory_space=pl.ANY))
        args += (into,)
    (out,), carried = _carry_call(
        body, carry, name=name, grid=(m // tm, n // tn, nk), in_specs=in_specs, out_specs=[o_spec],
        out_shape=[jax.ShapeDtypeStruct((m if out_rows is None else out_rows, n), out_dtype)],
        scratch_shapes=[pltpu.VMEM((tm, tn), F32)] if nk > 1 else [], args=args, own_aliases=aliases)
    return out if carry is None else (out, carried)


def _row_spec(tr, width, col_block=0):
    return pl.BlockSpec((tr, width), lambda i: (i, col_block))


def _full_spec(shape):
    return pl.BlockSpec(shape, lambda i: tuple(0 for _ in shape))


def _rmsnorm_fwd(x, g, *, name, tr=256):
    s, d = x.shape

    def body(x_ref, g_ref, o_ref):
        xv = x_ref[...]
        r = lax.rsqrt(jnp.mean(xv * xv, axis=-1, keepdims=True) + EPS)
        o_ref[...] = ((xv * r) * g_ref[...]).astype(BF16)

    return pl.pallas_call(
        body, name=name, grid=(s // tr,),
        in_specs=[_row_spec(tr, d), _full_spec((1, d))], out_specs=_row_spec(tr, d),
        out_shape=jax.ShapeDtypeStruct((s, d), BF16), compiler_params=_cparams(("parallel",)),
    )(x, g)


def _rmsnorm_bwd(dres, dh, x, g, *, name, tr=256):
    s, d = x.shape

    def body(dres_ref, dh_ref, x_ref, g_ref, dx_ref, dxb_ref, dg_ref):
        i = pl.program_id(0)
        xv = x_ref[...]
        r = lax.rsqrt(jnp.mean(xv * xv, axis=-1, keepdims=True) + EPS)
        xhat = xv * r
        dhv = dh_ref[...].astype(F32)
        dxhat = dhv * g_ref[...]
        dx = dres_ref[...] + r * (dxhat - xhat * jnp.mean(dxhat * xhat, axis=-1, keepdims=True))
        dx_ref[...] = dx
        dxb_ref[...] = dx.astype(BF16)
        dgp = jnp.sum(dhv * xhat, axis=0, keepdims=True)

        @pl.when(i == 0)
        def _():
            dg_ref[...] = dgp

        @pl.when(i > 0)
        def _():
            dg_ref[...] += dgp

    return pl.pallas_call(
        body, name=name, grid=(s // tr,),
        in_specs=[_row_spec(tr, d), _row_spec(tr, d), _row_spec(tr, d), _full_spec((1, d))],
        out_specs=[_row_spec(tr, d), _row_spec(tr, d), _full_spec((1, d))],
        out_shape=[jax.ShapeDtypeStruct((s, d), F32), jax.ShapeDtypeStruct((s, d), BF16),
                   jax.ShapeDtypeStruct((1, d), F32)],
        compiler_params=_cparams(("arbitrary",)),
    )(dres, dh, x, g)


def _final_loss_bwd(x3, target, g, *, name, tr=256):
    s, d = x3.shape

    def body(x_ref, t_ref, g_ref, loss_ref, dx_ref, dg_ref):
        i = pl.program_id(0)
        xv = x_ref[...]
        r = lax.rsqrt(jnp.mean(xv * xv, axis=-1, keepdims=True) + EPS)
        xhat = xv * r
        diff = xhat * g_ref[...] - t_ref[...]
        lp = jnp.zeros((1, 128), F32) + (0.5 / d) * jnp.sum(diff * diff)
        dy = diff * (1.0 / d)
        dxhat = dy * g_ref[...]
        dx_ref[...] = r * (dxhat - xhat * jnp.mean(dxhat * xhat, axis=-1, keepdims=True))
        dgp = jnp.sum(dy * xhat, axis=0, keepdims=True)

        @pl.when(i == 0)
        def _():
            dg_ref[...] = dgp
            loss_ref[...] = lp

        @pl.when(i > 0)
        def _():
            dg_ref[...] += dgp
            loss_ref[...] += lp

    return pl.pallas_call(
        body, name=name, grid=(s // tr,),
        in_specs=[_row_spec(tr, d), _row_spec(tr, d), _full_spec((1, d))],
        out_specs=[_full_spec((1, 128)), _row_spec(tr, d), _full_spec((1, d))],
        out_shape=[jax.ShapeDtypeStruct((1, 128), F32), jax.ShapeDtypeStruct((s, d), F32),
                   jax.ShapeDtypeStruct((1, d), F32)],
        compiler_params=_cparams(("arbitrary",)),
    )(x3, target, g)


def _merge_fwd(ya, yb, zuvg, *, name, tr=256):
    s, d = ya.shape

    def body(ya_ref, yb_ref, ga_ref, gb_ref, o_ref):
        o_ref[...] = (_sigmoid(ga_ref[...]) * ya_ref[...] + _sigmoid(gb_ref[...]) * yb_ref[...]).astype(BF16)

    return pl.pallas_call(
        body, name=name, grid=(s // tr,),
        in_specs=[_row_spec(tr, d), _row_spec(tr, d), _row_spec(tr, d, 2), _row_spec(tr, d, 3)],
        out_specs=_row_spec(tr, d),
        out_shape=jax.ShapeDtypeStruct((s, d), BF16), compiler_params=_cparams(("parallel",)),
    )(ya, yb, zuvg, zuvg)


def _merge_bwd(dm, ya, yb, zuvg, *, name, tr=256):
    s, d = ya.shape

    def body(dm_ref, ya_ref, yb_ref, ga_ref, gb_ref, dya_ref, dyb_ref, dga_ref, dgb_ref):
        dmv = dm_ref[...]
        sa = _sigmoid(ga_ref[...])
        sb = _sigmoid(gb_ref[...])
        dya_ref[...] = (dmv * sa).astype(BF16)
        dyb_ref[...] = (dmv * sb).astype(BF16)
        dga_ref[...] = (dmv * ya_ref[...] * (sa * (1.0 - sa))).astype(BF16)
        dgb_ref[...] = (dmv * yb_ref[...] * (sb * (1.0 - sb))).astype(BF16)

    o = jax.ShapeDtypeStruct((s, d), BF16)
    return pl.pallas_call(
        body, name=name, grid=(s // tr,),
        in_specs=[_row_spec(tr, d)] * 3 + [_row_spec(tr, d, 2), _row_spec(tr, d, 3)], out_specs=[_row_spec(tr, d)] * 4,
        out_shape=[o, o, o, o], compiler_params=_cparams(("parallel",)),
    )(dm, ya, yb, zuvg, zuvg)


def _ple_fwd(x2, ple, gp, *, name, tr=256):
    s, d = x2.shape

    def body(x_ref, ple_ref, gp_ref, o_ref):
        o_ref[...] = x_ref[...] + ple_ref[...] * _sigmoid(gp_ref[...])

    return pl.pallas_call(
        body, name=name, grid=(s // tr,),
        in_specs=[_row_spec(tr, d)] * 3, out_specs=_row_spec(tr, d),
        out_shape=jax.ShapeDtypeStruct((s, d), F32), compiler_params=_cparams(("parallel",)),
    )(x2, ple, gp)


def _ple_bwd(dx3, ple, gp, *, name, tr=256):
    s, d = dx3.shape

    def body(dx_ref, ple_ref, gp_ref, dple_ref, dgp_ref):
        sg = _sigmoid(gp_ref[...])
        dxv = dx_ref[...]
        dple_ref[...] = (dxv * sg).astype(BF16)
        dgp_ref[...] = (dxv * ple_ref[...] * (sg * (1.0 - sg))).astype(BF16)

    o = jax.ShapeDtypeStruct((s, d), BF16)
    return pl.pallas_call(
        body, name=name, grid=(s // tr,),
        in_specs=[_row_spec(tr, d)] * 3, out_specs=[_row_spec(tr, d)] * 2,
        out_shape=[o, o], compiler_params=_cparams(("parallel",)),
    )(dx3, ple, gp)


def _masked_ws(ws_ref, g):
    row = lax.broadcasted_iota(I32, (GBLOCK, GBLOCK), 0)
    col = lax.broadcasted_iota(I32, (GBLOCK, GBLOCK), 1)
    keep = (col // CHUNK) <= (row // CHUNK)
    return jnp.where(keep, ws_ref[g], 0.0), keep


def _layernorm_parts(zv):
    mu = jnp.mean(zv, axis=-1, keepdims=True)
    xc = zv - mu
    rs = lax.rsqrt(jnp.mean(xc * xc, axis=-1, keepdims=True) + EPS)
    return xc * rs, rs


def _gmlp_fwd(zuvg, ln_g, ln_b, w_s, bs_t, *, name):
    s, w = zuvg.shape[0], GROUPS * GDIM

    def body(zu_ref, zv_ref, lng_ref, lnb_ref, ws_ref, bs_ref, a_ref):
        zu = _gelu(zu_ref[...])
        zv = _gelu(zv_ref[...])
        xhat, _ = _layernorm_parts(zv)
        vln = (xhat * lng_ref[...] + lnb_ref[...]).astype(BF16)
        for g in range(GROUPS):
            wm, _ = _masked_ws(ws_ref, g)
            mixed = _dot(wm.astype(BF16), vln[:, g * GDIM:(g + 1) * GDIM], NN) + bs_ref[:, g:g + 1]
            a_ref[:, g * GDIM:(g + 1) * GDIM] = (zu[:, g * GDIM:(g + 1) * GDIM] * mixed).astype(BF16)

    return pl.pallas_call(
        body, name=name, grid=(s // GBLOCK,),
        in_specs=[_row_spec(GBLOCK, w, 0), _row_spec(GBLOCK, w, 1), _full_spec((1, w)), _full_spec((1, w)),
                  _full_spec((GROUPS, GBLOCK, GBLOCK)), _full_spec((GBLOCK, 128))],
        out_specs=_row_spec(GBLOCK, w),
        out_shape=jax.ShapeDtypeStruct((s, w), BF16), compiler_params=_cparams(("parallel",)),
    )(zuvg, zuvg, ln_g, ln_b, w_s, bs_t)


def _gmlp_bwd(da, zuvg, ln_g, ln_b, w_s, bs_t, carry=None, *, name):
    s, w = zuvg.shape[0], GROUPS * GDIM

    def body(da_ref, zu_ref, zv_ref, lng_ref, lnb_ref, ws_ref, bs_ref,
             dzu_ref, dzv_ref, dws_ref, dbs_ref, dlng_ref, dlnb_ref, dvln_ref):
        i = pl.program_id(0)
        zu, dzu_g = _gelu_and_grad(zu_ref[...])
        zv, dzv_g = _gelu_and_grad(zv_ref[...])
        xhat, rs = _layernorm_parts(zv)
        vln = (xhat * lng_ref[...] + lnb_ref[...]).astype(BF16)
        dav = da_ref[...].astype(F32)
        lane = lax.broadcasted_iota(I32, (GBLOCK, 128), 1)
        dbs = jnp.zeros((GBLOCK, 128), F32)

        @pl.when(i == 0)
        def _():
            dws_ref[...] = jnp.zeros_like(dws_ref)

        for g in range(GROUPS):
            sl = slice(g * GDIM, (g + 1) * GDIM)
            wm, keep = _masked_ws(ws_ref, g)
            wmb = wm.astype(BF16)
            vg = vln[:, sl]
            mixed = _dot(wmb, vg, NN) + bs_ref[:, g:g + 1]
            dag = dav[:, sl]
            dzu_ref[:, sl] = (dag * mixed * dzu_g[:, sl]).astype(BF16)
            dmix = dag * zu[:, sl]
            dmb = dmix.astype(BF16)
            dws_ref[g] += jnp.where(keep, _dot(dmb, vg, NT), 0.0)
            dbs = jnp.where(lane == g, jnp.sum(dmix, axis=1, keepdims=True), dbs)
            dvln_ref[:, sl] = _dot(wmb, dmb, TN)
        dvln = dvln_ref[...]
        dxhat = dvln * lng_ref[...]
        dzv = rs * (dxhat - jnp.mean(dxhat, axis=-1, keepdims=True)
                    - xhat * jnp.mean(dxhat * xhat, axis=-1, keepdims=True))
        dzv_ref[...] = (dzv * dzv_g).astype(BF16)
        dlng = jnp.sum(dvln * xhat, axis=0, keepdims=True)
        dlnb = jnp.sum(dvln, axis=0, keepdims=True)

        @pl.when(i == 0)
        def _():
            dbs_ref[...] = dbs
            dlng_ref[...] = dlng
            dlnb_ref[...] = dlnb

        @pl.when(i > 0)
        def _():
            dbs_ref[...] += dbs
            dlng_ref[...] += dlng
            dlnb_ref[...] += dlnb

    return _carry_call(
        body, carry, name=name, grid=(s // GBLOCK,),
        in_specs=[_row_spec(GBLOCK, w), _row_spec(GBLOCK, w, 0), _row_spec(GBLOCK, w, 1), _full_spec((1, w)),
                  _full_spec((1, w)), _full_spec((GROUPS, GBLOCK, GBLOCK)), _full_spec((GBLOCK, 128))],
        out_specs=[_row_spec(GBLOCK, w), _row_spec(GBLOCK, w), _full_spec((GROUPS, GBLOCK, GBLOCK)),
                   _full_spec((GBLOCK, 128)), _full_spec((1, w)), _full_spec((1, w))],
        out_shape=[jax.ShapeDtypeStruct((s, w), BF16), jax.ShapeDtypeStruct((s, w), BF16),
                   jax.ShapeDtypeStruct((GROUPS, GBLOCK, GBLOCK), F32), jax.ShapeDtypeStruct((GBLOCK, 128), F32),
                   jax.ShapeDtypeStruct((1, w), F32), jax.ShapeDtypeStruct((1, w), F32)],
        scratch_shapes=[pltpu.VMEM((GBLOCK, w), F32)], args=[da, zuvg, zuvg, ln_g, ln_b, w_s, bs_t])


def _shift_down(u, k):
    row = lax.broadcasted_iota(I32, u.shape, 0)
    return jnp.where(row >= k, pltpu.roll(u, k, 0), 0.0)


def _shift_up(u, k):
    s = u.shape[0]
    row = lax.broadcasted_iota(I32, u.shape, 0)
    return jnp.where(row < s - k, pltpu.roll(u, s - k, 0), 0.0)


def _conv(u, w_ref, b_ref):
    return b_ref[...] + w_ref[0:1, :] * _shift_down(u, 2) + w_ref[1:2, :] * _shift_down(u, 1) + w_ref[2:3, :] * u


def _conv_specs(s, f, tc):
    nc = f // tc
    half = lambda rows: [pl.BlockSpec((rows, tc), lambda j: (0, j)), pl.BlockSpec((rows, tc), lambda j: (0, nc + j))]
    return half(s), half(3), half(1)


def _convglu_fwd(up, conv_w, conv_b, *, name, tc=256):
    s, f = up.shape[0], up.shape[1] // 2
    up_specs, w_specs, b_specs = _conv_specs(s, f, tc)

    def body(ua_ref, ug_ref, wa_ref, wg_ref, ba_ref, bg_ref, o_ref):
        ca = _conv(ua_ref[...], wa_ref, ba_ref)
        cg = _conv(ug_ref[...], wg_ref, bg_ref)
        o_ref[...] = (_gelu(ca) * cg).astype(BF16)

    return pl.pallas_call(
        body, name=name, grid=(f // tc,),
        in_specs=up_specs + w_specs + b_specs, out_specs=up_specs[0],
        out_shape=jax.ShapeDtypeStruct((s, f), BF16), compiler_params=_cparams(("parallel",)),
    )(up, up, conv_w, conv_w, conv_b, conv_b)


def _convglu_bwd(dact, up, conv_w, conv_b, *, name, tc=256):
    s, f = up.shape[0], up.shape[1] // 2
    up_specs, w_specs, b_specs = _conv_specs(s, f, tc)

    def half(dc, taps, w_ref, du_ref, dw_ref, db_ref):
        db_ref[...] = jnp.sum(dc, axis=0, keepdims=True)
        for k in range(3):
            dw_ref[k:k + 1, :] = jnp.sum(dc * taps[k], axis=0, keepdims=True)
        du = w_ref[2:3, :] * dc + w_ref[1:2, :] * _shift_up(dc, 1) + w_ref[0:1, :] * _shift_up(dc, 2)
        du_ref[...] = du.astype(BF16)

    def body(d_ref, ua_ref, ug_ref, wa_ref, wg_ref, ba_ref, bg_ref,
             dua_ref, dug_ref, dwa_ref, dwg_ref, dba_ref, dbg_ref):
        taps_a = (_shift_down(ua_ref[...], 2), _shift_down(ua_ref[...], 1), ua_ref[...])
        taps_g = (_shift_down(ug_ref[...], 2), _shift_down(ug_ref[...], 1), ug_ref[...])
        conv = lambda taps, w_ref, b_ref: b_ref[...] + w_ref[0:1, :] * taps[0] + w_ref[1:2, :] * taps[1] + w_ref[2:3, :] * taps[2]
        ca = conv(taps_a, wa_ref, ba_ref)
        cg = conv(taps_g, wg_ref, bg_ref)
        ga, dga = _gelu_and_grad(ca)
        dv = d_ref[...].astype(F32)
        half(dv * cg * dga, taps_a, wa_ref, dua_ref, dwa_ref, dba_ref)
        half(dv * ga, taps_g, wg_ref, dug_ref, dwg_ref, dbg_ref)

    col, w3, b1 = up_specs[0], w_specs[0], b_specs[0]
    return pl.pallas_call(
        body, name=name, grid=(f // tc,),
        in_specs=[col] + up_specs + w_specs + b_specs, out_specs=[col, col, w3, w3, b1, b1],
        out_shape=[jax.ShapeDtypeStruct((s, f), BF16), jax.ShapeDtypeStruct((s, f), BF16),
                   jax.ShapeDtypeStruct((3, f), F32), jax.ShapeDtypeStruct((3, f), F32),
                   jax.ShapeDtypeStruct((1, f), F32), jax.ShapeDtypeStruct((1, f), F32)],
        compiler_params=_cparams(("parallel",)),
    )(dact, up, up, conv_w, conv_w, conv_b, conv_b)


def _tri_dot(tri, x):
    b0 = x.astype(BF16)
    r1 = x - b0.astype(F32)
    b1 = r1.astype(BF16)
    b2 = (r1 - b1.astype(F32)).astype(BF16)
    return _dot(tri, b0, NN) + _dot(tri, b1, NN) + _dot(tri, b2, NN)


def _log_sigmoid(x):
    return jnp.minimum(x, 0.0) - jnp.log(1.0 + jnp.exp(-jnp.abs(x)))


def _expand_heads(col16, rows):
    head_of_lane = lax.broadcasted_iota(I32, (rows, HEADS * HEAD_DIM), 1) // HEAD_DIM
    out = jnp.zeros((rows, HEADS * HEAD_DIM), F32)
    for h in range(HEADS):
        out = jnp.where(head_of_lane == h, col16[:, h:h + 1], out)
    return out


def _forget_cumsum(f_logit, b_f, *, name):
    s = f_logit.shape[0]
    nb = s // 128

    def body(f_ref, b_ref, cqe_ref):
        row = lax.broadcasted_iota(I32, (128, 128), 0)
        col = lax.broadcasted_iota(I32, (128, 128), 1)
        tri = (col <= row).astype(BF16)

        def step(n, carry):
            r0 = pl.multiple_of(n * 128, 128)
            lf = _log_sigmoid(f_ref[pl.ds(r0, 128), :] + b_ref[...])
            cum = _tri_dot(tri, lf) + carry
            cqe_ref[pl.ds(r0, 128), :] = _expand_heads(cum, 128)
            return cum[127:128, :]

        lax.fori_loop(0, nb, step, jnp.zeros((1, 128), F32))

    return pl.pallas_call(
        body, name=name, grid=(1,),
        in_specs=[_full_spec((s, 128)), _full_spec((1, 128))],
        out_specs=_full_spec((s, HEADS * HEAD_DIM)),
        out_shape=jax.ShapeDtypeStruct((s, HEADS * HEAD_DIM), F32),
        compiler_params=_cparams(("arbitrary",)),
    )(f_logit, b_f)


def _forget_bwd(dcq16, sum_q16, f_logit, b_f, *, name):
    s = f_logit.shape[0]
    nb = s // 128

    def body(a_ref, k_ref, f_ref, b_ref, df_ref, db_ref):
        row = lax.broadcasted_iota(I32, (128, 128), 0)
        col = lax.broadcasted_iota(I32, (128, 128), 1)
        tri_rev = (col >= row).astype(BF16)

        def step(m, carry):
            suffix, dbsum = carry
            n = nb - 1 - m
            r0 = pl.multiple_of(n * 128, 128)
            dcum = a_ref[pl.ds(r0, 128), :] - k_ref[pl.ds(r0, 128), :]
            dlf = _tri_dot(tri_rev, dcum) + suffix
            df = dlf * _sigmoid(-(f_ref[pl.ds(r0, 128), :] + b_ref[...]))
            df_ref[pl.ds(r0, 128), :] = df.astype(BF16)
            return dlf[0:1, :], dbsum + jnp.sum(df, axis=0, keepdims=True)

        _, dbsum = lax.fori_loop(0, nb, step, (jnp.zeros((1, 128), F32), jnp.zeros((1, 128), F32)))
        db_ref[...] = dbsum

    return pl.pallas_call(
        body, name=name, grid=(1,),
        in_specs=[_full_spec((s, 128))] * 3 + [_full_spec((1, 128))],
        out_specs=[_full_spec((s, 128)), _full_spec((1, 128))],
        out_shape=[jax.ShapeDtypeStruct((s, 128), BF16), jax.ShapeDtypeStruct((1, 128), F32)],
        compiler_params=_cparams(("arbitrary",)),
    )(dcq16, sum_q16, f_logit, b_f)


ATT_T = 256


def _head_lanes(rows):
    return lax.broadcasted_iota(I32, (rows, 128), 1) < HEAD_DIM


def _bf16_pieces(c):
    p0 = c.astype(BF16).astype(F32)
    r = c - p0
    p1 = r.astype(BF16).astype(F32)
    p2 = (r - p1).astype(BF16).astype(F32)
    return p0, p1, p2


def _col_reduce(x, op):
    rows = x.shape[0]
    while rows > 8:
        rows //= 2
        x = op(x[:rows], x[rows:])
    return jnp.max(x, axis=0, keepdims=True) if op is jnp.maximum else jnp.sum(x, axis=0, keepdims=True)


def _attn_prep(qkv, cqe, *, name):
    s = qkv.shape[0]
    npair = HEADS // 2

    def body(q_ref, k_ref, v_ref, c_ref, qa_ref, ka_ref, vt_ref):
        rows = 128
        lane = lax.broadcasted_iota(I32, (rows, 128), 1)

        def chunk(n, _):
            r0 = pl.multiple_of(n * rows, rows)
            sl = pl.ds(r0, rows)
            qv = q_ref[sl, :].astype(F32) * ATT_SCALE
            kv = k_ref[sl, :].astype(F32)
            for e in range(2):
                mine = (lane < HEAD_DIM) if e == 0 else (lane >= HEAD_DIM)
                base = HEAD_DIM * (1 - e)
                p0, p1, p2 = _bf16_pieces(c_ref[sl, HEAD_DIM * e:HEAD_DIM * e + 1])
                ones_hi = jnp.where((lane >= base + 3) & (lane < base + 6), 1.0, 0.0)
                ones_lo = jnp.where((lane >= base) & (lane < base + 3), 1.0, 0.0)
                qa = jnp.where(mine, qv, jnp.where(lane == base, p0, jnp.where(lane == base + 1, p1,
                               jnp.where(lane == base + 2, p2, ones_hi))))
                ka = jnp.where(mine, kv, jnp.where(lane == base + 3, -p0, jnp.where(lane == base + 4, -p1,
                               jnp.where(lane == base + 5, -p2, ones_lo))))
                qa_ref[e, sl, :] = qa.astype(BF16)
                ka_ref[e, sl, :] = ka.astype(BF16)
            vt_ref[0, :, sl] = v_ref[sl, :].astype(F32).T.astype(BF16)
            return 0

        lax.fori_loop(0, s // rows, chunk, 0)

    pair = pl.BlockSpec((2, s, 128), lambda hp: (hp, 0, 0))
    return pl.pallas_call(
        body, name=name, grid=(npair,),
        in_specs=[pl.BlockSpec((s, 128), lambda hp: (0, hp)), pl.BlockSpec((s, 128), lambda hp: (0, npair + hp)),
                  pl.BlockSpec((s, 128), lambda hp: (0, 2 * npair + hp)), pl.BlockSpec((s, 128), lambda hp: (0, hp))],
        out_specs=[pair, pair, pl.BlockSpec((1, 128, s), lambda hp: (hp, 0, 0))],
        out_shape=[jax.ShapeDtypeStruct((HEADS, s, 128), BF16), jax.ShapeDtypeStruct((HEADS, s, 128), BF16),
                   jax.ShapeDtypeStruct((npair, 128, s), BF16)],
        compiler_params=_cparams(("parallel",)),
    )(qkv, qkv, qkv, cqe)


def _attn_fwd(qa, ka, vt, carry=None, *, name):
    s = qa.shape[1]
    t = ATT_T
    nq = s // t
    npair = HEADS // 2

    def body(qa_ref, ka_ref, vt_ref, o_ref, lse_ref):
        i = pl.program_id(1)
        krow = lax.broadcasted_iota(I32, (t, t), 0)
        qcol = lax.broadcasted_iota(I32, (t, t), 1)
        sub = lax.broadcasted_iota(I32, (128, t), 0)
        row8 = lax.broadcasted_iota(I32, (8, t), 0)
        qbs = (qa_ref[0], qa_ref[1])
        tk = 2 * t

        def step(j, carry, diag):
            c0 = pl.multiple_of(j * tk, tk)
            vtb = vt_ref[0, :, pl.ds(c0, tk)]
            sts = [_dot(ka_ref[e, pl.ds(c0, tk), :], qbs[e], NT) for e in range(2)]
            if diag:
                keep = (lax.broadcasted_iota(I32, (tk, t), 0) - lax.broadcasted_iota(I32, (tk, t), 1)) <= t * (i % 2)
                sts = [jnp.where(keep, st, NEG) for st in sts]
            pts, stats = [], []
            for e in range(2):
                m, l, _ = carry[e]
                m_new = jnp.maximum(m, _col_reduce(sts[e], jnp.maximum))
                alpha = jnp.exp(m - m_new)
                pt = jnp.exp(sts[e] - m_new)
                stats.append((m_new, alpha, alpha * l + _col_reduce(pt, jnp.add)))
                pts.append(pt.astype(BF16))
            pvs = [_dot(vtb, pts[e], NN) for e in range(2)]
            return tuple((stats[e][0], stats[e][2], stats[e][1] * carry[e][2] + pvs[e]) for e in range(2))

        init = (jnp.full((1, t), NEG, F32), jnp.zeros((1, t), F32), jnp.zeros((128, t), F32))
        carry = lax.fori_loop(0, i // 2, functools.partial(step, diag=False), (init, init))
        (m0, l0, acc0), (m1, l1, acc1) = step(i // 2, carry, True)
        o_pair = jnp.where(sub < HEAD_DIM, acc0 / l0, acc1 / l1)
        o_ref[...] = o_pair.T.astype(BF16)
        lse_ref[0] = jnp.where(row8 == 0, m0 + jnp.log(l0), jnp.where(row8 == 1, m1 + jnp.log(l1), 0.0))

    return _carry_call(
        body, carry, name=name, grid=(npair, nq),
        in_specs=[pl.BlockSpec((2, t, 128), lambda hp, i: (hp, i, 0)), pl.BlockSpec((2, s, 128), lambda hp, i: (hp, 0, 0)),
                  pl.BlockSpec((1, 128, s), lambda hp, i: (hp, 0, 0))],
        out_specs=[pl.BlockSpec((t, 128), lambda hp, i: (i, hp)), pl.BlockSpec((1, 8, t), lambda hp, i: (hp, 0, i))],
        out_shape=[jax.ShapeDtypeStruct((s, HEADS * HEAD_DIM), BF16), jax.ShapeDtypeStruct((npair, 8, s), F32)],
        scratch_shapes=[], args=[qa, ka, vt])


def _attn_delta(do, o, carry=None, *, name):
    s = do.shape[0]

    def body(do_ref, o_ref, d_ref):
        prod = do_ref[...].astype(F32) * o_ref[...].astype(F32)
        row = lax.broadcasted_iota(I32, (8, 128), 0)
        lane = lax.broadcasted_iota(I32, (8, 128), 1)
        sel = ((row == 0) & (lane < HEAD_DIM) | (row == 1) & (lane >= HEAD_DIM)).astype(BF16)
        p0, p1, p2 = _bf16_pieces(prod)
        d_ref[0] = (_dot(sel, p0.astype(BF16), NT) + _dot(sel, p1.astype(BF16), NT)) + _dot(sel, p2.astype(BF16), NT)

    pair = pl.BlockSpec((s, 128), lambda hp: (0, hp))
    (delta3,), carried = _carry_call(
        body, carry, name=name, grid=(HEADS // 2,), in_specs=[pair, pair],
        out_specs=[pl.BlockSpec((1, 8, s), lambda hp: (hp, 0, 0))],
        out_shape=[jax.ShapeDtypeStruct((HEADS // 2, 8, s), F32)], scratch_shapes=[], args=[do, o])
    return delta3, carried


def _attn_bwd(qa, ka, qkv, do, lse3, delta3, carry=None, *, name):
    s = qa.shape[1]
    t = ATT_T
    nb = s // t
    npair = HEADS // 2

    def body(qa_ref, ka_ref, v_ref, do_ref, lse_ref, delta_ref, dq_ref, dk_ref, dv_ref, aux_ref, dcq_ref, dqt):
        hp = pl.program_id(0)
        first = _head_lanes(t)
        lane = lax.broadcasted_iota(I32, (t, 128), 1)
        dqt[...] = jnp.zeros_like(dqt)

        @pl.when(hp == 0)
        def _():
            aux_ref[...] = jnp.zeros_like(aux_ref)

        krow = lax.broadcasted_iota(I32, (t, t), 0)
        qcol = lax.broadcasted_iota(I32, (t, t), 1)

        def key_block(j, _):
            c0 = pl.multiple_of(j * t, t)
            vb = v_ref[pl.ds(c0, t), :]
            kbs = (ka_ref[0, pl.ds(c0, t), :], ka_ref[1, pl.ds(c0, t), :])
            kbts = tuple(kb.astype(F32).T.astype(BF16) for kb in kbs)
            vhs = (jnp.where(first, vb, jnp.zeros_like(vb)), jnp.where(first, jnp.zeros_like(vb), vb))

            def query_block(i, carry, diag):
                r0 = pl.multiple_of(i * t, t)
                dob = do_ref[pl.ds(r0, t), :]
                sts = [_dot(kbs[e], qa_ref[e, pl.ds(r0, t), :], NT) for e in range(2)]
                dpts = [_dot(vhs[e], dob, NT) for e in range(2)]
                ptbs, dsbs = [], []
                for e in range(2):
                    st = jnp.where(krow <= qcol, sts[e], NEG) if diag else sts[e]
                    pt = jnp.exp(st - lse_ref[0, e:e + 1, pl.ds(r0, t)])
                    dsbs.append((pt * (dpts[e] - delta_ref[0, e:e + 1, pl.ds(r0, t)])).astype(BF16))
                    ptbs.append(pt.astype(BF16))
                out = []
                for e in range(2):
                    dk_a, dv_a = carry[e]
                    dv_a = dv_a + _dot(ptbs[e], dob, NN)
                    dk_a = dk_a + _dot(dsbs[e], qa_ref[e, pl.ds(r0, t), :], NN)
                    dqt[e, :, pl.ds(r0, t)] += _dot(kbts[e], dsbs[e], NN)
                    out.append((dk_a, dv_a))
                return tuple(out)

            zero = jnp.zeros((t, 128), F32)
            carry = query_block(j, ((zero, zero), (zero, zero)), True)
            (dk0, dv0), (dk1, dv1) = lax.fori_loop(j + 1, nb, functools.partial(query_block, diag=False), carry)
            dk_ref[pl.ds(c0, t), :] = jnp.where(first, dk0, dk1).astype(BF16)
            dv_ref[pl.ds(c0, t), :] = jnp.where(first, dv0, dv1).astype(BF16)
            sum_q = jnp.where(lane == 2 * hp, dk0[:, HEAD_DIM + 3:HEAD_DIM + 4],
                              jnp.where(lane == 2 * hp + 1, dk1[:, 3:4], aux_ref[pl.ds(c0, t), :]))
            aux_ref[pl.ds(c0, t), :] = sum_q
            return 0

        lax.fori_loop(0, nb, key_block, 0)
        sub = lax.broadcasted_iota(I32, (128, s), 0)
        row8 = lax.broadcasted_iota(I32, (8, s), 0)
        dq_ref[...] = (jnp.where(sub < HEAD_DIM, dqt[0], dqt[1]) * ATT_SCALE).T.astype(BF16)
        dcq_ref[0] = jnp.where(row8 == 0, dqt[0, HEAD_DIM:HEAD_DIM + 1, :], jnp.where(row8 == 1, dqt[1, 0:1, :], 0.0))

    def pair_cols(off):
        return pl.BlockSpec((s, 128), lambda hp: (0, off + hp))

    heads = pl.BlockSpec((2, s, 128), lambda hp: (hp, 0, 0))
    rows = pl.BlockSpec((1, 8, s), lambda hp: (hp, 0, 0))
    wide = jax.ShapeDtypeStruct((s, HEADS * HEAD_DIM), BF16)
    return _carry_call(
        body, carry, name=name, grid=(npair,),
        in_specs=[heads, heads, pair_cols(2 * npair), pair_cols(0), rows, rows],
        out_specs=[pair_cols(0), pair_cols(0), pair_cols(0), pl.BlockSpec((s, 128), lambda hp: (0, 0)), rows],
        out_shape=[wide, wide, wide, jax.ShapeDtypeStruct((s, 128), F32), jax.ShapeDtypeStruct((npair, 8, s), F32)],
        scratch_shapes=[pltpu.VMEM((2, 128, s), F32)], args=[qa, ka, qkv, do, lse3, delta3])


def _adam_math(w, g, m, v):
    m = ADAM_B1 * m + (1.0 - ADAM_B1) * g
    v = ADAM_B2 * v + (1.0 - ADAM_B2) * (g * g)
    m_hat = m / (1.0 - ADAM_B1 ** ADAM_STEP)
    v_hat = v / (1.0 - ADAM_B2 ** ADAM_STEP)
    delta = -ADAM_LR * (m_hat / (jnp.sqrt(v_hat) + ADAM_EPS) + ADAM_WD * w)
    return delta, m, v


def _sum_pairs(keep, recv, pos, *, name):
    _, r, c = recv.shape
    tr = _row_tile(r, 512)

    def body(pos_ref, a_ref, b_ref, o32_ref, o16_ref):
        tot = a_ref[...].astype(F32) + b_ref[...].astype(F32)
        o16_ref[...] = tot.astype(BF16)

        @pl.when(pl.program_id(1) == 2 * pos_ref[0] + pos_ref[1])
        def _():
            o32_ref[...] = tot

    out = pl.BlockSpec((1, tr, c), lambda i, q, pos: (q, i, 0))
    grid_spec = pltpu.PrefetchScalarGridSpec(
        num_scalar_prefetch=1, grid=(r // tr, 4),
        in_specs=[pl.BlockSpec((1, tr, c), lambda i, q, pos: (2 * q + pos[2], i, 0)), out],
        out_specs=[pl.BlockSpec((1, tr, c), lambda i, q, pos: (0, i, 0)), out])
    return pl.pallas_call(
        body, name=name, grid_spec=grid_spec,
        out_shape=[jax.ShapeDtypeStruct((1, r, c), F32), jax.ShapeDtypeStruct((4, r, c), BF16)],
        compiler_params=_cparams(("arbitrary", "arbitrary")),
    )(pos, keep, recv)


def _adam_sharded(psum, recv, w, m, v, pos, *, name):
    r, c = w.shape
    tr = _row_tile(r, 320)

    def body(pos_ref, p_ref, r_ref, w_ref, m_ref, v_ref, g_ref, d_ref, mo_ref, vo_ref):
        g = p_ref[0] + r_ref[0].astype(F32) + r_ref[1].astype(F32) + r_ref[2].astype(F32)
        delta, mn, vn = _adam_math(w_ref[...], g, m_ref[...], v_ref[...])
        g_ref[...] = g
        d_ref[...] = delta
        mo_ref[...] = mn
        vo_ref[...] = vn

    row = pl.BlockSpec((tr, c), lambda i, pos: (i, 0))
    grid_spec = pltpu.PrefetchScalarGridSpec(
        num_scalar_prefetch=1, grid=(r // tr,),
        in_specs=[pl.BlockSpec((1, tr, c), lambda i, pos: (0, i, 0)),
                  pl.BlockSpec((3, tr, c), lambda i, pos: (0, i, 0)), row, row, row],
        out_specs=[row, row, row, row])
    o = jax.ShapeDtypeStruct((r, c), F32)
    return pl.pallas_call(
        body, name=name, grid_spec=grid_spec, out_shape=[o, o, o, o],
        compiler_params=_cparams(("parallel",)),
    )(pos, psum, recv, w, m, v)


def _adam_replicated(chip_sums, last, w, m, v, *, name):
    r = w.shape[0]

    def body(s_ref, l_ref, w_ref, m_ref, v_ref, g_ref, d_ref, mo_ref, vo_ref):
        g = (((s_ref[0] + s_ref[1]) + s_ref[2]) + s_ref[3]) + l_ref[...]
        delta, mn, vn = _adam_math(w_ref[...], g, m_ref[...], v_ref[...])
        g_ref[...] = g
        d_ref[...] = delta
        mo_ref[...] = mn
        vo_ref[...] = vn

    o = jax.ShapeDtypeStruct((r, 1024), F32)
    full = _full_spec((r, 1024))
    return pl.pallas_call(
        body, name=name, grid=(1,),
        in_specs=[_full_spec((4, r, 1024)), full, full, full, full], out_specs=[full] * 4, out_shape=[o] * 4,
        compiler_params=_cparams(("arbitrary",)),
    )(chip_sums, last, w, m, v)


ASM_OUT = 256
ASM_SRC = 304


def _w_in_row(r):
    return r if r < 2048 else (r + O_G - 2048 if r < 4096 else r - 2048)


def _assemble_wt_main(g, *, name):
    table = []
    for blk in range(MAIN_COLS // ASM_OUT):
        j, l0 = divmod(_w_in_row(blk * ASM_OUT), IN_SHARD)
        sb = l0 // ASM_SRC
        n_a = min(ASM_OUT, min(IN_SHARD, (sb + 1) * ASM_SRC) - l0)
        if n_a == ASM_OUT:
            nxt = (j, sb)
        elif l0 + n_a == IN_SHARD:
            nxt = (j + 1, 0)
        else:
            nxt = (j, sb + 1)
        table.append((j, sb, l0 - sb * ASM_SRC, n_a) + nxt)

    def body(tab_ref, a_ref, b_ref, o_ref):
        blk = pl.program_id(0)
        off, n_a = tab_ref[blk, 2], tab_ref[blk, 3]
        r = lax.broadcasted_iota(I32, (ASM_OUT, ASM_SRC), 0)
        k = lax.broadcasted_iota(I32, (ASM_OUT, ASM_SRC), 1)
        sel_a = ((k == r + off) & (r < n_a)).astype(BF16)
        sel_b = ((k == r - n_a) & (r >= n_a)).astype(BF16)
        o_ref[...] = (_dot(sel_a, a_ref[0], NN) + _dot(sel_b, b_ref[0], NN)).astype(BF16)

    src = lambda c: pl.BlockSpec((1, ASM_SRC, D_MODEL), lambda blk, tab: (tab[blk, c], tab[blk, c + 1], 0))
    grid_spec = pltpu.PrefetchScalarGridSpec(
        num_scalar_prefetch=1, grid=(len(table),), in_specs=[src(0), src(4)],
        out_specs=pl.BlockSpec((ASM_OUT, D_MODEL), lambda blk, tab: (blk, 0)))
    return pl.pallas_call(
        body, name=name, grid_spec=grid_spec, out_shape=jax.ShapeDtypeStruct((MAIN_COLS, D_MODEL), BF16),
        compiler_params=_cparams(("parallel",)),
    )(jnp.asarray(table, I32), g, g)


def _pair_sum_small(mine, theirs, *, name):
    def body(a_ref, b_ref, o_ref):
        o_ref[...] = a_ref[...] + b_ref[...]

    full = _full_spec(mine.shape)
    return pl.pallas_call(
        body, name=name, grid=(1,), in_specs=[full, full], out_specs=full,
        out_shape=jax.ShapeDtypeStruct(mine.shape, F32), compiler_params=_cparams(("arbitrary",)),
    )(mine, theirs)


ANY = pl.BlockSpec(memory_space=pl.ANY)
OTHER_CHIPS = ((1, 0), (0, 1), (1, 1))


class _Carry:
    def __init__(self, inputs, out_shapes, scratch, start, wait, aliases=None):
        self.inputs, self.out_shapes, self.scratch = list(inputs), list(out_shapes), list(scratch)
        self.start, self.wait, self.aliases = start, wait, dict(aliases or {})


def _carried(body, carry, n_in, n_out, grid):
    if carry is None:
        return body
    ci, co, cs = len(carry.inputs), len(carry.out_shapes), len(carry.scratch)

    def wrapped(*refs):
        ins, cins = refs[:n_in], refs[n_in:n_in + ci]
        outs, couts = refs[n_in + ci:n_in + ci + n_out], refs[n_in + ci + n_out:n_in + ci + n_out + co]
        rest = refs[n_in + ci + n_out + co:]
        scratch, cscr = rest[:len(rest) - cs], rest[len(rest) - cs:]
        first, last = None, None
        for axis, size in enumerate(grid):
            f, l = pl.program_id(axis) == 0, pl.program_id(axis) == size - 1
            first = f if first is None else first & f
            last = l if last is None else last & l

        @pl.when(first)
        def _():
            carry.start(cins, couts, cscr)

        body(*ins, *outs, *scratch)

        @pl.when(last)
        def _():
            carry.wait(cins, couts, cscr)

    return wrapped


def _carry_call(body, carry, *, name, grid, in_specs, out_specs, out_shape, scratch_shapes, args, vmem=True,
                own_aliases=None):
    n_in, n_out = len(in_specs), len(out_specs)
    extra_in = [ANY] * len(carry.inputs) if carry else []
    extra_out = [ANY] * len(carry.out_shapes) if carry else []
    aliases = dict(own_aliases or {})
    if carry:
        aliases.update({n_in + i: n_out + o for i, o in carry.aliases.items()})
    out = pl.pallas_call(
        _carried(body, carry, n_in, n_out, grid), name=name, grid=grid,
        in_specs=list(in_specs) + extra_in, out_specs=list(out_specs) + extra_out,
        out_shape=list(out_shape) + (carry.out_shapes if carry else []),
        scratch_shapes=list(scratch_shapes) + (carry.scratch if carry else []),
        input_output_aliases=aliases,
        compiler_params=_cparams(("arbitrary",) * len(grid)) if vmem else None,
    )(*args, *(carry.inputs if carry else []))
    return list(out[:n_out]), list(out[n_out:])


def _run_carry(carry, *, name):
    return _carry_call(lambda: None, carry, name=name, grid=(1,), in_specs=[], out_specs=[], out_shape=[],
                       scratch_shapes=[], args=[], vmem=False)[1]


def _sems(n):
    return [pltpu.SemaphoreType.DMA((n,)), pltpu.SemaphoreType.DMA((n,))]


def _carry_gather1(shards):
    n = len(shards)

    def copies(x_refs, out_refs, scr, with_arrivals):
        send_sems, recv_sems, local_sems = scr
        x, y, c = lax.axis_index("x"), lax.axis_index("y"), lax.axis_index("c")
        peers = [(x, y, 1 - c)] + [(x ^ fx, y ^ fy, c) for fx, fy in OTHER_CHIPS]
        local, sends, arrivals = [], [], []
        for t, (x_ref, out_ref) in enumerate(zip(x_refs, out_refs)):
            local.append(pltpu.make_async_copy(x_ref, out_ref.at[4 * x + 2 * y + c], local_sems.at[t]))
            for k, (px, py, pc) in enumerate(peers):
                sems = dict(send_sem=send_sems.at[4 * t + k], recv_sem=recv_sems.at[4 * t + k],
                            device_id=(px, py, pc), device_id_type=MESH)
                sends.append(pltpu.make_async_remote_copy(src_ref=x_ref, dst_ref=out_ref.at[4 * x + 2 * y + c], **sems))
                if with_arrivals:
                    arrivals.append(
                        pltpu.make_async_remote_copy(src_ref=x_ref, dst_ref=out_ref.at[4 * px + 2 * py + pc], **sems))
        return local, sends, arrivals

    def start(x_refs, out_refs, scr):
        local, sends, _ = copies(x_refs, out_refs, scr, False)
        for cp in local + sends:
            cp.start()

    def wait(x_refs, out_refs, scr):
        local, sends, arrivals = copies(x_refs, out_refs, scr, True)
        for cp in arrivals:
            cp.wait_recv()
        for cp in sends:
            cp.wait_send()
        for cp in local:
            cp.wait()

    return _Carry(shards, [jax.ShapeDtypeStruct((N_DEV,) + a.shape, a.dtype) for a in shards],
                  _sems(4 * n) + [pltpu.SemaphoreType.DMA((n,))], start, wait)


def _carry_gather2(gathered):
    n = len(gathered)

    def copies(in_refs, g_refs, scr, with_arrivals):
        send_sems, recv_sems = scr
        x, y, c = lax.axis_index("x"), lax.axis_index("y"), lax.axis_index("c")
        sends, arrivals = [], []
        for t in range(n):
            for j, (fx, fy) in enumerate(OTHER_CHIPS):
                px, py = x ^ fx, y ^ fy
                sems = dict(send_sem=send_sems.at[3 * t + j], recv_sem=recv_sems.at[3 * t + j],
                            device_id=(x, y, 1 - c), device_id_type=MESH)
                mine, theirs = 4 * px + 2 * py + c, 4 * px + 2 * py + (1 - c)
                sends.append(pltpu.make_async_remote_copy(src_ref=in_refs[t].at[mine], dst_ref=g_refs[t].at[mine], **sems))
                if with_arrivals:
                    arrivals.append(pltpu.make_async_remote_copy(
                        src_ref=in_refs[t].at[mine], dst_ref=g_refs[t].at[theirs], **sems))
        return sends, arrivals

    def start(in_refs, g_refs, scr):
        for cp in copies(in_refs, g_refs, scr, False)[0]:
            cp.start()

    def wait(in_refs, g_refs, scr):
        sends, arrivals = copies(in_refs, g_refs, scr, True)
        for cp in arrivals:
            cp.wait_recv()
        for cp in sends:
            cp.wait_send()

    return _Carry(gathered, [jax.ShapeDtypeStruct(a.shape, a.dtype) for a in gathered], _sems(3 * n), start, wait,
                  aliases={t: t for t in range(n)})


def _allreduce_rows(x, *, name):
    def body(x_ref, o_ref, sib_ref, mine_ref, tab_ref, send_sems, recv_sems):
        x, y, c = lax.axis_index("x"), lax.axis_index("y"), lax.axis_index("c")
        swap = pltpu.make_async_remote_copy(src_ref=x_ref, dst_ref=sib_ref, send_sem=send_sems.at[0],
                                            recv_sem=recv_sems.at[0], device_id=(x, y, 1 - c), device_id_type=MESH)
        swap.start()
        swap.wait()
        mine_ref[...] = x_ref[...] + sib_ref[...]
        tab_ref[pl.ds(2 * x + y, 1)] = mine_ref[...][None]

        def copy(k, slot):
            fx, fy = OTHER_CHIPS[k]
            return pltpu.make_async_remote_copy(
                src_ref=mine_ref, dst_ref=tab_ref.at[slot], send_sem=send_sems.at[1 + k], recv_sem=recv_sems.at[1 + k],
                device_id=(x ^ fx, y ^ fy, c), device_id_type=MESH)

        for k in range(3):
            copy(k, 2 * x + y).start()
        for k, (fx, fy) in enumerate(OTHER_CHIPS):
            copy(k, 2 * (x ^ fx) + (y ^ fy)).wait()
        o_ref[...] = ((tab_ref[0] + tab_ref[1]) + tab_ref[2]) + tab_ref[3]

    vmem = pl.BlockSpec(memory_space=pltpu.VMEM)
    return pl.pallas_call(
        body, name=name, out_shape=jax.ShapeDtypeStruct(x.shape, F32), in_specs=[vmem], out_specs=vmem,
        scratch_shapes=[pltpu.VMEM(x.shape, F32), pltpu.VMEM(x.shape, F32), pltpu.VMEM((4,) + x.shape, F32)] + _sems(4),
    )(x)


def _allgather(shards, *, name):
    n = len(shards)

    def body(*refs):
        x_refs, out_refs = refs[:n], refs[n:2 * n]
        send_sems, recv_sems, local_sems = refs[2 * n:]
        x, y, c = lax.axis_index("x"), lax.axis_index("y"), lax.axis_index("c")
        me, sibling = (x, y, c), (x, y, 1 - c)
        chips = [(x ^ fx, y ^ fy) for fx, fy in OTHER_CHIPS]

        def copy(t, k, block, to, from_input=False):
            px, py, pc = block
            slab = out_refs[t].at[4 * px + 2 * py + pc]
            return pltpu.make_async_remote_copy(
                src_ref=x_refs[t] if from_input else slab, dst_ref=slab,
                send_sem=send_sems.at[7 * t + k], recv_sem=recv_sems.at[7 * t + k], device_id=to, device_id_type=MESH)

        mine = [pltpu.make_async_copy(x_refs[t], out_refs[t].at[4 * x + 2 * y + c], local_sems.at[t]) for t in range(n)]
        for cp in mine:
            cp.start()
        first = []
        for t in range(n):
            first.append(copy(t, 0, me, sibling, from_input=True))
            first += [copy(t, 1 + j, me, (*chip, c), from_input=True) for j, chip in enumerate(chips)]
        for cp in first:
            cp.start()
        passed = []
        for j, chip in enumerate(chips):
            for t in range(n):
                copy(t, 1 + j, (*chip, c), me).wait_recv()
                fwd = copy(t, 4 + j, (*chip, c), sibling)
                fwd.start()
                passed.append(fwd)
        for t in range(n):
            copy(t, 0, sibling, me).wait_recv()
            for j, chip in enumerate(chips):
                copy(t, 4 + j, (*chip, 1 - c), me).wait_recv()
        for cp in first + passed:
            cp.wait_send()
        for cp in mine:
            cp.wait()

    return pl.pallas_call(
        body, name=name, out_shape=[jax.ShapeDtypeStruct((N_DEV,) + a.shape, a.dtype) for a in shards],
        in_specs=[ANY] * n, out_specs=[ANY] * n,
        scratch_shapes=[pltpu.SemaphoreType.DMA((7 * n,)), pltpu.SemaphoreType.DMA((7 * n,)),
                        pltpu.SemaphoreType.DMA((n,))],
    )(*shards)


def _carry_sibling(slabs, small=None):
    n = len(slabs)
    extra = [] if small is None else [small]

    def copies(in_refs, out_refs, scr):
        send_sems, recv_sems = scr
        x, y, c = lax.axis_index("x"), lax.axis_index("y"), lax.axis_index("c")
        sibling = (x, y, 1 - c)
        out = []
        for t in range(n):
            for q in range(4):
                out.append(pltpu.make_async_remote_copy(
                    src_ref=in_refs[t].at[2 * q + (1 - c)], dst_ref=out_refs[t].at[q],
                    send_sem=send_sems.at[4 * t + q], recv_sem=recv_sems.at[4 * t + q],
                    device_id=sibling, device_id_type=MESH))
        if extra:
            out.append(pltpu.make_async_remote_copy(
                src_ref=in_refs[n], dst_ref=out_refs[n], send_sem=send_sems.at[4 * n], recv_sem=recv_sems.at[4 * n],
                device_id=sibling, device_id_type=MESH))
        return out

    def start(*refs):
        for cp in copies(*refs):
            cp.start()

    def wait(*refs):
        for cp in copies(*refs):
            cp.wait()

    return _Carry(list(slabs) + extra,
                  [jax.ShapeDtypeStruct((4,) + a.shape[1:], a.dtype) for a in slabs]
                  + [jax.ShapeDtypeStruct(a.shape, a.dtype) for a in extra], _sems(4 * n + 1), start, wait)


def _carry_chips(psums, small_sum=None):
    n = len(psums)
    table = small_sum is not None

    def copies(in_refs, out_refs, scr, arrivals):
        send_sems, recv_sems = scr[0], scr[1]
        x, y, c = lax.axis_index("x"), lax.axis_index("y"), lax.axis_index("c")
        out = []
        for k, (fx, fy) in enumerate(OTHER_CHIPS):
            px, py = x ^ fx, y ^ fy
            for t in range(n):
                out.append(pltpu.make_async_remote_copy(
                    src_ref=in_refs[t].at[2 * px + py], dst_ref=out_refs[t].at[k],
                    send_sem=send_sems.at[3 * t + k], recv_sem=recv_sems.at[3 * t + k],
                    device_id=(px, py, c), device_id_type=MESH))
            if table:
                slot = 2 * px + py if arrivals else 2 * x + y
                out.append(pltpu.make_async_remote_copy(
                    src_ref=in_refs[n], dst_ref=out_refs[n].at[slot], send_sem=send_sems.at[3 * n + k],
                    recv_sem=recv_sems.at[3 * n + k], device_id=(px, py, c), device_id_type=MESH))
        return out

    def own(in_refs, out_refs, scr):
        x, y = lax.axis_index("x"), lax.axis_index("y")
        return pltpu.make_async_copy(in_refs[n], out_refs[n].at[2 * x + y], scr[2])

    def start(in_refs, out_refs, scr):
        if table:
            own(in_refs, out_refs, scr).start()
        for cp in copies(in_refs, out_refs, scr, False):
            cp.start()

    def wait(in_refs, out_refs, scr):
        for cp in copies(in_refs, out_refs, scr, True):
            cp.wait()
        if table:
            own(in_refs, out_refs, scr).wait()

    out_shapes = [jax.ShapeDtypeStruct((3,) + a.shape[1:], a.dtype) for a in psums]
    if table:
        out_shapes.append(jax.ShapeDtypeStruct((4,) + small_sum.shape, F32))
    return _Carry(list(psums) + ([small_sum] if table else []), out_shapes,
                  _sems(3 * n + 3) + ([pltpu.SemaphoreType.DMA] if table else []), start, wait)


def _to_comm(name, kind, block, dtype=BF16):
    a = block[0]
    if kind == "cols":
        a = a.T
        if name == "w_in":
            a = jnp.pad(a, ((0, IN_SHARD_PAD - IN_SHARD), (0, 0)))
    return a if kind == "f32" else a.astype(dtype)


def _from_comm(name, kind, a):
    if kind == "cols":
        if name == "w_in":
            a = a[:IN_SHARD]
        a = a.T
    return a[None]


def _assemble_weights(g):
    out = {}
    if "w_in" in g:
        out["wt_main"] = _assemble_wt_main(g["w_in"], name="assemble_w_in")
        j, l0 = divmod(O_F, IN_SHARD)
        out["wt_f"] = jnp.pad(g["w_in"][j, l0:l0 + HEADS], ((0, 128 - HEADS), (0, 0)))
    square = dict(w_branch_a="w_a", w_branch_b="w_b", w_out="w_out", w_ple_gate="w_pg")
    for long, short in square.items():
        if long in g:
            out[short] = g[long].reshape(D_MODEL, D_MODEL)
    if "w_up" in g:
        out["wt_up"] = g["w_up"].reshape(2 * D_FF, D_MODEL)
    if "conv_w" in g:
        out["conv_w"] = g["conv_w"].transpose(1, 0, 2).reshape(3, 2 * D_FF)
    if "w_down" in g:
        out["w_down"] = g["w_down"].reshape(D_FF, D_MODEL)
    if "w_ple" in g:
        out["wt_ple"] = g["w_ple"].reshape(D_MODEL, PLE_DIM)
    return out


def _grad_slabs(gr):
    out = {}
    if "wt_main" in gr:
        gm, gf = gr["wt_main"], gr["wt_f"]
        segments = ((0, 2048, gm, 0), (2048, O_F, gm, 2048), (O_F, O_G, gf, -O_F), (O_G, IN_COLS, gm, 2048 - O_G))
        slabs = []
        for j in range(N_DEV):
            lo, hi = j * IN_SHARD, (j + 1) * IN_SHARD
            pieces = [src[max(lo, a) + shift:min(hi, b) + shift] for a, b, src, shift in segments if max(lo, a) < min(hi, b)]
            pieces.append(jnp.zeros((IN_SHARD_PAD - IN_SHARD, D_MODEL), gm.dtype))
            slabs.append(jnp.concatenate(pieces, axis=0))
        out["w_in"] = jnp.stack(slabs)
    rows = dict(w_a="w_branch_a", w_b="w_branch_b", w_out="w_out", wt_up="w_up", w_down="w_down", w_pg="w_ple_gate")
    for short, long in rows.items():
        if short in gr:
            out[long] = gr[short].reshape(N_DEV, -1, D_MODEL)
    if "conv_w" in gr:
        out["conv_w"] = gr["conv_w"].reshape(3, N_DEV, -1).transpose(1, 0, 2)
    if "wt_ple" in gr:
        out["w_ple"] = gr["wt_ple"].reshape(N_DEV, -1, PLE_DIM)
    return {k: v.astype(BF16) for k, v in out.items()}


def _rows(a, rows):
    flat = a.reshape(-1)
    return jnp.pad(flat, (0, rows * 1024 - flat.shape[0])).reshape(rows, 1024)


def _pack_small(parts):
    return jnp.concatenate([_rows(parts[n].astype(F32), r) for n, r in SMALL], axis=0)


def _small(packed, name, shape):
    off, r = SMALL_OFF[name]
    n = math.prod(shape)
    return packed[off:off + r].reshape(-1)[:n].reshape(shape)


class _Exchanges:
    def __init__(self, later, shards, pos):
        self.later, self.shards, self.pos, self.reduced = later, shards, pos, {}

    def gather1(self):
        return _carry_gather1(self.shards)

    def gather2(self, level1):
        return _carry_gather2(level1)

    def weights(self, full):
        return _assemble_weights(dict(zip(self.later, full)))

    def sibling(self, grads):
        self.slabs = _grad_slabs(grads)
        return _carry_sibling([self.slabs[n] for n in self.later])

    def sibling_small(self, small_g):
        self.small_g = small_g
        return _carry_sibling([], small_g)

    def chips(self, from_sib, small_sib):
        sums = [_sum_pairs(self.slabs[n], r, self.pos, name="sum_sibling_" + n) for n, r in zip(self.later, from_sib)]
        self.sums32 = [s32 for s32, _ in sums]
        return _carry_chips([s16 for _, s16 in sums], _pair_sum_small(self.small_g, small_sib, name="sum_sibling_small"))

    def chips_done(self, carried):
        *from_chips, self.table = carried
        self.reduced.update({n: (s32, r) for n, s32, r in zip(self.later, self.sums32, from_chips)})

    def w_in_chips(self, grads_in):
        slab = _grad_slabs(grads_in)["w_in"]
        (from_sib,) = _run_carry(_carry_sibling([slab]), name="exchange_sibling_w_in")
        self.s32_in, s16 = _sum_pairs(slab, from_sib, self.pos, name="sum_sibling_w_in")
        return _carry_chips([s16])

    def w_in_chips_done(self, carried):
        self.reduced["w_in"] = (self.s32_in, carried[0])


def _local_step(x, p, target, w, sm, ex=None):
    s = x.shape[0]
    mm = _matmul
    wt_main = w["wt_main"]
    conv_b = sm["conv_b"]
    bs_t = jnp.pad(sm["gmlp_b_s"].T, ((0, 0), (0, 128 - GROUPS)))
    b_f = jnp.pad(sm["b_f"], ((0, 0), (0, 128 - HEADS)))
    big = dict(tm=1024, tn=1024, tk=1024)
    whole_s = dict(tn=1024, tk=s)

    h = _rmsnorm_fwd(x, sm["norm_mix_g"], name="norm_mix")
    qkv = mm(h, wt_main, mode="nt", out_dtype=BF16, name="in_qkv", n=3072, b_off=4, **big)
    f_logit = mm(h, w["wt_f"], mode="nt", out_dtype=F32, name="in_f", tm=1024, tk=1024)
    cqe = _forget_cumsum(f_logit, b_f, name="forget_cumsum")
    qa, ka, vt = _attn_prep(qkv, cqe, name="attn_prep")
    uvg = dict(mode="nt", out_dtype=F32, name="in_uvg", n=4096, **big)
    if ex is None:
        (b, lse3), _ = _attn_fwd(qa, ka, vt, name="attn_fwd")
        zuvg = mm(h, wt_main, **uvg)
    else:
        (b, lse3), level1 = _attn_fwd(qa, ka, vt, ex.gather1(), name="attn_fwd")
        zuvg, full = mm(h, wt_main, carry=ex.gather2(level1), **uvg)
        w = {**w, **ex.weights(full)}
    a = _gmlp_fwd(zuvg, sm["gmlp_ln_g"], sm["gmlp_ln_b"], sm["gmlp_w_s"], bs_t, name="gmlp_fwd")
    wt_up, conv_w = w["wt_up"], w["conv_w"]
    ya = mm(a, w["w_a"], mode="nn", out_dtype=F32, name="branch_a", **big)
    yb = mm(b, w["w_b"], mode="nn", out_dtype=F32, name="branch_b", **big)
    merged = _merge_fwd(ya, yb, zuvg, name="merge_fwd")
    x1 = mm(merged, w["w_out"], mode="nn", out_dtype=F32, name="out_proj", add=x, **big)
    h2 = _rmsnorm_fwd(x1, sm["norm_ffn_g"], name="norm_ffn")
    up = mm(h2, wt_up, mode="nt", out_dtype=F32, name="up", tm=1024, tn=512, tk=1024)
    act = _convglu_fwd(up, conv_w, conv_b, name="convglu_fwd")
    x2 = mm(act, w["w_down"], mode="nn", out_dtype=F32, name="down", tm=1024, tn=1024, tk=1408, add=x1)
    h3 = _rmsnorm_fwd(x2, sm["norm_ple_g"], name="norm_ple")
    ple = mm(p, w["wt_ple"], mode="nt", out_dtype=F32, name="ple", tm=1024, tn=1024, tk=256)
    gp = mm(h3, w["w_pg"], mode="nn", out_dtype=F32, name="ple_gate", **big)
    x3 = _ple_fwd(x2, ple, gp, name="ple_fwd")

    loss, dx3, d_norm_final = _final_loss_bwd(x3, target, sm["norm_final_g"], name="loss_bwd")
    dple, dgp = _ple_bwd(dx3, ple, gp, name="ple_bwd")
    g_wt_ple = mm(dple, p, mode="tn", out_dtype=BF16, name="d_w_ple", tm=512, tn=256, tk=s)
    g_w_pg = mm(h3, dgp, mode="tn", out_dtype=BF16, name="d_w_pg", tm=256, **whole_s)
    dh3 = mm(dgp, w["w_pg"], mode="nt", out_dtype=F32, name="d_h3", **big)
    dx2, dx2b, d_norm_ple = _rmsnorm_bwd(dx3, dh3, x2, sm["norm_ple_g"], name="norm_ple_bwd")
    g_w_down = mm(act, dx2b, mode="tn", out_dtype=BF16, name="d_w_down", tm=256, **whole_s)
    dact = mm(dx2b, w["w_down"], mode="nt", out_dtype=BF16, name="d_act", tm=1024, tn=1408, tk=1024)
    dup_a, dup_g, dcw_a, dcw_g, dcb_a, dcb_g = _convglu_bwd(dact, up, conv_w, conv_b, name="convglu_bwd")
    g_wt_up = mm(dup_a, h2, mode="tn", out_dtype=BF16, name="d_w_up_a", tm=256, out_rows=2 * D_FF, **whole_s)
    g_wt_up = mm(dup_g, h2, mode="tn", out_dtype=BF16, name="d_w_up_g", tm=256, out_rows=2 * D_FF,
                 o_off=D_FF // 256, into=g_wt_up, **whole_s)
    dh2 = mm(dup_a, wt_up, mode="nn", out_dtype=F32, name="d_h2_a", tm=1024, tn=1024, tk=1408)
    dh2 = mm(dup_g, wt_up, mode="nn", out_dtype=F32, name="d_h2_g", tm=1024, tn=1024, tk=1408, b_off=2, add=dh2)
    dx1, dx1b, d_norm_ffn = _rmsnorm_bwd(dx2, dh2, x1, sm["norm_ffn_g"], name="norm_ffn_bwd")
    g_w_out = mm(merged, dx1b, mode="tn", out_dtype=BF16, name="d_w_out", tm=256, **whole_s)
    dmerged = mm(dx1b, w["w_out"], mode="nt", out_dtype=F32, name="d_merged", **big)
    dya, dyb, dga, dgb = _merge_bwd(dmerged, ya, yb, zuvg, name="merge_bwd")
    g_w_a = mm(a, dya, mode="tn", out_dtype=BF16, name="d_w_a", tm=256, **whole_s)
    g_w_b = mm(b, dyb, mode="tn", out_dtype=BF16, name="d_w_b", tm=256, **whole_s)
    da = mm(dya, w["w_a"], mode="nt", out_dtype=BF16, name="d_a", **big)
    db = mm(dyb, w["w_b"], mode="nt", out_dtype=BF16, name="d_b", **big)
    grads = dict(w_a=g_w_a, w_b=g_w_b, w_out=g_w_out, wt_up=g_wt_up, conv_w=jnp.concatenate([dcw_a, dcw_g], axis=1),
                 w_down=g_w_down, wt_ple=g_wt_ple, w_pg=g_w_pg)
    gmlp_args = (da, zuvg, sm["gmlp_ln_g"], sm["gmlp_ln_b"], sm["gmlp_w_s"], bs_t)
    if ex is None:
        (dzu, dzv, d_w_s, d_bs_t, d_ln_g, d_ln_b), _ = _gmlp_bwd(*gmlp_args, name="gmlp_bwd")
    else:
        (dzu, dzv, d_w_s, d_bs_t, d_ln_g, d_ln_b), from_sib = _gmlp_bwd(*gmlp_args, ex.sibling(grads), name="gmlp_bwd")
    small = dict(norm_mix_g=jnp.zeros((1, D_MODEL), F32), b_f=jnp.zeros((1, HEADS), F32), gmlp_ln_g=d_ln_g,
                 gmlp_ln_b=d_ln_b, gmlp_w_s=d_w_s, gmlp_b_s=d_bs_t[:, :GROUPS].T, norm_ffn_g=d_norm_ffn,
                 conv_b=jnp.concatenate([dcb_a, dcb_g], axis=1), norm_ple_g=d_norm_ple, norm_final_g=d_norm_final)
    if ex is None:
        delta3, _ = _attn_delta(db, b, name="attn_delta")
        (dq, dk, dv, aux, dcq3), _ = _attn_bwd(qa, ka, qkv, db, lse3, delta3, name="attn_bwd")
    else:
        delta3, (small_sib,) = _attn_delta(db, b, ex.sibling_small(_pack_small(small)), name="attn_delta")
        (dq, dk, dv, aux, dcq3), carried = _attn_bwd(qa, ka, qkv, db, lse3, delta3, ex.chips(from_sib, small_sib),
                                                     name="attn_bwd")
        ex.chips_done(carried)
    dcq16 = jnp.pad(dcq3[:, :2, :].reshape(HEADS, s).T, ((0, 0), (0, 128 - HEADS)))
    dzf, d_b_f = _forget_bwd(dcq16, aux, f_logit, b_f, name="forget_bwd")
    dz = jnp.concatenate([dzu, dzv, dga, dgb, dq, dk, dv], axis=1)
    g_wt_main = mm(dz, h, mode="tn", out_dtype=BF16, name="d_w_main", tm=512, **whole_s)
    g_wt_f = mm(dzf, h, mode="tn", out_dtype=BF16, name="d_w_f", **whole_s)
    grads = dict(grads, wt_main=g_wt_main, wt_f=g_wt_f)
    dh_main = dict(mode="nn", out_dtype=F32, name="d_h_main", **big)
    if ex is None:
        dh = mm(dz, wt_main, **dh_main)
    else:
        dh, carried = mm(dz, wt_main, carry=ex.w_in_chips(dict(wt_main=g_wt_main, wt_f=g_wt_f)), **dh_main)
        ex.w_in_chips_done(carried)
    dh = mm(dzf, w["wt_f"], mode="nn", out_dtype=F32, name="d_h_f", tm=1024, tn=1024, add=dh)
    dx0, _, d_norm_mix = _rmsnorm_bwd(dx1, dh, x, sm["norm_mix_g"], name="norm_mix_bwd")
    return loss, dx0, grads, dict(small, norm_mix_g=d_norm_mix, b_f=d_b_f[:, :HEADS])


def kernel(x, p, norm_mix_g, w_in, b_f, gmlp_ln_g, gmlp_ln_b, gmlp_w_s, gmlp_b_s, w_branch_a, w_branch_b, w_out, norm_ffn_g, w_up, conv_w, conv_b, w_down, norm_ple_g, w_ple, w_ple_gate, norm_final_g, loss_target, m_norm_mix_g, m_w_in, m_b_f, m_gmlp_ln_g, m_gmlp_ln_b, m_gmlp_w_s, m_gmlp_b_s, m_w_branch_a, m_w_branch_b, m_w_out, m_norm_ffn_g, m_w_up, m_conv_w, m_conv_b, m_w_down, m_norm_ple_g, m_w_ple, m_w_ple_gate, m_norm_final_g, v_norm_mix_g, v_w_in, v_b_f, v_gmlp_ln_g, v_gmlp_ln_b, v_gmlp_w_s, v_gmlp_b_s, v_w_branch_a, v_w_branch_b, v_w_out, v_norm_ffn_g, v_w_up, v_conv_w, v_conv_b, v_w_down, v_norm_ple_g, v_w_ple, v_w_ple_gate, v_norm_final_g):
    given = dict(locals())
    weights = {n: given[n] for n in WEIGHT_ORDER}
    mom_m = {n: given["m_" + n] for n in WEIGHT_ORDER}
    mom_v = {n: given["v_" + n] for n in WEIGHT_ORDER}
    pos = jnp.stack([lax.axis_index("x"), lax.axis_index("y"), lax.axis_index("c")]).astype(I32)
    names = [n for n, _ in SHARDED]
    kinds = dict(SHARDED)

    later = [n for n in names if n != "w_in"]

    first = _allgather([_to_comm("w_in", kinds["w_in"], weights["w_in"])], name="allgather_w_in")
    ex = _Exchanges(later, [_to_comm(n, kinds[n], weights[n]) for n in later], pos)

    sm = dict(norm_mix_g=norm_mix_g, b_f=b_f, gmlp_ln_g=gmlp_ln_g, gmlp_ln_b=gmlp_ln_b, gmlp_w_s=gmlp_w_s[0],
              gmlp_b_s=gmlp_b_s[0], norm_ffn_g=norm_ffn_g, conv_b=conv_b, norm_ple_g=norm_ple_g,
              norm_final_g=norm_final_g.reshape(1, D_MODEL))
    loss_part, dx0, grads, small = _local_step(
        x[0], p[0, 0], loss_target[0], _assemble_weights({"w_in": first[0]}), sm, ex)

    b_f_and_loss = jnp.concatenate([small["b_f"].reshape(-1), loss_part[0, :1]])
    last = _allreduce_rows(jnp.concatenate([_rows(small["norm_mix_g"], 8), _rows(b_f_and_loss, 8)], axis=0),
                           name="allreduce_last")
    loss = last[8, HEADS]
    small_last = jnp.pad(last, ((0, SMALL_ROWS - 16), (0, 0)))

    grad, delta, new_m, new_v = {}, {}, {}, {}
    for n in names:
        s32, r = ex.reduced[n]
        outs = _adam_sharded(s32, r, *[_to_comm(n, kinds[n], src[n], F32) for src in (weights, mom_m, mom_v)], pos,
                             name="adam_" + n)
        grad[n], delta[n], new_m[n], new_v[n] = [_from_comm(n, kinds[n], o) for o in outs]
    replicated = [n for n, _ in SMALL]
    rep = lambda src: _pack_small({n: src[n] for n in replicated})
    packed = _adam_replicated(ex.table, small_last, rep(weights), rep(mom_m), rep(mom_v), name="adam_replicated")
    for out, pk in zip((grad, delta, new_m, new_v), packed):
        for n in replicated:
            out[n] = _small(pk, n, weights[n].shape)

    return (loss, dx0[None], *[grad[n] for n in WEIGHT_ORDER], *[delta[n] for n in WEIGHT_ORDER],
            *[new_m[n] for n in WEIGHT_ORDER], *[new_v[n] for n in WEIGHT_ORDER])
```

```python
import functools
import math

import jax
import jax.numpy as jnp
from jax import lax
from jax.experimental import pallas as pl
from jax.experimental.pallas import tpu as pltpu

F32 = jnp.float32
BF16 = jnp.bfloat16
I32 = jnp.int32

D_MODEL = 1024
GROUPS = 8
GDIM = 128
GBLOCK = 128
CHUNK = 64
HEADS = 16
HEAD_DIM = 64
D_FF = 2816
PLE_DIM = 256
EPS = 1e-6
N_DEV = 8
ATT_SCALE = HEAD_DIM ** -0.5
NEG = -1e30

ADAM_LR = 0.001
ADAM_B1 = 0.9
ADAM_B2 = 0.999
ADAM_EPS = 1e-08
ADAM_WD = 0.01
ADAM_STEP = 10

V7X_VMEM_LIMIT = 48 * 1024 * 1024
MESH = pl.DeviceIdType.MESH

O_F = 2 * 1024 + 3 * 1024
O_G = O_F + HEADS
IN_COLS = O_G + 2 * D_MODEL
MAIN_COLS = IN_COLS - HEADS
IN_SHARD = IN_COLS // N_DEV
IN_SHARD_PAD = 912

SHARDED = (("w_in", "cols"), ("w_branch_a", "rows"), ("w_branch_b", "rows"), ("w_out", "rows"), ("w_up", "cols"),
           ("conv_w", "f32"), ("w_down", "rows"), ("w_ple", "cols"), ("w_ple_gate", "rows"))

SMALL = (("norm_mix_g", 8), ("b_f", 8), ("gmlp_ln_g", 8), ("gmlp_ln_b", 8), ("gmlp_w_s", 128), ("gmlp_b_s", 8),
         ("norm_ffn_g", 8), ("conv_b", 8), ("norm_ple_g", 8), ("norm_final_g", 8))
SMALL_OFF = {}
_o = 0
for _n, _r in SMALL:
    SMALL_OFF[_n] = (_o, _r)
    _o += _r
SMALL_ROWS = _o

WEIGHT_ORDER = ("norm_mix_g", "w_in", "b_f", "gmlp_ln_g", "gmlp_ln_b", "gmlp_w_s", "gmlp_b_s", "w_branch_a",
                "w_branch_b", "w_out", "norm_ffn_g", "w_up", "conv_w", "conv_b", "w_down", "norm_ple_g", "w_ple",
                "w_ple_gate", "norm_final_g")


def _cparams(sem):
    return pltpu.CompilerParams(dimension_semantics=sem, vmem_limit_bytes=V7X_VMEM_LIMIT)


def _gelu(x):
    c = math.sqrt(2.0 / math.pi)
    return 0.5 * x * (1.0 + jnp.tanh(c * (x + 0.044715 * x * x * x)))


def _gelu_and_grad(x):
    c = math.sqrt(2.0 / math.pi)
    t = jnp.tanh(c * (x + 0.044715 * x * x * x))
    g = 0.5 * x * (1.0 + t)
    dg = 0.5 * (1.0 + t) + 0.5 * x * (1.0 - t * t) * (c * (1.0 + 3.0 * 0.044715 * x * x))
    return g, dg


def _sigmoid(x):
    return 1.0 / (1.0 + jnp.exp(-x))


def _dot(a, b, dims):
    return lax.dot_general(a, b, (dims, ((), ())), preferred_element_type=F32)


NN = ((1,), (0,))
NT = ((1,), (1,))
TN = ((0,), (0,))


def _row_tile(rows, most):
    best = None
    for t in range(16, min(rows, most) + 1, 16):
        if rows % t == 0:
            best = t
    return best if best is not None else rows


def _matmul(a, b, *, mode, out_dtype, name, tm=512, tn=512, tk=512, add=None, n=None, b_off=0,
            out_rows=None, o_off=0, into=None, carry=None):
    if mode == "tn":
        kdim, m = a.shape
    else:
        m, kdim = a.shape
    if n is None:
        n = b.shape[0] if mode == "nt" else b.shape[1]
    tm, tn, tk = min(tm, m), min(tn, n), min(tk, kdim)
    assert m % tm == 0 and n % tn == 0 and kdim % tk == 0, (name, m, n, kdim, tm, tn, tk)
    nk = kdim // tk
    dims = {"nn": NN, "nt": NT, "tn": TN}[mode]

    def finish(r, add_ref, o_ref):
        if add_ref is not None:
            r = add_ref[...].astype(F32) + r
        o_ref[...] = r.astype(out_dtype)

    def body(*refs):
        refs = list(refs)
        a_ref, b_ref = refs[:2]
        add_ref = refs[2] if add is not None else None
        o_ref = refs[2 + (add is not None) + (into is not None)]
        part = _dot(a_ref[...].astype(BF16), b_ref[...].astype(BF16), dims)
        if nk == 1:
            finish(part, add_ref, o_ref)
            return
        acc_ref = refs[-1]
        k = pl.program_id(2)

        @pl.when(k == 0)
        def _():
            acc_ref[...] = part

        @pl.when((k > 0) & (k < nk - 1))
        def _():
            acc_ref[...] += part

        @pl.when(k == nk - 1)
        def _():
            finish(acc_ref[...] + part, add_ref, o_ref)

    a_spec = pl.BlockSpec((tk, tm), lambda i, j, k: (k, i)) if mode == "tn" else pl.BlockSpec((tm, tk), lambda i, j, k: (i, k))
    if mode == "nt":
        b_spec = pl.BlockSpec((tn, tk), lambda i, j, k: (j + b_off, k))
    else:
        b_spec = pl.BlockSpec((tk, tn), lambda i, j, k: (k + b_off, j))
    o_spec = pl.BlockSpec((tm, tn), lambda i, j, k: (i + o_off, j))
    in_specs = [a_spec, b_spec] + ([pl.BlockSpec((tm, tn), lambda i, j, k: (i, j))] if add is not None else [])
    args = (a, b) + ((add,) if add is not None else ())
    aliases = {}
    if into is not None:
        aliases = {len(args): 0}
        in_specs.append(pl.BlockSpec(memory_space=pl.ANY))
        args += (into,)
    (out,), carried = _carry_call(
        body, carry, name=name, grid=(m // tm, n // tn, nk), in_specs=in_specs, out_specs=[o_spec],
        out_shape=[jax.ShapeDtypeStruct((m if out_rows is None else out_rows, n), out_dtype)],
        scratch_shapes=[pltpu.VMEM((tm, tn), F32)] if nk > 1 else [], args=args, own_aliases=aliases)
    return out if carry is None else (out, carried)


def _row_spec(tr, width, col_block=0):
    return pl.BlockSpec((tr, width), lambda i: (i, col_block))


def _full_spec(shape):
    return pl.BlockSpec(shape, lambda i: tuple(0 for _ in shape))


def _rmsnorm_fwd(x, g, *, name, tr=256):
    s, d = x.shape

    def body(x_ref, g_ref, o_ref):
        xv = x_ref[...]
        r = lax.rsqrt(jnp.mean(xv * xv, axis=-1, keepdims=True) + EPS)
        o_ref[...] = ((xv * r) * g_ref[...]).astype(BF16)

    return pl.pallas_call(
        body, name=name, grid=(s // tr,),
        in_specs=[_row_spec(tr, d), _full_spec((1, d))], out_specs=_row_spec(tr, d),
        out_shape=jax.ShapeDtypeStruct((s, d), BF16), compiler_params=_cparams(("parallel",)),
    )(x, g)


def _rmsnorm_bwd(dres, dh, x, g, *, name, tr=256):
    s, d = x.shape

    def body(dres_ref, dh_ref, x_ref, g_ref, dx_ref, dxb_ref, dg_ref):
        i = pl.program_id(0)
        xv = x_ref[...]
        r = lax.rsqrt(jnp.mean(xv * xv, axis=-1, keepdims=True) + EPS)
        xhat = xv * r
        dhv = dh_ref[...].astype(F32)
        dxhat = dhv * g_ref[...]
        dx = dres_ref[...] + r * (dxhat - xhat * jnp.mean(dxhat * xhat, axis=-1, keepdims=True))
        dx_ref[...] = dx
        dxb_ref[...] = dx.astype(BF16)
        dgp = jnp.sum(dhv * xhat, axis=0, keepdims=True)

        @pl.when(i == 0)
        def _():
            dg_ref[...] = dgp

        @pl.when(i > 0)
        def _():
            dg_ref[...] += dgp

    return pl.pallas_call(
        body, name=name, grid=(s // tr,),
        in_specs=[_row_spec(tr, d), _row_spec(tr, d), _row_spec(tr, d), _full_spec((1, d))],
        out_specs=[_row_spec(tr, d), _row_spec(tr, d), _full_spec((1, d))],
        out_shape=[jax.ShapeDtypeStruct((s, d), F32), jax.ShapeDtypeStruct((s, d), BF16),
                   jax.ShapeDtypeStruct((1, d), F32)],
        compiler_params=_cparams(("arbitrary",)),
    )(dres, dh, x, g)


def _final_loss_bwd(x3, target, g, *, name, tr=256):
    s, d = x3.shape

    def body(x_ref, t_ref, g_ref, loss_ref, dx_ref, dg_ref):
        i = pl.program_id(0)
        xv = x_ref[...]
        r = lax.rsqrt(jnp.mean(xv * xv, axis=-1, keepdims=True) + EPS)
        xhat = xv * r
        diff = xhat * g_ref[...] - t_ref[...]
        lp = jnp.zeros((1, 128), F32) + (0.5 / d) * jnp.sum(diff * diff)
        dy = diff * (1.0 / d)
        dxhat = dy * g_ref[...]
        dx_ref[...] = r * (dxhat - xhat * jnp.mean(dxhat * xhat, axis=-1, keepdims=True))
        dgp = jnp.sum(dy * xhat, axis=0, keepdims=True)

        @pl.when(i == 0)
        def _():
            dg_ref[...] = dgp
            loss_ref[...] = lp

        @pl.when(i > 0)
        def _():
            dg_ref[...] += dgp
            loss_ref[...] += lp

    return pl.pallas_call(
        body, name=name, grid=(s // tr,),
        in_specs=[_row_spec(tr, d), _row_spec(tr, d), _full_spec((1, d))],
        out_specs=[_full_spec((1, 128)), _row_spec(tr, d), _full_spec((1, d))],
        out_shape=[jax.ShapeDtypeStruct((1, 128), F32), jax.ShapeDtypeStruct((s, d), F32),
                   jax.ShapeDtypeStruct((1, d), F32)],
        compiler_params=_cparams(("arbitrary",)),
    )(x3, target, g)


def _merge_fwd(ya, yb, zuvg, *, name, tr=256):
    s, d = ya.shape

    def body(ya_ref, yb_ref, ga_ref, gb_ref, o_ref):
        o_ref[...] = (_sigmoid(ga_ref[...]) * ya_ref[...] + _sigmoid(gb_ref[...]) * yb_ref[...]).astype(BF16)

    return pl.pallas_call(
        body, name=name, grid=(s // tr,),
        in_specs=[_row_spec(tr, d), _row_spec(tr, d), _row_spec(tr, d, 2), _row_spec(tr, d, 3)],
        out_specs=_row_spec(tr, d),
        out_shape=jax.ShapeDtypeStruct((s, d), BF16), compiler_params=_cparams(("parallel",)),
    )(ya, yb, zuvg, zuvg)


def _merge_bwd(dm, ya, yb, zuvg, *, name, tr=256):
    s, d = ya.shape

    def body(dm_ref, ya_ref, yb_ref, ga_ref, gb_ref, dya_ref, dyb_ref, dga_ref, dgb_ref):
        dmv = dm_ref[...]
        sa = _sigmoid(ga_ref[...])
        sb = _sigmoid(gb_ref[...])
        dya_ref[...] = (dmv * sa).astype(BF16)
        dyb_ref[...] = (dmv * sb).astype(BF16)
        dga_ref[...] = (dmv * ya_ref[...] * (sa * (1.0 - sa))).astype(BF16)
        dgb_ref[...] = (dmv * yb_ref[...] * (sb * (1.0 - sb))).astype(BF16)

    o = jax.ShapeDtypeStruct((s, d), BF16)
    return pl.pallas_call(
        body, name=name, grid=(s // tr,),
        in_specs=[_row_spec(tr, d)] * 3 + [_row_spec(tr, d, 2), _row_spec(tr, d, 3)], out_specs=[_row_spec(tr, d)] * 4,
        out_shape=[o, o, o, o], compiler_params=_cparams(("parallel",)),
    )(dm, ya, yb, zuvg, zuvg)


def _ple_fwd(x2, ple, gp, *, name, tr=256):
    s, d = x2.shape

    def body(x_ref, ple_ref, gp_ref, o_ref):
        o_ref[...] = x_ref[...] + ple_ref[...] * _sigmoid(gp_ref[...])

    return pl.pallas_call(
        body, name=name, grid=(s // tr,),
        in_specs=[_row_spec(tr, d)] * 3, out_specs=_row_spec(tr, d),
        out_shape=jax.ShapeDtypeStruct((s, d), F32), compiler_params=_cparams(("parallel",)),
    )(x2, ple, gp)


def _ple_bwd(dx3, ple, gp, *, name, tr=256):
    s, d = dx3.shape

    def body(dx_ref, ple_ref, gp_ref, dple_ref, dgp_ref):
        sg = _sigmoid(gp_ref[...])
        dxv = dx_ref[...]
        dple_ref[...] = (dxv * sg).astype(BF16)
        dgp_ref[...] = (dxv * ple_ref[...] * (sg * (1.0 - sg))).astype(BF16)

    o = jax.ShapeDtypeStruct((s, d), BF16)
    return pl.pallas_call(
        body, name=name, grid=(s // tr,),
        in_specs=[_row_spec(tr, d)] * 3, out_specs=[_row_spec(tr, d)] * 2,
        out_shape=[o, o], compiler_params=_cparams(("parallel",)),
    )(dx3, ple, gp)


def _masked_ws(ws_ref, g):
    row = lax.broadcasted_iota(I32, (GBLOCK, GBLOCK), 0)
    col = lax.broadcasted_iota(I32, (GBLOCK, GBLOCK), 1)
    keep = (col // CHUNK) <= (row // CHUNK)
    return jnp.where(keep, ws_ref[g], 0.0), keep


def _layernorm_parts(zv):
    mu = jnp.mean(zv, axis=-1, keepdims=True)
    xc = zv - mu
    rs = lax.rsqrt(jnp.mean(xc * xc, axis=-1, keepdims=True) + EPS)
    return xc * rs, rs


def _gmlp_fwd(zuvg, ln_g, ln_b, w_s, bs_t, *, name):
    s, w = zuvg.shape[0], GROUPS * GDIM

    def body(zu_ref, zv_ref, lng_ref, lnb_ref, ws_ref, bs_ref, a_ref):
        zu = _gelu(zu_ref[...])
        zv = _gelu(zv_ref[...])
        xhat, _ = _layernorm_parts(zv)
        vln = (xhat * lng_ref[...] + lnb_ref[...]).astype(BF16)
        for g in range(GROUPS):
            wm, _ = _masked_ws(ws_ref, g)
            mixed = _dot(wm.astype(BF16), vln[:, g * GDIM:(g + 1) * GDIM], NN) + bs_ref[:, g:g + 1]
            a_ref[:, g * GDIM:(g + 1) * GDIM] = (zu[:, g * GDIM:(g + 1) * GDIM] * mixed).astype(BF16)

    return pl.pallas_call(
        body, name=name, grid=(s // GBLOCK,),
        in_specs=[_row_spec(GBLOCK, w, 0), _row_spec(GBLOCK, w, 1), _full_spec((1, w)), _full_spec((1, w)),
                  _full_spec((GROUPS, GBLOCK, GBLOCK)), _full_spec((GBLOCK, 128))],
        out_specs=_row_spec(GBLOCK, w),
        out_shape=jax.ShapeDtypeStruct((s, w), BF16), compiler_params=_cparams(("parallel",)),
    )(zuvg, zuvg, ln_g, ln_b, w_s, bs_t)


def _gmlp_bwd(da, zuvg, ln_g, ln_b, w_s, bs_t, carry=None, *, name):
    s, w = zuvg.shape[0], GROUPS * GDIM

    def body(da_ref, zu_ref, zv_ref, lng_ref, lnb_ref, ws_ref, bs_ref,
             dzu_ref, dzv_ref, dws_ref, dbs_ref, dlng_ref, dlnb_ref, dvln_ref):
        i = pl.program_id(0)
        zu, dzu_g = _gelu_and_grad(zu_ref[...])
        zv, dzv_g = _gelu_and_grad(zv_ref[...])
        xhat, rs = _layernorm_parts(zv)
        vln = (xhat * lng_ref[...] + lnb_ref[...]).astype(BF16)
        dav = da_ref[...].astype(F32)
        lane = lax.broadcasted_iota(I32, (GBLOCK, 128), 1)
        dbs = jnp.zeros((GBLOCK, 128), F32)

        @pl.when(i == 0)
        def _():
            dws_ref[...] = jnp.zeros_like(dws_ref)

        for g in range(GROUPS):
            sl = slice(g * GDIM, (g + 1) * GDIM)
            wm, keep = _masked_ws(ws_ref, g)
            wmb = wm.astype(BF16)
            vg = vln[:, sl]
            mixed = _dot(wmb, vg, NN) + bs_ref[:, g:g + 1]
            dag = dav[:, sl]
            dzu_ref[:, sl] = (dag * mixed * dzu_g[:, sl]).astype(BF16)
            dmix = dag * zu[:, sl]
            dmb = dmix.astype(BF16)
            dws_ref[g] += jnp.where(keep, _dot(dmb, vg, NT), 0.0)
            dbs = jnp.where(lane == g, jnp.sum(dmix, axis=1, keepdims=True), dbs)
            dvln_ref[:, sl] = _dot(wmb, dmb, TN)
        dvln = dvln_ref[...]
        dxhat = dvln * lng_ref[...]
        dzv = rs * (dxhat - jnp.mean(dxhat, axis=-1, keepdims=True)
                    - xhat * jnp.mean(dxhat * xhat, axis=-1, keepdims=True))
        dzv_ref[...] = (dzv * dzv_g).astype(BF16)
        dlng = jnp.sum(dvln * xhat, axis=0, keepdims=True)
        dlnb = jnp.sum(dvln, axis=0, keepdims=True)

        @pl.when(i == 0)
        def _():
            dbs_ref[...] = dbs
            dlng_ref[...] = dlng
            dlnb_ref[...] = dlnb

        @pl.when(i > 0)
        def _():
            dbs_ref[...] += dbs
            dlng_ref[...] += dlng
            dlnb_ref[...] += dlnb

    return _carry_call(
        body, carry, name=name, grid=(s // GBLOCK,),
        in_specs=[_row_spec(GBLOCK, w), _row_spec(GBLOCK, w, 0), _row_spec(GBLOCK, w, 1), _full_spec((1, w)),
                  _full_spec((1, w)), _full_spec((GROUPS, GBLOCK, GBLOCK)), _full_spec((GBLOCK, 128))],
        out_specs=[_row_spec(GBLOCK, w), _row_spec(GBLOCK, w), _full_spec((GROUPS, GBLOCK, GBLOCK)),
                   _full_spec((GBLOCK, 128)), _full_spec((1, w)), _full_spec((1, w))],
        out_shape=[jax.ShapeDtypeStruct((s, w), BF16), jax.ShapeDtypeStruct((s, w), BF16),
                   jax.ShapeDtypeStruct((GROUPS, GBLOCK, GBLOCK), F32), jax.ShapeDtypeStruct((GBLOCK, 128), F32),
                   jax.ShapeDtypeStruct((1, w), F32), jax.ShapeDtypeStruct((1, w), F32)],
        scratch_shapes=[pltpu.VMEM((GBLOCK, w), F32)], args=[da, zuvg, zuvg, ln_g, ln_b, w_s, bs_t])


def _shift_down(u, k):
    row = lax.broadcasted_iota(I32, u.shape, 0)
    return jnp.where(row >= k, pltpu.roll(u, k, 0), 0.0)


def _shift_up(u, k):
    s = u.shape[0]
    row = lax.broadcasted_iota(I32, u.shape, 0)
    return jnp.where(row < s - k, pltpu.roll(u, s - k, 0), 0.0)


def _conv(u, w_ref, b_ref):
    return b_ref[...] + w_ref[0:1, :] * _shift_down(u, 2) + w_ref[1:2, :] * _shift_down(u, 1) + w_ref[2:3, :] * u


def _conv_specs(s, f, tc):
    nc = f // tc
    half = lambda rows: [pl.BlockSpec((rows, tc), lambda j: (0, j)), pl.BlockSpec((rows, tc), lambda j: (0, nc + j))]
    return half(s), half(3), half(1)


def _convglu_fwd(up, conv_w, conv_b, *, name, tc=256):
    s, f = up.shape[0], up.shape[1] // 2
    up_specs, w_specs, b_specs = _conv_specs(s, f, tc)

    def body(ua_ref, ug_ref, wa_ref, wg_ref, ba_ref, bg_ref, o_ref):
        ca = _conv(ua_ref[...], wa_ref, ba_ref)
        cg = _conv(ug_ref[...], wg_ref, bg_ref)
        o_ref[...] = (_gelu(ca) * cg).astype(BF16)

    return pl.pallas_call(
        body, name=name, grid=(f // tc,),
        in_specs=up_specs + w_specs + b_specs, out_specs=up_specs[0],
        out_shape=jax.ShapeDtypeStruct((s, f), BF16), compiler_params=_cparams(("parallel",)),
    )(up, up, conv_w, conv_w, conv_b, conv_b)


def _convglu_bwd(dact, up, conv_w, conv_b, *, name, tc=256):
    s, f = up.shape[0], up.shape[1] // 2
    up_specs, w_specs, b_specs = _conv_specs(s, f, tc)

    def half(dc, taps, w_ref, du_ref, dw_ref, db_ref):
        db_ref[...] = jnp.sum(dc, axis=0, keepdims=True)
        for k in range(3):
            dw_ref[k:k + 1, :] = jnp.sum(dc * taps[k], axis=0, keepdims=True)
        du = w_ref[2:3, :] * dc + w_ref[1:2, :] * _shift_up(dc, 1) + w_ref[0:1, :] * _shift_up(dc, 2)
        du_ref[...] = du.astype(BF16)

    def body(d_ref, ua_ref, ug_ref, wa_ref, wg_ref, ba_ref, bg_ref,
             dua_ref, dug_ref, dwa_ref, dwg_ref, dba_ref, dbg_ref):
        taps_a = (_shift_down(ua_ref[...], 2), _shift_down(ua_ref[...], 1), ua_ref[...])
        taps_g = (_shift_down(ug_ref[...], 2), _shift_down(ug_ref[...], 1), ug_ref[...])
        conv = lambda taps, w_ref, b_ref: b_ref[...] + w_ref[0:1, :] * taps[0] + w_ref[1:2, :] * taps[1] + w_ref[2:3, :] * taps[2]
        ca = conv(taps_a, wa_ref, ba_ref)
        cg = conv(taps_g, wg_ref, bg_ref)
        ga, dga = _gelu_and_grad(ca)
        dv = d_ref[...].astype(F32)
        half(dv * cg * dga, taps_a, wa_ref, dua_ref, dwa_ref, dba_ref)
        half(dv * ga, taps_g, wg_ref, dug_ref, dwg_ref, dbg_ref)

    col, w3, b1 = up_specs[0], w_specs[0], b_specs[0]
    return pl.pallas_call(
        body, name=name, grid=(f // tc,),
        in_specs=[col] + up_specs + w_specs + b_specs, out_specs=[col, col, w3, w3, b1, b1],
        out_shape=[jax.ShapeDtypeStruct((s, f), BF16), jax.ShapeDtypeStruct((s, f), BF16),
                   jax.ShapeDtypeStruct((3, f), F32), jax.ShapeDtypeStruct((3, f), F32),
                   jax.ShapeDtypeStruct((1, f), F32), jax.ShapeDtypeStruct((1, f), F32)],
        compiler_params=_cparams(("parallel",)),
    )(dact, up, up, conv_w, conv_w, conv_b, conv_b)


def _tri_dot(tri, x):
    b0 = x.astype(BF16)
    r1 = x - b0.astype(F32)
    b1 = r1.astype(BF16)
    b2 = (r1 - b1.astype(F32)).astype(BF16)
    return _dot(tri, b0, NN) + _dot(tri, b1, NN) + _dot(tri, b2, NN)


def _log_sigmoid(x):
    return jnp.minimum(x, 0.0) - jnp.log(1.0 + jnp.exp(-jnp.abs(x)))


def _expand_heads(col16, rows):
    src = lax.broadcasted_iota(I32, (128, HEADS * HEAD_DIM), 0)
    dst = lax.broadcasted_iota(I32, (128, HEADS * HEAD_DIM), 1) // HEAD_DIM
    spread = (src == dst).astype(BF16)
    p0, p1, p2 = _bf16_pieces(col16)
    return (_dot(p0.astype(BF16), spread, NN) + _dot(p1.astype(BF16), spread, NN)) + _dot(p2.astype(BF16), spread, NN)


def _forget_cumsum(f_logit, b_f, *, name):
    s = f_logit.shape[0]
    nb = s // 128

    def body(f_ref, b_ref, cqe_ref):
        row = lax.broadcasted_iota(I32, (128, 128), 0)
        col = lax.broadcasted_iota(I32, (128, 128), 1)
        tri = (col <= row).astype(BF16)

        def step(n, carry):
            r0 = pl.multiple_of(n * 128, 128)
            lf = _log_sigmoid(f_ref[pl.ds(r0, 128), :] + b_ref[...])
            cum = _tri_dot(tri, lf) + carry
            cqe_ref[pl.ds(r0, 128), :] = _expand_heads(cum, 128)
            return cum[127:128, :]

        lax.fori_loop(0, nb, step, jnp.zeros((1, 128), F32))

    return pl.pallas_call(
        body, name=name, grid=(1,),
        in_specs=[_full_spec((s, 128)), _full_spec((1, 128))],
        out_specs=_full_spec((s, HEADS * HEAD_DIM)),
        out_shape=jax.ShapeDtypeStruct((s, HEADS * HEAD_DIM), F32),
        compiler_params=_cparams(("arbitrary",)),
    )(f_logit, b_f)


def _forget_bwd(dcq16, sum_q16, f_logit, b_f, *, name):
    s = f_logit.shape[0]
    nb = s // 128

    def body(a_ref, k_ref, f_ref, b_ref, df_ref, db_ref):
        row = lax.broadcasted_iota(I32, (128, 128), 0)
        col = lax.broadcasted_iota(I32, (128, 128), 1)
        tri_rev = (col >= row).astype(BF16)

        def step(m, carry):
            suffix, dbsum = carry
            n = nb - 1 - m
            r0 = pl.multiple_of(n * 128, 128)
            dcum = a_ref[pl.ds(r0, 128), :] - k_ref[pl.ds(r0, 128), :]
            dlf = _tri_dot(tri_rev, dcum) + suffix
            df = dlf * _sigmoid(-(f_ref[pl.ds(r0, 128), :] + b_ref[...]))
            df_ref[pl.ds(r0, 128), :] = df.astype(BF16)
            return dlf[0:1, :], dbsum + jnp.sum(df, axis=0, keepdims=True)

        _, dbsum = lax.fori_loop(0, nb, step, (jnp.zeros((1, 128), F32), jnp.zeros((1, 128), F32)))
        db_ref[...] = dbsum

    return pl.pallas_call(
        body, name=name, grid=(1,),
        in_specs=[_full_spec((s, 128))] * 3 + [_full_spec((1, 128))],
        out_specs=[_full_spec((s, 128)), _full_spec((1, 128))],
        out_shape=[jax.ShapeDtypeStruct((s, 128), BF16), jax.ShapeDtypeStruct((1, 128), F32)],
        compiler_params=_cparams(("arbitrary",)),
    )(dcq16, sum_q16, f_logit, b_f)


ATT_T = 256


def _head_lanes(rows):
    return lax.broadcasted_iota(I32, (rows, 128), 1) < HEAD_DIM


def _bf16_pieces(c):
    p0 = c.astype(BF16).astype(F32)
    r = c - p0
    p1 = r.astype(BF16).astype(F32)
    p2 = (r - p1).astype(BF16).astype(F32)
    return p0, p1, p2


def _col_reduce(x, op):
    rows = x.shape[0]
    while rows > 8:
        rows //= 2
        x = op(x[:rows], x[rows:])
    return jnp.max(x, axis=0, keepdims=True) if op is jnp.maximum else jnp.sum(x, axis=0, keepdims=True)


def _attn_prep(qkv, cqe, *, name):
    s = qkv.shape[0]
    npair = HEADS // 2

    def body(q_ref, k_ref, v_ref, c_ref, qa_ref, ka_ref, vt_ref):
        rows = 128
        lane = lax.broadcasted_iota(I32, (rows, 128), 1)

        def chunk(n, _):
            r0 = pl.multiple_of(n * rows, rows)
            sl = pl.ds(r0, rows)
            qv = q_ref[sl, :].astype(F32) * ATT_SCALE
            kv = k_ref[sl, :].astype(F32)
            p0, p1, p2 = _bf16_pieces(pltpu.roll(c_ref[sl, :], HEAD_DIM, 1))
            for e in range(2):
                mine = (lane < HEAD_DIM) if e == 0 else (lane >= HEAD_DIM)
                base = HEAD_DIM * (1 - e)
                ones_hi = jnp.where((lane >= base + 3) & (lane < base + 6), 1.0, 0.0)
                ones_lo = jnp.where((lane >= base) & (lane < base + 3), 1.0, 0.0)
                qa = jnp.where(mine, qv, jnp.where(lane == base, p0, jnp.where(lane == base + 1, p1,
                               jnp.where(lane == base + 2, p2, ones_hi))))
                ka = jnp.where(mine, kv, jnp.where(lane == base + 3, -p0, jnp.where(lane == base + 4, -p1,
                               jnp.where(lane == base + 5, -p2, ones_lo))))
                qa_ref[e, sl, :] = qa.astype(BF16)
                ka_ref[e, sl, :] = ka.astype(BF16)
            vt_ref[0, :, sl] = v_ref[sl, :].astype(F32).T.astype(BF16)
            return 0

        lax.fori_loop(0, s // rows, chunk, 0)

    pair = pl.BlockSpec((2, s, 128), lambda hp: (hp, 0, 0))
    return pl.pallas_call(
        body, name=name, grid=(npair,),
        in_specs=[pl.BlockSpec((s, 128), lambda hp: (0, hp)), pl.BlockSpec((s, 128), lambda hp: (0, npair + hp)),
                  pl.BlockSpec((s, 128), lambda hp: (0, 2 * npair + hp)), pl.BlockSpec((s, 128), lambda hp: (0, hp))],
        out_specs=[pair, pair, pl.BlockSpec((1, 128, s), lambda hp: (hp, 0, 0))],
        out_shape=[jax.ShapeDtypeStruct((HEADS, s, 128), BF16), jax.ShapeDtypeStruct((HEADS, s, 128), BF16),
                   jax.ShapeDtypeStruct((npair, 128, s), BF16)],
        compiler_params=_cparams(("parallel",)),
    )(qkv, qkv, qkv, cqe)


def _attn_fwd(qa, ka, vt, carry=None, *, name):
    s = qa.shape[1]
    t = 2 * ATT_T
    nq = s // t
    npair = HEADS // 2

    def body(qa_ref, ka_ref, vt_ref, o_ref, lse_ref):
        i = pl.program_id(1)
        krow = lax.broadcasted_iota(I32, (t, t), 0)
        qcol = lax.broadcasted_iota(I32, (t, t), 1)
        sub = lax.broadcasted_iota(I32, (128, t), 0)
        row8 = lax.broadcasted_iota(I32, (8, t), 0)
        qbs = (qa_ref[0], qa_ref[1])
        tk = t

        def step(j, carry, diag):
            c0 = pl.multiple_of(j * tk, tk)
            vtb = vt_ref[0, :, pl.ds(c0, tk)]
            sts = [_dot(ka_ref[e, pl.ds(c0, tk), :], qbs[e], NT) for e in range(2)]
            if diag:
                sts = [jnp.where(krow <= qcol, st, NEG) for st in sts]
            pts, stats = [], []
            for e in range(2):
                m, l, _ = carry[e]
                m_new = jnp.maximum(m, _col_reduce(sts[e], jnp.maximum))
                alpha = jnp.exp(m - m_new)
                pt = jnp.exp(sts[e] - m_new)
                stats.append((m_new, alpha, alpha * l + _col_reduce(pt, jnp.add)))
                pts.append(pt.astype(BF16))
            pvs = [_dot(vtb, pts[e], NN) for e in range(2)]
            return tuple((stats[e][0], stats[e][2], stats[e][1] * carry[e][2] + pvs[e]) for e in range(2))

        init = (jnp.full((1, t), NEG, F32), jnp.zeros((1, t), F32), jnp.zeros((128, t), F32))
        carry = lax.fori_loop(0, i, functools.partial(step, diag=False), (init, init))
        (m0, l0, acc0), (m1, l1, acc1) = step(i, carry, True)
        o_pair = jnp.where(sub < HEAD_DIM, acc0 / l0, acc1 / l1)
        o_ref[...] = o_pair.T.astype(BF16)
        lse_ref[0] = jnp.where(row8 == 0, m0 + jnp.log(l0), jnp.where(row8 == 1, m1 + jnp.log(l1), 0.0))

    return _carry_call(
        body, carry, name=name, grid=(npair, nq),
        in_specs=[pl.BlockSpec((2, t, 128), lambda hp, i: (hp, i, 0)), pl.BlockSpec((2, s, 128), lambda hp, i: (hp, 0, 0)),
                  pl.BlockSpec((1, 128, s), lambda hp, i: (hp, 0, 0))],
        out_specs=[pl.BlockSpec((t, 128), lambda hp, i: (i, hp)), pl.BlockSpec((1, 8, t), lambda hp, i: (hp, 0, i))],
        out_shape=[jax.ShapeDtypeStruct((s, HEADS * HEAD_DIM), BF16), jax.ShapeDtypeStruct((npair, 8, s), F32)],
        scratch_shapes=[], args=[qa, ka, vt])


def _attn_delta(do, o, carry=None, *, name):
    s = do.shape[0]

    def body(do_ref, o_ref, d_ref):
        prod = do_ref[...].astype(F32) * o_ref[...].astype(F32)
        row = lax.broadcasted_iota(I32, (8, 128), 0)
        lane = lax.broadcasted_iota(I32, (8, 128), 1)
        sel = ((row == 0) & (lane < HEAD_DIM) | (row == 1) & (lane >= HEAD_DIM)).astype(BF16)
        p0, p1, p2 = _bf16_pieces(prod)
        d_ref[0] = (_dot(sel, p0.astype(BF16), NT) + _dot(sel, p1.astype(BF16), NT)) + _dot(sel, p2.astype(BF16), NT)

    pair = pl.BlockSpec((s, 128), lambda hp: (0, hp))
    (delta3,), carried = _carry_call(
        body, carry, name=name, grid=(HEADS // 2,), in_specs=[pair, pair],
        out_specs=[pl.BlockSpec((1, 8, s), lambda hp: (hp, 0, 0))],
        out_shape=[jax.ShapeDtypeStruct((HEADS // 2, 8, s), F32)], scratch_shapes=[], args=[do, o])
    return delta3, carried


def _attn_bwd(qa, ka, qkv, do, lse3, delta3, carry=None, *, name):
    s = qa.shape[1]
    t = 2 * ATT_T
    nb = s // t
    npair = HEADS // 2

    def body(qa_ref, ka_ref, v_ref, do_ref, lse_ref, delta_ref, dq_ref, dk_ref, dv_ref, aux_ref, dcq_ref, dqt):
        hp = pl.program_id(0)
        first = _head_lanes(t)
        lane = lax.broadcasted_iota(I32, (t, 128), 1)
        dqt[...] = jnp.zeros_like(dqt)

        @pl.when(hp == 0)
        def _():
            aux_ref[...] = jnp.zeros_like(aux_ref)

        krow = lax.broadcasted_iota(I32, (t, t), 0)
        qcol = lax.broadcasted_iota(I32, (t, t), 1)

        def key_block(j, _):
            c0 = pl.multiple_of(j * t, t)
            vb = v_ref[pl.ds(c0, t), :]
            kbs = (ka_ref[0, pl.ds(c0, t), :], ka_ref[1, pl.ds(c0, t), :])
            kbts = tuple(kb.astype(F32).T.astype(BF16) for kb in kbs)
            vhs = (jnp.where(first, vb, jnp.zeros_like(vb)), jnp.where(first, jnp.zeros_like(vb), vb))

            def query_block(i, carry, diag):
                r0 = pl.multiple_of(i * t, t)
                dob = do_ref[pl.ds(r0, t), :]
                sts = [_dot(kbs[e], qa_ref[e, pl.ds(r0, t), :], NT) for e in range(2)]
                dpts = [_dot(vhs[e], dob, NT) for e in range(2)]
                ptbs, dsbs = [], []
                for e in range(2):
                    st = jnp.where(krow <= qcol, sts[e], NEG) if diag else sts[e]
                    pt = jnp.exp(st - lse_ref[0, e:e + 1, pl.ds(r0, t)])
                    dsbs.append((pt * (dpts[e] - delta_ref[0, e:e + 1, pl.ds(r0, t)])).astype(BF16))
                    ptbs.append(pt.astype(BF16))
                out = []
                for e in range(2):
                    dk_a, dv_a = carry[e]
                    dv_a = dv_a + _dot(ptbs[e], dob, NN)
                    dk_a = dk_a + _dot(dsbs[e], qa_ref[e, pl.ds(r0, t), :], NN)
                    dqt[e, :, pl.ds(r0, t)] += _dot(kbts[e], dsbs[e], NN)
                    out.append((dk_a, dv_a))
                return tuple(out)

            zero = jnp.zeros((t, 128), F32)
            carry = query_block(j, ((zero, zero), (zero, zero)), True)
            (dk0, dv0), (dk1, dv1) = lax.fori_loop(j + 1, nb, functools.partial(query_block, diag=False), carry)
            dk_ref[pl.ds(c0, t), :] = jnp.where(first, dk0, dk1).astype(BF16)
            dv_ref[pl.ds(c0, t), :] = jnp.where(first, dv0, dv1).astype(BF16)
            sum_q = jnp.where(lane == 2 * hp, dk0[:, HEAD_DIM + 3:HEAD_DIM + 4],
                              jnp.where(lane == 2 * hp + 1, dk1[:, 3:4], aux_ref[pl.ds(c0, t), :]))
            aux_ref[pl.ds(c0, t), :] = sum_q
            return 0

        lax.fori_loop(0, nb, key_block, 0)
        sub = lax.broadcasted_iota(I32, (128, s), 0)
        row8 = lax.broadcasted_iota(I32, (8, s), 0)
        dq_ref[...] = (jnp.where(sub < HEAD_DIM, dqt[0], dqt[1]) * ATT_SCALE).T.astype(BF16)
        dcq_ref[0] = jnp.where(row8 == 0, dqt[0, HEAD_DIM:HEAD_DIM + 1, :], jnp.where(row8 == 1, dqt[1, 0:1, :], 0.0))

    def pair_cols(off):
        return pl.BlockSpec((s, 128), lambda hp: (0, off + hp))

    heads = pl.BlockSpec((2, s, 128), lambda hp: (hp, 0, 0))
    rows = pl.BlockSpec((1, 8, s), lambda hp: (hp, 0, 0))
    wide = jax.ShapeDtypeStruct((s, HEADS * HEAD_DIM), BF16)
    return _carry_call(
        body, carry, name=name, grid=(npair,),
        in_specs=[heads, heads, pair_cols(2 * npair), pair_cols(0), rows, rows],
        out_specs=[pair_cols(0), pair_cols(0), pair_cols(0), pl.BlockSpec((s, 128), lambda hp: (0, 0)), rows],
        out_shape=[wide, wide, wide, jax.ShapeDtypeStruct((s, 128), F32), jax.ShapeDtypeStruct((npair, 8, s), F32)],
        scratch_shapes=[pltpu.VMEM((2, 128, s), F32)], args=[qa, ka, qkv, do, lse3, delta3])


def _adam_math(w, g, m, v):
    m = ADAM_B1 * m + (1.0 - ADAM_B1) * g
    v = ADAM_B2 * v + (1.0 - ADAM_B2) * (g * g)
    m_hat = m / (1.0 - ADAM_B1 ** ADAM_STEP)
    v_hat = v / (1.0 - ADAM_B2 ** ADAM_STEP)
    delta = -ADAM_LR * (m_hat / (jnp.sqrt(v_hat) + ADAM_EPS) + ADAM_WD * w)
    return delta, m, v


def _sum_pairs(keep, recv, pos, *, name):
    _, r, c = recv.shape
    tr = _row_tile(r, 512)

    def body(pos_ref, a_ref, b_ref, o32_ref, o16_ref):
        tot = a_ref[...].astype(F32) + b_ref[...].astype(F32)
        o16_ref[...] = tot.astype(BF16)

        @pl.when(pl.program_id(1) == 2 * pos_ref[0] + pos_ref[1])
        def _():
            o32_ref[...] = tot

    out = pl.BlockSpec((1, tr, c), lambda i, q, pos: (q, i, 0))
    grid_spec = pltpu.PrefetchScalarGridSpec(
        num_scalar_prefetch=1, grid=(r // tr, 4),
        in_specs=[pl.BlockSpec((1, tr, c), lambda i, q, pos: (2 * q + pos[2], i, 0)), out],
        out_specs=[pl.BlockSpec((1, tr, c), lambda i, q, pos: (0, i, 0)), out])
    return pl.pallas_call(
        body, name=name, grid_spec=grid_spec,
        out_shape=[jax.ShapeDtypeStruct((1, r, c), F32), jax.ShapeDtypeStruct((4, r, c), BF16)],
        compiler_params=_cparams(("arbitrary", "arbitrary")),
    )(pos, keep, recv)


def _adam_sharded(psum, recv, w, m, v, pos, *, name):
    r, c = w.shape
    tr = _row_tile(r, 320)

    def body(pos_ref, p_ref, r_ref, w_ref, m_ref, v_ref, g_ref, d_ref, mo_ref, vo_ref):
        g = p_ref[0] + r_ref[0].astype(F32) + r_ref[1].astype(F32) + r_ref[2].astype(F32)
        delta, mn, vn = _adam_math(w_ref[...], g, m_ref[...], v_ref[...])
        g_ref[...] = g
        d_ref[...] = delta
        mo_ref[...] = mn
        vo_ref[...] = vn

    row = pl.BlockSpec((tr, c), lambda i, pos: (i, 0))
    grid_spec = pltpu.PrefetchScalarGridSpec(
        num_scalar_prefetch=1, grid=(r // tr,),
        in_specs=[pl.BlockSpec((1, tr, c), lambda i, pos: (0, i, 0)),
                  pl.BlockSpec((3, tr, c), lambda i, pos: (0, i, 0)), row, row, row],
        out_specs=[row, row, row, row])
    o = jax.ShapeDtypeStruct((r, c), F32)
    return pl.pallas_call(
        body, name=name, grid_spec=grid_spec, out_shape=[o, o, o, o],
        compiler_params=_cparams(("parallel",)),
    )(pos, psum, recv, w, m, v)


def _adam_replicated(chip_sums, last, w, m, v, *, name):
    r = w.shape[0]

    def body(s_ref, l_ref, w_ref, m_ref, v_ref, g_ref, d_ref, mo_ref, vo_ref):
        g = (((s_ref[0] + s_ref[1]) + s_ref[2]) + s_ref[3]) + l_ref[...]
        delta, mn, vn = _adam_math(w_ref[...], g, m_ref[...], v_ref[...])
        g_ref[...] = g
        d_ref[...] = delta
        mo_ref[...] = mn
        vo_ref[...] = vn

    o = jax.ShapeDtypeStruct((r, 1024), F32)
    full = _full_spec((r, 1024))
    return pl.pallas_call(
        body, name=name, grid=(1,),
        in_specs=[_full_spec((4, r, 1024)), full, full, full, full], out_specs=[full] * 4, out_shape=[o] * 4,
        compiler_params=_cparams(("arbitrary",)),
    )(chip_sums, last, w, m, v)


ASM_OUT = 256
ASM_SRC = 304


def _w_in_row(r):
    return r if r < 2048 else (r + O_G - 2048 if r < 4096 else r - 2048)


def _assemble_wt_main(g, *, name):
    table = []
    for blk in range(MAIN_COLS // ASM_OUT):
        j, l0 = divmod(_w_in_row(blk * ASM_OUT), IN_SHARD)
        sb = l0 // ASM_SRC
        n_a = min(ASM_OUT, min(IN_SHARD, (sb + 1) * ASM_SRC) - l0)
        if n_a == ASM_OUT:
            nxt = (j, sb)
        elif l0 + n_a == IN_SHARD:
            nxt = (j + 1, 0)
        else:
            nxt = (j, sb + 1)
        table.append((j, sb, l0 - sb * ASM_SRC, n_a) + nxt)

    def body(tab_ref, a_ref, b_ref, o_ref):
        blk = pl.program_id(0)
        off, n_a = tab_ref[blk, 2], tab_ref[blk, 3]
        r = lax.broadcasted_iota(I32, (ASM_OUT, ASM_SRC), 0)
        k = lax.broadcasted_iota(I32, (ASM_OUT, ASM_SRC), 1)
        sel_a = ((k == r + off) & (r < n_a)).astype(BF16)
        sel_b = ((k == r - n_a) & (r >= n_a)).astype(BF16)
        o_ref[...] = (_dot(sel_a, a_ref[0], NN) + _dot(sel_b, b_ref[0], NN)).astype(BF16)

    src = lambda c: pl.BlockSpec((1, ASM_SRC, D_MODEL), lambda blk, tab: (tab[blk, c], tab[blk, c + 1], 0))
    grid_spec = pltpu.PrefetchScalarGridSpec(
        num_scalar_prefetch=1, grid=(len(table),), in_specs=[src(0), src(4)],
        out_specs=pl.BlockSpec((ASM_OUT, D_MODEL), lambda blk, tab: (blk, 0)))
    return pl.pallas_call(
        body, name=name, grid_spec=grid_spec, out_shape=jax.ShapeDtypeStruct((MAIN_COLS, D_MODEL), BF16),
        compiler_params=_cparams(("parallel",)),
    )(jnp.asarray(table, I32), g, g)


def _pair_sum_small(mine, theirs, *, name):
    def body(a_ref, b_ref, o_ref):
        o_ref[...] = a_ref[...] + b_ref[...]

    full = _full_spec(mine.shape)
    return pl.pallas_call(
        body, name=name, grid=(1,), in_specs=[full, full], out_specs=full,
        out_shape=jax.ShapeDtypeStruct(mine.shape, F32), compiler_params=_cparams(("arbitrary",)),
    )(mine, theirs)


ANY = pl.BlockSpec(memory_space=pl.ANY)
OTHER_CHIPS = ((1, 0), (0, 1), (1, 1))


class _Carry:
    def __init__(self, inputs, out_shapes, scratch, start, wait, aliases=None):
        self.inputs, self.out_shapes, self.scratch = list(inputs), list(out_shapes), list(scratch)
        self.start, self.wait, self.aliases = start, wait, dict(aliases or {})


def _carried(body, carry, n_in, n_out, grid):
    if carry is None:
        return body
    ci, co, cs = len(carry.inputs), len(carry.out_shapes), len(carry.scratch)

    def wrapped(*refs):
        ins, cins = refs[:n_in], refs[n_in:n_in + ci]
        outs, couts = refs[n_in + ci:n_in + ci + n_out], refs[n_in + ci + n_out:n_in + ci + n_out + co]
        rest = refs[n_in + ci + n_out + co:]
        scratch, cscr = rest[:len(rest) - cs], rest[len(rest) - cs:]
        first, last = None, None
        for axis, size in enumerate(grid):
            f, l = pl.program_id(axis) == 0, pl.program_id(axis) == size - 1
            first = f if first is None else first & f
            last = l if last is None else last & l

        @pl.when(first)
        def _():
            carry.start(cins, couts, cscr)

        body(*ins, *outs, *scratch)

        @pl.when(last)
        def _():
            carry.wait(cins, couts, cscr)

    return wrapped


def _carry_call(body, carry, *, name, grid, in_specs, out_specs, out_shape, scratch_shapes, args, vmem=True,
                own_aliases=None):
    n_in, n_out = len(in_specs), len(out_specs)
    extra_in = [ANY] * len(carry.inputs) if carry else []
    extra_out = [ANY] * len(carry.out_shapes) if carry else []
    aliases = dict(own_aliases or {})
    if carry:
        aliases.update({n_in + i: n_out + o for i, o in carry.aliases.items()})
    out = pl.pallas_call(
        _carried(body, carry, n_in, n_out, grid), name=name, grid=grid,
        in_specs=list(in_specs) + extra_in, out_specs=list(out_specs) + extra_out,
        out_shape=list(out_shape) + (carry.out_shapes if carry else []),
        scratch_shapes=list(scratch_shapes) + (carry.scratch if carry else []),
        input_output_aliases=aliases,
        compiler_params=_cparams(("arbitrary",) * len(grid)) if vmem else None,
    )(*args, *(carry.inputs if carry else []))
    return list(out[:n_out]), list(out[n_out:])


def _run_carry(carry, *, name):
    return _carry_call(lambda: None, carry, name=name, grid=(1,), in_specs=[], out_specs=[], out_shape=[],
                       scratch_shapes=[], args=[], vmem=False)[1]


def _sems(n):
    return [pltpu.SemaphoreType.DMA((n,)), pltpu.SemaphoreType.DMA((n,))]


def _carry_gather1(shards):
    n = len(shards)

    def copies(x_refs, out_refs, scr, with_arrivals):
        send_sems, recv_sems, local_sems = scr
        x, y, c = lax.axis_index("x"), lax.axis_index("y"), lax.axis_index("c")
        peers = [(x, y, 1 - c)] + [(x ^ fx, y ^ fy, c) for fx, fy in OTHER_CHIPS]
        local, sends, arrivals = [], [], []
        for t, (x_ref, out_ref) in enumerate(zip(x_refs, out_refs)):
            local.append(pltpu.make_async_copy(x_ref, out_ref.at[4 * x + 2 * y + c], local_sems.at[t]))
            for k, (px, py, pc) in enumerate(peers):
                sems = dict(send_sem=send_sems.at[4 * t + k], recv_sem=recv_sems.at[4 * t + k],
                            device_id=(px, py, pc), device_id_type=MESH)
                sends.append(pltpu.make_async_remote_copy(src_ref=x_ref, dst_ref=out_ref.at[4 * x + 2 * y + c], **sems))
                if with_arrivals:
                    arrivals.append(
                        pltpu.make_async_remote_copy(src_ref=x_ref, dst_ref=out_ref.at[4 * px + 2 * py + pc], **sems))
        return local, sends, arrivals

    def start(x_refs, out_refs, scr):
        local, sends, _ = copies(x_refs, out_refs, scr, False)
        for cp in local + sends:
            cp.start()

    def wait(x_refs, out_refs, scr):
        local, sends, arrivals = copies(x_refs, out_refs, scr, True)
        for cp in arrivals:
            cp.wait_recv()
        for cp in sends:
            cp.wait_send()
        for cp in local:
            cp.wait()

    return _Carry(shards, [jax.ShapeDtypeStruct((N_DEV,) + a.shape, a.dtype) for a in shards],
                  _sems(4 * n) + [pltpu.SemaphoreType.DMA((n,))], start, wait)


def _carry_gather2(gathered):
    n = len(gathered)

    def copies(in_refs, g_refs, scr, with_arrivals):
        send_sems, recv_sems = scr
        x, y, c = lax.axis_index("x"), lax.axis_index("y"), lax.axis_index("c")
        sends, arrivals = [], []
        for t in range(n):
            for j, (fx, fy) in enumerate(OTHER_CHIPS):
                px, py = x ^ fx, y ^ fy
                sems = dict(send_sem=send_sems.at[3 * t + j], recv_sem=recv_sems.at[3 * t + j],
                            device_id=(x, y, 1 - c), device_id_type=MESH)
                mine, theirs = 4 * px + 2 * py + c, 4 * px + 2 * py + (1 - c)
                sends.append(pltpu.make_async_remote_copy(src_ref=in_refs[t].at[mine], dst_ref=g_refs[t].at[mine], **sems))
                if with_arrivals:
                    arrivals.append(pltpu.make_async_remote_copy(
                        src_ref=in_refs[t].at[mine], dst_ref=g_refs[t].at[theirs], **sems))
        return sends, arrivals

    def start(in_refs, g_refs, scr):
        for cp in copies(in_refs, g_refs, scr, False)[0]:
            cp.start()

    def wait(in_refs, g_refs, scr):
        sends, arrivals = copies(in_refs, g_refs, scr, True)
        for cp in arrivals:
            cp.wait_recv()
        for cp in sends:
            cp.wait_send()

    return _Carry(gathered, [jax.ShapeDtypeStruct(a.shape, a.dtype) for a in gathered], _sems(3 * n), start, wait,
                  aliases={t: t for t in range(n)})


def _allreduce_rows(x, *, name):
    def body(x_ref, o_ref, sib_ref, mine_ref, tab_ref, send_sems, recv_sems):
        x, y, c = lax.axis_index("x"), lax.axis_index("y"), lax.axis_index("c")
        swap = pltpu.make_async_remote_copy(src_ref=x_ref, dst_ref=sib_ref, send_sem=send_sems.at[0],
                                            recv_sem=recv_sems.at[0], device_id=(x, y, 1 - c), device_id_type=MESH)
        swap.start()
        swap.wait()
        mine_ref[...] = x_ref[...] + sib_ref[...]
        tab_ref[pl.ds(2 * x + y, 1)] = mine_ref[...][None]

        def copy(k, slot):
            fx, fy = OTHER_CHIPS[k]
            return pltpu.make_async_remote_copy(
                src_ref=mine_ref, dst_ref=tab_ref.at[slot], send_sem=send_sems.at[1 + k], recv_sem=recv_sems.at[1 + k],
                device_id=(x ^ fx, y ^ fy, c), device_id_type=MESH)

        for k in range(3):
            copy(k, 2 * x + y).start()
        for k, (fx, fy) in enumerate(OTHER_CHIPS):
            copy(k, 2 * (x ^ fx) + (y ^ fy)).wait()
        o_ref[...] = ((tab_ref[0] + tab_ref[1]) + tab_ref[2]) + tab_ref[3]

    vmem = pl.BlockSpec(memory_space=pltpu.VMEM)
    return pl.pallas_call(
        body, name=name, out_shape=jax.ShapeDtypeStruct(x.shape, F32), in_specs=[vmem], out_specs=vmem,
        scratch_shapes=[pltpu.VMEM(x.shape, F32), pltpu.VMEM(x.shape, F32), pltpu.VMEM((4,) + x.shape, F32)] + _sems(4),
    )(x)


def _allgather(shards, *, name):
    n = len(shards)

    def body(*refs):
        x_refs, out_refs = refs[:n], refs[n:2 * n]
        send_sems, recv_sems, local_sems = refs[2 * n:]
        x, y, c = lax.axis_index("x"), lax.axis_index("y"), lax.axis_index("c")
        me, sibling = (x, y, c), (x, y, 1 - c)
        chips = [(x ^ fx, y ^ fy) for fx, fy in OTHER_CHIPS]

        def copy(t, k, block, to, from_input=False):
            px, py, pc = block
            slab = out_refs[t].at[4 * px + 2 * py + pc]
            return pltpu.make_async_remote_copy(
                src_ref=x_refs[t] if from_input else slab, dst_ref=slab,
                send_sem=send_sems.at[7 * t + k], recv_sem=recv_sems.at[7 * t + k], device_id=to, device_id_type=MESH)

        mine = [pltpu.make_async_copy(x_refs[t], out_refs[t].at[4 * x + 2 * y + c], local_sems.at[t]) for t in range(n)]
        for cp in mine:
            cp.start()
        first = []
        for t in range(n):
            first.append(copy(t, 0, me, sibling, from_input=True))
            first += [copy(t, 1 + j, me, (*chip, c), from_input=True) for j, chip in enumerate(chips)]
        for cp in first:
            cp.start()
        passed = []
        for j, chip in enumerate(chips):
            for t in range(n):
                copy(t, 1 + j, (*chip, c), me).wait_recv()
                fwd = copy(t, 4 + j, (*chip, c), sibling)
                fwd.start()
                passed.append(fwd)
        for t in range(n):
            copy(t, 0, sibling, me).wait_recv()
            for j, chip in enumerate(chips):
                copy(t, 4 + j, (*chip, 1 - c), me).wait_recv()
        for cp in first + passed:
            cp.wait_send()
        for cp in mine:
            cp.wait()

    return pl.pallas_call(
        body, name=name, out_shape=[jax.ShapeDtypeStruct((N_DEV,) + a.shape, a.dtype) for a in shards],
        in_specs=[ANY] * n, out_specs=[ANY] * n,
        scratch_shapes=[pltpu.SemaphoreType.DMA((7 * n,)), pltpu.SemaphoreType.DMA((7 * n,)),
                        pltpu.SemaphoreType.DMA((n,))],
    )(*shards)


def _carry_sibling(slabs, small=None):
    n = len(slabs)
    extra = [] if small is None else [small]

    def copies(in_refs, out_refs, scr):
        send_sems, recv_sems = scr
        x, y, c = lax.axis_index("x"), lax.axis_index("y"), lax.axis_index("c")
        sibling = (x, y, 1 - c)
        out = []
        for t in range(n):
            for q in range(4):
                out.append(pltpu.make_async_remote_copy(
                    src_ref=in_refs[t].at[2 * q + (1 - c)], dst_ref=out_refs[t].at[q],
                    send_sem=send_sems.at[4 * t + q], recv_sem=recv_sems.at[4 * t + q],
                    device_id=sibling, device_id_type=MESH))
        if extra:
            out.append(pltpu.make_async_remote_copy(
                src_ref=in_refs[n], dst_ref=out_refs[n], send_sem=send_sems.at[4 * n], recv_sem=recv_sems.at[4 * n],
                device_id=sibling, device_id_type=MESH))
        return out

    def start(*refs):
        for cp in copies(*refs):
            cp.start()

    def wait(*refs):
        for cp in copies(*refs):
            cp.wait()

    return _Carry(list(slabs) + extra,
                  [jax.ShapeDtypeStruct((4,) + a.shape[1:], a.dtype) for a in slabs]
                  + [jax.ShapeDtypeStruct(a.shape, a.dtype) for a in extra], _sems(4 * n + 1), start, wait)


def _carry_chips(psums, small_sum=None):
    n = len(psums)
    table = small_sum is not None

    def copies(in_refs, out_refs, scr, arrivals):
        send_sems, recv_sems = scr[0], scr[1]
        x, y, c = lax.axis_index("x"), lax.axis_index("y"), lax.axis_index("c")
        out = []
        for k, (fx, fy) in enumerate(OTHER_CHIPS):
            px, py = x ^ fx, y ^ fy
            for t in range(n):
                out.append(pltpu.make_async_remote_copy(
                    src_ref=in_refs[t].at[2 * px + py], dst_ref=out_refs[t].at[k],
                    send_sem=send_sems.at[3 * t + k], recv_sem=recv_sems.at[3 * t + k],
                    device_id=(px, py, c), device_id_type=MESH))
            if table:
                slot = 2 * px + py if arrivals else 2 * x + y
                out.append(pltpu.make_async_remote_copy(
                    src_ref=in_refs[n], dst_ref=out_refs[n].at[slot], send_sem=send_sems.at[3 * n + k],
                    recv_sem=recv_sems.at[3 * n + k], device_id=(px, py, c), device_id_type=MESH))
        return out

    def own(in_refs, out_refs, scr):
        x, y = lax.axis_index("x"), lax.axis_index("y")
        return pltpu.make_async_copy(in_refs[n], out_refs[n].at[2 * x + y], scr[2])

    def start(in_refs, out_refs, scr):
        if table:
            own(in_refs, out_refs, scr).start()
        for cp in copies(in_refs, out_refs, scr, False):
            cp.start()

    def wait(in_refs, out_refs, scr):
        for cp in copies(in_refs, out_refs, scr, True):
            cp.wait()
        if table:
            own(in_refs, out_refs, scr).wait()

    out_shapes = [jax.ShapeDtypeStruct((3,) + a.shape[1:], a.dtype) for a in psums]
    if table:
        out_shapes.append(jax.ShapeDtypeStruct((4,) + small_sum.shape, F32))
    return _Carry(list(psums) + ([small_sum] if table else []), out_shapes,
                  _sems(3 * n + 3) + ([pltpu.SemaphoreType.DMA] if table else []), start, wait)


def _to_comm(name, kind, block, dtype=BF16):
    a = block[0]
    if kind == "cols":
        a = a.T
        if name == "w_in":
            a = jnp.pad(a, ((0, IN_SHARD_PAD - IN_SHARD), (0, 0)))
    return a if kind == "f32" else a.astype(dtype)


def _from_comm(name, kind, a):
    if kind == "cols":
        if name == "w_in":
            a = a[:IN_SHARD]
        a = a.T
    return a[None]


def _assemble_weights(g):
    out = {}
    if "w_in" in g:
        out["wt_main"] = _assemble_wt_main(g["w_in"], name="assemble_w_in")
        j, l0 = divmod(O_F, IN_SHARD)
        out["wt_f"] = jnp.pad(g["w_in"][j, l0:l0 + HEADS], ((0, 128 - HEADS), (0, 0)))
    square = dict(w_branch_a="w_a", w_branch_b="w_b", w_out="w_out", w_ple_gate="w_pg")
    for long, short in square.items():
        if long in g:
            out[short] = g[long].reshape(D_MODEL, D_MODEL)
    if "w_up" in g:
        out["wt_up"] = g["w_up"].reshape(2 * D_FF, D_MODEL)
    if "conv_w" in g:
        out["conv_w"] = g["conv_w"].transpose(1, 0, 2).reshape(3, 2 * D_FF)
    if "w_down" in g:
        out["w_down"] = g["w_down"].reshape(D_FF, D_MODEL)
    if "w_ple" in g:
        out["wt_ple"] = g["w_ple"].reshape(D_MODEL, PLE_DIM)
    return out


def _grad_slabs(gr):
    out = {}
    if "wt_main" in gr:
        gm, gf = gr["wt_main"], gr["wt_f"]
        segments = ((0, 2048, gm, 0), (2048, O_F, gm, 2048), (O_F, O_G, gf, -O_F), (O_G, IN_COLS, gm, 2048 - O_G))
        slabs = []
        for j in range(N_DEV):
            lo, hi = j * IN_SHARD, (j + 1) * IN_SHARD
            pieces = [src[max(lo, a) + shift:min(hi, b) + shift] for a, b, src, shift in segments if max(lo, a) < min(hi, b)]
            pieces.append(jnp.zeros((IN_SHARD_PAD - IN_SHARD, D_MODEL), gm.dtype))
            slabs.append(jnp.concatenate(pieces, axis=0))
        out["w_in"] = jnp.stack(slabs)
    rows = dict(w_a="w_branch_a", w_b="w_branch_b", w_out="w_out", wt_up="w_up", w_down="w_down", w_pg="w_ple_gate")
    for short, long in rows.items():
        if short in gr:
            out[long] = gr[short].reshape(N_DEV, -1, D_MODEL)
    if "conv_w" in gr:
        out["conv_w"] = gr["conv_w"].reshape(3, N_DEV, -1).transpose(1, 0, 2)
    if "wt_ple" in gr:
        out["w_ple"] = gr["wt_ple"].reshape(N_DEV, -1, PLE_DIM)
    return {k: v.astype(BF16) for k, v in out.items()}


def _rows(a, rows):
    flat = a.reshape(-1)
    return jnp.pad(flat, (0, rows * 1024 - flat.shape[0])).reshape(rows, 1024)


def _pack_small(parts):
    return jnp.concatenate([_rows(parts[n].astype(F32), r) for n, r in SMALL], axis=0)


def _small(packed, name, shape):
    off, r = SMALL_OFF[name]
    n = math.prod(shape)
    return packed[off:off + r].reshape(-1)[:n].reshape(shape)


class _Exchanges:
    def __init__(self, later, shards, pos):
        self.later, self.shards, self.pos, self.reduced = later, shards, pos, {}

    def gather1(self):
        return _carry_gather1(self.shards)

    def gather2(self, level1):
        return _carry_gather2(level1)

    def weights(self, full):
        return _assemble_weights(dict(zip(self.later, full)))

    def sibling(self, grads):
        self.slabs = _grad_slabs(grads)
        return _carry_sibling([self.slabs[n] for n in self.later])

    def sibling_small(self, small_g):
        self.small_g = small_g
        return _carry_sibling([], small_g)

    def chips(self, from_sib, small_sib):
        sums = [_sum_pairs(self.slabs[n], r, self.pos, name="sum_sibling_" + n) for n, r in zip(self.later, from_sib)]
        self.sums32 = [s32 for s32, _ in sums]
        return _carry_chips([s16 for _, s16 in sums], _pair_sum_small(self.small_g, small_sib, name="sum_sibling_small"))

    def chips_done(self, carried):
        *from_chips, self.table = carried
        self.reduced.update({n: (s32, r) for n, s32, r in zip(self.later, self.sums32, from_chips)})

    def w_in_chips(self, grads_in):
        slab = _grad_slabs(grads_in)["w_in"]
        (from_sib,) = _run_carry(_carry_sibling([slab]), name="exchange_sibling_w_in")
        self.s32_in, s16 = _sum_pairs(slab, from_sib, self.pos, name="sum_sibling_w_in")
        return _carry_chips([s16])

    def w_in_chips_done(self, carried):
        self.reduced["w_in"] = (self.s32_in, carried[0])


def _local_step(x, p, target, w, sm, ex=None):
    s = x.shape[0]
    mm = _matmul
    wt_main = w["wt_main"]
    conv_b = sm["conv_b"]
    bs_t = jnp.pad(sm["gmlp_b_s"].T, ((0, 0), (0, 128 - GROUPS)))
    b_f = jnp.pad(sm["b_f"], ((0, 0), (0, 128 - HEADS)))
    big = dict(tm=1024, tn=1024, tk=1024)
    whole_s = dict(tn=1024, tk=s)

    h = _rmsnorm_fwd(x, sm["norm_mix_g"], name="norm_mix")
    qkv = mm(h, wt_main, mode="nt", out_dtype=BF16, name="in_qkv", n=3072, b_off=4, **big)
    f_logit = mm(h, w["wt_f"], mode="nt", out_dtype=F32, name="in_f", tm=1024, tk=1024)
    cqe = _forget_cumsum(f_logit, b_f, name="forget_cumsum")
    qa, ka, vt = _attn_prep(qkv, cqe, name="attn_prep")
    uvg = dict(mode="nt", out_dtype=F32, name="in_uvg", n=4096, **big)
    if ex is None:
        (b, lse3), _ = _attn_fwd(qa, ka, vt, name="attn_fwd")
        zuvg = mm(h, wt_main, **uvg)
    else:
        (b, lse3), level1 = _attn_fwd(qa, ka, vt, ex.gather1(), name="attn_fwd")
        zuvg, full = mm(h, wt_main, carry=ex.gather2(level1), **uvg)
        w = {**w, **ex.weights(full)}
    a = _gmlp_fwd(zuvg, sm["gmlp_ln_g"], sm["gmlp_ln_b"], sm["gmlp_w_s"], bs_t, name="gmlp_fwd")
    wt_up, conv_w = w["wt_up"], w["conv_w"]
    ya = mm(a, w["w_a"], mode="nn", out_dtype=F32, name="branch_a", **big)
    yb = mm(b, w["w_b"], mode="nn", out_dtype=F32, name="branch_b", **big)
    merged = _merge_fwd(ya, yb, zuvg, name="merge_fwd")
    x1 = mm(merged, w["w_out"], mode="nn", out_dtype=F32, name="out_proj", add=x, **big)
    h2 = _rmsnorm_fwd(x1, sm["norm_ffn_g"], name="norm_ffn")
    up = mm(h2, wt_up, mode="nt", out_dtype=F32, name="up", tm=1024, tn=512, tk=1024)
    act = _convglu_fwd(up, conv_w, conv_b, name="convglu_fwd")
    x2 = mm(act, w["w_down"], mode="nn", out_dtype=F32, name="down", tm=1024, tn=1024, tk=1408, add=x1)
    h3 = _rmsnorm_fwd(x2, sm["norm_ple_g"], name="norm_ple")
    ple = mm(p, w["wt_ple"], mode="nt", out_dtype=F32, name="ple", tm=1024, tn=1024, tk=256)
    gp = mm(h3, w["w_pg"], mode="nn", out_dtype=F32, name="ple_gate", **big)
    x3 = _ple_fwd(x2, ple, gp, name="ple_fwd")

    loss, dx3, d_norm_final = _final_loss_bwd(x3, target, sm["norm_final_g"], name="loss_bwd")
    dple, dgp = _ple_bwd(dx3, ple, gp, name="ple_bwd")
    g_wt_ple = mm(dple, p, mode="tn", out_dtype=BF16, name="d_w_ple", tm=512, tn=256, tk=s)
    g_w_pg = mm(h3, dgp, mode="tn", out_dtype=BF16, name="d_w_pg", tm=256, **whole_s)
    dh3 = mm(dgp, w["w_pg"], mode="nt", out_dtype=F32, name="d_h3", **big)
    dx2, dx2b, d_norm_ple = _rmsnorm_bwd(dx3, dh3, x2, sm["norm_ple_g"], name="norm_ple_bwd")
    g_w_down = mm(act, dx2b, mode="tn", out_dtype=BF16, name="d_w_down", tm=256, **whole_s)
    dact = mm(dx2b, w["w_down"], mode="nt", out_dtype=BF16, name="d_act", tm=1024, tn=1408, tk=1024)
    dup_a, dup_g, dcw_a, dcw_g, dcb_a, dcb_g = _convglu_bwd(dact, up, conv_w, conv_b, name="convglu_bwd")
    g_wt_up = mm(dup_a, h2, mode="tn", out_dtype=BF16, name="d_w_up_a", tm=256, out_rows=2 * D_FF, **whole_s)
    g_wt_up = mm(dup_g, h2, mode="tn", out_dtype=BF16, name="d_w_up_g", tm=256, out_rows=2 * D_FF,
                 o_off=D_FF // 256, into=g_wt_up, **whole_s)
    dh2 = mm(dup_a, wt_up, mode="nn", out_dtype=F32, name="d_h2_a", tm=1024, tn=1024, tk=1408)
    dh2 = mm(dup_g, wt_up, mode="nn", out_dtype=F32, name="d_h2_g", tm=1024, tn=1024, tk=1408, b_off=2, add=dh2)
    dx1, dx1b, d_norm_ffn = _rmsnorm_bwd(dx2, dh2, x1, sm["norm_ffn_g"], name="norm_ffn_bwd")
    g_w_out = mm(merged, dx1b, mode="tn", out_dtype=BF16, name="d_w_out", tm=256, **whole_s)
    dmerged = mm(dx1b, w["w_out"], mode="nt", out_dtype=F32, name="d_merged", **big)
    dya, dyb, dga, dgb = _merge_bwd(dmerged, ya, yb, zuvg, name="merge_bwd")
    g_w_a = mm(a, dya, mode="tn", out_dtype=BF16, name="d_w_a", tm=256, **whole_s)
    g_w_b = mm(b, dyb, mode="tn", out_dtype=BF16, name="d_w_b", tm=256, **whole_s)
    da = mm(dya, w["w_a"], mode="nt", out_dtype=BF16, name="d_a", **big)
    db = mm(dyb, w["w_b"], mode="nt", out_dtype=BF16, name="d_b", **big)
    grads = dict(w_a=g_w_a, w_b=g_w_b, w_out=g_w_out, wt_up=g_wt_up, conv_w=jnp.concatenate([dcw_a, dcw_g], axis=1),
                 w_down=g_w_down, wt_ple=g_wt_ple, w_pg=g_w_pg)
    gmlp_args = (da, zuvg, sm["gmlp_ln_g"], sm["gmlp_ln_b"], sm["gmlp_w_s"], bs_t)
    if ex is None:
        (dzu, dzv, d_w_s, d_bs_t, d_ln_g, d_ln_b), _ = _gmlp_bwd(*gmlp_args, name="gmlp_bwd")
    else:
        (dzu, dzv, d_w_s, d_bs_t, d_ln_g, d_ln_b), from_sib = _gmlp_bwd(*gmlp_args, ex.sibling(grads), name="gmlp_bwd")
    small = dict(norm_mix_g=jnp.zeros((1, D_MODEL), F32), b_f=jnp.zeros((1, HEADS), F32), gmlp_ln_g=d_ln_g,
                 gmlp_ln_b=d_ln_b, gmlp_w_s=d_w_s, gmlp_b_s=d_bs_t[:, :GROUPS].T, norm_ffn_g=d_norm_ffn,
                 conv_b=jnp.concatenate([dcb_a, dcb_g], axis=1), norm_ple_g=d_norm_ple, norm_final_g=d_norm_final)
    if ex is None:
        delta3, _ = _attn_delta(db, b, name="attn_delta")
        (dq, dk, dv, aux, dcq3), _ = _attn_bwd(qa, ka, qkv, db, lse3, delta3, name="attn_bwd")
    else:
        delta3, (small_sib,) = _attn_delta(db, b, ex.sibling_small(_pack_small(small)), name="attn_delta")
        (dq, dk, dv, aux, dcq3), carried = _attn_bwd(qa, ka, qkv, db, lse3, delta3, ex.chips(from_sib, small_sib),
                                                     name="attn_bwd")
        ex.chips_done(carried)
    dcq16 = jnp.pad(dcq3[:, :2, :].reshape(HEADS, s).T, ((0, 0), (0, 128 - HEADS)))
    dzf, d_b_f = _forget_bwd(dcq16, aux, f_logit, b_f, name="forget_bwd")
    dz = jnp.concatenate([dzu, dzv, dga, dgb, dq, dk, dv], axis=1)
    g_wt_main = mm(dz, h, mode="tn", out_dtype=BF16, name="d_w_main", tm=512, **whole_s)
    g_wt_f = mm(dzf, h, mode="tn", out_dtype=BF16, name="d_w_f", **whole_s)
    grads = dict(grads, wt_main=g_wt_main, wt_f=g_wt_f)
    dh_main = dict(mode="nn", out_dtype=F32, name="d_h_main", **big)
    if ex is None:
        dh = mm(dz, wt_main, **dh_main)
    else:
        dh, carried = mm(dz, wt_main, carry=ex.w_in_chips(dict(wt_main=g_wt_main, wt_f=g_wt_f)), **dh_main)
        ex.w_in_chips_done(carried)
    dh = mm(dzf, w["wt_f"], mode="nn", out_dtype=F32, name="d_h_f", tm=1024, tn=1024, add=dh)
    dx0, _, d_norm_mix = _rmsnorm_bwd(dx1, dh, x, sm["norm_mix_g"], name="norm_mix_bwd")
    return loss, dx0, grads, dict(small, norm_mix_g=d_norm_mix, b_f=d_b_f[:, :HEADS])


def kernel(x, p, norm_mix_g, w_in, b_f, gmlp_ln_g, gmlp_ln_b, gmlp_w_s, gmlp_b_s, w_branch_a, w_branch_b, w_out, norm_ffn_g, w_up, conv_w, conv_b, w_down, norm_ple_g, w_ple, w_ple_gate, norm_final_g, loss_target, m_norm_mix_g, m_w_in, m_b_f, m_gmlp_ln_g, m_gmlp_ln_b, m_gmlp_w_s, m_gmlp_b_s, m_w_branch_a, m_w_branch_b, m_w_out, m_norm_ffn_g, m_w_up, m_conv_w, m_conv_b, m_w_down, m_norm_ple_g, m_w_ple, m_w_ple_gate, m_norm_final_g, v_norm_mix_g, v_w_in, v_b_f, v_gmlp_ln_g, v_gmlp_ln_b, v_gmlp_w_s, v_gmlp_b_s, v_w_branch_a, v_w_branch_b, v_w_out, v_norm_ffn_g, v_w_up, v_conv_w, v_conv_b, v_w_down, v_norm_ple_g, v_w_ple, v_w_ple_gate, v_norm_final_g):
    given = dict(locals())
    weights = {n: given[n] for n in WEIGHT_ORDER}
    mom_m = {n: given["m_" + n] for n in WEIGHT_ORDER}
    mom_v = {n: given["v_" + n] for n in WEIGHT_ORDER}
    pos = jnp.stack([lax.axis_index("x"), lax.axis_index("y"), lax.axis_index("c")]).astype(I32)
    names = [n for n, _ in SHARDED]
    kinds = dict(SHARDED)

    later = [n for n in names if n != "w_in"]

    first = _allgather([_to_comm("w_in", kinds["w_in"], weights["w_in"])], name="allgather_w_in")
    ex = _Exchanges(later, [_to_comm(n, kinds[n], weights[n]) for n in later], pos)

    sm = dict(norm_mix_g=norm_mix_g, b_f=b_f, gmlp_ln_g=gmlp_ln_g, gmlp_ln_b=gmlp_ln_b, gmlp_w_s=gmlp_w_s[0],
              gmlp_b_s=gmlp_b_s[0], norm_ffn_g=norm_ffn_g, conv_b=conv_b, norm_ple_g=norm_ple_g,
              norm_final_g=norm_final_g.reshape(1, D_MODEL))
    loss_part, dx0, grads, small = _local_step(
        x[0], p[0, 0], loss_target[0], _assemble_weights({"w_in": first[0]}), sm, ex)

    b_f_and_loss = jnp.concatenate([small["b_f"].reshape(-1), loss_part[0, :1]])
    last = _allreduce_rows(jnp.concatenate([_rows(small["norm_mix_g"], 8), _rows(b_f_and_loss, 8)], axis=0),
                           name="allreduce_last")
    loss = last[8, HEADS]
    small_last = jnp.pad(last, ((0, SMALL_ROWS - 16), (0, 0)))

    grad, delta, new_m, new_v = {}, {}, {}, {}
    for n in names:
        s32, r = ex.reduced[n]
        outs = _adam_sharded(s32, r, *[_to_comm(n, kinds[n], src[n], F32) for src in (weights, mom_m, mom_v)], pos,
                             name="adam_" + n)
        grad[n], delta[n], new_m[n], new_v[n] = [_from_comm(n, kinds[n], o) for o in outs]
    replicated = [n for n, _ in SMALL]
    rep = lambda src: _pack_small({n: src[n] for n in replicated})
    packed = _adam_replicated(ex.table, small_last, rep(weights), rep(mom_m), rep(mom_v), name="adam_replicated")
    for out, pk in zip((grad, delta, new_m, new_v), packed):
        for n in replicated:
            out[n] = _small(pk, n, weights[n].shape)

    return (loss, dx0[None], *[grad[n] for n in WEIGHT_ORDER], *[delta[n] for n in WEIGHT_ORDER],
            *[new_m[n] for n in WEIGHT_ORDER], *[new_v[n] for n in WEIGHT_ORDER])
```

```python
import functools
import math

import jax
import jax.numpy as jnp
from jax import lax
from jax.experimental import pallas as pl
from jax.experimental.pallas import tpu as pltpu

F32 = jnp.float32
BF16 = jnp.bfloat16
I32 = jnp.int32

D_MODEL = 1024
GROUPS = 8
GDIM = 128
GBLOCK = 128
CHUNK = 64
HEADS = 16
HEAD_DIM = 64
D_FF = 2816
PLE_DIM = 256
EPS = 1e-6
N_DEV = 8
ATT_SCALE = HEAD_DIM ** -0.5
NEG = -1e30

ADAM_LR = 0.001
ADAM_B1 = 0.9
ADAM_B2 = 0.999
ADAM_EPS = 1e-08
ADAM_WD = 0.01
ADAM_STEP = 10

V7X_VMEM_LIMIT = 48 * 1024 * 1024
MESH = pl.DeviceIdType.MESH

O_F = 2 * 1024 + 3 * 1024
O_G = O_F + HEADS
IN_COLS = O_G + 2 * D_MODEL
MAIN_COLS = IN_COLS - HEADS
IN_SHARD = IN_COLS // N_DEV
IN_SHARD_PAD = 912

SHARDED = (("w_in", "cols"), ("w_branch_a", "rows"), ("w_branch_b", "rows"), ("w_out", "rows"), ("w_up", "cols"),
           ("conv_w", "f32"), ("w_down", "rows"), ("w_ple", "cols"), ("w_ple_gate", "rows"))

SMALL = (("norm_mix_g", 8), ("b_f", 8), ("gmlp_ln_g", 8), ("gmlp_ln_b", 8), ("gmlp_w_s", 128), ("gmlp_b_s", 8),
         ("norm_ffn_g", 8), ("conv_b", 8), ("norm_ple_g", 8), ("norm_final_g", 8))
SMALL_OFF = {}
_o = 0
for _n, _r in SMALL:
    SMALL_OFF[_n] = (_o, _r)
    _o += _r
SMALL_ROWS = _o

WEIGHT_ORDER = ("norm_mix_g", "w_in", "b_f", "gmlp_ln_g", "gmlp_ln_b", "gmlp_w_s", "gmlp_b_s", "w_branch_a",
                "w_branch_b", "w_out", "norm_ffn_g", "w_up", "conv_w", "conv_b", "w_down", "norm_ple_g", "w_ple",
                "w_ple_gate", "norm_final_g")


def _cparams(sem):
    return pltpu.CompilerParams(dimension_semantics=sem, vmem_limit_bytes=V7X_VMEM_LIMIT)


def _gelu(x):
    c = math.sqrt(2.0 / math.pi)
    return 0.5 * x * (1.0 + jnp.tanh(c * (x + 0.044715 * x * x * x)))


def _gelu_and_grad(x):
    c = math.sqrt(2.0 / math.pi)
    t = jnp.tanh(c * (x + 0.044715 * x * x * x))
    g = 0.5 * x * (1.0 + t)
    dg = 0.5 * (1.0 + t) + 0.5 * x * (1.0 - t * t) * (c * (1.0 + 3.0 * 0.044715 * x * x))
    return g, dg


def _sigmoid(x):
    return 1.0 / (1.0 + jnp.exp(-x))


def _dot(a, b, dims):
    return lax.dot_general(a, b, (dims, ((), ())), preferred_element_type=F32)


NN = ((1,), (0,))
NT = ((1,), (1,))
TN = ((0,), (0,))


def _row_tile(rows, most):
    best = None
    for t in range(16, min(rows, most) + 1, 16):
        if rows % t == 0:
            best = t
    return best if best is not None else rows


def _matmul(a, b, *, mode, out_dtype, name, tm=512, tn=512, tk=512, add=None, n=None, b_off=0,
            out_rows=None, o_off=0, into=None, carry=None):
    if mode == "tn":
        kdim, m = a.shape
    else:
        m, kdim = a.shape
    if n is None:
        n = b.shape[0] if mode == "nt" else b.shape[1]
    tm, tn, tk = min(tm, m), min(tn, n), min(tk, kdim)
    assert m % tm == 0 and n % tn == 0 and kdim % tk == 0, (name, m, n, kdim, tm, tn, tk)
    nk = kdim // tk
    dims = {"nn": NN, "nt": NT, "tn": TN}[mode]

    def finish(r, add_ref, o_ref):
        if add_ref is not None:
            r = add_ref[...].astype(F32) + r
        o_ref[...] = r.astype(out_dtype)

    def body(*refs):
        refs = list(refs)
        a_ref, b_ref = refs[:2]
        add_ref = refs[2] if add is not None else None
        o_ref = refs[2 + (add is not None) + (into is not None)]
        part = _dot(a_ref[...].astype(BF16), b_ref[...].astype(BF16), dims)
        if nk == 1:
            finish(part, add_ref, o_ref)
            return
        acc_ref = refs[-1]
        k = pl.program_id(2)

        @pl.when(k == 0)
        def _():
            acc_ref[...] = part

        @pl.when((k > 0) & (k < nk - 1))
        def _():
            acc_ref[...] += part

        @pl.when(k == nk - 1)
        def _():
            finish(acc_ref[...] + part, add_ref, o_ref)

    a_spec = pl.BlockSpec((tk, tm), lambda i, j, k: (k, i)) if mode == "tn" else pl.BlockSpec((tm, tk), lambda i, j, k: (i, k))
    if mode == "nt":
        b_spec = pl.BlockSpec((tn, tk), lambda i, j, k: (j + b_off, k))
    else:
        b_spec = pl.BlockSpec((tk, tn), lambda i, j, k: (k + b_off, j))
    o_spec = pl.BlockSpec((tm, tn), lambda i, j, k: (i + o_off, j))
    in_specs = [a_spec, b_spec] + ([pl.BlockSpec((tm, tn), lambda i, j, k: (i, j))] if add is not None else [])
    args = (a, b) + ((add,) if add is not None else ())
    aliases = {}
    if into is not None:
        aliases = {len(args): 0}
        in_specs.append(pl.BlockSpec(memory_space=pl.ANY))
        args += (into,)
    (out,), carried = _carry_call(
        body, carry, name=name, grid=(m // tm, n // tn, nk), in_specs=in_specs, out_specs=[o_spec],
        out_shape=[jax.ShapeDtypeStruct((m if out_rows is None else out_rows, n), out_dtype)],
        scratch_shapes=[pltpu.VMEM((tm, tn), F32)] if nk > 1 else [], args=args, own_aliases=aliases)
    return out if carry is None else (out, carried)


def _row_spec(tr, width, col_block=0):
    return pl.BlockSpec((tr, width), lambda i: (i, col_block))


def _full_spec(shape):
    return pl.BlockSpec(shape, lambda i: tuple(0 for _ in shape))


def _rmsnorm_fwd(x, g, *, name, tr=256):
    s, d = x.shape

    def body(x_ref, g_ref, o_ref):
        xv = x_ref[...]
        r = lax.rsqrt(jnp.mean(xv * xv, axis=-1, keepdims=True) + EPS)
        o_ref[...] = ((xv * r) * g_ref[...]).astype(BF16)

    return pl.pallas_call(
        body, name=name, grid=(s // tr,),
        in_specs=[_row_spec(tr, d), _full_spec((1, d))], out_specs=_row_spec(tr, d),
        out_shape=jax.ShapeDtypeStruct((s, d), BF16), compiler_params=_cparams(("parallel",)),
    )(x, g)


def _rmsnorm_bwd(dres, dh, x, g, *, name, tr=256):
    s, d = x.shape

    def body(dres_ref, dh_ref, x_ref, g_ref, dx_ref, dxb_ref, dg_ref):
        i = pl.program_id(0)
        xv = x_ref[...]
        r = lax.rsqrt(jnp.mean(xv * xv, axis=-1, keepdims=True) + EPS)
        xhat = xv * r
        dhv = dh_ref[...].astype(F32)
        dxhat = dhv * g_ref[...]
        dx = dres_ref[...] + r * (dxhat - xhat * jnp.mean(dxhat * xhat, axis=-1, keepdims=True))
        dx_ref[...] = dx
        dxb_ref[...] = dx.astype(BF16)
        dgp = jnp.sum(dhv * xhat, axis=0, keepdims=True)

        @pl.when(i == 0)
        def _():
            dg_ref[...] = dgp

        @pl.when(i > 0)
        def _():
            dg_ref[...] += dgp

    return pl.pallas_call(
        body, name=name, grid=(s // tr,),
        in_specs=[_row_spec(tr, d), _row_spec(tr, d), _row_spec(tr, d), _full_spec((1, d))],
        out_specs=[_row_spec(tr, d), _row_spec(tr, d), _full_spec((1, d))],
        out_shape=[jax.ShapeDtypeStruct((s, d), F32), jax.ShapeDtypeStruct((s, d), BF16),
                   jax.ShapeDtypeStruct((1, d), F32)],
        compiler_params=_cparams(("arbitrary",)),
    )(dres, dh, x, g)


def _final_loss_bwd(x3, target, g, *, name, tr=256):
    s, d = x3.shape

    def body(x_ref, t_ref, g_ref, loss_ref, dx_ref, dg_ref):
        i = pl.program_id(0)
        xv = x_ref[...]
        r = lax.rsqrt(jnp.mean(xv * xv, axis=-1, keepdims=True) + EPS)
        xhat = xv * r
        diff = xhat * g_ref[...] - t_ref[...]
        lp = jnp.zeros((1, 128), F32) + (0.5 / d) * jnp.sum(diff * diff)
        dy = diff * (1.0 / d)
        dxhat = dy * g_ref[...]
        dx_ref[...] = r * (dxhat - xhat * jnp.mean(dxhat * xhat, axis=-1, keepdims=True))
        dgp = jnp.sum(dy * xhat, axis=0, keepdims=True)

        @pl.when(i == 0)
        def _():
            dg_ref[...] = dgp
            loss_ref[...] = lp

        @pl.when(i > 0)
        def _():
            dg_ref[...] += dgp
            loss_ref[...] += lp

    return pl.pallas_call(
        body, name=name, grid=(s // tr,),
        in_specs=[_row_spec(tr, d), _row_spec(tr, d), _full_spec((1, d))],
        out_specs=[_full_spec((1, 128)), _row_spec(tr, d), _full_spec((1, d))],
        out_shape=[jax.ShapeDtypeStruct((1, 128), F32), jax.ShapeDtypeStruct((s, d), F32),
                   jax.ShapeDtypeStruct((1, d), F32)],
        compiler_params=_cparams(("arbitrary",)),
    )(x3, target, g)


def _merge_fwd(ya, yb, zuvg, *, name, tr=256):
    s, d = ya.shape

    def body(ya_ref, yb_ref, ga_ref, gb_ref, o_ref):
        o_ref[...] = (_sigmoid(ga_ref[...]) * ya_ref[...] + _sigmoid(gb_ref[...]) * yb_ref[...]).astype(BF16)

    return pl.pallas_call(
        body, name=name, grid=(s // tr,),
        in_specs=[_row_spec(tr, d), _row_spec(tr, d), _row_spec(tr, d, 2), _row_spec(tr, d, 3)],
        out_specs=_row_spec(tr, d),
        out_shape=jax.ShapeDtypeStruct((s, d), BF16), compiler_params=_cparams(("parallel",)),
    )(ya, yb, zuvg, zuvg)


def _merge_bwd(dm, ya, yb, zuvg, *, name, tr=256):
    s, d = ya.shape

    def body(dm_ref, ya_ref, yb_ref, ga_ref, gb_ref, dya_ref, dyb_ref, dga_ref, dgb_ref):
        dmv = dm_ref[...]
        sa = _sigmoid(ga_ref[...])
        sb = _sigmoid(gb_ref[...])
        dya_ref[...] = (dmv * sa).astype(BF16)
        dyb_ref[...] = (dmv * sb).astype(BF16)
        dga_ref[...] = (dmv * ya_ref[...] * (sa * (1.0 - sa))).astype(BF16)
        dgb_ref[...] = (dmv * yb_ref[...] * (sb * (1.0 - sb))).astype(BF16)

    o = jax.ShapeDtypeStruct((s, d), BF16)
    return pl.pallas_call(
        body, name=name, grid=(s // tr,),
        in_specs=[_row_spec(tr, d)] * 3 + [_row_spec(tr, d, 2), _row_spec(tr, d, 3)], out_specs=[_row_spec(tr, d)] * 4,
        out_shape=[o, o, o, o], compiler_params=_cparams(("parallel",)),
    )(dm, ya, yb, zuvg, zuvg)


def _ple_fwd(x2, ple, gp, *, name, tr=256):
    s, d = x2.shape

    def body(x_ref, ple_ref, gp_ref, o_ref):
        o_ref[...] = x_ref[...] + ple_ref[...] * _sigmoid(gp_ref[...])

    return pl.pallas_call(
        body, name=name, grid=(s // tr,),
        in_specs=[_row_spec(tr, d)] * 3, out_specs=_row_spec(tr, d),
        out_shape=jax.ShapeDtypeStruct((s, d), F32), compiler_params=_cparams(("parallel",)),
    )(x2, ple, gp)


def _ple_bwd(dx3, ple, gp, *, name, tr=256):
    s, d = dx3.shape

    def body(dx_ref, ple_ref, gp_ref, dple_ref, dgp_ref):
        sg = _sigmoid(gp_ref[...])
        dxv = dx_ref[...]
        dple_ref[...] = (dxv * sg).astype(BF16)
        dgp_ref[...] = (dxv * ple_ref[...] * (sg * (1.0 - sg))).astype(BF16)

    o = jax.ShapeDtypeStruct((s, d), BF16)
    return pl.pallas_call(
        body, name=name, grid=(s // tr,),
        in_specs=[_row_spec(tr, d)] * 3, out_specs=[_row_spec(tr, d)] * 2,
        out_shape=[o, o], compiler_params=_cparams(("parallel",)),
    )(dx3, ple, gp)


def _masked_ws(ws_ref, g):
    row = lax.broadcasted_iota(I32, (GBLOCK, GBLOCK), 0)
    col = lax.broadcasted_iota(I32, (GBLOCK, GBLOCK), 1)
    keep = (col // CHUNK) <= (row // CHUNK)
    return jnp.where(keep, ws_ref[g], 0.0), keep


def _layernorm_parts(zv):
    mu = jnp.mean(zv, axis=-1, keepdims=True)
    xc = zv - mu
    rs = lax.rsqrt(jnp.mean(xc * xc, axis=-1, keepdims=True) + EPS)
    return xc * rs, rs


def _gmlp_fwd(zuvg, ln_g, ln_b, w_s, bs_t, *, name):
    s, w = zuvg.shape[0], GROUPS * GDIM

    def body(zu_ref, zv_ref, lng_ref, lnb_ref, ws_ref, bs_ref, a_ref):
        zu = _gelu(zu_ref[...])
        zv = _gelu(zv_ref[...])
        xhat, _ = _layernorm_parts(zv)
        vln = (xhat * lng_ref[...] + lnb_ref[...]).astype(BF16)
        for g in range(GROUPS):
            wm, _ = _masked_ws(ws_ref, g)
            mixed = _dot(wm.astype(BF16), vln[:, g * GDIM:(g + 1) * GDIM], NN) + bs_ref[:, g:g + 1]
            a_ref[:, g * GDIM:(g + 1) * GDIM] = (zu[:, g * GDIM:(g + 1) * GDIM] * mixed).astype(BF16)

    return pl.pallas_call(
        body, name=name, grid=(s // GBLOCK,),
        in_specs=[_row_spec(GBLOCK, w, 0), _row_spec(GBLOCK, w, 1), _full_spec((1, w)), _full_spec((1, w)),
                  _full_spec((GROUPS, GBLOCK, GBLOCK)), _full_spec((GBLOCK, 128))],
        out_specs=_row_spec(GBLOCK, w),
        out_shape=jax.ShapeDtypeStruct((s, w), BF16), compiler_params=_cparams(("parallel",)),
    )(zuvg, zuvg, ln_g, ln_b, w_s, bs_t)


def _gmlp_bwd(da, zuvg, ln_g, ln_b, w_s, bs_t, carry=None, *, name):
    s, w = zuvg.shape[0], GROUPS * GDIM

    def body(da_ref, zu_ref, zv_ref, lng_ref, lnb_ref, ws_ref, bs_ref,
             dzu_ref, dzv_ref, dws_ref, dbs_ref, dlng_ref, dlnb_ref, dvln_ref):
        i = pl.program_id(0)
        zu, dzu_g = _gelu_and_grad(zu_ref[...])
        zv, dzv_g = _gelu_and_grad(zv_ref[...])
        xhat, rs = _layernorm_parts(zv)
        vln = (xhat * lng_ref[...] + lnb_ref[...]).astype(BF16)
        dav = da_ref[...].astype(F32)
        lane = lax.broadcasted_iota(I32, (GBLOCK, 128), 1)
        dbs = jnp.zeros((GBLOCK, 128), F32)

        @pl.when(i == 0)
        def _():
            dws_ref[...] = jnp.zeros_like(dws_ref)

        for g in range(GROUPS):
            sl = slice(g * GDIM, (g + 1) * GDIM)
            wm, keep = _masked_ws(ws_ref, g)
            wmb = wm.astype(BF16)
            vg = vln[:, sl]
            mixed = _dot(wmb, vg, NN) + bs_ref[:, g:g + 1]
            dag = dav[:, sl]
            dzu_ref[:, sl] = (dag * mixed * dzu_g[:, sl]).astype(BF16)
            dmix = dag * zu[:, sl]
            dmb = dmix.astype(BF16)
            dws_ref[g] += jnp.where(keep, _dot(dmb, vg, NT), 0.0)
            dbs = jnp.where(lane == g, jnp.sum(dmix, axis=1, keepdims=True), dbs)
            dvln_ref[:, sl] = _dot(wmb, dmb, TN)
        dvln = dvln_ref[...]
        dxhat = dvln * lng_ref[...]
        dzv = rs * (dxhat - jnp.mean(dxhat, axis=-1, keepdims=True)
                    - xhat * jnp.mean(dxhat * xhat, axis=-1, keepdims=True))
        dzv_ref[...] = (dzv * dzv_g).astype(BF16)
        dlng = jnp.sum(dvln * xhat, axis=0, keepdims=True)
        dlnb = jnp.sum(dvln, axis=0, keepdims=True)

        @pl.when(i == 0)
        def _():
            dbs_ref[...] = dbs
            dlng_ref[...] = dlng
            dlnb_ref[...] = dlnb

        @pl.when(i > 0)
        def _():
            dbs_ref[...] += dbs
            dlng_ref[...] += dlng
            dlnb_ref[...] += dlnb

    return _carry_call(
        body, carry, name=name, grid=(s // GBLOCK,),
        in_specs=[_row_spec(GBLOCK, w), _row_spec(GBLOCK, w, 0), _row_spec(GBLOCK, w, 1), _full_spec((1, w)),
                  _full_spec((1, w)), _full_spec((GROUPS, GBLOCK, GBLOCK)), _full_spec((GBLOCK, 128))],
        out_specs=[_row_spec(GBLOCK, w), _row_spec(GBLOCK, w), _full_spec((GROUPS, GBLOCK, GBLOCK)),
                   _full_spec((GBLOCK, 128)), _full_spec((1, w)), _full_spec((1, w))],
        out_shape=[jax.ShapeDtypeStruct((s, w), BF16), jax.ShapeDtypeStruct((s, w), BF16),
                   jax.ShapeDtypeStruct((GROUPS, GBLOCK, GBLOCK), F32), jax.ShapeDtypeStruct((GBLOCK, 128), F32),
                   jax.ShapeDtypeStruct((1, w), F32), jax.ShapeDtypeStruct((1, w), F32)],
        scratch_shapes=[pltpu.VMEM((GBLOCK, w), F32)], args=[da, zuvg, zuvg, ln_g, ln_b, w_s, bs_t])


def _shift_down(u, k):
    row = lax.broadcasted_iota(I32, u.shape, 0)
    return jnp.where(row >= k, pltpu.roll(u, k, 0), 0.0)


def _shift_up(u, k):
    s = u.shape[0]
    row = lax.broadcasted_iota(I32, u.shape, 0)
    return jnp.where(row < s - k, pltpu.roll(u, s - k, 0), 0.0)


def _conv(u, w_ref, b_ref):
    return b_ref[...] + w_ref[0:1, :] * _shift_down(u, 2) + w_ref[1:2, :] * _shift_down(u, 1) + w_ref[2:3, :] * u


def _conv_specs(s, f, tc):
    nc = f // tc
    half = lambda rows: [pl.BlockSpec((rows, tc), lambda j: (0, j)), pl.BlockSpec((rows, tc), lambda j: (0, nc + j))]
    return half(s), half(3), half(1)


def _convglu_fwd(up, conv_w, conv_b, *, name, tc=256):
    s, f = up.shape[0], up.shape[1] // 2
    up_specs, w_specs, b_specs = _conv_specs(s, f, tc)

    def body(ua_ref, ug_ref, wa_ref, wg_ref, ba_ref, bg_ref, o_ref):
        ca = _conv(ua_ref[...], wa_ref, ba_ref)
        cg = _conv(ug_ref[...], wg_ref, bg_ref)
        o_ref[...] = (_gelu(ca) * cg).astype(BF16)

    return pl.pallas_call(
        body, name=name, grid=(f // tc,),
        in_specs=up_specs + w_specs + b_specs, out_specs=up_specs[0],
        out_shape=jax.ShapeDtypeStruct((s, f), BF16), compiler_params=_cparams(("parallel",)),
    )(up, up, conv_w, conv_w, conv_b, conv_b)


def _convglu_bwd(dact, up, conv_w, conv_b, *, name, tc=256):
    s, f = up.shape[0], up.shape[1] // 2
    up_specs, w_specs, b_specs = _conv_specs(s, f, tc)

    def half(dc, taps, w_ref, du_ref, dw_ref, db_ref):
        db_ref[...] = jnp.sum(dc, axis=0, keepdims=True)
        for k in range(3):
            dw_ref[k:k + 1, :] = jnp.sum(dc * taps[k], axis=0, keepdims=True)
        du = w_ref[2:3, :] * dc + w_ref[1:2, :] * _shift_up(dc, 1) + w_ref[0:1, :] * _shift_up(dc, 2)
        du_ref[...] = du.astype(BF16)

    def body(d_ref, ua_ref, ug_ref, wa_ref, wg_ref, ba_ref, bg_ref,
             dua_ref, dug_ref, dwa_ref, dwg_ref, dba_ref, dbg_ref):
        taps_a = (_shift_down(ua_ref[...], 2), _shift_down(ua_ref[...], 1), ua_ref[...])
        taps_g = (_shift_down(ug_ref[...], 2), _shift_down(ug_ref[...], 1), ug_ref[...])
        conv = lambda taps, w_ref, b_ref: b_ref[...] + w_ref[0:1, :] * taps[0] + w_ref[1:2, :] * taps[1] + w_ref[2:3, :] * taps[2]
        ca = conv(taps_a, wa_ref, ba_ref)
        cg = conv(taps_g, wg_ref, bg_ref)
        ga, dga = _gelu_and_grad(ca)
        dv = d_ref[...].astype(F32)
        half(dv * cg * dga, taps_a, wa_ref, dua_ref, dwa_ref, dba_ref)
        half(dv * ga, taps_g, wg_ref, dug_ref, dwg_ref, dbg_ref)

    col, w3, b1 = up_specs[0], w_specs[0], b_specs[0]
    return pl.pallas_call(
        body, name=name, grid=(f // tc,),
        in_specs=[col] + up_specs + w_specs + b_specs, out_specs=[col, col, w3, w3, b1, b1],
        out_shape=[jax.ShapeDtypeStruct((s, f), BF16), jax.ShapeDtypeStruct((s, f), BF16),
                   jax.ShapeDtypeStruct((3, f), F32), jax.ShapeDtypeStruct((3, f), F32),
                   jax.ShapeDtypeStruct((1, f), F32), jax.ShapeDtypeStruct((1, f), F32)],
        compiler_params=_cparams(("parallel",)),
    )(dact, up, up, conv_w, conv_w, conv_b, conv_b)


def _tri_dot(tri, x):
    b0 = x.astype(BF16)
    r1 = x - b0.astype(F32)
    b1 = r1.astype(BF16)
    b2 = (r1 - b1.astype(F32)).astype(BF16)
    return _dot(tri, b0, NN) + _dot(tri, b1, NN) + _dot(tri, b2, NN)


def _log_sigmoid(x):
    return jnp.minimum(x, 0.0) - jnp.log(1.0 + jnp.exp(-jnp.abs(x)))


def _expand_heads(col16, rows):
    src = lax.broadcasted_iota(I32, (128, HEADS * HEAD_DIM), 0)
    dst = lax.broadcasted_iota(I32, (128, HEADS * HEAD_DIM), 1) // HEAD_DIM
    spread = (src == dst).astype(BF16)
    p0, p1, p2 = _bf16_pieces(col16)
    return (_dot(p0.astype(BF16), spread, NN) + _dot(p1.astype(BF16), spread, NN)) + _dot(p2.astype(BF16), spread, NN)


def _forget_cumsum(f_logit, b_f, *, name):
    s = f_logit.shape[0]
    nb = s // 128

    def body(f_ref, b_ref, cqe_ref):
        row = lax.broadcasted_iota(I32, (128, 128), 0)
        col = lax.broadcasted_iota(I32, (128, 128), 1)
        tri = (col <= row).astype(BF16)

        def step(n, carry):
            r0 = pl.multiple_of(n * 128, 128)
            lf = _log_sigmoid(f_ref[pl.ds(r0, 128), :] + b_ref[...])
            cum = _tri_dot(tri, lf) + carry
            cqe_ref[pl.ds(r0, 128), :] = _expand_heads(cum, 128)
            return cum[127:128, :]

        lax.fori_loop(0, nb, step, jnp.zeros((1, 128), F32))

    return pl.pallas_call(
        body, name=name, grid=(1,),
        in_specs=[_full_spec((s, 128)), _full_spec((1, 128))],
        out_specs=_full_spec((s, HEADS * HEAD_DIM)),
        out_shape=jax.ShapeDtypeStruct((s, HEADS * HEAD_DIM), F32),
        compiler_params=_cparams(("arbitrary",)),
    )(f_logit, b_f)


def _forget_bwd(dcq16, sum_q16, f_logit, b_f, *, name):
    s = f_logit.shape[0]
    nb = s // 128

    def body(a_ref, k_ref, f_ref, b_ref, df_ref, db_ref):
        row = lax.broadcasted_iota(I32, (128, 128), 0)
        col = lax.broadcasted_iota(I32, (128, 128), 1)
        tri_rev = (col >= row).astype(BF16)

        def step(m, carry):
            suffix, dbsum = carry
            n = nb - 1 - m
            r0 = pl.multiple_of(n * 128, 128)
            dcum = a_ref[pl.ds(r0, 128), :] - k_ref[pl.ds(r0, 128), :]
            dlf = _tri_dot(tri_rev, dcum) + suffix
            df = dlf * _sigmoid(-(f_ref[pl.ds(r0, 128), :] + b_ref[...]))
            df_ref[pl.ds(r0, 128), :] = df.astype(BF16)
            return dlf[0:1, :], dbsum + jnp.sum(df, axis=0, keepdims=True)

        _, dbsum = lax.fori_loop(0, nb, step, (jnp.zeros((1, 128), F32), jnp.zeros((1, 128), F32)))
        db_ref[...] = dbsum

    return pl.pallas_call(
        body, name=name, grid=(1,),
        in_specs=[_full_spec((s, 128))] * 3 + [_full_spec((1, 128))],
        out_specs=[_full_spec((s, 128)), _full_spec((1, 128))],
        out_shape=[jax.ShapeDtypeStruct((s, 128), BF16), jax.ShapeDtypeStruct((1, 128), F32)],
        compiler_params=_cparams(("arbitrary",)),
    )(dcq16, sum_q16, f_logit, b_f)


ATT_T = 256


def _head_lanes(rows):
    return lax.broadcasted_iota(I32, (rows, 128), 1) < HEAD_DIM


def _bf16_pieces(c):
    p0 = c.astype(BF16).astype(F32)
    r = c - p0
    p1 = r.astype(BF16).astype(F32)
    p2 = (r - p1).astype(BF16).astype(F32)
    return p0, p1, p2


def _col_reduce(x, op):
    rows = x.shape[0]
    while rows > 8:
        rows //= 2
        x = op(x[:rows], x[rows:])
    return jnp.max(x, axis=0, keepdims=True) if op is jnp.maximum else jnp.sum(x, axis=0, keepdims=True)


def _attn_prep(qkv, cqe, carry=None, *, name):
    s = qkv.shape[0]
    npair = HEADS // 2

    def body(q_ref, k_ref, v_ref, c_ref, qa_ref, ka_ref, vt_ref):
        rows = 128
        lane = lax.broadcasted_iota(I32, (rows, 128), 1)

        def chunk(n, _):
            r0 = pl.multiple_of(n * rows, rows)
            sl = pl.ds(r0, rows)
            qv = q_ref[sl, :].astype(F32) * ATT_SCALE
            kv = k_ref[sl, :].astype(F32)
            p0, p1, p2 = _bf16_pieces(pltpu.roll(c_ref[sl, :], HEAD_DIM, 1))
            for e in range(2):
                mine = (lane < HEAD_DIM) if e == 0 else (lane >= HEAD_DIM)
                base = HEAD_DIM * (1 - e)
                ones_hi = jnp.where((lane >= base + 3) & (lane < base + 6), 1.0, 0.0)
                ones_lo = jnp.where((lane >= base) & (lane < base + 3), 1.0, 0.0)
                qa = jnp.where(mine, qv, jnp.where(lane == base, p0, jnp.where(lane == base + 1, p1,
                               jnp.where(lane == base + 2, p2, ones_hi))))
                ka = jnp.where(mine, kv, jnp.where(lane == base + 3, -p0, jnp.where(lane == base + 4, -p1,
                               jnp.where(lane == base + 5, -p2, ones_lo))))
                qa_ref[e, sl, :] = qa.astype(BF16)
                ka_ref[e, sl, :] = ka.astype(BF16)
            vt_ref[0, :, sl] = v_ref[sl, :].astype(F32).T.astype(BF16)
            return 0

        lax.fori_loop(0, s // rows, chunk, 0)

    pair = pl.BlockSpec((2, s, 128), lambda hp: (hp, 0, 0))
    return _carry_call(
        body, carry, name=name, grid=(npair,),
        in_specs=[pl.BlockSpec((s, 128), lambda hp: (0, hp)), pl.BlockSpec((s, 128), lambda hp: (0, npair + hp)),
                  pl.BlockSpec((s, 128), lambda hp: (0, 2 * npair + hp)), pl.BlockSpec((s, 128), lambda hp: (0, hp))],
        out_specs=[pair, pair, pl.BlockSpec((1, 128, s), lambda hp: (hp, 0, 0))],
        out_shape=[jax.ShapeDtypeStruct((HEADS, s, 128), BF16), jax.ShapeDtypeStruct((HEADS, s, 128), BF16),
                   jax.ShapeDtypeStruct((npair, 128, s), BF16)],
        scratch_shapes=[], args=[qkv, qkv, qkv, cqe])


def _attn_fwd(qa, ka, vt, carry=None, *, name):
    s = qa.shape[1]
    t = 2 * ATT_T
    nq = s // t
    npair = HEADS // 2

    def body(qa_ref, ka_ref, vt_ref, o_ref, lse_ref):
        i = pl.program_id(1)
        krow = lax.broadcasted_iota(I32, (t, t), 0)
        qcol = lax.broadcasted_iota(I32, (t, t), 1)
        sub = lax.broadcasted_iota(I32, (128, t), 0)
        row8 = lax.broadcasted_iota(I32, (8, t), 0)
        qbs = (qa_ref[0], qa_ref[1])
        tk = t

        def step(j, carry, diag):
            c0 = pl.multiple_of(j * tk, tk)
            vtb = vt_ref[0, :, pl.ds(c0, tk)]
            sts = [_dot(ka_ref[e, pl.ds(c0, tk), :], qbs[e], NT) for e in range(2)]
            if diag:
                sts = [jnp.where(krow <= qcol, st, NEG) for st in sts]
            pts, stats = [], []
            for e in range(2):
                m, l, _ = carry[e]
                m_new = jnp.maximum(m, _col_reduce(sts[e], jnp.maximum))
                alpha = jnp.exp(m - m_new)
                pt = jnp.exp(sts[e] - m_new)
                stats.append((m_new, alpha, alpha * l + _col_reduce(pt, jnp.add)))
                pts.append(pt.astype(BF16))
            pvs = [_dot(vtb, pts[e], NN) for e in range(2)]
            return tuple((stats[e][0], stats[e][2], stats[e][1] * carry[e][2] + pvs[e]) for e in range(2))

        init = (jnp.full((1, t), NEG, F32), jnp.zeros((1, t), F32), jnp.zeros((128, t), F32))
        carry = lax.fori_loop(0, i, functools.partial(step, diag=False), (init, init))
        (m0, l0, acc0), (m1, l1, acc1) = step(i, carry, True)
        o_pair = jnp.where(sub < HEAD_DIM, acc0 / l0, acc1 / l1)
        o_ref[...] = o_pair.T.astype(BF16)
        lse_ref[0] = jnp.where(row8 == 0, m0 + jnp.log(l0), jnp.where(row8 == 1, m1 + jnp.log(l1), 0.0))

    return _carry_call(
        body, carry, name=name, grid=(npair, nq),
        in_specs=[pl.BlockSpec((2, t, 128), lambda hp, i: (hp, i, 0)), pl.BlockSpec((2, s, 128), lambda hp, i: (hp, 0, 0)),
                  pl.BlockSpec((1, 128, s), lambda hp, i: (hp, 0, 0))],
        out_specs=[pl.BlockSpec((t, 128), lambda hp, i: (i, hp)), pl.BlockSpec((1, 8, t), lambda hp, i: (hp, 0, i))],
        out_shape=[jax.ShapeDtypeStruct((s, HEADS * HEAD_DIM), BF16), jax.ShapeDtypeStruct((npair, 8, s), F32)],
        scratch_shapes=[], args=[qa, ka, vt])


def _attn_delta(do, o, carry=None, *, name):
    s = do.shape[0]

    def body(do_ref, o_ref, d_ref):
        prod = do_ref[...].astype(F32) * o_ref[...].astype(F32)
        row = lax.broadcasted_iota(I32, (8, 128), 0)
        lane = lax.broadcasted_iota(I32, (8, 128), 1)
        sel = ((row == 0) & (lane < HEAD_DIM) | (row == 1) & (lane >= HEAD_DIM)).astype(BF16)
        p0, p1, p2 = _bf16_pieces(prod)
        d_ref[0] = (_dot(sel, p0.astype(BF16), NT) + _dot(sel, p1.astype(BF16), NT)) + _dot(sel, p2.astype(BF16), NT)

    pair = pl.BlockSpec((s, 128), lambda hp: (0, hp))
    (delta3,), carried = _carry_call(
        body, carry, name=name, grid=(HEADS // 2,), in_specs=[pair, pair],
        out_specs=[pl.BlockSpec((1, 8, s), lambda hp: (hp, 0, 0))],
        out_shape=[jax.ShapeDtypeStruct((HEADS // 2, 8, s), F32)], scratch_shapes=[], args=[do, o])
    return delta3, carried


def _attn_bwd(qa, ka, qkv, do, lse3, delta3, carry=None, *, name):
    s = qa.shape[1]
    t = 2 * ATT_T
    nb = s // t
    npair = HEADS // 2

    def body(qa_ref, ka_ref, v_ref, do_ref, lse_ref, delta_ref, dq_ref, dk_ref, dv_ref, aux_ref, dcq_ref, dqt):
        hp = pl.program_id(0)
        first = _head_lanes(t)
        lane = lax.broadcasted_iota(I32, (t, 128), 1)
        dqt[...] = jnp.zeros_like(dqt)

        @pl.when(hp == 0)
        def _():
            aux_ref[...] = jnp.zeros_like(aux_ref)

        krow = lax.broadcasted_iota(I32, (t, t), 0)
        qcol = lax.broadcasted_iota(I32, (t, t), 1)

        def key_block(j, _):
            c0 = pl.multiple_of(j * t, t)
            vb = v_ref[pl.ds(c0, t), :]
            kbs = (ka_ref[0, pl.ds(c0, t), :], ka_ref[1, pl.ds(c0, t), :])
            kbts = tuple(kb.astype(F32).T.astype(BF16) for kb in kbs)
            vhs = (jnp.where(first, vb, jnp.zeros_like(vb)), jnp.where(first, jnp.zeros_like(vb), vb))

            def query_block(i, carry, diag):
                r0 = pl.multiple_of(i * t, t)
                dob = do_ref[pl.ds(r0, t), :]
                sts = [_dot(kbs[e], qa_ref[e, pl.ds(r0, t), :], NT) for e in range(2)]
                dpts = [_dot(vhs[e], dob, NT) for e in range(2)]
                ptbs, dsbs = [], []
                for e in range(2):
                    st = jnp.where(krow <= qcol, sts[e], NEG) if diag else sts[e]
                    pt = jnp.exp(st - lse_ref[0, e:e + 1, pl.ds(r0, t)])
                    dsbs.append((pt * (dpts[e] - delta_ref[0, e:e + 1, pl.ds(r0, t)])).astype(BF16))
                    ptbs.append(pt.astype(BF16))
                out = []
                for e in range(2):
                    dk_a, dv_a = carry[e]
                    dv_a = dv_a + _dot(ptbs[e], dob, NN)
                    dk_a = dk_a + _dot(dsbs[e], qa_ref[e, pl.ds(r0, t), :], NN)
                    dqt[e, :, pl.ds(r0, t)] += _dot(kbts[e], dsbs[e], NN)
                    out.append((dk_a, dv_a))
                return tuple(out)

            zero = jnp.zeros((t, 128), F32)
            carry = query_block(j, ((zero, zero), (zero, zero)), True)
            (dk0, dv0), (dk1, dv1) = lax.fori_loop(j + 1, nb, functools.partial(query_block, diag=False), carry)
            dk_ref[pl.ds(c0, t), :] = jnp.where(first, dk0, dk1).astype(BF16)
            dv_ref[pl.ds(c0, t), :] = jnp.where(first, dv0, dv1).astype(BF16)
            sum_q = jnp.where(lane == 2 * hp, dk0[:, HEAD_DIM + 3:HEAD_DIM + 4],
                              jnp.where(lane == 2 * hp + 1, dk1[:, 3:4], aux_ref[pl.ds(c0, t), :]))
            aux_ref[pl.ds(c0, t), :] = sum_q
            return 0

        lax.fori_loop(0, nb, key_block, 0)
        sub = lax.broadcasted_iota(I32, (128, s), 0)
        row8 = lax.broadcasted_iota(I32, (8, s), 0)
        dq_ref[...] = (jnp.where(sub < HEAD_DIM, dqt[0], dqt[1]) * ATT_SCALE).T.astype(BF16)
        dcq_ref[0] = jnp.where(row8 == 0, dqt[0, HEAD_DIM:HEAD_DIM + 1, :], jnp.where(row8 == 1, dqt[1, 0:1, :], 0.0))

    def pair_cols(off):
        return pl.BlockSpec((s, 128), lambda hp: (0, off + hp))

    heads = pl.BlockSpec((2, s, 128), lambda hp: (hp, 0, 0))
    rows = pl.BlockSpec((1, 8, s), lambda hp: (hp, 0, 0))
    wide = jax.ShapeDtypeStruct((s, HEADS * HEAD_DIM), BF16)
    return _carry_call(
        body, carry, name=name, grid=(npair,),
        in_specs=[heads, heads, pair_cols(2 * npair), pair_cols(0), rows, rows],
        out_specs=[pair_cols(0), pair_cols(0), pair_cols(0), pl.BlockSpec((s, 128), lambda hp: (0, 0)), rows],
        out_shape=[wide, wide, wide, jax.ShapeDtypeStruct((s, 128), F32), jax.ShapeDtypeStruct((npair, 8, s), F32)],
        scratch_shapes=[pltpu.VMEM((2, 128, s), F32)], args=[qa, ka, qkv, do, lse3, delta3])


def _adam_math(w, g, m, v):
    m = ADAM_B1 * m + (1.0 - ADAM_B1) * g
    v = ADAM_B2 * v + (1.0 - ADAM_B2) * (g * g)
    m_hat = m / (1.0 - ADAM_B1 ** ADAM_STEP)
    v_hat = v / (1.0 - ADAM_B2 ** ADAM_STEP)
    delta = -ADAM_LR * (m_hat / (jnp.sqrt(v_hat) + ADAM_EPS) + ADAM_WD * w)
    return delta, m, v


def _sum_pairs(keep, recv, pos, *, name):
    _, r, c = recv.shape
    tr = _row_tile(r, 512)

    def body(pos_ref, a_ref, b_ref, o32_ref, o16_ref):
        tot = a_ref[...].astype(F32) + b_ref[...].astype(F32)
        o16_ref[...] = tot.astype(BF16)

        @pl.when(pl.program_id(1) == 2 * pos_ref[0] + pos_ref[1])
        def _():
            o32_ref[...] = tot

    out = pl.BlockSpec((1, tr, c), lambda i, q, pos: (q, i, 0))
    grid_spec = pltpu.PrefetchScalarGridSpec(
        num_scalar_prefetch=1, grid=(r // tr, 4),
        in_specs=[pl.BlockSpec((1, tr, c), lambda i, q, pos: (2 * q + pos[2], i, 0)), out],
        out_specs=[pl.BlockSpec((1, tr, c), lambda i, q, pos: (0, i, 0)), out])
    return pl.pallas_call(
        body, name=name, grid_spec=grid_spec,
        out_shape=[jax.ShapeDtypeStruct((1, r, c), F32), jax.ShapeDtypeStruct((4, r, c), BF16)],
        compiler_params=_cparams(("arbitrary", "arbitrary")),
    )(pos, keep, recv)


def _adam_sharded(psum, recv, w, m, v, pos, *, name):
    r, c = w.shape
    tr = _row_tile(r, 320)

    def body(pos_ref, p_ref, r_ref, w_ref, m_ref, v_ref, g_ref, d_ref, mo_ref, vo_ref):
        g = p_ref[0] + r_ref[0].astype(F32) + r_ref[1].astype(F32) + r_ref[2].astype(F32)
        delta, mn, vn = _adam_math(w_ref[...], g, m_ref[...], v_ref[...])
        g_ref[...] = g
        d_ref[...] = delta
        mo_ref[...] = mn
        vo_ref[...] = vn

    row = pl.BlockSpec((tr, c), lambda i, pos: (i, 0))
    grid_spec = pltpu.PrefetchScalarGridSpec(
        num_scalar_prefetch=1, grid=(r // tr,),
        in_specs=[pl.BlockSpec((1, tr, c), lambda i, pos: (0, i, 0)),
                  pl.BlockSpec((3, tr, c), lambda i, pos: (0, i, 0)), row, row, row],
        out_specs=[row, row, row, row])
    o = jax.ShapeDtypeStruct((r, c), F32)
    return pl.pallas_call(
        body, name=name, grid_spec=grid_spec, out_shape=[o, o, o, o],
        compiler_params=_cparams(("parallel",)),
    )(pos, psum, recv, w, m, v)


def _adam_replicated(chip_sums, last, w, m, v, *, name):
    r = w.shape[0]

    def body(s_ref, l_ref, w_ref, m_ref, v_ref, g_ref, d_ref, mo_ref, vo_ref):
        g = (((s_ref[0] + s_ref[1]) + s_ref[2]) + s_ref[3]) + l_ref[...]
        delta, mn, vn = _adam_math(w_ref[...], g, m_ref[...], v_ref[...])
        g_ref[...] = g
        d_ref[...] = delta
        mo_ref[...] = mn
        vo_ref[...] = vn

    o = jax.ShapeDtypeStruct((r, 1024), F32)
    full = _full_spec((r, 1024))
    return pl.pallas_call(
        body, name=name, grid=(1,),
        in_specs=[_full_spec((4, r, 1024)), full, full, full, full], out_specs=[full] * 4, out_shape=[o] * 4,
        compiler_params=_cparams(("arbitrary",)),
    )(chip_sums, last, w, m, v)


ASM_OUT = 256
ASM_SRC = 304


def _w_in_row(r):
    return r if r < 2048 else (r + O_G - 2048 if r < 4096 else r - 2048)


def _assemble_wt_main(g, *, name):
    table = []
    for blk in range(MAIN_COLS // ASM_OUT):
        j, l0 = divmod(_w_in_row(blk * ASM_OUT), IN_SHARD)
        sb = l0 // ASM_SRC
        n_a = min(ASM_OUT, min(IN_SHARD, (sb + 1) * ASM_SRC) - l0)
        if n_a == ASM_OUT:
            nxt = (j, sb)
        elif l0 + n_a == IN_SHARD:
            nxt = (j + 1, 0)
        else:
            nxt = (j, sb + 1)
        table.append((j, sb, l0 - sb * ASM_SRC, n_a) + nxt)

    def body(tab_ref, a_ref, b_ref, o_ref):
        blk = pl.program_id(0)
        off, n_a = tab_ref[blk, 2], tab_ref[blk, 3]
        r = lax.broadcasted_iota(I32, (ASM_OUT, ASM_SRC), 0)
        k = lax.broadcasted_iota(I32, (ASM_OUT, ASM_SRC), 1)
        sel_a = ((k == r + off) & (r < n_a)).astype(BF16)
        sel_b = ((k == r - n_a) & (r >= n_a)).astype(BF16)
        o_ref[...] = (_dot(sel_a, a_ref[0], NN) + _dot(sel_b, b_ref[0], NN)).astype(BF16)

    src = lambda c: pl.BlockSpec((1, ASM_SRC, D_MODEL), lambda blk, tab: (tab[blk, c], tab[blk, c + 1], 0))
    grid_spec = pltpu.PrefetchScalarGridSpec(
        num_scalar_prefetch=1, grid=(len(table),), in_specs=[src(0), src(4)],
        out_specs=pl.BlockSpec((ASM_OUT, D_MODEL), lambda blk, tab: (blk, 0)))
    return pl.pallas_call(
        body, name=name, grid_spec=grid_spec, out_shape=jax.ShapeDtypeStruct((MAIN_COLS, D_MODEL), BF16),
        compiler_params=_cparams(("parallel",)),
    )(jnp.asarray(table, I32), g, g)


def _pair_sum_small(mine, theirs, *, name):
    def body(a_ref, b_ref, o_ref):
        o_ref[...] = a_ref[...] + b_ref[...]

    full = _full_spec(mine.shape)
    return pl.pallas_call(
        body, name=name, grid=(1,), in_specs=[full, full], out_specs=full,
        out_shape=jax.ShapeDtypeStruct(mine.shape, F32), compiler_params=_cparams(("arbitrary",)),
    )(mine, theirs)


ANY = pl.BlockSpec(memory_space=pl.ANY)
OTHER_CHIPS = ((1, 0), (0, 1), (1, 1))


class _Carry:
    def __init__(self, inputs, out_shapes, scratch, start, wait, aliases=None):
        self.inputs, self.out_shapes, self.scratch = list(inputs), list(out_shapes), list(scratch)
        self.start, self.wait, self.aliases = start, wait, dict(aliases or {})


def _carry_join(*carries):
    n_in = [len(c.inputs) for c in carries]
    n_out = [len(c.out_shapes) for c in carries]
    n_scr = [len(c.scratch) for c in carries]

    def split(refs, counts):
        out, k = [], 0
        for n in counts:
            out.append(refs[k:k + n])
            k += n
        return out

    def start(ins, outs, scr):
        for c, i, o, s in zip(carries, split(ins, n_in), split(outs, n_out), split(scr, n_scr)):
            c.start(i, o, s)

    def wait(ins, outs, scr):
        for c, i, o, s in zip(carries, split(ins, n_in), split(outs, n_out), split(scr, n_scr)):
            c.wait(i, o, s)

    aliases = {}
    for k, c in enumerate(carries):
        aliases.update({sum(n_in[:k]) + i: sum(n_out[:k]) + o for i, o in c.aliases.items()})
    joined = _Carry(sum((c.inputs for c in carries), []), sum((c.out_shapes for c in carries), []),
                    sum((c.scratch for c in carries), []), start, wait, aliases)
    joined.counts = n_out
    joined.split = lambda results: split(results, n_out)
    return joined


def _carried(body, carry, n_in, n_out, grid):
    if carry is None:
        return body
    ci, co, cs = len(carry.inputs), len(carry.out_shapes), len(carry.scratch)

    def wrapped(*refs):
        ins, cins = refs[:n_in], refs[n_in:n_in + ci]
        outs, couts = refs[n_in + ci:n_in + ci + n_out], refs[n_in + ci + n_out:n_in + ci + n_out + co]
        rest = refs[n_in + ci + n_out + co:]
        scratch, cscr = rest[:len(rest) - cs], rest[len(rest) - cs:]
        first, last = None, None
        for axis, size in enumerate(grid):
            f, l = pl.program_id(axis) == 0, pl.program_id(axis) == size - 1
            first = f if first is None else first & f
            last = l if last is None else last & l

        @pl.when(first)
        def _():
            carry.start(cins, couts, cscr)

        body(*ins, *outs, *scratch)

        @pl.when(last)
        def _():
            carry.wait(cins, couts, cscr)

    return wrapped


def _carry_call(body, carry, *, name, grid, in_specs, out_specs, out_shape, scratch_shapes, args, vmem=True,
                own_aliases=None):
    n_in, n_out = len(in_specs), len(out_specs)
    extra_in = [ANY] * len(carry.inputs) if carry else []
    extra_out = [ANY] * len(carry.out_shapes) if carry else []
    aliases = dict(own_aliases or {})
    if carry:
        aliases.update({n_in + i: n_out + o for i, o in carry.aliases.items()})
    out = pl.pallas_call(
        _carried(body, carry, n_in, n_out, grid), name=name, grid=grid,
        in_specs=list(in_specs) + extra_in, out_specs=list(out_specs) + extra_out,
        out_shape=list(out_shape) + (carry.out_shapes if carry else []),
        scratch_shapes=list(scratch_shapes) + (carry.scratch if carry else []),
        input_output_aliases=aliases,
        compiler_params=_cparams(("arbitrary",) * len(grid)) if vmem else None,
    )(*args, *(carry.inputs if carry else []))
    return list(out[:n_out]), list(out[n_out:])


def _run_carry(carry, *, name):
    return _carry_call(lambda: None, carry, name=name, grid=(1,), in_specs=[], out_specs=[], out_shape=[],
                       scratch_shapes=[], args=[], vmem=False)[1]


def _sems(n):
    return [pltpu.SemaphoreType.DMA((n,)), pltpu.SemaphoreType.DMA((n,))]


def _carry_gather1(shards):
    n = len(shards)

    def copies(x_refs, out_refs, scr, with_arrivals):
        send_sems, recv_sems, local_sems = scr
        x, y, c = lax.axis_index("x"), lax.axis_index("y"), lax.axis_index("c")
        peers = [(x, y, 1 - c)] + [(x ^ fx, y ^ fy, c) for fx, fy in OTHER_CHIPS]
        local, sends, arrivals = [], [], []
        for t, (x_ref, out_ref) in enumerate(zip(x_refs, out_refs)):
            local.append(pltpu.make_async_copy(x_ref, out_ref.at[4 * x + 2 * y + c], local_sems.at[t]))
            for k, (px, py, pc) in enumerate(peers):
                sems = dict(send_sem=send_sems.at[4 * t + k], recv_sem=recv_sems.at[4 * t + k],
                            device_id=(px, py, pc), device_id_type=MESH)
                sends.append(pltpu.make_async_remote_copy(src_ref=x_ref, dst_ref=out_ref.at[4 * x + 2 * y + c], **sems))
                if with_arrivals:
                    arrivals.append(
                        pltpu.make_async_remote_copy(src_ref=x_ref, dst_ref=out_ref.at[4 * px + 2 * py + pc], **sems))
        return local, sends, arrivals

    def start(x_refs, out_refs, scr):
        local, sends, _ = copies(x_refs, out_refs, scr, False)
        for cp in local + sends:
            cp.start()

    def wait(x_refs, out_refs, scr):
        local, sends, arrivals = copies(x_refs, out_refs, scr, True)
        for cp in arrivals:
            cp.wait_recv()
        for cp in sends:
            cp.wait_send()
        for cp in local:
            cp.wait()

    return _Carry(shards, [jax.ShapeDtypeStruct((N_DEV,) + a.shape, a.dtype) for a in shards],
                  _sems(4 * n) + [pltpu.SemaphoreType.DMA((n,))], start, wait)


def _carry_gather2(gathered):
    n = len(gathered)

    def copies(in_refs, g_refs, scr, with_arrivals):
        send_sems, recv_sems = scr
        x, y, c = lax.axis_index("x"), lax.axis_index("y"), lax.axis_index("c")
        sends, arrivals = [], []
        for t in range(n):
            for j, (fx, fy) in enumerate(OTHER_CHIPS):
                px, py = x ^ fx, y ^ fy
                sems = dict(send_sem=send_sems.at[3 * t + j], recv_sem=recv_sems.at[3 * t + j],
                            device_id=(x, y, 1 - c), device_id_type=MESH)
                mine, theirs = 4 * px + 2 * py + c, 4 * px + 2 * py + (1 - c)
                sends.append(pltpu.make_async_remote_copy(src_ref=in_refs[t].at[mine], dst_ref=g_refs[t].at[mine], **sems))
                if with_arrivals:
                    arrivals.append(pltpu.make_async_remote_copy(
                        src_ref=in_refs[t].at[mine], dst_ref=g_refs[t].at[theirs], **sems))
        return sends, arrivals

    def start(in_refs, g_refs, scr):
        for cp in copies(in_refs, g_refs, scr, False)[0]:
            cp.start()

    def wait(in_refs, g_refs, scr):
        sends, arrivals = copies(in_refs, g_refs, scr, True)
        for cp in arrivals:
            cp.wait_recv()
        for cp in sends:
            cp.wait_send()

    return _Carry(gathered, [jax.ShapeDtypeStruct(a.shape, a.dtype) for a in gathered], _sems(3 * n), start, wait,
                  aliases={t: t for t in range(n)})


def _allreduce_rows(x, *, name):
    def body(x_ref, o_ref, sib_ref, mine_ref, tab_ref, send_sems, recv_sems):
        x, y, c = lax.axis_index("x"), lax.axis_index("y"), lax.axis_index("c")
        swap = pltpu.make_async_remote_copy(src_ref=x_ref, dst_ref=sib_ref, send_sem=send_sems.at[0],
                                            recv_sem=recv_sems.at[0], device_id=(x, y, 1 - c), device_id_type=MESH)
        swap.start()
        swap.wait()
        mine_ref[...] = x_ref[...] + sib_ref[...]
        tab_ref[pl.ds(2 * x + y, 1)] = mine_ref[...][None]

        def copy(k, slot):
            fx, fy = OTHER_CHIPS[k]
            return pltpu.make_async_remote_copy(
                src_ref=mine_ref, dst_ref=tab_ref.at[slot], send_sem=send_sems.at[1 + k], recv_sem=recv_sems.at[1 + k],
                device_id=(x ^ fx, y ^ fy, c), device_id_type=MESH)

        for k in range(3):
            copy(k, 2 * x + y).start()
        for k, (fx, fy) in enumerate(OTHER_CHIPS):
            copy(k, 2 * (x ^ fx) + (y ^ fy)).wait()
        o_ref[...] = ((tab_ref[0] + tab_ref[1]) + tab_ref[2]) + tab_ref[3]

    vmem = pl.BlockSpec(memory_space=pltpu.VMEM)
    return pl.pallas_call(
        body, name=name, out_shape=jax.ShapeDtypeStruct(x.shape, F32), in_specs=[vmem], out_specs=vmem,
        scratch_shapes=[pltpu.VMEM(x.shape, F32), pltpu.VMEM(x.shape, F32), pltpu.VMEM((4,) + x.shape, F32)] + _sems(4),
    )(x)


def _allgather(shards, *, name):
    n = len(shards)

    def body(*refs):
        x_refs, out_refs = refs[:n], refs[n:2 * n]
        send_sems, recv_sems, local_sems = refs[2 * n:]
        x, y, c = lax.axis_index("x"), lax.axis_index("y"), lax.axis_index("c")
        me, sibling = (x, y, c), (x, y, 1 - c)
        chips = [(x ^ fx, y ^ fy) for fx, fy in OTHER_CHIPS]

        def copy(t, k, block, to, from_input=False):
            px, py, pc = block
            slab = out_refs[t].at[4 * px + 2 * py + pc]
            return pltpu.make_async_remote_copy(
                src_ref=x_refs[t] if from_input else slab, dst_ref=slab,
                send_sem=send_sems.at[7 * t + k], recv_sem=recv_sems.at[7 * t + k], device_id=to, device_id_type=MESH)

        mine = [pltpu.make_async_copy(x_refs[t], out_refs[t].at[4 * x + 2 * y + c], local_sems.at[t]) for t in range(n)]
        for cp in mine:
            cp.start()
        first = []
        for t in range(n):
            first.append(copy(t, 0, me, sibling, from_input=True))
            first += [copy(t, 1 + j, me, (*chip, c), from_input=True) for j, chip in enumerate(chips)]
        for cp in first:
            cp.start()
        passed = []
        for j, chip in enumerate(chips):
            for t in range(n):
                copy(t, 1 + j, (*chip, c), me).wait_recv()
                fwd = copy(t, 4 + j, (*chip, c), sibling)
                fwd.start()
                passed.append(fwd)
        for t in range(n):
            copy(t, 0, sibling, me).wait_recv()
            for j, chip in enumerate(chips):
                copy(t, 4 + j, (*chip, 1 - c), me).wait_recv()
        for cp in first + passed:
            cp.wait_send()
        for cp in mine:
            cp.wait()

    return pl.pallas_call(
        body, name=name, out_shape=[jax.ShapeDtypeStruct((N_DEV,) + a.shape, a.dtype) for a in shards],
        in_specs=[ANY] * n, out_specs=[ANY] * n,
        scratch_shapes=[pltpu.SemaphoreType.DMA((7 * n,)), pltpu.SemaphoreType.DMA((7 * n,)),
                        pltpu.SemaphoreType.DMA((n,))],
    )(*shards)


def _carry_sibling(slabs, small=None):
    n = len(slabs)
    extra = [] if small is None else [small]

    def copies(in_refs, out_refs, scr):
        send_sems, recv_sems = scr
        x, y, c = lax.axis_index("x"), lax.axis_index("y"), lax.axis_index("c")
        sibling = (x, y, 1 - c)
        out = []
        for t in range(n):
            for q in range(4):
                out.append(pltpu.make_async_remote_copy(
                    src_ref=in_refs[t].at[2 * q + (1 - c)], dst_ref=out_refs[t].at[q],
                    send_sem=send_sems.at[4 * t + q], recv_sem=recv_sems.at[4 * t + q],
                    device_id=sibling, device_id_type=MESH))
        if extra:
            out.append(pltpu.make_async_remote_copy(
                src_ref=in_refs[n], dst_ref=out_refs[n], send_sem=send_sems.at[4 * n], recv_sem=recv_sems.at[4 * n],
                device_id=sibling, device_id_type=MESH))
        return out

    def start(*refs):
        for cp in copies(*refs):
            cp.start()

    def wait(*refs):
        for cp in copies(*refs):
            cp.wait()

    return _Carry(list(slabs) + extra,
                  [jax.ShapeDtypeStruct((4,) + a.shape[1:], a.dtype) for a in slabs]
                  + [jax.ShapeDtypeStruct(a.shape, a.dtype) for a in extra], _sems(4 * n + 1), start, wait)


def _carry_chips(psums, small_sum=None):
    n = len(psums)
    table = small_sum is not None

    def copies(in_refs, out_refs, scr, arrivals):
        send_sems, recv_sems = scr[0], scr[1]
        x, y, c = lax.axis_index("x"), lax.axis_index("y"), lax.axis_index("c")
        out = []
        for k, (fx, fy) in enumerate(OTHER_CHIPS):
            px, py = x ^ fx, y ^ fy
            for t in range(n):
                out.append(pltpu.make_async_remote_copy(
                    src_ref=in_refs[t].at[2 * px + py], dst_ref=out_refs[t].at[k],
                    send_sem=send_sems.at[3 * t + k], recv_sem=recv_sems.at[3 * t + k],
                    device_id=(px, py, c), device_id_type=MESH))
            if table:
                slot = 2 * px + py if arrivals else 2 * x + y
                out.append(pltpu.make_async_remote_copy(
                    src_ref=in_refs[n], dst_ref=out_refs[n].at[slot], send_sem=send_sems.at[3 * n + k],
                    recv_sem=recv_sems.at[3 * n + k], device_id=(px, py, c), device_id_type=MESH))
        return out

    def own(in_refs, out_refs, scr):
        x, y = lax.axis_index("x"), lax.axis_index("y")
        return pltpu.make_async_copy(in_refs[n], out_refs[n].at[2 * x + y], scr[2])

    def start(in_refs, out_refs, scr):
        if table:
            own(in_refs, out_refs, scr).start()
        for cp in copies(in_refs, out_refs, scr, False):
            cp.start()

    def wait(in_refs, out_refs, scr):
        for cp in copies(in_refs, out_refs, scr, True):
            cp.wait()
        if table:
            own(in_refs, out_refs, scr).wait()

    out_shapes = [jax.ShapeDtypeStruct((3,) + a.shape[1:], a.dtype) for a in psums]
    if table:
        out_shapes.append(jax.ShapeDtypeStruct((4,) + small_sum.shape, F32))
    return _Carry(list(psums) + ([small_sum] if table else []), out_shapes,
                  _sems(3 * n + 3) + ([pltpu.SemaphoreType.DMA] if table else []), start, wait)


def _to_comm(name, kind, block, dtype=BF16):
    a = block[0]
    if kind == "cols":
        a = a.T
        if name == "w_in":
            a = jnp.pad(a, ((0, IN_SHARD_PAD - IN_SHARD), (0, 0)))
    return a if kind == "f32" else a.astype(dtype)


def _from_comm(name, kind, a):
    if kind == "cols":
        if name == "w_in":
            a = a[:IN_SHARD]
        a = a.T
    return a[None]


def _assemble_weights(g):
    out = {}
    if "w_in" in g:
        out["wt_main"] = _assemble_wt_main(g["w_in"], name="assemble_w_in")
        j, l0 = divmod(O_F, IN_SHARD)
        out["wt_f"] = jnp.pad(g["w_in"][j, l0:l0 + HEADS], ((0, 128 - HEADS), (0, 0)))
    square = dict(w_branch_a="w_a", w_branch_b="w_b", w_out="w_out", w_ple_gate="w_pg")
    for long, short in square.items():
        if long in g:
            out[short] = g[long].reshape(D_MODEL, D_MODEL)
    if "w_up" in g:
        out["wt_up"] = g["w_up"].reshape(2 * D_FF, D_MODEL)
    if "conv_w" in g:
        out["conv_w"] = g["conv_w"].transpose(1, 0, 2).reshape(3, 2 * D_FF)
    if "w_down" in g:
        out["w_down"] = g["w_down"].reshape(D_FF, D_MODEL)
    if "w_ple" in g:
        out["wt_ple"] = g["w_ple"].reshape(D_MODEL, PLE_DIM)
    return out


def _grad_slabs(gr):
    out = {}
    if "wt_main" in gr:
        gm, gf = gr["wt_main"], gr["wt_f"]
        segments = ((0, 2048, gm, 0), (2048, O_F, gm, 2048), (O_F, O_G, gf, -O_F), (O_G, IN_COLS, gm, 2048 - O_G))
        slabs = []
        for j in range(N_DEV):
            lo, hi = j * IN_SHARD, (j + 1) * IN_SHARD
            pieces = [src[max(lo, a) + shift:min(hi, b) + shift] for a, b, src, shift in segments if max(lo, a) < min(hi, b)]
            pieces.append(jnp.zeros((IN_SHARD_PAD - IN_SHARD, D_MODEL), gm.dtype))
            slabs.append(jnp.concatenate(pieces, axis=0))
        out["w_in"] = jnp.stack(slabs)
    rows = dict(w_a="w_branch_a", w_b="w_branch_b", w_out="w_out", wt_up="w_up", w_down="w_down", w_pg="w_ple_gate")
    for short, long in rows.items():
        if short in gr:
            out[long] = gr[short].reshape(N_DEV, -1, D_MODEL)
    if "conv_w" in gr:
        out["conv_w"] = gr["conv_w"].reshape(3, N_DEV, -1).transpose(1, 0, 2)
    if "wt_ple" in gr:
        out["w_ple"] = gr["wt_ple"].reshape(N_DEV, -1, PLE_DIM)
    return {k: v.astype(BF16) for k, v in out.items()}


def _rows(a, rows):
    flat = a.reshape(-1)
    return jnp.pad(flat, (0, rows * 1024 - flat.shape[0])).reshape(rows, 1024)


def _pack_small(parts):
    return jnp.concatenate([_rows(parts[n].astype(F32), r) for n, r in SMALL], axis=0)


def _small(packed, name, shape):
    off, r = SMALL_OFF[name]
    n = math.prod(shape)
    return packed[off:off + r].reshape(-1)[:n].reshape(shape)


class _Exchanges:
    W_S_ROWS = SMALL_OFF["gmlp_w_s"]

    def __init__(self, later, shards, pos):
        self.later, self.shards, self.pos = later, dict(zip(later, shards)), pos
        self.level1, self.slabs, self.from_sib, self.sums32, self.reduced, self.tables = {}, {}, {}, {}, {}, {}

    def gather1(self, names):
        carry = _carry_gather1([self.shards[n] for n in names])
        carry.names = names
        return carry

    def gather1_done(self, carry, results):
        self.level1.update(zip(carry.names, results))

    def gather2(self):
        return _carry_gather2([self.level1[n] for n in self.later])

    def weights(self, full):
        return _assemble_weights(dict(zip(self.later, full)))

    def sibling(self, grads):
        slabs = _grad_slabs(grads)
        self.slabs.update(slabs)
        carry = _carry_sibling(list(slabs.values()))
        carry.names = list(slabs)
        return carry

    def sibling_done(self, carry, results):
        self.from_sib.update(zip(carry.names, results))

    def chips(self, names, table=None):
        sums = {n: _sum_pairs(self.slabs[n], self.from_sib[n], self.pos, name="sum_sibling_" + n) for n in names}
        self.sums32.update({n: s32 for n, (s32, _) in sums.items()})
        carry = _carry_chips([s16 for _, s16 in sums.values()], None if table is None else self.table_part(table))
        carry.names, carry.table = list(names), table
        return carry

    def chips_done(self, carry, results):
        if carry.table is not None:
            *results, self.tables[carry.table] = results
        self.reduced.update({n: (self.sums32[n], r) for n, r in zip(carry.names, results)})

    def sibling_small(self, small_g):
        self.small_g = small_g
        return _carry_sibling([], small_g)

    def sibling_small_done(self, small_sib):
        self.small_chip = _pair_sum_small(self.small_g, small_sib, name="sum_sibling_small")

    def table_part(self, which):
        off, rows = self.W_S_ROWS
        if which == "w_s":
            return self.small_chip[off:off + rows]
        return jnp.concatenate([self.small_chip[:off], self.small_chip[off + rows:]], axis=0)

    def table(self):
        off = self.W_S_ROWS[0]
        rest = self.tables["rest"]
        return jnp.concatenate([rest[:, :off], self.tables["w_s"], rest[:, off:]], axis=1)


def _local_step(x, p, target, w, sm, ex=None):
    s = x.shape[0]
    mm = _matmul
    wt_main = w["wt_main"]
    conv_b = sm["conv_b"]
    bs_t = jnp.pad(sm["gmlp_b_s"].T, ((0, 0), (0, 128 - GROUPS)))
    b_f = jnp.pad(sm["b_f"], ((0, 0), (0, 128 - HEADS)))
    big = dict(tm=1024, tn=1024, tk=1024)
    whole_s = dict(tn=1024, tk=s)

    h = _rmsnorm_fwd(x, sm["norm_mix_g"], name="norm_mix")
    qkv_args = dict(mode="nt", out_dtype=BF16, name="in_qkv", n=3072, b_off=4, **big)
    f_logit = mm(h, w["wt_f"], mode="nt", out_dtype=F32, name="in_f", tm=1024, tk=1024)
    cqe = _forget_cumsum(f_logit, b_f, name="forget_cumsum")
    uvg = dict(mode="nt", out_dtype=F32, name="in_uvg", n=4096, **big)
    if ex is None:
        qkv = mm(h, wt_main, **qkv_args)
        (qa, ka, vt), _ = _attn_prep(qkv, cqe, name="attn_prep")
        (b, lse3), _ = _attn_fwd(qa, ka, vt, name="attn_fwd")
        zuvg = mm(h, wt_main, **uvg)
    else:
        groups = (["w_branch_a"], ["w_branch_b"], [n for n in ex.later if n not in ("w_branch_a", "w_branch_b")])
        carries = [ex.gather1(names) for names in groups]
        qkv, got0 = mm(h, wt_main, carry=carries[0], **qkv_args)
        (qa, ka, vt), got1 = _attn_prep(qkv, cqe, carries[1], name="attn_prep")
        (b, lse3), got2 = _attn_fwd(qa, ka, vt, carries[2], name="attn_fwd")
        for carry, got in zip(carries, (got0, got1, got2)):
            ex.gather1_done(carry, got)
        zuvg, full = mm(h, wt_main, carry=ex.gather2(), **uvg)
        w = {**w, **ex.weights(full)}
    a = _gmlp_fwd(zuvg, sm["gmlp_ln_g"], sm["gmlp_ln_b"], sm["gmlp_w_s"], bs_t, name="gmlp_fwd")
    wt_up, conv_w = w["wt_up"], w["conv_w"]
    ya = mm(a, w["w_a"], mode="nn", out_dtype=F32, name="branch_a", **big)
    yb = mm(b, w["w_b"], mode="nn", out_dtype=F32, name="branch_b", **big)
    merged = _merge_fwd(ya, yb, zuvg, name="merge_fwd")
    x1 = mm(merged, w["w_out"], mode="nn", out_dtype=F32, name="out_proj", add=x, **big)
    h2 = _rmsnorm_fwd(x1, sm["norm_ffn_g"], name="norm_ffn")
    up = mm(h2, wt_up, mode="nt", out_dtype=F32, name="up", tm=1024, tn=512, tk=1024)
    act = _convglu_fwd(up, conv_w, conv_b, name="convglu_fwd")
    x2 = mm(act, w["w_down"], mode="nn", out_dtype=F32, name="down", tm=1024, tn=1024, tk=1408, add=x1)
    h3 = _rmsnorm_fwd(x2, sm["norm_ple_g"], name="norm_ple")
    ple = mm(p, w["wt_ple"], mode="nt", out_dtype=F32, name="ple", tm=1024, tn=1024, tk=256)
    gp = mm(h3, w["w_pg"], mode="nn", out_dtype=F32, name="ple_gate", **big)
    x3 = _ple_fwd(x2, ple, gp, name="ple_fwd")

    loss, dx3, d_norm_final = _final_loss_bwd(x3, target, sm["norm_final_g"], name="loss_bwd")
    dple, dgp = _ple_bwd(dx3, ple, gp, name="ple_bwd")
    g_wt_ple = mm(dple, p, mode="tn", out_dtype=BF16, name="d_w_ple", tm=512, tn=256, tk=s)
    g_w_pg = mm(h3, dgp, mode="tn", out_dtype=BF16, name="d_w_pg", tm=256, **whole_s)
    dh3 = mm(dgp, w["w_pg"], mode="nt", out_dtype=F32, name="d_h3", **big)
    dx2, dx2b, d_norm_ple = _rmsnorm_bwd(dx3, dh3, x2, sm["norm_ple_g"], name="norm_ple_bwd")
    g_w_down = mm(act, dx2b, mode="tn", out_dtype=BF16, name="d_w_down", tm=256, **whole_s)
    dact_args = dict(mode="nt", out_dtype=BF16, name="d_act", tm=1024, tn=1408, tk=1024)
    if ex is None:
        dact = mm(dx2b, w["w_down"], **dact_args)
    else:
        early = ex.sibling(dict(w_pg=g_w_pg, wt_ple=g_wt_ple))
        dact, got = mm(dx2b, w["w_down"], carry=early, **dact_args)
        ex.sibling_done(early, got)
    dup_a, dup_g, dcw_a, dcw_g, dcb_a, dcb_g = _convglu_bwd(dact, up, conv_w, conv_b, name="convglu_bwd")
    g_wt_up = mm(dup_a, h2, mode="tn", out_dtype=BF16, name="d_w_up_a", tm=256, out_rows=2 * D_FF, **whole_s)
    g_wt_up = mm(dup_g, h2, mode="tn", out_dtype=BF16, name="d_w_up_g", tm=256, out_rows=2 * D_FF,
                 o_off=D_FF // 256, into=g_wt_up, **whole_s)
    dh2 = mm(dup_a, wt_up, mode="nn", out_dtype=F32, name="d_h2_a", tm=1024, tn=1024, tk=1408)
    dh2 = mm(dup_g, wt_up, mode="nn", out_dtype=F32, name="d_h2_g", tm=1024, tn=1024, tk=1408, b_off=2, add=dh2)
    dx1, dx1b, d_norm_ffn = _rmsnorm_bwd(dx2, dh2, x1, sm["norm_ffn_g"], name="norm_ffn_bwd")
    g_w_out = mm(merged, dx1b, mode="tn", out_dtype=BF16, name="d_w_out", tm=256, **whole_s)
    dmerged = mm(dx1b, w["w_out"], mode="nt", out_dtype=F32, name="d_merged", **big)
    dya, dyb, dga, dgb = _merge_bwd(dmerged, ya, yb, zuvg, name="merge_bwd")
    g_w_a = mm(a, dya, mode="tn", out_dtype=BF16, name="d_w_a", tm=256, **whole_s)
    g_w_b = mm(b, dyb, mode="tn", out_dtype=BF16, name="d_w_b", tm=256, **whole_s)
    da = mm(dya, w["w_a"], mode="nt", out_dtype=BF16, name="d_a", **big)
    db = mm(dyb, w["w_b"], mode="nt", out_dtype=BF16, name="d_b", **big)
    grads = dict(w_a=g_w_a, w_b=g_w_b, w_out=g_w_out, wt_up=g_wt_up, conv_w=jnp.concatenate([dcw_a, dcw_g], axis=1),
                 w_down=g_w_down, wt_ple=g_wt_ple, w_pg=g_w_pg)
    gmlp_args = (da, zuvg, sm["gmlp_ln_g"], sm["gmlp_ln_b"], sm["gmlp_w_s"], bs_t)
    if ex is None:
        (dzu, dzv, d_w_s, d_bs_t, d_ln_g, d_ln_b), _ = _gmlp_bwd(*gmlp_args, name="gmlp_bwd")
    else:
        rest = ex.sibling({k: v for k, v in grads.items() if k not in ("w_pg", "wt_ple")})
        early_chips = ex.chips(early.names)
        both = _carry_join(rest, early_chips)
        (dzu, dzv, d_w_s, d_bs_t, d_ln_g, d_ln_b), got = _gmlp_bwd(*gmlp_args, both, name="gmlp_bwd")
        got_rest, got_early = both.split(got)
        ex.sibling_done(rest, got_rest)
        ex.chips_done(early_chips, got_early)
    small = dict(norm_mix_g=jnp.zeros((1, D_MODEL), F32), b_f=jnp.zeros((1, HEADS), F32), gmlp_ln_g=d_ln_g,
                 gmlp_ln_b=d_ln_b, gmlp_w_s=d_w_s, gmlp_b_s=d_bs_t[:, :GROUPS].T, norm_ffn_g=d_norm_ffn,
                 conv_b=jnp.concatenate([dcb_a, dcb_g], axis=1), norm_ple_g=d_norm_ple, norm_final_g=d_norm_final)
    if ex is None:
        delta3, _ = _attn_delta(db, b, name="attn_delta")
        (dq, dk, dv, aux, dcq3), _ = _attn_bwd(qa, ka, qkv, db, lse3, delta3, name="attn_bwd")
    else:
        delta3, (small_sib,) = _attn_delta(db, b, ex.sibling_small(_pack_small(small)), name="attn_delta")
        ex.sibling_small_done(small_sib)
        main_chips = ex.chips(rest.names, table="rest")
        (dq, dk, dv, aux, dcq3), got = _attn_bwd(qa, ka, qkv, db, lse3, delta3, main_chips, name="attn_bwd")
        ex.chips_done(main_chips, got)
    dcq16 = jnp.pad(dcq3[:, :2, :].reshape(HEADS, s).T, ((0, 0), (0, 128 - HEADS)))
    dzf, d_b_f = _forget_bwd(dcq16, aux, f_logit, b_f, name="forget_bwd")
    dz = jnp.concatenate([dzu, dzv, dga, dgb, dq, dk, dv], axis=1)
    dw_main = dict(mode="tn", out_dtype=BF16, name="d_w_main", tm=512, **whole_s)
    if ex is None:
        g_wt_main = mm(dz, h, **dw_main)
    else:
        w_s_chips = ex.chips([], table="w_s")
        g_wt_main, got = mm(dz, h, carry=w_s_chips, **dw_main)
        ex.chips_done(w_s_chips, got)
    g_wt_f = mm(dzf, h, mode="tn", out_dtype=BF16, name="d_w_f", **whole_s)
    grads = dict(grads, wt_main=g_wt_main, wt_f=g_wt_f)
    dh_main = dict(mode="nn", out_dtype=F32, name="d_h_main", **big)
    if ex is None:
        dh = mm(dz, wt_main, **dh_main)
    else:
        w_in_sib = ex.sibling(dict(wt_main=g_wt_main, wt_f=g_wt_f))
        ex.sibling_done(w_in_sib, _run_carry(w_in_sib, name="exchange_sibling_w_in"))
        w_in_chips = ex.chips(w_in_sib.names)
        dh, got = mm(dz, wt_main, carry=w_in_chips, **dh_main)
        ex.chips_done(w_in_chips, got)
    dh = mm(dzf, w["wt_f"], mode="nn", out_dtype=F32, name="d_h_f", tm=1024, tn=1024, add=dh)
    dx0, _, d_norm_mix = _rmsnorm_bwd(dx1, dh, x, sm["norm_mix_g"], name="norm_mix_bwd")
    return loss, dx0, grads, dict(small, norm_mix_g=d_norm_mix, b_f=d_b_f[:, :HEADS])


def kernel(x, p, norm_mix_g, w_in, b_f, gmlp_ln_g, gmlp_ln_b, gmlp_w_s, gmlp_b_s, w_branch_a, w_branch_b, w_out, norm_ffn_g, w_up, conv_w, conv_b, w_down, norm_ple_g, w_ple, w_ple_gate, norm_final_g, loss_target, m_norm_mix_g, m_w_in, m_b_f, m_gmlp_ln_g, m_gmlp_ln_b, m_gmlp_w_s, m_gmlp_b_s, m_w_branch_a, m_w_branch_b, m_w_out, m_norm_ffn_g, m_w_up, m_conv_w, m_conv_b, m_w_down, m_norm_ple_g, m_w_ple, m_w_ple_gate, m_norm_final_g, v_norm_mix_g, v_w_in, v_b_f, v_gmlp_ln_g, v_gmlp_ln_b, v_gmlp_w_s, v_gmlp_b_s, v_w_branch_a, v_w_branch_b, v_w_out, v_norm_ffn_g, v_w_up, v_conv_w, v_conv_b, v_w_down, v_norm_ple_g, v_w_ple, v_w_ple_gate, v_norm_final_g):
    given = dict(locals())
    weights = {n: given[n] for n in WEIGHT_ORDER}
    mom_m = {n: given["m_" + n] for n in WEIGHT_ORDER}
    mom_v = {n: given["v_" + n] for n in WEIGHT_ORDER}
    pos = jnp.stack([lax.axis_index("x"), lax.axis_index("y"), lax.axis_index("c")]).astype(I32)
    names = [n for n, _ in SHARDED]
    kinds = dict(SHARDED)

    later = [n for n in names if n != "w_in"]

    first = _allgather([_to_comm("w_in", kinds["w_in"], weights["w_in"])], name="allgather_w_in")
    ex = _Exchanges(later, [_to_comm(n, kinds[n], weights[n]) for n in later], pos)

    sm = dict(norm_mix_g=norm_mix_g, b_f=b_f, gmlp_ln_g=gmlp_ln_g, gmlp_ln_b=gmlp_ln_b, gmlp_w_s=gmlp_w_s[0],
              gmlp_b_s=gmlp_b_s[0], norm_ffn_g=norm_ffn_g, conv_b=conv_b, norm_ple_g=norm_ple_g,
              norm_final_g=norm_final_g.reshape(1, D_MODEL))
    loss_part, dx0, grads, small = _local_step(
        x[0], p[0, 0], loss_target[0], _assemble_weights({"w_in": first[0]}), sm, ex)

    b_f_and_loss = jnp.concatenate([small["b_f"].reshape(-1), loss_part[0, :1]])
    last = _allreduce_rows(jnp.concatenate([_rows(small["norm_mix_g"], 8), _rows(b_f_and_loss, 8)], axis=0),
                           name="allreduce_last")
    loss = last[8, HEADS]
    small_last = jnp.pad(last, ((0, SMALL_ROWS - 16), (0, 0)))

    grad, delta, new_m, new_v = {}, {}, {}, {}
    for n in names:
        s32, r = ex.reduced[n]
        outs = _adam_sharded(s32, r, *[_to_comm(n, kinds[n], src[n], F32) for src in (weights, mom_m, mom_v)], pos,
                             name="adam_" + n)
        grad[n], delta[n], new_m[n], new_v[n] = [_from_comm(n, kinds[n], o) for o in outs]
    replicated = [n for n, _ in SMALL]
    rep = lambda src: _pack_small({n: src[n] for n in replicated})
    packed = _adam_replicated(ex.table(), small_last, rep(weights), rep(mom_m), rep(mom_v), name="adam_replicated")
    for out, pk in zip((grad, delta, new_m, new_v), packed):
        for n in replicated:
            out[n] = _small(pk, n, weights[n].shape)

    return (loss, dx0[None], *[grad[n] for n in WEIGHT_ORDER], *[delta[n] for n in WEIGHT_ORDER],
            *[new_m[n] for n in WEIGHT_ORDER], *[new_v[n] for n in WEIGHT_ORDER])
```

```python
import functools
import math

import jax
import jax.numpy as jnp
from jax import lax
from jax.experimental import pallas as pl
from jax.experimental.pallas import tpu as pltpu

F32 = jnp.float32
BF16 = jnp.bfloat16
I32 = jnp.int32

D_MODEL = 1024
GROUPS = 8
GDIM = 128
GBLOCK = 128
CHUNK = 64
HEADS = 16
HEAD_DIM = 64
D_FF = 2816
PLE_DIM = 256
EPS = 1e-6
N_DEV = 8
ATT_SCALE = HEAD_DIM ** -0.5
NEG = -1e30

ADAM_LR = 0.001
ADAM_B1 = 0.9
ADAM_B2 = 0.999
ADAM_EPS = 1e-08
ADAM_WD = 0.01
ADAM_STEP = 10

V7X_VMEM_LIMIT = 48 * 1024 * 1024
MESH = pl.DeviceIdType.MESH

O_F = 2 * 1024 + 3 * 1024
O_G = O_F + HEADS
IN_COLS = O_G + 2 * D_MODEL
MAIN_COLS = IN_COLS - HEADS
IN_SHARD = IN_COLS // N_DEV
IN_SHARD_PAD = 912

SHARDED = (("w_in", "cols"), ("w_branch_a", "rows"), ("w_branch_b", "rows"), ("w_out", "rows"), ("w_up", "cols"),
           ("conv_w", "f32"), ("w_down", "rows"), ("w_ple", "cols"), ("w_ple_gate", "rows"))

SMALL = (("norm_mix_g", 8), ("b_f", 8), ("gmlp_ln_g", 8), ("gmlp_ln_b", 8), ("gmlp_w_s", 128), ("gmlp_b_s", 8),
         ("norm_ffn_g", 8), ("conv_b", 8), ("norm_ple_g", 8), ("norm_final_g", 8))
SMALL_OFF = {}
_o = 0
for _n, _r in SMALL:
    SMALL_OFF[_n] = (_o, _r)
    _o += _r
SMALL_ROWS = _o

WEIGHT_ORDER = ("norm_mix_g", "w_in", "b_f", "gmlp_ln_g", "gmlp_ln_b", "gmlp_w_s", "gmlp_b_s", "w_branch_a",
                "w_branch_b", "w_out", "norm_ffn_g", "w_up", "conv_w", "conv_b", "w_down", "norm_ple_g", "w_ple",
                "w_ple_gate", "norm_final_g")


def _cparams(sem):
    return pltpu.CompilerParams(dimension_semantics=sem, vmem_limit_bytes=V7X_VMEM_LIMIT)


def _gelu(x):
    c = math.sqrt(2.0 / math.pi)
    return 0.5 * x * (1.0 + jnp.tanh(c * (x + 0.044715 * x * x * x)))


def _gelu_and_grad(x):
    c = math.sqrt(2.0 / math.pi)
    t = jnp.tanh(c * (x + 0.044715 * x * x * x))
    g = 0.5 * x * (1.0 + t)
    dg = 0.5 * (1.0 + t) + 0.5 * x * (1.0 - t * t) * (c * (1.0 + 3.0 * 0.044715 * x * x))
    return g, dg


def _sigmoid(x):
    return 1.0 / (1.0 + jnp.exp(-x))


def _dot(a, b, dims):
    return lax.dot_general(a, b, (dims, ((), ())), preferred_element_type=F32)


NN = ((1,), (0,))
NT = ((1,), (1,))
TN = ((0,), (0,))


def _row_tile(rows, most):
    best = None
    for t in range(16, min(rows, most) + 1, 16):
        if rows % t == 0:
            best = t
    return best if best is not None else rows


def _matmul(a, b, *, mode, out_dtype, name, tm=512, tn=512, tk=512, add=None, n=None, b_off=0,
            out_rows=None, o_off=0, into=None, carry=None):
    if mode == "tn":
        kdim, m = a.shape
    else:
        m, kdim = a.shape
    if n is None:
        n = b.shape[0] if mode == "nt" else b.shape[1]
    tm, tn, tk = min(tm, m), min(tn, n), min(tk, kdim)
    assert m % tm == 0 and n % tn == 0 and kdim % tk == 0, (name, m, n, kdim, tm, tn, tk)
    nk = kdim // tk
    dims = {"nn": NN, "nt": NT, "tn": TN}[mode]

    def finish(r, add_ref, o_ref):
        if add_ref is not None:
            r = add_ref[...].astype(F32) + r
        o_ref[...] = r.astype(out_dtype)

    def body(*refs):
        refs = list(refs)
        a_ref, b_ref = refs[:2]
        add_ref = refs[2] if add is not None else None
        o_ref = refs[2 + (add is not None) + (into is not None)]
        part = _dot(a_ref[...].astype(BF16), b_ref[...].astype(BF16), dims)
        if nk == 1:
            finish(part, add_ref, o_ref)
            return
        acc_ref = refs[-1]
        k = pl.program_id(2)

        @pl.when(k == 0)
        def _():
            acc_ref[...] = part

        @pl.when((k > 0) & (k < nk - 1))
        def _():
            acc_ref[...] += part

        @pl.when(k == nk - 1)
        def _():
            finish(acc_ref[...] + part, add_ref, o_ref)

    a_spec = pl.BlockSpec((tk, tm), lambda i, j, k: (k, i)) if mode == "tn" else pl.BlockSpec((tm, tk), lambda i, j, k: (i, k))
    if mode == "nt":
        b_spec = pl.BlockSpec((tn, tk), lambda i, j, k: (j + b_off, k))
    else:
        b_spec = pl.BlockSpec((tk, tn), lambda i, j, k: (k + b_off, j))
    o_spec = pl.BlockSpec((tm, tn), lambda i, j, k: (i + o_off, j))
    in_specs = [a_spec, b_spec] + ([pl.BlockSpec((tm, tn), lambda i, j, k: (i, j))] if add is not None else [])
    args = (a, b) + ((add,) if add is not None else ())
    aliases = {}
    if into is not None:
        aliases = {len(args): 0}
        in_specs.append(pl.BlockSpec(memory_space=pl.ANY))
        args += (into,)
    (out,), carried = _carry_call(
        body, carry, name=name, grid=(m // tm, n // tn, nk), in_specs=in_specs, out_specs=[o_spec],
        out_shape=[jax.ShapeDtypeStruct((m if out_rows is None else out_rows, n), out_dtype)],
        scratch_shapes=[pltpu.VMEM((tm, tn), F32)] if nk > 1 else [], args=args, own_aliases=aliases)
    return out if carry is None else (out, carried)


def _row_spec(tr, width, col_block=0):
    return pl.BlockSpec((tr, width), lambda i: (i, col_block))


def _full_spec(shape):
    return pl.BlockSpec(shape, lambda i: tuple(0 for _ in shape))


def _rmsnorm_fwd(x, g, *, name, tr=256):
    s, d = x.shape

    def body(x_ref, g_ref, o_ref):
        xv = x_ref[...]
        r = lax.rsqrt(jnp.mean(xv * xv, axis=-1, keepdims=True) + EPS)
        o_ref[...] = ((xv * r) * g_ref[...]).astype(BF16)

    return pl.pallas_call(
        body, name=name, grid=(s // tr,),
        in_specs=[_row_spec(tr, d), _full_spec((1, d))], out_specs=_row_spec(tr, d),
        out_shape=jax.ShapeDtypeStruct((s, d), BF16), compiler_params=_cparams(("parallel",)),
    )(x, g)


def _matmul_rmsnorm_bwd(a_parts, b, dres, x, g, *, mode, tk, name, extra=None, tm=512, carry=None):
    s, d = x.shape
    spans, lo = [], 0
    for a in a_parts:
        spans.append((lo, lo + a.shape[1] // tk))
        lo = spans[-1][1]
    n_main, total = lo, lo + (extra is not None)
    n_parts = len(a_parts)

    def body(*refs):
        a_refs, b_ref = refs[:n_parts], refs[n_parts]
        k0 = n_parts + 1
        ax_ref, bx_ref = (refs[k0], refs[k0 + 1]) if extra is not None else (None, None)
        k0 += 2 * (extra is not None)
        dres_ref, x_ref, g_ref, dx_ref, dxb_ref, dg_ref, acc_ref = refs[k0:k0 + 7]
        i, kk = pl.program_id(0), pl.program_id(1)

        def accumulate(part, first):
            if first:
                @pl.when(kk == 0)
                def _():
                    acc_ref[...] = part

                @pl.when(kk > 0)
                def _():
                    acc_ref[...] += part
            else:
                acc_ref[...] += part

        for p, (a_ref, (lo_p, hi_p)) in enumerate(zip(a_refs, spans)):
            @pl.when((kk >= lo_p) & (kk < hi_p))
            def _(a_ref=a_ref, lo_p=lo_p):
                accumulate(_dot(a_ref[...].astype(BF16), b_ref[...].astype(BF16), NN if mode == "nn" else NT), lo_p == 0)

        if extra is not None:
            @pl.when(kk == n_main)
            def _():
                accumulate(_dot(ax_ref[...].astype(BF16), bx_ref[...].astype(BF16), NN), False)

        @pl.when(kk == total - 1)
        def _():
            dhv = acc_ref[...]
            xv = x_ref[...]
            r = lax.rsqrt(jnp.mean(xv * xv, axis=-1, keepdims=True) + EPS)
            xhat = xv * r
            dxhat = dhv * g_ref[...]
            dx = dres_ref[...] + r * (dxhat - xhat * jnp.mean(dxhat * xhat, axis=-1, keepdims=True))
            dx_ref[...] = dx
            dxb_ref[...] = dx.astype(BF16)
            dgp = jnp.sum(dhv * xhat, axis=0, keepdims=True)

            @pl.when(i == 0)
            def _():
                dg_ref[...] = dgp

            @pl.when(i > 0)
            def _():
                dg_ref[...] += dgp

    a_specs = [pl.BlockSpec((tm, tk), lambda i, kk, lo_p=lo_p, hi_p=hi_p: (i, jnp.clip(kk - lo_p, 0, hi_p - lo_p - 1)))
               for lo_p, hi_p in spans]
    step = lambda kk: jnp.minimum(kk, n_main - 1)
    b_spec = (pl.BlockSpec((tk, d), lambda i, kk: (step(kk), 0)) if mode == "nn"
              else pl.BlockSpec((d, tk), lambda i, kk: (0, step(kk))))
    rows = pl.BlockSpec((tm, d), lambda i, kk: (i, 0))
    one = pl.BlockSpec((1, d), lambda i, kk: (0, 0))
    x_specs, x_args = [], []
    if extra is not None:
        kx = extra[0].shape[1]
        x_specs = [pl.BlockSpec((tm, kx), lambda i, kk: (i, 0)), pl.BlockSpec((kx, d), lambda i, kk: (0, 0))]
        x_args = list(extra)
    (dx, dxb, dg), carried = _carry_call(
        body, carry, name=name, grid=(s // tm, total),
        in_specs=a_specs + [b_spec] + x_specs + [rows, rows, one], out_specs=[rows, rows, one],
        out_shape=[jax.ShapeDtypeStruct((s, d), F32), jax.ShapeDtypeStruct((s, d), BF16), jax.ShapeDtypeStruct((1, d), F32)],
        scratch_shapes=[pltpu.VMEM((tm, d), F32)], args=list(a_parts) + [b] + x_args + [dres, x, g])
    return (dx, dxb, dg), carried


def _grad_w_parts(a_parts, b, *, name, tm=512, carry=None):
    s, width = a_parts[0].shape
    per, n = width // tm, b.shape[1]

    def body(*refs):
        a_refs, b_ref, o_ref = refs[:len(a_parts)], refs[len(a_parts)], refs[len(a_parts) + 1]
        i = pl.program_id(0)
        for p, a_ref in enumerate(a_refs):
            @pl.when(i // per == p)
            def _(a_ref=a_ref):
                o_ref[...] = _dot(a_ref[...].astype(BF16), b_ref[...].astype(BF16), TN).astype(BF16)

    a_specs = [pl.BlockSpec((s, tm), lambda i, p=p: (0, jnp.clip(i - p * per, 0, per - 1))) for p in range(len(a_parts))]
    (out,), carried = _carry_call(
        body, carry, name=name, grid=(len(a_parts) * per,),
        in_specs=a_specs + [pl.BlockSpec((s, n), lambda i: (0, 0))], out_specs=[pl.BlockSpec((tm, n), lambda i: (i, 0))],
        out_shape=[jax.ShapeDtypeStruct((len(a_parts) * width, n), BF16)], scratch_shapes=[], args=list(a_parts) + [b])
    return out, carried


def _final_loss_bwd(x3, target, g, *, name, tr=256):
    s, d = x3.shape

    def body(x_ref, t_ref, g_ref, loss_ref, dx_ref, dg_ref):
        i = pl.program_id(0)
        xv = x_ref[...]
        r = lax.rsqrt(jnp.mean(xv * xv, axis=-1, keepdims=True) + EPS)
        xhat = xv * r
        diff = xhat * g_ref[...] - t_ref[...]
        lp = jnp.zeros((1, 128), F32) + (0.5 / d) * jnp.sum(diff * diff)
        dy = diff * (1.0 / d)
        dxhat = dy * g_ref[...]
        dx_ref[...] = r * (dxhat - xhat * jnp.mean(dxhat * xhat, axis=-1, keepdims=True))
        dgp = jnp.sum(dy * xhat, axis=0, keepdims=True)

        @pl.when(i == 0)
        def _():
            dg_ref[...] = dgp
            loss_ref[...] = lp

        @pl.when(i > 0)
        def _():
            dg_ref[...] += dgp
            loss_ref[...] += lp

    return pl.pallas_call(
        body, name=name, grid=(s // tr,),
        in_specs=[_row_spec(tr, d), _row_spec(tr, d), _full_spec((1, d))],
        out_specs=[_full_spec((1, 128)), _row_spec(tr, d), _full_spec((1, d))],
        out_shape=[jax.ShapeDtypeStruct((1, 128), F32), jax.ShapeDtypeStruct((s, d), F32),
                   jax.ShapeDtypeStruct((1, d), F32)],
        compiler_params=_cparams(("arbitrary",)),
    )(x3, target, g)


def _merge_fwd(ya, yb, zuvg, *, name, tr=256):
    s, d = ya.shape

    def body(ya_ref, yb_ref, ga_ref, gb_ref, o_ref):
        o_ref[...] = (_sigmoid(ga_ref[...]) * ya_ref[...] + _sigmoid(gb_ref[...]) * yb_ref[...]).astype(BF16)

    return pl.pallas_call(
        body, name=name, grid=(s // tr,),
        in_specs=[_row_spec(tr, d), _row_spec(tr, d), _row_spec(tr, d, 2), _row_spec(tr, d, 3)],
        out_specs=_row_spec(tr, d),
        out_shape=jax.ShapeDtypeStruct((s, d), BF16), compiler_params=_cparams(("parallel",)),
    )(ya, yb, zuvg, zuvg)


def _merge_bwd(dm, ya, yb, zuvg, *, name, tr=256):
    s, d = ya.shape

    def body(dm_ref, ya_ref, yb_ref, ga_ref, gb_ref, dya_ref, dyb_ref, dga_ref, dgb_ref):
        dmv = dm_ref[...]
        sa = _sigmoid(ga_ref[...])
        sb = _sigmoid(gb_ref[...])
        dya_ref[...] = (dmv * sa).astype(BF16)
        dyb_ref[...] = (dmv * sb).astype(BF16)
        dga_ref[...] = (dmv * ya_ref[...] * (sa * (1.0 - sa))).astype(BF16)
        dgb_ref[...] = (dmv * yb_ref[...] * (sb * (1.0 - sb))).astype(BF16)

    o = jax.ShapeDtypeStruct((s, d), BF16)
    return pl.pallas_call(
        body, name=name, grid=(s // tr,),
        in_specs=[_row_spec(tr, d)] * 3 + [_row_spec(tr, d, 2), _row_spec(tr, d, 3)], out_specs=[_row_spec(tr, d)] * 4,
        out_shape=[o, o, o, o], compiler_params=_cparams(("parallel",)),
    )(dm, ya, yb, zuvg, zuvg)


def _ple_fwd(x2, ple, gp, *, name, tr=256):
    s, d = x2.shape

    def body(x_ref, ple_ref, gp_ref, o_ref):
        o_ref[...] = x_ref[...] + ple_ref[...] * _sigmoid(gp_ref[...])

    return pl.pallas_call(
        body, name=name, grid=(s // tr,),
        in_specs=[_row_spec(tr, d)] * 3, out_specs=_row_spec(tr, d),
        out_shape=jax.ShapeDtypeStruct((s, d), F32), compiler_params=_cparams(("parallel",)),
    )(x2, ple, gp)


def _ple_bwd(dx3, ple, gp, *, name, tr=256):
    s, d = dx3.shape

    def body(dx_ref, ple_ref, gp_ref, dple_ref, dgp_ref):
        sg = _sigmoid(gp_ref[...])
        dxv = dx_ref[...]
        dple_ref[...] = (dxv * sg).astype(BF16)
        dgp_ref[...] = (dxv * ple_ref[...] * (sg * (1.0 - sg))).astype(BF16)

    o = jax.ShapeDtypeStruct((s, d), BF16)
    return pl.pallas_call(
        body, name=name, grid=(s // tr,),
        in_specs=[_row_spec(tr, d)] * 3, out_specs=[_row_spec(tr, d)] * 2,
        out_shape=[o, o], compiler_params=_cparams(("parallel",)),
    )(dx3, ple, gp)


def _masked_ws(ws_ref, g):
    row = lax.broadcasted_iota(I32, (GBLOCK, GBLOCK), 0)
    col = lax.broadcasted_iota(I32, (GBLOCK, GBLOCK), 1)
    keep = (col // CHUNK) <= (row // CHUNK)
    return jnp.where(keep, ws_ref[g], 0.0), keep


def _layernorm_parts(zv):
    mu = jnp.mean(zv, axis=-1, keepdims=True)
    xc = zv - mu
    rs = lax.rsqrt(jnp.mean(xc * xc, axis=-1, keepdims=True) + EPS)
    return xc * rs, rs


def _gmlp_fwd(zuvg, ln_g, ln_b, w_s, bs_t, *, name):
    s, w = zuvg.shape[0], GROUPS * GDIM

    def body(zu_ref, zv_ref, lng_ref, lnb_ref, ws_ref, bs_ref, a_ref):
        zu = _gelu(zu_ref[...])
        zv = _gelu(zv_ref[...])
        xhat, _ = _layernorm_parts(zv)
        vln = (xhat * lng_ref[...] + lnb_ref[...]).astype(BF16)
        for g in range(GROUPS):
            wm, _ = _masked_ws(ws_ref, g)
            mixed = _dot(wm.astype(BF16), vln[:, g * GDIM:(g + 1) * GDIM], NN) + bs_ref[:, g:g + 1]
            a_ref[:, g * GDIM:(g + 1) * GDIM] = (zu[:, g * GDIM:(g + 1) * GDIM] * mixed).astype(BF16)

    return pl.pallas_call(
        body, name=name, grid=(s // GBLOCK,),
        in_specs=[_row_spec(GBLOCK, w, 0), _row_spec(GBLOCK, w, 1), _full_spec((1, w)), _full_spec((1, w)),
                  _full_spec((GROUPS, GBLOCK, GBLOCK)), _full_spec((GBLOCK, 128))],
        out_specs=_row_spec(GBLOCK, w),
        out_shape=jax.ShapeDtypeStruct((s, w), BF16), compiler_params=_cparams(("parallel",)),
    )(zuvg, zuvg, ln_g, ln_b, w_s, bs_t)


def _gmlp_bwd(da, zuvg, ln_g, ln_b, w_s, bs_t, carry=None, *, name):
    s, w = zuvg.shape[0], GROUPS * GDIM

    def body(da_ref, zu_ref, zv_ref, lng_ref, lnb_ref, ws_ref, bs_ref,
             dzu_ref, dzv_ref, dws_ref, dbs_ref, dlng_ref, dlnb_ref, dvln_ref):
        i = pl.program_id(0)
        zu, dzu_g = _gelu_and_grad(zu_ref[...])
        zv, dzv_g = _gelu_and_grad(zv_ref[...])
        xhat, rs = _layernorm_parts(zv)
        vln = (xhat * lng_ref[...] + lnb_ref[...]).astype(BF16)
        dav = da_ref[...].astype(F32)
        lane = lax.broadcasted_iota(I32, (GBLOCK, 128), 1)
        dbs = jnp.zeros((GBLOCK, 128), F32)

        @pl.when(i == 0)
        def _():
            dws_ref[...] = jnp.zeros_like(dws_ref)

        for g in range(GROUPS):
            sl = slice(g * GDIM, (g + 1) * GDIM)
            wm, keep = _masked_ws(ws_ref, g)
            wmb = wm.astype(BF16)
            vg = vln[:, sl]
            mixed = _dot(wmb, vg, NN) + bs_ref[:, g:g + 1]
            dag = dav[:, sl]
            dzu_ref[:, sl] = (dag * mixed * dzu_g[:, sl]).astype(BF16)
            dmix = dag * zu[:, sl]
            dmb = dmix.astype(BF16)
            dws_ref[g] += jnp.where(keep, _dot(dmb, vg, NT), 0.0)
            dbs = jnp.where(lane == g, jnp.sum(dmix, axis=1, keepdims=True), dbs)
            dvln_ref[:, sl] = _dot(wmb, dmb, TN)
        dvln = dvln_ref[...]
        dxhat = dvln * lng_ref[...]
        dzv = rs * (dxhat - jnp.mean(dxhat, axis=-1, keepdims=True)
                    - xhat * jnp.mean(dxhat * xhat, axis=-1, keepdims=True))
        dzv_ref[...] = (dzv * dzv_g).astype(BF16)
        dlng = jnp.sum(dvln * xhat, axis=0, keepdims=True)
        dlnb = jnp.sum(dvln, axis=0, keepdims=True)

        @pl.when(i == 0)
        def _():
            dbs_ref[...] = dbs
            dlng_ref[...] = dlng
            dlnb_ref[...] = dlnb

        @pl.when(i > 0)
        def _():
            dbs_ref[...] += dbs
            dlng_ref[...] += dlng
            dlnb_ref[...] += dlnb

    return _carry_call(
        body, carry, name=name, grid=(s // GBLOCK,),
        in_specs=[_row_spec(GBLOCK, w), _row_spec(GBLOCK, w, 0), _row_spec(GBLOCK, w, 1), _full_spec((1, w)),
                  _full_spec((1, w)), _full_spec((GROUPS, GBLOCK, GBLOCK)), _full_spec((GBLOCK, 128))],
        out_specs=[_row_spec(GBLOCK, w), _row_spec(GBLOCK, w), _full_spec((GROUPS, GBLOCK, GBLOCK)),
                   _full_spec((GBLOCK, 128)), _full_spec((1, w)), _full_spec((1, w))],
        out_shape=[jax.ShapeDtypeStruct((s, w), BF16), jax.ShapeDtypeStruct((s, w), BF16),
                   jax.ShapeDtypeStruct((GROUPS, GBLOCK, GBLOCK), F32), jax.ShapeDtypeStruct((GBLOCK, 128), F32),
                   jax.ShapeDtypeStruct((1, w), F32), jax.ShapeDtypeStruct((1, w), F32)],
        scratch_shapes=[pltpu.VMEM((GBLOCK, w), F32)], args=[da, zuvg, zuvg, ln_g, ln_b, w_s, bs_t])


def _shift_down(u, k):
    row = lax.broadcasted_iota(I32, u.shape, 0)
    return jnp.where(row >= k, pltpu.roll(u, k, 0), 0.0)


def _shift_up(u, k):
    s = u.shape[0]
    row = lax.broadcasted_iota(I32, u.shape, 0)
    return jnp.where(row < s - k, pltpu.roll(u, s - k, 0), 0.0)


def _conv(u, w_ref, b_ref):
    return b_ref[...] + w_ref[0:1, :] * _shift_down(u, 2) + w_ref[1:2, :] * _shift_down(u, 1) + w_ref[2:3, :] * u


def _conv_specs(s, f, tc):
    nc = f // tc
    half = lambda rows: [pl.BlockSpec((rows, tc), lambda j: (0, j)), pl.BlockSpec((rows, tc), lambda j: (0, nc + j))]
    return half(s), half(3), half(1)


def _convglu_fwd(up, conv_w, conv_b, *, name, tc=256):
    s, f = up.shape[0], up.shape[1] // 2
    up_specs, w_specs, b_specs = _conv_specs(s, f, tc)

    def body(ua_ref, ug_ref, wa_ref, wg_ref, ba_ref, bg_ref, o_ref):
        ca = _conv(ua_ref[...], wa_ref, ba_ref)
        cg = _conv(ug_ref[...], wg_ref, bg_ref)
        o_ref[...] = (_gelu(ca) * cg).astype(BF16)

    return pl.pallas_call(
        body, name=name, grid=(f // tc,),
        in_specs=up_specs + w_specs + b_specs, out_specs=up_specs[0],
        out_shape=jax.ShapeDtypeStruct((s, f), BF16), compiler_params=_cparams(("parallel",)),
    )(up, up, conv_w, conv_w, conv_b, conv_b)


def _convglu_bwd(dact, up, conv_w, conv_b, *, name, tc=256):
    s, f = up.shape[0], up.shape[1] // 2
    up_specs, w_specs, b_specs = _conv_specs(s, f, tc)

    def half(dc, taps, w_ref, du_ref, dw_ref, db_ref):
        db_ref[...] = jnp.sum(dc, axis=0, keepdims=True)
        for k in range(3):
            dw_ref[k:k + 1, :] = jnp.sum(dc * taps[k], axis=0, keepdims=True)
        du = w_ref[2:3, :] * dc + w_ref[1:2, :] * _shift_up(dc, 1) + w_ref[0:1, :] * _shift_up(dc, 2)
        du_ref[...] = du.astype(BF16)

    def body(d_ref, ua_ref, ug_ref, wa_ref, wg_ref, ba_ref, bg_ref,
             dua_ref, dug_ref, dwa_ref, dwg_ref, dba_ref, dbg_ref):
        taps_a = (_shift_down(ua_ref[...], 2), _shift_down(ua_ref[...], 1), ua_ref[...])
        taps_g = (_shift_down(ug_ref[...], 2), _shift_down(ug_ref[...], 1), ug_ref[...])
        conv = lambda taps, w_ref, b_ref: b_ref[...] + w_ref[0:1, :] * taps[0] + w_ref[1:2, :] * taps[1] + w_ref[2:3, :] * taps[2]
        ca = conv(taps_a, wa_ref, ba_ref)
        cg = conv(taps_g, wg_ref, bg_ref)
        ga, dga = _gelu_and_grad(ca)
        dv = d_ref[...].astype(F32)
        half(dv * cg * dga, taps_a, wa_ref, dua_ref, dwa_ref, dba_ref)
        half(dv * ga, taps_g, wg_ref, dug_ref, dwg_ref, dbg_ref)

    col, w3, b1 = up_specs[0], w_specs[0], b_specs[0]
    return pl.pallas_call(
        body, name=name, grid=(f // tc,),
        in_specs=[col] + up_specs + w_specs + b_specs, out_specs=[col, col, w3, w3, b1, b1],
        out_shape=[jax.ShapeDtypeStruct((s, f), BF16), jax.ShapeDtypeStruct((s, f), BF16),
                   jax.ShapeDtypeStruct((3, f), F32), jax.ShapeDtypeStruct((3, f), F32),
                   jax.ShapeDtypeStruct((1, f), F32), jax.ShapeDtypeStruct((1, f), F32)],
        compiler_params=_cparams(("parallel",)),
    )(dact, up, up, conv_w, conv_w, conv_b, conv_b)


def _tri_dot(tri, x):
    b0 = x.astype(BF16)
    r1 = x - b0.astype(F32)
    b1 = r1.astype(BF16)
    b2 = (r1 - b1.astype(F32)).astype(BF16)
    return _dot(tri, b0, NN) + _dot(tri, b1, NN) + _dot(tri, b2, NN)


def _log_sigmoid(x):
    return jnp.minimum(x, 0.0) - jnp.log(1.0 + jnp.exp(-jnp.abs(x)))


def _expand_heads(col16, rows):
    src = lax.broadcasted_iota(I32, (128, HEADS * HEAD_DIM), 0)
    dst = lax.broadcasted_iota(I32, (128, HEADS * HEAD_DIM), 1) // HEAD_DIM
    spread = (src == dst).astype(BF16)
    p0, p1, p2 = _bf16_pieces(col16)
    return (_dot(p0.astype(BF16), spread, NN) + _dot(p1.astype(BF16), spread, NN)) + _dot(p2.astype(BF16), spread, NN)


def _forget_cumsum(f_logit, b_f, *, name):
    s = f_logit.shape[0]
    nb = s // 128

    def body(f_ref, b_ref, cqe_ref):
        row = lax.broadcasted_iota(I32, (128, 128), 0)
        col = lax.broadcasted_iota(I32, (128, 128), 1)
        tri = (col <= row).astype(BF16)

        def step(n, carry):
            r0 = pl.multiple_of(n * 128, 128)
            lf = _log_sigmoid(f_ref[pl.ds(r0, 128), :] + b_ref[...])
            cum = _tri_dot(tri, lf) + carry
            cqe_ref[pl.ds(r0, 128), :] = _expand_heads(cum, 128)
            return cum[127:128, :]

        lax.fori_loop(0, nb, step, jnp.zeros((1, 128), F32))

    return pl.pallas_call(
        body, name=name, grid=(1,),
        in_specs=[_full_spec((s, 128)), _full_spec((1, 128))],
        out_specs=_full_spec((s, HEADS * HEAD_DIM)),
        out_shape=jax.ShapeDtypeStruct((s, HEADS * HEAD_DIM), F32),
        compiler_params=_cparams(("arbitrary",)),
    )(f_logit, b_f)


def _forget_bwd(dcq16, sum_q16, f_logit, b_f, *, name):
    s = f_logit.shape[0]
    nb = s // 128

    def body(a_ref, k_ref, f_ref, b_ref, df_ref, db_ref):
        row = lax.broadcasted_iota(I32, (128, 128), 0)
        col = lax.broadcasted_iota(I32, (128, 128), 1)
        tri_rev = (col >= row).astype(BF16)

        def step(m, carry):
            suffix, dbsum = carry
            n = nb - 1 - m
            r0 = pl.multiple_of(n * 128, 128)
            dcum = a_ref[pl.ds(r0, 128), :] - k_ref[pl.ds(r0, 128), :]
            dlf = _tri_dot(tri_rev, dcum) + suffix
            df = dlf * _sigmoid(-(f_ref[pl.ds(r0, 128), :] + b_ref[...]))
            df_ref[pl.ds(r0, 128), :] = df.astype(BF16)
            return dlf[0:1, :], dbsum + jnp.sum(df, axis=0, keepdims=True)

        _, dbsum = lax.fori_loop(0, nb, step, (jnp.zeros((1, 128), F32), jnp.zeros((1, 128), F32)))
        db_ref[...] = dbsum

    return pl.pallas_call(
        body, name=name, grid=(1,),
        in_specs=[_full_spec((s, 128))] * 3 + [_full_spec((1, 128))],
        out_specs=[_full_spec((s, 128)), _full_spec((1, 128))],
        out_shape=[jax.ShapeDtypeStruct((s, 128), BF16), jax.ShapeDtypeStruct((1, 128), F32)],
        compiler_params=_cparams(("arbitrary",)),
    )(dcq16, sum_q16, f_logit, b_f)


ATT_T = 256


def _head_lanes(rows):
    return lax.broadcasted_iota(I32, (rows, 128), 1) < HEAD_DIM


def _bf16_pieces(c):
    p0 = c.astype(BF16).astype(F32)
    r = c - p0
    p1 = r.astype(BF16).astype(F32)
    p2 = (r - p1).astype(BF16).astype(F32)
    return p0, p1, p2


def _col_reduce(x, op):
    rows = x.shape[0]
    while rows > 8:
        rows //= 2
        x = op(x[:rows], x[rows:])
    return jnp.max(x, axis=0, keepdims=True) if op is jnp.maximum else jnp.sum(x, axis=0, keepdims=True)


def _attn_prep(qkv, cqe, carry=None, *, name):
    s = qkv.shape[0]
    npair = HEADS // 2

    def body(q_ref, k_ref, v_ref, c_ref, qa_ref, ka_ref, vt_ref):
        rows = 128
        lane = lax.broadcasted_iota(I32, (rows, 128), 1)

        def chunk(n, _):
            r0 = pl.multiple_of(n * rows, rows)
            sl = pl.ds(r0, rows)
            qv = q_ref[sl, :].astype(F32) * ATT_SCALE
            kv = k_ref[sl, :].astype(F32)
            p0, p1, p2 = _bf16_pieces(pltpu.roll(c_ref[sl, :], HEAD_DIM, 1))
            for e in range(2):
                mine = (lane < HEAD_DIM) if e == 0 else (lane >= HEAD_DIM)
                base = HEAD_DIM * (1 - e)
                ones_hi = jnp.where((lane >= base + 3) & (lane < base + 6), 1.0, 0.0)
                ones_lo = jnp.where((lane >= base) & (lane < base + 3), 1.0, 0.0)
                qa = jnp.where(mine, qv, jnp.where(lane == base, p0, jnp.where(lane == base + 1, p1,
                               jnp.where(lane == base + 2, p2, ones_hi))))
                ka = jnp.where(mine, kv, jnp.where(lane == base + 3, -p0, jnp.where(lane == base + 4, -p1,
                               jnp.where(lane == base + 5, -p2, ones_lo))))
                qa_ref[e, sl, :] = qa.astype(BF16)
                ka_ref[e, sl, :] = ka.astype(BF16)
            vt_ref[0, :, sl] = v_ref[sl, :].astype(F32).T.astype(BF16)
            return 0

        lax.fori_loop(0, s // rows, chunk, 0)

    pair = pl.BlockSpec((2, s, 128), lambda hp: (hp, 0, 0))
    return _carry_call(
        body, carry, name=name, grid=(npair,),
        in_specs=[pl.BlockSpec((s, 128), lambda hp: (0, hp)), pl.BlockSpec((s, 128), lambda hp: (0, npair + hp)),
                  pl.BlockSpec((s, 128), lambda hp: (0, 2 * npair + hp)), pl.BlockSpec((s, 128), lambda hp: (0, hp))],
        out_specs=[pair, pair, pl.BlockSpec((1, 128, s), lambda hp: (hp, 0, 0))],
        out_shape=[jax.ShapeDtypeStruct((HEADS, s, 128), BF16), jax.ShapeDtypeStruct((HEADS, s, 128), BF16),
                   jax.ShapeDtypeStruct((npair, 128, s), BF16)],
        scratch_shapes=[], args=[qkv, qkv, qkv, cqe])


def _attn_fwd(qa, ka, vt, carry=None, *, name):
    s = qa.shape[1]
    t = 2 * ATT_T
    nq = s // t
    npair = HEADS // 2

    def body(qa_ref, ka_ref, vt_ref, o_ref, lse_ref):
        i = pl.program_id(1)
        krow = lax.broadcasted_iota(I32, (t, t), 0)
        qcol = lax.broadcasted_iota(I32, (t, t), 1)
        sub = lax.broadcasted_iota(I32, (128, t), 0)
        row8 = lax.broadcasted_iota(I32, (8, t), 0)
        qbs = (qa_ref[0], qa_ref[1])
        tk = t

        def step(j, carry, diag):
            c0 = pl.multiple_of(j * tk, tk)
            vtb = vt_ref[0, :, pl.ds(c0, tk)]
            sts = [_dot(ka_ref[e, pl.ds(c0, tk), :], qbs[e], NT) for e in range(2)]
            if diag:
                sts = [jnp.where(krow <= qcol, st, NEG) for st in sts]
            pts, stats = [], []
            for e in range(2):
                m, l, _ = carry[e]
                m_new = jnp.maximum(m, _col_reduce(sts[e], jnp.maximum))
                alpha = jnp.exp(m - m_new)
                pt = jnp.exp(sts[e] - m_new)
                stats.append((m_new, alpha, alpha * l + _col_reduce(pt, jnp.add)))
                pts.append(pt.astype(BF16))
            pvs = [_dot(vtb, pts[e], NN) for e in range(2)]
            return tuple((stats[e][0], stats[e][2], stats[e][1] * carry[e][2] + pvs[e]) for e in range(2))

        init = (jnp.full((1, t), NEG, F32), jnp.zeros((1, t), F32), jnp.zeros((128, t), F32))
        carry = lax.fori_loop(0, i, functools.partial(step, diag=False), (init, init))
        (m0, l0, acc0), (m1, l1, acc1) = step(i, carry, True)
        o_pair = jnp.where(sub < HEAD_DIM, acc0 / l0, acc1 / l1)
        o_ref[...] = o_pair.T.astype(BF16)
        lse_ref[0] = jnp.where(row8 == 0, m0 + jnp.log(l0), jnp.where(row8 == 1, m1 + jnp.log(l1), 0.0))

    return _carry_call(
        body, carry, name=name, grid=(npair, nq),
        in_specs=[pl.BlockSpec((2, t, 128), lambda hp, i: (hp, i, 0)), pl.BlockSpec((2, s, 128), lambda hp, i: (hp, 0, 0)),
                  pl.BlockSpec((1, 128, s), lambda hp, i: (hp, 0, 0))],
        out_specs=[pl.BlockSpec((t, 128), lambda hp, i: (i, hp)), pl.BlockSpec((1, 8, t), lambda hp, i: (hp, 0, i))],
        out_shape=[jax.ShapeDtypeStruct((s, HEADS * HEAD_DIM), BF16), jax.ShapeDtypeStruct((npair, 8, s), F32)],
        scratch_shapes=[], args=[qa, ka, vt])


def _attn_delta(do, o, carry=None, *, name):
    s = do.shape[0]

    def body(do_ref, o_ref, d_ref):
        prod = do_ref[...].astype(F32) * o_ref[...].astype(F32)
        row = lax.broadcasted_iota(I32, (8, 128), 0)
        lane = lax.broadcasted_iota(I32, (8, 128), 1)
        sel = ((row == 0) & (lane < HEAD_DIM) | (row == 1) & (lane >= HEAD_DIM)).astype(BF16)
        p0, p1, p2 = _bf16_pieces(prod)
        d_ref[0] = (_dot(sel, p0.astype(BF16), NT) + _dot(sel, p1.astype(BF16), NT)) + _dot(sel, p2.astype(BF16), NT)

    pair = pl.BlockSpec((s, 128), lambda hp: (0, hp))
    (delta3,), carried = _carry_call(
        body, carry, name=name, grid=(HEADS // 2,), in_specs=[pair, pair],
        out_specs=[pl.BlockSpec((1, 8, s), lambda hp: (hp, 0, 0))],
        out_shape=[jax.ShapeDtypeStruct((HEADS // 2, 8, s), F32)], scratch_shapes=[], args=[do, o])
    return delta3, carried


def _attn_bwd(qa, ka, qkv, do, lse3, delta3, carry=None, *, name):
    s = qa.shape[1]
    t = 2 * ATT_T
    nb = s // t
    npair = HEADS // 2

    def body(qa_ref, ka_ref, v_ref, do_ref, lse_ref, delta_ref, dq_ref, dk_ref, dv_ref, aux_ref, dcq_ref, dqt):
        hp = pl.program_id(0)
        first = _head_lanes(t)
        lane = lax.broadcasted_iota(I32, (t, 128), 1)
        dqt[...] = jnp.zeros_like(dqt)

        @pl.when(hp == 0)
        def _():
            aux_ref[...] = jnp.zeros_like(aux_ref)

        krow = lax.broadcasted_iota(I32, (t, t), 0)
        qcol = lax.broadcasted_iota(I32, (t, t), 1)

        def key_block(j, _):
            c0 = pl.multiple_of(j * t, t)
            vb = v_ref[pl.ds(c0, t), :]
            kbs = (ka_ref[0, pl.ds(c0, t), :], ka_ref[1, pl.ds(c0, t), :])
            kbts = tuple(kb.astype(F32).T.astype(BF16) for kb in kbs)
            vhs = (jnp.where(first, vb, jnp.zeros_like(vb)), jnp.where(first, jnp.zeros_like(vb), vb))

            def query_block(i, carry, diag):
                r0 = pl.multiple_of(i * t, t)
                dob = do_ref[pl.ds(r0, t), :]
                sts = [_dot(kbs[e], qa_ref[e, pl.ds(r0, t), :], NT) for e in range(2)]
                dpts = [_dot(vhs[e], dob, NT) for e in range(2)]
                ptbs, dsbs = [], []
                for e in range(2):
                    st = jnp.where(krow <= qcol, sts[e], NEG) if diag else sts[e]
                    pt = jnp.exp(st - lse_ref[0, e:e + 1, pl.ds(r0, t)])
                    dsbs.append((pt * (dpts[e] - delta_ref[0, e:e + 1, pl.ds(r0, t)])).astype(BF16))
                    ptbs.append(pt.astype(BF16))
                out = []
                for e in range(2):
                    dk_a, dv_a = carry[e]
                    dv_a = dv_a + _dot(ptbs[e], dob, NN)
                    dk_a = dk_a + _dot(dsbs[e], qa_ref[e, pl.ds(r0, t), :], NN)
                    dqt[e, :, pl.ds(r0, t)] += _dot(kbts[e], dsbs[e], NN)
                    out.append((dk_a, dv_a))
                return tuple(out)

            zero = jnp.zeros((t, 128), F32)
            carry = query_block(j, ((zero, zero), (zero, zero)), True)
            (dk0, dv0), (dk1, dv1) = lax.fori_loop(j + 1, nb, functools.partial(query_block, diag=False), carry)
            dk_ref[pl.ds(c0, t), :] = jnp.where(first, dk0, dk1).astype(BF16)
            dv_ref[pl.ds(c0, t), :] = jnp.where(first, dv0, dv1).astype(BF16)
            sum_q = jnp.where(lane == 2 * hp, dk0[:, HEAD_DIM + 3:HEAD_DIM + 4],
                              jnp.where(lane == 2 * hp + 1, dk1[:, 3:4], aux_ref[pl.ds(c0, t), :]))
            aux_ref[pl.ds(c0, t), :] = sum_q
            return 0

        lax.fori_loop(0, nb, key_block, 0)
        sub = lax.broadcasted_iota(I32, (128, s), 0)
        row8 = lax.broadcasted_iota(I32, (8, s), 0)
        dq_ref[...] = (jnp.where(sub < HEAD_DIM, dqt[0], dqt[1]) * ATT_SCALE).T.astype(BF16)
        dcq_ref[0] = jnp.where(row8 == 0, dqt[0, HEAD_DIM:HEAD_DIM + 1, :], jnp.where(row8 == 1, dqt[1, 0:1, :], 0.0))

    def pair_cols(off):
        return pl.BlockSpec((s, 128), lambda hp: (0, off + hp))

    heads = pl.BlockSpec((2, s, 128), lambda hp: (hp, 0, 0))
    rows = pl.BlockSpec((1, 8, s), lambda hp: (hp, 0, 0))
    wide = jax.ShapeDtypeStruct((s, HEADS * HEAD_DIM), BF16)
    return _carry_call(
        body, carry, name=name, grid=(npair,),
        in_specs=[heads, heads, pair_cols(2 * npair), pair_cols(0), rows, rows],
        out_specs=[pair_cols(0), pair_cols(0), pair_cols(0), pl.BlockSpec((s, 128), lambda hp: (0, 0)), rows],
        out_shape=[wide, wide, wide, jax.ShapeDtypeStruct((s, 128), F32), jax.ShapeDtypeStruct((npair, 8, s), F32)],
        scratch_shapes=[pltpu.VMEM((2, 128, s), F32)], args=[qa, ka, qkv, do, lse3, delta3])


def _adam_math(w, g, m, v):
    m = ADAM_B1 * m + (1.0 - ADAM_B1) * g
    v = ADAM_B2 * v + (1.0 - ADAM_B2) * (g * g)
    m_hat = m / (1.0 - ADAM_B1 ** ADAM_STEP)
    v_hat = v / (1.0 - ADAM_B2 ** ADAM_STEP)
    delta = -ADAM_LR * (m_hat / (jnp.sqrt(v_hat) + ADAM_EPS) + ADAM_WD * w)
    return delta, m, v


def _sum_pairs(keep, recv, pos, *, name):
    _, r, c = recv.shape
    tr = _row_tile(r, 512)

    def body(pos_ref, a_ref, b_ref, o32_ref, o16_ref):
        tot = a_ref[...].astype(F32) + b_ref[...].astype(F32)
        o16_ref[...] = tot.astype(BF16)

        @pl.when(pl.program_id(1) == 2 * pos_ref[0] + pos_ref[1])
        def _():
            o32_ref[...] = tot

    out = pl.BlockSpec((1, tr, c), lambda i, q, pos: (q, i, 0))
    grid_spec = pltpu.PrefetchScalarGridSpec(
        num_scalar_prefetch=1, grid=(r // tr, 4),
        in_specs=[pl.BlockSpec((1, tr, c), lambda i, q, pos: (2 * q + pos[2], i, 0)), out],
        out_specs=[pl.BlockSpec((1, tr, c), lambda i, q, pos: (0, i, 0)), out])
    return pl.pallas_call(
        body, name=name, grid_spec=grid_spec,
        out_shape=[jax.ShapeDtypeStruct((1, r, c), F32), jax.ShapeDtypeStruct((4, r, c), BF16)],
        compiler_params=_cparams(("arbitrary", "arbitrary")),
    )(pos, keep, recv)


def _adam_sharded(psum, recv, w, m, v, pos, *, name):
    r, c = w.shape
    tr = _row_tile(r, 320)

    def body(pos_ref, p_ref, r_ref, w_ref, m_ref, v_ref, g_ref, d_ref, mo_ref, vo_ref):
        g = p_ref[0] + r_ref[0].astype(F32) + r_ref[1].astype(F32) + r_ref[2].astype(F32)
        delta, mn, vn = _adam_math(w_ref[...], g, m_ref[...], v_ref[...])
        g_ref[...] = g
        d_ref[...] = delta
        mo_ref[...] = mn
        vo_ref[...] = vn

    row = pl.BlockSpec((tr, c), lambda i, pos: (i, 0))
    grid_spec = pltpu.PrefetchScalarGridSpec(
        num_scalar_prefetch=1, grid=(r // tr,),
        in_specs=[pl.BlockSpec((1, tr, c), lambda i, pos: (0, i, 0)),
                  pl.BlockSpec((3, tr, c), lambda i, pos: (0, i, 0)), row, row, row],
        out_specs=[row, row, row, row])
    o = jax.ShapeDtypeStruct((r, c), F32)
    return pl.pallas_call(
        body, name=name, grid_spec=grid_spec, out_shape=[o, o, o, o],
        compiler_params=_cparams(("parallel",)),
    )(pos, psum, recv, w, m, v)


def _adam_replicated(chip_sums, last, w, m, v, *, name):
    r = w.shape[0]

    def body(s_ref, l_ref, w_ref, m_ref, v_ref, g_ref, d_ref, mo_ref, vo_ref):
        g = (((s_ref[0] + s_ref[1]) + s_ref[2]) + s_ref[3]) + l_ref[...]
        delta, mn, vn = _adam_math(w_ref[...], g, m_ref[...], v_ref[...])
        g_ref[...] = g
        d_ref[...] = delta
        mo_ref[...] = mn
        vo_ref[...] = vn

    o = jax.ShapeDtypeStruct((r, 1024), F32)
    full = _full_spec((r, 1024))
    return pl.pallas_call(
        body, name=name, grid=(1,),
        in_specs=[_full_spec((4, r, 1024)), full, full, full, full], out_specs=[full] * 4, out_shape=[o] * 4,
        compiler_params=_cparams(("arbitrary",)),
    )(chip_sums, last, w, m, v)


ASM_OUT = 256
ASM_SRC = 304


def _w_in_row(r):
    return r if r < 2048 else (r + O_G - 2048 if r < 4096 else r - 2048)


def _assemble_wt_main(g, *, name):
    table = []
    for blk in range(MAIN_COLS // ASM_OUT):
        j, l0 = divmod(_w_in_row(blk * ASM_OUT), IN_SHARD)
        sb = l0 // ASM_SRC
        n_a = min(ASM_OUT, min(IN_SHARD, (sb + 1) * ASM_SRC) - l0)
        if n_a == ASM_OUT:
            nxt = (j, sb)
        elif l0 + n_a == IN_SHARD:
            nxt = (j + 1, 0)
        else:
            nxt = (j, sb + 1)
        table.append((j, sb, l0 - sb * ASM_SRC, n_a) + nxt)

    def body(tab_ref, a_ref, b_ref, o_ref):
        blk = pl.program_id(0)
        off, n_a = tab_ref[blk, 2], tab_ref[blk, 3]
        r = lax.broadcasted_iota(I32, (ASM_OUT, ASM_SRC), 0)
        k = lax.broadcasted_iota(I32, (ASM_OUT, ASM_SRC), 1)
        sel_a = ((k == r + off) & (r < n_a)).astype(BF16)
        sel_b = ((k == r - n_a) & (r >= n_a)).astype(BF16)
        o_ref[...] = (_dot(sel_a, a_ref[0], NN) + _dot(sel_b, b_ref[0], NN)).astype(BF16)

    src = lambda c: pl.BlockSpec((1, ASM_SRC, D_MODEL), lambda blk, tab: (tab[blk, c], tab[blk, c + 1], 0))
    grid_spec = pltpu.PrefetchScalarGridSpec(
        num_scalar_prefetch=1, grid=(len(table),), in_specs=[src(0), src(4)],
        out_specs=pl.BlockSpec((ASM_OUT, D_MODEL), lambda blk, tab: (blk, 0)))
    return pl.pallas_call(
        body, name=name, grid_spec=grid_spec, out_shape=jax.ShapeDtypeStruct((MAIN_COLS, D_MODEL), BF16),
        compiler_params=_cparams(("parallel",)),
    )(jnp.asarray(table, I32), g, g)


def _pair_sum_small(mine, theirs, *, name):
    def body(a_ref, b_ref, o_ref):
        o_ref[...] = a_ref[...] + b_ref[...]

    full = _full_spec(mine.shape)
    return pl.pallas_call(
        body, name=name, grid=(1,), in_specs=[full, full], out_specs=full,
        out_shape=jax.ShapeDtypeStruct(mine.shape, F32), compiler_params=_cparams(("arbitrary",)),
    )(mine, theirs)


ANY = pl.BlockSpec(memory_space=pl.ANY)
OTHER_CHIPS = ((1, 0), (0, 1), (1, 1))


class _Carry:
    def __init__(self, inputs, out_shapes, scratch, start, wait, aliases=None):
        self.inputs, self.out_shapes, self.scratch = list(inputs), list(out_shapes), list(scratch)
        self.start, self.wait, self.aliases = start, wait, dict(aliases or {})


def _carry_join(*carries):
    n_in = [len(c.inputs) for c in carries]
    n_out = [len(c.out_shapes) for c in carries]
    n_scr = [len(c.scratch) for c in carries]

    def split(refs, counts):
        out, k = [], 0
        for n in counts:
            out.append(refs[k:k + n])
            k += n
        return out

    def start(ins, outs, scr):
        for c, i, o, s in zip(carries, split(ins, n_in), split(outs, n_out), split(scr, n_scr)):
            c.start(i, o, s)

    def wait(ins, outs, scr):
        for c, i, o, s in zip(carries, split(ins, n_in), split(outs, n_out), split(scr, n_scr)):
            c.wait(i, o, s)

    aliases = {}
    for k, c in enumerate(carries):
        aliases.update({sum(n_in[:k]) + i: sum(n_out[:k]) + o for i, o in c.aliases.items()})
    joined = _Carry(sum((c.inputs for c in carries), []), sum((c.out_shapes for c in carries), []),
                    sum((c.scratch for c in carries), []), start, wait, aliases)
    joined.counts = n_out
    joined.split = lambda results: split(results, n_out)
    return joined


def _carried(body, carry, n_in, n_out, grid):
    if carry is None:
        return body
    ci, co, cs = len(carry.inputs), len(carry.out_shapes), len(carry.scratch)

    def wrapped(*refs):
        ins, cins = refs[:n_in], refs[n_in:n_in + ci]
        outs, couts = refs[n_in + ci:n_in + ci + n_out], refs[n_in + ci + n_out:n_in + ci + n_out + co]
        rest = refs[n_in + ci + n_out + co:]
        scratch, cscr = rest[:len(rest) - cs], rest[len(rest) - cs:]
        first, last = None, None
        for axis, size in enumerate(grid):
            f, l = pl.program_id(axis) == 0, pl.program_id(axis) == size - 1
            first = f if first is None else first & f
            last = l if last is None else last & l

        @pl.when(first)
        def _():
            carry.start(cins, couts, cscr)

        body(*ins, *outs, *scratch)

        @pl.when(last)
        def _():
            carry.wait(cins, couts, cscr)

    return wrapped


def _carry_call(body, carry, *, name, grid, in_specs, out_specs, out_shape, scratch_shapes, args, vmem=True,
                own_aliases=None):
    n_in, n_out = len(in_specs), len(out_specs)
    extra_in = [ANY] * len(carry.inputs) if carry else []
    extra_out = [ANY] * len(carry.out_shapes) if carry else []
    aliases = dict(own_aliases or {})
    if carry:
        aliases.update({n_in + i: n_out + o for i, o in carry.aliases.items()})
    out = pl.pallas_call(
        _carried(body, carry, n_in, n_out, grid), name=name, grid=grid,
        in_specs=list(in_specs) + extra_in, out_specs=list(out_specs) + extra_out,
        out_shape=list(out_shape) + (carry.out_shapes if carry else []),
        scratch_shapes=list(scratch_shapes) + (carry.scratch if carry else []),
        input_output_aliases=aliases,
        compiler_params=_cparams(("arbitrary",) * len(grid)) if vmem else None,
    )(*args, *(carry.inputs if carry else []))
    return list(out[:n_out]), list(out[n_out:])


def _run_carry(carry, *, name):
    return _carry_call(lambda: None, carry, name=name, grid=(1,), in_specs=[], out_specs=[], out_shape=[],
                       scratch_shapes=[], args=[], vmem=False)[1]


def _sems(n):
    return [pltpu.SemaphoreType.DMA((n,)), pltpu.SemaphoreType.DMA((n,))]


def _carry_gather1(shards):
    n = len(shards)

    def copies(x_refs, out_refs, scr, with_arrivals):
        send_sems, recv_sems, local_sems = scr
        x, y, c = lax.axis_index("x"), lax.axis_index("y"), lax.axis_index("c")
        peers = [(x, y, 1 - c)] + [(x ^ fx, y ^ fy, c) for fx, fy in OTHER_CHIPS]
        local, sends, arrivals = [], [], []
        for t, (x_ref, out_ref) in enumerate(zip(x_refs, out_refs)):
            local.append(pltpu.make_async_copy(x_ref, out_ref.at[4 * x + 2 * y + c], local_sems.at[t]))
            for k, (px, py, pc) in enumerate(peers):
                sems = dict(send_sem=send_sems.at[4 * t + k], recv_sem=recv_sems.at[4 * t + k],
                            device_id=(px, py, pc), device_id_type=MESH)
                sends.append(pltpu.make_async_remote_copy(src_ref=x_ref, dst_ref=out_ref.at[4 * x + 2 * y + c], **sems))
                if with_arrivals:
                    arrivals.append(
                        pltpu.make_async_remote_copy(src_ref=x_ref, dst_ref=out_ref.at[4 * px + 2 * py + pc], **sems))
        return local, sends, arrivals

    def start(x_refs, out_refs, scr):
        local, sends, _ = copies(x_refs, out_refs, scr, False)
        for cp in local + sends:
            cp.start()

    def wait(x_refs, out_refs, scr):
        local, sends, arrivals = copies(x_refs, out_refs, scr, True)
        for cp in arrivals:
            cp.wait_recv()
        for cp in sends:
            cp.wait_send()
        for cp in local:
            cp.wait()

    return _Carry(shards, [jax.ShapeDtypeStruct((N_DEV,) + a.shape, a.dtype) for a in shards],
                  _sems(4 * n) + [pltpu.SemaphoreType.DMA((n,))], start, wait)


def _carry_gather2(gathered):
    n = len(gathered)

    def copies(in_refs, g_refs, scr, with_arrivals):
        send_sems, recv_sems = scr
        x, y, c = lax.axis_index("x"), lax.axis_index("y"), lax.axis_index("c")
        sends, arrivals = [], []
        for t in range(n):
            for j, (fx, fy) in enumerate(OTHER_CHIPS):
                px, py = x ^ fx, y ^ fy
                sems = dict(send_sem=send_sems.at[3 * t + j], recv_sem=recv_sems.at[3 * t + j],
                            device_id=(x, y, 1 - c), device_id_type=MESH)
                mine, theirs = 4 * px + 2 * py + c, 4 * px + 2 * py + (1 - c)
                sends.append(pltpu.make_async_remote_copy(src_ref=in_refs[t].at[mine], dst_ref=g_refs[t].at[mine], **sems))
                if with_arrivals:
                    arrivals.append(pltpu.make_async_remote_copy(
                        src_ref=in_refs[t].at[mine], dst_ref=g_refs[t].at[theirs], **sems))
        return sends, arrivals

    def start(in_refs, g_refs, scr):
        for cp in copies(in_refs, g_refs, scr, False)[0]:
            cp.start()

    def wait(in_refs, g_refs, scr):
        sends, arrivals = copies(in_refs, g_refs, scr, True)
        for cp in arrivals:
            cp.wait_recv()
        for cp in sends:
            cp.wait_send()

    return _Carry(gathered, [jax.ShapeDtypeStruct(a.shape, a.dtype) for a in gathered], _sems(3 * n), start, wait,
                  aliases={t: t for t in range(n)})


def _allreduce_rows(x, *, name):
    def body(x_ref, o_ref, sib_ref, mine_ref, tab_ref, send_sems, recv_sems):
        x, y, c = lax.axis_index("x"), lax.axis_index("y"), lax.axis_index("c")
        swap = pltpu.make_async_remote_copy(src_ref=x_ref, dst_ref=sib_ref, send_sem=send_sems.at[0],
                                            recv_sem=recv_sems.at[0], device_id=(x, y, 1 - c), device_id_type=MESH)
        swap.start()
        swap.wait()
        mine_ref[...] = x_ref[...] + sib_ref[...]
        tab_ref[pl.ds(2 * x + y, 1)] = mine_ref[...][None]

        def copy(k, slot):
            fx, fy = OTHER_CHIPS[k]
            return pltpu.make_async_remote_copy(
                src_ref=mine_ref, dst_ref=tab_ref.at[slot], send_sem=send_sems.at[1 + k], recv_sem=recv_sems.at[1 + k],
                device_id=(x ^ fx, y ^ fy, c), device_id_type=MESH)

        for k in range(3):
            copy(k, 2 * x + y).start()
        for k, (fx, fy) in enumerate(OTHER_CHIPS):
            copy(k, 2 * (x ^ fx) + (y ^ fy)).wait()
        o_ref[...] = ((tab_ref[0] + tab_ref[1]) + tab_ref[2]) + tab_ref[3]

    vmem = pl.BlockSpec(memory_space=pltpu.VMEM)
    return pl.pallas_call(
        body, name=name, out_shape=jax.ShapeDtypeStruct(x.shape, F32), in_specs=[vmem], out_specs=vmem,
        scratch_shapes=[pltpu.VMEM(x.shape, F32), pltpu.VMEM(x.shape, F32), pltpu.VMEM((4,) + x.shape, F32)] + _sems(4),
    )(x)


def _allgather(shards, *, name):
    n = len(shards)

    def body(*refs):
        x_refs, out_refs = refs[:n], refs[n:2 * n]
        send_sems, recv_sems, local_sems = refs[2 * n:]
        x, y, c = lax.axis_index("x"), lax.axis_index("y"), lax.axis_index("c")
        me, sibling = (x, y, c), (x, y, 1 - c)
        chips = [(x ^ fx, y ^ fy) for fx, fy in OTHER_CHIPS]

        def copy(t, k, block, to, from_input=False):
            px, py, pc = block
            slab = out_refs[t].at[4 * px + 2 * py + pc]
            return pltpu.make_async_remote_copy(
                src_ref=x_refs[t] if from_input else slab, dst_ref=slab,
                send_sem=send_sems.at[7 * t + k], recv_sem=recv_sems.at[7 * t + k], device_id=to, device_id_type=MESH)

        mine = [pltpu.make_async_copy(x_refs[t], out_refs[t].at[4 * x + 2 * y + c], local_sems.at[t]) for t in range(n)]
        for cp in mine:
            cp.start()
        first = []
        for t in range(n):
            first.append(copy(t, 0, me, sibling, from_input=True))
            first += [copy(t, 1 + j, me, (*chip, c), from_input=True) for j, chip in enumerate(chips)]
        for cp in first:
            cp.start()
        passed = []
        for j, chip in enumerate(chips):
            for t in range(n):
                copy(t, 1 + j, (*chip, c), me).wait_recv()
                fwd = copy(t, 4 + j, (*chip, c), sibling)
                fwd.start()
                passed.append(fwd)
        for t in range(n):
            copy(t, 0, sibling, me).wait_recv()
            for j, chip in enumerate(chips):
                copy(t, 4 + j, (*chip, 1 - c), me).wait_recv()
        for cp in first + passed:
            cp.wait_send()
        for cp in mine:
            cp.wait()

    return pl.pallas_call(
        body, name=name, out_shape=[jax.ShapeDtypeStruct((N_DEV,) + a.shape, a.dtype) for a in shards],
        in_specs=[ANY] * n, out_specs=[ANY] * n,
        scratch_shapes=[pltpu.SemaphoreType.DMA((7 * n,)), pltpu.SemaphoreType.DMA((7 * n,)),
                        pltpu.SemaphoreType.DMA((n,))],
    )(*shards)


def _carry_sibling(slabs, small=None):
    n = len(slabs)
    extra = [] if small is None else [small]

    def copies(in_refs, out_refs, scr):
        send_sems, recv_sems = scr
        x, y, c = lax.axis_index("x"), lax.axis_index("y"), lax.axis_index("c")
        sibling = (x, y, 1 - c)
        out = []
        for t in range(n):
            for q in range(4):
                out.append(pltpu.make_async_remote_copy(
                    src_ref=in_refs[t].at[2 * q + (1 - c)], dst_ref=out_refs[t].at[q],
                    send_sem=send_sems.at[4 * t + q], recv_sem=recv_sems.at[4 * t + q],
                    device_id=sibling, device_id_type=MESH))
        if extra:
            out.append(pltpu.make_async_remote_copy(
                src_ref=in_refs[n], dst_ref=out_refs[n], send_sem=send_sems.at[4 * n], recv_sem=recv_sems.at[4 * n],
                device_id=sibling, device_id_type=MESH))
        return out

    def start(*refs):
        for cp in copies(*refs):
            cp.start()

    def wait(*refs):
        for cp in copies(*refs):
            cp.wait()

    return _Carry(list(slabs) + extra,
                  [jax.ShapeDtypeStruct((4,) + a.shape[1:], a.dtype) for a in slabs]
                  + [jax.ShapeDtypeStruct(a.shape, a.dtype) for a in extra], _sems(4 * n + 1), start, wait)


def _carry_chips(psums, small_sum=None):
    n = len(psums)
    table = small_sum is not None

    def copies(in_refs, out_refs, scr, arrivals):
        send_sems, recv_sems = scr[0], scr[1]
        x, y, c = lax.axis_index("x"), lax.axis_index("y"), lax.axis_index("c")
        out = []
        for k, (fx, fy) in enumerate(OTHER_CHIPS):
            px, py = x ^ fx, y ^ fy
            for t in range(n):
                out.append(pltpu.make_async_remote_copy(
                    src_ref=in_refs[t].at[2 * px + py], dst_ref=out_refs[t].at[k],
                    send_sem=send_sems.at[3 * t + k], recv_sem=recv_sems.at[3 * t + k],
                    device_id=(px, py, c), device_id_type=MESH))
            if table:
                slot = 2 * px + py if arrivals else 2 * x + y
                out.append(pltpu.make_async_remote_copy(
                    src_ref=in_refs[n], dst_ref=out_refs[n].at[slot], send_sem=send_sems.at[3 * n + k],
                    recv_sem=recv_sems.at[3 * n + k], device_id=(px, py, c), device_id_type=MESH))
        return out

    def own(in_refs, out_refs, scr):
        x, y = lax.axis_index("x"), lax.axis_index("y")
        return pltpu.make_async_copy(in_refs[n], out_refs[n].at[2 * x + y], scr[2])

    def start(in_refs, out_refs, scr):
        if table:
            own(in_refs, out_refs, scr).start()
        for cp in copies(in_refs, out_refs, scr, False):
            cp.start()

    def wait(in_refs, out_refs, scr):
        for cp in copies(in_refs, out_refs, scr, True):
            cp.wait()
        if table:
            own(in_refs, out_refs, scr).wait()

    out_shapes = [jax.ShapeDtypeStruct((3,) + a.shape[1:], a.dtype) for a in psums]
    if table:
        out_shapes.append(jax.ShapeDtypeStruct((4,) + small_sum.shape, F32))
    return _Carry(list(psums) + ([small_sum] if table else []), out_shapes,
                  _sems(3 * n + 3) + ([pltpu.SemaphoreType.DMA] if table else []), start, wait)


def _to_comm(name, kind, block, dtype=BF16):
    a = block[0]
    if kind == "cols":
        a = a.T
        if name == "w_in":
            a = jnp.pad(a, ((0, IN_SHARD_PAD - IN_SHARD), (0, 0)))
    return a if kind == "f32" else a.astype(dtype)


def _from_comm(name, kind, a):
    if kind == "cols":
        if name == "w_in":
            a = a[:IN_SHARD]
        a = a.T
    return a[None]


def _assemble_weights(g):
    out = {}
    if "w_in" in g:
        out["wt_main"] = _assemble_wt_main(g["w_in"], name="assemble_w_in")
        j, l0 = divmod(O_F, IN_SHARD)
        out["wt_f"] = jnp.pad(g["w_in"][j, l0:l0 + HEADS], ((0, 128 - HEADS), (0, 0)))
    square = dict(w_branch_a="w_a", w_branch_b="w_b", w_out="w_out", w_ple_gate="w_pg")
    for long, short in square.items():
        if long in g:
            out[short] = g[long].reshape(D_MODEL, D_MODEL)
    if "w_up" in g:
        out["wt_up"] = g["w_up"].reshape(2 * D_FF, D_MODEL)
    if "conv_w" in g:
        out["conv_w"] = g["conv_w"].transpose(1, 0, 2).reshape(3, 2 * D_FF)
    if "w_down" in g:
        out["w_down"] = g["w_down"].reshape(D_FF, D_MODEL)
    if "w_ple" in g:
        out["wt_ple"] = g["w_ple"].reshape(D_MODEL, PLE_DIM)
    return out


def _grad_slabs(gr):
    out = {}
    if "wt_main" in gr:
        gm, gf = gr["wt_main"], gr["wt_f"]
        segments = ((0, 2048, gm, 0), (2048, O_F, gm, 2048), (O_F, O_G, gf, -O_F), (O_G, IN_COLS, gm, 2048 - O_G))
        slabs = []
        for j in range(N_DEV):
            lo, hi = j * IN_SHARD, (j + 1) * IN_SHARD
            pieces = [src[max(lo, a) + shift:min(hi, b) + shift] for a, b, src, shift in segments if max(lo, a) < min(hi, b)]
            pieces.append(jnp.zeros((IN_SHARD_PAD - IN_SHARD, D_MODEL), gm.dtype))
            slabs.append(jnp.concatenate(pieces, axis=0))
        out["w_in"] = jnp.stack(slabs)
    rows = dict(w_a="w_branch_a", w_b="w_branch_b", w_out="w_out", wt_up="w_up", w_down="w_down", w_pg="w_ple_gate")
    for short, long in rows.items():
        if short in gr:
            out[long] = gr[short].reshape(N_DEV, -1, D_MODEL)
    if "conv_w" in gr:
        out["conv_w"] = gr["conv_w"].reshape(3, N_DEV, -1).transpose(1, 0, 2)
    if "wt_ple" in gr:
        out["w_ple"] = gr["wt_ple"].reshape(N_DEV, -1, PLE_DIM)
    return {k: v.astype(BF16) for k, v in out.items()}


def _rows(a, rows):
    flat = a.reshape(-1)
    return jnp.pad(flat, (0, rows * 1024 - flat.shape[0])).reshape(rows, 1024)


def _pack_small(parts):
    return jnp.concatenate([_rows(parts[n].astype(F32), r) for n, r in SMALL], axis=0)


def _small(packed, name, shape):
    off, r = SMALL_OFF[name]
    n = math.prod(shape)
    return packed[off:off + r].reshape(-1)[:n].reshape(shape)


class _Exchanges:
    W_S_ROWS = SMALL_OFF["gmlp_w_s"]

    def __init__(self, later, shards, pos):
        self.later, self.shards, self.pos = later, dict(zip(later, shards)), pos
        self.level1, self.slabs, self.from_sib, self.sums32, self.reduced, self.tables = {}, {}, {}, {}, {}, {}

    def gather1(self, names):
        carry = _carry_gather1([self.shards[n] for n in names])
        carry.names = names
        return carry

    def gather1_done(self, carry, results):
        self.level1.update(zip(carry.names, results))

    def gather2(self):
        return _carry_gather2([self.level1[n] for n in self.later])

    def weights(self, full):
        return _assemble_weights(dict(zip(self.later, full)))

    def sibling(self, grads):
        slabs = _grad_slabs(grads)
        self.slabs.update(slabs)
        carry = _carry_sibling(list(slabs.values()))
        carry.names = list(slabs)
        return carry

    def sibling_done(self, carry, results):
        self.from_sib.update(zip(carry.names, results))

    def chips(self, names, table=None):
        sums = {n: _sum_pairs(self.slabs[n], self.from_sib[n], self.pos, name="sum_sibling_" + n) for n in names}
        self.sums32.update({n: s32 for n, (s32, _) in sums.items()})
        carry = _carry_chips([s16 for _, s16 in sums.values()], None if table is None else self.table_part(table))
        carry.names, carry.table = list(names), table
        return carry

    def chips_done(self, carry, results):
        if carry.table is not None:
            *results, self.tables[carry.table] = results
        self.reduced.update({n: (self.sums32[n], r) for n, r in zip(carry.names, results)})

    def sibling_small(self, small_g):
        self.small_g = small_g
        return _carry_sibling([], small_g)

    def sibling_small_done(self, small_sib):
        self.small_chip = _pair_sum_small(self.small_g, small_sib, name="sum_sibling_small")

    def table_part(self, which):
        off, rows = self.W_S_ROWS
        if which == "w_s":
            return self.small_chip[off:off + rows]
        return jnp.concatenate([self.small_chip[:off], self.small_chip[off + rows:]], axis=0)

    def table(self):
        off = self.W_S_ROWS[0]
        rest = self.tables["rest"]
        return jnp.concatenate([rest[:, :off], self.tables["w_s"], rest[:, off:]], axis=1)


def _local_step(x, p, target, w, sm, ex=None):
    s = x.shape[0]
    mm = _matmul
    wt_main = w["wt_main"]
    conv_b = sm["conv_b"]
    bs_t = jnp.pad(sm["gmlp_b_s"].T, ((0, 0), (0, 128 - GROUPS)))
    b_f = jnp.pad(sm["b_f"], ((0, 0), (0, 128 - HEADS)))
    big = dict(tm=1024, tn=1024, tk=1024)
    whole_s = dict(tn=1024, tk=s)

    h = _rmsnorm_fwd(x, sm["norm_mix_g"], name="norm_mix")
    qkv_args = dict(mode="nt", out_dtype=BF16, name="in_qkv", n=3072, b_off=4, **big)
    f_logit = mm(h, w["wt_f"], mode="nt", out_dtype=F32, name="in_f", tm=1024, tk=1024)
    cqe = _forget_cumsum(f_logit, b_f, name="forget_cumsum")
    uvg = dict(mode="nt", out_dtype=F32, name="in_uvg", n=4096, **big)
    if ex is None:
        qkv = mm(h, wt_main, **qkv_args)
        (qa, ka, vt), _ = _attn_prep(qkv, cqe, name="attn_prep")
        (b, lse3), _ = _attn_fwd(qa, ka, vt, name="attn_fwd")
        zuvg = mm(h, wt_main, **uvg)
    else:
        groups = (["w_branch_a"], ["w_branch_b"], [n for n in ex.later if n not in ("w_branch_a", "w_branch_b")])
        carries = [ex.gather1(names) for names in groups]
        qkv, got0 = mm(h, wt_main, carry=carries[0], **qkv_args)
        (qa, ka, vt), got1 = _attn_prep(qkv, cqe, carries[1], name="attn_prep")
        (b, lse3), got2 = _attn_fwd(qa, ka, vt, carries[2], name="attn_fwd")
        for carry, got in zip(carries, (got0, got1, got2)):
            ex.gather1_done(carry, got)
        zuvg, full = mm(h, wt_main, carry=ex.gather2(), **uvg)
        w = {**w, **ex.weights(full)}
    a = _gmlp_fwd(zuvg, sm["gmlp_ln_g"], sm["gmlp_ln_b"], sm["gmlp_w_s"], bs_t, name="gmlp_fwd")
    wt_up, conv_w = w["wt_up"], w["conv_w"]
    ya = mm(a, w["w_a"], mode="nn", out_dtype=F32, name="branch_a", **big)
    yb = mm(b, w["w_b"], mode="nn", out_dtype=F32, name="branch_b", **big)
    merged = _merge_fwd(ya, yb, zuvg, name="merge_fwd")
    x1 = mm(merged, w["w_out"], mode="nn", out_dtype=F32, name="out_proj", add=x, **big)
    h2 = _rmsnorm_fwd(x1, sm["norm_ffn_g"], name="norm_ffn")
    up = mm(h2, wt_up, mode="nt", out_dtype=F32, name="up", tm=1024, tn=512, tk=1024)
    act = _convglu_fwd(up, conv_w, conv_b, name="convglu_fwd")
    x2 = mm(act, w["w_down"], mode="nn", out_dtype=F32, name="down", tm=1024, tn=1024, tk=1408, add=x1)
    h3 = _rmsnorm_fwd(x2, sm["norm_ple_g"], name="norm_ple")
    ple = mm(p, w["wt_ple"], mode="nt", out_dtype=F32, name="ple", tm=1024, tn=1024, tk=256)
    gp = mm(h3, w["w_pg"], mode="nn", out_dtype=F32, name="ple_gate", **big)
    x3 = _ple_fwd(x2, ple, gp, name="ple_fwd")

    loss, dx3, d_norm_final = _final_loss_bwd(x3, target, sm["norm_final_g"], name="loss_bwd")
    dple, dgp = _ple_bwd(dx3, ple, gp, name="ple_bwd")
    g_wt_ple = mm(dple, p, mode="tn", out_dtype=BF16, name="d_w_ple", tm=512, tn=256, tk=s)
    g_w_pg = mm(h3, dgp, mode="tn", out_dtype=BF16, name="d_w_pg", tm=256, **whole_s)
    (dx2, dx2b, d_norm_ple), _ = _matmul_rmsnorm_bwd([dgp], w["w_pg"], dx3, x2, sm["norm_ple_g"], mode="nt", tk=1024,
                                                     name="d_h3_norm_ple_bwd")
    g_w_down = mm(act, dx2b, mode="tn", out_dtype=BF16, name="d_w_down", tm=256, **whole_s)
    dact_args = dict(mode="nt", out_dtype=BF16, name="d_act", tm=1024, tn=1408, tk=1024)
    if ex is None:
        dact = mm(dx2b, w["w_down"], **dact_args)
    else:
        early = ex.sibling(dict(w_pg=g_w_pg, wt_ple=g_wt_ple))
        dact, got = mm(dx2b, w["w_down"], carry=early, **dact_args)
        ex.sibling_done(early, got)
    dup_a, dup_g, dcw_a, dcw_g, dcb_a, dcb_g = _convglu_bwd(dact, up, conv_w, conv_b, name="convglu_bwd")
    g_wt_up = mm(dup_a, h2, mode="tn", out_dtype=BF16, name="d_w_up_a", tm=256, out_rows=2 * D_FF, **whole_s)
    g_wt_up = mm(dup_g, h2, mode="tn", out_dtype=BF16, name="d_w_up_g", tm=256, out_rows=2 * D_FF,
                 o_off=D_FF // 256, into=g_wt_up, **whole_s)
    (dx1, dx1b, d_norm_ffn), _ = _matmul_rmsnorm_bwd([dup_a, dup_g], wt_up, dx2, x1, sm["norm_ffn_g"], mode="nn",
                                                     tk=1408, name="d_h2_norm_ffn_bwd")
    g_w_out = mm(merged, dx1b, mode="tn", out_dtype=BF16, name="d_w_out", tm=256, **whole_s)
    dmerged = mm(dx1b, w["w_out"], mode="nt", out_dtype=F32, name="d_merged", **big)
    dya, dyb, dga, dgb = _merge_bwd(dmerged, ya, yb, zuvg, name="merge_bwd")
    g_w_a = mm(a, dya, mode="tn", out_dtype=BF16, name="d_w_a", tm=256, **whole_s)
    g_w_b = mm(b, dyb, mode="tn", out_dtype=BF16, name="d_w_b", tm=256, **whole_s)
    da = mm(dya, w["w_a"], mode="nt", out_dtype=BF16, name="d_a", **big)
    db = mm(dyb, w["w_b"], mode="nt", out_dtype=BF16, name="d_b", **big)
    grads = dict(w_a=g_w_a, w_b=g_w_b, w_out=g_w_out, wt_up=g_wt_up, conv_w=jnp.concatenate([dcw_a, dcw_g], axis=1),
                 w_down=g_w_down, wt_ple=g_wt_ple, w_pg=g_w_pg)
    gmlp_args = (da, zuvg, sm["gmlp_ln_g"], sm["gmlp_ln_b"], sm["gmlp_w_s"], bs_t)
    if ex is None:
        (dzu, dzv, d_w_s, d_bs_t, d_ln_g, d_ln_b), _ = _gmlp_bwd(*gmlp_args, name="gmlp_bwd")
    else:
        rest = ex.sibling({k: v for k, v in grads.items() if k not in ("w_pg", "wt_ple")})
        early_chips = ex.chips(early.names)
        both = _carry_join(rest, early_chips)
        (dzu, dzv, d_w_s, d_bs_t, d_ln_g, d_ln_b), got = _gmlp_bwd(*gmlp_args, both, name="gmlp_bwd")
        got_rest, got_early = both.split(got)
        ex.sibling_done(rest, got_rest)
        ex.chips_done(early_chips, got_early)
    small = dict(norm_mix_g=jnp.zeros((1, D_MODEL), F32), b_f=jnp.zeros((1, HEADS), F32), gmlp_ln_g=d_ln_g,
                 gmlp_ln_b=d_ln_b, gmlp_w_s=d_w_s, gmlp_b_s=d_bs_t[:, :GROUPS].T, norm_ffn_g=d_norm_ffn,
                 conv_b=jnp.concatenate([dcb_a, dcb_g], axis=1), norm_ple_g=d_norm_ple, norm_final_g=d_norm_final)
    if ex is None:
        delta3, _ = _attn_delta(db, b, name="attn_delta")
        (dq, dk, dv, aux, dcq3), _ = _attn_bwd(qa, ka, qkv, db, lse3, delta3, name="attn_bwd")
    else:
        delta3, (small_sib,) = _attn_delta(db, b, ex.sibling_small(_pack_small(small)), name="attn_delta")
        ex.sibling_small_done(small_sib)
        main_chips = ex.chips(rest.names, table="rest")
        (dq, dk, dv, aux, dcq3), got = _attn_bwd(qa, ka, qkv, db, lse3, delta3, main_chips, name="attn_bwd")
        ex.chips_done(main_chips, got)
    dcq16 = jnp.pad(dcq3[:, :2, :].reshape(HEADS, s).T, ((0, 0), (0, 128 - HEADS)))
    dzf, d_b_f = _forget_bwd(dcq16, aux, f_logit, b_f, name="forget_bwd")
    dz_parts = [dzu, dzv, dga, dgb, dq, dk, dv]
    w_s_chips = None if ex is None else ex.chips([], table="w_s")
    g_wt_main, got = _grad_w_parts(dz_parts, h, name="d_w_main", tm=256, carry=w_s_chips)
    if ex is not None:
        ex.chips_done(w_s_chips, got)
    g_wt_f = mm(dzf, h, mode="tn", out_dtype=BF16, name="d_w_f", **whole_s)
    grads = dict(grads, wt_main=g_wt_main, wt_f=g_wt_f)
    w_in_chips = None
    if ex is not None:
        w_in_sib = ex.sibling(dict(wt_main=g_wt_main, wt_f=g_wt_f))
        ex.sibling_done(w_in_sib, _run_carry(w_in_sib, name="exchange_sibling_w_in"))
        w_in_chips = ex.chips(w_in_sib.names)
    (dx0, _, d_norm_mix), got = _matmul_rmsnorm_bwd(dz_parts, wt_main, dx1, x, sm["norm_mix_g"], mode="nn", tk=1024,
                                                    extra=(dzf, w["wt_f"]), name="d_h_norm_mix_bwd", carry=w_in_chips)
    if ex is not None:
        ex.chips_done(w_in_chips, got)
    return loss, dx0, grads, dict(small, norm_mix_g=d_norm_mix, b_f=d_b_f[:, :HEADS])


def kernel(x, p, norm_mix_g, w_in, b_f, gmlp_ln_g, gmlp_ln_b, gmlp_w_s, gmlp_b_s, w_branch_a, w_branch_b, w_out, norm_ffn_g, w_up, conv_w, conv_b, w_down, norm_ple_g, w_ple, w_ple_gate, norm_final_g, loss_target, m_norm_mix_g, m_w_in, m_b_f, m_gmlp_ln_g, m_gmlp_ln_b, m_gmlp_w_s, m_gmlp_b_s, m_w_branch_a, m_w_branch_b, m_w_out, m_norm_ffn_g, m_w_up, m_conv_w, m_conv_b, m_w_down, m_norm_ple_g, m_w_ple, m_w_ple_gate, m_norm_final_g, v_norm_mix_g, v_w_in, v_b_f, v_gmlp_ln_g, v_gmlp_ln_b, v_gmlp_w_s, v_gmlp_b_s, v_w_branch_a, v_w_branch_b, v_w_out, v_norm_ffn_g, v_w_up, v_conv_w, v_conv_b, v_w_down, v_norm_ple_g, v_w_ple, v_w_ple_gate, v_norm_final_g):
    given = dict(locals())
    weights = {n: given[n] for n in WEIGHT_ORDER}
    mom_m = {n: given["m_" + n] for n in WEIGHT_ORDER}
    mom_v = {n: given["v_" + n] for n in WEIGHT_ORDER}
    pos = jnp.stack([lax.axis_index("x"), lax.axis_index("y"), lax.axis_index("c")]).astype(I32)
    names = [n for n, _ in SHARDED]
    kinds = dict(SHARDED)

    later = [n for n in names if n != "w_in"]

    first = _allgather([_to_comm("w_in", kinds["w_in"], weights["w_in"])], name="allgather_w_in")
    ex = _Exchanges(later, [_to_comm(n, kinds[n], weights[n]) for n in later], pos)

    sm = dict(norm_mix_g=norm_mix_g, b_f=b_f, gmlp_ln_g=gmlp_ln_g, gmlp_ln_b=gmlp_ln_b, gmlp_w_s=gmlp_w_s[0],
              gmlp_b_s=gmlp_b_s[0], norm_ffn_g=norm_ffn_g, conv_b=conv_b, norm_ple_g=norm_ple_g,
              norm_final_g=norm_final_g.reshape(1, D_MODEL))
    loss_part, dx0, grads, small = _local_step(
        x[0], p[0, 0], loss_target[0], _assemble_weights({"w_in": first[0]}), sm, ex)

    b_f_and_loss = jnp.concatenate([small["b_f"].reshape(-1), loss_part[0, :1]])
    last = _allreduce_rows(jnp.concatenate([_rows(small["norm_mix_g"], 8), _rows(b_f_and_loss, 8)], axis=0),
                           name="allreduce_last")
    loss = last[8, HEADS]
    small_last = jnp.pad(last, ((0, SMALL_ROWS - 16), (0, 0)))

    grad, delta, new_m, new_v = {}, {}, {}, {}
    for n in names:
        s32, r = ex.reduced[n]
        outs = _adam_sharded(s32, r, *[_to_comm(n, kinds[n], src[n], F32) for src in (weights, mom_m, mom_v)], pos,
                             name="adam_" + n)
        grad[n], delta[n], new_m[n], new_v[n] = [_from_comm(n, kinds[n], o) for o in outs]
    replicated = [n for n, _ in SMALL]
    rep = lambda src: _pack_small({n: src[n] for n in replicated})
    packed = _adam_replicated(ex.table(), small_last, rep(weights), rep(mom_m), rep(mom_v), name="adam_replicated")
    for out, pk in zip((grad, delta, new_m, new_v), packed):
        for n in replicated:
            out[n] = _small(pk, n, weights[n].shape)

    return (loss, dx0[None], *[grad[n] for n in WEIGHT_ORDER], *[delta[n] for n in WEIGHT_ORDER],
            *[new_m[n] for n in WEIGHT_ORDER], *[new_v[n] for n in WEIGHT_ORDER])
```

```python
import functools
import math

import jax
import jax.numpy as jnp
from jax import lax
from jax.experimental import pallas as pl
from jax.experimental.pallas import tpu as pltpu

F32 = jnp.float32
BF16 = jnp.bfloat16
I32 = jnp.int32

D_MODEL = 1024
GROUPS = 8
GDIM = 128
GBLOCK = 128
CHUNK = 64
HEADS = 16
HEAD_DIM = 64
D_FF = 2816
PLE_DIM = 256
EPS = 1e-6
N_DEV = 8
ATT_SCALE = HEAD_DIM ** -0.5
NEG = -1e30

ADAM_LR = 0.001
ADAM_B1 = 0.9
ADAM_B2 = 0.999
ADAM_EPS = 1e-08
ADAM_WD = 0.01
ADAM_STEP = 10

V7X_VMEM_LIMIT = 48 * 1024 * 1024
MESH = pl.DeviceIdType.MESH

O_F = 2 * 1024 + 3 * 1024
O_G = O_F + HEADS
IN_COLS = O_G + 2 * D_MODEL
MAIN_COLS = IN_COLS - HEADS
IN_SHARD = IN_COLS // N_DEV
IN_SHARD_PAD = 912

SHARDED = (("w_in", "cols"), ("w_branch_a", "rows"), ("w_branch_b", "rows"), ("w_out", "rows"), ("w_up", "cols"),
           ("conv_w", "f32"), ("w_down", "rows"), ("w_ple", "cols"), ("w_ple_gate", "rows"))

SMALL = (("norm_mix_g", 8), ("b_f", 8), ("gmlp_ln_g", 8), ("gmlp_ln_b", 8), ("gmlp_w_s", 128), ("gmlp_b_s", 8),
         ("norm_ffn_g", 8), ("conv_b", 8), ("norm_ple_g", 8), ("norm_final_g", 8))
SMALL_OFF = {}
_o = 0
for _n, _r in SMALL:
    SMALL_OFF[_n] = (_o, _r)
    _o += _r
SMALL_ROWS = _o

WEIGHT_ORDER = ("norm_mix_g", "w_in", "b_f", "gmlp_ln_g", "gmlp_ln_b", "gmlp_w_s", "gmlp_b_s", "w_branch_a",
                "w_branch_b", "w_out", "norm_ffn_g", "w_up", "conv_w", "conv_b", "w_down", "norm_ple_g", "w_ple",
                "w_ple_gate", "norm_final_g")


def _cparams(sem):
    return pltpu.CompilerParams(dimension_semantics=sem, vmem_limit_bytes=V7X_VMEM_LIMIT)


def _gelu(x):
    c = math.sqrt(2.0 / math.pi)
    return 0.5 * x * (1.0 + jnp.tanh(c * (x + 0.044715 * x * x * x)))


def _gelu_and_grad(x):
    c = math.sqrt(2.0 / math.pi)
    t = jnp.tanh(c * (x + 0.044715 * x * x * x))
    g = 0.5 * x * (1.0 + t)
    dg = 0.5 * (1.0 + t) + 0.5 * x * (1.0 - t * t) * (c * (1.0 + 3.0 * 0.044715 * x * x))
    return g, dg


def _sigmoid(x):
    return 1.0 / (1.0 + jnp.exp(-x))


def _dot(a, b, dims):
    return lax.dot_general(a, b, (dims, ((), ())), preferred_element_type=F32)


NN = ((1,), (0,))
NT = ((1,), (1,))
TN = ((0,), (0,))


def _row_tile(rows, most):
    best = None
    for t in range(16, min(rows, most) + 1, 16):
        if rows % t == 0:
            best = t
    return best if best is not None else rows


def _matmul(a, b, *, mode, out_dtype, name, tm=512, tn=512, tk=512, add=None, n=None, b_off=0,
            out_rows=None, o_off=0, into=None, norm_g=None, carry=None):
    if mode == "tn":
        kdim, m = a.shape
    else:
        m, kdim = a.shape
    if n is None:
        n = b.shape[0] if mode == "nt" else b.shape[1]
    tm, tn, tk = min(tm, m), min(tn, n), min(tk, kdim)
    assert m % tm == 0 and n % tn == 0 and kdim % tk == 0, (name, m, n, kdim, tm, tn, tk)
    nk = kdim // tk
    dims = {"nn": NN, "nt": NT, "tn": TN}[mode]

    n_in = 2 + (add is not None) + (into is not None) + (norm_g is not None)
    assert norm_g is None or tn == n, "the RMS norm needs whole rows"

    def finish(r, refs):
        if add is not None:
            r = refs[2][...].astype(F32) + r
        refs[n_in][...] = r.astype(out_dtype)
        if norm_g is not None:
            rs = lax.rsqrt(jnp.mean(r * r, axis=-1, keepdims=True) + EPS)
            refs[n_in + 1][...] = ((r * rs) * refs[n_in - 1][...]).astype(BF16)

    def body(*refs):
        a_ref, b_ref = refs[:2]
        part = _dot(a_ref[...].astype(BF16), b_ref[...].astype(BF16), dims)
        if nk == 1:
            finish(part, refs)
            return
        acc_ref = refs[-1]
        k = pl.program_id(2)

        @pl.when(k == 0)
        def _():
            acc_ref[...] = part

        @pl.when((k > 0) & (k < nk - 1))
        def _():
            acc_ref[...] += part

        @pl.when(k == nk - 1)
        def _():
            finish(acc_ref[...] + part, refs)

    a_spec = pl.BlockSpec((tk, tm), lambda i, j, k: (k, i)) if mode == "tn" else pl.BlockSpec((tm, tk), lambda i, j, k: (i, k))
    if mode == "nt":
        b_spec = pl.BlockSpec((tn, tk), lambda i, j, k: (j + b_off, k))
    else:
        b_spec = pl.BlockSpec((tk, tn), lambda i, j, k: (k + b_off, j))
    o_spec = pl.BlockSpec((tm, tn), lambda i, j, k: (i + o_off, j))
    in_specs = [a_spec, b_spec] + ([pl.BlockSpec((tm, tn), lambda i, j, k: (i, j))] if add is not None else [])
    args = (a, b) + ((add,) if add is not None else ())
    aliases = {}
    if into is not None:
        aliases = {len(args): 0}
        in_specs.append(pl.BlockSpec(memory_space=pl.ANY))
        args += (into,)
    out_specs = [o_spec]
    out_shape = [jax.ShapeDtypeStruct((m if out_rows is None else out_rows, n), out_dtype)]
    if norm_g is not None:
        in_specs.append(pl.BlockSpec((1, n), lambda i, j, k: (0, 0)))
        args += (norm_g,)
        out_specs.append(pl.BlockSpec((tm, tn), lambda i, j, k: (i, j)))
        out_shape.append(jax.ShapeDtypeStruct((m, n), BF16))
    outs, carried = _carry_call(
        body, carry, name=name, grid=(m // tm, n // tn, nk), in_specs=in_specs, out_specs=out_specs,
        out_shape=out_shape, scratch_shapes=[pltpu.VMEM((tm, tn), F32)] if nk > 1 else [], args=args,
        own_aliases=aliases)
    out = outs[0] if norm_g is None else tuple(outs)
    return out if carry is None else (out, carried)


def _row_spec(tr, width, col_block=0):
    return pl.BlockSpec((tr, width), lambda i: (i, col_block))


def _full_spec(shape):
    return pl.BlockSpec(shape, lambda i: tuple(0 for _ in shape))


def _rmsnorm_fwd(x, g, *, name, tr=256):
    s, d = x.shape

    def body(x_ref, g_ref, o_ref):
        xv = x_ref[...]
        r = lax.rsqrt(jnp.mean(xv * xv, axis=-1, keepdims=True) + EPS)
        o_ref[...] = ((xv * r) * g_ref[...]).astype(BF16)

    return pl.pallas_call(
        body, name=name, grid=(s // tr,),
        in_specs=[_row_spec(tr, d), _full_spec((1, d))], out_specs=_row_spec(tr, d),
        out_shape=jax.ShapeDtypeStruct((s, d), BF16), compiler_params=_cparams(("parallel",)),
    )(x, g)


def _matmul_rmsnorm_bwd(a_parts, b, dres, x, g, *, mode, tk, name, extra=None, tm=512, carry=None):
    s, d = x.shape
    spans, lo = [], 0
    for a in a_parts:
        spans.append((lo, lo + a.shape[1] // tk))
        lo = spans[-1][1]
    n_main, total = lo, lo + (extra is not None)
    n_parts = len(a_parts)

    def body(*refs):
        a_refs, b_ref = refs[:n_parts], refs[n_parts]
        k0 = n_parts + 1
        ax_ref, bx_ref = (refs[k0], refs[k0 + 1]) if extra is not None else (None, None)
        k0 += 2 * (extra is not None)
        dres_ref, x_ref, g_ref, dx_ref, dxb_ref, dg_ref, acc_ref = refs[k0:k0 + 7]
        i, kk = pl.program_id(0), pl.program_id(1)

        def accumulate(part, first):
            if first:
                @pl.when(kk == 0)
                def _():
                    acc_ref[...] = part

                @pl.when(kk > 0)
                def _():
                    acc_ref[...] += part
            else:
                acc_ref[...] += part

        for p, (a_ref, (lo_p, hi_p)) in enumerate(zip(a_refs, spans)):
            @pl.when((kk >= lo_p) & (kk < hi_p))
            def _(a_ref=a_ref, lo_p=lo_p):
                accumulate(_dot(a_ref[...].astype(BF16), b_ref[...].astype(BF16), NN if mode == "nn" else NT), lo_p == 0)

        if extra is not None:
            @pl.when(kk == n_main)
            def _():
                accumulate(_dot(ax_ref[...].astype(BF16), bx_ref[...].astype(BF16), NN), False)

        @pl.when(kk == total - 1)
        def _():
            dhv = acc_ref[...]
            xv = x_ref[...]
            r = lax.rsqrt(jnp.mean(xv * xv, axis=-1, keepdims=True) + EPS)
            xhat = xv * r
            dxhat = dhv * g_ref[...]
            dx = dres_ref[...] + r * (dxhat - xhat * jnp.mean(dxhat * xhat, axis=-1, keepdims=True))
            dx_ref[...] = dx
            dxb_ref[...] = dx.astype(BF16)
            dgp = jnp.sum(dhv * xhat, axis=0, keepdims=True)

            @pl.when(i == 0)
            def _():
                dg_ref[...] = dgp

            @pl.when(i > 0)
            def _():
                dg_ref[...] += dgp

    a_specs = [pl.BlockSpec((tm, tk), lambda i, kk, lo_p=lo_p, hi_p=hi_p: (i, jnp.clip(kk - lo_p, 0, hi_p - lo_p - 1)))
               for lo_p, hi_p in spans]
    step = lambda kk: jnp.minimum(kk, n_main - 1)
    b_spec = (pl.BlockSpec((tk, d), lambda i, kk: (step(kk), 0)) if mode == "nn"
              else pl.BlockSpec((d, tk), lambda i, kk: (0, step(kk))))
    rows = pl.BlockSpec((tm, d), lambda i, kk: (i, 0))
    one = pl.BlockSpec((1, d), lambda i, kk: (0, 0))
    x_specs, x_args = [], []
    if extra is not None:
        kx = extra[0].shape[1]
        x_specs = [pl.BlockSpec((tm, kx), lambda i, kk: (i, 0)), pl.BlockSpec((kx, d), lambda i, kk: (0, 0))]
        x_args = list(extra)
    (dx, dxb, dg), carried = _carry_call(
        body, carry, name=name, grid=(s // tm, total),
        in_specs=a_specs + [b_spec] + x_specs + [rows, rows, one], out_specs=[rows, rows, one],
        out_shape=[jax.ShapeDtypeStruct((s, d), F32), jax.ShapeDtypeStruct((s, d), BF16), jax.ShapeDtypeStruct((1, d), F32)],
        scratch_shapes=[pltpu.VMEM((tm, d), F32)], args=list(a_parts) + [b] + x_args + [dres, x, g])
    return (dx, dxb, dg), carried


def _grad_w_parts(a_parts, b, *, name, tm=512, carry=None):
    s, width = a_parts[0].shape
    per, n = width // tm, b.shape[1]

    def body(*refs):
        a_refs, b_ref, o_ref = refs[:len(a_parts)], refs[len(a_parts)], refs[len(a_parts) + 1]
        i = pl.program_id(0)
        for p, a_ref in enumerate(a_refs):
            @pl.when(i // per == p)
            def _(a_ref=a_ref):
                o_ref[...] = _dot(a_ref[...].astype(BF16), b_ref[...].astype(BF16), TN).astype(BF16)

    a_specs = [pl.BlockSpec((s, tm), lambda i, p=p: (0, jnp.clip(i - p * per, 0, per - 1))) for p in range(len(a_parts))]
    (out,), carried = _carry_call(
        body, carry, name=name, grid=(len(a_parts) * per,),
        in_specs=a_specs + [pl.BlockSpec((s, n), lambda i: (0, 0))], out_specs=[pl.BlockSpec((tm, n), lambda i: (i, 0))],
        out_shape=[jax.ShapeDtypeStruct((len(a_parts) * width, n), BF16)], scratch_shapes=[], args=list(a_parts) + [b])
    return out, carried


def _ple_loss(p, wt_ple, h3, w_pg, x2, target, g, *, name, tm=256):
    s, d = x2.shape
    kp = p.shape[1]

    def body(p_ref, wp_ref, h_ref, wg_ref, x_ref, t_ref, g_ref, loss_ref, dx_ref, dple_ref, dgp_ref, dg_ref):
        i = pl.program_id(0)
        ple = _dot(p_ref[...].astype(BF16), wp_ref[...], NT)
        sg = _sigmoid(_dot(h_ref[...], wg_ref[...], NN))
        xv = x_ref[...] + ple * sg
        r = lax.rsqrt(jnp.mean(xv * xv, axis=-1, keepdims=True) + EPS)
        xhat = xv * r
        diff = xhat * g_ref[...] - t_ref[...]
        lp = jnp.zeros((1, 128), F32) + (0.5 / d) * jnp.sum(diff * diff)
        dy = diff * (1.0 / d)
        dxhat = dy * g_ref[...]
        dx = r * (dxhat - xhat * jnp.mean(dxhat * xhat, axis=-1, keepdims=True))
        dx_ref[...] = dx
        dple_ref[...] = (dx * sg).astype(BF16)
        dgp_ref[...] = (dx * ple * (sg * (1.0 - sg))).astype(BF16)
        dgp = jnp.sum(dy * xhat, axis=0, keepdims=True)

        @pl.when(i == 0)
        def _():
            dg_ref[...] = dgp
            loss_ref[...] = lp

        @pl.when(i > 0)
        def _():
            dg_ref[...] += dgp
            loss_ref[...] += lp

    rows = _row_spec(tm, d)
    return pl.pallas_call(
        body, name=name, grid=(s // tm,),
        in_specs=[_row_spec(tm, kp), _full_spec((d, kp)), rows, _full_spec((d, d)), rows, rows, _full_spec((1, d))],
        out_specs=[_full_spec((1, 128)), rows, rows, rows, _full_spec((1, d))],
        out_shape=[jax.ShapeDtypeStruct((1, 128), F32), jax.ShapeDtypeStruct((s, d), F32),
                   jax.ShapeDtypeStruct((s, d), BF16), jax.ShapeDtypeStruct((s, d), BF16),
                   jax.ShapeDtypeStruct((1, d), F32)],
        compiler_params=_cparams(("arbitrary",)),
    )(p, wt_ple, h3, w_pg, x2, target, g)


def _merge_fwd(ya, yb, zuvg, *, name, tr=256):
    s, d = ya.shape

    def body(ya_ref, yb_ref, ga_ref, gb_ref, o_ref):
        o_ref[...] = (_sigmoid(ga_ref[...]) * ya_ref[...] + _sigmoid(gb_ref[...]) * yb_ref[...]).astype(BF16)

    return pl.pallas_call(
        body, name=name, grid=(s // tr,),
        in_specs=[_row_spec(tr, d), _row_spec(tr, d), _row_spec(tr, d, 2), _row_spec(tr, d, 3)],
        out_specs=_row_spec(tr, d),
        out_shape=jax.ShapeDtypeStruct((s, d), BF16), compiler_params=_cparams(("parallel",)),
    )(ya, yb, zuvg, zuvg)


def _merge_bwd(dm, ya, yb, zuvg, *, name, tr=256):
    s, d = ya.shape

    def body(dm_ref, ya_ref, yb_ref, ga_ref, gb_ref, dya_ref, dyb_ref, dga_ref, dgb_ref):
        dmv = dm_ref[...]
        sa = _sigmoid(ga_ref[...])
        sb = _sigmoid(gb_ref[...])
        dya_ref[...] = (dmv * sa).astype(BF16)
        dyb_ref[...] = (dmv * sb).astype(BF16)
        dga_ref[...] = (dmv * ya_ref[...] * (sa * (1.0 - sa))).astype(BF16)
        dgb_ref[...] = (dmv * yb_ref[...] * (sb * (1.0 - sb))).astype(BF16)

    o = jax.ShapeDtypeStruct((s, d), BF16)
    return pl.pallas_call(
        body, name=name, grid=(s // tr,),
        in_specs=[_row_spec(tr, d)] * 3 + [_row_spec(tr, d, 2), _row_spec(tr, d, 3)], out_specs=[_row_spec(tr, d)] * 4,
        out_shape=[o, o, o, o], compiler_params=_cparams(("parallel",)),
    )(dm, ya, yb, zuvg, zuvg)


def _masked_ws(ws_ref, g):
    row = lax.broadcasted_iota(I32, (GBLOCK, GBLOCK), 0)
    col = lax.broadcasted_iota(I32, (GBLOCK, GBLOCK), 1)
    keep = (col // CHUNK) <= (row // CHUNK)
    return jnp.where(keep, ws_ref[g], 0.0), keep


def _layernorm_parts(zv):
    mu = jnp.mean(zv, axis=-1, keepdims=True)
    xc = zv - mu
    rs = lax.rsqrt(jnp.mean(xc * xc, axis=-1, keepdims=True) + EPS)
    return xc * rs, rs


def _gmlp_fwd(zuvg, ln_g, ln_b, w_s, bs_t, *, name):
    s, w = zuvg.shape[0], GROUPS * GDIM

    def body(zu_ref, zv_ref, lng_ref, lnb_ref, ws_ref, bs_ref, a_ref):
        zu = _gelu(zu_ref[...])
        zv = _gelu(zv_ref[...])
        xhat, _ = _layernorm_parts(zv)
        vln = (xhat * lng_ref[...] + lnb_ref[...]).astype(BF16)
        for g in range(GROUPS):
            wm, _ = _masked_ws(ws_ref, g)
            mixed = _dot(wm.astype(BF16), vln[:, g * GDIM:(g + 1) * GDIM], NN) + bs_ref[:, g:g + 1]
            a_ref[:, g * GDIM:(g + 1) * GDIM] = (zu[:, g * GDIM:(g + 1) * GDIM] * mixed).astype(BF16)

    return pl.pallas_call(
        body, name=name, grid=(s // GBLOCK,),
        in_specs=[_row_spec(GBLOCK, w, 0), _row_spec(GBLOCK, w, 1), _full_spec((1, w)), _full_spec((1, w)),
                  _full_spec((GROUPS, GBLOCK, GBLOCK)), _full_spec((GBLOCK, 128))],
        out_specs=_row_spec(GBLOCK, w),
        out_shape=jax.ShapeDtypeStruct((s, w), BF16), compiler_params=_cparams(("parallel",)),
    )(zuvg, zuvg, ln_g, ln_b, w_s, bs_t)


def _gmlp_bwd(da, zuvg, ln_g, ln_b, w_s, bs_t, carry=None, *, name):
    s, w = zuvg.shape[0], GROUPS * GDIM

    def body(da_ref, zu_ref, zv_ref, lng_ref, lnb_ref, ws_ref, bs_ref,
             dzu_ref, dzv_ref, dws_ref, dbs_ref, dlng_ref, dlnb_ref, dvln_ref):
        i = pl.program_id(0)
        zu, dzu_g = _gelu_and_grad(zu_ref[...])
        zv, dzv_g = _gelu_and_grad(zv_ref[...])
        xhat, rs = _layernorm_parts(zv)
        vln = (xhat * lng_ref[...] + lnb_ref[...]).astype(BF16)
        dav = da_ref[...].astype(F32)
        lane = lax.broadcasted_iota(I32, (GBLOCK, 128), 1)
        dbs = jnp.zeros((GBLOCK, 128), F32)

        @pl.when(i == 0)
        def _():
            dws_ref[...] = jnp.zeros_like(dws_ref)

        for g in range(GROUPS):
            sl = slice(g * GDIM, (g + 1) * GDIM)
            wm, keep = _masked_ws(ws_ref, g)
            wmb = wm.astype(BF16)
            vg = vln[:, sl]
            mixed = _dot(wmb, vg, NN) + bs_ref[:, g:g + 1]
            dag = dav[:, sl]
            dzu_ref[:, sl] = (dag * mixed * dzu_g[:, sl]).astype(BF16)
            dmix = dag * zu[:, sl]
            dmb = dmix.astype(BF16)
            dws_ref[g] += jnp.where(keep, _dot(dmb, vg, NT), 0.0)
            dbs = jnp.where(lane == g, jnp.sum(dmix, axis=1, keepdims=True), dbs)
            dvln_ref[:, sl] = _dot(wmb, dmb, TN)
        dvln = dvln_ref[...]
        dxhat = dvln * lng_ref[...]
        dzv = rs * (dxhat - jnp.mean(dxhat, axis=-1, keepdims=True)
                    - xhat * jnp.mean(dxhat * xhat, axis=-1, keepdims=True))
        dzv_ref[...] = (dzv * dzv_g).astype(BF16)
        dlng = jnp.sum(dvln * xhat, axis=0, keepdims=True)
        dlnb = jnp.sum(dvln, axis=0, keepdims=True)

        @pl.when(i == 0)
        def _():
            dbs_ref[...] = dbs
            dlng_ref[...] = dlng
            dlnb_ref[...] = dlnb

        @pl.when(i > 0)
        def _():
            dbs_ref[...] += dbs
            dlng_ref[...] += dlng
            dlnb_ref[...] += dlnb

    return _carry_call(
        body, carry, name=name, grid=(s // GBLOCK,),
        in_specs=[_row_spec(GBLOCK, w), _row_spec(GBLOCK, w, 0), _row_spec(GBLOCK, w, 1), _full_spec((1, w)),
                  _full_spec((1, w)), _full_spec((GROUPS, GBLOCK, GBLOCK)), _full_spec((GBLOCK, 128))],
        out_specs=[_row_spec(GBLOCK, w), _row_spec(GBLOCK, w), _full_spec((GROUPS, GBLOCK, GBLOCK)),
                   _full_spec((GBLOCK, 128)), _full_spec((1, w)), _full_spec((1, w))],
        out_shape=[jax.ShapeDtypeStruct((s, w), BF16), jax.ShapeDtypeStruct((s, w), BF16),
                   jax.ShapeDtypeStruct((GROUPS, GBLOCK, GBLOCK), F32), jax.ShapeDtypeStruct((GBLOCK, 128), F32),
                   jax.ShapeDtypeStruct((1, w), F32), jax.ShapeDtypeStruct((1, w), F32)],
        scratch_shapes=[pltpu.VMEM((GBLOCK, w), F32)], args=[da, zuvg, zuvg, ln_g, ln_b, w_s, bs_t])


def _shift_down(u, k):
    row = lax.broadcasted_iota(I32, u.shape, 0)
    return jnp.where(row >= k, pltpu.roll(u, k, 0), 0.0)


def _shift_up(u, k):
    s = u.shape[0]
    row = lax.broadcasted_iota(I32, u.shape, 0)
    return jnp.where(row < s - k, pltpu.roll(u, s - k, 0), 0.0)


def _conv(u, w_ref, b_ref):
    return b_ref[...] + w_ref[0:1, :] * _shift_down(u, 2) + w_ref[1:2, :] * _shift_down(u, 1) + w_ref[2:3, :] * u


def _conv_specs(s, f, tc):
    nc = f // tc
    half = lambda rows: [pl.BlockSpec((rows, tc), lambda j: (0, j)), pl.BlockSpec((rows, tc), lambda j: (0, nc + j))]
    return half(s), half(3), half(1)


def _convglu_fwd(up, conv_w, conv_b, *, name, tc=256):
    s, f = up.shape[0], up.shape[1] // 2
    up_specs, w_specs, b_specs = _conv_specs(s, f, tc)

    def body(ua_ref, ug_ref, wa_ref, wg_ref, ba_ref, bg_ref, o_ref):
        ca = _conv(ua_ref[...], wa_ref, ba_ref)
        cg = _conv(ug_ref[...], wg_ref, bg_ref)
        o_ref[...] = (_gelu(ca) * cg).astype(BF16)

    return pl.pallas_call(
        body, name=name, grid=(f // tc,),
        in_specs=up_specs + w_specs + b_specs, out_specs=up_specs[0],
        out_shape=jax.ShapeDtypeStruct((s, f), BF16), compiler_params=_cparams(("parallel",)),
    )(up, up, conv_w, conv_w, conv_b, conv_b)


def _convglu_bwd(dact, up, conv_w, conv_b, *, name, tc=256):
    s, f = up.shape[0], up.shape[1] // 2
    up_specs, w_specs, b_specs = _conv_specs(s, f, tc)

    def half(dc, taps, w_ref, du_ref, dw_ref, db_ref):
        db_ref[...] = jnp.sum(dc, axis=0, keepdims=True)
        for k in range(3):
            dw_ref[k:k + 1, :] = jnp.sum(dc * taps[k], axis=0, keepdims=True)
        du = w_ref[2:3, :] * dc + w_ref[1:2, :] * _shift_up(dc, 1) + w_ref[0:1, :] * _shift_up(dc, 2)
        du_ref[...] = du.astype(BF16)

    def body(d_ref, ua_ref, ug_ref, wa_ref, wg_ref, ba_ref, bg_ref,
             dua_ref, dug_ref, dwa_ref, dwg_ref, dba_ref, dbg_ref):
        taps_a = (_shift_down(ua_ref[...], 2), _shift_down(ua_ref[...], 1), ua_ref[...])
        taps_g = (_shift_down(ug_ref[...], 2), _shift_down(ug_ref[...], 1), ug_ref[...])
        conv = lambda taps, w_ref, b_ref: b_ref[...] + w_ref[0:1, :] * taps[0] + w_ref[1:2, :] * taps[1] + w_ref[2:3, :] * taps[2]
        ca = conv(taps_a, wa_ref, ba_ref)
        cg = conv(taps_g, wg_ref, bg_ref)
        ga, dga = _gelu_and_grad(ca)
        dv = d_ref[...].astype(F32)
        half(dv * cg * dga, taps_a, wa_ref, dua_ref, dwa_ref, dba_ref)
        half(dv * ga, taps_g, wg_ref, dug_ref, dwg_ref, dbg_ref)

    col, w3, b1 = up_specs[0], w_specs[0], b_specs[0]
    return pl.pallas_call(
        body, name=name, grid=(f // tc,),
        in_specs=[col] + up_specs + w_specs + b_specs, out_specs=[col, col, w3, w3, b1, b1],
        out_shape=[jax.ShapeDtypeStruct((s, f), BF16), jax.ShapeDtypeStruct((s, f), BF16),
                   jax.ShapeDtypeStruct((3, f), F32), jax.ShapeDtypeStruct((3, f), F32),
                   jax.ShapeDtypeStruct((1, f), F32), jax.ShapeDtypeStruct((1, f), F32)],
        compiler_params=_cparams(("parallel",)),
    )(dact, up, up, conv_w, conv_w, conv_b, conv_b)


def _tri_dot(tri, x):
    b0 = x.astype(BF16)
    r1 = x - b0.astype(F32)
    b1 = r1.astype(BF16)
    b2 = (r1 - b1.astype(F32)).astype(BF16)
    return _dot(tri, b0, NN) + _dot(tri, b1, NN) + _dot(tri, b2, NN)


def _log_sigmoid(x):
    return jnp.minimum(x, 0.0) - jnp.log(1.0 + jnp.exp(-jnp.abs(x)))


def _expand_heads(col16, rows):
    src = lax.broadcasted_iota(I32, (128, HEADS * HEAD_DIM), 0)
    dst = lax.broadcasted_iota(I32, (128, HEADS * HEAD_DIM), 1) // HEAD_DIM
    spread = (src == dst).astype(BF16)
    p0, p1, p2 = _bf16_pieces(col16)
    return (_dot(p0.astype(BF16), spread, NN) + _dot(p1.astype(BF16), spread, NN)) + _dot(p2.astype(BF16), spread, NN)


def _forget_cumsum(f_logit, b_f, *, name):
    s = f_logit.shape[0]
    nb = s // 128

    def body(f_ref, b_ref, cqe_ref):
        row = lax.broadcasted_iota(I32, (128, 128), 0)
        col = lax.broadcasted_iota(I32, (128, 128), 1)
        tri = (col <= row).astype(BF16)

        def step(n, carry):
            r0 = pl.multiple_of(n * 128, 128)
            lf = _log_sigmoid(f_ref[pl.ds(r0, 128), :] + b_ref[...])
            cum = _tri_dot(tri, lf) + carry
            cqe_ref[pl.ds(r0, 128), :] = _expand_heads(cum, 128)
            return cum[127:128, :]

        lax.fori_loop(0, nb, step, jnp.zeros((1, 128), F32))

    return pl.pallas_call(
        body, name=name, grid=(1,),
        in_specs=[_full_spec((s, 128)), _full_spec((1, 128))],
        out_specs=_full_spec((s, HEADS * HEAD_DIM)),
        out_shape=jax.ShapeDtypeStruct((s, HEADS * HEAD_DIM), F32),
        compiler_params=_cparams(("arbitrary",)),
    )(f_logit, b_f)


def _forget_bwd(dcq16, sum_q16, f_logit, b_f, *, name):
    s = f_logit.shape[0]
    nb = s // 128

    def body(a_ref, k_ref, f_ref, b_ref, df_ref, db_ref):
        row = lax.broadcasted_iota(I32, (128, 128), 0)
        col = lax.broadcasted_iota(I32, (128, 128), 1)
        tri_rev = (col >= row).astype(BF16)

        def step(m, carry):
            suffix, dbsum = carry
            n = nb - 1 - m
            r0 = pl.multiple_of(n * 128, 128)
            dcum = a_ref[pl.ds(r0, 128), :] - k_ref[pl.ds(r0, 128), :]
            dlf = _tri_dot(tri_rev, dcum) + suffix
            df = dlf * _sigmoid(-(f_ref[pl.ds(r0, 128), :] + b_ref[...]))
            df_ref[pl.ds(r0, 128), :] = df.astype(BF16)
            return dlf[0:1, :], dbsum + jnp.sum(df, axis=0, keepdims=True)

        _, dbsum = lax.fori_loop(0, nb, step, (jnp.zeros((1, 128), F32), jnp.zeros((1, 128), F32)))
        db_ref[...] = dbsum

    return pl.pallas_call(
        body, name=name, grid=(1,),
        in_specs=[_full_spec((s, 128))] * 3 + [_full_spec((1, 128))],
        out_specs=[_full_spec((s, 128)), _full_spec((1, 128))],
        out_shape=[jax.ShapeDtypeStruct((s, 128), BF16), jax.ShapeDtypeStruct((1, 128), F32)],
        compiler_params=_cparams(("arbitrary",)),
    )(dcq16, sum_q16, f_logit, b_f)


ATT_T = 256


def _head_lanes(rows):
    return lax.broadcasted_iota(I32, (rows, 128), 1) < HEAD_DIM


def _bf16_pieces(c):
    p0 = c.astype(BF16).astype(F32)
    r = c - p0
    p1 = r.astype(BF16).astype(F32)
    p2 = (r - p1).astype(BF16).astype(F32)
    return p0, p1, p2


def _col_reduce(x, op):
    rows = x.shape[0]
    while rows > 8:
        rows //= 2
        x = op(x[:rows], x[rows:])
    return jnp.max(x, axis=0, keepdims=True) if op is jnp.maximum else jnp.sum(x, axis=0, keepdims=True)


def _attn_prep(qkv, cqe, carry=None, *, name):
    s = qkv.shape[0]
    npair = HEADS // 2

    def body(q_ref, k_ref, v_ref, c_ref, qa_ref, ka_ref, vt_ref):
        rows = 128
        lane = lax.broadcasted_iota(I32, (rows, 128), 1)

        def chunk(n, _):
            r0 = pl.multiple_of(n * rows, rows)
            sl = pl.ds(r0, rows)
            qv = q_ref[sl, :].astype(F32) * ATT_SCALE
            kv = k_ref[sl, :].astype(F32)
            p0, p1, p2 = _bf16_pieces(pltpu.roll(c_ref[sl, :], HEAD_DIM, 1))
            for e in range(2):
                mine = (lane < HEAD_DIM) if e == 0 else (lane >= HEAD_DIM)
                base = HEAD_DIM * (1 - e)
                ones_hi = jnp.where((lane >= base + 3) & (lane < base + 6), 1.0, 0.0)
                ones_lo = jnp.where((lane >= base) & (lane < base + 3), 1.0, 0.0)
                qa = jnp.where(mine, qv, jnp.where(lane == base, p0, jnp.where(lane == base + 1, p1,
                               jnp.where(lane == base + 2, p2, ones_hi))))
                ka = jnp.where(mine, kv, jnp.where(lane == base + 3, -p0, jnp.where(lane == base + 4, -p1,
                               jnp.where(lane == base + 5, -p2, ones_lo))))
                qa_ref[e, sl, :] = qa.astype(BF16)
                ka_ref[e, sl, :] = ka.astype(BF16)
            vt_ref[0, :, sl] = v_ref[sl, :].astype(F32).T.astype(BF16)
            return 0

        lax.fori_loop(0, s // rows, chunk, 0)

    pair = pl.BlockSpec((2, s, 128), lambda hp: (hp, 0, 0))
    return _carry_call(
        body, carry, name=name, grid=(npair,),
        in_specs=[pl.BlockSpec((s, 128), lambda hp: (0, hp)), pl.BlockSpec((s, 128), lambda hp: (0, npair + hp)),
                  pl.BlockSpec((s, 128), lambda hp: (0, 2 * npair + hp)), pl.BlockSpec((s, 128), lambda hp: (0, hp))],
        out_specs=[pair, pair, pl.BlockSpec((1, 128, s), lambda hp: (hp, 0, 0))],
        out_shape=[jax.ShapeDtypeStruct((HEADS, s, 128), BF16), jax.ShapeDtypeStruct((HEADS, s, 128), BF16),
                   jax.ShapeDtypeStruct((npair, 128, s), BF16)],
        scratch_shapes=[], args=[qkv, qkv, qkv, cqe])


def _attn_fwd(qa, ka, vt, carry=None, *, name):
    s = qa.shape[1]
    t = 2 * ATT_T
    nq = s // t
    npair = HEADS // 2

    def body(qa_ref, ka_ref, vt_ref, o_ref, lse_ref):
        i = pl.program_id(1)
        krow = lax.broadcasted_iota(I32, (t, t), 0)
        qcol = lax.broadcasted_iota(I32, (t, t), 1)
        sub = lax.broadcasted_iota(I32, (128, t), 0)
        row8 = lax.broadcasted_iota(I32, (8, t), 0)
        qbs = (qa_ref[0], qa_ref[1])
        tk = t

        def step(j, carry, diag):
            c0 = pl.multiple_of(j * tk, tk)
            vtb = vt_ref[0, :, pl.ds(c0, tk)]
            sts = [_dot(ka_ref[e, pl.ds(c0, tk), :], qbs[e], NT) for e in range(2)]
            if diag:
                sts = [jnp.where(krow <= qcol, st, NEG) for st in sts]
            pts, stats = [], []
            for e in range(2):
                m, l, _ = carry[e]
                m_new = jnp.maximum(m, _col_reduce(sts[e], jnp.maximum))
                alpha = jnp.exp(m - m_new)
                pt = jnp.exp(sts[e] - m_new)
                stats.append((m_new, alpha, alpha * l + _col_reduce(pt, jnp.add)))
                pts.append(pt.astype(BF16))
            pvs = [_dot(vtb, pts[e], NN) for e in range(2)]
            return tuple((stats[e][0], stats[e][2], stats[e][1] * carry[e][2] + pvs[e]) for e in range(2))

        init = (jnp.full((1, t), NEG, F32), jnp.zeros((1, t), F32), jnp.zeros((128, t), F32))
        carry = lax.fori_loop(0, i, functools.partial(step, diag=False), (init, init))
        (m0, l0, acc0), (m1, l1, acc1) = step(i, carry, True)
        o_pair = jnp.where(sub < HEAD_DIM, acc0 / l0, acc1 / l1)
        o_ref[...] = o_pair.T.astype(BF16)
        lse_ref[0] = jnp.where(row8 == 0, m0 + jnp.log(l0), jnp.where(row8 == 1, m1 + jnp.log(l1), 0.0))

    return _carry_call(
        body, carry, name=name, grid=(npair, nq),
        in_specs=[pl.BlockSpec((2, t, 128), lambda hp, i: (hp, i, 0)), pl.BlockSpec((2, s, 128), lambda hp, i: (hp, 0, 0)),
                  pl.BlockSpec((1, 128, s), lambda hp, i: (hp, 0, 0))],
        out_specs=[pl.BlockSpec((t, 128), lambda hp, i: (i, hp)), pl.BlockSpec((1, 8, t), lambda hp, i: (hp, 0, i))],
        out_shape=[jax.ShapeDtypeStruct((s, HEADS * HEAD_DIM), BF16), jax.ShapeDtypeStruct((npair, 8, s), F32)],
        scratch_shapes=[], args=[qa, ka, vt])


def _attn_delta(do, o, carry=None, *, name):
    s = do.shape[0]

    def body(do_ref, o_ref, d_ref):
        prod = do_ref[...].astype(F32) * o_ref[...].astype(F32)
        row = lax.broadcasted_iota(I32, (8, 128), 0)
        lane = lax.broadcasted_iota(I32, (8, 128), 1)
        sel = ((row == 0) & (lane < HEAD_DIM) | (row == 1) & (lane >= HEAD_DIM)).astype(BF16)
        p0, p1, p2 = _bf16_pieces(prod)
        d_ref[0] = (_dot(sel, p0.astype(BF16), NT) + _dot(sel, p1.astype(BF16), NT)) + _dot(sel, p2.astype(BF16), NT)

    pair = pl.BlockSpec((s, 128), lambda hp: (0, hp))
    (delta3,), carried = _carry_call(
        body, carry, name=name, grid=(HEADS // 2,), in_specs=[pair, pair],
        out_specs=[pl.BlockSpec((1, 8, s), lambda hp: (hp, 0, 0))],
        out_shape=[jax.ShapeDtypeStruct((HEADS // 2, 8, s), F32)], scratch_shapes=[], args=[do, o])
    return delta3, carried


def _attn_bwd(qa, ka, qkv, do, lse3, delta3, carry=None, *, name):
    s = qa.shape[1]
    t = 2 * ATT_T
    nb = s // t
    npair = HEADS // 2

    def body(qa_ref, ka_ref, v_ref, do_ref, lse_ref, delta_ref, dq_ref, dk_ref, dv_ref, aux_ref, dcq_ref, dqt):
        hp = pl.program_id(0)
        first = _head_lanes(t)
        lane = lax.broadcasted_iota(I32, (t, 128), 1)
        dqt[...] = jnp.zeros_like(dqt)

        @pl.when(hp == 0)
        def _():
            aux_ref[...] = jnp.zeros_like(aux_ref)

        krow = lax.broadcasted_iota(I32, (t, t), 0)
        qcol = lax.broadcasted_iota(I32, (t, t), 1)

        def key_block(j, _):
            c0 = pl.multiple_of(j * t, t)
            vb = v_ref[pl.ds(c0, t), :]
            kbs = (ka_ref[0, pl.ds(c0, t), :], ka_ref[1, pl.ds(c0, t), :])
            kbts = tuple(kb.astype(F32).T.astype(BF16) for kb in kbs)
            vhs = (jnp.where(first, vb, jnp.zeros_like(vb)), jnp.where(first, jnp.zeros_like(vb), vb))

            def query_block(i, carry, diag):
                r0 = pl.multiple_of(i * t, t)
                dob = do_ref[pl.ds(r0, t), :]
                sts = [_dot(kbs[e], qa_ref[e, pl.ds(r0, t), :], NT) for e in range(2)]
                dpts = [_dot(vhs[e], dob, NT) for e in range(2)]
                ptbs, dsbs = [], []
                for e in range(2):
                    st = jnp.where(krow <= qcol, sts[e], NEG) if diag else sts[e]
                    pt = jnp.exp(st - lse_ref[0, e:e + 1, pl.ds(r0, t)])
                    dsbs.append((pt * (dpts[e] - delta_ref[0, e:e + 1, pl.ds(r0, t)])).astype(BF16))
                    ptbs.append(pt.astype(BF16))
                out = []
                for e in range(2):
                    dk_a, dv_a = carry[e]
                    dv_a = dv_a + _dot(ptbs[e], dob, NN)
                    dk_a = dk_a + _dot(dsbs[e], qa_ref[e, pl.ds(r0, t), :], NN)
                    dqt[e, :, pl.ds(r0, t)] += _dot(kbts[e], dsbs[e], NN)
                    out.append((dk_a, dv_a))
                return tuple(out)

            zero = jnp.zeros((t, 128), F32)
            carry = query_block(j, ((zero, zero), (zero, zero)), True)
            (dk0, dv0), (dk1, dv1) = lax.fori_loop(j + 1, nb, functools.partial(query_block, diag=False), carry)
            dk_ref[pl.ds(c0, t), :] = jnp.where(first, dk0, dk1).astype(BF16)
            dv_ref[pl.ds(c0, t), :] = jnp.where(first, dv0, dv1).astype(BF16)
            sum_q = jnp.where(lane == 2 * hp, dk0[:, HEAD_DIM + 3:HEAD_DIM + 4],
                              jnp.where(lane == 2 * hp + 1, dk1[:, 3:4], aux_ref[pl.ds(c0, t), :]))
            aux_ref[pl.ds(c0, t), :] = sum_q
            return 0

        lax.fori_loop(0, nb, key_block, 0)
        sub = lax.broadcasted_iota(I32, (128, s), 0)
        row8 = lax.broadcasted_iota(I32, (8, s), 0)
        dq_ref[...] = (jnp.where(sub < HEAD_DIM, dqt[0], dqt[1]) * ATT_SCALE).T.astype(BF16)
        dcq_ref[0] = jnp.where(row8 == 0, dqt[0, HEAD_DIM:HEAD_DIM + 1, :], jnp.where(row8 == 1, dqt[1, 0:1, :], 0.0))

    def pair_cols(off):
        return pl.BlockSpec((s, 128), lambda hp: (0, off + hp))

    heads = pl.BlockSpec((2, s, 128), lambda hp: (hp, 0, 0))
    rows = pl.BlockSpec((1, 8, s), lambda hp: (hp, 0, 0))
    wide = jax.ShapeDtypeStruct((s, HEADS * HEAD_DIM), BF16)
    return _carry_call(
        body, carry, name=name, grid=(npair,),
        in_specs=[heads, heads, pair_cols(2 * npair), pair_cols(0), rows, rows],
        out_specs=[pair_cols(0), pair_cols(0), pair_cols(0), pl.BlockSpec((s, 128), lambda hp: (0, 0)), rows],
        out_shape=[wide, wide, wide, jax.ShapeDtypeStruct((s, 128), F32), jax.ShapeDtypeStruct((npair, 8, s), F32)],
        scratch_shapes=[pltpu.VMEM((2, 128, s), F32)], args=[qa, ka, qkv, do, lse3, delta3])


def _adam_math(w, g, m, v):
    m = ADAM_B1 * m + (1.0 - ADAM_B1) * g
    v = ADAM_B2 * v + (1.0 - ADAM_B2) * (g * g)
    m_hat = m / (1.0 - ADAM_B1 ** ADAM_STEP)
    v_hat = v / (1.0 - ADAM_B2 ** ADAM_STEP)
    delta = -ADAM_LR * (m_hat / (jnp.sqrt(v_hat) + ADAM_EPS) + ADAM_WD * w)
    return delta, m, v


def _sum_pairs(keep, recv, pos, *, name):
    _, r, c = recv.shape
    tr = _row_tile(r, 512)

    def body(pos_ref, a_ref, b_ref, o32_ref, o16_ref):
        tot = a_ref[...].astype(F32) + b_ref[...].astype(F32)
        o16_ref[...] = tot.astype(BF16)

        @pl.when(pl.program_id(1) == 2 * pos_ref[0] + pos_ref[1])
        def _():
            o32_ref[...] = tot

    out = pl.BlockSpec((1, tr, c), lambda i, q, pos: (q, i, 0))
    grid_spec = pltpu.PrefetchScalarGridSpec(
        num_scalar_prefetch=1, grid=(r // tr, 4),
        in_specs=[pl.BlockSpec((1, tr, c), lambda i, q, pos: (2 * q + pos[2], i, 0)), out],
        out_specs=[pl.BlockSpec((1, tr, c), lambda i, q, pos: (0, i, 0)), out])
    return pl.pallas_call(
        body, name=name, grid_spec=grid_spec,
        out_shape=[jax.ShapeDtypeStruct((1, r, c), F32), jax.ShapeDtypeStruct((4, r, c), BF16)],
        compiler_params=_cparams(("arbitrary", "arbitrary")),
    )(pos, keep, recv)


def _adam_sharded(psum, recv, w, m, v, pos, *, name):
    r, c = w.shape
    tr = _row_tile(r, 320)

    def body(pos_ref, p_ref, r_ref, w_ref, m_ref, v_ref, g_ref, d_ref, mo_ref, vo_ref):
        g = p_ref[0] + r_ref[0].astype(F32) + r_ref[1].astype(F32) + r_ref[2].astype(F32)
        delta, mn, vn = _adam_math(w_ref[...], g, m_ref[...], v_ref[...])
        g_ref[...] = g
        d_ref[...] = delta
        mo_ref[...] = mn
        vo_ref[...] = vn

    row = pl.BlockSpec((tr, c), lambda i, pos: (i, 0))
    grid_spec = pltpu.PrefetchScalarGridSpec(
        num_scalar_prefetch=1, grid=(r // tr,),
        in_specs=[pl.BlockSpec((1, tr, c), lambda i, pos: (0, i, 0)),
                  pl.BlockSpec((3, tr, c), lambda i, pos: (0, i, 0)), row, row, row],
        out_specs=[row, row, row, row])
    o = jax.ShapeDtypeStruct((r, c), F32)
    return pl.pallas_call(
        body, name=name, grid_spec=grid_spec, out_shape=[o, o, o, o],
        compiler_params=_cparams(("parallel",)),
    )(pos, psum, recv, w, m, v)


def _adam_replicated(chip_sums, last, w, m, v, *, name):
    r = w.shape[0]

    def body(s_ref, l_ref, w_ref, m_ref, v_ref, g_ref, d_ref, mo_ref, vo_ref):
        g = (((s_ref[0] + s_ref[1]) + s_ref[2]) + s_ref[3]) + l_ref[...]
        delta, mn, vn = _adam_math(w_ref[...], g, m_ref[...], v_ref[...])
        g_ref[...] = g
        d_ref[...] = delta
        mo_ref[...] = mn
        vo_ref[...] = vn

    o = jax.ShapeDtypeStruct((r, 1024), F32)
    full = _full_spec((r, 1024))
    return pl.pallas_call(
        body, name=name, grid=(1,),
        in_specs=[_full_spec((4, r, 1024)), full, full, full, full], out_specs=[full] * 4, out_shape=[o] * 4,
        compiler_params=_cparams(("arbitrary",)),
    )(chip_sums, last, w, m, v)


ASM_OUT = 256
ASM_SRC = 304


def _w_in_row(r):
    return r if r < 2048 else (r + O_G - 2048 if r < 4096 else r - 2048)


def _assemble_wt_main(g, *, name):
    table = []
    for blk in range(MAIN_COLS // ASM_OUT):
        j, l0 = divmod(_w_in_row(blk * ASM_OUT), IN_SHARD)
        sb = l0 // ASM_SRC
        n_a = min(ASM_OUT, min(IN_SHARD, (sb + 1) * ASM_SRC) - l0)
        if n_a == ASM_OUT:
            nxt = (j, sb)
        elif l0 + n_a == IN_SHARD:
            nxt = (j + 1, 0)
        else:
            nxt = (j, sb + 1)
        table.append((j, sb, l0 - sb * ASM_SRC, n_a) + nxt)

    def body(tab_ref, a_ref, b_ref, o_ref):
        blk = pl.program_id(0)
        off, n_a = tab_ref[blk, 2], tab_ref[blk, 3]
        r = lax.broadcasted_iota(I32, (ASM_OUT, ASM_SRC), 0)
        k = lax.broadcasted_iota(I32, (ASM_OUT, ASM_SRC), 1)
        sel_a = ((k == r + off) & (r < n_a)).astype(BF16)
        sel_b = ((k == r - n_a) & (r >= n_a)).astype(BF16)
        o_ref[...] = (_dot(sel_a, a_ref[0], NN) + _dot(sel_b, b_ref[0], NN)).astype(BF16)

    src = lambda c: pl.BlockSpec((1, ASM_SRC, D_MODEL), lambda blk, tab: (tab[blk, c], tab[blk, c + 1], 0))
    grid_spec = pltpu.PrefetchScalarGridSpec(
        num_scalar_prefetch=1, grid=(len(table),), in_specs=[src(0), src(4)],
        out_specs=pl.BlockSpec((ASM_OUT, D_MODEL), lambda blk, tab: (blk, 0)))
    return pl.pallas_call(
        body, name=name, grid_spec=grid_spec, out_shape=jax.ShapeDtypeStruct((MAIN_COLS, D_MODEL), BF16),
        compiler_params=_cparams(("parallel",)),
    )(jnp.asarray(table, I32), g, g)


def _pair_sum_small(mine, theirs, *, name):
    def body(a_ref, b_ref, o_ref):
        o_ref[...] = a_ref[...] + b_ref[...]

    full = _full_spec(mine.shape)
    return pl.pallas_call(
        body, name=name, grid=(1,), in_specs=[full, full], out_specs=full,
        out_shape=jax.ShapeDtypeStruct(mine.shape, F32), compiler_params=_cparams(("arbitrary",)),
    )(mine, theirs)


ANY = pl.BlockSpec(memory_space=pl.ANY)
OTHER_CHIPS = ((1, 0), (0, 1), (1, 1))


class _Carry:
    def __init__(self, inputs, out_shapes, scratch, start, wait, aliases=None):
        self.inputs, self.out_shapes, self.scratch = list(inputs), list(out_shapes), list(scratch)
        self.start, self.wait, self.aliases = start, wait, dict(aliases or {})


def _carry_join(*carries):
    n_in = [len(c.inputs) for c in carries]
    n_out = [len(c.out_shapes) for c in carries]
    n_scr = [len(c.scratch) for c in carries]

    def split(refs, counts):
        out, k = [], 0
        for n in counts:
            out.append(refs[k:k + n])
            k += n
        return out

    def start(ins, outs, scr):
        for c, i, o, s in zip(carries, split(ins, n_in), split(outs, n_out), split(scr, n_scr)):
            c.start(i, o, s)

    def wait(ins, outs, scr):
        for c, i, o, s in zip(carries, split(ins, n_in), split(outs, n_out), split(scr, n_scr)):
            c.wait(i, o, s)

    aliases = {}
    for k, c in enumerate(carries):
        aliases.update({sum(n_in[:k]) + i: sum(n_out[:k]) + o for i, o in c.aliases.items()})
    joined = _Carry(sum((c.inputs for c in carries), []), sum((c.out_shapes for c in carries), []),
                    sum((c.scratch for c in carries), []), start, wait, aliases)
    joined.counts = n_out
    joined.split = lambda results: split(results, n_out)
    return joined


def _carried(body, carry, n_in, n_out, grid):
    if carry is None:
        return body
    ci, co, cs = len(carry.inputs), len(carry.out_shapes), len(carry.scratch)

    def wrapped(*refs):
        ins, cins = refs[:n_in], refs[n_in:n_in + ci]
        outs, couts = refs[n_in + ci:n_in + ci + n_out], refs[n_in + ci + n_out:n_in + ci + n_out + co]
        rest = refs[n_in + ci + n_out + co:]
        scratch, cscr = rest[:len(rest) - cs], rest[len(rest) - cs:]
        first, last = None, None
        for axis, size in enumerate(grid):
            f, l = pl.program_id(axis) == 0, pl.program_id(axis) == size - 1
            first = f if first is None else first & f
            last = l if last is None else last & l

        @pl.when(first)
        def _():
            carry.start(cins, couts, cscr)

        body(*ins, *outs, *scratch)

        @pl.when(last)
        def _():
            carry.wait(cins, couts, cscr)

    return wrapped


def _carry_call(body, carry, *, name, grid, in_specs, out_specs, out_shape, scratch_shapes, args, vmem=True,
                own_aliases=None):
    n_in, n_out = len(in_specs), len(out_specs)
    extra_in = [ANY] * len(carry.inputs) if carry else []
    extra_out = [ANY] * len(carry.out_shapes) if carry else []
    aliases = dict(own_aliases or {})
    if carry:
        aliases.update({n_in + i: n_out + o for i, o in carry.aliases.items()})
    out = pl.pallas_call(
        _carried(body, carry, n_in, n_out, grid), name=name, grid=grid,
        in_specs=list(in_specs) + extra_in, out_specs=list(out_specs) + extra_out,
        out_shape=list(out_shape) + (carry.out_shapes if carry else []),
        scratch_shapes=list(scratch_shapes) + (carry.scratch if carry else []),
        input_output_aliases=aliases,
        compiler_params=_cparams(("arbitrary",) * len(grid)) if vmem else None,
    )(*args, *(carry.inputs if carry else []))
    return list(out[:n_out]), list(out[n_out:])


def _run_carry(carry, *, name):
    return _carry_call(lambda: None, carry, name=name, grid=(1,), in_specs=[], out_specs=[], out_shape=[],
                       scratch_shapes=[], args=[], vmem=False)[1]


def _sems(n):
    return [pltpu.SemaphoreType.DMA((n,)), pltpu.SemaphoreType.DMA((n,))]


def _carry_gather1(shards):
    n = len(shards)

    def copies(x_refs, out_refs, scr, with_arrivals):
        send_sems, recv_sems, local_sems = scr
        x, y, c = lax.axis_index("x"), lax.axis_index("y"), lax.axis_index("c")
        peers = [(x, y, 1 - c)] + [(x ^ fx, y ^ fy, c) for fx, fy in OTHER_CHIPS]
        local, sends, arrivals = [], [], []
        for t, (x_ref, out_ref) in enumerate(zip(x_refs, out_refs)):
            local.append(pltpu.make_async_copy(x_ref, out_ref.at[4 * x + 2 * y + c], local_sems.at[t]))
            for k, (px, py, pc) in enumerate(peers):
                sems = dict(send_sem=send_sems.at[4 * t + k], recv_sem=recv_sems.at[4 * t + k],
                            device_id=(px, py, pc), device_id_type=MESH)
                sends.append(pltpu.make_async_remote_copy(src_ref=x_ref, dst_ref=out_ref.at[4 * x + 2 * y + c], **sems))
                if with_arrivals:
                    arrivals.append(
                        pltpu.make_async_remote_copy(src_ref=x_ref, dst_ref=out_ref.at[4 * px + 2 * py + pc], **sems))
        return local, sends, arrivals

    def start(x_refs, out_refs, scr):
        local, sends, _ = copies(x_refs, out_refs, scr, False)
        for cp in local + sends:
            cp.start()

    def wait(x_refs, out_refs, scr):
        local, sends, arrivals = copies(x_refs, out_refs, scr, True)
        for cp in arrivals:
            cp.wait_recv()
        for cp in sends:
            cp.wait_send()
        for cp in local:
            cp.wait()

    return _Carry(shards, [jax.ShapeDtypeStruct((N_DEV,) + a.shape, a.dtype) for a in shards],
                  _sems(4 * n) + [pltpu.SemaphoreType.DMA((n,))], start, wait)


def _carry_gather2(gathered):
    n = len(gathered)

    def copies(in_refs, g_refs, scr, with_arrivals):
        send_sems, recv_sems = scr
        x, y, c = lax.axis_index("x"), lax.axis_index("y"), lax.axis_index("c")
        sends, arrivals = [], []
        for t in range(n):
            for j, (fx, fy) in enumerate(OTHER_CHIPS):
                px, py = x ^ fx, y ^ fy
                sems = dict(send_sem=send_sems.at[3 * t + j], recv_sem=recv_sems.at[3 * t + j],
                            device_id=(x, y, 1 - c), device_id_type=MESH)
                mine, theirs = 4 * px + 2 * py + c, 4 * px + 2 * py + (1 - c)
                sends.append(pltpu.make_async_remote_copy(src_ref=in_refs[t].at[mine], dst_ref=g_refs[t].at[mine], **sems))
                if with_arrivals:
                    arrivals.append(pltpu.make_async_remote_copy(
                        src_ref=in_refs[t].at[mine], dst_ref=g_refs[t].at[theirs], **sems))
        return sends, arrivals

    def start(in_refs, g_refs, scr):
        for cp in copies(in_refs, g_refs, scr, False)[0]:
            cp.start()

    def wait(in_refs, g_refs, scr):
        sends, arrivals = copies(in_refs, g_refs, scr, True)
        for cp in arrivals:
            cp.wait_recv()
        for cp in sends:
            cp.wait_send()

    return _Carry(gathered, [jax.ShapeDtypeStruct(a.shape, a.dtype) for a in gathered], _sems(3 * n), start, wait,
                  aliases={t: t for t in range(n)})


def _allreduce_rows(x, *, name):
    def body(x_ref, o_ref, sib_ref, mine_ref, tab_ref, send_sems, recv_sems):
        x, y, c = lax.axis_index("x"), lax.axis_index("y"), lax.axis_index("c")
        swap = pltpu.make_async_remote_copy(src_ref=x_ref, dst_ref=sib_ref, send_sem=send_sems.at[0],
                                            recv_sem=recv_sems.at[0], device_id=(x, y, 1 - c), device_id_type=MESH)
        swap.start()
        swap.wait()
        mine_ref[...] = x_ref[...] + sib_ref[...]
        tab_ref[pl.ds(2 * x + y, 1)] = mine_ref[...][None]

        def copy(k, slot):
            fx, fy = OTHER_CHIPS[k]
            return pltpu.make_async_remote_copy(
                src_ref=mine_ref, dst_ref=tab_ref.at[slot], send_sem=send_sems.at[1 + k], recv_sem=recv_sems.at[1 + k],
                device_id=(x ^ fx, y ^ fy, c), device_id_type=MESH)

        for k in range(3):
            copy(k, 2 * x + y).start()
        for k, (fx, fy) in enumerate(OTHER_CHIPS):
            copy(k, 2 * (x ^ fx) + (y ^ fy)).wait()
        o_ref[...] = ((tab_ref[0] + tab_ref[1]) + tab_ref[2]) + tab_ref[3]

    vmem = pl.BlockSpec(memory_space=pltpu.VMEM)
    return pl.pallas_call(
        body, name=name, out_shape=jax.ShapeDtypeStruct(x.shape, F32), in_specs=[vmem], out_specs=vmem,
        scratch_shapes=[pltpu.VMEM(x.shape, F32), pltpu.VMEM(x.shape, F32), pltpu.VMEM((4,) + x.shape, F32)] + _sems(4),
    )(x)


def _allgather(shards, *, name):
    n = len(shards)

    def body(*refs):
        x_refs, out_refs = refs[:n], refs[n:2 * n]
        send_sems, recv_sems, local_sems = refs[2 * n:]
        x, y, c = lax.axis_index("x"), lax.axis_index("y"), lax.axis_index("c")
        me, sibling = (x, y, c), (x, y, 1 - c)
        chips = [(x ^ fx, y ^ fy) for fx, fy in OTHER_CHIPS]

        def copy(t, k, block, to, from_input=False):
            px, py, pc = block
            slab = out_refs[t].at[4 * px + 2 * py + pc]
            return pltpu.make_async_remote_copy(
                src_ref=x_refs[t] if from_input else slab, dst_ref=slab,
                send_sem=send_sems.at[7 * t + k], recv_sem=recv_sems.at[7 * t + k], device_id=to, device_id_type=MESH)

        mine = [pltpu.make_async_copy(x_refs[t], out_refs[t].at[4 * x + 2 * y + c], local_sems.at[t]) for t in range(n)]
        for cp in mine:
            cp.start()
        first = []
        for t in range(n):
            first.append(copy(t, 0, me, sibling, from_input=True))
            first += [copy(t, 1 + j, me, (*chip, c), from_input=True) for j, chip in enumerate(chips)]
        for cp in first:
            cp.start()
        passed = []
        for j, chip in enumerate(chips):
            for t in range(n):
                copy(t, 1 + j, (*chip, c), me).wait_recv()
                fwd = copy(t, 4 + j, (*chip, c), sibling)
                fwd.start()
                passed.append(fwd)
        for t in range(n):
            copy(t, 0, sibling, me).wait_recv()
            for j, chip in enumerate(chips):
                copy(t, 4 + j, (*chip, 1 - c), me).wait_recv()
        for cp in first + passed:
            cp.wait_send()
        for cp in mine:
            cp.wait()

    return pl.pallas_call(
        body, name=name, out_shape=[jax.ShapeDtypeStruct((N_DEV,) + a.shape, a.dtype) for a in shards],
        in_specs=[ANY] * n, out_specs=[ANY] * n,
        scratch_shapes=[pltpu.SemaphoreType.DMA((7 * n,)), pltpu.SemaphoreType.DMA((7 * n,)),
                        pltpu.SemaphoreType.DMA((n,))],
    )(*shards)


def _carry_sibling(slabs, small=None):
    n = len(slabs)
    extra = [] if small is None else [small]

    def copies(in_refs, out_refs, scr):
        send_sems, recv_sems = scr
        x, y, c = lax.axis_index("x"), lax.axis_index("y"), lax.axis_index("c")
        sibling = (x, y, 1 - c)
        out = []
        for t in range(n):
            for q in range(4):
                out.append(pltpu.make_async_remote_copy(
                    src_ref=in_refs[t].at[2 * q + (1 - c)], dst_ref=out_refs[t].at[q],
                    send_sem=send_sems.at[4 * t + q], recv_sem=recv_sems.at[4 * t + q],
                    device_id=sibling, device_id_type=MESH))
        if extra:
            out.append(pltpu.make_async_remote_copy(
                src_ref=in_refs[n], dst_ref=out_refs[n], send_sem=send_sems.at[4 * n], recv_sem=recv_sems.at[4 * n],
                device_id=sibling, device_id_type=MESH))
        return out

    def start(*refs):
        for cp in copies(*refs):
            cp.start()

    def wait(*refs):
        for cp in copies(*refs):
            cp.wait()

    return _Carry(list(slabs) + extra,
                  [jax.ShapeDtypeStruct((4,) + a.shape[1:], a.dtype) for a in slabs]
                  + [jax.ShapeDtypeStruct(a.shape, a.dtype) for a in extra], _sems(4 * n + 1), start, wait)


def _carry_chips(psums, small_sum=None):
    n = len(psums)
    table = small_sum is not None

    def copies(in_refs, out_refs, scr, arrivals):
        send_sems, recv_sems = scr[0], scr[1]
        x, y, c = lax.axis_index("x"), lax.axis_index("y"), lax.axis_index("c")
        out = []
        for k, (fx, fy) in enumerate(OTHER_CHIPS):
            px, py = x ^ fx, y ^ fy
            for t in range(n):
                out.append(pltpu.make_async_remote_copy(
                    src_ref=in_refs[t].at[2 * px + py], dst_ref=out_refs[t].at[k],
                    send_sem=send_sems.at[3 * t + k], recv_sem=recv_sems.at[3 * t + k],
                    device_id=(px, py, c), device_id_type=MESH))
            if table:
                slot = 2 * px + py if arrivals else 2 * x + y
                out.append(pltpu.make_async_remote_copy(
                    src_ref=in_refs[n], dst_ref=out_refs[n].at[slot], send_sem=send_sems.at[3 * n + k],
                    recv_sem=recv_sems.at[3 * n + k], device_id=(px, py, c), device_id_type=MESH))
        return out

    def own(in_refs, out_refs, scr):
        x, y = lax.axis_index("x"), lax.axis_index("y")
        return pltpu.make_async_copy(in_refs[n], out_refs[n].at[2 * x + y], scr[2])

    def start(in_refs, out_refs, scr):
        if table:
            own(in_refs, out_refs, scr).start()
        for cp in copies(in_refs, out_refs, scr, False):
            cp.start()

    def wait(in_refs, out_refs, scr):
        for cp in copies(in_refs, out_refs, scr, True):
            cp.wait()
        if table:
            own(in_refs, out_refs, scr).wait()

    out_shapes = [jax.ShapeDtypeStruct((3,) + a.shape[1:], a.dtype) for a in psums]
    if table:
        out_shapes.append(jax.ShapeDtypeStruct((4,) + small_sum.shape, F32))
    return _Carry(list(psums) + ([small_sum] if table else []), out_shapes,
                  _sems(3 * n + 3) + ([pltpu.SemaphoreType.DMA] if table else []), start, wait)


def _to_comm(name, kind, block, dtype=BF16):
    a = block[0]
    if kind == "cols":
        a = a.T
        if name == "w_in":
            a = jnp.pad(a, ((0, IN_SHARD_PAD - IN_SHARD), (0, 0)))
    return a if kind == "f32" else a.astype(dtype)


def _from_comm(name, kind, a):
    if kind == "cols":
        if name == "w_in":
            a = a[:IN_SHARD]
        a = a.T
    return a[None]


def _assemble_weights(g):
    out = {}
    if "w_in" in g:
        out["wt_main"] = _assemble_wt_main(g["w_in"], name="assemble_w_in")
        j, l0 = divmod(O_F, IN_SHARD)
        out["wt_f"] = jnp.pad(g["w_in"][j, l0:l0 + HEADS], ((0, 128 - HEADS), (0, 0)))
    square = dict(w_branch_a="w_a", w_branch_b="w_b", w_out="w_out", w_ple_gate="w_pg")
    for long, short in square.items():
        if long in g:
            out[short] = g[long].reshape(D_MODEL, D_MODEL)
    if "w_up" in g:
        out["wt_up"] = g["w_up"].reshape(2 * D_FF, D_MODEL)
    if "conv_w" in g:
        out["conv_w"] = g["conv_w"].transpose(1, 0, 2).reshape(3, 2 * D_FF)
    if "w_down" in g:
        out["w_down"] = g["w_down"].reshape(D_FF, D_MODEL)
    if "w_ple" in g:
        out["wt_ple"] = g["w_ple"].reshape(D_MODEL, PLE_DIM)
    return out


def _grad_slabs(gr):
    out = {}
    if "wt_main" in gr:
        gm, gf = gr["wt_main"], gr["wt_f"]
        segments = ((0, 2048, gm, 0), (2048, O_F, gm, 2048), (O_F, O_G, gf, -O_F), (O_G, IN_COLS, gm, 2048 - O_G))
        slabs = []
        for j in range(N_DEV):
            lo, hi = j * IN_SHARD, (j + 1) * IN_SHARD
            pieces = [src[max(lo, a) + shift:min(hi, b) + shift] for a, b, src, shift in segments if max(lo, a) < min(hi, b)]
            pieces.append(jnp.zeros((IN_SHARD_PAD - IN_SHARD, D_MODEL), gm.dtype))
            slabs.append(jnp.concatenate(pieces, axis=0))
        out["w_in"] = jnp.stack(slabs)
    rows = dict(w_a="w_branch_a", w_b="w_branch_b", w_out="w_out", wt_up="w_up", w_down="w_down", w_pg="w_ple_gate")
    for short, long in rows.items():
        if short in gr:
            out[long] = gr[short].reshape(N_DEV, -1, D_MODEL)
    if "conv_w" in gr:
        out["conv_w"] = gr["conv_w"].reshape(3, N_DEV, -1).transpose(1, 0, 2)
    if "wt_ple" in gr:
        out["w_ple"] = gr["wt_ple"].reshape(N_DEV, -1, PLE_DIM)
    return {k: v.astype(BF16) for k, v in out.items()}


def _rows(a, rows):
    flat = a.reshape(-1)
    return jnp.pad(flat, (0, rows * 1024 - flat.shape[0])).reshape(rows, 1024)


def _pack_small(parts):
    return jnp.concatenate([_rows(parts[n].astype(F32), r) for n, r in SMALL], axis=0)


def _small(packed, name, shape):
    off, r = SMALL_OFF[name]
    n = math.prod(shape)
    return packed[off:off + r].reshape(-1)[:n].reshape(shape)


class _Exchanges:
    W_S_ROWS = SMALL_OFF["gmlp_w_s"]

    def __init__(self, later, shards, pos):
        self.later, self.shards, self.pos = later, dict(zip(later, shards)), pos
        self.level1, self.slabs, self.from_sib, self.sums32, self.reduced, self.tables = {}, {}, {}, {}, {}, {}

    def gather1(self, names):
        carry = _carry_gather1([self.shards[n] for n in names])
        carry.names = names
        return carry

    def gather1_done(self, carry, results):
        self.level1.update(zip(carry.names, results))

    def gather2(self):
        return _carry_gather2([self.level1[n] for n in self.later])

    def weights(self, full):
        return _assemble_weights(dict(zip(self.later, full)))

    def sibling(self, grads):
        slabs = _grad_slabs(grads)
        self.slabs.update(slabs)
        carry = _carry_sibling(list(slabs.values()))
        carry.names = list(slabs)
        return carry

    def sibling_done(self, carry, results):
        self.from_sib.update(zip(carry.names, results))

    def chips(self, names, table=None):
        sums = {n: _sum_pairs(self.slabs[n], self.from_sib[n], self.pos, name="sum_sibling_" + n) for n in names}
        self.sums32.update({n: s32 for n, (s32, _) in sums.items()})
        carry = _carry_chips([s16 for _, s16 in sums.values()], None if table is None else self.table_part(table))
        carry.names, carry.table = list(names), table
        return carry

    def chips_done(self, carry, results):
        if carry.table is not None:
            *results, self.tables[carry.table] = results
        self.reduced.update({n: (self.sums32[n], r) for n, r in zip(carry.names, results)})

    def sibling_small(self, small_g):
        self.small_g = small_g
        return _carry_sibling([], small_g)

    def sibling_small_done(self, small_sib):
        self.small_chip = _pair_sum_small(self.small_g, small_sib, name="sum_sibling_small")

    def table_part(self, which):
        off, rows = self.W_S_ROWS
        if which == "w_s":
            return self.small_chip[off:off + rows]
        return jnp.concatenate([self.small_chip[:off], self.small_chip[off + rows:]], axis=0)

    def table(self):
        off = self.W_S_ROWS[0]
        rest = self.tables["rest"]
        return jnp.concatenate([rest[:, :off], self.tables["w_s"], rest[:, off:]], axis=1)


def _local_step(x, p, target, w, sm, ex=None):
    s = x.shape[0]
    mm = _matmul
    wt_main = w["wt_main"]
    conv_b = sm["conv_b"]
    bs_t = jnp.pad(sm["gmlp_b_s"].T, ((0, 0), (0, 128 - GROUPS)))
    b_f = jnp.pad(sm["b_f"], ((0, 0), (0, 128 - HEADS)))
    big = dict(tm=1024, tn=1024, tk=1024)
    whole_s = dict(tn=1024, tk=s)

    h = _rmsnorm_fwd(x, sm["norm_mix_g"], name="norm_mix")
    qkv_args = dict(mode="nt", out_dtype=BF16, name="in_qkv", n=3072, b_off=4, **big)
    f_logit = mm(h, w["wt_f"], mode="nt", out_dtype=F32, name="in_f", tm=1024, tk=1024)
    cqe = _forget_cumsum(f_logit, b_f, name="forget_cumsum")
    uvg = dict(mode="nt", out_dtype=F32, name="in_uvg", n=4096, **big)
    if ex is None:
        qkv = mm(h, wt_main, **qkv_args)
        (qa, ka, vt), _ = _attn_prep(qkv, cqe, name="attn_prep")
        (b, lse3), _ = _attn_fwd(qa, ka, vt, name="attn_fwd")
        zuvg = mm(h, wt_main, **uvg)
    else:
        groups = (["w_branch_a"], ["w_branch_b"], [n for n in ex.later if n not in ("w_branch_a", "w_branch_b")])
        carries = [ex.gather1(names) for names in groups]
        qkv, got0 = mm(h, wt_main, carry=carries[0], **qkv_args)
        (qa, ka, vt), got1 = _attn_prep(qkv, cqe, carries[1], name="attn_prep")
        (b, lse3), got2 = _attn_fwd(qa, ka, vt, carries[2], name="attn_fwd")
        for carry, got in zip(carries, (got0, got1, got2)):
            ex.gather1_done(carry, got)
        zuvg, full = mm(h, wt_main, carry=ex.gather2(), **uvg)
        w = {**w, **ex.weights(full)}
    a = _gmlp_fwd(zuvg, sm["gmlp_ln_g"], sm["gmlp_ln_b"], sm["gmlp_w_s"], bs_t, name="gmlp_fwd")
    wt_up, conv_w = w["wt_up"], w["conv_w"]
    ya = mm(a, w["w_a"], mode="nn", out_dtype=F32, name="branch_a", **big)
    yb = mm(b, w["w_b"], mode="nn", out_dtype=F32, name="branch_b", **big)
    merged = _merge_fwd(ya, yb, zuvg, name="merge_fwd")
    x1, h2 = mm(merged, w["w_out"], mode="nn", out_dtype=F32, name="out_proj", add=x, norm_g=sm["norm_ffn_g"], **big)
    up = mm(h2, wt_up, mode="nt", out_dtype=F32, name="up", tm=1024, tn=512, tk=1024)
    act = _convglu_fwd(up, conv_w, conv_b, name="convglu_fwd")
    x2, h3 = mm(act, w["w_down"], mode="nn", out_dtype=F32, name="down", tm=1024, tn=1024, tk=1408, add=x1,
                norm_g=sm["norm_ple_g"])

    loss, dx3, dple, dgp, d_norm_final = _ple_loss(p, w["wt_ple"], h3, w["w_pg"], x2, target, sm["norm_final_g"],
                                                   name="ple_loss")
    g_wt_ple = mm(dple, p, mode="tn", out_dtype=BF16, name="d_w_ple", tm=512, tn=256, tk=s)
    g_w_pg = mm(h3, dgp, mode="tn", out_dtype=BF16, name="d_w_pg", tm=256, **whole_s)
    (dx2, dx2b, d_norm_ple), _ = _matmul_rmsnorm_bwd([dgp], w["w_pg"], dx3, x2, sm["norm_ple_g"], mode="nt", tk=1024,
                                                     name="d_h3_norm_ple_bwd")
    g_w_down = mm(act, dx2b, mode="tn", out_dtype=BF16, name="d_w_down", tm=256, **whole_s)
    dact_args = dict(mode="nt", out_dtype=BF16, name="d_act", tm=1024, tn=1408, tk=1024)
    if ex is None:
        dact = mm(dx2b, w["w_down"], **dact_args)
    else:
        early = ex.sibling(dict(w_pg=g_w_pg, wt_ple=g_wt_ple))
        dact, got = mm(dx2b, w["w_down"], carry=early, **dact_args)
        ex.sibling_done(early, got)
    dup_a, dup_g, dcw_a, dcw_g, dcb_a, dcb_g = _convglu_bwd(dact, up, conv_w, conv_b, name="convglu_bwd")
    g_wt_up = mm(dup_a, h2, mode="tn", out_dtype=BF16, name="d_w_up_a", tm=256, out_rows=2 * D_FF, **whole_s)
    g_wt_up = mm(dup_g, h2, mode="tn", out_dtype=BF16, name="d_w_up_g", tm=256, out_rows=2 * D_FF,
                 o_off=D_FF // 256, into=g_wt_up, **whole_s)
    (dx1, dx1b, d_norm_ffn), _ = _matmul_rmsnorm_bwd([dup_a, dup_g], wt_up, dx2, x1, sm["norm_ffn_g"], mode="nn",
                                                     tk=1408, name="d_h2_norm_ffn_bwd")
    g_w_out = mm(merged, dx1b, mode="tn", out_dtype=BF16, name="d_w_out", tm=256, **whole_s)
    dmerged = mm(dx1b, w["w_out"], mode="nt", out_dtype=F32, name="d_merged", **big)
    dya, dyb, dga, dgb = _merge_bwd(dmerged, ya, yb, zuvg, name="merge_bwd")
    g_w_a = mm(a, dya, mode="tn", out_dtype=BF16, name="d_w_a", tm=256, **whole_s)
    g_w_b = mm(b, dyb, mode="tn", out_dtype=BF16, name="d_w_b", tm=256, **whole_s)
    da = mm(dya, w["w_a"], mode="nt", out_dtype=BF16, name="d_a", **big)
    db = mm(dyb, w["w_b"], mode="nt", out_dtype=BF16, name="d_b", **big)
    grads = dict(w_a=g_w_a, w_b=g_w_b, w_out=g_w_out, wt_up=g_wt_up, conv_w=jnp.concatenate([dcw_a, dcw_g], axis=1),
                 w_down=g_w_down, wt_ple=g_wt_ple, w_pg=g_w_pg)
    gmlp_args = (da, zuvg, sm["gmlp_ln_g"], sm["gmlp_ln_b"], sm["gmlp_w_s"], bs_t)
    if ex is None:
        (dzu, dzv, d_w_s, d_bs_t, d_ln_g, d_ln_b), _ = _gmlp_bwd(*gmlp_args, name="gmlp_bwd")
    else:
        rest = ex.sibling({k: v for k, v in grads.items() if k not in ("w_pg", "wt_ple")})
        early_chips = ex.chips(early.names)
        both = _carry_join(rest, early_chips)
        (dzu, dzv, d_w_s, d_bs_t, d_ln_g, d_ln_b), got = _gmlp_bwd(*gmlp_args, both, name="gmlp_bwd")
        got_rest, got_early = both.split(got)
        ex.sibling_done(rest, got_rest)
        ex.chips_done(early_chips, got_early)
    small = dict(norm_mix_g=jnp.zeros((1, D_MODEL), F32), b_f=jnp.zeros((1, HEADS), F32), gmlp_ln_g=d_ln_g,
                 gmlp_ln_b=d_ln_b, gmlp_w_s=d_w_s, gmlp_b_s=d_bs_t[:, :GROUPS].T, norm_ffn_g=d_norm_ffn,
                 conv_b=jnp.concatenate([dcb_a, dcb_g], axis=1), norm_ple_g=d_norm_ple, norm_final_g=d_norm_final)
    if ex is None:
        delta3, _ = _attn_delta(db, b, name="attn_delta")
        (dq, dk, dv, aux, dcq3), _ = _attn_bwd(qa, ka, qkv, db, lse3, delta3, name="attn_bwd")
    else:
        delta3, (small_sib,) = _attn_delta(db, b, ex.sibling_small(_pack_small(small)), name="attn_delta")
        ex.sibling_small_done(small_sib)
        main_chips = ex.chips(rest.names, table="rest")
        (dq, dk, dv, aux, dcq3), got = _attn_bwd(qa, ka, qkv, db, lse3, delta3, main_chips, name="attn_bwd")
        ex.chips_done(main_chips, got)
    dcq16 = jnp.pad(dcq3[:, :2, :].reshape(HEADS, s).T, ((0, 0), (0, 128 - HEADS)))
    dzf, d_b_f = _forget_bwd(dcq16, aux, f_logit, b_f, name="forget_bwd")
    dz_parts = [dzu, dzv, dga, dgb, dq, dk, dv]
    w_s_chips = None if ex is None else ex.chips([], table="w_s")
    g_wt_main, got = _grad_w_parts(dz_parts, h, name="d_w_main", tm=512, carry=w_s_chips)
    if ex is not None:
        ex.chips_done(w_s_chips, got)
    g_wt_f = mm(dzf, h, mode="tn", out_dtype=BF16, name="d_w_f", **whole_s)
    grads = dict(grads, wt_main=g_wt_main, wt_f=g_wt_f)
    w_in_chips = None
    if ex is not None:
        w_in_sib = ex.sibling(dict(wt_main=g_wt_main, wt_f=g_wt_f))
        ex.sibling_done(w_in_sib, _run_carry(w_in_sib, name="exchange_sibling_w_in"))
        w_in_chips = ex.chips(w_in_sib.names)
    (dx0, _, d_norm_mix), got = _matmul_rmsnorm_bwd(dz_parts, wt_main, dx1, x, sm["norm_mix_g"], mode="nn", tk=1024,
                                                    extra=(dzf, w["wt_f"]), name="d_h_norm_mix_bwd", carry=w_in_chips)
    if ex is not None:
        ex.chips_done(w_in_chips, got)
    return loss, dx0, grads, dict(small, norm_mix_g=d_norm_mix, b_f=d_b_f[:, :HEADS])


def kernel(x, p, norm_mix_g, w_in, b_f, gmlp_ln_g, gmlp_ln_b, gmlp_w_s, gmlp_b_s, w_branch_a, w_branch_b, w_out, norm_ffn_g, w_up, conv_w, conv_b, w_down, norm_ple_g, w_ple, w_ple_gate, norm_final_g, loss_target, m_norm_mix_g, m_w_in, m_b_f, m_gmlp_ln_g, m_gmlp_ln_b, m_gmlp_w_s, m_gmlp_b_s, m_w_branch_a, m_w_branch_b, m_w_out, m_norm_ffn_g, m_w_up, m_conv_w, m_conv_b, m_w_down, m_norm_ple_g, m_w_ple, m_w_ple_gate, m_norm_final_g, v_norm_mix_g, v_w_in, v_b_f, v_gmlp_ln_g, v_gmlp_ln_b, v_gmlp_w_s, v_gmlp_b_s, v_w_branch_a, v_w_branch_b, v_w_out, v_norm_ffn_g, v_w_up, v_conv_w, v_conv_b, v_w_down, v_norm_ple_g, v_w_ple, v_w_ple_gate, v_norm_final_g):
    given = dict(locals())
    weights = {n: given[n] for n in WEIGHT_ORDER}
    mom_m = {n: given["m_" + n] for n in WEIGHT_ORDER}
    mom_v = {n: given["v_" + n] for n in WEIGHT_ORDER}
    pos = jnp.stack([lax.axis_index("x"), lax.axis_index("y"), lax.axis_index("c")]).astype(I32)
    names = [n for n, _ in SHARDED]
    kinds = dict(SHARDED)

    later = [n for n in names if n != "w_in"]

    first = _allgather([_to_comm("w_in", kinds["w_in"], weights["w_in"])], name="allgather_w_in")
    ex = _Exchanges(later, [_to_comm(n, kinds[n], weights[n]) for n in later], pos)

    sm = dict(norm_mix_g=norm_mix_g, b_f=b_f, gmlp_ln_g=gmlp_ln_g, gmlp_ln_b=gmlp_ln_b, gmlp_w_s=gmlp_w_s[0],
              gmlp_b_s=gmlp_b_s[0], norm_ffn_g=norm_ffn_g, conv_b=conv_b, norm_ple_g=norm_ple_g,
              norm_final_g=norm_final_g.reshape(1, D_MODEL))
    loss_part, dx0, grads, small = _local_step(
        x[0], p[0, 0], loss_target[0], _assemble_weights({"w_in": first[0]}), sm, ex)

    b_f_and_loss = jnp.concatenate([small["b_f"].reshape(-1), loss_part[0, :1]])
    last = _allreduce_rows(jnp.concatenate([_rows(small["norm_mix_g"], 8), _rows(b_f_and_loss, 8)], axis=0),
                           name="allreduce_last")
    loss = last[8, HEADS]
    small_last = jnp.pad(last, ((0, SMALL_ROWS - 16), (0, 0)))

    grad, delta, new_m, new_v = {}, {}, {}, {}
    for n in names:
        s32, r = ex.reduced[n]
        outs = _adam_sharded(s32, r, *[_to_comm(n, kinds[n], src[n], F32) for src in (weights, mom_m, mom_v)], pos,
                             name="adam_" + n)
        grad[n], delta[n], new_m[n], new_v[n] = [_from_comm(n, kinds[n], o) for o in outs]
    replicated = [n for n, _ in SMALL]
    rep = lambda src: _pack_small({n: src[n] for n in replicated})
    packed = _adam_replicated(ex.table(), small_last, rep(weights), rep(mom_m), rep(mom_v), name="adam_replicated")
    for out, pk in zip((grad, delta, new_m, new_v), packed):
        for n in replicated:
            out[n] = _small(pk, n, weights[n].shape)

    return (loss, dx0[None], *[grad[n] for n in WEIGHT_ORDER], *[delta[n] for n in WEIGHT_ORDER],
            *[new_m[n] for n in WEIGHT_ORDER], *[new_v[n] for n in WEIGHT_ORDER])
```

```python
import functools
import math

import jax
import jax.numpy as jnp
from jax import lax
from jax.experimental import pallas as pl
from jax.experimental.pallas import tpu as pltpu

F32 = jnp.float32
BF16 = jnp.bfloat16
I32 = jnp.int32

D_MODEL = 1024
GROUPS = 8
GDIM = 128
GBLOCK = 128
CHUNK = 64
HEADS = 16
HEAD_DIM = 64
D_FF = 2816
PLE_DIM = 256
EPS = 1e-6
N_DEV = 8
ATT_SCALE = HEAD_DIM ** -0.5
NEG = -1e30

ADAM_LR = 0.001
ADAM_B1 = 0.9
ADAM_B2 = 0.999
ADAM_EPS = 1e-08
ADAM_WD = 0.01
ADAM_STEP = 10

V7X_VMEM_LIMIT = 48 * 1024 * 1024
MESH = pl.DeviceIdType.MESH

O_F = 2 * 1024 + 3 * 1024
O_G = O_F + HEADS
IN_COLS = O_G + 2 * D_MODEL
MAIN_COLS = IN_COLS - HEADS
IN_SHARD = IN_COLS // N_DEV
IN_SHARD_PAD = 912

SHARDED = (("w_in", "cols"), ("w_branch_a", "rows"), ("w_branch_b", "rows"), ("w_out", "rows"), ("w_up", "cols"),
           ("conv_w", "f32"), ("w_down", "rows"), ("w_ple", "cols"), ("w_ple_gate", "rows"))

SMALL = (("norm_mix_g", 8), ("b_f", 8), ("gmlp_ln_g", 8), ("gmlp_ln_b", 8), ("gmlp_w_s", 128), ("gmlp_b_s", 8),
         ("norm_ffn_g", 8), ("conv_b", 8), ("norm_ple_g", 8), ("norm_final_g", 8))
SMALL_OFF = {}
_o = 0
for _n, _r in SMALL:
    SMALL_OFF[_n] = (_o, _r)
    _o += _r
SMALL_ROWS = _o

WEIGHT_ORDER = ("norm_mix_g", "w_in", "b_f", "gmlp_ln_g", "gmlp_ln_b", "gmlp_w_s", "gmlp_b_s", "w_branch_a",
                "w_branch_b", "w_out", "norm_ffn_g", "w_up", "conv_w", "conv_b", "w_down", "norm_ple_g", "w_ple",
                "w_ple_gate", "norm_final_g")


def _cparams(sem):
    return pltpu.CompilerParams(dimension_semantics=sem, vmem_limit_bytes=V7X_VMEM_LIMIT)


def _gelu(x):
    c = math.sqrt(2.0 / math.pi)
    return 0.5 * x * (1.0 + jnp.tanh(c * (x + 0.044715 * x * x * x)))


def _gelu_and_grad(x):
    c = math.sqrt(2.0 / math.pi)
    t = jnp.tanh(c * (x + 0.044715 * x * x * x))
    g = 0.5 * x * (1.0 + t)
    dg = 0.5 * (1.0 + t) + 0.5 * x * (1.0 - t * t) * (c * (1.0 + 3.0 * 0.044715 * x * x))
    return g, dg


def _sigmoid(x):
    return 1.0 / (1.0 + jnp.exp(-x))


def _dot(a, b, dims):
    return lax.dot_general(a, b, (dims, ((), ())), preferred_element_type=F32)


NN = ((1,), (0,))
NT = ((1,), (1,))
TN = ((0,), (0,))


def _row_tile(rows, most):
    best = None
    for t in range(16, min(rows, most) + 1, 16):
        if rows % t == 0:
            best = t
    return best if best is not None else rows


def _matmul(a, b, *, mode, out_dtype, name, tm=512, tn=512, tk=512, add=None, n=None, b_off=0,
            out_rows=None, o_off=0, into=None, norm_g=None, carry=None):
    if mode == "tn":
        kdim, m = a.shape
    else:
        m, kdim = a.shape
    if n is None:
        n = b.shape[0] if mode == "nt" else b.shape[1]
    tm, tn, tk = min(tm, m), min(tn, n), min(tk, kdim)
    assert m % tm == 0 and n % tn == 0 and kdim % tk == 0, (name, m, n, kdim, tm, tn, tk)
    nk = kdim // tk
    dims = {"nn": NN, "nt": NT, "tn": TN}[mode]

    n_in = 2 + (add is not None) + (into is not None) + (norm_g is not None)
    assert norm_g is None or tn == n, "the RMS norm needs whole rows"

    def finish(r, refs):
        if add is not None:
            r = refs[2][...].astype(F32) + r
        refs[n_in][...] = r.astype(out_dtype)
        if norm_g is not None:
            rs = lax.rsqrt(jnp.mean(r * r, axis=-1, keepdims=True) + EPS)
            refs[n_in + 1][...] = ((r * rs) * refs[n_in - 1][...]).astype(BF16)

    def body(*refs):
        a_ref, b_ref = refs[:2]
        part = _dot(a_ref[...].astype(BF16), b_ref[...].astype(BF16), dims)
        if nk == 1:
            finish(part, refs)
            return
        acc_ref = refs[-1]
        k = pl.program_id(2)

        @pl.when(k == 0)
        def _():
            acc_ref[...] = part

        @pl.when((k > 0) & (k < nk - 1))
        def _():
            acc_ref[...] += part

        @pl.when(k == nk - 1)
        def _():
            finish(acc_ref[...] + part, refs)

    a_spec = pl.BlockSpec((tk, tm), lambda i, j, k: (k, i)) if mode == "tn" else pl.BlockSpec((tm, tk), lambda i, j, k: (i, k))
    if mode == "nt":
        b_spec = pl.BlockSpec((tn, tk), lambda i, j, k: (j + b_off, k))
    else:
        b_spec = pl.BlockSpec((tk, tn), lambda i, j, k: (k + b_off, j))
    o_spec = pl.BlockSpec((tm, tn), lambda i, j, k: (i + o_off, j))
    in_specs = [a_spec, b_spec] + ([pl.BlockSpec((tm, tn), lambda i, j, k: (i, j))] if add is not None else [])
    args = (a, b) + ((add,) if add is not None else ())
    aliases = {}
    if into is not None:
        aliases = {len(args): 0}
        in_specs.append(pl.BlockSpec(memory_space=pl.ANY))
        args += (into,)
    out_specs = [o_spec]
    out_shape = [jax.ShapeDtypeStruct((m if out_rows is None else out_rows, n), out_dtype)]
    if norm_g is not None:
        in_specs.append(pl.BlockSpec((1, n), lambda i, j, k: (0, 0)))
        args += (norm_g,)
        out_specs.append(pl.BlockSpec((tm, tn), lambda i, j, k: (i, j)))
        out_shape.append(jax.ShapeDtypeStruct((m, n), BF16))
    outs, carried = _carry_call(
        body, carry, name=name, grid=(m // tm, n // tn, nk), in_specs=in_specs, out_specs=out_specs,
        out_shape=out_shape, scratch_shapes=[pltpu.VMEM((tm, tn), F32)] if nk > 1 else [], args=args,
        own_aliases=aliases)
    out = outs[0] if norm_g is None else tuple(outs)
    return out if carry is None else (out, carried)


def _row_spec(tr, width, col_block=0):
    return pl.BlockSpec((tr, width), lambda i: (i, col_block))


def _full_spec(shape):
    return pl.BlockSpec(shape, lambda i: tuple(0 for _ in shape))


def _rmsnorm_fwd(x, g, *, name, tr=256):
    s, d = x.shape

    def body(x_ref, g_ref, o_ref):
        xv = x_ref[...]
        r = lax.rsqrt(jnp.mean(xv * xv, axis=-1, keepdims=True) + EPS)
        o_ref[...] = ((xv * r) * g_ref[...]).astype(BF16)

    return pl.pallas_call(
        body, name=name, grid=(s // tr,),
        in_specs=[_row_spec(tr, d), _full_spec((1, d))], out_specs=_row_spec(tr, d),
        out_shape=jax.ShapeDtypeStruct((s, d), BF16), compiler_params=_cparams(("parallel",)),
    )(x, g)


def _matmul_rmsnorm_bwd(a_parts, b, dres, x, g, *, mode, tk, name, extra=None, tm=512, carry=None):
    s, d = x.shape
    spans, lo = [], 0
    for a in a_parts:
        spans.append((lo, lo + a.shape[1] // tk))
        lo = spans[-1][1]
    n_main, total = lo, lo + (extra is not None)
    n_parts = len(a_parts)

    def body(*refs):
        a_refs, b_ref = refs[:n_parts], refs[n_parts]
        k0 = n_parts + 1
        ax_ref, bx_ref = (refs[k0], refs[k0 + 1]) if extra is not None else (None, None)
        k0 += 2 * (extra is not None)
        dres_ref, x_ref, g_ref, dx_ref, dxb_ref, dg_ref, acc_ref = refs[k0:k0 + 7]
        i, kk = pl.program_id(0), pl.program_id(1)

        def accumulate(part, first):
            if first:
                @pl.when(kk == 0)
                def _():
                    acc_ref[...] = part

                @pl.when(kk > 0)
                def _():
                    acc_ref[...] += part
            else:
                acc_ref[...] += part

        for p, (a_ref, (lo_p, hi_p)) in enumerate(zip(a_refs, spans)):
            @pl.when((kk >= lo_p) & (kk < hi_p))
            def _(a_ref=a_ref, lo_p=lo_p):
                accumulate(_dot(a_ref[...].astype(BF16), b_ref[...].astype(BF16), NN if mode == "nn" else NT), lo_p == 0)

        if extra is not None:
            @pl.when(kk == n_main)
            def _():
                accumulate(_dot(ax_ref[...].astype(BF16), bx_ref[...].astype(BF16), NN), False)

        @pl.when(kk == total - 1)
        def _():
            dhv = acc_ref[...]
            xv = x_ref[...]
            r = lax.rsqrt(jnp.mean(xv * xv, axis=-1, keepdims=True) + EPS)
            xhat = xv * r
            dxhat = dhv * g_ref[...]
            dx = dres_ref[...] + r * (dxhat - xhat * jnp.mean(dxhat * xhat, axis=-1, keepdims=True))
            dx_ref[...] = dx
            dxb_ref[...] = dx.astype(BF16)
            dgp = jnp.sum(dhv * xhat, axis=0, keepdims=True)

            @pl.when(i == 0)
            def _():
                dg_ref[...] = dgp

            @pl.when(i > 0)
            def _():
                dg_ref[...] += dgp

    a_specs = [pl.BlockSpec((tm, tk), lambda i, kk, lo_p=lo_p, hi_p=hi_p: (i, jnp.clip(kk - lo_p, 0, hi_p - lo_p - 1)))
               for lo_p, hi_p in spans]
    step = lambda kk: jnp.minimum(kk, n_main - 1)
    b_spec = (pl.BlockSpec((tk, d), lambda i, kk: (step(kk), 0)) if mode == "nn"
              else pl.BlockSpec((d, tk), lambda i, kk: (0, step(kk))))
    rows = pl.BlockSpec((tm, d), lambda i, kk: (i, 0))
    one = pl.BlockSpec((1, d), lambda i, kk: (0, 0))
    x_specs, x_args = [], []
    if extra is not None:
        kx = extra[0].shape[1]
        x_specs = [pl.BlockSpec((tm, kx), lambda i, kk: (i, 0)), pl.BlockSpec((kx, d), lambda i, kk: (0, 0))]
        x_args = list(extra)
    (dx, dxb, dg), carried = _carry_call(
        body, carry, name=name, grid=(s // tm, total),
        in_specs=a_specs + [b_spec] + x_specs + [rows, rows, one], out_specs=[rows, rows, one],
        out_shape=[jax.ShapeDtypeStruct((s, d), F32), jax.ShapeDtypeStruct((s, d), BF16), jax.ShapeDtypeStruct((1, d), F32)],
        scratch_shapes=[pltpu.VMEM((tm, d), F32)], args=list(a_parts) + [b] + x_args + [dres, x, g])
    return (dx, dxb, dg), carried


def _grad_w_parts(a_parts, b, *, name, tm=512, carry=None):
    s, width = a_parts[0].shape
    per, n = width // tm, b.shape[1]

    def body(*refs):
        a_refs, b_ref, o_ref = refs[:len(a_parts)], refs[len(a_parts)], refs[len(a_parts) + 1]
        i = pl.program_id(0)
        for p, a_ref in enumerate(a_refs):
            @pl.when(i // per == p)
            def _(a_ref=a_ref):
                o_ref[...] = _dot(a_ref[...].astype(BF16), b_ref[...].astype(BF16), TN).astype(BF16)

    a_specs = [pl.BlockSpec((s, tm), lambda i, p=p: (0, jnp.clip(i - p * per, 0, per - 1))) for p in range(len(a_parts))]
    (out,), carried = _carry_call(
        body, carry, name=name, grid=(len(a_parts) * per,),
        in_specs=a_specs + [pl.BlockSpec((s, n), lambda i: (0, 0))], out_specs=[pl.BlockSpec((tm, n), lambda i: (i, 0))],
        out_shape=[jax.ShapeDtypeStruct((len(a_parts) * width, n), BF16)], scratch_shapes=[], args=list(a_parts) + [b])
    return out, carried


def _ple_loss(p, wt_ple, h3, w_pg, x2, target, g, *, name, tm=256):
    s, d = x2.shape
    kp = p.shape[1]

    def body(p_ref, wp_ref, h_ref, wg_ref, x_ref, t_ref, g_ref, loss_ref, dx_ref, dple_ref, dgp_ref, dg_ref):
        i = pl.program_id(0)
        ple = _dot(p_ref[...].astype(BF16), wp_ref[...], NT)
        sg = _sigmoid(_dot(h_ref[...], wg_ref[...], NN))
        xv = x_ref[...] + ple * sg
        r = lax.rsqrt(jnp.mean(xv * xv, axis=-1, keepdims=True) + EPS)
        xhat = xv * r
        diff = xhat * g_ref[...] - t_ref[...]
        lp = jnp.zeros((1, 128), F32) + (0.5 / d) * jnp.sum(diff * diff)
        dy = diff * (1.0 / d)
        dxhat = dy * g_ref[...]
        dx = r * (dxhat - xhat * jnp.mean(dxhat * xhat, axis=-1, keepdims=True))
        dx_ref[...] = dx
        dple_ref[...] = (dx * sg).astype(BF16)
        dgp_ref[...] = (dx * ple * (sg * (1.0 - sg))).astype(BF16)
        dgp = jnp.sum(dy * xhat, axis=0, keepdims=True)

        @pl.when(i == 0)
        def _():
            dg_ref[...] = dgp
            loss_ref[...] = lp

        @pl.when(i > 0)
        def _():
            dg_ref[...] += dgp
            loss_ref[...] += lp

    rows = _row_spec(tm, d)
    return pl.pallas_call(
        body, name=name, grid=(s // tm,),
        in_specs=[_row_spec(tm, kp), _full_spec((d, kp)), rows, _full_spec((d, d)), rows, rows, _full_spec((1, d))],
        out_specs=[_full_spec((1, 128)), rows, rows, rows, _full_spec((1, d))],
        out_shape=[jax.ShapeDtypeStruct((1, 128), F32), jax.ShapeDtypeStruct((s, d), F32),
                   jax.ShapeDtypeStruct((s, d), BF16), jax.ShapeDtypeStruct((s, d), BF16),
                   jax.ShapeDtypeStruct((1, d), F32)],
        compiler_params=_cparams(("arbitrary",)),
    )(p, wt_ple, h3, w_pg, x2, target, g)


def _branches_merge(a, b, w_a, w_b, zuvg, *, name, tm=512):
    s, d = a.shape

    def body(a_ref, b_ref, wa_ref, wb_ref, ga_ref, gb_ref, ya_ref, yb_ref, o_ref):
        ya = _dot(a_ref[...], wa_ref[...], NN)
        yb = _dot(b_ref[...], wb_ref[...], NN)
        ya_ref[...] = ya
        yb_ref[...] = yb
        o_ref[...] = (_sigmoid(ga_ref[...]) * ya + _sigmoid(gb_ref[...]) * yb).astype(BF16)

    rows = _row_spec(tm, d)
    return pl.pallas_call(
        body, name=name, grid=(s // tm,),
        in_specs=[rows, rows, _full_spec((d, d)), _full_spec((d, d)), _row_spec(tm, d, 2), _row_spec(tm, d, 3)],
        out_specs=[rows, rows, rows],
        out_shape=[jax.ShapeDtypeStruct((s, d), F32), jax.ShapeDtypeStruct((s, d), F32), jax.ShapeDtypeStruct((s, d), BF16)],
        compiler_params=_cparams(("parallel",)),
    )(a, b, w_a, w_b, zuvg, zuvg)


def _merge_bwd(dx1b, w_out, ya, yb, zuvg, *, name, tm=512):
    s, d = ya.shape

    def body(dx_ref, w_ref, ya_ref, yb_ref, ga_ref, gb_ref, dya_ref, dyb_ref, dga_ref, dgb_ref):
        dmv = _dot(dx_ref[...], w_ref[...], NT)
        sa = _sigmoid(ga_ref[...])
        sb = _sigmoid(gb_ref[...])
        dya_ref[...] = (dmv * sa).astype(BF16)
        dyb_ref[...] = (dmv * sb).astype(BF16)
        dga_ref[...] = (dmv * ya_ref[...] * (sa * (1.0 - sa))).astype(BF16)
        dgb_ref[...] = (dmv * yb_ref[...] * (sb * (1.0 - sb))).astype(BF16)

    rows = _row_spec(tm, d)
    o = jax.ShapeDtypeStruct((s, d), BF16)
    return pl.pallas_call(
        body, name=name, grid=(s // tm,),
        in_specs=[rows, _full_spec((d, d)), rows, rows, _row_spec(tm, d, 2), _row_spec(tm, d, 3)],
        out_specs=[rows] * 4, out_shape=[o, o, o, o], compiler_params=_cparams(("parallel",)),
    )(dx1b, w_out, ya, yb, zuvg, zuvg)


def _masked_ws(ws_ref, g):
    row = lax.broadcasted_iota(I32, (GBLOCK, GBLOCK), 0)
    col = lax.broadcasted_iota(I32, (GBLOCK, GBLOCK), 1)
    keep = (col // CHUNK) <= (row // CHUNK)
    return jnp.where(keep, ws_ref[g], 0.0), keep


def _layernorm_parts(zv):
    mu = jnp.mean(zv, axis=-1, keepdims=True)
    xc = zv - mu
    rs = lax.rsqrt(jnp.mean(xc * xc, axis=-1, keepdims=True) + EPS)
    return xc * rs, rs


def _gmlp_fwd(zuvg, ln_g, ln_b, w_s, bs_t, *, name):
    s, w = zuvg.shape[0], GROUPS * GDIM

    def body(zu_ref, zv_ref, lng_ref, lnb_ref, ws_ref, bs_ref, a_ref):
        zu = _gelu(zu_ref[...])
        zv = _gelu(zv_ref[...])
        xhat, _ = _layernorm_parts(zv)
        vln = (xhat * lng_ref[...] + lnb_ref[...]).astype(BF16)
        for g in range(GROUPS):
            wm, _ = _masked_ws(ws_ref, g)
            mixed = _dot(wm.astype(BF16), vln[:, g * GDIM:(g + 1) * GDIM], NN) + bs_ref[:, g:g + 1]
            a_ref[:, g * GDIM:(g + 1) * GDIM] = (zu[:, g * GDIM:(g + 1) * GDIM] * mixed).astype(BF16)

    return pl.pallas_call(
        body, name=name, grid=(s // GBLOCK,),
        in_specs=[_row_spec(GBLOCK, w, 0), _row_spec(GBLOCK, w, 1), _full_spec((1, w)), _full_spec((1, w)),
                  _full_spec((GROUPS, GBLOCK, GBLOCK)), _full_spec((GBLOCK, 128))],
        out_specs=_row_spec(GBLOCK, w),
        out_shape=jax.ShapeDtypeStruct((s, w), BF16), compiler_params=_cparams(("parallel",)),
    )(zuvg, zuvg, ln_g, ln_b, w_s, bs_t)


def _gmlp_bwd(da, zuvg, ln_g, ln_b, w_s, bs_t, carry=None, *, name):
    s, w = zuvg.shape[0], GROUPS * GDIM

    def body(da_ref, zu_ref, zv_ref, lng_ref, lnb_ref, ws_ref, bs_ref,
             dzu_ref, dzv_ref, dws_ref, dbs_ref, dlng_ref, dlnb_ref, dvln_ref):
        i = pl.program_id(0)
        zu, dzu_g = _gelu_and_grad(zu_ref[...])
        zv, dzv_g = _gelu_and_grad(zv_ref[...])
        xhat, rs = _layernorm_parts(zv)
        vln = (xhat * lng_ref[...] + lnb_ref[...]).astype(BF16)
        dav = da_ref[...].astype(F32)
        lane = lax.broadcasted_iota(I32, (GBLOCK, 128), 1)
        dbs = jnp.zeros((GBLOCK, 128), F32)

        @pl.when(i == 0)
        def _():
            dws_ref[...] = jnp.zeros_like(dws_ref)

        for g in range(GROUPS):
            sl = slice(g * GDIM, (g + 1) * GDIM)
            wm, keep = _masked_ws(ws_ref, g)
            wmb = wm.astype(BF16)
            vg = vln[:, sl]
            mixed = _dot(wmb, vg, NN) + bs_ref[:, g:g + 1]
            dag = dav[:, sl]
            dzu_ref[:, sl] = (dag * mixed * dzu_g[:, sl]).astype(BF16)
            dmix = dag * zu[:, sl]
            dmb = dmix.astype(BF16)
            dws_ref[g] += jnp.where(keep, _dot(dmb, vg, NT), 0.0)
            dbs = jnp.where(lane == g, jnp.sum(dmix, axis=1, keepdims=True), dbs)
            dvln_ref[:, sl] = _dot(wmb, dmb, TN)
        dvln = dvln_ref[...]
        dxhat = dvln * lng_ref[...]
        dzv = rs * (dxhat - jnp.mean(dxhat, axis=-1, keepdims=True)
                    - xhat * jnp.mean(dxhat * xhat, axis=-1, keepdims=True))
        dzv_ref[...] = (dzv * dzv_g).astype(BF16)
        dlng = jnp.sum(dvln * xhat, axis=0, keepdims=True)
        dlnb = jnp.sum(dvln, axis=0, keepdims=True)

        @pl.when(i == 0)
        def _():
            dbs_ref[...] = dbs
            dlng_ref[...] = dlng
            dlnb_ref[...] = dlnb

        @pl.when(i > 0)
        def _():
            dbs_ref[...] += dbs
            dlng_ref[...] += dlng
            dlnb_ref[...] += dlnb

    return _carry_call(
        body, carry, name=name, grid=(s // GBLOCK,),
        in_specs=[_row_spec(GBLOCK, w), _row_spec(GBLOCK, w, 0), _row_spec(GBLOCK, w, 1), _full_spec((1, w)),
                  _full_spec((1, w)), _full_spec((GROUPS, GBLOCK, GBLOCK)), _full_spec((GBLOCK, 128))],
        out_specs=[_row_spec(GBLOCK, w), _row_spec(GBLOCK, w), _full_spec((GROUPS, GBLOCK, GBLOCK)),
                   _full_spec((GBLOCK, 128)), _full_spec((1, w)), _full_spec((1, w))],
        out_shape=[jax.ShapeDtypeStruct((s, w), BF16), jax.ShapeDtypeStruct((s, w), BF16),
                   jax.ShapeDtypeStruct((GROUPS, GBLOCK, GBLOCK), F32), jax.ShapeDtypeStruct((GBLOCK, 128), F32),
                   jax.ShapeDtypeStruct((1, w), F32), jax.ShapeDtypeStruct((1, w), F32)],
        scratch_shapes=[pltpu.VMEM((GBLOCK, w), F32)], args=[da, zuvg, zuvg, ln_g, ln_b, w_s, bs_t])


def _shift_down(u, k):
    row = lax.broadcasted_iota(I32, u.shape, 0)
    return jnp.where(row >= k, pltpu.roll(u, k, 0), 0.0)


def _shift_up(u, k):
    s = u.shape[0]
    row = lax.broadcasted_iota(I32, u.shape, 0)
    return jnp.where(row < s - k, pltpu.roll(u, s - k, 0), 0.0)


def _conv(u, w_ref, b_ref):
    return b_ref[...] + w_ref[0:1, :] * _shift_down(u, 2) + w_ref[1:2, :] * _shift_down(u, 1) + w_ref[2:3, :] * u


def _conv_specs(s, f, tc):
    nc = f // tc
    half = lambda rows: [pl.BlockSpec((rows, tc), lambda j: (0, j)), pl.BlockSpec((rows, tc), lambda j: (0, nc + j))]
    return half(s), half(3), half(1)


def _convglu_fwd(up, conv_w, conv_b, *, name, tc=256):
    s, f = up.shape[0], up.shape[1] // 2
    up_specs, w_specs, b_specs = _conv_specs(s, f, tc)

    def body(ua_ref, ug_ref, wa_ref, wg_ref, ba_ref, bg_ref, o_ref):
        ca = _conv(ua_ref[...], wa_ref, ba_ref)
        cg = _conv(ug_ref[...], wg_ref, bg_ref)
        o_ref[...] = (_gelu(ca) * cg).astype(BF16)

    return pl.pallas_call(
        body, name=name, grid=(f // tc,),
        in_specs=up_specs + w_specs + b_specs, out_specs=up_specs[0],
        out_shape=jax.ShapeDtypeStruct((s, f), BF16), compiler_params=_cparams(("parallel",)),
    )(up, up, conv_w, conv_w, conv_b, conv_b)


def _convglu_bwd(dact, up, conv_w, conv_b, *, name, tc=256):
    s, f = up.shape[0], up.shape[1] // 2
    up_specs, w_specs, b_specs = _conv_specs(s, f, tc)

    def half(dc, taps, w_ref, du_ref, dw_ref, db_ref):
        db_ref[...] = jnp.sum(dc, axis=0, keepdims=True)
        for k in range(3):
            dw_ref[k:k + 1, :] = jnp.sum(dc * taps[k], axis=0, keepdims=True)
        du = w_ref[2:3, :] * dc + w_ref[1:2, :] * _shift_up(dc, 1) + w_ref[0:1, :] * _shift_up(dc, 2)
        du_ref[...] = du.astype(BF16)

    def body(d_ref, ua_ref, ug_ref, wa_ref, wg_ref, ba_ref, bg_ref,
             dua_ref, dug_ref, dwa_ref, dwg_ref, dba_ref, dbg_ref):
        taps_a = (_shift_down(ua_ref[...], 2), _shift_down(ua_ref[...], 1), ua_ref[...])
        taps_g = (_shift_down(ug_ref[...], 2), _shift_down(ug_ref[...], 1), ug_ref[...])
        conv = lambda taps, w_ref, b_ref: b_ref[...] + w_ref[0:1, :] * taps[0] + w_ref[1:2, :] * taps[1] + w_ref[2:3, :] * taps[2]
        ca = conv(taps_a, wa_ref, ba_ref)
        cg = conv(taps_g, wg_ref, bg_ref)
        ga, dga = _gelu_and_grad(ca)
        dv = d_ref[...].astype(F32)
        half(dv * cg * dga, taps_a, wa_ref, dua_ref, dwa_ref, dba_ref)
        half(dv * ga, taps_g, wg_ref, dug_ref, dwg_ref, dbg_ref)

    col, w3, b1 = up_specs[0], w_specs[0], b_specs[0]
    return pl.pallas_call(
        body, name=name, grid=(f // tc,),
        in_specs=[col] + up_specs + w_specs + b_specs, out_specs=[col, col, w3, w3, b1, b1],
        out_shape=[jax.ShapeDtypeStruct((s, f), BF16), jax.ShapeDtypeStruct((s, f), BF16),
                   jax.ShapeDtypeStruct((3, f), F32), jax.ShapeDtypeStruct((3, f), F32),
                   jax.ShapeDtypeStruct((1, f), F32), jax.ShapeDtypeStruct((1, f), F32)],
        compiler_params=_cparams(("parallel",)),
    )(dact, up, up, conv_w, conv_w, conv_b, conv_b)


def _tri_dot(tri, x):
    b0 = x.astype(BF16)
    r1 = x - b0.astype(F32)
    b1 = r1.astype(BF16)
    b2 = (r1 - b1.astype(F32)).astype(BF16)
    return _dot(tri, b0, NN) + _dot(tri, b1, NN) + _dot(tri, b2, NN)


def _log_sigmoid(x):
    return jnp.minimum(x, 0.0) - jnp.log(1.0 + jnp.exp(-jnp.abs(x)))


def _expand_heads(col16, rows):
    src = lax.broadcasted_iota(I32, (128, HEADS * HEAD_DIM), 0)
    dst = lax.broadcasted_iota(I32, (128, HEADS * HEAD_DIM), 1) // HEAD_DIM
    spread = (src == dst).astype(BF16)
    p0, p1, p2 = _bf16_pieces(col16)
    return (_dot(p0.astype(BF16), spread, NN) + _dot(p1.astype(BF16), spread, NN)) + _dot(p2.astype(BF16), spread, NN)


def _forget_cumsum(f_logit, b_f, *, name):
    s = f_logit.shape[0]
    nb = s // 128

    def body(f_ref, b_ref, cqe_ref):
        row = lax.broadcasted_iota(I32, (128, 128), 0)
        col = lax.broadcasted_iota(I32, (128, 128), 1)
        tri = (col <= row).astype(BF16)

        def step(n, carry):
            r0 = pl.multiple_of(n * 128, 128)
            lf = _log_sigmoid(f_ref[pl.ds(r0, 128), :] + b_ref[...])
            cum = _tri_dot(tri, lf) + carry
            cqe_ref[pl.ds(r0, 128), :] = _expand_heads(cum, 128)
            return cum[127:128, :]

        lax.fori_loop(0, nb, step, jnp.zeros((1, 128), F32))

    return pl.pallas_call(
        body, name=name, grid=(1,),
        in_specs=[_full_spec((s, 128)), _full_spec((1, 128))],
        out_specs=_full_spec((s, HEADS * HEAD_DIM)),
        out_shape=jax.ShapeDtypeStruct((s, HEADS * HEAD_DIM), F32),
        compiler_params=_cparams(("arbitrary",)),
    )(f_logit, b_f)


def _forget_bwd(dcq16, sum_q16, f_logit, b_f, *, name):
    s = f_logit.shape[0]
    nb = s // 128

    def body(a_ref, k_ref, f_ref, b_ref, df_ref, db_ref):
        row = lax.broadcasted_iota(I32, (128, 128), 0)
        col = lax.broadcasted_iota(I32, (128, 128), 1)
        tri_rev = (col >= row).astype(BF16)

        def step(m, carry):
            suffix, dbsum = carry
            n = nb - 1 - m
            r0 = pl.multiple_of(n * 128, 128)
            dcum = a_ref[pl.ds(r0, 128), :] - k_ref[pl.ds(r0, 128), :]
            dlf = _tri_dot(tri_rev, dcum) + suffix
            df = dlf * _sigmoid(-(f_ref[pl.ds(r0, 128), :] + b_ref[...]))
            df_ref[pl.ds(r0, 128), :] = df.astype(BF16)
            return dlf[0:1, :], dbsum + jnp.sum(df, axis=0, keepdims=True)

        _, dbsum = lax.fori_loop(0, nb, step, (jnp.zeros((1, 128), F32), jnp.zeros((1, 128), F32)))
        db_ref[...] = dbsum

    return pl.pallas_call(
        body, name=name, grid=(1,),
        in_specs=[_full_spec((s, 128))] * 3 + [_full_spec((1, 128))],
        out_specs=[_full_spec((s, 128)), _full_spec((1, 128))],
        out_shape=[jax.ShapeDtypeStruct((s, 128), BF16), jax.ShapeDtypeStruct((1, 128), F32)],
        compiler_params=_cparams(("arbitrary",)),
    )(dcq16, sum_q16, f_logit, b_f)


ATT_T = 256


def _head_lanes(rows):
    return lax.broadcasted_iota(I32, (rows, 128), 1) < HEAD_DIM


def _bf16_pieces(c):
    p0 = c.astype(BF16).astype(F32)
    r = c - p0
    p1 = r.astype(BF16).astype(F32)
    p2 = (r - p1).astype(BF16).astype(F32)
    return p0, p1, p2


def _col_reduce(x, op):
    rows = x.shape[0]
    while rows > 8:
        rows //= 2
        x = op(x[:rows], x[rows:])
    return jnp.max(x, axis=0, keepdims=True) if op is jnp.maximum else jnp.sum(x, axis=0, keepdims=True)


def _attn_prep(qkv, cqe, carry=None, *, name):
    s = qkv.shape[0]
    npair = HEADS // 2

    def body(q_ref, k_ref, v_ref, c_ref, qa_ref, ka_ref, vt_ref):
        rows = 128
        lane = lax.broadcasted_iota(I32, (rows, 128), 1)

        def chunk(n, _):
            r0 = pl.multiple_of(n * rows, rows)
            sl = pl.ds(r0, rows)
            qv = q_ref[sl, :].astype(F32) * ATT_SCALE
            kv = k_ref[sl, :].astype(F32)
            p0, p1, p2 = _bf16_pieces(pltpu.roll(c_ref[sl, :], HEAD_DIM, 1))
            for e in range(2):
                mine = (lane < HEAD_DIM) if e == 0 else (lane >= HEAD_DIM)
                base = HEAD_DIM * (1 - e)
                ones_hi = jnp.where((lane >= base + 3) & (lane < base + 6), 1.0, 0.0)
                ones_lo = jnp.where((lane >= base) & (lane < base + 3), 1.0, 0.0)
                qa = jnp.where(mine, qv, jnp.where(lane == base, p0, jnp.where(lane == base + 1, p1,
                               jnp.where(lane == base + 2, p2, ones_hi))))
                ka = jnp.where(mine, kv, jnp.where(lane == base + 3, -p0, jnp.where(lane == base + 4, -p1,
                               jnp.where(lane == base + 5, -p2, ones_lo))))
                qa_ref[e, sl, :] = qa.astype(BF16)
                ka_ref[e, sl, :] = ka.astype(BF16)
            vt_ref[0, :, sl] = v_ref[sl, :].astype(F32).T.astype(BF16)
            return 0

        lax.fori_loop(0, s // rows, chunk, 0)

    pair = pl.BlockSpec((2, s, 128), lambda hp: (hp, 0, 0))
    return _carry_call(
        body, carry, name=name, grid=(npair,),
        in_specs=[pl.BlockSpec((s, 128), lambda hp: (0, hp)), pl.BlockSpec((s, 128), lambda hp: (0, npair + hp)),
                  pl.BlockSpec((s, 128), lambda hp: (0, 2 * npair + hp)), pl.BlockSpec((s, 128), lambda hp: (0, hp))],
        out_specs=[pair, pair, pl.BlockSpec((1, 128, s), lambda hp: (hp, 0, 0))],
        out_shape=[jax.ShapeDtypeStruct((HEADS, s, 128), BF16), jax.ShapeDtypeStruct((HEADS, s, 128), BF16),
                   jax.ShapeDtypeStruct((npair, 128, s), BF16)],
        scratch_shapes=[], args=[qkv, qkv, qkv, cqe])


def _attn_fwd(qa, ka, vt, carry=None, *, name):
    s = qa.shape[1]
    t = 2 * ATT_T
    nq = s // t
    npair = HEADS // 2

    def body(qa_ref, ka_ref, vt_ref, o_ref, lse_ref):
        i = pl.program_id(1)
        krow = lax.broadcasted_iota(I32, (t, t), 0)
        qcol = lax.broadcasted_iota(I32, (t, t), 1)
        sub = lax.broadcasted_iota(I32, (128, t), 0)
        row8 = lax.broadcasted_iota(I32, (8, t), 0)
        qbs = (qa_ref[0], qa_ref[1])
        tk = t

        def step(j, carry, diag):
            c0 = pl.multiple_of(j * tk, tk)
            vtb = vt_ref[0, :, pl.ds(c0, tk)]
            sts = [_dot(ka_ref[e, pl.ds(c0, tk), :], qbs[e], NT) for e in range(2)]
            if diag:
                sts = [jnp.where(krow <= qcol, st, NEG) for st in sts]
            pts, stats = [], []
            for e in range(2):
                m, l, _ = carry[e]
                m_new = jnp.maximum(m, _col_reduce(sts[e], jnp.maximum))
                alpha = jnp.exp(m - m_new)
                pt = jnp.exp(sts[e] - m_new)
                stats.append((m_new, alpha, alpha * l + _col_reduce(pt, jnp.add)))
                pts.append(pt.astype(BF16))
            pvs = [_dot(vtb, pts[e], NN) for e in range(2)]
            return tuple((stats[e][0], stats[e][2], stats[e][1] * carry[e][2] + pvs[e]) for e in range(2))

        init = (jnp.full((1, t), NEG, F32), jnp.zeros((1, t), F32), jnp.zeros((128, t), F32))
        carry = lax.fori_loop(0, i, functools.partial(step, diag=False), (init, init))
        (m0, l0, acc0), (m1, l1, acc1) = step(i, carry, True)
        o_pair = jnp.where(sub < HEAD_DIM, acc0 / l0, acc1 / l1)
        o_ref[...] = o_pair.T.astype(BF16)
        lse_ref[0] = jnp.where(row8 == 0, m0 + jnp.log(l0), jnp.where(row8 == 1, m1 + jnp.log(l1), 0.0))

    return _carry_call(
        body, carry, name=name, grid=(npair, nq),
        in_specs=[pl.BlockSpec((2, t, 128), lambda hp, i: (hp, i, 0)), pl.BlockSpec((2, s, 128), lambda hp, i: (hp, 0, 0)),
                  pl.BlockSpec((1, 128, s), lambda hp, i: (hp, 0, 0))],
        out_specs=[pl.BlockSpec((t, 128), lambda hp, i: (i, hp)), pl.BlockSpec((1, 8, t), lambda hp, i: (hp, 0, i))],
        out_shape=[jax.ShapeDtypeStruct((s, HEADS * HEAD_DIM), BF16), jax.ShapeDtypeStruct((npair, 8, s), F32)],
        scratch_shapes=[], args=[qa, ka, vt])


def _attn_delta(do, o, carry=None, *, name):
    s = do.shape[0]

    def body(do_ref, o_ref, d_ref):
        prod = do_ref[...].astype(F32) * o_ref[...].astype(F32)
        row = lax.broadcasted_iota(I32, (8, 128), 0)
        lane = lax.broadcasted_iota(I32, (8, 128), 1)
        sel = ((row == 0) & (lane < HEAD_DIM) | (row == 1) & (lane >= HEAD_DIM)).astype(BF16)
        p0, p1, p2 = _bf16_pieces(prod)
        d_ref[0] = (_dot(sel, p0.astype(BF16), NT) + _dot(sel, p1.astype(BF16), NT)) + _dot(sel, p2.astype(BF16), NT)

    pair = pl.BlockSpec((s, 128), lambda hp: (0, hp))
    (delta3,), carried = _carry_call(
        body, carry, name=name, grid=(HEADS // 2,), in_specs=[pair, pair],
        out_specs=[pl.BlockSpec((1, 8, s), lambda hp: (hp, 0, 0))],
        out_shape=[jax.ShapeDtypeStruct((HEADS // 2, 8, s), F32)], scratch_shapes=[], args=[do, o])
    return delta3, carried


def _attn_bwd(qa, ka, qkv, do, lse3, delta3, carry=None, *, name):
    s = qa.shape[1]
    t = 2 * ATT_T
    nb = s // t
    npair = HEADS // 2

    def body(qa_ref, ka_ref, v_ref, do_ref, lse_ref, delta_ref, dq_ref, dk_ref, dv_ref, aux_ref, dcq_ref, dqt):
        hp = pl.program_id(0)
        first = _head_lanes(t)
        lane = lax.broadcasted_iota(I32, (t, 128), 1)
        dqt[...] = jnp.zeros_like(dqt)

        @pl.when(hp == 0)
        def _():
            aux_ref[...] = jnp.zeros_like(aux_ref)

        krow = lax.broadcasted_iota(I32, (t, t), 0)
        qcol = lax.broadcasted_iota(I32, (t, t), 1)

        def key_block(j, _):
            c0 = pl.multiple_of(j * t, t)
            vb = v_ref[pl.ds(c0, t), :]
            kbs = (ka_ref[0, pl.ds(c0, t), :], ka_ref[1, pl.ds(c0, t), :])
            kbts = tuple(kb.astype(F32).T.astype(BF16) for kb in kbs)
            vhs = (jnp.where(first, vb, jnp.zeros_like(vb)), jnp.where(first, jnp.zeros_like(vb), vb))

            def query_block(i, carry, diag):
                r0 = pl.multiple_of(i * t, t)
                dob = do_ref[pl.ds(r0, t), :]
                sts = [_dot(kbs[e], qa_ref[e, pl.ds(r0, t), :], NT) for e in range(2)]
                dpts = [_dot(vhs[e], dob, NT) for e in range(2)]
                ptbs, dsbs = [], []
                for e in range(2):
                    st = jnp.where(krow <= qcol, sts[e], NEG) if diag else sts[e]
                    pt = jnp.exp(st - lse_ref[0, e:e + 1, pl.ds(r0, t)])
                    dsbs.append((pt * (dpts[e] - delta_ref[0, e:e + 1, pl.ds(r0, t)])).astype(BF16))
                    ptbs.append(pt.astype(BF16))
                out = []
                for e in range(2):
                    dk_a, dv_a = carry[e]
                    dv_a = dv_a + _dot(ptbs[e], dob, NN)
                    dk_a = dk_a + _dot(dsbs[e], qa_ref[e, pl.ds(r0, t), :], NN)
                    dqt[e, :, pl.ds(r0, t)] += _dot(kbts[e], dsbs[e], NN)
                    out.append((dk_a, dv_a))
                return tuple(out)

            zero = jnp.zeros((t, 128), F32)
            carry = query_block(j, ((zero, zero), (zero, zero)), True)
            (dk0, dv0), (dk1, dv1) = lax.fori_loop(j + 1, nb, functools.partial(query_block, diag=False), carry)
            dk_ref[pl.ds(c0, t), :] = jnp.where(first, dk0, dk1).astype(BF16)
            dv_ref[pl.ds(c0, t), :] = jnp.where(first, dv0, dv1).astype(BF16)
            sum_q = jnp.where(lane == 2 * hp, dk0[:, HEAD_DIM + 3:HEAD_DIM + 4],
                              jnp.where(lane == 2 * hp + 1, dk1[:, 3:4], aux_ref[pl.ds(c0, t), :]))
            aux_ref[pl.ds(c0, t), :] = sum_q
            return 0

        lax.fori_loop(0, nb, key_block, 0)
        sub = lax.broadcasted_iota(I32, (128, s), 0)
        row8 = lax.broadcasted_iota(I32, (8, s), 0)
        dq_ref[...] = (jnp.where(sub < HEAD_DIM, dqt[0], dqt[1]) * ATT_SCALE).T.astype(BF16)
        dcq_ref[0] = jnp.where(row8 == 0, dqt[0, HEAD_DIM:HEAD_DIM + 1, :], jnp.where(row8 == 1, dqt[1, 0:1, :], 0.0))

    def pair_cols(off):
        return pl.BlockSpec((s, 128), lambda hp: (0, off + hp))

    heads = pl.BlockSpec((2, s, 128), lambda hp: (hp, 0, 0))
    rows = pl.BlockSpec((1, 8, s), lambda hp: (hp, 0, 0))
    wide = jax.ShapeDtypeStruct((s, HEADS * HEAD_DIM), BF16)
    return _carry_call(
        body, carry, name=name, grid=(npair,),
        in_specs=[heads, heads, pair_cols(2 * npair), pair_cols(0), rows, rows],
        out_specs=[pair_cols(0), pair_cols(0), pair_cols(0), pl.BlockSpec((s, 128), lambda hp: (0, 0)), rows],
        out_shape=[wide, wide, wide, jax.ShapeDtypeStruct((s, 128), F32), jax.ShapeDtypeStruct((npair, 8, s), F32)],
        scratch_shapes=[pltpu.VMEM((2, 128, s), F32)], args=[qa, ka, qkv, do, lse3, delta3])


def _adam_math(w, g, m, v):
    m = ADAM_B1 * m + (1.0 - ADAM_B1) * g
    v = ADAM_B2 * v + (1.0 - ADAM_B2) * (g * g)
    m_hat = m / (1.0 - ADAM_B1 ** ADAM_STEP)
    v_hat = v / (1.0 - ADAM_B2 ** ADAM_STEP)
    delta = -ADAM_LR * (m_hat / (jnp.sqrt(v_hat) + ADAM_EPS) + ADAM_WD * w)
    return delta, m, v


def _sum_pairs(keep, recv, pos, *, name):
    _, r, c = recv.shape
    tr = _row_tile(r, 512)

    def body(pos_ref, a_ref, b_ref, o32_ref, o16_ref):
        tot = a_ref[...].astype(F32) + b_ref[...].astype(F32)
        o16_ref[...] = tot.astype(BF16)

        @pl.when(pl.program_id(1) == 2 * pos_ref[0] + pos_ref[1])
        def _():
            o32_ref[...] = tot

    out = pl.BlockSpec((1, tr, c), lambda i, q, pos: (q, i, 0))
    grid_spec = pltpu.PrefetchScalarGridSpec(
        num_scalar_prefetch=1, grid=(r // tr, 4),
        in_specs=[pl.BlockSpec((1, tr, c), lambda i, q, pos: (2 * q + pos[2], i, 0)), out],
        out_specs=[pl.BlockSpec((1, tr, c), lambda i, q, pos: (0, i, 0)), out])
    return pl.pallas_call(
        body, name=name, grid_spec=grid_spec,
        out_shape=[jax.ShapeDtypeStruct((1, r, c), F32), jax.ShapeDtypeStruct((4, r, c), BF16)],
        compiler_params=_cparams(("arbitrary", "arbitrary")),
    )(pos, keep, recv)


def _adam_sharded(psum, recv, w, m, v, pos, *, name):
    r, c = w.shape
    tr = _row_tile(r, 320)

    def body(pos_ref, p_ref, r_ref, w_ref, m_ref, v_ref, g_ref, d_ref, mo_ref, vo_ref):
        g = p_ref[0] + r_ref[0].astype(F32) + r_ref[1].astype(F32) + r_ref[2].astype(F32)
        delta, mn, vn = _adam_math(w_ref[...], g, m_ref[...], v_ref[...])
        g_ref[...] = g
        d_ref[...] = delta
        mo_ref[...] = mn
        vo_ref[...] = vn

    row = pl.BlockSpec((tr, c), lambda i, pos: (i, 0))
    grid_spec = pltpu.PrefetchScalarGridSpec(
        num_scalar_prefetch=1, grid=(r // tr,),
        in_specs=[pl.BlockSpec((1, tr, c), lambda i, pos: (0, i, 0)),
                  pl.BlockSpec((3, tr, c), lambda i, pos: (0, i, 0)), row, row, row],
        out_specs=[row, row, row, row])
    o = jax.ShapeDtypeStruct((r, c), F32)
    return pl.pallas_call(
        body, name=name, grid_spec=grid_spec, out_shape=[o, o, o, o],
        compiler_params=_cparams(("parallel",)),
    )(pos, psum, recv, w, m, v)


def _adam_replicated(chip_sums, last, w, m, v, *, name):
    r = w.shape[0]

    def body(s_ref, l_ref, w_ref, m_ref, v_ref, g_ref, d_ref, mo_ref, vo_ref):
        g = (((s_ref[0] + s_ref[1]) + s_ref[2]) + s_ref[3]) + l_ref[...]
        delta, mn, vn = _adam_math(w_ref[...], g, m_ref[...], v_ref[...])
        g_ref[...] = g
        d_ref[...] = delta
        mo_ref[...] = mn
        vo_ref[...] = vn

    o = jax.ShapeDtypeStruct((r, 1024), F32)
    full = _full_spec((r, 1024))
    return pl.pallas_call(
        body, name=name, grid=(1,),
        in_specs=[_full_spec((4, r, 1024)), full, full, full, full], out_specs=[full] * 4, out_shape=[o] * 4,
        compiler_params=_cparams(("arbitrary",)),
    )(chip_sums, last, w, m, v)


ASM_OUT = 256
ASM_SRC = 304


def _w_in_row(r):
    return r if r < 2048 else (r + O_G - 2048 if r < 4096 else r - 2048)


def _assemble_wt_main(g, *, name):
    table = []
    for blk in range(MAIN_COLS // ASM_OUT):
        j, l0 = divmod(_w_in_row(blk * ASM_OUT), IN_SHARD)
        sb = l0 // ASM_SRC
        n_a = min(ASM_OUT, min(IN_SHARD, (sb + 1) * ASM_SRC) - l0)
        if n_a == ASM_OUT:
            nxt = (j, sb)
        elif l0 + n_a == IN_SHARD:
            nxt = (j + 1, 0)
        else:
            nxt = (j, sb + 1)
        table.append((j, sb, l0 - sb * ASM_SRC, n_a) + nxt)

    def body(tab_ref, a_ref, b_ref, o_ref):
        blk = pl.program_id(0)
        off, n_a = tab_ref[blk, 2], tab_ref[blk, 3]
        r = lax.broadcasted_iota(I32, (ASM_OUT, ASM_SRC), 0)
        k = lax.broadcasted_iota(I32, (ASM_OUT, ASM_SRC), 1)
        sel_a = ((k == r + off) & (r < n_a)).astype(BF16)
        sel_b = ((k == r - n_a) & (r >= n_a)).astype(BF16)
        o_ref[...] = (_dot(sel_a, a_ref[0], NN) + _dot(sel_b, b_ref[0], NN)).astype(BF16)

    src = lambda c: pl.BlockSpec((1, ASM_SRC, D_MODEL), lambda blk, tab: (tab[blk, c], tab[blk, c + 1], 0))
    grid_spec = pltpu.PrefetchScalarGridSpec(
        num_scalar_prefetch=1, grid=(len(table),), in_specs=[src(0), src(4)],
        out_specs=pl.BlockSpec((ASM_OUT, D_MODEL), lambda blk, tab: (blk, 0)))
    return pl.pallas_call(
        body, name=name, grid_spec=grid_spec, out_shape=jax.ShapeDtypeStruct((MAIN_COLS, D_MODEL), BF16),
        compiler_params=_cparams(("parallel",)),
    )(jnp.asarray(table, I32), g, g)


def _pair_sum_small(mine, theirs, *, name):
    def body(a_ref, b_ref, o_ref):
        o_ref[...] = a_ref[...] + b_ref[...]

    full = _full_spec(mine.shape)
    return pl.pallas_call(
        body, name=name, grid=(1,), in_specs=[full, full], out_specs=full,
        out_shape=jax.ShapeDtypeStruct(mine.shape, F32), compiler_params=_cparams(("arbitrary",)),
    )(mine, theirs)


ANY = pl.BlockSpec(memory_space=pl.ANY)
OTHER_CHIPS = ((1, 0), (0, 1), (1, 1))


class _Carry:
    def __init__(self, inputs, out_shapes, scratch, start, wait, aliases=None):
        self.inputs, self.out_shapes, self.scratch = list(inputs), list(out_shapes), list(scratch)
        self.start, self.wait, self.aliases = start, wait, dict(aliases or {})


def _carry_join(*carries):
    n_in = [len(c.inputs) for c in carries]
    n_out = [len(c.out_shapes) for c in carries]
    n_scr = [len(c.scratch) for c in carries]

    def split(refs, counts):
        out, k = [], 0
        for n in counts:
            out.append(refs[k:k + n])
            k += n
        return out

    def start(ins, outs, scr):
        for c, i, o, s in zip(carries, split(ins, n_in), split(outs, n_out), split(scr, n_scr)):
            c.start(i, o, s)

    def wait(ins, outs, scr):
        for c, i, o, s in zip(carries, split(ins, n_in), split(outs, n_out), split(scr, n_scr)):
            c.wait(i, o, s)

    aliases = {}
    for k, c in enumerate(carries):
        aliases.update({sum(n_in[:k]) + i: sum(n_out[:k]) + o for i, o in c.aliases.items()})
    joined = _Carry(sum((c.inputs for c in carries), []), sum((c.out_shapes for c in carries), []),
                    sum((c.scratch for c in carries), []), start, wait, aliases)
    joined.counts = n_out
    joined.split = lambda results: split(results, n_out)
    return joined


def _carried(body, carry, n_in, n_out, grid):
    if carry is None:
        return body
    ci, co, cs = len(carry.inputs), len(carry.out_shapes), len(carry.scratch)

    def wrapped(*refs):
        ins, cins = refs[:n_in], refs[n_in:n_in + ci]
        outs, couts = refs[n_in + ci:n_in + ci + n_out], refs[n_in + ci + n_out:n_in + ci + n_out + co]
        rest = refs[n_in + ci + n_out + co:]
        scratch, cscr = rest[:len(rest) - cs], rest[len(rest) - cs:]
        first, last = None, None
        for axis, size in enumerate(grid):
            f, l = pl.program_id(axis) == 0, pl.program_id(axis) == size - 1
            first = f if first is None else first & f
            last = l if last is None else last & l

        @pl.when(first)
        def _():
            carry.start(cins, couts, cscr)

        body(*ins, *outs, *scratch)

        @pl.when(last)
        def _():
            carry.wait(cins, couts, cscr)

    return wrapped


def _carry_call(body, carry, *, name, grid, in_specs, out_specs, out_shape, scratch_shapes, args, vmem=True,
                own_aliases=None):
    n_in, n_out = len(in_specs), len(out_specs)
    extra_in = [ANY] * len(carry.inputs) if carry else []
    extra_out = [ANY] * len(carry.out_shapes) if carry else []
    aliases = dict(own_aliases or {})
    if carry:
        aliases.update({n_in + i: n_out + o for i, o in carry.aliases.items()})
    out = pl.pallas_call(
        _carried(body, carry, n_in, n_out, grid), name=name, grid=grid,
        in_specs=list(in_specs) + extra_in, out_specs=list(out_specs) + extra_out,
        out_shape=list(out_shape) + (carry.out_shapes if carry else []),
        scratch_shapes=list(scratch_shapes) + (carry.scratch if carry else []),
        input_output_aliases=aliases,
        compiler_params=_cparams(("arbitrary",) * len(grid)) if vmem else None,
    )(*args, *(carry.inputs if carry else []))
    return list(out[:n_out]), list(out[n_out:])


def _run_carry(carry, *, name):
    return _carry_call(lambda: None, carry, name=name, grid=(1,), in_specs=[], out_specs=[], out_shape=[],
                       scratch_shapes=[], args=[], vmem=False)[1]


def _sems(n):
    return [pltpu.SemaphoreType.DMA((n,)), pltpu.SemaphoreType.DMA((n,))]


def _carry_gather1(shards):
    n = len(shards)

    def copies(x_refs, out_refs, scr, with_arrivals):
        send_sems, recv_sems, local_sems = scr
        x, y, c = lax.axis_index("x"), lax.axis_index("y"), lax.axis_index("c")
        peers = [(x, y, 1 - c)] + [(x ^ fx, y ^ fy, c) for fx, fy in OTHER_CHIPS]
        local, sends, arrivals = [], [], []
        for t, (x_ref, out_ref) in enumerate(zip(x_refs, out_refs)):
            local.append(pltpu.make_async_copy(x_ref, out_ref.at[4 * x + 2 * y + c], local_sems.at[t]))
            for k, (px, py, pc) in enumerate(peers):
                sems = dict(send_sem=send_sems.at[4 * t + k], recv_sem=recv_sems.at[4 * t + k],
                            device_id=(px, py, pc), device_id_type=MESH)
                sends.append(pltpu.make_async_remote_copy(src_ref=x_ref, dst_ref=out_ref.at[4 * x + 2 * y + c], **sems))
                if with_arrivals:
                    arrivals.append(
                        pltpu.make_async_remote_copy(src_ref=x_ref, dst_ref=out_ref.at[4 * px + 2 * py + pc], **sems))
        return local, sends, arrivals

    def start(x_refs, out_refs, scr):
        local, sends, _ = copies(x_refs, out_refs, scr, False)
        for cp in local + sends:
            cp.start()

    def wait(x_refs, out_refs, scr):
        local, sends, arrivals = copies(x_refs, out_refs, scr, True)
        for cp in arrivals:
            cp.wait_recv()
        for cp in sends:
            cp.wait_send()
        for cp in local:
            cp.wait()

    return _Carry(shards, [jax.ShapeDtypeStruct((N_DEV,) + a.shape, a.dtype) for a in shards],
                  _sems(4 * n) + [pltpu.SemaphoreType.DMA((n,))], start, wait)


def _carry_gather2(gathered):
    n = len(gathered)

    def copies(in_refs, g_refs, scr, with_arrivals):
        send_sems, recv_sems = scr
        x, y, c = lax.axis_index("x"), lax.axis_index("y"), lax.axis_index("c")
        sends, arrivals = [], []
        for t in range(n):
            for j, (fx, fy) in enumerate(OTHER_CHIPS):
                px, py = x ^ fx, y ^ fy
                sems = dict(send_sem=send_sems.at[3 * t + j], recv_sem=recv_sems.at[3 * t + j],
                            device_id=(x, y, 1 - c), device_id_type=MESH)
                mine, theirs = 4 * px + 2 * py + c, 4 * px + 2 * py + (1 - c)
                sends.append(pltpu.make_async_remote_copy(src_ref=in_refs[t].at[mine], dst_ref=g_refs[t].at[mine], **sems))
                if with_arrivals:
                    arrivals.append(pltpu.make_async_remote_copy(
                        src_ref=in_refs[t].at[mine], dst_ref=g_refs[t].at[theirs], **sems))
        return sends, arrivals

    def start(in_refs, g_refs, scr):
        for cp in copies(in_refs, g_refs, scr, False)[0]:
            cp.start()

    def wait(in_refs, g_refs, scr):
        sends, arrivals = copies(in_refs, g_refs, scr, True)
        for cp in arrivals:
            cp.wait_recv()
        for cp in sends:
            cp.wait_send()

    return _Carry(gathered, [jax.ShapeDtypeStruct(a.shape, a.dtype) for a in gathered], _sems(3 * n), start, wait,
                  aliases={t: t for t in range(n)})


def _allreduce_rows(x, *, name):
    def body(x_ref, o_ref, sib_ref, mine_ref, tab_ref, send_sems, recv_sems):
        x, y, c = lax.axis_index("x"), lax.axis_index("y"), lax.axis_index("c")
        swap = pltpu.make_async_remote_copy(src_ref=x_ref, dst_ref=sib_ref, send_sem=send_sems.at[0],
                                            recv_sem=recv_sems.at[0], device_id=(x, y, 1 - c), device_id_type=MESH)
        swap.start()
        swap.wait()
        mine_ref[...] = x_ref[...] + sib_ref[...]
        tab_ref[pl.ds(2 * x + y, 1)] = mine_ref[...][None]

        def copy(k, slot):
            fx, fy = OTHER_CHIPS[k]
            return pltpu.make_async_remote_copy(
                src_ref=mine_ref, dst_ref=tab_ref.at[slot], send_sem=send_sems.at[1 + k], recv_sem=recv_sems.at[1 + k],
                device_id=(x ^ fx, y ^ fy, c), device_id_type=MESH)

        for k in range(3):
            copy(k, 2 * x + y).start()
        for k, (fx, fy) in enumerate(OTHER_CHIPS):
            copy(k, 2 * (x ^ fx) + (y ^ fy)).wait()
        o_ref[...] = ((tab_ref[0] + tab_ref[1]) + tab_ref[2]) + tab_ref[3]

    vmem = pl.BlockSpec(memory_space=pltpu.VMEM)
    return pl.pallas_call(
        body, name=name, out_shape=jax.ShapeDtypeStruct(x.shape, F32), in_specs=[vmem], out_specs=vmem,
        scratch_shapes=[pltpu.VMEM(x.shape, F32), pltpu.VMEM(x.shape, F32), pltpu.VMEM((4,) + x.shape, F32)] + _sems(4),
    )(x)


def _allgather(shards, *, name):
    n = len(shards)

    def body(*refs):
        x_refs, out_refs = refs[:n], refs[n:2 * n]
        send_sems, recv_sems, local_sems = refs[2 * n:]
        x, y, c = lax.axis_index("x"), lax.axis_index("y"), lax.axis_index("c")
        me, sibling = (x, y, c), (x, y, 1 - c)
        chips = [(x ^ fx, y ^ fy) for fx, fy in OTHER_CHIPS]

        def copy(t, k, block, to, from_input=False):
            px, py, pc = block
            slab = out_refs[t].at[4 * px + 2 * py + pc]
            return pltpu.make_async_remote_copy(
                src_ref=x_refs[t] if from_input else slab, dst_ref=slab,
                send_sem=send_sems.at[7 * t + k], recv_sem=recv_sems.at[7 * t + k], device_id=to, device_id_type=MESH)

        mine = [pltpu.make_async_copy(x_refs[t], out_refs[t].at[4 * x + 2 * y + c], local_sems.at[t]) for t in range(n)]
        for cp in mine:
            cp.start()
        first = []
        for t in range(n):
            first.append(copy(t, 0, me, sibling, from_input=True))
            first += [copy(t, 1 + j, me, (*chip, c), from_input=True) for j, chip in enumerate(chips)]
        for cp in first:
            cp.start()
        passed = []
        for j, chip in enumerate(chips):
            for t in range(n):
                copy(t, 1 + j, (*chip, c), me).wait_recv()
                fwd = copy(t, 4 + j, (*chip, c), sibling)
                fwd.start()
                passed.append(fwd)
        for t in range(n):
            copy(t, 0, sibling, me).wait_recv()
            for j, chip in enumerate(chips):
                copy(t, 4 + j, (*chip, 1 - c), me).wait_recv()
        for cp in first + passed:
            cp.wait_send()
        for cp in mine:
            cp.wait()

    return pl.pallas_call(
        body, name=name, out_shape=[jax.ShapeDtypeStruct((N_DEV,) + a.shape, a.dtype) for a in shards],
        in_specs=[ANY] * n, out_specs=[ANY] * n,
        scratch_shapes=[pltpu.SemaphoreType.DMA((7 * n,)), pltpu.SemaphoreType.DMA((7 * n,)),
                        pltpu.SemaphoreType.DMA((n,))],
    )(*shards)


def _carry_sibling(slabs, small=None):
    n = len(slabs)
    extra = [] if small is None else [small]

    def copies(in_refs, out_refs, scr):
        send_sems, recv_sems = scr
        x, y, c = lax.axis_index("x"), lax.axis_index("y"), lax.axis_index("c")
        sibling = (x, y, 1 - c)
        out = []
        for t in range(n):
            for q in range(4):
                out.append(pltpu.make_async_remote_copy(
                    src_ref=in_refs[t].at[2 * q + (1 - c)], dst_ref=out_refs[t].at[q],
                    send_sem=send_sems.at[4 * t + q], recv_sem=recv_sems.at[4 * t + q],
                    device_id=sibling, device_id_type=MESH))
        if extra:
            out.append(pltpu.make_async_remote_copy(
                src_ref=in_refs[n], dst_ref=out_refs[n], send_sem=send_sems.at[4 * n], recv_sem=recv_sems.at[4 * n],
                device_id=sibling, device_id_type=MESH))
        return out

    def start(*refs):
        for cp in copies(*refs):
            cp.start()

    def wait(*refs):
        for cp in copies(*refs):
            cp.wait()

    return _Carry(list(slabs) + extra,
                  [jax.ShapeDtypeStruct((4,) + a.shape[1:], a.dtype) for a in slabs]
                  + [jax.ShapeDtypeStruct(a.shape, a.dtype) for a in extra], _sems(4 * n + 1), start, wait)


def _carry_chips(psums, small_sum=None):
    n = len(psums)
    table = small_sum is not None

    def copies(in_refs, out_refs, scr, arrivals):
        send_sems, recv_sems = scr[0], scr[1]
        x, y, c = lax.axis_index("x"), lax.axis_index("y"), lax.axis_index("c")
        out = []
        for k, (fx, fy) in enumerate(OTHER_CHIPS):
            px, py = x ^ fx, y ^ fy
            for t in range(n):
                out.append(pltpu.make_async_remote_copy(
                    src_ref=in_refs[t].at[2 * px + py], dst_ref=out_refs[t].at[k],
                    send_sem=send_sems.at[3 * t + k], recv_sem=recv_sems.at[3 * t + k],
                    device_id=(px, py, c), device_id_type=MESH))
            if table:
                slot = 2 * px + py if arrivals else 2 * x + y
                out.append(pltpu.make_async_remote_copy(
                    src_ref=in_refs[n], dst_ref=out_refs[n].at[slot], send_sem=send_sems.at[3 * n + k],
                    recv_sem=recv_sems.at[3 * n + k], device_id=(px, py, c), device_id_type=MESH))
        return out

    def own(in_refs, out_refs, scr):
        x, y = lax.axis_index("x"), lax.axis_index("y")
        return pltpu.make_async_copy(in_refs[n], out_refs[n].at[2 * x + y], scr[2])

    def start(in_refs, out_refs, scr):
        if table:
            own(in_refs, out_refs, scr).start()
        for cp in copies(in_refs, out_refs, scr, False):
            cp.start()

    def wait(in_refs, out_refs, scr):
        for cp in copies(in_refs, out_refs, scr, True):
            cp.wait()
        if table:
            own(in_refs, out_refs, scr).wait()

    out_shapes = [jax.ShapeDtypeStruct((3,) + a.shape[1:], a.dtype) for a in psums]
    if table:
        out_shapes.append(jax.ShapeDtypeStruct((4,) + small_sum.shape, F32))
    return _Carry(list(psums) + ([small_sum] if table else []), out_shapes,
                  _sems(3 * n + 3) + ([pltpu.SemaphoreType.DMA] if table else []), start, wait)


def _to_comm(name, kind, block, dtype=BF16):
    a = block[0]
    if kind == "cols":
        a = a.T
        if name == "w_in":
            a = jnp.pad(a, ((0, IN_SHARD_PAD - IN_SHARD), (0, 0)))
    return a if kind == "f32" else a.astype(dtype)


def _from_comm(name, kind, a):
    if kind == "cols":
        if name == "w_in":
            a = a[:IN_SHARD]
        a = a.T
    return a[None]


def _assemble_weights(g):
    out = {}
    if "w_in" in g:
        out["wt_main"] = _assemble_wt_main(g["w_in"], name="assemble_w_in")
        j, l0 = divmod(O_F, IN_SHARD)
        out["wt_f"] = jnp.pad(g["w_in"][j, l0:l0 + HEADS], ((0, 128 - HEADS), (0, 0)))
    square = dict(w_branch_a="w_a", w_branch_b="w_b", w_out="w_out", w_ple_gate="w_pg")
    for long, short in square.items():
        if long in g:
            out[short] = g[long].reshape(D_MODEL, D_MODEL)
    if "w_up" in g:
        out["wt_up"] = g["w_up"].reshape(2 * D_FF, D_MODEL)
    if "conv_w" in g:
        out["conv_w"] = g["conv_w"].transpose(1, 0, 2).reshape(3, 2 * D_FF)
    if "w_down" in g:
        out["w_down"] = g["w_down"].reshape(D_FF, D_MODEL)
    if "w_ple" in g:
        out["wt_ple"] = g["w_ple"].reshape(D_MODEL, PLE_DIM)
    return out


def _grad_slabs(gr):
    out = {}
    if "wt_main" in gr:
        gm, gf = gr["wt_main"], gr["wt_f"]
        segments = ((0, 2048, gm, 0), (2048, O_F, gm, 2048), (O_F, O_G, gf, -O_F), (O_G, IN_COLS, gm, 2048 - O_G))
        slabs = []
        for j in range(N_DEV):
            lo, hi = j * IN_SHARD, (j + 1) * IN_SHARD
            pieces = [src[max(lo, a) + shift:min(hi, b) + shift] for a, b, src, shift in segments if max(lo, a) < min(hi, b)]
            pieces.append(jnp.zeros((IN_SHARD_PAD - IN_SHARD, D_MODEL), gm.dtype))
            slabs.append(jnp.concatenate(pieces, axis=0))
        out["w_in"] = jnp.stack(slabs)
    rows = dict(w_a="w_branch_a", w_b="w_branch_b", w_out="w_out", wt_up="w_up", w_down="w_down", w_pg="w_ple_gate")
    for short, long in rows.items():
        if short in gr:
            out[long] = gr[short].reshape(N_DEV, -1, D_MODEL)
    if "conv_w" in gr:
        out["conv_w"] = gr["conv_w"].reshape(3, N_DEV, -1).transpose(1, 0, 2)
    if "wt_ple" in gr:
        out["w_ple"] = gr["wt_ple"].reshape(N_DEV, -1, PLE_DIM)
    return {k: v.astype(BF16) for k, v in out.items()}


def _rows(a, rows):
    flat = a.reshape(-1)
    return jnp.pad(flat, (0, rows * 1024 - flat.shape[0])).reshape(rows, 1024)


def _pack_small(parts):
    return jnp.concatenate([_rows(parts[n].astype(F32), r) for n, r in SMALL], axis=0)


def _small(packed, name, shape):
    off, r = SMALL_OFF[name]
    n = math.prod(shape)
    return packed[off:off + r].reshape(-1)[:n].reshape(shape)


class _Exchanges:
    W_S_ROWS = SMALL_OFF["gmlp_w_s"]

    def __init__(self, later, shards, pos):
        self.later, self.shards, self.pos = later, dict(zip(later, shards)), pos
        self.level1, self.slabs, self.from_sib, self.sums32, self.reduced, self.tables = {}, {}, {}, {}, {}, {}

    def gather1(self, names):
        carry = _carry_gather1([self.shards[n] for n in names])
        carry.names = names
        return carry

    def gather1_done(self, carry, results):
        self.level1.update(zip(carry.names, results))

    def gather2(self):
        return _carry_gather2([self.level1[n] for n in self.later])

    def weights(self, full):
        return _assemble_weights(dict(zip(self.later, full)))

    def sibling(self, grads):
        slabs = _grad_slabs(grads)
        self.slabs.update(slabs)
        carry = _carry_sibling(list(slabs.values()))
        carry.names = list(slabs)
        return carry

    def sibling_done(self, carry, results):
        self.from_sib.update(zip(carry.names, results))

    def chips(self, names, table=None):
        sums = {n: _sum_pairs(self.slabs[n], self.from_sib[n], self.pos, name="sum_sibling_" + n) for n in names}
        self.sums32.update({n: s32 for n, (s32, _) in sums.items()})
        carry = _carry_chips([s16 for _, s16 in sums.values()], None if table is None else self.table_part(table))
        carry.names, carry.table = list(names), table
        return carry

    def chips_done(self, carry, results):
        if carry.table is not None:
            *results, self.tables[carry.table] = results
        self.reduced.update({n: (self.sums32[n], r) for n, r in zip(carry.names, results)})

    def sibling_small(self, small_g):
        self.small_g = small_g
        return _carry_sibling([], small_g)

    def sibling_small_done(self, small_sib):
        self.small_chip = _pair_sum_small(self.small_g, small_sib, name="sum_sibling_small")

    def table_part(self, which):
        off, rows = self.W_S_ROWS
        if which == "w_s":
            return self.small_chip[off:off + rows]
        return jnp.concatenate([self.small_chip[:off], self.small_chip[off + rows:]], axis=0)

    def table(self):
        off = self.W_S_ROWS[0]
        rest = self.tables["rest"]
        return jnp.concatenate([rest[:, :off], self.tables["w_s"], rest[:, off:]], axis=1)


def _local_step(x, p, target, w, sm, ex=None):
    s = x.shape[0]
    mm = _matmul
    wt_main = w["wt_main"]
    conv_b = sm["conv_b"]
    bs_t = jnp.pad(sm["gmlp_b_s"].T, ((0, 0), (0, 128 - GROUPS)))
    b_f = jnp.pad(sm["b_f"], ((0, 0), (0, 128 - HEADS)))
    big = dict(tm=1024, tn=1024, tk=1024)
    whole_s = dict(tn=1024, tk=s)

    h = _rmsnorm_fwd(x, sm["norm_mix_g"], name="norm_mix")
    qkv_args = dict(mode="nt", out_dtype=BF16, name="in_qkv", n=3072, b_off=4, **big)
    f_logit = mm(h, w["wt_f"], mode="nt", out_dtype=F32, name="in_f", tm=1024, tk=1024)
    cqe = _forget_cumsum(f_logit, b_f, name="forget_cumsum")
    uvg = dict(mode="nt", out_dtype=F32, name="in_uvg", n=4096, **big)
    if ex is None:
        qkv = mm(h, wt_main, **qkv_args)
        (qa, ka, vt), _ = _attn_prep(qkv, cqe, name="attn_prep")
        (b, lse3), _ = _attn_fwd(qa, ka, vt, name="attn_fwd")
        zuvg = mm(h, wt_main, **uvg)
    else:
        groups = (["w_branch_a"], ["w_branch_b"], [n for n in ex.later if n not in ("w_branch_a", "w_branch_b")])
        carries = [ex.gather1(names) for names in groups]
        qkv, got0 = mm(h, wt_main, carry=carries[0], **qkv_args)
        (qa, ka, vt), got1 = _attn_prep(qkv, cqe, carries[1], name="attn_prep")
        (b, lse3), got2 = _attn_fwd(qa, ka, vt, carries[2], name="attn_fwd")
        for carry, got in zip(carries, (got0, got1, got2)):
            ex.gather1_done(carry, got)
        zuvg, full = mm(h, wt_main, carry=ex.gather2(), **uvg)
        w = {**w, **ex.weights(full)}
    a = _gmlp_fwd(zuvg, sm["gmlp_ln_g"], sm["gmlp_ln_b"], sm["gmlp_w_s"], bs_t, name="gmlp_fwd")
    wt_up, conv_w = w["wt_up"], w["conv_w"]
    ya, yb, merged = _branches_merge(a, b, w["w_a"], w["w_b"], zuvg, name="branches_merge")
    x1, h2 = mm(merged, w["w_out"], mode="nn", out_dtype=F32, name="out_proj", add=x, norm_g=sm["norm_ffn_g"], **big)
    up = mm(h2, wt_up, mode="nt", out_dtype=F32, name="up", tm=1024, tn=512, tk=1024)
    act = _convglu_fwd(up, conv_w, conv_b, name="convglu_fwd")
    x2, h3 = mm(act, w["w_down"], mode="nn", out_dtype=F32, name="down", tm=1024, tn=1024, tk=1408, add=x1,
                norm_g=sm["norm_ple_g"])

    loss, dx3, dple, dgp, d_norm_final = _ple_loss(p, w["wt_ple"], h3, w["w_pg"], x2, target, sm["norm_final_g"],
                                                   name="ple_loss")
    g_wt_ple = mm(dple, p, mode="tn", out_dtype=BF16, name="d_w_ple", tm=512, tn=256, tk=s)
    g_w_pg = mm(h3, dgp, mode="tn", out_dtype=BF16, name="d_w_pg", tm=256, **whole_s)
    (dx2, dx2b, d_norm_ple), _ = _matmul_rmsnorm_bwd([dgp], w["w_pg"], dx3, x2, sm["norm_ple_g"], mode="nt", tk=1024,
                                                     name="d_h3_norm_ple_bwd")
    g_w_down = mm(act, dx2b, mode="tn", out_dtype=BF16, name="d_w_down", tm=256, **whole_s)
    dact_args = dict(mode="nt", out_dtype=BF16, name="d_act", tm=1024, tn=1408, tk=1024)
    if ex is None:
        dact = mm(dx2b, w["w_down"], **dact_args)
    else:
        early = ex.sibling(dict(w_pg=g_w_pg, wt_ple=g_wt_ple))
        dact, got = mm(dx2b, w["w_down"], carry=early, **dact_args)
        ex.sibling_done(early, got)
    dup_a, dup_g, dcw_a, dcw_g, dcb_a, dcb_g = _convglu_bwd(dact, up, conv_w, conv_b, name="convglu_bwd")
    g_wt_up = mm(dup_a, h2, mode="tn", out_dtype=BF16, name="d_w_up_a", tm=256, out_rows=2 * D_FF, **whole_s)
    g_wt_up = mm(dup_g, h2, mode="tn", out_dtype=BF16, name="d_w_up_g", tm=256, out_rows=2 * D_FF,
                 o_off=D_FF // 256, into=g_wt_up, **whole_s)
    (dx1, dx1b, d_norm_ffn), _ = _matmul_rmsnorm_bwd([dup_a, dup_g], wt_up, dx2, x1, sm["norm_ffn_g"], mode="nn",
                                                     tk=1408, name="d_h2_norm_ffn_bwd")
    g_w_out = mm(merged, dx1b, mode="tn", out_dtype=BF16, name="d_w_out", tm=256, **whole_s)
    dya, dyb, dga, dgb = _merge_bwd(dx1b, w["w_out"], ya, yb, zuvg, name="merge_bwd")
    g_w_a = mm(a, dya, mode="tn", out_dtype=BF16, name="d_w_a", tm=256, **whole_s)
    g_w_b = mm(b, dyb, mode="tn", out_dtype=BF16, name="d_w_b", tm=256, **whole_s)
    da = mm(dya, w["w_a"], mode="nt", out_dtype=BF16, name="d_a", **big)
    db = mm(dyb, w["w_b"], mode="nt", out_dtype=BF16, name="d_b", **big)
    grads = dict(w_a=g_w_a, w_b=g_w_b, w_out=g_w_out, wt_up=g_wt_up, conv_w=jnp.concatenate([dcw_a, dcw_g], axis=1),
                 w_down=g_w_down, wt_ple=g_wt_ple, w_pg=g_w_pg)
    gmlp_args = (da, zuvg, sm["gmlp_ln_g"], sm["gmlp_ln_b"], sm["gmlp_w_s"], bs_t)
    if ex is None:
        (dzu, dzv, d_w_s, d_bs_t, d_ln_g, d_ln_b), _ = _gmlp_bwd(*gmlp_args, name="gmlp_bwd")
    else:
        rest = ex.sibling({k: v for k, v in grads.items() if k not in ("w_pg", "wt_ple")})
        early_chips = ex.chips(early.names)
        both = _carry_join(rest, early_chips)
        (dzu, dzv, d_w_s, d_bs_t, d_ln_g, d_ln_b), got = _gmlp_bwd(*gmlp_args, both, name="gmlp_bwd")
        got_rest, got_early = both.split(got)
        ex.sibling_done(rest, got_rest)
        ex.chips_done(early_chips, got_early)
    small = dict(norm_mix_g=jnp.zeros((1, D_MODEL), F32), b_f=jnp.zeros((1, HEADS), F32), gmlp_ln_g=d_ln_g,
                 gmlp_ln_b=d_ln_b, gmlp_w_s=d_w_s, gmlp_b_s=d_bs_t[:, :GROUPS].T, norm_ffn_g=d_norm_ffn,
                 conv_b=jnp.concatenate([dcb_a, dcb_g], axis=1), norm_ple_g=d_norm_ple, norm_final_g=d_norm_final)
    if ex is None:
        delta3, _ = _attn_delta(db, b, name="attn_delta")
        (dq, dk, dv, aux, dcq3), _ = _attn_bwd(qa, ka, qkv, db, lse3, delta3, name="attn_bwd")
    else:
        delta3, (small_sib,) = _attn_delta(db, b, ex.sibling_small(_pack_small(small)), name="attn_delta")
        ex.sibling_small_done(small_sib)
        main_chips = ex.chips(rest.names, table="rest")
        (dq, dk, dv, aux, dcq3), got = _attn_bwd(qa, ka, qkv, db, lse3, delta3, main_chips, name="attn_bwd")
        ex.chips_done(main_chips, got)
    dcq16 = jnp.pad(dcq3[:, :2, :].reshape(HEADS, s).T, ((0, 0), (0, 128 - HEADS)))
    dzf, d_b_f = _forget_bwd(dcq16, aux, f_logit, b_f, name="forget_bwd")
    dz_parts = [dzu, dzv, dga, dgb, dq, dk, dv]
    w_s_chips = None if ex is None else ex.chips([], table="w_s")
    g_wt_main, got = _grad_w_parts(dz_parts, h, name="d_w_main", tm=512, carry=w_s_chips)
    if ex is not None:
        ex.chips_done(w_s_chips, got)
    g_wt_f = mm(dzf, h, mode="tn", out_dtype=BF16, name="d_w_f", **whole_s)
    grads = dict(grads, wt_main=g_wt_main, wt_f=g_wt_f)
    w_in_chips = None
    if ex is not None:
        w_in_sib = ex.sibling(dict(wt_main=g_wt_main, wt_f=g_wt_f))
        ex.sibling_done(w_in_sib, _run_carry(w_in_sib, name="exchange_sibling_w_in"))
        w_in_chips = ex.chips(w_in_sib.names)
    (dx0, _, d_norm_mix), got = _matmul_rmsnorm_bwd(dz_parts, wt_main, dx1, x, sm["norm_mix_g"], mode="nn", tk=1024,
                                                    extra=(dzf, w["wt_f"]), name="d_h_norm_mix_bwd", carry=w_in_chips)
    if ex is not None:
        ex.chips_done(w_in_chips, got)
    return loss, dx0, grads, dict(small, norm_mix_g=d_norm_mix, b_f=d_b_f[:, :HEADS])


def kernel(x, p, norm_mix_g, w_in, b_f, gmlp_ln_g, gmlp_ln_b, gmlp_w_s, gmlp_b_s, w_branch_a, w_branch_b, w_out, norm_ffn_g, w_up, conv_w, conv_b, w_down, norm_ple_g, w_ple, w_ple_gate, norm_final_g, loss_target, m_norm_mix_g, m_w_in, m_b_f, m_gmlp_ln_g, m_gmlp_ln_b, m_gmlp_w_s, m_gmlp_b_s, m_w_branch_a, m_w_branch_b, m_w_out, m_norm_ffn_g, m_w_up, m_conv_w, m_conv_b, m_w_down, m_norm_ple_g, m_w_ple, m_w_ple_gate, m_norm_final_g, v_norm_mix_g, v_w_in, v_b_f, v_gmlp_ln_g, v_gmlp_ln_b, v_gmlp_w_s, v_gmlp_b_s, v_w_branch_a, v_w_branch_b, v_w_out, v_norm_ffn_g, v_w_up, v_conv_w, v_conv_b, v_w_down, v_norm_ple_g, v_w_ple, v_w_ple_gate, v_norm_final_g):
    given = dict(locals())
    weights = {n: given[n] for n in WEIGHT_ORDER}
    mom_m = {n: given["m_" + n] for n in WEIGHT_ORDER}
    mom_v = {n: given["v_" + n] for n in WEIGHT_ORDER}
    pos = jnp.stack([lax.axis_index("x"), lax.axis_index("y"), lax.axis_index("c")]).astype(I32)
    names = [n for n, _ in SHARDED]
    kinds = dict(SHARDED)

    later = [n for n in names if n != "w_in"]

    first = _allgather([_to_comm("w_in", kinds["w_in"], weights["w_in"])], name="allgather_w_in")
    ex = _Exchanges(later, [_to_comm(n, kinds[n], weights[n]) for n in later], pos)

    sm = dict(norm_mix_g=norm_mix_g, b_f=b_f, gmlp_ln_g=gmlp_ln_g, gmlp_ln_b=gmlp_ln_b, gmlp_w_s=gmlp_w_s[0],
              gmlp_b_s=gmlp_b_s[0], norm_ffn_g=norm_ffn_g, conv_b=conv_b, norm_ple_g=norm_ple_g,
              norm_final_g=norm_final_g.reshape(1, D_MODEL))
    loss_part, dx0, grads, small = _local_step(
        x[0], p[0, 0], loss_target[0], _assemble_weights({"w_in": first[0]}), sm, ex)

    b_f_and_loss = jnp.concatenate([small["b_f"].reshape(-1), loss_part[0, :1]])
    last = _allreduce_rows(jnp.concatenate([_rows(small["norm_mix_g"], 8), _rows(b_f_and_loss, 8)], axis=0),
                           name="allreduce_last")
    loss = last[8, HEADS]
    small_last = jnp.pad(last, ((0, SMALL_ROWS - 16), (0, 0)))

    grad, delta, new_m, new_v = {}, {}, {}, {}
    for n in names:
        s32, r = ex.reduced[n]
        outs = _adam_sharded(s32, r, *[_to_comm(n, kinds[n], src[n], F32) for src in (weights, mom_m, mom_v)], pos,
                             name="adam_" + n)
        grad[n], delta[n], new_m[n], new_v[n] = [_from_comm(n, kinds[n], o) for o in outs]
    replicated = [n for n, _ in SMALL]
    rep = lambda src: _pack_small({n: src[n] for n in replicated})
    packed = _adam_replicated(ex.table(), small_last, rep(weights), rep(mom_m), rep(mom_v), name="adam_replicated")
    for out, pk in zip((grad, delta, new_m, new_v), packed):
        for n in replicated:
            out[n] = _small(pk, n, weights[n].shape)

    return (loss, dx0[None], *[grad[n] for n in WEIGHT_ORDER], *[delta[n] for n in WEIGHT_ORDER],
            *[new_m[n] for n in WEIGHT_ORDER], *[new_v[n] for n in WEIGHT_ORDER])
```

```python
import functools
import math

import jax
import jax.numpy as jnp
from jax import lax
from jax.experimental import pallas as pl
from jax.experimental.pallas import tpu as pltpu

F32 = jnp.float32
BF16 = jnp.bfloat16
I32 = jnp.int32

D_MODEL = 1024
GROUPS = 8
GDIM = 128
GBLOCK = 128
CHUNK = 64
HEADS = 16
HEAD_DIM = 64
D_FF = 2816
PLE_DIM = 256
EPS = 1e-6
N_DEV = 8
ATT_SCALE = HEAD_DIM ** -0.5
NEG = -1e30

ADAM_LR = 0.001
ADAM_B1 = 0.9
ADAM_B2 = 0.999
ADAM_EPS = 1e-08
ADAM_WD = 0.01
ADAM_STEP = 10

V7X_VMEM_LIMIT = 48 * 1024 * 1024
MESH = pl.DeviceIdType.MESH

O_F = 2 * 1024 + 3 * 1024
O_G = O_F + HEADS
IN_COLS = O_G + 2 * D_MODEL
MAIN_COLS = IN_COLS - HEADS
IN_SHARD = IN_COLS // N_DEV
IN_SHARD_PAD = 912

SHARDED = (("w_in", "cols"), ("w_branch_a", "rows"), ("w_branch_b", "rows"), ("w_out", "rows"), ("w_up", "cols"),
           ("conv_w", "f32"), ("w_down", "rows"), ("w_ple", "cols"), ("w_ple_gate", "rows"))

SMALL = (("norm_mix_g", 8), ("b_f", 8), ("gmlp_ln_g", 8), ("gmlp_ln_b", 8), ("gmlp_w_s", 128), ("gmlp_b_s", 8),
         ("norm_ffn_g", 8), ("conv_b", 8), ("norm_ple_g", 8), ("norm_final_g", 8))
SMALL_OFF = {}
_o = 0
for _n, _r in SMALL:
    SMALL_OFF[_n] = (_o, _r)
    _o += _r
SMALL_ROWS = _o

WEIGHT_ORDER = ("norm_mix_g", "w_in", "b_f", "gmlp_ln_g", "gmlp_ln_b", "gmlp_w_s", "gmlp_b_s", "w_branch_a",
                "w_branch_b", "w_out", "norm_ffn_g", "w_up", "conv_w", "conv_b", "w_down", "norm_ple_g", "w_ple",
                "w_ple_gate", "norm_final_g")


def _cparams(sem):
    return pltpu.CompilerParams(dimension_semantics=sem, vmem_limit_bytes=V7X_VMEM_LIMIT)


def _gelu(x):
    c = math.sqrt(2.0 / math.pi)
    return 0.5 * x * (1.0 + jnp.tanh(c * (x + 0.044715 * x * x * x)))


def _gelu_and_grad(x):
    c = math.sqrt(2.0 / math.pi)
    t = jnp.tanh(c * (x + 0.044715 * x * x * x))
    g = 0.5 * x * (1.0 + t)
    dg = 0.5 * (1.0 + t) + 0.5 * x * (1.0 - t * t) * (c * (1.0 + 3.0 * 0.044715 * x * x))
    return g, dg


def _sigmoid(x):
    return 1.0 / (1.0 + jnp.exp(-x))


def _dot(a, b, dims):
    return lax.dot_general(a, b, (dims, ((), ())), preferred_element_type=F32)


NN = ((1,), (0,))
NT = ((1,), (1,))
TN = ((0,), (0,))


def _row_tile(rows, most):
    best = None
    for t in range(16, min(rows, most) + 1, 16):
        if rows % t == 0:
            best = t
    return best if best is not None else rows


def _matmul(a, b, *, mode, out_dtype, name, tm=512, tn=512, tk=512, add=None, n=None, b_off=0,
            out_rows=None, o_off=0, into=None, norm_g=None, carry=None):
    if mode == "tn":
        kdim, m = a.shape
    else:
        m, kdim = a.shape
    if n is None:
        n = b.shape[0] if mode == "nt" else b.shape[1]
    tm, tn, tk = min(tm, m), min(tn, n), min(tk, kdim)
    assert m % tm == 0 and n % tn == 0 and kdim % tk == 0, (name, m, n, kdim, tm, tn, tk)
    nk = kdim // tk
    dims = {"nn": NN, "nt": NT, "tn": TN}[mode]

    n_in = 2 + (add is not None) + (into is not None) + (norm_g is not None)
    assert norm_g is None or tn == n, "the RMS norm needs whole rows"

    def finish(r, refs):
        if add is not None:
            r = refs[2][...].astype(F32) + r
        refs[n_in][...] = r.astype(out_dtype)
        if norm_g is not None:
            rs = lax.rsqrt(jnp.mean(r * r, axis=-1, keepdims=True) + EPS)
            refs[n_in + 1][...] = ((r * rs) * refs[n_in - 1][...]).astype(BF16)

    def body(*refs):
        a_ref, b_ref = refs[:2]
        part = _dot(a_ref[...].astype(BF16), b_ref[...].astype(BF16), dims)
        if nk == 1:
            finish(part, refs)
            return
        acc_ref = refs[-1]
        k = pl.program_id(2)

        @pl.when(k == 0)
        def _():
            acc_ref[...] = part

        @pl.when((k > 0) & (k < nk - 1))
        def _():
            acc_ref[...] += part

        @pl.when(k == nk - 1)
        def _():
            finish(acc_ref[...] + part, refs)

    a_spec = pl.BlockSpec((tk, tm), lambda i, j, k: (k, i)) if mode == "tn" else pl.BlockSpec((tm, tk), lambda i, j, k: (i, k))
    if mode == "nt":
        b_spec = pl.BlockSpec((tn, tk), lambda i, j, k: (j + b_off, k))
    else:
        b_spec = pl.BlockSpec((tk, tn), lambda i, j, k: (k + b_off, j))
    o_spec = pl.BlockSpec((tm, tn), lambda i, j, k: (i + o_off, j))
    in_specs = [a_spec, b_spec] + ([pl.BlockSpec((tm, tn), lambda i, j, k: (i, j))] if add is not None else [])
    args = (a, b) + ((add,) if add is not None else ())
    aliases = {}
    if into is not None:
        aliases = {len(args): 0}
        in_specs.append(pl.BlockSpec(memory_space=pl.ANY))
        args += (into,)
    out_specs = [o_spec]
    out_shape = [jax.ShapeDtypeStruct((m if out_rows is None else out_rows, n), out_dtype)]
    if norm_g is not None:
        in_specs.append(pl.BlockSpec((1, n), lambda i, j, k: (0, 0)))
        args += (norm_g,)
        out_specs.append(pl.BlockSpec((tm, tn), lambda i, j, k: (i, j)))
        out_shape.append(jax.ShapeDtypeStruct((m, n), BF16))
    outs, carried = _carry_call(
        body, carry, name=name, grid=(m // tm, n // tn, nk), in_specs=in_specs, out_specs=out_specs,
        out_shape=out_shape, scratch_shapes=[pltpu.VMEM((tm, tn), F32)] if nk > 1 else [], args=args,
        own_aliases=aliases)
    out = outs[0] if norm_g is None else tuple(outs)
    return out if carry is None else (out, carried)


def _row_spec(tr, width, col_block=0):
    return pl.BlockSpec((tr, width), lambda i: (i, col_block))


def _full_spec(shape):
    return pl.BlockSpec(shape, lambda i: tuple(0 for _ in shape))


def _rmsnorm_fwd(x, g, *, name, tr=256):
    s, d = x.shape

    def body(x_ref, g_ref, o_ref):
        xv = x_ref[...]
        r = lax.rsqrt(jnp.mean(xv * xv, axis=-1, keepdims=True) + EPS)
        o_ref[...] = ((xv * r) * g_ref[...]).astype(BF16)

    return pl.pallas_call(
        body, name=name, grid=(s // tr,),
        in_specs=[_row_spec(tr, d), _full_spec((1, d))], out_specs=_row_spec(tr, d),
        out_shape=jax.ShapeDtypeStruct((s, d), BF16), compiler_params=_cparams(("parallel",)),
    )(x, g)


def _matmul_rmsnorm_bwd(a_parts, b, dres, x, g, *, mode, tk, name, extra=None, tm=512, carry=None, lead=False,
                        resident=False):
    s, d = x.shape
    n_row = s // tm
    spans, lo = [], 0
    for a in a_parts:
        spans.append((lo, lo + a.shape[1] // tk))
        lo = spans[-1][1]
    n_main, total = lo, lo + (extra is not None)
    n_parts = len(a_parts)

    def body(*refs):
        a_refs, b_ref = refs[:n_parts], refs[n_parts]
        k0 = n_parts + 1
        ax_ref, bx_ref = (refs[k0], refs[k0 + 1]) if extra is not None else (None, None)
        k0 += 2 * (extra is not None)
        dres_ref, x_ref, g_ref, dx_ref, dxb_ref, dg_ref, acc_all = refs[k0:k0 + 7]
        if resident:
            kk, i = pl.program_id(0), pl.program_id(1)
            acc_ref = acc_all.at[pl.ds(pl.multiple_of(i * tm, tm), tm)]
        else:
            i, kk = pl.program_id(0), pl.program_id(1)
            acc_ref = acc_all

        def accumulate(part, first):
            if first:
                @pl.when(kk == 0)
                def _():
                    acc_ref[...] = part

                @pl.when(kk > 0)
                def _():
                    acc_ref[...] += part
            else:
                acc_ref[...] += part

        for p, (a_ref, (lo_p, hi_p)) in enumerate(zip(a_refs, spans)):
            @pl.when((kk >= lo_p) & (kk < hi_p))
            def _(a_ref=a_ref, lo_p=lo_p):
                accumulate(_dot(a_ref[...].astype(BF16), b_ref[...].astype(BF16), NN if mode == "nn" else NT), lo_p == 0)

        if extra is not None:
            @pl.when(kk == n_main)
            def _():
                accumulate(_dot(ax_ref[...].astype(BF16), bx_ref[...].astype(BF16), NN), False)

        @pl.when(kk == total - 1)
        def _():
            dhv = acc_ref[...]
            xv = x_ref[...]
            r = lax.rsqrt(jnp.mean(xv * xv, axis=-1, keepdims=True) + EPS)
            xhat = xv * r
            dxhat = dhv * g_ref[...]
            dx = dres_ref[...] + r * (dxhat - xhat * jnp.mean(dxhat * xhat, axis=-1, keepdims=True))
            dx_ref[...] = dx
            dxb_ref[...] = dx.astype(BF16)
            dgp = jnp.sum(dhv * xhat, axis=0, keepdims=True)

            @pl.when(i == 0)
            def _():
                dg_ref[...] = dgp

            @pl.when(i > 0)
            def _():
                dg_ref[...] += dgp

    def spec(shape, index):
        return pl.BlockSpec(shape, (lambda kk, i: index(i, kk)) if resident else index)

    def row(i, kk, lo_p, hi_p):
        if not resident:
            return i
        return jnp.where(kk < lo_p, 0, jnp.where(kk >= hi_p, n_row - 1, i))

    a_specs = [spec((tm, tk), lambda i, kk, lo_p=lo_p, hi_p=hi_p: (row(i, kk, lo_p, hi_p),
                                                                    jnp.clip(kk - lo_p, 0, hi_p - lo_p - 1)))
               for lo_p, hi_p in spans]
    step = lambda kk: jnp.minimum(kk, n_main - 1)
    b_spec = (spec((tk, d), lambda i, kk: (step(kk), 0)) if mode == "nn"
              else spec((d, tk), lambda i, kk: (0, step(kk))))
    rows = spec((tm, d), lambda i, kk: (row(i, kk, total - 1, total), 0))
    one = spec((1, d), lambda i, kk: (0, 0))
    dx_spec, dx_shape = rows, jax.ShapeDtypeStruct((s, d), F32)
    if lead:
        dx_spec = spec((None, tm, d), lambda i, kk: (0, row(i, kk, total - 1, total), 0))
        dx_shape = jax.ShapeDtypeStruct((1, s, d), F32)
    x_specs, x_args = [], []
    if extra is not None:
        kx = extra[0].shape[1]
        x_specs = [spec((tm, kx), lambda i, kk: (row(i, kk, n_main, total), 0)), spec((kx, d), lambda i, kk: (0, 0))]
        x_args = list(extra)
    (dx, dxb, dg), carried = _carry_call(
        body, carry, name=name, grid=(total, n_row) if resident else (n_row, total),
        in_specs=a_specs + [b_spec] + x_specs + [rows, rows, one], out_specs=[dx_spec, rows, one],
        out_shape=[dx_shape, jax.ShapeDtypeStruct((s, d), BF16), jax.ShapeDtypeStruct((1, d), F32)],
        scratch_shapes=[pltpu.VMEM((s if resident else tm, d), F32)], args=list(a_parts) + [b] + x_args + [dres, x, g])
    return (dx, dxb, dg), carried


def _grad_w_parts(a_parts, b, *, name, tm=512, carry=None):
    s, width = a_parts[0].shape
    per, n = width // tm, b.shape[1]

    def body(*refs):
        a_refs, b_ref, o_ref = refs[:len(a_parts)], refs[len(a_parts)], refs[len(a_parts) + 1]
        i = pl.program_id(0)
        for p, a_ref in enumerate(a_refs):
            @pl.when(i // per == p)
            def _(a_ref=a_ref):
                o_ref[...] = _dot(a_ref[...].astype(BF16), b_ref[...].astype(BF16), TN).astype(BF16)

    a_specs = [pl.BlockSpec((s, tm), lambda i, p=p: (0, jnp.clip(i - p * per, 0, per - 1))) for p in range(len(a_parts))]
    (out,), carried = _carry_call(
        body, carry, name=name, grid=(len(a_parts) * per,),
        in_specs=a_specs + [pl.BlockSpec((s, n), lambda i: (0, 0))], out_specs=[pl.BlockSpec((tm, n), lambda i: (i, 0))],
        out_shape=[jax.ShapeDtypeStruct((len(a_parts) * width, n), BF16)], scratch_shapes=[], args=list(a_parts) + [b])
    return out, carried


def _ple_loss(p, wt_ple, h3, w_pg, x2, target, g, *, name, tm=256):
    s, d = x2.shape
    kp = p.shape[1]

    def body(p_ref, wp_ref, h_ref, wg_ref, x_ref, t_ref, g_ref, loss_ref, dx_ref, dple_ref, dgp_ref, dg_ref):
        i = pl.program_id(0)
        ple = _dot(p_ref[...].astype(BF16), wp_ref[...], NT)
        sg = _sigmoid(_dot(h_ref[...], wg_ref[...], NN))
        xv = x_ref[...] + ple * sg
        r = lax.rsqrt(jnp.mean(xv * xv, axis=-1, keepdims=True) + EPS)
        xhat = xv * r
        diff = xhat * g_ref[...] - t_ref[...]
        lp = jnp.zeros((1, 128), F32) + (0.5 / d) * jnp.sum(diff * diff)
        dy = diff * (1.0 / d)
        dxhat = dy * g_ref[...]
        dx = r * (dxhat - xhat * jnp.mean(dxhat * xhat, axis=-1, keepdims=True))
        dx_ref[...] = dx
        dple_ref[...] = (dx * sg).astype(BF16)
        dgp_ref[...] = (dx * ple * (sg * (1.0 - sg))).astype(BF16)
        dgp = jnp.sum(dy * xhat, axis=0, keepdims=True)

        @pl.when(i == 0)
        def _():
            dg_ref[...] = dgp
            loss_ref[...] = lp

        @pl.when(i > 0)
        def _():
            dg_ref[...] += dgp
            loss_ref[...] += lp

    rows = _row_spec(tm, d)
    return pl.pallas_call(
        body, name=name, grid=(s // tm,),
        in_specs=[_row_spec(tm, kp), _full_spec((d, kp)), rows, _full_spec((d, d)), rows, rows, _full_spec((1, d))],
        out_specs=[_full_spec((1, 128)), rows, rows, rows, _full_spec((1, d))],
        out_shape=[jax.ShapeDtypeStruct((1, 128), F32), jax.ShapeDtypeStruct((s, d), F32),
                   jax.ShapeDtypeStruct((s, d), BF16), jax.ShapeDtypeStruct((s, d), BF16),
                   jax.ShapeDtypeStruct((1, d), F32)],
        compiler_params=_cparams(("arbitrary",)),
    )(p, wt_ple, h3, w_pg, x2, target, g)


def _branches_merge(a, b, w_a, w_b, zuvg, *, name, tm=512):
    s, d = a.shape

    def body(a_ref, b_ref, wa_ref, wb_ref, ga_ref, gb_ref, ya_ref, yb_ref, o_ref):
        ya = _dot(a_ref[...], wa_ref[...], NN)
        yb = _dot(b_ref[...], wb_ref[...], NN)
        ya_ref[...] = ya
        yb_ref[...] = yb
        o_ref[...] = (_sigmoid(ga_ref[...]) * ya + _sigmoid(gb_ref[...]) * yb).astype(BF16)

    rows = _row_spec(tm, d)
    return pl.pallas_call(
        body, name=name, grid=(s // tm,),
        in_specs=[rows, rows, _full_spec((d, d)), _full_spec((d, d)), _row_spec(tm, d, 2), _row_spec(tm, d, 3)],
        out_specs=[rows, rows, rows],
        out_shape=[jax.ShapeDtypeStruct((s, d), F32), jax.ShapeDtypeStruct((s, d), F32), jax.ShapeDtypeStruct((s, d), BF16)],
        compiler_params=_cparams(("parallel",)),
    )(a, b, w_a, w_b, zuvg, zuvg)


def _merge_bwd(dx1b, w_out, ya, yb, zuvg, *, name, tm=512):
    s, d = ya.shape

    def body(dx_ref, w_ref, ya_ref, yb_ref, ga_ref, gb_ref, dya_ref, dyb_ref, dga_ref, dgb_ref):
        dmv = _dot(dx_ref[...], w_ref[...], NT)
        sa = _sigmoid(ga_ref[...])
        sb = _sigmoid(gb_ref[...])
        dya_ref[...] = (dmv * sa).astype(BF16)
        dyb_ref[...] = (dmv * sb).astype(BF16)
        dga_ref[...] = (dmv * ya_ref[...] * (sa * (1.0 - sa))).astype(BF16)
        dgb_ref[...] = (dmv * yb_ref[...] * (sb * (1.0 - sb))).astype(BF16)

    rows = _row_spec(tm, d)
    o = jax.ShapeDtypeStruct((s, d), BF16)
    return pl.pallas_call(
        body, name=name, grid=(s // tm,),
        in_specs=[rows, _full_spec((d, d)), rows, rows, _row_spec(tm, d, 2), _row_spec(tm, d, 3)],
        out_specs=[rows] * 4, out_shape=[o, o, o, o], compiler_params=_cparams(("parallel",)),
    )(dx1b, w_out, ya, yb, zuvg, zuvg)


def _masked_ws(ws_ref, g):
    row = lax.broadcasted_iota(I32, (GBLOCK, GBLOCK), 0)
    col = lax.broadcasted_iota(I32, (GBLOCK, GBLOCK), 1)
    keep = (col // CHUNK) <= (row // CHUNK)
    return jnp.where(keep, ws_ref[g], 0.0), keep


def _layernorm_parts(zv):
    mu = jnp.mean(zv, axis=-1, keepdims=True)
    xc = zv - mu
    rs = lax.rsqrt(jnp.mean(xc * xc, axis=-1, keepdims=True) + EPS)
    return xc * rs, rs


def _gmlp_fwd(zuvg, ln_g, ln_b, w_s, bs_t, *, name):
    s, w = zuvg.shape[0], GROUPS * GDIM

    def body(zu_ref, zv_ref, lng_ref, lnb_ref, ws_ref, bs_ref, a_ref):
        zu = _gelu(zu_ref[...])
        zv = _gelu(zv_ref[...])
        xhat, _ = _layernorm_parts(zv)
        vln = (xhat * lng_ref[...] + lnb_ref[...]).astype(BF16)
        for g in range(GROUPS):
            wm, _ = _masked_ws(ws_ref, g)
            mixed = _dot(wm.astype(BF16), vln[:, g * GDIM:(g + 1) * GDIM], NN) + bs_ref[:, g:g + 1]
            a_ref[:, g * GDIM:(g + 1) * GDIM] = (zu[:, g * GDIM:(g + 1) * GDIM] * mixed).astype(BF16)

    return pl.pallas_call(
        body, name=name, grid=(s // GBLOCK,),
        in_specs=[_row_spec(GBLOCK, w, 0), _row_spec(GBLOCK, w, 1), _full_spec((1, w)), _full_spec((1, w)),
                  _full_spec((GROUPS, GBLOCK, GBLOCK)), _full_spec((GBLOCK, 128))],
        out_specs=_row_spec(GBLOCK, w),
        out_shape=jax.ShapeDtypeStruct((s, w), BF16), compiler_params=_cparams(("parallel",)),
    )(zuvg, zuvg, ln_g, ln_b, w_s, bs_t)


def _gmlp_bwd(da, zuvg, ln_g, ln_b, w_s, bs_t, carry=None, *, name):
    s, w = zuvg.shape[0], GROUPS * GDIM

    def body(da_ref, zu_ref, zv_ref, lng_ref, lnb_ref, ws_ref, bs_ref,
             dzu_ref, dzv_ref, dws_ref, dbs_ref, dlng_ref, dlnb_ref, dvln_ref):
        i = pl.program_id(0)
        zu, dzu_g = _gelu_and_grad(zu_ref[...])
        zv, dzv_g = _gelu_and_grad(zv_ref[...])
        xhat, rs = _layernorm_parts(zv)
        vln = (xhat * lng_ref[...] + lnb_ref[...]).astype(BF16)
        dav = da_ref[...].astype(F32)
        lane = lax.broadcasted_iota(I32, (GBLOCK, 128), 1)
        dbs = jnp.zeros((GBLOCK, 128), F32)

        @pl.when(i == 0)
        def _():
            dws_ref[...] = jnp.zeros_like(dws_ref)

        for g in range(GROUPS):
            sl = slice(g * GDIM, (g + 1) * GDIM)
            wm, keep = _masked_ws(ws_ref, g)
            wmb = wm.astype(BF16)
            vg = vln[:, sl]
            mixed = _dot(wmb, vg, NN) + bs_ref[:, g:g + 1]
            dag = dav[:, sl]
            dzu_ref[:, sl] = (dag * mixed * dzu_g[:, sl]).astype(BF16)
            dmix = dag * zu[:, sl]
            dmb = dmix.astype(BF16)
            dws_ref[g] += jnp.where(keep, _dot(dmb, vg, NT), 0.0)
            dbs = jnp.where(lane == g, jnp.sum(dmix, axis=1, keepdims=True), dbs)
            dvln_ref[:, sl] = _dot(wmb, dmb, TN)
        dvln = dvln_ref[...]
        dxhat = dvln * lng_ref[...]
        dzv = rs * (dxhat - jnp.mean(dxhat, axis=-1, keepdims=True)
                    - xhat * jnp.mean(dxhat * xhat, axis=-1, keepdims=True))
        dzv_ref[...] = (dzv * dzv_g).astype(BF16)
        dlng = jnp.sum(dvln * xhat, axis=0, keepdims=True)
        dlnb = jnp.sum(dvln, axis=0, keepdims=True)

        @pl.when(i == 0)
        def _():
            dbs_ref[...] = dbs
            dlng_ref[...] = dlng
            dlnb_ref[...] = dlnb

        @pl.when(i > 0)
        def _():
            dbs_ref[...] += dbs
            dlng_ref[...] += dlng
            dlnb_ref[...] += dlnb

    return _carry_call(
        body, carry, name=name, grid=(s // GBLOCK,),
        in_specs=[_row_spec(GBLOCK, w), _row_spec(GBLOCK, w, 0), _row_spec(GBLOCK, w, 1), _full_spec((1, w)),
                  _full_spec((1, w)), _full_spec((GROUPS, GBLOCK, GBLOCK)), _full_spec((GBLOCK, 128))],
        out_specs=[_row_spec(GBLOCK, w), _row_spec(GBLOCK, w), _full_spec((GROUPS, GBLOCK, GBLOCK)),
                   _full_spec((GBLOCK, 128)), _full_spec((1, w)), _full_spec((1, w))],
        out_shape=[jax.ShapeDtypeStruct((s, w), BF16), jax.ShapeDtypeStruct((s, w), BF16),
                   jax.ShapeDtypeStruct((GROUPS, GBLOCK, GBLOCK), F32), jax.ShapeDtypeStruct((GBLOCK, 128), F32),
                   jax.ShapeDtypeStruct((1, w), F32), jax.ShapeDtypeStruct((1, w), F32)],
        scratch_shapes=[pltpu.VMEM((GBLOCK, w), F32)], args=[da, zuvg, zuvg, ln_g, ln_b, w_s, bs_t])


def _shift_down(u, k):
    row = lax.broadcasted_iota(I32, u.shape, 0)
    return jnp.where(row >= k, pltpu.roll(u, k, 0), 0.0)


def _shift_up(u, k):
    s = u.shape[0]
    row = lax.broadcasted_iota(I32, u.shape, 0)
    return jnp.where(row < s - k, pltpu.roll(u, s - k, 0), 0.0)


def _conv(u, w_ref, b_ref):
    return b_ref[...] + w_ref[0:1, :] * _shift_down(u, 2) + w_ref[1:2, :] * _shift_down(u, 1) + w_ref[2:3, :] * u


def _conv_specs(s, f, tc):
    nc = f // tc
    half = lambda rows: [pl.BlockSpec((rows, tc), lambda j: (0, j)), pl.BlockSpec((rows, tc), lambda j: (0, nc + j))]
    return half(s), half(3), half(1)


def _up_convglu(h2, wt_up, conv_w, conv_b, *, name, tc=256):
    s, d = h2.shape
    f = wt_up.shape[0] // 2
    nc = f // tc
    _, w_specs, b_specs = _conv_specs(s, f, tc)

    def body(h_ref, ta_ref, tg_ref, wa_ref, wg_ref, ba_ref, bg_ref, ua_ref, ug_ref, o_ref):
        ua = _dot(h_ref[...], ta_ref[...], NT)
        ug = _dot(h_ref[...], tg_ref[...], NT)
        ua_ref[...] = ua
        ug_ref[...] = ug
        o_ref[...] = (_gelu(_conv(ua, wa_ref, ba_ref)) * _conv(ug, wg_ref, bg_ref)).astype(BF16)

    col = pl.BlockSpec((s, tc), lambda j: (0, j))
    return pl.pallas_call(
        body, name=name, grid=(nc,),
        in_specs=[_full_spec((s, d)), pl.BlockSpec((tc, d), lambda j: (j, 0)), pl.BlockSpec((tc, d), lambda j: (nc + j, 0))]
        + w_specs + b_specs,
        out_specs=[col, col, col],
        out_shape=[jax.ShapeDtypeStruct((s, f), F32), jax.ShapeDtypeStruct((s, f), F32), jax.ShapeDtypeStruct((s, f), BF16)],
        compiler_params=_cparams(("parallel",)),
    )(h2, wt_up, wt_up, conv_w, conv_w, conv_b, conv_b)


def _convglu_bwd(dact, up_a, up_g, conv_w, conv_b, *, name, tc=256):
    s, f = up_a.shape
    _, w_specs, b_specs = _conv_specs(s, f, tc)
    up_specs = [pl.BlockSpec((s, tc), lambda j: (0, j))] * 2

    def half(dc, taps, w_ref, du_ref, dw_ref, db_ref):
        db_ref[...] = jnp.sum(dc, axis=0, keepdims=True)
        for k in range(3):
            dw_ref[k:k + 1, :] = jnp.sum(dc * taps[k], axis=0, keepdims=True)
        du = w_ref[2:3, :] * dc + w_ref[1:2, :] * _shift_up(dc, 1) + w_ref[0:1, :] * _shift_up(dc, 2)
        du_ref[...] = du.astype(BF16)

    def body(d_ref, ua_ref, ug_ref, wa_ref, wg_ref, ba_ref, bg_ref,
             dua_ref, dug_ref, dwa_ref, dwg_ref, dba_ref, dbg_ref):
        taps_a = (_shift_down(ua_ref[...], 2), _shift_down(ua_ref[...], 1), ua_ref[...])
        taps_g = (_shift_down(ug_ref[...], 2), _shift_down(ug_ref[...], 1), ug_ref[...])
        conv = lambda taps, w_ref, b_ref: b_ref[...] + w_ref[0:1, :] * taps[0] + w_ref[1:2, :] * taps[1] + w_ref[2:3, :] * taps[2]
        ca = conv(taps_a, wa_ref, ba_ref)
        cg = conv(taps_g, wg_ref, bg_ref)
        ga, dga = _gelu_and_grad(ca)
        dv = d_ref[...].astype(F32)
        half(dv * cg * dga, taps_a, wa_ref, dua_ref, dwa_ref, dba_ref)
        half(dv * ga, taps_g, wg_ref, dug_ref, dwg_ref, dbg_ref)

    col, w3, b1 = up_specs[0], w_specs[0], b_specs[0]
    return pl.pallas_call(
        body, name=name, grid=(f // tc,),
        in_specs=[col] + up_specs + w_specs + b_specs, out_specs=[col, col, w3, w3, b1, b1],
        out_shape=[jax.ShapeDtypeStruct((s, f), BF16), jax.ShapeDtypeStruct((s, f), BF16),
                   jax.ShapeDtypeStruct((3, f), F32), jax.ShapeDtypeStruct((3, f), F32),
                   jax.ShapeDtypeStruct((1, f), F32), jax.ShapeDtypeStruct((1, f), F32)],
        compiler_params=_cparams(("parallel",)),
    )(dact, up_a, up_g, conv_w, conv_w, conv_b, conv_b)


def _tri_dot(tri, x):
    b0 = x.astype(BF16)
    r1 = x - b0.astype(F32)
    b1 = r1.astype(BF16)
    b2 = (r1 - b1.astype(F32)).astype(BF16)
    return _dot(tri, b0, NN) + _dot(tri, b1, NN) + _dot(tri, b2, NN)


def _log_sigmoid(x):
    return jnp.minimum(x, 0.0) - jnp.log(1.0 + jnp.exp(-jnp.abs(x)))


def _expand_heads(col16, rows):
    src = lax.broadcasted_iota(I32, (128, HEADS * HEAD_DIM), 0)
    dst = lax.broadcasted_iota(I32, (128, HEADS * HEAD_DIM), 1) // HEAD_DIM
    spread = (src == dst).astype(BF16)
    p0, p1, p2 = _bf16_pieces(col16)
    return (_dot(p0.astype(BF16), spread, NN) + _dot(p1.astype(BF16), spread, NN)) + _dot(p2.astype(BF16), spread, NN)


def _forget_cumsum(f_logit, b_f, *, name):
    s = f_logit.shape[0]
    nb = s // 128

    def body(f_ref, b_ref, cqe_ref):
        row = lax.broadcasted_iota(I32, (128, 128), 0)
        col = lax.broadcasted_iota(I32, (128, 128), 1)
        tri = (col <= row).astype(BF16)

        def step(n, carry):
            r0 = pl.multiple_of(n * 128, 128)
            lf = _log_sigmoid(f_ref[pl.ds(r0, 128), :] + b_ref[...])
            cum = _tri_dot(tri, lf) + carry
            cqe_ref[pl.ds(r0, 128), :] = _expand_heads(cum, 128)
            return cum[127:128, :]

        lax.fori_loop(0, nb, step, jnp.zeros((1, 128), F32))

    return pl.pallas_call(
        body, name=name, grid=(1,),
        in_specs=[_full_spec((s, 128)), _full_spec((1, 128))],
        out_specs=_full_spec((s, HEADS * HEAD_DIM)),
        out_shape=jax.ShapeDtypeStruct((s, HEADS * HEAD_DIM), F32),
        compiler_params=_cparams(("arbitrary",)),
    )(f_logit, b_f)


def _forget_bwd(dcq16, sum_q16, f_logit, b_f, *, name):
    s = f_logit.shape[0]
    nb = s // 128

    def body(a_ref, k_ref, f_ref, b_ref, df_ref, db_ref):
        row = lax.broadcasted_iota(I32, (128, 128), 0)
        col = lax.broadcasted_iota(I32, (128, 128), 1)
        tri_rev = (col >= row).astype(BF16)

        def step(m, carry):
            suffix, dbsum = carry
            n = nb - 1 - m
            r0 = pl.multiple_of(n * 128, 128)
            dcum = a_ref[pl.ds(r0, 128), :] - k_ref[pl.ds(r0, 128), :]
            dlf = _tri_dot(tri_rev, dcum) + suffix
            df = dlf * _sigmoid(-(f_ref[pl.ds(r0, 128), :] + b_ref[...]))
            df_ref[pl.ds(r0, 128), :] = df.astype(BF16)
            return dlf[0:1, :], dbsum + jnp.sum(df, axis=0, keepdims=True)

        _, dbsum = lax.fori_loop(0, nb, step, (jnp.zeros((1, 128), F32), jnp.zeros((1, 128), F32)))
        db_ref[...] = dbsum

    return pl.pallas_call(
        body, name=name, grid=(1,),
        in_specs=[_full_spec((s, 128))] * 3 + [_full_spec((1, 128))],
        out_specs=[_full_spec((s, 128)), _full_spec((1, 128))],
        out_shape=[jax.ShapeDtypeStruct((s, 128), BF16), jax.ShapeDtypeStruct((1, 128), F32)],
        compiler_params=_cparams(("arbitrary",)),
    )(dcq16, sum_q16, f_logit, b_f)


ATT_T = 256


def _head_lanes(rows):
    return lax.broadcasted_iota(I32, (rows, 128), 1) < HEAD_DIM


def _bf16_pieces(c):
    p0 = c.astype(BF16).astype(F32)
    r = c - p0
    p1 = r.astype(BF16).astype(F32)
    p2 = (r - p1).astype(BF16).astype(F32)
    return p0, p1, p2


def _col_reduce(x, op):
    rows = x.shape[0]
    while rows > 8:
        rows //= 2
        x = op(x[:rows], x[rows:])
    return jnp.max(x, axis=0, keepdims=True) if op is jnp.maximum else jnp.sum(x, axis=0, keepdims=True)


def _attn_prep(qkv, cqe, carry=None, *, name):
    s = qkv.shape[0]
    npair = HEADS // 2

    def body(q_ref, k_ref, v_ref, c_ref, qa_ref, ka_ref, vt_ref):
        rows = 128
        lane = lax.broadcasted_iota(I32, (rows, 128), 1)

        def chunk(n, _):
            r0 = pl.multiple_of(n * rows, rows)
            sl = pl.ds(r0, rows)
            qv = q_ref[sl, :].astype(F32) * ATT_SCALE
            kv = k_ref[sl, :].astype(F32)
            p0, p1, p2 = _bf16_pieces(pltpu.roll(c_ref[sl, :], HEAD_DIM, 1))
            for e in range(2):
                mine = (lane < HEAD_DIM) if e == 0 else (lane >= HEAD_DIM)
                base = HEAD_DIM * (1 - e)
                ones_hi = jnp.where((lane >= base + 3) & (lane < base + 6), 1.0, 0.0)
                ones_lo = jnp.where((lane >= base) & (lane < base + 3), 1.0, 0.0)
                qa = jnp.where(mine, qv, jnp.where(lane == base, p0, jnp.where(lane == base + 1, p1,
                               jnp.where(lane == base + 2, p2, ones_hi))))
                ka = jnp.where(mine, kv, jnp.where(lane == base + 3, -p0, jnp.where(lane == base + 4, -p1,
                               jnp.where(lane == base + 5, -p2, ones_lo))))
                qa_ref[e, sl, :] = qa.astype(BF16)
                ka_ref[e, sl, :] = ka.astype(BF16)
            vt_ref[0, :, sl] = v_ref[sl, :].astype(F32).T.astype(BF16)
            return 0

        lax.fori_loop(0, s // rows, chunk, 0)

    pair = pl.BlockSpec((2, s, 128), lambda hp: (hp, 0, 0))
    return _carry_call(
        body, carry, name=name, grid=(npair,),
        in_specs=[pl.BlockSpec((s, 128), lambda hp: (0, hp)), pl.BlockSpec((s, 128), lambda hp: (0, npair + hp)),
                  pl.BlockSpec((s, 128), lambda hp: (0, 2 * npair + hp)), pl.BlockSpec((s, 128), lambda hp: (0, hp))],
        out_specs=[pair, pair, pl.BlockSpec((1, 128, s), lambda hp: (hp, 0, 0))],
        out_shape=[jax.ShapeDtypeStruct((HEADS, s, 128), BF16), jax.ShapeDtypeStruct((HEADS, s, 128), BF16),
                   jax.ShapeDtypeStruct((npair, 128, s), BF16)],
        scratch_shapes=[], args=[qkv, qkv, qkv, cqe])


def _attn_fwd(qa, ka, vt, carry=None, *, name):
    s = qa.shape[1]
    t = 2 * ATT_T
    nq = s // t
    npair = HEADS // 2

    def body(qa_ref, ka_ref, vt_ref, o_ref, lse_ref):
        i = pl.program_id(1)
        krow = lax.broadcasted_iota(I32, (t, t), 0)
        qcol = lax.broadcasted_iota(I32, (t, t), 1)
        sub = lax.broadcasted_iota(I32, (128, t), 0)
        row8 = lax.broadcasted_iota(I32, (8, t), 0)
        qbs = (qa_ref[0], qa_ref[1])
        tk = t

        def step(j, carry, diag):
            c0 = pl.multiple_of(j * tk, tk)
            vtb = vt_ref[0, :, pl.ds(c0, tk)]
            sts = [_dot(ka_ref[e, pl.ds(c0, tk), :], qbs[e], NT) for e in range(2)]
            if diag:
                sts = [jnp.where(krow <= qcol, st, NEG) for st in sts]
            pts, stats = [], []
            for e in range(2):
                m, l, _ = carry[e]
                m_new = jnp.maximum(m, _col_reduce(sts[e], jnp.maximum))
                alpha = jnp.exp(m - m_new)
                pt = jnp.exp(sts[e] - m_new)
                stats.append((m_new, alpha, alpha * l + _col_reduce(pt, jnp.add)))
                pts.append(pt.astype(BF16))
            pvs = [_dot(vtb, pts[e], NN) for e in range(2)]
            return tuple((stats[e][0], stats[e][2], stats[e][1] * carry[e][2] + pvs[e]) for e in range(2))

        init = (jnp.full((1, t), NEG, F32), jnp.zeros((1, t), F32), jnp.zeros((128, t), F32))
        carry = lax.fori_loop(0, i, functools.partial(step, diag=False), (init, init))
        (m0, l0, acc0), (m1, l1, acc1) = step(i, carry, True)
        o_pair = jnp.where(sub < HEAD_DIM, acc0 / l0, acc1 / l1)
        o_ref[...] = o_pair.T.astype(BF16)
        lse_ref[0] = jnp.where(row8 == 0, m0 + jnp.log(l0), jnp.where(row8 == 1, m1 + jnp.log(l1), 0.0))

    return _carry_call(
        body, carry, name=name, grid=(npair, nq),
        in_specs=[pl.BlockSpec((2, t, 128), lambda hp, i: (hp, i, 0)), pl.BlockSpec((2, s, 128), lambda hp, i: (hp, 0, 0)),
                  pl.BlockSpec((1, 128, s), lambda hp, i: (hp, 0, 0))],
        out_specs=[pl.BlockSpec((t, 128), lambda hp, i: (i, hp)), pl.BlockSpec((1, 8, t), lambda hp, i: (hp, 0, i))],
        out_shape=[jax.ShapeDtypeStruct((s, HEADS * HEAD_DIM), BF16), jax.ShapeDtypeStruct((npair, 8, s), F32)],
        scratch_shapes=[], args=[qa, ka, vt])


def _attn_delta(do, o, carry=None, *, name):
    s = do.shape[0]

    def body(do_ref, o_ref, d_ref):
        prod = do_ref[...].astype(F32) * o_ref[...].astype(F32)
        row = lax.broadcasted_iota(I32, (8, 128), 0)
        lane = lax.broadcasted_iota(I32, (8, 128), 1)
        sel = ((row == 0) & (lane < HEAD_DIM) | (row == 1) & (lane >= HEAD_DIM)).astype(BF16)
        p0, p1, p2 = _bf16_pieces(prod)
        d_ref[0] = (_dot(sel, p0.astype(BF16), NT) + _dot(sel, p1.astype(BF16), NT)) + _dot(sel, p2.astype(BF16), NT)

    pair = pl.BlockSpec((s, 128), lambda hp: (0, hp))
    (delta3,), carried = _carry_call(
        body, carry, name=name, grid=(HEADS // 2,), in_specs=[pair, pair],
        out_specs=[pl.BlockSpec((1, 8, s), lambda hp: (hp, 0, 0))],
        out_shape=[jax.ShapeDtypeStruct((HEADS // 2, 8, s), F32)], scratch_shapes=[], args=[do, o])
    return delta3, carried


def _attn_bwd(qa, ka, qkv, do, lse3, delta3, carry=None, *, name):
    s = qa.shape[1]
    t = 2 * ATT_T
    nb = s // t
    npair = HEADS // 2

    def body(qa_ref, ka_ref, v_ref, do_ref, lse_ref, delta_ref, dq_ref, dk_ref, dv_ref, aux_ref, dcq_ref, dqt):
        hp = pl.program_id(0)
        first = _head_lanes(t)
        lane = lax.broadcasted_iota(I32, (t, 128), 1)
        dqt[...] = jnp.zeros_like(dqt)

        @pl.when(hp == 0)
        def _():
            aux_ref[...] = jnp.zeros_like(aux_ref)

        krow = lax.broadcasted_iota(I32, (t, t), 0)
        qcol = lax.broadcasted_iota(I32, (t, t), 1)

        def key_block(j, _):
            c0 = pl.multiple_of(j * t, t)
            vb = v_ref[pl.ds(c0, t), :]
            kbs = (ka_ref[0, pl.ds(c0, t), :], ka_ref[1, pl.ds(c0, t), :])
            kbts = tuple(kb.astype(F32).T.astype(BF16) for kb in kbs)
            vhs = (jnp.where(first, vb, jnp.zeros_like(vb)), jnp.where(first, jnp.zeros_like(vb), vb))

            def query_block(i, carry, diag):
                r0 = pl.multiple_of(i * t, t)
                dob = do_ref[pl.ds(r0, t), :]
                sts = [_dot(kbs[e], qa_ref[e, pl.ds(r0, t), :], NT) for e in range(2)]
                dpts = [_dot(vhs[e], dob, NT) for e in range(2)]
                ptbs, dsbs = [], []
                for e in range(2):
                    st = jnp.where(krow <= qcol, sts[e], NEG) if diag else sts[e]
                    pt = jnp.exp(st - lse_ref[0, e:e + 1, pl.ds(r0, t)])
                    dsbs.append((pt * (dpts[e] - delta_ref[0, e:e + 1, pl.ds(r0, t)])).astype(BF16))
                    ptbs.append(pt.astype(BF16))
                out = []
                for e in range(2):
                    dk_a, dv_a = carry[e]
                    dv_a = dv_a + _dot(ptbs[e], dob, NN)
                    dk_a = dk_a + _dot(dsbs[e], qa_ref[e, pl.ds(r0, t), :], NN)
                    dqt[e, :, pl.ds(r0, t)] += _dot(kbts[e], dsbs[e], NN)
                    out.append((dk_a, dv_a))
                return tuple(out)

            zero = jnp.zeros((t, 128), F32)
            carry = query_block(j, ((zero, zero), (zero, zero)), True)
            (dk0, dv0), (dk1, dv1) = lax.fori_loop(j + 1, nb, functools.partial(query_block, diag=False), carry)
            dk_ref[pl.ds(c0, t), :] = jnp.where(first, dk0, dk1).astype(BF16)
            dv_ref[pl.ds(c0, t), :] = jnp.where(first, dv0, dv1).astype(BF16)
            sum_q = jnp.where(lane == 2 * hp, dk0[:, HEAD_DIM + 3:HEAD_DIM + 4],
                              jnp.where(lane == 2 * hp + 1, dk1[:, 3:4], aux_ref[pl.ds(c0, t), :]))
            aux_ref[pl.ds(c0, t), :] = sum_q
            return 0

        lax.fori_loop(0, nb, key_block, 0)
        sub = lax.broadcasted_iota(I32, (128, s), 0)
        row8 = lax.broadcasted_iota(I32, (8, s), 0)
        dq_ref[...] = (jnp.where(sub < HEAD_DIM, dqt[0], dqt[1]) * ATT_SCALE).T.astype(BF16)
        dcq_ref[0] = jnp.where(row8 == 0, dqt[0, HEAD_DIM:HEAD_DIM + 1, :], jnp.where(row8 == 1, dqt[1, 0:1, :], 0.0))

    def pair_cols(off):
        return pl.BlockSpec((s, 128), lambda hp: (0, off + hp))

    heads = pl.BlockSpec((2, s, 128), lambda hp: (hp, 0, 0))
    rows = pl.BlockSpec((1, 8, s), lambda hp: (hp, 0, 0))
    wide = jax.ShapeDtypeStruct((s, HEADS * HEAD_DIM), BF16)
    return _carry_call(
        body, carry, name=name, grid=(npair,),
        in_specs=[heads, heads, pair_cols(2 * npair), pair_cols(0), rows, rows],
        out_specs=[pair_cols(0), pair_cols(0), pair_cols(0), pl.BlockSpec((s, 128), lambda hp: (0, 0)), rows],
        out_shape=[wide, wide, wide, jax.ShapeDtypeStruct((s, 128), F32), jax.ShapeDtypeStruct((npair, 8, s), F32)],
        scratch_shapes=[pltpu.VMEM((2, 128, s), F32)], args=[qa, ka, qkv, do, lse3, delta3])


def _adam_math(w, g, m, v):
    m = ADAM_B1 * m + (1.0 - ADAM_B1) * g
    v = ADAM_B2 * v + (1.0 - ADAM_B2) * (g * g)
    m_hat = m / (1.0 - ADAM_B1 ** ADAM_STEP)
    v_hat = v / (1.0 - ADAM_B2 ** ADAM_STEP)
    delta = -ADAM_LR * (m_hat / (jnp.sqrt(v_hat) + ADAM_EPS) + ADAM_WD * w)
    return delta, m, v


def _sum_pairs(keep, recv, pos, *, name):
    _, r, c = recv.shape
    tr = _row_tile(r, 512)

    def body(pos_ref, a_ref, b_ref, o32_ref, o16_ref):
        tot = a_ref[...].astype(F32) + b_ref[...].astype(F32)
        o16_ref[...] = tot.astype(BF16)

        @pl.when(pl.program_id(1) == 2 * pos_ref[0] + pos_ref[1])
        def _():
            o32_ref[...] = tot

    out = pl.BlockSpec((1, tr, c), lambda i, q, pos: (q, i, 0))
    grid_spec = pltpu.PrefetchScalarGridSpec(
        num_scalar_prefetch=1, grid=(r // tr, 4),
        in_specs=[pl.BlockSpec((1, tr, c), lambda i, q, pos: (2 * q + pos[2], i, 0)), out],
        out_specs=[pl.BlockSpec((1, tr, c), lambda i, q, pos: (0, i, 0)), out])
    return pl.pallas_call(
        body, name=name, grid_spec=grid_spec,
        out_shape=[jax.ShapeDtypeStruct((1, r, c), F32), jax.ShapeDtypeStruct((4, r, c), BF16)],
        compiler_params=_cparams(("arbitrary", "arbitrary")),
    )(pos, keep, recv)


def _adam_sharded(psum, recv, w, m, v, pos, *, name):
    r, c = w.shape
    tr = _row_tile(r, 320)

    def body(pos_ref, p_ref, r_ref, w_ref, m_ref, v_ref, g_ref, d_ref, mo_ref, vo_ref):
        g = p_ref[0] + r_ref[0].astype(F32) + r_ref[1].astype(F32) + r_ref[2].astype(F32)
        delta, mn, vn = _adam_math(w_ref[...], g, m_ref[...], v_ref[...])
        g_ref[...] = g
        d_ref[...] = delta
        mo_ref[...] = mn
        vo_ref[...] = vn

    row = pl.BlockSpec((tr, c), lambda i, pos: (i, 0))
    grid_spec = pltpu.PrefetchScalarGridSpec(
        num_scalar_prefetch=1, grid=(r // tr,),
        in_specs=[pl.BlockSpec((1, tr, c), lambda i, pos: (0, i, 0)),
                  pl.BlockSpec((3, tr, c), lambda i, pos: (0, i, 0)), row, row, row],
        out_specs=[row, row, row, row])
    o = jax.ShapeDtypeStruct((r, c), F32)
    return pl.pallas_call(
        body, name=name, grid_spec=grid_spec, out_shape=[o, o, o, o],
        compiler_params=_cparams(("parallel",)),
    )(pos, psum, recv, w, m, v)


def _adam_replicated(chip_sums, last, w, m, v, *, name):
    r = w.shape[0]

    def body(s_ref, l_ref, w_ref, m_ref, v_ref, g_ref, d_ref, mo_ref, vo_ref):
        g = (((s_ref[0] + s_ref[1]) + s_ref[2]) + s_ref[3]) + l_ref[...]
        delta, mn, vn = _adam_math(w_ref[...], g, m_ref[...], v_ref[...])
        g_ref[...] = g
        d_ref[...] = delta
        mo_ref[...] = mn
        vo_ref[...] = vn

    o = jax.ShapeDtypeStruct((r, 1024), F32)
    full = _full_spec((r, 1024))
    return pl.pallas_call(
        body, name=name, grid=(1,),
        in_specs=[_full_spec((4, r, 1024)), full, full, full, full], out_specs=[full] * 4, out_shape=[o] * 4,
        compiler_params=_cparams(("arbitrary",)),
    )(chip_sums, last, w, m, v)


ASM_OUT = 256
ASM_SRC = 304


def _w_in_row(r):
    return r if r < 2048 else (r + O_G - 2048 if r < 4096 else r - 2048)


def _assemble_wt_main(g, *, name):
    table = []
    for blk in range(MAIN_COLS // ASM_OUT):
        j, l0 = divmod(_w_in_row(blk * ASM_OUT), IN_SHARD)
        sb = l0 // ASM_SRC
        n_a = min(ASM_OUT, min(IN_SHARD, (sb + 1) * ASM_SRC) - l0)
        if n_a == ASM_OUT:
            nxt = (j, sb)
        elif l0 + n_a == IN_SHARD:
            nxt = (j + 1, 0)
        else:
            nxt = (j, sb + 1)
        table.append((j, sb, l0 - sb * ASM_SRC, n_a) + nxt)

    def body(tab_ref, a_ref, b_ref, o_ref):
        blk = pl.program_id(0)
        off, n_a = tab_ref[blk, 2], tab_ref[blk, 3]
        r = lax.broadcasted_iota(I32, (ASM_OUT, ASM_SRC), 0)
        k = lax.broadcasted_iota(I32, (ASM_OUT, ASM_SRC), 1)
        sel_a = ((k == r + off) & (r < n_a)).astype(BF16)
        sel_b = ((k == r - n_a) & (r >= n_a)).astype(BF16)
        o_ref[...] = (_dot(sel_a, a_ref[0], NN) + _dot(sel_b, b_ref[0], NN)).astype(BF16)

    src = lambda c: pl.BlockSpec((1, ASM_SRC, D_MODEL), lambda blk, tab: (tab[blk, c], tab[blk, c + 1], 0))
    grid_spec = pltpu.PrefetchScalarGridSpec(
        num_scalar_prefetch=1, grid=(len(table),), in_specs=[src(0), src(4)],
        out_specs=pl.BlockSpec((ASM_OUT, D_MODEL), lambda blk, tab: (blk, 0)))
    return pl.pallas_call(
        body, name=name, grid_spec=grid_spec, out_shape=jax.ShapeDtypeStruct((MAIN_COLS, D_MODEL), BF16),
        compiler_params=_cparams(("parallel",)),
    )(jnp.asarray(table, I32), g, g)


def _pair_sum_small(mine, theirs, *, name):
    def body(a_ref, b_ref, o_ref):
        o_ref[...] = a_ref[...] + b_ref[...]

    full = _full_spec(mine.shape)
    return pl.pallas_call(
        body, name=name, grid=(1,), in_specs=[full, full], out_specs=full,
        out_shape=jax.ShapeDtypeStruct(mine.shape, F32), compiler_params=_cparams(("arbitrary",)),
    )(mine, theirs)


ANY = pl.BlockSpec(memory_space=pl.ANY)
OTHER_CHIPS = ((1, 0), (0, 1), (1, 1))


class _Carry:
    def __init__(self, inputs, out_shapes, scratch, start, wait, aliases=None):
        self.inputs, self.out_shapes, self.scratch = list(inputs), list(out_shapes), list(scratch)
        self.start, self.wait, self.aliases = start, wait, dict(aliases or {})


def _carry_join(*carries):
    n_in = [len(c.inputs) for c in carries]
    n_out = [len(c.out_shapes) for c in carries]
    n_scr = [len(c.scratch) for c in carries]

    def split(refs, counts):
        out, k = [], 0
        for n in counts:
            out.append(refs[k:k + n])
            k += n
        return out

    def start(ins, outs, scr):
        for c, i, o, s in zip(carries, split(ins, n_in), split(outs, n_out), split(scr, n_scr)):
            c.start(i, o, s)

    def wait(ins, outs, scr):
        for c, i, o, s in zip(carries, split(ins, n_in), split(outs, n_out), split(scr, n_scr)):
            c.wait(i, o, s)

    aliases = {}
    for k, c in enumerate(carries):
        aliases.update({sum(n_in[:k]) + i: sum(n_out[:k]) + o for i, o in c.aliases.items()})
    joined = _Carry(sum((c.inputs for c in carries), []), sum((c.out_shapes for c in carries), []),
                    sum((c.scratch for c in carries), []), start, wait, aliases)
    joined.counts = n_out
    joined.split = lambda results: split(results, n_out)
    return joined


def _carried(body, carry, n_in, n_out, grid):
    if carry is None:
        return body
    ci, co, cs = len(carry.inputs), len(carry.out_shapes), len(carry.scratch)

    def wrapped(*refs):
        ins, cins = refs[:n_in], refs[n_in:n_in + ci]
        outs, couts = refs[n_in + ci:n_in + ci + n_out], refs[n_in + ci + n_out:n_in + ci + n_out + co]
        rest = refs[n_in + ci + n_out + co:]
        scratch, cscr = rest[:len(rest) - cs], rest[len(rest) - cs:]
        first, last = None, None
        for axis, size in enumerate(grid):
            f, l = pl.program_id(axis) == 0, pl.program_id(axis) == size - 1
            first = f if first is None else first & f
            last = l if last is None else last & l

        @pl.when(first)
        def _():
            carry.start(cins, couts, cscr)

        body(*ins, *outs, *scratch)

        @pl.when(last)
        def _():
            carry.wait(cins, couts, cscr)

    return wrapped


def _carry_call(body, carry, *, name, grid, in_specs, out_specs, out_shape, scratch_shapes, args, vmem=True,
                own_aliases=None):
    n_in, n_out = len(in_specs), len(out_specs)
    extra_in = [ANY] * len(carry.inputs) if carry else []
    extra_out = [ANY] * len(carry.out_shapes) if carry else []
    aliases = dict(own_aliases or {})
    if carry:
        aliases.update({n_in + i: n_out + o for i, o in carry.aliases.items()})
    out = pl.pallas_call(
        _carried(body, carry, n_in, n_out, grid), name=name, grid=grid,
        in_specs=list(in_specs) + extra_in, out_specs=list(out_specs) + extra_out,
        out_shape=list(out_shape) + (carry.out_shapes if carry else []),
        scratch_shapes=list(scratch_shapes) + (carry.scratch if carry else []),
        input_output_aliases=aliases,
        compiler_params=_cparams(("arbitrary",) * len(grid)) if vmem else None,
    )(*args, *(carry.inputs if carry else []))
    return list(out[:n_out]), list(out[n_out:])


def _run_carry(carry, *, name):
    return _carry_call(lambda: None, carry, name=name, grid=(1,), in_specs=[], out_specs=[], out_shape=[],
                       scratch_shapes=[], args=[], vmem=False)[1]


def _sems(n):
    return [pltpu.SemaphoreType.DMA((n,)), pltpu.SemaphoreType.DMA((n,))]


def _carry_gather1(shards):
    n = len(shards)

    def copies(x_refs, out_refs, scr, with_arrivals):
        send_sems, recv_sems, local_sems = scr
        x, y, c = lax.axis_index("x"), lax.axis_index("y"), lax.axis_index("c")
        peers = [(x, y, 1 - c)] + [(x ^ fx, y ^ fy, c) for fx, fy in OTHER_CHIPS]
        local, sends, arrivals = [], [], []
        for t, (x_ref, out_ref) in enumerate(zip(x_refs, out_refs)):
            local.append(pltpu.make_async_copy(x_ref, out_ref.at[4 * x + 2 * y + c], local_sems.at[t]))
            for k, (px, py, pc) in enumerate(peers):
                sems = dict(send_sem=send_sems.at[4 * t + k], recv_sem=recv_sems.at[4 * t + k],
                            device_id=(px, py, pc), device_id_type=MESH)
                sends.append(pltpu.make_async_remote_copy(src_ref=x_ref, dst_ref=out_ref.at[4 * x + 2 * y + c], **sems))
                if with_arrivals:
                    arrivals.append(
                        pltpu.make_async_remote_copy(src_ref=x_ref, dst_ref=out_ref.at[4 * px + 2 * py + pc], **sems))
        return local, sends, arrivals

    def start(x_refs, out_refs, scr):
        local, sends, _ = copies(x_refs, out_refs, scr, False)
        for cp in local + sends:
            cp.start()

    def wait(x_refs, out_refs, scr):
        local, sends, arrivals = copies(x_refs, out_refs, scr, True)
        for cp in arrivals:
            cp.wait_recv()
        for cp in sends:
            cp.wait_send()
        for cp in local:
            cp.wait()

    return _Carry(shards, [jax.ShapeDtypeStruct((N_DEV,) + a.shape, a.dtype) for a in shards],
                  _sems(4 * n) + [pltpu.SemaphoreType.DMA((n,))], start, wait)


def _carry_gather2(gathered):
    n = len(gathered)

    def copies(in_refs, g_refs, scr, with_arrivals):
        send_sems, recv_sems = scr
        x, y, c = lax.axis_index("x"), lax.axis_index("y"), lax.axis_index("c")
        sends, arrivals = [], []
        for t in range(n):
            for j, (fx, fy) in enumerate(OTHER_CHIPS):
                px, py = x ^ fx, y ^ fy
                sems = dict(send_sem=send_sems.at[3 * t + j], recv_sem=recv_sems.at[3 * t + j],
                            device_id=(x, y, 1 - c), device_id_type=MESH)
                mine, theirs = 4 * px + 2 * py + c, 4 * px + 2 * py + (1 - c)
                sends.append(pltpu.make_async_remote_copy(src_ref=in_refs[t].at[mine], dst_ref=g_refs[t].at[mine], **sems))
                if with_arrivals:
                    arrivals.append(pltpu.make_async_remote_copy(
                        src_ref=in_refs[t].at[mine], dst_ref=g_refs[t].at[theirs], **sems))
        return sends, arrivals

    def start(in_refs, g_refs, scr):
        for cp in copies(in_refs, g_refs, scr, False)[0]:
            cp.start()

    def wait(in_refs, g_refs, scr):
        sends, arrivals = copies(in_refs, g_refs, scr, True)
        for cp in arrivals:
            cp.wait_recv()
        for cp in sends:
            cp.wait_send()

    return _Carry(gathered, [jax.ShapeDtypeStruct(a.shape, a.dtype) for a in gathered], _sems(3 * n), start, wait,
                  aliases={t: t for t in range(n)})


def _allreduce_rows(x, *, name):
    def body(x_ref, o_ref, sib_ref, mine_ref, tab_ref, send_sems, recv_sems):
        x, y, c = lax.axis_index("x"), lax.axis_index("y"), lax.axis_index("c")
        swap = pltpu.make_async_remote_copy(src_ref=x_ref, dst_ref=sib_ref, send_sem=send_sems.at[0],
                                            recv_sem=recv_sems.at[0], device_id=(x, y, 1 - c), device_id_type=MESH)
        swap.start()
        swap.wait()
        mine_ref[...] = x_ref[...] + sib_ref[...]
        tab_ref[pl.ds(2 * x + y, 1)] = mine_ref[...][None]

        def copy(k, slot):
            fx, fy = OTHER_CHIPS[k]
            return pltpu.make_async_remote_copy(
                src_ref=mine_ref, dst_ref=tab_ref.at[slot], send_sem=send_sems.at[1 + k], recv_sem=recv_sems.at[1 + k],
                device_id=(x ^ fx, y ^ fy, c), device_id_type=MESH)

        for k in range(3):
            copy(k, 2 * x + y).start()
        for k, (fx, fy) in enumerate(OTHER_CHIPS):
            copy(k, 2 * (x ^ fx) + (y ^ fy)).wait()
        o_ref[...] = ((tab_ref[0] + tab_ref[1]) + tab_ref[2]) + tab_ref[3]

    vmem = pl.BlockSpec(memory_space=pltpu.VMEM)
    return pl.pallas_call(
        body, name=name, out_shape=jax.ShapeDtypeStruct(x.shape, F32), in_specs=[vmem], out_specs=vmem,
        scratch_shapes=[pltpu.VMEM(x.shape, F32), pltpu.VMEM(x.shape, F32), pltpu.VMEM((4,) + x.shape, F32)] + _sems(4),
    )(x)


def _allgather(shards, *, name):
    n = len(shards)

    def body(*refs):
        x_refs, out_refs = refs[:n], refs[n:2 * n]
        send_sems, recv_sems, local_sems = refs[2 * n:]
        x, y, c = lax.axis_index("x"), lax.axis_index("y"), lax.axis_index("c")
        me, sibling = (x, y, c), (x, y, 1 - c)
        chips = [(x ^ fx, y ^ fy) for fx, fy in OTHER_CHIPS]

        def copy(t, k, block, to, from_input=False):
            px, py, pc = block
            slab = out_refs[t].at[4 * px + 2 * py + pc]
            return pltpu.make_async_remote_copy(
                src_ref=x_refs[t] if from_input else slab, dst_ref=slab,
                send_sem=send_sems.at[7 * t + k], recv_sem=recv_sems.at[7 * t + k], device_id=to, device_id_type=MESH)

        mine = [pltpu.make_async_copy(x_refs[t], out_refs[t].at[4 * x + 2 * y + c], local_sems.at[t]) for t in range(n)]
        for cp in mine:
            cp.start()
        first = []
        for t in range(n):
            first.append(copy(t, 0, me, sibling, from_input=True))
            first += [copy(t, 1 + j, me, (*chip, c), from_input=True) for j, chip in enumerate(chips)]
        for cp in first:
            cp.start()
        passed = []
        for j, chip in enumerate(chips):
            for t in range(n):
                copy(t, 1 + j, (*chip, c), me).wait_recv()
                fwd = copy(t, 4 + j, (*chip, c), sibling)
                fwd.start()
                passed.append(fwd)
        for t in range(n):
            copy(t, 0, sibling, me).wait_recv()
            for j, chip in enumerate(chips):
                copy(t, 4 + j, (*chip, 1 - c), me).wait_recv()
        for cp in first + passed:
            cp.wait_send()
        for cp in mine:
            cp.wait()

    return pl.pallas_call(
        body, name=name, out_shape=[jax.ShapeDtypeStruct((N_DEV,) + a.shape, a.dtype) for a in shards],
        in_specs=[ANY] * n, out_specs=[ANY] * n,
        scratch_shapes=[pltpu.SemaphoreType.DMA((7 * n,)), pltpu.SemaphoreType.DMA((7 * n,)),
                        pltpu.SemaphoreType.DMA((n,))],
    )(*shards)


def _carry_sibling(slabs, small=None):
    n = len(slabs)
    extra = [] if small is None else [small]

    def copies(in_refs, out_refs, scr):
        send_sems, recv_sems = scr
        x, y, c = lax.axis_index("x"), lax.axis_index("y"), lax.axis_index("c")
        sibling = (x, y, 1 - c)
        out = []
        for t in range(n):
            for q in range(4):
                out.append(pltpu.make_async_remote_copy(
                    src_ref=in_refs[t].at[2 * q + (1 - c)], dst_ref=out_refs[t].at[q],
                    send_sem=send_sems.at[4 * t + q], recv_sem=recv_sems.at[4 * t + q],
                    device_id=sibling, device_id_type=MESH))
        if extra:
            out.append(pltpu.make_async_remote_copy(
                src_ref=in_refs[n], dst_ref=out_refs[n], send_sem=send_sems.at[4 * n], recv_sem=recv_sems.at[4 * n],
                device_id=sibling, device_id_type=MESH))
        return out

    def start(*refs):
        for cp in copies(*refs):
            cp.start()

    def wait(*refs):
        for cp in copies(*refs):
            cp.wait()

    return _Carry(list(slabs) + extra,
                  [jax.ShapeDtypeStruct((4,) + a.shape[1:], a.dtype) for a in slabs]
                  + [jax.ShapeDtypeStruct(a.shape, a.dtype) for a in extra], _sems(4 * n + 1), start, wait)


def _carry_chips(psums, small_sum=None):
    n = len(psums)
    table = small_sum is not None

    def copies(in_refs, out_refs, scr, arrivals):
        send_sems, recv_sems = scr[0], scr[1]
        x, y, c = lax.axis_index("x"), lax.axis_index("y"), lax.axis_index("c")
        out = []
        for k, (fx, fy) in enumerate(OTHER_CHIPS):
            px, py = x ^ fx, y ^ fy
            for t in range(n):
                out.append(pltpu.make_async_remote_copy(
                    src_ref=in_refs[t].at[2 * px + py], dst_ref=out_refs[t].at[k],
                    send_sem=send_sems.at[3 * t + k], recv_sem=recv_sems.at[3 * t + k],
                    device_id=(px, py, c), device_id_type=MESH))
            if table:
                slot = 2 * px + py if arrivals else 2 * x + y
                out.append(pltpu.make_async_remote_copy(
                    src_ref=in_refs[n], dst_ref=out_refs[n].at[slot], send_sem=send_sems.at[3 * n + k],
                    recv_sem=recv_sems.at[3 * n + k], device_id=(px, py, c), device_id_type=MESH))
        return out

    def own(in_refs, out_refs, scr):
        x, y = lax.axis_index("x"), lax.axis_index("y")
        return pltpu.make_async_copy(in_refs[n], out_refs[n].at[2 * x + y], scr[2])

    def start(in_refs, out_refs, scr):
        if table:
            own(in_refs, out_refs, scr).start()
        for cp in copies(in_refs, out_refs, scr, False):
            cp.start()

    def wait(in_refs, out_refs, scr):
        for cp in copies(in_refs, out_refs, scr, True):
            cp.wait()
        if table:
            own(in_refs, out_refs, scr).wait()

    out_shapes = [jax.ShapeDtypeStruct((3,) + a.shape[1:], a.dtype) for a in psums]
    if table:
        out_shapes.append(jax.ShapeDtypeStruct((4,) + small_sum.shape, F32))
    return _Carry(list(psums) + ([small_sum] if table else []), out_shapes,
                  _sems(3 * n + 3) + ([pltpu.SemaphoreType.DMA] if table else []), start, wait)


def _to_comm(name, kind, block, dtype=BF16):
    a = block[0]
    if kind == "cols":
        a = a.T
        if name == "w_in":
            a = jnp.pad(a, ((0, IN_SHARD_PAD - IN_SHARD), (0, 0)))
    return a if kind == "f32" else a.astype(dtype)


def _from_comm(name, kind, a):
    if kind == "cols":
        if name == "w_in":
            a = a[:IN_SHARD]
        a = a.T
    return a[None]


def _assemble_weights(g):
    out = {}
    if "w_in" in g:
        out["wt_main"] = _assemble_wt_main(g["w_in"], name="assemble_w_in")
        j, l0 = divmod(O_F, IN_SHARD)
        out["wt_f"] = jnp.pad(g["w_in"][j, l0:l0 + HEADS], ((0, 128 - HEADS), (0, 0)))
    square = dict(w_branch_a="w_a", w_branch_b="w_b", w_out="w_out", w_ple_gate="w_pg")
    for long, short in square.items():
        if long in g:
            out[short] = g[long].reshape(D_MODEL, D_MODEL)
    if "w_up" in g:
        out["wt_up"] = g["w_up"].reshape(2 * D_FF, D_MODEL)
    if "conv_w" in g:
        out["conv_w"] = g["conv_w"].transpose(1, 0, 2).reshape(3, 2 * D_FF)
    if "w_down" in g:
        out["w_down"] = g["w_down"].reshape(D_FF, D_MODEL)
    if "w_ple" in g:
        out["wt_ple"] = g["w_ple"].reshape(D_MODEL, PLE_DIM)
    return out


def _grad_slabs(gr):
    out = {}
    if "wt_main" in gr:
        gm, gf = gr["wt_main"], gr["wt_f"]
        segments = ((0, 2048, gm, 0), (2048, O_F, gm, 2048), (O_F, O_G, gf, -O_F), (O_G, IN_COLS, gm, 2048 - O_G))
        slabs = []
        for j in range(N_DEV):
            lo, hi = j * IN_SHARD, (j + 1) * IN_SHARD
            pieces = [src[max(lo, a) + shift:min(hi, b) + shift] for a, b, src, shift in segments if max(lo, a) < min(hi, b)]
            pieces.append(jnp.zeros((IN_SHARD_PAD - IN_SHARD, D_MODEL), gm.dtype))
            slabs.append(jnp.concatenate(pieces, axis=0))
        out["w_in"] = jnp.stack(slabs)
    rows = dict(w_a="w_branch_a", w_b="w_branch_b", w_out="w_out", wt_up="w_up", w_down="w_down", w_pg="w_ple_gate")
    for short, long in rows.items():
        if short in gr:
            out[long] = gr[short].reshape(N_DEV, -1, D_MODEL)
    if "conv_w" in gr:
        out["conv_w"] = gr["conv_w"].reshape(3, N_DEV, -1).transpose(1, 0, 2)
    if "wt_ple" in gr:
        out["w_ple"] = gr["wt_ple"].reshape(N_DEV, -1, PLE_DIM)
    return {k: v.astype(BF16) for k, v in out.items()}


def _rows(a, rows):
    flat = a.reshape(-1)
    return jnp.pad(flat, (0, rows * 1024 - flat.shape[0])).reshape(rows, 1024)


def _pack_small(parts):
    return jnp.concatenate([_rows(parts[n].astype(F32), r) for n, r in SMALL], axis=0)


def _small(packed, name, shape):
    off, r = SMALL_OFF[name]
    n = math.prod(shape)
    return packed[off:off + r].reshape(-1)[:n].reshape(shape)


class _Exchanges:
    W_S_ROWS = SMALL_OFF["gmlp_w_s"]

    def __init__(self, later, shards, pos):
        self.later, self.shards, self.pos = later, dict(zip(later, shards)), pos
        self.level1, self.slabs, self.from_sib, self.sums32, self.reduced, self.tables = {}, {}, {}, {}, {}, {}

    def gather1(self, names):
        carry = _carry_gather1([self.shards[n] for n in names])
        carry.names = names
        return carry

    def gather1_done(self, carry, results):
        self.level1.update(zip(carry.names, results))

    def gather2(self):
        return _carry_gather2([self.level1[n] for n in self.later])

    def weights(self, full):
        return _assemble_weights(dict(zip(self.later, full)))

    def sibling(self, grads):
        slabs = _grad_slabs(grads)
        self.slabs.update(slabs)
        carry = _carry_sibling(list(slabs.values()))
        carry.names = list(slabs)
        return carry

    def sibling_done(self, carry, results):
        self.from_sib.update(zip(carry.names, results))

    def chips(self, names, table=None):
        sums = {n: _sum_pairs(self.slabs[n], self.from_sib[n], self.pos, name="sum_sibling_" + n) for n in names}
        self.sums32.update({n: s32 for n, (s32, _) in sums.items()})
        carry = _carry_chips([s16 for _, s16 in sums.values()], None if table is None else self.table_part(table))
        carry.names, carry.table = list(names), table
        return carry

    def chips_done(self, carry, results):
        if carry.table is not None:
            *results, self.tables[carry.table] = results
        self.reduced.update({n: (self.sums32[n], r) for n, r in zip(carry.names, results)})

    def sibling_small(self, small_g):
        self.small_g = small_g
        return _carry_sibling([], small_g)

    def sibling_small_done(self, small_sib):
        self.small_chip = _pair_sum_small(self.small_g, small_sib, name="sum_sibling_small")

    def table_part(self, which):
        off, rows = self.W_S_ROWS
        if which == "w_s":
            return self.small_chip[off:off + rows]
        return jnp.concatenate([self.small_chip[:off], self.small_chip[off + rows:]], axis=0)

    def table(self):
        off = self.W_S_ROWS[0]
        rest = self.tables["rest"]
        return jnp.concatenate([rest[:, :off], self.tables["w_s"], rest[:, off:]], axis=1)


def _local_step(x, p, target, w, sm, ex=None):
    s = x.shape[0]
    mm = _matmul
    wt_main = w["wt_main"]
    conv_b = sm["conv_b"]
    bs_t = jnp.pad(sm["gmlp_b_s"].T, ((0, 0), (0, 128 - GROUPS)))
    b_f = jnp.pad(sm["b_f"], ((0, 0), (0, 128 - HEADS)))
    big = dict(tm=1024, tn=1024, tk=1024)
    whole_s = dict(tn=1024, tk=s)

    h = _rmsnorm_fwd(x, sm["norm_mix_g"], name="norm_mix")
    tall = dict(tm=s, tn=512, tk=1024)
    qkv_args = dict(mode="nt", out_dtype=BF16, name="in_qkv", n=3072, b_off=8, **tall)
    f_logit = mm(h, w["wt_f"], mode="nt", out_dtype=F32, name="in_f", tm=1024, tk=1024)
    cqe = _forget_cumsum(f_logit, b_f, name="forget_cumsum")
    uvg = dict(mode="nt", out_dtype=F32, name="in_uvg", n=4096, **tall)
    if ex is None:
        qkv = mm(h, wt_main, **qkv_args)
        (qa, ka, vt), _ = _attn_prep(qkv, cqe, name="attn_prep")
        (b, lse3), _ = _attn_fwd(qa, ka, vt, name="attn_fwd")
        zuvg = mm(h, wt_main, **uvg)
    else:
        groups = (["w_branch_a"], ["w_branch_b"], [n for n in ex.later if n not in ("w_branch_a", "w_branch_b")])
        carries = [ex.gather1(names) for names in groups]
        qkv, got0 = mm(h, wt_main, carry=carries[0], **qkv_args)
        (qa, ka, vt), got1 = _attn_prep(qkv, cqe, carries[1], name="attn_prep")
        (b, lse3), got2 = _attn_fwd(qa, ka, vt, carries[2], name="attn_fwd")
        for carry, got in zip(carries, (got0, got1, got2)):
            ex.gather1_done(carry, got)
        zuvg, full = mm(h, wt_main, carry=ex.gather2(), **uvg)
        w = {**w, **ex.weights(full)}
    a = _gmlp_fwd(zuvg, sm["gmlp_ln_g"], sm["gmlp_ln_b"], sm["gmlp_w_s"], bs_t, name="gmlp_fwd")
    wt_up, conv_w = w["wt_up"], w["conv_w"]
    ya, yb, merged = _branches_merge(a, b, w["w_a"], w["w_b"], zuvg, name="branches_merge")
    x1, h2 = mm(merged, w["w_out"], mode="nn", out_dtype=F32, name="out_proj", add=x, norm_g=sm["norm_ffn_g"], **big)
    up_a, up_g, act = _up_convglu(h2, wt_up, conv_w, conv_b, name="up_convglu")
    x2, h3 = mm(act, w["w_down"], mode="nn", out_dtype=F32, name="down", tm=1024, tn=1024, tk=1408, add=x1,
                norm_g=sm["norm_ple_g"])

    loss, dx3, dple, dgp, d_norm_final = _ple_loss(p, w["wt_ple"], h3, w["w_pg"], x2, target, sm["norm_final_g"],
                                                   name="ple_loss")
    g_wt_ple = mm(dple, p, mode="tn", out_dtype=BF16, name="d_w_ple", tm=512, tn=256, tk=s)
    g_w_pg = mm(h3, dgp, mode="tn", out_dtype=BF16, name="d_w_pg", tm=256, **whole_s)
    (dx2, dx2b, d_norm_ple), _ = _matmul_rmsnorm_bwd([dgp], w["w_pg"], dx3, x2, sm["norm_ple_g"], mode="nt", tk=1024,
                                                     name="d_h3_norm_ple_bwd")
    g_w_down = mm(act, dx2b, mode="tn", out_dtype=BF16, name="d_w_down", tm=256, **whole_s)
    dact_args = dict(mode="nt", out_dtype=BF16, name="d_act", tm=s, tn=256, tk=1024)
    if ex is None:
        dact = mm(dx2b, w["w_down"], **dact_args)
    else:
        early = ex.sibling(dict(w_pg=g_w_pg, wt_ple=g_wt_ple))
        dact, got = mm(dx2b, w["w_down"], carry=early, **dact_args)
        ex.sibling_done(early, got)
    dup_a, dup_g, dcw_a, dcw_g, dcb_a, dcb_g = _convglu_bwd(dact, up_a, up_g, conv_w, conv_b, name="convglu_bwd")
    g_wt_up = mm(dup_a, h2, mode="tn", out_dtype=BF16, name="d_w_up_a", tm=256, out_rows=2 * D_FF, **whole_s)
    g_wt_up = mm(dup_g, h2, mode="tn", out_dtype=BF16, name="d_w_up_g", tm=256, out_rows=2 * D_FF,
                 o_off=D_FF // 256, into=g_wt_up, **whole_s)
    (dx1, dx1b, d_norm_ffn), _ = _matmul_rmsnorm_bwd([dup_a, dup_g], wt_up, dx2, x1, sm["norm_ffn_g"], mode="nn",
                                                     tk=1408, name="d_h2_norm_ffn_bwd", resident=True)
    g_w_out = mm(merged, dx1b, mode="tn", out_dtype=BF16, name="d_w_out", tm=256, **whole_s)
    dya, dyb, dga, dgb = _merge_bwd(dx1b, w["w_out"], ya, yb, zuvg, name="merge_bwd")
    g_w_a = mm(a, dya, mode="tn", out_dtype=BF16, name="d_w_a", tm=256, **whole_s)
    g_w_b = mm(b, dyb, mode="tn", out_dtype=BF16, name="d_w_b", tm=256, **whole_s)
    da = mm(dya, w["w_a"], mode="nt", out_dtype=BF16, name="d_a", **big)
    db = mm(dyb, w["w_b"], mode="nt", out_dtype=BF16, name="d_b", **big)
    grads = dict(w_a=g_w_a, w_b=g_w_b, w_out=g_w_out, wt_up=g_wt_up, conv_w=jnp.concatenate([dcw_a, dcw_g], axis=1),
                 w_down=g_w_down, wt_ple=g_wt_ple, w_pg=g_w_pg)
    gmlp_args = (da, zuvg, sm["gmlp_ln_g"], sm["gmlp_ln_b"], sm["gmlp_w_s"], bs_t)
    if ex is None:
        (dzu, dzv, d_w_s, d_bs_t, d_ln_g, d_ln_b), _ = _gmlp_bwd(*gmlp_args, name="gmlp_bwd")
    else:
        rest = ex.sibling({k: v for k, v in grads.items() if k not in ("w_pg", "wt_ple")})
        early_chips = ex.chips(early.names)
        both = _carry_join(rest, early_chips)
        (dzu, dzv, d_w_s, d_bs_t, d_ln_g, d_ln_b), got = _gmlp_bwd(*gmlp_args, both, name="gmlp_bwd")
        got_rest, got_early = both.split(got)
        ex.sibling_done(rest, got_rest)
        ex.chips_done(early_chips, got_early)
    small = dict(norm_mix_g=jnp.zeros((1, D_MODEL), F32), b_f=jnp.zeros((1, HEADS), F32), gmlp_ln_g=d_ln_g,
                 gmlp_ln_b=d_ln_b, gmlp_w_s=d_w_s, gmlp_b_s=d_bs_t[:, :GROUPS].T, norm_ffn_g=d_norm_ffn,
                 conv_b=jnp.concatenate([dcb_a, dcb_g], axis=1), norm_ple_g=d_norm_ple, norm_final_g=d_norm_final)
    if ex is None:
        delta3, _ = _attn_delta(db, b, name="attn_delta")
        (dq, dk, dv, aux, dcq3), _ = _attn_bwd(qa, ka, qkv, db, lse3, delta3, name="attn_bwd")
    else:
        delta3, (small_sib,) = _attn_delta(db, b, ex.sibling_small(_pack_small(small)), name="attn_delta")
        ex.sibling_small_done(small_sib)
        main_chips = ex.chips(rest.names, table="rest")
        (dq, dk, dv, aux, dcq3), got = _attn_bwd(qa, ka, qkv, db, lse3, delta3, main_chips, name="attn_bwd")
        ex.chips_done(main_chips, got)
    dcq16 = jnp.pad(dcq3[:, :2, :].reshape(HEADS, s).T, ((0, 0), (0, 128 - HEADS)))
    dzf, d_b_f = _forget_bwd(dcq16, aux, f_logit, b_f, name="forget_bwd")
    dz_parts = [dzu, dzv, dga, dgb, dq, dk, dv]
    w_s_chips = None if ex is None else ex.chips([], table="w_s")
    g_wt_main, got = _grad_w_parts(dz_parts, h, name="d_w_main", tm=512, carry=w_s_chips)
    if ex is not None:
        ex.chips_done(w_s_chips, got)
    g_wt_f = mm(dzf, h, mode="tn", out_dtype=BF16, name="d_w_f", **whole_s)
    grads = dict(grads, wt_main=g_wt_main, wt_f=g_wt_f)
    w_in_chips = None
    if ex is not None:
        w_in_sib = ex.sibling(dict(wt_main=g_wt_main, wt_f=g_wt_f))
        ex.sibling_done(w_in_sib, _run_carry(w_in_sib, name="exchange_sibling_w_in"))
        w_in_chips = ex.chips(w_in_sib.names)
    (dx0, _, d_norm_mix), got = _matmul_rmsnorm_bwd(dz_parts, wt_main, dx1, x, sm["norm_mix_g"], mode="nn", tk=1024,
                                                    extra=(dzf, w["wt_f"]), name="d_h_norm_mix_bwd", carry=w_in_chips,
                                                    lead=True)
    if ex is not None:
        ex.chips_done(w_in_chips, got)
    return loss, dx0, grads, dict(small, norm_mix_g=d_norm_mix, b_f=d_b_f[:, :HEADS])


def kernel(x, p, norm_mix_g, w_in, b_f, gmlp_ln_g, gmlp_ln_b, gmlp_w_s, gmlp_b_s, w_branch_a, w_branch_b, w_out, norm_ffn_g, w_up, conv_w, conv_b, w_down, norm_ple_g, w_ple, w_ple_gate, norm_final_g, loss_target, m_norm_mix_g, m_w_in, m_b_f, m_gmlp_ln_g, m_gmlp_ln_b, m_gmlp_w_s, m_gmlp_b_s, m_w_branch_a, m_w_branch_b, m_w_out, m_norm_ffn_g, m_w_up, m_conv_w, m_conv_b, m_w_down, m_norm_ple_g, m_w_ple, m_w_ple_gate, m_norm_final_g, v_norm_mix_g, v_w_in, v_b_f, v_gmlp_ln_g, v_gmlp_ln_b, v_gmlp_w_s, v_gmlp_b_s, v_w_branch_a, v_w_branch_b, v_w_out, v_norm_ffn_g, v_w_up, v_conv_w, v_conv_b, v_w_down, v_norm_ple_g, v_w_ple, v_w_ple_gate, v_norm_final_g):
    given = dict(locals())
    weights = {n: given[n] for n in WEIGHT_ORDER}
    mom_m = {n: given["m_" + n] for n in WEIGHT_ORDER}
    mom_v = {n: given["v_" + n] for n in WEIGHT_ORDER}
    pos = jnp.stack([lax.axis_index("x"), lax.axis_index("y"), lax.axis_index("c")]).astype(I32)
    names = [n for n, _ in SHARDED]
    kinds = dict(SHARDED)

    later = [n for n in names if n != "w_in"]

    first = _allgather([_to_comm("w_in", kinds["w_in"], weights["w_in"])], name="allgather_w_in")
    ex = _Exchanges(later, [_to_comm(n, kinds[n], weights[n]) for n in later], pos)

    sm = dict(norm_mix_g=norm_mix_g, b_f=b_f, gmlp_ln_g=gmlp_ln_g, gmlp_ln_b=gmlp_ln_b, gmlp_w_s=gmlp_w_s[0],
              gmlp_b_s=gmlp_b_s[0], norm_ffn_g=norm_ffn_g, conv_b=conv_b, norm_ple_g=norm_ple_g,
              norm_final_g=norm_final_g.reshape(1, D_MODEL))
    loss_part, dx0, grads, small = _local_step(
        x[0], p[0, 0], loss_target[0], _assemble_weights({"w_in": first[0]}), sm, ex)

    b_f_and_loss = jnp.concatenate([small["b_f"].reshape(-1), loss_part[0, :1]])
    last = _allreduce_rows(jnp.concatenate([_rows(small["norm_mix_g"], 8), _rows(b_f_and_loss, 8)], axis=0),
                           name="allreduce_last")
    loss = last[8, HEADS]
    small_last = jnp.pad(last, ((0, SMALL_ROWS - 16), (0, 0)))

    grad, delta, new_m, new_v = {}, {}, {}, {}
    for n in names:
        s32, r = ex.reduced[n]
        outs = _adam_sharded(s32, r, *[_to_comm(n, kinds[n], src[n], F32) for src in (weights, mom_m, mom_v)], pos,
                             name="adam_" + n)
        grad[n], delta[n], new_m[n], new_v[n] = [_from_comm(n, kinds[n], o) for o in outs]
    replicated = [n for n, _ in SMALL]
    rep = lambda src: _pack_small({n: src[n] for n in replicated})
    packed = _adam_replicated(ex.table(), small_last, rep(weights), rep(mom_m), rep(mom_v), name="adam_replicated")
    for out, pk in zip((grad, delta, new_m, new_v), packed):
        for n in replicated:
            out[n] = _small(pk, n, weights[n].shape)

    return (loss, dx0, *[grad[n] for n in WEIGHT_ORDER], *[delta[n] for n in WEIGHT_ORDER],
            *[new_m[n] for n in WEIGHT_ORDER], *[new_v[n] for n in WEIGHT_ORDER])
```

```python
import functools
import math

import jax
import jax.numpy as jnp
from jax import lax
from jax.experimental import pallas as pl
from jax.experimental.pallas import tpu as pltpu

F32 = jnp.float32
BF16 = jnp.bfloat16
I32 = jnp.int32

D_MODEL = 1024
GROUPS = 8
GDIM = 128
GBLOCK = 128
CHUNK = 64
HEADS = 16
HEAD_DIM = 64
D_FF = 2816
PLE_DIM = 256
EPS = 1e-6
N_DEV = 8
ATT_SCALE = HEAD_DIM ** -0.5
NEG = -1e30

ADAM_LR = 0.001
ADAM_B1 = 0.9
ADAM_B2 = 0.999
ADAM_EPS = 1e-08
ADAM_WD = 0.01
ADAM_STEP = 10

V7X_VMEM_LIMIT = 48 * 1024 * 1024
MESH = pl.DeviceIdType.MESH

O_F = 2 * 1024 + 3 * 1024
O_G = O_F + HEADS
IN_COLS = O_G + 2 * D_MODEL
MAIN_COLS = IN_COLS - HEADS
IN_SHARD = IN_COLS // N_DEV
IN_SHARD_PAD = 912

SHARDED = (("w_in", "cols"), ("w_branch_a", "rows"), ("w_branch_b", "rows"), ("w_out", "rows"), ("w_up", "cols"),
           ("conv_w", "f32"), ("w_down", "rows"), ("w_ple", "cols"), ("w_ple_gate", "rows"))

SMALL = (("norm_mix_g", 8), ("b_f", 8), ("gmlp_ln_g", 8), ("gmlp_ln_b", 8), ("gmlp_w_s", 128), ("gmlp_b_s", 8),
         ("norm_ffn_g", 8), ("conv_b", 8), ("norm_ple_g", 8), ("norm_final_g", 8))
SMALL_OFF = {}
_o = 0
for _n, _r in SMALL:
    SMALL_OFF[_n] = (_o, _r)
    _o += _r
SMALL_ROWS = _o

WEIGHT_ORDER = ("norm_mix_g", "w_in", "b_f", "gmlp_ln_g", "gmlp_ln_b", "gmlp_w_s", "gmlp_b_s", "w_branch_a",
                "w_branch_b", "w_out", "norm_ffn_g", "w_up", "conv_w", "conv_b", "w_down", "norm_ple_g", "w_ple",
                "w_ple_gate", "norm_final_g")


def _cparams(sem):
    return pltpu.CompilerParams(dimension_semantics=sem, vmem_limit_bytes=V7X_VMEM_LIMIT)


def _gelu(x):
    c = math.sqrt(2.0 / math.pi)
    return 0.5 * x * (1.0 + jnp.tanh(c * (x + 0.044715 * x * x * x)))


def _gelu_and_grad(x):
    c = math.sqrt(2.0 / math.pi)
    t = jnp.tanh(c * (x + 0.044715 * x * x * x))
    g = 0.5 * x * (1.0 + t)
    dg = 0.5 * (1.0 + t) + 0.5 * x * (1.0 - t * t) * (c * (1.0 + 3.0 * 0.044715 * x * x))
    return g, dg


def _sigmoid(x):
    return 1.0 / (1.0 + jnp.exp(-x))


def _dot(a, b, dims):
    return lax.dot_general(a, b, (dims, ((), ())), preferred_element_type=F32)


NN = ((1,), (0,))
NT = ((1,), (1,))
TN = ((0,), (0,))


def _row_tile(rows, most):
    best = None
    for t in range(16, min(rows, most) + 1, 16):
        if rows % t == 0:
            best = t
    return best if best is not None else rows


def _matmul(a, b, *, mode, out_dtype, name, tm=512, tn=512, tk=512, add=None, n=None, b_off=0,
            out_rows=None, o_off=0, into=None, norm_g=None, carry=None):
    if mode == "tn":
        kdim, m = a.shape
    else:
        m, kdim = a.shape
    if n is None:
        n = b.shape[0] if mode == "nt" else b.shape[1]
    tm, tn, tk = min(tm, m), min(tn, n), min(tk, kdim)
    assert m % tm == 0 and n % tn == 0 and kdim % tk == 0, (name, m, n, kdim, tm, tn, tk)
    nk = kdim // tk
    dims = {"nn": NN, "nt": NT, "tn": TN}[mode]

    n_in = 2 + (add is not None) + (into is not None) + (norm_g is not None)
    assert norm_g is None or tn == n, "the RMS norm needs whole rows"

    def finish(r, refs):
        if add is not None:
            r = refs[2][...].astype(F32) + r
        refs[n_in][...] = r.astype(out_dtype)
        if norm_g is not None:
            rs = lax.rsqrt(jnp.mean(r * r, axis=-1, keepdims=True) + EPS)
            refs[n_in + 1][...] = ((r * rs) * refs[n_in - 1][...]).astype(BF16)

    def body(*refs):
        a_ref, b_ref = refs[:2]
        part = _dot(a_ref[...].astype(BF16), b_ref[...].astype(BF16), dims)
        if nk == 1:
            finish(part, refs)
            return
        acc_ref = refs[-1]
        k = pl.program_id(2)

        @pl.when(k == 0)
        def _():
            acc_ref[...] = part

        @pl.when((k > 0) & (k < nk - 1))
        def _():
            acc_ref[...] += part

        @pl.when(k == nk - 1)
        def _():
            finish(acc_ref[...] + part, refs)

    a_spec = pl.BlockSpec((tk, tm), lambda i, j, k: (k, i)) if mode == "tn" else pl.BlockSpec((tm, tk), lambda i, j, k: (i, k))
    if mode == "nt":
        b_spec = pl.BlockSpec((tn, tk), lambda i, j, k: (j + b_off, k))
    else:
        b_spec = pl.BlockSpec((tk, tn), lambda i, j, k: (k + b_off, j))
    o_spec = pl.BlockSpec((tm, tn), lambda i, j, k: (i + o_off, j))
    in_specs = [a_spec, b_spec] + ([pl.BlockSpec((tm, tn), lambda i, j, k: (i, j))] if add is not None else [])
    args = (a, b) + ((add,) if add is not None else ())
    aliases = {}
    if into is not None:
        aliases = {len(args): 0}
        in_specs.append(pl.BlockSpec(memory_space=pl.ANY))
        args += (into,)
    out_specs = [o_spec]
    out_shape = [jax.ShapeDtypeStruct((m if out_rows is None else out_rows, n), out_dtype)]
    if norm_g is not None:
        in_specs.append(pl.BlockSpec((1, n), lambda i, j, k: (0, 0)))
        args += (norm_g,)
        out_specs.append(pl.BlockSpec((tm, tn), lambda i, j, k: (i, j)))
        out_shape.append(jax.ShapeDtypeStruct((m, n), BF16))
    outs, carried = _carry_call(
        body, carry, name=name, grid=(m // tm, n // tn, nk), in_specs=in_specs, out_specs=out_specs,
        out_shape=out_shape, scratch_shapes=[pltpu.VMEM((tm, tn), F32)] if nk > 1 else [], args=args,
        own_aliases=aliases)
    out = outs[0] if norm_g is None else tuple(outs)
    return out if carry is None else (out, carried)


def _row_spec(tr, width, col_block=0):
    return pl.BlockSpec((tr, width), lambda i: (i, col_block))


def _full_spec(shape):
    return pl.BlockSpec(shape, lambda i: tuple(0 for _ in shape))


def _rmsnorm_fwd(x, g, *, name, tr=256):
    s, d = x.shape

    def body(x_ref, g_ref, o_ref):
        xv = x_ref[...]
        r = lax.rsqrt(jnp.mean(xv * xv, axis=-1, keepdims=True) + EPS)
        o_ref[...] = ((xv * r) * g_ref[...]).astype(BF16)

    return pl.pallas_call(
        body, name=name, grid=(s // tr,),
        in_specs=[_row_spec(tr, d), _full_spec((1, d))], out_specs=_row_spec(tr, d),
        out_shape=jax.ShapeDtypeStruct((s, d), BF16), compiler_params=_cparams(("parallel",)),
    )(x, g)


def _matmul_rmsnorm_bwd(a_parts, b, dres, x, g, *, mode, tk, name, extra=None, tm=512, carry=None, lead=False,
                        resident=False):
    s, d = x.shape
    n_row = s // tm
    spans, lo = [], 0
    for a in a_parts:
        spans.append((lo, lo + a.shape[1] // tk))
        lo = spans[-1][1]
    n_main, total = lo, lo + (extra is not None)
    n_parts = len(a_parts)

    def body(*refs):
        a_refs, b_ref = refs[:n_parts], refs[n_parts]
        k0 = n_parts + 1
        ax_ref, bx_ref = (refs[k0], refs[k0 + 1]) if extra is not None else (None, None)
        k0 += 2 * (extra is not None)
        dres_ref, x_ref, g_ref, dx_ref, dxb_ref, dg_ref, acc_all = refs[k0:k0 + 7]
        if resident:
            kk, i = pl.program_id(0), pl.program_id(1)
            acc_ref = acc_all.at[pl.ds(pl.multiple_of(i * tm, tm), tm)]
        else:
            i, kk = pl.program_id(0), pl.program_id(1)
            acc_ref = acc_all

        def accumulate(part, first):
            if first:
                @pl.when(kk == 0)
                def _():
                    acc_ref[...] = part

                @pl.when(kk > 0)
                def _():
                    acc_ref[...] += part
            else:
                acc_ref[...] += part

        for p, (a_ref, (lo_p, hi_p)) in enumerate(zip(a_refs, spans)):
            @pl.when((kk >= lo_p) & (kk < hi_p))
            def _(a_ref=a_ref, lo_p=lo_p):
                accumulate(_dot(a_ref[...].astype(BF16), b_ref[...].astype(BF16), NN if mode == "nn" else NT), lo_p == 0)

        if extra is not None:
            @pl.when(kk == n_main)
            def _():
                accumulate(_dot(ax_ref[...].astype(BF16), bx_ref[...].astype(BF16), NN), False)

        @pl.when(kk == total - 1)
        def _():
            dhv = acc_ref[...]
            xv = x_ref[...]
            r = lax.rsqrt(jnp.mean(xv * xv, axis=-1, keepdims=True) + EPS)
            xhat = xv * r
            dxhat = dhv * g_ref[...]
            dx = dres_ref[...] + r * (dxhat - xhat * jnp.mean(dxhat * xhat, axis=-1, keepdims=True))
            dx_ref[...] = dx
            dxb_ref[...] = dx.astype(BF16)
            dgp = jnp.sum(dhv * xhat, axis=0, keepdims=True)

            @pl.when(i == 0)
            def _():
                dg_ref[...] = dgp

            @pl.when(i > 0)
            def _():
                dg_ref[...] += dgp

    def spec(shape, index):
        return pl.BlockSpec(shape, (lambda kk, i: index(i, kk)) if resident else index)

    def row(i, kk, lo_p, hi_p):
        if not resident:
            return i
        return jnp.where(kk < lo_p, 0, jnp.where(kk >= hi_p, n_row - 1, i))

    a_specs = [spec((tm, tk), lambda i, kk, lo_p=lo_p, hi_p=hi_p: (row(i, kk, lo_p, hi_p),
                                                                    jnp.clip(kk - lo_p, 0, hi_p - lo_p - 1)))
               for lo_p, hi_p in spans]
    step = lambda kk: jnp.minimum(kk, n_main - 1)
    b_spec = (spec((tk, d), lambda i, kk: (step(kk), 0)) if mode == "nn"
              else spec((d, tk), lambda i, kk: (0, step(kk))))
    rows = spec((tm, d), lambda i, kk: (row(i, kk, total - 1, total), 0))
    one = spec((1, d), lambda i, kk: (0, 0))
    dx_spec, dx_shape = rows, jax.ShapeDtypeStruct((s, d), F32)
    if lead:
        dx_spec = spec((None, tm, d), lambda i, kk: (0, row(i, kk, total - 1, total), 0))
        dx_shape = jax.ShapeDtypeStruct((1, s, d), F32)
    x_specs, x_args = [], []
    if extra is not None:
        kx = extra[0].shape[1]
        x_specs = [spec((tm, kx), lambda i, kk: (row(i, kk, n_main, total), 0)), spec((kx, d), lambda i, kk: (0, 0))]
        x_args = list(extra)
    (dx, dxb, dg), carried = _carry_call(
        body, carry, name=name, grid=(total, n_row) if resident else (n_row, total),
        in_specs=a_specs + [b_spec] + x_specs + [rows, rows, one], out_specs=[dx_spec, rows, one],
        out_shape=[dx_shape, jax.ShapeDtypeStruct((s, d), BF16), jax.ShapeDtypeStruct((1, d), F32)],
        scratch_shapes=[pltpu.VMEM((s if resident else tm, d), F32)], args=list(a_parts) + [b] + x_args + [dres, x, g])
    return (dx, dxb, dg), carried


def _grad_w_parts(a_parts, b, *, name, tm=512, carry=None):
    s, width = a_parts[0].shape
    per, n = width // tm, b.shape[1]

    def body(*refs):
        a_refs, b_ref, o_ref = refs[:len(a_parts)], refs[len(a_parts)], refs[len(a_parts) + 1]
        i = pl.program_id(0)
        for p, a_ref in enumerate(a_refs):
            @pl.when(i // per == p)
            def _(a_ref=a_ref):
                o_ref[...] = _dot(a_ref[...].astype(BF16), b_ref[...].astype(BF16), TN).astype(BF16)

    a_specs = [pl.BlockSpec((s, tm), lambda i, p=p: (0, jnp.clip(i - p * per, 0, per - 1))) for p in range(len(a_parts))]
    (out,), carried = _carry_call(
        body, carry, name=name, grid=(len(a_parts) * per,),
        in_specs=a_specs + [pl.BlockSpec((s, n), lambda i: (0, 0))], out_specs=[pl.BlockSpec((tm, n), lambda i: (i, 0))],
        out_shape=[jax.ShapeDtypeStruct((len(a_parts) * width, n), BF16)], scratch_shapes=[], args=list(a_parts) + [b])
    return out, carried


def _ple_loss(p, wt_ple, h3, w_pg, x2, target, g, *, name, tm=256):
    s, d = x2.shape
    kp = p.shape[1]

    def body(p_ref, wp_ref, h_ref, wg_ref, x_ref, t_ref, g_ref, loss_ref, dx_ref, dple_ref, dgp_ref, dg_ref):
        i = pl.program_id(0)
        ple = _dot(p_ref[...].astype(BF16), wp_ref[...], NT)
        sg = _sigmoid(_dot(h_ref[...], wg_ref[...], NN))
        xv = x_ref[...] + ple * sg
        r = lax.rsqrt(jnp.mean(xv * xv, axis=-1, keepdims=True) + EPS)
        xhat = xv * r
        diff = xhat * g_ref[...] - t_ref[...]
        lp = jnp.zeros((1, 128), F32) + (0.5 / d) * jnp.sum(diff * diff)
        dy = diff * (1.0 / d)
        dxhat = dy * g_ref[...]
        dx = r * (dxhat - xhat * jnp.mean(dxhat * xhat, axis=-1, keepdims=True))
        dx_ref[...] = dx
        dple_ref[...] = (dx * sg).astype(BF16)
        dgp_ref[...] = (dx * ple * (sg * (1.0 - sg))).astype(BF16)
        dgp = jnp.sum(dy * xhat, axis=0, keepdims=True)

        @pl.when(i == 0)
        def _():
            dg_ref[...] = dgp
            loss_ref[...] = lp

        @pl.when(i > 0)
        def _():
            dg_ref[...] += dgp
            loss_ref[...] += lp

    rows = _row_spec(tm, d)
    return pl.pallas_call(
        body, name=name, grid=(s // tm,),
        in_specs=[_row_spec(tm, kp), _full_spec((d, kp)), rows, _full_spec((d, d)), rows, rows, _full_spec((1, d))],
        out_specs=[_full_spec((1, 128)), rows, rows, rows, _full_spec((1, d))],
        out_shape=[jax.ShapeDtypeStruct((1, 128), F32), jax.ShapeDtypeStruct((s, d), F32),
                   jax.ShapeDtypeStruct((s, d), BF16), jax.ShapeDtypeStruct((s, d), BF16),
                   jax.ShapeDtypeStruct((1, d), F32)],
        compiler_params=_cparams(("arbitrary",)),
    )(p, wt_ple, h3, w_pg, x2, target, g)


def _branches_merge(a, b, w_a, w_b, zuvg, *, name, tm=512):
    s, d = a.shape

    def body(a_ref, b_ref, wa_ref, wb_ref, ga_ref, gb_ref, ya_ref, yb_ref, o_ref):
        ya = _dot(a_ref[...], wa_ref[...], NN)
        yb = _dot(b_ref[...], wb_ref[...], NN)
        ya_ref[...] = ya
        yb_ref[...] = yb
        o_ref[...] = (_sigmoid(ga_ref[...]) * ya + _sigmoid(gb_ref[...]) * yb).astype(BF16)

    rows = _row_spec(tm, d)
    return pl.pallas_call(
        body, name=name, grid=(s // tm,),
        in_specs=[rows, rows, _full_spec((d, d)), _full_spec((d, d)), _row_spec(tm, d, 2), _row_spec(tm, d, 3)],
        out_specs=[rows, rows, rows],
        out_shape=[jax.ShapeDtypeStruct((s, d), F32), jax.ShapeDtypeStruct((s, d), F32), jax.ShapeDtypeStruct((s, d), BF16)],
        compiler_params=_cparams(("parallel",)),
    )(a, b, w_a, w_b, zuvg, zuvg)


def _merge_bwd(dx1b, w_out, ya, yb, zuvg, *, name, tm=512):
    s, d = ya.shape

    def body(dx_ref, w_ref, ya_ref, yb_ref, ga_ref, gb_ref, dya_ref, dyb_ref, dga_ref, dgb_ref):
        dmv = _dot(dx_ref[...], w_ref[...], NT)
        sa = _sigmoid(ga_ref[...])
        sb = _sigmoid(gb_ref[...])
        dya_ref[...] = (dmv * sa).astype(BF16)
        dyb_ref[...] = (dmv * sb).astype(BF16)
        dga_ref[...] = (dmv * ya_ref[...] * (sa * (1.0 - sa))).astype(BF16)
        dgb_ref[...] = (dmv * yb_ref[...] * (sb * (1.0 - sb))).astype(BF16)

    rows = _row_spec(tm, d)
    o = jax.ShapeDtypeStruct((s, d), BF16)
    return pl.pallas_call(
        body, name=name, grid=(s // tm,),
        in_specs=[rows, _full_spec((d, d)), rows, rows, _row_spec(tm, d, 2), _row_spec(tm, d, 3)],
        out_specs=[rows] * 4, out_shape=[o, o, o, o], compiler_params=_cparams(("parallel",)),
    )(dx1b, w_out, ya, yb, zuvg, zuvg)


def _masked_ws(ws_ref, g):
    row = lax.broadcasted_iota(I32, (GBLOCK, GBLOCK), 0)
    col = lax.broadcasted_iota(I32, (GBLOCK, GBLOCK), 1)
    keep = (col // CHUNK) <= (row // CHUNK)
    return jnp.where(keep, ws_ref[g], 0.0), keep


def _layernorm_parts(zv):
    mu = jnp.mean(zv, axis=-1, keepdims=True)
    xc = zv - mu
    rs = lax.rsqrt(jnp.mean(xc * xc, axis=-1, keepdims=True) + EPS)
    return xc * rs, rs


def _gmlp_fwd(zuvg, ln_g, ln_b, w_s, bs_t, *, name):
    s, w = zuvg.shape[0], GROUPS * GDIM

    def body(zu_ref, zv_ref, lng_ref, lnb_ref, ws_ref, bs_ref, a_ref):
        zu = _gelu(zu_ref[...])
        zv = _gelu(zv_ref[...])
        xhat, _ = _layernorm_parts(zv)
        vln = (xhat * lng_ref[...] + lnb_ref[...]).astype(BF16)
        for g in range(GROUPS):
            wm, _ = _masked_ws(ws_ref, g)
            mixed = _dot(wm.astype(BF16), vln[:, g * GDIM:(g + 1) * GDIM], NN) + bs_ref[:, g:g + 1]
            a_ref[:, g * GDIM:(g + 1) * GDIM] = (zu[:, g * GDIM:(g + 1) * GDIM] * mixed).astype(BF16)

    return pl.pallas_call(
        body, name=name, grid=(s // GBLOCK,),
        in_specs=[_row_spec(GBLOCK, w, 0), _row_spec(GBLOCK, w, 1), _full_spec((1, w)), _full_spec((1, w)),
                  _full_spec((GROUPS, GBLOCK, GBLOCK)), _full_spec((GBLOCK, 128))],
        out_specs=_row_spec(GBLOCK, w),
        out_shape=jax.ShapeDtypeStruct((s, w), BF16), compiler_params=_cparams(("parallel",)),
    )(zuvg, zuvg, ln_g, ln_b, w_s, bs_t)


def _gmlp_bwd(da, zuvg, ln_g, ln_b, w_s, bs_t, carry=None, *, name):
    s, w = zuvg.shape[0], GROUPS * GDIM

    def body(da_ref, zu_ref, zv_ref, lng_ref, lnb_ref, ws_ref, bs_ref,
             dzu_ref, dzv_ref, dws_ref, dbs_ref, dlng_ref, dlnb_ref, dvln_ref):
        i = pl.program_id(0)
        zu, dzu_g = _gelu_and_grad(zu_ref[...])
        zv, dzv_g = _gelu_and_grad(zv_ref[...])
        xhat, rs = _layernorm_parts(zv)
        vln = (xhat * lng_ref[...] + lnb_ref[...]).astype(BF16)
        dav = da_ref[...].astype(F32)
        lane = lax.broadcasted_iota(I32, (GBLOCK, 128), 1)
        dbs = jnp.zeros((GBLOCK, 128), F32)

        @pl.when(i == 0)
        def _():
            dws_ref[...] = jnp.zeros_like(dws_ref)

        for g in range(GROUPS):
            sl = slice(g * GDIM, (g + 1) * GDIM)
            wm, keep = _masked_ws(ws_ref, g)
            wmb = wm.astype(BF16)
            vg = vln[:, sl]
            mixed = _dot(wmb, vg, NN) + bs_ref[:, g:g + 1]
            dag = dav[:, sl]
            dzu_ref[:, sl] = (dag * mixed * dzu_g[:, sl]).astype(BF16)
            dmix = dag * zu[:, sl]
            dmb = dmix.astype(BF16)
            dws_ref[g] += jnp.where(keep, _dot(dmb, vg, NT), 0.0)
            dbs = jnp.where(lane == g, jnp.sum(dmix, axis=1, keepdims=True), dbs)
            dvln_ref[:, sl] = _dot(wmb, dmb, TN)
        dvln = dvln_ref[...]
        dxhat = dvln * lng_ref[...]
        dzv = rs * (dxhat - jnp.mean(dxhat, axis=-1, keepdims=True)
                    - xhat * jnp.mean(dxhat * xhat, axis=-1, keepdims=True))
        dzv_ref[...] = (dzv * dzv_g).astype(BF16)
        dlng = jnp.sum(dvln * xhat, axis=0, keepdims=True)
        dlnb = jnp.sum(dvln, axis=0, keepdims=True)

        @pl.when(i == 0)
        def _():
            dbs_ref[...] = dbs
            dlng_ref[...] = dlng
            dlnb_ref[...] = dlnb

        @pl.when(i > 0)
        def _():
            dbs_ref[...] += dbs
            dlng_ref[...] += dlng
            dlnb_ref[...] += dlnb

    return _carry_call(
        body, carry, name=name, grid=(s // GBLOCK,),
        in_specs=[_row_spec(GBLOCK, w), _row_spec(GBLOCK, w, 0), _row_spec(GBLOCK, w, 1), _full_spec((1, w)),
                  _full_spec((1, w)), _full_spec((GROUPS, GBLOCK, GBLOCK)), _full_spec((GBLOCK, 128))],
        out_specs=[_row_spec(GBLOCK, w), _row_spec(GBLOCK, w), _full_spec((GROUPS, GBLOCK, GBLOCK)),
                   _full_spec((GBLOCK, 128)), _full_spec((1, w)), _full_spec((1, w))],
        out_shape=[jax.ShapeDtypeStruct((s, w), BF16), jax.ShapeDtypeStruct((s, w), BF16),
                   jax.ShapeDtypeStruct((GROUPS, GBLOCK, GBLOCK), F32), jax.ShapeDtypeStruct((GBLOCK, 128), F32),
                   jax.ShapeDtypeStruct((1, w), F32), jax.ShapeDtypeStruct((1, w), F32)],
        scratch_shapes=[pltpu.VMEM((GBLOCK, w), F32)], args=[da, zuvg, zuvg, ln_g, ln_b, w_s, bs_t])


def _shift_down(u, k):
    row = lax.broadcasted_iota(I32, u.shape, 0)
    return jnp.where(row >= k, pltpu.roll(u, k, 0), 0.0)


def _shift_up(u, k):
    s = u.shape[0]
    row = lax.broadcasted_iota(I32, u.shape, 0)
    return jnp.where(row < s - k, pltpu.roll(u, s - k, 0), 0.0)


def _conv(u, w_ref, b_ref):
    return b_ref[...] + w_ref[0:1, :] * _shift_down(u, 2) + w_ref[1:2, :] * _shift_down(u, 1) + w_ref[2:3, :] * u


def _conv_specs(s, f, tc):
    nc = f // tc
    half = lambda rows: [pl.BlockSpec((rows, tc), lambda j: (0, j)), pl.BlockSpec((rows, tc), lambda j: (0, nc + j))]
    return half(s), half(3), half(1)


def _up_convglu(h2, wt_up, conv_w, conv_b, *, name, tc=256):
    s, d = h2.shape
    f = wt_up.shape[0] // 2
    nc = f // tc
    _, w_specs, b_specs = _conv_specs(s, f, tc)

    def body(h_ref, ta_ref, tg_ref, wa_ref, wg_ref, ba_ref, bg_ref, ua_ref, ug_ref, o_ref):
        ua = _dot(h_ref[...], ta_ref[...], NT)
        ua_ref[...] = ua
        ga = _gelu(_conv(ua, wa_ref, ba_ref))
        ug = _dot(h_ref[...], tg_ref[...], NT)
        ug_ref[...] = ug
        o_ref[...] = (ga * _conv(ug, wg_ref, bg_ref)).astype(BF16)

    col = pl.BlockSpec((s, tc), lambda j: (0, j))
    return pl.pallas_call(
        body, name=name, grid=(nc,),
        in_specs=[_full_spec((s, d)), pl.BlockSpec((tc, d), lambda j: (j, 0)), pl.BlockSpec((tc, d), lambda j: (nc + j, 0))]
        + w_specs + b_specs,
        out_specs=[col, col, col],
        out_shape=[jax.ShapeDtypeStruct((s, f), F32), jax.ShapeDtypeStruct((s, f), F32), jax.ShapeDtypeStruct((s, f), BF16)],
        compiler_params=_cparams(("parallel",)),
    )(h2, wt_up, wt_up, conv_w, conv_w, conv_b, conv_b)


def _convglu_bwd(dact, up_a, up_g, conv_w, conv_b, *, name, tc=256):
    s, f = up_a.shape
    _, w_specs, b_specs = _conv_specs(s, f, tc)
    up_specs = [pl.BlockSpec((s, tc), lambda j: (0, j))] * 2

    def half(dc, taps, w_ref, du_ref, dw_ref, db_ref):
        db_ref[...] = jnp.sum(dc, axis=0, keepdims=True)
        for k in range(3):
            dw_ref[k:k + 1, :] = jnp.sum(dc * taps[k], axis=0, keepdims=True)
        du = w_ref[2:3, :] * dc + w_ref[1:2, :] * _shift_up(dc, 1) + w_ref[0:1, :] * _shift_up(dc, 2)
        du_ref[...] = du.astype(BF16)

    def body(d_ref, ua_ref, ug_ref, wa_ref, wg_ref, ba_ref, bg_ref,
             dua_ref, dug_ref, dwa_ref, dwg_ref, dba_ref, dbg_ref):
        taps_a = (_shift_down(ua_ref[...], 2), _shift_down(ua_ref[...], 1), ua_ref[...])
        taps_g = (_shift_down(ug_ref[...], 2), _shift_down(ug_ref[...], 1), ug_ref[...])
        conv = lambda taps, w_ref, b_ref: b_ref[...] + w_ref[0:1, :] * taps[0] + w_ref[1:2, :] * taps[1] + w_ref[2:3, :] * taps[2]
        ca = conv(taps_a, wa_ref, ba_ref)
        cg = conv(taps_g, wg_ref, bg_ref)
        ga, dga = _gelu_and_grad(ca)
        dv = d_ref[...].astype(F32)
        half(dv * cg * dga, taps_a, wa_ref, dua_ref, dwa_ref, dba_ref)
        half(dv * ga, taps_g, wg_ref, dug_ref, dwg_ref, dbg_ref)

    col, w3, b1 = up_specs[0], w_specs[0], b_specs[0]
    return pl.pallas_call(
        body, name=name, grid=(f // tc,),
        in_specs=[col] + up_specs + w_specs + b_specs, out_specs=[col, col, w3, w3, b1, b1],
        out_shape=[jax.ShapeDtypeStruct((s, f), BF16), jax.ShapeDtypeStruct((s, f), BF16),
                   jax.ShapeDtypeStruct((3, f), F32), jax.ShapeDtypeStruct((3, f), F32),
                   jax.ShapeDtypeStruct((1, f), F32), jax.ShapeDtypeStruct((1, f), F32)],
        compiler_params=_cparams(("parallel",)),
    )(dact, up_a, up_g, conv_w, conv_w, conv_b, conv_b)


def _tri_dot(tri, x):
    b0 = x.astype(BF16)
    r1 = x - b0.astype(F32)
    b1 = r1.astype(BF16)
    b2 = (r1 - b1.astype(F32)).astype(BF16)
    return _dot(tri, b0, NN) + _dot(tri, b1, NN) + _dot(tri, b2, NN)


def _log_sigmoid(x):
    return jnp.minimum(x, 0.0) - jnp.log(1.0 + jnp.exp(-jnp.abs(x)))


def _expand_heads(col16, rows):
    src = lax.broadcasted_iota(I32, (128, HEADS * HEAD_DIM), 0)
    dst = lax.broadcasted_iota(I32, (128, HEADS * HEAD_DIM), 1) // HEAD_DIM
    spread = (src == dst).astype(BF16)
    p0, p1, p2 = _bf16_pieces(col16)
    return (_dot(p0.astype(BF16), spread, NN) + _dot(p1.astype(BF16), spread, NN)) + _dot(p2.astype(BF16), spread, NN)


def _forget_cumsum(f_logit, b_f, *, name):
    s = f_logit.shape[0]
    nb = s // 128

    def body(f_ref, b_ref, cqe_ref):
        row = lax.broadcasted_iota(I32, (128, 128), 0)
        col = lax.broadcasted_iota(I32, (128, 128), 1)
        tri = (col <= row).astype(BF16)

        def step(n, carry):
            r0 = pl.multiple_of(n * 128, 128)
            lf = _log_sigmoid(f_ref[pl.ds(r0, 128), :] + b_ref[...])
            cum = _tri_dot(tri, lf) + carry
            cqe_ref[pl.ds(r0, 128), :] = _expand_heads(cum, 128)
            return cum[127:128, :]

        lax.fori_loop(0, nb, step, jnp.zeros((1, 128), F32))

    return pl.pallas_call(
        body, name=name, grid=(1,),
        in_specs=[_full_spec((s, 128)), _full_spec((1, 128))],
        out_specs=_full_spec((s, HEADS * HEAD_DIM)),
        out_shape=jax.ShapeDtypeStruct((s, HEADS * HEAD_DIM), F32),
        compiler_params=_cparams(("arbitrary",)),
    )(f_logit, b_f)


def _forget_bwd(dcq16, sum_q16, f_logit, b_f, *, name):
    s = f_logit.shape[0]
    nb = s // 128

    def body(a_ref, k_ref, f_ref, b_ref, df_ref, db_ref):
        row = lax.broadcasted_iota(I32, (128, 128), 0)
        col = lax.broadcasted_iota(I32, (128, 128), 1)
        tri_rev = (col >= row).astype(BF16)

        def step(m, carry):
            suffix, dbsum = carry
            n = nb - 1 - m
            r0 = pl.multiple_of(n * 128, 128)
            dcum = a_ref[pl.ds(r0, 128), :] - k_ref[pl.ds(r0, 128), :]
            dlf = _tri_dot(tri_rev, dcum) + suffix
            df = dlf * _sigmoid(-(f_ref[pl.ds(r0, 128), :] + b_ref[...]))
            df_ref[pl.ds(r0, 128), :] = df.astype(BF16)
            return dlf[0:1, :], dbsum + jnp.sum(df, axis=0, keepdims=True)

        _, dbsum = lax.fori_loop(0, nb, step, (jnp.zeros((1, 128), F32), jnp.zeros((1, 128), F32)))
        db_ref[...] = dbsum

    return pl.pallas_call(
        body, name=name, grid=(1,),
        in_specs=[_full_spec((s, 128))] * 3 + [_full_spec((1, 128))],
        out_specs=[_full_spec((s, 128)), _full_spec((1, 128))],
        out_shape=[jax.ShapeDtypeStruct((s, 128), BF16), jax.ShapeDtypeStruct((1, 128), F32)],
        compiler_params=_cparams(("arbitrary",)),
    )(dcq16, sum_q16, f_logit, b_f)


ATT_T = 256


def _head_lanes(rows):
    return lax.broadcasted_iota(I32, (rows, 128), 1) < HEAD_DIM


def _bf16_pieces(c):
    p0 = c.astype(BF16).astype(F32)
    r = c - p0
    p1 = r.astype(BF16).astype(F32)
    p2 = (r - p1).astype(BF16).astype(F32)
    return p0, p1, p2


def _col_reduce(x, op):
    rows = x.shape[0]
    while rows > 8:
        rows //= 2
        x = op(x[:rows], x[rows:])
    return jnp.max(x, axis=0, keepdims=True) if op is jnp.maximum else jnp.sum(x, axis=0, keepdims=True)


def _attn_prep(qkv, cqe, carry=None, *, name):
    s = qkv.shape[0]
    npair = HEADS // 2

    def body(q_ref, k_ref, v_ref, c_ref, qa_ref, ka_ref, vt_ref):
        rows = 128
        lane = lax.broadcasted_iota(I32, (rows, 128), 1)

        def chunk(n, _):
            r0 = pl.multiple_of(n * rows, rows)
            sl = pl.ds(r0, rows)
            qv = q_ref[sl, :].astype(F32) * ATT_SCALE
            kv = k_ref[sl, :].astype(F32)
            p0, p1, p2 = _bf16_pieces(pltpu.roll(c_ref[sl, :], HEAD_DIM, 1))
            for e in range(2):
                mine = (lane < HEAD_DIM) if e == 0 else (lane >= HEAD_DIM)
                base = HEAD_DIM * (1 - e)
                ones_hi = jnp.where((lane >= base + 3) & (lane < base + 6), 1.0, 0.0)
                ones_lo = jnp.where((lane >= base) & (lane < base + 3), 1.0, 0.0)
                qa = jnp.where(mine, qv, jnp.where(lane == base, p0, jnp.where(lane == base + 1, p1,
                               jnp.where(lane == base + 2, p2, ones_hi))))
                ka = jnp.where(mine, kv, jnp.where(lane == base + 3, -p0, jnp.where(lane == base + 4, -p1,
                               jnp.where(lane == base + 5, -p2, ones_lo))))
                qa_ref[e, sl, :] = qa.astype(BF16)
                ka_ref[e, sl, :] = ka.astype(BF16)
            vt_ref[0, :, sl] = v_ref[sl, :].astype(F32).T.astype(BF16)
            return 0

        lax.fori_loop(0, s // rows, chunk, 0)

    pair = pl.BlockSpec((2, s, 128), lambda hp: (hp, 0, 0))
    return _carry_call(
        body, carry, name=name, grid=(npair,),
        in_specs=[pl.BlockSpec((s, 128), lambda hp: (0, hp)), pl.BlockSpec((s, 128), lambda hp: (0, npair + hp)),
                  pl.BlockSpec((s, 128), lambda hp: (0, 2 * npair + hp)), pl.BlockSpec((s, 128), lambda hp: (0, hp))],
        out_specs=[pair, pair, pl.BlockSpec((1, 128, s), lambda hp: (hp, 0, 0))],
        out_shape=[jax.ShapeDtypeStruct((HEADS, s, 128), BF16), jax.ShapeDtypeStruct((HEADS, s, 128), BF16),
                   jax.ShapeDtypeStruct((npair, 128, s), BF16)],
        scratch_shapes=[], args=[qkv, qkv, qkv, cqe])


def _attn_fwd(qa, ka, vt, carry=None, *, name):
    s = qa.shape[1]
    t = 2 * ATT_T
    nq = s // t
    npair = HEADS // 2

    def body(qa_ref, ka_ref, vt_ref, o_ref, lse_ref):
        i = pl.program_id(1)
        krow = lax.broadcasted_iota(I32, (t, t), 0)
        qcol = lax.broadcasted_iota(I32, (t, t), 1)
        sub = lax.broadcasted_iota(I32, (128, t), 0)
        row8 = lax.broadcasted_iota(I32, (8, t), 0)
        qbs = (qa_ref[0], qa_ref[1])
        tk = t

        def step(j, carry, diag):
            c0 = pl.multiple_of(j * tk, tk)
            vtb = vt_ref[0, :, pl.ds(c0, tk)]
            sts = [_dot(ka_ref[e, pl.ds(c0, tk), :], qbs[e], NT) for e in range(2)]
            if diag:
                sts = [jnp.where(krow <= qcol, st, NEG) for st in sts]
            pts, stats = [], []
            for e in range(2):
                m, l, _ = carry[e]
                m_new = jnp.maximum(m, _col_reduce(sts[e], jnp.maximum))
                alpha = jnp.exp(m - m_new)
                pt = jnp.exp(sts[e] - m_new)
                stats.append((m_new, alpha, alpha * l + _col_reduce(pt, jnp.add)))
                pts.append(pt.astype(BF16))
            pvs = [_dot(vtb, pts[e], NN) for e in range(2)]
            return tuple((stats[e][0], stats[e][2], stats[e][1] * carry[e][2] + pvs[e]) for e in range(2))

        init = (jnp.full((1, t), NEG, F32), jnp.zeros((1, t), F32), jnp.zeros((128, t), F32))
        carry = lax.fori_loop(0, i, functools.partial(step, diag=False), (init, init))
        (m0, l0, acc0), (m1, l1, acc1) = step(i, carry, True)
        o_pair = jnp.where(sub < HEAD_DIM, acc0 / l0, acc1 / l1)
        o_ref[...] = o_pair.T.astype(BF16)
        lse_ref[0] = jnp.where(row8 == 0, m0 + jnp.log(l0), jnp.where(row8 == 1, m1 + jnp.log(l1), 0.0))

    return _carry_call(
        body, carry, name=name, grid=(npair, nq),
        in_specs=[pl.BlockSpec((2, t, 128), lambda hp, i: (hp, i, 0)), pl.BlockSpec((2, s, 128), lambda hp, i: (hp, 0, 0)),
                  pl.BlockSpec((1, 128, s), lambda hp, i: (hp, 0, 0))],
        out_specs=[pl.BlockSpec((t, 128), lambda hp, i: (i, hp)), pl.BlockSpec((1, 8, t), lambda hp, i: (hp, 0, i))],
        out_shape=[jax.ShapeDtypeStruct((s, HEADS * HEAD_DIM), BF16), jax.ShapeDtypeStruct((npair, 8, s), F32)],
        scratch_shapes=[], args=[qa, ka, vt])


def _attn_delta(do, o, carry=None, *, name):
    s = do.shape[0]

    def body(do_ref, o_ref, d_ref):
        prod = do_ref[...].astype(F32) * o_ref[...].astype(F32)
        row = lax.broadcasted_iota(I32, (8, 128), 0)
        lane = lax.broadcasted_iota(I32, (8, 128), 1)
        sel = ((row == 0) & (lane < HEAD_DIM) | (row == 1) & (lane >= HEAD_DIM)).astype(BF16)
        p0, p1, p2 = _bf16_pieces(prod)
        d_ref[0] = (_dot(sel, p0.astype(BF16), NT) + _dot(sel, p1.astype(BF16), NT)) + _dot(sel, p2.astype(BF16), NT)

    pair = pl.BlockSpec((s, 128), lambda hp: (0, hp))
    (delta3,), carried = _carry_call(
        body, carry, name=name, grid=(HEADS // 2,), in_specs=[pair, pair],
        out_specs=[pl.BlockSpec((1, 8, s), lambda hp: (hp, 0, 0))],
        out_shape=[jax.ShapeDtypeStruct((HEADS // 2, 8, s), F32)], scratch_shapes=[], args=[do, o])
    return delta3, carried


def _attn_bwd(qa, ka, qkv, do, lse3, delta3, carry=None, *, name):
    s = qa.shape[1]
    t = 2 * ATT_T
    nb = s // t
    npair = HEADS // 2

    def body(qa_ref, ka_ref, v_ref, do_ref, lse_ref, delta_ref, dq_ref, dk_ref, dv_ref, aux_ref, dcq_ref, dqt):
        hp = pl.program_id(0)
        first = _head_lanes(t)
        lane = lax.broadcasted_iota(I32, (t, 128), 1)
        dqt[...] = jnp.zeros_like(dqt)

        @pl.when(hp == 0)
        def _():
            aux_ref[...] = jnp.zeros_like(aux_ref)

        krow = lax.broadcasted_iota(I32, (t, t), 0)
        qcol = lax.broadcasted_iota(I32, (t, t), 1)

        def key_block(j, _):
            c0 = pl.multiple_of(j * t, t)
            vb = v_ref[pl.ds(c0, t), :]
            kbs = (ka_ref[0, pl.ds(c0, t), :], ka_ref[1, pl.ds(c0, t), :])
            kbts = tuple(kb.astype(F32).T.astype(BF16) for kb in kbs)
            vhs = (jnp.where(first, vb, jnp.zeros_like(vb)), jnp.where(first, jnp.zeros_like(vb), vb))

            def query_block(i, carry, diag):
                r0 = pl.multiple_of(i * t, t)
                dob = do_ref[pl.ds(r0, t), :]
                sts = [_dot(kbs[e], qa_ref[e, pl.ds(r0, t), :], NT) for e in range(2)]
                dpts = [_dot(vhs[e], dob, NT) for e in range(2)]
                ptbs, dsbs = [], []
                for e in range(2):
                    st = jnp.where(krow <= qcol, sts[e], NEG) if diag else sts[e]
                    pt = jnp.exp(st - lse_ref[0, e:e + 1, pl.ds(r0, t)])
                    dsbs.append((pt * (dpts[e] - delta_ref[0, e:e + 1, pl.ds(r0, t)])).astype(BF16))
                    ptbs.append(pt.astype(BF16))
                out = []
                for e in range(2):
                    dk_a, dv_a = carry[e]
                    dv_a = dv_a + _dot(ptbs[e], dob, NN)
                    dk_a = dk_a + _dot(dsbs[e], qa_ref[e, pl.ds(r0, t), :], NN)
                    dqt[e, :, pl.ds(r0, t)] += _dot(kbts[e], dsbs[e], NN)
                    out.append((dk_a, dv_a))
                return tuple(out)

            zero = jnp.zeros((t, 128), F32)
            carry = query_block(j, ((zero, zero), (zero, zero)), True)
            (dk0, dv0), (dk1, dv1) = lax.fori_loop(j + 1, nb, functools.partial(query_block, diag=False), carry)
            dk_ref[pl.ds(c0, t), :] = jnp.where(first, dk0, dk1).astype(BF16)
            dv_ref[pl.ds(c0, t), :] = jnp.where(first, dv0, dv1).astype(BF16)
            sum_q = jnp.where(lane == 2 * hp, dk0[:, HEAD_DIM + 3:HEAD_DIM + 4],
                              jnp.where(lane == 2 * hp + 1, dk1[:, 3:4], aux_ref[pl.ds(c0, t), :]))
            aux_ref[pl.ds(c0, t), :] = sum_q
            return 0

        lax.fori_loop(0, nb, key_block, 0)
        sub = lax.broadcasted_iota(I32, (128, s), 0)
        row8 = lax.broadcasted_iota(I32, (8, s), 0)
        dq_ref[...] = (jnp.where(sub < HEAD_DIM, dqt[0], dqt[1]) * ATT_SCALE).T.astype(BF16)
        dcq_ref[0] = jnp.where(row8 == 0, dqt[0, HEAD_DIM:HEAD_DIM + 1, :], jnp.where(row8 == 1, dqt[1, 0:1, :], 0.0))

    def pair_cols(off):
        return pl.BlockSpec((s, 128), lambda hp: (0, off + hp))

    heads = pl.BlockSpec((2, s, 128), lambda hp: (hp, 0, 0))
    rows = pl.BlockSpec((1, 8, s), lambda hp: (hp, 0, 0))
    wide = jax.ShapeDtypeStruct((s, HEADS * HEAD_DIM), BF16)
    return _carry_call(
        body, carry, name=name, grid=(npair,),
        in_specs=[heads, heads, pair_cols(2 * npair), pair_cols(0), rows, rows],
        out_specs=[pair_cols(0), pair_cols(0), pair_cols(0), pl.BlockSpec((s, 128), lambda hp: (0, 0)), rows],
        out_shape=[wide, wide, wide, jax.ShapeDtypeStruct((s, 128), F32), jax.ShapeDtypeStruct((npair, 8, s), F32)],
        scratch_shapes=[pltpu.VMEM((2, 128, s), F32)], args=[qa, ka, qkv, do, lse3, delta3])


def _adam_math(w, g, m, v):
    m = ADAM_B1 * m + (1.0 - ADAM_B1) * g
    v = ADAM_B2 * v + (1.0 - ADAM_B2) * (g * g)
    m_hat = m / (1.0 - ADAM_B1 ** ADAM_STEP)
    v_hat = v / (1.0 - ADAM_B2 ** ADAM_STEP)
    delta = -ADAM_LR * (m_hat / (jnp.sqrt(v_hat) + ADAM_EPS) + ADAM_WD * w)
    return delta, m, v


def _sum_pairs(keep, recv, pos, *, name):
    _, r, c = recv.shape
    tr = _row_tile(r, 512)

    def body(pos_ref, a_ref, b_ref, o32_ref, o16_ref):
        tot = a_ref[...].astype(F32) + b_ref[...].astype(F32)
        o16_ref[...] = tot.astype(BF16)

        @pl.when(pl.program_id(1) == 2 * pos_ref[0] + pos_ref[1])
        def _():
            o32_ref[...] = tot

    out = pl.BlockSpec((1, tr, c), lambda i, q, pos: (q, i, 0))
    grid_spec = pltpu.PrefetchScalarGridSpec(
        num_scalar_prefetch=1, grid=(r // tr, 4),
        in_specs=[pl.BlockSpec((1, tr, c), lambda i, q, pos: (2 * q + pos[2], i, 0)), out],
        out_specs=[pl.BlockSpec((1, tr, c), lambda i, q, pos: (0, i, 0)), out])
    return pl.pallas_call(
        body, name=name, grid_spec=grid_spec,
        out_shape=[jax.ShapeDtypeStruct((1, r, c), F32), jax.ShapeDtypeStruct((4, r, c), BF16)],
        compiler_params=_cparams(("arbitrary", "arbitrary")),
    )(pos, keep, recv)


def _adam_sharded(psum, recv, w, m, v, pos, *, name):
    r, c = w.shape
    tr = _row_tile(r, 320)

    def body(pos_ref, p_ref, r_ref, w_ref, m_ref, v_ref, g_ref, d_ref, mo_ref, vo_ref):
        g = p_ref[0] + r_ref[0].astype(F32) + r_ref[1].astype(F32) + r_ref[2].astype(F32)
        delta, mn, vn = _adam_math(w_ref[...], g, m_ref[...], v_ref[...])
        g_ref[...] = g
        d_ref[...] = delta
        mo_ref[...] = mn
        vo_ref[...] = vn

    row = pl.BlockSpec((tr, c), lambda i, pos: (i, 0))
    grid_spec = pltpu.PrefetchScalarGridSpec(
        num_scalar_prefetch=1, grid=(r // tr,),
        in_specs=[pl.BlockSpec((1, tr, c), lambda i, pos: (0, i, 0)),
                  pl.BlockSpec((3, tr, c), lambda i, pos: (0, i, 0)), row, row, row],
        out_specs=[row, row, row, row])
    o = jax.ShapeDtypeStruct((r, c), F32)
    return pl.pallas_call(
        body, name=name, grid_spec=grid_spec, out_shape=[o, o, o, o],
        compiler_params=_cparams(("parallel",)),
    )(pos, psum, recv, w, m, v)


def _adam_replicated(chip_sums, last, w, m, v, *, name):
    r = w.shape[0]

    def body(s_ref, l_ref, w_ref, m_ref, v_ref, g_ref, d_ref, mo_ref, vo_ref):
        g = (((s_ref[0] + s_ref[1]) + s_ref[2]) + s_ref[3]) + l_ref[...]
        delta, mn, vn = _adam_math(w_ref[...], g, m_ref[...], v_ref[...])
        g_ref[...] = g
        d_ref[...] = delta
        mo_ref[...] = mn
        vo_ref[...] = vn

    o = jax.ShapeDtypeStruct((r, 1024), F32)
    full = _full_spec((r, 1024))
    return pl.pallas_call(
        body, name=name, grid=(1,),
        in_specs=[_full_spec((4, r, 1024)), full, full, full, full], out_specs=[full] * 4, out_shape=[o] * 4,
        compiler_params=_cparams(("arbitrary",)),
    )(chip_sums, last, w, m, v)


ASM_OUT = 256
ASM_SRC = 304


def _w_in_row(r):
    return r if r < 2048 else (r + O_G - 2048 if r < 4096 else r - 2048)


def _assemble_wt_main(g, *, name):
    table = []
    for blk in range(MAIN_COLS // ASM_OUT):
        j, l0 = divmod(_w_in_row(blk * ASM_OUT), IN_SHARD)
        sb = l0 // ASM_SRC
        n_a = min(ASM_OUT, min(IN_SHARD, (sb + 1) * ASM_SRC) - l0)
        if n_a == ASM_OUT:
            nxt = (j, sb)
        elif l0 + n_a == IN_SHARD:
            nxt = (j + 1, 0)
        else:
            nxt = (j, sb + 1)
        table.append((j, sb, l0 - sb * ASM_SRC, n_a) + nxt)

    def body(tab_ref, a_ref, b_ref, o_ref):
        blk = pl.program_id(0)
        off, n_a = tab_ref[blk, 2], tab_ref[blk, 3]
        r = lax.broadcasted_iota(I32, (ASM_OUT, ASM_SRC), 0)
        k = lax.broadcasted_iota(I32, (ASM_OUT, ASM_SRC), 1)
        sel_a = ((k == r + off) & (r < n_a)).astype(BF16)
        sel_b = ((k == r - n_a) & (r >= n_a)).astype(BF16)
        o_ref[...] = (_dot(sel_a, a_ref[0], NN) + _dot(sel_b, b_ref[0], NN)).astype(BF16)

    src = lambda c: pl.BlockSpec((1, ASM_SRC, D_MODEL), lambda blk, tab: (tab[blk, c], tab[blk, c + 1], 0))
    grid_spec = pltpu.PrefetchScalarGridSpec(
        num_scalar_prefetch=1, grid=(len(table),), in_specs=[src(0), src(4)],
        out_specs=pl.BlockSpec((ASM_OUT, D_MODEL), lambda blk, tab: (blk, 0)))
    return pl.pallas_call(
        body, name=name, grid_spec=grid_spec, out_shape=jax.ShapeDtypeStruct((MAIN_COLS, D_MODEL), BF16),
        compiler_params=_cparams(("parallel",)),
    )(jnp.asarray(table, I32), g, g)


def _pair_sum_small(mine, theirs, *, name):
    def body(a_ref, b_ref, o_ref):
        o_ref[...] = a_ref[...] + b_ref[...]

    full = _full_spec(mine.shape)
    return pl.pallas_call(
        body, name=name, grid=(1,), in_specs=[full, full], out_specs=full,
        out_shape=jax.ShapeDtypeStruct(mine.shape, F32), compiler_params=_cparams(("arbitrary",)),
    )(mine, theirs)


ANY = pl.BlockSpec(memory_space=pl.ANY)
OTHER_CHIPS = ((1, 0), (0, 1), (1, 1))


class _Carry:
    def __init__(self, inputs, out_shapes, scratch, start, wait, aliases=None):
        self.inputs, self.out_shapes, self.scratch = list(inputs), list(out_shapes), list(scratch)
        self.start, self.wait, self.aliases = start, wait, dict(aliases or {})


def _carry_join(*carries):
    n_in = [len(c.inputs) for c in carries]
    n_out = [len(c.out_shapes) for c in carries]
    n_scr = [len(c.scratch) for c in carries]

    def split(refs, counts):
        out, k = [], 0
        for n in counts:
            out.append(refs[k:k + n])
            k += n
        return out

    def start(ins, outs, scr):
        for c, i, o, s in zip(carries, split(ins, n_in), split(outs, n_out), split(scr, n_scr)):
            c.start(i, o, s)

    def wait(ins, outs, scr):
        for c, i, o, s in zip(carries, split(ins, n_in), split(outs, n_out), split(scr, n_scr)):
            c.wait(i, o, s)

    aliases = {}
    for k, c in enumerate(carries):
        aliases.update({sum(n_in[:k]) + i: sum(n_out[:k]) + o for i, o in c.aliases.items()})
    joined = _Carry(sum((c.inputs for c in carries), []), sum((c.out_shapes for c in carries), []),
                    sum((c.scratch for c in carries), []), start, wait, aliases)
    joined.counts = n_out
    joined.split = lambda results: split(results, n_out)
    return joined


def _carried(body, carry, n_in, n_out, grid):
    if carry is None:
        return body
    ci, co, cs = len(carry.inputs), len(carry.out_shapes), len(carry.scratch)

    def wrapped(*refs):
        ins, cins = refs[:n_in], refs[n_in:n_in + ci]
        outs, couts = refs[n_in + ci:n_in + ci + n_out], refs[n_in + ci + n_out:n_in + ci + n_out + co]
        rest = refs[n_in + ci + n_out + co:]
        scratch, cscr = rest[:len(rest) - cs], rest[len(rest) - cs:]
        first, last = None, None
        for axis, size in enumerate(grid):
            f, l = pl.program_id(axis) == 0, pl.program_id(axis) == size - 1
            first = f if first is None else first & f
            last = l if last is None else last & l

        @pl.when(first)
        def _():
            carry.start(cins, couts, cscr)

        body(*ins, *outs, *scratch)

        @pl.when(last)
        def _():
            carry.wait(cins, couts, cscr)

    return wrapped


def _carry_call(body, carry, *, name, grid, in_specs, out_specs, out_shape, scratch_shapes, args, vmem=True,
                own_aliases=None):
    n_in, n_out = len(in_specs), len(out_specs)
    extra_in = [ANY] * len(carry.inputs) if carry else []
    extra_out = [ANY] * len(carry.out_shapes) if carry else []
    aliases = dict(own_aliases or {})
    if carry:
        aliases.update({n_in + i: n_out + o for i, o in carry.aliases.items()})
    out = pl.pallas_call(
        _carried(body, carry, n_in, n_out, grid), name=name, grid=grid,
        in_specs=list(in_specs) + extra_in, out_specs=list(out_specs) + extra_out,
        out_shape=list(out_shape) + (carry.out_shapes if carry else []),
        scratch_shapes=list(scratch_shapes) + (carry.scratch if carry else []),
        input_output_aliases=aliases,
        compiler_params=_cparams(("arbitrary",) * len(grid)) if vmem else None,
    )(*args, *(carry.inputs if carry else []))
    return list(out[:n_out]), list(out[n_out:])


def _run_carry(carry, *, name):
    return _carry_call(lambda: None, carry, name=name, grid=(1,), in_specs=[], out_specs=[], out_shape=[],
                       scratch_shapes=[], args=[], vmem=False)[1]


def _sems(n):
    return [pltpu.SemaphoreType.DMA((n,)), pltpu.SemaphoreType.DMA((n,))]


def _carry_gather1(shards):
    n = len(shards)

    def copies(x_refs, out_refs, scr, with_arrivals):
        send_sems, recv_sems, local_sems = scr
        x, y, c = lax.axis_index("x"), lax.axis_index("y"), lax.axis_index("c")
        peers = [(x, y, 1 - c)] + [(x ^ fx, y ^ fy, c) for fx, fy in OTHER_CHIPS]
        local, sends, arrivals = [], [], []
        for t, (x_ref, out_ref) in enumerate(zip(x_refs, out_refs)):
            local.append(pltpu.make_async_copy(x_ref, out_ref.at[4 * x + 2 * y + c], local_sems.at[t]))
            for k, (px, py, pc) in enumerate(peers):
                sems = dict(send_sem=send_sems.at[4 * t + k], recv_sem=recv_sems.at[4 * t + k],
                            device_id=(px, py, pc), device_id_type=MESH)
                sends.append(pltpu.make_async_remote_copy(src_ref=x_ref, dst_ref=out_ref.at[4 * x + 2 * y + c], **sems))
                if with_arrivals:
                    arrivals.append(
                        pltpu.make_async_remote_copy(src_ref=x_ref, dst_ref=out_ref.at[4 * px + 2 * py + pc], **sems))
        return local, sends, arrivals

    def start(x_refs, out_refs, scr):
        local, sends, _ = copies(x_refs, out_refs, scr, False)
        for cp in local + sends:
            cp.start()

    def wait(x_refs, out_refs, scr):
        local, sends, arrivals = copies(x_refs, out_refs, scr, True)
        for cp in arrivals:
            cp.wait_recv()
        for cp in sends:
            cp.wait_send()
        for cp in local:
            cp.wait()

    return _Carry(shards, [jax.ShapeDtypeStruct((N_DEV,) + a.shape, a.dtype) for a in shards],
                  _sems(4 * n) + [pltpu.SemaphoreType.DMA((n,))], start, wait)


def _carry_gather2(gathered):
    n = len(gathered)

    def copies(in_refs, g_refs, scr, with_arrivals):
        send_sems, recv_sems = scr
        x, y, c = lax.axis_index("x"), lax.axis_index("y"), lax.axis_index("c")
        sends, arrivals = [], []
        for t in range(n):
            for j, (fx, fy) in enumerate(OTHER_CHIPS):
                px, py = x ^ fx, y ^ fy
                sems = dict(send_sem=send_sems.at[3 * t + j], recv_sem=recv_sems.at[3 * t + j],
                            device_id=(x, y, 1 - c), device_id_type=MESH)
                mine, theirs = 4 * px + 2 * py + c, 4 * px + 2 * py + (1 - c)
                sends.append(pltpu.make_async_remote_copy(src_ref=in_refs[t].at[mine], dst_ref=g_refs[t].at[mine], **sems))
                if with_arrivals:
                    arrivals.append(pltpu.make_async_remote_copy(
                        src_ref=in_refs[t].at[mine], dst_ref=g_refs[t].at[theirs], **sems))
        return sends, arrivals

    def start(in_refs, g_refs, scr):
        for cp in copies(in_refs, g_refs, scr, False)[0]:
            cp.start()

    def wait(in_refs, g_refs, scr):
        sends, arrivals = copies(in_refs, g_refs, scr, True)
        for cp in arrivals:
            cp.wait_recv()
        for cp in sends:
            cp.wait_send()

    return _Carry(gathered, [jax.ShapeDtypeStruct(a.shape, a.dtype) for a in gathered], _sems(3 * n), start, wait,
                  aliases={t: t for t in range(n)})


def _allreduce_rows(x, *, name):
    def body(x_ref, o_ref, sib_ref, mine_ref, tab_ref, send_sems, recv_sems):
        x, y, c = lax.axis_index("x"), lax.axis_index("y"), lax.axis_index("c")
        swap = pltpu.make_async_remote_copy(src_ref=x_ref, dst_ref=sib_ref, send_sem=send_sems.at[0],
                                            recv_sem=recv_sems.at[0], device_id=(x, y, 1 - c), device_id_type=MESH)
        swap.start()
        swap.wait()
        mine_ref[...] = x_ref[...] + sib_ref[...]
        tab_ref[pl.ds(2 * x + y, 1)] = mine_ref[...][None]

        def copy(k, slot):
            fx, fy = OTHER_CHIPS[k]
            return pltpu.make_async_remote_copy(
                src_ref=mine_ref, dst_ref=tab_ref.at[slot], send_sem=send_sems.at[1 + k], recv_sem=recv_sems.at[1 + k],
                device_id=(x ^ fx, y ^ fy, c), device_id_type=MESH)

        for k in range(3):
            copy(k, 2 * x + y).start()
        for k, (fx, fy) in enumerate(OTHER_CHIPS):
            copy(k, 2 * (x ^ fx) + (y ^ fy)).wait()
        o_ref[...] = ((tab_ref[0] + tab_ref[1]) + tab_ref[2]) + tab_ref[3]

    vmem = pl.BlockSpec(memory_space=pltpu.VMEM)
    return pl.pallas_call(
        body, name=name, out_shape=jax.ShapeDtypeStruct(x.shape, F32), in_specs=[vmem], out_specs=vmem,
        scratch_shapes=[pltpu.VMEM(x.shape, F32), pltpu.VMEM(x.shape, F32), pltpu.VMEM((4,) + x.shape, F32)] + _sems(4),
    )(x)


def _allgather(shards, *, name):
    n = len(shards)

    def body(*refs):
        x_refs, out_refs = refs[:n], refs[n:2 * n]
        send_sems, recv_sems, local_sems = refs[2 * n:]
        x, y, c = lax.axis_index("x"), lax.axis_index("y"), lax.axis_index("c")
        me, sibling = (x, y, c), (x, y, 1 - c)
        chips = [(x ^ fx, y ^ fy) for fx, fy in OTHER_CHIPS]

        def copy(t, k, block, to, from_input=False):
            px, py, pc = block
            slab = out_refs[t].at[4 * px + 2 * py + pc]
            return pltpu.make_async_remote_copy(
                src_ref=x_refs[t] if from_input else slab, dst_ref=slab,
                send_sem=send_sems.at[7 * t + k], recv_sem=recv_sems.at[7 * t + k], device_id=to, device_id_type=MESH)

        mine = [pltpu.make_async_copy(x_refs[t], out_refs[t].at[4 * x + 2 * y + c], local_sems.at[t]) for t in range(n)]
        for cp in mine:
            cp.start()
        first = []
        for t in range(n):
            first.append(copy(t, 0, me, sibling, from_input=True))
            first += [copy(t, 1 + j, me, (*chip, c), from_input=True) for j, chip in enumerate(chips)]
        for cp in first:
            cp.start()
        passed = []
        for j, chip in enumerate(chips):
            for t in range(n):
                copy(t, 1 + j, (*chip, c), me).wait_recv()
                fwd = copy(t, 4 + j, (*chip, c), sibling)
                fwd.start()
                passed.append(fwd)
        for t in range(n):
            copy(t, 0, sibling, me).wait_recv()
            for j, chip in enumerate(chips):
                copy(t, 4 + j, (*chip, 1 - c), me).wait_recv()
        for cp in first + passed:
            cp.wait_send()
        for cp in mine:
            cp.wait()

    return pl.pallas_call(
        body, name=name, out_shape=[jax.ShapeDtypeStruct((N_DEV,) + a.shape, a.dtype) for a in shards],
        in_specs=[ANY] * n, out_specs=[ANY] * n,
        scratch_shapes=[pltpu.SemaphoreType.DMA((7 * n,)), pltpu.SemaphoreType.DMA((7 * n,)),
                        pltpu.SemaphoreType.DMA((n,))],
    )(*shards)


def _carry_sibling(slabs, small=None):
    n = len(slabs)
    extra = [] if small is None else [small]

    def copies(in_refs, out_refs, scr):
        send_sems, recv_sems = scr
        x, y, c = lax.axis_index("x"), lax.axis_index("y"), lax.axis_index("c")
        sibling = (x, y, 1 - c)
        out = []
        for t in range(n):
            for q in range(4):
                out.append(pltpu.make_async_remote_copy(
                    src_ref=in_refs[t].at[2 * q + (1 - c)], dst_ref=out_refs[t].at[q],
                    send_sem=send_sems.at[4 * t + q], recv_sem=recv_sems.at[4 * t + q],
                    device_id=sibling, device_id_type=MESH))
        if extra:
            out.append(pltpu.make_async_remote_copy(
                src_ref=in_refs[n], dst_ref=out_refs[n], send_sem=send_sems.at[4 * n], recv_sem=recv_sems.at[4 * n],
                device_id=sibling, device_id_type=MESH))
        return out

    def start(*refs):
        for cp in copies(*refs):
            cp.start()

    def wait(*refs):
        for cp in copies(*refs):
            cp.wait()

    return _Carry(list(slabs) + extra,
                  [jax.ShapeDtypeStruct((4,) + a.shape[1:], a.dtype) for a in slabs]
                  + [jax.ShapeDtypeStruct(a.shape, a.dtype) for a in extra], _sems(4 * n + 1), start, wait)


def _carry_chips(psums, small_sum=None):
    n = len(psums)
    table = small_sum is not None

    def copies(in_refs, out_refs, scr, arrivals):
        send_sems, recv_sems = scr[0], scr[1]
        x, y, c = lax.axis_index("x"), lax.axis_index("y"), lax.axis_index("c")
        out = []
        for k, (fx, fy) in enumerate(OTHER_CHIPS):
            px, py = x ^ fx, y ^ fy
            for t in range(n):
                out.append(pltpu.make_async_remote_copy(
                    src_ref=in_refs[t].at[2 * px + py], dst_ref=out_refs[t].at[k],
                    send_sem=send_sems.at[3 * t + k], recv_sem=recv_sems.at[3 * t + k],
                    device_id=(px, py, c), device_id_type=MESH))
            if table:
                slot = 2 * px + py if arrivals else 2 * x + y
                out.append(pltpu.make_async_remote_copy(
                    src_ref=in_refs[n], dst_ref=out_refs[n].at[slot], send_sem=send_sems.at[3 * n + k],
                    recv_sem=recv_sems.at[3 * n + k], device_id=(px, py, c), device_id_type=MESH))
        return out

    def own(in_refs, out_refs, scr):
        x, y = lax.axis_index("x"), lax.axis_index("y")
        return pltpu.make_async_copy(in_refs[n], out_refs[n].at[2 * x + y], scr[2])

    def start(in_refs, out_refs, scr):
        if table:
            own(in_refs, out_refs, scr).start()
        for cp in copies(in_refs, out_refs, scr, False):
            cp.start()

    def wait(in_refs, out_refs, scr):
        for cp in copies(in_refs, out_refs, scr, True):
            cp.wait()
        if table:
            own(in_refs, out_refs, scr).wait()

    out_shapes = [jax.ShapeDtypeStruct((3,) + a.shape[1:], a.dtype) for a in psums]
    if table:
        out_shapes.append(jax.ShapeDtypeStruct((4,) + small_sum.shape, F32))
    return _Carry(list(psums) + ([small_sum] if table else []), out_shapes,
                  _sems(3 * n + 3) + ([pltpu.SemaphoreType.DMA] if table else []), start, wait)


def _to_comm(name, kind, block, dtype=BF16):
    a = block[0]
    if kind == "cols":
        a = a.T
        if name == "w_in":
            a = jnp.pad(a, ((0, IN_SHARD_PAD - IN_SHARD), (0, 0)))
    return a if kind == "f32" else a.astype(dtype)


def _from_comm(name, kind, a):
    if kind == "cols":
        if name == "w_in":
            a = a[:IN_SHARD]
        a = a.T
    return a[None]


def _assemble_weights(g):
    out = {}
    if "w_in" in g:
        out["wt_main"] = _assemble_wt_main(g["w_in"], name="assemble_w_in")
        j, l0 = divmod(O_F, IN_SHARD)
        out["wt_f"] = jnp.pad(g["w_in"][j, l0:l0 + HEADS], ((0, 128 - HEADS), (0, 0)))
    square = dict(w_branch_a="w_a", w_branch_b="w_b", w_out="w_out", w_ple_gate="w_pg")
    for long, short in square.items():
        if long in g:
            out[short] = g[long].reshape(D_MODEL, D_MODEL)
    if "w_up" in g:
        out["wt_up"] = g["w_up"].reshape(2 * D_FF, D_MODEL)
    if "conv_w" in g:
        out["conv_w"] = g["conv_w"].transpose(1, 0, 2).reshape(3, 2 * D_FF)
    if "w_down" in g:
        out["w_down"] = g["w_down"].reshape(D_FF, D_MODEL)
    if "w_ple" in g:
        out["wt_ple"] = g["w_ple"].reshape(D_MODEL, PLE_DIM)
    return out


def _grad_slabs(gr):
    out = {}
    if "wt_main" in gr:
        gm, gf = gr["wt_main"], gr["wt_f"]
        segments = ((0, 2048, gm, 0), (2048, O_F, gm, 2048), (O_F, O_G, gf, -O_F), (O_G, IN_COLS, gm, 2048 - O_G))
        slabs = []
        for j in range(N_DEV):
            lo, hi = j * IN_SHARD, (j + 1) * IN_SHARD
            pieces = [src[max(lo, a) + shift:min(hi, b) + shift] for a, b, src, shift in segments if max(lo, a) < min(hi, b)]
            pieces.append(jnp.zeros((IN_SHARD_PAD - IN_SHARD, D_MODEL), gm.dtype))
            slabs.append(jnp.concatenate(pieces, axis=0))
        out["w_in"] = jnp.stack(slabs)
    rows = dict(w_a="w_branch_a", w_b="w_branch_b", w_out="w_out", wt_up="w_up", w_down="w_down", w_pg="w_ple_gate")
    for short, long in rows.items():
        if short in gr:
            out[long] = gr[short].reshape(N_DEV, -1, D_MODEL)
    if "conv_w" in gr:
        out["conv_w"] = gr["conv_w"].reshape(3, N_DEV, -1).transpose(1, 0, 2)
    if "wt_ple" in gr:
        out["w_ple"] = gr["wt_ple"].reshape(N_DEV, -1, PLE_DIM)
    return {k: v.astype(BF16) for k, v in out.items()}


def _rows(a, rows):
    flat = a.reshape(-1)
    return jnp.pad(flat, (0, rows * 1024 - flat.shape[0])).reshape(rows, 1024)


def _pack_small(parts):
    return jnp.concatenate([_rows(parts[n].astype(F32), r) for n, r in SMALL], axis=0)


def _small(packed, name, shape):
    off, r = SMALL_OFF[name]
    n = math.prod(shape)
    return packed[off:off + r].reshape(-1)[:n].reshape(shape)


class _Exchanges:
    W_S_ROWS = SMALL_OFF["gmlp_w_s"]

    def __init__(self, later, shards, pos):
        self.later, self.shards, self.pos = later, dict(zip(later, shards)), pos
        self.level1, self.slabs, self.from_sib, self.sums32, self.reduced, self.tables = {}, {}, {}, {}, {}, {}

    def gather1(self, names):
        carry = _carry_gather1([self.shards[n] for n in names])
        carry.names = names
        return carry

    def gather1_done(self, carry, results):
        self.level1.update(zip(carry.names, results))

    def gather2(self):
        return _carry_gather2([self.level1[n] for n in self.later])

    def weights(self, full):
        return _assemble_weights(dict(zip(self.later, full)))

    def sibling(self, grads):
        slabs = _grad_slabs(grads)
        self.slabs.update(slabs)
        carry = _carry_sibling(list(slabs.values()))
        carry.names = list(slabs)
        return carry

    def sibling_done(self, carry, results):
        self.from_sib.update(zip(carry.names, results))

    def chips(self, names, table=None):
        sums = {n: _sum_pairs(self.slabs[n], self.from_sib[n], self.pos, name="sum_sibling_" + n) for n in names}
        self.sums32.update({n: s32 for n, (s32, _) in sums.items()})
        carry = _carry_chips([s16 for _, s16 in sums.values()], None if table is None else self.table_part(table))
        carry.names, carry.table = list(names), table
        return carry

    def chips_done(self, carry, results):
        if carry.table is not None:
            *results, self.tables[carry.table] = results
        self.reduced.update({n: (self.sums32[n], r) for n, r in zip(carry.names, results)})

    def sibling_small(self, small_g):
        self.small_g = small_g
        return _carry_sibling([], small_g)

    def sibling_small_done(self, small_sib):
        self.small_chip = _pair_sum_small(self.small_g, small_sib, name="sum_sibling_small")

    def table_part(self, which):
        off, rows = self.W_S_ROWS
        if which == "w_s":
            return self.small_chip[off:off + rows]
        return jnp.concatenate([self.small_chip[:off], self.small_chip[off + rows:]], axis=0)

    def table(self):
        off = self.W_S_ROWS[0]
        rest = self.tables["rest"]
        return jnp.concatenate([rest[:, :off], self.tables["w_s"], rest[:, off:]], axis=1)


def _local_step(x, p, target, w, sm, ex=None):
    s = x.shape[0]
    mm = _matmul
    wt_main = w["wt_main"]
    conv_b = sm["conv_b"]
    bs_t = jnp.pad(sm["gmlp_b_s"].T, ((0, 0), (0, 128 - GROUPS)))
    b_f = jnp.pad(sm["b_f"], ((0, 0), (0, 128 - HEADS)))
    big = dict(tm=1024, tn=1024, tk=1024)
    whole_s = dict(tn=1024, tk=s)

    h = _rmsnorm_fwd(x, sm["norm_mix_g"], name="norm_mix")
    tall = dict(tm=s, tn=512, tk=1024)
    qkv_args = dict(mode="nt", out_dtype=BF16, name="in_qkv", n=3072, b_off=8, **tall)
    f_logit = mm(h, w["wt_f"], mode="nt", out_dtype=F32, name="in_f", tm=1024, tk=1024)
    cqe = _forget_cumsum(f_logit, b_f, name="forget_cumsum")
    uvg = dict(mode="nt", out_dtype=F32, name="in_uvg", n=4096, **tall)
    if ex is None:
        qkv = mm(h, wt_main, **qkv_args)
        (qa, ka, vt), _ = _attn_prep(qkv, cqe, name="attn_prep")
        (b, lse3), _ = _attn_fwd(qa, ka, vt, name="attn_fwd")
        zuvg = mm(h, wt_main, **uvg)
    else:
        groups = (["w_branch_a"], ["w_branch_b"], [n for n in ex.later if n not in ("w_branch_a", "w_branch_b")])
        carries = [ex.gather1(names) for names in groups]
        qkv, got0 = mm(h, wt_main, carry=carries[0], **qkv_args)
        (qa, ka, vt), got1 = _attn_prep(qkv, cqe, carries[1], name="attn_prep")
        (b, lse3), got2 = _attn_fwd(qa, ka, vt, carries[2], name="attn_fwd")
        for carry, got in zip(carries, (got0, got1, got2)):
            ex.gather1_done(carry, got)
        zuvg, full = mm(h, wt_main, carry=ex.gather2(), **uvg)
        w = {**w, **ex.weights(full)}
    a = _gmlp_fwd(zuvg, sm["gmlp_ln_g"], sm["gmlp_ln_b"], sm["gmlp_w_s"], bs_t, name="gmlp_fwd")
    wt_up, conv_w = w["wt_up"], w["conv_w"]
    ya, yb, merged = _branches_merge(a, b, w["w_a"], w["w_b"], zuvg, name="branches_merge")
    x1, h2 = mm(merged, w["w_out"], mode="nn", out_dtype=F32, name="out_proj", add=x, norm_g=sm["norm_ffn_g"], **big)
    up_a, up_g, act = _up_convglu(h2, wt_up, conv_w, conv_b, name="up_convglu")
    x2, h3 = mm(act, w["w_down"], mode="nn", out_dtype=F32, name="down", tm=1024, tn=1024, tk=1408, add=x1,
                norm_g=sm["norm_ple_g"])

    loss, dx3, dple, dgp, d_norm_final = _ple_loss(p, w["wt_ple"], h3, w["w_pg"], x2, target, sm["norm_final_g"],
                                                   name="ple_loss")
    g_wt_ple = mm(dple, p, mode="tn", out_dtype=BF16, name="d_w_ple", tm=512, tn=256, tk=s)
    g_w_pg = mm(h3, dgp, mode="tn", out_dtype=BF16, name="d_w_pg", tm=512, **whole_s)
    (dx2, dx2b, d_norm_ple), _ = _matmul_rmsnorm_bwd([dgp], w["w_pg"], dx3, x2, sm["norm_ple_g"], mode="nt", tk=1024,
                                                     name="d_h3_norm_ple_bwd")
    g_w_down = mm(act, dx2b, mode="tn", out_dtype=BF16, name="d_w_down", tm=1408, **whole_s)
    dact_args = dict(mode="nt", out_dtype=BF16, name="d_act", tm=s, tn=256, tk=1024)
    if ex is None:
        dact = mm(dx2b, w["w_down"], **dact_args)
    else:
        early = ex.sibling(dict(w_pg=g_w_pg, wt_ple=g_wt_ple))
        dact, got = mm(dx2b, w["w_down"], carry=early, **dact_args)
        ex.sibling_done(early, got)
    dup_a, dup_g, dcw_a, dcw_g, dcb_a, dcb_g = _convglu_bwd(dact, up_a, up_g, conv_w, conv_b, name="convglu_bwd")
    g_wt_up = mm(dup_a, h2, mode="tn", out_dtype=BF16, name="d_w_up_a", tm=1408, out_rows=2 * D_FF, **whole_s)
    g_wt_up = mm(dup_g, h2, mode="tn", out_dtype=BF16, name="d_w_up_g", tm=256, out_rows=2 * D_FF,
                 o_off=D_FF // 256, into=g_wt_up, **whole_s)
    (dx1, dx1b, d_norm_ffn), _ = _matmul_rmsnorm_bwd([dup_a, dup_g], wt_up, dx2, x1, sm["norm_ffn_g"], mode="nn",
                                                     tk=1408, name="d_h2_norm_ffn_bwd", resident=True)
    g_w_out = mm(merged, dx1b, mode="tn", out_dtype=BF16, name="d_w_out", tm=512, **whole_s)
    dya, dyb, dga, dgb = _merge_bwd(dx1b, w["w_out"], ya, yb, zuvg, name="merge_bwd")
    g_w_a = mm(a, dya, mode="tn", out_dtype=BF16, name="d_w_a", tm=512, **whole_s)
    g_w_b = mm(b, dyb, mode="tn", out_dtype=BF16, name="d_w_b", tm=1024, **whole_s)
    da = mm(dya, w["w_a"], mode="nt", out_dtype=BF16, name="d_a", **big)
    db = mm(dyb, w["w_b"], mode="nt", out_dtype=BF16, name="d_b", **big)
    grads = dict(w_a=g_w_a, w_b=g_w_b, w_out=g_w_out, wt_up=g_wt_up, conv_w=jnp.concatenate([dcw_a, dcw_g], axis=1),
                 w_down=g_w_down, wt_ple=g_wt_ple, w_pg=g_w_pg)
    gmlp_args = (da, zuvg, sm["gmlp_ln_g"], sm["gmlp_ln_b"], sm["gmlp_w_s"], bs_t)
    if ex is None:
        (dzu, dzv, d_w_s, d_bs_t, d_ln_g, d_ln_b), _ = _gmlp_bwd(*gmlp_args, name="gmlp_bwd")
    else:
        rest = ex.sibling({k: v for k, v in grads.items() if k not in ("w_pg", "wt_ple")})
        early_chips = ex.chips(early.names)
        both = _carry_join(rest, early_chips)
        (dzu, dzv, d_w_s, d_bs_t, d_ln_g, d_ln_b), got = _gmlp_bwd(*gmlp_args, both, name="gmlp_bwd")
        got_rest, got_early = both.split(got)
        ex.sibling_done(rest, got_rest)
        ex.chips_done(early_chips, got_early)
    small = dict(norm_mix_g=jnp.zeros((1, D_MODEL), F32), b_f=jnp.zeros((1, HEADS), F32), gmlp_ln_g=d_ln_g,
                 gmlp_ln_b=d_ln_b, gmlp_w_s=d_w_s, gmlp_b_s=d_bs_t[:, :GROUPS].T, norm_ffn_g=d_norm_ffn,
                 conv_b=jnp.concatenate([dcb_a, dcb_g], axis=1), norm_ple_g=d_norm_ple, norm_final_g=d_norm_final)
    if ex is None:
        delta3, _ = _attn_delta(db, b, name="attn_delta")
        (dq, dk, dv, aux, dcq3), _ = _attn_bwd(qa, ka, qkv, db, lse3, delta3, name="attn_bwd")
    else:
        delta3, (small_sib,) = _attn_delta(db, b, ex.sibling_small(_pack_small(small)), name="attn_delta")
        ex.sibling_small_done(small_sib)
        main_chips = ex.chips(rest.names, table="rest")
        (dq, dk, dv, aux, dcq3), got = _attn_bwd(qa, ka, qkv, db, lse3, delta3, main_chips, name="attn_bwd")
        ex.chips_done(main_chips, got)
    dcq16 = jnp.pad(dcq3[:, :2, :].reshape(HEADS, s).T, ((0, 0), (0, 128 - HEADS)))
    dzf, d_b_f = _forget_bwd(dcq16, aux, f_logit, b_f, name="forget_bwd")
    dz_parts = [dzu, dzv, dga, dgb, dq, dk, dv]
    w_s_chips = None if ex is None else ex.chips([], table="w_s")
    g_wt_main, got = _grad_w_parts(dz_parts, h, name="d_w_main", tm=512, carry=w_s_chips)
    if ex is not None:
        ex.chips_done(w_s_chips, got)
    g_wt_f = mm(dzf, h, mode="tn", out_dtype=BF16, name="d_w_f", **whole_s)
    grads = dict(grads, wt_main=g_wt_main, wt_f=g_wt_f)
    w_in_chips = None
    if ex is not None:
        w_in_sib = ex.sibling(dict(wt_main=g_wt_main, wt_f=g_wt_f))
        ex.sibling_done(w_in_sib, _run_carry(w_in_sib, name="exchange_sibling_w_in"))
        w_in_chips = ex.chips(w_in_sib.names)
    (dx0, _, d_norm_mix), got = _matmul_rmsnorm_bwd(dz_parts, wt_main, dx1, x, sm["norm_mix_g"], mode="nn", tk=1024,
                                                    extra=(dzf, w["wt_f"]), name="d_h_norm_mix_bwd", carry=w_in_chips,
                                                    lead=True)
    if ex is not None:
        ex.chips_done(w_in_chips, got)
    return loss, dx0, grads, dict(small, norm_mix_g=d_norm_mix, b_f=d_b_f[:, :HEADS])


def kernel(x, p, norm_mix_g, w_in, b_f, gmlp_ln_g, gmlp_ln_b, gmlp_w_s, gmlp_b_s, w_branch_a, w_branch_b, w_out, norm_ffn_g, w_up, conv_w, conv_b, w_down, norm_ple_g, w_ple, w_ple_gate, norm_final_g, loss_target, m_norm_mix_g, m_w_in, m_b_f, m_gmlp_ln_g, m_gmlp_ln_b, m_gmlp_w_s, m_gmlp_b_s, m_w_branch_a, m_w_branch_b, m_w_out, m_norm_ffn_g, m_w_up, m_conv_w, m_conv_b, m_w_down, m_norm_ple_g, m_w_ple, m_w_ple_gate, m_norm_final_g, v_norm_mix_g, v_w_in, v_b_f, v_gmlp_ln_g, v_gmlp_ln_b, v_gmlp_w_s, v_gmlp_b_s, v_w_branch_a, v_w_branch_b, v_w_out, v_norm_ffn_g, v_w_up, v_conv_w, v_conv_b, v_w_down, v_norm_ple_g, v_w_ple, v_w_ple_gate, v_norm_final_g):
    given = dict(locals())
    weights = {n: given[n] for n in WEIGHT_ORDER}
    mom_m = {n: given["m_" + n] for n in WEIGHT_ORDER}
    mom_v = {n: given["v_" + n] for n in WEIGHT_ORDER}
    pos = jnp.stack([lax.axis_index("x"), lax.axis_index("y"), lax.axis_index("c")]).astype(I32)
    names = [n for n, _ in SHARDED]
    kinds = dict(SHARDED)

    later = [n for n in names if n != "w_in"]

    first = _allgather([_to_comm("w_in", kinds["w_in"], weights["w_in"])], name="allgather_w_in")
    ex = _Exchanges(later, [_to_comm(n, kinds[n], weights[n]) for n in later], pos)

    sm = dict(norm_mix_g=norm_mix_g, b_f=b_f, gmlp_ln_g=gmlp_ln_g, gmlp_ln_b=gmlp_ln_b, gmlp_w_s=gmlp_w_s[0],
              gmlp_b_s=gmlp_b_s[0], norm_ffn_g=norm_ffn_g, conv_b=conv_b, norm_ple_g=norm_ple_g,
              norm_final_g=norm_final_g.reshape(1, D_MODEL))
    loss_part, dx0, grads, small = _local_step(
        x[0], p[0, 0], loss_target[0], _assemble_weights({"w_in": first[0]}), sm, ex)

    b_f_and_loss = jnp.concatenate([small["b_f"].reshape(-1), loss_part[0, :1]])
    last = _allreduce_rows(jnp.concatenate([_rows(small["norm_mix_g"], 8), _rows(b_f_and_loss, 8)], axis=0),
                           name="allreduce_last")
    loss = last[8, HEADS]
    small_last = jnp.pad(last, ((0, SMALL_ROWS - 16), (0, 0)))

    grad, delta, new_m, new_v = {}, {}, {}, {}
    for n in names:
        s32, r = ex.reduced[n]
        outs = _adam_sharded(s32, r, *[_to_comm(n, kinds[n], src[n], F32) for src in (weights, mom_m, mom_v)], pos,
                             name="adam_" + n)
        grad[n], delta[n], new_m[n], new_v[n] = [_from_comm(n, kinds[n], o) for o in outs]
    replicated = [n for n, _ in SMALL]
    rep = lambda src: _pack_small({n: src[n] for n in replicated})
    packed = _adam_replicated(ex.table(), small_last, rep(weights), rep(mom_m), rep(mom_v), name="adam_replicated")
    for out, pk in zip((grad, delta, new_m, new_v), packed):
        for n in replicated:
            out[n] = _small(pk, n, weights[n].shape)

    return (loss, dx0, *[grad[n] for n in WEIGHT_ORDER], *[delta[n] for n in WEIGHT_ORDER],
            *[new_m[n] for n in WEIGHT_ORDER], *[new_v[n] for n in WEIGHT_ORDER])
```

```python
import functools
import math

import jax
import jax.numpy as jnp
from jax import lax
from jax.experimental import pallas as pl
from jax.experimental.pallas import tpu as pltpu

F32 = jnp.float32
BF16 = jnp.bfloat16
I32 = jnp.int32

D_MODEL = 1024
GROUPS = 8
GDIM = 128
GBLOCK = 128
CHUNK = 64
HEADS = 16
HEAD_DIM = 64
D_FF = 2816
PLE_DIM = 256
EPS = 1e-6
N_DEV = 8
ATT_SCALE = HEAD_DIM ** -0.5
NEG = -1e30

ADAM_LR = 0.001
ADAM_B1 = 0.9
ADAM_B2 = 0.999
ADAM_EPS = 1e-08
ADAM_WD = 0.01
ADAM_STEP = 10

V7X_VMEM_LIMIT = 48 * 1024 * 1024
MESH = pl.DeviceIdType.MESH

O_F = 2 * 1024 + 3 * 1024
O_G = O_F + HEADS
IN_COLS = O_G + 2 * D_MODEL
MAIN_COLS = IN_COLS - HEADS
IN_SHARD = IN_COLS // N_DEV
IN_SHARD_PAD = 912

SHARDED = (("w_in", "cols"), ("w_branch_a", "rows"), ("w_branch_b", "rows"), ("w_out", "rows"), ("w_up", "cols"),
           ("conv_w", "f32"), ("w_down", "rows"), ("w_ple", "cols"), ("w_ple_gate", "rows"))

SMALL = (("norm_mix_g", 8), ("b_f", 8), ("gmlp_ln_g", 8), ("gmlp_ln_b", 8), ("gmlp_w_s", 128), ("gmlp_b_s", 8),
         ("norm_ffn_g", 8), ("conv_b", 8), ("norm_ple_g", 8), ("norm_final_g", 8))
SMALL_OFF = {}
_o = 0
for _n, _r in SMALL:
    SMALL_OFF[_n] = (_o, _r)
    _o += _r
SMALL_ROWS = _o

WEIGHT_ORDER = ("norm_mix_g", "w_in", "b_f", "gmlp_ln_g", "gmlp_ln_b", "gmlp_w_s", "gmlp_b_s", "w_branch_a",
                "w_branch_b", "w_out", "norm_ffn_g", "w_up", "conv_w", "conv_b", "w_down", "norm_ple_g", "w_ple",
                "w_ple_gate", "norm_final_g")


def _cparams(sem):
    return pltpu.CompilerParams(dimension_semantics=sem, vmem_limit_bytes=V7X_VMEM_LIMIT)


def _gelu(x):
    c = math.sqrt(2.0 / math.pi)
    return 0.5 * x * (1.0 + jnp.tanh(c * (x + 0.044715 * x * x * x)))


def _gelu_and_grad(x):
    c = math.sqrt(2.0 / math.pi)
    t = jnp.tanh(c * (x + 0.044715 * x * x * x))
    g = 0.5 * x * (1.0 + t)
    dg = 0.5 * (1.0 + t) + 0.5 * x * (1.0 - t * t) * (c * (1.0 + 3.0 * 0.044715 * x * x))
    return g, dg


def _sigmoid(x):
    return 1.0 / (1.0 + jnp.exp(-x))


def _dot(a, b, dims):
    return lax.dot_general(a, b, (dims, ((), ())), preferred_element_type=F32)


NN = ((1,), (0,))
NT = ((1,), (1,))
TN = ((0,), (0,))


def _row_tile(rows, most):
    best = None
    for t in range(16, min(rows, most) + 1, 16):
        if rows % t == 0:
            best = t
    return best if best is not None else rows


def _matmul(a, b, *, mode, out_dtype, name, tm=512, tn=512, tk=512, add=None, n=None, b_off=0,
            out_rows=None, o_off=0, into=None, norm_g=None, carry=None):
    if mode == "tn":
        kdim, m = a.shape
    else:
        m, kdim = a.shape
    if n is None:
        n = b.shape[0] if mode == "nt" else b.shape[1]
    tm, tn, tk = min(tm, m), min(tn, n), min(tk, kdim)
    assert m % tm == 0 and n % tn == 0 and kdim % tk == 0, (name, m, n, kdim, tm, tn, tk)
    nk = kdim // tk
    dims = {"nn": NN, "nt": NT, "tn": TN}[mode]

    n_in = 2 + (add is not None) + (into is not None) + (norm_g is not None)
    assert norm_g is None or tn == n, "the RMS norm needs whole rows"

    def finish(r, refs):
        if add is not None:
            r = refs[2][...].astype(F32) + r
        refs[n_in][...] = r.astype(out_dtype)
        if norm_g is not None:
            rs = lax.rsqrt(jnp.mean(r * r, axis=-1, keepdims=True) + EPS)
            refs[n_in + 1][...] = ((r * rs) * refs[n_in - 1][...]).astype(BF16)

    def body(*refs):
        a_ref, b_ref = refs[:2]
        part = _dot(a_ref[...].astype(BF16), b_ref[...].astype(BF16), dims)
        if nk == 1:
            finish(part, refs)
            return
        acc_ref = refs[-1]
        k = pl.program_id(2)

        @pl.when(k == 0)
        def _():
            acc_ref[...] = part

        @pl.when((k > 0) & (k < nk - 1))
        def _():
            acc_ref[...] += part

        @pl.when(k == nk - 1)
        def _():
            finish(acc_ref[...] + part, refs)

    a_spec = pl.BlockSpec((tk, tm), lambda i, j, k: (k, i)) if mode == "tn" else pl.BlockSpec((tm, tk), lambda i, j, k: (i, k))
    if mode == "nt":
        b_spec = pl.BlockSpec((tn, tk), lambda i, j, k: (j + b_off, k))
    else:
        b_spec = pl.BlockSpec((tk, tn), lambda i, j, k: (k + b_off, j))
    o_spec = pl.BlockSpec((tm, tn), lambda i, j, k: (i + o_off, j))
    in_specs = [a_spec, b_spec] + ([pl.BlockSpec((tm, tn), lambda i, j, k: (i, j))] if add is not None else [])
    args = (a, b) + ((add,) if add is not None else ())
    aliases = {}
    if into is not None:
        aliases = {len(args): 0}
        in_specs.append(pl.BlockSpec(memory_space=pl.ANY))
        args += (into,)
    out_specs = [o_spec]
    out_shape = [jax.ShapeDtypeStruct((m if out_rows is None else out_rows, n), out_dtype)]
    if norm_g is not None:
        in_specs.append(pl.BlockSpec((1, n), lambda i, j, k: (0, 0)))
        args += (norm_g,)
        out_specs.append(pl.BlockSpec((tm, tn), lambda i, j, k: (i, j)))
        out_shape.append(jax.ShapeDtypeStruct((m, n), BF16))
    outs, carried = _carry_call(
        body, carry, name=name, grid=(m // tm, n // tn, nk), in_specs=in_specs, out_specs=out_specs,
        out_shape=out_shape, scratch_shapes=[pltpu.VMEM((tm, tn), F32)] if nk > 1 else [], args=args,
        own_aliases=aliases)
    out = outs[0] if norm_g is None else tuple(outs)
    return out if carry is None else (out, carried)


def _row_spec(tr, width, col_block=0):
    return pl.BlockSpec((tr, width), lambda i: (i, col_block))


def _full_spec(shape):
    return pl.BlockSpec(shape, lambda i: tuple(0 for _ in shape))


def _rmsnorm_fwd(x, g, *, name, tr=256):
    s, d = x.shape

    def body(x_ref, g_ref, o_ref):
        xv = x_ref[...]
        r = lax.rsqrt(jnp.mean(xv * xv, axis=-1, keepdims=True) + EPS)
        o_ref[...] = ((xv * r) * g_ref[...]).astype(BF16)

    return pl.pallas_call(
        body, name=name, grid=(s // tr,),
        in_specs=[_row_spec(tr, d), _full_spec((1, d))], out_specs=_row_spec(tr, d),
        out_shape=jax.ShapeDtypeStruct((s, d), BF16), compiler_params=_cparams(("parallel",)),
    )(x, g)


def _matmul_rmsnorm_bwd(a_parts, b, dres, x, g, *, mode, tk, name, extra=None, tm=512, carry=None, lead=False,
                        resident=False):
    s, d = x.shape
    n_row = s // tm
    spans, lo = [], 0
    for a in a_parts:
        spans.append((lo, lo + a.shape[1] // tk))
        lo = spans[-1][1]
    n_main, total = lo, lo + (extra is not None)
    n_parts = len(a_parts)

    def body(*refs):
        a_refs, b_ref = refs[:n_parts], refs[n_parts]
        k0 = n_parts + 1
        ax_ref, bx_ref = (refs[k0], refs[k0 + 1]) if extra is not None else (None, None)
        k0 += 2 * (extra is not None)
        dres_ref, x_ref, g_ref, dx_ref, dxb_ref, dg_ref, acc_all = refs[k0:k0 + 7]
        if resident:
            kk, i = pl.program_id(0), pl.program_id(1)
            acc_ref = acc_all.at[pl.ds(pl.multiple_of(i * tm, tm), tm)]
        else:
            i, kk = pl.program_id(0), pl.program_id(1)
            acc_ref = acc_all

        def accumulate(part, first):
            if first:
                @pl.when(kk == 0)
                def _():
                    acc_ref[...] = part

                @pl.when(kk > 0)
                def _():
                    acc_ref[...] += part
            else:
                acc_ref[...] += part

        for p, (a_ref, (lo_p, hi_p)) in enumerate(zip(a_refs, spans)):
            @pl.when((kk >= lo_p) & (kk < hi_p))
            def _(a_ref=a_ref, lo_p=lo_p):
                accumulate(_dot(a_ref[...].astype(BF16), b_ref[...].astype(BF16), NN if mode == "nn" else NT), lo_p == 0)

        if extra is not None:
            @pl.when(kk == n_main)
            def _():
                accumulate(_dot(ax_ref[...].astype(BF16), bx_ref[...].astype(BF16), NN), False)

        @pl.when(kk == total - 1)
        def _():
            dhv = acc_ref[...]
            xv = x_ref[...]
            r = lax.rsqrt(jnp.mean(xv * xv, axis=-1, keepdims=True) + EPS)
            xhat = xv * r
            dxhat = dhv * g_ref[...]
            dx = dres_ref[...] + r * (dxhat - xhat * jnp.mean(dxhat * xhat, axis=-1, keepdims=True))
            dx_ref[...] = dx
            dxb_ref[...] = dx.astype(BF16)
            dgp = jnp.sum(dhv * xhat, axis=0, keepdims=True)

            @pl.when(i == 0)
            def _():
                dg_ref[...] = dgp

            @pl.when(i > 0)
            def _():
                dg_ref[...] += dgp

    def spec(shape, index):
        return pl.BlockSpec(shape, (lambda kk, i: index(i, kk)) if resident else index)

    def row(i, kk, lo_p, hi_p):
        if not resident:
            return i
        return jnp.where(kk < lo_p, 0, jnp.where(kk >= hi_p, n_row - 1, i))

    a_specs = [spec((tm, tk), lambda i, kk, lo_p=lo_p, hi_p=hi_p: (row(i, kk, lo_p, hi_p),
                                                                    jnp.clip(kk - lo_p, 0, hi_p - lo_p - 1)))
               for lo_p, hi_p in spans]
    step = lambda kk: jnp.minimum(kk, n_main - 1)
    b_spec = (spec((tk, d), lambda i, kk: (step(kk), 0)) if mode == "nn"
              else spec((d, tk), lambda i, kk: (0, step(kk))))
    rows = spec((tm, d), lambda i, kk: (row(i, kk, total - 1, total), 0))
    one = spec((1, d), lambda i, kk: (0, 0))
    dx_spec, dx_shape = rows, jax.ShapeDtypeStruct((s, d), F32)
    if lead:
        dx_spec = spec((None, tm, d), lambda i, kk: (0, row(i, kk, total - 1, total), 0))
        dx_shape = jax.ShapeDtypeStruct((1, s, d), F32)
    x_specs, x_args = [], []
    if extra is not None:
        kx = extra[0].shape[1]
        x_specs = [spec((tm, kx), lambda i, kk: (row(i, kk, n_main, total), 0)), spec((kx, d), lambda i, kk: (0, 0))]
        x_args = list(extra)
    (dx, dxb, dg), carried = _carry_call(
        body, carry, name=name, grid=(total, n_row) if resident else (n_row, total),
        in_specs=a_specs + [b_spec] + x_specs + [rows, rows, one], out_specs=[dx_spec, rows, one],
        out_shape=[dx_shape, jax.ShapeDtypeStruct((s, d), BF16), jax.ShapeDtypeStruct((1, d), F32)],
        scratch_shapes=[pltpu.VMEM((s if resident else tm, d), F32)], args=list(a_parts) + [b] + x_args + [dres, x, g])
    return (dx, dxb, dg), carried


def _grad_w_parts(a_parts, b, *, name, tm=512, carry=None):
    s, width = a_parts[0].shape
    per, n = width // tm, b.shape[1]

    def body(*refs):
        a_refs, b_ref, o_ref = refs[:len(a_parts)], refs[len(a_parts)], refs[len(a_parts) + 1]
        i = pl.program_id(0)
        for p, a_ref in enumerate(a_refs):
            @pl.when(i // per == p)
            def _(a_ref=a_ref):
                o_ref[...] = _dot(a_ref[...].astype(BF16), b_ref[...].astype(BF16), TN).astype(BF16)

    a_specs = [pl.BlockSpec((s, tm), lambda i, p=p: (0, jnp.clip(i - p * per, 0, per - 1))) for p in range(len(a_parts))]
    (out,), carried = _carry_call(
        body, carry, name=name, grid=(len(a_parts) * per,),
        in_specs=a_specs + [pl.BlockSpec((s, n), lambda i: (0, 0))], out_specs=[pl.BlockSpec((tm, n), lambda i: (i, 0))],
        out_shape=[jax.ShapeDtypeStruct((len(a_parts) * width, n), BF16)], scratch_shapes=[], args=list(a_parts) + [b])
    return out, carried


def _ple_loss(p, wt_ple, h3, w_pg, x2, target, g, *, name, tm=256):
    s, d = x2.shape
    kp = p.shape[1]

    def body(p_ref, wp_ref, h_ref, wg_ref, x_ref, t_ref, g_ref, loss_ref, dx_ref, dple_ref, dgp_ref, dg_ref):
        i = pl.program_id(0)
        ple = _dot(p_ref[...].astype(BF16), wp_ref[...], NT)
        sg = _sigmoid(_dot(h_ref[...], wg_ref[...], NN))
        xv = x_ref[...] + ple * sg
        r = lax.rsqrt(jnp.mean(xv * xv, axis=-1, keepdims=True) + EPS)
        xhat = xv * r
        diff = xhat * g_ref[...] - t_ref[...]
        lp = jnp.zeros((1, 128), F32) + (0.5 / d) * jnp.sum(diff * diff)
        dy = diff * (1.0 / d)
        dxhat = dy * g_ref[...]
        dx = r * (dxhat - xhat * jnp.mean(dxhat * xhat, axis=-1, keepdims=True))
        dx_ref[...] = dx
        dple_ref[...] = (dx * sg).astype(BF16)
        dgp_ref[...] = (dx * ple * (sg * (1.0 - sg))).astype(BF16)
        dgp = jnp.sum(dy * xhat, axis=0, keepdims=True)

        @pl.when(i == 0)
        def _():
            dg_ref[...] = dgp
            loss_ref[...] = lp

        @pl.when(i > 0)
        def _():
            dg_ref[...] += dgp
            loss_ref[...] += lp

    rows = _row_spec(tm, d)
    return pl.pallas_call(
        body, name=name, grid=(s // tm,),
        in_specs=[_row_spec(tm, kp), _full_spec((d, kp)), rows, _full_spec((d, d)), rows, rows, _full_spec((1, d))],
        out_specs=[_full_spec((1, 128)), rows, rows, rows, _full_spec((1, d))],
        out_shape=[jax.ShapeDtypeStruct((1, 128), F32), jax.ShapeDtypeStruct((s, d), F32),
                   jax.ShapeDtypeStruct((s, d), BF16), jax.ShapeDtypeStruct((s, d), BF16),
                   jax.ShapeDtypeStruct((1, d), F32)],
        compiler_params=_cparams(("arbitrary",)),
    )(p, wt_ple, h3, w_pg, x2, target, g)


def _branches_merge(a, b, w_a, w_b, zuvg, *, name, tm=512):
    s, d = a.shape

    def body(a_ref, b_ref, wa_ref, wb_ref, ga_ref, gb_ref, ya_ref, yb_ref, o_ref):
        ya = _dot(a_ref[...], wa_ref[...], NN)
        yb = _dot(b_ref[...], wb_ref[...], NN)
        ya_ref[...] = ya
        yb_ref[...] = yb
        o_ref[...] = (_sigmoid(ga_ref[...]) * ya + _sigmoid(gb_ref[...]) * yb).astype(BF16)

    rows = _row_spec(tm, d)
    return pl.pallas_call(
        body, name=name, grid=(s // tm,),
        in_specs=[rows, rows, _full_spec((d, d)), _full_spec((d, d)), _row_spec(tm, d, 2), _row_spec(tm, d, 3)],
        out_specs=[rows, rows, rows],
        out_shape=[jax.ShapeDtypeStruct((s, d), F32), jax.ShapeDtypeStruct((s, d), F32), jax.ShapeDtypeStruct((s, d), BF16)],
        compiler_params=_cparams(("parallel",)),
    )(a, b, w_a, w_b, zuvg, zuvg)


def _merge_bwd(dx1b, w_out, ya, yb, zuvg, *, name, tm=512):
    s, d = ya.shape

    def body(dx_ref, w_ref, ya_ref, yb_ref, ga_ref, gb_ref, dya_ref, dyb_ref, dga_ref, dgb_ref):
        dmv = _dot(dx_ref[...], w_ref[...], NT)
        sa = _sigmoid(ga_ref[...])
        sb = _sigmoid(gb_ref[...])
        dya_ref[...] = (dmv * sa).astype(BF16)
        dyb_ref[...] = (dmv * sb).astype(BF16)
        dga_ref[...] = (dmv * ya_ref[...] * (sa * (1.0 - sa))).astype(BF16)
        dgb_ref[...] = (dmv * yb_ref[...] * (sb * (1.0 - sb))).astype(BF16)

    rows = _row_spec(tm, d)
    o = jax.ShapeDtypeStruct((s, d), BF16)
    return pl.pallas_call(
        body, name=name, grid=(s // tm,),
        in_specs=[rows, _full_spec((d, d)), rows, rows, _row_spec(tm, d, 2), _row_spec(tm, d, 3)],
        out_specs=[rows] * 4, out_shape=[o, o, o, o], compiler_params=_cparams(("parallel",)),
    )(dx1b, w_out, ya, yb, zuvg, zuvg)


def _masked_ws(ws_ref, g):
    row = lax.broadcasted_iota(I32, (GBLOCK, GBLOCK), 0)
    col = lax.broadcasted_iota(I32, (GBLOCK, GBLOCK), 1)
    keep = (col // CHUNK) <= (row // CHUNK)
    return jnp.where(keep, ws_ref[g], 0.0), keep


def _layernorm_parts(zv):
    mu = jnp.mean(zv, axis=-1, keepdims=True)
    xc = zv - mu
    rs = lax.rsqrt(jnp.mean(xc * xc, axis=-1, keepdims=True) + EPS)
    return xc * rs, rs


def _gmlp_fwd(zuvg, ln_g, ln_b, w_s, bs_t, *, name):
    s, w = zuvg.shape[0], GROUPS * GDIM

    def body(zu_ref, zv_ref, lng_ref, lnb_ref, ws_ref, bs_ref, a_ref):
        zu = _gelu(zu_ref[...])
        zv = _gelu(zv_ref[...])
        xhat, _ = _layernorm_parts(zv)
        vln = (xhat * lng_ref[...] + lnb_ref[...]).astype(BF16)
        for g in range(GROUPS):
            wm, _ = _masked_ws(ws_ref, g)
            mixed = _dot(wm.astype(BF16), vln[:, g * GDIM:(g + 1) * GDIM], NN) + bs_ref[:, g:g + 1]
            a_ref[:, g * GDIM:(g + 1) * GDIM] = (zu[:, g * GDIM:(g + 1) * GDIM] * mixed).astype(BF16)

    return pl.pallas_call(
        body, name=name, grid=(s // GBLOCK,),
        in_specs=[_row_spec(GBLOCK, w, 0), _row_spec(GBLOCK, w, 1), _full_spec((1, w)), _full_spec((1, w)),
                  _full_spec((GROUPS, GBLOCK, GBLOCK)), _full_spec((GBLOCK, 128))],
        out_specs=_row_spec(GBLOCK, w),
        out_shape=jax.ShapeDtypeStruct((s, w), BF16), compiler_params=_cparams(("parallel",)),
    )(zuvg, zuvg, ln_g, ln_b, w_s, bs_t)


def _gmlp_bwd(da, zuvg, ln_g, ln_b, w_s, bs_t, carry=None, *, name):
    s, w = zuvg.shape[0], GROUPS * GDIM

    def body(da_ref, zu_ref, zv_ref, lng_ref, lnb_ref, ws_ref, bs_ref,
             dzu_ref, dzv_ref, dws_ref, dbs_ref, dlng_ref, dlnb_ref, dvln_ref):
        i = pl.program_id(0)
        zu, dzu_g = _gelu_and_grad(zu_ref[...])
        zv, dzv_g = _gelu_and_grad(zv_ref[...])
        xhat, rs = _layernorm_parts(zv)
        vln = (xhat * lng_ref[...] + lnb_ref[...]).astype(BF16)
        dav = da_ref[...].astype(F32)
        lane = lax.broadcasted_iota(I32, (GBLOCK, 128), 1)
        dbs = jnp.zeros((GBLOCK, 128), F32)

        @pl.when(i == 0)
        def _():
            dws_ref[...] = jnp.zeros_like(dws_ref)

        for g in range(GROUPS):
            sl = slice(g * GDIM, (g + 1) * GDIM)
            wm, keep = _masked_ws(ws_ref, g)
            wmb = wm.astype(BF16)
            vg = vln[:, sl]
            mixed = _dot(wmb, vg, NN) + bs_ref[:, g:g + 1]
            dag = dav[:, sl]
            dzu_ref[:, sl] = (dag * mixed * dzu_g[:, sl]).astype(BF16)
            dmix = dag * zu[:, sl]
            dmb = dmix.astype(BF16)
            dws_ref[g] += jnp.where(keep, _dot(dmb, vg, NT), 0.0)
            dbs = jnp.where(lane == g, jnp.sum(dmix, axis=1, keepdims=True), dbs)
            dvln_ref[:, sl] = _dot(wmb, dmb, TN)
        dvln = dvln_ref[...]
        dxhat = dvln * lng_ref[...]
        dzv = rs * (dxhat - jnp.mean(dxhat, axis=-1, keepdims=True)
                    - xhat * jnp.mean(dxhat * xhat, axis=-1, keepdims=True))
        dzv_ref[...] = (dzv * dzv_g).astype(BF16)
        dlng = jnp.sum(dvln * xhat, axis=0, keepdims=True)
        dlnb = jnp.sum(dvln, axis=0, keepdims=True)

        @pl.when(i == 0)
        def _():
            dbs_ref[...] = dbs
            dlng_ref[...] = dlng
            dlnb_ref[...] = dlnb

        @pl.when(i > 0)
        def _():
            dbs_ref[...] += dbs
            dlng_ref[...] += dlng
            dlnb_ref[...] += dlnb

    return _carry_call(
        body, carry, name=name, grid=(s // GBLOCK,),
        in_specs=[_row_spec(GBLOCK, w), _row_spec(GBLOCK, w, 0), _row_spec(GBLOCK, w, 1), _full_spec((1, w)),
                  _full_spec((1, w)), _full_spec((GROUPS, GBLOCK, GBLOCK)), _full_spec((GBLOCK, 128))],
        out_specs=[_row_spec(GBLOCK, w), _row_spec(GBLOCK, w), _full_spec((GROUPS, GBLOCK, GBLOCK)),
                   _full_spec((GBLOCK, 128)), _full_spec((1, w)), _full_spec((1, w))],
        out_shape=[jax.ShapeDtypeStruct((s, w), BF16), jax.ShapeDtypeStruct((s, w), BF16),
                   jax.ShapeDtypeStruct((GROUPS, GBLOCK, GBLOCK), F32), jax.ShapeDtypeStruct((GBLOCK, 128), F32),
                   jax.ShapeDtypeStruct((1, w), F32), jax.ShapeDtypeStruct((1, w), F32)],
        scratch_shapes=[pltpu.VMEM((GBLOCK, w), F32)], args=[da, zuvg, zuvg, ln_g, ln_b, w_s, bs_t])


def _shift_down(u, k):
    row = lax.broadcasted_iota(I32, u.shape, 0)
    return jnp.where(row >= k, pltpu.roll(u, k, 0), 0.0)


def _shift_up(u, k):
    s = u.shape[0]
    row = lax.broadcasted_iota(I32, u.shape, 0)
    return jnp.where(row < s - k, pltpu.roll(u, s - k, 0), 0.0)


def _conv(u, w_ref, b_ref):
    return b_ref[...] + w_ref[0:1, :] * _shift_down(u, 2) + w_ref[1:2, :] * _shift_down(u, 1) + w_ref[2:3, :] * u


def _conv_specs(s, f, tc):
    nc = f // tc
    half = lambda rows: [pl.BlockSpec((rows, tc), lambda j: (0, j)), pl.BlockSpec((rows, tc), lambda j: (0, nc + j))]
    return half(s), half(3), half(1)


def _up_convglu(h2, wt_up, conv_w, conv_b, *, name, tc=256):
    s, d = h2.shape
    f = wt_up.shape[0] // 2
    nc = f // tc
    _, w_specs, b_specs = _conv_specs(s, f, tc)

    def body(h_ref, ta_ref, tg_ref, wa_ref, wg_ref, ba_ref, bg_ref, ua_ref, ug_ref, o_ref):
        ua = _dot(h_ref[...], ta_ref[...], NT)
        ua_ref[...] = ua
        ga = _gelu(_conv(ua, wa_ref, ba_ref))
        ug = _dot(h_ref[...], tg_ref[...], NT)
        ug_ref[...] = ug
        o_ref[...] = (ga * _conv(ug, wg_ref, bg_ref)).astype(BF16)

    col = pl.BlockSpec((s, tc), lambda j: (0, j))
    return pl.pallas_call(
        body, name=name, grid=(nc,),
        in_specs=[_full_spec((s, d)), pl.BlockSpec((tc, d), lambda j: (j, 0)), pl.BlockSpec((tc, d), lambda j: (nc + j, 0))]
        + w_specs + b_specs,
        out_specs=[col, col, col],
        out_shape=[jax.ShapeDtypeStruct((s, f), F32), jax.ShapeDtypeStruct((s, f), F32), jax.ShapeDtypeStruct((s, f), BF16)],
        compiler_params=_cparams(("parallel",)),
    )(h2, wt_up, wt_up, conv_w, conv_w, conv_b, conv_b)


def _convglu_bwd(dact, up_a, up_g, conv_w, conv_b, *, name, tc=256):
    s, f = up_a.shape
    _, w_specs, b_specs = _conv_specs(s, f, tc)
    up_specs = [pl.BlockSpec((s, tc), lambda j: (0, j))] * 2

    def half(dc, taps, w_ref, du_ref, dw_ref, db_ref):
        db_ref[...] = jnp.sum(dc, axis=0, keepdims=True)
        for k in range(3):
            dw_ref[k:k + 1, :] = jnp.sum(dc * taps[k], axis=0, keepdims=True)
        du = w_ref[2:3, :] * dc + w_ref[1:2, :] * _shift_up(dc, 1) + w_ref[0:1, :] * _shift_up(dc, 2)
        du_ref[...] = du.astype(BF16)

    def body(d_ref, ua_ref, ug_ref, wa_ref, wg_ref, ba_ref, bg_ref,
             dua_ref, dug_ref, dwa_ref, dwg_ref, dba_ref, dbg_ref):
        taps_a = (_shift_down(ua_ref[...], 2), _shift_down(ua_ref[...], 1), ua_ref[...])
        taps_g = (_shift_down(ug_ref[...], 2), _shift_down(ug_ref[...], 1), ug_ref[...])
        conv = lambda taps, w_ref, b_ref: b_ref[...] + w_ref[0:1, :] * taps[0] + w_ref[1:2, :] * taps[1] + w_ref[2:3, :] * taps[2]
        ca = conv(taps_a, wa_ref, ba_ref)
        cg = conv(taps_g, wg_ref, bg_ref)
        ga, dga = _gelu_and_grad(ca)
        dv = d_ref[...].astype(F32)
        half(dv * cg * dga, taps_a, wa_ref, dua_ref, dwa_ref, dba_ref)
        half(dv * ga, taps_g, wg_ref, dug_ref, dwg_ref, dbg_ref)

    col, w3, b1 = up_specs[0], w_specs[0], b_specs[0]
    return pl.pallas_call(
        body, name=name, grid=(f // tc,),
        in_specs=[col] + up_specs + w_specs + b_specs, out_specs=[col, col, w3, w3, b1, b1],
        out_shape=[jax.ShapeDtypeStruct((s, f), BF16), jax.ShapeDtypeStruct((s, f), BF16),
                   jax.ShapeDtypeStruct((3, f), F32), jax.ShapeDtypeStruct((3, f), F32),
                   jax.ShapeDtypeStruct((1, f), F32), jax.ShapeDtypeStruct((1, f), F32)],
        compiler_params=_cparams(("parallel",)),
    )(dact, up_a, up_g, conv_w, conv_w, conv_b, conv_b)


def _tri_dot(tri, x):
    b0 = x.astype(BF16)
    r1 = x - b0.astype(F32)
    b1 = r1.astype(BF16)
    b2 = (r1 - b1.astype(F32)).astype(BF16)
    return _dot(tri, b0, NN) + _dot(tri, b1, NN) + _dot(tri, b2, NN)


def _log_sigmoid(x):
    return jnp.minimum(x, 0.0) - jnp.log(1.0 + jnp.exp(-jnp.abs(x)))


def _expand_heads(col16, rows):
    src = lax.broadcasted_iota(I32, (128, HEADS * HEAD_DIM), 0)
    dst = lax.broadcasted_iota(I32, (128, HEADS * HEAD_DIM), 1) // HEAD_DIM
    spread = (src == dst).astype(BF16)
    p0, p1, p2 = _bf16_pieces(col16)
    return (_dot(p0.astype(BF16), spread, NN) + _dot(p1.astype(BF16), spread, NN)) + _dot(p2.astype(BF16), spread, NN)


def _forget_cumsum(f_logit, b_f, *, name):
    s = f_logit.shape[0]
    nb = s // 128

    def body(f_ref, b_ref, cqe_ref):
        row = lax.broadcasted_iota(I32, (128, 128), 0)
        col = lax.broadcasted_iota(I32, (128, 128), 1)
        tri = (col <= row).astype(BF16)

        def step(n, carry):
            r0 = pl.multiple_of(n * 128, 128)
            lf = _log_sigmoid(f_ref[pl.ds(r0, 128), :] + b_ref[...])
            cum = _tri_dot(tri, lf) + carry
            cqe_ref[pl.ds(r0, 128), :] = _expand_heads(cum, 128)
            return cum[127:128, :]

        lax.fori_loop(0, nb, step, jnp.zeros((1, 128), F32))

    return pl.pallas_call(
        body, name=name, grid=(1,),
        in_specs=[_full_spec((s, 128)), _full_spec((1, 128))],
        out_specs=_full_spec((s, HEADS * HEAD_DIM)),
        out_shape=jax.ShapeDtypeStruct((s, HEADS * HEAD_DIM), F32),
        compiler_params=_cparams(("arbitrary",)),
    )(f_logit, b_f)


def _forget_bwd(dcq16, sum_q16, f_logit, b_f, *, name):
    s = f_logit.shape[0]
    nb = s // 128

    def body(a_ref, k_ref, f_ref, b_ref, df_ref, db_ref):
        row = lax.broadcasted_iota(I32, (128, 128), 0)
        col = lax.broadcasted_iota(I32, (128, 128), 1)
        tri_rev = (col >= row).astype(BF16)

        def step(m, carry):
            suffix, dbsum = carry
            n = nb - 1 - m
            r0 = pl.multiple_of(n * 128, 128)
            dcum = a_ref[pl.ds(r0, 128), :] - k_ref[pl.ds(r0, 128), :]
            dlf = _tri_dot(tri_rev, dcum) + suffix
            df = dlf * _sigmoid(-(f_ref[pl.ds(r0, 128), :] + b_ref[...]))
            df_ref[pl.ds(r0, 128), :] = df.astype(BF16)
            return dlf[0:1, :], dbsum + jnp.sum(df, axis=0, keepdims=True)

        _, dbsum = lax.fori_loop(0, nb, step, (jnp.zeros((1, 128), F32), jnp.zeros((1, 128), F32)))
        db_ref[...] = dbsum

    return pl.pallas_call(
        body, name=name, grid=(1,),
        in_specs=[_full_spec((s, 128))] * 3 + [_full_spec((1, 128))],
        out_specs=[_full_spec((s, 128)), _full_spec((1, 128))],
        out_shape=[jax.ShapeDtypeStruct((s, 128), BF16), jax.ShapeDtypeStruct((1, 128), F32)],
        compiler_params=_cparams(("arbitrary",)),
    )(dcq16, sum_q16, f_logit, b_f)


ATT_T = 256


def _head_lanes(rows):
    return lax.broadcasted_iota(I32, (rows, 128), 1) < HEAD_DIM


def _bf16_pieces(c):
    p0 = c.astype(BF16).astype(F32)
    r = c - p0
    p1 = r.astype(BF16).astype(F32)
    p2 = (r - p1).astype(BF16).astype(F32)
    return p0, p1, p2


def _col_reduce(x, op):
    rows = x.shape[0]
    while rows > 8:
        rows //= 2
        x = op(x[:rows], x[rows:])
    return jnp.max(x, axis=0, keepdims=True) if op is jnp.maximum else jnp.sum(x, axis=0, keepdims=True)


def _attn_prep(qkv, cqe, carry=None, *, name):
    s = qkv.shape[0]
    npair = HEADS // 2

    def body(q_ref, k_ref, v_ref, c_ref, qa_ref, ka_ref, vt_ref):
        rows = 128
        lane = lax.broadcasted_iota(I32, (rows, 128), 1)

        def chunk(n, _):
            r0 = pl.multiple_of(n * rows, rows)
            sl = pl.ds(r0, rows)
            qv = q_ref[sl, :].astype(F32) * ATT_SCALE
            kv = k_ref[sl, :].astype(F32)
            p0, p1, p2 = _bf16_pieces(pltpu.roll(c_ref[sl, :], HEAD_DIM, 1))
            for e in range(2):
                mine = (lane < HEAD_DIM) if e == 0 else (lane >= HEAD_DIM)
                base = HEAD_DIM * (1 - e)
                ones_hi = jnp.where((lane >= base + 3) & (lane < base + 6), 1.0, 0.0)
                ones_lo = jnp.where((lane >= base) & (lane < base + 3), 1.0, 0.0)
                qa = jnp.where(mine, qv, jnp.where(lane == base, p0, jnp.where(lane == base + 1, p1,
                               jnp.where(lane == base + 2, p2, ones_hi))))
                ka = jnp.where(mine, kv, jnp.where(lane == base + 3, -p0, jnp.where(lane == base + 4, -p1,
                               jnp.where(lane == base + 5, -p2, ones_lo))))
                qa_ref[e, sl, :] = qa.astype(BF16)
                ka_ref[e, sl, :] = ka.astype(BF16)
            vt_ref[0, :, sl] = v_ref[sl, :].astype(F32).T.astype(BF16)
            return 0

        lax.fori_loop(0, s // rows, chunk, 0)

    pair = pl.BlockSpec((2, s, 128), lambda hp: (hp, 0, 0))
    return _carry_call(
        body, carry, name=name, grid=(npair,),
        in_specs=[pl.BlockSpec((s, 128), lambda hp: (0, hp)), pl.BlockSpec((s, 128), lambda hp: (0, npair + hp)),
                  pl.BlockSpec((s, 128), lambda hp: (0, 2 * npair + hp)), pl.BlockSpec((s, 128), lambda hp: (0, hp))],
        out_specs=[pair, pair, pl.BlockSpec((1, 128, s), lambda hp: (hp, 0, 0))],
        out_shape=[jax.ShapeDtypeStruct((HEADS, s, 128), BF16), jax.ShapeDtypeStruct((HEADS, s, 128), BF16),
                   jax.ShapeDtypeStruct((npair, 128, s), BF16)],
        scratch_shapes=[], args=[qkv, qkv, qkv, cqe])


def _attn_fwd(qa, ka, vt, carry=None, *, name):
    s = qa.shape[1]
    t = 2 * ATT_T
    nq = s // t
    npair = HEADS // 2

    def body(qa_ref, ka_ref, vt_ref, o_ref, lse_ref):
        i = pl.program_id(1)
        krow = lax.broadcasted_iota(I32, (t, t), 0)
        qcol = lax.broadcasted_iota(I32, (t, t), 1)
        sub = lax.broadcasted_iota(I32, (128, t), 0)
        row8 = lax.broadcasted_iota(I32, (8, t), 0)
        qbs = (qa_ref[0], qa_ref[1])
        tk = t

        def step(j, carry, diag):
            c0 = pl.multiple_of(j * tk, tk)
            vtb = vt_ref[0, :, pl.ds(c0, tk)]
            sts = [_dot(ka_ref[e, pl.ds(c0, tk), :], qbs[e], NT) for e in range(2)]
            if diag:
                sts = [jnp.where(krow <= qcol, st, NEG) for st in sts]
            pts, stats = [], []
            for e in range(2):
                m, l, _ = carry[e]
                m_new = jnp.maximum(m, _col_reduce(sts[e], jnp.maximum))
                alpha = jnp.exp(m - m_new)
                pt = jnp.exp(sts[e] - m_new)
                stats.append((m_new, alpha, alpha * l + _col_reduce(pt, jnp.add)))
                pts.append(pt.astype(BF16))
            pvs = [_dot(vtb, pts[e], NN) for e in range(2)]
            return tuple((stats[e][0], stats[e][2], stats[e][1] * carry[e][2] + pvs[e]) for e in range(2))

        init = (jnp.full((1, t), NEG, F32), jnp.zeros((1, t), F32), jnp.zeros((128, t), F32))
        carry = lax.fori_loop(0, i, functools.partial(step, diag=False), (init, init))
        (m0, l0, acc0), (m1, l1, acc1) = step(i, carry, True)
        o_pair = jnp.where(sub < HEAD_DIM, acc0 / l0, acc1 / l1)
        o_ref[...] = o_pair.T.astype(BF16)
        lse_ref[0] = jnp.where(row8 == 0, m0 + jnp.log(l0), jnp.where(row8 == 1, m1 + jnp.log(l1), 0.0))

    return _carry_call(
        body, carry, name=name, grid=(npair, nq),
        in_specs=[pl.BlockSpec((2, t, 128), lambda hp, i: (hp, i, 0)), pl.BlockSpec((2, s, 128), lambda hp, i: (hp, 0, 0)),
                  pl.BlockSpec((1, 128, s), lambda hp, i: (hp, 0, 0))],
        out_specs=[pl.BlockSpec((t, 128), lambda hp, i: (i, hp)), pl.BlockSpec((1, 8, t), lambda hp, i: (hp, 0, i))],
        out_shape=[jax.ShapeDtypeStruct((s, HEADS * HEAD_DIM), BF16), jax.ShapeDtypeStruct((npair, 8, s), F32)],
        scratch_shapes=[], args=[qa, ka, vt])


def _attn_delta(do, o, carry=None, *, name):
    s = do.shape[0]

    def body(do_ref, o_ref, d_ref):
        prod = do_ref[...].astype(F32) * o_ref[...].astype(F32)
        row = lax.broadcasted_iota(I32, (8, 128), 0)
        lane = lax.broadcasted_iota(I32, (8, 128), 1)
        sel = ((row == 0) & (lane < HEAD_DIM) | (row == 1) & (lane >= HEAD_DIM)).astype(BF16)
        p0, p1, p2 = _bf16_pieces(prod)
        d_ref[0] = (_dot(sel, p0.astype(BF16), NT) + _dot(sel, p1.astype(BF16), NT)) + _dot(sel, p2.astype(BF16), NT)

    pair = pl.BlockSpec((s, 128), lambda hp: (0, hp))
    (delta3,), carried = _carry_call(
        body, carry, name=name, grid=(HEADS // 2,), in_specs=[pair, pair],
        out_specs=[pl.BlockSpec((1, 8, s), lambda hp: (hp, 0, 0))],
        out_shape=[jax.ShapeDtypeStruct((HEADS // 2, 8, s), F32)], scratch_shapes=[], args=[do, o])
    return delta3, carried


def _attn_bwd(qa, ka, qkv, do, lse3, delta3, carry=None, *, name):
    s = qa.shape[1]
    t = 2 * ATT_T
    nb = s // t
    npair = HEADS // 2

    def body(qa_ref, ka_ref, v_ref, do_ref, lse_ref, delta_ref, dq_ref, dk_ref, dv_ref, aux_ref, dcq_ref, dqt):
        hp = pl.program_id(0)
        first = _head_lanes(t)
        lane = lax.broadcasted_iota(I32, (t, 128), 1)
        dqt[...] = jnp.zeros_like(dqt)

        @pl.when(hp == 0)
        def _():
            aux_ref[...] = jnp.zeros_like(aux_ref)

        krow = lax.broadcasted_iota(I32, (t, t), 0)
        qcol = lax.broadcasted_iota(I32, (t, t), 1)

        def key_block(j, _):
            c0 = pl.multiple_of(j * t, t)
            vb = v_ref[pl.ds(c0, t), :]
            kbs = (ka_ref[0, pl.ds(c0, t), :], ka_ref[1, pl.ds(c0, t), :])
            kbts = tuple(kb.astype(F32).T.astype(BF16) for kb in kbs)
            vhs = (jnp.where(first, vb, jnp.zeros_like(vb)), jnp.where(first, jnp.zeros_like(vb), vb))

            def query_block(i, carry, diag):
                r0 = pl.multiple_of(i * t, t)
                dob = do_ref[pl.ds(r0, t), :]
                sts = [_dot(kbs[e], qa_ref[e, pl.ds(r0, t), :], NT) for e in range(2)]
                dpts = [_dot(vhs[e], dob, NT) for e in range(2)]
                ptbs, dsbs = [], []
                for e in range(2):
                    st = jnp.where(krow <= qcol, sts[e], NEG) if diag else sts[e]
                    pt = jnp.exp(st - lse_ref[0, e:e + 1, pl.ds(r0, t)])
                    dsbs.append((pt * (dpts[e] - delta_ref[0, e:e + 1, pl.ds(r0, t)])).astype(BF16))
                    ptbs.append(pt.astype(BF16))
                out = []
                for e in range(2):
                    dk_a, dv_a = carry[e]
                    dv_a = dv_a + _dot(ptbs[e], dob, NN)
                    dk_a = dk_a + _dot(dsbs[e], qa_ref[e, pl.ds(r0, t), :], NN)
                    dqt[e, :, pl.ds(r0, t)] += _dot(kbts[e], dsbs[e], NN)
                    out.append((dk_a, dv_a))
                return tuple(out)

            zero = jnp.zeros((t, 128), F32)
            carry = query_block(j, ((zero, zero), (zero, zero)), True)
            (dk0, dv0), (dk1, dv1) = lax.fori_loop(j + 1, nb, functools.partial(query_block, diag=False), carry)
            dk_ref[pl.ds(c0, t), :] = jnp.where(first, dk0, dk1).astype(BF16)
            dv_ref[pl.ds(c0, t), :] = jnp.where(first, dv0, dv1).astype(BF16)
            sum_q = jnp.where(lane == 2 * hp, dk0[:, HEAD_DIM + 3:HEAD_DIM + 4],
                              jnp.where(lane == 2 * hp + 1, dk1[:, 3:4], aux_ref[pl.ds(c0, t), :]))
            aux_ref[pl.ds(c0, t), :] = sum_q
            return 0

        lax.fori_loop(0, nb, key_block, 0)
        sub = lax.broadcasted_iota(I32, (128, s), 0)
        row8 = lax.broadcasted_iota(I32, (8, s), 0)
        dq_ref[...] = (jnp.where(sub < HEAD_DIM, dqt[0], dqt[1]) * ATT_SCALE).T.astype(BF16)
        dcq_ref[0] = jnp.where(row8 == 0, dqt[0, HEAD_DIM:HEAD_DIM + 1, :], jnp.where(row8 == 1, dqt[1, 0:1, :], 0.0))

    def pair_cols(off):
        return pl.BlockSpec((s, 128), lambda hp: (0, off + hp))

    heads = pl.BlockSpec((2, s, 128), lambda hp: (hp, 0, 0))
    rows = pl.BlockSpec((1, 8, s), lambda hp: (hp, 0, 0))
    wide = jax.ShapeDtypeStruct((s, HEADS * HEAD_DIM), BF16)
    return _carry_call(
        body, carry, name=name, grid=(npair,),
        in_specs=[heads, heads, pair_cols(2 * npair), pair_cols(0), rows, rows],
        out_specs=[pair_cols(0), pair_cols(0), pair_cols(0), pl.BlockSpec((s, 128), lambda hp: (0, 0)), rows],
        out_shape=[wide, wide, wide, jax.ShapeDtypeStruct((s, 128), F32), jax.ShapeDtypeStruct((npair, 8, s), F32)],
        scratch_shapes=[pltpu.VMEM((2, 128, s), F32)], args=[qa, ka, qkv, do, lse3, delta3])


def _adam_math(w, g, m, v):
    m = ADAM_B1 * m + (1.0 - ADAM_B1) * g
    v = ADAM_B2 * v + (1.0 - ADAM_B2) * (g * g)
    m_hat = m / (1.0 - ADAM_B1 ** ADAM_STEP)
    v_hat = v / (1.0 - ADAM_B2 ** ADAM_STEP)
    delta = -ADAM_LR * (m_hat / (jnp.sqrt(v_hat) + ADAM_EPS) + ADAM_WD * w)
    return delta, m, v


def _sum_pairs(keep, recv, pos, *, name):
    _, r, c = recv.shape
    tr = _row_tile(r, 512)

    def body(pos_ref, a_ref, b_ref, o32_ref, o16_ref):
        tot = a_ref[...].astype(F32) + b_ref[...].astype(F32)
        o16_ref[...] = tot.astype(BF16)

        @pl.when(pl.program_id(1) == 2 * pos_ref[0] + pos_ref[1])
        def _():
            o32_ref[...] = tot

    out = pl.BlockSpec((1, tr, c), lambda i, q, pos: (q, i, 0))
    grid_spec = pltpu.PrefetchScalarGridSpec(
        num_scalar_prefetch=1, grid=(r // tr, 4),
        in_specs=[pl.BlockSpec((1, tr, c), lambda i, q, pos: (2 * q + pos[2], i, 0)), out],
        out_specs=[pl.BlockSpec((1, tr, c), lambda i, q, pos: (0, i, 0)), out])
    return pl.pallas_call(
        body, name=name, grid_spec=grid_spec,
        out_shape=[jax.ShapeDtypeStruct((1, r, c), F32), jax.ShapeDtypeStruct((4, r, c), BF16)],
        compiler_params=_cparams(("arbitrary", "arbitrary")),
    )(pos, keep, recv)


def _adam_sharded(psum, recv, w, m, v, pos, *, name):
    r, c = w.shape
    rg = psum.shape[1]

    def body(pos_ref, p_ref, r_ref, w_ref, m_ref, v_ref, g_ref, d_ref, mo_ref, vo_ref):
        part = lambda ref, q: ref[q] if rg == r else ref[q, :r, :]
        g = part(p_ref, 0) + part(r_ref, 0).astype(F32) + part(r_ref, 1).astype(F32) + part(r_ref, 2).astype(F32)
        delta, mn, vn = _adam_math(w_ref[...], g, m_ref[...], v_ref[...])
        g_ref[...] = g
        d_ref[...] = delta
        mo_ref[...] = mn
        vo_ref[...] = vn

    if rg == r:
        tr = _row_tile(r, 320)
        grid = (r // tr,)
        row = pl.BlockSpec((tr, c), lambda i, pos: (i, 0))
        sums = lambda n: pl.BlockSpec((n, tr, c), lambda i, pos: (0, i, 0))
    else:
        tc = 256
        grid = (c // tc,)
        row = pl.BlockSpec((r, tc), lambda i, pos: (0, i))
        sums = lambda n: pl.BlockSpec((n, rg, tc), lambda i, pos: (0, 0, i))
    grid_spec = pltpu.PrefetchScalarGridSpec(
        num_scalar_prefetch=1, grid=grid, in_specs=[sums(1), sums(3), row, row, row], out_specs=[row, row, row, row])
    o = jax.ShapeDtypeStruct((r, c), F32)
    return pl.pallas_call(
        body, name=name, grid_spec=grid_spec, out_shape=[o, o, o, o],
        compiler_params=_cparams(("parallel",)),
    )(pos, psum, recv, w, m, v)


def _adam_replicated(chip_sums, last, w, m, v, *, name):
    r = w.shape[0]

    def body(s_ref, l_ref, w_ref, m_ref, v_ref, g_ref, d_ref, mo_ref, vo_ref):
        g = (((s_ref[0] + s_ref[1]) + s_ref[2]) + s_ref[3]) + l_ref[...]
        delta, mn, vn = _adam_math(w_ref[...], g, m_ref[...], v_ref[...])
        g_ref[...] = g
        d_ref[...] = delta
        mo_ref[...] = mn
        vo_ref[...] = vn

    o = jax.ShapeDtypeStruct((r, 1024), F32)
    full = _full_spec((r, 1024))
    return pl.pallas_call(
        body, name=name, grid=(1,),
        in_specs=[_full_spec((4, r, 1024)), full, full, full, full], out_specs=[full] * 4, out_shape=[o] * 4,
        compiler_params=_cparams(("arbitrary",)),
    )(chip_sums, last, w, m, v)


ASM_OUT = 256
ASM_SRC = 304


def _w_in_row(r):
    return r if r < 2048 else (r + O_G - 2048 if r < 4096 else r - 2048)


def _assemble_wt_main(g, *, name):
    table = []
    for blk in range(MAIN_COLS // ASM_OUT):
        j, l0 = divmod(_w_in_row(blk * ASM_OUT), IN_SHARD)
        sb = l0 // ASM_SRC
        n_a = min(ASM_OUT, min(IN_SHARD, (sb + 1) * ASM_SRC) - l0)
        if n_a == ASM_OUT:
            nxt = (j, sb)
        elif l0 + n_a == IN_SHARD:
            nxt = (j + 1, 0)
        else:
            nxt = (j, sb + 1)
        table.append((j, sb, l0 - sb * ASM_SRC, n_a) + nxt)

    def body(tab_ref, a_ref, b_ref, o_ref):
        blk = pl.program_id(0)
        off, n_a = tab_ref[blk, 2], tab_ref[blk, 3]
        r = lax.broadcasted_iota(I32, (ASM_OUT, ASM_SRC), 0)
        k = lax.broadcasted_iota(I32, (ASM_OUT, ASM_SRC), 1)
        sel_a = ((k == r + off) & (r < n_a)).astype(BF16)
        sel_b = ((k == r - n_a) & (r >= n_a)).astype(BF16)
        o_ref[...] = (_dot(sel_a, a_ref[0], NN) + _dot(sel_b, b_ref[0], NN)).astype(BF16)

    src = lambda c: pl.BlockSpec((1, ASM_SRC, D_MODEL), lambda blk, tab: (tab[blk, c], tab[blk, c + 1], 0))
    grid_spec = pltpu.PrefetchScalarGridSpec(
        num_scalar_prefetch=1, grid=(len(table),), in_specs=[src(0), src(4)],
        out_specs=pl.BlockSpec((ASM_OUT, D_MODEL), lambda blk, tab: (blk, 0)))
    return pl.pallas_call(
        body, name=name, grid_spec=grid_spec, out_shape=jax.ShapeDtypeStruct((MAIN_COLS, D_MODEL), BF16),
        compiler_params=_cparams(("parallel",)),
    )(jnp.asarray(table, I32), g, g)


def _pair_sum_small(mine, theirs, *, name):
    def body(a_ref, b_ref, o_ref):
        o_ref[...] = a_ref[...] + b_ref[...]

    full = _full_spec(mine.shape)
    return pl.pallas_call(
        body, name=name, grid=(1,), in_specs=[full, full], out_specs=full,
        out_shape=jax.ShapeDtypeStruct(mine.shape, F32), compiler_params=_cparams(("arbitrary",)),
    )(mine, theirs)


ANY = pl.BlockSpec(memory_space=pl.ANY)
OTHER_CHIPS = ((1, 0), (0, 1), (1, 1))


class _Carry:
    def __init__(self, inputs, out_shapes, scratch, start, wait, aliases=None):
        self.inputs, self.out_shapes, self.scratch = list(inputs), list(out_shapes), list(scratch)
        self.start, self.wait, self.aliases = start, wait, dict(aliases or {})


def _carry_join(*carries):
    n_in = [len(c.inputs) for c in carries]
    n_out = [len(c.out_shapes) for c in carries]
    n_scr = [len(c.scratch) for c in carries]

    def split(refs, counts):
        out, k = [], 0
        for n in counts:
            out.append(refs[k:k + n])
            k += n
        return out

    def start(ins, outs, scr):
        for c, i, o, s in zip(carries, split(ins, n_in), split(outs, n_out), split(scr, n_scr)):
            c.start(i, o, s)

    def wait(ins, outs, scr):
        for c, i, o, s in zip(carries, split(ins, n_in), split(outs, n_out), split(scr, n_scr)):
            c.wait(i, o, s)

    aliases = {}
    for k, c in enumerate(carries):
        aliases.update({sum(n_in[:k]) + i: sum(n_out[:k]) + o for i, o in c.aliases.items()})
    joined = _Carry(sum((c.inputs for c in carries), []), sum((c.out_shapes for c in carries), []),
                    sum((c.scratch for c in carries), []), start, wait, aliases)
    joined.counts = n_out
    joined.split = lambda results: split(results, n_out)
    return joined


def _carried(body, carry, n_in, n_out, grid):
    if carry is None:
        return body
    ci, co, cs = len(carry.inputs), len(carry.out_shapes), len(carry.scratch)

    def wrapped(*refs):
        ins, cins = refs[:n_in], refs[n_in:n_in + ci]
        outs, couts = refs[n_in + ci:n_in + ci + n_out], refs[n_in + ci + n_out:n_in + ci + n_out + co]
        rest = refs[n_in + ci + n_out + co:]
        scratch, cscr = rest[:len(rest) - cs], rest[len(rest) - cs:]
        first, last = None, None
        for axis, size in enumerate(grid):
            f, l = pl.program_id(axis) == 0, pl.program_id(axis) == size - 1
            first = f if first is None else first & f
            last = l if last is None else last & l

        @pl.when(first)
        def _():
            carry.start(cins, couts, cscr)

        body(*ins, *outs, *scratch)

        @pl.when(last)
        def _():
            carry.wait(cins, couts, cscr)

    return wrapped


def _carry_call(body, carry, *, name, grid, in_specs, out_specs, out_shape, scratch_shapes, args, vmem=True,
                own_aliases=None):
    n_in, n_out = len(in_specs), len(out_specs)
    extra_in = [ANY] * len(carry.inputs) if carry else []
    extra_out = [ANY] * len(carry.out_shapes) if carry else []
    aliases = dict(own_aliases or {})
    if carry:
        aliases.update({n_in + i: n_out + o for i, o in carry.aliases.items()})
    out = pl.pallas_call(
        _carried(body, carry, n_in, n_out, grid), name=name, grid=grid,
        in_specs=list(in_specs) + extra_in, out_specs=list(out_specs) + extra_out,
        out_shape=list(out_shape) + (carry.out_shapes if carry else []),
        scratch_shapes=list(scratch_shapes) + (carry.scratch if carry else []),
        input_output_aliases=aliases,
        compiler_params=_cparams(("arbitrary",) * len(grid)) if vmem else None,
    )(*args, *(carry.inputs if carry else []))
    return list(out[:n_out]), list(out[n_out:])


def _run_carry(carry, *, name):
    return _carry_call(lambda: None, carry, name=name, grid=(1,), in_specs=[], out_specs=[], out_shape=[],
                       scratch_shapes=[], args=[], vmem=False)[1]


def _sems(n):
    return [pltpu.SemaphoreType.DMA((n,)), pltpu.SemaphoreType.DMA((n,))]


def _carry_gather1(shards):
    n = len(shards)

    def copies(x_refs, out_refs, scr, with_arrivals):
        send_sems, recv_sems, local_sems = scr
        x, y, c = lax.axis_index("x"), lax.axis_index("y"), lax.axis_index("c")
        peers = [(x, y, 1 - c)] + [(x ^ fx, y ^ fy, c) for fx, fy in OTHER_CHIPS]
        local, sends, arrivals = [], [], []
        for t, (x_ref, out_ref) in enumerate(zip(x_refs, out_refs)):
            local.append(pltpu.make_async_copy(x_ref, out_ref.at[4 * x + 2 * y + c], local_sems.at[t]))
            for k, (px, py, pc) in enumerate(peers):
                sems = dict(send_sem=send_sems.at[4 * t + k], recv_sem=recv_sems.at[4 * t + k],
                            device_id=(px, py, pc), device_id_type=MESH)
                sends.append(pltpu.make_async_remote_copy(src_ref=x_ref, dst_ref=out_ref.at[4 * x + 2 * y + c], **sems))
                if with_arrivals:
                    arrivals.append(
                        pltpu.make_async_remote_copy(src_ref=x_ref, dst_ref=out_ref.at[4 * px + 2 * py + pc], **sems))
        return local, sends, arrivals

    def start(x_refs, out_refs, scr):
        local, sends, _ = copies(x_refs, out_refs, scr, False)
        for cp in local + sends:
            cp.start()

    def wait(x_refs, out_refs, scr):
        local, sends, arrivals = copies(x_refs, out_refs, scr, True)
        for cp in arrivals:
            cp.wait_recv()
        for cp in sends:
            cp.wait_send()
        for cp in local:
            cp.wait()

    return _Carry(shards, [jax.ShapeDtypeStruct((N_DEV,) + a.shape, a.dtype) for a in shards],
                  _sems(4 * n) + [pltpu.SemaphoreType.DMA((n,))], start, wait)


def _carry_gather2(gathered):
    n = len(gathered)

    def copies(in_refs, g_refs, scr, with_arrivals):
        send_sems, recv_sems = scr
        x, y, c = lax.axis_index("x"), lax.axis_index("y"), lax.axis_index("c")
        sends, arrivals = [], []
        for t in range(n):
            for j, (fx, fy) in enumerate(OTHER_CHIPS):
                px, py = x ^ fx, y ^ fy
                sems = dict(send_sem=send_sems.at[3 * t + j], recv_sem=recv_sems.at[3 * t + j],
                            device_id=(x, y, 1 - c), device_id_type=MESH)
                mine, theirs = 4 * px + 2 * py + c, 4 * px + 2 * py + (1 - c)
                sends.append(pltpu.make_async_remote_copy(src_ref=in_refs[t].at[mine], dst_ref=g_refs[t].at[mine], **sems))
                if with_arrivals:
                    arrivals.append(pltpu.make_async_remote_copy(
                        src_ref=in_refs[t].at[mine], dst_ref=g_refs[t].at[theirs], **sems))
        return sends, arrivals

    def start(in_refs, g_refs, scr):
        for cp in copies(in_refs, g_refs, scr, False)[0]:
            cp.start()

    def wait(in_refs, g_refs, scr):
        sends, arrivals = copies(in_refs, g_refs, scr, True)
        for cp in arrivals:
            cp.wait_recv()
        for cp in sends:
            cp.wait_send()

    return _Carry(gathered, [jax.ShapeDtypeStruct(a.shape, a.dtype) for a in gathered], _sems(3 * n), start, wait,
                  aliases={t: t for t in range(n)})


def _allreduce_rows(x, *, name):
    def body(x_ref, o_ref, sib_ref, mine_ref, tab_ref, send_sems, recv_sems):
        x, y, c = lax.axis_index("x"), lax.axis_index("y"), lax.axis_index("c")
        swap = pltpu.make_async_remote_copy(src_ref=x_ref, dst_ref=sib_ref, send_sem=send_sems.at[0],
                                            recv_sem=recv_sems.at[0], device_id=(x, y, 1 - c), device_id_type=MESH)
        swap.start()
        swap.wait()
        mine_ref[...] = x_ref[...] + sib_ref[...]
        tab_ref[pl.ds(2 * x + y, 1)] = mine_ref[...][None]

        def copy(k, slot):
            fx, fy = OTHER_CHIPS[k]
            return pltpu.make_async_remote_copy(
                src_ref=mine_ref, dst_ref=tab_ref.at[slot], send_sem=send_sems.at[1 + k], recv_sem=recv_sems.at[1 + k],
                device_id=(x ^ fx, y ^ fy, c), device_id_type=MESH)

        for k in range(3):
            copy(k, 2 * x + y).start()
        for k, (fx, fy) in enumerate(OTHER_CHIPS):
            copy(k, 2 * (x ^ fx) + (y ^ fy)).wait()
        o_ref[...] = ((tab_ref[0] + tab_ref[1]) + tab_ref[2]) + tab_ref[3]

    vmem = pl.BlockSpec(memory_space=pltpu.VMEM)
    return pl.pallas_call(
        body, name=name, out_shape=jax.ShapeDtypeStruct(x.shape, F32), in_specs=[vmem], out_specs=vmem,
        scratch_shapes=[pltpu.VMEM(x.shape, F32), pltpu.VMEM(x.shape, F32), pltpu.VMEM((4,) + x.shape, F32)] + _sems(4),
    )(x)


def _allgather(shards, *, name):
    n = len(shards)

    def body(*refs):
        x_refs, out_refs = refs[:n], refs[n:2 * n]
        send_sems, recv_sems, local_sems = refs[2 * n:]
        x, y, c = lax.axis_index("x"), lax.axis_index("y"), lax.axis_index("c")
        me, sibling = (x, y, c), (x, y, 1 - c)
        chips = [(x ^ fx, y ^ fy) for fx, fy in OTHER_CHIPS]

        def copy(t, k, block, to, from_input=False):
            px, py, pc = block
            slab = out_refs[t].at[4 * px + 2 * py + pc]
            return pltpu.make_async_remote_copy(
                src_ref=x_refs[t] if from_input else slab, dst_ref=slab,
                send_sem=send_sems.at[7 * t + k], recv_sem=recv_sems.at[7 * t + k], device_id=to, device_id_type=MESH)

        mine = [pltpu.make_async_copy(x_refs[t], out_refs[t].at[4 * x + 2 * y + c], local_sems.at[t]) for t in range(n)]
        for cp in mine:
            cp.start()
        first = []
        for t in range(n):
            first.append(copy(t, 0, me, sibling, from_input=True))
            first += [copy(t, 1 + j, me, (*chip, c), from_input=True) for j, chip in enumerate(chips)]
        for cp in first:
            cp.start()
        passed = []
        for j, chip in enumerate(chips):
            for t in range(n):
                copy(t, 1 + j, (*chip, c), me).wait_recv()
                fwd = copy(t, 4 + j, (*chip, c), sibling)
                fwd.start()
                passed.append(fwd)
        for t in range(n):
            copy(t, 0, sibling, me).wait_recv()
            for j, chip in enumerate(chips):
                copy(t, 4 + j, (*chip, 1 - c), me).wait_recv()
        for cp in first + passed:
            cp.wait_send()
        for cp in mine:
            cp.wait()

    return pl.pallas_call(
        body, name=name, out_shape=[jax.ShapeDtypeStruct((N_DEV,) + a.shape, a.dtype) for a in shards],
        in_specs=[ANY] * n, out_specs=[ANY] * n,
        scratch_shapes=[pltpu.SemaphoreType.DMA((7 * n,)), pltpu.SemaphoreType.DMA((7 * n,)),
                        pltpu.SemaphoreType.DMA((n,))],
    )(*shards)


def _carry_sibling(slabs, small=None):
    n = len(slabs)
    extra = [] if small is None else [small]

    def copies(in_refs, out_refs, scr):
        send_sems, recv_sems = scr
        x, y, c = lax.axis_index("x"), lax.axis_index("y"), lax.axis_index("c")
        sibling = (x, y, 1 - c)
        out = []
        for t in range(n):
            for q in range(4):
                out.append(pltpu.make_async_remote_copy(
                    src_ref=in_refs[t].at[2 * q + (1 - c)], dst_ref=out_refs[t].at[q],
                    send_sem=send_sems.at[4 * t + q], recv_sem=recv_sems.at[4 * t + q],
                    device_id=sibling, device_id_type=MESH))
        if extra:
            out.append(pltpu.make_async_remote_copy(
                src_ref=in_refs[n], dst_ref=out_refs[n], send_sem=send_sems.at[4 * n], recv_sem=recv_sems.at[4 * n],
                device_id=sibling, device_id_type=MESH))
        return out

    def start(*refs):
        for cp in copies(*refs):
            cp.start()

    def wait(*refs):
        for cp in copies(*refs):
            cp.wait()

    return _Carry(list(slabs) + extra,
                  [jax.ShapeDtypeStruct((4,) + a.shape[1:], a.dtype) for a in slabs]
                  + [jax.ShapeDtypeStruct(a.shape, a.dtype) for a in extra], _sems(4 * n + 1), start, wait)


def _carry_chips(psums, small_sum=None):
    n = len(psums)
    table = small_sum is not None

    def copies(in_refs, out_refs, scr, arrivals):
        send_sems, recv_sems = scr[0], scr[1]
        x, y, c = lax.axis_index("x"), lax.axis_index("y"), lax.axis_index("c")
        out = []
        for k, (fx, fy) in enumerate(OTHER_CHIPS):
            px, py = x ^ fx, y ^ fy
            for t in range(n):
                out.append(pltpu.make_async_remote_copy(
                    src_ref=in_refs[t].at[2 * px + py], dst_ref=out_refs[t].at[k],
                    send_sem=send_sems.at[3 * t + k], recv_sem=recv_sems.at[3 * t + k],
                    device_id=(px, py, c), device_id_type=MESH))
            if table:
                slot = 2 * px + py if arrivals else 2 * x + y
                out.append(pltpu.make_async_remote_copy(
                    src_ref=in_refs[n], dst_ref=out_refs[n].at[slot], send_sem=send_sems.at[3 * n + k],
                    recv_sem=recv_sems.at[3 * n + k], device_id=(px, py, c), device_id_type=MESH))
        return out

    def own(in_refs, out_refs, scr):
        x, y = lax.axis_index("x"), lax.axis_index("y")
        return pltpu.make_async_copy(in_refs[n], out_refs[n].at[2 * x + y], scr[2])

    def start(in_refs, out_refs, scr):
        if table:
            own(in_refs, out_refs, scr).start()
        for cp in copies(in_refs, out_refs, scr, False):
            cp.start()

    def wait(in_refs, out_refs, scr):
        for cp in copies(in_refs, out_refs, scr, True):
            cp.wait()
        if table:
            own(in_refs, out_refs, scr).wait()

    out_shapes = [jax.ShapeDtypeStruct((3,) + a.shape[1:], a.dtype) for a in psums]
    if table:
        out_shapes.append(jax.ShapeDtypeStruct((4,) + small_sum.shape, F32))
    return _Carry(list(psums) + ([small_sum] if table else []), out_shapes,
                  _sems(3 * n + 3) + ([pltpu.SemaphoreType.DMA] if table else []), start, wait)


def _to_comm(name, kind, block, dtype=BF16):
    a = block[0]
    if kind == "cols":
        a = a.T
        if name == "w_in" and dtype == BF16:
            a = jnp.pad(a, ((0, IN_SHARD_PAD - IN_SHARD), (0, 0)))
    return a if kind == "f32" else a.astype(dtype)


def _from_comm(name, kind, a):
    if kind == "cols":
        if name == "w_in" and a.shape[0] != IN_SHARD:
            a = a[:IN_SHARD]
        a = a.T
    return a[None]


def _assemble_weights(g):
    out = {}
    if "w_in" in g:
        out["wt_main"] = _assemble_wt_main(g["w_in"], name="assemble_w_in")
        j, l0 = divmod(O_F, IN_SHARD)
        out["wt_f"] = jnp.pad(g["w_in"][j, l0:l0 + HEADS], ((0, 128 - HEADS), (0, 0)))
    square = dict(w_branch_a="w_a", w_branch_b="w_b", w_out="w_out", w_ple_gate="w_pg")
    for long, short in square.items():
        if long in g:
            out[short] = g[long].reshape(D_MODEL, D_MODEL)
    if "w_up" in g:
        out["wt_up"] = g["w_up"].reshape(2 * D_FF, D_MODEL)
    if "conv_w" in g:
        out["conv_w"] = g["conv_w"].transpose(1, 0, 2).reshape(3, 2 * D_FF)
    if "w_down" in g:
        out["w_down"] = g["w_down"].reshape(D_FF, D_MODEL)
    if "w_ple" in g:
        out["wt_ple"] = g["w_ple"].reshape(D_MODEL, PLE_DIM)
    return out


def _grad_slabs(gr):
    out = {}
    if "wt_main" in gr:
        gm, gf = gr["wt_main"], gr["wt_f"]
        segments = ((0, 2048, gm, 0), (2048, O_F, gm, 2048), (O_F, O_G, gf, -O_F), (O_G, IN_COLS, gm, 2048 - O_G))
        slabs = []
        for j in range(N_DEV):
            lo, hi = j * IN_SHARD, (j + 1) * IN_SHARD
            pieces = [src[max(lo, a) + shift:min(hi, b) + shift] for a, b, src, shift in segments if max(lo, a) < min(hi, b)]
            pieces.append(jnp.zeros((IN_SHARD_PAD - IN_SHARD, D_MODEL), gm.dtype))
            slabs.append(jnp.concatenate(pieces, axis=0))
        out["w_in"] = jnp.stack(slabs)
    rows = dict(w_a="w_branch_a", w_b="w_branch_b", w_out="w_out", wt_up="w_up", w_down="w_down", w_pg="w_ple_gate")
    for short, long in rows.items():
        if short in gr:
            out[long] = gr[short].reshape(N_DEV, -1, D_MODEL)
    if "conv_w" in gr:
        out["conv_w"] = gr["conv_w"].reshape(3, N_DEV, -1).transpose(1, 0, 2)
    if "wt_ple" in gr:
        out["w_ple"] = gr["wt_ple"].reshape(N_DEV, -1, PLE_DIM)
    return {k: v.astype(BF16) for k, v in out.items()}


def _rows(a, rows):
    flat = a.reshape(-1)
    return jnp.pad(flat, (0, rows * 1024 - flat.shape[0])).reshape(rows, 1024)


def _pack_small(parts):
    return jnp.concatenate([_rows(parts[n].astype(F32), r) for n, r in SMALL], axis=0)


def _small(packed, name, shape):
    off, r = SMALL_OFF[name]
    n = math.prod(shape)
    return packed[off:off + r].reshape(-1)[:n].reshape(shape)


class _Exchanges:
    W_S_ROWS = SMALL_OFF["gmlp_w_s"]

    def __init__(self, later, shards, pos):
        self.later, self.shards, self.pos = later, dict(zip(later, shards)), pos
        self.level1, self.slabs, self.from_sib, self.sums32, self.reduced, self.tables = {}, {}, {}, {}, {}, {}

    def gather1(self, names):
        carry = _carry_gather1([self.shards[n] for n in names])
        carry.names = names
        return carry

    def gather1_done(self, carry, results):
        self.level1.update(zip(carry.names, results))

    def gather2(self):
        return _carry_gather2([self.level1[n] for n in self.later])

    def weights(self, full):
        return _assemble_weights(dict(zip(self.later, full)))

    def sibling(self, grads):
        slabs = _grad_slabs(grads)
        self.slabs.update(slabs)
        carry = _carry_sibling(list(slabs.values()))
        carry.names = list(slabs)
        return carry

    def sibling_done(self, carry, results):
        self.from_sib.update(zip(carry.names, results))

    def chips(self, names, table=None):
        sums = {n: _sum_pairs(self.slabs[n], self.from_sib[n], self.pos, name="sum_sibling_" + n) for n in names}
        self.sums32.update({n: s32 for n, (s32, _) in sums.items()})
        carry = _carry_chips([s16 for _, s16 in sums.values()], None if table is None else self.table_part(table))
        carry.names, carry.table = list(names), table
        return carry

    def chips_done(self, carry, results):
        if carry.table is not None:
            *results, self.tables[carry.table] = results
        self.reduced.update({n: (self.sums32[n], r) for n, r in zip(carry.names, results)})

    def sibling_small(self, small_g):
        self.small_g = small_g
        return _carry_sibling([], small_g)

    def sibling_small_done(self, small_sib):
        self.small_chip = _pair_sum_small(self.small_g, small_sib, name="sum_sibling_small")

    def table_part(self, which):
        off, rows = self.W_S_ROWS
        if which == "w_s":
            return self.small_chip[off:off + rows]
        return jnp.concatenate([self.small_chip[:off], self.small_chip[off + rows:]], axis=0)

    def table(self):
        off = self.W_S_ROWS[0]
        rest = self.tables["rest"]
        return jnp.concatenate([rest[:, :off], self.tables["w_s"], rest[:, off:]], axis=1)


def _local_step(x, p, target, w, sm, ex=None):
    s = x.shape[0]
    mm = _matmul
    wt_main = w["wt_main"]
    conv_b = sm["conv_b"]
    bs_t = jnp.pad(sm["gmlp_b_s"].T, ((0, 0), (0, 128 - GROUPS)))
    b_f = jnp.pad(sm["b_f"], ((0, 0), (0, 128 - HEADS)))
    big = dict(tm=1024, tn=1024, tk=1024)
    whole_s = dict(tn=1024, tk=s)

    h = _rmsnorm_fwd(x, sm["norm_mix_g"], name="norm_mix")
    tall = dict(tm=s, tn=512, tk=1024)
    qkv_args = dict(mode="nt", out_dtype=BF16, name="in_qkv", n=3072, b_off=8, **tall)
    f_logit = mm(h, w["wt_f"], mode="nt", out_dtype=F32, name="in_f", tm=1024, tk=1024)
    cqe = _forget_cumsum(f_logit, b_f, name="forget_cumsum")
    uvg = dict(mode="nt", out_dtype=F32, name="in_uvg", n=4096, **tall)
    if ex is None:
        qkv = mm(h, wt_main, **qkv_args)
        (qa, ka, vt), _ = _attn_prep(qkv, cqe, name="attn_prep")
        (b, lse3), _ = _attn_fwd(qa, ka, vt, name="attn_fwd")
        zuvg = mm(h, wt_main, **uvg)
    else:
        groups = (["w_branch_a"], ["w_branch_b"], [n for n in ex.later if n not in ("w_branch_a", "w_branch_b")])
        carries = [ex.gather1(names) for names in groups]
        qkv, got0 = mm(h, wt_main, carry=carries[0], **qkv_args)
        (qa, ka, vt), got1 = _attn_prep(qkv, cqe, carries[1], name="attn_prep")
        (b, lse3), got2 = _attn_fwd(qa, ka, vt, carries[2], name="attn_fwd")
        for carry, got in zip(carries, (got0, got1, got2)):
            ex.gather1_done(carry, got)
        zuvg, full = mm(h, wt_main, carry=ex.gather2(), **uvg)
        w = {**w, **ex.weights(full)}
    a = _gmlp_fwd(zuvg, sm["gmlp_ln_g"], sm["gmlp_ln_b"], sm["gmlp_w_s"], bs_t, name="gmlp_fwd")
    wt_up, conv_w = w["wt_up"], w["conv_w"]
    ya, yb, merged = _branches_merge(a, b, w["w_a"], w["w_b"], zuvg, name="branches_merge")
    x1, h2 = mm(merged, w["w_out"], mode="nn", out_dtype=F32, name="out_proj", add=x, norm_g=sm["norm_ffn_g"], **big)
    up_a, up_g, act = _up_convglu(h2, wt_up, conv_w, conv_b, name="up_convglu")
    x2, h3 = mm(act, w["w_down"], mode="nn", out_dtype=F32, name="down", tm=1024, tn=1024, tk=1408, add=x1,
                norm_g=sm["norm_ple_g"])

    loss, dx3, dple, dgp, d_norm_final = _ple_loss(p, w["wt_ple"], h3, w["w_pg"], x2, target, sm["norm_final_g"],
                                                   name="ple_loss")
    g_wt_ple = mm(dple, p, mode="tn", out_dtype=BF16, name="d_w_ple", tm=512, tn=256, tk=s)
    g_w_pg = mm(h3, dgp, mode="tn", out_dtype=BF16, name="d_w_pg", tm=512, **whole_s)
    (dx2, dx2b, d_norm_ple), _ = _matmul_rmsnorm_bwd([dgp], w["w_pg"], dx3, x2, sm["norm_ple_g"], mode="nt", tk=1024,
                                                     name="d_h3_norm_ple_bwd")
    g_w_down = mm(act, dx2b, mode="tn", out_dtype=BF16, name="d_w_down", tm=1408, **whole_s)
    dact_args = dict(mode="nt", out_dtype=BF16, name="d_act", tm=s, tn=256, tk=1024)
    if ex is None:
        dact = mm(dx2b, w["w_down"], **dact_args)
    else:
        early = ex.sibling(dict(w_pg=g_w_pg, wt_ple=g_wt_ple))
        dact, got = mm(dx2b, w["w_down"], carry=early, **dact_args)
        ex.sibling_done(early, got)
    dup_a, dup_g, dcw_a, dcw_g, dcb_a, dcb_g = _convglu_bwd(dact, up_a, up_g, conv_w, conv_b, name="convglu_bwd")
    g_wt_up = mm(dup_a, h2, mode="tn", out_dtype=BF16, name="d_w_up_a", tm=1408, out_rows=2 * D_FF, **whole_s)
    g_wt_up = mm(dup_g, h2, mode="tn", out_dtype=BF16, name="d_w_up_g", tm=1408, out_rows=2 * D_FF,
                 o_off=D_FF // 1408, into=g_wt_up, **whole_s)
    (dx1, dx1b, d_norm_ffn), _ = _matmul_rmsnorm_bwd([dup_a, dup_g], wt_up, dx2, x1, sm["norm_ffn_g"], mode="nn",
                                                     tk=1408, name="d_h2_norm_ffn_bwd", resident=True)
    g_w_out = mm(merged, dx1b, mode="tn", out_dtype=BF16, name="d_w_out", tm=512, **whole_s)
    dya, dyb, dga, dgb = _merge_bwd(dx1b, w["w_out"], ya, yb, zuvg, name="merge_bwd")
    g_w_a = mm(a, dya, mode="tn", out_dtype=BF16, name="d_w_a", tm=512, **whole_s)
    g_w_b = mm(b, dyb, mode="tn", out_dtype=BF16, name="d_w_b", tm=512, **whole_s)
    da = mm(dya, w["w_a"], mode="nt", out_dtype=BF16, name="d_a", **big)
    db = mm(dyb, w["w_b"], mode="nt", out_dtype=BF16, name="d_b", **big)
    grads = dict(w_a=g_w_a, w_b=g_w_b, w_out=g_w_out, wt_up=g_wt_up, conv_w=jnp.concatenate([dcw_a, dcw_g], axis=1),
                 w_down=g_w_down, wt_ple=g_wt_ple, w_pg=g_w_pg)
    gmlp_args = (da, zuvg, sm["gmlp_ln_g"], sm["gmlp_ln_b"], sm["gmlp_w_s"], bs_t)
    if ex is None:
        (dzu, dzv, d_w_s, d_bs_t, d_ln_g, d_ln_b), _ = _gmlp_bwd(*gmlp_args, name="gmlp_bwd")
    else:
        rest = ex.sibling({k: v for k, v in grads.items() if k not in ("w_pg", "wt_ple")})
        early_chips = ex.chips(early.names)
        both = _carry_join(rest, early_chips)
        (dzu, dzv, d_w_s, d_bs_t, d_ln_g, d_ln_b), got = _gmlp_bwd(*gmlp_args, both, name="gmlp_bwd")
        got_rest, got_early = both.split(got)
        ex.sibling_done(rest, got_rest)
        ex.chips_done(early_chips, got_early)
    small = dict(norm_mix_g=jnp.zeros((1, D_MODEL), F32), b_f=jnp.zeros((1, HEADS), F32), gmlp_ln_g=d_ln_g,
                 gmlp_ln_b=d_ln_b, gmlp_w_s=d_w_s, gmlp_b_s=d_bs_t[:, :GROUPS].T, norm_ffn_g=d_norm_ffn,
                 conv_b=jnp.concatenate([dcb_a, dcb_g], axis=1), norm_ple_g=d_norm_ple, norm_final_g=d_norm_final)
    if ex is None:
        delta3, _ = _attn_delta(db, b, name="attn_delta")
        (dq, dk, dv, aux, dcq3), _ = _attn_bwd(qa, ka, qkv, db, lse3, delta3, name="attn_bwd")
    else:
        delta3, (small_sib,) = _attn_delta(db, b, ex.sibling_small(_pack_small(small)), name="attn_delta")
        ex.sibling_small_done(small_sib)
        main_chips = ex.chips(rest.names, table="rest")
        (dq, dk, dv, aux, dcq3), got = _attn_bwd(qa, ka, qkv, db, lse3, delta3, main_chips, name="attn_bwd")
        ex.chips_done(main_chips, got)
    dcq16 = jnp.pad(dcq3[:, :2, :].reshape(HEADS, s).T, ((0, 0), (0, 128 - HEADS)))
    dzf, d_b_f = _forget_bwd(dcq16, aux, f_logit, b_f, name="forget_bwd")
    dz_parts = [dzu, dzv, dga, dgb, dq, dk, dv]
    w_s_chips = None if ex is None else ex.chips([], table="w_s")
    g_wt_main, got = _grad_w_parts(dz_parts, h, name="d_w_main", tm=512, carry=w_s_chips)
    if ex is not None:
        ex.chips_done(w_s_chips, got)
    g_wt_f = mm(dzf, h, mode="tn", out_dtype=BF16, name="d_w_f", **whole_s)
    grads = dict(grads, wt_main=g_wt_main, wt_f=g_wt_f)
    w_in_chips = None
    if ex is not None:
        w_in_sib = ex.sibling(dict(wt_main=g_wt_main, wt_f=g_wt_f))
        ex.sibling_done(w_in_sib, _run_carry(w_in_sib, name="exchange_sibling_w_in"))
        w_in_chips = ex.chips(w_in_sib.names)
    (dx0, _, d_norm_mix), got = _matmul_rmsnorm_bwd(dz_parts, wt_main, dx1, x, sm["norm_mix_g"], mode="nn", tk=1024,
                                                    extra=(dzf, w["wt_f"]), name="d_h_norm_mix_bwd", carry=w_in_chips,
                                                    lead=True)
    if ex is not None:
        ex.chips_done(w_in_chips, got)
    return loss, dx0, grads, dict(small, norm_mix_g=d_norm_mix, b_f=d_b_f[:, :HEADS])


def kernel(x, p, norm_mix_g, w_in, b_f, gmlp_ln_g, gmlp_ln_b, gmlp_w_s, gmlp_b_s, w_branch_a, w_branch_b, w_out, norm_ffn_g, w_up, conv_w, conv_b, w_down, norm_ple_g, w_ple, w_ple_gate, norm_final_g, loss_target, m_norm_mix_g, m_w_in, m_b_f, m_gmlp_ln_g, m_gmlp_ln_b, m_gmlp_w_s, m_gmlp_b_s, m_w_branch_a, m_w_branch_b, m_w_out, m_norm_ffn_g, m_w_up, m_conv_w, m_conv_b, m_w_down, m_norm_ple_g, m_w_ple, m_w_ple_gate, m_norm_final_g, v_norm_mix_g, v_w_in, v_b_f, v_gmlp_ln_g, v_gmlp_ln_b, v_gmlp_w_s, v_gmlp_b_s, v_w_branch_a, v_w_branch_b, v_w_out, v_norm_ffn_g, v_w_up, v_conv_w, v_conv_b, v_w_down, v_norm_ple_g, v_w_ple, v_w_ple_gate, v_norm_final_g):
    given = dict(locals())
    weights = {n: given[n] for n in WEIGHT_ORDER}
    mom_m = {n: given["m_" + n] for n in WEIGHT_ORDER}
    mom_v = {n: given["v_" + n] for n in WEIGHT_ORDER}
    pos = jnp.stack([lax.axis_index("x"), lax.axis_index("y"), lax.axis_index("c")]).astype(I32)
    names = [n for n, _ in SHARDED]
    kinds = dict(SHARDED)

    later = [n for n in names if n != "w_in"]

    first = _allgather([_to_comm("w_in", kinds["w_in"], weights["w_in"])], name="allgather_w_in")
    ex = _Exchanges(later, [_to_comm(n, kinds[n], weights[n]) for n in later], pos)

    sm = dict(norm_mix_g=norm_mix_g, b_f=b_f, gmlp_ln_g=gmlp_ln_g, gmlp_ln_b=gmlp_ln_b, gmlp_w_s=gmlp_w_s[0],
              gmlp_b_s=gmlp_b_s[0], norm_ffn_g=norm_ffn_g, conv_b=conv_b, norm_ple_g=norm_ple_g,
              norm_final_g=norm_final_g.reshape(1, D_MODEL))
    loss_part, dx0, grads, small = _local_step(
        x[0], p[0, 0], loss_target[0], _assemble_weights({"w_in": first[0]}), sm, ex)

    b_f_and_loss = jnp.concatenate([small["b_f"].reshape(-1), loss_part[0, :1]])
    last = _allreduce_rows(jnp.concatenate([_rows(small["norm_mix_g"], 8), _rows(b_f_and_loss, 8)], axis=0),
                           name="allreduce_last")
    loss = last[8, HEADS]
    small_last = jnp.pad(last, ((0, SMALL_ROWS - 16), (0, 0)))

    grad, delta, new_m, new_v = {}, {}, {}, {}
    for n in names:
        s32, r = ex.reduced[n]
        outs = _adam_sharded(s32, r, *[_to_comm(n, kinds[n], src[n], F32) for src in (weights, mom_m, mom_v)], pos,
                             name="adam_" + n)
        grad[n], delta[n], new_m[n], new_v[n] = [_from_comm(n, kinds[n], o) for o in outs]
    replicated = [n for n, _ in SMALL]
    rep = lambda src: _pack_small({n: src[n] for n in replicated})
    packed = _adam_replicated(ex.table(), small_last, rep(weights), rep(mom_m), rep(mom_v), name="adam_replicated")
    for out, pk in zip((grad, delta, new_m, new_v), packed):
        for n in replicated:
            out[n] = _small(pk, n, weights[n].shape)

    return (loss, dx0, *[grad[n] for n in WEIGHT_ORDER], *[delta[n] for n in WEIGHT_ORDER],
            *[new_m[n] for n in WEIGHT_ORDER], *[new_v[n] for n in WEIGHT_ORDER])
```

```python
import functools
import math

import jax
import jax.numpy as jnp
from jax import lax
from jax.experimental import pallas as pl
from jax.experimental.pallas import tpu as pltpu

F32 = jnp.float32
BF16 = jnp.bfloat16
I32 = jnp.int32

D_MODEL = 1024
GROUPS = 8
GDIM = 128
GBLOCK = 128
CHUNK = 64
HEADS = 16
HEAD_DIM = 64
D_FF = 2816
PLE_DIM = 256
EPS = 1e-6
N_DEV = 8
ATT_SCALE = HEAD_DIM ** -0.5
NEG = -1e30

ADAM_LR = 0.001
ADAM_B1 = 0.9
ADAM_B2 = 0.999
ADAM_EPS = 1e-08
ADAM_WD = 0.01
ADAM_STEP = 10

V7X_VMEM_LIMIT = 48 * 1024 * 1024
MESH = pl.DeviceIdType.MESH

O_F = 2 * 1024 + 3 * 1024
O_G = O_F + HEADS
IN_COLS = O_G + 2 * D_MODEL
MAIN_COLS = IN_COLS - HEADS
IN_SHARD = IN_COLS // N_DEV
IN_SHARD_PAD = 912

SHARDED = (("w_in", "cols"), ("w_branch_a", "rows"), ("w_branch_b", "rows"), ("w_out", "rows"), ("w_up", "cols"),
           ("conv_w", "f32"), ("w_down", "rows"), ("w_ple", "cols"), ("w_ple_gate", "rows"))

SMALL = (("norm_mix_g", 8), ("b_f", 8), ("gmlp_ln_g", 8), ("gmlp_ln_b", 8), ("gmlp_w_s", 128), ("gmlp_b_s", 8),
         ("norm_ffn_g", 8), ("conv_b", 8), ("norm_ple_g", 8), ("norm_final_g", 8))
SMALL_OFF = {}
_o = 0
for _n, _r in SMALL:
    SMALL_OFF[_n] = (_o, _r)
    _o += _r
SMALL_ROWS = _o

WEIGHT_ORDER = ("norm_mix_g", "w_in", "b_f", "gmlp_ln_g", "gmlp_ln_b", "gmlp_w_s", "gmlp_b_s", "w_branch_a",
                "w_branch_b", "w_out", "norm_ffn_g", "w_up", "conv_w", "conv_b", "w_down", "norm_ple_g", "w_ple",
                "w_ple_gate", "norm_final_g")


def _cparams(sem):
    return pltpu.CompilerParams(dimension_semantics=sem, vmem_limit_bytes=V7X_VMEM_LIMIT)


def _gelu(x):
    c = math.sqrt(2.0 / math.pi)
    return 0.5 * x * (1.0 + jnp.tanh(c * (x + 0.044715 * x * x * x)))


def _gelu_and_grad(x):
    c = math.sqrt(2.0 / math.pi)
    t = jnp.tanh(c * (x + 0.044715 * x * x * x))
    g = 0.5 * x * (1.0 + t)
    dg = 0.5 * (1.0 + t) + 0.5 * x * (1.0 - t * t) * (c * (1.0 + 3.0 * 0.044715 * x * x))
    return g, dg


def _sigmoid(x):
    return 1.0 / (1.0 + jnp.exp(-x))


def _dot(a, b, dims):
    return lax.dot_general(a, b, (dims, ((), ())), preferred_element_type=F32)


NN = ((1,), (0,))
NT = ((1,), (1,))
TN = ((0,), (0,))


def _row_tile(rows, most):
    best = None
    for t in range(16, min(rows, most) + 1, 16):
        if rows % t == 0:
            best = t
    return best if best is not None else rows


def _matmul(a, b, *, mode, out_dtype, name, tm=512, tn=512, tk=512, add=None, n=None, b_off=0,
            out_rows=None, o_off=0, into=None, norm_g=None, carry=None):
    if mode == "tn":
        kdim, m = a.shape
    else:
        m, kdim = a.shape
    if n is None:
        n = b.shape[0] if mode == "nt" else b.shape[1]
    tm, tn, tk = min(tm, m), min(tn, n), min(tk, kdim)
    assert m % tm == 0 and n % tn == 0 and kdim % tk == 0, (name, m, n, kdim, tm, tn, tk)
    nk = kdim // tk
    dims = {"nn": NN, "nt": NT, "tn": TN}[mode]

    n_in = 2 + (add is not None) + (into is not None) + (norm_g is not None)
    assert norm_g is None or tn == n, "the RMS norm needs whole rows"

    def finish(r, refs):
        if add is not None:
            r = refs[2][...].astype(F32) + r
        refs[n_in][...] = r.astype(out_dtype)
        if norm_g is not None:
            rs = lax.rsqrt(jnp.mean(r * r, axis=-1, keepdims=True) + EPS)
            refs[n_in + 1][...] = ((r * rs) * refs[n_in - 1][...]).astype(BF16)

    def body(*refs):
        a_ref, b_ref = refs[:2]
        part = _dot(a_ref[...].astype(BF16), b_ref[...].astype(BF16), dims)
        if nk == 1:
            finish(part, refs)
            return
        acc_ref = refs[-1]
        k = pl.program_id(2)

        @pl.when(k == 0)
        def _():
            acc_ref[...] = part

        @pl.when((k > 0) & (k < nk - 1))
        def _():
            acc_ref[...] += part

        @pl.when(k == nk - 1)
        def _():
            finish(acc_ref[...] + part, refs)

    a_spec = pl.BlockSpec((tk, tm), lambda i, j, k: (k, i)) if mode == "tn" else pl.BlockSpec((tm, tk), lambda i, j, k: (i, k))
    if mode == "nt":
        b_spec = pl.BlockSpec((tn, tk), lambda i, j, k: (j + b_off, k))
    else:
        b_spec = pl.BlockSpec((tk, tn), lambda i, j, k: (k + b_off, j))
    o_spec = pl.BlockSpec((tm, tn), lambda i, j, k: (i + o_off, j))
    in_specs = [a_spec, b_spec] + ([pl.BlockSpec((tm, tn), lambda i, j, k: (i, j))] if add is not None else [])
    args = (a, b) + ((add,) if add is not None else ())
    aliases = {}
    if into is not None:
        aliases = {len(args): 0}
        in_specs.append(pl.BlockSpec(memory_space=pl.ANY))
        args += (into,)
    out_specs = [o_spec]
    out_shape = [jax.ShapeDtypeStruct((m if out_rows is None else out_rows, n), out_dtype)]
    if norm_g is not None:
        in_specs.append(pl.BlockSpec((1, n), lambda i, j, k: (0, 0)))
        args += (norm_g,)
        out_specs.append(pl.BlockSpec((tm, tn), lambda i, j, k: (i, j)))
        out_shape.append(jax.ShapeDtypeStruct((m, n), BF16))
    outs, carried = _carry_call(
        body, carry, name=name, grid=(m // tm, n // tn, nk), in_specs=in_specs, out_specs=out_specs,
        out_shape=out_shape, scratch_shapes=[pltpu.VMEM((tm, tn), F32)] if nk > 1 else [], args=args,
        own_aliases=aliases)
    out = outs[0] if norm_g is None else tuple(outs)
    return out if carry is None else (out, carried)


def _row_spec(tr, width, col_block=0):
    return pl.BlockSpec((tr, width), lambda i: (i, col_block))


def _full_spec(shape):
    return pl.BlockSpec(shape, lambda i: tuple(0 for _ in shape))


def _rmsnorm_fwd(x, g, *, name, tr=256):
    s, d = x.shape

    def body(x_ref, g_ref, o_ref):
        xv = x_ref[...]
        r = lax.rsqrt(jnp.mean(xv * xv, axis=-1, keepdims=True) + EPS)
        o_ref[...] = ((xv * r) * g_ref[...]).astype(BF16)

    return pl.pallas_call(
        body, name=name, grid=(s // tr,),
        in_specs=[_row_spec(tr, d), _full_spec((1, d))], out_specs=_row_spec(tr, d),
        out_shape=jax.ShapeDtypeStruct((s, d), BF16), compiler_params=_cparams(("parallel",)),
    )(x, g)


def _matmul_rmsnorm_bwd(a_parts, b, dres, x, g, *, mode, tk, name, extra=None, tm=512, carry=None, lead=False,
                        resident=False):
    s, d = x.shape
    n_row = s // tm
    spans, lo = [], 0
    for a in a_parts:
        spans.append((lo, lo + a.shape[1] // tk))
        lo = spans[-1][1]
    n_main, total = lo, lo + (extra is not None)
    n_parts = len(a_parts)

    def body(*refs):
        a_refs, b_ref = refs[:n_parts], refs[n_parts]
        k0 = n_parts + 1
        ax_ref, bx_ref = (refs[k0], refs[k0 + 1]) if extra is not None else (None, None)
        k0 += 2 * (extra is not None)
        dres_ref, x_ref, g_ref, dx_ref, dxb_ref, dg_ref, acc_all = refs[k0:k0 + 7]
        if resident:
            kk, i = pl.program_id(0), pl.program_id(1)
            acc_ref = acc_all.at[pl.ds(pl.multiple_of(i * tm, tm), tm)]
        else:
            i, kk = pl.program_id(0), pl.program_id(1)
            acc_ref = acc_all

        def accumulate(part, first):
            if first:
                @pl.when(kk == 0)
                def _():
                    acc_ref[...] = part

                @pl.when(kk > 0)
                def _():
                    acc_ref[...] += part
            else:
                acc_ref[...] += part

        for p, (a_ref, (lo_p, hi_p)) in enumerate(zip(a_refs, spans)):
            @pl.when((kk >= lo_p) & (kk < hi_p))
            def _(a_ref=a_ref, lo_p=lo_p):
                accumulate(_dot(a_ref[...].astype(BF16), b_ref[...].astype(BF16), NN if mode == "nn" else NT), lo_p == 0)

        if extra is not None:
            @pl.when(kk == n_main)
            def _():
                accumulate(_dot(ax_ref[...].astype(BF16), bx_ref[...].astype(BF16), NN), False)

        @pl.when(kk == total - 1)
        def _():
            dhv = acc_ref[...]
            xv = x_ref[...]
            r = lax.rsqrt(jnp.mean(xv * xv, axis=-1, keepdims=True) + EPS)
            xhat = xv * r
            dxhat = dhv * g_ref[...]
            dx = dres_ref[...] + r * (dxhat - xhat * jnp.mean(dxhat * xhat, axis=-1, keepdims=True))
            dx_ref[...] = dx
            dxb_ref[...] = dx.astype(BF16)
            dgp = jnp.sum(dhv * xhat, axis=0, keepdims=True)

            @pl.when(i == 0)
            def _():
                dg_ref[...] = dgp

            @pl.when(i > 0)
            def _():
                dg_ref[...] += dgp

    def spec(shape, index):
        return pl.BlockSpec(shape, (lambda kk, i: index(i, kk)) if resident else index)

    def row(i, kk, lo_p, hi_p):
        if not resident:
            return i
        return jnp.where(kk < lo_p, 0, jnp.where(kk >= hi_p, n_row - 1, i))

    a_specs = [spec((tm, tk), lambda i, kk, lo_p=lo_p, hi_p=hi_p: (row(i, kk, lo_p, hi_p),
                                                                    jnp.clip(kk - lo_p, 0, hi_p - lo_p - 1)))
               for lo_p, hi_p in spans]
    step = lambda kk: jnp.minimum(kk, n_main - 1)
    b_spec = (spec((tk, d), lambda i, kk: (step(kk), 0)) if mode == "nn"
              else spec((d, tk), lambda i, kk: (0, step(kk))))
    rows = spec((tm, d), lambda i, kk: (row(i, kk, total - 1, total), 0))
    one = spec((1, d), lambda i, kk: (0, 0))
    dx_spec, dx_shape = rows, jax.ShapeDtypeStruct((s, d), F32)
    if lead:
        dx_spec = spec((None, tm, d), lambda i, kk: (0, row(i, kk, total - 1, total), 0))
        dx_shape = jax.ShapeDtypeStruct((1, s, d), F32)
    x_specs, x_args = [], []
    if extra is not None:
        kx = extra[0].shape[1]
        x_specs = [spec((tm, kx), lambda i, kk: (row(i, kk, n_main, total), 0)), spec((kx, d), lambda i, kk: (0, 0))]
        x_args = list(extra)
    (dx, dxb, dg), carried = _carry_call(
        body, carry, name=name, grid=(total, n_row) if resident else (n_row, total),
        in_specs=a_specs + [b_spec] + x_specs + [rows, rows, one], out_specs=[dx_spec, rows, one],
        out_shape=[dx_shape, jax.ShapeDtypeStruct((s, d), BF16), jax.ShapeDtypeStruct((1, d), F32)],
        scratch_shapes=[pltpu.VMEM((s if resident else tm, d), F32)], args=list(a_parts) + [b] + x_args + [dres, x, g])
    return (dx, dxb, dg), carried


def _grad_w_parts(a_parts, b, *, name, tm=512, carry=None):
    s, width = a_parts[0].shape
    per, n = width // tm, b.shape[1]

    def body(*refs):
        a_refs, b_ref, o_ref = refs[:len(a_parts)], refs[len(a_parts)], refs[len(a_parts) + 1]
        i = pl.program_id(0)
        for p, a_ref in enumerate(a_refs):
            @pl.when(i // per == p)
            def _(a_ref=a_ref):
                o_ref[...] = _dot(a_ref[...].astype(BF16), b_ref[...].astype(BF16), TN).astype(BF16)

    a_specs = [pl.BlockSpec((s, tm), lambda i, p=p: (0, jnp.clip(i - p * per, 0, per - 1))) for p in range(len(a_parts))]
    (out,), carried = _carry_call(
        body, carry, name=name, grid=(len(a_parts) * per,),
        in_specs=a_specs + [pl.BlockSpec((s, n), lambda i: (0, 0))], out_specs=[pl.BlockSpec((tm, n), lambda i: (i, 0))],
        out_shape=[jax.ShapeDtypeStruct((len(a_parts) * width, n), BF16)], scratch_shapes=[], args=list(a_parts) + [b])
    return out, carried


def _ple_loss(p, wt_ple, h3, w_pg, x2, target, g, *, name, tm=256):
    s, d = x2.shape
    kp = p.shape[1]

    def body(p_ref, wp_ref, h_ref, wg_ref, x_ref, t_ref, g_ref, loss_ref, dx_ref, dple_ref, dgp_ref, dg_ref):
        i = pl.program_id(0)
        ple = _dot(p_ref[...].astype(BF16), wp_ref[...], NT)
        sg = _sigmoid(_dot(h_ref[...], wg_ref[...], NN))
        xv = x_ref[...] + ple * sg
        r = lax.rsqrt(jnp.mean(xv * xv, axis=-1, keepdims=True) + EPS)
        xhat = xv * r
        diff = xhat * g_ref[...] - t_ref[...]
        lp = jnp.zeros((1, 128), F32) + (0.5 / d) * jnp.sum(diff * diff)
        dy = diff * (1.0 / d)
        dxhat = dy * g_ref[...]
        dx = r * (dxhat - xhat * jnp.mean(dxhat * xhat, axis=-1, keepdims=True))
        dx_ref[...] = dx
        dple_ref[...] = (dx * sg).astype(BF16)
        dgp_ref[...] = (dx * ple * (sg * (1.0 - sg))).astype(BF16)
        dgp = jnp.sum(dy * xhat, axis=0, keepdims=True)

        @pl.when(i == 0)
        def _():
            dg_ref[...] = dgp
            loss_ref[...] = lp

        @pl.when(i > 0)
        def _():
            dg_ref[...] += dgp
            loss_ref[...] += lp

    rows = _row_spec(tm, d)
    return pl.pallas_call(
        body, name=name, grid=(s // tm,),
        in_specs=[_row_spec(tm, kp), _full_spec((d, kp)), rows, _full_spec((d, d)), rows, rows, _full_spec((1, d))],
        out_specs=[_full_spec((1, 128)), rows, rows, rows, _full_spec((1, d))],
        out_shape=[jax.ShapeDtypeStruct((1, 128), F32), jax.ShapeDtypeStruct((s, d), F32),
                   jax.ShapeDtypeStruct((s, d), BF16), jax.ShapeDtypeStruct((s, d), BF16),
                   jax.ShapeDtypeStruct((1, d), F32)],
        compiler_params=_cparams(("arbitrary",)),
    )(p, wt_ple, h3, w_pg, x2, target, g)


def _branches_merge(a, b, w_a, w_b, zuvg, *, name, tm=512):
    s, d = a.shape

    def body(a_ref, b_ref, wa_ref, wb_ref, ga_ref, gb_ref, ya_ref, yb_ref, o_ref):
        ya = _dot(a_ref[...], wa_ref[...], NN)
        yb = _dot(b_ref[...], wb_ref[...], NN)
        ya_ref[...] = ya
        yb_ref[...] = yb
        o_ref[...] = (_sigmoid(ga_ref[...]) * ya + _sigmoid(gb_ref[...]) * yb).astype(BF16)

    rows = _row_spec(tm, d)
    return pl.pallas_call(
        body, name=name, grid=(s // tm,),
        in_specs=[rows, rows, _full_spec((d, d)), _full_spec((d, d)), _row_spec(tm, d, 2), _row_spec(tm, d, 3)],
        out_specs=[rows, rows, rows],
        out_shape=[jax.ShapeDtypeStruct((s, d), F32), jax.ShapeDtypeStruct((s, d), F32), jax.ShapeDtypeStruct((s, d), BF16)],
        compiler_params=_cparams(("parallel",)),
    )(a, b, w_a, w_b, zuvg, zuvg)


def _merge_bwd(dx1b, w_out, ya, yb, zuvg, *, name, tm=512):
    s, d = ya.shape

    def body(dx_ref, w_ref, ya_ref, yb_ref, ga_ref, gb_ref, dya_ref, dyb_ref, dga_ref, dgb_ref):
        dmv = _dot(dx_ref[...], w_ref[...], NT)
        sa = _sigmoid(ga_ref[...])
        sb = _sigmoid(gb_ref[...])
        dya_ref[...] = (dmv * sa).astype(BF16)
        dyb_ref[...] = (dmv * sb).astype(BF16)
        dga_ref[...] = (dmv * ya_ref[...] * (sa * (1.0 - sa))).astype(BF16)
        dgb_ref[...] = (dmv * yb_ref[...] * (sb * (1.0 - sb))).astype(BF16)

    rows = _row_spec(tm, d)
    o = jax.ShapeDtypeStruct((s, d), BF16)
    return pl.pallas_call(
        body, name=name, grid=(s // tm,),
        in_specs=[rows, _full_spec((d, d)), rows, rows, _row_spec(tm, d, 2), _row_spec(tm, d, 3)],
        out_specs=[rows] * 4, out_shape=[o, o, o, o], compiler_params=_cparams(("parallel",)),
    )(dx1b, w_out, ya, yb, zuvg, zuvg)


def _masked_ws(ws_ref, g):
    row = lax.broadcasted_iota(I32, (GBLOCK, GBLOCK), 0)
    col = lax.broadcasted_iota(I32, (GBLOCK, GBLOCK), 1)
    keep = (col // CHUNK) <= (row // CHUNK)
    return jnp.where(keep, ws_ref[g], 0.0), keep


def _layernorm_parts(zv):
    mu = jnp.mean(zv, axis=-1, keepdims=True)
    xc = zv - mu
    rs = lax.rsqrt(jnp.mean(xc * xc, axis=-1, keepdims=True) + EPS)
    return xc * rs, rs


def _gmlp_fwd(zuvg, ln_g, ln_b, w_s, bs_t, *, name):
    s, w = zuvg.shape[0], GROUPS * GDIM

    def body(zu_ref, zv_ref, lng_ref, lnb_ref, ws_ref, bs_ref, a_ref):
        zu = _gelu(zu_ref[...])
        zv = _gelu(zv_ref[...])
        xhat, _ = _layernorm_parts(zv)
        vln = (xhat * lng_ref[...] + lnb_ref[...]).astype(BF16)
        for g in range(GROUPS):
            wm, _ = _masked_ws(ws_ref, g)
            mixed = _dot(wm.astype(BF16), vln[:, g * GDIM:(g + 1) * GDIM], NN) + bs_ref[:, g:g + 1]
            a_ref[:, g * GDIM:(g + 1) * GDIM] = (zu[:, g * GDIM:(g + 1) * GDIM] * mixed).astype(BF16)

    return pl.pallas_call(
        body, name=name, grid=(s // GBLOCK,),
        in_specs=[_row_spec(GBLOCK, w, 0), _row_spec(GBLOCK, w, 1), _full_spec((1, w)), _full_spec((1, w)),
                  _full_spec((GROUPS, GBLOCK, GBLOCK)), _full_spec((GBLOCK, 128))],
        out_specs=_row_spec(GBLOCK, w),
        out_shape=jax.ShapeDtypeStruct((s, w), BF16), compiler_params=_cparams(("parallel",)),
    )(zuvg, zuvg, ln_g, ln_b, w_s, bs_t)


def _gmlp_bwd(da, zuvg, ln_g, ln_b, w_s, bs_t, carry=None, *, name):
    s, w = zuvg.shape[0], GROUPS * GDIM

    def body(da_ref, zu_ref, zv_ref, lng_ref, lnb_ref, ws_ref, bs_ref,
             dzu_ref, dzv_ref, dws_ref, dbs_ref, dlng_ref, dlnb_ref, dvln_ref):
        i = pl.program_id(0)
        zu, dzu_g = _gelu_and_grad(zu_ref[...])
        zv, dzv_g = _gelu_and_grad(zv_ref[...])
        xhat, rs = _layernorm_parts(zv)
        vln = (xhat * lng_ref[...] + lnb_ref[...]).astype(BF16)
        dav = da_ref[...].astype(F32)
        lane = lax.broadcasted_iota(I32, (GBLOCK, 128), 1)
        dbs = jnp.zeros((GBLOCK, 128), F32)

        @pl.when(i == 0)
        def _():
            dws_ref[...] = jnp.zeros_like(dws_ref)

        for g in range(GROUPS):
            sl = slice(g * GDIM, (g + 1) * GDIM)
            wm, keep = _masked_ws(ws_ref, g)
            wmb = wm.astype(BF16)
            vg = vln[:, sl]
            mixed = _dot(wmb, vg, NN) + bs_ref[:, g:g + 1]
            dag = dav[:, sl]
            dzu_ref[:, sl] = (dag * mixed * dzu_g[:, sl]).astype(BF16)
            dmix = dag * zu[:, sl]
            dmb = dmix.astype(BF16)
            dws_ref[g] += jnp.where(keep, _dot(dmb, vg, NT), 0.0)
            dbs = jnp.where(lane == g, jnp.sum(dmix, axis=1, keepdims=True), dbs)
            dvln_ref[:, sl] = _dot(wmb, dmb, TN)
        dvln = dvln_ref[...]
        dxhat = dvln * lng_ref[...]
        dzv = rs * (dxhat - jnp.mean(dxhat, axis=-1, keepdims=True)
                    - xhat * jnp.mean(dxhat * xhat, axis=-1, keepdims=True))
        dzv_ref[...] = (dzv * dzv_g).astype(BF16)
        dlng = jnp.sum(dvln * xhat, axis=0, keepdims=True)
        dlnb = jnp.sum(dvln, axis=0, keepdims=True)

        @pl.when(i == 0)
        def _():
            dbs_ref[...] = dbs
            dlng_ref[...] = dlng
            dlnb_ref[...] = dlnb

        @pl.when(i > 0)
        def _():
            dbs_ref[...] += dbs
            dlng_ref[...] += dlng
            dlnb_ref[...] += dlnb

    return _carry_call(
        body, carry, name=name, grid=(s // GBLOCK,),
        in_specs=[_row_spec(GBLOCK, w), _row_spec(GBLOCK, w, 0), _row_spec(GBLOCK, w, 1), _full_spec((1, w)),
                  _full_spec((1, w)), _full_spec((GROUPS, GBLOCK, GBLOCK)), _full_spec((GBLOCK, 128))],
        out_specs=[_row_spec(GBLOCK, w), _row_spec(GBLOCK, w), _full_spec((GROUPS, GBLOCK, GBLOCK)),
                   _full_spec((GBLOCK, 128)), _full_spec((1, w)), _full_spec((1, w))],
        out_shape=[jax.ShapeDtypeStruct((s, w), BF16), jax.ShapeDtypeStruct((s, w), BF16),
                   jax.ShapeDtypeStruct((GROUPS, GBLOCK, GBLOCK), F32), jax.ShapeDtypeStruct((GBLOCK, 128), F32),
                   jax.ShapeDtypeStruct((1, w), F32), jax.ShapeDtypeStruct((1, w), F32)],
        scratch_shapes=[pltpu.VMEM((GBLOCK, w), F32)], args=[da, zuvg, zuvg, ln_g, ln_b, w_s, bs_t])


def _shift_down(u, k):
    row = lax.broadcasted_iota(I32, u.shape, 0)
    return jnp.where(row >= k, pltpu.roll(u, k, 0), 0.0)


def _shift_up(u, k):
    s = u.shape[0]
    row = lax.broadcasted_iota(I32, u.shape, 0)
    return jnp.where(row < s - k, pltpu.roll(u, s - k, 0), 0.0)


def _conv(u, w_ref, b_ref):
    return b_ref[...] + w_ref[0:1, :] * _shift_down(u, 2) + w_ref[1:2, :] * _shift_down(u, 1) + w_ref[2:3, :] * u


def _conv_specs(s, f, tc):
    nc = f // tc
    half = lambda rows: [pl.BlockSpec((rows, tc), lambda j: (0, j)), pl.BlockSpec((rows, tc), lambda j: (0, nc + j))]
    return half(s), half(3), half(1)


def _up_convglu(h2, wt_up, conv_w, conv_b, *, name, tc=256):
    s, d = h2.shape
    f = wt_up.shape[0] // 2
    nc = f // tc
    _, w_specs, b_specs = _conv_specs(s, f, tc)

    def body(h_ref, ta_ref, tg_ref, wa_ref, wg_ref, ba_ref, bg_ref, ua_ref, ug_ref, o_ref):
        ua = _dot(h_ref[...], ta_ref[...], NT)
        ua_ref[...] = ua
        ga = _gelu(_conv(ua, wa_ref, ba_ref))
        ug = _dot(h_ref[...], tg_ref[...], NT)
        ug_ref[...] = ug
        o_ref[...] = (ga * _conv(ug, wg_ref, bg_ref)).astype(BF16)

    col = pl.BlockSpec((s, tc), lambda j: (0, j))
    return pl.pallas_call(
        body, name=name, grid=(nc,),
        in_specs=[_full_spec((s, d)), pl.BlockSpec((tc, d), lambda j: (j, 0)), pl.BlockSpec((tc, d), lambda j: (nc + j, 0))]
        + w_specs + b_specs,
        out_specs=[col, col, col],
        out_shape=[jax.ShapeDtypeStruct((s, f), F32), jax.ShapeDtypeStruct((s, f), F32), jax.ShapeDtypeStruct((s, f), BF16)],
        compiler_params=_cparams(("parallel",)),
    )(h2, wt_up, wt_up, conv_w, conv_w, conv_b, conv_b)


def _convglu_bwd(dact, up_a, up_g, conv_w, conv_b, *, name, tc=256):
    s, f = up_a.shape
    _, w_specs, b_specs = _conv_specs(s, f, tc)
    up_specs = [pl.BlockSpec((s, tc), lambda j: (0, j))] * 2

    def half(dc, taps, w_ref, du_ref, dw_ref, db_ref):
        db_ref[...] = jnp.sum(dc, axis=0, keepdims=True)
        for k in range(3):
            dw_ref[k:k + 1, :] = jnp.sum(dc * taps[k], axis=0, keepdims=True)
        du = w_ref[2:3, :] * dc + w_ref[1:2, :] * _shift_up(dc, 1) + w_ref[0:1, :] * _shift_up(dc, 2)
        du_ref[...] = du.astype(BF16)

    def body(d_ref, ua_ref, ug_ref, wa_ref, wg_ref, ba_ref, bg_ref,
             dua_ref, dug_ref, dwa_ref, dwg_ref, dba_ref, dbg_ref):
        taps_a = (_shift_down(ua_ref[...], 2), _shift_down(ua_ref[...], 1), ua_ref[...])
        taps_g = (_shift_down(ug_ref[...], 2), _shift_down(ug_ref[...], 1), ug_ref[...])
        conv = lambda taps, w_ref, b_ref: b_ref[...] + w_ref[0:1, :] * taps[0] + w_ref[1:2, :] * taps[1] + w_ref[2:3, :] * taps[2]
        ca = conv(taps_a, wa_ref, ba_ref)
        cg = conv(taps_g, wg_ref, bg_ref)
        ga, dga = _gelu_and_grad(ca)
        dv = d_ref[...].astype(F32)
        half(dv * cg * dga, taps_a, wa_ref, dua_ref, dwa_ref, dba_ref)
        half(dv * ga, taps_g, wg_ref, dug_ref, dwg_ref, dbg_ref)

    col, w3, b1 = up_specs[0], w_specs[0], b_specs[0]
    return pl.pallas_call(
        body, name=name, grid=(f // tc,),
        in_specs=[col] + up_specs + w_specs + b_specs, out_specs=[col, col, w3, w3, b1, b1],
        out_shape=[jax.ShapeDtypeStruct((s, f), BF16), jax.ShapeDtypeStruct((s, f), BF16),
                   jax.ShapeDtypeStruct((3, f), F32), jax.ShapeDtypeStruct((3, f), F32),
                   jax.ShapeDtypeStruct((1, f), F32), jax.ShapeDtypeStruct((1, f), F32)],
        compiler_params=_cparams(("parallel",)),
    )(dact, up_a, up_g, conv_w, conv_w, conv_b, conv_b)


def _tri_dot(tri, x):
    b0 = x.astype(BF16)
    r1 = x - b0.astype(F32)
    b1 = r1.astype(BF16)
    b2 = (r1 - b1.astype(F32)).astype(BF16)
    return _dot(tri, b0, NN) + _dot(tri, b1, NN) + _dot(tri, b2, NN)


def _log_sigmoid(x):
    return jnp.minimum(x, 0.0) - jnp.log(1.0 + jnp.exp(-jnp.abs(x)))


def _expand_heads(col16, rows):
    src = lax.broadcasted_iota(I32, (128, HEADS * HEAD_DIM), 0)
    dst = lax.broadcasted_iota(I32, (128, HEADS * HEAD_DIM), 1) // HEAD_DIM
    spread = (src == dst).astype(BF16)
    p0, p1, p2 = _bf16_pieces(col16)
    return (_dot(p0.astype(BF16), spread, NN) + _dot(p1.astype(BF16), spread, NN)) + _dot(p2.astype(BF16), spread, NN)


def _forget_cumsum(f_logit, b_f, *, name):
    s = f_logit.shape[0]
    nb = s // 128

    def body(f_ref, b_ref, cqe_ref):
        row = lax.broadcasted_iota(I32, (128, 128), 0)
        col = lax.broadcasted_iota(I32, (128, 128), 1)
        tri = (col <= row).astype(BF16)

        def step(n, carry):
            r0 = pl.multiple_of(n * 128, 128)
            lf = _log_sigmoid(f_ref[pl.ds(r0, 128), :] + b_ref[...])
            cum = _tri_dot(tri, lf) + carry
            cqe_ref[pl.ds(r0, 128), :] = _expand_heads(cum, 128)
            return cum[127:128, :]

        lax.fori_loop(0, nb, step, jnp.zeros((1, 128), F32))

    return pl.pallas_call(
        body, name=name, grid=(1,),
        in_specs=[_full_spec((s, 128)), _full_spec((1, 128))],
        out_specs=_full_spec((s, HEADS * HEAD_DIM)),
        out_shape=jax.ShapeDtypeStruct((s, HEADS * HEAD_DIM), F32),
        compiler_params=_cparams(("arbitrary",)),
    )(f_logit, b_f)


def _forget_bwd(dcq16, sum_q16, f_logit, b_f, *, name):
    s = f_logit.shape[0]
    nb = s // 128

    def body(a_ref, k_ref, f_ref, b_ref, df_ref, db_ref):
        row = lax.broadcasted_iota(I32, (128, 128), 0)
        col = lax.broadcasted_iota(I32, (128, 128), 1)
        tri_rev = (col >= row).astype(BF16)

        def step(m, carry):
            suffix, dbsum = carry
            n = nb - 1 - m
            r0 = pl.multiple_of(n * 128, 128)
            dcum = a_ref[pl.ds(r0, 128), :] - k_ref[pl.ds(r0, 128), :]
            dlf = _tri_dot(tri_rev, dcum) + suffix
            df = dlf * _sigmoid(-(f_ref[pl.ds(r0, 128), :] + b_ref[...]))
            df_ref[pl.ds(r0, 128), :] = df.astype(BF16)
            return dlf[0:1, :], dbsum + jnp.sum(df, axis=0, keepdims=True)

        _, dbsum = lax.fori_loop(0, nb, step, (jnp.zeros((1, 128), F32), jnp.zeros((1, 128), F32)))
        db_ref[...] = dbsum

    return pl.pallas_call(
        body, name=name, grid=(1,),
        in_specs=[_full_spec((s, 128))] * 3 + [_full_spec((1, 128))],
        out_specs=[_full_spec((s, 128)), _full_spec((1, 128))],
        out_shape=[jax.ShapeDtypeStruct((s, 128), BF16), jax.ShapeDtypeStruct((1, 128), F32)],
        compiler_params=_cparams(("arbitrary",)),
    )(dcq16, sum_q16, f_logit, b_f)


ATT_T = 256


def _head_lanes(rows):
    return lax.broadcasted_iota(I32, (rows, 128), 1) < HEAD_DIM


def _bf16_pieces(c):
    p0 = c.astype(BF16).astype(F32)
    r = c - p0
    p1 = r.astype(BF16).astype(F32)
    p2 = (r - p1).astype(BF16).astype(F32)
    return p0, p1, p2


def _col_reduce(x, op):
    rows = x.shape[0]
    while rows > 8:
        rows //= 2
        x = op(x[:rows], x[rows:])
    return jnp.max(x, axis=0, keepdims=True) if op is jnp.maximum else jnp.sum(x, axis=0, keepdims=True)


def _attn_prep(qkv, cqe, carry=None, *, name):
    s = qkv.shape[0]
    npair = HEADS // 2

    def body(q_ref, k_ref, v_ref, c_ref, qa_ref, ka_ref, vt_ref):
        rows = 128
        lane = lax.broadcasted_iota(I32, (rows, 128), 1)

        def chunk(n, _):
            r0 = pl.multiple_of(n * rows, rows)
            sl = pl.ds(r0, rows)
            qv = q_ref[sl, :].astype(F32) * ATT_SCALE
            kv = k_ref[sl, :].astype(F32)
            p0, p1, p2 = _bf16_pieces(pltpu.roll(c_ref[sl, :], HEAD_DIM, 1))
            for e in range(2):
                mine = (lane < HEAD_DIM) if e == 0 else (lane >= HEAD_DIM)
                base = HEAD_DIM * (1 - e)
                ones_hi = jnp.where((lane >= base + 3) & (lane < base + 6), 1.0, 0.0)
                ones_lo = jnp.where((lane >= base) & (lane < base + 3), 1.0, 0.0)
                qa = jnp.where(mine, qv, jnp.where(lane == base, p0, jnp.where(lane == base + 1, p1,
                               jnp.where(lane == base + 2, p2, ones_hi))))
                ka = jnp.where(mine, kv, jnp.where(lane == base + 3, -p0, jnp.where(lane == base + 4, -p1,
                               jnp.where(lane == base + 5, -p2, ones_lo))))
                qa_ref[e, sl, :] = qa.astype(BF16)
                ka_ref[e, sl, :] = ka.astype(BF16)
            vt_ref[0, :, sl] = v_ref[sl, :].astype(F32).T.astype(BF16)
            return 0

        lax.fori_loop(0, s // rows, chunk, 0)

    pair = pl.BlockSpec((2, s, 128), lambda hp: (hp, 0, 0))
    return _carry_call(
        body, carry, name=name, grid=(npair,),
        in_specs=[pl.BlockSpec((s, 128), lambda hp: (0, hp)), pl.BlockSpec((s, 128), lambda hp: (0, npair + hp)),
                  pl.BlockSpec((s, 128), lambda hp: (0, 2 * npair + hp)), pl.BlockSpec((s, 128), lambda hp: (0, hp))],
        out_specs=[pair, pair, pl.BlockSpec((1, 128, s), lambda hp: (hp, 0, 0))],
        out_shape=[jax.ShapeDtypeStruct((HEADS, s, 128), BF16), jax.ShapeDtypeStruct((HEADS, s, 128), BF16),
                   jax.ShapeDtypeStruct((npair, 128, s), BF16)],
        scratch_shapes=[], args=[qkv, qkv, qkv, cqe])


def _attn_fwd(qa, ka, vt, carry=None, *, name):
    s = qa.shape[1]
    t = 2 * ATT_T
    nq = s // t
    npair = HEADS // 2

    def body(qa_ref, ka_ref, vt_ref, o_ref, lse_ref):
        i = pl.program_id(1)
        krow = lax.broadcasted_iota(I32, (t, t), 0)
        qcol = lax.broadcasted_iota(I32, (t, t), 1)
        sub = lax.broadcasted_iota(I32, (128, t), 0)
        row8 = lax.broadcasted_iota(I32, (8, t), 0)
        qbs = (qa_ref[0], qa_ref[1])
        tk = t

        def step(j, carry, diag):
            c0 = pl.multiple_of(j * tk, tk)
            vtb = vt_ref[0, :, pl.ds(c0, tk)]
            sts = [_dot(ka_ref[e, pl.ds(c0, tk), :], qbs[e], NT) for e in range(2)]
            if diag:
                sts = [jnp.where(krow <= qcol, st, NEG) for st in sts]
            pts, stats = [], []
            for e in range(2):
                m, l, _ = carry[e]
                m_new = jnp.maximum(m, _col_reduce(sts[e], jnp.maximum))
                alpha = jnp.exp(m - m_new)
                pt = jnp.exp(sts[e] - m_new)
                stats.append((m_new, alpha, alpha * l + _col_reduce(pt, jnp.add)))
                pts.append(pt.astype(BF16))
            pvs = [_dot(vtb, pts[e], NN) for e in range(2)]
            return tuple((stats[e][0], stats[e][2], stats[e][1] * carry[e][2] + pvs[e]) for e in range(2))

        init = (jnp.full((1, t), NEG, F32), jnp.zeros((1, t), F32), jnp.zeros((128, t), F32))
        carry = lax.fori_loop(0, i, functools.partial(step, diag=False), (init, init))
        (m0, l0, acc0), (m1, l1, acc1) = step(i, carry, True)
        o_pair = jnp.where(sub < HEAD_DIM, acc0 / l0, acc1 / l1)
        o_ref[...] = o_pair.T.astype(BF16)
        lse_ref[0] = jnp.where(row8 == 0, m0 + jnp.log(l0), jnp.where(row8 == 1, m1 + jnp.log(l1), 0.0))

    return _carry_call(
        body, carry, name=name, grid=(npair, nq),
        in_specs=[pl.BlockSpec((2, t, 128), lambda hp, i: (hp, i, 0)), pl.BlockSpec((2, s, 128), lambda hp, i: (hp, 0, 0)),
                  pl.BlockSpec((1, 128, s), lambda hp, i: (hp, 0, 0))],
        out_specs=[pl.BlockSpec((t, 128), lambda hp, i: (i, hp)), pl.BlockSpec((1, 8, t), lambda hp, i: (hp, 0, i))],
        out_shape=[jax.ShapeDtypeStruct((s, HEADS * HEAD_DIM), BF16), jax.ShapeDtypeStruct((npair, 8, s), F32)],
        scratch_shapes=[], args=[qa, ka, vt])


def _attn_delta(do, o, carry=None, *, name):
    s = do.shape[0]

    def body(do_ref, o_ref, d_ref):
        prod = do_ref[...].astype(F32) * o_ref[...].astype(F32)
        row = lax.broadcasted_iota(I32, (8, 128), 0)
        lane = lax.broadcasted_iota(I32, (8, 128), 1)
        sel = ((row == 0) & (lane < HEAD_DIM) | (row == 1) & (lane >= HEAD_DIM)).astype(BF16)
        p0, p1, p2 = _bf16_pieces(prod)
        d_ref[0] = (_dot(sel, p0.astype(BF16), NT) + _dot(sel, p1.astype(BF16), NT)) + _dot(sel, p2.astype(BF16), NT)

    pair = pl.BlockSpec((s, 128), lambda hp: (0, hp))
    (delta3,), carried = _carry_call(
        body, carry, name=name, grid=(HEADS // 2,), in_specs=[pair, pair],
        out_specs=[pl.BlockSpec((1, 8, s), lambda hp: (hp, 0, 0))],
        out_shape=[jax.ShapeDtypeStruct((HEADS // 2, 8, s), F32)], scratch_shapes=[], args=[do, o])
    return delta3, carried


def _attn_bwd(qa, ka, qkv, do, lse3, delta3, carry=None, *, name):
    s = qa.shape[1]
    t = 2 * ATT_T
    nb = s // t
    npair = HEADS // 2

    def body(qa_ref, ka_ref, v_ref, do_ref, lse_ref, delta_ref, dq_ref, dk_ref, dv_ref, aux_ref, dcq_ref, dqt):
        hp = pl.program_id(0)
        first = _head_lanes(t)
        lane = lax.broadcasted_iota(I32, (t, 128), 1)
        dqt[...] = jnp.zeros_like(dqt)

        @pl.when(hp == 0)
        def _():
            aux_ref[...] = jnp.zeros_like(aux_ref)

        krow = lax.broadcasted_iota(I32, (t, t), 0)
        qcol = lax.broadcasted_iota(I32, (t, t), 1)

        def key_block(j, _):
            c0 = pl.multiple_of(j * t, t)
            vb = v_ref[pl.ds(c0, t), :]
            kbs = (ka_ref[0, pl.ds(c0, t), :], ka_ref[1, pl.ds(c0, t), :])
            kbts = tuple(kb.astype(F32).T.astype(BF16) for kb in kbs)
            vhs = (jnp.where(first, vb, jnp.zeros_like(vb)), jnp.where(first, jnp.zeros_like(vb), vb))

            def query_block(i, carry, diag):
                r0 = pl.multiple_of(i * t, t)
                dob = do_ref[pl.ds(r0, t), :]
                sts = [_dot(kbs[e], qa_ref[e, pl.ds(r0, t), :], NT) for e in range(2)]
                dpts = [_dot(vhs[e], dob, NT) for e in range(2)]
                ptbs, dsbs = [], []
                for e in range(2):
                    st = jnp.where(krow <= qcol, sts[e], NEG) if diag else sts[e]
                    pt = jnp.exp(st - lse_ref[0, e:e + 1, pl.ds(r0, t)])
                    dsbs.append((pt * (dpts[e] - delta_ref[0, e:e + 1, pl.ds(r0, t)])).astype(BF16))
                    ptbs.append(pt.astype(BF16))
                out = []
                for e in range(2):
                    dk_a, dv_a = carry[e]
                    dv_a = dv_a + _dot(ptbs[e], dob, NN)
                    dk_a = dk_a + _dot(dsbs[e], qa_ref[e, pl.ds(r0, t), :], NN)
                    dqt[e, :, pl.ds(r0, t)] += _dot(kbts[e], dsbs[e], NN)
                    out.append((dk_a, dv_a))
                return tuple(out)

            zero = jnp.zeros((t, 128), F32)
            carry = query_block(j, ((zero, zero), (zero, zero)), True)
            (dk0, dv0), (dk1, dv1) = lax.fori_loop(j + 1, nb, functools.partial(query_block, diag=False), carry)
            dk_ref[pl.ds(c0, t), :] = jnp.where(first, dk0, dk1).astype(BF16)
            dv_ref[pl.ds(c0, t), :] = jnp.where(first, dv0, dv1).astype(BF16)
            sum_q = jnp.where(lane == 2 * hp, dk0[:, HEAD_DIM + 3:HEAD_DIM + 4],
                              jnp.where(lane == 2 * hp + 1, dk1[:, 3:4], aux_ref[pl.ds(c0, t), :]))
            aux_ref[pl.ds(c0, t), :] = sum_q
            return 0

        lax.fori_loop(0, nb, key_block, 0)
        sub = lax.broadcasted_iota(I32, (128, s), 0)
        row8 = lax.broadcasted_iota(I32, (8, s), 0)
        dq_ref[...] = (jnp.where(sub < HEAD_DIM, dqt[0], dqt[1]) * ATT_SCALE).T.astype(BF16)
        dcq_ref[0] = jnp.where(row8 == 0, dqt[0, HEAD_DIM:HEAD_DIM + 1, :], jnp.where(row8 == 1, dqt[1, 0:1, :], 0.0))

    def pair_cols(off):
        return pl.BlockSpec((s, 128), lambda hp: (0, off + hp))

    heads = pl.BlockSpec((2, s, 128), lambda hp: (hp, 0, 0))
    rows = pl.BlockSpec((1, 8, s), lambda hp: (hp, 0, 0))
    wide = jax.ShapeDtypeStruct((s, HEADS * HEAD_DIM), BF16)
    return _carry_call(
        body, carry, name=name, grid=(npair,),
        in_specs=[heads, heads, pair_cols(2 * npair), pair_cols(0), rows, rows],
        out_specs=[pair_cols(0), pair_cols(0), pair_cols(0), pl.BlockSpec((s, 128), lambda hp: (0, 0)), rows],
        out_shape=[wide, wide, wide, jax.ShapeDtypeStruct((s, 128), F32), jax.ShapeDtypeStruct((npair, 8, s), F32)],
        scratch_shapes=[pltpu.VMEM((2, 128, s), F32)], args=[qa, ka, qkv, do, lse3, delta3])


def _adam_math(w, g, m, v):
    m = ADAM_B1 * m + (1.0 - ADAM_B1) * g
    v = ADAM_B2 * v + (1.0 - ADAM_B2) * (g * g)
    m_hat = m / (1.0 - ADAM_B1 ** ADAM_STEP)
    v_hat = v / (1.0 - ADAM_B2 ** ADAM_STEP)
    delta = -ADAM_LR * (m_hat / (jnp.sqrt(v_hat) + ADAM_EPS) + ADAM_WD * w)
    return delta, m, v


def _sum_pairs(keep, recv, pos, *, name):
    _, r, c = recv.shape
    tr = _row_tile(r, 512)

    def body(pos_ref, a_ref, b_ref, o32_ref, o16_ref):
        tot = a_ref[...].astype(F32) + b_ref[...].astype(F32)
        o16_ref[...] = tot.astype(BF16)

        @pl.when(pl.program_id(1) == 2 * pos_ref[0] + pos_ref[1])
        def _():
            o32_ref[...] = tot

    out = pl.BlockSpec((1, tr, c), lambda i, q, pos: (q, i, 0))
    grid_spec = pltpu.PrefetchScalarGridSpec(
        num_scalar_prefetch=1, grid=(r // tr, 4),
        in_specs=[pl.BlockSpec((1, tr, c), lambda i, q, pos: (2 * q + pos[2], i, 0)), out],
        out_specs=[pl.BlockSpec((1, tr, c), lambda i, q, pos: (0, i, 0)), out])
    return pl.pallas_call(
        body, name=name, grid_spec=grid_spec,
        out_shape=[jax.ShapeDtypeStruct((1, r, c), F32), jax.ShapeDtypeStruct((4, r, c), BF16)],
        compiler_params=_cparams(("arbitrary", "arbitrary")),
    )(pos, keep, recv)


def _adam_sharded(psum, recv, w, m, v, pos, *, name):
    r, c = w.shape
    rg = psum.shape[1]

    def body(pos_ref, p_ref, r_ref, w_ref, m_ref, v_ref, g_ref, d_ref, mo_ref, vo_ref):
        part = lambda ref, q: ref[q] if rg == r else ref[q, :r, :]
        g = part(p_ref, 0) + part(r_ref, 0).astype(F32) + part(r_ref, 1).astype(F32) + part(r_ref, 2).astype(F32)
        delta, mn, vn = _adam_math(w_ref[...], g, m_ref[...], v_ref[...])
        g_ref[...] = g
        d_ref[...] = delta
        mo_ref[...] = mn
        vo_ref[...] = vn

    if rg == r:
        tr = _row_tile(r, 320)
        grid = (r // tr,)
        row = pl.BlockSpec((tr, c), lambda i, pos: (i, 0))
        sums = lambda n: pl.BlockSpec((n, tr, c), lambda i, pos: (0, i, 0))
    else:
        tc = 256
        grid = (c // tc,)
        row = pl.BlockSpec((r, tc), lambda i, pos: (0, i))
        sums = lambda n: pl.BlockSpec((n, rg, tc), lambda i, pos: (0, 0, i))
    grid_spec = pltpu.PrefetchScalarGridSpec(
        num_scalar_prefetch=1, grid=grid, in_specs=[sums(1), sums(3), row, row, row], out_specs=[row, row, row, row])
    o = jax.ShapeDtypeStruct((r, c), F32)
    return pl.pallas_call(
        body, name=name, grid_spec=grid_spec, out_shape=[o, o, o, o],
        compiler_params=_cparams(("parallel",)),
    )(pos, psum, recv, w, m, v)


def _adam_replicated(chip_sums, last, w, m, v, *, name):
    r = w.shape[0]

    def body(s_ref, l_ref, w_ref, m_ref, v_ref, g_ref, d_ref, mo_ref, vo_ref):
        g = (((s_ref[0] + s_ref[1]) + s_ref[2]) + s_ref[3]) + l_ref[...]
        delta, mn, vn = _adam_math(w_ref[...], g, m_ref[...], v_ref[...])
        g_ref[...] = g
        d_ref[...] = delta
        mo_ref[...] = mn
        vo_ref[...] = vn

    o = jax.ShapeDtypeStruct((r, 1024), F32)
    full = _full_spec((r, 1024))
    return pl.pallas_call(
        body, name=name, grid=(1,),
        in_specs=[_full_spec((4, r, 1024)), full, full, full, full], out_specs=[full] * 4, out_shape=[o] * 4,
        compiler_params=_cparams(("arbitrary",)),
    )(chip_sums, last, w, m, v)


ASM_OUT = 256
ASM_SRC = 304


def _w_in_row(r):
    return r if r < 2048 else (r + O_G - 2048 if r < 4096 else r - 2048)


def _assemble_wt_main(g, *, name):
    table = []
    for blk in range(MAIN_COLS // ASM_OUT):
        j, l0 = divmod(_w_in_row(blk * ASM_OUT), IN_SHARD)
        sb = l0 // ASM_SRC
        n_a = min(ASM_OUT, min(IN_SHARD, (sb + 1) * ASM_SRC) - l0)
        if n_a == ASM_OUT:
            nxt = (j, sb)
        elif l0 + n_a == IN_SHARD:
            nxt = (j + 1, 0)
        else:
            nxt = (j, sb + 1)
        table.append((j, sb, l0 - sb * ASM_SRC, n_a) + nxt)

    def body(tab_ref, a_ref, b_ref, o_ref):
        blk = pl.program_id(0)
        off, n_a = tab_ref[blk, 2], tab_ref[blk, 3]
        r = lax.broadcasted_iota(I32, (ASM_OUT, ASM_SRC), 0)
        k = lax.broadcasted_iota(I32, (ASM_OUT, ASM_SRC), 1)
        sel_a = ((k == r + off) & (r < n_a)).astype(BF16)
        sel_b = ((k == r - n_a) & (r >= n_a)).astype(BF16)
        o_ref[...] = (_dot(sel_a, a_ref[0], NN) + _dot(sel_b, b_ref[0], NN)).astype(BF16)

    src = lambda c: pl.BlockSpec((1, ASM_SRC, D_MODEL), lambda blk, tab: (tab[blk, c], tab[blk, c + 1], 0))
    grid_spec = pltpu.PrefetchScalarGridSpec(
        num_scalar_prefetch=1, grid=(len(table),), in_specs=[src(0), src(4)],
        out_specs=pl.BlockSpec((ASM_OUT, D_MODEL), lambda blk, tab: (blk, 0)))
    return pl.pallas_call(
        body, name=name, grid_spec=grid_spec, out_shape=jax.ShapeDtypeStruct((MAIN_COLS, D_MODEL), BF16),
        compiler_params=_cparams(("parallel",)),
    )(jnp.asarray(table, I32), g, g)


def _pair_sum_small(mine, theirs, *, name):
    def body(a_ref, b_ref, o_ref):
        o_ref[...] = a_ref[...] + b_ref[...]

    full = _full_spec(mine.shape)
    return pl.pallas_call(
        body, name=name, grid=(1,), in_specs=[full, full], out_specs=full,
        out_shape=jax.ShapeDtypeStruct(mine.shape, F32), compiler_params=_cparams(("arbitrary",)),
    )(mine, theirs)


ANY = pl.BlockSpec(memory_space=pl.ANY)
OTHER_CHIPS = ((1, 0), (0, 1), (1, 1))


class _Carry:
    def __init__(self, inputs, out_shapes, scratch, start, wait, aliases=None):
        self.inputs, self.out_shapes, self.scratch = list(inputs), list(out_shapes), list(scratch)
        self.start, self.wait, self.aliases = start, wait, dict(aliases or {})


def _carry_join(*carries):
    n_in = [len(c.inputs) for c in carries]
    n_out = [len(c.out_shapes) for c in carries]
    n_scr = [len(c.scratch) for c in carries]

    def split(refs, counts):
        out, k = [], 0
        for n in counts:
            out.append(refs[k:k + n])
            k += n
        return out

    def start(ins, outs, scr):
        for c, i, o, s in zip(carries, split(ins, n_in), split(outs, n_out), split(scr, n_scr)):
            c.start(i, o, s)

    def wait(ins, outs, scr):
        for c, i, o, s in zip(carries, split(ins, n_in), split(outs, n_out), split(scr, n_scr)):
            c.wait(i, o, s)

    aliases = {}
    for k, c in enumerate(carries):
        aliases.update({sum(n_in[:k]) + i: sum(n_out[:k]) + o for i, o in c.aliases.items()})
    joined = _Carry(sum((c.inputs for c in carries), []), sum((c.out_shapes for c in carries), []),
                    sum((c.scratch for c in carries), []), start, wait, aliases)
    joined.counts = n_out
    joined.split = lambda results: split(results, n_out)
    return joined


def _carried(body, carry, n_in, n_out, grid):
    if carry is None:
        return body
    ci, co, cs = len(carry.inputs), len(carry.out_shapes), len(carry.scratch)

    def wrapped(*refs):
        ins, cins = refs[:n_in], refs[n_in:n_in + ci]
        outs, couts = refs[n_in + ci:n_in + ci + n_out], refs[n_in + ci + n_out:n_in + ci + n_out + co]
        rest = refs[n_in + ci + n_out + co:]
        scratch, cscr = rest[:len(rest) - cs], rest[len(rest) - cs:]
        first, last = None, None
        for axis, size in enumerate(grid):
            f, l = pl.program_id(axis) == 0, pl.program_id(axis) == size - 1
            first = f if first is None else first & f
            last = l if last is None else last & l

        @pl.when(first)
        def _():
            carry.start(cins, couts, cscr)

        body(*ins, *outs, *scratch)

        @pl.when(last)
        def _():
            carry.wait(cins, couts, cscr)

    return wrapped


def _carry_call(body, carry, *, name, grid, in_specs, out_specs, out_shape, scratch_shapes, args, vmem=True,
                own_aliases=None):
    n_in, n_out = len(in_specs), len(out_specs)
    extra_in = [ANY] * len(carry.inputs) if carry else []
    extra_out = [ANY] * len(carry.out_shapes) if carry else []
    aliases = dict(own_aliases or {})
    if carry:
        aliases.update({n_in + i: n_out + o for i, o in carry.aliases.items()})
    out = pl.pallas_call(
        _carried(body, carry, n_in, n_out, grid), name=name, grid=grid,
        in_specs=list(in_specs) + extra_in, out_specs=list(out_specs) + extra_out,
        out_shape=list(out_shape) + (carry.out_shapes if carry else []),
        scratch_shapes=list(scratch_shapes) + (carry.scratch if carry else []),
        input_output_aliases=aliases,
        compiler_params=_cparams(("arbitrary",) * len(grid)) if vmem else None,
    )(*args, *(carry.inputs if carry else []))
    return list(out[:n_out]), list(out[n_out:])


def _run_carry(carry, *, name):
    return _carry_call(lambda: None, carry, name=name, grid=(1,), in_specs=[], out_specs=[], out_shape=[],
                       scratch_shapes=[], args=[], vmem=False)[1]


def _sems(n):
    return [pltpu.SemaphoreType.DMA((n,)), pltpu.SemaphoreType.DMA((n,))]


def _carry_gather1(shards):
    n = len(shards)

    def copies(x_refs, out_refs, scr, with_arrivals):
        send_sems, recv_sems, local_sems = scr
        x, y, c = lax.axis_index("x"), lax.axis_index("y"), lax.axis_index("c")
        peers = [(x, y, 1 - c)] + [(x ^ fx, y ^ fy, c) for fx, fy in OTHER_CHIPS]
        local, sends, arrivals = [], [], []
        for t, (x_ref, out_ref) in enumerate(zip(x_refs, out_refs)):
            local.append(pltpu.make_async_copy(x_ref, out_ref.at[4 * x + 2 * y + c], local_sems.at[t]))
            for k, (px, py, pc) in enumerate(peers):
                sems = dict(send_sem=send_sems.at[4 * t + k], recv_sem=recv_sems.at[4 * t + k],
                            device_id=(px, py, pc), device_id_type=MESH)
                sends.append(pltpu.make_async_remote_copy(src_ref=x_ref, dst_ref=out_ref.at[4 * x + 2 * y + c], **sems))
                if with_arrivals:
                    arrivals.append(
                        pltpu.make_async_remote_copy(src_ref=x_ref, dst_ref=out_ref.at[4 * px + 2 * py + pc], **sems))
        return local, sends, arrivals

    def start(x_refs, out_refs, scr):
        local, sends, _ = copies(x_refs, out_refs, scr, False)
        for cp in local + sends:
            cp.start()

    def wait(x_refs, out_refs, scr):
        local, sends, arrivals = copies(x_refs, out_refs, scr, True)
        for cp in arrivals:
            cp.wait_recv()
        for cp in sends:
            cp.wait_send()
        for cp in local:
            cp.wait()

    return _Carry(shards, [jax.ShapeDtypeStruct((N_DEV,) + a.shape, a.dtype) for a in shards],
                  _sems(4 * n) + [pltpu.SemaphoreType.DMA((n,))], start, wait)


def _carry_gather2(gathered):
    n = len(gathered)

    def copies(in_refs, g_refs, scr, with_arrivals):
        send_sems, recv_sems = scr
        x, y, c = lax.axis_index("x"), lax.axis_index("y"), lax.axis_index("c")
        sends, arrivals = [], []
        for t in range(n):
            for j, (fx, fy) in enumerate(OTHER_CHIPS):
                px, py = x ^ fx, y ^ fy
                sems = dict(send_sem=send_sems.at[3 * t + j], recv_sem=recv_sems.at[3 * t + j],
                            device_id=(x, y, 1 - c), device_id_type=MESH)
                mine, theirs = 4 * px + 2 * py + c, 4 * px + 2 * py + (1 - c)
                sends.append(pltpu.make_async_remote_copy(src_ref=in_refs[t].at[mine], dst_ref=g_refs[t].at[mine], **sems))
                if with_arrivals:
                    arrivals.append(pltpu.make_async_remote_copy(
                        src_ref=in_refs[t].at[mine], dst_ref=g_refs[t].at[theirs], **sems))
        return sends, arrivals

    def start(in_refs, g_refs, scr):
        for cp in copies(in_refs, g_refs, scr, False)[0]:
            cp.start()

    def wait(in_refs, g_refs, scr):
        sends, arrivals = copies(in_refs, g_refs, scr, True)
        for cp in arrivals:
            cp.wait_recv()
        for cp in sends:
            cp.wait_send()

    return _Carry(gathered, [jax.ShapeDtypeStruct(a.shape, a.dtype) for a in gathered], _sems(3 * n), start, wait,
                  aliases={t: t for t in range(n)})


def _allreduce_rows(x, *, name):
    def body(x_ref, o_ref, sib_ref, mine_ref, tab_ref, send_sems, recv_sems):
        x, y, c = lax.axis_index("x"), lax.axis_index("y"), lax.axis_index("c")
        swap = pltpu.make_async_remote_copy(src_ref=x_ref, dst_ref=sib_ref, send_sem=send_sems.at[0],
                                            recv_sem=recv_sems.at[0], device_id=(x, y, 1 - c), device_id_type=MESH)
        swap.start()
        swap.wait()
        mine_ref[...] = x_ref[...] + sib_ref[...]
        tab_ref[pl.ds(2 * x + y, 1)] = mine_ref[...][None]

        def copy(k, slot):
            fx, fy = OTHER_CHIPS[k]
            return pltpu.make_async_remote_copy(
                src_ref=mine_ref, dst_ref=tab_ref.at[slot], send_sem=send_sems.at[1 + k], recv_sem=recv_sems.at[1 + k],
                device_id=(x ^ fx, y ^ fy, c), device_id_type=MESH)

        for k in range(3):
            copy(k, 2 * x + y).start()
        for k, (fx, fy) in enumerate(OTHER_CHIPS):
            copy(k, 2 * (x ^ fx) + (y ^ fy)).wait()
        o_ref[...] = ((tab_ref[0] + tab_ref[1]) + tab_ref[2]) + tab_ref[3]

    vmem = pl.BlockSpec(memory_space=pltpu.VMEM)
    return pl.pallas_call(
        body, name=name, out_shape=jax.ShapeDtypeStruct(x.shape, F32), in_specs=[vmem], out_specs=vmem,
        scratch_shapes=[pltpu.VMEM(x.shape, F32), pltpu.VMEM(x.shape, F32), pltpu.VMEM((4,) + x.shape, F32)] + _sems(4),
    )(x)


def _allgather(shards, *, name):
    n = len(shards)
    per = 10

    def body(*refs):
        x_refs, out_refs = refs[:n], refs[n:2 * n]
        send_sems, recv_sems, local_sems = refs[2 * n:]
        x, y, c = lax.axis_index("x"), lax.axis_index("y"), lax.axis_index("c")
        me, sibling = (x, y, c), (x, y, 1 - c)
        near_x, near_y, across = (1 - x, y), (x, 1 - y), (1 - x, 1 - y)

        def rows(ref, t, half):
            r = shards[t].shape[0]
            h = -(-(r // 2) // 16) * 16
            if half is None:
                return ref
            return ref.at[pl.ds(0, h)] if half == 0 else ref.at[pl.ds(h, r - h)]

        def copy(t, k, block, half, to, from_input=False):
            px, py, pc = block
            slab = rows(out_refs[t].at[4 * px + 2 * py + pc], t, half)
            return pltpu.make_async_remote_copy(
                src_ref=rows(x_refs[t], t, half) if from_input else slab, dst_ref=slab,
                send_sem=send_sems.at[per * t + k], recv_sem=recv_sems.at[per * t + k], device_id=to,
                device_id_type=MESH)

        mine = [pltpu.make_async_copy(x_refs[t], out_refs[t].at[4 * x + 2 * y + c], local_sems.at[t]) for t in range(n)]
        for cp in mine:
            cp.start()
        sent = []

        def send(cp):
            cp.start()
            sent.append(cp)

        for t in range(n):
            send(copy(t, 0, me, None, sibling, from_input=True))
            send(copy(t, 1, me, 0, (*near_x, c), from_input=True))
            send(copy(t, 2, me, 1, (*near_y, c), from_input=True))
            send(copy(t, 3, me, 1, (*near_x, c), from_input=True))
            send(copy(t, 4, me, 0, (*near_y, c), from_input=True))
        for t in range(n):
            copy(t, 1, (*near_x, c), 0, me).wait_recv()
            send(copy(t, 5, (*near_x, c), 0, (*near_y, c)))
            copy(t, 2, (*near_y, c), 1, me).wait_recv()
            send(copy(t, 6, (*near_y, c), 1, (*near_x, c)))
        for t in range(n):
            copy(t, 3, (*near_x, c), 1, me).wait_recv()
            send(copy(t, 7, (*near_x, c), None, sibling))
            copy(t, 4, (*near_y, c), 0, me).wait_recv()
            send(copy(t, 8, (*near_y, c), None, sibling))
            copy(t, 5, (*across, c), 0, me).wait_recv()
            copy(t, 6, (*across, c), 1, me).wait_recv()
            send(copy(t, 9, (*across, c), None, sibling))
        for t in range(n):
            copy(t, 0, sibling, None, me).wait_recv()
            for k, chip in ((7, near_x), (8, near_y), (9, across)):
                copy(t, k, (*chip, 1 - c), None, me).wait_recv()
        for cp in sent:
            cp.wait_send()
        for cp in mine:
            cp.wait()

    return pl.pallas_call(
        body, name=name, out_shape=[jax.ShapeDtypeStruct((N_DEV,) + a.shape, a.dtype) for a in shards],
        in_specs=[ANY] * n, out_specs=[ANY] * n,
        scratch_shapes=[pltpu.SemaphoreType.DMA((per * n,)), pltpu.SemaphoreType.DMA((per * n,)),
                        pltpu.SemaphoreType.DMA((n,))],
    )(*shards)


def _carry_sibling(slabs, small=None):
    n = len(slabs)
    extra = [] if small is None else [small]

    def copies(in_refs, out_refs, scr):
        send_sems, recv_sems = scr
        x, y, c = lax.axis_index("x"), lax.axis_index("y"), lax.axis_index("c")
        sibling = (x, y, 1 - c)
        out = []
        for t in range(n):
            for q in range(4):
                out.append(pltpu.make_async_remote_copy(
                    src_ref=in_refs[t].at[2 * q + (1 - c)], dst_ref=out_refs[t].at[q],
                    send_sem=send_sems.at[4 * t + q], recv_sem=recv_sems.at[4 * t + q],
                    device_id=sibling, device_id_type=MESH))
        if extra:
            out.append(pltpu.make_async_remote_copy(
                src_ref=in_refs[n], dst_ref=out_refs[n], send_sem=send_sems.at[4 * n], recv_sem=recv_sems.at[4 * n],
                device_id=sibling, device_id_type=MESH))
        return out

    def start(*refs):
        for cp in copies(*refs):
            cp.start()

    def wait(*refs):
        for cp in copies(*refs):
            cp.wait()

    return _Carry(list(slabs) + extra,
                  [jax.ShapeDtypeStruct((4,) + a.shape[1:], a.dtype) for a in slabs]
                  + [jax.ShapeDtypeStruct(a.shape, a.dtype) for a in extra], _sems(4 * n + 1), start, wait)


def _carry_chips(psums, small_sum=None):
    n = len(psums)
    table = small_sum is not None

    def copies(in_refs, out_refs, scr, arrivals):
        send_sems, recv_sems = scr[0], scr[1]
        x, y, c = lax.axis_index("x"), lax.axis_index("y"), lax.axis_index("c")
        out = []
        for k, (fx, fy) in enumerate(OTHER_CHIPS):
            px, py = x ^ fx, y ^ fy
            for t in range(n):
                out.append(pltpu.make_async_remote_copy(
                    src_ref=in_refs[t].at[2 * px + py], dst_ref=out_refs[t].at[k],
                    send_sem=send_sems.at[3 * t + k], recv_sem=recv_sems.at[3 * t + k],
                    device_id=(px, py, c), device_id_type=MESH))
            if table:
                slot = 2 * px + py if arrivals else 2 * x + y
                out.append(pltpu.make_async_remote_copy(
                    src_ref=in_refs[n], dst_ref=out_refs[n].at[slot], send_sem=send_sems.at[3 * n + k],
                    recv_sem=recv_sems.at[3 * n + k], device_id=(px, py, c), device_id_type=MESH))
        return out

    def own(in_refs, out_refs, scr):
        x, y = lax.axis_index("x"), lax.axis_index("y")
        return pltpu.make_async_copy(in_refs[n], out_refs[n].at[2 * x + y], scr[2])

    def start(in_refs, out_refs, scr):
        if table:
            own(in_refs, out_refs, scr).start()
        for cp in copies(in_refs, out_refs, scr, False):
            cp.start()

    def wait(in_refs, out_refs, scr):
        for cp in copies(in_refs, out_refs, scr, True):
            cp.wait()
        if table:
            own(in_refs, out_refs, scr).wait()

    out_shapes = [jax.ShapeDtypeStruct((3,) + a.shape[1:], a.dtype) for a in psums]
    if table:
        out_shapes.append(jax.ShapeDtypeStruct((4,) + small_sum.shape, F32))
    return _Carry(list(psums) + ([small_sum] if table else []), out_shapes,
                  _sems(3 * n + 3) + ([pltpu.SemaphoreType.DMA] if table else []), start, wait)


def _to_comm(name, kind, block, dtype=BF16):
    a = block[0]
    if kind == "cols":
        a = a.T
        if name == "w_in" and dtype == BF16:
            a = jnp.pad(a, ((0, IN_SHARD_PAD - IN_SHARD), (0, 0)))
    return a if kind == "f32" else a.astype(dtype)


def _from_comm(name, kind, a):
    if kind == "cols":
        if name == "w_in" and a.shape[0] != IN_SHARD:
            a = a[:IN_SHARD]
        a = a.T
    return a[None]


def _assemble_weights(g):
    out = {}
    if "w_in" in g:
        out["wt_main"] = _assemble_wt_main(g["w_in"], name="assemble_w_in")
        j, l0 = divmod(O_F, IN_SHARD)
        out["wt_f"] = jnp.pad(g["w_in"][j, l0:l0 + HEADS], ((0, 128 - HEADS), (0, 0)))
    square = dict(w_branch_a="w_a", w_branch_b="w_b", w_out="w_out", w_ple_gate="w_pg")
    for long, short in square.items():
        if long in g:
            out[short] = g[long].reshape(D_MODEL, D_MODEL)
    if "w_up" in g:
        out["wt_up"] = g["w_up"].reshape(2 * D_FF, D_MODEL)
    if "conv_w" in g:
        out["conv_w"] = g["conv_w"].transpose(1, 0, 2).reshape(3, 2 * D_FF)
    if "w_down" in g:
        out["w_down"] = g["w_down"].reshape(D_FF, D_MODEL)
    if "w_ple" in g:
        out["wt_ple"] = g["w_ple"].reshape(D_MODEL, PLE_DIM)
    return out


def _grad_slabs(gr):
    out = {}
    if "wt_main" in gr:
        gm, gf = gr["wt_main"], gr["wt_f"]
        segments = ((0, 2048, gm, 0), (2048, O_F, gm, 2048), (O_F, O_G, gf, -O_F), (O_G, IN_COLS, gm, 2048 - O_G))
        slabs = []
        for j in range(N_DEV):
            lo, hi = j * IN_SHARD, (j + 1) * IN_SHARD
            pieces = [src[max(lo, a) + shift:min(hi, b) + shift] for a, b, src, shift in segments if max(lo, a) < min(hi, b)]
            pieces.append(jnp.zeros((IN_SHARD_PAD - IN_SHARD, D_MODEL), gm.dtype))
            slabs.append(jnp.concatenate(pieces, axis=0))
        out["w_in"] = jnp.stack(slabs)
    rows = dict(w_a="w_branch_a", w_b="w_branch_b", w_out="w_out", wt_up="w_up", w_down="w_down", w_pg="w_ple_gate")
    for short, long in rows.items():
        if short in gr:
            out[long] = gr[short].reshape(N_DEV, -1, D_MODEL)
    if "conv_w" in gr:
        out["conv_w"] = gr["conv_w"].reshape(3, N_DEV, -1).transpose(1, 0, 2)
    if "wt_ple" in gr:
        out["w_ple"] = gr["wt_ple"].reshape(N_DEV, -1, PLE_DIM)
    return {k: v.astype(BF16) for k, v in out.items()}


def _rows(a, rows):
    flat = a.reshape(-1)
    return jnp.pad(flat, (0, rows * 1024 - flat.shape[0])).reshape(rows, 1024)


def _pack_small(parts):
    return jnp.concatenate([_rows(parts[n].astype(F32), r) for n, r in SMALL], axis=0)


def _small(packed, name, shape):
    off, r = SMALL_OFF[name]
    n = math.prod(shape)
    return packed[off:off + r].reshape(-1)[:n].reshape(shape)


class _Exchanges:
    W_S_ROWS = SMALL_OFF["gmlp_w_s"]

    def __init__(self, later, shards, pos):
        self.later, self.shards, self.pos = later, dict(zip(later, shards)), pos
        self.level1, self.slabs, self.from_sib, self.sums32, self.reduced, self.tables = {}, {}, {}, {}, {}, {}

    def gather1(self, names):
        carry = _carry_gather1([self.shards[n] for n in names])
        carry.names = names
        return carry

    def gather1_done(self, carry, results):
        self.level1.update(zip(carry.names, results))

    def gather2(self):
        return _carry_gather2([self.level1[n] for n in self.later])

    def weights(self, full):
        return _assemble_weights(dict(zip(self.later, full)))

    def sibling(self, grads):
        slabs = _grad_slabs(grads)
        self.slabs.update(slabs)
        carry = _carry_sibling(list(slabs.values()))
        carry.names = list(slabs)
        return carry

    def sibling_done(self, carry, results):
        self.from_sib.update(zip(carry.names, results))

    def chips(self, names, table=None):
        sums = {n: _sum_pairs(self.slabs[n], self.from_sib[n], self.pos, name="sum_sibling_" + n) for n in names}
        self.sums32.update({n: s32 for n, (s32, _) in sums.items()})
        carry = _carry_chips([s16 for _, s16 in sums.values()], None if table is None else self.table_part(table))
        carry.names, carry.table = list(names), table
        return carry

    def chips_done(self, carry, results):
        if carry.table is not None:
            *results, self.tables[carry.table] = results
        self.reduced.update({n: (self.sums32[n], r) for n, r in zip(carry.names, results)})

    def sibling_small(self, small_g):
        self.small_g = small_g
        return _carry_sibling([], small_g)

    def sibling_small_done(self, small_sib):
        self.small_chip = _pair_sum_small(self.small_g, small_sib, name="sum_sibling_small")

    def table_part(self, which):
        off, rows = self.W_S_ROWS
        if which == "w_s":
            return self.small_chip[off:off + rows]
        return jnp.concatenate([self.small_chip[:off], self.small_chip[off + rows:]], axis=0)

    def table(self):
        off = self.W_S_ROWS[0]
        rest = self.tables["rest"]
        return jnp.concatenate([rest[:, :off], self.tables["w_s"], rest[:, off:]], axis=1)


def _local_step(x, p, target, w, sm, ex=None):
    s = x.shape[0]
    mm = _matmul
    wt_main = w["wt_main"]
    conv_b = sm["conv_b"]
    bs_t = jnp.pad(sm["gmlp_b_s"].T, ((0, 0), (0, 128 - GROUPS)))
    b_f = jnp.pad(sm["b_f"], ((0, 0), (0, 128 - HEADS)))
    big = dict(tm=1024, tn=1024, tk=1024)
    whole_s = dict(tn=1024, tk=s)

    h = _rmsnorm_fwd(x, sm["norm_mix_g"], name="norm_mix")
    tall = dict(tm=s, tn=512, tk=1024)
    qkv_args = dict(mode="nt", out_dtype=BF16, name="in_qkv", n=3072, b_off=8, **tall)
    f_logit = mm(h, w["wt_f"], mode="nt", out_dtype=F32, name="in_f", tm=1024, tk=1024)
    cqe = _forget_cumsum(f_logit, b_f, name="forget_cumsum")
    uvg = dict(mode="nt", out_dtype=F32, name="in_uvg", n=4096, **tall)
    if ex is None:
        qkv = mm(h, wt_main, **qkv_args)
        (qa, ka, vt), _ = _attn_prep(qkv, cqe, name="attn_prep")
        (b, lse3), _ = _attn_fwd(qa, ka, vt, name="attn_fwd")
        zuvg = mm(h, wt_main, **uvg)
    else:
        groups = (["w_branch_a"], ["w_branch_b"], [n for n in ex.later if n not in ("w_branch_a", "w_branch_b")])
        carries = [ex.gather1(names) for names in groups]
        qkv, got0 = mm(h, wt_main, carry=carries[0], **qkv_args)
        (qa, ka, vt), got1 = _attn_prep(qkv, cqe, carries[1], name="attn_prep")
        (b, lse3), got2 = _attn_fwd(qa, ka, vt, carries[2], name="attn_fwd")
        for carry, got in zip(carries, (got0, got1, got2)):
            ex.gather1_done(carry, got)
        zuvg, full = mm(h, wt_main, carry=ex.gather2(), **uvg)
        w = {**w, **ex.weights(full)}
    a = _gmlp_fwd(zuvg, sm["gmlp_ln_g"], sm["gmlp_ln_b"], sm["gmlp_w_s"], bs_t, name="gmlp_fwd")
    wt_up, conv_w = w["wt_up"], w["conv_w"]
    ya, yb, merged = _branches_merge(a, b, w["w_a"], w["w_b"], zuvg, name="branches_merge")
    x1, h2 = mm(merged, w["w_out"], mode="nn", out_dtype=F32, name="out_proj", add=x, norm_g=sm["norm_ffn_g"], **big)
    up_a, up_g, act = _up_convglu(h2, wt_up, conv_w, conv_b, name="up_convglu")
    x2, h3 = mm(act, w["w_down"], mode="nn", out_dtype=F32, name="down", tm=1024, tn=1024, tk=1408, add=x1,
                norm_g=sm["norm_ple_g"])

    loss, dx3, dple, dgp, d_norm_final = _ple_loss(p, w["wt_ple"], h3, w["w_pg"], x2, target, sm["norm_final_g"],
                                                   name="ple_loss")
    g_wt_ple = mm(dple, p, mode="tn", out_dtype=BF16, name="d_w_ple", tm=512, tn=256, tk=s)
    g_w_pg = mm(h3, dgp, mode="tn", out_dtype=BF16, name="d_w_pg", tm=512, **whole_s)
    (dx2, dx2b, d_norm_ple), _ = _matmul_rmsnorm_bwd([dgp], w["w_pg"], dx3, x2, sm["norm_ple_g"], mode="nt", tk=1024,
                                                     name="d_h3_norm_ple_bwd")
    g_w_down = mm(act, dx2b, mode="tn", out_dtype=BF16, name="d_w_down", tm=1408, **whole_s)
    dact_args = dict(mode="nt", out_dtype=BF16, name="d_act", tm=s, tn=256, tk=1024)
    if ex is None:
        dact = mm(dx2b, w["w_down"], **dact_args)
    else:
        early = ex.sibling(dict(w_pg=g_w_pg, wt_ple=g_wt_ple))
        dact, got = mm(dx2b, w["w_down"], carry=early, **dact_args)
        ex.sibling_done(early, got)
    dup_a, dup_g, dcw_a, dcw_g, dcb_a, dcb_g = _convglu_bwd(dact, up_a, up_g, conv_w, conv_b, name="convglu_bwd")
    g_wt_up = mm(dup_a, h2, mode="tn", out_dtype=BF16, name="d_w_up_a", tm=1408, out_rows=2 * D_FF, **whole_s)
    g_wt_up = mm(dup_g, h2, mode="tn", out_dtype=BF16, name="d_w_up_g", tm=1408, out_rows=2 * D_FF,
                 o_off=D_FF // 1408, into=g_wt_up, **whole_s)
    (dx1, dx1b, d_norm_ffn), _ = _matmul_rmsnorm_bwd([dup_a, dup_g], wt_up, dx2, x1, sm["norm_ffn_g"], mode="nn",
                                                     tk=1408, name="d_h2_norm_ffn_bwd", resident=True)
    g_w_out = mm(merged, dx1b, mode="tn", out_dtype=BF16, name="d_w_out", tm=512, **whole_s)
    dya, dyb, dga, dgb = _merge_bwd(dx1b, w["w_out"], ya, yb, zuvg, name="merge_bwd")
    g_w_a = mm(a, dya, mode="tn", out_dtype=BF16, name="d_w_a", tm=512, **whole_s)
    g_w_b = mm(b, dyb, mode="tn", out_dtype=BF16, name="d_w_b", tm=512, **whole_s)
    da = mm(dya, w["w_a"], mode="nt", out_dtype=BF16, name="d_a", **big)
    db = mm(dyb, w["w_b"], mode="nt", out_dtype=BF16, name="d_b", **big)
    grads = dict(w_a=g_w_a, w_b=g_w_b, w_out=g_w_out, wt_up=g_wt_up, conv_w=jnp.concatenate([dcw_a, dcw_g], axis=1),
                 w_down=g_w_down, wt_ple=g_wt_ple, w_pg=g_w_pg)
    gmlp_args = (da, zuvg, sm["gmlp_ln_g"], sm["gmlp_ln_b"], sm["gmlp_w_s"], bs_t)
    if ex is None:
        (dzu, dzv, d_w_s, d_bs_t, d_ln_g, d_ln_b), _ = _gmlp_bwd(*gmlp_args, name="gmlp_bwd")
    else:
        rest = ex.sibling({k: v for k, v in grads.items() if k not in ("w_pg", "wt_ple")})
        early_chips = ex.chips(early.names)
        both = _carry_join(rest, early_chips)
        (dzu, dzv, d_w_s, d_bs_t, d_ln_g, d_ln_b), got = _gmlp_bwd(*gmlp_args, both, name="gmlp_bwd")
        got_rest, got_early = both.split(got)
        ex.sibling_done(rest, got_rest)
        ex.chips_done(early_chips, got_early)
    small = dict(norm_mix_g=jnp.zeros((1, D_MODEL), F32), b_f=jnp.zeros((1, HEADS), F32), gmlp_ln_g=d_ln_g,
                 gmlp_ln_b=d_ln_b, gmlp_w_s=d_w_s, gmlp_b_s=d_bs_t[:, :GROUPS].T, norm_ffn_g=d_norm_ffn,
                 conv_b=jnp.concatenate([dcb_a, dcb_g], axis=1), norm_ple_g=d_norm_ple, norm_final_g=d_norm_final)
    if ex is None:
        delta3, _ = _attn_delta(db, b, name="attn_delta")
        (dq, dk, dv, aux, dcq3), _ = _attn_bwd(qa, ka, qkv, db, lse3, delta3, name="attn_bwd")
    else:
        delta3, (small_sib,) = _attn_delta(db, b, ex.sibling_small(_pack_small(small)), name="attn_delta")
        ex.sibling_small_done(small_sib)
        main_chips = ex.chips(rest.names, table="rest")
        (dq, dk, dv, aux, dcq3), got = _attn_bwd(qa, ka, qkv, db, lse3, delta3, main_chips, name="attn_bwd")
        ex.chips_done(main_chips, got)
    dcq16 = jnp.pad(dcq3[:, :2, :].reshape(HEADS, s).T, ((0, 0), (0, 128 - HEADS)))
    dzf, d_b_f = _forget_bwd(dcq16, aux, f_logit, b_f, name="forget_bwd")
    dz_parts = [dzu, dzv, dga, dgb, dq, dk, dv]
    w_s_chips = None if ex is None else ex.chips([], table="w_s")
    g_wt_main, got = _grad_w_parts(dz_parts, h, name="d_w_main", tm=512, carry=w_s_chips)
    if ex is not None:
        ex.chips_done(w_s_chips, got)
    g_wt_f = mm(dzf, h, mode="tn", out_dtype=BF16, name="d_w_f", **whole_s)
    grads = dict(grads, wt_main=g_wt_main, wt_f=g_wt_f)
    w_in_chips = None
    if ex is not None:
        w_in_sib = ex.sibling(dict(wt_main=g_wt_main, wt_f=g_wt_f))
        ex.sibling_done(w_in_sib, _run_carry(w_in_sib, name="exchange_sibling_w_in"))
        w_in_chips = ex.chips(w_in_sib.names)
    (dx0, _, d_norm_mix), got = _matmul_rmsnorm_bwd(dz_parts, wt_main, dx1, x, sm["norm_mix_g"], mode="nn", tk=1024,
                                                    extra=(dzf, w["wt_f"]), name="d_h_norm_mix_bwd", carry=w_in_chips,
                                                    lead=True)
    if ex is not None:
        ex.chips_done(w_in_chips, got)
    return loss, dx0, grads, dict(small, norm_mix_g=d_norm_mix, b_f=d_b_f[:, :HEADS])


def kernel(x, p, norm_mix_g, w_in, b_f, gmlp_ln_g, gmlp_ln_b, gmlp_w_s, gmlp_b_s, w_branch_a, w_branch_b, w_out, norm_ffn_g, w_up, conv_w, conv_b, w_down, norm_ple_g, w_ple, w_ple_gate, norm_final_g, loss_target, m_norm_mix_g, m_w_in, m_b_f, m_gmlp_ln_g, m_gmlp_ln_b, m_gmlp_w_s, m_gmlp_b_s, m_w_branch_a, m_w_branch_b, m_w_out, m_norm_ffn_g, m_w_up, m_conv_w, m_conv_b, m_w_down, m_norm_ple_g, m_w_ple, m_w_ple_gate, m_norm_final_g, v_norm_mix_g, v_w_in, v_b_f, v_gmlp_ln_g, v_gmlp_ln_b, v_gmlp_w_s, v_gmlp_b_s, v_w_branch_a, v_w_branch_b, v_w_out, v_norm_ffn_g, v_w_up, v_conv_w, v_conv_b, v_w_down, v_norm_ple_g, v_w_ple, v_w_ple_gate, v_norm_final_g):
    given = dict(locals())
    weights = {n: given[n] for n in WEIGHT_ORDER}
    mom_m = {n: given["m_" + n] for n in WEIGHT_ORDER}
    mom_v = {n: given["v_" + n] for n in WEIGHT_ORDER}
    pos = jnp.stack([lax.axis_index("x"), lax.axis_index("y"), lax.axis_index("c")]).astype(I32)
    names = [n for n, _ in SHARDED]
    kinds = dict(SHARDED)

    later = [n for n in names if n != "w_in"]

    first = _allgather([_to_comm("w_in", kinds["w_in"], weights["w_in"])], name="allgather_w_in")
    ex = _Exchanges(later, [_to_comm(n, kinds[n], weights[n]) for n in later], pos)

    sm = dict(norm_mix_g=norm_mix_g, b_f=b_f, gmlp_ln_g=gmlp_ln_g, gmlp_ln_b=gmlp_ln_b, gmlp_w_s=gmlp_w_s[0],
              gmlp_b_s=gmlp_b_s[0], norm_ffn_g=norm_ffn_g, conv_b=conv_b, norm_ple_g=norm_ple_g,
              norm_final_g=norm_final_g.reshape(1, D_MODEL))
    loss_part, dx0, grads, small = _local_step(
        x[0], p[0, 0], loss_target[0], _assemble_weights({"w_in": first[0]}), sm, ex)

    b_f_and_loss = jnp.concatenate([small["b_f"].reshape(-1), loss_part[0, :1]])
    last = _allreduce_rows(jnp.concatenate([_rows(small["norm_mix_g"], 8), _rows(b_f_and_loss, 8)], axis=0),
                           name="allreduce_last")
    loss = last[8, HEADS]
    small_last = jnp.pad(last, ((0, SMALL_ROWS - 16), (0, 0)))

    grad, delta, new_m, new_v = {}, {}, {}, {}
    for n in names:
        s32, r = ex.reduced[n]
        outs = _adam_sharded(s32, r, *[_to_comm(n, kinds[n], src[n], F32) for src in (weights, mom_m, mom_v)], pos,
                             name="adam_" + n)
        grad[n], delta[n], new_m[n], new_v[n] = [_from_comm(n, kinds[n], o) for o in outs]
    replicated = [n for n, _ in SMALL]
    rep = lambda src: _pack_small({n: src[n] for n in replicated})
    packed = _adam_replicated(ex.table(), small_last, rep(weights), rep(mom_m), rep(mom_v), name="adam_replicated")
    for out, pk in zip((grad, delta, new_m, new_v), packed):
        for n in replicated:
            out[n] = _small(pk, n, weights[n].shape)

    return (loss, dx0, *[grad[n] for n in WEIGHT_ORDER], *[delta[n] for n in WEIGHT_ORDER],
            *[new_m[n] for n in WEIGHT_ORDER], *[new_v[n] for n in WEIGHT_ORDER])
```

```python
import functools
import math

import jax
import jax.numpy as jnp
from jax import lax
from jax.experimental import pallas as pl
from jax.experimental.pallas import tpu as pltpu

F32 = jnp.float32
BF16 = jnp.bfloat16
I32 = jnp.int32

D_MODEL = 1024
GROUPS = 8
GDIM = 128
GBLOCK = 128
CHUNK = 64
HEADS = 16
HEAD_DIM = 64
D_FF = 2816
PLE_DIM = 256
EPS = 1e-6
N_DEV = 8
ATT_SCALE = HEAD_DIM ** -0.5
NEG = -1e30

ADAM_LR = 0.001
ADAM_B1 = 0.9
ADAM_B2 = 0.999
ADAM_EPS = 1e-08
ADAM_WD = 0.01
ADAM_STEP = 10

V7X_VMEM_LIMIT = 48 * 1024 * 1024
MESH = pl.DeviceIdType.MESH

O_F = 2 * 1024 + 3 * 1024
O_G = O_F + HEADS
IN_COLS = O_G + 2 * D_MODEL
MAIN_COLS = IN_COLS - HEADS
IN_SHARD = IN_COLS // N_DEV
IN_SHARD_PAD = 912

SHARDED = (("w_in", "cols"), ("w_branch_a", "rows"), ("w_branch_b", "rows"), ("w_out", "rows"), ("w_up", "cols"),
           ("conv_w", "f32"), ("w_down", "rows"), ("w_ple", "cols"), ("w_ple_gate", "rows"))

SMALL = (("norm_mix_g", 8), ("b_f", 8), ("gmlp_ln_g", 8), ("gmlp_ln_b", 8), ("gmlp_w_s", 128), ("gmlp_b_s", 8),
         ("norm_ffn_g", 8), ("conv_b", 8), ("norm_ple_g", 8), ("norm_final_g", 8))
SMALL_OFF = {}
_o = 0
for _n, _r in SMALL:
    SMALL_OFF[_n] = (_o, _r)
    _o += _r
SMALL_ROWS = _o

WEIGHT_ORDER = ("norm_mix_g", "w_in", "b_f", "gmlp_ln_g", "gmlp_ln_b", "gmlp_w_s", "gmlp_b_s", "w_branch_a",
                "w_branch_b", "w_out", "norm_ffn_g", "w_up", "conv_w", "conv_b", "w_down", "norm_ple_g", "w_ple",
                "w_ple_gate", "norm_final_g")


def _cparams(sem):
    return pltpu.CompilerParams(dimension_semantics=sem, vmem_limit_bytes=V7X_VMEM_LIMIT)


def _gelu(x):
    c = math.sqrt(2.0 / math.pi)
    return 0.5 * x * (1.0 + jnp.tanh(c * (x + 0.044715 * x * x * x)))


def _gelu_and_grad(x):
    c = math.sqrt(2.0 / math.pi)
    t = jnp.tanh(c * (x + 0.044715 * x * x * x))
    g = 0.5 * x * (1.0 + t)
    dg = 0.5 * (1.0 + t) + 0.5 * x * (1.0 - t * t) * (c * (1.0 + 3.0 * 0.044715 * x * x))
    return g, dg


def _sigmoid(x):
    return 1.0 / (1.0 + jnp.exp(-x))


def _dot(a, b, dims):
    return lax.dot_general(a, b, (dims, ((), ())), preferred_element_type=F32)


NN = ((1,), (0,))
NT = ((1,), (1,))
TN = ((0,), (0,))


def _row_tile(rows, most):
    best = None
    for t in range(16, min(rows, most) + 1, 16):
        if rows % t == 0:
            best = t
    return best if best is not None else rows


def _matmul(a, b, *, mode, out_dtype, name, tm=512, tn=512, tk=512, add=None, n=None, b_off=0,
            out_rows=None, o_off=0, into=None, norm_g=None, carry=None):
    if mode == "tn":
        kdim, m = a.shape
    else:
        m, kdim = a.shape
    if n is None:
        n = b.shape[0] if mode == "nt" else b.shape[1]
    tm, tn, tk = min(tm, m), min(tn, n), min(tk, kdim)
    assert m % tm == 0 and n % tn == 0 and kdim % tk == 0, (name, m, n, kdim, tm, tn, tk)
    nk = kdim // tk
    dims = {"nn": NN, "nt": NT, "tn": TN}[mode]

    n_in = 2 + (add is not None) + (into is not None) + (norm_g is not None)
    assert norm_g is None or tn == n, "the RMS norm needs whole rows"

    def finish(r, refs):
        if add is not None:
            r = refs[2][...].astype(F32) + r
        refs[n_in][...] = r.astype(out_dtype)
        if norm_g is not None:
            rs = lax.rsqrt(jnp.mean(r * r, axis=-1, keepdims=True) + EPS)
            refs[n_in + 1][...] = ((r * rs) * refs[n_in - 1][...]).astype(BF16)

    def body(*refs):
        a_ref, b_ref = refs[:2]
        part = _dot(a_ref[...].astype(BF16), b_ref[...].astype(BF16), dims)
        if nk == 1:
            finish(part, refs)
            return
        acc_ref = refs[-1]
        k = pl.program_id(2)

        @pl.when(k == 0)
        def _():
            acc_ref[...] = part

        @pl.when((k > 0) & (k < nk - 1))
        def _():
            acc_ref[...] += part

        @pl.when(k == nk - 1)
        def _():
            finish(acc_ref[...] + part, refs)

    a_spec = pl.BlockSpec((tk, tm), lambda i, j, k: (k, i)) if mode == "tn" else pl.BlockSpec((tm, tk), lambda i, j, k: (i, k))
    if mode == "nt":
        b_spec = pl.BlockSpec((tn, tk), lambda i, j, k: (j + b_off, k))
    else:
        b_spec = pl.BlockSpec((tk, tn), lambda i, j, k: (k + b_off, j))
    o_spec = pl.BlockSpec((tm, tn), lambda i, j, k: (i + o_off, j))
    in_specs = [a_spec, b_spec] + ([pl.BlockSpec((tm, tn), lambda i, j, k: (i, j))] if add is not None else [])
    args = (a, b) + ((add,) if add is not None else ())
    aliases = {}
    if into is not None:
        aliases = {len(args): 0}
        in_specs.append(pl.BlockSpec(memory_space=pl.ANY))
        args += (into,)
    out_specs = [o_spec]
    out_shape = [jax.ShapeDtypeStruct((m if out_rows is None else out_rows, n), out_dtype)]
    if norm_g is not None:
        in_specs.append(pl.BlockSpec((1, n), lambda i, j, k: (0, 0)))
        args += (norm_g,)
        out_specs.append(pl.BlockSpec((tm, tn), lambda i, j, k: (i, j)))
        out_shape.append(jax.ShapeDtypeStruct((m, n), BF16))
    outs, carried = _carry_call(
        body, carry, name=name, grid=(m // tm, n // tn, nk), in_specs=in_specs, out_specs=out_specs,
        out_shape=out_shape, scratch_shapes=[pltpu.VMEM((tm, tn), F32)] if nk > 1 else [], args=args,
        own_aliases=aliases)
    out = outs[0] if norm_g is None else tuple(outs)
    return out if carry is None else (out, carried)


def _row_spec(tr, width, col_block=0):
    return pl.BlockSpec((tr, width), lambda i: (i, col_block))


def _full_spec(shape):
    return pl.BlockSpec(shape, lambda i: tuple(0 for _ in shape))


def _rmsnorm_fwd(x, g, *, name, tr=256):
    s, d = x.shape

    def body(x_ref, g_ref, o_ref):
        xv = x_ref[...]
        r = lax.rsqrt(jnp.mean(xv * xv, axis=-1, keepdims=True) + EPS)
        o_ref[...] = ((xv * r) * g_ref[...]).astype(BF16)

    return pl.pallas_call(
        body, name=name, grid=(s // tr,),
        in_specs=[_row_spec(tr, d), _full_spec((1, d))], out_specs=_row_spec(tr, d),
        out_shape=jax.ShapeDtypeStruct((s, d), BF16), compiler_params=_cparams(("parallel",)),
    )(x, g)


def _matmul_rmsnorm_bwd(a_parts, b, dres, x, g, *, mode, tk, name, extra=None, tm=512, carry=None, lead=False,
                        resident=False):
    s, d = x.shape
    n_row = s // tm
    spans, lo = [], 0
    for a in a_parts:
        spans.append((lo, lo + a.shape[1] // tk))
        lo = spans[-1][1]
    n_main, total = lo, lo + (extra is not None)
    n_parts = len(a_parts)

    def body(*refs):
        a_refs, b_ref = refs[:n_parts], refs[n_parts]
        k0 = n_parts + 1
        ax_ref, bx_ref = (refs[k0], refs[k0 + 1]) if extra is not None else (None, None)
        k0 += 2 * (extra is not None)
        dres_ref, x_ref, g_ref, dx_ref, dxb_ref, dg_ref, acc_all = refs[k0:k0 + 7]
        if resident:
            kk, i = pl.program_id(0), pl.program_id(1)
            acc_ref = acc_all.at[pl.ds(pl.multiple_of(i * tm, tm), tm)]
        else:
            i, kk = pl.program_id(0), pl.program_id(1)
            acc_ref = acc_all

        def accumulate(part, first):
            if first:
                @pl.when(kk == 0)
                def _():
                    acc_ref[...] = part

                @pl.when(kk > 0)
                def _():
                    acc_ref[...] += part
            else:
                acc_ref[...] += part

        for p, (a_ref, (lo_p, hi_p)) in enumerate(zip(a_refs, spans)):
            @pl.when((kk >= lo_p) & (kk < hi_p))
            def _(a_ref=a_ref, lo_p=lo_p):
                accumulate(_dot(a_ref[...].astype(BF16), b_ref[...].astype(BF16), NN if mode == "nn" else NT), lo_p == 0)

        if extra is not None:
            @pl.when(kk == n_main)
            def _():
                accumulate(_dot(ax_ref[...].astype(BF16), bx_ref[...].astype(BF16), NN), False)

        @pl.when(kk == total - 1)
        def _():
            dhv = acc_ref[...]
            xv = x_ref[...]
            r = lax.rsqrt(jnp.mean(xv * xv, axis=-1, keepdims=True) + EPS)
            xhat = xv * r
            dxhat = dhv * g_ref[...]
            dx = dres_ref[...] + r * (dxhat - xhat * jnp.mean(dxhat * xhat, axis=-1, keepdims=True))
            dx_ref[...] = dx
            dxb_ref[...] = dx.astype(BF16)
            dgp = jnp.sum(dhv * xhat, axis=0, keepdims=True)

            @pl.when(i == 0)
            def _():
                dg_ref[...] = dgp

            @pl.when(i > 0)
            def _():
                dg_ref[...] += dgp

    def spec(shape, index):
        return pl.BlockSpec(shape, (lambda kk, i: index(i, kk)) if resident else index)

    def row(i, kk, lo_p, hi_p):
        if not resident:
            return i
        return jnp.where(kk < lo_p, 0, jnp.where(kk >= hi_p, n_row - 1, i))

    a_specs = [spec((tm, tk), lambda i, kk, lo_p=lo_p, hi_p=hi_p: (row(i, kk, lo_p, hi_p),
                                                                    jnp.clip(kk - lo_p, 0, hi_p - lo_p - 1)))
               for lo_p, hi_p in spans]
    step = lambda kk: jnp.minimum(kk, n_main - 1)
    b_spec = (spec((tk, d), lambda i, kk: (step(kk), 0)) if mode == "nn"
              else spec((d, tk), lambda i, kk: (0, step(kk))))
    rows = spec((tm, d), lambda i, kk: (row(i, kk, total - 1, total), 0))
    one = spec((1, d), lambda i, kk: (0, 0))
    dx_spec, dx_shape = rows, jax.ShapeDtypeStruct((s, d), F32)
    if lead:
        dx_spec = spec((None, tm, d), lambda i, kk: (0, row(i, kk, total - 1, total), 0))
        dx_shape = jax.ShapeDtypeStruct((1, s, d), F32)
    x_specs, x_args = [], []
    if extra is not None:
        kx = extra[0].shape[1]
        x_specs = [spec((tm, kx), lambda i, kk: (row(i, kk, n_main, total), 0)), spec((kx, d), lambda i, kk: (0, 0))]
        x_args = list(extra)
    (dx, dxb, dg), carried = _carry_call(
        body, carry, name=name, grid=(total, n_row) if resident else (n_row, total),
        in_specs=a_specs + [b_spec] + x_specs + [rows, rows, one], out_specs=[dx_spec, rows, one],
        out_shape=[dx_shape, jax.ShapeDtypeStruct((s, d), BF16), jax.ShapeDtypeStruct((1, d), F32)],
        scratch_shapes=[pltpu.VMEM((s if resident else tm, d), F32)], args=list(a_parts) + [b] + x_args + [dres, x, g])
    return (dx, dxb, dg), carried


def _grad_w_parts(a_parts, b, *, name, tm=512, carry=None):
    s, width = a_parts[0].shape
    per, n = width // tm, b.shape[1]

    def body(*refs):
        a_refs, b_ref, o_ref = refs[:len(a_parts)], refs[len(a_parts)], refs[len(a_parts) + 1]
        i = pl.program_id(0)
        for p, a_ref in enumerate(a_refs):
            @pl.when(i // per == p)
            def _(a_ref=a_ref):
                o_ref[...] = _dot(a_ref[...].astype(BF16), b_ref[...].astype(BF16), TN).astype(BF16)

    a_specs = [pl.BlockSpec((s, tm), lambda i, p=p: (0, jnp.clip(i - p * per, 0, per - 1))) for p in range(len(a_parts))]
    (out,), carried = _carry_call(
        body, carry, name=name, grid=(len(a_parts) * per,),
        in_specs=a_specs + [pl.BlockSpec((s, n), lambda i: (0, 0))], out_specs=[pl.BlockSpec((tm, n), lambda i: (i, 0))],
        out_shape=[jax.ShapeDtypeStruct((len(a_parts) * width, n), BF16)], scratch_shapes=[], args=list(a_parts) + [b])
    return out, carried


def _ple_loss(p, wt_ple, h3, w_pg, x2, target, g, *, name, tm=256):
    s, d = x2.shape
    kp = p.shape[1]

    def body(p_ref, wp_ref, h_ref, wg_ref, x_ref, t_ref, g_ref, loss_ref, dx_ref, dple_ref, dgp_ref, dg_ref):
        i = pl.program_id(0)
        ple = _dot(p_ref[...].astype(BF16), wp_ref[...], NT)
        sg = _sigmoid(_dot(h_ref[...], wg_ref[...], NN))
        xv = x_ref[...] + ple * sg
        r = lax.rsqrt(jnp.mean(xv * xv, axis=-1, keepdims=True) + EPS)
        xhat = xv * r
        diff = xhat * g_ref[...] - t_ref[...]
        lp = jnp.zeros((1, 128), F32) + (0.5 / d) * jnp.sum(diff * diff)
        dy = diff * (1.0 / d)
        dxhat = dy * g_ref[...]
        dx = r * (dxhat - xhat * jnp.mean(dxhat * xhat, axis=-1, keepdims=True))
        dx_ref[...] = dx
        dple_ref[...] = (dx * sg).astype(BF16)
        dgp_ref[...] = (dx * ple * (sg * (1.0 - sg))).astype(BF16)
        dgp = jnp.sum(dy * xhat, axis=0, keepdims=True)

        @pl.when(i == 0)
        def _():
            dg_ref[...] = dgp
            loss_ref[...] = lp

        @pl.when(i > 0)
        def _():
            dg_ref[...] += dgp
            loss_ref[...] += lp

    rows = _row_spec(tm, d)
    return pl.pallas_call(
        body, name=name, grid=(s // tm,),
        in_specs=[_row_spec(tm, kp), _full_spec((d, kp)), rows, _full_spec((d, d)), rows, rows, _full_spec((1, d))],
        out_specs=[_full_spec((1, 128)), rows, rows, rows, _full_spec((1, d))],
        out_shape=[jax.ShapeDtypeStruct((1, 128), F32), jax.ShapeDtypeStruct((s, d), F32),
                   jax.ShapeDtypeStruct((s, d), BF16), jax.ShapeDtypeStruct((s, d), BF16),
                   jax.ShapeDtypeStruct((1, d), F32)],
        compiler_params=_cparams(("arbitrary",)),
    )(p, wt_ple, h3, w_pg, x2, target, g)


def _branches_merge(a, b, w_a, w_b, zuvg, *, name, tm=512):
    s, d = a.shape

    def body(a_ref, b_ref, wa_ref, wb_ref, ga_ref, gb_ref, ya_ref, yb_ref, o_ref):
        ya = _dot(a_ref[...], wa_ref[...], NN)
        yb = _dot(b_ref[...], wb_ref[...], NN)
        ya_ref[...] = ya
        yb_ref[...] = yb
        o_ref[...] = (_sigmoid(ga_ref[...]) * ya + _sigmoid(gb_ref[...]) * yb).astype(BF16)

    rows = _row_spec(tm, d)
    return pl.pallas_call(
        body, name=name, grid=(s // tm,),
        in_specs=[rows, rows, _full_spec((d, d)), _full_spec((d, d)), _row_spec(tm, d, 2), _row_spec(tm, d, 3)],
        out_specs=[rows, rows, rows],
        out_shape=[jax.ShapeDtypeStruct((s, d), F32), jax.ShapeDtypeStruct((s, d), F32), jax.ShapeDtypeStruct((s, d), BF16)],
        compiler_params=_cparams(("parallel",)),
    )(a, b, w_a, w_b, zuvg, zuvg)


def _merge_bwd(dx1b, w_out, ya, yb, zuvg, *, name, tm=512):
    s, d = ya.shape

    def body(dx_ref, w_ref, ya_ref, yb_ref, ga_ref, gb_ref, dya_ref, dyb_ref, dga_ref, dgb_ref):
        dmv = _dot(dx_ref[...], w_ref[...], NT)
        sa = _sigmoid(ga_ref[...])
        sb = _sigmoid(gb_ref[...])
        dya_ref[...] = (dmv * sa).astype(BF16)
        dyb_ref[...] = (dmv * sb).astype(BF16)
        dga_ref[...] = (dmv * ya_ref[...] * (sa * (1.0 - sa))).astype(BF16)
        dgb_ref[...] = (dmv * yb_ref[...] * (sb * (1.0 - sb))).astype(BF16)

    rows = _row_spec(tm, d)
    o = jax.ShapeDtypeStruct((s, d), BF16)
    return pl.pallas_call(
        body, name=name, grid=(s // tm,),
        in_specs=[rows, _full_spec((d, d)), rows, rows, _row_spec(tm, d, 2), _row_spec(tm, d, 3)],
        out_specs=[rows] * 4, out_shape=[o, o, o, o], compiler_params=_cparams(("parallel",)),
    )(dx1b, w_out, ya, yb, zuvg, zuvg)


def _masked_ws(ws_ref, g):
    row = lax.broadcasted_iota(I32, (GBLOCK, GBLOCK), 0)
    col = lax.broadcasted_iota(I32, (GBLOCK, GBLOCK), 1)
    keep = (col // CHUNK) <= (row // CHUNK)
    return jnp.where(keep, ws_ref[g], 0.0), keep


def _layernorm_parts(zv):
    mu = jnp.mean(zv, axis=-1, keepdims=True)
    xc = zv - mu
    rs = lax.rsqrt(jnp.mean(xc * xc, axis=-1, keepdims=True) + EPS)
    return xc * rs, rs


def _gmlp_fwd(zuvg, ln_g, ln_b, w_s, bs_t, *, name):
    s, w = zuvg.shape[0], GROUPS * GDIM

    def body(zu_ref, zv_ref, lng_ref, lnb_ref, ws_ref, bs_ref, a_ref):
        zu = _gelu(zu_ref[...])
        zv = _gelu(zv_ref[...])
        xhat, _ = _layernorm_parts(zv)
        vln = (xhat * lng_ref[...] + lnb_ref[...]).astype(BF16)
        for g in range(GROUPS):
            wm, _ = _masked_ws(ws_ref, g)
            mixed = _dot(wm.astype(BF16), vln[:, g * GDIM:(g + 1) * GDIM], NN) + bs_ref[:, g:g + 1]
            a_ref[:, g * GDIM:(g + 1) * GDIM] = (zu[:, g * GDIM:(g + 1) * GDIM] * mixed).astype(BF16)

    return pl.pallas_call(
        body, name=name, grid=(s // GBLOCK,),
        in_specs=[_row_spec(GBLOCK, w, 0), _row_spec(GBLOCK, w, 1), _full_spec((1, w)), _full_spec((1, w)),
                  _full_spec((GROUPS, GBLOCK, GBLOCK)), _full_spec((GBLOCK, 128))],
        out_specs=_row_spec(GBLOCK, w),
        out_shape=jax.ShapeDtypeStruct((s, w), BF16), compiler_params=_cparams(("parallel",)),
    )(zuvg, zuvg, ln_g, ln_b, w_s, bs_t)


def _gmlp_bwd(da, zuvg, ln_g, ln_b, w_s, bs_t, carry=None, *, name):
    s, w = zuvg.shape[0], GROUPS * GDIM

    def body(da_ref, zu_ref, zv_ref, lng_ref, lnb_ref, ws_ref, bs_ref,
             dzu_ref, dzv_ref, dws_ref, dbs_ref, dlng_ref, dlnb_ref, dvln_ref):
        i = pl.program_id(0)
        zu, dzu_g = _gelu_and_grad(zu_ref[...])
        zv, dzv_g = _gelu_and_grad(zv_ref[...])
        xhat, rs = _layernorm_parts(zv)
        vln = (xhat * lng_ref[...] + lnb_ref[...]).astype(BF16)
        dav = da_ref[...].astype(F32)
        lane = lax.broadcasted_iota(I32, (GBLOCK, 128), 1)
        dbs = jnp.zeros((GBLOCK, 128), F32)

        @pl.when(i == 0)
        def _():
            dws_ref[...] = jnp.zeros_like(dws_ref)

        for g in range(GROUPS):
            sl = slice(g * GDIM, (g + 1) * GDIM)
            wm, keep = _masked_ws(ws_ref, g)
            wmb = wm.astype(BF16)
            vg = vln[:, sl]
            mixed = _dot(wmb, vg, NN) + bs_ref[:, g:g + 1]
            dag = dav[:, sl]
            dzu_ref[:, sl] = (dag * mixed * dzu_g[:, sl]).astype(BF16)
            dmix = dag * zu[:, sl]
            dmb = dmix.astype(BF16)
            dws_ref[g] += jnp.where(keep, _dot(dmb, vg, NT), 0.0)
            dbs = jnp.where(lane == g, jnp.sum(dmix, axis=1, keepdims=True), dbs)
            dvln_ref[:, sl] = _dot(wmb, dmb, TN)
        dvln = dvln_ref[...]
        dxhat = dvln * lng_ref[...]
        dzv = rs * (dxhat - jnp.mean(dxhat, axis=-1, keepdims=True)
                    - xhat * jnp.mean(dxhat * xhat, axis=-1, keepdims=True))
        dzv_ref[...] = (dzv * dzv_g).astype(BF16)
        dlng = jnp.sum(dvln * xhat, axis=0, keepdims=True)
        dlnb = jnp.sum(dvln, axis=0, keepdims=True)

        @pl.when(i == 0)
        def _():
            dbs_ref[...] = dbs
            dlng_ref[...] = dlng
            dlnb_ref[...] = dlnb

        @pl.when(i > 0)
        def _():
            dbs_ref[...] += dbs
            dlng_ref[...] += dlng
            dlnb_ref[...] += dlnb

    return _carry_call(
        body, carry, name=name, grid=(s // GBLOCK,),
        in_specs=[_row_spec(GBLOCK, w), _row_spec(GBLOCK, w, 0), _row_spec(GBLOCK, w, 1), _full_spec((1, w)),
                  _full_spec((1, w)), _full_spec((GROUPS, GBLOCK, GBLOCK)), _full_spec((GBLOCK, 128))],
        out_specs=[_row_spec(GBLOCK, w), _row_spec(GBLOCK, w), _full_spec((GROUPS, GBLOCK, GBLOCK)),
                   _full_spec((GBLOCK, 128)), _full_spec((1, w)), _full_spec((1, w))],
        out_shape=[jax.ShapeDtypeStruct((s, w), BF16), jax.ShapeDtypeStruct((s, w), BF16),
                   jax.ShapeDtypeStruct((GROUPS, GBLOCK, GBLOCK), F32), jax.ShapeDtypeStruct((GBLOCK, 128), F32),
                   jax.ShapeDtypeStruct((1, w), F32), jax.ShapeDtypeStruct((1, w), F32)],
        scratch_shapes=[pltpu.VMEM((GBLOCK, w), F32)], args=[da, zuvg, zuvg, ln_g, ln_b, w_s, bs_t])


def _shift_down(u, k):
    row = lax.broadcasted_iota(I32, u.shape, 0)
    return jnp.where(row >= k, pltpu.roll(u, k, 0), 0.0)


def _shift_up(u, k):
    s = u.shape[0]
    row = lax.broadcasted_iota(I32, u.shape, 0)
    return jnp.where(row < s - k, pltpu.roll(u, s - k, 0), 0.0)


def _conv(u, w_ref, b_ref):
    return b_ref[...] + w_ref[0:1, :] * _shift_down(u, 2) + w_ref[1:2, :] * _shift_down(u, 1) + w_ref[2:3, :] * u


def _conv_specs(s, f, tc):
    nc = f // tc
    half = lambda rows: [pl.BlockSpec((rows, tc), lambda j: (0, j)), pl.BlockSpec((rows, tc), lambda j: (0, nc + j))]
    return half(s), half(3), half(1)


def _up_convglu(h2, wt_up, conv_w, conv_b, *, name, tc=256):
    s, d = h2.shape
    f = wt_up.shape[0] // 2
    nc = f // tc
    _, w_specs, b_specs = _conv_specs(s, f, tc)

    def body(h_ref, ta_ref, tg_ref, wa_ref, wg_ref, ba_ref, bg_ref, ua_ref, ug_ref, o_ref):
        ua = _dot(h_ref[...], ta_ref[...], NT)
        ua_ref[...] = ua
        ga = _gelu(_conv(ua, wa_ref, ba_ref))
        ug = _dot(h_ref[...], tg_ref[...], NT)
        ug_ref[...] = ug
        o_ref[...] = (ga * _conv(ug, wg_ref, bg_ref)).astype(BF16)

    col = pl.BlockSpec((s, tc), lambda j: (0, j))
    return pl.pallas_call(
        body, name=name, grid=(nc,),
        in_specs=[_full_spec((s, d)), pl.BlockSpec((tc, d), lambda j: (j, 0)), pl.BlockSpec((tc, d), lambda j: (nc + j, 0))]
        + w_specs + b_specs,
        out_specs=[col, col, col],
        out_shape=[jax.ShapeDtypeStruct((s, f), F32), jax.ShapeDtypeStruct((s, f), F32), jax.ShapeDtypeStruct((s, f), BF16)],
        compiler_params=_cparams(("parallel",)),
    )(h2, wt_up, wt_up, conv_w, conv_w, conv_b, conv_b)


def _convglu_bwd(dact, up_a, up_g, conv_w, conv_b, *, name, tc=256):
    s, f = up_a.shape
    _, w_specs, b_specs = _conv_specs(s, f, tc)
    up_specs = [pl.BlockSpec((s, tc), lambda j: (0, j))] * 2

    def half(dc, taps, w_ref, du_ref, dw_ref, db_ref):
        db_ref[...] = jnp.sum(dc, axis=0, keepdims=True)
        for k in range(3):
            dw_ref[k:k + 1, :] = jnp.sum(dc * taps[k], axis=0, keepdims=True)
        du = w_ref[2:3, :] * dc + w_ref[1:2, :] * _shift_up(dc, 1) + w_ref[0:1, :] * _shift_up(dc, 2)
        du_ref[...] = du.astype(BF16)

    def body(d_ref, ua_ref, ug_ref, wa_ref, wg_ref, ba_ref, bg_ref,
             dua_ref, dug_ref, dwa_ref, dwg_ref, dba_ref, dbg_ref):
        taps_a = (_shift_down(ua_ref[...], 2), _shift_down(ua_ref[...], 1), ua_ref[...])
        taps_g = (_shift_down(ug_ref[...], 2), _shift_down(ug_ref[...], 1), ug_ref[...])
        conv = lambda taps, w_ref, b_ref: b_ref[...] + w_ref[0:1, :] * taps[0] + w_ref[1:2, :] * taps[1] + w_ref[2:3, :] * taps[2]
        ca = conv(taps_a, wa_ref, ba_ref)
        cg = conv(taps_g, wg_ref, bg_ref)
        ga, dga = _gelu_and_grad(ca)
        dv = d_ref[...].astype(F32)
        half(dv * cg * dga, taps_a, wa_ref, dua_ref, dwa_ref, dba_ref)
        half(dv * ga, taps_g, wg_ref, dug_ref, dwg_ref, dbg_ref)

    col, w3, b1 = up_specs[0], w_specs[0], b_specs[0]
    return pl.pallas_call(
        body, name=name, grid=(f // tc,),
        in_specs=[col] + up_specs + w_specs + b_specs, out_specs=[col, col, w3, w3, b1, b1],
        out_shape=[jax.ShapeDtypeStruct((s, f), BF16), jax.ShapeDtypeStruct((s, f), BF16),
                   jax.ShapeDtypeStruct((3, f), F32), jax.ShapeDtypeStruct((3, f), F32),
                   jax.ShapeDtypeStruct((1, f), F32), jax.ShapeDtypeStruct((1, f), F32)],
        compiler_params=_cparams(("parallel",)),
    )(dact, up_a, up_g, conv_w, conv_w, conv_b, conv_b)


def _tri_dot(tri, x):
    b0 = x.astype(BF16)
    r1 = x - b0.astype(F32)
    b1 = r1.astype(BF16)
    b2 = (r1 - b1.astype(F32)).astype(BF16)
    return _dot(tri, b0, NN) + _dot(tri, b1, NN) + _dot(tri, b2, NN)


def _log_sigmoid(x):
    return jnp.minimum(x, 0.0) - jnp.log(1.0 + jnp.exp(-jnp.abs(x)))


def _expand_heads(col16, rows):
    src = lax.broadcasted_iota(I32, (128, HEADS * HEAD_DIM), 0)
    dst = lax.broadcasted_iota(I32, (128, HEADS * HEAD_DIM), 1) // HEAD_DIM
    spread = (src == dst).astype(BF16)
    p0, p1, p2 = _bf16_pieces(col16)
    return (_dot(p0.astype(BF16), spread, NN) + _dot(p1.astype(BF16), spread, NN)) + _dot(p2.astype(BF16), spread, NN)


def _forget_cumsum(f_logit, b_f, *, name):
    s = f_logit.shape[0]
    nb = s // 128

    def body(f_ref, b_ref, cqe_ref):
        row = lax.broadcasted_iota(I32, (128, 128), 0)
        col = lax.broadcasted_iota(I32, (128, 128), 1)
        tri = (col <= row).astype(BF16)

        def step(n, carry):
            r0 = pl.multiple_of(n * 128, 128)
            lf = _log_sigmoid(f_ref[pl.ds(r0, 128), :] + b_ref[...])
            cum = _tri_dot(tri, lf) + carry
            cqe_ref[pl.ds(r0, 128), :] = _expand_heads(cum, 128)
            return cum[127:128, :]

        lax.fori_loop(0, nb, step, jnp.zeros((1, 128), F32))

    return pl.pallas_call(
        body, name=name, grid=(1,),
        in_specs=[_full_spec((s, 128)), _full_spec((1, 128))],
        out_specs=_full_spec((s, HEADS * HEAD_DIM)),
        out_shape=jax.ShapeDtypeStruct((s, HEADS * HEAD_DIM), F32),
        compiler_params=_cparams(("arbitrary",)),
    )(f_logit, b_f)


def _forget_bwd(dcq16, sum_q16, f_logit, b_f, *, name):
    s = f_logit.shape[0]
    nb = s // 128

    def body(a_ref, k_ref, f_ref, b_ref, df_ref, db_ref):
        row = lax.broadcasted_iota(I32, (128, 128), 0)
        col = lax.broadcasted_iota(I32, (128, 128), 1)
        tri_rev = (col >= row).astype(BF16)

        def step(m, carry):
            suffix, dbsum = carry
            n = nb - 1 - m
            r0 = pl.multiple_of(n * 128, 128)
            dcum = a_ref[pl.ds(r0, 128), :] - k_ref[pl.ds(r0, 128), :]
            dlf = _tri_dot(tri_rev, dcum) + suffix
            df = dlf * _sigmoid(-(f_ref[pl.ds(r0, 128), :] + b_ref[...]))
            df_ref[pl.ds(r0, 128), :] = df.astype(BF16)
            return dlf[0:1, :], dbsum + jnp.sum(df, axis=0, keepdims=True)

        _, dbsum = lax.fori_loop(0, nb, step, (jnp.zeros((1, 128), F32), jnp.zeros((1, 128), F32)))
        db_ref[...] = dbsum

    return pl.pallas_call(
        body, name=name, grid=(1,),
        in_specs=[_full_spec((s, 128))] * 3 + [_full_spec((1, 128))],
        out_specs=[_full_spec((s, 128)), _full_spec((1, 128))],
        out_shape=[jax.ShapeDtypeStruct((s, 128), BF16), jax.ShapeDtypeStruct((1, 128), F32)],
        compiler_params=_cparams(("arbitrary",)),
    )(dcq16, sum_q16, f_logit, b_f)


ATT_T = 256


def _head_lanes(rows):
    return lax.broadcasted_iota(I32, (rows, 128), 1) < HEAD_DIM


def _bf16_pieces(c):
    p0 = c.astype(BF16).astype(F32)
    r = c - p0
    p1 = r.astype(BF16).astype(F32)
    p2 = (r - p1).astype(BF16).astype(F32)
    return p0, p1, p2


def _col_reduce(x, op):
    rows = x.shape[0]
    while rows > 8:
        rows //= 2
        x = op(x[:rows], x[rows:])
    return jnp.max(x, axis=0, keepdims=True) if op is jnp.maximum else jnp.sum(x, axis=0, keepdims=True)


def _attn_prep(qkv, cqe, carry=None, *, name):
    s = qkv.shape[0]
    npair = HEADS // 2

    def body(q_ref, k_ref, v_ref, c_ref, qa_ref, ka_ref, vt_ref):
        rows = 128
        lane = lax.broadcasted_iota(I32, (rows, 128), 1)

        def chunk(n, _):
            r0 = pl.multiple_of(n * rows, rows)
            sl = pl.ds(r0, rows)
            qv = q_ref[sl, :].astype(F32) * ATT_SCALE
            kv = k_ref[sl, :].astype(F32)
            p0, p1, p2 = _bf16_pieces(pltpu.roll(c_ref[sl, :], HEAD_DIM, 1))
            for e in range(2):
                mine = (lane < HEAD_DIM) if e == 0 else (lane >= HEAD_DIM)
                base = HEAD_DIM * (1 - e)
                ones_hi = jnp.where((lane >= base + 3) & (lane < base + 6), 1.0, 0.0)
                ones_lo = jnp.where((lane >= base) & (lane < base + 3), 1.0, 0.0)
                qa = jnp.where(mine, qv, jnp.where(lane == base, p0, jnp.where(lane == base + 1, p1,
                               jnp.where(lane == base + 2, p2, ones_hi))))
                ka = jnp.where(mine, kv, jnp.where(lane == base + 3, -p0, jnp.where(lane == base + 4, -p1,
                               jnp.where(lane == base + 5, -p2, ones_lo))))
                qa_ref[e, sl, :] = qa.astype(BF16)
                ka_ref[e, sl, :] = ka.astype(BF16)
            vt_ref[0, :, sl] = v_ref[sl, :].astype(F32).T.astype(BF16)
            return 0

        lax.fori_loop(0, s // rows, chunk, 0)

    pair = pl.BlockSpec((2, s, 128), lambda hp: (hp, 0, 0))
    return _carry_call(
        body, carry, name=name, grid=(npair,),
        in_specs=[pl.BlockSpec((s, 128), lambda hp: (0, hp)), pl.BlockSpec((s, 128), lambda hp: (0, npair + hp)),
                  pl.BlockSpec((s, 128), lambda hp: (0, 2 * npair + hp)), pl.BlockSpec((s, 128), lambda hp: (0, hp))],
        out_specs=[pair, pair, pl.BlockSpec((1, 128, s), lambda hp: (hp, 0, 0))],
        out_shape=[jax.ShapeDtypeStruct((HEADS, s, 128), BF16), jax.ShapeDtypeStruct((HEADS, s, 128), BF16),
                   jax.ShapeDtypeStruct((npair, 128, s), BF16)],
        scratch_shapes=[], args=[qkv, qkv, qkv, cqe])


def _attn_fwd(qa, ka, vt, carry=None, *, name):
    s = qa.shape[1]
    t = 2 * ATT_T
    nq = s // t
    npair = HEADS // 2

    def body(qa_ref, ka_ref, vt_ref, o_ref, lse_ref):
        i = pl.program_id(1)
        krow = lax.broadcasted_iota(I32, (t, t), 0)
        qcol = lax.broadcasted_iota(I32, (t, t), 1)
        sub = lax.broadcasted_iota(I32, (128, t), 0)
        row8 = lax.broadcasted_iota(I32, (8, t), 0)
        qbs = (qa_ref[0], qa_ref[1])
        tk = t

        def step(j, carry, diag):
            c0 = pl.multiple_of(j * tk, tk)
            vtb = vt_ref[0, :, pl.ds(c0, tk)]
            sts = [_dot(ka_ref[e, pl.ds(c0, tk), :], qbs[e], NT) for e in range(2)]
            if diag:
                sts = [jnp.where(krow <= qcol, st, NEG) for st in sts]
            pts, stats = [], []
            for e in range(2):
                m, l, _ = carry[e]
                m_new = jnp.maximum(m, _col_reduce(sts[e], jnp.maximum))
                alpha = jnp.exp(m - m_new)
                pt = jnp.exp(sts[e] - m_new)
                stats.append((m_new, alpha, alpha * l + _col_reduce(pt, jnp.add)))
                pts.append(pt.astype(BF16))
            pvs = [_dot(vtb, pts[e], NN) for e in range(2)]
            return tuple((stats[e][0], stats[e][2], stats[e][1] * carry[e][2] + pvs[e]) for e in range(2))

        init = (jnp.full((1, t), NEG, F32), jnp.zeros((1, t), F32), jnp.zeros((128, t), F32))
        carry = lax.fori_loop(0, i, functools.partial(step, diag=False), (init, init))
        (m0, l0, acc0), (m1, l1, acc1) = step(i, carry, True)
        o_pair = jnp.where(sub < HEAD_DIM, acc0 / l0, acc1 / l1)
        o_ref[...] = o_pair.T.astype(BF16)
        lse_ref[0] = jnp.where(row8 == 0, m0 + jnp.log(l0), jnp.where(row8 == 1, m1 + jnp.log(l1), 0.0))

    return _carry_call(
        body, carry, name=name, grid=(npair, nq),
        in_specs=[pl.BlockSpec((2, t, 128), lambda hp, i: (hp, i, 0)), pl.BlockSpec((2, s, 128), lambda hp, i: (hp, 0, 0)),
                  pl.BlockSpec((1, 128, s), lambda hp, i: (hp, 0, 0))],
        out_specs=[pl.BlockSpec((t, 128), lambda hp, i: (i, hp)), pl.BlockSpec((1, 8, t), lambda hp, i: (hp, 0, i))],
        out_shape=[jax.ShapeDtypeStruct((s, HEADS * HEAD_DIM), BF16), jax.ShapeDtypeStruct((npair, 8, s), F32)],
        scratch_shapes=[], args=[qa, ka, vt])


def _attn_delta(do, o, carry=None, *, name):
    s = do.shape[0]

    def body(do_ref, o_ref, d_ref):
        prod = do_ref[...].astype(F32) * o_ref[...].astype(F32)
        row = lax.broadcasted_iota(I32, (8, 128), 0)
        lane = lax.broadcasted_iota(I32, (8, 128), 1)
        sel = ((row == 0) & (lane < HEAD_DIM) | (row == 1) & (lane >= HEAD_DIM)).astype(BF16)
        p0, p1, p2 = _bf16_pieces(prod)
        d_ref[0] = (_dot(sel, p0.astype(BF16), NT) + _dot(sel, p1.astype(BF16), NT)) + _dot(sel, p2.astype(BF16), NT)

    pair = pl.BlockSpec((s, 128), lambda hp: (0, hp))
    (delta3,), carried = _carry_call(
        body, carry, name=name, grid=(HEADS // 2,), in_specs=[pair, pair],
        out_specs=[pl.BlockSpec((1, 8, s), lambda hp: (hp, 0, 0))],
        out_shape=[jax.ShapeDtypeStruct((HEADS // 2, 8, s), F32)], scratch_shapes=[], args=[do, o])
    return delta3, carried


def _attn_bwd(qa, ka, qkv, do, lse3, delta3, carry=None, *, name):
    s = qa.shape[1]
    t = 2 * ATT_T
    nb = s // t
    npair = HEADS // 2

    def body(qa_ref, ka_ref, v_ref, do_ref, lse_ref, delta_ref, dq_ref, dk_ref, dv_ref, aux_ref, dcq_ref, dqt):
        hp = pl.program_id(0)
        first = _head_lanes(t)
        lane = lax.broadcasted_iota(I32, (t, 128), 1)
        dqt[...] = jnp.zeros_like(dqt)

        @pl.when(hp == 0)
        def _():
            aux_ref[...] = jnp.zeros_like(aux_ref)

        krow = lax.broadcasted_iota(I32, (t, t), 0)
        qcol = lax.broadcasted_iota(I32, (t, t), 1)

        def key_block(j, _):
            c0 = pl.multiple_of(j * t, t)
            vb = v_ref[pl.ds(c0, t), :]
            kbs = (ka_ref[0, pl.ds(c0, t), :], ka_ref[1, pl.ds(c0, t), :])
            kbts = tuple(kb.astype(F32).T.astype(BF16) for kb in kbs)
            vhs = (jnp.where(first, vb, jnp.zeros_like(vb)), jnp.where(first, jnp.zeros_like(vb), vb))

            def query_block(i, carry, diag):
                r0 = pl.multiple_of(i * t, t)
                dob = do_ref[pl.ds(r0, t), :]
                sts = [_dot(kbs[e], qa_ref[e, pl.ds(r0, t), :], NT) for e in range(2)]
                dpts = [_dot(vhs[e], dob, NT) for e in range(2)]
                ptbs, dsbs = [], []
                for e in range(2):
                    st = jnp.where(krow <= qcol, sts[e], NEG) if diag else sts[e]
                    pt = jnp.exp(st - lse_ref[0, e:e + 1, pl.ds(r0, t)])
                    dsbs.append((pt * (dpts[e] - delta_ref[0, e:e + 1, pl.ds(r0, t)])).astype(BF16))
                    ptbs.append(pt.astype(BF16))
                out = []
                for e in range(2):
                    dk_a, dv_a = carry[e]
                    dv_a = dv_a + _dot(ptbs[e], dob, NN)
                    dk_a = dk_a + _dot(dsbs[e], qa_ref[e, pl.ds(r0, t), :], NN)
                    dqt[e, :, pl.ds(r0, t)] += _dot(kbts[e], dsbs[e], NN)
                    out.append((dk_a, dv_a))
                return tuple(out)

            zero = jnp.zeros((t, 128), F32)
            carry = query_block(j, ((zero, zero), (zero, zero)), True)
            (dk0, dv0), (dk1, dv1) = lax.fori_loop(j + 1, nb, functools.partial(query_block, diag=False), carry)
            dk_ref[pl.ds(c0, t), :] = jnp.where(first, dk0, dk1).astype(BF16)
            dv_ref[pl.ds(c0, t), :] = jnp.where(first, dv0, dv1).astype(BF16)
            sum_q = jnp.where(lane == 2 * hp, dk0[:, HEAD_DIM + 3:HEAD_DIM + 4],
                              jnp.where(lane == 2 * hp + 1, dk1[:, 3:4], aux_ref[pl.ds(c0, t), :]))
            aux_ref[pl.ds(c0, t), :] = sum_q
            return 0

        lax.fori_loop(0, nb, key_block, 0)
        sub = lax.broadcasted_iota(I32, (128, s), 0)
        row8 = lax.broadcasted_iota(I32, (8, s), 0)
        dq_ref[...] = (jnp.where(sub < HEAD_DIM, dqt[0], dqt[1]) * ATT_SCALE).T.astype(BF16)
        dcq_ref[0] = jnp.where(row8 == 0, dqt[0, HEAD_DIM:HEAD_DIM + 1, :], jnp.where(row8 == 1, dqt[1, 0:1, :], 0.0))

    def pair_cols(off):
        return pl.BlockSpec((s, 128), lambda hp: (0, off + hp))

    heads = pl.BlockSpec((2, s, 128), lambda hp: (hp, 0, 0))
    rows = pl.BlockSpec((1, 8, s), lambda hp: (hp, 0, 0))
    wide = jax.ShapeDtypeStruct((s, HEADS * HEAD_DIM), BF16)
    return _carry_call(
        body, carry, name=name, grid=(npair,),
        in_specs=[heads, heads, pair_cols(2 * npair), pair_cols(0), rows, rows],
        out_specs=[pair_cols(0), pair_cols(0), pair_cols(0), pl.BlockSpec((s, 128), lambda hp: (0, 0)), rows],
        out_shape=[wide, wide, wide, jax.ShapeDtypeStruct((s, 128), F32), jax.ShapeDtypeStruct((npair, 8, s), F32)],
        scratch_shapes=[pltpu.VMEM((2, 128, s), F32)], args=[qa, ka, qkv, do, lse3, delta3])


def _adam_math(w, g, m, v):
    m = ADAM_B1 * m + (1.0 - ADAM_B1) * g
    v = ADAM_B2 * v + (1.0 - ADAM_B2) * (g * g)
    m_hat = m / (1.0 - ADAM_B1 ** ADAM_STEP)
    v_hat = v / (1.0 - ADAM_B2 ** ADAM_STEP)
    delta = -ADAM_LR * (m_hat / (jnp.sqrt(v_hat) + ADAM_EPS) + ADAM_WD * w)
    return delta, m, v


def _sum_pairs(keep, recv, pos, *, name):
    _, r, c = recv.shape
    tr = _row_tile(r, 512)

    def body(pos_ref, a_ref, b_ref, o32_ref, o16_ref):
        tot = a_ref[...].astype(F32) + b_ref[...].astype(F32)
        o16_ref[...] = tot.astype(BF16)

        @pl.when(pl.program_id(1) == 2 * pos_ref[0] + pos_ref[1])
        def _():
            o32_ref[...] = tot

    out = pl.BlockSpec((1, tr, c), lambda i, q, pos: (q, i, 0))
    grid_spec = pltpu.PrefetchScalarGridSpec(
        num_scalar_prefetch=1, grid=(r // tr, 4),
        in_specs=[pl.BlockSpec((1, tr, c), lambda i, q, pos: (2 * q + pos[2], i, 0)), out],
        out_specs=[pl.BlockSpec((1, tr, c), lambda i, q, pos: (0, i, 0)), out])
    return pl.pallas_call(
        body, name=name, grid_spec=grid_spec,
        out_shape=[jax.ShapeDtypeStruct((1, r, c), F32), jax.ShapeDtypeStruct((4, r, c), BF16)],
        compiler_params=_cparams(("arbitrary", "arbitrary")),
    )(pos, keep, recv)


def _adam_sharded(psum, recv, w, m, v, pos, *, name):
    r, c = w.shape
    rg = psum.shape[1]

    def body(pos_ref, p_ref, r_ref, w_ref, m_ref, v_ref, g_ref, d_ref, mo_ref, vo_ref):
        part = lambda ref, q: ref[q] if rg == r else ref[q, :r, :]
        g = part(p_ref, 0) + part(r_ref, 0).astype(F32) + part(r_ref, 1).astype(F32) + part(r_ref, 2).astype(F32)
        delta, mn, vn = _adam_math(w_ref[...], g, m_ref[...], v_ref[...])
        g_ref[...] = g
        d_ref[...] = delta
        mo_ref[...] = mn
        vo_ref[...] = vn

    if rg == r:
        tr = _row_tile(r, 320)
        grid = (r // tr,)
        row = pl.BlockSpec((tr, c), lambda i, pos: (i, 0))
        sums = lambda n: pl.BlockSpec((n, tr, c), lambda i, pos: (0, i, 0))
    else:
        tc = 256
        grid = (c // tc,)
        row = pl.BlockSpec((r, tc), lambda i, pos: (0, i))
        sums = lambda n: pl.BlockSpec((n, rg, tc), lambda i, pos: (0, 0, i))
    grid_spec = pltpu.PrefetchScalarGridSpec(
        num_scalar_prefetch=1, grid=grid, in_specs=[sums(1), sums(3), row, row, row], out_specs=[row, row, row, row])
    o = jax.ShapeDtypeStruct((r, c), F32)
    return pl.pallas_call(
        body, name=name, grid_spec=grid_spec, out_shape=[o, o, o, o],
        compiler_params=_cparams(("parallel",)),
    )(pos, psum, recv, w, m, v)


def _adam_replicated(chip_sums, last, w, m, v, *, name):
    r = w.shape[0]

    def body(s_ref, l_ref, w_ref, m_ref, v_ref, g_ref, d_ref, mo_ref, vo_ref):
        g = (((s_ref[0] + s_ref[1]) + s_ref[2]) + s_ref[3]) + l_ref[...]
        delta, mn, vn = _adam_math(w_ref[...], g, m_ref[...], v_ref[...])
        g_ref[...] = g
        d_ref[...] = delta
        mo_ref[...] = mn
        vo_ref[...] = vn

    o = jax.ShapeDtypeStruct((r, 1024), F32)
    full = _full_spec((r, 1024))
    return pl.pallas_call(
        body, name=name, grid=(1,),
        in_specs=[_full_spec((4, r, 1024)), full, full, full, full], out_specs=[full] * 4, out_shape=[o] * 4,
        compiler_params=_cparams(("arbitrary",)),
    )(chip_sums, last, w, m, v)


ASM_OUT = 256
ASM_SRC = 304


def _w_in_row(r):
    return r if r < 2048 else (r + O_G - 2048 if r < 4096 else r - 2048)


def _assemble_wt_main(g, *, name):
    table = []
    for blk in range(MAIN_COLS // ASM_OUT):
        j, l0 = divmod(_w_in_row(blk * ASM_OUT), IN_SHARD)
        sb = l0 // ASM_SRC
        n_a = min(ASM_OUT, min(IN_SHARD, (sb + 1) * ASM_SRC) - l0)
        if n_a == ASM_OUT:
            nxt = (j, sb)
        elif l0 + n_a == IN_SHARD:
            nxt = (j + 1, 0)
        else:
            nxt = (j, sb + 1)
        table.append((j, sb, l0 - sb * ASM_SRC, n_a) + nxt)

    def body(tab_ref, a_ref, b_ref, o_ref):
        blk = pl.program_id(0)
        off, n_a = tab_ref[blk, 2], tab_ref[blk, 3]
        r = lax.broadcasted_iota(I32, (ASM_OUT, ASM_SRC), 0)
        k = lax.broadcasted_iota(I32, (ASM_OUT, ASM_SRC), 1)
        sel_a = ((k == r + off) & (r < n_a)).astype(BF16)
        sel_b = ((k == r - n_a) & (r >= n_a)).astype(BF16)
        o_ref[...] = (_dot(sel_a, a_ref[0], NN) + _dot(sel_b, b_ref[0], NN)).astype(BF16)

    src = lambda c: pl.BlockSpec((1, ASM_SRC, D_MODEL), lambda blk, tab: (tab[blk, c], tab[blk, c + 1], 0))
    grid_spec = pltpu.PrefetchScalarGridSpec(
        num_scalar_prefetch=1, grid=(len(table),), in_specs=[src(0), src(4)],
        out_specs=pl.BlockSpec((ASM_OUT, D_MODEL), lambda blk, tab: (blk, 0)))
    return pl.pallas_call(
        body, name=name, grid_spec=grid_spec, out_shape=jax.ShapeDtypeStruct((MAIN_COLS, D_MODEL), BF16),
        compiler_params=_cparams(("parallel",)),
    )(jnp.asarray(table, I32), g, g)


def _pair_sum_small(mine, theirs, *, name):
    def body(a_ref, b_ref, o_ref):
        o_ref[...] = a_ref[...] + b_ref[...]

    full = _full_spec(mine.shape)
    return pl.pallas_call(
        body, name=name, grid=(1,), in_specs=[full, full], out_specs=full,
        out_shape=jax.ShapeDtypeStruct(mine.shape, F32), compiler_params=_cparams(("arbitrary",)),
    )(mine, theirs)


ANY = pl.BlockSpec(memory_space=pl.ANY)
OTHER_CHIPS = ((1, 0), (0, 1), (1, 1))


class _Carry:
    def __init__(self, inputs, out_shapes, scratch, start, wait, aliases=None, middle=None):
        self.inputs, self.out_shapes, self.scratch = list(inputs), list(out_shapes), list(scratch)
        self.start, self.wait, self.aliases, self.middle = start, wait, dict(aliases or {}), middle


def _carry_join(*carries):
    n_in = [len(c.inputs) for c in carries]
    n_out = [len(c.out_shapes) for c in carries]
    n_scr = [len(c.scratch) for c in carries]

    def split(refs, counts):
        out, k = [], 0
        for n in counts:
            out.append(refs[k:k + n])
            k += n
        return out

    def start(ins, outs, scr):
        for c, i, o, s in zip(carries, split(ins, n_in), split(outs, n_out), split(scr, n_scr)):
            c.start(i, o, s)

    def wait(ins, outs, scr):
        for c, i, o, s in zip(carries, split(ins, n_in), split(outs, n_out), split(scr, n_scr)):
            c.wait(i, o, s)

    def middle(ins, outs, scr):
        for c, i, o, s in zip(carries, split(ins, n_in), split(outs, n_out), split(scr, n_scr)):
            if c.middle is not None:
                c.middle(i, o, s)

    aliases = {}
    for k, c in enumerate(carries):
        aliases.update({sum(n_in[:k]) + i: sum(n_out[:k]) + o for i, o in c.aliases.items()})
    joined = _Carry(sum((c.inputs for c in carries), []), sum((c.out_shapes for c in carries), []),
                    sum((c.scratch for c in carries), []), start, wait, aliases,
                    middle if any(c.middle is not None for c in carries) else None)
    joined.counts = n_out
    joined.split = lambda results: split(results, n_out)
    return joined


def _carried(body, carry, n_in, n_out, grid):
    if carry is None:
        return body
    ci, co, cs = len(carry.inputs), len(carry.out_shapes), len(carry.scratch)

    def wrapped(*refs):
        ins, cins = refs[:n_in], refs[n_in:n_in + ci]
        outs, couts = refs[n_in + ci:n_in + ci + n_out], refs[n_in + ci + n_out:n_in + ci + n_out + co]
        rest = refs[n_in + ci + n_out + co:]
        scratch, cscr = rest[:len(rest) - cs], rest[len(rest) - cs:]
        first, last, step, steps = None, None, 0, 1
        for axis, size in enumerate(grid):
            f, l = pl.program_id(axis) == 0, pl.program_id(axis) == size - 1
            first = f if first is None else first & f
            last = l if last is None else last & l
            step, steps = step * size + pl.program_id(axis), steps * size

        @pl.when(first)
        def _():
            carry.start(cins, couts, cscr)

        if carry.middle is not None:
            @pl.when(step == steps // 2)
            def _():
                carry.middle(cins, couts, cscr)

        body(*ins, *outs, *scratch)

        @pl.when(last)
        def _():
            carry.wait(cins, couts, cscr)

    return wrapped


def _carry_call(body, carry, *, name, grid, in_specs, out_specs, out_shape, scratch_shapes, args, vmem=True,
                own_aliases=None):
    n_in, n_out = len(in_specs), len(out_specs)
    extra_in = [ANY] * len(carry.inputs) if carry else []
    extra_out = [ANY] * len(carry.out_shapes) if carry else []
    aliases = dict(own_aliases or {})
    if carry:
        aliases.update({n_in + i: n_out + o for i, o in carry.aliases.items()})
    out = pl.pallas_call(
        _carried(body, carry, n_in, n_out, grid), name=name, grid=grid,
        in_specs=list(in_specs) + extra_in, out_specs=list(out_specs) + extra_out,
        out_shape=list(out_shape) + (carry.out_shapes if carry else []),
        scratch_shapes=list(scratch_shapes) + (carry.scratch if carry else []),
        input_output_aliases=aliases,
        compiler_params=_cparams(("arbitrary",) * len(grid)) if vmem else None,
    )(*args, *(carry.inputs if carry else []))
    return list(out[:n_out]), list(out[n_out:])


def _run_carry(carry, *, name):
    return _carry_call(lambda: None, carry, name=name, grid=(1,), in_specs=[], out_specs=[], out_shape=[],
                       scratch_shapes=[], args=[], vmem=False)[1]


def _sems(n):
    return [pltpu.SemaphoreType.DMA((n,)), pltpu.SemaphoreType.DMA((n,))]


def _carry_gather1(shards):
    n = len(shards)
    per = 7

    def plan(x_refs, out_refs, scr):
        send_sems, recv_sems, local_sems = scr
        x, y, c = lax.axis_index("x"), lax.axis_index("y"), lax.axis_index("c")
        me, sibling = (x, y, c), (x, y, 1 - c)
        near_x, near_y, across = (1 - x, y, c), (x, 1 - y, c), (1 - x, 1 - y, c)

        def rows(ref, t, half):
            r = shards[t].shape[0]
            h = r if r < 32 else -(-(r // 2) // 16) * 16
            if half is None or h == r:
                return ref
            return ref.at[pl.ds(0, h)] if half == 0 else ref.at[pl.ds(h, r - h)]

        def copy(t, k, block, half, to, from_input=False):
            px, py, pc = block
            slab = rows(out_refs[t].at[4 * px + 2 * py + pc], t, half)
            return pltpu.make_async_remote_copy(
                src_ref=rows(x_refs[t], t, half) if from_input else slab, dst_ref=slab,
                send_sem=send_sems.at[per * t + k], recv_sem=recv_sems.at[per * t + k], device_id=to,
                device_id_type=MESH)

        two = [shards[t].shape[0] >= 32 for t in range(n)]
        local = lambda t: pltpu.make_async_copy(x_refs[t], out_refs[t].at[4 * x + 2 * y + c], local_sems.at[t])
        first = lambda t: ([(0, me, None, sibling), (1, me, 0, near_x)]
                           + ([(2, me, 1, near_y), (3, me, 1, near_x)] if two[t] else []) + [(4, me, 0, near_y)])
        passed = lambda t: [(5, near_x, 0, near_y)] + ([(6, near_y, 1, near_x)] if two[t] else [])
        early = lambda t: [(1, near_x, 0, me)] + ([(2, near_y, 1, me)] if two[t] else [])
        late = lambda t: ([(0, sibling, None, me), (4, near_y, 0, me), (5, across, 0, me)]
                          + ([(3, near_x, 1, me), (6, across, 1, me)] if two[t] else []))
        return copy, local, first, passed, early, late

    def start(x_refs, out_refs, scr):
        copy, local, first, _, _, _ = plan(x_refs, out_refs, scr)
        for urgent in (True, False):
            for t in range(n):
                if not urgent:
                    local(t).start()
                for k, block, half, to in first(t):
                    if (k in (1, 2)) == urgent:
                        copy(t, k, block, half, to, from_input=True).start()

    def middle(x_refs, out_refs, scr):
        copy, _, _, passed, early, _ = plan(x_refs, out_refs, scr)
        for t in range(n):
            for (k, block, half, to), fwd in zip(early(t), passed(t)):
                copy(t, k, block, half, to).wait_recv()
                copy(t, *fwd).start()

    def wait(x_refs, out_refs, scr):
        copy, local, first, passed, _, late = plan(x_refs, out_refs, scr)
        for t in range(n):
            for k, block, half, to in late(t):
                copy(t, k, block, half, to).wait_recv()
        for t in range(n):
            for k, block, half, to in first(t):
                copy(t, k, block, half, to, from_input=True).wait_send()
            for k, block, half, to in passed(t):
                copy(t, k, block, half, to).wait_send()
            local(t).wait()

    return _Carry(shards, [jax.ShapeDtypeStruct((N_DEV,) + a.shape, a.dtype) for a in shards],
                  _sems(per * n) + [pltpu.SemaphoreType.DMA((n,))], start, wait, middle=middle)


def _carry_gather2(gathered):
    n = len(gathered)

    def copies(in_refs, g_refs, scr, with_arrivals):
        send_sems, recv_sems = scr
        x, y, c = lax.axis_index("x"), lax.axis_index("y"), lax.axis_index("c")
        sends, arrivals = [], []
        for t in range(n):
            for j, (fx, fy) in enumerate(OTHER_CHIPS):
                px, py = x ^ fx, y ^ fy
                sems = dict(send_sem=send_sems.at[3 * t + j], recv_sem=recv_sems.at[3 * t + j],
                            device_id=(x, y, 1 - c), device_id_type=MESH)
                mine, theirs = 4 * px + 2 * py + c, 4 * px + 2 * py + (1 - c)
                sends.append(pltpu.make_async_remote_copy(src_ref=in_refs[t].at[mine], dst_ref=g_refs[t].at[mine], **sems))
                if with_arrivals:
                    arrivals.append(pltpu.make_async_remote_copy(
                        src_ref=in_refs[t].at[mine], dst_ref=g_refs[t].at[theirs], **sems))
        return sends, arrivals

    def start(in_refs, g_refs, scr):
        for cp in copies(in_refs, g_refs, scr, False)[0]:
            cp.start()

    def wait(in_refs, g_refs, scr):
        sends, arrivals = copies(in_refs, g_refs, scr, True)
        for cp in arrivals:
            cp.wait_recv()
        for cp in sends:
            cp.wait_send()

    return _Carry(gathered, [jax.ShapeDtypeStruct(a.shape, a.dtype) for a in gathered], _sems(3 * n), start, wait,
                  aliases={t: t for t in range(n)})


def _allreduce_rows(x, *, name):
    def body(x_ref, o_ref, sib_ref, mine_ref, tab_ref, send_sems, recv_sems):
        x, y, c = lax.axis_index("x"), lax.axis_index("y"), lax.axis_index("c")
        swap = pltpu.make_async_remote_copy(src_ref=x_ref, dst_ref=sib_ref, send_sem=send_sems.at[0],
                                            recv_sem=recv_sems.at[0], device_id=(x, y, 1 - c), device_id_type=MESH)
        swap.start()
        swap.wait()
        mine_ref[...] = x_ref[...] + sib_ref[...]
        tab_ref[pl.ds(2 * x + y, 1)] = mine_ref[...][None]

        def copy(k, slot):
            fx, fy = OTHER_CHIPS[k]
            return pltpu.make_async_remote_copy(
                src_ref=mine_ref, dst_ref=tab_ref.at[slot], send_sem=send_sems.at[1 + k], recv_sem=recv_sems.at[1 + k],
                device_id=(x ^ fx, y ^ fy, c), device_id_type=MESH)

        for k in range(3):
            copy(k, 2 * x + y).start()
        for k, (fx, fy) in enumerate(OTHER_CHIPS):
            copy(k, 2 * (x ^ fx) + (y ^ fy)).wait()
        o_ref[...] = ((tab_ref[0] + tab_ref[1]) + tab_ref[2]) + tab_ref[3]

    vmem = pl.BlockSpec(memory_space=pltpu.VMEM)
    return pl.pallas_call(
        body, name=name, out_shape=jax.ShapeDtypeStruct(x.shape, F32), in_specs=[vmem], out_specs=vmem,
        scratch_shapes=[pltpu.VMEM(x.shape, F32), pltpu.VMEM(x.shape, F32), pltpu.VMEM((4,) + x.shape, F32)] + _sems(4),
    )(x)


def _allgather(shards, *, name):
    n = len(shards)
    per = 10

    def body(*refs):
        x_refs, out_refs = refs[:n], refs[n:2 * n]
        send_sems, recv_sems, local_sems = refs[2 * n:]
        x, y, c = lax.axis_index("x"), lax.axis_index("y"), lax.axis_index("c")
        me, sibling = (x, y, c), (x, y, 1 - c)
        near_x, near_y, across = (1 - x, y), (x, 1 - y), (1 - x, 1 - y)

        def rows(ref, t, half):
            r = shards[t].shape[0]
            h = -(-(r // 2) // 16) * 16
            if half is None:
                return ref
            return ref.at[pl.ds(0, h)] if half == 0 else ref.at[pl.ds(h, r - h)]

        def copy(t, k, block, half, to, from_input=False):
            px, py, pc = block
            slab = rows(out_refs[t].at[4 * px + 2 * py + pc], t, half)
            return pltpu.make_async_remote_copy(
                src_ref=rows(x_refs[t], t, half) if from_input else slab, dst_ref=slab,
                send_sem=send_sems.at[per * t + k], recv_sem=recv_sems.at[per * t + k], device_id=to,
                device_id_type=MESH)

        mine = [pltpu.make_async_copy(x_refs[t], out_refs[t].at[4 * x + 2 * y + c], local_sems.at[t]) for t in range(n)]
        for cp in mine:
            cp.start()
        sent = []

        def send(cp):
            cp.start()
            sent.append(cp)

        for t in range(n):
            send(copy(t, 0, me, None, sibling, from_input=True))
            send(copy(t, 1, me, 0, (*near_x, c), from_input=True))
            send(copy(t, 2, me, 1, (*near_y, c), from_input=True))
            send(copy(t, 3, me, 1, (*near_x, c), from_input=True))
            send(copy(t, 4, me, 0, (*near_y, c), from_input=True))
        for t in range(n):
            copy(t, 1, (*near_x, c), 0, me).wait_recv()
            send(copy(t, 5, (*near_x, c), 0, (*near_y, c)))
            copy(t, 2, (*near_y, c), 1, me).wait_recv()
            send(copy(t, 6, (*near_y, c), 1, (*near_x, c)))
        for t in range(n):
            copy(t, 3, (*near_x, c), 1, me).wait_recv()
            send(copy(t, 7, (*near_x, c), None, sibling))
            copy(t, 4, (*near_y, c), 0, me).wait_recv()
            send(copy(t, 8, (*near_y, c), None, sibling))
            copy(t, 5, (*across, c), 0, me).wait_recv()
            copy(t, 6, (*across, c), 1, me).wait_recv()
            send(copy(t, 9, (*across, c), None, sibling))
        for t in range(n):
            copy(t, 0, sibling, None, me).wait_recv()
            for k, chip in ((7, near_x), (8, near_y), (9, across)):
                copy(t, k, (*chip, 1 - c), None, me).wait_recv()
        for cp in sent:
            cp.wait_send()
        for cp in mine:
            cp.wait()

    return pl.pallas_call(
        body, name=name, out_shape=[jax.ShapeDtypeStruct((N_DEV,) + a.shape, a.dtype) for a in shards],
        in_specs=[ANY] * n, out_specs=[ANY] * n,
        scratch_shapes=[pltpu.SemaphoreType.DMA((per * n,)), pltpu.SemaphoreType.DMA((per * n,)),
                        pltpu.SemaphoreType.DMA((n,))],
    )(*shards)


def _carry_sibling(slabs, small=None):
    n = len(slabs)
    extra = [] if small is None else [small]

    def copies(in_refs, out_refs, scr):
        send_sems, recv_sems = scr
        x, y, c = lax.axis_index("x"), lax.axis_index("y"), lax.axis_index("c")
        sibling = (x, y, 1 - c)
        out = []
        for t in range(n):
            for q in range(4):
                out.append(pltpu.make_async_remote_copy(
                    src_ref=in_refs[t].at[2 * q + (1 - c)], dst_ref=out_refs[t].at[q],
                    send_sem=send_sems.at[4 * t + q], recv_sem=recv_sems.at[4 * t + q],
                    device_id=sibling, device_id_type=MESH))
        if extra:
            out.append(pltpu.make_async_remote_copy(
                src_ref=in_refs[n], dst_ref=out_refs[n], send_sem=send_sems.at[4 * n], recv_sem=recv_sems.at[4 * n],
                device_id=sibling, device_id_type=MESH))
        return out

    def start(*refs):
        for cp in copies(*refs):
            cp.start()

    def wait(*refs):
        for cp in copies(*refs):
            cp.wait()

    return _Carry(list(slabs) + extra,
                  [jax.ShapeDtypeStruct((4,) + a.shape[1:], a.dtype) for a in slabs]
                  + [jax.ShapeDtypeStruct(a.shape, a.dtype) for a in extra], _sems(4 * n + 1), start, wait)


def _carry_chips(psums, small_sum=None):
    n = len(psums)
    table = small_sum is not None

    def copies(in_refs, out_refs, scr, arrivals):
        send_sems, recv_sems = scr[0], scr[1]
        x, y, c = lax.axis_index("x"), lax.axis_index("y"), lax.axis_index("c")
        out = []
        for k, (fx, fy) in enumerate(OTHER_CHIPS):
            px, py = x ^ fx, y ^ fy
            for t in range(n):
                out.append(pltpu.make_async_remote_copy(
                    src_ref=in_refs[t].at[2 * px + py], dst_ref=out_refs[t].at[k],
                    send_sem=send_sems.at[3 * t + k], recv_sem=recv_sems.at[3 * t + k],
                    device_id=(px, py, c), device_id_type=MESH))
            if table:
                slot = 2 * px + py if arrivals else 2 * x + y
                out.append(pltpu.make_async_remote_copy(
                    src_ref=in_refs[n], dst_ref=out_refs[n].at[slot], send_sem=send_sems.at[3 * n + k],
                    recv_sem=recv_sems.at[3 * n + k], device_id=(px, py, c), device_id_type=MESH))
        return out

    def own(in_refs, out_refs, scr):
        x, y = lax.axis_index("x"), lax.axis_index("y")
        return pltpu.make_async_copy(in_refs[n], out_refs[n].at[2 * x + y], scr[2])

    def start(in_refs, out_refs, scr):
        if table:
            own(in_refs, out_refs, scr).start()
        for cp in copies(in_refs, out_refs, scr, False):
            cp.start()

    def wait(in_refs, out_refs, scr):
        for cp in copies(in_refs, out_refs, scr, True):
            cp.wait()
        if table:
            own(in_refs, out_refs, scr).wait()

    out_shapes = [jax.ShapeDtypeStruct((3,) + a.shape[1:], a.dtype) for a in psums]
    if table:
        out_shapes.append(jax.ShapeDtypeStruct((4,) + small_sum.shape, F32))
    return _Carry(list(psums) + ([small_sum] if table else []), out_shapes,
                  _sems(3 * n + 3) + ([pltpu.SemaphoreType.DMA] if table else []), start, wait)


def _to_comm(name, kind, block, dtype=BF16):
    a = block[0]
    if kind == "cols":
        a = a.T
        if name == "w_in" and dtype == BF16:
            a = jnp.pad(a, ((0, IN_SHARD_PAD - IN_SHARD), (0, 0)))
    return a if kind == "f32" else a.astype(dtype)


def _from_comm(name, kind, a):
    if kind == "cols":
        if name == "w_in" and a.shape[0] != IN_SHARD:
            a = a[:IN_SHARD]
        a = a.T
    return a[None]


def _assemble_weights(g):
    out = {}
    if "w_in" in g:
        out["wt_main"] = _assemble_wt_main(g["w_in"], name="assemble_w_in")
        j, l0 = divmod(O_F, IN_SHARD)
        out["wt_f"] = jnp.pad(g["w_in"][j, l0:l0 + HEADS], ((0, 128 - HEADS), (0, 0)))
    square = dict(w_branch_a="w_a", w_branch_b="w_b", w_out="w_out", w_ple_gate="w_pg")
    for long, short in square.items():
        if long in g:
            out[short] = g[long].reshape(D_MODEL, D_MODEL)
    if "w_up" in g:
        out["wt_up"] = g["w_up"].reshape(2 * D_FF, D_MODEL)
    if "conv_w" in g:
        out["conv_w"] = g["conv_w"].transpose(1, 0, 2).reshape(3, 2 * D_FF)
    if "w_down" in g:
        out["w_down"] = g["w_down"].reshape(D_FF, D_MODEL)
    if "w_ple" in g:
        out["wt_ple"] = g["w_ple"].reshape(D_MODEL, PLE_DIM)
    return out


def _grad_slabs(gr):
    out = {}
    if "wt_main" in gr:
        gm, gf = gr["wt_main"], gr["wt_f"]
        segments = ((0, 2048, gm, 0), (2048, O_F, gm, 2048), (O_F, O_G, gf, -O_F), (O_G, IN_COLS, gm, 2048 - O_G))
        slabs = []
        for j in range(N_DEV):
            lo, hi = j * IN_SHARD, (j + 1) * IN_SHARD
            pieces = [src[max(lo, a) + shift:min(hi, b) + shift] for a, b, src, shift in segments if max(lo, a) < min(hi, b)]
            pieces.append(jnp.zeros((IN_SHARD_PAD - IN_SHARD, D_MODEL), gm.dtype))
            slabs.append(jnp.concatenate(pieces, axis=0))
        out["w_in"] = jnp.stack(slabs)
    rows = dict(w_a="w_branch_a", w_b="w_branch_b", w_out="w_out", wt_up="w_up", w_down="w_down", w_pg="w_ple_gate")
    for short, long in rows.items():
        if short in gr:
            out[long] = gr[short].reshape(N_DEV, -1, D_MODEL)
    if "conv_w" in gr:
        out["conv_w"] = gr["conv_w"].reshape(3, N_DEV, -1).transpose(1, 0, 2)
    if "wt_ple" in gr:
        out["w_ple"] = gr["wt_ple"].reshape(N_DEV, -1, PLE_DIM)
    return {k: v.astype(BF16) for k, v in out.items()}


def _rows(a, rows):
    flat = a.reshape(-1)
    return jnp.pad(flat, (0, rows * 1024 - flat.shape[0])).reshape(rows, 1024)


def _pack_small(parts):
    return jnp.concatenate([_rows(parts[n].astype(F32), r) for n, r in SMALL], axis=0)


def _small(packed, name, shape):
    off, r = SMALL_OFF[name]
    n = math.prod(shape)
    return packed[off:off + r].reshape(-1)[:n].reshape(shape)


class _Exchanges:
    W_S_ROWS = SMALL_OFF["gmlp_w_s"]

    def __init__(self, later, shards, pos):
        self.later, self.shards, self.pos = later, dict(zip(later, shards)), pos
        self.level1, self.slabs, self.from_sib, self.sums32, self.reduced, self.tables = {}, {}, {}, {}, {}, {}

    def gather1(self, names):
        carry = _carry_gather1([self.shards[n] for n in names])
        carry.names = names
        return carry

    def gather1_done(self, carry, results):
        self.level1.update(zip(carry.names, results))

    def gather2(self):
        return _carry_gather2([self.level1[n] for n in self.later])

    def weights(self, full):
        return _assemble_weights(dict(zip(self.later, full)))

    def sibling(self, grads):
        slabs = _grad_slabs(grads)
        self.slabs.update(slabs)
        carry = _carry_sibling(list(slabs.values()))
        carry.names = list(slabs)
        return carry

    def sibling_done(self, carry, results):
        self.from_sib.update(zip(carry.names, results))

    def chips(self, names, table=None):
        sums = {n: _sum_pairs(self.slabs[n], self.from_sib[n], self.pos, name="sum_sibling_" + n) for n in names}
        self.sums32.update({n: s32 for n, (s32, _) in sums.items()})
        carry = _carry_chips([s16 for _, s16 in sums.values()], None if table is None else self.table_part(table))
        carry.names, carry.table = list(names), table
        return carry

    def chips_done(self, carry, results):
        if carry.table is not None:
            *results, self.tables[carry.table] = results
        self.reduced.update({n: (self.sums32[n], r) for n, r in zip(carry.names, results)})

    def sibling_small(self, small_g):
        self.small_g = small_g
        return _carry_sibling([], small_g)

    def sibling_small_done(self, small_sib):
        self.small_chip = _pair_sum_small(self.small_g, small_sib, name="sum_sibling_small")

    def table_part(self, which):
        off, rows = self.W_S_ROWS
        if which == "w_s":
            return self.small_chip[off:off + rows]
        return jnp.concatenate([self.small_chip[:off], self.small_chip[off + rows:]], axis=0)

    def table(self):
        off = self.W_S_ROWS[0]
        rest = self.tables["rest"]
        return jnp.concatenate([rest[:, :off], self.tables["w_s"], rest[:, off:]], axis=1)


def _local_step(x, p, target, w, sm, ex=None):
    s = x.shape[0]
    mm = _matmul
    wt_main = w["wt_main"]
    conv_b = sm["conv_b"]
    bs_t = jnp.pad(sm["gmlp_b_s"].T, ((0, 0), (0, 128 - GROUPS)))
    b_f = jnp.pad(sm["b_f"], ((0, 0), (0, 128 - HEADS)))
    big = dict(tm=1024, tn=1024, tk=1024)
    whole_s = dict(tn=1024, tk=s)

    h = _rmsnorm_fwd(x, sm["norm_mix_g"], name="norm_mix")
    tall = dict(tm=s, tn=512, tk=1024)
    qkv_args = dict(mode="nt", out_dtype=BF16, name="in_qkv", n=3072, b_off=8, **tall)
    f_logit = mm(h, w["wt_f"], mode="nt", out_dtype=F32, name="in_f", tm=1024, tk=1024)
    cqe = _forget_cumsum(f_logit, b_f, name="forget_cumsum")
    uvg = dict(mode="nt", out_dtype=F32, name="in_uvg", n=4096, **tall)
    if ex is None:
        qkv = mm(h, wt_main, **qkv_args)
        (qa, ka, vt), _ = _attn_prep(qkv, cqe, name="attn_prep")
        (b, lse3), _ = _attn_fwd(qa, ka, vt, name="attn_fwd")
        zuvg = mm(h, wt_main, **uvg)
    else:
        groups = (["w_branch_a"], ["w_branch_b"], [n for n in ex.later if n not in ("w_branch_a", "w_branch_b")])
        carries = [ex.gather1(names) for names in groups]
        qkv, got0 = mm(h, wt_main, carry=carries[0], **qkv_args)
        (qa, ka, vt), got1 = _attn_prep(qkv, cqe, carries[1], name="attn_prep")
        (b, lse3), got2 = _attn_fwd(qa, ka, vt, carries[2], name="attn_fwd")
        for carry, got in zip(carries, (got0, got1, got2)):
            ex.gather1_done(carry, got)
        zuvg, full = mm(h, wt_main, carry=ex.gather2(), **uvg)
        w = {**w, **ex.weights(full)}
    a = _gmlp_fwd(zuvg, sm["gmlp_ln_g"], sm["gmlp_ln_b"], sm["gmlp_w_s"], bs_t, name="gmlp_fwd")
    wt_up, conv_w = w["wt_up"], w["conv_w"]
    ya, yb, merged = _branches_merge(a, b, w["w_a"], w["w_b"], zuvg, name="branches_merge")
    x1, h2 = mm(merged, w["w_out"], mode="nn", out_dtype=F32, name="out_proj", add=x, norm_g=sm["norm_ffn_g"], **big)
    up_a, up_g, act = _up_convglu(h2, wt_up, conv_w, conv_b, name="up_convglu")
    x2, h3 = mm(act, w["w_down"], mode="nn", out_dtype=F32, name="down", tm=1024, tn=1024, tk=1408, add=x1,
                norm_g=sm["norm_ple_g"])

    loss, dx3, dple, dgp, d_norm_final = _ple_loss(p, w["wt_ple"], h3, w["w_pg"], x2, target, sm["norm_final_g"],
                                                   name="ple_loss")
    g_wt_ple = mm(dple, p, mode="tn", out_dtype=BF16, name="d_w_ple", tm=512, tn=256, tk=s)
    g_w_pg = mm(h3, dgp, mode="tn", out_dtype=BF16, name="d_w_pg", tm=512, **whole_s)
    (dx2, dx2b, d_norm_ple), _ = _matmul_rmsnorm_bwd([dgp], w["w_pg"], dx3, x2, sm["norm_ple_g"], mode="nt", tk=1024,
                                                     name="d_h3_norm_ple_bwd")
    g_w_down = mm(act, dx2b, mode="tn", out_dtype=BF16, name="d_w_down", tm=1408, **whole_s)
    dact_args = dict(mode="nt", out_dtype=BF16, name="d_act", tm=s, tn=256, tk=1024)
    if ex is None:
        dact = mm(dx2b, w["w_down"], **dact_args)
    else:
        early = ex.sibling(dict(w_pg=g_w_pg, wt_ple=g_wt_ple))
        dact, got = mm(dx2b, w["w_down"], carry=early, **dact_args)
        ex.sibling_done(early, got)
    dup_a, dup_g, dcw_a, dcw_g, dcb_a, dcb_g = _convglu_bwd(dact, up_a, up_g, conv_w, conv_b, name="convglu_bwd")
    g_wt_up = mm(dup_a, h2, mode="tn", out_dtype=BF16, name="d_w_up_a", tm=1408, out_rows=2 * D_FF, **whole_s)
    g_wt_up = mm(dup_g, h2, mode="tn", out_dtype=BF16, name="d_w_up_g", tm=1408, out_rows=2 * D_FF,
                 o_off=D_FF // 1408, into=g_wt_up, **whole_s)
    (dx1, dx1b, d_norm_ffn), _ = _matmul_rmsnorm_bwd([dup_a, dup_g], wt_up, dx2, x1, sm["norm_ffn_g"], mode="nn",
                                                     tk=1408, name="d_h2_norm_ffn_bwd", resident=True)
    g_w_out = mm(merged, dx1b, mode="tn", out_dtype=BF16, name="d_w_out", tm=512, **whole_s)
    dya, dyb, dga, dgb = _merge_bwd(dx1b, w["w_out"], ya, yb, zuvg, name="merge_bwd")
    g_w_a = mm(a, dya, mode="tn", out_dtype=BF16, name="d_w_a", tm=512, **whole_s)
    g_w_b = mm(b, dyb, mode="tn", out_dtype=BF16, name="d_w_b", tm=512, **whole_s)
    da = mm(dya, w["w_a"], mode="nt", out_dtype=BF16, name="d_a", **big)
    db = mm(dyb, w["w_b"], mode="nt", out_dtype=BF16, name="d_b", **big)
    grads = dict(w_a=g_w_a, w_b=g_w_b, w_out=g_w_out, wt_up=g_wt_up, conv_w=jnp.concatenate([dcw_a, dcw_g], axis=1),
                 w_down=g_w_down, wt_ple=g_wt_ple, w_pg=g_w_pg)
    gmlp_args = (da, zuvg, sm["gmlp_ln_g"], sm["gmlp_ln_b"], sm["gmlp_w_s"], bs_t)
    if ex is None:
        (dzu, dzv, d_w_s, d_bs_t, d_ln_g, d_ln_b), _ = _gmlp_bwd(*gmlp_args, name="gmlp_bwd")
    else:
        rest = ex.sibling({k: v for k, v in grads.items() if k not in ("w_pg", "wt_ple")})
        early_chips = ex.chips(early.names)
        both = _carry_join(rest, early_chips)
        (dzu, dzv, d_w_s, d_bs_t, d_ln_g, d_ln_b), got = _gmlp_bwd(*gmlp_args, both, name="gmlp_bwd")
        got_rest, got_early = both.split(got)
        ex.sibling_done(rest, got_rest)
        ex.chips_done(early_chips, got_early)
    small = dict(norm_mix_g=jnp.zeros((1, D_MODEL), F32), b_f=jnp.zeros((1, HEADS), F32), gmlp_ln_g=d_ln_g,
                 gmlp_ln_b=d_ln_b, gmlp_w_s=d_w_s, gmlp_b_s=d_bs_t[:, :GROUPS].T, norm_ffn_g=d_norm_ffn,
                 conv_b=jnp.concatenate([dcb_a, dcb_g], axis=1), norm_ple_g=d_norm_ple, norm_final_g=d_norm_final)
    if ex is None:
        delta3, _ = _attn_delta(db, b, name="attn_delta")
        (dq, dk, dv, aux, dcq3), _ = _attn_bwd(qa, ka, qkv, db, lse3, delta3, name="attn_bwd")
    else:
        delta3, (small_sib,) = _attn_delta(db, b, ex.sibling_small(_pack_small(small)), name="attn_delta")
        ex.sibling_small_done(small_sib)
        main_chips = ex.chips(rest.names, table="rest")
        (dq, dk, dv, aux, dcq3), got = _attn_bwd(qa, ka, qkv, db, lse3, delta3, main_chips, name="attn_bwd")
        ex.chips_done(main_chips, got)
    dcq16 = jnp.pad(dcq3[:, :2, :].reshape(HEADS, s).T, ((0, 0), (0, 128 - HEADS)))
    dzf, d_b_f = _forget_bwd(dcq16, aux, f_logit, b_f, name="forget_bwd")
    dz_parts = [dzu, dzv, dga, dgb, dq, dk, dv]
    w_s_chips = None if ex is None else ex.chips([], table="w_s")
    g_wt_main, got = _grad_w_parts(dz_parts, h, name="d_w_main", tm=512, carry=w_s_chips)
    if ex is not None:
        ex.chips_done(w_s_chips, got)
    g_wt_f = mm(dzf, h, mode="tn", out_dtype=BF16, name="d_w_f", **whole_s)
    grads = dict(grads, wt_main=g_wt_main, wt_f=g_wt_f)
    w_in_chips = None
    if ex is not None:
        w_in_sib = ex.sibling(dict(wt_main=g_wt_main, wt_f=g_wt_f))
        ex.sibling_done(w_in_sib, _run_carry(w_in_sib, name="exchange_sibling_w_in"))
        w_in_chips = ex.chips(w_in_sib.names)
    (dx0, _, d_norm_mix), got = _matmul_rmsnorm_bwd(dz_parts, wt_main, dx1, x, sm["norm_mix_g"], mode="nn", tk=1024,
                                                    extra=(dzf, w["wt_f"]), name="d_h_norm_mix_bwd", carry=w_in_chips,
                                                    lead=True)
    if ex is not None:
        ex.chips_done(w_in_chips, got)
    return loss, dx0, grads, dict(small, norm_mix_g=d_norm_mix, b_f=d_b_f[:, :HEADS])


def kernel(x, p, norm_mix_g, w_in, b_f, gmlp_ln_g, gmlp_ln_b, gmlp_w_s, gmlp_b_s, w_branch_a, w_branch_b, w_out, norm_ffn_g, w_up, conv_w, conv_b, w_down, norm_ple_g, w_ple, w_ple_gate, norm_final_g, loss_target, m_norm_mix_g, m_w_in, m_b_f, m_gmlp_ln_g, m_gmlp_ln_b, m_gmlp_w_s, m_gmlp_b_s, m_w_branch_a, m_w_branch_b, m_w_out, m_norm_ffn_g, m_w_up, m_conv_w, m_conv_b, m_w_down, m_norm_ple_g, m_w_ple, m_w_ple_gate, m_norm_final_g, v_norm_mix_g, v_w_in, v_b_f, v_gmlp_ln_g, v_gmlp_ln_b, v_gmlp_w_s, v_gmlp_b_s, v_w_branch_a, v_w_branch_b, v_w_out, v_norm_ffn_g, v_w_up, v_conv_w, v_conv_b, v_w_down, v_norm_ple_g, v_w_ple, v_w_ple_gate, v_norm_final_g):
    given = dict(locals())
    weights = {n: given[n] for n in WEIGHT_ORDER}
    mom_m = {n: given["m_" + n] for n in WEIGHT_ORDER}
    mom_v = {n: given["v_" + n] for n in WEIGHT_ORDER}
    pos = jnp.stack([lax.axis_index("x"), lax.axis_index("y"), lax.axis_index("c")]).astype(I32)
    names = [n for n, _ in SHARDED]
    kinds = dict(SHARDED)

    later = [n for n in names if n != "w_in"]

    first = _allgather([_to_comm("w_in", kinds["w_in"], weights["w_in"])], name="allgather_w_in")
    ex = _Exchanges(later, [_to_comm(n, kinds[n], weights[n]) for n in later], pos)

    sm = dict(norm_mix_g=norm_mix_g, b_f=b_f, gmlp_ln_g=gmlp_ln_g, gmlp_ln_b=gmlp_ln_b, gmlp_w_s=gmlp_w_s[0],
              gmlp_b_s=gmlp_b_s[0], norm_ffn_g=norm_ffn_g, conv_b=conv_b, norm_ple_g=norm_ple_g,
              norm_final_g=norm_final_g.reshape(1, D_MODEL))
    loss_part, dx0, grads, small = _local_step(
        x[0], p[0, 0], loss_target[0], _assemble_weights({"w_in": first[0]}), sm, ex)

    b_f_and_loss = jnp.concatenate([small["b_f"].reshape(-1), loss_part[0, :1]])
    last = _allreduce_rows(jnp.concatenate([_rows(small["norm_mix_g"], 8), _rows(b_f_and_loss, 8)], axis=0),
                           name="allreduce_last")
    loss = last[8, HEADS]
    small_last = jnp.pad(last, ((0, SMALL_ROWS - 16), (0, 0)))

    grad, delta, new_m, new_v = {}, {}, {}, {}
    for n in names:
        s32, r = ex.reduced[n]
        outs = _adam_sharded(s32, r, *[_to_comm(n, kinds[n], src[n], F32) for src in (weights, mom_m, mom_v)], pos,
                             name="adam_" + n)
        grad[n], delta[n], new_m[n], new_v[n] = [_from_comm(n, kinds[n], o) for o in outs]
    replicated = [n for n, _ in SMALL]
    rep = lambda src: _pack_small({n: src[n] for n in replicated})
    packed = _adam_replicated(ex.table(), small_last, rep(weights), rep(mom_m), rep(mom_v), name="adam_replicated")
    for out, pk in zip((grad, delta, new_m, new_v), packed):
        for n in replicated:
            out[n] = _small(pk, n, weights[n].shape)

    return (loss, dx0, *[grad[n] for n in WEIGHT_ORDER], *[delta[n] for n in WEIGHT_ORDER],
            *[new_m[n] for n in WEIGHT_ORDER], *[new_v[n] for n in WEIGHT_ORDER])
```

```python
import functools
import math

import jax
import jax.numpy as jnp
from jax import lax
from jax.experimental import pallas as pl
from jax.experimental.pallas import tpu as pltpu

F32 = jnp.float32
BF16 = jnp.bfloat16
I32 = jnp.int32

D_MODEL = 1024
GROUPS = 8
GDIM = 128
GBLOCK = 128
CHUNK = 64
HEADS = 16
HEAD_DIM = 64
D_FF = 2816
PLE_DIM = 256
EPS = 1e-6
N_DEV = 8
ATT_SCALE = HEAD_DIM ** -0.5
NEG = -1e30

ADAM_LR = 0.001
ADAM_B1 = 0.9
ADAM_B2 = 0.999
ADAM_EPS = 1e-08
ADAM_WD = 0.01
ADAM_STEP = 10

V7X_VMEM_LIMIT = 48 * 1024 * 1024
MESH = pl.DeviceIdType.MESH

O_F = 2 * 1024 + 3 * 1024
O_G = O_F + HEADS
IN_COLS = O_G + 2 * D_MODEL
MAIN_COLS = IN_COLS - HEADS
IN_SHARD = IN_COLS // N_DEV
IN_SHARD_PAD = 912

SHARDED = (("w_in", "cols"), ("w_branch_a", "rows"), ("w_branch_b", "rows"), ("w_out", "rows"), ("w_up", "cols"),
           ("conv_w", "f32"), ("w_down", "rows"), ("w_ple", "cols"), ("w_ple_gate", "rows"))

SMALL = (("norm_mix_g", 8), ("b_f", 8), ("gmlp_ln_g", 8), ("gmlp_ln_b", 8), ("gmlp_w_s", 128), ("gmlp_b_s", 8),
         ("norm_ffn_g", 8), ("conv_b", 8), ("norm_ple_g", 8), ("norm_final_g", 8))
SMALL_OFF = {}
_o = 0
for _n, _r in SMALL:
    SMALL_OFF[_n] = (_o, _r)
    _o += _r
SMALL_ROWS = _o

WEIGHT_ORDER = ("norm_mix_g", "w_in", "b_f", "gmlp_ln_g", "gmlp_ln_b", "gmlp_w_s", "gmlp_b_s", "w_branch_a",
                "w_branch_b", "w_out", "norm_ffn_g", "w_up", "conv_w", "conv_b", "w_down", "norm_ple_g", "w_ple",
                "w_ple_gate", "norm_final_g")


def _cparams(sem):
    return pltpu.CompilerParams(dimension_semantics=sem, vmem_limit_bytes=V7X_VMEM_LIMIT)


def _gelu(x):
    c = math.sqrt(2.0 / math.pi)
    return 0.5 * x * (1.0 + jnp.tanh(c * (x + 0.044715 * x * x * x)))


def _gelu_and_grad(x):
    c = math.sqrt(2.0 / math.pi)
    t = jnp.tanh(c * (x + 0.044715 * x * x * x))
    g = 0.5 * x * (1.0 + t)
    dg = 0.5 * (1.0 + t) + 0.5 * x * (1.0 - t * t) * (c * (1.0 + 3.0 * 0.044715 * x * x))
    return g, dg


def _sigmoid(x):
    return 1.0 / (1.0 + jnp.exp(-x))


def _dot(a, b, dims):
    return lax.dot_general(a, b, (dims, ((), ())), preferred_element_type=F32)


NN = ((1,), (0,))
NT = ((1,), (1,))
TN = ((0,), (0,))


def _row_tile(rows, most):
    best = None
    for t in range(16, min(rows, most) + 1, 16):
        if rows % t == 0:
            best = t
    return best if best is not None else rows


def _matmul(a, b, *, mode, out_dtype, name, tm=512, tn=512, tk=512, add=None, n=None, b_off=0,
            out_rows=None, o_off=0, into=None, norm_g=None, carry=None):
    if mode == "tn":
        kdim, m = a.shape
    else:
        m, kdim = a.shape
    if n is None:
        n = b.shape[0] if mode == "nt" else b.shape[1]
    tm, tn, tk = min(tm, m), min(tn, n), min(tk, kdim)
    assert m % tm == 0 and n % tn == 0 and kdim % tk == 0, (name, m, n, kdim, tm, tn, tk)
    nk = kdim // tk
    dims = {"nn": NN, "nt": NT, "tn": TN}[mode]

    n_in = 2 + (add is not None) + (into is not None) + (norm_g is not None)
    assert norm_g is None or tn == n, "the RMS norm needs whole rows"

    def finish(r, refs):
        if add is not None:
            r = refs[2][...].astype(F32) + r
        refs[n_in][...] = r.astype(out_dtype)
        if norm_g is not None:
            rs = lax.rsqrt(jnp.mean(r * r, axis=-1, keepdims=True) + EPS)
            refs[n_in + 1][...] = ((r * rs) * refs[n_in - 1][...]).astype(BF16)

    def body(*refs):
        a_ref, b_ref = refs[:2]
        part = _dot(a_ref[...].astype(BF16), b_ref[...].astype(BF16), dims)
        if nk == 1:
            finish(part, refs)
            return
        acc_ref = refs[-1]
        k = pl.program_id(2)

        @pl.when(k == 0)
        def _():
            acc_ref[...] = part

        @pl.when((k > 0) & (k < nk - 1))
        def _():
            acc_ref[...] += part

        @pl.when(k == nk - 1)
        def _():
            finish(acc_ref[...] + part, refs)

    a_spec = pl.BlockSpec((tk, tm), lambda i, j, k: (k, i)) if mode == "tn" else pl.BlockSpec((tm, tk), lambda i, j, k: (i, k))
    if mode == "nt":
        b_spec = pl.BlockSpec((tn, tk), lambda i, j, k: (j + b_off, k))
    else:
        b_spec = pl.BlockSpec((tk, tn), lambda i, j, k: (k + b_off, j))
    o_spec = pl.BlockSpec((tm, tn), lambda i, j, k: (i + o_off, j))
    in_specs = [a_spec, b_spec] + ([pl.BlockSpec((tm, tn), lambda i, j, k: (i, j))] if add is not None else [])
    args = (a, b) + ((add,) if add is not None else ())
    aliases = {}
    if into is not None:
        aliases = {len(args): 0}
        in_specs.append(pl.BlockSpec(memory_space=pl.ANY))
        args += (into,)
    out_specs = [o_spec]
    out_shape = [jax.ShapeDtypeStruct((m if out_rows is None else out_rows, n), out_dtype)]
    if norm_g is not None:
        in_specs.append(pl.BlockSpec((1, n), lambda i, j, k: (0, 0)))
        args += (norm_g,)
        out_specs.append(pl.BlockSpec((tm, tn), lambda i, j, k: (i, j)))
        out_shape.append(jax.ShapeDtypeStruct((m, n), BF16))
    outs, carried = _carry_call(
        body, carry, name=name, grid=(m // tm, n // tn, nk), in_specs=in_specs, out_specs=out_specs,
        out_shape=out_shape, scratch_shapes=[pltpu.VMEM((tm, tn), F32)] if nk > 1 else [], args=args,
        own_aliases=aliases)
    out = outs[0] if norm_g is None else tuple(outs)
    return out if carry is None else (out, carried)


def _row_spec(tr, width, col_block=0):
    return pl.BlockSpec((tr, width), lambda i: (i, col_block))


def _full_spec(shape):
    return pl.BlockSpec(shape, lambda i: tuple(0 for _ in shape))


def _rmsnorm_fwd(x, g, *, name, tr=256):
    s, d = x.shape

    def body(x_ref, g_ref, o_ref):
        xv = x_ref[...]
        r = lax.rsqrt(jnp.mean(xv * xv, axis=-1, keepdims=True) + EPS)
        o_ref[...] = ((xv * r) * g_ref[...]).astype(BF16)

    return pl.pallas_call(
        body, name=name, grid=(s // tr,),
        in_specs=[_row_spec(tr, d), _full_spec((1, d))], out_specs=_row_spec(tr, d),
        out_shape=jax.ShapeDtypeStruct((s, d), BF16), compiler_params=_cparams(("parallel",)),
    )(x, g)


def _matmul_rmsnorm_bwd(a_parts, b, dres, x, g, *, mode, tk, name, extra=None, tm=512, carry=None, lead=False,
                        resident=False):
    s, d = x.shape
    n_row = s // tm
    spans, lo = [], 0
    for a in a_parts:
        spans.append((lo, lo + a.shape[1] // tk))
        lo = spans[-1][1]
    n_main, total = lo, lo + (extra is not None)
    n_parts = len(a_parts)

    def body(*refs):
        a_refs, b_ref = refs[:n_parts], refs[n_parts]
        k0 = n_parts + 1
        ax_ref, bx_ref = (refs[k0], refs[k0 + 1]) if extra is not None else (None, None)
        k0 += 2 * (extra is not None)
        dres_ref, x_ref, g_ref, dx_ref, dxb_ref, dg_ref, acc_all = refs[k0:k0 + 7]
        if resident:
            kk, i = pl.program_id(0), pl.program_id(1)
            acc_ref = acc_all.at[pl.ds(pl.multiple_of(i * tm, tm), tm)]
        else:
            i, kk = pl.program_id(0), pl.program_id(1)
            acc_ref = acc_all

        def accumulate(part, first):
            if first:
                @pl.when(kk == 0)
                def _():
                    acc_ref[...] = part

                @pl.when(kk > 0)
                def _():
                    acc_ref[...] += part
            else:
                acc_ref[...] += part

        for p, (a_ref, (lo_p, hi_p)) in enumerate(zip(a_refs, spans)):
            @pl.when((kk >= lo_p) & (kk < hi_p))
            def _(a_ref=a_ref, lo_p=lo_p):
                accumulate(_dot(a_ref[...].astype(BF16), b_ref[...].astype(BF16), NN if mode == "nn" else NT), lo_p == 0)

        if extra is not None:
            @pl.when(kk == n_main)
            def _():
                accumulate(_dot(ax_ref[...].astype(BF16), bx_ref[...].astype(BF16), NN), False)

        @pl.when(kk == total - 1)
        def _():
            dhv = acc_ref[...]
            xv = x_ref[...]
            r = lax.rsqrt(jnp.mean(xv * xv, axis=-1, keepdims=True) + EPS)
            xhat = xv * r
            dxhat = dhv * g_ref[...]
            dx = dres_ref[...] + r * (dxhat - xhat * jnp.mean(dxhat * xhat, axis=-1, keepdims=True))
            dx_ref[...] = dx
            dxb_ref[...] = dx.astype(BF16)
            dgp = jnp.sum(dhv * xhat, axis=0, keepdims=True)

            @pl.when(i == 0)
            def _():
                dg_ref[...] = dgp

            @pl.when(i > 0)
            def _():
                dg_ref[...] += dgp

    def spec(shape, index):
        return pl.BlockSpec(shape, (lambda kk, i: index(i, kk)) if resident else index)

    def row(i, kk, lo_p, hi_p):
        if not resident:
            return i
        return jnp.where(kk < lo_p, 0, jnp.where(kk >= hi_p, n_row - 1, i))

    a_specs = [spec((tm, tk), lambda i, kk, lo_p=lo_p, hi_p=hi_p: (row(i, kk, lo_p, hi_p),
                                                                    jnp.clip(kk - lo_p, 0, hi_p - lo_p - 1)))
               for lo_p, hi_p in spans]
    step = lambda kk: jnp.minimum(kk, n_main - 1)
    b_spec = (spec((tk, d), lambda i, kk: (step(kk), 0)) if mode == "nn"
              else spec((d, tk), lambda i, kk: (0, step(kk))))
    rows = spec((tm, d), lambda i, kk: (row(i, kk, total - 1, total), 0))
    one = spec((1, d), lambda i, kk: (0, 0))
    dx_spec, dx_shape = rows, jax.ShapeDtypeStruct((s, d), F32)
    if lead:
        dx_spec = spec((None, tm, d), lambda i, kk: (0, row(i, kk, total - 1, total), 0))
        dx_shape = jax.ShapeDtypeStruct((1, s, d), F32)
    x_specs, x_args = [], []
    if extra is not None:
        kx = extra[0].shape[1]
        x_specs = [spec((tm, kx), lambda i, kk: (row(i, kk, n_main, total), 0)), spec((kx, d), lambda i, kk: (0, 0))]
        x_args = list(extra)
    (dx, dxb, dg), carried = _carry_call(
        body, carry, name=name, grid=(total, n_row) if resident else (n_row, total),
        in_specs=a_specs + [b_spec] + x_specs + [rows, rows, one], out_specs=[dx_spec, rows, one],
        out_shape=[dx_shape, jax.ShapeDtypeStruct((s, d), BF16), jax.ShapeDtypeStruct((1, d), F32)],
        scratch_shapes=[pltpu.VMEM((s if resident else tm, d), F32)], args=list(a_parts) + [b] + x_args + [dres, x, g])
    return (dx, dxb, dg), carried


def _grad_w_parts(a_parts, b, *, name, tm=512, carry=None):
    s, width = a_parts[0].shape
    per, n = width // tm, b.shape[1]

    def body(*refs):
        a_refs, b_ref, o_ref = refs[:len(a_parts)], refs[len(a_parts)], refs[len(a_parts) + 1]
        i = pl.program_id(0)
        for p, a_ref in enumerate(a_refs):
            @pl.when(i // per == p)
            def _(a_ref=a_ref):
                o_ref[...] = _dot(a_ref[...].astype(BF16), b_ref[...].astype(BF16), TN).astype(BF16)

    a_specs = [pl.BlockSpec((s, tm), lambda i, p=p: (0, jnp.clip(i - p * per, 0, per - 1))) for p in range(len(a_parts))]
    (out,), carried = _carry_call(
        body, carry, name=name, grid=(len(a_parts) * per,),
        in_specs=a_specs + [pl.BlockSpec((s, n), lambda i: (0, 0))], out_specs=[pl.BlockSpec((tm, n), lambda i: (i, 0))],
        out_shape=[jax.ShapeDtypeStruct((len(a_parts) * width, n), BF16)], scratch_shapes=[], args=list(a_parts) + [b])
    return out, carried


def _ple_loss(p, wt_ple, h3, w_pg, x2, target, g, *, name, tm=256):
    s, d = x2.shape
    kp = p.shape[1]

    def body(p_ref, wp_ref, h_ref, wg_ref, x_ref, t_ref, g_ref, loss_ref, dx_ref, dple_ref, dgp_ref, dg_ref):
        i = pl.program_id(0)
        ple = _dot(p_ref[...].astype(BF16), wp_ref[...], NT)
        sg = _sigmoid(_dot(h_ref[...], wg_ref[...], NN))
        xv = x_ref[...] + ple * sg
        r = lax.rsqrt(jnp.mean(xv * xv, axis=-1, keepdims=True) + EPS)
        xhat = xv * r
        diff = xhat * g_ref[...] - t_ref[...]
        lp = jnp.zeros((1, 128), F32) + (0.5 / d) * jnp.sum(diff * diff)
        dy = diff * (1.0 / d)
        dxhat = dy * g_ref[...]
        dx = r * (dxhat - xhat * jnp.mean(dxhat * xhat, axis=-1, keepdims=True))
        dx_ref[...] = dx
        dple_ref[...] = (dx * sg).astype(BF16)
        dgp_ref[...] = (dx * ple * (sg * (1.0 - sg))).astype(BF16)
        dgp = jnp.sum(dy * xhat, axis=0, keepdims=True)

        @pl.when(i == 0)
        def _():
            dg_ref[...] = dgp
            loss_ref[...] = lp

        @pl.when(i > 0)
        def _():
            dg_ref[...] += dgp
            loss_ref[...] += lp

    rows = _row_spec(tm, d)
    return pl.pallas_call(
        body, name=name, grid=(s // tm,),
        in_specs=[_row_spec(tm, kp), _full_spec((d, kp)), rows, _full_spec((d, d)), rows, rows, _full_spec((1, d))],
        out_specs=[_full_spec((1, 128)), rows, rows, rows, _full_spec((1, d))],
        out_shape=[jax.ShapeDtypeStruct((1, 128), F32), jax.ShapeDtypeStruct((s, d), F32),
                   jax.ShapeDtypeStruct((s, d), BF16), jax.ShapeDtypeStruct((s, d), BF16),
                   jax.ShapeDtypeStruct((1, d), F32)],
        compiler_params=_cparams(("arbitrary",)),
    )(p, wt_ple, h3, w_pg, x2, target, g)


def _branches_merge(a, b, w_a, w_b, zuvg, *, name, tm=512):
    s, d = a.shape

    def body(a_ref, b_ref, wa_ref, wb_ref, ga_ref, gb_ref, ya_ref, yb_ref, o_ref):
        ya = _dot(a_ref[...], wa_ref[...], NN)
        yb = _dot(b_ref[...], wb_ref[...], NN)
        ya_ref[...] = ya
        yb_ref[...] = yb
        o_ref[...] = (_sigmoid(ga_ref[...]) * ya + _sigmoid(gb_ref[...]) * yb).astype(BF16)

    rows = _row_spec(tm, d)
    return pl.pallas_call(
        body, name=name, grid=(s // tm,),
        in_specs=[rows, rows, _full_spec((d, d)), _full_spec((d, d)), _row_spec(tm, d, 2), _row_spec(tm, d, 3)],
        out_specs=[rows, rows, rows],
        out_shape=[jax.ShapeDtypeStruct((s, d), F32), jax.ShapeDtypeStruct((s, d), F32), jax.ShapeDtypeStruct((s, d), BF16)],
        compiler_params=_cparams(("parallel",)),
    )(a, b, w_a, w_b, zuvg, zuvg)


def _merge_bwd(dx1b, w_out, ya, yb, zuvg, *, name, tm=512):
    s, d = ya.shape

    def body(dx_ref, w_ref, ya_ref, yb_ref, ga_ref, gb_ref, dya_ref, dyb_ref, dga_ref, dgb_ref):
        dmv = _dot(dx_ref[...], w_ref[...], NT)
        sa = _sigmoid(ga_ref[...])
        sb = _sigmoid(gb_ref[...])
        dya_ref[...] = (dmv * sa).astype(BF16)
        dyb_ref[...] = (dmv * sb).astype(BF16)
        dga_ref[...] = (dmv * ya_ref[...] * (sa * (1.0 - sa))).astype(BF16)
        dgb_ref[...] = (dmv * yb_ref[...] * (sb * (1.0 - sb))).astype(BF16)

    rows = _row_spec(tm, d)
    o = jax.ShapeDtypeStruct((s, d), BF16)
    return pl.pallas_call(
        body, name=name, grid=(s // tm,),
        in_specs=[rows, _full_spec((d, d)), rows, rows, _row_spec(tm, d, 2), _row_spec(tm, d, 3)],
        out_specs=[rows] * 4, out_shape=[o, o, o, o], compiler_params=_cparams(("parallel",)),
    )(dx1b, w_out, ya, yb, zuvg, zuvg)


def _masked_ws(ws_ref, g):
    row = lax.broadcasted_iota(I32, (GBLOCK, GBLOCK), 0)
    col = lax.broadcasted_iota(I32, (GBLOCK, GBLOCK), 1)
    keep = (col // CHUNK) <= (row // CHUNK)
    return jnp.where(keep, ws_ref[g], 0.0), keep


def _layernorm_parts(zv):
    mu = jnp.mean(zv, axis=-1, keepdims=True)
    xc = zv - mu
    rs = lax.rsqrt(jnp.mean(xc * xc, axis=-1, keepdims=True) + EPS)
    return xc * rs, rs


def _gmlp_fwd(zuvg, ln_g, ln_b, w_s, bs_t, *, name):
    s, w = zuvg.shape[0], GROUPS * GDIM

    def body(zu_ref, zv_ref, lng_ref, lnb_ref, ws_ref, bs_ref, a_ref):
        zu = _gelu(zu_ref[...])
        zv = _gelu(zv_ref[...])
        xhat, _ = _layernorm_parts(zv)
        vln = (xhat * lng_ref[...] + lnb_ref[...]).astype(BF16)
        for g in range(GROUPS):
            wm, _ = _masked_ws(ws_ref, g)
            mixed = _dot(wm.astype(BF16), vln[:, g * GDIM:(g + 1) * GDIM], NN) + bs_ref[:, g:g + 1]
            a_ref[:, g * GDIM:(g + 1) * GDIM] = (zu[:, g * GDIM:(g + 1) * GDIM] * mixed).astype(BF16)

    return pl.pallas_call(
        body, name=name, grid=(s // GBLOCK,),
        in_specs=[_row_spec(GBLOCK, w, 0), _row_spec(GBLOCK, w, 1), _full_spec((1, w)), _full_spec((1, w)),
                  _full_spec((GROUPS, GBLOCK, GBLOCK)), _full_spec((GBLOCK, 128))],
        out_specs=_row_spec(GBLOCK, w),
        out_shape=jax.ShapeDtypeStruct((s, w), BF16), compiler_params=_cparams(("parallel",)),
    )(zuvg, zuvg, ln_g, ln_b, w_s, bs_t)


def _gmlp_bwd(da, zuvg, ln_g, ln_b, w_s, bs_t, carry=None, *, name):
    s, w = zuvg.shape[0], GROUPS * GDIM

    def body(da_ref, zu_ref, zv_ref, lng_ref, lnb_ref, ws_ref, bs_ref,
             dzu_ref, dzv_ref, dws_ref, dbs_ref, dlng_ref, dlnb_ref, dvln_ref):
        i = pl.program_id(0)
        zu, dzu_g = _gelu_and_grad(zu_ref[...])
        zv, dzv_g = _gelu_and_grad(zv_ref[...])
        xhat, rs = _layernorm_parts(zv)
        vln = (xhat * lng_ref[...] + lnb_ref[...]).astype(BF16)
        dav = da_ref[...].astype(F32)
        lane = lax.broadcasted_iota(I32, (GBLOCK, 128), 1)
        dbs = jnp.zeros((GBLOCK, 128), F32)

        @pl.when(i == 0)
        def _():
            dws_ref[...] = jnp.zeros_like(dws_ref)

        for g in range(GROUPS):
            sl = slice(g * GDIM, (g + 1) * GDIM)
            wm, keep = _masked_ws(ws_ref, g)
            wmb = wm.astype(BF16)
            vg = vln[:, sl]
            mixed = _dot(wmb, vg, NN) + bs_ref[:, g:g + 1]
            dag = dav[:, sl]
            dzu_ref[:, sl] = (dag * mixed * dzu_g[:, sl]).astype(BF16)
            dmix = dag * zu[:, sl]
            dmb = dmix.astype(BF16)
            dws_ref[g] += jnp.where(keep, _dot(dmb, vg, NT), 0.0)
            dbs = jnp.where(lane == g, jnp.sum(dmix, axis=1, keepdims=True), dbs)
            dvln_ref[:, sl] = _dot(wmb, dmb, TN)
        dvln = dvln_ref[...]
        dxhat = dvln * lng_ref[...]
        dzv = rs * (dxhat - jnp.mean(dxhat, axis=-1, keepdims=True)
                    - xhat * jnp.mean(dxhat * xhat, axis=-1, keepdims=True))
        dzv_ref[...] = (dzv * dzv_g).astype(BF16)
        dlng = jnp.sum(dvln * xhat, axis=0, keepdims=True)
        dlnb = jnp.sum(dvln, axis=0, keepdims=True)

        @pl.when(i == 0)
        def _():
            dbs_ref[...] = dbs
            dlng_ref[...] = dlng
            dlnb_ref[...] = dlnb

        @pl.when(i > 0)
        def _():
            dbs_ref[...] += dbs
            dlng_ref[...] += dlng
            dlnb_ref[...] += dlnb

    return _carry_call(
        body, carry, name=name, grid=(s // GBLOCK,),
        in_specs=[_row_spec(GBLOCK, w), _row_spec(GBLOCK, w, 0), _row_spec(GBLOCK, w, 1), _full_spec((1, w)),
                  _full_spec((1, w)), _full_spec((GROUPS, GBLOCK, GBLOCK)), _full_spec((GBLOCK, 128))],
        out_specs=[_row_spec(GBLOCK, w), _row_spec(GBLOCK, w), _full_spec((GROUPS, GBLOCK, GBLOCK)),
                   _full_spec((GBLOCK, 128)), _full_spec((1, w)), _full_spec((1, w))],
        out_shape=[jax.ShapeDtypeStruct((s, w), BF16), jax.ShapeDtypeStruct((s, w), BF16),
                   jax.ShapeDtypeStruct((GROUPS, GBLOCK, GBLOCK), F32), jax.ShapeDtypeStruct((GBLOCK, 128), F32),
                   jax.ShapeDtypeStruct((1, w), F32), jax.ShapeDtypeStruct((1, w), F32)],
        scratch_shapes=[pltpu.VMEM((GBLOCK, w), F32)], args=[da, zuvg, zuvg, ln_g, ln_b, w_s, bs_t])


def _shift_down(u, k):
    row = lax.broadcasted_iota(I32, u.shape, 0)
    return jnp.where(row >= k, pltpu.roll(u, k, 0), 0.0)


def _shift_up(u, k):
    s = u.shape[0]
    row = lax.broadcasted_iota(I32, u.shape, 0)
    return jnp.where(row < s - k, pltpu.roll(u, s - k, 0), 0.0)


def _conv(u, w_ref, b_ref):
    return b_ref[...] + w_ref[0:1, :] * _shift_down(u, 2) + w_ref[1:2, :] * _shift_down(u, 1) + w_ref[2:3, :] * u


def _conv_specs(s, f, tc):
    nc = f // tc
    half = lambda rows: [pl.BlockSpec((rows, tc), lambda j: (0, j)), pl.BlockSpec((rows, tc), lambda j: (0, nc + j))]
    return half(s), half(3), half(1)


def _up_convglu(h2, wt_up, conv_w, conv_b, *, name, tc=256):
    s, d = h2.shape
    f = wt_up.shape[0] // 2
    nc = f // tc
    _, w_specs, b_specs = _conv_specs(s, f, tc)

    def body(h_ref, ta_ref, tg_ref, wa_ref, wg_ref, ba_ref, bg_ref, ua_ref, ug_ref, o_ref):
        ua = _dot(h_ref[...], ta_ref[...], NT)
        ua_ref[...] = ua
        ga = _gelu(_conv(ua, wa_ref, ba_ref))
        ug = _dot(h_ref[...], tg_ref[...], NT)
        ug_ref[...] = ug
        o_ref[...] = (ga * _conv(ug, wg_ref, bg_ref)).astype(BF16)

    col = pl.BlockSpec((s, tc), lambda j: (0, j))
    return pl.pallas_call(
        body, name=name, grid=(nc,),
        in_specs=[_full_spec((s, d)), pl.BlockSpec((tc, d), lambda j: (j, 0)), pl.BlockSpec((tc, d), lambda j: (nc + j, 0))]
        + w_specs + b_specs,
        out_specs=[col, col, col],
        out_shape=[jax.ShapeDtypeStruct((s, f), F32), jax.ShapeDtypeStruct((s, f), F32), jax.ShapeDtypeStruct((s, f), BF16)],
        compiler_params=_cparams(("parallel",)),
    )(h2, wt_up, wt_up, conv_w, conv_w, conv_b, conv_b)


def _convglu_bwd(dact, up_a, up_g, conv_w, conv_b, *, name, tc=256):
    s, f = up_a.shape
    _, w_specs, b_specs = _conv_specs(s, f, tc)
    up_specs = [pl.BlockSpec((s, tc), lambda j: (0, j))] * 2

    def half(dc, taps, w_ref, du_ref, dw_ref, db_ref):
        db_ref[...] = jnp.sum(dc, axis=0, keepdims=True)
        for k in range(3):
            dw_ref[k:k + 1, :] = jnp.sum(dc * taps[k], axis=0, keepdims=True)
        du = w_ref[2:3, :] * dc + w_ref[1:2, :] * _shift_up(dc, 1) + w_ref[0:1, :] * _shift_up(dc, 2)
        du_ref[...] = du.astype(BF16)

    def body(d_ref, ua_ref, ug_ref, wa_ref, wg_ref, ba_ref, bg_ref,
             dua_ref, dug_ref, dwa_ref, dwg_ref, dba_ref, dbg_ref):
        taps_a = (_shift_down(ua_ref[...], 2), _shift_down(ua_ref[...], 1), ua_ref[...])
        taps_g = (_shift_down(ug_ref[...], 2), _shift_down(ug_ref[...], 1), ug_ref[...])
        conv = lambda taps, w_ref, b_ref: b_ref[...] + w_ref[0:1, :] * taps[0] + w_ref[1:2, :] * taps[1] + w_ref[2:3, :] * taps[2]
        ca = conv(taps_a, wa_ref, ba_ref)
        cg = conv(taps_g, wg_ref, bg_ref)
        ga, dga = _gelu_and_grad(ca)
        dv = d_ref[...].astype(F32)
        half(dv * cg * dga, taps_a, wa_ref, dua_ref, dwa_ref, dba_ref)
        half(dv * ga, taps_g, wg_ref, dug_ref, dwg_ref, dbg_ref)

    col, w3, b1 = up_specs[0], w_specs[0], b_specs[0]
    return pl.pallas_call(
        body, name=name, grid=(f // tc,),
        in_specs=[col] + up_specs + w_specs + b_specs, out_specs=[col, col, w3, w3, b1, b1],
        out_shape=[jax.ShapeDtypeStruct((s, f), BF16), jax.ShapeDtypeStruct((s, f), BF16),
                   jax.ShapeDtypeStruct((3, f), F32), jax.ShapeDtypeStruct((3, f), F32),
                   jax.ShapeDtypeStruct((1, f), F32), jax.ShapeDtypeStruct((1, f), F32)],
        compiler_params=_cparams(("parallel",)),
    )(dact, up_a, up_g, conv_w, conv_w, conv_b, conv_b)


def _tri_dot(tri, x):
    b0 = x.astype(BF16)
    r1 = x - b0.astype(F32)
    b1 = r1.astype(BF16)
    b2 = (r1 - b1.astype(F32)).astype(BF16)
    return _dot(tri, b0, NN) + _dot(tri, b1, NN) + _dot(tri, b2, NN)


def _log_sigmoid(x):
    return jnp.minimum(x, 0.0) - jnp.log(1.0 + jnp.exp(-jnp.abs(x)))


def _expand_heads(col16, rows):
    src = lax.broadcasted_iota(I32, (128, HEADS * HEAD_DIM), 0)
    dst = lax.broadcasted_iota(I32, (128, HEADS * HEAD_DIM), 1) // HEAD_DIM
    spread = (src == dst).astype(BF16)
    p0, p1, p2 = _bf16_pieces(col16)
    return (_dot(p0.astype(BF16), spread, NN) + _dot(p1.astype(BF16), spread, NN)) + _dot(p2.astype(BF16), spread, NN)


def _forget_cumsum(f_logit, b_f, *, name):
    s = f_logit.shape[0]
    nb = s // 128

    def body(f_ref, b_ref, cqe_ref):
        row = lax.broadcasted_iota(I32, (128, 128), 0)
        col = lax.broadcasted_iota(I32, (128, 128), 1)
        tri = (col <= row).astype(BF16)

        def step(n, carry):
            r0 = pl.multiple_of(n * 128, 128)
            lf = _log_sigmoid(f_ref[pl.ds(r0, 128), :] + b_ref[...])
            cum = _tri_dot(tri, lf) + carry
            cqe_ref[pl.ds(r0, 128), :] = _expand_heads(cum, 128)
            return cum[127:128, :]

        lax.fori_loop(0, nb, step, jnp.zeros((1, 128), F32))

    return pl.pallas_call(
        body, name=name, grid=(1,),
        in_specs=[_full_spec((s, 128)), _full_spec((1, 128))],
        out_specs=_full_spec((s, HEADS * HEAD_DIM)),
        out_shape=jax.ShapeDtypeStruct((s, HEADS * HEAD_DIM), F32),
        compiler_params=_cparams(("arbitrary",)),
    )(f_logit, b_f)


def _forget_bwd(dcq16, sum_q16, f_logit, b_f, *, name):
    s = f_logit.shape[0]
    nb = s // 128

    def body(a_ref, k_ref, f_ref, b_ref, df_ref, db_ref):
        row = lax.broadcasted_iota(I32, (128, 128), 0)
        col = lax.broadcasted_iota(I32, (128, 128), 1)
        tri_rev = (col >= row).astype(BF16)

        def step(m, carry):
            suffix, dbsum = carry
            n = nb - 1 - m
            r0 = pl.multiple_of(n * 128, 128)
            dcum = a_ref[pl.ds(r0, 128), :] - k_ref[pl.ds(r0, 128), :]
            dlf = _tri_dot(tri_rev, dcum) + suffix
            df = dlf * _sigmoid(-(f_ref[pl.ds(r0, 128), :] + b_ref[...]))
            df_ref[pl.ds(r0, 128), :] = df.astype(BF16)
            return dlf[0:1, :], dbsum + jnp.sum(df, axis=0, keepdims=True)

        _, dbsum = lax.fori_loop(0, nb, step, (jnp.zeros((1, 128), F32), jnp.zeros((1, 128), F32)))
        db_ref[...] = dbsum

    return pl.pallas_call(
        body, name=name, grid=(1,),
        in_specs=[_full_spec((s, 128))] * 3 + [_full_spec((1, 128))],
        out_specs=[_full_spec((s, 128)), _full_spec((1, 128))],
        out_shape=[jax.ShapeDtypeStruct((s, 128), BF16), jax.ShapeDtypeStruct((1, 128), F32)],
        compiler_params=_cparams(("arbitrary",)),
    )(dcq16, sum_q16, f_logit, b_f)


ATT_T = 256


def _head_lanes(rows):
    return lax.broadcasted_iota(I32, (rows, 128), 1) < HEAD_DIM


def _bf16_pieces(c):
    p0 = c.astype(BF16).astype(F32)
    r = c - p0
    p1 = r.astype(BF16).astype(F32)
    p2 = (r - p1).astype(BF16).astype(F32)
    return p0, p1, p2


def _col_reduce(x, op):
    rows = x.shape[0]
    while rows > 8:
        rows //= 2
        x = op(x[:rows], x[rows:])
    return jnp.max(x, axis=0, keepdims=True) if op is jnp.maximum else jnp.sum(x, axis=0, keepdims=True)


def _attn_prep(qkv, cqe, carry=None, *, name):
    s = qkv.shape[0]
    npair = HEADS // 2

    def body(q_ref, k_ref, v_ref, c_ref, qa_ref, ka_ref, vt_ref):
        rows = 128
        lane = lax.broadcasted_iota(I32, (rows, 128), 1)

        def chunk(n, _):
            r0 = pl.multiple_of(n * rows, rows)
            sl = pl.ds(r0, rows)
            qv = q_ref[sl, :].astype(F32) * ATT_SCALE
            kv = k_ref[sl, :].astype(F32)
            p0, p1, p2 = _bf16_pieces(pltpu.roll(c_ref[sl, :], HEAD_DIM, 1))
            for e in range(2):
                mine = (lane < HEAD_DIM) if e == 0 else (lane >= HEAD_DIM)
                base = HEAD_DIM * (1 - e)
                ones_hi = jnp.where((lane >= base + 3) & (lane < base + 6), 1.0, 0.0)
                ones_lo = jnp.where((lane >= base) & (lane < base + 3), 1.0, 0.0)
                qa = jnp.where(mine, qv, jnp.where(lane == base, p0, jnp.where(lane == base + 1, p1,
                               jnp.where(lane == base + 2, p2, ones_hi))))
                ka = jnp.where(mine, kv, jnp.where(lane == base + 3, -p0, jnp.where(lane == base + 4, -p1,
                               jnp.where(lane == base + 5, -p2, ones_lo))))
                qa_ref[e, sl, :] = qa.astype(BF16)
                ka_ref[e, sl, :] = ka.astype(BF16)
            vt_ref[0, :, sl] = v_ref[sl, :].astype(F32).T.astype(BF16)
            return 0

        lax.fori_loop(0, s // rows, chunk, 0)

    pair = pl.BlockSpec((2, s, 128), lambda hp: (hp, 0, 0))
    return _carry_call(
        body, carry, name=name, grid=(npair,),
        in_specs=[pl.BlockSpec((s, 128), lambda hp: (0, hp)), pl.BlockSpec((s, 128), lambda hp: (0, npair + hp)),
                  pl.BlockSpec((s, 128), lambda hp: (0, 2 * npair + hp)), pl.BlockSpec((s, 128), lambda hp: (0, hp))],
        out_specs=[pair, pair, pl.BlockSpec((1, 128, s), lambda hp: (hp, 0, 0))],
        out_shape=[jax.ShapeDtypeStruct((HEADS, s, 128), BF16), jax.ShapeDtypeStruct((HEADS, s, 128), BF16),
                   jax.ShapeDtypeStruct((npair, 128, s), BF16)],
        scratch_shapes=[], args=[qkv, qkv, qkv, cqe])


def _attn_fwd(qa, ka, vt, carry=None, *, name):
    s = qa.shape[1]
    t = 2 * ATT_T
    nq = s // t
    npair = HEADS // 2

    def body(qa_ref, ka_ref, vt_ref, o_ref, lse_ref):
        i = pl.program_id(1)
        krow = lax.broadcasted_iota(I32, (t, t), 0)
        qcol = lax.broadcasted_iota(I32, (t, t), 1)
        sub = lax.broadcasted_iota(I32, (128, t), 0)
        row8 = lax.broadcasted_iota(I32, (8, t), 0)
        qbs = (qa_ref[0], qa_ref[1])
        tk = t

        def step(j, carry, diag):
            c0 = pl.multiple_of(j * tk, tk)
            vtb = vt_ref[0, :, pl.ds(c0, tk)]
            sts = [_dot(ka_ref[e, pl.ds(c0, tk), :], qbs[e], NT) for e in range(2)]
            if diag:
                sts = [jnp.where(krow <= qcol, st, NEG) for st in sts]
            pts, stats = [], []
            for e in range(2):
                m, l, _ = carry[e]
                m_new = jnp.maximum(m, _col_reduce(sts[e], jnp.maximum))
                alpha = jnp.exp(m - m_new)
                pt = jnp.exp(sts[e] - m_new)
                stats.append((m_new, alpha, alpha * l + _col_reduce(pt, jnp.add)))
                pts.append(pt.astype(BF16))
            pvs = [_dot(vtb, pts[e], NN) for e in range(2)]
            return tuple((stats[e][0], stats[e][2], stats[e][1] * carry[e][2] + pvs[e]) for e in range(2))

        init = (jnp.full((1, t), NEG, F32), jnp.zeros((1, t), F32), jnp.zeros((128, t), F32))
        carry = lax.fori_loop(0, i, functools.partial(step, diag=False), (init, init))
        (m0, l0, acc0), (m1, l1, acc1) = step(i, carry, True)
        o_pair = jnp.where(sub < HEAD_DIM, acc0 / l0, acc1 / l1)
        o_ref[...] = o_pair.T.astype(BF16)
        lse_ref[0] = jnp.where(row8 == 0, m0 + jnp.log(l0), jnp.where(row8 == 1, m1 + jnp.log(l1), 0.0))

    return _carry_call(
        body, carry, name=name, grid=(npair, nq),
        in_specs=[pl.BlockSpec((2, t, 128), lambda hp, i: (hp, i, 0)), pl.BlockSpec((2, s, 128), lambda hp, i: (hp, 0, 0)),
                  pl.BlockSpec((1, 128, s), lambda hp, i: (hp, 0, 0))],
        out_specs=[pl.BlockSpec((t, 128), lambda hp, i: (i, hp)), pl.BlockSpec((1, 8, t), lambda hp, i: (hp, 0, i))],
        out_shape=[jax.ShapeDtypeStruct((s, HEADS * HEAD_DIM), BF16), jax.ShapeDtypeStruct((npair, 8, s), F32)],
        scratch_shapes=[], args=[qa, ka, vt])


def _attn_delta(do, o, carry=None, *, name):
    s = do.shape[0]

    def body(do_ref, o_ref, d_ref):
        prod = do_ref[...].astype(F32) * o_ref[...].astype(F32)
        row = lax.broadcasted_iota(I32, (8, 128), 0)
        lane = lax.broadcasted_iota(I32, (8, 128), 1)
        sel = ((row == 0) & (lane < HEAD_DIM) | (row == 1) & (lane >= HEAD_DIM)).astype(BF16)
        p0, p1, p2 = _bf16_pieces(prod)
        d_ref[0] = (_dot(sel, p0.astype(BF16), NT) + _dot(sel, p1.astype(BF16), NT)) + _dot(sel, p2.astype(BF16), NT)

    pair = pl.BlockSpec((s, 128), lambda hp: (0, hp))
    (delta3,), carried = _carry_call(
        body, carry, name=name, grid=(HEADS // 2,), in_specs=[pair, pair],
        out_specs=[pl.BlockSpec((1, 8, s), lambda hp: (hp, 0, 0))],
        out_shape=[jax.ShapeDtypeStruct((HEADS // 2, 8, s), F32)], scratch_shapes=[], args=[do, o])
    return delta3, carried


def _attn_bwd(qa, ka, qkv, do, lse3, delta3, carry=None, *, name):
    s = qa.shape[1]
    t = 2 * ATT_T
    nb = s // t
    npair = HEADS // 2

    def body(qa_ref, ka_ref, v_ref, do_ref, lse_ref, delta_ref, dq_ref, dk_ref, dv_ref, aux_ref, dcq_ref, dqt):
        hp = pl.program_id(0)
        first = _head_lanes(t)
        lane = lax.broadcasted_iota(I32, (t, 128), 1)
        dqt[...] = jnp.zeros_like(dqt)

        @pl.when(hp == 0)
        def _():
            aux_ref[...] = jnp.zeros_like(aux_ref)

        krow = lax.broadcasted_iota(I32, (t, t), 0)
        qcol = lax.broadcasted_iota(I32, (t, t), 1)

        def key_block(j, _):
            c0 = pl.multiple_of(j * t, t)
            vb = v_ref[pl.ds(c0, t), :]
            kbs = (ka_ref[0, pl.ds(c0, t), :], ka_ref[1, pl.ds(c0, t), :])
            kbts = tuple(kb.astype(F32).T.astype(BF16) for kb in kbs)
            vhs = (jnp.where(first, vb, jnp.zeros_like(vb)), jnp.where(first, jnp.zeros_like(vb), vb))

            def query_block(i, carry, diag):
                r0 = pl.multiple_of(i * t, t)
                dob = do_ref[pl.ds(r0, t), :]
                sts = [_dot(kbs[e], qa_ref[e, pl.ds(r0, t), :], NT) for e in range(2)]
                dpts = [_dot(vhs[e], dob, NT) for e in range(2)]
                ptbs, dsbs = [], []
                for e in range(2):
                    st = jnp.where(krow <= qcol, sts[e], NEG) if diag else sts[e]
                    pt = jnp.exp(st - lse_ref[0, e:e + 1, pl.ds(r0, t)])
                    dsbs.append((pt * (dpts[e] - delta_ref[0, e:e + 1, pl.ds(r0, t)])).astype(BF16))
                    ptbs.append(pt.astype(BF16))
                out = []
                for e in range(2):
                    dk_a, dv_a = carry[e]
                    dv_a = dv_a + _dot(ptbs[e], dob, NN)
                    dk_a = dk_a + _dot(dsbs[e], qa_ref[e, pl.ds(r0, t), :], NN)
                    dqt[e, :, pl.ds(r0, t)] += _dot(kbts[e], dsbs[e], NN)
                    out.append((dk_a, dv_a))
                return tuple(out)

            zero = jnp.zeros((t, 128), F32)
            carry = query_block(j, ((zero, zero), (zero, zero)), True)
            (dk0, dv0), (dk1, dv1) = lax.fori_loop(j + 1, nb, functools.partial(query_block, diag=False), carry)
            dk_ref[pl.ds(c0, t), :] = jnp.where(first, dk0, dk1).astype(BF16)
            dv_ref[pl.ds(c0, t), :] = jnp.where(first, dv0, dv1).astype(BF16)
            sum_q = jnp.where(lane == 2 * hp, dk0[:, HEAD_DIM + 3:HEAD_DIM + 4],
                              jnp.where(lane == 2 * hp + 1, dk1[:, 3:4], aux_ref[pl.ds(c0, t), :]))
            aux_ref[pl.ds(c0, t), :] = sum_q
            return 0

        lax.fori_loop(0, nb, key_block, 0)
        sub = lax.broadcasted_iota(I32, (128, s), 0)
        row8 = lax.broadcasted_iota(I32, (8, s), 0)
        dq_ref[...] = (jnp.where(sub < HEAD_DIM, dqt[0], dqt[1]) * ATT_SCALE).T.astype(BF16)
        dcq_ref[0] = jnp.where(row8 == 0, dqt[0, HEAD_DIM:HEAD_DIM + 1, :], jnp.where(row8 == 1, dqt[1, 0:1, :], 0.0))

    def pair_cols(off):
        return pl.BlockSpec((s, 128), lambda hp: (0, off + hp))

    heads = pl.BlockSpec((2, s, 128), lambda hp: (hp, 0, 0))
    rows = pl.BlockSpec((1, 8, s), lambda hp: (hp, 0, 0))
    wide = jax.ShapeDtypeStruct((s, HEADS * HEAD_DIM), BF16)
    return _carry_call(
        body, carry, name=name, grid=(npair,),
        in_specs=[heads, heads, pair_cols(2 * npair), pair_cols(0), rows, rows],
        out_specs=[pair_cols(0), pair_cols(0), pair_cols(0), pl.BlockSpec((s, 128), lambda hp: (0, 0)), rows],
        out_shape=[wide, wide, wide, jax.ShapeDtypeStruct((s, 128), F32), jax.ShapeDtypeStruct((npair, 8, s), F32)],
        scratch_shapes=[pltpu.VMEM((2, 128, s), F32)], args=[qa, ka, qkv, do, lse3, delta3])


def _adam_math(w, g, m, v):
    m = ADAM_B1 * m + (1.0 - ADAM_B1) * g
    v = ADAM_B2 * v + (1.0 - ADAM_B2) * (g * g)
    m_hat = m / (1.0 - ADAM_B1 ** ADAM_STEP)
    v_hat = v / (1.0 - ADAM_B2 ** ADAM_STEP)
    delta = -ADAM_LR * (m_hat / (jnp.sqrt(v_hat) + ADAM_EPS) + ADAM_WD * w)
    return delta, m, v


def _sum_pairs(keep, recv, pos, *, name):
    _, r, c = recv.shape
    tr = _row_tile(r, 512)

    def body(pos_ref, a_ref, b_ref, o32_ref, o16_ref):
        tot = a_ref[...].astype(F32) + b_ref[...].astype(F32)
        o16_ref[...] = tot.astype(BF16)

        @pl.when(pl.program_id(1) == 2 * pos_ref[0] + pos_ref[1])
        def _():
            o32_ref[...] = tot

    out = pl.BlockSpec((1, tr, c), lambda i, q, pos: (q, i, 0))
    grid_spec = pltpu.PrefetchScalarGridSpec(
        num_scalar_prefetch=1, grid=(r // tr, 4),
        in_specs=[pl.BlockSpec((1, tr, c), lambda i, q, pos: (2 * q + pos[2], i, 0)), out],
        out_specs=[pl.BlockSpec((1, tr, c), lambda i, q, pos: (0, i, 0)), out])
    return pl.pallas_call(
        body, name=name, grid_spec=grid_spec,
        out_shape=[jax.ShapeDtypeStruct((1, r, c), F32), jax.ShapeDtypeStruct((4, r, c), BF16)],
        compiler_params=_cparams(("arbitrary", "arbitrary")),
    )(pos, keep, recv)


def _adam_sharded(psum, recv, w, m, v, pos, *, name):
    r, c = w.shape
    rg = psum.shape[1]

    def body(pos_ref, p_ref, r_ref, w_ref, m_ref, v_ref, g_ref, d_ref, mo_ref, vo_ref):
        part = lambda ref, q: ref[q] if rg == r else ref[q, :r, :]
        g = part(p_ref, 0) + part(r_ref, 0).astype(F32) + part(r_ref, 1).astype(F32) + part(r_ref, 2).astype(F32)
        delta, mn, vn = _adam_math(w_ref[...], g, m_ref[...], v_ref[...])
        g_ref[...] = g
        d_ref[...] = delta
        mo_ref[...] = mn
        vo_ref[...] = vn

    if rg == r:
        tr = _row_tile(r, 320)
        grid = (r // tr,)
        row = pl.BlockSpec((tr, c), lambda i, pos: (i, 0))
        sums = lambda n: pl.BlockSpec((n, tr, c), lambda i, pos: (0, i, 0))
    else:
        tc = 256
        grid = (c // tc,)
        row = pl.BlockSpec((r, tc), lambda i, pos: (0, i))
        sums = lambda n: pl.BlockSpec((n, rg, tc), lambda i, pos: (0, 0, i))
    grid_spec = pltpu.PrefetchScalarGridSpec(
        num_scalar_prefetch=1, grid=grid, in_specs=[sums(1), sums(3), row, row, row], out_specs=[row, row, row, row])
    o = jax.ShapeDtypeStruct((r, c), F32)
    return pl.pallas_call(
        body, name=name, grid_spec=grid_spec, out_shape=[o, o, o, o],
        compiler_params=_cparams(("parallel",)),
    )(pos, psum, recv, w, m, v)


def _adam_replicated(chip_sums, last, w, m, v, *, name):
    r = w.shape[0]

    def body(s_ref, l_ref, w_ref, m_ref, v_ref, g_ref, d_ref, mo_ref, vo_ref):
        g = (((s_ref[0] + s_ref[1]) + s_ref[2]) + s_ref[3]) + l_ref[...]
        delta, mn, vn = _adam_math(w_ref[...], g, m_ref[...], v_ref[...])
        g_ref[...] = g
        d_ref[...] = delta
        mo_ref[...] = mn
        vo_ref[...] = vn

    o = jax.ShapeDtypeStruct((r, 1024), F32)
    full = _full_spec((r, 1024))
    return pl.pallas_call(
        body, name=name, grid=(1,),
        in_specs=[_full_spec((4, r, 1024)), full, full, full, full], out_specs=[full] * 4, out_shape=[o] * 4,
        compiler_params=_cparams(("arbitrary",)),
    )(chip_sums, last, w, m, v)


ASM_OUT = 256
ASM_SRC = 304


def _w_in_row(r):
    return r if r < 2048 else (r + O_G - 2048 if r < 4096 else r - 2048)


def _assemble_wt_main(g, *, name):
    table = []
    for blk in range(MAIN_COLS // ASM_OUT):
        j, l0 = divmod(_w_in_row(blk * ASM_OUT), IN_SHARD)
        sb = l0 // ASM_SRC
        n_a = min(ASM_OUT, min(IN_SHARD, (sb + 1) * ASM_SRC) - l0)
        if n_a == ASM_OUT:
            nxt = (j, sb)
        elif l0 + n_a == IN_SHARD:
            nxt = (j + 1, 0)
        else:
            nxt = (j, sb + 1)
        table.append((j, sb, l0 - sb * ASM_SRC, n_a) + nxt)

    def body(tab_ref, a_ref, b_ref, o_ref):
        blk = pl.program_id(0)
        off, n_a = tab_ref[blk, 2], tab_ref[blk, 3]
        r = lax.broadcasted_iota(I32, (ASM_OUT, ASM_SRC), 0)
        k = lax.broadcasted_iota(I32, (ASM_OUT, ASM_SRC), 1)
        sel_a = ((k == r + off) & (r < n_a)).astype(BF16)
        sel_b = ((k == r - n_a) & (r >= n_a)).astype(BF16)
        o_ref[...] = (_dot(sel_a, a_ref[0], NN) + _dot(sel_b, b_ref[0], NN)).astype(BF16)

    src = lambda c: pl.BlockSpec((1, ASM_SRC, D_MODEL), lambda blk, tab: (tab[blk, c], tab[blk, c + 1], 0))
    grid_spec = pltpu.PrefetchScalarGridSpec(
        num_scalar_prefetch=1, grid=(len(table),), in_specs=[src(0), src(4)],
        out_specs=pl.BlockSpec((ASM_OUT, D_MODEL), lambda blk, tab: (blk, 0)))
    return pl.pallas_call(
        body, name=name, grid_spec=grid_spec, out_shape=jax.ShapeDtypeStruct((MAIN_COLS, D_MODEL), BF16),
        compiler_params=_cparams(("parallel",)),
    )(jnp.asarray(table, I32), g, g)


def _pair_sum_small(mine, theirs, *, name):
    def body(a_ref, b_ref, o_ref):
        o_ref[...] = a_ref[...] + b_ref[...]

    full = _full_spec(mine.shape)
    return pl.pallas_call(
        body, name=name, grid=(1,), in_specs=[full, full], out_specs=full,
        out_shape=jax.ShapeDtypeStruct(mine.shape, F32), compiler_params=_cparams(("arbitrary",)),
    )(mine, theirs)


ANY = pl.BlockSpec(memory_space=pl.ANY)
OTHER_CHIPS = ((1, 0), (0, 1), (1, 1))


class _Carry:
    def __init__(self, inputs, out_shapes, scratch, start, wait, aliases=None, middle=None):
        self.inputs, self.out_shapes, self.scratch = list(inputs), list(out_shapes), list(scratch)
        self.start, self.wait, self.aliases, self.middle = start, wait, dict(aliases or {}), middle


def _carry_join(*carries):
    n_in = [len(c.inputs) for c in carries]
    n_out = [len(c.out_shapes) for c in carries]
    n_scr = [len(c.scratch) for c in carries]

    def split(refs, counts):
        out, k = [], 0
        for n in counts:
            out.append(refs[k:k + n])
            k += n
        return out

    def start(ins, outs, scr):
        for c, i, o, s in zip(carries, split(ins, n_in), split(outs, n_out), split(scr, n_scr)):
            c.start(i, o, s)

    def wait(ins, outs, scr):
        for c, i, o, s in zip(carries, split(ins, n_in), split(outs, n_out), split(scr, n_scr)):
            c.wait(i, o, s)

    def middle(ins, outs, scr):
        for c, i, o, s in zip(carries, split(ins, n_in), split(outs, n_out), split(scr, n_scr)):
            if c.middle is not None:
                c.middle(i, o, s)

    aliases = {}
    for k, c in enumerate(carries):
        aliases.update({sum(n_in[:k]) + i: sum(n_out[:k]) + o for i, o in c.aliases.items()})
    joined = _Carry(sum((c.inputs for c in carries), []), sum((c.out_shapes for c in carries), []),
                    sum((c.scratch for c in carries), []), start, wait, aliases,
                    middle if any(c.middle is not None for c in carries) else None)
    joined.counts = n_out
    joined.split = lambda results: split(results, n_out)
    return joined


def _carried(body, carry, n_in, n_out, grid):
    if carry is None:
        return body
    ci, co, cs = len(carry.inputs), len(carry.out_shapes), len(carry.scratch)

    def wrapped(*refs):
        ins, cins = refs[:n_in], refs[n_in:n_in + ci]
        outs, couts = refs[n_in + ci:n_in + ci + n_out], refs[n_in + ci + n_out:n_in + ci + n_out + co]
        rest = refs[n_in + ci + n_out + co:]
        scratch, cscr = rest[:len(rest) - cs], rest[len(rest) - cs:]
        first, last, step, steps = None, None, 0, 1
        for axis, size in enumerate(grid):
            f, l = pl.program_id(axis) == 0, pl.program_id(axis) == size - 1
            first = f if first is None else first & f
            last = l if last is None else last & l
            step, steps = step * size + pl.program_id(axis), steps * size

        @pl.when(first)
        def _():
            carry.start(cins, couts, cscr)

        if carry.middle is not None:
            @pl.when(step == steps // 2)
            def _():
                carry.middle(cins, couts, cscr)

        body(*ins, *outs, *scratch)

        @pl.when(last)
        def _():
            carry.wait(cins, couts, cscr)

    return wrapped


def _carry_call(body, carry, *, name, grid, in_specs, out_specs, out_shape, scratch_shapes, args, vmem=True,
                own_aliases=None):
    n_in, n_out = len(in_specs), len(out_specs)
    extra_in = [ANY] * len(carry.inputs) if carry else []
    extra_out = [ANY] * len(carry.out_shapes) if carry else []
    aliases = dict(own_aliases or {})
    if carry:
        aliases.update({n_in + i: n_out + o for i, o in carry.aliases.items()})
    out = pl.pallas_call(
        _carried(body, carry, n_in, n_out, grid), name=name, grid=grid,
        in_specs=list(in_specs) + extra_in, out_specs=list(out_specs) + extra_out,
        out_shape=list(out_shape) + (carry.out_shapes if carry else []),
        scratch_shapes=list(scratch_shapes) + (carry.scratch if carry else []),
        input_output_aliases=aliases,
        compiler_params=_cparams(("arbitrary",) * len(grid)) if vmem else None,
    )(*args, *(carry.inputs if carry else []))
    return list(out[:n_out]), list(out[n_out:])


def _run_carry(carry, *, name):
    return _carry_call(lambda: None, carry, name=name, grid=(1,), in_specs=[], out_specs=[], out_shape=[],
                       scratch_shapes=[], args=[], vmem=False)[1]


def _sems(n):
    return [pltpu.SemaphoreType.DMA((n,)), pltpu.SemaphoreType.DMA((n,))]


def _carry_gather1(shards):
    n = len(shards)
    per = 7

    def plan(x_refs, out_refs, scr):
        send_sems, recv_sems, local_sems = scr
        x, y, c = lax.axis_index("x"), lax.axis_index("y"), lax.axis_index("c")
        me, sibling = (x, y, c), (x, y, 1 - c)
        near_x, near_y, across = (1 - x, y, c), (x, 1 - y, c), (1 - x, 1 - y, c)

        def rows(ref, t, half):
            r = shards[t].shape[0]
            h = r if r < 32 else -(-(r // 2) // 16) * 16
            if half is None or h == r:
                return ref
            return ref.at[pl.ds(0, h)] if half == 0 else ref.at[pl.ds(h, r - h)]

        def copy(t, k, block, half, to, from_input=False):
            px, py, pc = block
            slab = rows(out_refs[t].at[4 * px + 2 * py + pc], t, half)
            return pltpu.make_async_remote_copy(
                src_ref=rows(x_refs[t], t, half) if from_input else slab, dst_ref=slab,
                send_sem=send_sems.at[per * t + k], recv_sem=recv_sems.at[per * t + k], device_id=to,
                device_id_type=MESH)

        two = [shards[t].shape[0] >= 32 for t in range(n)]
        local = lambda t: pltpu.make_async_copy(x_refs[t], out_refs[t].at[4 * x + 2 * y + c], local_sems.at[t])
        first = lambda t: ([(0, me, None, sibling), (1, me, 0, near_x)]
                           + ([(2, me, 1, near_y), (3, me, 1, near_x)] if two[t] else []) + [(4, me, 0, near_y)])
        passed = lambda t: [(5, near_x, 0, near_y)] + ([(6, near_y, 1, near_x)] if two[t] else [])
        early = lambda t: [(1, near_x, 0, me)] + ([(2, near_y, 1, me)] if two[t] else [])
        late = lambda t: ([(0, sibling, None, me), (4, near_y, 0, me), (5, across, 0, me)]
                          + ([(3, near_x, 1, me), (6, across, 1, me)] if two[t] else []))
        return copy, local, first, passed, early, late

    def start(x_refs, out_refs, scr):
        copy, local, first, _, _, _ = plan(x_refs, out_refs, scr)
        for urgent in (True, False):
            for t in range(n):
                if not urgent:
                    local(t).start()
                for k, block, half, to in first(t):
                    if (k in (1, 2)) == urgent:
                        copy(t, k, block, half, to, from_input=True).start()

    def middle(x_refs, out_refs, scr):
        copy, _, _, passed, early, _ = plan(x_refs, out_refs, scr)
        for t in range(n):
            for (k, block, half, to), fwd in zip(early(t), passed(t)):
                copy(t, k, block, half, to).wait_recv()
                copy(t, *fwd).start()

    def wait(x_refs, out_refs, scr):
        copy, local, first, passed, _, late = plan(x_refs, out_refs, scr)
        for t in range(n):
            for k, block, half, to in late(t):
                copy(t, k, block, half, to).wait_recv()
        for t in range(n):
            for k, block, half, to in first(t):
                copy(t, k, block, half, to, from_input=True).wait_send()
            for k, block, half, to in passed(t):
                copy(t, k, block, half, to).wait_send()
            local(t).wait()

    return _Carry(shards, [jax.ShapeDtypeStruct((N_DEV,) + a.shape, a.dtype) for a in shards],
                  _sems(per * n) + [pltpu.SemaphoreType.DMA((n,))], start, wait, middle=middle)


def _carry_gather2(gathered):
    n = len(gathered)

    def copies(in_refs, g_refs, scr, with_arrivals):
        send_sems, recv_sems = scr
        x, y, c = lax.axis_index("x"), lax.axis_index("y"), lax.axis_index("c")
        sends, arrivals = [], []
        for t in range(n):
            for j, (fx, fy) in enumerate(OTHER_CHIPS):
                px, py = x ^ fx, y ^ fy
                sems = dict(send_sem=send_sems.at[3 * t + j], recv_sem=recv_sems.at[3 * t + j],
                            device_id=(x, y, 1 - c), device_id_type=MESH)
                mine, theirs = 4 * px + 2 * py + c, 4 * px + 2 * py + (1 - c)
                sends.append(pltpu.make_async_remote_copy(src_ref=in_refs[t].at[mine], dst_ref=g_refs[t].at[mine], **sems))
                if with_arrivals:
                    arrivals.append(pltpu.make_async_remote_copy(
                        src_ref=in_refs[t].at[mine], dst_ref=g_refs[t].at[theirs], **sems))
        return sends, arrivals

    def start(in_refs, g_refs, scr):
        for cp in copies(in_refs, g_refs, scr, False)[0]:
            cp.start()

    def wait(in_refs, g_refs, scr):
        sends, arrivals = copies(in_refs, g_refs, scr, True)
        for cp in arrivals:
            cp.wait_recv()
        for cp in sends:
            cp.wait_send()

    return _Carry(gathered, [jax.ShapeDtypeStruct(a.shape, a.dtype) for a in gathered], _sems(3 * n), start, wait,
                  aliases={t: t for t in range(n)})


def _allreduce_rows(x, *, name):
    def body(x_ref, o_ref, sib_ref, mine_ref, tab_ref, send_sems, recv_sems):
        x, y, c = lax.axis_index("x"), lax.axis_index("y"), lax.axis_index("c")
        swap = pltpu.make_async_remote_copy(src_ref=x_ref, dst_ref=sib_ref, send_sem=send_sems.at[0],
                                            recv_sem=recv_sems.at[0], device_id=(x, y, 1 - c), device_id_type=MESH)
        swap.start()
        swap.wait()
        mine_ref[...] = x_ref[...] + sib_ref[...]
        tab_ref[pl.ds(2 * x + y, 1)] = mine_ref[...][None]

        def copy(k, slot):
            fx, fy = OTHER_CHIPS[k]
            return pltpu.make_async_remote_copy(
                src_ref=mine_ref, dst_ref=tab_ref.at[slot], send_sem=send_sems.at[1 + k], recv_sem=recv_sems.at[1 + k],
                device_id=(x ^ fx, y ^ fy, c), device_id_type=MESH)

        for k in range(3):
            copy(k, 2 * x + y).start()
        for k, (fx, fy) in enumerate(OTHER_CHIPS):
            copy(k, 2 * (x ^ fx) + (y ^ fy)).wait()
        o_ref[...] = ((tab_ref[0] + tab_ref[1]) + tab_ref[2]) + tab_ref[3]

    vmem = pl.BlockSpec(memory_space=pltpu.VMEM)
    return pl.pallas_call(
        body, name=name, out_shape=jax.ShapeDtypeStruct(x.shape, F32), in_specs=[vmem], out_specs=vmem,
        scratch_shapes=[pltpu.VMEM(x.shape, F32), pltpu.VMEM(x.shape, F32), pltpu.VMEM((4,) + x.shape, F32)] + _sems(4),
    )(x)


def _allgather(shards, *, name):
    n = len(shards)
    per = 10

    def body(*refs):
        x_refs, out_refs = refs[:n], refs[n:2 * n]
        send_sems, recv_sems, local_sems = refs[2 * n:]
        x, y, c = lax.axis_index("x"), lax.axis_index("y"), lax.axis_index("c")
        me, sibling = (x, y, c), (x, y, 1 - c)
        near_x, near_y, across = (1 - x, y), (x, 1 - y), (1 - x, 1 - y)

        def rows(ref, t, half):
            r = shards[t].shape[0]
            h = -(-(r // 2) // 16) * 16
            if half is None:
                return ref
            return ref.at[pl.ds(0, h)] if half == 0 else ref.at[pl.ds(h, r - h)]

        def copy(t, k, block, half, to, from_input=False):
            px, py, pc = block
            slab = rows(out_refs[t].at[4 * px + 2 * py + pc], t, half)
            return pltpu.make_async_remote_copy(
                src_ref=rows(x_refs[t], t, half) if from_input else slab, dst_ref=slab,
                send_sem=send_sems.at[per * t + k], recv_sem=recv_sems.at[per * t + k], device_id=to,
                device_id_type=MESH)

        mine = [pltpu.make_async_copy(x_refs[t], out_refs[t].at[4 * x + 2 * y + c], local_sems.at[t]) for t in range(n)]
        for cp in mine:
            cp.start()
        sent = []

        def send(cp):
            cp.start()
            sent.append(cp)

        for t in range(n):
            send(copy(t, 0, me, None, sibling, from_input=True))
            send(copy(t, 1, me, 0, (*near_x, c), from_input=True))
            send(copy(t, 2, me, 1, (*near_y, c), from_input=True))
            send(copy(t, 3, me, 1, (*near_x, c), from_input=True))
            send(copy(t, 4, me, 0, (*near_y, c), from_input=True))
        for t in range(n):
            copy(t, 1, (*near_x, c), 0, me).wait_recv()
            send(copy(t, 5, (*near_x, c), 0, (*near_y, c)))
            copy(t, 2, (*near_y, c), 1, me).wait_recv()
            send(copy(t, 6, (*near_y, c), 1, (*near_x, c)))
        for t in range(n):
            copy(t, 3, (*near_x, c), 1, me).wait_recv()
            send(copy(t, 7, (*near_x, c), None, sibling))
            copy(t, 4, (*near_y, c), 0, me).wait_recv()
            send(copy(t, 8, (*near_y, c), None, sibling))
            copy(t, 5, (*across, c), 0, me).wait_recv()
            copy(t, 6, (*across, c), 1, me).wait_recv()
            send(copy(t, 9, (*across, c), None, sibling))
        for t in range(n):
            copy(t, 0, sibling, None, me).wait_recv()
            for k, chip in ((7, near_x), (8, near_y), (9, across)):
                copy(t, k, (*chip, 1 - c), None, me).wait_recv()
        for cp in sent:
            cp.wait_send()
        for cp in mine:
            cp.wait()

    return pl.pallas_call(
        body, name=name, out_shape=[jax.ShapeDtypeStruct((N_DEV,) + a.shape, a.dtype) for a in shards],
        in_specs=[ANY] * n, out_specs=[ANY] * n,
        scratch_shapes=[pltpu.SemaphoreType.DMA((per * n,)), pltpu.SemaphoreType.DMA((per * n,)),
                        pltpu.SemaphoreType.DMA((n,))],
    )(*shards)


def _carry_sibling(slabs, small=None):
    n = len(slabs)
    extra = [] if small is None else [small]

    def copies(in_refs, out_refs, scr):
        send_sems, recv_sems = scr
        x, y, c = lax.axis_index("x"), lax.axis_index("y"), lax.axis_index("c")
        sibling = (x, y, 1 - c)
        out = []
        for t in range(n):
            for q in range(4):
                out.append(pltpu.make_async_remote_copy(
                    src_ref=in_refs[t].at[2 * q + (1 - c)], dst_ref=out_refs[t].at[q],
                    send_sem=send_sems.at[4 * t + q], recv_sem=recv_sems.at[4 * t + q],
                    device_id=sibling, device_id_type=MESH))
        if extra:
            out.append(pltpu.make_async_remote_copy(
                src_ref=in_refs[n], dst_ref=out_refs[n], send_sem=send_sems.at[4 * n], recv_sem=recv_sems.at[4 * n],
                device_id=sibling, device_id_type=MESH))
        return out

    def start(*refs):
        for cp in copies(*refs):
            cp.start()

    def wait(*refs):
        for cp in copies(*refs):
            cp.wait()

    return _Carry(list(slabs) + extra,
                  [jax.ShapeDtypeStruct((4,) + a.shape[1:], a.dtype) for a in slabs]
                  + [jax.ShapeDtypeStruct(a.shape, a.dtype) for a in extra], _sems(4 * n + 1), start, wait)


def _carry_chips(psums, small_sum=None):
    n = len(psums)
    table = small_sum is not None

    def copies(in_refs, out_refs, scr, arrivals):
        send_sems, recv_sems = scr[0], scr[1]
        x, y, c = lax.axis_index("x"), lax.axis_index("y"), lax.axis_index("c")
        out = []
        for k, (fx, fy) in enumerate(OTHER_CHIPS):
            px, py = x ^ fx, y ^ fy
            for t in range(n):
                out.append(pltpu.make_async_remote_copy(
                    src_ref=in_refs[t].at[2 * px + py], dst_ref=out_refs[t].at[k],
                    send_sem=send_sems.at[3 * t + k], recv_sem=recv_sems.at[3 * t + k],
                    device_id=(px, py, c), device_id_type=MESH))
            if table:
                slot = 2 * px + py if arrivals else 2 * x + y
                out.append(pltpu.make_async_remote_copy(
                    src_ref=in_refs[n], dst_ref=out_refs[n].at[slot], send_sem=send_sems.at[3 * n + k],
                    recv_sem=recv_sems.at[3 * n + k], device_id=(px, py, c), device_id_type=MESH))
        return out

    def own(in_refs, out_refs, scr):
        x, y = lax.axis_index("x"), lax.axis_index("y")
        return pltpu.make_async_copy(in_refs[n], out_refs[n].at[2 * x + y], scr[2])

    def start(in_refs, out_refs, scr):
        if table:
            own(in_refs, out_refs, scr).start()
        for cp in copies(in_refs, out_refs, scr, False):
            cp.start()

    def wait(in_refs, out_refs, scr):
        for cp in copies(in_refs, out_refs, scr, True):
            cp.wait()
        if table:
            own(in_refs, out_refs, scr).wait()

    out_shapes = [jax.ShapeDtypeStruct((3,) + a.shape[1:], a.dtype) for a in psums]
    if table:
        out_shapes.append(jax.ShapeDtypeStruct((4,) + small_sum.shape, F32))
    return _Carry(list(psums) + ([small_sum] if table else []), out_shapes,
                  _sems(3 * n + 3) + ([pltpu.SemaphoreType.DMA] if table else []), start, wait)


def _to_comm(name, kind, block, dtype=BF16):
    a = block[0]
    if kind == "cols":
        a = a.T
        if name == "w_in" and dtype == BF16:
            a = jnp.pad(a, ((0, IN_SHARD_PAD - IN_SHARD), (0, 0)))
    return a if kind == "f32" else a.astype(dtype)


def _from_comm(name, kind, a):
    if kind == "cols":
        if name == "w_in" and a.shape[0] != IN_SHARD:
            a = a[:IN_SHARD]
        a = a.T
    return a[None]


def _assemble_weights(g):
    out = {}
    if "w_in" in g:
        out["wt_main"] = _assemble_wt_main(g["w_in"], name="assemble_w_in")
        j, l0 = divmod(O_F, IN_SHARD)
        out["wt_f"] = jnp.pad(g["w_in"][j, l0:l0 + HEADS], ((0, 128 - HEADS), (0, 0)))
    square = dict(w_branch_a="w_a", w_branch_b="w_b", w_out="w_out", w_ple_gate="w_pg")
    for long, short in square.items():
        if long in g:
            out[short] = g[long].reshape(D_MODEL, D_MODEL)
    if "w_up" in g:
        out["wt_up"] = g["w_up"].reshape(2 * D_FF, D_MODEL)
    if "conv_w" in g:
        out["conv_w"] = g["conv_w"].transpose(1, 0, 2).reshape(3, 2 * D_FF)
    if "w_down" in g:
        out["w_down"] = g["w_down"].reshape(D_FF, D_MODEL)
    if "w_ple" in g:
        out["wt_ple"] = g["w_ple"].reshape(D_MODEL, PLE_DIM)
    return out


def _grad_slabs(gr):
    out = {}
    if "wt_main" in gr:
        gm, gf = gr["wt_main"], gr["wt_f"]
        segments = ((0, 2048, gm, 0), (2048, O_F, gm, 2048), (O_F, O_G, gf, -O_F), (O_G, IN_COLS, gm, 2048 - O_G))
        slabs = []
        for j in range(N_DEV):
            lo, hi = j * IN_SHARD, (j + 1) * IN_SHARD
            pieces = [src[max(lo, a) + shift:min(hi, b) + shift] for a, b, src, shift in segments if max(lo, a) < min(hi, b)]
            pieces.append(jnp.zeros((IN_SHARD_PAD - IN_SHARD, D_MODEL), gm.dtype))
            slabs.append(jnp.concatenate(pieces, axis=0))
        out["w_in"] = jnp.stack(slabs)
    rows = dict(w_a="w_branch_a", w_b="w_branch_b", w_out="w_out", wt_up="w_up", w_down="w_down", w_pg="w_ple_gate")
    for short, long in rows.items():
        if short in gr:
            out[long] = gr[short].reshape(N_DEV, -1, D_MODEL)
    if "conv_w" in gr:
        out["conv_w"] = gr["conv_w"].reshape(3, N_DEV, -1).transpose(1, 0, 2)
    if "wt_ple" in gr:
        out["w_ple"] = gr["wt_ple"].reshape(N_DEV, -1, PLE_DIM)
    return {k: v.astype(BF16) for k, v in out.items()}


def _rows(a, rows):
    flat = a.reshape(-1)
    return jnp.pad(flat, (0, rows * 1024 - flat.shape[0])).reshape(rows, 1024)


def _pack_small(parts):
    return jnp.concatenate([_rows(parts[n].astype(F32), r) for n, r in SMALL], axis=0)


def _small(packed, name, shape):
    off, r = SMALL_OFF[name]
    n = math.prod(shape)
    return packed[off:off + r].reshape(-1)[:n].reshape(shape)


class _Exchanges:
    W_S_ROWS = SMALL_OFF["gmlp_w_s"]

    def __init__(self, later, shards, pos):
        self.later, self.shards, self.pos = later, dict(zip(later, shards)), pos
        self.level1, self.slabs, self.from_sib, self.sums32, self.reduced, self.tables = {}, {}, {}, {}, {}, {}

    def gather1(self, names):
        carry = _carry_gather1([self.shards[n] for n in names])
        carry.names = names
        return carry

    def gather1_done(self, carry, results):
        self.level1.update(zip(carry.names, results))

    def gather2(self):
        return _carry_gather2([self.level1[n] for n in self.later])

    def weights(self, full):
        return _assemble_weights(dict(zip(self.later, full)))

    def sibling(self, grads):
        slabs = _grad_slabs(grads)
        self.slabs.update(slabs)
        carry = _carry_sibling(list(slabs.values()))
        carry.names = list(slabs)
        return carry

    def sibling_done(self, carry, results):
        self.from_sib.update(zip(carry.names, results))

    def chips(self, names, table=None):
        sums = {n: _sum_pairs(self.slabs[n], self.from_sib[n], self.pos, name="sum_sibling_" + n) for n in names}
        self.sums32.update({n: s32 for n, (s32, _) in sums.items()})
        carry = _carry_chips([s16 for _, s16 in sums.values()], None if table is None else self.table_part(table))
        carry.names, carry.table = list(names), table
        return carry

    def chips_done(self, carry, results):
        if carry.table is not None:
            *results, self.tables[carry.table] = results
        self.reduced.update({n: (self.sums32[n], r) for n, r in zip(carry.names, results)})

    def sibling_small(self, small_g):
        self.small_g = small_g
        return _carry_sibling([], small_g)

    def sibling_small_done(self, small_sib):
        self.small_chip = _pair_sum_small(self.small_g, small_sib, name="sum_sibling_small")

    def table_part(self, which):
        off, rows = self.W_S_ROWS
        if which == "w_s":
            return self.small_chip[off:off + rows]
        return jnp.concatenate([self.small_chip[:off], self.small_chip[off + rows:]], axis=0)

    def table(self):
        off = self.W_S_ROWS[0]
        rest = self.tables["rest"]
        return jnp.concatenate([rest[:, :off], self.tables["w_s"], rest[:, off:]], axis=1)


def _local_step(x, p, target, w, sm, ex=None):
    s = x.shape[0]
    mm = _matmul
    wt_main = w["wt_main"]
    conv_b = sm["conv_b"]
    bs_t = jnp.pad(sm["gmlp_b_s"].T, ((0, 0), (0, 128 - GROUPS)))
    b_f = jnp.pad(sm["b_f"], ((0, 0), (0, 128 - HEADS)))
    big = dict(tm=1024, tn=1024, tk=1024)
    whole_s = dict(tn=1024, tk=s)

    h = _rmsnorm_fwd(x, sm["norm_mix_g"], name="norm_mix")
    tall = dict(tm=s, tn=512, tk=1024)
    qkv_args = dict(mode="nt", out_dtype=BF16, name="in_qkv", n=3072, b_off=8, **tall)
    f_logit = mm(h, w["wt_f"], mode="nt", out_dtype=F32, name="in_f", tm=1024, tk=1024)
    cqe = _forget_cumsum(f_logit, b_f, name="forget_cumsum")
    uvg = dict(mode="nt", out_dtype=F32, name="in_uvg", n=4096, **tall)
    if ex is None:
        qkv = mm(h, wt_main, **qkv_args)
        (qa, ka, vt), _ = _attn_prep(qkv, cqe, name="attn_prep")
        (b, lse3), _ = _attn_fwd(qa, ka, vt, name="attn_fwd")
        zuvg = mm(h, wt_main, **uvg)
    else:
        groups = (["w_branch_a"], ["w_branch_b"], [n for n in ex.later if n not in ("w_branch_a", "w_branch_b")])
        carries = [ex.gather1(names) for names in groups]
        qkv, got0 = mm(h, wt_main, carry=carries[0], **qkv_args)
        (qa, ka, vt), got1 = _attn_prep(qkv, cqe, carries[1], name="attn_prep")
        (b, lse3), got2 = _attn_fwd(qa, ka, vt, carries[2], name="attn_fwd")
        for carry, got in zip(carries, (got0, got1, got2)):
            ex.gather1_done(carry, got)
        zuvg, full = mm(h, wt_main, carry=ex.gather2(), **uvg)
        w = {**w, **ex.weights(full)}
    a = _gmlp_fwd(zuvg, sm["gmlp_ln_g"], sm["gmlp_ln_b"], sm["gmlp_w_s"], bs_t, name="gmlp_fwd")
    wt_up, conv_w = w["wt_up"], w["conv_w"]
    ya, yb, merged = _branches_merge(a, b, w["w_a"], w["w_b"], zuvg, name="branches_merge")
    x1, h2 = mm(merged, w["w_out"], mode="nn", out_dtype=F32, name="out_proj", add=x, norm_g=sm["norm_ffn_g"], **big)
    up_a, up_g, act = _up_convglu(h2, wt_up, conv_w, conv_b, name="up_convglu")
    x2, h3 = mm(act, w["w_down"], mode="nn", out_dtype=F32, name="down", tm=1024, tn=1024, tk=1408, add=x1,
                norm_g=sm["norm_ple_g"])

    loss, dx3, dple, dgp, d_norm_final = _ple_loss(p, w["wt_ple"], h3, w["w_pg"], x2, target, sm["norm_final_g"],
                                                   name="ple_loss")
    g_wt_ple = mm(dple, p, mode="tn", out_dtype=BF16, name="d_w_ple", tm=512, tn=256, tk=s)
    g_w_pg = mm(h3, dgp, mode="tn", out_dtype=BF16, name="d_w_pg", tm=512, **whole_s)
    (dx2, dx2b, d_norm_ple), _ = _matmul_rmsnorm_bwd([dgp], w["w_pg"], dx3, x2, sm["norm_ple_g"], mode="nt", tk=1024,
                                                     name="d_h3_norm_ple_bwd")
    g_w_down = mm(act, dx2b, mode="tn", out_dtype=BF16, name="d_w_down", tm=1408, **whole_s)
    dact_args = dict(mode="nt", out_dtype=BF16, name="d_act", tm=s, tn=256, tk=1024)
    if ex is None:
        dact = mm(dx2b, w["w_down"], **dact_args)
    else:
        early = ex.sibling(dict(w_pg=g_w_pg, wt_ple=g_wt_ple))
        dact, got = mm(dx2b, w["w_down"], carry=early, **dact_args)
        ex.sibling_done(early, got)
    dup_a, dup_g, dcw_a, dcw_g, dcb_a, dcb_g = _convglu_bwd(dact, up_a, up_g, conv_w, conv_b, name="convglu_bwd")
    g_wt_up = mm(dup_a, h2, mode="tn", out_dtype=BF16, name="d_w_up_a", tm=1408, out_rows=2 * D_FF, **whole_s)
    g_wt_up = mm(dup_g, h2, mode="tn", out_dtype=BF16, name="d_w_up_g", tm=1408, out_rows=2 * D_FF,
                 o_off=D_FF // 1408, into=g_wt_up, **whole_s)
    up_sib = None if ex is None else ex.sibling(dict(wt_up=g_wt_up))
    (dx1, dx1b, d_norm_ffn), got = _matmul_rmsnorm_bwd([dup_a, dup_g], wt_up, dx2, x1, sm["norm_ffn_g"], mode="nn",
                                                       tk=1408, name="d_h2_norm_ffn_bwd", resident=True, carry=up_sib)
    if ex is not None:
        ex.sibling_done(up_sib, got)
    g_w_out = mm(merged, dx1b, mode="tn", out_dtype=BF16, name="d_w_out", tm=512, **whole_s)
    dya, dyb, dga, dgb = _merge_bwd(dx1b, w["w_out"], ya, yb, zuvg, name="merge_bwd")
    g_w_a = mm(a, dya, mode="tn", out_dtype=BF16, name="d_w_a", tm=512, **whole_s)
    g_w_b = mm(b, dyb, mode="tn", out_dtype=BF16, name="d_w_b", tm=512, **whole_s)
    da = mm(dya, w["w_a"], mode="nt", out_dtype=BF16, name="d_a", **big)
    db = mm(dyb, w["w_b"], mode="nt", out_dtype=BF16, name="d_b", **big)
    grads = dict(w_a=g_w_a, w_b=g_w_b, w_out=g_w_out, wt_up=g_wt_up, conv_w=jnp.concatenate([dcw_a, dcw_g], axis=1),
                 w_down=g_w_down, wt_ple=g_wt_ple, w_pg=g_w_pg)
    gmlp_args = (da, zuvg, sm["gmlp_ln_g"], sm["gmlp_ln_b"], sm["gmlp_w_s"], bs_t)
    if ex is None:
        (dzu, dzv, d_w_s, d_bs_t, d_ln_g, d_ln_b), _ = _gmlp_bwd(*gmlp_args, name="gmlp_bwd")
    else:
        rest = ex.sibling({k: v for k, v in grads.items() if k not in ("w_pg", "wt_ple", "wt_up")})
        early_chips = ex.chips(early.names)
        both = _carry_join(rest, early_chips)
        (dzu, dzv, d_w_s, d_bs_t, d_ln_g, d_ln_b), got = _gmlp_bwd(*gmlp_args, both, name="gmlp_bwd")
        got_rest, got_early = both.split(got)
        ex.sibling_done(rest, got_rest)
        ex.chips_done(early_chips, got_early)
    small = dict(norm_mix_g=jnp.zeros((1, D_MODEL), F32), b_f=jnp.zeros((1, HEADS), F32), gmlp_ln_g=d_ln_g,
                 gmlp_ln_b=d_ln_b, gmlp_w_s=d_w_s, gmlp_b_s=d_bs_t[:, :GROUPS].T, norm_ffn_g=d_norm_ffn,
                 conv_b=jnp.concatenate([dcb_a, dcb_g], axis=1), norm_ple_g=d_norm_ple, norm_final_g=d_norm_final)
    if ex is None:
        delta3, _ = _attn_delta(db, b, name="attn_delta")
        (dq, dk, dv, aux, dcq3), _ = _attn_bwd(qa, ka, qkv, db, lse3, delta3, name="attn_bwd")
    else:
        delta3, (small_sib,) = _attn_delta(db, b, ex.sibling_small(_pack_small(small)), name="attn_delta")
        ex.sibling_small_done(small_sib)
        main_chips = ex.chips(up_sib.names + rest.names, table="rest")
        (dq, dk, dv, aux, dcq3), got = _attn_bwd(qa, ka, qkv, db, lse3, delta3, main_chips, name="attn_bwd")
        ex.chips_done(main_chips, got)
    dcq16 = jnp.pad(dcq3[:, :2, :].reshape(HEADS, s).T, ((0, 0), (0, 128 - HEADS)))
    dzf, d_b_f = _forget_bwd(dcq16, aux, f_logit, b_f, name="forget_bwd")
    dz_parts = [dzu, dzv, dga, dgb, dq, dk, dv]
    w_s_chips = None if ex is None else ex.chips([], table="w_s")
    g_wt_main, got = _grad_w_parts(dz_parts, h, name="d_w_main", tm=512, carry=w_s_chips)
    if ex is not None:
        ex.chips_done(w_s_chips, got)
    g_wt_f = mm(dzf, h, mode="tn", out_dtype=BF16, name="d_w_f", **whole_s)
    grads = dict(grads, wt_main=g_wt_main, wt_f=g_wt_f)
    w_in_chips = None
    if ex is not None:
        w_in_sib = ex.sibling(dict(wt_main=g_wt_main, wt_f=g_wt_f))
        ex.sibling_done(w_in_sib, _run_carry(w_in_sib, name="exchange_sibling_w_in"))
        w_in_chips = ex.chips(w_in_sib.names)
    (dx0, _, d_norm_mix), got = _matmul_rmsnorm_bwd(dz_parts, wt_main, dx1, x, sm["norm_mix_g"], mode="nn", tk=1024,
                                                    extra=(dzf, w["wt_f"]), name="d_h_norm_mix_bwd", carry=w_in_chips,
                                                    lead=True)
    if ex is not None:
        ex.chips_done(w_in_chips, got)
    return loss, dx0, grads, dict(small, norm_mix_g=d_norm_mix, b_f=d_b_f[:, :HEADS])


def kernel(x, p, norm_mix_g, w_in, b_f, gmlp_ln_g, gmlp_ln_b, gmlp_w_s, gmlp_b_s, w_branch_a, w_branch_b, w_out, norm_ffn_g, w_up, conv_w, conv_b, w_down, norm_ple_g, w_ple, w_ple_gate, norm_final_g, loss_target, m_norm_mix_g, m_w_in, m_b_f, m_gmlp_ln_g, m_gmlp_ln_b, m_gmlp_w_s, m_gmlp_b_s, m_w_branch_a, m_w_branch_b, m_w_out, m_norm_ffn_g, m_w_up, m_conv_w, m_conv_b, m_w_down, m_norm_ple_g, m_w_ple, m_w_ple_gate, m_norm_final_g, v_norm_mix_g, v_w_in, v_b_f, v_gmlp_ln_g, v_gmlp_ln_b, v_gmlp_w_s, v_gmlp_b_s, v_w_branch_a, v_w_branch_b, v_w_out, v_norm_ffn_g, v_w_up, v_conv_w, v_conv_b, v_w_down, v_norm_ple_g, v_w_ple, v_w_ple_gate, v_norm_final_g):
    given = dict(locals())
    weights = {n: given[n] for n in WEIGHT_ORDER}
    mom_m = {n: given["m_" + n] for n in WEIGHT_ORDER}
    mom_v = {n: given["v_" + n] for n in WEIGHT_ORDER}
    pos = jnp.stack([lax.axis_index("x"), lax.axis_index("y"), lax.axis_index("c")]).astype(I32)
    names = [n for n, _ in SHARDED]
    kinds = dict(SHARDED)

    later = [n for n in names if n != "w_in"]

    first = _allgather([_to_comm("w_in", kinds["w_in"], weights["w_in"])], name="allgather_w_in")
    ex = _Exchanges(later, [_to_comm(n, kinds[n], weights[n]) for n in later], pos)

    sm = dict(norm_mix_g=norm_mix_g, b_f=b_f, gmlp_ln_g=gmlp_ln_g, gmlp_ln_b=gmlp_ln_b, gmlp_w_s=gmlp_w_s[0],
              gmlp_b_s=gmlp_b_s[0], norm_ffn_g=norm_ffn_g, conv_b=conv_b, norm_ple_g=norm_ple_g,
              norm_final_g=norm_final_g.reshape(1, D_MODEL))
    loss_part, dx0, grads, small = _local_step(
        x[0], p[0, 0], loss_target[0], _assemble_weights({"w_in": first[0]}), sm, ex)

    b_f_and_loss = jnp.concatenate([small["b_f"].reshape(-1), loss_part[0, :1]])
    last = _allreduce_rows(jnp.concatenate([_rows(small["norm_mix_g"], 8), _rows(b_f_and_loss, 8)], axis=0),
                           name="allreduce_last")
    loss = last[8, HEADS]
    small_last = jnp.pad(last, ((0, SMALL_ROWS - 16), (0, 0)))

    grad, delta, new_m, new_v = {}, {}, {}, {}
    for n in names:
        s32, r = ex.reduced[n]
        outs = _adam_sharded(s32, r, *[_to_comm(n, kinds[n], src[n], F32) for src in (weights, mom_m, mom_v)], pos,
                             name="adam_" + n)
        grad[n], delta[n], new_m[n], new_v[n] = [_from_comm(n, kinds[n], o) for o in outs]
    replicated = [n for n, _ in SMALL]
    rep = lambda src: _pack_small({n: src[n] for n in replicated})
    packed = _adam_replicated(ex.table(), small_last, rep(weights), rep(mom_m), rep(mom_v), name="adam_replicated")
    for out, pk in zip((grad, delta, new_m, new_v), packed):
        for n in replicated:
            out[n] = _small(pk, n, weights[n].shape)

    return (loss, dx0, *[grad[n] for n in WEIGHT_ORDER], *[delta[n] for n in WEIGHT_ORDER],
            *[new_m[n] for n in WEIGHT_ORDER], *[new_v[n] for n in WEIGHT_ORDER])
```

```python
import functools
import math

import jax
import jax.numpy as jnp
from jax import lax
from jax.experimental import pallas as pl
from jax.experimental.pallas import tpu as pltpu

F32 = jnp.float32
BF16 = jnp.bfloat16
I32 = jnp.int32

D_MODEL = 1024
GROUPS = 8
GDIM = 128
GBLOCK = 128
CHUNK = 64
HEADS = 16
HEAD_DIM = 64
D_FF = 2816
PLE_DIM = 256
EPS = 1e-6
N_DEV = 8
ATT_SCALE = HEAD_DIM ** -0.5
NEG = -1e30

ADAM_LR = 0.001
ADAM_B1 = 0.9
ADAM_B2 = 0.999
ADAM_EPS = 1e-08
ADAM_WD = 0.01
ADAM_STEP = 10

V7X_VMEM_LIMIT = 48 * 1024 * 1024
MESH = pl.DeviceIdType.MESH

O_F = 2 * 1024 + 3 * 1024
O_G = O_F + HEADS
IN_COLS = O_G + 2 * D_MODEL
MAIN_COLS = IN_COLS - HEADS
IN_SHARD = IN_COLS // N_DEV
IN_SHARD_PAD = 912

SHARDED = (("w_in", "cols"), ("w_branch_a", "rows"), ("w_branch_b", "rows"), ("w_out", "rows"), ("w_up", "cols"),
           ("conv_w", "f32"), ("w_down", "rows"), ("w_ple", "cols"), ("w_ple_gate", "rows"))

SMALL = (("norm_mix_g", 8), ("b_f", 8), ("gmlp_ln_g", 8), ("gmlp_ln_b", 8), ("gmlp_w_s", 128), ("gmlp_b_s", 8),
         ("norm_ffn_g", 8), ("conv_b", 8), ("norm_ple_g", 8), ("norm_final_g", 8))
SMALL_OFF = {}
_o = 0
for _n, _r in SMALL:
    SMALL_OFF[_n] = (_o, _r)
    _o += _r
SMALL_ROWS = _o

WEIGHT_ORDER = ("norm_mix_g", "w_in", "b_f", "gmlp_ln_g", "gmlp_ln_b", "gmlp_w_s", "gmlp_b_s", "w_branch_a",
                "w_branch_b", "w_out", "norm_ffn_g", "w_up", "conv_w", "conv_b", "w_down", "norm_ple_g", "w_ple",
                "w_ple_gate", "norm_final_g")


def _cparams(sem):
    return pltpu.CompilerParams(dimension_semantics=sem, vmem_limit_bytes=V7X_VMEM_LIMIT)


def _gelu(x):
    c = math.sqrt(2.0 / math.pi)
    return 0.5 * x * (1.0 + jnp.tanh(c * (x + 0.044715 * x * x * x)))


def _gelu_and_grad(x):
    c = math.sqrt(2.0 / math.pi)
    t = jnp.tanh(c * (x + 0.044715 * x * x * x))
    g = 0.5 * x * (1.0 + t)
    dg = 0.5 * (1.0 + t) + 0.5 * x * (1.0 - t * t) * (c * (1.0 + 3.0 * 0.044715 * x * x))
    return g, dg


def _sigmoid(x):
    return 1.0 / (1.0 + jnp.exp(-x))


def _dot(a, b, dims):
    return lax.dot_general(a, b, (dims, ((), ())), preferred_element_type=F32)


NN = ((1,), (0,))
NT = ((1,), (1,))
TN = ((0,), (0,))


def _row_tile(rows, most):
    best = None
    for t in range(16, min(rows, most) + 1, 16):
        if rows % t == 0:
            best = t
    return best if best is not None else rows


def _matmul(a, b, *, mode, out_dtype, name, tm=512, tn=512, tk=512, add=None, n=None, b_off=0,
            out_rows=None, o_off=0, into=None, norm_g=None, carry=None):
    if mode == "tn":
        kdim, m = a.shape
    else:
        m, kdim = a.shape
    if n is None:
        n = b.shape[0] if mode == "nt" else b.shape[1]
    tm, tn, tk = min(tm, m), min(tn, n), min(tk, kdim)
    assert m % tm == 0 and n % tn == 0 and kdim % tk == 0, (name, m, n, kdim, tm, tn, tk)
    nk = kdim // tk
    dims = {"nn": NN, "nt": NT, "tn": TN}[mode]

    n_in = 2 + (add is not None) + (into is not None) + (norm_g is not None)
    assert norm_g is None or tn == n, "the RMS norm needs whole rows"

    def finish(r, refs):
        if add is not None:
            r = refs[2][...].astype(F32) + r
        refs[n_in][...] = r.astype(out_dtype)
        if norm_g is not None:
            rs = lax.rsqrt(jnp.mean(r * r, axis=-1, keepdims=True) + EPS)
            refs[n_in + 1][...] = ((r * rs) * refs[n_in - 1][...]).astype(BF16)

    def body(*refs):
        a_ref, b_ref = refs[:2]
        part = _dot(a_ref[...].astype(BF16), b_ref[...].astype(BF16), dims)
        if nk == 1:
            finish(part, refs)
            return
        acc_ref = refs[-1]
        k = pl.program_id(2)

        @pl.when(k == 0)
        def _():
            acc_ref[...] = part

        @pl.when((k > 0) & (k < nk - 1))
        def _():
            acc_ref[...] += part

        @pl.when(k == nk - 1)
        def _():
            finish(acc_ref[...] + part, refs)

    a_spec = pl.BlockSpec((tk, tm), lambda i, j, k: (k, i)) if mode == "tn" else pl.BlockSpec((tm, tk), lambda i, j, k: (i, k))
    if mode == "nt":
        b_spec = pl.BlockSpec((tn, tk), lambda i, j, k: (j + b_off, k))
    else:
        b_spec = pl.BlockSpec((tk, tn), lambda i, j, k: (k + b_off, j))
    o_spec = pl.BlockSpec((tm, tn), lambda i, j, k: (i + o_off, j))
    in_specs = [a_spec, b_spec] + ([pl.BlockSpec((tm, tn), lambda i, j, k: (i, j))] if add is not None else [])
    args = (a, b) + ((add,) if add is not None else ())
    aliases = {}
    if into is not None:
        aliases = {len(args): 0}
        in_specs.append(pl.BlockSpec(memory_space=pl.ANY))
        args += (into,)
    out_specs = [o_spec]
    out_shape = [jax.ShapeDtypeStruct((m if out_rows is None else out_rows, n), out_dtype)]
    if norm_g is not None:
        in_specs.append(pl.BlockSpec((1, n), lambda i, j, k: (0, 0)))
        args += (norm_g,)
        out_specs.append(pl.BlockSpec((tm, tn), lambda i, j, k: (i, j)))
        out_shape.append(jax.ShapeDtypeStruct((m, n), BF16))
    outs, carried = _carry_call(
        body, carry, name=name, grid=(m // tm, n // tn, nk), in_specs=in_specs, out_specs=out_specs,
        out_shape=out_shape, scratch_shapes=[pltpu.VMEM((tm, tn), F32)] if nk > 1 else [], args=args,
        own_aliases=aliases)
    out = outs[0] if norm_g is None else tuple(outs)
    return out if carry is None else (out, carried)


def _row_spec(tr, width, col_block=0):
    return pl.BlockSpec((tr, width), lambda i: (i, col_block))


def _full_spec(shape):
    return pl.BlockSpec(shape, lambda i: tuple(0 for _ in shape))


def _rmsnorm_fwd(x, g, *, name, tr=256):
    s, d = x.shape

    def body(x_ref, g_ref, o_ref):
        xv = x_ref[...]
        r = lax.rsqrt(jnp.mean(xv * xv, axis=-1, keepdims=True) + EPS)
        o_ref[...] = ((xv * r) * g_ref[...]).astype(BF16)

    return pl.pallas_call(
        body, name=name, grid=(s // tr,),
        in_specs=[_row_spec(tr, d), _full_spec((1, d))], out_specs=_row_spec(tr, d),
        out_shape=jax.ShapeDtypeStruct((s, d), BF16), compiler_params=_cparams(("parallel",)),
    )(x, g)


def _matmul_rmsnorm_bwd(a_parts, b, dres, x, g, *, mode, tk, name, extra=None, tm=512, carry=None, lead=False,
                        resident=False):
    s, d = x.shape
    n_row = s // tm
    spans, lo = [], 0
    for a in a_parts:
        spans.append((lo, lo + a.shape[1] // tk))
        lo = spans[-1][1]
    n_main, total = lo, lo + (extra is not None)
    n_parts = len(a_parts)

    def body(*refs):
        a_refs, b_ref = refs[:n_parts], refs[n_parts]
        k0 = n_parts + 1
        ax_ref, bx_ref = (refs[k0], refs[k0 + 1]) if extra is not None else (None, None)
        k0 += 2 * (extra is not None)
        dres_ref, x_ref, g_ref, dx_ref, dxb_ref, dg_ref, acc_all = refs[k0:k0 + 7]
        if resident:
            kk, i = pl.program_id(0), pl.program_id(1)
            acc_ref = acc_all.at[pl.ds(pl.multiple_of(i * tm, tm), tm)]
        else:
            i, kk = pl.program_id(0), pl.program_id(1)
            acc_ref = acc_all

        def accumulate(part, first):
            if first:
                @pl.when(kk == 0)
                def _():
                    acc_ref[...] = part

                @pl.when(kk > 0)
                def _():
                    acc_ref[...] += part
            else:
                acc_ref[...] += part

        for p, (a_ref, (lo_p, hi_p)) in enumerate(zip(a_refs, spans)):
            @pl.when((kk >= lo_p) & (kk < hi_p))
            def _(a_ref=a_ref, lo_p=lo_p):
                accumulate(_dot(a_ref[...].astype(BF16), b_ref[...].astype(BF16), NN if mode == "nn" else NT), lo_p == 0)

        if extra is not None:
            @pl.when(kk == n_main)
            def _():
                accumulate(_dot(ax_ref[...].astype(BF16), bx_ref[...].astype(BF16), NN), False)

        @pl.when(kk == total - 1)
        def _():
            dhv = acc_ref[...]
            xv = x_ref[...]
            r = lax.rsqrt(jnp.mean(xv * xv, axis=-1, keepdims=True) + EPS)
            xhat = xv * r
            dxhat = dhv * g_ref[...]
            dx = dres_ref[...] + r * (dxhat - xhat * jnp.mean(dxhat * xhat, axis=-1, keepdims=True))
            dx_ref[...] = dx
            dxb_ref[...] = dx.astype(BF16)
            dgp = jnp.sum(dhv * xhat, axis=0, keepdims=True)

            @pl.when(i == 0)
            def _():
                dg_ref[...] = dgp

            @pl.when(i > 0)
            def _():
                dg_ref[...] += dgp

    def spec(shape, index):
        return pl.BlockSpec(shape, (lambda kk, i: index(i, kk)) if resident else index)

    def row(i, kk, lo_p, hi_p):
        if not resident:
            return i
        return jnp.where(kk < lo_p, 0, jnp.where(kk >= hi_p, n_row - 1, i))

    a_specs = [spec((tm, tk), lambda i, kk, lo_p=lo_p, hi_p=hi_p: (row(i, kk, lo_p, hi_p),
                                                                    jnp.clip(kk - lo_p, 0, hi_p - lo_p - 1)))
               for lo_p, hi_p in spans]
    step = lambda kk: jnp.minimum(kk, n_main - 1)
    b_spec = (spec((tk, d), lambda i, kk: (step(kk), 0)) if mode == "nn"
              else spec((d, tk), lambda i, kk: (0, step(kk))))
    rows = spec((tm, d), lambda i, kk: (row(i, kk, total - 1, total), 0))
    one = spec((1, d), lambda i, kk: (0, 0))
    dx_spec, dx_shape = rows, jax.ShapeDtypeStruct((s, d), F32)
    if lead:
        dx_spec = spec((None, tm, d), lambda i, kk: (0, row(i, kk, total - 1, total), 0))
        dx_shape = jax.ShapeDtypeStruct((1, s, d), F32)
    x_specs, x_args = [], []
    if extra is not None:
        kx = extra[0].shape[1]
        x_specs = [spec((tm, kx), lambda i, kk: (row(i, kk, n_main, total), 0)), spec((kx, d), lambda i, kk: (0, 0))]
        x_args = list(extra)
    (dx, dxb, dg), carried = _carry_call(
        body, carry, name=name, grid=(total, n_row) if resident else (n_row, total),
        in_specs=a_specs + [b_spec] + x_specs + [rows, rows, one], out_specs=[dx_spec, rows, one],
        out_shape=[dx_shape, jax.ShapeDtypeStruct((s, d), BF16), jax.ShapeDtypeStruct((1, d), F32)],
        scratch_shapes=[pltpu.VMEM((s if resident else tm, d), F32)], args=list(a_parts) + [b] + x_args + [dres, x, g])
    return (dx, dxb, dg), carried


def _grad_w_parts(a_parts, b, *, name, tm=512, carry=None):
    s, width = a_parts[0].shape
    per, n = width // tm, b.shape[1]

    def body(*refs):
        a_refs, b_ref, o_ref = refs[:len(a_parts)], refs[len(a_parts)], refs[len(a_parts) + 1]
        i = pl.program_id(0)
        for p, a_ref in enumerate(a_refs):
            @pl.when(i // per == p)
            def _(a_ref=a_ref):
                o_ref[...] = _dot(a_ref[...].astype(BF16), b_ref[...].astype(BF16), TN).astype(BF16)

    a_specs = [pl.BlockSpec((s, tm), lambda i, p=p: (0, jnp.clip(i - p * per, 0, per - 1))) for p in range(len(a_parts))]
    (out,), carried = _carry_call(
        body, carry, name=name, grid=(len(a_parts) * per,),
        in_specs=a_specs + [pl.BlockSpec((s, n), lambda i: (0, 0))], out_specs=[pl.BlockSpec((tm, n), lambda i: (i, 0))],
        out_shape=[jax.ShapeDtypeStruct((len(a_parts) * width, n), BF16)], scratch_shapes=[], args=list(a_parts) + [b])
    return out, carried


def _ple_loss(p, wt_ple, h3, w_pg, x2, target, g, *, name, tm=256):
    s, d = x2.shape
    kp = p.shape[1]

    def body(p_ref, wp_ref, h_ref, wg_ref, x_ref, t_ref, g_ref, loss_ref, dx_ref, dple_ref, dgp_ref, dg_ref):
        i = pl.program_id(0)
        ple = _dot(p_ref[...].astype(BF16), wp_ref[...], NT)
        sg = _sigmoid(_dot(h_ref[...], wg_ref[...], NN))
        xv = x_ref[...] + ple * sg
        r = lax.rsqrt(jnp.mean(xv * xv, axis=-1, keepdims=True) + EPS)
        xhat = xv * r
        diff = xhat * g_ref[...] - t_ref[...]
        lp = jnp.zeros((1, 128), F32) + (0.5 / d) * jnp.sum(diff * diff)
        dy = diff * (1.0 / d)
        dxhat = dy * g_ref[...]
        dx = r * (dxhat - xhat * jnp.mean(dxhat * xhat, axis=-1, keepdims=True))
        dx_ref[...] = dx
        dple_ref[...] = (dx * sg).astype(BF16)
        dgp_ref[...] = (dx * ple * (sg * (1.0 - sg))).astype(BF16)
        dgp = jnp.sum(dy * xhat, axis=0, keepdims=True)

        @pl.when(i == 0)
        def _():
            dg_ref[...] = dgp
            loss_ref[...] = lp

        @pl.when(i > 0)
        def _():
            dg_ref[...] += dgp
            loss_ref[...] += lp

    rows = _row_spec(tm, d)
    return pl.pallas_call(
        body, name=name, grid=(s // tm,),
        in_specs=[_row_spec(tm, kp), _full_spec((d, kp)), rows, _full_spec((d, d)), rows, rows, _full_spec((1, d))],
        out_specs=[_full_spec((1, 128)), rows, rows, rows, _full_spec((1, d))],
        out_shape=[jax.ShapeDtypeStruct((1, 128), F32), jax.ShapeDtypeStruct((s, d), F32),
                   jax.ShapeDtypeStruct((s, d), BF16), jax.ShapeDtypeStruct((s, d), BF16),
                   jax.ShapeDtypeStruct((1, d), F32)],
        compiler_params=_cparams(("arbitrary",)),
    )(p, wt_ple, h3, w_pg, x2, target, g)


def _branches_merge(a, b, w_a, w_b, zuvg, *, name, tm=512):
    s, d = a.shape

    def body(a_ref, b_ref, wa_ref, wb_ref, ga_ref, gb_ref, ya_ref, yb_ref, o_ref):
        ya = _dot(a_ref[...], wa_ref[...], NN)
        yb = _dot(b_ref[...], wb_ref[...], NN)
        ya_ref[...] = ya
        yb_ref[...] = yb
        o_ref[...] = (_sigmoid(ga_ref[...]) * ya + _sigmoid(gb_ref[...]) * yb).astype(BF16)

    rows = _row_spec(tm, d)
    return pl.pallas_call(
        body, name=name, grid=(s // tm,),
        in_specs=[rows, rows, _full_spec((d, d)), _full_spec((d, d)), _row_spec(tm, d, 2), _row_spec(tm, d, 3)],
        out_specs=[rows, rows, rows],
        out_shape=[jax.ShapeDtypeStruct((s, d), F32), jax.ShapeDtypeStruct((s, d), F32), jax.ShapeDtypeStruct((s, d), BF16)],
        compiler_params=_cparams(("parallel",)),
    )(a, b, w_a, w_b, zuvg, zuvg)


def _merge_bwd(dx1b, w_out, ya, yb, zuvg, *, name, tm=512):
    s, d = ya.shape

    def body(dx_ref, w_ref, ya_ref, yb_ref, ga_ref, gb_ref, dya_ref, dyb_ref, dga_ref, dgb_ref):
        dmv = _dot(dx_ref[...], w_ref[...], NT)
        sa = _sigmoid(ga_ref[...])
        sb = _sigmoid(gb_ref[...])
        dya_ref[...] = (dmv * sa).astype(BF16)
        dyb_ref[...] = (dmv * sb).astype(BF16)
        dga_ref[...] = (dmv * ya_ref[...] * (sa * (1.0 - sa))).astype(BF16)
        dgb_ref[...] = (dmv * yb_ref[...] * (sb * (1.0 - sb))).astype(BF16)

    rows = _row_spec(tm, d)
    o = jax.ShapeDtypeStruct((s, d), BF16)
    return pl.pallas_call(
        body, name=name, grid=(s // tm,),
        in_specs=[rows, _full_spec((d, d)), rows, rows, _row_spec(tm, d, 2), _row_spec(tm, d, 3)],
        out_specs=[rows] * 4, out_shape=[o, o, o, o], compiler_params=_cparams(("parallel",)),
    )(dx1b, w_out, ya, yb, zuvg, zuvg)


def _masked_ws(ws_ref, g):
    row = lax.broadcasted_iota(I32, (GBLOCK, GBLOCK), 0)
    col = lax.broadcasted_iota(I32, (GBLOCK, GBLOCK), 1)
    keep = (col // CHUNK) <= (row // CHUNK)
    return jnp.where(keep, ws_ref[g], 0.0), keep


def _layernorm_parts(zv):
    mu = jnp.mean(zv, axis=-1, keepdims=True)
    xc = zv - mu
    rs = lax.rsqrt(jnp.mean(xc * xc, axis=-1, keepdims=True) + EPS)
    return xc * rs, rs


def _gmlp_fwd(zuvg, ln_g, ln_b, w_s, bs_t, *, name):
    s, w = zuvg.shape[0], GROUPS * GDIM

    def body(zu_ref, zv_ref, lng_ref, lnb_ref, ws_ref, bs_ref, a_ref):
        zu = _gelu(zu_ref[...])
        zv = _gelu(zv_ref[...])
        xhat, _ = _layernorm_parts(zv)
        vln = (xhat * lng_ref[...] + lnb_ref[...]).astype(BF16)
        for g in range(GROUPS):
            wm, _ = _masked_ws(ws_ref, g)
            mixed = _dot(wm.astype(BF16), vln[:, g * GDIM:(g + 1) * GDIM], NN) + bs_ref[:, g:g + 1]
            a_ref[:, g * GDIM:(g + 1) * GDIM] = (zu[:, g * GDIM:(g + 1) * GDIM] * mixed).astype(BF16)

    return pl.pallas_call(
        body, name=name, grid=(s // GBLOCK,),
        in_specs=[_row_spec(GBLOCK, w, 0), _row_spec(GBLOCK, w, 1), _full_spec((1, w)), _full_spec((1, w)),
                  _full_spec((GROUPS, GBLOCK, GBLOCK)), _full_spec((GBLOCK, 128))],
        out_specs=_row_spec(GBLOCK, w),
        out_shape=jax.ShapeDtypeStruct((s, w), BF16), compiler_params=_cparams(("parallel",)),
    )(zuvg, zuvg, ln_g, ln_b, w_s, bs_t)


def _gmlp_bwd(da, zuvg, ln_g, ln_b, w_s, bs_t, carry=None, *, name):
    s, w = zuvg.shape[0], GROUPS * GDIM

    def body(da_ref, zu_ref, zv_ref, lng_ref, lnb_ref, ws_ref, bs_ref,
             dzu_ref, dzv_ref, dws_ref, dbs_ref, dlng_ref, dlnb_ref, dvln_ref):
        i = pl.program_id(0)
        zu, dzu_g = _gelu_and_grad(zu_ref[...])
        zv, dzv_g = _gelu_and_grad(zv_ref[...])
        xhat, rs = _layernorm_parts(zv)
        vln = (xhat * lng_ref[...] + lnb_ref[...]).astype(BF16)
        dav = da_ref[...].astype(F32)
        lane = lax.broadcasted_iota(I32, (GBLOCK, 128), 1)
        dbs = jnp.zeros((GBLOCK, 128), F32)

        @pl.when(i == 0)
        def _():
            dws_ref[...] = jnp.zeros_like(dws_ref)

        for g in range(GROUPS):
            sl = slice(g * GDIM, (g + 1) * GDIM)
            wm, keep = _masked_ws(ws_ref, g)
            wmb = wm.astype(BF16)
            vg = vln[:, sl]
            mixed = _dot(wmb, vg, NN) + bs_ref[:, g:g + 1]
            dag = dav[:, sl]
            dzu_ref[:, sl] = (dag * mixed * dzu_g[:, sl]).astype(BF16)
            dmix = dag * zu[:, sl]
            dmb = dmix.astype(BF16)
            dws_ref[g] += jnp.where(keep, _dot(dmb, vg, NT), 0.0)
            dbs = jnp.where(lane == g, jnp.sum(dmix, axis=1, keepdims=True), dbs)
            dvln_ref[:, sl] = _dot(wmb, dmb, TN)
        dvln = dvln_ref[...]
        dxhat = dvln * lng_ref[...]
        dzv = rs * (dxhat - jnp.mean(dxhat, axis=-1, keepdims=True)
                    - xhat * jnp.mean(dxhat * xhat, axis=-1, keepdims=True))
        dzv_ref[...] = (dzv * dzv_g).astype(BF16)
        dlng = jnp.sum(dvln * xhat, axis=0, keepdims=True)
        dlnb = jnp.sum(dvln, axis=0, keepdims=True)

        @pl.when(i == 0)
        def _():
            dbs_ref[...] = dbs
            dlng_ref[...] = dlng
            dlnb_ref[...] = dlnb

        @pl.when(i > 0)
        def _():
            dbs_ref[...] += dbs
            dlng_ref[...] += dlng
            dlnb_ref[...] += dlnb

    return _carry_call(
        body, carry, name=name, grid=(s // GBLOCK,),
        in_specs=[_row_spec(GBLOCK, w), _row_spec(GBLOCK, w, 0), _row_spec(GBLOCK, w, 1), _full_spec((1, w)),
                  _full_spec((1, w)), _full_spec((GROUPS, GBLOCK, GBLOCK)), _full_spec((GBLOCK, 128))],
        out_specs=[_row_spec(GBLOCK, w), _row_spec(GBLOCK, w), _full_spec((GROUPS, GBLOCK, GBLOCK)),
                   _full_spec((GBLOCK, 128)), _full_spec((1, w)), _full_spec((1, w))],
        out_shape=[jax.ShapeDtypeStruct((s, w), BF16), jax.ShapeDtypeStruct((s, w), BF16),
                   jax.ShapeDtypeStruct((GROUPS, GBLOCK, GBLOCK), F32), jax.ShapeDtypeStruct((GBLOCK, 128), F32),
                   jax.ShapeDtypeStruct((1, w), F32), jax.ShapeDtypeStruct((1, w), F32)],
        scratch_shapes=[pltpu.VMEM((GBLOCK, w), F32)], args=[da, zuvg, zuvg, ln_g, ln_b, w_s, bs_t])


def _shift_down(u, k):
    row = lax.broadcasted_iota(I32, u.shape, 0)
    return jnp.where(row >= k, pltpu.roll(u, k, 0), 0.0)


def _shift_up(u, k):
    s = u.shape[0]
    row = lax.broadcasted_iota(I32, u.shape, 0)
    return jnp.where(row < s - k, pltpu.roll(u, s - k, 0), 0.0)


def _conv(u, w_ref, b_ref):
    return b_ref[...] + w_ref[0:1, :] * _shift_down(u, 2) + w_ref[1:2, :] * _shift_down(u, 1) + w_ref[2:3, :] * u


def _conv_specs(s, f, tc):
    nc = f // tc
    half = lambda rows: [pl.BlockSpec((rows, tc), lambda j: (0, j)), pl.BlockSpec((rows, tc), lambda j: (0, nc + j))]
    return half(s), half(3), half(1)


def _up_convglu(h2, wt_up, conv_w, conv_b, *, name, tc=256):
    s, d = h2.shape
    f = wt_up.shape[0] // 2
    nc = f // tc
    _, w_specs, b_specs = _conv_specs(s, f, tc)

    def body(h_ref, ta_ref, tg_ref, wa_ref, wg_ref, ba_ref, bg_ref, ua_ref, ug_ref, o_ref):
        ua = _dot(h_ref[...], ta_ref[...], NT)
        ua_ref[...] = ua
        ga = _gelu(_conv(ua, wa_ref, ba_ref))
        ug = _dot(h_ref[...], tg_ref[...], NT)
        ug_ref[...] = ug
        o_ref[...] = (ga * _conv(ug, wg_ref, bg_ref)).astype(BF16)

    col = pl.BlockSpec((s, tc), lambda j: (0, j))
    return pl.pallas_call(
        body, name=name, grid=(nc,),
        in_specs=[_full_spec((s, d)), pl.BlockSpec((tc, d), lambda j: (j, 0)), pl.BlockSpec((tc, d), lambda j: (nc + j, 0))]
        + w_specs + b_specs,
        out_specs=[col, col, col],
        out_shape=[jax.ShapeDtypeStruct((s, f), F32), jax.ShapeDtypeStruct((s, f), F32), jax.ShapeDtypeStruct((s, f), BF16)],
        compiler_params=_cparams(("parallel",)),
    )(h2, wt_up, wt_up, conv_w, conv_w, conv_b, conv_b)


def _convglu_bwd(dact, up_a, up_g, conv_w, conv_b, *, name, tc=256):
    s, f = up_a.shape
    _, w_specs, b_specs = _conv_specs(s, f, tc)
    up_specs = [pl.BlockSpec((s, tc), lambda j: (0, j))] * 2

    def half(dc, taps, w_ref, du_ref, dw_ref, db_ref):
        db_ref[...] = jnp.sum(dc, axis=0, keepdims=True)
        for k in range(3):
            dw_ref[k:k + 1, :] = jnp.sum(dc * taps[k], axis=0, keepdims=True)
        du = w_ref[2:3, :] * dc + w_ref[1:2, :] * _shift_up(dc, 1) + w_ref[0:1, :] * _shift_up(dc, 2)
        du_ref[...] = du.astype(BF16)

    def body(d_ref, ua_ref, ug_ref, wa_ref, wg_ref, ba_ref, bg_ref,
             dua_ref, dug_ref, dwa_ref, dwg_ref, dba_ref, dbg_ref):
        taps_a = (_shift_down(ua_ref[...], 2), _shift_down(ua_ref[...], 1), ua_ref[...])
        taps_g = (_shift_down(ug_ref[...], 2), _shift_down(ug_ref[...], 1), ug_ref[...])
        conv = lambda taps, w_ref, b_ref: b_ref[...] + w_ref[0:1, :] * taps[0] + w_ref[1:2, :] * taps[1] + w_ref[2:3, :] * taps[2]
        ca = conv(taps_a, wa_ref, ba_ref)
        cg = conv(taps_g, wg_ref, bg_ref)
        ga, dga = _gelu_and_grad(ca)
        dv = d_ref[...].astype(F32)
        half(dv * cg * dga, taps_a, wa_ref, dua_ref, dwa_ref, dba_ref)
        half(dv * ga, taps_g, wg_ref, dug_ref, dwg_ref, dbg_ref)

    col, w3, b1 = up_specs[0], w_specs[0], b_specs[0]
    return pl.pallas_call(
        body, name=name, grid=(f // tc,),
        in_specs=[col] + up_specs + w_specs + b_specs, out_specs=[col, col, w3, w3, b1, b1],
        out_shape=[jax.ShapeDtypeStruct((s, f), BF16), jax.ShapeDtypeStruct((s, f), BF16),
                   jax.ShapeDtypeStruct((3, f), F32), jax.ShapeDtypeStruct((3, f), F32),
                   jax.ShapeDtypeStruct((1, f), F32), jax.ShapeDtypeStruct((1, f), F32)],
        compiler_params=_cparams(("parallel",)),
    )(dact, up_a, up_g, conv_w, conv_w, conv_b, conv_b)


def _tri_dot(tri, x):
    b0 = x.astype(BF16)
    r1 = x - b0.astype(F32)
    b1 = r1.astype(BF16)
    b2 = (r1 - b1.astype(F32)).astype(BF16)
    return _dot(tri, b0, NN) + _dot(tri, b1, NN) + _dot(tri, b2, NN)


def _log_sigmoid(x):
    return jnp.minimum(x, 0.0) - jnp.log(1.0 + jnp.exp(-jnp.abs(x)))


def _expand_heads(col16, rows):
    src = lax.broadcasted_iota(I32, (128, HEADS * HEAD_DIM), 0)
    dst = lax.broadcasted_iota(I32, (128, HEADS * HEAD_DIM), 1) // HEAD_DIM
    spread = (src == dst).astype(BF16)
    p0, p1, p2 = _bf16_pieces(col16)
    return (_dot(p0.astype(BF16), spread, NN) + _dot(p1.astype(BF16), spread, NN)) + _dot(p2.astype(BF16), spread, NN)


def _forget_cumsum(f_logit, b_f, *, name):
    s = f_logit.shape[0]
    nb = s // 128

    def body(f_ref, b_ref, cqe_ref):
        row = lax.broadcasted_iota(I32, (128, 128), 0)
        col = lax.broadcasted_iota(I32, (128, 128), 1)
        tri = (col <= row).astype(BF16)

        def step(n, carry):
            r0 = pl.multiple_of(n * 128, 128)
            lf = _log_sigmoid(f_ref[pl.ds(r0, 128), :] + b_ref[...])
            cum = _tri_dot(tri, lf) + carry
            cqe_ref[pl.ds(r0, 128), :] = _expand_heads(cum, 128)
            return cum[127:128, :]

        lax.fori_loop(0, nb, step, jnp.zeros((1, 128), F32))

    return pl.pallas_call(
        body, name=name, grid=(1,),
        in_specs=[_full_spec((s, 128)), _full_spec((1, 128))],
        out_specs=_full_spec((s, HEADS * HEAD_DIM)),
        out_shape=jax.ShapeDtypeStruct((s, HEADS * HEAD_DIM), F32),
        compiler_params=_cparams(("arbitrary",)),
    )(f_logit, b_f)


def _forget_bwd(dcq16, sum_q16, f_logit, b_f, *, name):
    s = f_logit.shape[0]
    nb = s // 128

    def body(a_ref, k_ref, f_ref, b_ref, df_ref, db_ref):
        row = lax.broadcasted_iota(I32, (128, 128), 0)
        col = lax.broadcasted_iota(I32, (128, 128), 1)
        tri_rev = (col >= row).astype(BF16)

        def step(m, carry):
            suffix, dbsum = carry
            n = nb - 1 - m
            r0 = pl.multiple_of(n * 128, 128)
            dcum = a_ref[pl.ds(r0, 128), :] - k_ref[pl.ds(r0, 128), :]
            dlf = _tri_dot(tri_rev, dcum) + suffix
            df = dlf * _sigmoid(-(f_ref[pl.ds(r0, 128), :] + b_ref[...]))
            df_ref[pl.ds(r0, 128), :] = df.astype(BF16)
            return dlf[0:1, :], dbsum + jnp.sum(df, axis=0, keepdims=True)

        _, dbsum = lax.fori_loop(0, nb, step, (jnp.zeros((1, 128), F32), jnp.zeros((1, 128), F32)))
        db_ref[...] = dbsum

    return pl.pallas_call(
        body, name=name, grid=(1,),
        in_specs=[_full_spec((s, 128))] * 3 + [_full_spec((1, 128))],
        out_specs=[_full_spec((s, 128)), _full_spec((1, 128))],
        out_shape=[jax.ShapeDtypeStruct((s, 128), BF16), jax.ShapeDtypeStruct((1, 128), F32)],
        compiler_params=_cparams(("arbitrary",)),
    )(dcq16, sum_q16, f_logit, b_f)


ATT_T = 256


def _head_lanes(rows):
    return lax.broadcasted_iota(I32, (rows, 128), 1) < HEAD_DIM


def _bf16_pieces(c):
    p0 = c.astype(BF16).astype(F32)
    r = c - p0
    p1 = r.astype(BF16).astype(F32)
    p2 = (r - p1).astype(BF16).astype(F32)
    return p0, p1, p2


def _col_reduce(x, op):
    rows = x.shape[0]
    while rows > 8:
        rows //= 2
        x = op(x[:rows], x[rows:])
    return jnp.max(x, axis=0, keepdims=True) if op is jnp.maximum else jnp.sum(x, axis=0, keepdims=True)


def _attn_prep(qkv, cqe, carry=None, *, name):
    s = qkv.shape[0]
    npair = HEADS // 2

    def body(q_ref, k_ref, v_ref, c_ref, qa_ref, ka_ref, vt_ref):
        rows = 128
        lane = lax.broadcasted_iota(I32, (rows, 128), 1)

        def chunk(n, _):
            r0 = pl.multiple_of(n * rows, rows)
            sl = pl.ds(r0, rows)
            qv = q_ref[sl, :].astype(F32) * ATT_SCALE
            kv = k_ref[sl, :].astype(F32)
            p0, p1, p2 = _bf16_pieces(pltpu.roll(c_ref[sl, :], HEAD_DIM, 1))
            for e in range(2):
                mine = (lane < HEAD_DIM) if e == 0 else (lane >= HEAD_DIM)
                base = HEAD_DIM * (1 - e)
                ones_hi = jnp.where((lane >= base + 3) & (lane < base + 6), 1.0, 0.0)
                ones_lo = jnp.where((lane >= base) & (lane < base + 3), 1.0, 0.0)
                qa = jnp.where(mine, qv, jnp.where(lane == base, p0, jnp.where(lane == base + 1, p1,
                               jnp.where(lane == base + 2, p2, ones_hi))))
                ka = jnp.where(mine, kv, jnp.where(lane == base + 3, -p0, jnp.where(lane == base + 4, -p1,
                               jnp.where(lane == base + 5, -p2, ones_lo))))
                qa_ref[e, sl, :] = qa.astype(BF16)
                ka_ref[e, sl, :] = ka.astype(BF16)
            vt_ref[0, :, sl] = v_ref[sl, :].astype(F32).T.astype(BF16)
            return 0

        lax.fori_loop(0, s // rows, chunk, 0)

    pair = pl.BlockSpec((2, s, 128), lambda hp: (hp, 0, 0))
    return _carry_call(
        body, carry, name=name, grid=(npair,),
        in_specs=[pl.BlockSpec((s, 128), lambda hp: (0, hp)), pl.BlockSpec((s, 128), lambda hp: (0, npair + hp)),
                  pl.BlockSpec((s, 128), lambda hp: (0, 2 * npair + hp)), pl.BlockSpec((s, 128), lambda hp: (0, hp))],
        out_specs=[pair, pair, pl.BlockSpec((1, 128, s), lambda hp: (hp, 0, 0))],
        out_shape=[jax.ShapeDtypeStruct((HEADS, s, 128), BF16), jax.ShapeDtypeStruct((HEADS, s, 128), BF16),
                   jax.ShapeDtypeStruct((npair, 128, s), BF16)],
        scratch_shapes=[], args=[qkv, qkv, qkv, cqe])


def _attn_fwd(qa, ka, vt, carry=None, *, name):
    s = qa.shape[1]
    t = 2 * ATT_T
    nq = s // t
    npair = HEADS // 2

    def body(qa_ref, ka_ref, vt_ref, o_ref, lse_ref):
        i = pl.program_id(1)
        krow = lax.broadcasted_iota(I32, (t, t), 0)
        qcol = lax.broadcasted_iota(I32, (t, t), 1)
        sub = lax.broadcasted_iota(I32, (128, t), 0)
        row8 = lax.broadcasted_iota(I32, (8, t), 0)
        qbs = (qa_ref[0], qa_ref[1])
        tk = t

        def step(j, carry, diag):
            c0 = pl.multiple_of(j * tk, tk)
            vtb = vt_ref[0, :, pl.ds(c0, tk)]
            sts = [_dot(ka_ref[e, pl.ds(c0, tk), :], qbs[e], NT) for e in range(2)]
            if diag:
                sts = [jnp.where(krow <= qcol, st, NEG) for st in sts]
            pts, stats = [], []
            for e in range(2):
                m, l, _ = carry[e]
                m_new = jnp.maximum(m, _col_reduce(sts[e], jnp.maximum))
                alpha = jnp.exp(m - m_new)
                pt = jnp.exp(sts[e] - m_new)
                stats.append((m_new, alpha, alpha * l + _col_reduce(pt, jnp.add)))
                pts.append(pt.astype(BF16))
            pvs = [_dot(vtb, pts[e], NN) for e in range(2)]
            return tuple((stats[e][0], stats[e][2], stats[e][1] * carry[e][2] + pvs[e]) for e in range(2))

        init = (jnp.full((1, t), NEG, F32), jnp.zeros((1, t), F32), jnp.zeros((128, t), F32))
        carry = lax.fori_loop(0, i, functools.partial(step, diag=False), (init, init))
        (m0, l0, acc0), (m1, l1, acc1) = step(i, carry, True)
        o_pair = jnp.where(sub < HEAD_DIM, acc0 / l0, acc1 / l1)
        o_ref[...] = o_pair.T.astype(BF16)
        lse_ref[0] = jnp.where(row8 == 0, m0 + jnp.log(l0), jnp.where(row8 == 1, m1 + jnp.log(l1), 0.0))

    return _carry_call(
        body, carry, name=name, grid=(npair, nq),
        in_specs=[pl.BlockSpec((2, t, 128), lambda hp, i: (hp, i, 0)), pl.BlockSpec((2, s, 128), lambda hp, i: (hp, 0, 0)),
                  pl.BlockSpec((1, 128, s), lambda hp, i: (hp, 0, 0))],
        out_specs=[pl.BlockSpec((t, 128), lambda hp, i: (i, hp)), pl.BlockSpec((1, 8, t), lambda hp, i: (hp, 0, i))],
        out_shape=[jax.ShapeDtypeStruct((s, HEADS * HEAD_DIM), BF16), jax.ShapeDtypeStruct((npair, 8, s), F32)],
        scratch_shapes=[], args=[qa, ka, vt])


def _attn_delta(do, o, carry=None, *, name):
    s = do.shape[0]

    def body(do_ref, o_ref, d_ref):
        prod = do_ref[...].astype(F32) * o_ref[...].astype(F32)
        row = lax.broadcasted_iota(I32, (8, 128), 0)
        lane = lax.broadcasted_iota(I32, (8, 128), 1)
        sel = ((row == 0) & (lane < HEAD_DIM) | (row == 1) & (lane >= HEAD_DIM)).astype(BF16)
        p0, p1, p2 = _bf16_pieces(prod)
        d_ref[0] = (_dot(sel, p0.astype(BF16), NT) + _dot(sel, p1.astype(BF16), NT)) + _dot(sel, p2.astype(BF16), NT)

    pair = pl.BlockSpec((s, 128), lambda hp: (0, hp))
    (delta3,), carried = _carry_call(
        body, carry, name=name, grid=(HEADS // 2,), in_specs=[pair, pair],
        out_specs=[pl.BlockSpec((1, 8, s), lambda hp: (hp, 0, 0))],
        out_shape=[jax.ShapeDtypeStruct((HEADS // 2, 8, s), F32)], scratch_shapes=[], args=[do, o])
    return delta3, carried


def _attn_bwd(qa, ka, qkv, do, lse3, delta3, carry=None, *, name):
    s = qa.shape[1]
    t = 2 * ATT_T
    nb = s // t
    npair = HEADS // 2

    def body(qa_ref, ka_ref, v_ref, do_ref, lse_ref, delta_ref, dq_ref, dk_ref, dv_ref, aux_ref, dcq_ref, dqt):
        hp = pl.program_id(0)
        first = _head_lanes(t)
        lane = lax.broadcasted_iota(I32, (t, 128), 1)
        dqt[...] = jnp.zeros_like(dqt)

        @pl.when(hp == 0)
        def _():
            aux_ref[...] = jnp.zeros_like(aux_ref)

        krow = lax.broadcasted_iota(I32, (t, t), 0)
        qcol = lax.broadcasted_iota(I32, (t, t), 1)

        def key_block(j, _):
            c0 = pl.multiple_of(j * t, t)
            vb = v_ref[pl.ds(c0, t), :]
            kbs = (ka_ref[0, pl.ds(c0, t), :], ka_ref[1, pl.ds(c0, t), :])
            kbts = tuple(kb.astype(F32).T.astype(BF16) for kb in kbs)
            vhs = (jnp.where(first, vb, jnp.zeros_like(vb)), jnp.where(first, jnp.zeros_like(vb), vb))

            def query_block(i, carry, diag):
                r0 = pl.multiple_of(i * t, t)
                dob = do_ref[pl.ds(r0, t), :]
                sts = [_dot(kbs[e], qa_ref[e, pl.ds(r0, t), :], NT) for e in range(2)]
                dpts = [_dot(vhs[e], dob, NT) for e in range(2)]
                ptbs, dsbs = [], []
                for e in range(2):
                    st = jnp.where(krow <= qcol, sts[e], NEG) if diag else sts[e]
                    pt = jnp.exp(st - lse_ref[0, e:e + 1, pl.ds(r0, t)])
                    dsbs.append((pt * (dpts[e] - delta_ref[0, e:e + 1, pl.ds(r0, t)])).astype(BF16))
                    ptbs.append(pt.astype(BF16))
                out = []
                for e in range(2):
                    dk_a, dv_a = carry[e]
                    dv_a = dv_a + _dot(ptbs[e], dob, NN)
                    dk_a = dk_a + _dot(dsbs[e], qa_ref[e, pl.ds(r0, t), :], NN)
                    dqt[e, :, pl.ds(r0, t)] += _dot(kbts[e], dsbs[e], NN)
                    out.append((dk_a, dv_a))
                return tuple(out)

            zero = jnp.zeros((t, 128), F32)
            carry = query_block(j, ((zero, zero), (zero, zero)), True)
            (dk0, dv0), (dk1, dv1) = lax.fori_loop(j + 1, nb, functools.partial(query_block, diag=False), carry)
            dk_ref[pl.ds(c0, t), :] = jnp.where(first, dk0, dk1).astype(BF16)
            dv_ref[pl.ds(c0, t), :] = jnp.where(first, dv0, dv1).astype(BF16)
            sum_q = jnp.where(lane == 2 * hp, dk0[:, HEAD_DIM + 3:HEAD_DIM + 4],
                              jnp.where(lane == 2 * hp + 1, dk1[:, 3:4], aux_ref[pl.ds(c0, t), :]))
            aux_ref[pl.ds(c0, t), :] = sum_q
            return 0

        lax.fori_loop(0, nb, key_block, 0)
        sub = lax.broadcasted_iota(I32, (128, s), 0)
        row8 = lax.broadcasted_iota(I32, (8, s), 0)
        dq_ref[...] = (jnp.where(sub < HEAD_DIM, dqt[0], dqt[1]) * ATT_SCALE).T.astype(BF16)
        dcq_ref[0] = jnp.where(row8 == 0, dqt[0, HEAD_DIM:HEAD_DIM + 1, :], jnp.where(row8 == 1, dqt[1, 0:1, :], 0.0))

    def pair_cols(off):
        return pl.BlockSpec((s, 128), lambda hp: (0, off + hp))

    heads = pl.BlockSpec((2, s, 128), lambda hp: (hp, 0, 0))
    rows = pl.BlockSpec((1, 8, s), lambda hp: (hp, 0, 0))
    wide = jax.ShapeDtypeStruct((s, HEADS * HEAD_DIM), BF16)
    return _carry_call(
        body, carry, name=name, grid=(npair,),
        in_specs=[heads, heads, pair_cols(2 * npair), pair_cols(0), rows, rows],
        out_specs=[pair_cols(0), pair_cols(0), pair_cols(0), pl.BlockSpec((s, 128), lambda hp: (0, 0)), rows],
        out_shape=[wide, wide, wide, jax.ShapeDtypeStruct((s, 128), F32), jax.ShapeDtypeStruct((npair, 8, s), F32)],
        scratch_shapes=[pltpu.VMEM((2, 128, s), F32)], args=[qa, ka, qkv, do, lse3, delta3])


def _adam_math(w, g, m, v):
    m = ADAM_B1 * m + (1.0 - ADAM_B1) * g
    v = ADAM_B2 * v + (1.0 - ADAM_B2) * (g * g)
    m_hat = m / (1.0 - ADAM_B1 ** ADAM_STEP)
    v_hat = v / (1.0 - ADAM_B2 ** ADAM_STEP)
    delta = -ADAM_LR * (m_hat / (jnp.sqrt(v_hat) + ADAM_EPS) + ADAM_WD * w)
    return delta, m, v


def _sum_pairs(keep, recv, pos, *, name):
    _, r, c = recv.shape
    tr = _row_tile(r, 1024)

    def body(pos_ref, a_ref, b_ref, o32_ref, o16_ref):
        tot = a_ref[...].astype(F32) + b_ref[...].astype(F32)
        o16_ref[...] = tot.astype(BF16)

        @pl.when(pl.program_id(1) == 2 * pos_ref[0] + pos_ref[1])
        def _():
            o32_ref[...] = tot

    out = pl.BlockSpec((1, tr, c), lambda i, q, pos: (q, i, 0))
    grid_spec = pltpu.PrefetchScalarGridSpec(
        num_scalar_prefetch=1, grid=(r // tr, 4),
        in_specs=[pl.BlockSpec((1, tr, c), lambda i, q, pos: (2 * q + pos[2], i, 0)), out],
        out_specs=[pl.BlockSpec((1, tr, c), lambda i, q, pos: (0, i, 0)), out])
    return pl.pallas_call(
        body, name=name, grid_spec=grid_spec,
        out_shape=[jax.ShapeDtypeStruct((1, r, c), F32), jax.ShapeDtypeStruct((4, r, c), BF16)],
        compiler_params=_cparams(("arbitrary", "arbitrary")),
    )(pos, keep, recv)


def _adam_sharded(psum, recv, w, m, v, pos, *, name):
    r, c = w.shape
    rg = psum.shape[1]

    def body(pos_ref, p_ref, r_ref, w_ref, m_ref, v_ref, g_ref, d_ref, mo_ref, vo_ref):
        part = lambda ref, q: ref[q] if rg == r else ref[q, :r, :]
        g = part(p_ref, 0) + part(r_ref, 0).astype(F32) + part(r_ref, 1).astype(F32) + part(r_ref, 2).astype(F32)
        delta, mn, vn = _adam_math(w_ref[...], g, m_ref[...], v_ref[...])
        g_ref[...] = g
        d_ref[...] = delta
        mo_ref[...] = mn
        vo_ref[...] = vn

    if rg == r:
        tr = _row_tile(r, 512)
        grid = (r // tr,)
        row = pl.BlockSpec((tr, c), lambda i, pos: (i, 0))
        sums = lambda n: pl.BlockSpec((n, tr, c), lambda i, pos: (0, i, 0))
    else:
        tc = 256
        grid = (c // tc,)
        row = pl.BlockSpec((r, tc), lambda i, pos: (0, i))
        sums = lambda n: pl.BlockSpec((n, rg, tc), lambda i, pos: (0, 0, i))
    grid_spec = pltpu.PrefetchScalarGridSpec(
        num_scalar_prefetch=1, grid=grid, in_specs=[sums(1), sums(3), row, row, row], out_specs=[row, row, row, row])
    o = jax.ShapeDtypeStruct((r, c), F32)
    return pl.pallas_call(
        body, name=name, grid_spec=grid_spec, out_shape=[o, o, o, o],
        compiler_params=_cparams(("parallel",)),
    )(pos, psum, recv, w, m, v)


def _adam_replicated(chip_sums, last, w, m, v, *, name):
    r = w.shape[0]

    def body(s_ref, l_ref, w_ref, m_ref, v_ref, g_ref, d_ref, mo_ref, vo_ref):
        g = (((s_ref[0] + s_ref[1]) + s_ref[2]) + s_ref[3]) + l_ref[...]
        delta, mn, vn = _adam_math(w_ref[...], g, m_ref[...], v_ref[...])
        g_ref[...] = g
        d_ref[...] = delta
        mo_ref[...] = mn
        vo_ref[...] = vn

    o = jax.ShapeDtypeStruct((r, 1024), F32)
    full = _full_spec((r, 1024))
    return pl.pallas_call(
        body, name=name, grid=(1,),
        in_specs=[_full_spec((4, r, 1024)), full, full, full, full], out_specs=[full] * 4, out_shape=[o] * 4,
        compiler_params=_cparams(("arbitrary",)),
    )(chip_sums, last, w, m, v)


ASM_OUT = 256
ASM_SRC = 304


def _w_in_row(r):
    return r if r < 2048 else (r + O_G - 2048 if r < 4096 else r - 2048)


def _assemble_wt_main(g, *, name):
    table = []
    for blk in range(MAIN_COLS // ASM_OUT):
        j, l0 = divmod(_w_in_row(blk * ASM_OUT), IN_SHARD)
        sb = l0 // ASM_SRC
        n_a = min(ASM_OUT, min(IN_SHARD, (sb + 1) * ASM_SRC) - l0)
        if n_a == ASM_OUT:
            nxt = (j, sb)
        elif l0 + n_a == IN_SHARD:
            nxt = (j + 1, 0)
        else:
            nxt = (j, sb + 1)
        table.append((j, sb, l0 - sb * ASM_SRC, n_a) + nxt)

    def body(tab_ref, a_ref, b_ref, o_ref):
        blk = pl.program_id(0)
        off, n_a = tab_ref[blk, 2], tab_ref[blk, 3]
        r = lax.broadcasted_iota(I32, (ASM_OUT, ASM_SRC), 0)
        k = lax.broadcasted_iota(I32, (ASM_OUT, ASM_SRC), 1)
        sel_a = ((k == r + off) & (r < n_a)).astype(BF16)
        sel_b = ((k == r - n_a) & (r >= n_a)).astype(BF16)
        o_ref[...] = (_dot(sel_a, a_ref[0], NN) + _dot(sel_b, b_ref[0], NN)).astype(BF16)

    src = lambda c: pl.BlockSpec((1, ASM_SRC, D_MODEL), lambda blk, tab: (tab[blk, c], tab[blk, c + 1], 0))
    grid_spec = pltpu.PrefetchScalarGridSpec(
        num_scalar_prefetch=1, grid=(len(table),), in_specs=[src(0), src(4)],
        out_specs=pl.BlockSpec((ASM_OUT, D_MODEL), lambda blk, tab: (blk, 0)))
    return pl.pallas_call(
        body, name=name, grid_spec=grid_spec, out_shape=jax.ShapeDtypeStruct((MAIN_COLS, D_MODEL), BF16),
        compiler_params=_cparams(("parallel",)),
    )(jnp.asarray(table, I32), g, g)


def _pair_sum_small(mine, theirs, *, name):
    def body(a_ref, b_ref, o_ref):
        o_ref[...] = a_ref[...] + b_ref[...]

    full = _full_spec(mine.shape)
    return pl.pallas_call(
        body, name=name, grid=(1,), in_specs=[full, full], out_specs=full,
        out_shape=jax.ShapeDtypeStruct(mine.shape, F32), compiler_params=_cparams(("arbitrary",)),
    )(mine, theirs)


ANY = pl.BlockSpec(memory_space=pl.ANY)
OTHER_CHIPS = ((1, 0), (0, 1), (1, 1))


class _Carry:
    def __init__(self, inputs, out_shapes, scratch, start, wait, aliases=None, middle=None):
        self.inputs, self.out_shapes, self.scratch = list(inputs), list(out_shapes), list(scratch)
        self.start, self.wait, self.aliases, self.middle = start, wait, dict(aliases or {}), middle


def _carry_join(*carries):
    n_in = [len(c.inputs) for c in carries]
    n_out = [len(c.out_shapes) for c in carries]
    n_scr = [len(c.scratch) for c in carries]

    def split(refs, counts):
        out, k = [], 0
        for n in counts:
            out.append(refs[k:k + n])
            k += n
        return out

    def start(ins, outs, scr):
        for c, i, o, s in zip(carries, split(ins, n_in), split(outs, n_out), split(scr, n_scr)):
            c.start(i, o, s)

    def wait(ins, outs, scr):
        for c, i, o, s in zip(carries, split(ins, n_in), split(outs, n_out), split(scr, n_scr)):
            c.wait(i, o, s)

    def middle(ins, outs, scr):
        for c, i, o, s in zip(carries, split(ins, n_in), split(outs, n_out), split(scr, n_scr)):
            if c.middle is not None:
                c.middle(i, o, s)

    aliases = {}
    for k, c in enumerate(carries):
        aliases.update({sum(n_in[:k]) + i: sum(n_out[:k]) + o for i, o in c.aliases.items()})
    joined = _Carry(sum((c.inputs for c in carries), []), sum((c.out_shapes for c in carries), []),
                    sum((c.scratch for c in carries), []), start, wait, aliases,
                    middle if any(c.middle is not None for c in carries) else None)
    joined.counts = n_out
    joined.split = lambda results: split(results, n_out)
    return joined


def _carried(body, carry, n_in, n_out, grid):
    if carry is None:
        return body
    ci, co, cs = len(carry.inputs), len(carry.out_shapes), len(carry.scratch)

    def wrapped(*refs):
        ins, cins = refs[:n_in], refs[n_in:n_in + ci]
        outs, couts = refs[n_in + ci:n_in + ci + n_out], refs[n_in + ci + n_out:n_in + ci + n_out + co]
        rest = refs[n_in + ci + n_out + co:]
        scratch, cscr = rest[:len(rest) - cs], rest[len(rest) - cs:]
        first, last, step, steps = None, None, 0, 1
        for axis, size in enumerate(grid):
            f, l = pl.program_id(axis) == 0, pl.program_id(axis) == size - 1
            first = f if first is None else first & f
            last = l if last is None else last & l
            step, steps = step * size + pl.program_id(axis), steps * size

        @pl.when(first)
        def _():
            carry.start(cins, couts, cscr)

        if carry.middle is not None:
            @pl.when(step == steps // 2)
            def _():
                carry.middle(cins, couts, cscr)

        body(*ins, *outs, *scratch)

        @pl.when(last)
        def _():
            carry.wait(cins, couts, cscr)

    return wrapped


def _carry_call(body, carry, *, name, grid, in_specs, out_specs, out_shape, scratch_shapes, args, vmem=True,
                own_aliases=None):
    n_in, n_out = len(in_specs), len(out_specs)
    extra_in = [ANY] * len(carry.inputs) if carry else []
    extra_out = [ANY] * len(carry.out_shapes) if carry else []
    aliases = dict(own_aliases or {})
    if carry:
        aliases.update({n_in + i: n_out + o for i, o in carry.aliases.items()})
    out = pl.pallas_call(
        _carried(body, carry, n_in, n_out, grid), name=name, grid=grid,
        in_specs=list(in_specs) + extra_in, out_specs=list(out_specs) + extra_out,
        out_shape=list(out_shape) + (carry.out_shapes if carry else []),
        scratch_shapes=list(scratch_shapes) + (carry.scratch if carry else []),
        input_output_aliases=aliases,
        compiler_params=_cparams(("arbitrary",) * len(grid)) if vmem else None,
    )(*args, *(carry.inputs if carry else []))
    return list(out[:n_out]), list(out[n_out:])


def _run_carry(carry, *, name):
    return _carry_call(lambda: None, carry, name=name, grid=(1,), in_specs=[], out_specs=[], out_shape=[],
                       scratch_shapes=[], args=[], vmem=False)[1]


def _sems(n):
    return [pltpu.SemaphoreType.DMA((n,)), pltpu.SemaphoreType.DMA((n,))]


def _carry_gather1(shards):
    n = len(shards)
    per = 7

    def plan(x_refs, out_refs, scr):
        send_sems, recv_sems, local_sems = scr
        x, y, c = lax.axis_index("x"), lax.axis_index("y"), lax.axis_index("c")
        me, sibling = (x, y, c), (x, y, 1 - c)
        near_x, near_y, across = (1 - x, y, c), (x, 1 - y, c), (1 - x, 1 - y, c)

        def rows(ref, t, half):
            r = shards[t].shape[0]
            h = r if r < 32 else -(-(r // 2) // 16) * 16
            if half is None or h == r:
                return ref
            return ref.at[pl.ds(0, h)] if half == 0 else ref.at[pl.ds(h, r - h)]

        def copy(t, k, block, half, to, from_input=False):
            px, py, pc = block
            slab = rows(out_refs[t].at[4 * px + 2 * py + pc], t, half)
            return pltpu.make_async_remote_copy(
                src_ref=rows(x_refs[t], t, half) if from_input else slab, dst_ref=slab,
                send_sem=send_sems.at[per * t + k], recv_sem=recv_sems.at[per * t + k], device_id=to,
                device_id_type=MESH)

        two = [shards[t].shape[0] >= 32 for t in range(n)]
        local = lambda t: pltpu.make_async_copy(x_refs[t], out_refs[t].at[4 * x + 2 * y + c], local_sems.at[t])
        first = lambda t: ([(0, me, None, sibling), (1, me, 0, near_x)]
                           + ([(2, me, 1, near_y), (3, me, 1, near_x)] if two[t] else []) + [(4, me, 0, near_y)])
        passed = lambda t: [(5, near_x, 0, near_y)] + ([(6, near_y, 1, near_x)] if two[t] else [])
        early = lambda t: [(1, near_x, 0, me)] + ([(2, near_y, 1, me)] if two[t] else [])
        late = lambda t: ([(0, sibling, None, me), (4, near_y, 0, me), (5, across, 0, me)]
                          + ([(3, near_x, 1, me), (6, across, 1, me)] if two[t] else []))
        return copy, local, first, passed, early, late

    def start(x_refs, out_refs, scr):
        copy, local, first, _, _, _ = plan(x_refs, out_refs, scr)
        for urgent in (True, False):
            for t in range(n):
                if not urgent:
                    local(t).start()
                for k, block, half, to in first(t):
                    if (k in (1, 2)) == urgent:
                        copy(t, k, block, half, to, from_input=True).start()

    def middle(x_refs, out_refs, scr):
        copy, _, _, passed, early, _ = plan(x_refs, out_refs, scr)
        for t in range(n):
            for (k, block, half, to), fwd in zip(early(t), passed(t)):
                copy(t, k, block, half, to).wait_recv()
                copy(t, *fwd).start()

    def wait(x_refs, out_refs, scr):
        copy, local, first, passed, _, late = plan(x_refs, out_refs, scr)
        for t in range(n):
            for k, block, half, to in late(t):
                copy(t, k, block, half, to).wait_recv()
        for t in range(n):
            for k, block, half, to in first(t):
                copy(t, k, block, half, to, from_input=True).wait_send()
            for k, block, half, to in passed(t):
                copy(t, k, block, half, to).wait_send()
            local(t).wait()

    return _Carry(shards, [jax.ShapeDtypeStruct((N_DEV,) + a.shape, a.dtype) for a in shards],
                  _sems(per * n) + [pltpu.SemaphoreType.DMA((n,))], start, wait, middle=middle)


def _carry_gather2(gathered):
    n = len(gathered)

    def copies(in_refs, g_refs, scr, with_arrivals):
        send_sems, recv_sems = scr
        x, y, c = lax.axis_index("x"), lax.axis_index("y"), lax.axis_index("c")
        sends, arrivals = [], []
        for t in range(n):
            for j, (fx, fy) in enumerate(OTHER_CHIPS):
                px, py = x ^ fx, y ^ fy
                sems = dict(send_sem=send_sems.at[3 * t + j], recv_sem=recv_sems.at[3 * t + j],
                            device_id=(x, y, 1 - c), device_id_type=MESH)
                mine, theirs = 4 * px + 2 * py + c, 4 * px + 2 * py + (1 - c)
                sends.append(pltpu.make_async_remote_copy(src_ref=in_refs[t].at[mine], dst_ref=g_refs[t].at[mine], **sems))
                if with_arrivals:
                    arrivals.append(pltpu.make_async_remote_copy(
                        src_ref=in_refs[t].at[mine], dst_ref=g_refs[t].at[theirs], **sems))
        return sends, arrivals

    def start(in_refs, g_refs, scr):
        for cp in copies(in_refs, g_refs, scr, False)[0]:
            cp.start()

    def wait(in_refs, g_refs, scr):
        sends, arrivals = copies(in_refs, g_refs, scr, True)
        for cp in arrivals:
            cp.wait_recv()
        for cp in sends:
            cp.wait_send()

    return _Carry(gathered, [jax.ShapeDtypeStruct(a.shape, a.dtype) for a in gathered], _sems(3 * n), start, wait,
                  aliases={t: t for t in range(n)})


def _allreduce_rows(x, *, name):
    def body(x_ref, o_ref, sib_ref, mine_ref, tab_ref, send_sems, recv_sems):
        x, y, c = lax.axis_index("x"), lax.axis_index("y"), lax.axis_index("c")
        swap = pltpu.make_async_remote_copy(src_ref=x_ref, dst_ref=sib_ref, send_sem=send_sems.at[0],
                                            recv_sem=recv_sems.at[0], device_id=(x, y, 1 - c), device_id_type=MESH)
        swap.start()
        swap.wait()
        mine_ref[...] = x_ref[...] + sib_ref[...]
        tab_ref[pl.ds(2 * x + y, 1)] = mine_ref[...][None]

        def copy(k, slot):
            fx, fy = OTHER_CHIPS[k]
            return pltpu.make_async_remote_copy(
                src_ref=mine_ref, dst_ref=tab_ref.at[slot], send_sem=send_sems.at[1 + k], recv_sem=recv_sems.at[1 + k],
                device_id=(x ^ fx, y ^ fy, c), device_id_type=MESH)

        for k in range(3):
            copy(k, 2 * x + y).start()
        for k, (fx, fy) in enumerate(OTHER_CHIPS):
            copy(k, 2 * (x ^ fx) + (y ^ fy)).wait()
        o_ref[...] = ((tab_ref[0] + tab_ref[1]) + tab_ref[2]) + tab_ref[3]

    vmem = pl.BlockSpec(memory_space=pltpu.VMEM)
    return pl.pallas_call(
        body, name=name, out_shape=jax.ShapeDtypeStruct(x.shape, F32), in_specs=[vmem], out_specs=vmem,
        scratch_shapes=[pltpu.VMEM(x.shape, F32), pltpu.VMEM(x.shape, F32), pltpu.VMEM((4,) + x.shape, F32)] + _sems(4),
    )(x)


def _allgather(shards, *, name):
    n = len(shards)
    per = 10

    def body(*refs):
        x_refs, out_refs = refs[:n], refs[n:2 * n]
        send_sems, recv_sems, local_sems = refs[2 * n:]
        x, y, c = lax.axis_index("x"), lax.axis_index("y"), lax.axis_index("c")
        me, sibling = (x, y, c), (x, y, 1 - c)
        near_x, near_y, across = (1 - x, y), (x, 1 - y), (1 - x, 1 - y)

        def rows(ref, t, half):
            r = shards[t].shape[0]
            h = -(-(r // 2) // 16) * 16
            if half is None:
                return ref
            return ref.at[pl.ds(0, h)] if half == 0 else ref.at[pl.ds(h, r - h)]

        def copy(t, k, block, half, to, from_input=False):
            px, py, pc = block
            slab = rows(out_refs[t].at[4 * px + 2 * py + pc], t, half)
            return pltpu.make_async_remote_copy(
                src_ref=rows(x_refs[t], t, half) if from_input else slab, dst_ref=slab,
                send_sem=send_sems.at[per * t + k], recv_sem=recv_sems.at[per * t + k], device_id=to,
                device_id_type=MESH)

        mine = [pltpu.make_async_copy(x_refs[t], out_refs[t].at[4 * x + 2 * y + c], local_sems.at[t]) for t in range(n)]
        for cp in mine:
            cp.start()
        sent = []

        def send(cp):
            cp.start()
            sent.append(cp)

        for t in range(n):
            send(copy(t, 0, me, None, sibling, from_input=True))
            send(copy(t, 1, me, 0, (*near_x, c), from_input=True))
            send(copy(t, 2, me, 1, (*near_y, c), from_input=True))
            send(copy(t, 3, me, 1, (*near_x, c), from_input=True))
            send(copy(t, 4, me, 0, (*near_y, c), from_input=True))
        for t in range(n):
            copy(t, 1, (*near_x, c), 0, me).wait_recv()
            send(copy(t, 5, (*near_x, c), 0, (*near_y, c)))
            copy(t, 2, (*near_y, c), 1, me).wait_recv()
            send(copy(t, 6, (*near_y, c), 1, (*near_x, c)))
        for t in range(n):
            copy(t, 3, (*near_x, c), 1, me).wait_recv()
            send(copy(t, 7, (*near_x, c), None, sibling))
            copy(t, 4, (*near_y, c), 0, me).wait_recv()
            send(copy(t, 8, (*near_y, c), None, sibling))
            copy(t, 5, (*across, c), 0, me).wait_recv()
            copy(t, 6, (*across, c), 1, me).wait_recv()
            send(copy(t, 9, (*across, c), None, sibling))
        for t in range(n):
            copy(t, 0, sibling, None, me).wait_recv()
            for k, chip in ((7, near_x), (8, near_y), (9, across)):
                copy(t, k, (*chip, 1 - c), None, me).wait_recv()
        for cp in sent:
            cp.wait_send()
        for cp in mine:
            cp.wait()

    return pl.pallas_call(
        body, name=name, out_shape=[jax.ShapeDtypeStruct((N_DEV,) + a.shape, a.dtype) for a in shards],
        in_specs=[ANY] * n, out_specs=[ANY] * n,
        scratch_shapes=[pltpu.SemaphoreType.DMA((per * n,)), pltpu.SemaphoreType.DMA((per * n,)),
                        pltpu.SemaphoreType.DMA((n,))],
    )(*shards)


def _carry_sibling(slabs, small=None):
    n = len(slabs)
    extra = [] if small is None else [small]

    def copies(in_refs, out_refs, scr):
        send_sems, recv_sems = scr
        x, y, c = lax.axis_index("x"), lax.axis_index("y"), lax.axis_index("c")
        sibling = (x, y, 1 - c)
        out = []
        for t in range(n):
            for q in range(4):
                out.append(pltpu.make_async_remote_copy(
                    src_ref=in_refs[t].at[2 * q + (1 - c)], dst_ref=out_refs[t].at[q],
                    send_sem=send_sems.at[4 * t + q], recv_sem=recv_sems.at[4 * t + q],
                    device_id=sibling, device_id_type=MESH))
        if extra:
            out.append(pltpu.make_async_remote_copy(
                src_ref=in_refs[n], dst_ref=out_refs[n], send_sem=send_sems.at[4 * n], recv_sem=recv_sems.at[4 * n],
                device_id=sibling, device_id_type=MESH))
        return out

    def start(*refs):
        for cp in copies(*refs):
            cp.start()

    def wait(*refs):
        for cp in copies(*refs):
            cp.wait()

    return _Carry(list(slabs) + extra,
                  [jax.ShapeDtypeStruct((4,) + a.shape[1:], a.dtype) for a in slabs]
                  + [jax.ShapeDtypeStruct(a.shape, a.dtype) for a in extra], _sems(4 * n + 1), start, wait)


def _carry_chips(psums, small_sum=None):
    n = len(psums)
    table = small_sum is not None

    def copies(in_refs, out_refs, scr, arrivals):
        send_sems, recv_sems = scr[0], scr[1]
        x, y, c = lax.axis_index("x"), lax.axis_index("y"), lax.axis_index("c")
        out = []
        for k, (fx, fy) in enumerate(OTHER_CHIPS):
            px, py = x ^ fx, y ^ fy
            for t in range(n):
                out.append(pltpu.make_async_remote_copy(
                    src_ref=in_refs[t].at[2 * px + py], dst_ref=out_refs[t].at[k],
                    send_sem=send_sems.at[3 * t + k], recv_sem=recv_sems.at[3 * t + k],
                    device_id=(px, py, c), device_id_type=MESH))
            if table:
                slot = 2 * px + py if arrivals else 2 * x + y
                out.append(pltpu.make_async_remote_copy(
                    src_ref=in_refs[n], dst_ref=out_refs[n].at[slot], send_sem=send_sems.at[3 * n + k],
                    recv_sem=recv_sems.at[3 * n + k], device_id=(px, py, c), device_id_type=MESH))
        return out

    def own(in_refs, out_refs, scr):
        x, y = lax.axis_index("x"), lax.axis_index("y")
        return pltpu.make_async_copy(in_refs[n], out_refs[n].at[2 * x + y], scr[2])

    def start(in_refs, out_refs, scr):
        if table:
            own(in_refs, out_refs, scr).start()
        for cp in copies(in_refs, out_refs, scr, False):
            cp.start()

    def wait(in_refs, out_refs, scr):
        for cp in copies(in_refs, out_refs, scr, True):
            cp.wait()
        if table:
            own(in_refs, out_refs, scr).wait()

    out_shapes = [jax.ShapeDtypeStruct((3,) + a.shape[1:], a.dtype) for a in psums]
    if table:
        out_shapes.append(jax.ShapeDtypeStruct((4,) + small_sum.shape, F32))
    return _Carry(list(psums) + ([small_sum] if table else []), out_shapes,
                  _sems(3 * n + 3) + ([pltpu.SemaphoreType.DMA] if table else []), start, wait)


def _to_comm(name, kind, block, dtype=BF16):
    a = block[0]
    if kind == "cols":
        a = a.T
        if name == "w_in" and dtype == BF16:
            a = jnp.pad(a, ((0, IN_SHARD_PAD - IN_SHARD), (0, 0)))
    return a if kind == "f32" else a.astype(dtype)


def _from_comm(name, kind, a):
    if kind == "cols":
        if name == "w_in" and a.shape[0] != IN_SHARD:
            a = a[:IN_SHARD]
        a = a.T
    return a[None]


def _assemble_weights(g):
    out = {}
    if "w_in" in g:
        out["wt_main"] = _assemble_wt_main(g["w_in"], name="assemble_w_in")
        j, l0 = divmod(O_F, IN_SHARD)
        out["wt_f"] = jnp.pad(g["w_in"][j, l0:l0 + HEADS], ((0, 128 - HEADS), (0, 0)))
    square = dict(w_branch_a="w_a", w_branch_b="w_b", w_out="w_out", w_ple_gate="w_pg")
    for long, short in square.items():
        if long in g:
            out[short] = g[long].reshape(D_MODEL, D_MODEL)
    if "w_up" in g:
        out["wt_up"] = g["w_up"].reshape(2 * D_FF, D_MODEL)
    if "conv_w" in g:
        out["conv_w"] = g["conv_w"].transpose(1, 0, 2).reshape(3, 2 * D_FF)
    if "w_down" in g:
        out["w_down"] = g["w_down"].reshape(D_FF, D_MODEL)
    if "w_ple" in g:
        out["wt_ple"] = g["w_ple"].reshape(D_MODEL, PLE_DIM)
    return out


def _grad_slabs(gr):
    out = {}
    if "wt_main" in gr:
        gm, gf = gr["wt_main"], gr["wt_f"]
        segments = ((0, 2048, gm, 0), (2048, O_F, gm, 2048), (O_F, O_G, gf, -O_F), (O_G, IN_COLS, gm, 2048 - O_G))
        slabs = []
        for j in range(N_DEV):
            lo, hi = j * IN_SHARD, (j + 1) * IN_SHARD
            pieces = [src[max(lo, a) + shift:min(hi, b) + shift] for a, b, src, shift in segments if max(lo, a) < min(hi, b)]
            pieces.append(jnp.zeros((IN_SHARD_PAD - IN_SHARD, D_MODEL), gm.dtype))
            slabs.append(jnp.concatenate(pieces, axis=0))
        out["w_in"] = jnp.stack(slabs)
    rows = dict(w_a="w_branch_a", w_b="w_branch_b", w_out="w_out", wt_up="w_up", w_down="w_down", w_pg="w_ple_gate")
    for short, long in rows.items():
        if short in gr:
            out[long] = gr[short].reshape(N_DEV, -1, D_MODEL)
    if "conv_w" in gr:
        out["conv_w"] = gr["conv_w"].reshape(3, N_DEV, -1).transpose(1, 0, 2)
    if "wt_ple" in gr:
        out["w_ple"] = gr["wt_ple"].reshape(N_DEV, -1, PLE_DIM)
    return {k: v.astype(BF16) for k, v in out.items()}


def _rows(a, rows):
    flat = a.reshape(-1)
    return jnp.pad(flat, (0, rows * 1024 - flat.shape[0])).reshape(rows, 1024)


def _pack_small(parts):
    return jnp.concatenate([_rows(parts[n].astype(F32), r) for n, r in SMALL], axis=0)


def _small(packed, name, shape):
    off, r = SMALL_OFF[name]
    n = math.prod(shape)
    return packed[off:off + r].reshape(-1)[:n].reshape(shape)


class _Exchanges:
    W_S_ROWS = SMALL_OFF["gmlp_w_s"]

    def __init__(self, later, shards, pos):
        self.later, self.shards, self.pos = later, dict(zip(later, shards)), pos
        self.level1, self.slabs, self.from_sib, self.sums32, self.reduced, self.tables = {}, {}, {}, {}, {}, {}

    def gather1(self, names):
        carry = _carry_gather1([self.shards[n] for n in names])
        carry.names = names
        return carry

    def gather1_done(self, carry, results):
        self.level1.update(zip(carry.names, results))

    def gather2(self):
        return _carry_gather2([self.level1[n] for n in self.later])

    def weights(self, full):
        return _assemble_weights(dict(zip(self.later, full)))

    def sibling(self, grads):
        slabs = _grad_slabs(grads)
        self.slabs.update(slabs)
        carry = _carry_sibling(list(slabs.values()))
        carry.names = list(slabs)
        return carry

    def sibling_done(self, carry, results):
        self.from_sib.update(zip(carry.names, results))

    def chips(self, names, table=None):
        sums = {n: _sum_pairs(self.slabs[n], self.from_sib[n], self.pos, name="sum_sibling_" + n) for n in names}
        self.sums32.update({n: s32 for n, (s32, _) in sums.items()})
        carry = _carry_chips([s16 for _, s16 in sums.values()], None if table is None else self.table_part(table))
        carry.names, carry.table = list(names), table
        return carry

    def chips_done(self, carry, results):
        if carry.table is not None:
            *results, self.tables[carry.table] = results
        self.reduced.update({n: (self.sums32[n], r) for n, r in zip(carry.names, results)})

    def sibling_small(self, small_g):
        self.small_g = small_g
        return _carry_sibling([], small_g)

    def sibling_small_done(self, small_sib):
        self.small_chip = _pair_sum_small(self.small_g, small_sib, name="sum_sibling_small")

    def table_part(self, which):
        off, rows = self.W_S_ROWS
        if which == "w_s":
            return self.small_chip[off:off + rows]
        return jnp.concatenate([self.small_chip[:off], self.small_chip[off + rows:]], axis=0)

    def table(self):
        off = self.W_S_ROWS[0]
        rest = self.tables["rest"]
        return jnp.concatenate([rest[:, :off], self.tables["w_s"], rest[:, off:]], axis=1)


def _local_step(x, p, target, w, sm, ex=None):
    s = x.shape[0]
    mm = _matmul
    wt_main = w["wt_main"]
    conv_b = sm["conv_b"]
    bs_t = jnp.pad(sm["gmlp_b_s"].T, ((0, 0), (0, 128 - GROUPS)))
    b_f = jnp.pad(sm["b_f"], ((0, 0), (0, 128 - HEADS)))
    big = dict(tm=1024, tn=1024, tk=1024)
    whole_s = dict(tn=1024, tk=s)

    h = _rmsnorm_fwd(x, sm["norm_mix_g"], name="norm_mix")
    tall = dict(tm=s, tn=512, tk=1024)
    qkv_args = dict(mode="nt", out_dtype=BF16, name="in_qkv", n=3072, b_off=8, **tall)
    f_logit = mm(h, w["wt_f"], mode="nt", out_dtype=F32, name="in_f", tm=1024, tk=1024)
    cqe = _forget_cumsum(f_logit, b_f, name="forget_cumsum")
    uvg = dict(mode="nt", out_dtype=F32, name="in_uvg", n=4096, **tall)
    if ex is None:
        qkv = mm(h, wt_main, **qkv_args)
        (qa, ka, vt), _ = _attn_prep(qkv, cqe, name="attn_prep")
        (b, lse3), _ = _attn_fwd(qa, ka, vt, name="attn_fwd")
        zuvg = mm(h, wt_main, **uvg)
    else:
        groups = (["w_branch_a"], ["w_branch_b"], [n for n in ex.later if n not in ("w_branch_a", "w_branch_b")])
        carries = [ex.gather1(names) for names in groups]
        qkv, got0 = mm(h, wt_main, carry=carries[0], **qkv_args)
        (qa, ka, vt), got1 = _attn_prep(qkv, cqe, carries[1], name="attn_prep")
        (b, lse3), got2 = _attn_fwd(qa, ka, vt, carries[2], name="attn_fwd")
        for carry, got in zip(carries, (got0, got1, got2)):
            ex.gather1_done(carry, got)
        zuvg, full = mm(h, wt_main, carry=ex.gather2(), **uvg)
        w = {**w, **ex.weights(full)}
    a = _gmlp_fwd(zuvg, sm["gmlp_ln_g"], sm["gmlp_ln_b"], sm["gmlp_w_s"], bs_t, name="gmlp_fwd")
    wt_up, conv_w = w["wt_up"], w["conv_w"]
    ya, yb, merged = _branches_merge(a, b, w["w_a"], w["w_b"], zuvg, name="branches_merge")
    x1, h2 = mm(merged, w["w_out"], mode="nn", out_dtype=F32, name="out_proj", add=x, norm_g=sm["norm_ffn_g"], **big)
    up_a, up_g, act = _up_convglu(h2, wt_up, conv_w, conv_b, name="up_convglu")
    x2, h3 = mm(act, w["w_down"], mode="nn", out_dtype=F32, name="down", tm=1024, tn=1024, tk=1408, add=x1,
                norm_g=sm["norm_ple_g"])

    loss, dx3, dple, dgp, d_norm_final = _ple_loss(p, w["wt_ple"], h3, w["w_pg"], x2, target, sm["norm_final_g"],
                                                   name="ple_loss")
    g_wt_ple = mm(dple, p, mode="tn", out_dtype=BF16, name="d_w_ple", tm=512, tn=256, tk=s)
    g_w_pg = mm(h3, dgp, mode="tn", out_dtype=BF16, name="d_w_pg", tm=512, **whole_s)
    (dx2, dx2b, d_norm_ple), _ = _matmul_rmsnorm_bwd([dgp], w["w_pg"], dx3, x2, sm["norm_ple_g"], mode="nt", tk=1024,
                                                     name="d_h3_norm_ple_bwd")
    g_w_down = mm(act, dx2b, mode="tn", out_dtype=BF16, name="d_w_down", tm=1408, **whole_s)
    dact_args = dict(mode="nt", out_dtype=BF16, name="d_act", tm=s, tn=256, tk=1024)
    if ex is None:
        dact = mm(dx2b, w["w_down"], **dact_args)
    else:
        early = ex.sibling(dict(w_pg=g_w_pg, wt_ple=g_wt_ple))
        dact, got = mm(dx2b, w["w_down"], carry=early, **dact_args)
        ex.sibling_done(early, got)
    dup_a, dup_g, dcw_a, dcw_g, dcb_a, dcb_g = _convglu_bwd(dact, up_a, up_g, conv_w, conv_b, name="convglu_bwd")
    g_wt_up = mm(dup_a, h2, mode="tn", out_dtype=BF16, name="d_w_up_a", tm=1408, out_rows=2 * D_FF, **whole_s)
    g_wt_up = mm(dup_g, h2, mode="tn", out_dtype=BF16, name="d_w_up_g", tm=1408, out_rows=2 * D_FF,
                 o_off=D_FF // 1408, into=g_wt_up, **whole_s)
    (dx1, dx1b, d_norm_ffn), _ = _matmul_rmsnorm_bwd([dup_a, dup_g], wt_up, dx2, x1, sm["norm_ffn_g"], mode="nn",
                                                     tk=1408, name="d_h2_norm_ffn_bwd", resident=True)
    g_w_out = mm(merged, dx1b, mode="tn", out_dtype=BF16, name="d_w_out", tm=512, **whole_s)
    dya, dyb, dga, dgb = _merge_bwd(dx1b, w["w_out"], ya, yb, zuvg, name="merge_bwd")
    g_w_a = mm(a, dya, mode="tn", out_dtype=BF16, name="d_w_a", tm=512, **whole_s)
    g_w_b = mm(b, dyb, mode="tn", out_dtype=BF16, name="d_w_b", tm=512, **whole_s)
    da = mm(dya, w["w_a"], mode="nt", out_dtype=BF16, name="d_a", **big)
    db = mm(dyb, w["w_b"], mode="nt", out_dtype=BF16, name="d_b", **big)
    grads = dict(w_a=g_w_a, w_b=g_w_b, w_out=g_w_out, wt_up=g_wt_up, conv_w=jnp.concatenate([dcw_a, dcw_g], axis=1),
                 w_down=g_w_down, wt_ple=g_wt_ple, w_pg=g_w_pg)
    gmlp_args = (da, zuvg, sm["gmlp_ln_g"], sm["gmlp_ln_b"], sm["gmlp_w_s"], bs_t)
    if ex is None:
        (dzu, dzv, d_w_s, d_bs_t, d_ln_g, d_ln_b), _ = _gmlp_bwd(*gmlp_args, name="gmlp_bwd")
    else:
        rest = ex.sibling({k: v for k, v in grads.items() if k not in ("w_pg", "wt_ple")})
        early_chips = ex.chips(early.names)
        both = _carry_join(rest, early_chips)
        (dzu, dzv, d_w_s, d_bs_t, d_ln_g, d_ln_b), got = _gmlp_bwd(*gmlp_args, both, name="gmlp_bwd")
        got_rest, got_early = both.split(got)
        ex.sibling_done(rest, got_rest)
        ex.chips_done(early_chips, got_early)
    small = dict(norm_mix_g=jnp.zeros((1, D_MODEL), F32), b_f=jnp.zeros((1, HEADS), F32), gmlp_ln_g=d_ln_g,
                 gmlp_ln_b=d_ln_b, gmlp_w_s=d_w_s, gmlp_b_s=d_bs_t[:, :GROUPS].T, norm_ffn_g=d_norm_ffn,
                 conv_b=jnp.concatenate([dcb_a, dcb_g], axis=1), norm_ple_g=d_norm_ple, norm_final_g=d_norm_final)
    if ex is None:
        delta3, _ = _attn_delta(db, b, name="attn_delta")
        (dq, dk, dv, aux, dcq3), _ = _attn_bwd(qa, ka, qkv, db, lse3, delta3, name="attn_bwd")
    else:
        delta3, (small_sib,) = _attn_delta(db, b, ex.sibling_small(_pack_small(small)), name="attn_delta")
        ex.sibling_small_done(small_sib)
        main_chips = ex.chips(rest.names, table="rest")
        (dq, dk, dv, aux, dcq3), got = _attn_bwd(qa, ka, qkv, db, lse3, delta3, main_chips, name="attn_bwd")
        ex.chips_done(main_chips, got)
    dcq16 = jnp.pad(dcq3[:, :2, :].reshape(HEADS, s).T, ((0, 0), (0, 128 - HEADS)))
    dzf, d_b_f = _forget_bwd(dcq16, aux, f_logit, b_f, name="forget_bwd")
    dz_parts = [dzu, dzv, dga, dgb, dq, dk, dv]
    w_s_chips = None if ex is None else ex.chips([], table="w_s")
    g_wt_main, got = _grad_w_parts(dz_parts, h, name="d_w_main", tm=512, carry=w_s_chips)
    if ex is not None:
        ex.chips_done(w_s_chips, got)
    g_wt_f = mm(dzf, h, mode="tn", out_dtype=BF16, name="d_w_f", **whole_s)
    grads = dict(grads, wt_main=g_wt_main, wt_f=g_wt_f)
    w_in_chips = None
    if ex is not None:
        w_in_sib = ex.sibling(dict(wt_main=g_wt_main, wt_f=g_wt_f))
        ex.sibling_done(w_in_sib, _run_carry(w_in_sib, name="exchange_sibling_w_in"))
        w_in_chips = ex.chips(w_in_sib.names)
    (dx0, _, d_norm_mix), got = _matmul_rmsnorm_bwd(dz_parts, wt_main, dx1, x, sm["norm_mix_g"], mode="nn", tk=1024,
                                                    extra=(dzf, w["wt_f"]), name="d_h_norm_mix_bwd", carry=w_in_chips,
                                                    lead=True)
    if ex is not None:
        ex.chips_done(w_in_chips, got)
    return loss, dx0, grads, dict(small, norm_mix_g=d_norm_mix, b_f=d_b_f[:, :HEADS])


def kernel(x, p, norm_mix_g, w_in, b_f, gmlp_ln_g, gmlp_ln_b, gmlp_w_s, gmlp_b_s, w_branch_a, w_branch_b, w_out, norm_ffn_g, w_up, conv_w, conv_b, w_down, norm_ple_g, w_ple, w_ple_gate, norm_final_g, loss_target, m_norm_mix_g, m_w_in, m_b_f, m_gmlp_ln_g, m_gmlp_ln_b, m_gmlp_w_s, m_gmlp_b_s, m_w_branch_a, m_w_branch_b, m_w_out, m_norm_ffn_g, m_w_up, m_conv_w, m_conv_b, m_w_down, m_norm_ple_g, m_w_ple, m_w_ple_gate, m_norm_final_g, v_norm_mix_g, v_w_in, v_b_f, v_gmlp_ln_g, v_gmlp_ln_b, v_gmlp_w_s, v_gmlp_b_s, v_w_branch_a, v_w_branch_b, v_w_out, v_norm_ffn_g, v_w_up, v_conv_w, v_conv_b, v_w_down, v_norm_ple_g, v_w_ple, v_w_ple_gate, v_norm_final_g):
    given = dict(locals())
    weights = {n: given[n] for n in WEIGHT_ORDER}
    mom_m = {n: given["m_" + n] for n in WEIGHT_ORDER}
    mom_v = {n: given["v_" + n] for n in WEIGHT_ORDER}
    pos = jnp.stack([lax.axis_index("x"), lax.axis_index("y"), lax.axis_index("c")]).astype(I32)
    names = [n for n, _ in SHARDED]
    kinds = dict(SHARDED)

    later = [n for n in names if n != "w_in"]

    first = _allgather([_to_comm("w_in", kinds["w_in"], weights["w_in"])], name="allgather_w_in")
    ex = _Exchanges(later, [_to_comm(n, kinds[n], weights[n]) for n in later], pos)

    sm = dict(norm_mix_g=norm_mix_g, b_f=b_f, gmlp_ln_g=gmlp_ln_g, gmlp_ln_b=gmlp_ln_b, gmlp_w_s=gmlp_w_s[0],
              gmlp_b_s=gmlp_b_s[0], norm_ffn_g=norm_ffn_g, conv_b=conv_b, norm_ple_g=norm_ple_g,
              norm_final_g=norm_final_g.reshape(1, D_MODEL))
    loss_part, dx0, grads, small = _local_step(
        x[0], p[0, 0], loss_target[0], _assemble_weights({"w_in": first[0]}), sm, ex)

    b_f_and_loss = jnp.concatenate([small["b_f"].reshape(-1), loss_part[0, :1]])
    last = _allreduce_rows(jnp.concatenate([_rows(small["norm_mix_g"], 8), _rows(b_f_and_loss, 8)], axis=0),
                           name="allreduce_last")
    loss = last[8, HEADS]
    small_last = jnp.pad(last, ((0, SMALL_ROWS - 16), (0, 0)))

    grad, delta, new_m, new_v = {}, {}, {}, {}
    for n in names:
        s32, r = ex.reduced[n]
        outs = _adam_sharded(s32, r, *[_to_comm(n, kinds[n], src[n], F32) for src in (weights, mom_m, mom_v)], pos,
                             name="adam_" + n)
        grad[n], delta[n], new_m[n], new_v[n] = [_from_comm(n, kinds[n], o) for o in outs]
    replicated = [n for n, _ in SMALL]
    rep = lambda src: _pack_small({n: src[n] for n in replicated})
    packed = _adam_replicated(ex.table(), small_last, rep(weights), rep(mom_m), rep(mom_v), name="adam_replicated")
    for out, pk in zip((grad, delta, new_m, new_v), packed):
        for n in replicated:
            out[n] = _small(pk, n, weights[n].shape)

    return (loss, dx0, *[grad[n] for n in WEIGHT_ORDER], *[delta[n] for n in WEIGHT_ORDER],
            *[new_m[n] for n in WEIGHT_ORDER], *[new_v[n] for n in WEIGHT_ORDER])
```

```python
import functools
import math

import jax
import jax.numpy as jnp
from jax import lax
from jax.experimental import pallas as pl
from jax.experimental.pallas import tpu as pltpu

F32 = jnp.float32
BF16 = jnp.bfloat16
I32 = jnp.int32

D_MODEL = 1024
GROUPS = 8
GDIM = 128
GBLOCK = 128
CHUNK = 64
HEADS = 16
HEAD_DIM = 64
D_FF = 2816
PLE_DIM = 256
EPS = 1e-6
N_DEV = 8
ATT_SCALE = HEAD_DIM ** -0.5
NEG = -1e30

ADAM_LR = 0.001
ADAM_B1 = 0.9
ADAM_B2 = 0.999
ADAM_EPS = 1e-08
ADAM_WD = 0.01
ADAM_STEP = 10

V7X_VMEM_LIMIT = 48 * 1024 * 1024
MESH = pl.DeviceIdType.MESH

O_F = 2 * 1024 + 3 * 1024
O_G = O_F + HEADS
IN_COLS = O_G + 2 * D_MODEL
MAIN_COLS = IN_COLS - HEADS
IN_SHARD = IN_COLS // N_DEV
IN_SHARD_PAD = 912

SHARDED = (("w_in", "cols"), ("w_branch_a", "rows"), ("w_branch_b", "rows"), ("w_out", "rows"), ("w_up", "cols"),
           ("conv_w", "f32"), ("w_down", "rows"), ("w_ple", "cols"), ("w_ple_gate", "rows"))

SMALL = (("norm_mix_g", 8), ("b_f", 8), ("gmlp_ln_g", 8), ("gmlp_ln_b", 8), ("gmlp_w_s", 128), ("gmlp_b_s", 8),
         ("norm_ffn_g", 8), ("conv_b", 8), ("norm_ple_g", 8), ("norm_final_g", 8))
SMALL_OFF = {}
_o = 0
for _n, _r in SMALL:
    SMALL_OFF[_n] = (_o, _r)
    _o += _r
SMALL_ROWS = _o

WEIGHT_ORDER = ("norm_mix_g", "w_in", "b_f", "gmlp_ln_g", "gmlp_ln_b", "gmlp_w_s", "gmlp_b_s", "w_branch_a",
                "w_branch_b", "w_out", "norm_ffn_g", "w_up", "conv_w", "conv_b", "w_down", "norm_ple_g", "w_ple",
                "w_ple_gate", "norm_final_g")


def _cparams(sem):
    return pltpu.CompilerParams(dimension_semantics=sem, vmem_limit_bytes=V7X_VMEM_LIMIT)


def _gelu(x):
    c = math.sqrt(2.0 / math.pi)
    return 0.5 * x * (1.0 + jnp.tanh(c * (x + 0.044715 * x * x * x)))


def _gelu_and_grad(x):
    c = math.sqrt(2.0 / math.pi)
    t = jnp.tanh(c * (x + 0.044715 * x * x * x))
    g = 0.5 * x * (1.0 + t)
    dg = 0.5 * (1.0 + t) + 0.5 * x * (1.0 - t * t) * (c * (1.0 + 3.0 * 0.044715 * x * x))
    return g, dg


def _sigmoid(x):
    return 1.0 / (1.0 + jnp.exp(-x))


def _dot(a, b, dims):
    return lax.dot_general(a, b, (dims, ((), ())), preferred_element_type=F32)


NN = ((1,), (0,))
NT = ((1,), (1,))
TN = ((0,), (0,))


def _row_tile(rows, most):
    best = None
    for t in range(16, min(rows, most) + 1, 16):
        if rows % t == 0:
            best = t
    return best if best is not None else rows


def _matmul(a, b, *, mode, out_dtype, name, tm=512, tn=512, tk=512, add=None, n=None, b_off=0,
            out_rows=None, o_off=0, into=None, norm_g=None, carry=None):
    if mode == "tn":
        kdim, m = a.shape
    else:
        m, kdim = a.shape
    if n is None:
        n = b.shape[0] if mode == "nt" else b.shape[1]
    tm, tn, tk = min(tm, m), min(tn, n), min(tk, kdim)
    assert m % tm == 0 and n % tn == 0 and kdim % tk == 0, (name, m, n, kdim, tm, tn, tk)
    nk = kdim // tk
    dims = {"nn": NN, "nt": NT, "tn": TN}[mode]

    n_in = 2 + (add is not None) + (into is not None) + (norm_g is not None)
    assert norm_g is None or tn == n, "the RMS norm needs whole rows"

    def finish(r, refs):
        if add is not None:
            r = refs[2][...].astype(F32) + r
        refs[n_in][...] = r.astype(out_dtype)
        if norm_g is not None:
            rs = lax.rsqrt(jnp.mean(r * r, axis=-1, keepdims=True) + EPS)
            refs[n_in + 1][...] = ((r * rs) * refs[n_in - 1][...]).astype(BF16)

    def body(*refs):
        a_ref, b_ref = refs[:2]
        part = _dot(a_ref[...].astype(BF16), b_ref[...].astype(BF16), dims)
        if nk == 1:
            finish(part, refs)
            return
        acc_ref = refs[-1]
        k = pl.program_id(2)

        @pl.when(k == 0)
        def _():
            acc_ref[...] = part

        @pl.when((k > 0) & (k < nk - 1))
        def _():
            acc_ref[...] += part

        @pl.when(k == nk - 1)
        def _():
            finish(acc_ref[...] + part, refs)

    a_spec = pl.BlockSpec((tk, tm), lambda i, j, k: (k, i)) if mode == "tn" else pl.BlockSpec((tm, tk), lambda i, j, k: (i, k))
    if mode == "nt":
        b_spec = pl.BlockSpec((tn, tk), lambda i, j, k: (j + b_off, k))
    else:
        b_spec = pl.BlockSpec((tk, tn), lambda i, j, k: (k + b_off, j))
    o_spec = pl.BlockSpec((tm, tn), lambda i, j, k: (i + o_off, j))
    in_specs = [a_spec, b_spec] + ([pl.BlockSpec((tm, tn), lambda i, j, k: (i, j))] if add is not None else [])
    args = (a, b) + ((add,) if add is not None else ())
    aliases = {}
    if into is not None:
        aliases = {len(args): 0}
        in_specs.append(pl.BlockSpec(memory_space=pl.ANY))
        args += (into,)
    out_specs = [o_spec]
    out_shape = [jax.ShapeDtypeStruct((m if out_rows is None else out_rows, n), out_dtype)]
    if norm_g is not None:
        in_specs.append(pl.BlockSpec((1, n), lambda i, j, k: (0, 0)))
        args += (norm_g,)
        out_specs.append(pl.BlockSpec((tm, tn), lambda i, j, k: (i, j)))
        out_shape.append(jax.ShapeDtypeStruct((m, n), BF16))
    outs, carried = _carry_call(
        body, carry, name=name, grid=(m // tm, n // tn, nk), in_specs=in_specs, out_specs=out_specs,
        out_shape=out_shape, scratch_shapes=[pltpu.VMEM((tm, tn), F32)] if nk > 1 else [], args=args,
        own_aliases=aliases)
    out = outs[0] if norm_g is None else tuple(outs)
    return out if carry is None else (out, carried)


def _row_spec(tr, width, col_block=0):
    return pl.BlockSpec((tr, width), lambda i: (i, col_block))


def _full_spec(shape):
    return pl.BlockSpec(shape, lambda i: tuple(0 for _ in shape))


def _rmsnorm_fwd(x, g, *, name, tr=256):
    s, d = x.shape

    def body(x_ref, g_ref, o_ref):
        xv = x_ref[...]
        r = lax.rsqrt(jnp.mean(xv * xv, axis=-1, keepdims=True) + EPS)
        o_ref[...] = ((xv * r) * g_ref[...]).astype(BF16)

    return pl.pallas_call(
        body, name=name, grid=(s // tr,),
        in_specs=[_row_spec(tr, d), _full_spec((1, d))], out_specs=_row_spec(tr, d),
        out_shape=jax.ShapeDtypeStruct((s, d), BF16), compiler_params=_cparams(("parallel",)),
    )(x, g)


def _matmul_rmsnorm_bwd(a_parts, b, dres, x, g, *, mode, tk, name, extra=None, tm=512, carry=None, lead=False,
                        resident=False):
    s, d = x.shape
    n_row = s // tm
    spans, lo = [], 0
    for a in a_parts:
        spans.append((lo, lo + a.shape[1] // tk))
        lo = spans[-1][1]
    n_main, total = lo, lo + (extra is not None)
    n_parts = len(a_parts)

    def body(*refs):
        a_refs, b_ref = refs[:n_parts], refs[n_parts]
        k0 = n_parts + 1
        ax_ref, bx_ref = (refs[k0], refs[k0 + 1]) if extra is not None else (None, None)
        k0 += 2 * (extra is not None)
        dres_ref, x_ref, g_ref, dx_ref, dxb_ref, dg_ref, acc_all = refs[k0:k0 + 7]
        if resident:
            kk, i = pl.program_id(0), pl.program_id(1)
            acc_ref = acc_all.at[pl.ds(pl.multiple_of(i * tm, tm), tm)]
        else:
            i, kk = pl.program_id(0), pl.program_id(1)
            acc_ref = acc_all

        def accumulate(part, first):
            if first:
                @pl.when(kk == 0)
                def _():
                    acc_ref[...] = part

                @pl.when(kk > 0)
                def _():
                    acc_ref[...] += part
            else:
                acc_ref[...] += part

        for p, (a_ref, (lo_p, hi_p)) in enumerate(zip(a_refs, spans)):
            @pl.when((kk >= lo_p) & (kk < hi_p))
            def _(a_ref=a_ref, lo_p=lo_p):
                accumulate(_dot(a_ref[...].astype(BF16), b_ref[...].astype(BF16), NN if mode == "nn" else NT), lo_p == 0)

        if extra is not None:
            @pl.when(kk == n_main)
            def _():
                accumulate(_dot(ax_ref[...].astype(BF16), bx_ref[...].astype(BF16), NN), False)

        @pl.when(kk == total - 1)
        def _():
            dhv = acc_ref[...]
            xv = x_ref[...]
            r = lax.rsqrt(jnp.mean(xv * xv, axis=-1, keepdims=True) + EPS)
            xhat = xv * r
            dxhat = dhv * g_ref[...]
            dx = dres_ref[...] + r * (dxhat - xhat * jnp.mean(dxhat * xhat, axis=-1, keepdims=True))
            dx_ref[...] = dx
            dxb_ref[...] = dx.astype(BF16)
            dgp = jnp.sum(dhv * xhat, axis=0, keepdims=True)

            @pl.when(i == 0)
            def _():
                dg_ref[...] = dgp

            @pl.when(i > 0)
            def _():
                dg_ref[...] += dgp

    def spec(shape, index):
        return pl.BlockSpec(shape, (lambda kk, i: index(i, kk)) if resident else index)

    def row(i, kk, lo_p, hi_p):
        if not resident:
            return i
        return jnp.where(kk < lo_p, 0, jnp.where(kk >= hi_p, n_row - 1, i))

    a_specs = [spec((tm, tk), lambda i, kk, lo_p=lo_p, hi_p=hi_p: (row(i, kk, lo_p, hi_p),
                                                                    jnp.clip(kk - lo_p, 0, hi_p - lo_p - 1)))
               for lo_p, hi_p in spans]
    step = lambda kk: jnp.minimum(kk, n_main - 1)
    b_spec = (spec((tk, d), lambda i, kk: (step(kk), 0)) if mode == "nn"
              else spec((d, tk), lambda i, kk: (0, step(kk))))
    rows = spec((tm, d), lambda i, kk: (row(i, kk, total - 1, total), 0))
    one = spec((1, d), lambda i, kk: (0, 0))
    dx_spec, dx_shape = rows, jax.ShapeDtypeStruct((s, d), F32)
    if lead:
        dx_spec = spec((None, tm, d), lambda i, kk: (0, row(i, kk, total - 1, total), 0))
        dx_shape = jax.ShapeDtypeStruct((1, s, d), F32)
    x_specs, x_args = [], []
    if extra is not None:
        kx = extra[0].shape[1]
        x_specs = [spec((tm, kx), lambda i, kk: (row(i, kk, n_main, total), 0)), spec((kx, d), lambda i, kk: (0, 0))]
        x_args = list(extra)
    (dx, dxb, dg), carried = _carry_call(
        body, carry, name=name, grid=(total, n_row) if resident else (n_row, total),
        in_specs=a_specs + [b_spec] + x_specs + [rows, rows, one], out_specs=[dx_spec, rows, one],
        out_shape=[dx_shape, jax.ShapeDtypeStruct((s, d), BF16), jax.ShapeDtypeStruct((1, d), F32)],
        scratch_shapes=[pltpu.VMEM((s if resident else tm, d), F32)], args=list(a_parts) + [b] + x_args + [dres, x, g])
    return (dx, dxb, dg), carried


def _grad_w_parts(a_parts, b, *, name, tm=512, carry=None):
    s, width = a_parts[0].shape
    per, n = width // tm, b.shape[1]

    def body(*refs):
        a_refs, b_ref, o_ref = refs[:len(a_parts)], refs[len(a_parts)], refs[len(a_parts) + 1]
        i = pl.program_id(0)
        for p, a_ref in enumerate(a_refs):
            @pl.when(i // per == p)
            def _(a_ref=a_ref):
                o_ref[...] = _dot(a_ref[...].astype(BF16), b_ref[...].astype(BF16), TN).astype(BF16)

    a_specs = [pl.BlockSpec((s, tm), lambda i, p=p: (0, jnp.clip(i - p * per, 0, per - 1))) for p in range(len(a_parts))]
    (out,), carried = _carry_call(
        body, carry, name=name, grid=(len(a_parts) * per,),
        in_specs=a_specs + [pl.BlockSpec((s, n), lambda i: (0, 0))], out_specs=[pl.BlockSpec((tm, n), lambda i: (i, 0))],
        out_shape=[jax.ShapeDtypeStruct((len(a_parts) * width, n), BF16)], scratch_shapes=[], args=list(a_parts) + [b])
    return out, carried


def _ple_loss(p, wt_ple, h3, w_pg, x2, target, g, *, name, tm=256):
    s, d = x2.shape
    kp = p.shape[1]

    def body(p_ref, wp_ref, h_ref, wg_ref, x_ref, t_ref, g_ref, loss_ref, dx_ref, dple_ref, dgp_ref, dg_ref):
        i = pl.program_id(0)
        ple = _dot(p_ref[...].astype(BF16), wp_ref[...], NT)
        sg = _sigmoid(_dot(h_ref[...], wg_ref[...], NN))
        xv = x_ref[...] + ple * sg
        r = lax.rsqrt(jnp.mean(xv * xv, axis=-1, keepdims=True) + EPS)
        xhat = xv * r
        diff = xhat * g_ref[...] - t_ref[...]
        lp = jnp.zeros((1, 128), F32) + (0.5 / d) * jnp.sum(diff * diff)
        dy = diff * (1.0 / d)
        dxhat = dy * g_ref[...]
        dx = r * (dxhat - xhat * jnp.mean(dxhat * xhat, axis=-1, keepdims=True))
        dx_ref[...] = dx
        dple_ref[...] = (dx * sg).astype(BF16)
        dgp_ref[...] = (dx * ple * (sg * (1.0 - sg))).astype(BF16)
        dgp = jnp.sum(dy * xhat, axis=0, keepdims=True)

        @pl.when(i == 0)
        def _():
            dg_ref[...] = dgp
            loss_ref[...] = lp

        @pl.when(i > 0)
        def _():
            dg_ref[...] += dgp
            loss_ref[...] += lp

    rows = _row_spec(tm, d)
    return pl.pallas_call(
        body, name=name, grid=(s // tm,),
        in_specs=[_row_spec(tm, kp), _full_spec((d, kp)), rows, _full_spec((d, d)), rows, rows, _full_spec((1, d))],
        out_specs=[_full_spec((1, 128)), rows, rows, rows, _full_spec((1, d))],
        out_shape=[jax.ShapeDtypeStruct((1, 128), F32), jax.ShapeDtypeStruct((s, d), F32),
                   jax.ShapeDtypeStruct((s, d), BF16), jax.ShapeDtypeStruct((s, d), BF16),
                   jax.ShapeDtypeStruct((1, d), F32)],
        compiler_params=_cparams(("arbitrary",)),
    )(p, wt_ple, h3, w_pg, x2, target, g)


def _branches_merge(a, b, w_a, w_b, zuvg, *, name, tm=512):
    s, d = a.shape

    def body(a_ref, b_ref, wa_ref, wb_ref, ga_ref, gb_ref, ya_ref, yb_ref, o_ref):
        ya = _dot(a_ref[...], wa_ref[...], NN)
        yb = _dot(b_ref[...], wb_ref[...], NN)
        ya_ref[...] = ya
        yb_ref[...] = yb
        o_ref[...] = (_sigmoid(ga_ref[...]) * ya + _sigmoid(gb_ref[...]) * yb).astype(BF16)

    rows = _row_spec(tm, d)
    return pl.pallas_call(
        body, name=name, grid=(s // tm,),
        in_specs=[rows, rows, _full_spec((d, d)), _full_spec((d, d)), _row_spec(tm, d, 2), _row_spec(tm, d, 3)],
        out_specs=[rows, rows, rows],
        out_shape=[jax.ShapeDtypeStruct((s, d), F32), jax.ShapeDtypeStruct((s, d), F32), jax.ShapeDtypeStruct((s, d), BF16)],
        compiler_params=_cparams(("parallel",)),
    )(a, b, w_a, w_b, zuvg, zuvg)


def _merge_bwd(dx1b, w_out, ya, yb, zuvg, *, name, tm=512):
    s, d = ya.shape

    def body(dx_ref, w_ref, ya_ref, yb_ref, ga_ref, gb_ref, dya_ref, dyb_ref, dga_ref, dgb_ref):
        dmv = _dot(dx_ref[...], w_ref[...], NT)
        sa = _sigmoid(ga_ref[...])
        sb = _sigmoid(gb_ref[...])
        dya_ref[...] = (dmv * sa).astype(BF16)
        dyb_ref[...] = (dmv * sb).astype(BF16)
        dga_ref[...] = (dmv * ya_ref[...] * (sa * (1.0 - sa))).astype(BF16)
        dgb_ref[...] = (dmv * yb_ref[...] * (sb * (1.0 - sb))).astype(BF16)

    rows = _row_spec(tm, d)
    o = jax.ShapeDtypeStruct((s, d), BF16)
    return pl.pallas_call(
        body, name=name, grid=(s // tm,),
        in_specs=[rows, _full_spec((d, d)), rows, rows, _row_spec(tm, d, 2), _row_spec(tm, d, 3)],
        out_specs=[rows] * 4, out_shape=[o, o, o, o], compiler_params=_cparams(("parallel",)),
    )(dx1b, w_out, ya, yb, zuvg, zuvg)


def _masked_ws(ws_ref, g):
    row = lax.broadcasted_iota(I32, (GBLOCK, GBLOCK), 0)
    col = lax.broadcasted_iota(I32, (GBLOCK, GBLOCK), 1)
    keep = (col // CHUNK) <= (row // CHUNK)
    return jnp.where(keep, ws_ref[g], 0.0), keep


def _layernorm_parts(zv):
    mu = jnp.mean(zv, axis=-1, keepdims=True)
    xc = zv - mu
    rs = lax.rsqrt(jnp.mean(xc * xc, axis=-1, keepdims=True) + EPS)
    return xc * rs, rs


def _gmlp_fwd(zuvg, ln_g, ln_b, w_s, bs_t, *, name):
    s, w = zuvg.shape[0], GROUPS * GDIM

    def body(zu_ref, zv_ref, lng_ref, lnb_ref, ws_ref, bs_ref, a_ref):
        zu = _gelu(zu_ref[...])
        zv = _gelu(zv_ref[...])
        xhat, _ = _layernorm_parts(zv)
        vln = (xhat * lng_ref[...] + lnb_ref[...]).astype(BF16)
        for g in range(GROUPS):
            wm, _ = _masked_ws(ws_ref, g)
            mixed = _dot(wm.astype(BF16), vln[:, g * GDIM:(g + 1) * GDIM], NN) + bs_ref[:, g:g + 1]
            a_ref[:, g * GDIM:(g + 1) * GDIM] = (zu[:, g * GDIM:(g + 1) * GDIM] * mixed).astype(BF16)

    return pl.pallas_call(
        body, name=name, grid=(s // GBLOCK,),
        in_specs=[_row_spec(GBLOCK, w, 0), _row_spec(GBLOCK, w, 1), _full_spec((1, w)), _full_spec((1, w)),
                  _full_spec((GROUPS, GBLOCK, GBLOCK)), _full_spec((GBLOCK, 128))],
        out_specs=_row_spec(GBLOCK, w),
        out_shape=jax.ShapeDtypeStruct((s, w), BF16), compiler_params=_cparams(("parallel",)),
    )(zuvg, zuvg, ln_g, ln_b, w_s, bs_t)


def _gmlp_bwd(da, zuvg, ln_g, ln_b, w_s, bs_t, carry=None, *, name):
    s, w = zuvg.shape[0], GROUPS * GDIM

    def body(da_ref, zu_ref, zv_ref, lng_ref, lnb_ref, ws_ref, bs_ref,
             dzu_ref, dzv_ref, dws_ref, dbs_ref, dlng_ref, dlnb_ref, dvln_ref):
        i = pl.program_id(0)
        zu, dzu_g = _gelu_and_grad(zu_ref[...])
        zv, dzv_g = _gelu_and_grad(zv_ref[...])
        xhat, rs = _layernorm_parts(zv)
        vln = (xhat * lng_ref[...] + lnb_ref[...]).astype(BF16)
        dav = da_ref[...].astype(F32)
        lane = lax.broadcasted_iota(I32, (GBLOCK, 128), 1)
        dbs = jnp.zeros((GBLOCK, 128), F32)

        @pl.when(i == 0)
        def _():
            dws_ref[...] = jnp.zeros_like(dws_ref)

        for g in range(GROUPS):
            sl = slice(g * GDIM, (g + 1) * GDIM)
            wm, keep = _masked_ws(ws_ref, g)
            wmb = wm.astype(BF16)
            vg = vln[:, sl]
            mixed = _dot(wmb, vg, NN) + bs_ref[:, g:g + 1]
            dag = dav[:, sl]
            dzu_ref[:, sl] = (dag * mixed * dzu_g[:, sl]).astype(BF16)
            dmix = dag * zu[:, sl]
            dmb = dmix.astype(BF16)
            dws_ref[g] += jnp.where(keep, _dot(dmb, vg, NT), 0.0)
            dbs = jnp.where(lane == g, jnp.sum(dmix, axis=1, keepdims=True), dbs)
            dvln_ref[:, sl] = _dot(wmb, dmb, TN)
        dvln = dvln_ref[...]
        dxhat = dvln * lng_ref[...]
        dzv = rs * (dxhat - jnp.mean(dxhat, axis=-1, keepdims=True)
                    - xhat * jnp.mean(dxhat * xhat, axis=-1, keepdims=True))
        dzv_ref[...] = (dzv * dzv_g).astype(BF16)
        dlng = jnp.sum(dvln * xhat, axis=0, keepdims=True)
        dlnb = jnp.sum(dvln, axis=0, keepdims=True)

        @pl.when(i == 0)
        def _():
            dbs_ref[...] = dbs
            dlng_ref[...] = dlng
            dlnb_ref[...] = dlnb

        @pl.when(i > 0)
        def _():
            dbs_ref[...] += dbs
            dlng_ref[...] += dlng
            dlnb_ref[...] += dlnb

    return _carry_call(
        body, carry, name=name, grid=(s // GBLOCK,),
        in_specs=[_row_spec(GBLOCK, w), _row_spec(GBLOCK, w, 0), _row_spec(GBLOCK, w, 1), _full_spec((1, w)),
                  _full_spec((1, w)), _full_spec((GROUPS, GBLOCK, GBLOCK)), _full_spec((GBLOCK, 128))],
        out_specs=[_row_spec(GBLOCK, w), _row_spec(GBLOCK, w), _full_spec((GROUPS, GBLOCK, GBLOCK)),
                   _full_spec((GBLOCK, 128)), _full_spec((1, w)), _full_spec((1, w))],
        out_shape=[jax.ShapeDtypeStruct((s, w), BF16), jax.ShapeDtypeStruct((s, w), BF16),
                   jax.ShapeDtypeStruct((GROUPS, GBLOCK, GBLOCK), F32), jax.ShapeDtypeStruct((GBLOCK, 128), F32),
                   jax.ShapeDtypeStruct((1, w), F32), jax.ShapeDtypeStruct((1, w), F32)],
        scratch_shapes=[pltpu.VMEM((GBLOCK, w), F32)], args=[da, zuvg, zuvg, ln_g, ln_b, w_s, bs_t])


def _shift_down(u, k):
    row = lax.broadcasted_iota(I32, u.shape, 0)
    return jnp.where(row >= k, pltpu.roll(u, k, 0), 0.0)


def _shift_up(u, k):
    s = u.shape[0]
    row = lax.broadcasted_iota(I32, u.shape, 0)
    return jnp.where(row < s - k, pltpu.roll(u, s - k, 0), 0.0)


def _conv(u, w_ref, b_ref):
    return b_ref[...] + w_ref[0:1, :] * _shift_down(u, 2) + w_ref[1:2, :] * _shift_down(u, 1) + w_ref[2:3, :] * u


def _conv_specs(s, f, tc):
    nc = f // tc
    half = lambda rows: [pl.BlockSpec((rows, tc), lambda j: (0, j)), pl.BlockSpec((rows, tc), lambda j: (0, nc + j))]
    return half(s), half(3), half(1)


def _up_convglu(h2, wt_up, conv_w, conv_b, *, name, tc=256):
    s, d = h2.shape
    f = wt_up.shape[0] // 2
    nc = f // tc
    _, w_specs, b_specs = _conv_specs(s, f, tc)

    def body(h_ref, ta_ref, tg_ref, wa_ref, wg_ref, ba_ref, bg_ref, ua_ref, ug_ref, o_ref):
        ua = _dot(h_ref[...], ta_ref[...], NT)
        ua_ref[...] = ua
        ga = _gelu(_conv(ua, wa_ref, ba_ref))
        ug = _dot(h_ref[...], tg_ref[...], NT)
        ug_ref[...] = ug
        o_ref[...] = (ga * _conv(ug, wg_ref, bg_ref)).astype(BF16)

    col = pl.BlockSpec((s, tc), lambda j: (0, j))
    return pl.pallas_call(
        body, name=name, grid=(nc,),
        in_specs=[_full_spec((s, d)), pl.BlockSpec((tc, d), lambda j: (j, 0)), pl.BlockSpec((tc, d), lambda j: (nc + j, 0))]
        + w_specs + b_specs,
        out_specs=[col, col, col],
        out_shape=[jax.ShapeDtypeStruct((s, f), F32), jax.ShapeDtypeStruct((s, f), F32), jax.ShapeDtypeStruct((s, f), BF16)],
        compiler_params=_cparams(("parallel",)),
    )(h2, wt_up, wt_up, conv_w, conv_w, conv_b, conv_b)


def _convglu_bwd(dact, up_a, up_g, conv_w, conv_b, *, name, tc=256):
    s, f = up_a.shape
    _, w_specs, b_specs = _conv_specs(s, f, tc)
    up_specs = [pl.BlockSpec((s, tc), lambda j: (0, j))] * 2

    def half(dc, taps, w_ref, du_ref, dw_ref, db_ref):
        db_ref[...] = jnp.sum(dc, axis=0, keepdims=True)
        for k in range(3):
            dw_ref[k:k + 1, :] = jnp.sum(dc * taps[k], axis=0, keepdims=True)
        du = w_ref[2:3, :] * dc + w_ref[1:2, :] * _shift_up(dc, 1) + w_ref[0:1, :] * _shift_up(dc, 2)
        du_ref[...] = du.astype(BF16)

    def body(d_ref, ua_ref, ug_ref, wa_ref, wg_ref, ba_ref, bg_ref,
             dua_ref, dug_ref, dwa_ref, dwg_ref, dba_ref, dbg_ref):
        taps_a = (_shift_down(ua_ref[...], 2), _shift_down(ua_ref[...], 1), ua_ref[...])
        taps_g = (_shift_down(ug_ref[...], 2), _shift_down(ug_ref[...], 1), ug_ref[...])
        conv = lambda taps, w_ref, b_ref: b_ref[...] + w_ref[0:1, :] * taps[0] + w_ref[1:2, :] * taps[1] + w_ref[2:3, :] * taps[2]
        ca = conv(taps_a, wa_ref, ba_ref)
        cg = conv(taps_g, wg_ref, bg_ref)
        ga, dga = _gelu_and_grad(ca)
        dv = d_ref[...].astype(F32)
        half(dv * cg * dga, taps_a, wa_ref, dua_ref, dwa_ref, dba_ref)
        half(dv * ga, taps_g, wg_ref, dug_ref, dwg_ref, dbg_ref)

    col, w3, b1 = up_specs[0], w_specs[0], b_specs[0]
    return pl.pallas_call(
        body, name=name, grid=(f // tc,),
        in_specs=[col] + up_specs + w_specs + b_specs, out_specs=[col, col, w3, w3, b1, b1],
        out_shape=[jax.ShapeDtypeStruct((s, f), BF16), jax.ShapeDtypeStruct((s, f), BF16),
                   jax.ShapeDtypeStruct((3, f), F32), jax.ShapeDtypeStruct((3, f), F32),
                   jax.ShapeDtypeStruct((1, f), F32), jax.ShapeDtypeStruct((1, f), F32)],
        compiler_params=_cparams(("parallel",)),
    )(dact, up_a, up_g, conv_w, conv_w, conv_b, conv_b)


def _tri_dot(tri, x):
    b0 = x.astype(BF16)
    r1 = x - b0.astype(F32)
    b1 = r1.astype(BF16)
    b2 = (r1 - b1.astype(F32)).astype(BF16)
    return _dot(tri, b0, NN) + _dot(tri, b1, NN) + _dot(tri, b2, NN)


def _log_sigmoid(x):
    return jnp.minimum(x, 0.0) - jnp.log(1.0 + jnp.exp(-jnp.abs(x)))


def _expand_heads(col16, rows):
    src = lax.broadcasted_iota(I32, (128, HEADS * HEAD_DIM), 0)
    dst = lax.broadcasted_iota(I32, (128, HEADS * HEAD_DIM), 1) // HEAD_DIM
    spread = (src == dst).astype(BF16)
    p0, p1, p2 = _bf16_pieces(col16)
    return (_dot(p0.astype(BF16), spread, NN) + _dot(p1.astype(BF16), spread, NN)) + _dot(p2.astype(BF16), spread, NN)


def _forget_cumsum(f_logit, b_f, *, name):
    s = f_logit.shape[0]
    nb = s // 128

    def body(f_ref, b_ref, cqe_ref):
        row = lax.broadcasted_iota(I32, (128, 128), 0)
        col = lax.broadcasted_iota(I32, (128, 128), 1)
        tri = (col <= row).astype(BF16)

        def step(n, carry):
            r0 = pl.multiple_of(n * 128, 128)
            lf = _log_sigmoid(f_ref[pl.ds(r0, 128), :] + b_ref[...])
            cum = _tri_dot(tri, lf) + carry
            cqe_ref[pl.ds(r0, 128), :] = _expand_heads(cum, 128)
            return cum[127:128, :]

        lax.fori_loop(0, nb, step, jnp.zeros((1, 128), F32))

    return pl.pallas_call(
        body, name=name, grid=(1,),
        in_specs=[_full_spec((s, 128)), _full_spec((1, 128))],
        out_specs=_full_spec((s, HEADS * HEAD_DIM)),
        out_shape=jax.ShapeDtypeStruct((s, HEADS * HEAD_DIM), F32),
        compiler_params=_cparams(("arbitrary",)),
    )(f_logit, b_f)


def _forget_bwd(dcq16, sum_q16, f_logit, b_f, *, name):
    s = f_logit.shape[0]
    nb = s // 128

    def body(a_ref, k_ref, f_ref, b_ref, df_ref, db_ref):
        row = lax.broadcasted_iota(I32, (128, 128), 0)
        col = lax.broadcasted_iota(I32, (128, 128), 1)
        tri_rev = (col >= row).astype(BF16)

        def step(m, carry):
            suffix, dbsum = carry
            n = nb - 1 - m
            r0 = pl.multiple_of(n * 128, 128)
            dcum = a_ref[pl.ds(r0, 128), :] - k_ref[pl.ds(r0, 128), :]
            dlf = _tri_dot(tri_rev, dcum) + suffix
            df = dlf * _sigmoid(-(f_ref[pl.ds(r0, 128), :] + b_ref[...]))
            df_ref[pl.ds(r0, 128), :] = df.astype(BF16)
            return dlf[0:1, :], dbsum + jnp.sum(df, axis=0, keepdims=True)

        _, dbsum = lax.fori_loop(0, nb, step, (jnp.zeros((1, 128), F32), jnp.zeros((1, 128), F32)))
        db_ref[...] = dbsum

    return pl.pallas_call(
        body, name=name, grid=(1,),
        in_specs=[_full_spec((s, 128))] * 3 + [_full_spec((1, 128))],
        out_specs=[_full_spec((s, 128)), _full_spec((1, 128))],
        out_shape=[jax.ShapeDtypeStruct((s, 128), BF16), jax.ShapeDtypeStruct((1, 128), F32)],
        compiler_params=_cparams(("arbitrary",)),
    )(dcq16, sum_q16, f_logit, b_f)


ATT_T = 256


def _head_lanes(rows):
    return lax.broadcasted_iota(I32, (rows, 128), 1) < HEAD_DIM


def _bf16_pieces(c):
    p0 = c.astype(BF16).astype(F32)
    r = c - p0
    p1 = r.astype(BF16).astype(F32)
    p2 = (r - p1).astype(BF16).astype(F32)
    return p0, p1, p2


def _col_reduce(x, op):
    rows = x.shape[0]
    while rows > 8:
        rows //= 2
        x = op(x[:rows], x[rows:])
    return jnp.max(x, axis=0, keepdims=True) if op is jnp.maximum else jnp.sum(x, axis=0, keepdims=True)


def _attn_prep(qkv, cqe, carry=None, *, name):
    s = qkv.shape[0]
    npair = HEADS // 2

    def body(q_ref, k_ref, v_ref, c_ref, qa_ref, ka_ref, vt_ref):
        rows = 128
        lane = lax.broadcasted_iota(I32, (rows, 128), 1)

        def chunk(n, _):
            r0 = pl.multiple_of(n * rows, rows)
            sl = pl.ds(r0, rows)
            qv = q_ref[sl, :].astype(F32) * ATT_SCALE
            kv = k_ref[sl, :].astype(F32)
            p0, p1, p2 = _bf16_pieces(pltpu.roll(c_ref[sl, :], HEAD_DIM, 1))
            for e in range(2):
                mine = (lane < HEAD_DIM) if e == 0 else (lane >= HEAD_DIM)
                base = HEAD_DIM * (1 - e)
                ones_hi = jnp.where((lane >= base + 3) & (lane < base + 6), 1.0, 0.0)
                ones_lo = jnp.where((lane >= base) & (lane < base + 3), 1.0, 0.0)
                qa = jnp.where(mine, qv, jnp.where(lane == base, p0, jnp.where(lane == base + 1, p1,
                               jnp.where(lane == base + 2, p2, ones_hi))))
                ka = jnp.where(mine, kv, jnp.where(lane == base + 3, -p0, jnp.where(lane == base + 4, -p1,
                               jnp.where(lane == base + 5, -p2, ones_lo))))
                qa_ref[e, sl, :] = qa.astype(BF16)
                ka_ref[e, sl, :] = ka.astype(BF16)
            vt_ref[0, :, sl] = v_ref[sl, :].astype(F32).T.astype(BF16)
            return 0

        lax.fori_loop(0, s // rows, chunk, 0)

    pair = pl.BlockSpec((2, s, 128), lambda hp: (hp, 0, 0))
    return _carry_call(
        body, carry, name=name, grid=(npair,),
        in_specs=[pl.BlockSpec((s, 128), lambda hp: (0, hp)), pl.BlockSpec((s, 128), lambda hp: (0, npair + hp)),
                  pl.BlockSpec((s, 128), lambda hp: (0, 2 * npair + hp)), pl.BlockSpec((s, 128), lambda hp: (0, hp))],
        out_specs=[pair, pair, pl.BlockSpec((1, 128, s), lambda hp: (hp, 0, 0))],
        out_shape=[jax.ShapeDtypeStruct((HEADS, s, 128), BF16), jax.ShapeDtypeStruct((HEADS, s, 128), BF16),
                   jax.ShapeDtypeStruct((npair, 128, s), BF16)],
        scratch_shapes=[], args=[qkv, qkv, qkv, cqe])


def _attn_fwd(qa, ka, vt, carry=None, *, name):
    s = qa.shape[1]
    t = 2 * ATT_T
    nq = s // t
    npair = HEADS // 2

    def body(qa_ref, ka_ref, vt_ref, o_ref, lse_ref):
        i = pl.program_id(1)
        krow = lax.broadcasted_iota(I32, (t, t), 0)
        qcol = lax.broadcasted_iota(I32, (t, t), 1)
        sub = lax.broadcasted_iota(I32, (128, t), 0)
        row8 = lax.broadcasted_iota(I32, (8, t), 0)
        qbs = (qa_ref[0], qa_ref[1])
        tk = t

        def step(j, carry, diag):
            c0 = pl.multiple_of(j * tk, tk)
            vtb = vt_ref[0, :, pl.ds(c0, tk)]
            sts = [_dot(ka_ref[e, pl.ds(c0, tk), :], qbs[e], NT) for e in range(2)]
            if diag:
                sts = [jnp.where(krow <= qcol, st, NEG) for st in sts]
            pts, stats = [], []
            for e in range(2):
                m, l, _ = carry[e]
                m_new = jnp.maximum(m, _col_reduce(sts[e], jnp.maximum))
                alpha = jnp.exp(m - m_new)
                pt = jnp.exp(sts[e] - m_new)
                stats.append((m_new, alpha, alpha * l + _col_reduce(pt, jnp.add)))
                pts.append(pt.astype(BF16))
            pvs = [_dot(vtb, pts[e], NN) for e in range(2)]
            return tuple((stats[e][0], stats[e][2], stats[e][1] * carry[e][2] + pvs[e]) for e in range(2))

        init = (jnp.full((1, t), NEG, F32), jnp.zeros((1, t), F32), jnp.zeros((128, t), F32))
        carry = lax.fori_loop(0, i, functools.partial(step, diag=False), (init, init))
        (m0, l0, acc0), (m1, l1, acc1) = step(i, carry, True)
        o_pair = jnp.where(sub < HEAD_DIM, acc0 / l0, acc1 / l1)
        o_ref[...] = o_pair.T.astype(BF16)
        lse_ref[0] = jnp.where(row8 == 0, m0 + jnp.log(l0), jnp.where(row8 == 1, m1 + jnp.log(l1), 0.0))

    return _carry_call(
        body, carry, name=name, grid=(npair, nq),
        in_specs=[pl.BlockSpec((2, t, 128), lambda hp, i: (hp, i, 0)), pl.BlockSpec((2, s, 128), lambda hp, i: (hp, 0, 0)),
                  pl.BlockSpec((1, 128, s), lambda hp, i: (hp, 0, 0))],
        out_specs=[pl.BlockSpec((t, 128), lambda hp, i: (i, hp)), pl.BlockSpec((1, 8, t), lambda hp, i: (hp, 0, i))],
        out_shape=[jax.ShapeDtypeStruct((s, HEADS * HEAD_DIM), BF16), jax.ShapeDtypeStruct((npair, 8, s), F32)],
        scratch_shapes=[], args=[qa, ka, vt])


def _attn_delta(do, o, carry=None, *, name):
    s = do.shape[0]

    def body(do_ref, o_ref, d_ref):
        prod = do_ref[...].astype(F32) * o_ref[...].astype(F32)
        row = lax.broadcasted_iota(I32, (8, 128), 0)
        lane = lax.broadcasted_iota(I32, (8, 128), 1)
        sel = ((row == 0) & (lane < HEAD_DIM) | (row == 1) & (lane >= HEAD_DIM)).astype(BF16)
        p0, p1, p2 = _bf16_pieces(prod)
        d_ref[0] = (_dot(sel, p0.astype(BF16), NT) + _dot(sel, p1.astype(BF16), NT)) + _dot(sel, p2.astype(BF16), NT)

    pair = pl.BlockSpec((s, 128), lambda hp: (0, hp))
    (delta3,), carried = _carry_call(
        body, carry, name=name, grid=(HEADS // 2,), in_specs=[pair, pair],
        out_specs=[pl.BlockSpec((1, 8, s), lambda hp: (hp, 0, 0))],
        out_shape=[jax.ShapeDtypeStruct((HEADS // 2, 8, s), F32)], scratch_shapes=[], args=[do, o])
    return delta3, carried


def _attn_bwd(qa, ka, qkv, do, lse3, delta3, carry=None, *, name):
    s = qa.shape[1]
    t = 2 * ATT_T
    nb = s // t
    npair = HEADS // 2

    def body(qa_ref, ka_ref, v_ref, do_ref, lse_ref, delta_ref, dq_ref, dk_ref, dv_ref, aux_ref, dcq_ref, dqt):
        hp = pl.program_id(0)
        first = _head_lanes(t)
        lane = lax.broadcasted_iota(I32, (t, 128), 1)
        dqt[...] = jnp.zeros_like(dqt)

        @pl.when(hp == 0)
        def _():
            aux_ref[...] = jnp.zeros_like(aux_ref)

        krow = lax.broadcasted_iota(I32, (t, t), 0)
        qcol = lax.broadcasted_iota(I32, (t, t), 1)

        def key_block(j, _):
            c0 = pl.multiple_of(j * t, t)
            vb = v_ref[pl.ds(c0, t), :]
            kbs = (ka_ref[0, pl.ds(c0, t), :], ka_ref[1, pl.ds(c0, t), :])
            kbts = tuple(kb.astype(F32).T.astype(BF16) for kb in kbs)
            vhs = (jnp.where(first, vb, jnp.zeros_like(vb)), jnp.where(first, jnp.zeros_like(vb), vb))

            def query_block(i, carry, diag):
                r0 = pl.multiple_of(i * t, t)
                dob = do_ref[pl.ds(r0, t), :]
                sts = [_dot(kbs[e], qa_ref[e, pl.ds(r0, t), :], NT) for e in range(2)]
                dpts = [_dot(vhs[e], dob, NT) for e in range(2)]
                ptbs, dsbs = [], []
                for e in range(2):
                    st = jnp.where(krow <= qcol, sts[e], NEG) if diag else sts[e]
                    pt = jnp.exp(st - lse_ref[0, e:e + 1, pl.ds(r0, t)])
                    dsbs.append((pt * (dpts[e] - delta_ref[0, e:e + 1, pl.ds(r0, t)])).astype(BF16))
                    ptbs.append(pt.astype(BF16))
                out = []
                for e in range(2):
                    dk_a, dv_a = carry[e]
                    dv_a = dv_a + _dot(ptbs[e], dob, NN)
                    dk_a = dk_a + _dot(dsbs[e], qa_ref[e, pl.ds(r0, t), :], NN)
                    dqt[e, :, pl.ds(r0, t)] += _dot(kbts[e], dsbs[e], NN)
                    out.append((dk_a, dv_a))
                return tuple(out)

            zero = jnp.zeros((t, 128), F32)
            carry = query_block(j, ((zero, zero), (zero, zero)), True)
            (dk0, dv0), (dk1, dv1) = lax.fori_loop(j + 1, nb, functools.partial(query_block, diag=False), carry)
            dk_ref[pl.ds(c0, t), :] = jnp.where(first, dk0, dk1).astype(BF16)
            dv_ref[pl.ds(c0, t), :] = jnp.where(first, dv0, dv1).astype(BF16)
            sum_q = jnp.where(lane == 2 * hp, dk0[:, HEAD_DIM + 3:HEAD_DIM + 4],
                              jnp.where(lane == 2 * hp + 1, dk1[:, 3:4], aux_ref[pl.ds(c0, t), :]))
            aux_ref[pl.ds(c0, t), :] = sum_q
            return 0

        lax.fori_loop(0, nb, key_block, 0)
        sub = lax.broadcasted_iota(I32, (128, s), 0)
        row8 = lax.broadcasted_iota(I32, (8, s), 0)
        dq_ref[...] = (jnp.where(sub < HEAD_DIM, dqt[0], dqt[1]) * ATT_SCALE).T.astype(BF16)
        dcq_ref[0] = jnp.where(row8 == 0, dqt[0, HEAD_DIM:HEAD_DIM + 1, :], jnp.where(row8 == 1, dqt[1, 0:1, :], 0.0))

    def pair_cols(off):
        return pl.BlockSpec((s, 128), lambda hp: (0, off + hp))

    heads = pl.BlockSpec((2, s, 128), lambda hp: (hp, 0, 0))
    rows = pl.BlockSpec((1, 8, s), lambda hp: (hp, 0, 0))
    wide = jax.ShapeDtypeStruct((s, HEADS * HEAD_DIM), BF16)
    return _carry_call(
        body, carry, name=name, grid=(npair,),
        in_specs=[heads, heads, pair_cols(2 * npair), pair_cols(0), rows, rows],
        out_specs=[pair_cols(0), pair_cols(0), pair_cols(0), pl.BlockSpec((s, 128), lambda hp: (0, 0)), rows],
        out_shape=[wide, wide, wide, jax.ShapeDtypeStruct((s, 128), F32), jax.ShapeDtypeStruct((npair, 8, s), F32)],
        scratch_shapes=[pltpu.VMEM((2, 128, s), F32)], args=[qa, ka, qkv, do, lse3, delta3])


def _adam_math(w, g, m, v):
    m = ADAM_B1 * m + (1.0 - ADAM_B1) * g
    v = ADAM_B2 * v + (1.0 - ADAM_B2) * (g * g)
    m_hat = m / (1.0 - ADAM_B1 ** ADAM_STEP)
    v_hat = v / (1.0 - ADAM_B2 ** ADAM_STEP)
    delta = -ADAM_LR * (m_hat / (jnp.sqrt(v_hat) + ADAM_EPS) + ADAM_WD * w)
    return delta, m, v


def _sum_pairs(keep, recv, pos, *, name):
    _, r, c = recv.shape
    tr = _row_tile(r, 1024)

    def body(pos_ref, a_ref, b_ref, o32_ref, o16_ref):
        tot = a_ref[...].astype(F32) + b_ref[...].astype(F32)
        o16_ref[...] = tot.astype(BF16)

        @pl.when(pl.program_id(1) == 2 * pos_ref[0] + pos_ref[1])
        def _():
            o32_ref[...] = tot

    out = pl.BlockSpec((1, tr, c), lambda i, q, pos: (q, i, 0))
    grid_spec = pltpu.PrefetchScalarGridSpec(
        num_scalar_prefetch=1, grid=(r // tr, 4),
        in_specs=[pl.BlockSpec((1, tr, c), lambda i, q, pos: (2 * q + pos[2], i, 0)), out],
        out_specs=[pl.BlockSpec((1, tr, c), lambda i, q, pos: (0, i, 0)), out])
    return pl.pallas_call(
        body, name=name, grid_spec=grid_spec,
        out_shape=[jax.ShapeDtypeStruct((1, r, c), F32), jax.ShapeDtypeStruct((4, r, c), BF16)],
        compiler_params=_cparams(("arbitrary", "arbitrary")),
    )(pos, keep, recv)


def _adam_sharded(psum, recv, w, m, v, pos, *, name):
    r, c = w.shape
    rg = psum.shape[1]

    def body(pos_ref, p_ref, r_ref, w_ref, m_ref, v_ref, g_ref, d_ref, mo_ref, vo_ref):
        part = lambda ref, q: ref[q] if rg == r else ref[q, :r, :]
        g = part(p_ref, 0) + part(r_ref, 0).astype(F32) + part(r_ref, 1).astype(F32) + part(r_ref, 2).astype(F32)
        delta, mn, vn = _adam_math(w_ref[...], g, m_ref[...], v_ref[...])
        g_ref[...] = g
        d_ref[...] = delta
        mo_ref[...] = mn
        vo_ref[...] = vn

    if rg == r:
        tr = _row_tile(r, 320)
        grid = (r // tr,)
        row = pl.BlockSpec((tr, c), lambda i, pos: (i, 0))
        sums = lambda n: pl.BlockSpec((n, tr, c), lambda i, pos: (0, i, 0))
    else:
        tc = 256
        grid = (c // tc,)
        row = pl.BlockSpec((r, tc), lambda i, pos: (0, i))
        sums = lambda n: pl.BlockSpec((n, rg, tc), lambda i, pos: (0, 0, i))
    grid_spec = pltpu.PrefetchScalarGridSpec(
        num_scalar_prefetch=1, grid=grid, in_specs=[sums(1), sums(3), row, row, row], out_specs=[row, row, row, row])
    o = jax.ShapeDtypeStruct((r, c), F32)
    return pl.pallas_call(
        body, name=name, grid_spec=grid_spec, out_shape=[o, o, o, o],
        compiler_params=_cparams(("parallel",)),
    )(pos, psum, recv, w, m, v)


def _adam_replicated(chip_sums, last, w, m, v, *, name):
    r = w.shape[0]

    def body(s_ref, l_ref, w_ref, m_ref, v_ref, g_ref, d_ref, mo_ref, vo_ref):
        g = (((s_ref[0] + s_ref[1]) + s_ref[2]) + s_ref[3]) + l_ref[...]
        delta, mn, vn = _adam_math(w_ref[...], g, m_ref[...], v_ref[...])
        g_ref[...] = g
        d_ref[...] = delta
        mo_ref[...] = mn
        vo_ref[...] = vn

    o = jax.ShapeDtypeStruct((r, 1024), F32)
    full = _full_spec((r, 1024))
    return pl.pallas_call(
        body, name=name, grid=(1,),
        in_specs=[_full_spec((4, r, 1024)), full, full, full, full], out_specs=[full] * 4, out_shape=[o] * 4,
        compiler_params=_cparams(("arbitrary",)),
    )(chip_sums, last, w, m, v)


ASM_OUT = 256
ASM_SRC = 304


def _w_in_row(r):
    return r if r < 2048 else (r + O_G - 2048 if r < 4096 else r - 2048)


def _assemble_wt_main(g, *, name):
    table = []
    for blk in range(MAIN_COLS // ASM_OUT):
        j, l0 = divmod(_w_in_row(blk * ASM_OUT), IN_SHARD)
        sb = l0 // ASM_SRC
        n_a = min(ASM_OUT, min(IN_SHARD, (sb + 1) * ASM_SRC) - l0)
        if n_a == ASM_OUT:
            nxt = (j, sb)
        elif l0 + n_a == IN_SHARD:
            nxt = (j + 1, 0)
        else:
            nxt = (j, sb + 1)
        table.append((j, sb, l0 - sb * ASM_SRC, n_a) + nxt)

    def body(tab_ref, a_ref, b_ref, o_ref):
        blk = pl.program_id(0)
        off, n_a = tab_ref[blk, 2], tab_ref[blk, 3]
        r = lax.broadcasted_iota(I32, (ASM_OUT, ASM_SRC), 0)
        k = lax.broadcasted_iota(I32, (ASM_OUT, ASM_SRC), 1)
        sel_a = ((k == r + off) & (r < n_a)).astype(BF16)
        sel_b = ((k == r - n_a) & (r >= n_a)).astype(BF16)
        o_ref[...] = (_dot(sel_a, a_ref[0], NN) + _dot(sel_b, b_ref[0], NN)).astype(BF16)

    src = lambda c: pl.BlockSpec((1, ASM_SRC, D_MODEL), lambda blk, tab: (tab[blk, c], tab[blk, c + 1], 0))
    grid_spec = pltpu.PrefetchScalarGridSpec(
        num_scalar_prefetch=1, grid=(len(table),), in_specs=[src(0), src(4)],
        out_specs=pl.BlockSpec((ASM_OUT, D_MODEL), lambda blk, tab: (blk, 0)))
    return pl.pallas_call(
        body, name=name, grid_spec=grid_spec, out_shape=jax.ShapeDtypeStruct((MAIN_COLS, D_MODEL), BF16),
        compiler_params=_cparams(("parallel",)),
    )(jnp.asarray(table, I32), g, g)


def _pair_sum_small(mine, theirs, *, name):
    def body(a_ref, b_ref, o_ref):
        o_ref[...] = a_ref[...] + b_ref[...]

    full = _full_spec(mine.shape)
    return pl.pallas_call(
        body, name=name, grid=(1,), in_specs=[full, full], out_specs=full,
        out_shape=jax.ShapeDtypeStruct(mine.shape, F32), compiler_params=_cparams(("arbitrary",)),
    )(mine, theirs)


ANY = pl.BlockSpec(memory_space=pl.ANY)
OTHER_CHIPS = ((1, 0), (0, 1), (1, 1))


class _Carry:
    def __init__(self, inputs, out_shapes, scratch, start, wait, aliases=None, middle=None):
        self.inputs, self.out_shapes, self.scratch = list(inputs), list(out_shapes), list(scratch)
        self.start, self.wait, self.aliases, self.middle = start, wait, dict(aliases or {}), middle


def _carry_join(*carries):
    n_in = [len(c.inputs) for c in carries]
    n_out = [len(c.out_shapes) for c in carries]
    n_scr = [len(c.scratch) for c in carries]

    def split(refs, counts):
        out, k = [], 0
        for n in counts:
            out.append(refs[k:k + n])
            k += n
        return out

    def start(ins, outs, scr):
        for c, i, o, s in zip(carries, split(ins, n_in), split(outs, n_out), split(scr, n_scr)):
            c.start(i, o, s)

    def wait(ins, outs, scr):
        for c, i, o, s in zip(carries, split(ins, n_in), split(outs, n_out), split(scr, n_scr)):
            c.wait(i, o, s)

    def middle(ins, outs, scr):
        for c, i, o, s in zip(carries, split(ins, n_in), split(outs, n_out), split(scr, n_scr)):
            if c.middle is not None:
                c.middle(i, o, s)

    aliases = {}
    for k, c in enumerate(carries):
        aliases.update({sum(n_in[:k]) + i: sum(n_out[:k]) + o for i, o in c.aliases.items()})
    joined = _Carry(sum((c.inputs for c in carries), []), sum((c.out_shapes for c in carries), []),
                    sum((c.scratch for c in carries), []), start, wait, aliases,
                    middle if any(c.middle is not None for c in carries) else None)
    joined.counts = n_out
    joined.split = lambda results: split(results, n_out)
    return joined


def _carried(body, carry, n_in, n_out, grid):
    if carry is None:
        return body
    ci, co, cs = len(carry.inputs), len(carry.out_shapes), len(carry.scratch)

    def wrapped(*refs):
        ins, cins = refs[:n_in], refs[n_in:n_in + ci]
        outs, couts = refs[n_in + ci:n_in + ci + n_out], refs[n_in + ci + n_out:n_in + ci + n_out + co]
        rest = refs[n_in + ci + n_out + co:]
        scratch, cscr = rest[:len(rest) - cs], rest[len(rest) - cs:]
        first, last, step, steps = None, None, 0, 1
        for axis, size in enumerate(grid):
            f, l = pl.program_id(axis) == 0, pl.program_id(axis) == size - 1
            first = f if first is None else first & f
            last = l if last is None else last & l
            step, steps = step * size + pl.program_id(axis), steps * size

        @pl.when(first)
        def _():
            carry.start(cins, couts, cscr)

        if carry.middle is not None:
            @pl.when(step == steps // 2)
            def _():
                carry.middle(cins, couts, cscr)

        body(*ins, *outs, *scratch)

        @pl.when(last)
        def _():
            carry.wait(cins, couts, cscr)

    return wrapped


def _carry_call(body, carry, *, name, grid, in_specs, out_specs, out_shape, scratch_shapes, args, vmem=True,
                own_aliases=None):
    n_in, n_out = len(in_specs), len(out_specs)
    extra_in = [ANY] * len(carry.inputs) if carry else []
    extra_out = [ANY] * len(carry.out_shapes) if carry else []
    aliases = dict(own_aliases or {})
    if carry:
        aliases.update({n_in + i: n_out + o for i, o in carry.aliases.items()})
    out = pl.pallas_call(
        _carried(body, carry, n_in, n_out, grid), name=name, grid=grid,
        in_specs=list(in_specs) + extra_in, out_specs=list(out_specs) + extra_out,
        out_shape=list(out_shape) + (carry.out_shapes if carry else []),
        scratch_shapes=list(scratch_shapes) + (carry.scratch if carry else []),
        input_output_aliases=aliases,
        compiler_params=_cparams(("arbitrary",) * len(grid)) if vmem else None,
    )(*args, *(carry.inputs if carry else []))
    return list(out[:n_out]), list(out[n_out:])


def _run_carry(carry, *, name):
    return _carry_call(lambda: None, carry, name=name, grid=(1,), in_specs=[], out_specs=[], out_shape=[],
                       scratch_shapes=[], args=[], vmem=False)[1]


def _sems(n):
    return [pltpu.SemaphoreType.DMA((n,)), pltpu.SemaphoreType.DMA((n,))]


def _carry_gather1(shards):
    n = len(shards)
    per = 7

    def plan(x_refs, out_refs, scr):
        send_sems, recv_sems, local_sems = scr
        x, y, c = lax.axis_index("x"), lax.axis_index("y"), lax.axis_index("c")
        me, sibling = (x, y, c), (x, y, 1 - c)
        near_x, near_y, across = (1 - x, y, c), (x, 1 - y, c), (1 - x, 1 - y, c)

        def rows(ref, t, half):
            r = shards[t].shape[0]
            h = r if r < 32 else -(-(r // 2) // 16) * 16
            if half is None or h == r:
                return ref
            return ref.at[pl.ds(0, h)] if half == 0 else ref.at[pl.ds(h, r - h)]

        def copy(t, k, block, half, to, from_input=False):
            px, py, pc = block
            slab = rows(out_refs[t].at[4 * px + 2 * py + pc], t, half)
            return pltpu.make_async_remote_copy(
                src_ref=rows(x_refs[t], t, half) if from_input else slab, dst_ref=slab,
                send_sem=send_sems.at[per * t + k], recv_sem=recv_sems.at[per * t + k], device_id=to,
                device_id_type=MESH)

        two = [shards[t].shape[0] >= 32 for t in range(n)]
        local = lambda t: pltpu.make_async_copy(x_refs[t], out_refs[t].at[4 * x + 2 * y + c], local_sems.at[t])
        first = lambda t: ([(0, me, None, sibling), (1, me, 0, near_x)]
                           + ([(2, me, 1, near_y), (3, me, 1, near_x)] if two[t] else []) + [(4, me, 0, near_y)])
        passed = lambda t: [(5, near_x, 0, near_y)] + ([(6, near_y, 1, near_x)] if two[t] else [])
        early = lambda t: [(1, near_x, 0, me)] + ([(2, near_y, 1, me)] if two[t] else [])
        late = lambda t: ([(0, sibling, None, me), (4, near_y, 0, me), (5, across, 0, me)]
                          + ([(3, near_x, 1, me), (6, across, 1, me)] if two[t] else []))
        return copy, local, first, passed, early, late

    def start(x_refs, out_refs, scr):
        copy, local, first, _, _, _ = plan(x_refs, out_refs, scr)
        for urgent in (True, False):
            for t in range(n):
                if not urgent:
                    local(t).start()
                for k, block, half, to in first(t):
                    if (k in (1, 2)) == urgent:
                        copy(t, k, block, half, to, from_input=True).start()

    def middle(x_refs, out_refs, scr):
        copy, _, _, passed, early, _ = plan(x_refs, out_refs, scr)
        for t in range(n):
            for (k, block, half, to), fwd in zip(early(t), passed(t)):
                copy(t, k, block, half, to).wait_recv()
                copy(t, *fwd).start()

    def wait(x_refs, out_refs, scr):
        copy, local, first, passed, _, late = plan(x_refs, out_refs, scr)
        for t in range(n):
            for k, block, half, to in late(t):
                copy(t, k, block, half, to).wait_recv()
        for t in range(n):
            for k, block, half, to in first(t):
                copy(t, k, block, half, to, from_input=True).wait_send()
            for k, block, half, to in passed(t):
                copy(t, k, block, half, to).wait_send()
            local(t).wait()

    return _Carry(shards, [jax.ShapeDtypeStruct((N_DEV,) + a.shape, a.dtype) for a in shards],
                  _sems(per * n) + [pltpu.SemaphoreType.DMA((n,))], start, wait, middle=middle)


def _carry_gather2(gathered):
    n = len(gathered)

    def copies(in_refs, g_refs, scr, with_arrivals):
        send_sems, recv_sems = scr
        x, y, c = lax.axis_index("x"), lax.axis_index("y"), lax.axis_index("c")
        sends, arrivals = [], []
        for t in range(n):
            for j, (fx, fy) in enumerate(OTHER_CHIPS):
                px, py = x ^ fx, y ^ fy
                sems = dict(send_sem=send_sems.at[3 * t + j], recv_sem=recv_sems.at[3 * t + j],
                            device_id=(x, y, 1 - c), device_id_type=MESH)
                mine, theirs = 4 * px + 2 * py + c, 4 * px + 2 * py + (1 - c)
                sends.append(pltpu.make_async_remote_copy(src_ref=in_refs[t].at[mine], dst_ref=g_refs[t].at[mine], **sems))
                if with_arrivals:
                    arrivals.append(pltpu.make_async_remote_copy(
                        src_ref=in_refs[t].at[mine], dst_ref=g_refs[t].at[theirs], **sems))
        return sends, arrivals

    def start(in_refs, g_refs, scr):
        for cp in copies(in_refs, g_refs, scr, False)[0]:
            cp.start()

    def wait(in_refs, g_refs, scr):
        sends, arrivals = copies(in_refs, g_refs, scr, True)
        for cp in arrivals:
            cp.wait_recv()
        for cp in sends:
            cp.wait_send()

    return _Carry(gathered, [jax.ShapeDtypeStruct(a.shape, a.dtype) for a in gathered], _sems(3 * n), start, wait,
                  aliases={t: t for t in range(n)})


def _allreduce_rows(x, *, name):
    def body(x_ref, o_ref, sib_ref, mine_ref, tab_ref, send_sems, recv_sems):
        x, y, c = lax.axis_index("x"), lax.axis_index("y"), lax.axis_index("c")
        swap = pltpu.make_async_remote_copy(src_ref=x_ref, dst_ref=sib_ref, send_sem=send_sems.at[0],
                                            recv_sem=recv_sems.at[0], device_id=(x, y, 1 - c), device_id_type=MESH)
        swap.start()
        swap.wait()
        mine_ref[...] = x_ref[...] + sib_ref[...]
        tab_ref[pl.ds(2 * x + y, 1)] = mine_ref[...][None]

        def copy(k, slot):
            fx, fy = OTHER_CHIPS[k]
            return pltpu.make_async_remote_copy(
                src_ref=mine_ref, dst_ref=tab_ref.at[slot], send_sem=send_sems.at[1 + k], recv_sem=recv_sems.at[1 + k],
                device_id=(x ^ fx, y ^ fy, c), device_id_type=MESH)

        for k in range(3):
            copy(k, 2 * x + y).start()
        for k, (fx, fy) in enumerate(OTHER_CHIPS):
            copy(k, 2 * (x ^ fx) + (y ^ fy)).wait()
        o_ref[...] = ((tab_ref[0] + tab_ref[1]) + tab_ref[2]) + tab_ref[3]

    vmem = pl.BlockSpec(memory_space=pltpu.VMEM)
    return pl.pallas_call(
        body, name=name, out_shape=jax.ShapeDtypeStruct(x.shape, F32), in_specs=[vmem], out_specs=vmem,
        scratch_shapes=[pltpu.VMEM(x.shape, F32), pltpu.VMEM(x.shape, F32), pltpu.VMEM((4,) + x.shape, F32)] + _sems(4),
    )(x)


def _allgather(shards, *, name):
    n = len(shards)
    per = 10

    def body(*refs):
        x_refs, out_refs = refs[:n], refs[n:2 * n]
        send_sems, recv_sems, local_sems = refs[2 * n:]
        x, y, c = lax.axis_index("x"), lax.axis_index("y"), lax.axis_index("c")
        me, sibling = (x, y, c), (x, y, 1 - c)
        near_x, near_y, across = (1 - x, y), (x, 1 - y), (1 - x, 1 - y)

        def rows(ref, t, half):
            r = shards[t].shape[0]
            h = -(-(r // 2) // 16) * 16
            if half is None:
                return ref
            return ref.at[pl.ds(0, h)] if half == 0 else ref.at[pl.ds(h, r - h)]

        def copy(t, k, block, half, to, from_input=False):
            px, py, pc = block
            slab = rows(out_refs[t].at[4 * px + 2 * py + pc], t, half)
            return pltpu.make_async_remote_copy(
                src_ref=rows(x_refs[t], t, half) if from_input else slab, dst_ref=slab,
                send_sem=send_sems.at[per * t + k], recv_sem=recv_sems.at[per * t + k], device_id=to,
                device_id_type=MESH)

        mine = [pltpu.make_async_copy(x_refs[t], out_refs[t].at[4 * x + 2 * y + c], local_sems.at[t]) for t in range(n)]
        for cp in mine:
            cp.start()
        sent = []

        def send(cp):
            cp.start()
            sent.append(cp)

        for t in range(n):
            send(copy(t, 0, me, None, sibling, from_input=True))
            send(copy(t, 1, me, 0, (*near_x, c), from_input=True))
            send(copy(t, 2, me, 1, (*near_y, c), from_input=True))
            send(copy(t, 3, me, 1, (*near_x, c), from_input=True))
            send(copy(t, 4, me, 0, (*near_y, c), from_input=True))
        for t in range(n):
            copy(t, 1, (*near_x, c), 0, me).wait_recv()
            send(copy(t, 5, (*near_x, c), 0, (*near_y, c)))
            copy(t, 2, (*near_y, c), 1, me).wait_recv()
            send(copy(t, 6, (*near_y, c), 1, (*near_x, c)))
        for t in range(n):
            copy(t, 3, (*near_x, c), 1, me).wait_recv()
            send(copy(t, 7, (*near_x, c), None, sibling))
            copy(t, 4, (*near_y, c), 0, me).wait_recv()
            send(copy(t, 8, (*near_y, c), None, sibling))
            copy(t, 5, (*across, c), 0, me).wait_recv()
            copy(t, 6, (*across, c), 1, me).wait_recv()
            send(copy(t, 9, (*across, c), None, sibling))
        for t in range(n):
            copy(t, 0, sibling, None, me).wait_recv()
            for k, chip in ((7, near_x), (8, near_y), (9, across)):
                copy(t, k, (*chip, 1 - c), None, me).wait_recv()
        for cp in sent:
            cp.wait_send()
        for cp in mine:
            cp.wait()

    return pl.pallas_call(
        body, name=name, out_shape=[jax.ShapeDtypeStruct((N_DEV,) + a.shape, a.dtype) for a in shards],
        in_specs=[ANY] * n, out_specs=[ANY] * n,
        scratch_shapes=[pltpu.SemaphoreType.DMA((per * n,)), pltpu.SemaphoreType.DMA((per * n,)),
                        pltpu.SemaphoreType.DMA((n,))],
    )(*shards)


def _carry_sibling(slabs, small=None):
    n = len(slabs)
    extra = [] if small is None else [small]

    def copies(in_refs, out_refs, scr):
        send_sems, recv_sems = scr
        x, y, c = lax.axis_index("x"), lax.axis_index("y"), lax.axis_index("c")
        sibling = (x, y, 1 - c)
        out = []
        for t in range(n):
            for q in range(4):
                out.append(pltpu.make_async_remote_copy(
                    src_ref=in_refs[t].at[2 * q + (1 - c)], dst_ref=out_refs[t].at[q],
                    send_sem=send_sems.at[4 * t + q], recv_sem=recv_sems.at[4 * t + q],
                    device_id=sibling, device_id_type=MESH))
        if extra:
            out.append(pltpu.make_async_remote_copy(
                src_ref=in_refs[n], dst_ref=out_refs[n], send_sem=send_sems.at[4 * n], recv_sem=recv_sems.at[4 * n],
                device_id=sibling, device_id_type=MESH))
        return out

    def start(*refs):
        for cp in copies(*refs):
            cp.start()

    def wait(*refs):
        for cp in copies(*refs):
            cp.wait()

    return _Carry(list(slabs) + extra,
                  [jax.ShapeDtypeStruct((4,) + a.shape[1:], a.dtype) for a in slabs]
                  + [jax.ShapeDtypeStruct(a.shape, a.dtype) for a in extra], _sems(4 * n + 1), start, wait)


def _carry_chips(psums, small_sum=None):
    n = len(psums)
    table = small_sum is not None

    def copies(in_refs, out_refs, scr, arrivals):
        send_sems, recv_sems = scr[0], scr[1]
        x, y, c = lax.axis_index("x"), lax.axis_index("y"), lax.axis_index("c")
        out = []
        for k, (fx, fy) in enumerate(OTHER_CHIPS):
            px, py = x ^ fx, y ^ fy
            for t in range(n):
                out.append(pltpu.make_async_remote_copy(
                    src_ref=in_refs[t].at[2 * px + py], dst_ref=out_refs[t].at[k],
                    send_sem=send_sems.at[3 * t + k], recv_sem=recv_sems.at[3 * t + k],
                    device_id=(px, py, c), device_id_type=MESH))
            if table:
                slot = 2 * px + py if arrivals else 2 * x + y
                out.append(pltpu.make_async_remote_copy(
                    src_ref=in_refs[n], dst_ref=out_refs[n].at[slot], send_sem=send_sems.at[3 * n + k],
                    recv_sem=recv_sems.at[3 * n + k], device_id=(px, py, c), device_id_type=MESH))
        return out

    def own(in_refs, out_refs, scr):
        x, y = lax.axis_index("x"), lax.axis_index("y")
        return pltpu.make_async_copy(in_refs[n], out_refs[n].at[2 * x + y], scr[2])

    def start(in_refs, out_refs, scr):
        if table:
            own(in_refs, out_refs, scr).start()
        for cp in copies(in_refs, out_refs, scr, False):
            cp.start()

    def wait(in_refs, out_refs, scr):
        for cp in copies(in_refs, out_refs, scr, True):
            cp.wait()
        if table:
            own(in_refs, out_refs, scr).wait()

    out_shapes = [jax.ShapeDtypeStruct((3,) + a.shape[1:], a.dtype) for a in psums]
    if table:
        out_shapes.append(jax.ShapeDtypeStruct((4,) + small_sum.shape, F32))
    return _Carry(list(psums) + ([small_sum] if table else []), out_shapes,
                  _sems(3 * n + 3) + ([pltpu.SemaphoreType.DMA] if table else []), start, wait)


def _to_comm(name, kind, block, dtype=BF16):
    a = block[0]
    if kind == "cols":
        a = a.T
        if name == "w_in" and dtype == BF16:
            a = jnp.pad(a, ((0, IN_SHARD_PAD - IN_SHARD), (0, 0)))
    return a if kind == "f32" else a.astype(dtype)


def _from_comm(name, kind, a):
    if kind == "cols":
        if name == "w_in" and a.shape[0] != IN_SHARD:
            a = a[:IN_SHARD]
        a = a.T
    return a[None]


def _assemble_weights(g):
    out = {}
    if "w_in" in g:
        out["wt_main"] = _assemble_wt_main(g["w_in"], name="assemble_w_in")
        j, l0 = divmod(O_F, IN_SHARD)
        out["wt_f"] = jnp.pad(g["w_in"][j, l0:l0 + HEADS], ((0, 128 - HEADS), (0, 0)))
    square = dict(w_branch_a="w_a", w_branch_b="w_b", w_out="w_out", w_ple_gate="w_pg")
    for long, short in square.items():
        if long in g:
            out[short] = g[long].reshape(D_MODEL, D_MODEL)
    if "w_up" in g:
        out["wt_up"] = g["w_up"].reshape(2 * D_FF, D_MODEL)
    if "conv_w" in g:
        out["conv_w"] = g["conv_w"].transpose(1, 0, 2).reshape(3, 2 * D_FF)
    if "w_down" in g:
        out["w_down"] = g["w_down"].reshape(D_FF, D_MODEL)
    if "w_ple" in g:
        out["wt_ple"] = g["w_ple"].reshape(D_MODEL, PLE_DIM)
    return out


def _grad_slabs(gr):
    out = {}
    if "wt_main" in gr:
        gm, gf = gr["wt_main"], gr["wt_f"]
        segments = ((0, 2048, gm, 0), (2048, O_F, gm, 2048), (O_F, O_G, gf, -O_F), (O_G, IN_COLS, gm, 2048 - O_G))
        slabs = []
        for j in range(N_DEV):
            lo, hi = j * IN_SHARD, (j + 1) * IN_SHARD
            pieces = [src[max(lo, a) + shift:min(hi, b) + shift] for a, b, src, shift in segments if max(lo, a) < min(hi, b)]
            pieces.append(jnp.zeros((IN_SHARD_PAD - IN_SHARD, D_MODEL), gm.dtype))
            slabs.append(jnp.concatenate(pieces, axis=0))
        out["w_in"] = jnp.stack(slabs)
    rows = dict(w_a="w_branch_a", w_b="w_branch_b", w_out="w_out", wt_up="w_up", w_down="w_down", w_pg="w_ple_gate")
    for short, long in rows.items():
        if short in gr:
            out[long] = gr[short].reshape(N_DEV, -1, D_MODEL)
    if "conv_w" in gr:
        out["conv_w"] = gr["conv_w"].reshape(3, N_DEV, -1).transpose(1, 0, 2)
    if "wt_ple" in gr:
        out["w_ple"] = gr["wt_ple"].reshape(N_DEV, -1, PLE_DIM)
    return {k: v.astype(BF16) for k, v in out.items()}


def _rows(a, rows):
    flat = a.reshape(-1)
    return jnp.pad(flat, (0, rows * 1024 - flat.shape[0])).reshape(rows, 1024)


def _pack_small(parts):
    return jnp.concatenate([_rows(parts[n].astype(F32), r) for n, r in SMALL], axis=0)


def _small(packed, name, shape):
    off, r = SMALL_OFF[name]
    n = math.prod(shape)
    return packed[off:off + r].reshape(-1)[:n].reshape(shape)


class _Exchanges:
    W_S_ROWS = SMALL_OFF["gmlp_w_s"]

    def __init__(self, later, shards, pos):
        self.later, self.shards, self.pos = later, dict(zip(later, shards)), pos
        self.level1, self.slabs, self.from_sib, self.sums32, self.reduced, self.tables = {}, {}, {}, {}, {}, {}

    def gather1(self, names):
        carry = _carry_gather1([self.shards[n] for n in names])
        carry.names = names
        return carry

    def gather1_done(self, carry, results):
        self.level1.update(zip(carry.names, results))

    def gather2(self):
        return _carry_gather2([self.level1[n] for n in self.later])

    def weights(self, full):
        return _assemble_weights(dict(zip(self.later, full)))

    def sibling(self, grads):
        slabs = _grad_slabs(grads)
        self.slabs.update(slabs)
        carry = _carry_sibling(list(slabs.values()))
        carry.names = list(slabs)
        return carry

    def sibling_done(self, carry, results):
        self.from_sib.update(zip(carry.names, results))

    def chips(self, names, table=None):
        sums = {n: _sum_pairs(self.slabs[n], self.from_sib[n], self.pos, name="sum_sibling_" + n) for n in names}
        self.sums32.update({n: s32 for n, (s32, _) in sums.items()})
        carry = _carry_chips([s16 for _, s16 in sums.values()], None if table is None else self.table_part(table))
        carry.names, carry.table = list(names), table
        return carry

    def chips_done(self, carry, results):
        if carry.table is not None:
            *results, self.tables[carry.table] = results
        self.reduced.update({n: (self.sums32[n], r) for n, r in zip(carry.names, results)})

    def sibling_small(self, small_g):
        self.small_g = small_g
        return _carry_sibling([], small_g)

    def sibling_small_done(self, small_sib):
        self.small_chip = _pair_sum_small(self.small_g, small_sib, name="sum_sibling_small")

    def table_part(self, which):
        off, rows = self.W_S_ROWS
        if which == "w_s":
            return self.small_chip[off:off + rows]
        return jnp.concatenate([self.small_chip[:off], self.small_chip[off + rows:]], axis=0)

    def table(self):
        off = self.W_S_ROWS[0]
        rest = self.tables["rest"]
        return jnp.concatenate([rest[:, :off], self.tables["w_s"], rest[:, off:]], axis=1)


def _local_step(x, p, target, w, sm, ex=None):
    s = x.shape[0]
    mm = _matmul
    wt_main = w["wt_main"]
    conv_b = sm["conv_b"]
    bs_t = jnp.pad(sm["gmlp_b_s"].T, ((0, 0), (0, 128 - GROUPS)))
    b_f = jnp.pad(sm["b_f"], ((0, 0), (0, 128 - HEADS)))
    big = dict(tm=1024, tn=1024, tk=1024)
    whole_s = dict(tn=1024, tk=s)

    h = _rmsnorm_fwd(x, sm["norm_mix_g"], name="norm_mix")
    tall = dict(tm=s, tn=512, tk=1024)
    qkv_args = dict(mode="nt", out_dtype=BF16, name="in_qkv", n=3072, b_off=8, **tall)
    f_logit = mm(h, w["wt_f"], mode="nt", out_dtype=F32, name="in_f", tm=1024, tk=1024)
    cqe = _forget_cumsum(f_logit, b_f, name="forget_cumsum")
    uvg = dict(mode="nt", out_dtype=F32, name="in_uvg", n=4096, **tall)
    if ex is None:
        qkv = mm(h, wt_main, **qkv_args)
        (qa, ka, vt), _ = _attn_prep(qkv, cqe, name="attn_prep")
        (b, lse3), _ = _attn_fwd(qa, ka, vt, name="attn_fwd")
        zuvg = mm(h, wt_main, **uvg)
    else:
        groups = (["w_branch_a"], ["w_branch_b"], [n for n in ex.later if n not in ("w_branch_a", "w_branch_b")])
        carries = [ex.gather1(names) for names in groups]
        qkv, got0 = mm(h, wt_main, carry=carries[0], **qkv_args)
        (qa, ka, vt), got1 = _attn_prep(qkv, cqe, carries[1], name="attn_prep")
        (b, lse3), got2 = _attn_fwd(qa, ka, vt, carries[2], name="attn_fwd")
        for carry, got in zip(carries, (got0, got1, got2)):
            ex.gather1_done(carry, got)
        zuvg, full = mm(h, wt_main, carry=ex.gather2(), **uvg)
        w = {**w, **ex.weights(full)}
    a = _gmlp_fwd(zuvg, sm["gmlp_ln_g"], sm["gmlp_ln_b"], sm["gmlp_w_s"], bs_t, name="gmlp_fwd")
    wt_up, conv_w = w["wt_up"], w["conv_w"]
    ya, yb, merged = _branches_merge(a, b, w["w_a"], w["w_b"], zuvg, name="branches_merge")
    x1, h2 = mm(merged, w["w_out"], mode="nn", out_dtype=F32, name="out_proj", add=x, norm_g=sm["norm_ffn_g"], **big)
    up_a, up_g, act = _up_convglu(h2, wt_up, conv_w, conv_b, name="up_convglu")
    x2, h3 = mm(act, w["w_down"], mode="nn", out_dtype=F32, name="down", tm=1024, tn=1024, tk=1408, add=x1,
                norm_g=sm["norm_ple_g"])

    loss, dx3, dple, dgp, d_norm_final = _ple_loss(p, w["wt_ple"], h3, w["w_pg"], x2, target, sm["norm_final_g"],
                                                   name="ple_loss")
    g_wt_ple = mm(dple, p, mode="tn", out_dtype=BF16, name="d_w_ple", tm=512, tn=256, tk=s)
    g_w_pg = mm(h3, dgp, mode="tn", out_dtype=BF16, name="d_w_pg", tm=512, **whole_s)
    (dx2, dx2b, d_norm_ple), _ = _matmul_rmsnorm_bwd([dgp], w["w_pg"], dx3, x2, sm["norm_ple_g"], mode="nt", tk=1024,
                                                     name="d_h3_norm_ple_bwd")
    g_w_down = mm(act, dx2b, mode="tn", out_dtype=BF16, name="d_w_down", tm=1408, **whole_s)
    dact_args = dict(mode="nt", out_dtype=BF16, name="d_act", tm=s, tn=256, tk=1024)
    if ex is None:
        dact = mm(dx2b, w["w_down"], **dact_args)
    else:
        early = ex.sibling(dict(w_pg=g_w_pg, wt_ple=g_wt_ple))
        dact, got = mm(dx2b, w["w_down"], carry=early, **dact_args)
        ex.sibling_done(early, got)
    dup_a, dup_g, dcw_a, dcw_g, dcb_a, dcb_g = _convglu_bwd(dact, up_a, up_g, conv_w, conv_b, name="convglu_bwd")
    g_wt_up = mm(dup_a, h2, mode="tn", out_dtype=BF16, name="d_w_up_a", tm=1408, out_rows=2 * D_FF, **whole_s)
    g_wt_up = mm(dup_g, h2, mode="tn", out_dtype=BF16, name="d_w_up_g", tm=1408, out_rows=2 * D_FF,
                 o_off=D_FF // 1408, into=g_wt_up, **whole_s)
    (dx1, dx1b, d_norm_ffn), _ = _matmul_rmsnorm_bwd([dup_a, dup_g], wt_up, dx2, x1, sm["norm_ffn_g"], mode="nn",
                                                     tk=1408, name="d_h2_norm_ffn_bwd", resident=True)
    g_w_out = mm(merged, dx1b, mode="tn", out_dtype=BF16, name="d_w_out", tm=512, **whole_s)
    dya, dyb, dga, dgb = _merge_bwd(dx1b, w["w_out"], ya, yb, zuvg, name="merge_bwd")
    g_w_a = mm(a, dya, mode="tn", out_dtype=BF16, name="d_w_a", tm=512, **whole_s)
    g_w_b = mm(b, dyb, mode="tn", out_dtype=BF16, name="d_w_b", tm=512, **whole_s)
    da = mm(dya, w["w_a"], mode="nt", out_dtype=BF16, name="d_a", **big)
    db = mm(dyb, w["w_b"], mode="nt", out_dtype=BF16, name="d_b", **big)
    grads = dict(w_a=g_w_a, w_b=g_w_b, w_out=g_w_out, wt_up=g_wt_up, conv_w=jnp.concatenate([dcw_a, dcw_g], axis=1),
                 w_down=g_w_down, wt_ple=g_wt_ple, w_pg=g_w_pg)
    gmlp_args = (da, zuvg, sm["gmlp_ln_g"], sm["gmlp_ln_b"], sm["gmlp_w_s"], bs_t)
    if ex is None:
        (dzu, dzv, d_w_s, d_bs_t, d_ln_g, d_ln_b), _ = _gmlp_bwd(*gmlp_args, name="gmlp_bwd")
    else:
        rest = ex.sibling({k: v for k, v in grads.items() if k not in ("w_pg", "wt_ple")})
        early_chips = ex.chips(early.names)
        both = _carry_join(rest, early_chips)
        (dzu, dzv, d_w_s, d_bs_t, d_ln_g, d_ln_b), got = _gmlp_bwd(*gmlp_args, both, name="gmlp_bwd")
        got_rest, got_early = both.split(got)
        ex.sibling_done(rest, got_rest)
        ex.chips_done(early_chips, got_early)
    small = dict(norm_mix_g=jnp.zeros((1, D_MODEL), F32), b_f=jnp.zeros((1, HEADS), F32), gmlp_ln_g=d_ln_g,
                 gmlp_ln_b=d_ln_b, gmlp_w_s=d_w_s, gmlp_b_s=d_bs_t[:, :GROUPS].T, norm_ffn_g=d_norm_ffn,
                 conv_b=jnp.concatenate([dcb_a, dcb_g], axis=1), norm_ple_g=d_norm_ple, norm_final_g=d_norm_final)
    if ex is None:
        delta3, _ = _attn_delta(db, b, name="attn_delta")
        (dq, dk, dv, aux, dcq3), _ = _attn_bwd(qa, ka, qkv, db, lse3, delta3, name="attn_bwd")
    else:
        delta3, (small_sib,) = _attn_delta(db, b, ex.sibling_small(_pack_small(small)), name="attn_delta")
        ex.sibling_small_done(small_sib)
        main_chips = ex.chips(rest.names, table="rest")
        (dq, dk, dv, aux, dcq3), got = _attn_bwd(qa, ka, qkv, db, lse3, delta3, main_chips, name="attn_bwd")
        ex.chips_done(main_chips, got)
    dcq16 = jnp.pad(dcq3[:, :2, :].reshape(HEADS, s).T, ((0, 0), (0, 128 - HEADS)))
    dzf, d_b_f = _forget_bwd(dcq16, aux, f_logit, b_f, name="forget_bwd")
    dz_parts = [dzu, dzv, dga, dgb, dq, dk, dv]
    w_s_chips = None if ex is None else ex.chips([], table="w_s")
    g_wt_main, got = _grad_w_parts(dz_parts, h, name="d_w_main", tm=512, carry=w_s_chips)
    if ex is not None:
        ex.chips_done(w_s_chips, got)
    g_wt_f = mm(dzf, h, mode="tn", out_dtype=BF16, name="d_w_f", **whole_s)
    grads = dict(grads, wt_main=g_wt_main, wt_f=g_wt_f)
    w_in_chips = None
    if ex is not None:
        w_in_sib = ex.sibling(dict(wt_main=g_wt_main, wt_f=g_wt_f))
        ex.sibling_done(w_in_sib, _run_carry(w_in_sib, name="exchange_sibling_w_in"))
        w_in_chips = ex.chips(w_in_sib.names)
    (dx0, _, d_norm_mix), got = _matmul_rmsnorm_bwd(dz_parts, wt_main, dx1, x, sm["norm_mix_g"], mode="nn", tk=1024,
                                                    extra=(dzf, w["wt_f"]), name="d_h_norm_mix_bwd", carry=w_in_chips,
                                                    lead=True)
    if ex is not None:
        ex.chips_done(w_in_chips, got)
    return loss, dx0, grads, dict(small, norm_mix_g=d_norm_mix, b_f=d_b_f[:, :HEADS])


def kernel(x, p, norm_mix_g, w_in, b_f, gmlp_ln_g, gmlp_ln_b, gmlp_w_s, gmlp_b_s, w_branch_a, w_branch_b, w_out, norm_ffn_g, w_up, conv_w, conv_b, w_down, norm_ple_g, w_ple, w_ple_gate, norm_final_g, loss_target, m_norm_mix_g, m_w_in, m_b_f, m_gmlp_ln_g, m_gmlp_ln_b, m_gmlp_w_s, m_gmlp_b_s, m_w_branch_a, m_w_branch_b, m_w_out, m_norm_ffn_g, m_w_up, m_conv_w, m_conv_b, m_w_down, m_norm_ple_g, m_w_ple, m_w_ple_gate, m_norm_final_g, v_norm_mix_g, v_w_in, v_b_f, v_gmlp_ln_g, v_gmlp_ln_b, v_gmlp_w_s, v_gmlp_b_s, v_w_branch_a, v_w_branch_b, v_w_out, v_norm_ffn_g, v_w_up, v_conv_w, v_conv_b, v_w_down, v_norm_ple_g, v_w_ple, v_w_ple_gate, v_norm_final_g):
    given = dict(locals())
    weights = {n: given[n] for n in WEIGHT_ORDER}
    mom_m = {n: given["m_" + n] for n in WEIGHT_ORDER}
    mom_v = {n: given["v_" + n] for n in WEIGHT_ORDER}
    pos = jnp.stack([lax.axis_index("x"), lax.axis_index("y"), lax.axis_index("c")]).astype(I32)
    names = [n for n, _ in SHARDED]
    kinds = dict(SHARDED)

    later = [n for n in names if n != "w_in"]

    first = _allgather([_to_comm("w_in", kinds["w_in"], weights["w_in"])], name="allgather_w_in")
    ex = _Exchanges(later, [_to_comm(n, kinds[n], weights[n]) for n in later], pos)

    sm = dict(norm_mix_g=norm_mix_g, b_f=b_f, gmlp_ln_g=gmlp_ln_g, gmlp_ln_b=gmlp_ln_b, gmlp_w_s=gmlp_w_s[0],
              gmlp_b_s=gmlp_b_s[0], norm_ffn_g=norm_ffn_g, conv_b=conv_b, norm_ple_g=norm_ple_g,
              norm_final_g=norm_final_g.reshape(1, D_MODEL))
    loss_part, dx0, grads, small = _local_step(
        x[0], p[0, 0], loss_target[0], _assemble_weights({"w_in": first[0]}), sm, ex)

    b_f_and_loss = jnp.concatenate([small["b_f"].reshape(-1), loss_part[0, :1]])
    last = _allreduce_rows(jnp.concatenate([_rows(small["norm_mix_g"], 8), _rows(b_f_and_loss, 8)], axis=0),
                           name="allreduce_last")
    loss = last[8, HEADS]
    small_last = jnp.pad(last, ((0, SMALL_ROWS - 16), (0, 0)))

    grad, delta, new_m, new_v = {}, {}, {}, {}
    for n in names:
        s32, r = ex.reduced[n]
        outs = _adam_sharded(s32, r, *[_to_comm(n, kinds[n], src[n], F32) for src in (weights, mom_m, mom_v)], pos,
                             name="adam_" + n)
        grad[n], delta[n], new_m[n], new_v[n] = [_from_comm(n, kinds[n], o) for o in outs]
    replicated = [n for n, _ in SMALL]
    rep = lambda src: _pack_small({n: src[n] for n in replicated})
    packed = _adam_replicated(ex.table(), small_last, rep(weights), rep(mom_m), rep(mom_v), name="adam_replicated")
    for out, pk in zip((grad, delta, new_m, new_v), packed):
        for n in replicated:
            out[n] = _small(pk, n, weights[n].shape)

    return (loss, dx0, *[grad[n] for n in WEIGHT_ORDER], *[delta[n] for n in WEIGHT_ORDER],
            *[new_m[n] for n in WEIGHT_ORDER], *[new_v[n] for n in WEIGHT_ORDER])
```

```python
import functools
import math

import jax
import jax.numpy as jnp
from jax import lax
from jax.experimental import pallas as pl
from jax.experimental.pallas import tpu as pltpu

F32 = jnp.float32
BF16 = jnp.bfloat16
I32 = jnp.int32

D_MODEL = 1024
GROUPS = 8
GDIM = 128
GBLOCK = 128
CHUNK = 64
HEADS = 16
HEAD_DIM = 64
D_FF = 2816
PLE_DIM = 256
EPS = 1e-6
N_DEV = 8
ATT_SCALE = HEAD_DIM ** -0.5
NEG = -1e30

ADAM_LR = 0.001
ADAM_B1 = 0.9
ADAM_B2 = 0.999
ADAM_EPS = 1e-08
ADAM_WD = 0.01
ADAM_STEP = 10

V7X_VMEM_LIMIT = 48 * 1024 * 1024
MESH = pl.DeviceIdType.MESH

O_F = 2 * 1024 + 3 * 1024
O_G = O_F + HEADS
IN_COLS = O_G + 2 * D_MODEL
MAIN_COLS = IN_COLS - HEADS
IN_SHARD = IN_COLS // N_DEV
IN_SHARD_PAD = 912

SHARDED = (("w_in", "cols"), ("w_branch_a", "rows"), ("w_branch_b", "rows"), ("w_out", "rows"), ("w_up", "cols"),
           ("conv_w", "f32"), ("w_down", "rows"), ("w_ple", "cols"), ("w_ple_gate", "rows"))

SMALL = (("norm_mix_g", 8), ("b_f", 8), ("gmlp_ln_g", 8), ("gmlp_ln_b", 8), ("gmlp_w_s", 128), ("gmlp_b_s", 8),
         ("norm_ffn_g", 8), ("conv_b", 8), ("norm_ple_g", 8), ("norm_final_g", 8))
SMALL_OFF = {}
_o = 0
for _n, _r in SMALL:
    SMALL_OFF[_n] = (_o, _r)
    _o += _r
SMALL_ROWS = _o

WEIGHT_ORDER = ("norm_mix_g", "w_in", "b_f", "gmlp_ln_g", "gmlp_ln_b", "gmlp_w_s", "gmlp_b_s", "w_branch_a",
                "w_branch_b", "w_out", "norm_ffn_g", "w_up", "conv_w", "conv_b", "w_down", "norm_ple_g", "w_ple",
                "w_ple_gate", "norm_final_g")


def _cparams(sem):
    return pltpu.CompilerParams(dimension_semantics=sem, vmem_limit_bytes=V7X_VMEM_LIMIT)


def _gelu(x):
    c = math.sqrt(2.0 / math.pi)
    return 0.5 * x * (1.0 + jnp.tanh(c * (x + 0.044715 * x * x * x)))


def _gelu_and_grad(x):
    c = math.sqrt(2.0 / math.pi)
    t = jnp.tanh(c * (x + 0.044715 * x * x * x))
    g = 0.5 * x * (1.0 + t)
    dg = 0.5 * (1.0 + t) + 0.5 * x * (1.0 - t * t) * (c * (1.0 + 3.0 * 0.044715 * x * x))
    return g, dg


def _sigmoid(x):
    return 1.0 / (1.0 + jnp.exp(-x))


def _dot(a, b, dims):
    return lax.dot_general(a, b, (dims, ((), ())), preferred_element_type=F32)


NN = ((1,), (0,))
NT = ((1,), (1,))
TN = ((0,), (0,))


def _row_tile(rows, most):
    best = None
    for t in range(16, min(rows, most) + 1, 16):
        if rows % t == 0:
            best = t
    return best if best is not None else rows


def _matmul(a, b, *, mode, out_dtype, name, tm=512, tn=512, tk=512, add=None, n=None, b_off=0,
            out_rows=None, o_off=0, into=None, norm_g=None, carry=None):
    if mode == "tn":
        kdim, m = a.shape
    else:
        m, kdim = a.shape
    if n is None:
        n = b.shape[0] if mode == "nt" else b.shape[1]
    tm, tn, tk = min(tm, m), min(tn, n), min(tk, kdim)
    assert m % tm == 0 and n % tn == 0 and kdim % tk == 0, (name, m, n, kdim, tm, tn, tk)
    nk = kdim // tk
    dims = {"nn": NN, "nt": NT, "tn": TN}[mode]

    n_in = 2 + (add is not None) + (into is not None) + (norm_g is not None)
    assert norm_g is None or tn == n, "the RMS norm needs whole rows"

    def finish(r, refs):
        if add is not None:
            r = refs[2][...].astype(F32) + r
        refs[n_in][...] = r.astype(out_dtype)
        if norm_g is not None:
            rs = lax.rsqrt(jnp.mean(r * r, axis=-1, keepdims=True) + EPS)
            refs[n_in + 1][...] = ((r * rs) * refs[n_in - 1][...]).astype(BF16)

    def body(*refs):
        a_ref, b_ref = refs[:2]
        part = _dot(a_ref[...].astype(BF16), b_ref[...].astype(BF16), dims)
        if nk == 1:
            finish(part, refs)
            return
        acc_ref = refs[-1]
        k = pl.program_id(2)

        @pl.when(k == 0)
        def _():
            acc_ref[...] = part

        @pl.when((k > 0) & (k < nk - 1))
        def _():
            acc_ref[...] += part

        @pl.when(k == nk - 1)
        def _():
            finish(acc_ref[...] + part, refs)

    a_spec = pl.BlockSpec((tk, tm), lambda i, j, k: (k, i)) if mode == "tn" else pl.BlockSpec((tm, tk), lambda i, j, k: (i, k))
    if mode == "nt":
        b_spec = pl.BlockSpec((tn, tk), lambda i, j, k: (j + b_off, k))
    else:
        b_spec = pl.BlockSpec((tk, tn), lambda i, j, k: (k + b_off, j))
    o_spec = pl.BlockSpec((tm, tn), lambda i, j, k: (i + o_off, j))
    in_specs = [a_spec, b_spec] + ([pl.BlockSpec((tm, tn), lambda i, j, k: (i, j))] if add is not None else [])
    args = (a, b) + ((add,) if add is not None else ())
    aliases = {}
    if into is not None:
        aliases = {len(args): 0}
        in_specs.append(pl.BlockSpec(memory_space=pl.ANY))
        args += (into,)
    out_specs = [o_spec]
    out_shape = [jax.ShapeDtypeStruct((m if out_rows is None else out_rows, n), out_dtype)]
    if norm_g is not None:
        in_specs.append(pl.BlockSpec((1, n), lambda i, j, k: (0, 0)))
        args += (norm_g,)
        out_specs.append(pl.BlockSpec((tm, tn), lambda i, j, k: (i, j)))
        out_shape.append(jax.ShapeDtypeStruct((m, n), BF16))
    outs, carried = _carry_call(
        body, carry, name=name, grid=(m // tm, n // tn, nk), in_specs=in_specs, out_specs=out_specs,
        out_shape=out_shape, scratch_shapes=[pltpu.VMEM((tm, tn), F32)] if nk > 1 else [], args=args,
        own_aliases=aliases)
    out = outs[0] if norm_g is None else tuple(outs)
    return out if carry is None else (out, carried)


def _row_spec(tr, width, col_block=0):
    return pl.BlockSpec((tr, width), lambda i: (i, col_block))


def _full_spec(shape):
    return pl.BlockSpec(shape, lambda i: tuple(0 for _ in shape))


def _rmsnorm_fwd(x, g, *, name, tr=256):
    s, d = x.shape

    def body(x_ref, g_ref, o_ref):
        xv = x_ref[...]
        r = lax.rsqrt(jnp.mean(xv * xv, axis=-1, keepdims=True) + EPS)
        o_ref[...] = ((xv * r) * g_ref[...]).astype(BF16)

    return pl.pallas_call(
        body, name=name, grid=(s // tr,),
        in_specs=[_row_spec(tr, d), _full_spec((1, d))], out_specs=_row_spec(tr, d),
        out_shape=jax.ShapeDtypeStruct((s, d), BF16), compiler_params=_cparams(("parallel",)),
    )(x, g)


def _matmul_rmsnorm_bwd(a_parts, b, dres, x, g, *, mode, tk, name, extra=None, tm=512, carry=None, lead=False,
                        resident=False):
    s, d = x.shape
    n_row = s // tm
    spans, lo = [], 0
    for a in a_parts:
        spans.append((lo, lo + a.shape[1] // tk))
        lo = spans[-1][1]
    n_main, total = lo, lo + (extra is not None)
    n_parts = len(a_parts)

    def body(*refs):
        a_refs, b_ref = refs[:n_parts], refs[n_parts]
        k0 = n_parts + 1
        ax_ref, bx_ref = (refs[k0], refs[k0 + 1]) if extra is not None else (None, None)
        k0 += 2 * (extra is not None)
        dres_ref, x_ref, g_ref, dx_ref, dxb_ref, dg_ref, acc_all = refs[k0:k0 + 7]
        if resident:
            kk, i = pl.program_id(0), pl.program_id(1)
            acc_ref = acc_all.at[pl.ds(pl.multiple_of(i * tm, tm), tm)]
        else:
            i, kk = pl.program_id(0), pl.program_id(1)
            acc_ref = acc_all

        def accumulate(part, first):
            if first:
                @pl.when(kk == 0)
                def _():
                    acc_ref[...] = part

                @pl.when(kk > 0)
                def _():
                    acc_ref[...] += part
            else:
                acc_ref[...] += part

        for p, (a_ref, (lo_p, hi_p)) in enumerate(zip(a_refs, spans)):
            @pl.when((kk >= lo_p) & (kk < hi_p))
            def _(a_ref=a_ref, lo_p=lo_p):
                accumulate(_dot(a_ref[...].astype(BF16), b_ref[...].astype(BF16), NN if mode == "nn" else NT), lo_p == 0)

        if extra is not None:
            @pl.when(kk == n_main)
            def _():
                accumulate(_dot(ax_ref[...].astype(BF16), bx_ref[...].astype(BF16), NN), False)

        @pl.when(kk == total - 1)
        def _():
            dhv = acc_ref[...]
            xv = x_ref[...]
            r = lax.rsqrt(jnp.mean(xv * xv, axis=-1, keepdims=True) + EPS)
            xhat = xv * r
            dxhat = dhv * g_ref[...]
            dx = dres_ref[...] + r * (dxhat - xhat * jnp.mean(dxhat * xhat, axis=-1, keepdims=True))
            dx_ref[...] = dx
            dxb_ref[...] = dx.astype(BF16)
            dgp = jnp.sum(dhv * xhat, axis=0, keepdims=True)

            @pl.when(i == 0)
            def _():
                dg_ref[...] = dgp

            @pl.when(i > 0)
            def _():
                dg_ref[...] += dgp

    def spec(shape, index):
        return pl.BlockSpec(shape, (lambda kk, i: index(i, kk)) if resident else index)

    def row(i, kk, lo_p, hi_p):
        if not resident:
            return i
        return jnp.where(kk < lo_p, 0, jnp.where(kk >= hi_p, n_row - 1, i))

    a_specs = [spec((tm, tk), lambda i, kk, lo_p=lo_p, hi_p=hi_p: (row(i, kk, lo_p, hi_p),
                                                                    jnp.clip(kk - lo_p, 0, hi_p - lo_p - 1)))
               for lo_p, hi_p in spans]
    step = lambda kk: jnp.minimum(kk, n_main - 1)
    b_spec = (spec((tk, d), lambda i, kk: (step(kk), 0)) if mode == "nn"
              else spec((d, tk), lambda i, kk: (0, step(kk))))
    rows = spec((tm, d), lambda i, kk: (row(i, kk, total - 1, total), 0))
    one = spec((1, d), lambda i, kk: (0, 0))
    dx_spec, dx_shape = rows, jax.ShapeDtypeStruct((s, d), F32)
    if lead:
        dx_spec = spec((None, tm, d), lambda i, kk: (0, row(i, kk, total - 1, total), 0))
        dx_shape = jax.ShapeDtypeStruct((1, s, d), F32)
    x_specs, x_args = [], []
    if extra is not None:
        kx = extra[0].shape[1]
        x_specs = [spec((tm, kx), lambda i, kk: (row(i, kk, n_main, total), 0)), spec((kx, d), lambda i, kk: (0, 0))]
        x_args = list(extra)
    (dx, dxb, dg), carried = _carry_call(
        body, carry, name=name, grid=(total, n_row) if resident else (n_row, total),
        in_specs=a_specs + [b_spec] + x_specs + [rows, rows, one], out_specs=[dx_spec, rows, one],
        out_shape=[dx_shape, jax.ShapeDtypeStruct((s, d), BF16), jax.ShapeDtypeStruct((1, d), F32)],
        scratch_shapes=[pltpu.VMEM((s if resident else tm, d), F32)], args=list(a_parts) + [b] + x_args + [dres, x, g])
    return (dx, dxb, dg), carried


def _grad_w_parts(a_parts, b, *, name, tm=512, carry=None):
    s, width = a_parts[0].shape
    per, n = width // tm, b.shape[1]

    def body(*refs):
        a_refs, b_ref, o_ref = refs[:len(a_parts)], refs[len(a_parts)], refs[len(a_parts) + 1]
        i = pl.program_id(0)
        for p, a_ref in enumerate(a_refs):
            @pl.when(i // per == p)
            def _(a_ref=a_ref):
                o_ref[...] = _dot(a_ref[...].astype(BF16), b_ref[...].astype(BF16), TN).astype(BF16)

    a_specs = [pl.BlockSpec((s, tm), lambda i, p=p: (0, jnp.clip(i - p * per, 0, per - 1))) for p in range(len(a_parts))]
    (out,), carried = _carry_call(
        body, carry, name=name, grid=(len(a_parts) * per,),
        in_specs=a_specs + [pl.BlockSpec((s, n), lambda i: (0, 0))], out_specs=[pl.BlockSpec((tm, n), lambda i: (i, 0))],
        out_shape=[jax.ShapeDtypeStruct((len(a_parts) * width, n), BF16)], scratch_shapes=[], args=list(a_parts) + [b])
    return out, carried


def _ple_loss(p, wt_ple, h3, w_pg, x2, target, g, *, name, tm=256):
    s, d = x2.shape
    kp = p.shape[1]

    def body(p_ref, wp_ref, h_ref, wg_ref, x_ref, t_ref, g_ref, loss_ref, dx_ref, dple_ref, dgp_ref, dg_ref):
        i = pl.program_id(0)
        ple = _dot(p_ref[...].astype(BF16), wp_ref[...], NT)
        sg = _sigmoid(_dot(h_ref[...], wg_ref[...], NN))
        xv = x_ref[...] + ple * sg
        r = lax.rsqrt(jnp.mean(xv * xv, axis=-1, keepdims=True) + EPS)
        xhat = xv * r
        diff = xhat * g_ref[...] - t_ref[...]
        lp = jnp.zeros((1, 128), F32) + (0.5 / d) * jnp.sum(diff * diff)
        dy = diff * (1.0 / d)
        dxhat = dy * g_ref[...]
        dx = r * (dxhat - xhat * jnp.mean(dxhat * xhat, axis=-1, keepdims=True))
        dx_ref[...] = dx
        dple_ref[...] = (dx * sg).astype(BF16)
        dgp_ref[...] = (dx * ple * (sg * (1.0 - sg))).astype(BF16)
        dgp = jnp.sum(dy * xhat, axis=0, keepdims=True)

        @pl.when(i == 0)
        def _():
            dg_ref[...] = dgp
            loss_ref[...] = lp

        @pl.when(i > 0)
        def _():
            dg_ref[...] += dgp
            loss_ref[...] += lp

    rows = _row_spec(tm, d)
    return pl.pallas_call(
        body, name=name, grid=(s // tm,),
        in_specs=[_row_spec(tm, kp), _full_spec((d, kp)), rows, _full_spec((d, d)), rows, rows, _full_spec((1, d))],
        out_specs=[_full_spec((1, 128)), rows, rows, rows, _full_spec((1, d))],
        out_shape=[jax.ShapeDtypeStruct((1, 128), F32), jax.ShapeDtypeStruct((s, d), F32),
                   jax.ShapeDtypeStruct((s, d), BF16), jax.ShapeDtypeStruct((s, d), BF16),
                   jax.ShapeDtypeStruct((1, d), F32)],
        compiler_params=_cparams(("arbitrary",)),
    )(p, wt_ple, h3, w_pg, x2, target, g)


def _branches_merge(a, b, w_a, w_b, zuvg, *, name, tm=512):
    s, d = a.shape

    def body(a_ref, b_ref, wa_ref, wb_ref, ga_ref, gb_ref, ya_ref, yb_ref, o_ref):
        ya = _dot(a_ref[...], wa_ref[...], NN)
        yb = _dot(b_ref[...], wb_ref[...], NN)
        ya_ref[...] = ya
        yb_ref[...] = yb
        o_ref[...] = (_sigmoid(ga_ref[...]) * ya + _sigmoid(gb_ref[...]) * yb).astype(BF16)

    rows = _row_spec(tm, d)
    return pl.pallas_call(
        body, name=name, grid=(s // tm,),
        in_specs=[rows, rows, _full_spec((d, d)), _full_spec((d, d)), _row_spec(tm, d, 2), _row_spec(tm, d, 3)],
        out_specs=[rows, rows, rows],
        out_shape=[jax.ShapeDtypeStruct((s, d), F32), jax.ShapeDtypeStruct((s, d), F32), jax.ShapeDtypeStruct((s, d), BF16)],
        compiler_params=_cparams(("parallel",)),
    )(a, b, w_a, w_b, zuvg, zuvg)


def _merge_bwd(dx1b, w_out, ya, yb, zuvg, *, name, tm=512):
    s, d = ya.shape

    def body(dx_ref, w_ref, ya_ref, yb_ref, ga_ref, gb_ref, dya_ref, dyb_ref, dga_ref, dgb_ref):
        dmv = _dot(dx_ref[...], w_ref[...], NT)
        sa = _sigmoid(ga_ref[...])
        sb = _sigmoid(gb_ref[...])
        dya_ref[...] = (dmv * sa).astype(BF16)
        dyb_ref[...] = (dmv * sb).astype(BF16)
        dga_ref[...] = (dmv * ya_ref[...] * (sa * (1.0 - sa))).astype(BF16)
        dgb_ref[...] = (dmv * yb_ref[...] * (sb * (1.0 - sb))).astype(BF16)

    rows = _row_spec(tm, d)
    o = jax.ShapeDtypeStruct((s, d), BF16)
    return pl.pallas_call(
        body, name=name, grid=(s // tm,),
        in_specs=[rows, _full_spec((d, d)), rows, rows, _row_spec(tm, d, 2), _row_spec(tm, d, 3)],
        out_specs=[rows] * 4, out_shape=[o, o, o, o], compiler_params=_cparams(("parallel",)),
    )(dx1b, w_out, ya, yb, zuvg, zuvg)


def _masked_ws(ws_ref, g):
    row = lax.broadcasted_iota(I32, (GBLOCK, GBLOCK), 0)
    col = lax.broadcasted_iota(I32, (GBLOCK, GBLOCK), 1)
    keep = (col // CHUNK) <= (row // CHUNK)
    return jnp.where(keep, ws_ref[g], 0.0), keep


def _layernorm_parts(zv):
    mu = jnp.mean(zv, axis=-1, keepdims=True)
    xc = zv - mu
    rs = lax.rsqrt(jnp.mean(xc * xc, axis=-1, keepdims=True) + EPS)
    return xc * rs, rs


def _gmlp_fwd(zuvg, ln_g, ln_b, w_s, bs_t, *, name):
    s, w = zuvg.shape[0], GROUPS * GDIM

    def body(zu_ref, zv_ref, lng_ref, lnb_ref, ws_ref, bs_ref, a_ref):
        zu = _gelu(zu_ref[...])
        zv = _gelu(zv_ref[...])
        xhat, _ = _layernorm_parts(zv)
        vln = (xhat * lng_ref[...] + lnb_ref[...]).astype(BF16)
        for g in range(GROUPS):
            wm, _ = _masked_ws(ws_ref, g)
            mixed = _dot(wm.astype(BF16), vln[:, g * GDIM:(g + 1) * GDIM], NN) + bs_ref[:, g:g + 1]
            a_ref[:, g * GDIM:(g + 1) * GDIM] = (zu[:, g * GDIM:(g + 1) * GDIM] * mixed).astype(BF16)

    return pl.pallas_call(
        body, name=name, grid=(s // GBLOCK,),
        in_specs=[_row_spec(GBLOCK, w, 0), _row_spec(GBLOCK, w, 1), _full_spec((1, w)), _full_spec((1, w)),
                  _full_spec((GROUPS, GBLOCK, GBLOCK)), _full_spec((GBLOCK, 128))],
        out_specs=_row_spec(GBLOCK, w),
        out_shape=jax.ShapeDtypeStruct((s, w), BF16), compiler_params=_cparams(("parallel",)),
    )(zuvg, zuvg, ln_g, ln_b, w_s, bs_t)


def _gmlp_bwd(da, zuvg, ln_g, ln_b, w_s, bs_t, carry=None, *, name):
    s, w = zuvg.shape[0], GROUPS * GDIM

    def body(da_ref, zu_ref, zv_ref, lng_ref, lnb_ref, ws_ref, bs_ref,
             dzu_ref, dzv_ref, dws_ref, dbs_ref, dlng_ref, dlnb_ref, dvln_ref):
        i = pl.program_id(0)
        zu, dzu_g = _gelu_and_grad(zu_ref[...])
        zv, dzv_g = _gelu_and_grad(zv_ref[...])
        xhat, rs = _layernorm_parts(zv)
        vln = (xhat * lng_ref[...] + lnb_ref[...]).astype(BF16)
        dav = da_ref[...].astype(F32)
        lane = lax.broadcasted_iota(I32, (GBLOCK, 128), 1)
        dbs = jnp.zeros((GBLOCK, 128), F32)

        @pl.when(i == 0)
        def _():
            dws_ref[...] = jnp.zeros_like(dws_ref)

        for g in range(GROUPS):
            sl = slice(g * GDIM, (g + 1) * GDIM)
            wm, keep = _masked_ws(ws_ref, g)
            wmb = wm.astype(BF16)
            vg = vln[:, sl]
            mixed = _dot(wmb, vg, NN) + bs_ref[:, g:g + 1]
            dag = dav[:, sl]
            dzu_ref[:, sl] = (dag * mixed * dzu_g[:, sl]).astype(BF16)
            dmix = dag * zu[:, sl]
            dmb = dmix.astype(BF16)
            dws_ref[g] += jnp.where(keep, _dot(dmb, vg, NT), 0.0)
            dbs = jnp.where(lane == g, jnp.sum(dmix, axis=1, keepdims=True), dbs)
            dvln_ref[:, sl] = _dot(wmb, dmb, TN)
        dvln = dvln_ref[...]
        dxhat = dvln * lng_ref[...]
        dzv = rs * (dxhat - jnp.mean(dxhat, axis=-1, keepdims=True)
                    - xhat * jnp.mean(dxhat * xhat, axis=-1, keepdims=True))
        dzv_ref[...] = (dzv * dzv_g).astype(BF16)
        dlng = jnp.sum(dvln * xhat, axis=0, keepdims=True)
        dlnb = jnp.sum(dvln, axis=0, keepdims=True)

        @pl.when(i == 0)
        def _():
            dbs_ref[...] = dbs
            dlng_ref[...] = dlng
            dlnb_ref[...] = dlnb

        @pl.when(i > 0)
        def _():
            dbs_ref[...] += dbs
            dlng_ref[...] += dlng
            dlnb_ref[...] += dlnb

    return _carry_call(
        body, carry, name=name, grid=(s // GBLOCK,),
        in_specs=[_row_spec(GBLOCK, w), _row_spec(GBLOCK, w, 0), _row_spec(GBLOCK, w, 1), _full_spec((1, w)),
                  _full_spec((1, w)), _full_spec((GROUPS, GBLOCK, GBLOCK)), _full_spec((GBLOCK, 128))],
        out_specs=[_row_spec(GBLOCK, w), _row_spec(GBLOCK, w), _full_spec((GROUPS, GBLOCK, GBLOCK)),
                   _full_spec((GBLOCK, 128)), _full_spec((1, w)), _full_spec((1, w))],
        out_shape=[jax.ShapeDtypeStruct((s, w), BF16), jax.ShapeDtypeStruct((s, w), BF16),
                   jax.ShapeDtypeStruct((GROUPS, GBLOCK, GBLOCK), F32), jax.ShapeDtypeStruct((GBLOCK, 128), F32),
                   jax.ShapeDtypeStruct((1, w), F32), jax.ShapeDtypeStruct((1, w), F32)],
        scratch_shapes=[pltpu.VMEM((GBLOCK, w), F32)], args=[da, zuvg, zuvg, ln_g, ln_b, w_s, bs_t])


def _shift_down(u, k):
    row = lax.broadcasted_iota(I32, u.shape, 0)
    return jnp.where(row >= k, pltpu.roll(u, k, 0), 0.0)


def _shift_up(u, k):
    s = u.shape[0]
    row = lax.broadcasted_iota(I32, u.shape, 0)
    return jnp.where(row < s - k, pltpu.roll(u, s - k, 0), 0.0)


def _conv(u, w_ref, b_ref):
    return b_ref[...] + w_ref[0:1, :] * _shift_down(u, 2) + w_ref[1:2, :] * _shift_down(u, 1) + w_ref[2:3, :] * u


def _conv_specs(s, f, tc):
    nc = f // tc
    half = lambda rows: [pl.BlockSpec((rows, tc), lambda j: (0, j)), pl.BlockSpec((rows, tc), lambda j: (0, nc + j))]
    return half(s), half(3), half(1)


def _up_convglu(h2, wt_up, conv_w, conv_b, *, name, tc=256):
    s, d = h2.shape
    f = wt_up.shape[0] // 2
    nc = f // tc
    _, w_specs, b_specs = _conv_specs(s, f, tc)

    def body(h_ref, ta_ref, tg_ref, wa_ref, wg_ref, ba_ref, bg_ref, ua_ref, ug_ref, o_ref):
        ua = _dot(h_ref[...], ta_ref[...], NT)
        ua_ref[...] = ua
        ga = _gelu(_conv(ua, wa_ref, ba_ref))
        ug = _dot(h_ref[...], tg_ref[...], NT)
        ug_ref[...] = ug
        o_ref[...] = (ga * _conv(ug, wg_ref, bg_ref)).astype(BF16)

    col = pl.BlockSpec((s, tc), lambda j: (0, j))
    return pl.pallas_call(
        body, name=name, grid=(nc,),
        in_specs=[_full_spec((s, d)), pl.BlockSpec((tc, d), lambda j: (j, 0)), pl.BlockSpec((tc, d), lambda j: (nc + j, 0))]
        + w_specs + b_specs,
        out_specs=[col, col, col],
        out_shape=[jax.ShapeDtypeStruct((s, f), F32), jax.ShapeDtypeStruct((s, f), F32), jax.ShapeDtypeStruct((s, f), BF16)],
        compiler_params=_cparams(("parallel",)),
    )(h2, wt_up, wt_up, conv_w, conv_w, conv_b, conv_b)


def _convglu_bwd(dact, up_a, up_g, conv_w, conv_b, *, name, tc=256):
    s, f = up_a.shape
    _, w_specs, b_specs = _conv_specs(s, f, tc)
    up_specs = [pl.BlockSpec((s, tc), lambda j: (0, j))] * 2

    def half(dc, taps, w_ref, du_ref, dw_ref, db_ref):
        db_ref[...] = jnp.sum(dc, axis=0, keepdims=True)
        for k in range(3):
            dw_ref[k:k + 1, :] = jnp.sum(dc * taps[k], axis=0, keepdims=True)
        du = w_ref[2:3, :] * dc + w_ref[1:2, :] * _shift_up(dc, 1) + w_ref[0:1, :] * _shift_up(dc, 2)
        du_ref[...] = du.astype(BF16)

    def body(d_ref, ua_ref, ug_ref, wa_ref, wg_ref, ba_ref, bg_ref,
             dua_ref, dug_ref, dwa_ref, dwg_ref, dba_ref, dbg_ref):
        taps_a = (_shift_down(ua_ref[...], 2), _shift_down(ua_ref[...], 1), ua_ref[...])
        taps_g = (_shift_down(ug_ref[...], 2), _shift_down(ug_ref[...], 1), ug_ref[...])
        conv = lambda taps, w_ref, b_ref: b_ref[...] + w_ref[0:1, :] * taps[0] + w_ref[1:2, :] * taps[1] + w_ref[2:3, :] * taps[2]
        ca = conv(taps_a, wa_ref, ba_ref)
        cg = conv(taps_g, wg_ref, bg_ref)
        ga, dga = _gelu_and_grad(ca)
        dv = d_ref[...].astype(F32)
        half(dv * cg * dga, taps_a, wa_ref, dua_ref, dwa_ref, dba_ref)
        half(dv * ga, taps_g, wg_ref, dug_ref, dwg_ref, dbg_ref)

    col, w3, b1 = up_specs[0], w_specs[0], b_specs[0]
    return pl.pallas_call(
        body, name=name, grid=(f // tc,),
        in_specs=[col] + up_specs + w_specs + b_specs, out_specs=[col, col, w3, w3, b1, b1],
        out_shape=[jax.ShapeDtypeStruct((s, f), BF16), jax.ShapeDtypeStruct((s, f), BF16),
                   jax.ShapeDtypeStruct((3, f), F32), jax.ShapeDtypeStruct((3, f), F32),
                   jax.ShapeDtypeStruct((1, f), F32), jax.ShapeDtypeStruct((1, f), F32)],
        compiler_params=_cparams(("parallel",)),
    )(dact, up_a, up_g, conv_w, conv_w, conv_b, conv_b)


def _tri_dot(tri, x):
    b0 = x.astype(BF16)
    r1 = x - b0.astype(F32)
    b1 = r1.astype(BF16)
    b2 = (r1 - b1.astype(F32)).astype(BF16)
    return _dot(tri, b0, NN) + _dot(tri, b1, NN) + _dot(tri, b2, NN)


def _log_sigmoid(x):
    return jnp.minimum(x, 0.0) - jnp.log(1.0 + jnp.exp(-jnp.abs(x)))


def _expand_heads(col16, rows):
    src = lax.broadcasted_iota(I32, (128, HEADS * HEAD_DIM), 0)
    dst = lax.broadcasted_iota(I32, (128, HEADS * HEAD_DIM), 1) // HEAD_DIM
    spread = (src == dst).astype(BF16)
    p0, p1, p2 = _bf16_pieces(col16)
    return (_dot(p0.astype(BF16), spread, NN) + _dot(p1.astype(BF16), spread, NN)) + _dot(p2.astype(BF16), spread, NN)


def _forget_cumsum(f_logit, b_f, *, name):
    s = f_logit.shape[0]
    nb = s // 128

    def body(f_ref, b_ref, cqe_ref):
        row = lax.broadcasted_iota(I32, (128, 128), 0)
        col = lax.broadcasted_iota(I32, (128, 128), 1)
        tri = (col <= row).astype(BF16)

        def step(n, carry):
            r0 = pl.multiple_of(n * 128, 128)
            lf = _log_sigmoid(f_ref[pl.ds(r0, 128), :] + b_ref[...])
            cum = _tri_dot(tri, lf) + carry
            cqe_ref[pl.ds(r0, 128), :] = _expand_heads(cum, 128)
            return cum[127:128, :]

        lax.fori_loop(0, nb, step, jnp.zeros((1, 128), F32))

    return pl.pallas_call(
        body, name=name, grid=(1,),
        in_specs=[_full_spec((s, 128)), _full_spec((1, 128))],
        out_specs=_full_spec((s, HEADS * HEAD_DIM)),
        out_shape=jax.ShapeDtypeStruct((s, HEADS * HEAD_DIM), F32),
        compiler_params=_cparams(("arbitrary",)),
    )(f_logit, b_f)


def _forget_bwd(dcq16, sum_q16, f_logit, b_f, *, name):
    s = f_logit.shape[0]
    nb = s // 128

    def body(a_ref, k_ref, f_ref, b_ref, df_ref, db_ref):
        row = lax.broadcasted_iota(I32, (128, 128), 0)
        col = lax.broadcasted_iota(I32, (128, 128), 1)
        tri_rev = (col >= row).astype(BF16)

        def step(m, carry):
            suffix, dbsum = carry
            n = nb - 1 - m
            r0 = pl.multiple_of(n * 128, 128)
            dcum = a_ref[pl.ds(r0, 128), :] - k_ref[pl.ds(r0, 128), :]
            dlf = _tri_dot(tri_rev, dcum) + suffix
            df = dlf * _sigmoid(-(f_ref[pl.ds(r0, 128), :] + b_ref[...]))
            df_ref[pl.ds(r0, 128), :] = df.astype(BF16)
            return dlf[0:1, :], dbsum + jnp.sum(df, axis=0, keepdims=True)

        _, dbsum = lax.fori_loop(0, nb, step, (jnp.zeros((1, 128), F32), jnp.zeros((1, 128), F32)))
        db_ref[...] = dbsum

    return pl.pallas_call(
        body, name=name, grid=(1,),
        in_specs=[_full_spec((s, 128))] * 3 + [_full_spec((1, 128))],
        out_specs=[_full_spec((s, 128)), _full_spec((1, 128))],
        out_shape=[jax.ShapeDtypeStruct((s, 128), BF16), jax.ShapeDtypeStruct((1, 128), F32)],
        compiler_params=_cparams(("arbitrary",)),
    )(dcq16, sum_q16, f_logit, b_f)


ATT_T = 256


def _head_lanes(rows):
    return lax.broadcasted_iota(I32, (rows, 128), 1) < HEAD_DIM


def _bf16_pieces(c):
    p0 = c.astype(BF16).astype(F32)
    r = c - p0
    p1 = r.astype(BF16).astype(F32)
    p2 = (r - p1).astype(BF16).astype(F32)
    return p0, p1, p2


def _col_reduce(x, op):
    rows = x.shape[0]
    while rows > 8:
        rows //= 2
        x = op(x[:rows], x[rows:])
    return jnp.max(x, axis=0, keepdims=True) if op is jnp.maximum else jnp.sum(x, axis=0, keepdims=True)


def _attn_prep(qkv, cqe, carry=None, *, name):
    s = qkv.shape[0]
    npair = HEADS // 2

    def body(q_ref, k_ref, v_ref, c_ref, qa_ref, ka_ref, vt_ref):
        rows = 128
        lane = lax.broadcasted_iota(I32, (rows, 128), 1)

        def chunk(n, _):
            r0 = pl.multiple_of(n * rows, rows)
            sl = pl.ds(r0, rows)
            qv = q_ref[sl, :].astype(F32) * ATT_SCALE
            kv = k_ref[sl, :].astype(F32)
            p0, p1, p2 = _bf16_pieces(pltpu.roll(c_ref[sl, :], HEAD_DIM, 1))
            for e in range(2):
                mine = (lane < HEAD_DIM) if e == 0 else (lane >= HEAD_DIM)
                base = HEAD_DIM * (1 - e)
                ones_hi = jnp.where((lane >= base + 3) & (lane < base + 6), 1.0, 0.0)
                ones_lo = jnp.where((lane >= base) & (lane < base + 3), 1.0, 0.0)
                qa = jnp.where(mine, qv, jnp.where(lane == base, p0, jnp.where(lane == base + 1, p1,
                               jnp.where(lane == base + 2, p2, ones_hi))))
                ka = jnp.where(mine, kv, jnp.where(lane == base + 3, -p0, jnp.where(lane == base + 4, -p1,
                               jnp.where(lane == base + 5, -p2, ones_lo))))
                qa_ref[e, sl, :] = qa.astype(BF16)
                ka_ref[e, sl, :] = ka.astype(BF16)
            vt_ref[0, :, sl] = v_ref[sl, :].astype(F32).T.astype(BF16)
            return 0

        lax.fori_loop(0, s // rows, chunk, 0)

    pair = pl.BlockSpec((2, s, 128), lambda hp: (hp, 0, 0))
    return _carry_call(
        body, carry, name=name, grid=(npair,),
        in_specs=[pl.BlockSpec((s, 128), lambda hp: (0, hp)), pl.BlockSpec((s, 128), lambda hp: (0, npair + hp)),
                  pl.BlockSpec((s, 128), lambda hp: (0, 2 * npair + hp)), pl.BlockSpec((s, 128), lambda hp: (0, hp))],
        out_specs=[pair, pair, pl.BlockSpec((1, 128, s), lambda hp: (hp, 0, 0))],
        out_shape=[jax.ShapeDtypeStruct((HEADS, s, 128), BF16), jax.ShapeDtypeStruct((HEADS, s, 128), BF16),
                   jax.ShapeDtypeStruct((npair, 128, s), BF16)],
        scratch_shapes=[], args=[qkv, qkv, qkv, cqe])


def _attn_fwd(qa, ka, vt, carry=None, *, name):
    s = qa.shape[1]
    t = 2 * ATT_T
    nq = s // t
    npair = HEADS // 2

    def body(qa_ref, ka_ref, vt_ref, o_ref, lse_ref):
        i = pl.program_id(1)
        krow = lax.broadcasted_iota(I32, (t, t), 0)
        qcol = lax.broadcasted_iota(I32, (t, t), 1)
        sub = lax.broadcasted_iota(I32, (128, t), 0)
        row8 = lax.broadcasted_iota(I32, (8, t), 0)
        qbs = (qa_ref[0], qa_ref[1])
        tk = t

        def step(j, carry, diag):
            c0 = pl.multiple_of(j * tk, tk)
            vtb = vt_ref[0, :, pl.ds(c0, tk)]
            sts = [_dot(ka_ref[e, pl.ds(c0, tk), :], qbs[e], NT) for e in range(2)]
            if diag:
                sts = [jnp.where(krow <= qcol, st, NEG) for st in sts]
            pts, stats = [], []
            for e in range(2):
                m, l, _ = carry[e]
                m_new = jnp.maximum(m, _col_reduce(sts[e], jnp.maximum))
                alpha = jnp.exp(m - m_new)
                pt = jnp.exp(sts[e] - m_new)
                stats.append((m_new, alpha, alpha * l + _col_reduce(pt, jnp.add)))
                pts.append(pt.astype(BF16))
            pvs = [_dot(vtb, pts[e], NN) for e in range(2)]
            return tuple((stats[e][0], stats[e][2], stats[e][1] * carry[e][2] + pvs[e]) for e in range(2))

        init = (jnp.full((1, t), NEG, F32), jnp.zeros((1, t), F32), jnp.zeros((128, t), F32))
        carry = lax.fori_loop(0, i, functools.partial(step, diag=False), (init, init))
        (m0, l0, acc0), (m1, l1, acc1) = step(i, carry, True)
        o_pair = jnp.where(sub < HEAD_DIM, acc0 / l0, acc1 / l1)
        o_ref[...] = o_pair.T.astype(BF16)
        lse_ref[0] = jnp.where(row8 == 0, m0 + jnp.log(l0), jnp.where(row8 == 1, m1 + jnp.log(l1), 0.0))

    return _carry_call(
        body, carry, name=name, grid=(npair, nq),
        in_specs=[pl.BlockSpec((2, t, 128), lambda hp, i: (hp, i, 0)), pl.BlockSpec((2, s, 128), lambda hp, i: (hp, 0, 0)),
                  pl.BlockSpec((1, 128, s), lambda hp, i: (hp, 0, 0))],
        out_specs=[pl.BlockSpec((t, 128), lambda hp, i: (i, hp)), pl.BlockSpec((1, 8, t), lambda hp, i: (hp, 0, i))],
        out_shape=[jax.ShapeDtypeStruct((s, HEADS * HEAD_DIM), BF16), jax.ShapeDtypeStruct((npair, 8, s), F32)],
        scratch_shapes=[], args=[qa, ka, vt])


def _attn_delta(do, o, carry=None, *, name):
    s = do.shape[0]

    def body(do_ref, o_ref, d_ref):
        prod = do_ref[...].astype(F32) * o_ref[...].astype(F32)
        row = lax.broadcasted_iota(I32, (8, 128), 0)
        lane = lax.broadcasted_iota(I32, (8, 128), 1)
        sel = ((row == 0) & (lane < HEAD_DIM) | (row == 1) & (lane >= HEAD_DIM)).astype(BF16)
        p0, p1, p2 = _bf16_pieces(prod)
        d_ref[0] = (_dot(sel, p0.astype(BF16), NT) + _dot(sel, p1.astype(BF16), NT)) + _dot(sel, p2.astype(BF16), NT)

    pair = pl.BlockSpec((s, 128), lambda hp: (0, hp))
    (delta3,), carried = _carry_call(
        body, carry, name=name, grid=(HEADS // 2,), in_specs=[pair, pair],
        out_specs=[pl.BlockSpec((1, 8, s), lambda hp: (hp, 0, 0))],
        out_shape=[jax.ShapeDtypeStruct((HEADS // 2, 8, s), F32)], scratch_shapes=[], args=[do, o])
    return delta3, carried


def _attn_bwd(qa, ka, qkv, do, lse3, delta3, carry=None, *, name):
    s = qa.shape[1]
    t = 2 * ATT_T
    nb = s // t
    npair = HEADS // 2

    def body(qa_ref, ka_ref, v_ref, do_ref, lse_ref, delta_ref, dq_ref, dk_ref, dv_ref, aux_ref, dcq_ref, dqt):
        hp = pl.program_id(0)
        first = _head_lanes(t)
        lane = lax.broadcasted_iota(I32, (t, 128), 1)
        dqt[...] = jnp.zeros_like(dqt)

        @pl.when(hp == 0)
        def _():
            aux_ref[...] = jnp.zeros_like(aux_ref)

        krow = lax.broadcasted_iota(I32, (t, t), 0)
        qcol = lax.broadcasted_iota(I32, (t, t), 1)

        def key_block(j, _):
            c0 = pl.multiple_of(j * t, t)
            vb = v_ref[pl.ds(c0, t), :]
            kbs = (ka_ref[0, pl.ds(c0, t), :], ka_ref[1, pl.ds(c0, t), :])
            kbts = tuple(kb.astype(F32).T.astype(BF16) for kb in kbs)
            vhs = (jnp.where(first, vb, jnp.zeros_like(vb)), jnp.where(first, jnp.zeros_like(vb), vb))

            def query_block(i, carry, diag):
                r0 = pl.multiple_of(i * t, t)
                dob = do_ref[pl.ds(r0, t), :]
                sts = [_dot(kbs[e], qa_ref[e, pl.ds(r0, t), :], NT) for e in range(2)]
                dpts = [_dot(vhs[e], dob, NT) for e in range(2)]
                ptbs, dsbs = [], []
                for e in range(2):
                    st = jnp.where(krow <= qcol, sts[e], NEG) if diag else sts[e]
                    pt = jnp.exp(st - lse_ref[0, e:e + 1, pl.ds(r0, t)])
                    dsbs.append((pt * (dpts[e] - delta_ref[0, e:e + 1, pl.ds(r0, t)])).astype(BF16))
                    ptbs.append(pt.astype(BF16))
                out = []
                for e in range(2):
                    dk_a, dv_a = carry[e]
                    dv_a = dv_a + _dot(ptbs[e], dob, NN)
                    dk_a = dk_a + _dot(dsbs[e], qa_ref[e, pl.ds(r0, t), :], NN)
                    dqt[e, :, pl.ds(r0, t)] += _dot(kbts[e], dsbs[e], NN)
                    out.append((dk_a, dv_a))
                return tuple(out)

            zero = jnp.zeros((t, 128), F32)
            carry = query_block(j, ((zero, zero), (zero, zero)), True)
            (dk0, dv0), (dk1, dv1) = lax.fori_loop(j + 1, nb, functools.partial(query_block, diag=False), carry)
            dk_ref[pl.ds(c0, t), :] = jnp.where(first, dk0, dk1).astype(BF16)
            dv_ref[pl.ds(c0, t), :] = jnp.where(first, dv0, dv1).astype(BF16)
            sum_q = jnp.where(lane == 2 * hp, dk0[:, HEAD_DIM + 3:HEAD_DIM + 4],
                              jnp.where(lane == 2 * hp + 1, dk1[:, 3:4], aux_ref[pl.ds(c0, t), :]))
            aux_ref[pl.ds(c0, t), :] = sum_q
            return 0

        lax.fori_loop(0, nb, key_block, 0)
        sub = lax.broadcasted_iota(I32, (128, s), 0)
        row8 = lax.broadcasted_iota(I32, (8, s), 0)
        dq_ref[...] = (jnp.where(sub < HEAD_DIM, dqt[0], dqt[1]) * ATT_SCALE).T.astype(BF16)
        dcq_ref[0] = jnp.where(row8 == 0, dqt[0, HEAD_DIM:HEAD_DIM + 1, :], jnp.where(row8 == 1, dqt[1, 0:1, :], 0.0))

    def pair_cols(off):
        return pl.BlockSpec((s, 128), lambda hp: (0, off + hp))

    heads = pl.BlockSpec((2, s, 128), lambda hp: (hp, 0, 0))
    rows = pl.BlockSpec((1, 8, s), lambda hp: (hp, 0, 0))
    wide = jax.ShapeDtypeStruct((s, HEADS * HEAD_DIM), BF16)
    return _carry_call(
        body, carry, name=name, grid=(npair,),
        in_specs=[heads, heads, pair_cols(2 * npair), pair_cols(0), rows, rows],
        out_specs=[pair_cols(0), pair_cols(0), pair_cols(0), pl.BlockSpec((s, 128), lambda hp: (0, 0)), rows],
        out_shape=[wide, wide, wide, jax.ShapeDtypeStruct((s, 128), F32), jax.ShapeDtypeStruct((npair, 8, s), F32)],
        scratch_shapes=[pltpu.VMEM((2, 128, s), F32)], args=[qa, ka, qkv, do, lse3, delta3])


def _adam_math(w, g, m, v):
    m = ADAM_B1 * m + (1.0 - ADAM_B1) * g
    v = ADAM_B2 * v + (1.0 - ADAM_B2) * (g * g)
    m_hat = m / (1.0 - ADAM_B1 ** ADAM_STEP)
    v_hat = v / (1.0 - ADAM_B2 ** ADAM_STEP)
    delta = -ADAM_LR * (m_hat / (jnp.sqrt(v_hat) + ADAM_EPS) + ADAM_WD * w)
    return delta, m, v


def _sum_pairs(keep, recv, pos, *, name):
    _, r, c = recv.shape
    tr = _row_tile(r, 1024)

    def body(pos_ref, a_ref, b_ref, o32_ref, o16_ref):
        tot = a_ref[...].astype(F32) + b_ref[...].astype(F32)
        o16_ref[...] = tot.astype(BF16)

        @pl.when(pl.program_id(1) == 2 * pos_ref[0] + pos_ref[1])
        def _():
            o32_ref[...] = tot

    out = pl.BlockSpec((1, tr, c), lambda i, q, pos: (q, i, 0))
    grid_spec = pltpu.PrefetchScalarGridSpec(
        num_scalar_prefetch=1, grid=(r // tr, 4),
        in_specs=[pl.BlockSpec((1, tr, c), lambda i, q, pos: (2 * q + pos[2], i, 0)), out],
        out_specs=[pl.BlockSpec((1, tr, c), lambda i, q, pos: (0, i, 0)), out])
    return pl.pallas_call(
        body, name=name, grid_spec=grid_spec,
        out_shape=[jax.ShapeDtypeStruct((1, r, c), F32), jax.ShapeDtypeStruct((4, r, c), BF16)],
        compiler_params=_cparams(("arbitrary", "arbitrary")),
    )(pos, keep, recv)


def _adam_sharded(psum, recv, w, m, v, *, name, carry=None):
    r, c = w.shape
    rg = psum.shape[1]

    def body(p_ref, r_ref, w_ref, m_ref, v_ref, g_ref, d_ref, mo_ref, vo_ref):
        part = lambda ref, q: ref[q] if rg == r else ref[q, :r, :]
        g = part(p_ref, 0) + part(r_ref, 0).astype(F32) + part(r_ref, 1).astype(F32) + part(r_ref, 2).astype(F32)
        delta, mn, vn = _adam_math(w_ref[...], g, m_ref[...], v_ref[...])
        g_ref[...] = g
        d_ref[...] = delta
        mo_ref[...] = mn
        vo_ref[...] = vn

    if rg == r:
        tr = _row_tile(r, 320)
        grid = (r // tr,)
        row = pl.BlockSpec((tr, c), lambda i: (i, 0))
        sums = lambda n: pl.BlockSpec((n, tr, c), lambda i: (0, i, 0))
    else:
        tc = 256
        grid = (c // tc,)
        row = pl.BlockSpec((r, tc), lambda i: (0, i))
        sums = lambda n: pl.BlockSpec((n, rg, tc), lambda i: (0, 0, i))
    o = jax.ShapeDtypeStruct((r, c), F32)
    return _carry_call(
        body, carry, name=name, grid=grid, in_specs=[sums(1), sums(3), row, row, row], out_specs=[row, row, row, row],
        out_shape=[o, o, o, o], scratch_shapes=[], args=[psum, recv, w, m, v])


def _adam_replicated(chip_sums, last, w, m, v, *, name):
    r = w.shape[0]

    def body(s_ref, l_ref, w_ref, m_ref, v_ref, g_ref, d_ref, mo_ref, vo_ref):
        g = (((s_ref[0] + s_ref[1]) + s_ref[2]) + s_ref[3]) + l_ref[...]
        delta, mn, vn = _adam_math(w_ref[...], g, m_ref[...], v_ref[...])
        g_ref[...] = g
        d_ref[...] = delta
        mo_ref[...] = mn
        vo_ref[...] = vn

    o = jax.ShapeDtypeStruct((r, 1024), F32)
    full = _full_spec((r, 1024))
    return pl.pallas_call(
        body, name=name, grid=(1,),
        in_specs=[_full_spec((4, r, 1024)), full, full, full, full], out_specs=[full] * 4, out_shape=[o] * 4,
        compiler_params=_cparams(("arbitrary",)),
    )(chip_sums, last, w, m, v)


ASM_OUT = 256
ASM_SRC = 304


def _w_in_row(r):
    return r if r < 2048 else (r + O_G - 2048 if r < 4096 else r - 2048)


def _assemble_wt_main(g, *, name):
    table = []
    for blk in range(MAIN_COLS // ASM_OUT):
        j, l0 = divmod(_w_in_row(blk * ASM_OUT), IN_SHARD)
        sb = l0 // ASM_SRC
        n_a = min(ASM_OUT, min(IN_SHARD, (sb + 1) * ASM_SRC) - l0)
        if n_a == ASM_OUT:
            nxt = (j, sb)
        elif l0 + n_a == IN_SHARD:
            nxt = (j + 1, 0)
        else:
            nxt = (j, sb + 1)
        table.append((j, sb, l0 - sb * ASM_SRC, n_a) + nxt)

    def body(tab_ref, a_ref, b_ref, o_ref):
        blk = pl.program_id(0)
        off, n_a = tab_ref[blk, 2], tab_ref[blk, 3]
        r = lax.broadcasted_iota(I32, (ASM_OUT, ASM_SRC), 0)
        k = lax.broadcasted_iota(I32, (ASM_OUT, ASM_SRC), 1)
        sel_a = ((k == r + off) & (r < n_a)).astype(BF16)
        sel_b = ((k == r - n_a) & (r >= n_a)).astype(BF16)
        o_ref[...] = (_dot(sel_a, a_ref[0], NN) + _dot(sel_b, b_ref[0], NN)).astype(BF16)

    src = lambda c: pl.BlockSpec((1, ASM_SRC, D_MODEL), lambda blk, tab: (tab[blk, c], tab[blk, c + 1], 0))
    grid_spec = pltpu.PrefetchScalarGridSpec(
        num_scalar_prefetch=1, grid=(len(table),), in_specs=[src(0), src(4)],
        out_specs=pl.BlockSpec((ASM_OUT, D_MODEL), lambda blk, tab: (blk, 0)))
    return pl.pallas_call(
        body, name=name, grid_spec=grid_spec, out_shape=jax.ShapeDtypeStruct((MAIN_COLS, D_MODEL), BF16),
        compiler_params=_cparams(("parallel",)),
    )(jnp.asarray(table, I32), g, g)


def _pair_sum_small(mine, theirs, *, name):
    def body(a_ref, b_ref, o_ref):
        o_ref[...] = a_ref[...] + b_ref[...]

    full = _full_spec(mine.shape)
    return pl.pallas_call(
        body, name=name, grid=(1,), in_specs=[full, full], out_specs=full,
        out_shape=jax.ShapeDtypeStruct(mine.shape, F32), compiler_params=_cparams(("arbitrary",)),
    )(mine, theirs)


ANY = pl.BlockSpec(memory_space=pl.ANY)
OTHER_CHIPS = ((1, 0), (0, 1), (1, 1))


class _Carry:
    def __init__(self, inputs, out_shapes, scratch, start, wait, aliases=None, middle=None):
        self.inputs, self.out_shapes, self.scratch = list(inputs), list(out_shapes), list(scratch)
        self.start, self.wait, self.aliases, self.middle = start, wait, dict(aliases or {}), middle


def _carry_join(*carries):
    n_in = [len(c.inputs) for c in carries]
    n_out = [len(c.out_shapes) for c in carries]
    n_scr = [len(c.scratch) for c in carries]

    def split(refs, counts):
        out, k = [], 0
        for n in counts:
            out.append(refs[k:k + n])
            k += n
        return out

    def start(ins, outs, scr):
        for c, i, o, s in zip(carries, split(ins, n_in), split(outs, n_out), split(scr, n_scr)):
            c.start(i, o, s)

    def wait(ins, outs, scr):
        for c, i, o, s in zip(carries, split(ins, n_in), split(outs, n_out), split(scr, n_scr)):
            c.wait(i, o, s)

    def middle(ins, outs, scr):
        for c, i, o, s in zip(carries, split(ins, n_in), split(outs, n_out), split(scr, n_scr)):
            if c.middle is not None:
                c.middle(i, o, s)

    aliases = {}
    for k, c in enumerate(carries):
        aliases.update({sum(n_in[:k]) + i: sum(n_out[:k]) + o for i, o in c.aliases.items()})
    joined = _Carry(sum((c.inputs for c in carries), []), sum((c.out_shapes for c in carries), []),
                    sum((c.scratch for c in carries), []), start, wait, aliases,
                    middle if any(c.middle is not None for c in carries) else None)
    joined.counts = n_out
    joined.split = lambda results: split(results, n_out)
    return joined


def _carried(body, carry, n_in, n_out, grid):
    if carry is None:
        return body
    ci, co, cs = len(carry.inputs), len(carry.out_shapes), len(carry.scratch)

    def wrapped(*refs):
        ins, cins = refs[:n_in], refs[n_in:n_in + ci]
        outs, couts = refs[n_in + ci:n_in + ci + n_out], refs[n_in + ci + n_out:n_in + ci + n_out + co]
        rest = refs[n_in + ci + n_out + co:]
        scratch, cscr = rest[:len(rest) - cs], rest[len(rest) - cs:]
        first, last, step, steps = None, None, 0, 1
        for axis, size in enumerate(grid):
            f, l = pl.program_id(axis) == 0, pl.program_id(axis) == size - 1
            first = f if first is None else first & f
            last = l if last is None else last & l
            step, steps = step * size + pl.program_id(axis), steps * size

        @pl.when(first)
        def _():
            carry.start(cins, couts, cscr)

        if carry.middle is not None:
            @pl.when(step == steps // 2)
            def _():
                carry.middle(cins, couts, cscr)

        body(*ins, *outs, *scratch)

        @pl.when(last)
        def _():
            carry.wait(cins, couts, cscr)

    return wrapped


def _carry_call(body, carry, *, name, grid, in_specs, out_specs, out_shape, scratch_shapes, args, vmem=True,
                own_aliases=None):
    n_in, n_out = len(in_specs), len(out_specs)
    extra_in = [ANY] * len(carry.inputs) if carry else []
    extra_out = [ANY] * len(carry.out_shapes) if carry else []
    aliases = dict(own_aliases or {})
    if carry:
        aliases.update({n_in + i: n_out + o for i, o in carry.aliases.items()})
    out = pl.pallas_call(
        _carried(body, carry, n_in, n_out, grid), name=name, grid=grid,
        in_specs=list(in_specs) + extra_in, out_specs=list(out_specs) + extra_out,
        out_shape=list(out_shape) + (carry.out_shapes if carry else []),
        scratch_shapes=list(scratch_shapes) + (carry.scratch if carry else []),
        input_output_aliases=aliases,
        compiler_params=_cparams(("arbitrary",) * len(grid)) if vmem else None,
    )(*args, *(carry.inputs if carry else []))
    return list(out[:n_out]), list(out[n_out:])


def _run_carry(carry, *, name):
    return _carry_call(lambda: None, carry, name=name, grid=(1,), in_specs=[], out_specs=[], out_shape=[],
                       scratch_shapes=[], args=[], vmem=False)[1]


def _sems(n):
    return [pltpu.SemaphoreType.DMA((n,)), pltpu.SemaphoreType.DMA((n,))]


def _carry_gather1(shards):
    n = len(shards)
    per = 7

    def plan(x_refs, out_refs, scr):
        send_sems, recv_sems, local_sems = scr
        x, y, c = lax.axis_index("x"), lax.axis_index("y"), lax.axis_index("c")
        me, sibling = (x, y, c), (x, y, 1 - c)
        near_x, near_y, across = (1 - x, y, c), (x, 1 - y, c), (1 - x, 1 - y, c)

        def rows(ref, t, half):
            r = shards[t].shape[0]
            h = r if r < 32 else -(-(r // 2) // 16) * 16
            if half is None or h == r:
                return ref
            return ref.at[pl.ds(0, h)] if half == 0 else ref.at[pl.ds(h, r - h)]

        def copy(t, k, block, half, to, from_input=False):
            px, py, pc = block
            slab = rows(out_refs[t].at[4 * px + 2 * py + pc], t, half)
            return pltpu.make_async_remote_copy(
                src_ref=rows(x_refs[t], t, half) if from_input else slab, dst_ref=slab,
                send_sem=send_sems.at[per * t + k], recv_sem=recv_sems.at[per * t + k], device_id=to,
                device_id_type=MESH)

        two = [shards[t].shape[0] >= 32 for t in range(n)]
        local = lambda t: pltpu.make_async_copy(x_refs[t], out_refs[t].at[4 * x + 2 * y + c], local_sems.at[t])
        first = lambda t: ([(0, me, None, sibling), (1, me, 0, near_x)]
                           + ([(2, me, 1, near_y), (3, me, 1, near_x)] if two[t] else []) + [(4, me, 0, near_y)])
        passed = lambda t: [(5, near_x, 0, near_y)] + ([(6, near_y, 1, near_x)] if two[t] else [])
        early = lambda t: [(1, near_x, 0, me)] + ([(2, near_y, 1, me)] if two[t] else [])
        late = lambda t: ([(0, sibling, None, me), (4, near_y, 0, me), (5, across, 0, me)]
                          + ([(3, near_x, 1, me), (6, across, 1, me)] if two[t] else []))
        return copy, local, first, passed, early, late

    def start(x_refs, out_refs, scr):
        copy, local, first, _, _, _ = plan(x_refs, out_refs, scr)
        for urgent in (True, False):
            for t in range(n):
                if not urgent:
                    local(t).start()
                for k, block, half, to in first(t):
                    if (k in (1, 2)) == urgent:
                        copy(t, k, block, half, to, from_input=True).start()

    def middle(x_refs, out_refs, scr):
        copy, _, _, passed, early, _ = plan(x_refs, out_refs, scr)
        for t in range(n):
            for (k, block, half, to), fwd in zip(early(t), passed(t)):
                copy(t, k, block, half, to).wait_recv()
                copy(t, *fwd).start()

    def wait(x_refs, out_refs, scr):
        copy, local, first, passed, _, late = plan(x_refs, out_refs, scr)
        for t in range(n):
            for k, block, half, to in late(t):
                copy(t, k, block, half, to).wait_recv()
        for t in range(n):
            for k, block, half, to in first(t):
                copy(t, k, block, half, to, from_input=True).wait_send()
            for k, block, half, to in passed(t):
                copy(t, k, block, half, to).wait_send()
            local(t).wait()

    return _Carry(shards, [jax.ShapeDtypeStruct((N_DEV,) + a.shape, a.dtype) for a in shards],
                  _sems(per * n) + [pltpu.SemaphoreType.DMA((n,))], start, wait, middle=middle)


def _carry_gather2(gathered):
    n = len(gathered)

    def copies(in_refs, g_refs, scr, with_arrivals):
        send_sems, recv_sems = scr
        x, y, c = lax.axis_index("x"), lax.axis_index("y"), lax.axis_index("c")
        sends, arrivals = [], []
        for t in range(n):
            for j, (fx, fy) in enumerate(OTHER_CHIPS):
                px, py = x ^ fx, y ^ fy
                sems = dict(send_sem=send_sems.at[3 * t + j], recv_sem=recv_sems.at[3 * t + j],
                            device_id=(x, y, 1 - c), device_id_type=MESH)
                mine, theirs = 4 * px + 2 * py + c, 4 * px + 2 * py + (1 - c)
                sends.append(pltpu.make_async_remote_copy(src_ref=in_refs[t].at[mine], dst_ref=g_refs[t].at[mine], **sems))
                if with_arrivals:
                    arrivals.append(pltpu.make_async_remote_copy(
                        src_ref=in_refs[t].at[mine], dst_ref=g_refs[t].at[theirs], **sems))
        return sends, arrivals

    def start(in_refs, g_refs, scr):
        for cp in copies(in_refs, g_refs, scr, False)[0]:
            cp.start()

    def wait(in_refs, g_refs, scr):
        sends, arrivals = copies(in_refs, g_refs, scr, True)
        for cp in arrivals:
            cp.wait_recv()
        for cp in sends:
            cp.wait_send()

    return _Carry(gathered, [jax.ShapeDtypeStruct(a.shape, a.dtype) for a in gathered], _sems(3 * n), start, wait,
                  aliases={t: t for t in range(n)})


def _allreduce_rows(x, *, name):
    def body(x_ref, o_ref, sib_ref, mine_ref, tab_ref, send_sems, recv_sems):
        x, y, c = lax.axis_index("x"), lax.axis_index("y"), lax.axis_index("c")
        swap = pltpu.make_async_remote_copy(src_ref=x_ref, dst_ref=sib_ref, send_sem=send_sems.at[0],
                                            recv_sem=recv_sems.at[0], device_id=(x, y, 1 - c), device_id_type=MESH)
        swap.start()
        swap.wait()
        mine_ref[...] = x_ref[...] + sib_ref[...]
        tab_ref[pl.ds(2 * x + y, 1)] = mine_ref[...][None]

        def copy(k, slot):
            fx, fy = OTHER_CHIPS[k]
            return pltpu.make_async_remote_copy(
                src_ref=mine_ref, dst_ref=tab_ref.at[slot], send_sem=send_sems.at[1 + k], recv_sem=recv_sems.at[1 + k],
                device_id=(x ^ fx, y ^ fy, c), device_id_type=MESH)

        for k in range(3):
            copy(k, 2 * x + y).start()
        for k, (fx, fy) in enumerate(OTHER_CHIPS):
            copy(k, 2 * (x ^ fx) + (y ^ fy)).wait()
        o_ref[...] = ((tab_ref[0] + tab_ref[1]) + tab_ref[2]) + tab_ref[3]

    vmem = pl.BlockSpec(memory_space=pltpu.VMEM)
    return pl.pallas_call(
        body, name=name, out_shape=jax.ShapeDtypeStruct(x.shape, F32), in_specs=[vmem], out_specs=vmem,
        scratch_shapes=[pltpu.VMEM(x.shape, F32), pltpu.VMEM(x.shape, F32), pltpu.VMEM((4,) + x.shape, F32)] + _sems(4),
    )(x)


def _carry_rows_to_all(x):
    def copies(x_ref, out_ref, scr, with_arrivals):
        send_sems, recv_sems, local_sems = scr
        x, y, c = lax.axis_index("x"), lax.axis_index("y"), lax.axis_index("c")
        me = 4 * x + 2 * y + c
        local = pltpu.make_async_copy(x_ref, out_ref.at[me], local_sems.at[0])
        sends, arrivals = [], []
        for k in range(1, N_DEV):
            px, py, pc = x ^ (k >> 2), y ^ ((k >> 1) & 1), c ^ (k & 1)
            sems = dict(send_sem=send_sems.at[k - 1], recv_sem=recv_sems.at[k - 1], device_id=(px, py, pc),
                        device_id_type=MESH)
            sends.append(pltpu.make_async_remote_copy(src_ref=x_ref, dst_ref=out_ref.at[me], **sems))
            if with_arrivals:
                arrivals.append(pltpu.make_async_remote_copy(src_ref=x_ref, dst_ref=out_ref.at[4 * px + 2 * py + pc], **sems))
        return local, sends, arrivals

    def start(x_refs, out_refs, scr):
        local, sends, _ = copies(x_refs[0], out_refs[0], scr, False)
        for cp in [local] + sends:
            cp.start()

    def wait(x_refs, out_refs, scr):
        local, sends, arrivals = copies(x_refs[0], out_refs[0], scr, True)
        for cp in arrivals:
            cp.wait_recv()
        for cp in sends:
            cp.wait_send()
        local.wait()

    return _Carry([x], [jax.ShapeDtypeStruct((N_DEV,) + x.shape, x.dtype)],
                  _sems(N_DEV - 1) + [pltpu.SemaphoreType.DMA((1,))], start, wait)


def _sum_devices(table, *, name):
    def body(t_ref, o_ref):
        tot = t_ref[0]
        for j in range(1, N_DEV):
            tot = tot + t_ref[j]
        o_ref[...] = tot

    return pl.pallas_call(
        body, name=name, grid=(1,), in_specs=[_full_spec(table.shape)], out_specs=_full_spec(table.shape[1:]),
        out_shape=jax.ShapeDtypeStruct(table.shape[1:], F32), compiler_params=_cparams(("arbitrary",)),
    )(table)


def _allgather(shards, *, name):
    n = len(shards)
    per = 10

    def body(*refs):
        x_refs, out_refs = refs[:n], refs[n:2 * n]
        send_sems, recv_sems, local_sems = refs[2 * n:]
        x, y, c = lax.axis_index("x"), lax.axis_index("y"), lax.axis_index("c")
        me, sibling = (x, y, c), (x, y, 1 - c)
        near_x, near_y, across = (1 - x, y), (x, 1 - y), (1 - x, 1 - y)

        def rows(ref, t, half):
            r = shards[t].shape[0]
            h = -(-(r // 2) // 16) * 16
            if half is None:
                return ref
            return ref.at[pl.ds(0, h)] if half == 0 else ref.at[pl.ds(h, r - h)]

        def copy(t, k, block, half, to, from_input=False):
            px, py, pc = block
            slab = rows(out_refs[t].at[4 * px + 2 * py + pc], t, half)
            return pltpu.make_async_remote_copy(
                src_ref=rows(x_refs[t], t, half) if from_input else slab, dst_ref=slab,
                send_sem=send_sems.at[per * t + k], recv_sem=recv_sems.at[per * t + k], device_id=to,
                device_id_type=MESH)

        mine = [pltpu.make_async_copy(x_refs[t], out_refs[t].at[4 * x + 2 * y + c], local_sems.at[t]) for t in range(n)]
        for cp in mine:
            cp.start()
        sent = []

        def send(cp):
            cp.start()
            sent.append(cp)

        for t in range(n):
            send(copy(t, 0, me, None, sibling, from_input=True))
            send(copy(t, 1, me, 0, (*near_x, c), from_input=True))
            send(copy(t, 2, me, 1, (*near_y, c), from_input=True))
            send(copy(t, 3, me, 1, (*near_x, c), from_input=True))
            send(copy(t, 4, me, 0, (*near_y, c), from_input=True))
        for t in range(n):
            copy(t, 1, (*near_x, c), 0, me).wait_recv()
            send(copy(t, 5, (*near_x, c), 0, (*near_y, c)))
            copy(t, 2, (*near_y, c), 1, me).wait_recv()
            send(copy(t, 6, (*near_y, c), 1, (*near_x, c)))
        for t in range(n):
            copy(t, 3, (*near_x, c), 1, me).wait_recv()
            send(copy(t, 7, (*near_x, c), None, sibling))
            copy(t, 4, (*near_y, c), 0, me).wait_recv()
            send(copy(t, 8, (*near_y, c), None, sibling))
            copy(t, 5, (*across, c), 0, me).wait_recv()
            copy(t, 6, (*across, c), 1, me).wait_recv()
            send(copy(t, 9, (*across, c), None, sibling))
        for t in range(n):
            copy(t, 0, sibling, None, me).wait_recv()
            for k, chip in ((7, near_x), (8, near_y), (9, across)):
                copy(t, k, (*chip, 1 - c), None, me).wait_recv()
        for cp in sent:
            cp.wait_send()
        for cp in mine:
            cp.wait()

    return pl.pallas_call(
        body, name=name, out_shape=[jax.ShapeDtypeStruct((N_DEV,) + a.shape, a.dtype) for a in shards],
        in_specs=[ANY] * n, out_specs=[ANY] * n,
        scratch_shapes=[pltpu.SemaphoreType.DMA((per * n,)), pltpu.SemaphoreType.DMA((per * n,)),
                        pltpu.SemaphoreType.DMA((n,))],
    )(*shards)


def _carry_sibling(slabs, small=None):
    n = len(slabs)
    extra = [] if small is None else [small]

    def copies(in_refs, out_refs, scr):
        send_sems, recv_sems = scr
        x, y, c = lax.axis_index("x"), lax.axis_index("y"), lax.axis_index("c")
        sibling = (x, y, 1 - c)
        out = []
        for t in range(n):
            for q in range(4):
                out.append(pltpu.make_async_remote_copy(
                    src_ref=in_refs[t].at[2 * q + (1 - c)], dst_ref=out_refs[t].at[q],
                    send_sem=send_sems.at[4 * t + q], recv_sem=recv_sems.at[4 * t + q],
                    device_id=sibling, device_id_type=MESH))
        if extra:
            out.append(pltpu.make_async_remote_copy(
                src_ref=in_refs[n], dst_ref=out_refs[n], send_sem=send_sems.at[4 * n], recv_sem=recv_sems.at[4 * n],
                device_id=sibling, device_id_type=MESH))
        return out

    def start(*refs):
        for cp in copies(*refs):
            cp.start()

    def wait(*refs):
        for cp in copies(*refs):
            cp.wait()

    return _Carry(list(slabs) + extra,
                  [jax.ShapeDtypeStruct((4,) + a.shape[1:], a.dtype) for a in slabs]
                  + [jax.ShapeDtypeStruct(a.shape, a.dtype) for a in extra], _sems(4 * n + 1), start, wait)


def _carry_chips(psums, small_sum=None):
    n = len(psums)
    table = small_sum is not None

    def copies(in_refs, out_refs, scr, arrivals):
        send_sems, recv_sems = scr[0], scr[1]
        x, y, c = lax.axis_index("x"), lax.axis_index("y"), lax.axis_index("c")
        out = []
        for k, (fx, fy) in enumerate(OTHER_CHIPS):
            px, py = x ^ fx, y ^ fy
            for t in range(n):
                out.append(pltpu.make_async_remote_copy(
                    src_ref=in_refs[t].at[2 * px + py], dst_ref=out_refs[t].at[k],
                    send_sem=send_sems.at[3 * t + k], recv_sem=recv_sems.at[3 * t + k],
                    device_id=(px, py, c), device_id_type=MESH))
            if table:
                slot = 2 * px + py if arrivals else 2 * x + y
                out.append(pltpu.make_async_remote_copy(
                    src_ref=in_refs[n], dst_ref=out_refs[n].at[slot], send_sem=send_sems.at[3 * n + k],
                    recv_sem=recv_sems.at[3 * n + k], device_id=(px, py, c), device_id_type=MESH))
        return out

    def own(in_refs, out_refs, scr):
        x, y = lax.axis_index("x"), lax.axis_index("y")
        return pltpu.make_async_copy(in_refs[n], out_refs[n].at[2 * x + y], scr[2])

    def start(in_refs, out_refs, scr):
        if table:
            own(in_refs, out_refs, scr).start()
        for cp in copies(in_refs, out_refs, scr, False):
            cp.start()

    def wait(in_refs, out_refs, scr):
        for cp in copies(in_refs, out_refs, scr, True):
            cp.wait()
        if table:
            own(in_refs, out_refs, scr).wait()

    out_shapes = [jax.ShapeDtypeStruct((3,) + a.shape[1:], a.dtype) for a in psums]
    if table:
        out_shapes.append(jax.ShapeDtypeStruct((4,) + small_sum.shape, F32))
    return _Carry(list(psums) + ([small_sum] if table else []), out_shapes,
                  _sems(3 * n + 3) + ([pltpu.SemaphoreType.DMA] if table else []), start, wait)


def _to_comm(name, kind, block, dtype=BF16):
    a = block[0]
    if kind == "cols":
        a = a.T
        if name == "w_in" and dtype == BF16:
            a = jnp.pad(a, ((0, IN_SHARD_PAD - IN_SHARD), (0, 0)))
    return a if kind == "f32" else a.astype(dtype)


def _from_comm(name, kind, a):
    if kind == "cols":
        if name == "w_in" and a.shape[0] != IN_SHARD:
            a = a[:IN_SHARD]
        a = a.T
    return a[None]


def _assemble_weights(g):
    out = {}
    if "w_in" in g:
        out["wt_main"] = _assemble_wt_main(g["w_in"], name="assemble_w_in")
        j, l0 = divmod(O_F, IN_SHARD)
        out["wt_f"] = jnp.pad(g["w_in"][j, l0:l0 + HEADS], ((0, 128 - HEADS), (0, 0)))
    square = dict(w_branch_a="w_a", w_branch_b="w_b", w_out="w_out", w_ple_gate="w_pg")
    for long, short in square.items():
        if long in g:
            out[short] = g[long].reshape(D_MODEL, D_MODEL)
    if "w_up" in g:
        out["wt_up"] = g["w_up"].reshape(2 * D_FF, D_MODEL)
    if "conv_w" in g:
        out["conv_w"] = g["conv_w"].transpose(1, 0, 2).reshape(3, 2 * D_FF)
    if "w_down" in g:
        out["w_down"] = g["w_down"].reshape(D_FF, D_MODEL)
    if "w_ple" in g:
        out["wt_ple"] = g["w_ple"].reshape(D_MODEL, PLE_DIM)
    return out


def _grad_slabs(gr):
    out = {}
    if "wt_main" in gr:
        gm, gf = gr["wt_main"], gr["wt_f"]
        segments = ((0, 2048, gm, 0), (2048, O_F, gm, 2048), (O_F, O_G, gf, -O_F), (O_G, IN_COLS, gm, 2048 - O_G))
        slabs = []
        for j in range(N_DEV):
            lo, hi = j * IN_SHARD, (j + 1) * IN_SHARD
            pieces = [src[max(lo, a) + shift:min(hi, b) + shift] for a, b, src, shift in segments if max(lo, a) < min(hi, b)]
            pieces.append(jnp.zeros((IN_SHARD_PAD - IN_SHARD, D_MODEL), gm.dtype))
            slabs.append(jnp.concatenate(pieces, axis=0))
        out["w_in"] = jnp.stack(slabs)
    rows = dict(w_a="w_branch_a", w_b="w_branch_b", w_out="w_out", wt_up="w_up", w_down="w_down", w_pg="w_ple_gate")
    for short, long in rows.items():
        if short in gr:
            out[long] = gr[short].reshape(N_DEV, -1, D_MODEL)
    if "conv_w" in gr:
        out["conv_w"] = gr["conv_w"].reshape(3, N_DEV, -1).transpose(1, 0, 2)
    if "wt_ple" in gr:
        out["w_ple"] = gr["wt_ple"].reshape(N_DEV, -1, PLE_DIM)
    return {k: v.astype(BF16) for k, v in out.items()}


def _rows(a, rows):
    flat = a.reshape(-1)
    return jnp.pad(flat, (0, rows * 1024 - flat.shape[0])).reshape(rows, 1024)


def _pack_small(parts):
    return jnp.concatenate([_rows(parts[n].astype(F32), r) for n, r in SMALL], axis=0)


def _small(packed, name, shape):
    off, r = SMALL_OFF[name]
    n = math.prod(shape)
    return packed[off:off + r].reshape(-1)[:n].reshape(shape)


class _Exchanges:
    W_S_ROWS = SMALL_OFF["gmlp_w_s"]

    def __init__(self, later, shards, pos):
        self.later, self.shards, self.pos = later, dict(zip(later, shards)), pos
        self.level1, self.slabs, self.from_sib, self.sums32, self.reduced, self.tables = {}, {}, {}, {}, {}, {}

    def gather1(self, names):
        carry = _carry_gather1([self.shards[n] for n in names])
        carry.names = names
        return carry

    def gather1_done(self, carry, results):
        self.level1.update(zip(carry.names, results))

    def gather2(self):
        return _carry_gather2([self.level1[n] for n in self.later])

    def weights(self, full):
        return _assemble_weights(dict(zip(self.later, full)))

    def sibling(self, grads):
        slabs = _grad_slabs(grads)
        self.slabs.update(slabs)
        carry = _carry_sibling(list(slabs.values()))
        carry.names = list(slabs)
        return carry

    def sibling_done(self, carry, results):
        self.from_sib.update(zip(carry.names, results))

    def chips(self, names, table=None):
        sums = {n: _sum_pairs(self.slabs[n], self.from_sib[n], self.pos, name="sum_sibling_" + n) for n in names}
        self.sums32.update({n: s32 for n, (s32, _) in sums.items()})
        carry = _carry_chips([s16 for _, s16 in sums.values()], None if table is None else self.table_part(table))
        carry.names, carry.table = list(names), table
        return carry

    def chips_done(self, carry, results):
        if carry.table is not None:
            *results, self.tables[carry.table] = results
        self.reduced.update({n: (self.sums32[n], r) for n, r in zip(carry.names, results)})

    def sibling_small(self, small_g):
        self.small_g = small_g
        return _carry_sibling([], small_g)

    def sibling_small_done(self, small_sib):
        self.small_chip = _pair_sum_small(self.small_g, small_sib, name="sum_sibling_small")

    def table_part(self, which):
        off, rows = self.W_S_ROWS
        if which == "w_s":
            return self.small_chip[off:off + rows]
        return jnp.concatenate([self.small_chip[:off], self.small_chip[off + rows:]], axis=0)

    def table(self):
        off = self.W_S_ROWS[0]
        rest = self.tables["rest"]
        return jnp.concatenate([rest[:, :off], self.tables["w_s"], rest[:, off:]], axis=1)


def _local_step(x, p, target, w, sm, ex=None):
    s = x.shape[0]
    mm = _matmul
    wt_main = w["wt_main"]
    conv_b = sm["conv_b"]
    bs_t = jnp.pad(sm["gmlp_b_s"].T, ((0, 0), (0, 128 - GROUPS)))
    b_f = jnp.pad(sm["b_f"], ((0, 0), (0, 128 - HEADS)))
    big = dict(tm=1024, tn=1024, tk=1024)
    whole_s = dict(tn=1024, tk=s)

    h = _rmsnorm_fwd(x, sm["norm_mix_g"], name="norm_mix")
    tall = dict(tm=s, tn=512, tk=1024)
    qkv_args = dict(mode="nt", out_dtype=BF16, name="in_qkv", n=3072, b_off=8, **tall)
    f_logit = mm(h, w["wt_f"], mode="nt", out_dtype=F32, name="in_f", tm=1024, tk=1024)
    cqe = _forget_cumsum(f_logit, b_f, name="forget_cumsum")
    uvg = dict(mode="nt", out_dtype=F32, name="in_uvg", n=4096, **tall)
    if ex is None:
        qkv = mm(h, wt_main, **qkv_args)
        (qa, ka, vt), _ = _attn_prep(qkv, cqe, name="attn_prep")
        (b, lse3), _ = _attn_fwd(qa, ka, vt, name="attn_fwd")
        zuvg = mm(h, wt_main, **uvg)
    else:
        groups = (["w_branch_a"], ["w_branch_b"], [n for n in ex.later if n not in ("w_branch_a", "w_branch_b")])
        carries = [ex.gather1(names) for names in groups]
        qkv, got0 = mm(h, wt_main, carry=carries[0], **qkv_args)
        (qa, ka, vt), got1 = _attn_prep(qkv, cqe, carries[1], name="attn_prep")
        (b, lse3), got2 = _attn_fwd(qa, ka, vt, carries[2], name="attn_fwd")
        for carry, got in zip(carries, (got0, got1, got2)):
            ex.gather1_done(carry, got)
        zuvg, full = mm(h, wt_main, carry=ex.gather2(), **uvg)
        w = {**w, **ex.weights(full)}
    a = _gmlp_fwd(zuvg, sm["gmlp_ln_g"], sm["gmlp_ln_b"], sm["gmlp_w_s"], bs_t, name="gmlp_fwd")
    wt_up, conv_w = w["wt_up"], w["conv_w"]
    ya, yb, merged = _branches_merge(a, b, w["w_a"], w["w_b"], zuvg, name="branches_merge")
    x1, h2 = mm(merged, w["w_out"], mode="nn", out_dtype=F32, name="out_proj", add=x, norm_g=sm["norm_ffn_g"], **big)
    up_a, up_g, act = _up_convglu(h2, wt_up, conv_w, conv_b, name="up_convglu")
    x2, h3 = mm(act, w["w_down"], mode="nn", out_dtype=F32, name="down", tm=1024, tn=1024, tk=1408, add=x1,
                norm_g=sm["norm_ple_g"])

    loss, dx3, dple, dgp, d_norm_final = _ple_loss(p, w["wt_ple"], h3, w["w_pg"], x2, target, sm["norm_final_g"],
                                                   name="ple_loss")
    g_wt_ple = mm(dple, p, mode="tn", out_dtype=BF16, name="d_w_ple", tm=512, tn=256, tk=s)
    g_w_pg = mm(h3, dgp, mode="tn", out_dtype=BF16, name="d_w_pg", tm=512, **whole_s)
    (dx2, dx2b, d_norm_ple), _ = _matmul_rmsnorm_bwd([dgp], w["w_pg"], dx3, x2, sm["norm_ple_g"], mode="nt", tk=1024,
                                                     name="d_h3_norm_ple_bwd")
    g_w_down = mm(act, dx2b, mode="tn", out_dtype=BF16, name="d_w_down", tm=1408, **whole_s)
    dact_args = dict(mode="nt", out_dtype=BF16, name="d_act", tm=s, tn=256, tk=1024)
    if ex is None:
        dact = mm(dx2b, w["w_down"], **dact_args)
    else:
        early = ex.sibling(dict(w_pg=g_w_pg, wt_ple=g_wt_ple))
        dact, got = mm(dx2b, w["w_down"], carry=early, **dact_args)
        ex.sibling_done(early, got)
    dup_a, dup_g, dcw_a, dcw_g, dcb_a, dcb_g = _convglu_bwd(dact, up_a, up_g, conv_w, conv_b, name="convglu_bwd")
    g_wt_up = mm(dup_a, h2, mode="tn", out_dtype=BF16, name="d_w_up_a", tm=1408, out_rows=2 * D_FF, **whole_s)
    g_wt_up = mm(dup_g, h2, mode="tn", out_dtype=BF16, name="d_w_up_g", tm=1408, out_rows=2 * D_FF,
                 o_off=D_FF // 1408, into=g_wt_up, **whole_s)
    (dx1, dx1b, d_norm_ffn), _ = _matmul_rmsnorm_bwd([dup_a, dup_g], wt_up, dx2, x1, sm["norm_ffn_g"], mode="nn",
                                                     tk=1408, name="d_h2_norm_ffn_bwd", resident=True)
    g_w_out = mm(merged, dx1b, mode="tn", out_dtype=BF16, name="d_w_out", tm=512, **whole_s)
    dya, dyb, dga, dgb = _merge_bwd(dx1b, w["w_out"], ya, yb, zuvg, name="merge_bwd")
    g_w_a = mm(a, dya, mode="tn", out_dtype=BF16, name="d_w_a", tm=512, **whole_s)
    g_w_b = mm(b, dyb, mode="tn", out_dtype=BF16, name="d_w_b", tm=512, **whole_s)
    da = mm(dya, w["w_a"], mode="nt", out_dtype=BF16, name="d_a", **big)
    db = mm(dyb, w["w_b"], mode="nt", out_dtype=BF16, name="d_b", **big)
    grads = dict(w_a=g_w_a, w_b=g_w_b, w_out=g_w_out, wt_up=g_wt_up, conv_w=jnp.concatenate([dcw_a, dcw_g], axis=1),
                 w_down=g_w_down, wt_ple=g_wt_ple, w_pg=g_w_pg)
    gmlp_args = (da, zuvg, sm["gmlp_ln_g"], sm["gmlp_ln_b"], sm["gmlp_w_s"], bs_t)
    if ex is None:
        (dzu, dzv, d_w_s, d_bs_t, d_ln_g, d_ln_b), _ = _gmlp_bwd(*gmlp_args, name="gmlp_bwd")
    else:
        rest = ex.sibling({k: v for k, v in grads.items() if k not in ("w_pg", "wt_ple")})
        early_chips = ex.chips(early.names)
        both = _carry_join(rest, early_chips)
        (dzu, dzv, d_w_s, d_bs_t, d_ln_g, d_ln_b), got = _gmlp_bwd(*gmlp_args, both, name="gmlp_bwd")
        got_rest, got_early = both.split(got)
        ex.sibling_done(rest, got_rest)
        ex.chips_done(early_chips, got_early)
    small = dict(norm_mix_g=jnp.zeros((1, D_MODEL), F32), b_f=jnp.zeros((1, HEADS), F32), gmlp_ln_g=d_ln_g,
                 gmlp_ln_b=d_ln_b, gmlp_w_s=d_w_s, gmlp_b_s=d_bs_t[:, :GROUPS].T, norm_ffn_g=d_norm_ffn,
                 conv_b=jnp.concatenate([dcb_a, dcb_g], axis=1), norm_ple_g=d_norm_ple, norm_final_g=d_norm_final)
    if ex is None:
        delta3, _ = _attn_delta(db, b, name="attn_delta")
        (dq, dk, dv, aux, dcq3), _ = _attn_bwd(qa, ka, qkv, db, lse3, delta3, name="attn_bwd")
    else:
        delta3, (small_sib,) = _attn_delta(db, b, ex.sibling_small(_pack_small(small)), name="attn_delta")
        ex.sibling_small_done(small_sib)
        main_chips = ex.chips(rest.names, table="rest")
        (dq, dk, dv, aux, dcq3), got = _attn_bwd(qa, ka, qkv, db, lse3, delta3, main_chips, name="attn_bwd")
        ex.chips_done(main_chips, got)
    dcq16 = jnp.pad(dcq3[:, :2, :].reshape(HEADS, s).T, ((0, 0), (0, 128 - HEADS)))
    dzf, d_b_f = _forget_bwd(dcq16, aux, f_logit, b_f, name="forget_bwd")
    dz_parts = [dzu, dzv, dga, dgb, dq, dk, dv]
    w_s_chips = None if ex is None else ex.chips([], table="w_s")
    g_wt_main, got = _grad_w_parts(dz_parts, h, name="d_w_main", tm=512, carry=w_s_chips)
    if ex is not None:
        ex.chips_done(w_s_chips, got)
    g_wt_f = mm(dzf, h, mode="tn", out_dtype=BF16, name="d_w_f", **whole_s)
    grads = dict(grads, wt_main=g_wt_main, wt_f=g_wt_f)
    w_in_chips = None
    if ex is not None:
        w_in_sib = ex.sibling(dict(wt_main=g_wt_main, wt_f=g_wt_f))
        ex.sibling_done(w_in_sib, _run_carry(w_in_sib, name="exchange_sibling_w_in"))
        w_in_chips = ex.chips(w_in_sib.names)
    (dx0, _, d_norm_mix), got = _matmul_rmsnorm_bwd(dz_parts, wt_main, dx1, x, sm["norm_mix_g"], mode="nn", tk=1024,
                                                    extra=(dzf, w["wt_f"]), name="d_h_norm_mix_bwd", carry=w_in_chips,
                                                    lead=True)
    if ex is not None:
        ex.chips_done(w_in_chips, got)
    return loss, dx0, grads, dict(small, norm_mix_g=d_norm_mix, b_f=d_b_f[:, :HEADS])


def kernel(x, p, norm_mix_g, w_in, b_f, gmlp_ln_g, gmlp_ln_b, gmlp_w_s, gmlp_b_s, w_branch_a, w_branch_b, w_out, norm_ffn_g, w_up, conv_w, conv_b, w_down, norm_ple_g, w_ple, w_ple_gate, norm_final_g, loss_target, m_norm_mix_g, m_w_in, m_b_f, m_gmlp_ln_g, m_gmlp_ln_b, m_gmlp_w_s, m_gmlp_b_s, m_w_branch_a, m_w_branch_b, m_w_out, m_norm_ffn_g, m_w_up, m_conv_w, m_conv_b, m_w_down, m_norm_ple_g, m_w_ple, m_w_ple_gate, m_norm_final_g, v_norm_mix_g, v_w_in, v_b_f, v_gmlp_ln_g, v_gmlp_ln_b, v_gmlp_w_s, v_gmlp_b_s, v_w_branch_a, v_w_branch_b, v_w_out, v_norm_ffn_g, v_w_up, v_conv_w, v_conv_b, v_w_down, v_norm_ple_g, v_w_ple, v_w_ple_gate, v_norm_final_g):
    given = dict(locals())
    weights = {n: given[n] for n in WEIGHT_ORDER}
    mom_m = {n: given["m_" + n] for n in WEIGHT_ORDER}
    mom_v = {n: given["v_" + n] for n in WEIGHT_ORDER}
    pos = jnp.stack([lax.axis_index("x"), lax.axis_index("y"), lax.axis_index("c")]).astype(I32)
    names = [n for n, _ in SHARDED]
    kinds = dict(SHARDED)

    later = [n for n in names if n != "w_in"]

    first = _allgather([_to_comm("w_in", kinds["w_in"], weights["w_in"])], name="allgather_w_in")
    ex = _Exchanges(later, [_to_comm(n, kinds[n], weights[n]) for n in later], pos)

    sm = dict(norm_mix_g=norm_mix_g, b_f=b_f, gmlp_ln_g=gmlp_ln_g, gmlp_ln_b=gmlp_ln_b, gmlp_w_s=gmlp_w_s[0],
              gmlp_b_s=gmlp_b_s[0], norm_ffn_g=norm_ffn_g, conv_b=conv_b, norm_ple_g=norm_ple_g,
              norm_final_g=norm_final_g.reshape(1, D_MODEL))
    loss_part, dx0, grads, small = _local_step(
        x[0], p[0, 0], loss_target[0], _assemble_weights({"w_in": first[0]}), sm, ex)

    b_f_and_loss = jnp.concatenate([small["b_f"].reshape(-1), loss_part[0, :1]])
    to_all = _carry_rows_to_all(jnp.concatenate([_rows(small["norm_mix_g"], 8), _rows(b_f_and_loss, 8)], axis=0))

    grad, delta, new_m, new_v = {}, {}, {}, {}
    for n in names:
        s32, r = ex.reduced[n]
        outs, got = _adam_sharded(s32, r, *[_to_comm(n, kinds[n], src[n], F32) for src in (weights, mom_m, mom_v)],
                                  name="adam_" + n, carry=to_all if n == "w_in" else None)
        if n == "w_in":
            last = _sum_devices(got[0], name="sum_last")
        grad[n], delta[n], new_m[n], new_v[n] = [_from_comm(n, kinds[n], o) for o in outs]
    loss = last[8, HEADS]
    small_last = jnp.pad(last, ((0, SMALL_ROWS - 16), (0, 0)))
    replicated = [n for n, _ in SMALL]
    rep = lambda src: _pack_small({n: src[n] for n in replicated})
    packed = _adam_replicated(ex.table(), small_last, rep(weights), rep(mom_m), rep(mom_v), name="adam_replicated")
    for out, pk in zip((grad, delta, new_m, new_v), packed):
        for n in replicated:
            out[n] = _small(pk, n, weights[n].shape)

    return (loss, dx0, *[grad[n] for n in WEIGHT_ORDER], *[delta[n] for n in WEIGHT_ORDER],
            *[new_m[n] for n in WEIGHT_ORDER], *[new_v[n] for n in WEIGHT_ORDER])
```

```python
import functools
import math

import jax
import jax.numpy as jnp
from jax import lax
from jax.experimental import pallas as pl
from jax.experimental.pallas import tpu as pltpu

F32 = jnp.float32
BF16 = jnp.bfloat16
I32 = jnp.int32

D_MODEL = 1024
GROUPS = 8
GDIM = 128
GBLOCK = 128
CHUNK = 64
HEADS = 16
HEAD_DIM = 64
D_FF = 2816
PLE_DIM = 256
EPS = 1e-6
N_DEV = 8
ATT_SCALE = HEAD_DIM ** -0.5
NEG = -1e30

ADAM_LR = 0.001
ADAM_B1 = 0.9
ADAM_B2 = 0.999
ADAM_EPS = 1e-08
ADAM_WD = 0.01
ADAM_STEP = 10

V7X_VMEM_LIMIT = 48 * 1024 * 1024
MESH = pl.DeviceIdType.MESH

O_F = 2 * 1024 + 3 * 1024
O_G = O_F + HEADS
IN_COLS = O_G + 2 * D_MODEL
MAIN_COLS = IN_COLS - HEADS
IN_SHARD = IN_COLS // N_DEV
IN_SHARD_PAD = 912

SHARDED = (("w_in", "cols"), ("w_branch_a", "rows"), ("w_branch_b", "rows"), ("w_out", "rows"), ("w_up", "cols"),
           ("conv_w", "f32"), ("w_down", "rows"), ("w_ple", "cols"), ("w_ple_gate", "rows"))

SMALL = (("norm_mix_g", 8), ("b_f", 8), ("gmlp_ln_g", 8), ("gmlp_ln_b", 8), ("gmlp_w_s", 128), ("gmlp_b_s", 8),
         ("norm_ffn_g", 8), ("conv_b", 8), ("norm_ple_g", 8), ("norm_final_g", 8))
SMALL_OFF = {}
_o = 0
for _n, _r in SMALL:
    SMALL_OFF[_n] = (_o, _r)
    _o += _r
SMALL_ROWS = _o

WEIGHT_ORDER = ("norm_mix_g", "w_in", "b_f", "gmlp_ln_g", "gmlp_ln_b", "gmlp_w_s", "gmlp_b_s", "w_branch_a",
                "w_branch_b", "w_out", "norm_ffn_g", "w_up", "conv_w", "conv_b", "w_down", "norm_ple_g", "w_ple",
                "w_ple_gate", "norm_final_g")


def _cparams(sem):
    return pltpu.CompilerParams(dimension_semantics=sem, vmem_limit_bytes=V7X_VMEM_LIMIT)


def _gelu(x):
    c = math.sqrt(2.0 / math.pi)
    return 0.5 * x * (1.0 + jnp.tanh(c * (x + 0.044715 * x * x * x)))


def _gelu_and_grad(x):
    c = math.sqrt(2.0 / math.pi)
    t = jnp.tanh(c * (x + 0.044715 * x * x * x))
    g = 0.5 * x * (1.0 + t)
    dg = 0.5 * (1.0 + t) + 0.5 * x * (1.0 - t * t) * (c * (1.0 + 3.0 * 0.044715 * x * x))
    return g, dg


def _sigmoid(x):
    return 1.0 / (1.0 + jnp.exp(-x))


def _dot(a, b, dims):
    return lax.dot_general(a, b, (dims, ((), ())), preferred_element_type=F32)


NN = ((1,), (0,))
NT = ((1,), (1,))
TN = ((0,), (0,))


def _row_tile(rows, most):
    best = None
    for t in range(16, min(rows, most) + 1, 16):
        if rows % t == 0:
            best = t
    return best if best is not None else rows


def _matmul(a, b, *, mode, out_dtype, name, tm=512, tn=512, tk=512, add=None, n=None, b_off=0,
            out_rows=None, o_off=0, into=None, norm_g=None, carry=None):
    if mode == "tn":
        kdim, m = a.shape
    else:
        m, kdim = a.shape
    if n is None:
        n = b.shape[0] if mode == "nt" else b.shape[1]
    tm, tn, tk = min(tm, m), min(tn, n), min(tk, kdim)
    assert m % tm == 0 and n % tn == 0 and kdim % tk == 0, (name, m, n, kdim, tm, tn, tk)
    nk = kdim // tk
    dims = {"nn": NN, "nt": NT, "tn": TN}[mode]

    n_in = 2 + (add is not None) + (into is not None) + (norm_g is not None)
    assert norm_g is None or tn == n, "the RMS norm needs whole rows"

    def finish(r, refs):
        if add is not None:
            r = refs[2][...].astype(F32) + r
        refs[n_in][...] = r.astype(out_dtype)
        if norm_g is not None:
            rs = lax.rsqrt(jnp.mean(r * r, axis=-1, keepdims=True) + EPS)
            refs[n_in + 1][...] = ((r * rs) * refs[n_in - 1][...]).astype(BF16)

    def body(*refs):
        a_ref, b_ref = refs[:2]
        part = _dot(a_ref[...].astype(BF16), b_ref[...].astype(BF16), dims)
        if nk == 1:
            finish(part, refs)
            return
        acc_ref = refs[-1]
        k = pl.program_id(2)

        @pl.when(k == 0)
        def _():
            acc_ref[...] = part

        @pl.when((k > 0) & (k < nk - 1))
        def _():
            acc_ref[...] += part

        @pl.when(k == nk - 1)
        def _():
            finish(acc_ref[...] + part, refs)

    a_spec = pl.BlockSpec((tk, tm), lambda i, j, k: (k, i)) if mode == "tn" else pl.BlockSpec((tm, tk), lambda i, j, k: (i, k))
    if mode == "nt":
        b_spec = pl.BlockSpec((tn, tk), lambda i, j, k: (j + b_off, k))
    else:
        b_spec = pl.BlockSpec((tk, tn), lambda i, j, k: (k + b_off, j))
    o_spec = pl.BlockSpec((tm, tn), lambda i, j, k: (i + o_off, j))
    in_specs = [a_spec, b_spec] + ([pl.BlockSpec((tm, tn), lambda i, j, k: (i, j))] if add is not None else [])
    args = (a, b) + ((add,) if add is not None else ())
    aliases = {}
    if into is not None:
        aliases = {len(args): 0}
        in_specs.append(pl.BlockSpec(memory_space=pl.ANY))
        args += (into,)
    out_specs = [o_spec]
    out_shape = [jax.ShapeDtypeStruct((m if out_rows is None else out_rows, n), out_dtype)]
    if norm_g is not None:
        in_specs.append(pl.BlockSpec((1, n), lambda i, j, k: (0, 0)))
        args += (norm_g,)
        out_specs.append(pl.BlockSpec((tm, tn), lambda i, j, k: (i, j)))
        out_shape.append(jax.ShapeDtypeStruct((m, n), BF16))
    outs, carried = _carry_call(
        body, carry, name=name, grid=(m // tm, n // tn, nk), in_specs=in_specs, out_specs=out_specs,
        out_shape=out_shape, scratch_shapes=[pltpu.VMEM((tm, tn), F32)] if nk > 1 else [], args=args,
        own_aliases=aliases)
    out = outs[0] if norm_g is None else tuple(outs)
    return out if carry is None else (out, carried)


def _row_spec(tr, width, col_block=0):
    return pl.BlockSpec((tr, width), lambda i: (i, col_block))


def _full_spec(shape):
    return pl.BlockSpec(shape, lambda i: tuple(0 for _ in shape))


def _rmsnorm_fwd(x, g, *, name, tr=256):
    s, d = x.shape

    def body(x_ref, g_ref, o_ref):
        xv = x_ref[...]
        r = lax.rsqrt(jnp.mean(xv * xv, axis=-1, keepdims=True) + EPS)
        o_ref[...] = ((xv * r) * g_ref[...]).astype(BF16)

    return pl.pallas_call(
        body, name=name, grid=(s // tr,),
        in_specs=[_row_spec(tr, d), _full_spec((1, d))], out_specs=_row_spec(tr, d),
        out_shape=jax.ShapeDtypeStruct((s, d), BF16), compiler_params=_cparams(("parallel",)),
    )(x, g)


def _matmul_rmsnorm_bwd(a_parts, b, dres, x, g, *, mode, tk, name, extra=None, tm=512, carry=None, lead=False,
                        resident=False):
    s, d = x.shape
    n_row = s // tm
    spans, lo = [], 0
    for a in a_parts:
        spans.append((lo, lo + a.shape[1] // tk))
        lo = spans[-1][1]
    n_main, total = lo, lo + (extra is not None)
    n_parts = len(a_parts)

    def body(*refs):
        a_refs, b_ref = refs[:n_parts], refs[n_parts]
        k0 = n_parts + 1
        ax_ref, bx_ref = (refs[k0], refs[k0 + 1]) if extra is not None else (None, None)
        k0 += 2 * (extra is not None)
        dres_ref, x_ref, g_ref, dx_ref, dxb_ref, dg_ref, acc_all = refs[k0:k0 + 7]
        if resident:
            kk, i = pl.program_id(0), pl.program_id(1)
            acc_ref = acc_all.at[pl.ds(pl.multiple_of(i * tm, tm), tm)]
        else:
            i, kk = pl.program_id(0), pl.program_id(1)
            acc_ref = acc_all

        def accumulate(part, first):
            if first:
                @pl.when(kk == 0)
                def _():
                    acc_ref[...] = part

                @pl.when(kk > 0)
                def _():
                    acc_ref[...] += part
            else:
                acc_ref[...] += part

        for p, (a_ref, (lo_p, hi_p)) in enumerate(zip(a_refs, spans)):
            @pl.when((kk >= lo_p) & (kk < hi_p))
            def _(a_ref=a_ref, lo_p=lo_p):
                accumulate(_dot(a_ref[...].astype(BF16), b_ref[...].astype(BF16), NN if mode == "nn" else NT), lo_p == 0)

        if extra is not None:
            @pl.when(kk == n_main)
            def _():
                accumulate(_dot(ax_ref[...].astype(BF16), bx_ref[...].astype(BF16), NN), False)

        @pl.when(kk == total - 1)
        def _():
            dhv = acc_ref[...]
            xv = x_ref[...]
            r = lax.rsqrt(jnp.mean(xv * xv, axis=-1, keepdims=True) + EPS)
            xhat = xv * r
            dxhat = dhv * g_ref[...]
            dx = dres_ref[...] + r * (dxhat - xhat * jnp.mean(dxhat * xhat, axis=-1, keepdims=True))
            dx_ref[...] = dx
            dxb_ref[...] = dx.astype(BF16)
            dgp = jnp.sum(dhv * xhat, axis=0, keepdims=True)

            @pl.when(i == 0)
            def _():
                dg_ref[...] = dgp

            @pl.when(i > 0)
            def _():
                dg_ref[...] += dgp

    def spec(shape, index):
        return pl.BlockSpec(shape, (lambda kk, i: index(i, kk)) if resident else index)

    def row(i, kk, lo_p, hi_p):
        if not resident:
            return i
        return jnp.where(kk < lo_p, 0, jnp.where(kk >= hi_p, n_row - 1, i))

    a_specs = [spec((tm, tk), lambda i, kk, lo_p=lo_p, hi_p=hi_p: (row(i, kk, lo_p, hi_p),
                                                                    jnp.clip(kk - lo_p, 0, hi_p - lo_p - 1)))
               for lo_p, hi_p in spans]
    step = lambda kk: jnp.minimum(kk, n_main - 1)
    b_spec = (spec((tk, d), lambda i, kk: (step(kk), 0)) if mode == "nn"
              else spec((d, tk), lambda i, kk: (0, step(kk))))
    rows = spec((tm, d), lambda i, kk: (row(i, kk, total - 1, total), 0))
    one = spec((1, d), lambda i, kk: (0, 0))
    dx_spec, dx_shape = rows, jax.ShapeDtypeStruct((s, d), F32)
    if lead:
        dx_spec = spec((None, tm, d), lambda i, kk: (0, row(i, kk, total - 1, total), 0))
        dx_shape = jax.ShapeDtypeStruct((1, s, d), F32)
    x_specs, x_args = [], []
    if extra is not None:
        kx = extra[0].shape[1]
        x_specs = [spec((tm, kx), lambda i, kk: (row(i, kk, n_main, total), 0)), spec((kx, d), lambda i, kk: (0, 0))]
        x_args = list(extra)
    (dx, dxb, dg), carried = _carry_call(
        body, carry, name=name, grid=(total, n_row) if resident else (n_row, total),
        in_specs=a_specs + [b_spec] + x_specs + [rows, rows, one], out_specs=[dx_spec, rows, one],
        out_shape=[dx_shape, jax.ShapeDtypeStruct((s, d), BF16), jax.ShapeDtypeStruct((1, d), F32)],
        scratch_shapes=[pltpu.VMEM((s if resident else tm, d), F32)], args=list(a_parts) + [b] + x_args + [dres, x, g])
    return (dx, dxb, dg), carried


def _grad_w_parts(a_parts, b, *, name, tm=512, carry=None):
    s, width = a_parts[0].shape
    per, n = width // tm, b.shape[1]

    def body(*refs):
        a_refs, b_ref, o_ref = refs[:len(a_parts)], refs[len(a_parts)], refs[len(a_parts) + 1]
        i = pl.program_id(0)
        for p, a_ref in enumerate(a_refs):
            @pl.when(i // per == p)
            def _(a_ref=a_ref):
                o_ref[...] = _dot(a_ref[...].astype(BF16), b_ref[...].astype(BF16), TN).astype(BF16)

    a_specs = [pl.BlockSpec((s, tm), lambda i, p=p: (0, jnp.clip(i - p * per, 0, per - 1))) for p in range(len(a_parts))]
    (out,), carried = _carry_call(
        body, carry, name=name, grid=(len(a_parts) * per,),
        in_specs=a_specs + [pl.BlockSpec((s, n), lambda i: (0, 0))], out_specs=[pl.BlockSpec((tm, n), lambda i: (i, 0))],
        out_shape=[jax.ShapeDtypeStruct((len(a_parts) * width, n), BF16)], scratch_shapes=[], args=list(a_parts) + [b])
    return out, carried


def _ple_loss(p, wt_ple, h3, w_pg, x2, target, g, *, name, tm=256):
    s, d = x2.shape
    kp = p.shape[1]

    def body(p_ref, wp_ref, h_ref, wg_ref, x_ref, t_ref, g_ref, loss_ref, dx_ref, dple_ref, dgp_ref, dg_ref):
        i = pl.program_id(0)
        ple = _dot(p_ref[...].astype(BF16), wp_ref[...], NT)
        sg = _sigmoid(_dot(h_ref[...], wg_ref[...], NN))
        xv = x_ref[...] + ple * sg
        r = lax.rsqrt(jnp.mean(xv * xv, axis=-1, keepdims=True) + EPS)
        xhat = xv * r
        diff = xhat * g_ref[...] - t_ref[...]
        lp = jnp.zeros((1, 128), F32) + (0.5 / d) * jnp.sum(diff * diff)
        dy = diff * (1.0 / d)
        dxhat = dy * g_ref[...]
        dx = r * (dxhat - xhat * jnp.mean(dxhat * xhat, axis=-1, keepdims=True))
        dx_ref[...] = dx
        dple_ref[...] = (dx * sg).astype(BF16)
        dgp_ref[...] = (dx * ple * (sg * (1.0 - sg))).astype(BF16)
        dgp = jnp.sum(dy * xhat, axis=0, keepdims=True)

        @pl.when(i == 0)
        def _():
            dg_ref[...] = dgp
            loss_ref[...] = lp

        @pl.when(i > 0)
        def _():
            dg_ref[...] += dgp
            loss_ref[...] += lp

    rows = _row_spec(tm, d)
    return pl.pallas_call(
        body, name=name, grid=(s // tm,),
        in_specs=[_row_spec(tm, kp), _full_spec((d, kp)), rows, _full_spec((d, d)), rows, rows, _full_spec((1, d))],
        out_specs=[_full_spec((1, 128)), rows, rows, rows, _full_spec((1, d))],
        out_shape=[jax.ShapeDtypeStruct((1, 128), F32), jax.ShapeDtypeStruct((s, d), F32),
                   jax.ShapeDtypeStruct((s, d), BF16), jax.ShapeDtypeStruct((s, d), BF16),
                   jax.ShapeDtypeStruct((1, d), F32)],
        compiler_params=_cparams(("arbitrary",)),
    )(p, wt_ple, h3, w_pg, x2, target, g)


def _branches_merge(a, b, w_a, w_b, zuvg, *, name, tm=512):
    s, d = a.shape

    def body(a_ref, b_ref, wa_ref, wb_ref, ga_ref, gb_ref, ya_ref, yb_ref, o_ref):
        ya = _dot(a_ref[...], wa_ref[...], NN)
        yb = _dot(b_ref[...], wb_ref[...], NN)
        ya_ref[...] = ya
        yb_ref[...] = yb
        o_ref[...] = (_sigmoid(ga_ref[...]) * ya + _sigmoid(gb_ref[...]) * yb).astype(BF16)

    rows = _row_spec(tm, d)
    return pl.pallas_call(
        body, name=name, grid=(s // tm,),
        in_specs=[rows, rows, _full_spec((d, d)), _full_spec((d, d)), _row_spec(tm, d, 2), _row_spec(tm, d, 3)],
        out_specs=[rows, rows, rows],
        out_shape=[jax.ShapeDtypeStruct((s, d), F32), jax.ShapeDtypeStruct((s, d), F32), jax.ShapeDtypeStruct((s, d), BF16)],
        compiler_params=_cparams(("parallel",)),
    )(a, b, w_a, w_b, zuvg, zuvg)


def _merge_bwd(dx1b, w_out, ya, yb, zuvg, *, name, tm=512):
    s, d = ya.shape

    def body(dx_ref, w_ref, ya_ref, yb_ref, ga_ref, gb_ref, dya_ref, dyb_ref, dga_ref, dgb_ref):
        dmv = _dot(dx_ref[...], w_ref[...], NT)
        sa = _sigmoid(ga_ref[...])
        sb = _sigmoid(gb_ref[...])
        dya_ref[...] = (dmv * sa).astype(BF16)
        dyb_ref[...] = (dmv * sb).astype(BF16)
        dga_ref[...] = (dmv * ya_ref[...] * (sa * (1.0 - sa))).astype(BF16)
        dgb_ref[...] = (dmv * yb_ref[...] * (sb * (1.0 - sb))).astype(BF16)

    rows = _row_spec(tm, d)
    o = jax.ShapeDtypeStruct((s, d), BF16)
    return pl.pallas_call(
        body, name=name, grid=(s // tm,),
        in_specs=[rows, _full_spec((d, d)), rows, rows, _row_spec(tm, d, 2), _row_spec(tm, d, 3)],
        out_specs=[rows] * 4, out_shape=[o, o, o, o], compiler_params=_cparams(("parallel",)),
    )(dx1b, w_out, ya, yb, zuvg, zuvg)


def _masked_ws(ws_ref, g):
    row = lax.broadcasted_iota(I32, (GBLOCK, GBLOCK), 0)
    col = lax.broadcasted_iota(I32, (GBLOCK, GBLOCK), 1)
    keep = (col // CHUNK) <= (row // CHUNK)
    return jnp.where(keep, ws_ref[g], 0.0), keep


def _layernorm_parts(zv):
    mu = jnp.mean(zv, axis=-1, keepdims=True)
    xc = zv - mu
    rs = lax.rsqrt(jnp.mean(xc * xc, axis=-1, keepdims=True) + EPS)
    return xc * rs, rs


def _gmlp_fwd(zuvg, ln_g, ln_b, w_s, bs_t, *, name):
    s, w = zuvg.shape[0], GROUPS * GDIM

    def body(zu_ref, zv_ref, lng_ref, lnb_ref, ws_ref, bs_ref, a_ref):
        zu = _gelu(zu_ref[...])
        zv = _gelu(zv_ref[...])
        xhat, _ = _layernorm_parts(zv)
        vln = (xhat * lng_ref[...] + lnb_ref[...]).astype(BF16)
        for g in range(GROUPS):
            wm, _ = _masked_ws(ws_ref, g)
            mixed = _dot(wm.astype(BF16), vln[:, g * GDIM:(g + 1) * GDIM], NN) + bs_ref[:, g:g + 1]
            a_ref[:, g * GDIM:(g + 1) * GDIM] = (zu[:, g * GDIM:(g + 1) * GDIM] * mixed).astype(BF16)

    return pl.pallas_call(
        body, name=name, grid=(s // GBLOCK,),
        in_specs=[_row_spec(GBLOCK, w, 0), _row_spec(GBLOCK, w, 1), _full_spec((1, w)), _full_spec((1, w)),
                  _full_spec((GROUPS, GBLOCK, GBLOCK)), _full_spec((GBLOCK, 128))],
        out_specs=_row_spec(GBLOCK, w),
        out_shape=jax.ShapeDtypeStruct((s, w), BF16), compiler_params=_cparams(("parallel",)),
    )(zuvg, zuvg, ln_g, ln_b, w_s, bs_t)


def _gmlp_bwd(da, zuvg, ln_g, ln_b, w_s, bs_t, carry=None, *, name):
    s, w = zuvg.shape[0], GROUPS * GDIM

    def body(da_ref, zu_ref, zv_ref, lng_ref, lnb_ref, ws_ref, bs_ref,
             dzu_ref, dzv_ref, dws_ref, dbs_ref, dlng_ref, dlnb_ref, dvln_ref):
        i = pl.program_id(0)
        zu, dzu_g = _gelu_and_grad(zu_ref[...])
        zv, dzv_g = _gelu_and_grad(zv_ref[...])
        xhat, rs = _layernorm_parts(zv)
        vln = (xhat * lng_ref[...] + lnb_ref[...]).astype(BF16)
        dav = da_ref[...].astype(F32)
        lane = lax.broadcasted_iota(I32, (GBLOCK, 128), 1)
        dbs = jnp.zeros((GBLOCK, 128), F32)

        @pl.when(i == 0)
        def _():
            dws_ref[...] = jnp.zeros_like(dws_ref)

        for g in range(GROUPS):
            sl = slice(g * GDIM, (g + 1) * GDIM)
            wm, keep = _masked_ws(ws_ref, g)
            wmb = wm.astype(BF16)
            vg = vln[:, sl]
            mixed = _dot(wmb, vg, NN) + bs_ref[:, g:g + 1]
            dag = dav[:, sl]
            dzu_ref[:, sl] = (dag * mixed * dzu_g[:, sl]).astype(BF16)
            dmix = dag * zu[:, sl]
            dmb = dmix.astype(BF16)
            dws_ref[g] += jnp.where(keep, _dot(dmb, vg, NT), 0.0)
            dbs = jnp.where(lane == g, jnp.sum(dmix, axis=1, keepdims=True), dbs)
            dvln_ref[:, sl] = _dot(wmb, dmb, TN)
        dvln = dvln_ref[...]
        dxhat = dvln * lng_ref[...]
        dzv = rs * (dxhat - jnp.mean(dxhat, axis=-1, keepdims=True)
                    - xhat * jnp.mean(dxhat * xhat, axis=-1, keepdims=True))
        dzv_ref[...] = (dzv * dzv_g).astype(BF16)
        dlng = jnp.sum(dvln * xhat, axis=0, keepdims=True)
        dlnb = jnp.sum(dvln, axis=0, keepdims=True)

        @pl.when(i == 0)
        def _():
            dbs_ref[...] = dbs
            dlng_ref[...] = dlng
            dlnb_ref[...] = dlnb

        @pl.when(i > 0)
        def _():
            dbs_ref[...] += dbs
            dlng_ref[...] += dlng
            dlnb_ref[...] += dlnb

    return _carry_call(
        body, carry, name=name, grid=(s // GBLOCK,),
        in_specs=[_row_spec(GBLOCK, w), _row_spec(GBLOCK, w, 0), _row_spec(GBLOCK, w, 1), _full_spec((1, w)),
                  _full_spec((1, w)), _full_spec((GROUPS, GBLOCK, GBLOCK)), _full_spec((GBLOCK, 128))],
        out_specs=[_row_spec(GBLOCK, w), _row_spec(GBLOCK, w), _full_spec((GROUPS, GBLOCK, GBLOCK)),
                   _full_spec((GBLOCK, 128)), _full_spec((1, w)), _full_spec((1, w))],
        out_shape=[jax.ShapeDtypeStruct((s, w), BF16), jax.ShapeDtypeStruct((s, w), BF16),
                   jax.ShapeDtypeStruct((GROUPS, GBLOCK, GBLOCK), F32), jax.ShapeDtypeStruct((GBLOCK, 128), F32),
                   jax.ShapeDtypeStruct((1, w), F32), jax.ShapeDtypeStruct((1, w), F32)],
        scratch_shapes=[pltpu.VMEM((GBLOCK, w), F32)], args=[da, zuvg, zuvg, ln_g, ln_b, w_s, bs_t])


def _shift_down(u, k):
    row = lax.broadcasted_iota(I32, u.shape, 0)
    return jnp.where(row >= k, pltpu.roll(u, k, 0), 0.0)


def _shift_up(u, k):
    s = u.shape[0]
    row = lax.broadcasted_iota(I32, u.shape, 0)
    return jnp.where(row < s - k, pltpu.roll(u, s - k, 0), 0.0)


def _conv(u, w_ref, b_ref):
    return b_ref[...] + w_ref[0:1, :] * _shift_down(u, 2) + w_ref[1:2, :] * _shift_down(u, 1) + w_ref[2:3, :] * u


def _conv_specs(s, f, tc):
    nc = f // tc
    half = lambda rows: [pl.BlockSpec((rows, tc), lambda j: (0, j)), pl.BlockSpec((rows, tc), lambda j: (0, nc + j))]
    return half(s), half(3), half(1)


def _up_convglu(h2, wt_up, conv_w, conv_b, *, name, tc=256):
    s, d = h2.shape
    f = wt_up.shape[0] // 2
    nc = f // tc
    _, w_specs, b_specs = _conv_specs(s, f, tc)

    def body(h_ref, ta_ref, tg_ref, wa_ref, wg_ref, ba_ref, bg_ref, ua_ref, ug_ref, o_ref):
        ua = _dot(h_ref[...], ta_ref[...], NT)
        ua_ref[...] = ua
        ga = _gelu(_conv(ua, wa_ref, ba_ref))
        ug = _dot(h_ref[...], tg_ref[...], NT)
        ug_ref[...] = ug
        o_ref[...] = (ga * _conv(ug, wg_ref, bg_ref)).astype(BF16)

    col = pl.BlockSpec((s, tc), lambda j: (0, j))
    return pl.pallas_call(
        body, name=name, grid=(nc,),
        in_specs=[_full_spec((s, d)), pl.BlockSpec((tc, d), lambda j: (j, 0)), pl.BlockSpec((tc, d), lambda j: (nc + j, 0))]
        + w_specs + b_specs,
        out_specs=[col, col, col],
        out_shape=[jax.ShapeDtypeStruct((s, f), F32), jax.ShapeDtypeStruct((s, f), F32), jax.ShapeDtypeStruct((s, f), BF16)],
        compiler_params=_cparams(("parallel",)),
    )(h2, wt_up, wt_up, conv_w, conv_w, conv_b, conv_b)


def _convglu_bwd(dact, up_a, up_g, conv_w, conv_b, *, name, tc=256):
    s, f = up_a.shape
    _, w_specs, b_specs = _conv_specs(s, f, tc)
    up_specs = [pl.BlockSpec((s, tc), lambda j: (0, j))] * 2

    def half(dc, taps, w_ref, du_ref, dw_ref, db_ref):
        db_ref[...] = jnp.sum(dc, axis=0, keepdims=True)
        for k in range(3):
            dw_ref[k:k + 1, :] = jnp.sum(dc * taps[k], axis=0, keepdims=True)
        du = w_ref[2:3, :] * dc + w_ref[1:2, :] * _shift_up(dc, 1) + w_ref[0:1, :] * _shift_up(dc, 2)
        du_ref[...] = du.astype(BF16)

    def body(d_ref, ua_ref, ug_ref, wa_ref, wg_ref, ba_ref, bg_ref,
             dua_ref, dug_ref, dwa_ref, dwg_ref, dba_ref, dbg_ref):
        taps_a = (_shift_down(ua_ref[...], 2), _shift_down(ua_ref[...], 1), ua_ref[...])
        taps_g = (_shift_down(ug_ref[...], 2), _shift_down(ug_ref[...], 1), ug_ref[...])
        conv = lambda taps, w_ref, b_ref: b_ref[...] + w_ref[0:1, :] * taps[0] + w_ref[1:2, :] * taps[1] + w_ref[2:3, :] * taps[2]
        ca = conv(taps_a, wa_ref, ba_ref)
        cg = conv(taps_g, wg_ref, bg_ref)
        ga, dga = _gelu_and_grad(ca)
        dv = d_ref[...].astype(F32)
        half(dv * cg * dga, taps_a, wa_ref, dua_ref, dwa_ref, dba_ref)
        half(dv * ga, taps_g, wg_ref, dug_ref, dwg_ref, dbg_ref)

    col, w3, b1 = up_specs[0], w_specs[0], b_specs[0]
    return pl.pallas_call(
        body, name=name, grid=(f // tc,),
        in_specs=[col] + up_specs + w_specs + b_specs, out_specs=[col, col, w3, w3, b1, b1],
        out_shape=[jax.ShapeDtypeStruct((s, f), BF16), jax.ShapeDtypeStruct((s, f), BF16),
                   jax.ShapeDtypeStruct((3, f), F32), jax.ShapeDtypeStruct((3, f), F32),
                   jax.ShapeDtypeStruct((1, f), F32), jax.ShapeDtypeStruct((1, f), F32)],
        compiler_params=_cparams(("parallel",)),
    )(dact, up_a, up_g, conv_w, conv_w, conv_b, conv_b)


def _tri_dot(tri, x):
    b0 = x.astype(BF16)
    r1 = x - b0.astype(F32)
    b1 = r1.astype(BF16)
    b2 = (r1 - b1.astype(F32)).astype(BF16)
    return _dot(tri, b0, NN) + _dot(tri, b1, NN) + _dot(tri, b2, NN)


def _log_sigmoid(x):
    return jnp.minimum(x, 0.0) - jnp.log(1.0 + jnp.exp(-jnp.abs(x)))


def _expand_heads(col16, rows):
    src = lax.broadcasted_iota(I32, (128, HEADS * HEAD_DIM), 0)
    dst = lax.broadcasted_iota(I32, (128, HEADS * HEAD_DIM), 1) // HEAD_DIM
    spread = (src == dst).astype(BF16)
    p0, p1, p2 = _bf16_pieces(col16)
    return (_dot(p0.astype(BF16), spread, NN) + _dot(p1.astype(BF16), spread, NN)) + _dot(p2.astype(BF16), spread, NN)


def _forget_cumsum(f_logit, b_f, *, name):
    s = f_logit.shape[0]
    nb = s // 128

    def body(f_ref, b_ref, cqe_ref):
        row = lax.broadcasted_iota(I32, (128, 128), 0)
        col = lax.broadcasted_iota(I32, (128, 128), 1)
        tri = (col <= row).astype(BF16)

        def step(n, carry):
            r0 = pl.multiple_of(n * 128, 128)
            lf = _log_sigmoid(f_ref[pl.ds(r0, 128), :] + b_ref[...])
            cum = _tri_dot(tri, lf) + carry
            cqe_ref[pl.ds(r0, 128), :] = _expand_heads(cum, 128)
            return cum[127:128, :]

        lax.fori_loop(0, nb, step, jnp.zeros((1, 128), F32))

    return pl.pallas_call(
        body, name=name, grid=(1,),
        in_specs=[_full_spec((s, 128)), _full_spec((1, 128))],
        out_specs=_full_spec((s, HEADS * HEAD_DIM)),
        out_shape=jax.ShapeDtypeStruct((s, HEADS * HEAD_DIM), F32),
        compiler_params=_cparams(("arbitrary",)),
    )(f_logit, b_f)


def _forget_bwd(dcq16, sum_q16, f_logit, b_f, *, name):
    s = f_logit.shape[0]
    nb = s // 128

    def body(a_ref, k_ref, f_ref, b_ref, df_ref, db_ref):
        row = lax.broadcasted_iota(I32, (128, 128), 0)
        col = lax.broadcasted_iota(I32, (128, 128), 1)
        tri_rev = (col >= row).astype(BF16)

        def step(m, carry):
            suffix, dbsum = carry
            n = nb - 1 - m
            r0 = pl.multiple_of(n * 128, 128)
            dcum = a_ref[pl.ds(r0, 128), :] - k_ref[pl.ds(r0, 128), :]
            dlf = _tri_dot(tri_rev, dcum) + suffix
            df = dlf * _sigmoid(-(f_ref[pl.ds(r0, 128), :] + b_ref[...]))
            df_ref[pl.ds(r0, 128), :] = df.astype(BF16)
            return dlf[0:1, :], dbsum + jnp.sum(df, axis=0, keepdims=True)

        _, dbsum = lax.fori_loop(0, nb, step, (jnp.zeros((1, 128), F32), jnp.zeros((1, 128), F32)))
        db_ref[...] = dbsum

    return pl.pallas_call(
        body, name=name, grid=(1,),
        in_specs=[_full_spec((s, 128))] * 3 + [_full_spec((1, 128))],
        out_specs=[_full_spec((s, 128)), _full_spec((1, 128))],
        out_shape=[jax.ShapeDtypeStruct((s, 128), BF16), jax.ShapeDtypeStruct((1, 128), F32)],
        compiler_params=_cparams(("arbitrary",)),
    )(dcq16, sum_q16, f_logit, b_f)


ATT_T = 256


def _head_lanes(rows):
    return lax.broadcasted_iota(I32, (rows, 128), 1) < HEAD_DIM


def _bf16_pieces(c):
    p0 = c.astype(BF16).astype(F32)
    r = c - p0
    p1 = r.astype(BF16).astype(F32)
    p2 = (r - p1).astype(BF16).astype(F32)
    return p0, p1, p2


def _col_reduce(x, op):
    rows = x.shape[0]
    while rows > 8:
        rows //= 2
        x = op(x[:rows], x[rows:])
    return jnp.max(x, axis=0, keepdims=True) if op is jnp.maximum else jnp.sum(x, axis=0, keepdims=True)


def _attn_prep(qkv, cqe, carry=None, *, name):
    s = qkv.shape[0]
    npair = HEADS // 2

    def body(q_ref, k_ref, v_ref, c_ref, qa_ref, ka_ref, vt_ref):
        rows = 128
        lane = lax.broadcasted_iota(I32, (rows, 128), 1)

        def chunk(n, _):
            r0 = pl.multiple_of(n * rows, rows)
            sl = pl.ds(r0, rows)
            qv = q_ref[sl, :].astype(F32) * ATT_SCALE
            kv = k_ref[sl, :].astype(F32)
            p0, p1, p2 = _bf16_pieces(pltpu.roll(c_ref[sl, :], HEAD_DIM, 1))
            for e in range(2):
                mine = (lane < HEAD_DIM) if e == 0 else (lane >= HEAD_DIM)
                base = HEAD_DIM * (1 - e)
                ones_hi = jnp.where((lane >= base + 3) & (lane < base + 6), 1.0, 0.0)
                ones_lo = jnp.where((lane >= base) & (lane < base + 3), 1.0, 0.0)
                qa = jnp.where(mine, qv, jnp.where(lane == base, p0, jnp.where(lane == base + 1, p1,
                               jnp.where(lane == base + 2, p2, ones_hi))))
                ka = jnp.where(mine, kv, jnp.where(lane == base + 3, -p0, jnp.where(lane == base + 4, -p1,
                               jnp.where(lane == base + 5, -p2, ones_lo))))
                qa_ref[e, sl, :] = qa.astype(BF16)
                ka_ref[e, sl, :] = ka.astype(BF16)
            vt_ref[0, :, sl] = v_ref[sl, :].astype(F32).T.astype(BF16)
            return 0

        lax.fori_loop(0, s // rows, chunk, 0)

    pair = pl.BlockSpec((2, s, 128), lambda hp: (hp, 0, 0))
    return _carry_call(
        body, carry, name=name, grid=(npair,),
        in_specs=[pl.BlockSpec((s, 128), lambda hp: (0, hp)), pl.BlockSpec((s, 128), lambda hp: (0, npair + hp)),
                  pl.BlockSpec((s, 128), lambda hp: (0, 2 * npair + hp)), pl.BlockSpec((s, 128), lambda hp: (0, hp))],
        out_specs=[pair, pair, pl.BlockSpec((1, 128, s), lambda hp: (hp, 0, 0))],
        out_shape=[jax.ShapeDtypeStruct((HEADS, s, 128), BF16), jax.ShapeDtypeStruct((HEADS, s, 128), BF16),
                   jax.ShapeDtypeStruct((npair, 128, s), BF16)],
        scratch_shapes=[], args=[qkv, qkv, qkv, cqe])


def _attn_fwd(qa, ka, vt, carry=None, *, name):
    s = qa.shape[1]
    t = 2 * ATT_T
    nq = s // t
    npair = HEADS // 2

    def body(qa_ref, ka_ref, vt_ref, o_ref, lse_ref):
        i = pl.program_id(1)
        krow = lax.broadcasted_iota(I32, (t, t), 0)
        qcol = lax.broadcasted_iota(I32, (t, t), 1)
        sub = lax.broadcasted_iota(I32, (128, t), 0)
        row8 = lax.broadcasted_iota(I32, (8, t), 0)
        qbs = (qa_ref[0], qa_ref[1])
        tk = t

        def step(j, carry, diag):
            c0 = pl.multiple_of(j * tk, tk)
            vtb = vt_ref[0, :, pl.ds(c0, tk)]
            sts = [_dot(ka_ref[e, pl.ds(c0, tk), :], qbs[e], NT) for e in range(2)]
            if diag:
                sts = [jnp.where(krow <= qcol, st, NEG) for st in sts]
            pts, stats = [], []
            for e in range(2):
                m, l, _ = carry[e]
                m_new = jnp.maximum(m, _col_reduce(sts[e], jnp.maximum))
                alpha = jnp.exp(m - m_new)
                pt = jnp.exp(sts[e] - m_new)
                stats.append((m_new, alpha, alpha * l + _col_reduce(pt, jnp.add)))
                pts.append(pt.astype(BF16))
            pvs = [_dot(vtb, pts[e], NN) for e in range(2)]
            return tuple((stats[e][0], stats[e][2], stats[e][1] * carry[e][2] + pvs[e]) for e in range(2))

        init = (jnp.full((1, t), NEG, F32), jnp.zeros((1, t), F32), jnp.zeros((128, t), F32))
        carry = lax.fori_loop(0, i, functools.partial(step, diag=False), (init, init))
        (m0, l0, acc0), (m1, l1, acc1) = step(i, carry, True)
        o_pair = jnp.where(sub < HEAD_DIM, acc0 / l0, acc1 / l1)
        o_ref[...] = o_pair.T.astype(BF16)
        lse_ref[0] = jnp.where(row8 == 0, m0 + jnp.log(l0), jnp.where(row8 == 1, m1 + jnp.log(l1), 0.0))

    return _carry_call(
        body, carry, name=name, grid=(npair, nq),
        in_specs=[pl.BlockSpec((2, t, 128), lambda hp, i: (hp, i, 0)), pl.BlockSpec((2, s, 128), lambda hp, i: (hp, 0, 0)),
                  pl.BlockSpec((1, 128, s), lambda hp, i: (hp, 0, 0))],
        out_specs=[pl.BlockSpec((t, 128), lambda hp, i: (i, hp)), pl.BlockSpec((1, 8, t), lambda hp, i: (hp, 0, i))],
        out_shape=[jax.ShapeDtypeStruct((s, HEADS * HEAD_DIM), BF16), jax.ShapeDtypeStruct((npair, 8, s), F32)],
        scratch_shapes=[], args=[qa, ka, vt])


def _attn_delta(do, o, carry=None, *, name):
    s = do.shape[0]

    def body(do_ref, o_ref, d_ref):
        prod = do_ref[...].astype(F32) * o_ref[...].astype(F32)
        row = lax.broadcasted_iota(I32, (8, 128), 0)
        lane = lax.broadcasted_iota(I32, (8, 128), 1)
        sel = ((row == 0) & (lane < HEAD_DIM) | (row == 1) & (lane >= HEAD_DIM)).astype(BF16)
        p0, p1, p2 = _bf16_pieces(prod)
        d_ref[0] = (_dot(sel, p0.astype(BF16), NT) + _dot(sel, p1.astype(BF16), NT)) + _dot(sel, p2.astype(BF16), NT)

    pair = pl.BlockSpec((s, 128), lambda hp: (0, hp))
    (delta3,), carried = _carry_call(
        body, carry, name=name, grid=(HEADS // 2,), in_specs=[pair, pair],
        out_specs=[pl.BlockSpec((1, 8, s), lambda hp: (hp, 0, 0))],
        out_shape=[jax.ShapeDtypeStruct((HEADS // 2, 8, s), F32)], scratch_shapes=[], args=[do, o])
    return delta3, carried


def _attn_bwd(qa, ka, qkv, do, lse3, delta3, carry=None, *, name):
    s = qa.shape[1]
    t = 2 * ATT_T
    nb = s // t
    npair = HEADS // 2

    def body(qa_ref, ka_ref, v_ref, do_ref, lse_ref, delta_ref, dq_ref, dk_ref, dv_ref, aux_ref, dcq_ref, dqt):
        hp = pl.program_id(0)
        first = _head_lanes(t)
        lane = lax.broadcasted_iota(I32, (t, 128), 1)
        dqt[...] = jnp.zeros_like(dqt)

        @pl.when(hp == 0)
        def _():
            aux_ref[...] = jnp.zeros_like(aux_ref)

        krow = lax.broadcasted_iota(I32, (t, t), 0)
        qcol = lax.broadcasted_iota(I32, (t, t), 1)

        def key_block(j, _):
            c0 = pl.multiple_of(j * t, t)
            vb = v_ref[pl.ds(c0, t), :]
            kbs = (ka_ref[0, pl.ds(c0, t), :], ka_ref[1, pl.ds(c0, t), :])
            kbts = tuple(kb.astype(F32).T.astype(BF16) for kb in kbs)
            vhs = (jnp.where(first, vb, jnp.zeros_like(vb)), jnp.where(first, jnp.zeros_like(vb), vb))

            def query_block(i, carry, diag):
                r0 = pl.multiple_of(i * t, t)
                dob = do_ref[pl.ds(r0, t), :]
                sts = [_dot(kbs[e], qa_ref[e, pl.ds(r0, t), :], NT) for e in range(2)]
                dpts = [_dot(vhs[e], dob, NT) for e in range(2)]
                ptbs, dsbs = [], []
                for e in range(2):
                    st = jnp.where(krow <= qcol, sts[e], NEG) if diag else sts[e]
                    pt = jnp.exp(st - lse_ref[0, e:e + 1, pl.ds(r0, t)])
                    dsbs.append((pt * (dpts[e] - delta_ref[0, e:e + 1, pl.ds(r0, t)])).astype(BF16))
                    ptbs.append(pt.astype(BF16))
                out = []
                for e in range(2):
                    dk_a, dv_a = carry[e]
                    dv_a = dv_a + _dot(ptbs[e], dob, NN)
                    dk_a = dk_a + _dot(dsbs[e], qa_ref[e, pl.ds(r0, t), :], NN)
                    dqt[e, :, pl.ds(r0, t)] += _dot(kbts[e], dsbs[e], NN)
                    out.append((dk_a, dv_a))
                return tuple(out)

            zero = jnp.zeros((t, 128), F32)
            carry = query_block(j, ((zero, zero), (zero, zero)), True)
            (dk0, dv0), (dk1, dv1) = lax.fori_loop(j + 1, nb, functools.partial(query_block, diag=False), carry)
            dk_ref[pl.ds(c0, t), :] = jnp.where(first, dk0, dk1).astype(BF16)
            dv_ref[pl.ds(c0, t), :] = jnp.where(first, dv0, dv1).astype(BF16)
            sum_q = jnp.where(lane == 2 * hp, dk0[:, HEAD_DIM + 3:HEAD_DIM + 4],
                              jnp.where(lane == 2 * hp + 1, dk1[:, 3:4], aux_ref[pl.ds(c0, t), :]))
            aux_ref[pl.ds(c0, t), :] = sum_q
            return 0

        lax.fori_loop(0, nb, key_block, 0)
        sub = lax.broadcasted_iota(I32, (128, s), 0)
        row8 = lax.broadcasted_iota(I32, (8, s), 0)
        dq_ref[...] = (jnp.where(sub < HEAD_DIM, dqt[0], dqt[1]) * ATT_SCALE).T.astype(BF16)
        dcq_ref[0] = jnp.where(row8 == 0, dqt[0, HEAD_DIM:HEAD_DIM + 1, :], jnp.where(row8 == 1, dqt[1, 0:1, :], 0.0))

    def pair_cols(off):
        return pl.BlockSpec((s, 128), lambda hp: (0, off + hp))

    heads = pl.BlockSpec((2, s, 128), lambda hp: (hp, 0, 0))
    rows = pl.BlockSpec((1, 8, s), lambda hp: (hp, 0, 0))
    wide = jax.ShapeDtypeStruct((s, HEADS * HEAD_DIM), BF16)
    return _carry_call(
        body, carry, name=name, grid=(npair,),
        in_specs=[heads, heads, pair_cols(2 * npair), pair_cols(0), rows, rows],
        out_specs=[pair_cols(0), pair_cols(0), pair_cols(0), pl.BlockSpec((s, 128), lambda hp: (0, 0)), rows],
        out_shape=[wide, wide, wide, jax.ShapeDtypeStruct((s, 128), F32), jax.ShapeDtypeStruct((npair, 8, s), F32)],
        scratch_shapes=[pltpu.VMEM((2, 128, s), F32)], args=[qa, ka, qkv, do, lse3, delta3])


def _adam_math(w, g, m, v):
    m = ADAM_B1 * m + (1.0 - ADAM_B1) * g
    v = ADAM_B2 * v + (1.0 - ADAM_B2) * (g * g)
    m_hat = m / (1.0 - ADAM_B1 ** ADAM_STEP)
    v_hat = v / (1.0 - ADAM_B2 ** ADAM_STEP)
    delta = -ADAM_LR * (m_hat / (jnp.sqrt(v_hat) + ADAM_EPS) + ADAM_WD * w)
    return delta, m, v


def _sum_pairs(keep, recv, pos, *, name):
    _, r, c = recv.shape
    tr = _row_tile(r, 1024)

    def body(pos_ref, a_ref, b_ref, o32_ref, o16_ref):
        tot = a_ref[...].astype(F32) + b_ref[...].astype(F32)
        o16_ref[...] = tot.astype(BF16)

        @pl.when(pl.program_id(1) == 2 * pos_ref[0] + pos_ref[1])
        def _():
            o32_ref[...] = tot

    out = pl.BlockSpec((1, tr, c), lambda i, q, pos: (q, i, 0))
    grid_spec = pltpu.PrefetchScalarGridSpec(
        num_scalar_prefetch=1, grid=(r // tr, 4),
        in_specs=[pl.BlockSpec((1, tr, c), lambda i, q, pos: (2 * q + pos[2], i, 0)), out],
        out_specs=[pl.BlockSpec((1, tr, c), lambda i, q, pos: (0, i, 0)), out])
    return pl.pallas_call(
        body, name=name, grid_spec=grid_spec,
        out_shape=[jax.ShapeDtypeStruct((1, r, c), F32), jax.ShapeDtypeStruct((4, r, c), BF16)],
        compiler_params=_cparams(("arbitrary", "arbitrary")),
    )(pos, keep, recv)


def _adam_sharded(psum, recv, w, m, v, pos, *, name):
    r, c = w.shape
    rg = psum.shape[1]

    def body(pos_ref, p_ref, r_ref, w_ref, m_ref, v_ref, g_ref, d_ref, mo_ref, vo_ref):
        part = lambda ref, q: ref[q] if rg == r else ref[q, :r, :]
        g = part(p_ref, 0) + part(r_ref, 0).astype(F32) + part(r_ref, 1).astype(F32) + part(r_ref, 2).astype(F32)
        delta, mn, vn = _adam_math(w_ref[...], g, m_ref[...], v_ref[...])
        g_ref[...] = g
        d_ref[...] = delta
        mo_ref[...] = mn
        vo_ref[...] = vn

    if rg == r:
        tr = _row_tile(r, 320)
        grid = (r // tr,)
        row = pl.BlockSpec((tr, c), lambda i, pos: (i, 0))
        sums = lambda n: pl.BlockSpec((n, tr, c), lambda i, pos: (0, i, 0))
    else:
        tc = 256
        grid = (c // tc,)
        row = pl.BlockSpec((r, tc), lambda i, pos: (0, i))
        sums = lambda n: pl.BlockSpec((n, rg, tc), lambda i, pos: (0, 0, i))
    grid_spec = pltpu.PrefetchScalarGridSpec(
        num_scalar_prefetch=1, grid=grid, in_specs=[sums(1), sums(3), row, row, row], out_specs=[row, row, row, row])
    o = jax.ShapeDtypeStruct((r, c), F32)
    return pl.pallas_call(
        body, name=name, grid_spec=grid_spec, out_shape=[o, o, o, o],
        compiler_params=_cparams(("parallel",)),
    )(pos, psum, recv, w, m, v)


def _adam_replicated(chip_sums, last, w, m, v, *, name):
    r = w.shape[0]

    def body(s_ref, l_ref, w_ref, m_ref, v_ref, g_ref, d_ref, mo_ref, vo_ref):
        g = (((s_ref[0] + s_ref[1]) + s_ref[2]) + s_ref[3]) + l_ref[...]
        delta, mn, vn = _adam_math(w_ref[...], g, m_ref[...], v_ref[...])
        g_ref[...] = g
        d_ref[...] = delta
        mo_ref[...] = mn
        vo_ref[...] = vn

    o = jax.ShapeDtypeStruct((r, 1024), F32)
    full = _full_spec((r, 1024))
    return pl.pallas_call(
        body, name=name, grid=(1,),
        in_specs=[_full_spec((4, r, 1024)), full, full, full, full], out_specs=[full] * 4, out_shape=[o] * 4,
        compiler_params=_cparams(("arbitrary",)),
    )(chip_sums, last, w, m, v)


ASM_OUT = 256
ASM_SRC = 304


def _w_in_row(r):
    return r if r < 2048 else (r + O_G - 2048 if r < 4096 else r - 2048)


def _assemble_wt_main(g, *, name):
    win_a = ASM_OUT + 16
    table = []
    for blk in range(MAIN_COLS // ASM_OUT):
        j, l0 = divmod(_w_in_row(blk * ASM_OUT), IN_SHARD)
        start = min(l0 // 16 * 16, IN_SHARD_PAD - win_a)
        n_a = min(ASM_OUT, IN_SHARD - l0)
        table.append((j, start, l0 - start, n_a, int(n_a < ASM_OUT)))
    steps = len(table)

    def body(tab_ref, g_ref, o_ref, buf_a, buf_b, sems):
        blk = pl.program_id(0)
        slot = blk % 2

        def copy_a(step, sl):
            first = pl.multiple_of(tab_ref[step, 1], 16)
            return pltpu.make_async_copy(g_ref.at[tab_ref[step, 0], pl.ds(first, win_a)], buf_a.at[sl], sems.at[0, sl])

        def copy_b(step, sl):
            return pltpu.make_async_copy(g_ref.at[tab_ref[step, 0] + 1, pl.ds(0, ASM_OUT)], buf_b.at[sl], sems.at[1, sl])

        def fetch(step, sl):
            copy_a(step, sl).start()

            @pl.when(tab_ref[step, 4] == 1)
            def _():
                copy_b(step, sl).start()

        @pl.when(blk == 0)
        def _():
            fetch(0, 0)

        @pl.when(blk + 1 < steps)
        def _():
            fetch(blk + 1, 1 - slot)

        off, n_a = tab_ref[blk, 2], tab_ref[blk, 3]
        copy_a(blk, slot).wait()
        r = lax.broadcasted_iota(I32, (ASM_OUT, win_a), 0)
        k = lax.broadcasted_iota(I32, (ASM_OUT, win_a), 1)
        sel_a = ((k == r + off) & (r < n_a)).astype(BF16)
        o_ref[...] = _dot(sel_a, buf_a[slot], NN).astype(BF16)

        @pl.when(tab_ref[blk, 4] == 1)
        def _():
            copy_b(blk, slot).wait()
            rb = lax.broadcasted_iota(I32, (ASM_OUT, ASM_OUT), 0)
            kb = lax.broadcasted_iota(I32, (ASM_OUT, ASM_OUT), 1)
            sel_b = ((kb == rb - n_a) & (rb >= n_a)).astype(BF16)
            o_ref[...] += _dot(sel_b, buf_b[slot], NN).astype(BF16)

    grid_spec = pltpu.PrefetchScalarGridSpec(
        num_scalar_prefetch=1, grid=(steps,), in_specs=[ANY],
        out_specs=pl.BlockSpec((ASM_OUT, D_MODEL), lambda blk, tab: (blk, 0)),
        scratch_shapes=[pltpu.VMEM((2, win_a, D_MODEL), BF16), pltpu.VMEM((2, ASM_OUT, D_MODEL), BF16),
                        pltpu.SemaphoreType.DMA((2, 2))])
    return pl.pallas_call(
        body, name=name, grid_spec=grid_spec, out_shape=jax.ShapeDtypeStruct((MAIN_COLS, D_MODEL), BF16),
        compiler_params=_cparams(("arbitrary",)),
    )(jnp.asarray(table, I32), g)


def _pair_sum_small(mine, theirs, *, name):
    def body(a_ref, b_ref, o_ref):
        o_ref[...] = a_ref[...] + b_ref[...]

    full = _full_spec(mine.shape)
    return pl.pallas_call(
        body, name=name, grid=(1,), in_specs=[full, full], out_specs=full,
        out_shape=jax.ShapeDtypeStruct(mine.shape, F32), compiler_params=_cparams(("arbitrary",)),
    )(mine, theirs)


ANY = pl.BlockSpec(memory_space=pl.ANY)
OTHER_CHIPS = ((1, 0), (0, 1), (1, 1))


class _Carry:
    def __init__(self, inputs, out_shapes, scratch, start, wait, aliases=None, middle=None):
        self.inputs, self.out_shapes, self.scratch = list(inputs), list(out_shapes), list(scratch)
        self.start, self.wait, self.aliases, self.middle = start, wait, dict(aliases or {}), middle


def _carry_join(*carries):
    n_in = [len(c.inputs) for c in carries]
    n_out = [len(c.out_shapes) for c in carries]
    n_scr = [len(c.scratch) for c in carries]

    def split(refs, counts):
        out, k = [], 0
        for n in counts:
            out.append(refs[k:k + n])
            k += n
        return out

    def start(ins, outs, scr):
        for c, i, o, s in zip(carries, split(ins, n_in), split(outs, n_out), split(scr, n_scr)):
            c.start(i, o, s)

    def wait(ins, outs, scr):
        for c, i, o, s in zip(carries, split(ins, n_in), split(outs, n_out), split(scr, n_scr)):
            c.wait(i, o, s)

    def middle(ins, outs, scr):
        for c, i, o, s in zip(carries, split(ins, n_in), split(outs, n_out), split(scr, n_scr)):
            if c.middle is not None:
                c.middle(i, o, s)

    aliases = {}
    for k, c in enumerate(carries):
        aliases.update({sum(n_in[:k]) + i: sum(n_out[:k]) + o for i, o in c.aliases.items()})
    joined = _Carry(sum((c.inputs for c in carries), []), sum((c.out_shapes for c in carries), []),
                    sum((c.scratch for c in carries), []), start, wait, aliases,
                    middle if any(c.middle is not None for c in carries) else None)
    joined.counts = n_out
    joined.split = lambda results: split(results, n_out)
    return joined


def _carried(body, carry, n_in, n_out, grid):
    if carry is None:
        return body
    ci, co, cs = len(carry.inputs), len(carry.out_shapes), len(carry.scratch)

    def wrapped(*refs):
        ins, cins = refs[:n_in], refs[n_in:n_in + ci]
        outs, couts = refs[n_in + ci:n_in + ci + n_out], refs[n_in + ci + n_out:n_in + ci + n_out + co]
        rest = refs[n_in + ci + n_out + co:]
        scratch, cscr = rest[:len(rest) - cs], rest[len(rest) - cs:]
        first, last, step, steps = None, None, 0, 1
        for axis, size in enumerate(grid):
            f, l = pl.program_id(axis) == 0, pl.program_id(axis) == size - 1
            first = f if first is None else first & f
            last = l if last is None else last & l
            step, steps = step * size + pl.program_id(axis), steps * size

        @pl.when(first)
        def _():
            carry.start(cins, couts, cscr)

        if carry.middle is not None:
            @pl.when(step == steps // 2)
            def _():
                carry.middle(cins, couts, cscr)

        body(*ins, *outs, *scratch)

        @pl.when(last)
        def _():
            carry.wait(cins, couts, cscr)

    return wrapped


def _carry_call(body, carry, *, name, grid, in_specs, out_specs, out_shape, scratch_shapes, args, vmem=True,
                own_aliases=None):
    n_in, n_out = len(in_specs), len(out_specs)
    extra_in = [ANY] * len(carry.inputs) if carry else []
    extra_out = [ANY] * len(carry.out_shapes) if carry else []
    aliases = dict(own_aliases or {})
    if carry:
        aliases.update({n_in + i: n_out + o for i, o in carry.aliases.items()})
    out = pl.pallas_call(
        _carried(body, carry, n_in, n_out, grid), name=name, grid=grid,
        in_specs=list(in_specs) + extra_in, out_specs=list(out_specs) + extra_out,
        out_shape=list(out_shape) + (carry.out_shapes if carry else []),
        scratch_shapes=list(scratch_shapes) + (carry.scratch if carry else []),
        input_output_aliases=aliases,
        compiler_params=_cparams(("arbitrary",) * len(grid)) if vmem else None,
    )(*args, *(carry.inputs if carry else []))
    return list(out[:n_out]), list(out[n_out:])


def _run_carry(carry, *, name):
    return _carry_call(lambda: None, carry, name=name, grid=(1,), in_specs=[], out_specs=[], out_shape=[],
                       scratch_shapes=[], args=[], vmem=False)[1]


def _sems(n):
    return [pltpu.SemaphoreType.DMA((n,)), pltpu.SemaphoreType.DMA((n,))]


def _carry_gather1(shards):
    n = len(shards)
    per = 7

    def plan(x_refs, out_refs, scr):
        send_sems, recv_sems, local_sems = scr
        x, y, c = lax.axis_index("x"), lax.axis_index("y"), lax.axis_index("c")
        me, sibling = (x, y, c), (x, y, 1 - c)
        near_x, near_y, across = (1 - x, y, c), (x, 1 - y, c), (1 - x, 1 - y, c)

        def rows(ref, t, half):
            r = shards[t].shape[0]
            h = r if r < 32 else -(-(r // 2) // 16) * 16
            if half is None or h == r:
                return ref
            return ref.at[pl.ds(0, h)] if half == 0 else ref.at[pl.ds(h, r - h)]

        def copy(t, k, block, half, to, from_input=False):
            px, py, pc = block
            slab = rows(out_refs[t].at[4 * px + 2 * py + pc], t, half)
            return pltpu.make_async_remote_copy(
                src_ref=rows(x_refs[t], t, half) if from_input else slab, dst_ref=slab,
                send_sem=send_sems.at[per * t + k], recv_sem=recv_sems.at[per * t + k], device_id=to,
                device_id_type=MESH)

        two = [shards[t].shape[0] >= 32 for t in range(n)]
        local = lambda t: pltpu.make_async_copy(x_refs[t], out_refs[t].at[4 * x + 2 * y + c], local_sems.at[t])
        first = lambda t: ([(0, me, None, sibling), (1, me, 0, near_x)]
                           + ([(2, me, 1, near_y), (3, me, 1, near_x)] if two[t] else []) + [(4, me, 0, near_y)])
        passed = lambda t: [(5, near_x, 0, near_y)] + ([(6, near_y, 1, near_x)] if two[t] else [])
        early = lambda t: [(1, near_x, 0, me)] + ([(2, near_y, 1, me)] if two[t] else [])
        late = lambda t: ([(0, sibling, None, me), (4, near_y, 0, me), (5, across, 0, me)]
                          + ([(3, near_x, 1, me), (6, across, 1, me)] if two[t] else []))
        return copy, local, first, passed, early, late

    def start(x_refs, out_refs, scr):
        copy, local, first, _, _, _ = plan(x_refs, out_refs, scr)
        for urgent in (True, False):
            for t in range(n):
                if not urgent:
                    local(t).start()
                for k, block, half, to in first(t):
                    if (k in (1, 2)) == urgent:
                        copy(t, k, block, half, to, from_input=True).start()

    def middle(x_refs, out_refs, scr):
        copy, _, _, passed, early, _ = plan(x_refs, out_refs, scr)
        for t in range(n):
            for (k, block, half, to), fwd in zip(early(t), passed(t)):
                copy(t, k, block, half, to).wait_recv()
                copy(t, *fwd).start()

    def wait(x_refs, out_refs, scr):
        copy, local, first, passed, _, late = plan(x_refs, out_refs, scr)
        for t in range(n):
            for k, block, half, to in late(t):
                copy(t, k, block, half, to).wait_recv()
        for t in range(n):
            for k, block, half, to in first(t):
                copy(t, k, block, half, to, from_input=True).wait_send()
            for k, block, half, to in passed(t):
                copy(t, k, block, half, to).wait_send()
            local(t).wait()

    return _Carry(shards, [jax.ShapeDtypeStruct((N_DEV,) + a.shape, a.dtype) for a in shards],
                  _sems(per * n) + [pltpu.SemaphoreType.DMA((n,))], start, wait, middle=middle)


def _carry_gather2(gathered):
    n = len(gathered)

    def copies(in_refs, g_refs, scr, with_arrivals):
        send_sems, recv_sems = scr
        x, y, c = lax.axis_index("x"), lax.axis_index("y"), lax.axis_index("c")
        sends, arrivals = [], []
        for t in range(n):
            for j, (fx, fy) in enumerate(OTHER_CHIPS):
                px, py = x ^ fx, y ^ fy
                sems = dict(send_sem=send_sems.at[3 * t + j], recv_sem=recv_sems.at[3 * t + j],
                            device_id=(x, y, 1 - c), device_id_type=MESH)
                mine, theirs = 4 * px + 2 * py + c, 4 * px + 2 * py + (1 - c)
                sends.append(pltpu.make_async_remote_copy(src_ref=in_refs[t].at[mine], dst_ref=g_refs[t].at[mine], **sems))
                if with_arrivals:
                    arrivals.append(pltpu.make_async_remote_copy(
                        src_ref=in_refs[t].at[mine], dst_ref=g_refs[t].at[theirs], **sems))
        return sends, arrivals

    def start(in_refs, g_refs, scr):
        for cp in copies(in_refs, g_refs, scr, False)[0]:
            cp.start()

    def wait(in_refs, g_refs, scr):
        sends, arrivals = copies(in_refs, g_refs, scr, True)
        for cp in arrivals:
            cp.wait_recv()
        for cp in sends:
            cp.wait_send()

    return _Carry(gathered, [jax.ShapeDtypeStruct(a.shape, a.dtype) for a in gathered], _sems(3 * n), start, wait,
                  aliases={t: t for t in range(n)})


def _allreduce_rows(x, *, name):
    def body(x_ref, o_ref, sib_ref, mine_ref, tab_ref, send_sems, recv_sems):
        x, y, c = lax.axis_index("x"), lax.axis_index("y"), lax.axis_index("c")
        swap = pltpu.make_async_remote_copy(src_ref=x_ref, dst_ref=sib_ref, send_sem=send_sems.at[0],
                                            recv_sem=recv_sems.at[0], device_id=(x, y, 1 - c), device_id_type=MESH)
        swap.start()
        swap.wait()
        mine_ref[...] = x_ref[...] + sib_ref[...]
        tab_ref[pl.ds(2 * x + y, 1)] = mine_ref[...][None]

        def copy(k, slot):
            fx, fy = OTHER_CHIPS[k]
            return pltpu.make_async_remote_copy(
                src_ref=mine_ref, dst_ref=tab_ref.at[slot], send_sem=send_sems.at[1 + k], recv_sem=recv_sems.at[1 + k],
                device_id=(x ^ fx, y ^ fy, c), device_id_type=MESH)

        for k in range(3):
            copy(k, 2 * x + y).start()
        for k, (fx, fy) in enumerate(OTHER_CHIPS):
            copy(k, 2 * (x ^ fx) + (y ^ fy)).wait()
        o_ref[...] = ((tab_ref[0] + tab_ref[1]) + tab_ref[2]) + tab_ref[3]

    vmem = pl.BlockSpec(memory_space=pltpu.VMEM)
    return pl.pallas_call(
        body, name=name, out_shape=jax.ShapeDtypeStruct(x.shape, F32), in_specs=[vmem], out_specs=vmem,
        scratch_shapes=[pltpu.VMEM(x.shape, F32), pltpu.VMEM(x.shape, F32), pltpu.VMEM((4,) + x.shape, F32)] + _sems(4),
    )(x)


def _allgather(shards, *, name):
    n = len(shards)
    per = 10

    def body(*refs):
        x_refs, out_refs = refs[:n], refs[n:2 * n]
        send_sems, recv_sems, local_sems = refs[2 * n:]
        x, y, c = lax.axis_index("x"), lax.axis_index("y"), lax.axis_index("c")
        me, sibling = (x, y, c), (x, y, 1 - c)
        near_x, near_y, across = (1 - x, y), (x, 1 - y), (1 - x, 1 - y)

        def rows(ref, t, half):
            r = shards[t].shape[0]
            h = -(-(r // 2) // 16) * 16
            if half is None:
                return ref
            return ref.at[pl.ds(0, h)] if half == 0 else ref.at[pl.ds(h, r - h)]

        def copy(t, k, block, half, to, from_input=False):
            px, py, pc = block
            slab = rows(out_refs[t].at[4 * px + 2 * py + pc], t, half)
            return pltpu.make_async_remote_copy(
                src_ref=rows(x_refs[t], t, half) if from_input else slab, dst_ref=slab,
                send_sem=send_sems.at[per * t + k], recv_sem=recv_sems.at[per * t + k], device_id=to,
                device_id_type=MESH)

        mine = [pltpu.make_async_copy(x_refs[t], out_refs[t].at[4 * x + 2 * y + c], local_sems.at[t]) for t in range(n)]
        for cp in mine:
            cp.start()
        sent = []

        def send(cp):
            cp.start()
            sent.append(cp)

        for t in range(n):
            send(copy(t, 0, me, None, sibling, from_input=True))
            send(copy(t, 1, me, 0, (*near_x, c), from_input=True))
            send(copy(t, 2, me, 1, (*near_y, c), from_input=True))
            send(copy(t, 3, me, 1, (*near_x, c), from_input=True))
            send(copy(t, 4, me, 0, (*near_y, c), from_input=True))
        for t in range(n):
            copy(t, 1, (*near_x, c), 0, me).wait_recv()
            send(copy(t, 5, (*near_x, c), 0, (*near_y, c)))
            copy(t, 2, (*near_y, c), 1, me).wait_recv()
            send(copy(t, 6, (*near_y, c), 1, (*near_x, c)))
        for t in range(n):
            copy(t, 3, (*near_x, c), 1, me).wait_recv()
            send(copy(t, 7, (*near_x, c), None, sibling))
            copy(t, 4, (*near_y, c), 0, me).wait_recv()
            send(copy(t, 8, (*near_y, c), None, sibling))
            copy(t, 5, (*across, c), 0, me).wait_recv()
            copy(t, 6, (*across, c), 1, me).wait_recv()
            send(copy(t, 9, (*across, c), None, sibling))
        for t in range(n):
            copy(t, 0, sibling, None, me).wait_recv()
            for k, chip in ((7, near_x), (8, near_y), (9, across)):
                copy(t, k, (*chip, 1 - c), None, me).wait_recv()
        for cp in sent:
            cp.wait_send()
        for cp in mine:
            cp.wait()

    return pl.pallas_call(
        body, name=name, out_shape=[jax.ShapeDtypeStruct((N_DEV,) + a.shape, a.dtype) for a in shards],
        in_specs=[ANY] * n, out_specs=[ANY] * n,
        scratch_shapes=[pltpu.SemaphoreType.DMA((per * n,)), pltpu.SemaphoreType.DMA((per * n,)),
                        pltpu.SemaphoreType.DMA((n,))],
    )(*shards)


def _carry_sibling(slabs, small=None):
    n = len(slabs)
    extra = [] if small is None else [small]

    def copies(in_refs, out_refs, scr):
        send_sems, recv_sems = scr
        x, y, c = lax.axis_index("x"), lax.axis_index("y"), lax.axis_index("c")
        sibling = (x, y, 1 - c)
        out = []
        for t in range(n):
            for q in range(4):
                out.append(pltpu.make_async_remote_copy(
                    src_ref=in_refs[t].at[2 * q + (1 - c)], dst_ref=out_refs[t].at[q],
                    send_sem=send_sems.at[4 * t + q], recv_sem=recv_sems.at[4 * t + q],
                    device_id=sibling, device_id_type=MESH))
        if extra:
            out.append(pltpu.make_async_remote_copy(
                src_ref=in_refs[n], dst_ref=out_refs[n], send_sem=send_sems.at[4 * n], recv_sem=recv_sems.at[4 * n],
                device_id=sibling, device_id_type=MESH))
        return out

    def start(*refs):
        for cp in copies(*refs):
            cp.start()

    def wait(*refs):
        for cp in copies(*refs):
            cp.wait()

    return _Carry(list(slabs) + extra,
                  [jax.ShapeDtypeStruct((4,) + a.shape[1:], a.dtype) for a in slabs]
                  + [jax.ShapeDtypeStruct(a.shape, a.dtype) for a in extra], _sems(4 * n + 1), start, wait)


def _carry_chips(psums, small_sum=None):
    n = len(psums)
    table = small_sum is not None

    def copies(in_refs, out_refs, scr, arrivals):
        send_sems, recv_sems = scr[0], scr[1]
        x, y, c = lax.axis_index("x"), lax.axis_index("y"), lax.axis_index("c")
        out = []
        for k, (fx, fy) in enumerate(OTHER_CHIPS):
            px, py = x ^ fx, y ^ fy
            for t in range(n):
                out.append(pltpu.make_async_remote_copy(
                    src_ref=in_refs[t].at[2 * px + py], dst_ref=out_refs[t].at[k],
                    send_sem=send_sems.at[3 * t + k], recv_sem=recv_sems.at[3 * t + k],
                    device_id=(px, py, c), device_id_type=MESH))
            if table:
                slot = 2 * px + py if arrivals else 2 * x + y
                out.append(pltpu.make_async_remote_copy(
                    src_ref=in_refs[n], dst_ref=out_refs[n].at[slot], send_sem=send_sems.at[3 * n + k],
                    recv_sem=recv_sems.at[3 * n + k], device_id=(px, py, c), device_id_type=MESH))
        return out

    def own(in_refs, out_refs, scr):
        x, y = lax.axis_index("x"), lax.axis_index("y")
        return pltpu.make_async_copy(in_refs[n], out_refs[n].at[2 * x + y], scr[2])

    def start(in_refs, out_refs, scr):
        if table:
            own(in_refs, out_refs, scr).start()
        for cp in copies(in_refs, out_refs, scr, False):
            cp.start()

    def wait(in_refs, out_refs, scr):
        for cp in copies(in_refs, out_refs, scr, True):
            cp.wait()
        if table:
            own(in_refs, out_refs, scr).wait()

    out_shapes = [jax.ShapeDtypeStruct((3,) + a.shape[1:], a.dtype) for a in psums]
    if table:
        out_shapes.append(jax.ShapeDtypeStruct((4,) + small_sum.shape, F32))
    return _Carry(list(psums) + ([small_sum] if table else []), out_shapes,
                  _sems(3 * n + 3) + ([pltpu.SemaphoreType.DMA] if table else []), start, wait)


def _to_comm(name, kind, block, dtype=BF16):
    a = block[0]
    if kind == "cols":
        a = a.T
        if name == "w_in" and dtype == BF16:
            a = jnp.pad(a, ((0, IN_SHARD_PAD - IN_SHARD), (0, 0)))
    return a if kind == "f32" else a.astype(dtype)


def _from_comm(name, kind, a):
    if kind == "cols":
        if name == "w_in" and a.shape[0] != IN_SHARD:
            a = a[:IN_SHARD]
        a = a.T
    return a[None]


def _assemble_weights(g):
    out = {}
    if "w_in" in g:
        out["wt_main"] = _assemble_wt_main(g["w_in"], name="assemble_w_in")
        j, l0 = divmod(O_F, IN_SHARD)
        out["wt_f"] = jnp.pad(g["w_in"][j, l0:l0 + HEADS], ((0, 128 - HEADS), (0, 0)))
    square = dict(w_branch_a="w_a", w_branch_b="w_b", w_out="w_out", w_ple_gate="w_pg")
    for long, short in square.items():
        if long in g:
            out[short] = g[long].reshape(D_MODEL, D_MODEL)
    if "w_up" in g:
        out["wt_up"] = g["w_up"].reshape(2 * D_FF, D_MODEL)
    if "conv_w" in g:
        out["conv_w"] = g["conv_w"].transpose(1, 0, 2).reshape(3, 2 * D_FF)
    if "w_down" in g:
        out["w_down"] = g["w_down"].reshape(D_FF, D_MODEL)
    if "w_ple" in g:
        out["wt_ple"] = g["w_ple"].reshape(D_MODEL, PLE_DIM)
    return out


def _grad_slabs(gr):
    out = {}
    if "wt_main" in gr:
        gm, gf = gr["wt_main"], gr["wt_f"]
        segments = ((0, 2048, gm, 0), (2048, O_F, gm, 2048), (O_F, O_G, gf, -O_F), (O_G, IN_COLS, gm, 2048 - O_G))
        slabs = []
        for j in range(N_DEV):
            lo, hi = j * IN_SHARD, (j + 1) * IN_SHARD
            pieces = [src[max(lo, a) + shift:min(hi, b) + shift] for a, b, src, shift in segments if max(lo, a) < min(hi, b)]
            pieces.append(jnp.zeros((IN_SHARD_PAD - IN_SHARD, D_MODEL), gm.dtype))
            slabs.append(jnp.concatenate(pieces, axis=0))
        out["w_in"] = jnp.stack(slabs)
    rows = dict(w_a="w_branch_a", w_b="w_branch_b", w_out="w_out", wt_up="w_up", w_down="w_down", w_pg="w_ple_gate")
    for short, long in rows.items():
        if short in gr:
            out[long] = gr[short].reshape(N_DEV, -1, D_MODEL)
    if "conv_w" in gr:
        out["conv_w"] = gr["conv_w"].reshape(3, N_DEV, -1).transpose(1, 0, 2)
    if "wt_ple" in gr:
        out["w_ple"] = gr["wt_ple"].reshape(N_DEV, -1, PLE_DIM)
    return {k: v.astype(BF16) for k, v in out.items()}


def _rows(a, rows):
    flat = a.reshape(-1)
    return jnp.pad(flat, (0, rows * 1024 - flat.shape[0])).reshape(rows, 1024)


def _pack_small(parts):
    return jnp.concatenate([_rows(parts[n].astype(F32), r) for n, r in SMALL], axis=0)


def _small(packed, name, shape):
    off, r = SMALL_OFF[name]
    n = math.prod(shape)
    return packed[off:off + r].reshape(-1)[:n].reshape(shape)


class _Exchanges:
    W_S_ROWS = SMALL_OFF["gmlp_w_s"]

    def __init__(self, later, shards, pos):
        self.later, self.shards, self.pos = later, dict(zip(later, shards)), pos
        self.level1, self.slabs, self.from_sib, self.sums32, self.reduced, self.tables = {}, {}, {}, {}, {}, {}

    def gather1(self, names):
        carry = _carry_gather1([self.shards[n] for n in names])
        carry.names = names
        return carry

    def gather1_done(self, carry, results):
        self.level1.update(zip(carry.names, results))

    def gather2(self):
        return _carry_gather2([self.level1[n] for n in self.later])

    def weights(self, full):
        return _assemble_weights(dict(zip(self.later, full)))

    def sibling(self, grads):
        slabs = _grad_slabs(grads)
        self.slabs.update(slabs)
        carry = _carry_sibling(list(slabs.values()))
        carry.names = list(slabs)
        return carry

    def sibling_done(self, carry, results):
        self.from_sib.update(zip(carry.names, results))

    def chips(self, names, table=None):
        sums = {n: _sum_pairs(self.slabs[n], self.from_sib[n], self.pos, name="sum_sibling_" + n) for n in names}
        self.sums32.update({n: s32 for n, (s32, _) in sums.items()})
        carry = _carry_chips([s16 for _, s16 in sums.values()], None if table is None else self.table_part(table))
        carry.names, carry.table = list(names), table
        return carry

    def chips_done(self, carry, results):
        if carry.table is not None:
            *results, self.tables[carry.table] = results
        self.reduced.update({n: (self.sums32[n], r) for n, r in zip(carry.names, results)})

    def sibling_small(self, small_g):
        self.small_g = small_g
        return _carry_sibling([], small_g)

    def sibling_small_done(self, small_sib):
        self.small_chip = _pair_sum_small(self.small_g, small_sib, name="sum_sibling_small")

    def table_part(self, which):
        off, rows = self.W_S_ROWS
        if which == "w_s":
            return self.small_chip[off:off + rows]
        return jnp.concatenate([self.small_chip[:off], self.small_chip[off + rows:]], axis=0)

    def table(self):
        off = self.W_S_ROWS[0]
        rest = self.tables["rest"]
        return jnp.concatenate([rest[:, :off], self.tables["w_s"], rest[:, off:]], axis=1)


def _local_step(x, p, target, w, sm, ex=None):
    s = x.shape[0]
    mm = _matmul
    wt_main = w["wt_main"]
    conv_b = sm["conv_b"]
    bs_t = jnp.pad(sm["gmlp_b_s"].T, ((0, 0), (0, 128 - GROUPS)))
    b_f = jnp.pad(sm["b_f"], ((0, 0), (0, 128 - HEADS)))
    big = dict(tm=1024, tn=1024, tk=1024)
    whole_s = dict(tn=1024, tk=s)

    h = _rmsnorm_fwd(x, sm["norm_mix_g"], name="norm_mix")
    tall = dict(tm=s, tn=512, tk=1024)
    qkv_args = dict(mode="nt", out_dtype=BF16, name="in_qkv", n=3072, b_off=8, **tall)
    f_logit = mm(h, w["wt_f"], mode="nt", out_dtype=F32, name="in_f", tm=1024, tk=1024)
    cqe = _forget_cumsum(f_logit, b_f, name="forget_cumsum")
    uvg = dict(mode="nt", out_dtype=F32, name="in_uvg", n=4096, **tall)
    if ex is None:
        qkv = mm(h, wt_main, **qkv_args)
        (qa, ka, vt), _ = _attn_prep(qkv, cqe, name="attn_prep")
        (b, lse3), _ = _attn_fwd(qa, ka, vt, name="attn_fwd")
        zuvg = mm(h, wt_main, **uvg)
    else:
        groups = (["w_branch_a"], ["w_branch_b"], [n for n in ex.later if n not in ("w_branch_a", "w_branch_b")])
        carries = [ex.gather1(names) for names in groups]
        qkv, got0 = mm(h, wt_main, carry=carries[0], **qkv_args)
        (qa, ka, vt), got1 = _attn_prep(qkv, cqe, carries[1], name="attn_prep")
        (b, lse3), got2 = _attn_fwd(qa, ka, vt, carries[2], name="attn_fwd")
        for carry, got in zip(carries, (got0, got1, got2)):
            ex.gather1_done(carry, got)
        zuvg, full = mm(h, wt_main, carry=ex.gather2(), **uvg)
        w = {**w, **ex.weights(full)}
    a = _gmlp_fwd(zuvg, sm["gmlp_ln_g"], sm["gmlp_ln_b"], sm["gmlp_w_s"], bs_t, name="gmlp_fwd")
    wt_up, conv_w = w["wt_up"], w["conv_w"]
    ya, yb, merged = _branches_merge(a, b, w["w_a"], w["w_b"], zuvg, name="branches_merge")
    x1, h2 = mm(merged, w["w_out"], mode="nn", out_dtype=F32, name="out_proj", add=x, norm_g=sm["norm_ffn_g"], **big)
    up_a, up_g, act = _up_convglu(h2, wt_up, conv_w, conv_b, name="up_convglu")
    x2, h3 = mm(act, w["w_down"], mode="nn", out_dtype=F32, name="down", tm=1024, tn=1024, tk=1408, add=x1,
                norm_g=sm["norm_ple_g"])

    loss, dx3, dple, dgp, d_norm_final = _ple_loss(p, w["wt_ple"], h3, w["w_pg"], x2, target, sm["norm_final_g"],
                                                   name="ple_loss")
    g_wt_ple = mm(dple, p, mode="tn", out_dtype=BF16, name="d_w_ple", tm=512, tn=256, tk=s)
    g_w_pg = mm(h3, dgp, mode="tn", out_dtype=BF16, name="d_w_pg", tm=512, **whole_s)
    (dx2, dx2b, d_norm_ple), _ = _matmul_rmsnorm_bwd([dgp], w["w_pg"], dx3, x2, sm["norm_ple_g"], mode="nt", tk=1024,
                                                     name="d_h3_norm_ple_bwd")
    g_w_down = mm(act, dx2b, mode="tn", out_dtype=BF16, name="d_w_down", tm=1408, **whole_s)
    dact_args = dict(mode="nt", out_dtype=BF16, name="d_act", tm=s, tn=256, tk=1024)
    if ex is None:
        dact = mm(dx2b, w["w_down"], **dact_args)
    else:
        early = ex.sibling(dict(w_pg=g_w_pg, wt_ple=g_wt_ple))
        dact, got = mm(dx2b, w["w_down"], carry=early, **dact_args)
        ex.sibling_done(early, got)
    dup_a, dup_g, dcw_a, dcw_g, dcb_a, dcb_g = _convglu_bwd(dact, up_a, up_g, conv_w, conv_b, name="convglu_bwd")
    g_wt_up = mm(dup_a, h2, mode="tn", out_dtype=BF16, name="d_w_up_a", tm=1408, out_rows=2 * D_FF, **whole_s)
    g_wt_up = mm(dup_g, h2, mode="tn", out_dtype=BF16, name="d_w_up_g", tm=1408, out_rows=2 * D_FF,
                 o_off=D_FF // 1408, into=g_wt_up, **whole_s)
    (dx1, dx1b, d_norm_ffn), _ = _matmul_rmsnorm_bwd([dup_a, dup_g], wt_up, dx2, x1, sm["norm_ffn_g"], mode="nn",
                                                     tk=1408, name="d_h2_norm_ffn_bwd", resident=True)
    g_w_out = mm(merged, dx1b, mode="tn", out_dtype=BF16, name="d_w_out", tm=512, **whole_s)
    dya, dyb, dga, dgb = _merge_bwd(dx1b, w["w_out"], ya, yb, zuvg, name="merge_bwd")
    g_w_a = mm(a, dya, mode="tn", out_dtype=BF16, name="d_w_a", tm=512, **whole_s)
    g_w_b = mm(b, dyb, mode="tn", out_dtype=BF16, name="d_w_b", tm=512, **whole_s)
    da = mm(dya, w["w_a"], mode="nt", out_dtype=BF16, name="d_a", **big)
    db = mm(dyb, w["w_b"], mode="nt", out_dtype=BF16, name="d_b", **big)
    grads = dict(w_a=g_w_a, w_b=g_w_b, w_out=g_w_out, wt_up=g_wt_up, conv_w=jnp.concatenate([dcw_a, dcw_g], axis=1),
                 w_down=g_w_down, wt_ple=g_wt_ple, w_pg=g_w_pg)
    gmlp_args = (da, zuvg, sm["gmlp_ln_g"], sm["gmlp_ln_b"], sm["gmlp_w_s"], bs_t)
    if ex is None:
        (dzu, dzv, d_w_s, d_bs_t, d_ln_g, d_ln_b), _ = _gmlp_bwd(*gmlp_args, name="gmlp_bwd")
    else:
        rest = ex.sibling({k: v for k, v in grads.items() if k not in ("w_pg", "wt_ple")})
        early_chips = ex.chips(early.names)
        both = _carry_join(rest, early_chips)
        (dzu, dzv, d_w_s, d_bs_t, d_ln_g, d_ln_b), got = _gmlp_bwd(*gmlp_args, both, name="gmlp_bwd")
        got_rest, got_early = both.split(got)
        ex.sibling_done(rest, got_rest)
        ex.chips_done(early_chips, got_early)
    small = dict(norm_mix_g=jnp.zeros((1, D_MODEL), F32), b_f=jnp.zeros((1, HEADS), F32), gmlp_ln_g=d_ln_g,
                 gmlp_ln_b=d_ln_b, gmlp_w_s=d_w_s, gmlp_b_s=d_bs_t[:, :GROUPS].T, norm_ffn_g=d_norm_ffn,
                 conv_b=jnp.concatenate([dcb_a, dcb_g], axis=1), norm_ple_g=d_norm_ple, norm_final_g=d_norm_final)
    if ex is None:
        delta3, _ = _attn_delta(db, b, name="attn_delta")
        (dq, dk, dv, aux, dcq3), _ = _attn_bwd(qa, ka, qkv, db, lse3, delta3, name="attn_bwd")
    else:
        delta3, (small_sib,) = _attn_delta(db, b, ex.sibling_small(_pack_small(small)), name="attn_delta")
        ex.sibling_small_done(small_sib)
        main_chips = ex.chips(rest.names, table="rest")
        (dq, dk, dv, aux, dcq3), got = _attn_bwd(qa, ka, qkv, db, lse3, delta3, main_chips, name="attn_bwd")
        ex.chips_done(main_chips, got)
    dcq16 = jnp.pad(dcq3[:, :2, :].reshape(HEADS, s).T, ((0, 0), (0, 128 - HEADS)))
    dzf, d_b_f = _forget_bwd(dcq16, aux, f_logit, b_f, name="forget_bwd")
    dz_parts = [dzu, dzv, dga, dgb, dq, dk, dv]
    w_s_chips = None if ex is None else ex.chips([], table="w_s")
    g_wt_main, got = _grad_w_parts(dz_parts, h, name="d_w_main", tm=512, carry=w_s_chips)
    if ex is not None:
        ex.chips_done(w_s_chips, got)
    g_wt_f = mm(dzf, h, mode="tn", out_dtype=BF16, name="d_w_f", **whole_s)
    grads = dict(grads, wt_main=g_wt_main, wt_f=g_wt_f)
    w_in_chips = None
    if ex is not None:
        w_in_sib = ex.sibling(dict(wt_main=g_wt_main, wt_f=g_wt_f))
        ex.sibling_done(w_in_sib, _run_carry(w_in_sib, name="exchange_sibling_w_in"))
        w_in_chips = ex.chips(w_in_sib.names)
    (dx0, _, d_norm_mix), got = _matmul_rmsnorm_bwd(dz_parts, wt_main, dx1, x, sm["norm_mix_g"], mode="nn", tk=1024,
                                                    extra=(dzf, w["wt_f"]), name="d_h_norm_mix_bwd", carry=w_in_chips,
                                                    lead=True)
    if ex is not None:
        ex.chips_done(w_in_chips, got)
    return loss, dx0, grads, dict(small, norm_mix_g=d_norm_mix, b_f=d_b_f[:, :HEADS])


def kernel(x, p, norm_mix_g, w_in, b_f, gmlp_ln_g, gmlp_ln_b, gmlp_w_s, gmlp_b_s, w_branch_a, w_branch_b, w_out, norm_ffn_g, w_up, conv_w, conv_b, w_down, norm_ple_g, w_ple, w_ple_gate, norm_final_g, loss_target, m_norm_mix_g, m_w_in, m_b_f, m_gmlp_ln_g, m_gmlp_ln_b, m_gmlp_w_s, m_gmlp_b_s, m_w_branch_a, m_w_branch_b, m_w_out, m_norm_ffn_g, m_w_up, m_conv_w, m_conv_b, m_w_down, m_norm_ple_g, m_w_ple, m_w_ple_gate, m_norm_final_g, v_norm_mix_g, v_w_in, v_b_f, v_gmlp_ln_g, v_gmlp_ln_b, v_gmlp_w_s, v_gmlp_b_s, v_w_branch_a, v_w_branch_b, v_w_out, v_norm_ffn_g, v_w_up, v_conv_w, v_conv_b, v_w_down, v_norm_ple_g, v_w_ple, v_w_ple_gate, v_norm_final_g):
    given = dict(locals())
    weights = {n: given[n] for n in WEIGHT_ORDER}
    mom_m = {n: given["m_" + n] for n in WEIGHT_ORDER}
    mom_v = {n: given["v_" + n] for n in WEIGHT_ORDER}
    pos = jnp.stack([lax.axis_index("x"), lax.axis_index("y"), lax.axis_index("c")]).astype(I32)
    names = [n for n, _ in SHARDED]
    kinds = dict(SHARDED)

    later = [n for n in names if n != "w_in"]

    first = _allgather([_to_comm("w_in", kinds["w_in"], weights["w_in"])], name="allgather_w_in")
    ex = _Exchanges(later, [_to_comm(n, kinds[n], weights[n]) for n in later], pos)

    sm = dict(norm_mix_g=norm_mix_g, b_f=b_f, gmlp_ln_g=gmlp_ln_g, gmlp_ln_b=gmlp_ln_b, gmlp_w_s=gmlp_w_s[0],
              gmlp_b_s=gmlp_b_s[0], norm_ffn_g=norm_ffn_g, conv_b=conv_b, norm_ple_g=norm_ple_g,
              norm_final_g=norm_final_g.reshape(1, D_MODEL))
    loss_part, dx0, grads, small = _local_step(
        x[0], p[0, 0], loss_target[0], _assemble_weights({"w_in": first[0]}), sm, ex)

    b_f_and_loss = jnp.concatenate([small["b_f"].reshape(-1), loss_part[0, :1]])
    last = _allreduce_rows(jnp.concatenate([_rows(small["norm_mix_g"], 8), _rows(b_f_and_loss, 8)], axis=0),
                           name="allreduce_last")
    loss = last[8, HEADS]
    small_last = jnp.pad(last, ((0, SMALL_ROWS - 16), (0, 0)))

    grad, delta, new_m, new_v = {}, {}, {}, {}
    for n in names:
        s32, r = ex.reduced[n]
        outs = _adam_sharded(s32, r, *[_to_comm(n, kinds[n], src[n], F32) for src in (weights, mom_m, mom_v)], pos,
                             name="adam_" + n)
        grad[n], delta[n], new_m[n], new_v[n] = [_from_comm(n, kinds[n], o) for o in outs]
    replicated = [n for n, _ in SMALL]
    rep = lambda src: _pack_small({n: src[n] for n in replicated})
    packed = _adam_replicated(ex.table(), small_last, rep(weights), rep(mom_m), rep(mom_v), name="adam_replicated")
    for out, pk in zip((grad, delta, new_m, new_v), packed):
        for n in replicated:
            out[n] = _small(pk, n, weights[n].shape)

    return (loss, dx0, *[grad[n] for n in WEIGHT_ORDER], *[delta[n] for n in WEIGHT_ORDER],
            *[new_m[n] for n in WEIGHT_ORDER], *[new_v[n] for n in WEIGHT_ORDER])
```

```python
import functools
import math

import jax
import jax.numpy as jnp
from jax import lax
from jax.experimental import pallas as pl
from jax.experimental.pallas import tpu as pltpu

F32 = jnp.float32
BF16 = jnp.bfloat16
I32 = jnp.int32

D_MODEL = 1024
GROUPS = 8
GDIM = 128
GBLOCK = 128
CHUNK = 64
HEADS = 16
HEAD_DIM = 64
D_FF = 2816
PLE_DIM = 256
EPS = 1e-6
N_DEV = 8
ATT_SCALE = HEAD_DIM ** -0.5
NEG = -1e30

ADAM_LR = 0.001
ADAM_B1 = 0.9
ADAM_B2 = 0.999
ADAM_EPS = 1e-08
ADAM_WD = 0.01
ADAM_STEP = 10

V7X_VMEM_LIMIT = 48 * 1024 * 1024
MESH = pl.DeviceIdType.MESH

O_F = 2 * 1024 + 3 * 1024
O_G = O_F + HEADS
IN_COLS = O_G + 2 * D_MODEL
MAIN_COLS = IN_COLS - HEADS
IN_SHARD = IN_COLS // N_DEV
IN_SHARD_PAD = 912

SHARDED = (("w_in", "cols"), ("w_branch_a", "rows"), ("w_branch_b", "rows"), ("w_out", "rows"), ("w_up", "cols"),
           ("conv_w", "f32"), ("w_down", "rows"), ("w_ple", "cols"), ("w_ple_gate", "rows"))

SMALL = (("norm_mix_g", 8), ("b_f", 8), ("gmlp_ln_g", 8), ("gmlp_ln_b", 8), ("gmlp_w_s", 128), ("gmlp_b_s", 8),
         ("norm_ffn_g", 8), ("conv_b", 8), ("norm_ple_g", 8), ("norm_final_g", 8))
SMALL_OFF = {}
_o = 0
for _n, _r in SMALL:
    SMALL_OFF[_n] = (_o, _r)
    _o += _r
SMALL_ROWS = _o

WEIGHT_ORDER = ("norm_mix_g", "w_in", "b_f", "gmlp_ln_g", "gmlp_ln_b", "gmlp_w_s", "gmlp_b_s", "w_branch_a",
                "w_branch_b", "w_out", "norm_ffn_g", "w_up", "conv_w", "conv_b", "w_down", "norm_ple_g", "w_ple",
                "w_ple_gate", "norm_final_g")


def _cparams(sem):
    return pltpu.CompilerParams(dimension_semantics=sem, vmem_limit_bytes=V7X_VMEM_LIMIT)


def _gelu(x):
    c = math.sqrt(2.0 / math.pi)
    return 0.5 * x * (1.0 + jnp.tanh(c * (x + 0.044715 * x * x * x)))


def _gelu_and_grad(x):
    c = math.sqrt(2.0 / math.pi)
    t = jnp.tanh(c * (x + 0.044715 * x * x * x))
    g = 0.5 * x * (1.0 + t)
    dg = 0.5 * (1.0 + t) + 0.5 * x * (1.0 - t * t) * (c * (1.0 + 3.0 * 0.044715 * x * x))
    return g, dg


def _sigmoid(x):
    return 1.0 / (1.0 + jnp.exp(-x))


def _dot(a, b, dims):
    return lax.dot_general(a, b, (dims, ((), ())), preferred_element_type=F32)


NN = ((1,), (0,))
NT = ((1,), (1,))
TN = ((0,), (0,))


def _row_tile(rows, most):
    best = None
    for t in range(16, min(rows, most) + 1, 16):
        if rows % t == 0:
            best = t
    return best if best is not None else rows


def _matmul(a, b, *, mode, out_dtype, name, tm=512, tn=512, tk=512, add=None, n=None, b_off=0,
            out_rows=None, o_off=0, into=None, norm_g=None, carry=None):
    if mode == "tn":
        kdim, m = a.shape
    else:
        m, kdim = a.shape
    if n is None:
        n = b.shape[0] if mode == "nt" else b.shape[1]
    tm, tn, tk = min(tm, m), min(tn, n), min(tk, kdim)
    assert m % tm == 0 and n % tn == 0 and kdim % tk == 0, (name, m, n, kdim, tm, tn, tk)
    nk = kdim // tk
    dims = {"nn": NN, "nt": NT, "tn": TN}[mode]

    n_in = 2 + (add is not None) + (into is not None) + (norm_g is not None)
    assert norm_g is None or tn == n, "the RMS norm needs whole rows"

    def finish(r, refs):
        if add is not None:
            r = refs[2][...].astype(F32) + r
        refs[n_in][...] = r.astype(out_dtype)
        if norm_g is not None:
            rs = lax.rsqrt(jnp.mean(r * r, axis=-1, keepdims=True) + EPS)
            refs[n_in + 1][...] = ((r * rs) * refs[n_in - 1][...]).astype(BF16)

    def body(*refs):
        a_ref, b_ref = refs[:2]
        part = _dot(a_ref[...].astype(BF16), b_ref[...].astype(BF16), dims)
        if nk == 1:
            finish(part, refs)
            return
        acc_ref = refs[-1]
        k = pl.program_id(2)

        @pl.when(k == 0)
        def _():
            acc_ref[...] = part

        @pl.when((k > 0) & (k < nk - 1))
        def _():
            acc_ref[...] += part

        @pl.when(k == nk - 1)
        def _():
            finish(acc_ref[...] + part, refs)

    a_spec = pl.BlockSpec((tk, tm), lambda i, j, k: (k, i)) if mode == "tn" else pl.BlockSpec((tm, tk), lambda i, j, k: (i, k))
    if mode == "nt":
        b_spec = pl.BlockSpec((tn, tk), lambda i, j, k: (j + b_off, k))
    else:
        b_spec = pl.BlockSpec((tk, tn), lambda i, j, k: (k + b_off, j))
    o_spec = pl.BlockSpec((tm, tn), lambda i, j, k: (i + o_off, j))
    in_specs = [a_spec, b_spec] + ([pl.BlockSpec((tm, tn), lambda i, j, k: (i, j))] if add is not None else [])
    args = (a, b) + ((add,) if add is not None else ())
    aliases = {}
    if into is not None:
        aliases = {len(args): 0}
        in_specs.append(pl.BlockSpec(memory_space=pl.ANY))
        args += (into,)
    out_specs = [o_spec]
    out_shape = [jax.ShapeDtypeStruct((m if out_rows is None else out_rows, n), out_dtype)]
    if norm_g is not None:
        in_specs.append(pl.BlockSpec((1, n), lambda i, j, k: (0, 0)))
        args += (norm_g,)
        out_specs.append(pl.BlockSpec((tm, tn), lambda i, j, k: (i, j)))
        out_shape.append(jax.ShapeDtypeStruct((m, n), BF16))
    outs, carried = _carry_call(
        body, carry, name=name, grid=(m // tm, n // tn, nk), in_specs=in_specs, out_specs=out_specs,
        out_shape=out_shape, scratch_shapes=[pltpu.VMEM((tm, tn), F32)] if nk > 1 else [], args=args,
        own_aliases=aliases)
    out = outs[0] if norm_g is None else tuple(outs)
    return out if carry is None else (out, carried)


def _row_spec(tr, width, col_block=0):
    return pl.BlockSpec((tr, width), lambda i: (i, col_block))


def _full_spec(shape):
    return pl.BlockSpec(shape, lambda i: tuple(0 for _ in shape))


def _rmsnorm_fwd(x, g, *, name, tr=256):
    s, d = x.shape

    def body(x_ref, g_ref, o_ref):
        xv = x_ref[...]
        r = lax.rsqrt(jnp.mean(xv * xv, axis=-1, keepdims=True) + EPS)
        o_ref[...] = ((xv * r) * g_ref[...]).astype(BF16)

    return pl.pallas_call(
        body, name=name, grid=(s // tr,),
        in_specs=[_row_spec(tr, d), _full_spec((1, d))], out_specs=_row_spec(tr, d),
        out_shape=jax.ShapeDtypeStruct((s, d), BF16), compiler_params=_cparams(("parallel",)),
    )(x, g)


def _matmul_rmsnorm_bwd(a_parts, b, dres, x, g, *, mode, tk, name, extra=None, tm=512, carry=None, lead=False,
                        resident=False):
    s, d = x.shape
    n_row = s // tm
    spans, lo = [], 0
    for a in a_parts:
        spans.append((lo, lo + a.shape[1] // tk))
        lo = spans[-1][1]
    n_main, total = lo, lo + (extra is not None)
    n_parts = len(a_parts)

    def body(*refs):
        a_refs, b_ref = refs[:n_parts], refs[n_parts]
        k0 = n_parts + 1
        ax_ref, bx_ref = (refs[k0], refs[k0 + 1]) if extra is not None else (None, None)
        k0 += 2 * (extra is not None)
        dres_ref, x_ref, g_ref, dx_ref, dxb_ref, dg_ref, acc_all = refs[k0:k0 + 7]
        if resident:
            kk, i = pl.program_id(0), pl.program_id(1)
            acc_ref = acc_all.at[pl.ds(pl.multiple_of(i * tm, tm), tm)]
        else:
            i, kk = pl.program_id(0), pl.program_id(1)
            acc_ref = acc_all

        def accumulate(part, first):
            if first:
                @pl.when(kk == 0)
                def _():
                    acc_ref[...] = part

                @pl.when(kk > 0)
                def _():
                    acc_ref[...] += part
            else:
                acc_ref[...] += part

        for p, (a_ref, (lo_p, hi_p)) in enumerate(zip(a_refs, spans)):
            @pl.when((kk >= lo_p) & (kk < hi_p))
            def _(a_ref=a_ref, lo_p=lo_p):
                accumulate(_dot(a_ref[...].astype(BF16), b_ref[...].astype(BF16), NN if mode == "nn" else NT), lo_p == 0)

        if extra is not None:
            @pl.when(kk == n_main)
            def _():
                accumulate(_dot(ax_ref[...].astype(BF16), bx_ref[...].astype(BF16), NN), False)

        @pl.when(kk == total - 1)
        def _():
            dhv = acc_ref[...]
            xv = x_ref[...]
            r = lax.rsqrt(jnp.mean(xv * xv, axis=-1, keepdims=True) + EPS)
            xhat = xv * r
            dxhat = dhv * g_ref[...]
            dx = dres_ref[...] + r * (dxhat - xhat * jnp.mean(dxhat * xhat, axis=-1, keepdims=True))
            dx_ref[...] = dx
            dxb_ref[...] = dx.astype(BF16)
            dgp = jnp.sum(dhv * xhat, axis=0, keepdims=True)

            @pl.when(i == 0)
            def _():
                dg_ref[...] = dgp

            @pl.when(i > 0)
            def _():
                dg_ref[...] += dgp

    def spec(shape, index):
        return pl.BlockSpec(shape, (lambda kk, i: index(i, kk)) if resident else index)

    def row(i, kk, lo_p, hi_p):
        if not resident:
            return i
        return jnp.where(kk < lo_p, 0, jnp.where(kk >= hi_p, n_row - 1, i))

    a_specs = [spec((tm, tk), lambda i, kk, lo_p=lo_p, hi_p=hi_p: (row(i, kk, lo_p, hi_p),
                                                                    jnp.clip(kk - lo_p, 0, hi_p - lo_p - 1)))
               for lo_p, hi_p in spans]
    step = lambda kk: jnp.minimum(kk, n_main - 1)
    b_spec = (spec((tk, d), lambda i, kk: (step(kk), 0)) if mode == "nn"
              else spec((d, tk), lambda i, kk: (0, step(kk))))
    rows = spec((tm, d), lambda i, kk: (row(i, kk, total - 1, total), 0))
    one = spec((1, d), lambda i, kk: (0, 0))
    dx_spec, dx_shape = rows, jax.ShapeDtypeStruct((s, d), F32)
    if lead:
        dx_spec = spec((None, tm, d), lambda i, kk: (0, row(i, kk, total - 1, total), 0))
        dx_shape = jax.ShapeDtypeStruct((1, s, d), F32)
    x_specs, x_args = [], []
    if extra is not None:
        kx = extra[0].shape[1]
        x_specs = [spec((tm, kx), lambda i, kk: (row(i, kk, n_main, total), 0)), spec((kx, d), lambda i, kk: (0, 0))]
        x_args = list(extra)
    (dx, dxb, dg), carried = _carry_call(
        body, carry, name=name, grid=(total, n_row) if resident else (n_row, total),
        in_specs=a_specs + [b_spec] + x_specs + [rows, rows, one], out_specs=[dx_spec, rows, one],
        out_shape=[dx_shape, jax.ShapeDtypeStruct((s, d), BF16), jax.ShapeDtypeStruct((1, d), F32)],
        scratch_shapes=[pltpu.VMEM((s if resident else tm, d), F32)], args=list(a_parts) + [b] + x_args + [dres, x, g])
    return (dx, dxb, dg), carried


def _grad_w_parts(a_parts, b, *, name, tm=512, carry=None):
    s, width = a_parts[0].shape
    per, n = width // tm, b.shape[1]

    def body(*refs):
        a_refs, b_ref, o_ref = refs[:len(a_parts)], refs[len(a_parts)], refs[len(a_parts) + 1]
        i = pl.program_id(0)
        for p, a_ref in enumerate(a_refs):
            @pl.when(i // per == p)
            def _(a_ref=a_ref):
                o_ref[...] = _dot(a_ref[...].astype(BF16), b_ref[...].astype(BF16), TN).astype(BF16)

    a_specs = [pl.BlockSpec((s, tm), lambda i, p=p: (0, jnp.clip(i - p * per, 0, per - 1))) for p in range(len(a_parts))]
    (out,), carried = _carry_call(
        body, carry, name=name, grid=(len(a_parts) * per,),
        in_specs=a_specs + [pl.BlockSpec((s, n), lambda i: (0, 0))], out_specs=[pl.BlockSpec((tm, n), lambda i: (i, 0))],
        out_shape=[jax.ShapeDtypeStruct((len(a_parts) * width, n), BF16)], scratch_shapes=[], args=list(a_parts) + [b])
    return out, carried


def _ple_loss(p, wt_ple, h3, w_pg, x2, target, g, *, name, tm=256):
    s, d = x2.shape
    kp = p.shape[1]

    def body(p_ref, wp_ref, h_ref, wg_ref, x_ref, t_ref, g_ref, loss_ref, dx_ref, dple_ref, dgp_ref, dg_ref):
        i = pl.program_id(0)
        ple = _dot(p_ref[...].astype(BF16), wp_ref[...], NT)
        sg = _sigmoid(_dot(h_ref[...], wg_ref[...], NN))
        xv = x_ref[...] + ple * sg
        r = lax.rsqrt(jnp.mean(xv * xv, axis=-1, keepdims=True) + EPS)
        xhat = xv * r
        diff = xhat * g_ref[...] - t_ref[...]
        lp = jnp.zeros((1, 128), F32) + (0.5 / d) * jnp.sum(diff * diff)
        dy = diff * (1.0 / d)
        dxhat = dy * g_ref[...]
        dx = r * (dxhat - xhat * jnp.mean(dxhat * xhat, axis=-1, keepdims=True))
        dx_ref[...] = dx
        dple_ref[...] = (dx * sg).astype(BF16)
        dgp_ref[...] = (dx * ple * (sg * (1.0 - sg))).astype(BF16)
        dgp = jnp.sum(dy * xhat, axis=0, keepdims=True)

        @pl.when(i == 0)
        def _():
            dg_ref[...] = dgp
            loss_ref[...] = lp

        @pl.when(i > 0)
        def _():
            dg_ref[...] += dgp
            loss_ref[...] += lp

    rows = _row_spec(tm, d)
    return pl.pallas_call(
        body, name=name, grid=(s // tm,),
        in_specs=[_row_spec(tm, kp), _full_spec((d, kp)), rows, _full_spec((d, d)), rows, rows, _full_spec((1, d))],
        out_specs=[_full_spec((1, 128)), rows, rows, rows, _full_spec((1, d))],
        out_shape=[jax.ShapeDtypeStruct((1, 128), F32), jax.ShapeDtypeStruct((s, d), F32),
                   jax.ShapeDtypeStruct((s, d), BF16), jax.ShapeDtypeStruct((s, d), BF16),
                   jax.ShapeDtypeStruct((1, d), F32)],
        compiler_params=_cparams(("arbitrary",)),
    )(p, wt_ple, h3, w_pg, x2, target, g)


def _branches_merge(a, b, w_a, w_b, zuvg, *, name, tm=512):
    s, d = a.shape

    def body(a_ref, b_ref, wa_ref, wb_ref, ga_ref, gb_ref, ya_ref, yb_ref, o_ref):
        ya = _dot(a_ref[...], wa_ref[...], NN)
        yb = _dot(b_ref[...], wb_ref[...], NN)
        ya_ref[...] = ya
        yb_ref[...] = yb
        o_ref[...] = (_sigmoid(ga_ref[...]) * ya + _sigmoid(gb_ref[...]) * yb).astype(BF16)

    rows = _row_spec(tm, d)
    return pl.pallas_call(
        body, name=name, grid=(s // tm,),
        in_specs=[rows, rows, _full_spec((d, d)), _full_spec((d, d)), _row_spec(tm, d, 2), _row_spec(tm, d, 3)],
        out_specs=[rows, rows, rows],
        out_shape=[jax.ShapeDtypeStruct((s, d), F32), jax.ShapeDtypeStruct((s, d), F32), jax.ShapeDtypeStruct((s, d), BF16)],
        compiler_params=_cparams(("parallel",)),
    )(a, b, w_a, w_b, zuvg, zuvg)


def _merge_bwd(dx1b, w_out, ya, yb, zuvg, *, name, tm=512):
    s, d = ya.shape

    def body(dx_ref, w_ref, ya_ref, yb_ref, ga_ref, gb_ref, dya_ref, dyb_ref, dga_ref, dgb_ref):
        dmv = _dot(dx_ref[...], w_ref[...], NT)
        sa = _sigmoid(ga_ref[...])
        sb = _sigmoid(gb_ref[...])
        dya_ref[...] = (dmv * sa).astype(BF16)
        dyb_ref[...] = (dmv * sb).astype(BF16)
        dga_ref[...] = (dmv * ya_ref[...] * (sa * (1.0 - sa))).astype(BF16)
        dgb_ref[...] = (dmv * yb_ref[...] * (sb * (1.0 - sb))).astype(BF16)

    rows = _row_spec(tm, d)
    o = jax.ShapeDtypeStruct((s, d), BF16)
    return pl.pallas_call(
        body, name=name, grid=(s // tm,),
        in_specs=[rows, _full_spec((d, d)), rows, rows, _row_spec(tm, d, 2), _row_spec(tm, d, 3)],
        out_specs=[rows] * 4, out_shape=[o, o, o, o], compiler_params=_cparams(("parallel",)),
    )(dx1b, w_out, ya, yb, zuvg, zuvg)


def _masked_ws(ws_ref, g):
    row = lax.broadcasted_iota(I32, (GBLOCK, GBLOCK), 0)
    col = lax.broadcasted_iota(I32, (GBLOCK, GBLOCK), 1)
    keep = (col // CHUNK) <= (row // CHUNK)
    return jnp.where(keep, ws_ref[g], 0.0), keep


def _layernorm_parts(zv):
    mu = jnp.mean(zv, axis=-1, keepdims=True)
    xc = zv - mu
    rs = lax.rsqrt(jnp.mean(xc * xc, axis=-1, keepdims=True) + EPS)
    return xc * rs, rs


def _gmlp_fwd(zuvg, ln_g, ln_b, w_s, bs_t, *, name):
    s, w = zuvg.shape[0], GROUPS * GDIM

    def body(zu_ref, zv_ref, lng_ref, lnb_ref, ws_ref, bs_ref, a_ref):
        zu = _gelu(zu_ref[...])
        zv = _gelu(zv_ref[...])
        xhat, _ = _layernorm_parts(zv)
        vln = (xhat * lng_ref[...] + lnb_ref[...]).astype(BF16)
        for g in range(GROUPS):
            wm, _ = _masked_ws(ws_ref, g)
            mixed = _dot(wm.astype(BF16), vln[:, g * GDIM:(g + 1) * GDIM], NN) + bs_ref[:, g:g + 1]
            a_ref[:, g * GDIM:(g + 1) * GDIM] = (zu[:, g * GDIM:(g + 1) * GDIM] * mixed).astype(BF16)

    return pl.pallas_call(
        body, name=name, grid=(s // GBLOCK,),
        in_specs=[_row_spec(GBLOCK, w, 0), _row_spec(GBLOCK, w, 1), _full_spec((1, w)), _full_spec((1, w)),
                  _full_spec((GROUPS, GBLOCK, GBLOCK)), _full_spec((GBLOCK, 128))],
        out_specs=_row_spec(GBLOCK, w),
        out_shape=jax.ShapeDtypeStruct((s, w), BF16), compiler_params=_cparams(("parallel",)),
    )(zuvg, zuvg, ln_g, ln_b, w_s, bs_t)


def _gmlp_bwd(da, zuvg, ln_g, ln_b, w_s, bs_t, carry=None, *, name):
    s, w = zuvg.shape[0], GROUPS * GDIM

    def body(da_ref, zu_ref, zv_ref, lng_ref, lnb_ref, ws_ref, bs_ref,
             dzu_ref, dzv_ref, dws_ref, dbs_ref, dlng_ref, dlnb_ref, dvln_ref):
        i = pl.program_id(0)
        zu, dzu_g = _gelu_and_grad(zu_ref[...])
        zv, dzv_g = _gelu_and_grad(zv_ref[...])
        xhat, rs = _layernorm_parts(zv)
        vln = (xhat * lng_ref[...] + lnb_ref[...]).astype(BF16)
        dav = da_ref[...].astype(F32)
        lane = lax.broadcasted_iota(I32, (GBLOCK, 128), 1)
        dbs = jnp.zeros((GBLOCK, 128), F32)

        @pl.when(i == 0)
        def _():
            dws_ref[...] = jnp.zeros_like(dws_ref)

        for g in range(GROUPS):
            sl = slice(g * GDIM, (g + 1) * GDIM)
            wm, keep = _masked_ws(ws_ref, g)
            wmb = wm.astype(BF16)
            vg = vln[:, sl]
            mixed = _dot(wmb, vg, NN) + bs_ref[:, g:g + 1]
            dag = dav[:, sl]
            dzu_ref[:, sl] = (dag * mixed * dzu_g[:, sl]).astype(BF16)
            dmix = dag * zu[:, sl]
            dmb = dmix.astype(BF16)
            dws_ref[g] += jnp.where(keep, _dot(dmb, vg, NT), 0.0)
            dbs = jnp.where(lane == g, jnp.sum(dmix, axis=1, keepdims=True), dbs)
            dvln_ref[:, sl] = _dot(wmb, dmb, TN)
        dvln = dvln_ref[...]
        dxhat = dvln * lng_ref[...]
        dzv = rs * (dxhat - jnp.mean(dxhat, axis=-1, keepdims=True)
                    - xhat * jnp.mean(dxhat * xhat, axis=-1, keepdims=True))
        dzv_ref[...] = (dzv * dzv_g).astype(BF16)
        dlng = jnp.sum(dvln * xhat, axis=0, keepdims=True)
        dlnb = jnp.sum(dvln, axis=0, keepdims=True)

        @pl.when(i == 0)
        def _():
            dbs_ref[...] = dbs
            dlng_ref[...] = dlng
            dlnb_ref[...] = dlnb

        @pl.when(i > 0)
        def _():
            dbs_ref[...] += dbs
            dlng_ref[...] += dlng
            dlnb_ref[...] += dlnb

    return _carry_call(
        body, carry, name=name, grid=(s // GBLOCK,),
        in_specs=[_row_spec(GBLOCK, w), _row_spec(GBLOCK, w, 0), _row_spec(GBLOCK, w, 1), _full_spec((1, w)),
                  _full_spec((1, w)), _full_spec((GROUPS, GBLOCK, GBLOCK)), _full_spec((GBLOCK, 128))],
        out_specs=[_row_spec(GBLOCK, w), _row_spec(GBLOCK, w), _full_spec((GROUPS, GBLOCK, GBLOCK)),
                   _full_spec((GBLOCK, 128)), _full_spec((1, w)), _full_spec((1, w))],
        out_shape=[jax.ShapeDtypeStruct((s, w), BF16), jax.ShapeDtypeStruct((s, w), BF16),
                   jax.ShapeDtypeStruct((GROUPS, GBLOCK, GBLOCK), F32), jax.ShapeDtypeStruct((GBLOCK, 128), F32),
                   jax.ShapeDtypeStruct((1, w), F32), jax.ShapeDtypeStruct((1, w), F32)],
        scratch_shapes=[pltpu.VMEM((GBLOCK, w), F32)], args=[da, zuvg, zuvg, ln_g, ln_b, w_s, bs_t])


def _shift_down(u, k):
    row = lax.broadcasted_iota(I32, u.shape, 0)
    return jnp.where(row >= k, pltpu.roll(u, k, 0), 0.0)


def _shift_up(u, k):
    s = u.shape[0]
    row = lax.broadcasted_iota(I32, u.shape, 0)
    return jnp.where(row < s - k, pltpu.roll(u, s - k, 0), 0.0)


def _conv(u, w_ref, b_ref):
    return b_ref[...] + w_ref[0:1, :] * _shift_down(u, 2) + w_ref[1:2, :] * _shift_down(u, 1) + w_ref[2:3, :] * u


def _conv_specs(s, f, tc):
    nc = f // tc
    half = lambda rows: [pl.BlockSpec((rows, tc), lambda j: (0, j)), pl.BlockSpec((rows, tc), lambda j: (0, nc + j))]
    return half(s), half(3), half(1)


def _up_convglu(h2, wt_up, conv_w, conv_b, *, name, tc=256):
    s, d = h2.shape
    f = wt_up.shape[0] // 2
    nc = f // tc
    _, w_specs, b_specs = _conv_specs(s, f, tc)

    def body(h_ref, ta_ref, tg_ref, wa_ref, wg_ref, ba_ref, bg_ref, ua_ref, ug_ref, o_ref):
        ua = _dot(h_ref[...], ta_ref[...], NT)
        ua_ref[...] = ua
        ga = _gelu(_conv(ua, wa_ref, ba_ref))
        ug = _dot(h_ref[...], tg_ref[...], NT)
        ug_ref[...] = ug
        o_ref[...] = (ga * _conv(ug, wg_ref, bg_ref)).astype(BF16)

    col = pl.BlockSpec((s, tc), lambda j: (0, j))
    return pl.pallas_call(
        body, name=name, grid=(nc,),
        in_specs=[_full_spec((s, d)), pl.BlockSpec((tc, d), lambda j: (j, 0)), pl.BlockSpec((tc, d), lambda j: (nc + j, 0))]
        + w_specs + b_specs,
        out_specs=[col, col, col],
        out_shape=[jax.ShapeDtypeStruct((s, f), F32), jax.ShapeDtypeStruct((s, f), F32), jax.ShapeDtypeStruct((s, f), BF16)],
        compiler_params=_cparams(("parallel",)),
    )(h2, wt_up, wt_up, conv_w, conv_w, conv_b, conv_b)


def _convglu_bwd(dact, up_a, up_g, conv_w, conv_b, *, name, tc=256):
    s, f = up_a.shape
    _, w_specs, b_specs = _conv_specs(s, f, tc)
    up_specs = [pl.BlockSpec((s, tc), lambda j: (0, j))] * 2

    def half(dc, taps, w_ref, du_ref, dw_ref, db_ref):
        db_ref[...] = jnp.sum(dc, axis=0, keepdims=True)
        for k in range(3):
            dw_ref[k:k + 1, :] = jnp.sum(dc * taps[k], axis=0, keepdims=True)
        du = w_ref[2:3, :] * dc + w_ref[1:2, :] * _shift_up(dc, 1) + w_ref[0:1, :] * _shift_up(dc, 2)
        du_ref[...] = du.astype(BF16)

    def body(d_ref, ua_ref, ug_ref, wa_ref, wg_ref, ba_ref, bg_ref,
             dua_ref, dug_ref, dwa_ref, dwg_ref, dba_ref, dbg_ref):
        taps_a = (_shift_down(ua_ref[...], 2), _shift_down(ua_ref[...], 1), ua_ref[...])
        taps_g = (_shift_down(ug_ref[...], 2), _shift_down(ug_ref[...], 1), ug_ref[...])
        conv = lambda taps, w_ref, b_ref: b_ref[...] + w_ref[0:1, :] * taps[0] + w_ref[1:2, :] * taps[1] + w_ref[2:3, :] * taps[2]
        ca = conv(taps_a, wa_ref, ba_ref)
        cg = conv(taps_g, wg_ref, bg_ref)
        ga, dga = _gelu_and_grad(ca)
        dv = d_ref[...].astype(F32)
        half(dv * cg * dga, taps_a, wa_ref, dua_ref, dwa_ref, dba_ref)
        half(dv * ga, taps_g, wg_ref, dug_ref, dwg_ref, dbg_ref)

    col, w3, b1 = up_specs[0], w_specs[0], b_specs[0]
    return pl.pallas_call(
        body, name=name, grid=(f // tc,),
        in_specs=[col] + up_specs + w_specs + b_specs, out_specs=[col, col, w3, w3, b1, b1],
        out_shape=[jax.ShapeDtypeStruct((s, f), BF16), jax.ShapeDtypeStruct((s, f), BF16),
                   jax.ShapeDtypeStruct((3, f), F32), jax.ShapeDtypeStruct((3, f), F32),
                   jax.ShapeDtypeStruct((1, f), F32), jax.ShapeDtypeStruct((1, f), F32)],
        compiler_params=_cparams(("parallel",)),
    )(dact, up_a, up_g, conv_w, conv_w, conv_b, conv_b)


def _tri_dot(tri, x):
    b0 = x.astype(BF16)
    r1 = x - b0.astype(F32)
    b1 = r1.astype(BF16)
    b2 = (r1 - b1.astype(F32)).astype(BF16)
    return _dot(tri, b0, NN) + _dot(tri, b1, NN) + _dot(tri, b2, NN)


def _log_sigmoid(x):
    return jnp.minimum(x, 0.0) - jnp.log(1.0 + jnp.exp(-jnp.abs(x)))


def _expand_heads(col16, rows):
    src = lax.broadcasted_iota(I32, (128, HEADS * HEAD_DIM), 0)
    dst = lax.broadcasted_iota(I32, (128, HEADS * HEAD_DIM), 1) // HEAD_DIM
    spread = (src == dst).astype(BF16)
    p0, p1, p2 = _bf16_pieces(col16)
    return (_dot(p0.astype(BF16), spread, NN) + _dot(p1.astype(BF16), spread, NN)) + _dot(p2.astype(BF16), spread, NN)


def _forget_cumsum(f_logit, b_f, *, name):
    s = f_logit.shape[0]
    nb = s // 128

    def body(f_ref, b_ref, cqe_ref):
        row = lax.broadcasted_iota(I32, (128, 128), 0)
        col = lax.broadcasted_iota(I32, (128, 128), 1)
        tri = (col <= row).astype(BF16)

        def step(n, carry):
            r0 = pl.multiple_of(n * 128, 128)
            lf = _log_sigmoid(f_ref[pl.ds(r0, 128), :] + b_ref[...])
            cum = _tri_dot(tri, lf) + carry
            cqe_ref[pl.ds(r0, 128), :] = _expand_heads(cum, 128)
            return cum[127:128, :]

        lax.fori_loop(0, nb, step, jnp.zeros((1, 128), F32))

    return pl.pallas_call(
        body, name=name, grid=(1,),
        in_specs=[_full_spec((s, 128)), _full_spec((1, 128))],
        out_specs=_full_spec((s, HEADS * HEAD_DIM)),
        out_shape=jax.ShapeDtypeStruct((s, HEADS * HEAD_DIM), F32),
        compiler_params=_cparams(("arbitrary",)),
    )(f_logit, b_f)


def _forget_bwd(dcq16, sum_q16, f_logit, b_f, *, name):
    s = f_logit.shape[0]
    nb = s // 128

    def body(a_ref, k_ref, f_ref, b_ref, df_ref, db_ref):
        row = lax.broadcasted_iota(I32, (128, 128), 0)
        col = lax.broadcasted_iota(I32, (128, 128), 1)
        tri_rev = (col >= row).astype(BF16)

        def step(m, carry):
            suffix, dbsum = carry
            n = nb - 1 - m
            r0 = pl.multiple_of(n * 128, 128)
            dcum = a_ref[pl.ds(r0, 128), :] - k_ref[pl.ds(r0, 128), :]
            dlf = _tri_dot(tri_rev, dcum) + suffix
            df = dlf * _sigmoid(-(f_ref[pl.ds(r0, 128), :] + b_ref[...]))
            df_ref[pl.ds(r0, 128), :] = df.astype(BF16)
            return dlf[0:1, :], dbsum + jnp.sum(df, axis=0, keepdims=True)

        _, dbsum = lax.fori_loop(0, nb, step, (jnp.zeros((1, 128), F32), jnp.zeros((1, 128), F32)))
        db_ref[...] = dbsum

    return pl.pallas_call(
        body, name=name, grid=(1,),
        in_specs=[_full_spec((s, 128))] * 3 + [_full_spec((1, 128))],
        out_specs=[_full_spec((s, 128)), _full_spec((1, 128))],
        out_shape=[jax.ShapeDtypeStruct((s, 128), BF16), jax.ShapeDtypeStruct((1, 128), F32)],
        compiler_params=_cparams(("arbitrary",)),
    )(dcq16, sum_q16, f_logit, b_f)


ATT_T = 256


def _head_lanes(rows):
    return lax.broadcasted_iota(I32, (rows, 128), 1) < HEAD_DIM


def _bf16_pieces(c):
    p0 = c.astype(BF16).astype(F32)
    r = c - p0
    p1 = r.astype(BF16).astype(F32)
    p2 = (r - p1).astype(BF16).astype(F32)
    return p0, p1, p2


def _col_reduce(x, op):
    rows = x.shape[0]
    while rows > 8:
        rows //= 2
        x = op(x[:rows], x[rows:])
    return jnp.max(x, axis=0, keepdims=True) if op is jnp.maximum else jnp.sum(x, axis=0, keepdims=True)


def _attn_prep(qkv, cqe, carry=None, *, name):
    s = qkv.shape[0]
    npair = HEADS // 2

    def body(q_ref, k_ref, v_ref, c_ref, qa_ref, ka_ref, vt_ref):
        rows = 128
        lane = lax.broadcasted_iota(I32, (rows, 128), 1)

        def chunk(n, _):
            r0 = pl.multiple_of(n * rows, rows)
            sl = pl.ds(r0, rows)
            qv = q_ref[sl, :].astype(F32) * ATT_SCALE
            kv = k_ref[sl, :].astype(F32)
            p0, p1, p2 = _bf16_pieces(pltpu.roll(c_ref[sl, :], HEAD_DIM, 1))
            for e in range(2):
                mine = (lane < HEAD_DIM) if e == 0 else (lane >= HEAD_DIM)
                base = HEAD_DIM * (1 - e)
                ones_hi = jnp.where((lane >= base + 3) & (lane < base + 6), 1.0, 0.0)
                ones_lo = jnp.where((lane >= base) & (lane < base + 3), 1.0, 0.0)
                qa = jnp.where(mine, qv, jnp.where(lane == base, p0, jnp.where(lane == base + 1, p1,
                               jnp.where(lane == base + 2, p2, ones_hi))))
                ka = jnp.where(mine, kv, jnp.where(lane == base + 3, -p0, jnp.where(lane == base + 4, -p1,
                               jnp.where(lane == base + 5, -p2, ones_lo))))
                qa_ref[e, sl, :] = qa.astype(BF16)
                ka_ref[e, sl, :] = ka.astype(BF16)
            vt_ref[0, :, sl] = v_ref[sl, :].astype(F32).T.astype(BF16)
            return 0

        lax.fori_loop(0, s // rows, chunk, 0)

    pair = pl.BlockSpec((2, s, 128), lambda hp: (hp, 0, 0))
    return _carry_call(
        body, carry, name=name, grid=(npair,),
        in_specs=[pl.BlockSpec((s, 128), lambda hp: (0, hp)), pl.BlockSpec((s, 128), lambda hp: (0, npair + hp)),
                  pl.BlockSpec((s, 128), lambda hp: (0, 2 * npair + hp)), pl.BlockSpec((s, 128), lambda hp: (0, hp))],
        out_specs=[pair, pair, pl.BlockSpec((1, 128, s), lambda hp: (hp, 0, 0))],
        out_shape=[jax.ShapeDtypeStruct((HEADS, s, 128), BF16), jax.ShapeDtypeStruct((HEADS, s, 128), BF16),
                   jax.ShapeDtypeStruct((npair, 128, s), BF16)],
        scratch_shapes=[], args=[qkv, qkv, qkv, cqe])


def _attn_fwd(qa, ka, vt, carry=None, *, name):
    s = qa.shape[1]
    t = 2 * ATT_T
    nq = s // t
    npair = HEADS // 2

    def body(qa_ref, ka_ref, vt_ref, o_ref, lse_ref):
        i = pl.program_id(1)
        krow = lax.broadcasted_iota(I32, (t, t), 0)
        qcol = lax.broadcasted_iota(I32, (t, t), 1)
        sub = lax.broadcasted_iota(I32, (128, t), 0)
        row8 = lax.broadcasted_iota(I32, (8, t), 0)
        qbs = (qa_ref[0], qa_ref[1])
        tk = t

        def step(j, carry, diag):
            c0 = pl.multiple_of(j * tk, tk)
            vtb = vt_ref[0, :, pl.ds(c0, tk)]
            sts = [_dot(ka_ref[e, pl.ds(c0, tk), :], qbs[e], NT) for e in range(2)]
            if diag:
                sts = [jnp.where(krow <= qcol, st, NEG) for st in sts]
            pts, stats = [], []
            for e in range(2):
                m, l, _ = carry[e]
                m_new = jnp.maximum(m, _col_reduce(sts[e], jnp.maximum))
                alpha = jnp.exp(m - m_new)
                pt = jnp.exp(sts[e] - m_new)
                stats.append((m_new, alpha, alpha * l + _col_reduce(pt, jnp.add)))
                pts.append(pt.astype(BF16))
            pvs = [_dot(vtb, pts[e], NN) for e in range(2)]
            return tuple((stats[e][0], stats[e][2], stats[e][1] * carry[e][2] + pvs[e]) for e in range(2))

        init = (jnp.full((1, t), NEG, F32), jnp.zeros((1, t), F32), jnp.zeros((128, t), F32))
        carry = lax.fori_loop(0, i, functools.partial(step, diag=False), (init, init))
        (m0, l0, acc0), (m1, l1, acc1) = step(i, carry, True)
        o_pair = jnp.where(sub < HEAD_DIM, acc0 / l0, acc1 / l1)
        o_ref[...] = o_pair.T.astype(BF16)
        lse_ref[0] = jnp.where(row8 == 0, m0 + jnp.log(l0), jnp.where(row8 == 1, m1 + jnp.log(l1), 0.0))

    return _carry_call(
        body, carry, name=name, grid=(npair, nq),
        in_specs=[pl.BlockSpec((2, t, 128), lambda hp, i: (hp, i, 0)), pl.BlockSpec((2, s, 128), lambda hp, i: (hp, 0, 0)),
                  pl.BlockSpec((1, 128, s), lambda hp, i: (hp, 0, 0))],
        out_specs=[pl.BlockSpec((t, 128), lambda hp, i: (i, hp)), pl.BlockSpec((1, 8, t), lambda hp, i: (hp, 0, i))],
        out_shape=[jax.ShapeDtypeStruct((s, HEADS * HEAD_DIM), BF16), jax.ShapeDtypeStruct((npair, 8, s), F32)],
        scratch_shapes=[], args=[qa, ka, vt])


def _attn_delta(do, o, carry=None, *, name):
    s = do.shape[0]

    def body(do_ref, o_ref, d_ref):
        prod = do_ref[...].astype(F32) * o_ref[...].astype(F32)
        row = lax.broadcasted_iota(I32, (8, 128), 0)
        lane = lax.broadcasted_iota(I32, (8, 128), 1)
        sel = ((row == 0) & (lane < HEAD_DIM) | (row == 1) & (lane >= HEAD_DIM)).astype(BF16)
        p0, p1, p2 = _bf16_pieces(prod)
        d_ref[0] = (_dot(sel, p0.astype(BF16), NT) + _dot(sel, p1.astype(BF16), NT)) + _dot(sel, p2.astype(BF16), NT)

    pair = pl.BlockSpec((s, 128), lambda hp: (0, hp))
    (delta3,), carried = _carry_call(
        body, carry, name=name, grid=(HEADS // 2,), in_specs=[pair, pair],
        out_specs=[pl.BlockSpec((1, 8, s), lambda hp: (hp, 0, 0))],
        out_shape=[jax.ShapeDtypeStruct((HEADS // 2, 8, s), F32)], scratch_shapes=[], args=[do, o])
    return delta3, carried


def _attn_bwd(qa, ka, qkv, do, lse3, delta3, carry=None, *, name):
    s = qa.shape[1]
    t = 2 * ATT_T
    nb = s // t
    npair = HEADS // 2

    def body(qa_ref, ka_ref, v_ref, do_ref, lse_ref, delta_ref, dq_ref, dk_ref, dv_ref, aux_ref, dcq_ref, dqt):
        hp = pl.program_id(0)
        first = _head_lanes(t)
        lane = lax.broadcasted_iota(I32, (t, 128), 1)
        dqt[...] = jnp.zeros_like(dqt)

        @pl.when(hp == 0)
        def _():
            aux_ref[...] = jnp.zeros_like(aux_ref)

        krow = lax.broadcasted_iota(I32, (t, t), 0)
        qcol = lax.broadcasted_iota(I32, (t, t), 1)

        def key_block(j, _):
            c0 = pl.multiple_of(j * t, t)
            vb = v_ref[pl.ds(c0, t), :]
            kbs = (ka_ref[0, pl.ds(c0, t), :], ka_ref[1, pl.ds(c0, t), :])
            kbts = tuple(kb.astype(F32).T.astype(BF16) for kb in kbs)
            vhs = (jnp.where(first, vb, jnp.zeros_like(vb)), jnp.where(first, jnp.zeros_like(vb), vb))

            def query_block(i, carry, diag):
                r0 = pl.multiple_of(i * t, t)
                dob = do_ref[pl.ds(r0, t), :]
                sts = [_dot(kbs[e], qa_ref[e, pl.ds(r0, t), :], NT) for e in range(2)]
                dpts = [_dot(vhs[e], dob, NT) for e in range(2)]
                ptbs, dsbs = [], []
                for e in range(2):
                    st = jnp.where(krow <= qcol, sts[e], NEG) if diag else sts[e]
                    pt = jnp.exp(st - lse_ref[0, e:e + 1, pl.ds(r0, t)])
                    dsbs.append((pt * (dpts[e] - delta_ref[0, e:e + 1, pl.ds(r0, t)])).astype(BF16))
                    ptbs.append(pt.astype(BF16))
                out = []
                for e in range(2):
                    dk_a, dv_a = carry[e]
                    dv_a = dv_a + _dot(ptbs[e], dob, NN)
                    dk_a = dk_a + _dot(dsbs[e], qa_ref[e, pl.ds(r0, t), :], NN)
                    dqt[e, :, pl.ds(r0, t)] += _dot(kbts[e], dsbs[e], NN)
                    out.append((dk_a, dv_a))
                return tuple(out)

            zero = jnp.zeros((t, 128), F32)
            carry = query_block(j, ((zero, zero), (zero, zero)), True)
            (dk0, dv0), (dk1, dv1) = lax.fori_loop(j + 1, nb, functools.partial(query_block, diag=False), carry)
            dk_ref[pl.ds(c0, t), :] = jnp.where(first, dk0, dk1).astype(BF16)
            dv_ref[pl.ds(c0, t), :] = jnp.where(first, dv0, dv1).astype(BF16)
            sum_q = jnp.where(lane == 2 * hp, dk0[:, HEAD_DIM + 3:HEAD_DIM + 4],
                              jnp.where(lane == 2 * hp + 1, dk1[:, 3:4], aux_ref[pl.ds(c0, t), :]))
            aux_ref[pl.ds(c0, t), :] = sum_q
            return 0

        lax.fori_loop(0, nb, key_block, 0)
        sub = lax.broadcasted_iota(I32, (128, s), 0)
        row8 = lax.broadcasted_iota(I32, (8, s), 0)
        dq_ref[...] = (jnp.where(sub < HEAD_DIM, dqt[0], dqt[1]) * ATT_SCALE).T.astype(BF16)
        dcq_ref[0] = jnp.where(row8 == 0, dqt[0, HEAD_DIM:HEAD_DIM + 1, :], jnp.where(row8 == 1, dqt[1, 0:1, :], 0.0))

    def pair_cols(off):
        return pl.BlockSpec((s, 128), lambda hp: (0, off + hp))

    heads = pl.BlockSpec((2, s, 128), lambda hp: (hp, 0, 0))
    rows = pl.BlockSpec((1, 8, s), lambda hp: (hp, 0, 0))
    wide = jax.ShapeDtypeStruct((s, HEADS * HEAD_DIM), BF16)
    return _carry_call(
        body, carry, name=name, grid=(npair,),
        in_specs=[heads, heads, pair_cols(2 * npair), pair_cols(0), rows, rows],
        out_specs=[pair_cols(0), pair_cols(0), pair_cols(0), pl.BlockSpec((s, 128), lambda hp: (0, 0)), rows],
        out_shape=[wide, wide, wide, jax.ShapeDtypeStruct((s, 128), F32), jax.ShapeDtypeStruct((npair, 8, s), F32)],
        scratch_shapes=[pltpu.VMEM((2, 128, s), F32)], args=[qa, ka, qkv, do, lse3, delta3])


def _adam_math(w, g, m, v):
    m = ADAM_B1 * m + (1.0 - ADAM_B1) * g
    v = ADAM_B2 * v + (1.0 - ADAM_B2) * (g * g)
    m_hat = m / (1.0 - ADAM_B1 ** ADAM_STEP)
    v_hat = v / (1.0 - ADAM_B2 ** ADAM_STEP)
    delta = -ADAM_LR * (m_hat / (jnp.sqrt(v_hat) + ADAM_EPS) + ADAM_WD * w)
    return delta, m, v


def _sum_pairs(keep, recv, pos, *, name):
    _, r, c = recv.shape
    tr = _row_tile(r, 1024)

    def body(pos_ref, a_ref, b_ref, o32_ref, o16_ref):
        tot = a_ref[...].astype(F32) + b_ref[...].astype(F32)
        o16_ref[...] = tot.astype(BF16)

        @pl.when(pl.program_id(1) == 2 * pos_ref[0] + pos_ref[1])
        def _():
            o32_ref[...] = tot

    out = pl.BlockSpec((1, tr, c), lambda i, q, pos: (q, i, 0))
    grid_spec = pltpu.PrefetchScalarGridSpec(
        num_scalar_prefetch=1, grid=(r // tr, 4),
        in_specs=[pl.BlockSpec((1, tr, c), lambda i, q, pos: (2 * q + pos[2], i, 0)), out],
        out_specs=[pl.BlockSpec((1, tr, c), lambda i, q, pos: (0, i, 0)), out])
    return pl.pallas_call(
        body, name=name, grid_spec=grid_spec,
        out_shape=[jax.ShapeDtypeStruct((1, r, c), F32), jax.ShapeDtypeStruct((4, r, c), BF16)],
        compiler_params=_cparams(("arbitrary", "arbitrary")),
    )(pos, keep, recv)


def _adam_sharded(psum, recv, w, m, v, pos, *, name):
    r, c = w.shape
    rg = psum.shape[1]

    def body(pos_ref, p_ref, r_ref, w_ref, m_ref, v_ref, g_ref, d_ref, mo_ref, vo_ref):
        part = lambda ref, q: ref[q] if rg == r else ref[q, :r, :]
        g = part(p_ref, 0) + part(r_ref, 0).astype(F32) + part(r_ref, 1).astype(F32) + part(r_ref, 2).astype(F32)
        delta, mn, vn = _adam_math(w_ref[...], g, m_ref[...], v_ref[...])
        g_ref[...] = g
        d_ref[...] = delta
        mo_ref[...] = mn
        vo_ref[...] = vn

    if rg == r:
        tr = _row_tile(r, 320)
        grid = (r // tr,)
        row = pl.BlockSpec((tr, c), lambda i, pos: (i, 0))
        sums = lambda n: pl.BlockSpec((n, tr, c), lambda i, pos: (0, i, 0))
    else:
        tc = 256
        grid = (c // tc,)
        row = pl.BlockSpec((r, tc), lambda i, pos: (0, i))
        sums = lambda n: pl.BlockSpec((n, rg, tc), lambda i, pos: (0, 0, i))
    grid_spec = pltpu.PrefetchScalarGridSpec(
        num_scalar_prefetch=1, grid=grid, in_specs=[sums(1), sums(3), row, row, row], out_specs=[row, row, row, row])
    o = jax.ShapeDtypeStruct((r, c), F32)
    return pl.pallas_call(
        body, name=name, grid_spec=grid_spec, out_shape=[o, o, o, o],
        compiler_params=_cparams(("parallel",)),
    )(pos, psum, recv, w, m, v)


def _adam_replicated(chip_sums, last, w, m, v, *, name):
    r = w.shape[0]

    def body(s_ref, l_ref, w_ref, m_ref, v_ref, g_ref, d_ref, mo_ref, vo_ref):
        g = (((s_ref[0] + s_ref[1]) + s_ref[2]) + s_ref[3]) + l_ref[...]
        delta, mn, vn = _adam_math(w_ref[...], g, m_ref[...], v_ref[...])
        g_ref[...] = g
        d_ref[...] = delta
        mo_ref[...] = mn
        vo_ref[...] = vn

    o = jax.ShapeDtypeStruct((r, 1024), F32)
    full = _full_spec((r, 1024))
    return pl.pallas_call(
        body, name=name, grid=(1,),
        in_specs=[_full_spec((4, r, 1024)), full, full, full, full], out_specs=[full] * 4, out_shape=[o] * 4,
        compiler_params=_cparams(("arbitrary",)),
    )(chip_sums, last, w, m, v)


ASM_OUT = 512
ASM_SRC = 304


def _w_in_row(r):
    return r if r < 2048 else (r + O_G - 2048 if r < 4096 else r - 2048)


def _assemble_wt_main(g, *, name):
    win_a = ASM_OUT + 16
    table = []
    for blk in range(MAIN_COLS // ASM_OUT):
        j, l0 = divmod(_w_in_row(blk * ASM_OUT), IN_SHARD)
        start = min(l0 // 16 * 16, IN_SHARD_PAD - win_a)
        n_a = min(ASM_OUT, IN_SHARD - l0)
        table.append((j, start, l0 - start, n_a, int(n_a < ASM_OUT)))
    steps = len(table)

    def body(tab_ref, g_ref, o_ref, buf_a, buf_b, sems):
        blk = pl.program_id(0)
        slot = blk % 2

        def copy_a(step, sl):
            first = pl.multiple_of(tab_ref[step, 1], 16)
            return pltpu.make_async_copy(g_ref.at[tab_ref[step, 0], pl.ds(first, win_a)], buf_a.at[sl], sems.at[0, sl])

        def copy_b(step, sl):
            return pltpu.make_async_copy(g_ref.at[tab_ref[step, 0] + 1, pl.ds(0, ASM_OUT)], buf_b.at[sl], sems.at[1, sl])

        def fetch(step, sl):
            copy_a(step, sl).start()

            @pl.when(tab_ref[step, 4] == 1)
            def _():
                copy_b(step, sl).start()

        @pl.when(blk == 0)
        def _():
            fetch(0, 0)

        @pl.when(blk + 1 < steps)
        def _():
            fetch(blk + 1, 1 - slot)

        off, n_a = tab_ref[blk, 2], tab_ref[blk, 3]
        copy_a(blk, slot).wait()
        r = lax.broadcasted_iota(I32, (ASM_OUT, win_a), 0)
        k = lax.broadcasted_iota(I32, (ASM_OUT, win_a), 1)
        sel_a = ((k == r + off) & (r < n_a)).astype(BF16)
        o_ref[...] = _dot(sel_a, buf_a[slot], NN).astype(BF16)

        @pl.when(tab_ref[blk, 4] == 1)
        def _():
            copy_b(blk, slot).wait()
            rb = lax.broadcasted_iota(I32, (ASM_OUT, ASM_OUT), 0)
            kb = lax.broadcasted_iota(I32, (ASM_OUT, ASM_OUT), 1)
            sel_b = ((kb == rb - n_a) & (rb >= n_a)).astype(BF16)
            o_ref[...] += _dot(sel_b, buf_b[slot], NN).astype(BF16)

    grid_spec = pltpu.PrefetchScalarGridSpec(
        num_scalar_prefetch=1, grid=(steps,), in_specs=[ANY],
        out_specs=pl.BlockSpec((ASM_OUT, D_MODEL), lambda blk, tab: (blk, 0)),
        scratch_shapes=[pltpu.VMEM((2, win_a, D_MODEL), BF16), pltpu.VMEM((2, ASM_OUT, D_MODEL), BF16),
                        pltpu.SemaphoreType.DMA((2, 2))])
    return pl.pallas_call(
        body, name=name, grid_spec=grid_spec, out_shape=jax.ShapeDtypeStruct((MAIN_COLS, D_MODEL), BF16),
        compiler_params=_cparams(("arbitrary",)),
    )(jnp.asarray(table, I32), g)


def _pair_sum_small(mine, theirs, *, name):
    def body(a_ref, b_ref, o_ref):
        o_ref[...] = a_ref[...] + b_ref[...]

    full = _full_spec(mine.shape)
    return pl.pallas_call(
        body, name=name, grid=(1,), in_specs=[full, full], out_specs=full,
        out_shape=jax.ShapeDtypeStruct(mine.shape, F32), compiler_params=_cparams(("arbitrary",)),
    )(mine, theirs)


ANY = pl.BlockSpec(memory_space=pl.ANY)
OTHER_CHIPS = ((1, 0), (0, 1), (1, 1))


class _Carry:
    def __init__(self, inputs, out_shapes, scratch, start, wait, aliases=None, middle=None):
        self.inputs, self.out_shapes, self.scratch = list(inputs), list(out_shapes), list(scratch)
        self.start, self.wait, self.aliases, self.middle = start, wait, dict(aliases or {}), middle


def _carry_join(*carries):
    n_in = [len(c.inputs) for c in carries]
    n_out = [len(c.out_shapes) for c in carries]
    n_scr = [len(c.scratch) for c in carries]

    def split(refs, counts):
        out, k = [], 0
        for n in counts:
            out.append(refs[k:k + n])
            k += n
        return out

    def start(ins, outs, scr):
        for c, i, o, s in zip(carries, split(ins, n_in), split(outs, n_out), split(scr, n_scr)):
            c.start(i, o, s)

    def wait(ins, outs, scr):
        for c, i, o, s in zip(carries, split(ins, n_in), split(outs, n_out), split(scr, n_scr)):
            c.wait(i, o, s)

    def middle(ins, outs, scr):
        for c, i, o, s in zip(carries, split(ins, n_in), split(outs, n_out), split(scr, n_scr)):
            if c.middle is not None:
                c.middle(i, o, s)

    aliases = {}
    for k, c in enumerate(carries):
        aliases.update({sum(n_in[:k]) + i: sum(n_out[:k]) + o for i, o in c.aliases.items()})
    joined = _Carry(sum((c.inputs for c in carries), []), sum((c.out_shapes for c in carries), []),
                    sum((c.scratch for c in carries), []), start, wait, aliases,
                    middle if any(c.middle is not None for c in carries) else None)
    joined.counts = n_out
    joined.split = lambda results: split(results, n_out)
    return joined


def _carried(body, carry, n_in, n_out, grid):
    if carry is None:
        return body
    ci, co, cs = len(carry.inputs), len(carry.out_shapes), len(carry.scratch)

    def wrapped(*refs):
        ins, cins = refs[:n_in], refs[n_in:n_in + ci]
        outs, couts = refs[n_in + ci:n_in + ci + n_out], refs[n_in + ci + n_out:n_in + ci + n_out + co]
        rest = refs[n_in + ci + n_out + co:]
        scratch, cscr = rest[:len(rest) - cs], rest[len(rest) - cs:]
        first, last, step, steps = None, None, 0, 1
        for axis, size in enumerate(grid):
            f, l = pl.program_id(axis) == 0, pl.program_id(axis) == size - 1
            first = f if first is None else first & f
            last = l if last is None else last & l
            step, steps = step * size + pl.program_id(axis), steps * size

        @pl.when(first)
        def _():
            carry.start(cins, couts, cscr)

        if carry.middle is not None:
            @pl.when(step == steps // 2)
            def _():
                carry.middle(cins, couts, cscr)

        body(*ins, *outs, *scratch)

        @pl.when(last)
        def _():
            carry.wait(cins, couts, cscr)

    return wrapped


def _carry_call(body, carry, *, name, grid, in_specs, out_specs, out_shape, scratch_shapes, args, vmem=True,
                own_aliases=None):
    n_in, n_out = len(in_specs), len(out_specs)
    extra_in = [ANY] * len(carry.inputs) if carry else []
    extra_out = [ANY] * len(carry.out_shapes) if carry else []
    aliases = dict(own_aliases or {})
    if carry:
        aliases.update({n_in + i: n_out + o for i, o in carry.aliases.items()})
    out = pl.pallas_call(
        _carried(body, carry, n_in, n_out, grid), name=name, grid=grid,
        in_specs=list(in_specs) + extra_in, out_specs=list(out_specs) + extra_out,
        out_shape=list(out_shape) + (carry.out_shapes if carry else []),
        scratch_shapes=list(scratch_shapes) + (carry.scratch if carry else []),
        input_output_aliases=aliases,
        compiler_params=_cparams(("arbitrary",) * len(grid)) if vmem else None,
    )(*args, *(carry.inputs if carry else []))
    return list(out[:n_out]), list(out[n_out:])


def _run_carry(carry, *, name):
    return _carry_call(lambda: None, carry, name=name, grid=(1,), in_specs=[], out_specs=[], out_shape=[],
                       scratch_shapes=[], args=[], vmem=False)[1]


def _sems(n):
    return [pltpu.SemaphoreType.DMA((n,)), pltpu.SemaphoreType.DMA((n,))]


def _carry_gather1(shards):
    n = len(shards)
    per = 7

    def plan(x_refs, out_refs, scr):
        send_sems, recv_sems, local_sems = scr
        x, y, c = lax.axis_index("x"), lax.axis_index("y"), lax.axis_index("c")
        me, sibling = (x, y, c), (x, y, 1 - c)
        near_x, near_y, across = (1 - x, y, c), (x, 1 - y, c), (1 - x, 1 - y, c)

        def rows(ref, t, half):
            r = shards[t].shape[0]
            h = r if r < 32 else -(-(r // 2) // 16) * 16
            if half is None or h == r:
                return ref
            return ref.at[pl.ds(0, h)] if half == 0 else ref.at[pl.ds(h, r - h)]

        def copy(t, k, block, half, to, from_input=False):
            px, py, pc = block
            slab = rows(out_refs[t].at[4 * px + 2 * py + pc], t, half)
            return pltpu.make_async_remote_copy(
                src_ref=rows(x_refs[t], t, half) if from_input else slab, dst_ref=slab,
                send_sem=send_sems.at[per * t + k], recv_sem=recv_sems.at[per * t + k], device_id=to,
                device_id_type=MESH)

        two = [shards[t].shape[0] >= 32 for t in range(n)]
        local = lambda t: pltpu.make_async_copy(x_refs[t], out_refs[t].at[4 * x + 2 * y + c], local_sems.at[t])
        first = lambda t: ([(0, me, None, sibling), (1, me, 0, near_x)]
                           + ([(2, me, 1, near_y), (3, me, 1, near_x)] if two[t] else []) + [(4, me, 0, near_y)])
        passed = lambda t: [(5, near_x, 0, near_y)] + ([(6, near_y, 1, near_x)] if two[t] else [])
        early = lambda t: [(1, near_x, 0, me)] + ([(2, near_y, 1, me)] if two[t] else [])
        late = lambda t: ([(0, sibling, None, me), (4, near_y, 0, me), (5, across, 0, me)]
                          + ([(3, near_x, 1, me), (6, across, 1, me)] if two[t] else []))
        return copy, local, first, passed, early, late

    def start(x_refs, out_refs, scr):
        copy, local, first, _, _, _ = plan(x_refs, out_refs, scr)
        for urgent in (True, False):
            for t in range(n):
                if not urgent:
                    local(t).start()
                for k, block, half, to in first(t):
                    if (k in (1, 2)) == urgent:
                        copy(t, k, block, half, to, from_input=True).start()

    def middle(x_refs, out_refs, scr):
        copy, _, _, passed, early, _ = plan(x_refs, out_refs, scr)
        for t in range(n):
            for (k, block, half, to), fwd in zip(early(t), passed(t)):
                copy(t, k, block, half, to).wait_recv()
                copy(t, *fwd).start()

    def wait(x_refs, out_refs, scr):
        copy, local, first, passed, _, late = plan(x_refs, out_refs, scr)
        for t in range(n):
            for k, block, half, to in late(t):
                copy(t, k, block, half, to).wait_recv()
        for t in range(n):
            for k, block, half, to in first(t):
                copy(t, k, block, half, to, from_input=True).wait_send()
            for k, block, half, to in passed(t):
                copy(t, k, block, half, to).wait_send()
            local(t).wait()

    return _Carry(shards, [jax.ShapeDtypeStruct((N_DEV,) + a.shape, a.dtype) for a in shards],
                  _sems(per * n) + [pltpu.SemaphoreType.DMA((n,))], start, wait, middle=middle)


def _carry_gather2(gathered):
    n = len(gathered)

    def copies(in_refs, g_refs, scr, with_arrivals):
        send_sems, recv_sems = scr
        x, y, c = lax.axis_index("x"), lax.axis_index("y"), lax.axis_index("c")
        sends, arrivals = [], []
        for t in range(n):
            for j, (fx, fy) in enumerate(OTHER_CHIPS):
                px, py = x ^ fx, y ^ fy
                sems = dict(send_sem=send_sems.at[3 * t + j], recv_sem=recv_sems.at[3 * t + j],
                            device_id=(x, y, 1 - c), device_id_type=MESH)
                mine, theirs = 4 * px + 2 * py + c, 4 * px + 2 * py + (1 - c)
                sends.append(pltpu.make_async_remote_copy(src_ref=in_refs[t].at[mine], dst_ref=g_refs[t].at[mine], **sems))
                if with_arrivals:
                    arrivals.append(pltpu.make_async_remote_copy(
                        src_ref=in_refs[t].at[mine], dst_ref=g_refs[t].at[theirs], **sems))
        return sends, arrivals

    def start(in_refs, g_refs, scr):
        for cp in copies(in_refs, g_refs, scr, False)[0]:
            cp.start()

    def wait(in_refs, g_refs, scr):
        sends, arrivals = copies(in_refs, g_refs, scr, True)
        for cp in arrivals:
            cp.wait_recv()
        for cp in sends:
            cp.wait_send()

    return _Carry(gathered, [jax.ShapeDtypeStruct(a.shape, a.dtype) for a in gathered], _sems(3 * n), start, wait,
                  aliases={t: t for t in range(n)})


def _allreduce_rows(x, *, name):
    def body(x_ref, o_ref, sib_ref, mine_ref, tab_ref, send_sems, recv_sems):
        x, y, c = lax.axis_index("x"), lax.axis_index("y"), lax.axis_index("c")
        swap = pltpu.make_async_remote_copy(src_ref=x_ref, dst_ref=sib_ref, send_sem=send_sems.at[0],
                                            recv_sem=recv_sems.at[0], device_id=(x, y, 1 - c), device_id_type=MESH)
        swap.start()
        swap.wait()
        mine_ref[...] = x_ref[...] + sib_ref[...]
        tab_ref[pl.ds(2 * x + y, 1)] = mine_ref[...][None]

        def copy(k, slot):
            fx, fy = OTHER_CHIPS[k]
            return pltpu.make_async_remote_copy(
                src_ref=mine_ref, dst_ref=tab_ref.at[slot], send_sem=send_sems.at[1 + k], recv_sem=recv_sems.at[1 + k],
                device_id=(x ^ fx, y ^ fy, c), device_id_type=MESH)

        for k in range(3):
            copy(k, 2 * x + y).start()
        for k, (fx, fy) in enumerate(OTHER_CHIPS):
            copy(k, 2 * (x ^ fx) + (y ^ fy)).wait()
        o_ref[...] = ((tab_ref[0] + tab_ref[1]) + tab_ref[2]) + tab_ref[3]

    vmem = pl.BlockSpec(memory_space=pltpu.VMEM)
    return pl.pallas_call(
        body, name=name, out_shape=jax.ShapeDtypeStruct(x.shape, F32), in_specs=[vmem], out_specs=vmem,
        scratch_shapes=[pltpu.VMEM(x.shape, F32), pltpu.VMEM(x.shape, F32), pltpu.VMEM((4,) + x.shape, F32)] + _sems(4),
    )(x)


def _allgather(shards, *, name):
    n = len(shards)
    per = 10

    def body(*refs):
        x_refs, out_refs = refs[:n], refs[n:2 * n]
        send_sems, recv_sems, local_sems = refs[2 * n:]
        x, y, c = lax.axis_index("x"), lax.axis_index("y"), lax.axis_index("c")
        me, sibling = (x, y, c), (x, y, 1 - c)
        near_x, near_y, across = (1 - x, y), (x, 1 - y), (1 - x, 1 - y)

        def rows(ref, t, half):
            r = shards[t].shape[0]
            h = -(-(r // 2) // 16) * 16
            if half is None:
                return ref
            return ref.at[pl.ds(0, h)] if half == 0 else ref.at[pl.ds(h, r - h)]

        def copy(t, k, block, half, to, from_input=False):
            px, py, pc = block
            slab = rows(out_refs[t].at[4 * px + 2 * py + pc], t, half)
            return pltpu.make_async_remote_copy(
                src_ref=rows(x_refs[t], t, half) if from_input else slab, dst_ref=slab,
                send_sem=send_sems.at[per * t + k], recv_sem=recv_sems.at[per * t + k], device_id=to,
                device_id_type=MESH)

        mine = [pltpu.make_async_copy(x_refs[t], out_refs[t].at[4 * x + 2 * y + c], local_sems.at[t]) for t in range(n)]
        for cp in mine:
            cp.start()
        sent = []

        def send(cp):
            cp.start()
            sent.append(cp)

        for t in range(n):
            send(copy(t, 0, me, None, sibling, from_input=True))
            send(copy(t, 1, me, 0, (*near_x, c), from_input=True))
            send(copy(t, 2, me, 1, (*near_y, c), from_input=True))
            send(copy(t, 3, me, 1, (*near_x, c), from_input=True))
            send(copy(t, 4, me, 0, (*near_y, c), from_input=True))
        for t in range(n):
            copy(t, 1, (*near_x, c), 0, me).wait_recv()
            send(copy(t, 5, (*near_x, c), 0, (*near_y, c)))
            copy(t, 2, (*near_y, c), 1, me).wait_recv()
            send(copy(t, 6, (*near_y, c), 1, (*near_x, c)))
        for t in range(n):
            copy(t, 3, (*near_x, c), 1, me).wait_recv()
            send(copy(t, 7, (*near_x, c), None, sibling))
            copy(t, 4, (*near_y, c), 0, me).wait_recv()
            send(copy(t, 8, (*near_y, c), None, sibling))
            copy(t, 5, (*across, c), 0, me).wait_recv()
            copy(t, 6, (*across, c), 1, me).wait_recv()
            send(copy(t, 9, (*across, c), None, sibling))
        for t in range(n):
            copy(t, 0, sibling, None, me).wait_recv()
            for k, chip in ((7, near_x), (8, near_y), (9, across)):
                copy(t, k, (*chip, 1 - c), None, me).wait_recv()
        for cp in sent:
            cp.wait_send()
        for cp in mine:
            cp.wait()

    return pl.pallas_call(
        body, name=name, out_shape=[jax.ShapeDtypeStruct((N_DEV,) + a.shape, a.dtype) for a in shards],
        in_specs=[ANY] * n, out_specs=[ANY] * n,
        scratch_shapes=[pltpu.SemaphoreType.DMA((per * n,)), pltpu.SemaphoreType.DMA((per * n,)),
                        pltpu.SemaphoreType.DMA((n,))],
    )(*shards)


def _carry_sibling(slabs, small=None):
    n = len(slabs)
    extra = [] if small is None else [small]

    def copies(in_refs, out_refs, scr):
        send_sems, recv_sems = scr
        x, y, c = lax.axis_index("x"), lax.axis_index("y"), lax.axis_index("c")
        sibling = (x, y, 1 - c)
        out = []
        for t in range(n):
            for q in range(4):
                out.append(pltpu.make_async_remote_copy(
                    src_ref=in_refs[t].at[2 * q + (1 - c)], dst_ref=out_refs[t].at[q],
                    send_sem=send_sems.at[4 * t + q], recv_sem=recv_sems.at[4 * t + q],
                    device_id=sibling, device_id_type=MESH))
        if extra:
            out.append(pltpu.make_async_remote_copy(
                src_ref=in_refs[n], dst_ref=out_refs[n], send_sem=send_sems.at[4 * n], recv_sem=recv_sems.at[4 * n],
                device_id=sibling, device_id_type=MESH))
        return out

    def start(*refs):
        for cp in copies(*refs):
            cp.start()

    def wait(*refs):
        for cp in copies(*refs):
            cp.wait()

    return _Carry(list(slabs) + extra,
                  [jax.ShapeDtypeStruct((4,) + a.shape[1:], a.dtype) for a in slabs]
                  + [jax.ShapeDtypeStruct(a.shape, a.dtype) for a in extra], _sems(4 * n + 1), start, wait)


def _carry_chips(psums, small_sum=None):
    n = len(psums)
    table = small_sum is not None

    def copies(in_refs, out_refs, scr, arrivals):
        send_sems, recv_sems = scr[0], scr[1]
        x, y, c = lax.axis_index("x"), lax.axis_index("y"), lax.axis_index("c")
        out = []
        for k, (fx, fy) in enumerate(OTHER_CHIPS):
            px, py = x ^ fx, y ^ fy
            for t in range(n):
                out.append(pltpu.make_async_remote_copy(
                    src_ref=in_refs[t].at[2 * px + py], dst_ref=out_refs[t].at[k],
                    send_sem=send_sems.at[3 * t + k], recv_sem=recv_sems.at[3 * t + k],
                    device_id=(px, py, c), device_id_type=MESH))
            if table:
                slot = 2 * px + py if arrivals else 2 * x + y
                out.append(pltpu.make_async_remote_copy(
                    src_ref=in_refs[n], dst_ref=out_refs[n].at[slot], send_sem=send_sems.at[3 * n + k],
                    recv_sem=recv_sems.at[3 * n + k], device_id=(px, py, c), device_id_type=MESH))
        return out

    def own(in_refs, out_refs, scr):
        x, y = lax.axis_index("x"), lax.axis_index("y")
        return pltpu.make_async_copy(in_refs[n], out_refs[n].at[2 * x + y], scr[2])

    def start(in_refs, out_refs, scr):
        if table:
            own(in_refs, out_refs, scr).start()
        for cp in copies(in_refs, out_refs, scr, False):
            cp.start()

    def wait(in_refs, out_refs, scr):
        for cp in copies(in_refs, out_refs, scr, True):
            cp.wait()
        if table:
            own(in_refs, out_refs, scr).wait()

    out_shapes = [jax.ShapeDtypeStruct((3,) + a.shape[1:], a.dtype) for a in psums]
    if table:
        out_shapes.append(jax.ShapeDtypeStruct((4,) + small_sum.shape, F32))
    return _Carry(list(psums) + ([small_sum] if table else []), out_shapes,
                  _sems(3 * n + 3) + ([pltpu.SemaphoreType.DMA] if table else []), start, wait)


def _to_comm(name, kind, block, dtype=BF16):
    a = block[0]
    if kind == "cols":
        a = a.T
        if name == "w_in" and dtype == BF16:
            a = jnp.pad(a, ((0, IN_SHARD_PAD - IN_SHARD), (0, 0)))
    return a if kind == "f32" else a.astype(dtype)


def _from_comm(name, kind, a):
    if kind == "cols":
        if name == "w_in" and a.shape[0] != IN_SHARD:
            a = a[:IN_SHARD]
        a = a.T
    return a[None]


def _assemble_weights(g):
    out = {}
    if "w_in" in g:
        out["wt_main"] = _assemble_wt_main(g["w_in"], name="assemble_w_in")
        j, l0 = divmod(O_F, IN_SHARD)
        out["wt_f"] = jnp.pad(g["w_in"][j, l0:l0 + HEADS], ((0, 128 - HEADS), (0, 0)))
    square = dict(w_branch_a="w_a", w_branch_b="w_b", w_out="w_out", w_ple_gate="w_pg")
    for long, short in square.items():
        if long in g:
            out[short] = g[long].reshape(D_MODEL, D_MODEL)
    if "w_up" in g:
        out["wt_up"] = g["w_up"].reshape(2 * D_FF, D_MODEL)
    if "conv_w" in g:
        out["conv_w"] = g["conv_w"].transpose(1, 0, 2).reshape(3, 2 * D_FF)
    if "w_down" in g:
        out["w_down"] = g["w_down"].reshape(D_FF, D_MODEL)
    if "w_ple" in g:
        out["wt_ple"] = g["w_ple"].reshape(D_MODEL, PLE_DIM)
    return out


def _grad_slabs(gr):
    out = {}
    if "wt_main" in gr:
        gm, gf = gr["wt_main"], gr["wt_f"]
        segments = ((0, 2048, gm, 0), (2048, O_F, gm, 2048), (O_F, O_G, gf, -O_F), (O_G, IN_COLS, gm, 2048 - O_G))
        slabs = []
        for j in range(N_DEV):
            lo, hi = j * IN_SHARD, (j + 1) * IN_SHARD
            pieces = [src[max(lo, a) + shift:min(hi, b) + shift] for a, b, src, shift in segments if max(lo, a) < min(hi, b)]
            pieces.append(jnp.zeros((IN_SHARD_PAD - IN_SHARD, D_MODEL), gm.dtype))
            slabs.append(jnp.concatenate(pieces, axis=0))
        out["w_in"] = jnp.stack(slabs)
    rows = dict(w_a="w_branch_a", w_b="w_branch_b", w_out="w_out", wt_up="w_up", w_down="w_down", w_pg="w_ple_gate")
    for short, long in rows.items():
        if short in gr:
            out[long] = gr[short].reshape(N_DEV, -1, D_MODEL)
    if "conv_w" in gr:
        out["conv_w"] = gr["conv_w"].reshape(3, N_DEV, -1).transpose(1, 0, 2)
    if "wt_ple" in gr:
        out["w_ple"] = gr["wt_ple"].reshape(N_DEV, -1, PLE_DIM)
    return {k: v.astype(BF16) for k, v in out.items()}


def _rows(a, rows):
    flat = a.reshape(-1)
    return jnp.pad(flat, (0, rows * 1024 - flat.shape[0])).reshape(rows, 1024)


def _pack_small(parts):
    return jnp.concatenate([_rows(parts[n].astype(F32), r) for n, r in SMALL], axis=0)


def _small(packed, name, shape):
    off, r = SMALL_OFF[name]
    n = math.prod(shape)
    return packed[off:off + r].reshape(-1)[:n].reshape(shape)


class _Exchanges:
    W_S_ROWS = SMALL_OFF["gmlp_w_s"]

    def __init__(self, later, shards, pos):
        self.later, self.shards, self.pos = later, dict(zip(later, shards)), pos
        self.level1, self.slabs, self.from_sib, self.sums32, self.reduced, self.tables = {}, {}, {}, {}, {}, {}

    def gather1(self, names):
        carry = _carry_gather1([self.shards[n] for n in names])
        carry.names = names
        return carry

    def gather1_done(self, carry, results):
        self.level1.update(zip(carry.names, results))

    def gather2(self):
        return _carry_gather2([self.level1[n] for n in self.later])

    def weights(self, full):
        return _assemble_weights(dict(zip(self.later, full)))

    def sibling(self, grads):
        slabs = _grad_slabs(grads)
        self.slabs.update(slabs)
        carry = _carry_sibling(list(slabs.values()))
        carry.names = list(slabs)
        return carry

    def sibling_done(self, carry, results):
        self.from_sib.update(zip(carry.names, results))

    def chips(self, names, table=None):
        sums = {n: _sum_pairs(self.slabs[n], self.from_sib[n], self.pos, name="sum_sibling_" + n) for n in names}
        self.sums32.update({n: s32 for n, (s32, _) in sums.items()})
        carry = _carry_chips([s16 for _, s16 in sums.values()], None if table is None else self.table_part(table))
        carry.names, carry.table = list(names), table
        return carry

    def chips_done(self, carry, results):
        if carry.table is not None:
            *results, self.tables[carry.table] = results
        self.reduced.update({n: (self.sums32[n], r) for n, r in zip(carry.names, results)})

    def sibling_small(self, small_g):
        self.small_g = small_g
        return _carry_sibling([], small_g)

    def sibling_small_done(self, small_sib):
        self.small_chip = _pair_sum_small(self.small_g, small_sib, name="sum_sibling_small")

    def table_part(self, which):
        off, rows = self.W_S_ROWS
        if which == "w_s":
            return self.small_chip[off:off + rows]
        return jnp.concatenate([self.small_chip[:off], self.small_chip[off + rows:]], axis=0)

    def table(self):
        off = self.W_S_ROWS[0]
        rest = self.tables["rest"]
        return jnp.concatenate([rest[:, :off], self.tables["w_s"], rest[:, off:]], axis=1)


def _local_step(x, p, target, w, sm, ex=None):
    s = x.shape[0]
    mm = _matmul
    wt_main = w["wt_main"]
    conv_b = sm["conv_b"]
    bs_t = jnp.pad(sm["gmlp_b_s"].T, ((0, 0), (0, 128 - GROUPS)))
    b_f = jnp.pad(sm["b_f"], ((0, 0), (0, 128 - HEADS)))
    big = dict(tm=1024, tn=1024, tk=1024)
    whole_s = dict(tn=1024, tk=s)

    h = _rmsnorm_fwd(x, sm["norm_mix_g"], name="norm_mix")
    tall = dict(tm=s, tn=512, tk=1024)
    qkv_args = dict(mode="nt", out_dtype=BF16, name="in_qkv", n=3072, b_off=8, **tall)
    f_logit = mm(h, w["wt_f"], mode="nt", out_dtype=F32, name="in_f", tm=1024, tk=1024)
    cqe = _forget_cumsum(f_logit, b_f, name="forget_cumsum")
    uvg = dict(mode="nt", out_dtype=F32, name="in_uvg", n=4096, **tall)
    if ex is None:
        qkv = mm(h, wt_main, **qkv_args)
        (qa, ka, vt), _ = _attn_prep(qkv, cqe, name="attn_prep")
        (b, lse3), _ = _attn_fwd(qa, ka, vt, name="attn_fwd")
        zuvg = mm(h, wt_main, **uvg)
    else:
        groups = (["w_branch_a"], ["w_branch_b"], [n for n in ex.later if n not in ("w_branch_a", "w_branch_b")])
        carries = [ex.gather1(names) for names in groups]
        qkv, got0 = mm(h, wt_main, carry=carries[0], **qkv_args)
        (qa, ka, vt), got1 = _attn_prep(qkv, cqe, carries[1], name="attn_prep")
        (b, lse3), got2 = _attn_fwd(qa, ka, vt, carries[2], name="attn_fwd")
        for carry, got in zip(carries, (got0, got1, got2)):
            ex.gather1_done(carry, got)
        zuvg, full = mm(h, wt_main, carry=ex.gather2(), **uvg)
        w = {**w, **ex.weights(full)}
    a = _gmlp_fwd(zuvg, sm["gmlp_ln_g"], sm["gmlp_ln_b"], sm["gmlp_w_s"], bs_t, name="gmlp_fwd")
    wt_up, conv_w = w["wt_up"], w["conv_w"]
    ya, yb, merged = _branches_merge(a, b, w["w_a"], w["w_b"], zuvg, name="branches_merge")
    x1, h2 = mm(merged, w["w_out"], mode="nn", out_dtype=F32, name="out_proj", add=x, norm_g=sm["norm_ffn_g"], **big)
    up_a, up_g, act = _up_convglu(h2, wt_up, conv_w, conv_b, name="up_convglu")
    x2, h3 = mm(act, w["w_down"], mode="nn", out_dtype=F32, name="down", tm=1024, tn=1024, tk=1408, add=x1,
                norm_g=sm["norm_ple_g"])

    loss, dx3, dple, dgp, d_norm_final = _ple_loss(p, w["wt_ple"], h3, w["w_pg"], x2, target, sm["norm_final_g"],
                                                   name="ple_loss")
    g_wt_ple = mm(dple, p, mode="tn", out_dtype=BF16, name="d_w_ple", tm=512, tn=256, tk=s)
    g_w_pg = mm(h3, dgp, mode="tn", out_dtype=BF16, name="d_w_pg", tm=512, **whole_s)
    (dx2, dx2b, d_norm_ple), _ = _matmul_rmsnorm_bwd([dgp], w["w_pg"], dx3, x2, sm["norm_ple_g"], mode="nt", tk=1024,
                                                     name="d_h3_norm_ple_bwd")
    g_w_down = mm(act, dx2b, mode="tn", out_dtype=BF16, name="d_w_down", tm=1408, **whole_s)
    dact_args = dict(mode="nt", out_dtype=BF16, name="d_act", tm=s, tn=256, tk=1024)
    if ex is None:
        dact = mm(dx2b, w["w_down"], **dact_args)
    else:
        early = ex.sibling(dict(w_pg=g_w_pg, wt_ple=g_wt_ple))
        dact, got = mm(dx2b, w["w_down"], carry=early, **dact_args)
        ex.sibling_done(early, got)
    dup_a, dup_g, dcw_a, dcw_g, dcb_a, dcb_g = _convglu_bwd(dact, up_a, up_g, conv_w, conv_b, name="convglu_bwd")
    g_wt_up = mm(dup_a, h2, mode="tn", out_dtype=BF16, name="d_w_up_a", tm=1408, out_rows=2 * D_FF, **whole_s)
    g_wt_up = mm(dup_g, h2, mode="tn", out_dtype=BF16, name="d_w_up_g", tm=1408, out_rows=2 * D_FF,
                 o_off=D_FF // 1408, into=g_wt_up, **whole_s)
    (dx1, dx1b, d_norm_ffn), _ = _matmul_rmsnorm_bwd([dup_a, dup_g], wt_up, dx2, x1, sm["norm_ffn_g"], mode="nn",
                                                     tk=1408, name="d_h2_norm_ffn_bwd", resident=True)
    g_w_out = mm(merged, dx1b, mode="tn", out_dtype=BF16, name="d_w_out", tm=512, **whole_s)
    dya, dyb, dga, dgb = _merge_bwd(dx1b, w["w_out"], ya, yb, zuvg, name="merge_bwd")
    g_w_a = mm(a, dya, mode="tn", out_dtype=BF16, name="d_w_a", tm=512, **whole_s)
    g_w_b = mm(b, dyb, mode="tn", out_dtype=BF16, name="d_w_b", tm=512, **whole_s)
    da = mm(dya, w["w_a"], mode="nt", out_dtype=BF16, name="d_a", **big)
    db = mm(dyb, w["w_b"], mode="nt", out_dtype=BF16, name="d_b", **big)
    grads = dict(w_a=g_w_a, w_b=g_w_b, w_out=g_w_out, wt_up=g_wt_up, conv_w=jnp.concatenate([dcw_a, dcw_g], axis=1),
                 w_down=g_w_down, wt_ple=g_wt_ple, w_pg=g_w_pg)
    gmlp_args = (da, zuvg, sm["gmlp_ln_g"], sm["gmlp_ln_b"], sm["gmlp_w_s"], bs_t)
    if ex is None:
        (dzu, dzv, d_w_s, d_bs_t, d_ln_g, d_ln_b), _ = _gmlp_bwd(*gmlp_args, name="gmlp_bwd")
    else:
        rest = ex.sibling({k: v for k, v in grads.items() if k not in ("w_pg", "wt_ple")})
        early_chips = ex.chips(early.names)
        both = _carry_join(rest, early_chips)
        (dzu, dzv, d_w_s, d_bs_t, d_ln_g, d_ln_b), got = _gmlp_bwd(*gmlp_args, both, name="gmlp_bwd")
        got_rest, got_early = both.split(got)
        ex.sibling_done(rest, got_rest)
        ex.chips_done(early_chips, got_early)
    small = dict(norm_mix_g=jnp.zeros((1, D_MODEL), F32), b_f=jnp.zeros((1, HEADS), F32), gmlp_ln_g=d_ln_g,
                 gmlp_ln_b=d_ln_b, gmlp_w_s=d_w_s, gmlp_b_s=d_bs_t[:, :GROUPS].T, norm_ffn_g=d_norm_ffn,
                 conv_b=jnp.concatenate([dcb_a, dcb_g], axis=1), norm_ple_g=d_norm_ple, norm_final_g=d_norm_final)
    if ex is None:
        delta3, _ = _attn_delta(db, b, name="attn_delta")
        (dq, dk, dv, aux, dcq3), _ = _attn_bwd(qa, ka, qkv, db, lse3, delta3, name="attn_bwd")
    else:
        delta3, (small_sib,) = _attn_delta(db, b, ex.sibling_small(_pack_small(small)), name="attn_delta")
        ex.sibling_small_done(small_sib)
        main_chips = ex.chips(rest.names, table="rest")
        (dq, dk, dv, aux, dcq3), got = _attn_bwd(qa, ka, qkv, db, lse3, delta3, main_chips, name="attn_bwd")
        ex.chips_done(main_chips, got)
    dcq16 = jnp.pad(dcq3[:, :2, :].reshape(HEADS, s).T, ((0, 0), (0, 128 - HEADS)))
    dzf, d_b_f = _forget_bwd(dcq16, aux, f_logit, b_f, name="forget_bwd")
    dz_parts = [dzu, dzv, dga, dgb, dq, dk, dv]
    w_s_chips = None if ex is None else ex.chips([], table="w_s")
    g_wt_main, got = _grad_w_parts(dz_parts, h, name="d_w_main", tm=512, carry=w_s_chips)
    if ex is not None:
        ex.chips_done(w_s_chips, got)
    g_wt_f = mm(dzf, h, mode="tn", out_dtype=BF16, name="d_w_f", **whole_s)
    grads = dict(grads, wt_main=g_wt_main, wt_f=g_wt_f)
    w_in_chips = None
    if ex is not None:
        w_in_sib = ex.sibling(dict(wt_main=g_wt_main, wt_f=g_wt_f))
        ex.sibling_done(w_in_sib, _run_carry(w_in_sib, name="exchange_sibling_w_in"))
        w_in_chips = ex.chips(w_in_sib.names)
    (dx0, _, d_norm_mix), got = _matmul_rmsnorm_bwd(dz_parts, wt_main, dx1, x, sm["norm_mix_g"], mode="nn", tk=1024,
                                                    extra=(dzf, w["wt_f"]), name="d_h_norm_mix_bwd", carry=w_in_chips,
                                                    lead=True)
    if ex is not None:
        ex.chips_done(w_in_chips, got)
    return loss, dx0, grads, dict(small, norm_mix_g=d_norm_mix, b_f=d_b_f[:, :HEADS])


def kernel(x, p, norm_mix_g, w_in, b_f, gmlp_ln_g, gmlp_ln_b, gmlp_w_s, gmlp_b_s, w_branch_a, w_branch_b, w_out, norm_ffn_g, w_up, conv_w, conv_b, w_down, norm_ple_g, w_ple, w_ple_gate, norm_final_g, loss_target, m_norm_mix_g, m_w_in, m_b_f, m_gmlp_ln_g, m_gmlp_ln_b, m_gmlp_w_s, m_gmlp_b_s, m_w_branch_a, m_w_branch_b, m_w_out, m_norm_ffn_g, m_w_up, m_conv_w, m_conv_b, m_w_down, m_norm_ple_g, m_w_ple, m_w_ple_gate, m_norm_final_g, v_norm_mix_g, v_w_in, v_b_f, v_gmlp_ln_g, v_gmlp_ln_b, v_gmlp_w_s, v_gmlp_b_s, v_w_branch_a, v_w_branch_b, v_w_out, v_norm_ffn_g, v_w_up, v_conv_w, v_conv_b, v_w_down, v_norm_ple_g, v_w_ple, v_w_ple_gate, v_norm_final_g):
    given = dict(locals())
    weights = {n: given[n] for n in WEIGHT_ORDER}
    mom_m = {n: given["m_" + n] for n in WEIGHT_ORDER}
    mom_v = {n: given["v_" + n] for n in WEIGHT_ORDER}
    pos = jnp.stack([lax.axis_index("x"), lax.axis_index("y"), lax.axis_index("c")]).astype(I32)
    names = [n for n, _ in SHARDED]
    kinds = dict(SHARDED)

    later = [n for n in names if n != "w_in"]

    first = _allgather([_to_comm("w_in", kinds["w_in"], weights["w_in"])], name="allgather_w_in")
    ex = _Exchanges(later, [_to_comm(n, kinds[n], weights[n]) for n in later], pos)

    sm = dict(norm_mix_g=norm_mix_g, b_f=b_f, gmlp_ln_g=gmlp_ln_g, gmlp_ln_b=gmlp_ln_b, gmlp_w_s=gmlp_w_s[0],
              gmlp_b_s=gmlp_b_s[0], norm_ffn_g=norm_ffn_g, conv_b=conv_b, norm_ple_g=norm_ple_g,
              norm_final_g=norm_final_g.reshape(1, D_MODEL))
    loss_part, dx0, grads, small = _local_step(
        x[0], p[0, 0], loss_target[0], _assemble_weights({"w_in": first[0]}), sm, ex)

    b_f_and_loss = jnp.concatenate([small["b_f"].reshape(-1), loss_part[0, :1]])
    last = _allreduce_rows(jnp.concatenate([_rows(small["norm_mix_g"], 8), _rows(b_f_and_loss, 8)], axis=0),
                           name="allreduce_last")
    loss = last[8, HEADS]
    small_last = jnp.pad(last, ((0, SMALL_ROWS - 16), (0, 0)))

    grad, delta, new_m, new_v = {}, {}, {}, {}
    for n in names:
        s32, r = ex.reduced[n]
        outs = _adam_sharded(s32, r, *[_to_comm(n, kinds[n], src[n], F32) for src in (weights, mom_m, mom_v)], pos,
                             name="adam_" + n)
        grad[n], delta[n], new_m[n], new_v[n] = [_from_comm(n, kinds[n], o) for o in outs]
    replicated = [n for n, _ in SMALL]
    rep = lambda src: _pack_small({n: src[n] for n in replicated})
    packed = _adam_replicated(ex.table(), small_last, rep(weights), rep(mom_m), rep(mom_v), name="adam_replicated")
    for out, pk in zip((grad, delta, new_m, new_v), packed):
        for n in replicated:
            out[n] = _small(pk, n, weights[n].shape)

    return (loss, dx0, *[grad[n] for n in WEIGHT_ORDER], *[delta[n] for n in WEIGHT_ORDER],
            *[new_m[n] for n in WEIGHT_ORDER], *[new_v[n] for n in WEIGHT_ORDER])
```

```python
import functools
import math

import jax
import jax.numpy as jnp
from jax import lax
from jax.experimental import pallas as pl
from jax.experimental.pallas import tpu as pltpu

F32 = jnp.float32
BF16 = jnp.bfloat16
I32 = jnp.int32

D_MODEL = 1024
GROUPS = 8
GDIM = 128
GBLOCK = 128
CHUNK = 64
HEADS = 16
HEAD_DIM = 64
D_FF = 2816
PLE_DIM = 256
EPS = 1e-6
N_DEV = 8
ATT_SCALE = HEAD_DIM ** -0.5
NEG = -1e30

ADAM_LR = 0.001
ADAM_B1 = 0.9
ADAM_B2 = 0.999
ADAM_EPS = 1e-08
ADAM_WD = 0.01
ADAM_STEP = 10

V7X_VMEM_LIMIT = 48 * 1024 * 1024
MESH = pl.DeviceIdType.MESH

O_F = 2 * 1024 + 3 * 1024
O_G = O_F + HEADS
IN_COLS = O_G + 2 * D_MODEL
MAIN_COLS = IN_COLS - HEADS
IN_SHARD = IN_COLS // N_DEV
IN_SHARD_PAD = 912

SHARDED = (("w_in", "cols"), ("w_branch_a", "rows"), ("w_branch_b", "rows"), ("w_out", "rows"), ("w_up", "cols"),
           ("conv_w", "f32"), ("w_down", "rows"), ("w_ple", "cols"), ("w_ple_gate", "rows"))

SMALL = (("norm_mix_g", 8), ("b_f", 8), ("gmlp_ln_g", 8), ("gmlp_ln_b", 8), ("gmlp_w_s", 128), ("gmlp_b_s", 8),
         ("norm_ffn_g", 8), ("conv_b", 8), ("norm_ple_g", 8), ("norm_final_g", 8))
SMALL_OFF = {}
_o = 0
for _n, _r in SMALL:
    SMALL_OFF[_n] = (_o, _r)
    _o += _r
SMALL_ROWS = _o

WEIGHT_ORDER = ("norm_mix_g", "w_in", "b_f", "gmlp_ln_g", "gmlp_ln_b", "gmlp_w_s", "gmlp_b_s", "w_branch_a",
                "w_branch_b", "w_out", "norm_ffn_g", "w_up", "conv_w", "conv_b", "w_down", "norm_ple_g", "w_ple",
                "w_ple_gate", "norm_final_g")


def _cparams(sem):
    return pltpu.CompilerParams(dimension_semantics=sem, vmem_limit_bytes=V7X_VMEM_LIMIT)


def _gelu(x):
    c = math.sqrt(2.0 / math.pi)
    return 0.5 * x * (1.0 + jnp.tanh(c * (x + 0.044715 * x * x * x)))


def _gelu_and_grad(x):
    c = math.sqrt(2.0 / math.pi)
    t = jnp.tanh(c * (x + 0.044715 * x * x * x))
    g = 0.5 * x * (1.0 + t)
    dg = 0.5 * (1.0 + t) + 0.5 * x * (1.0 - t * t) * (c * (1.0 + 3.0 * 0.044715 * x * x))
    return g, dg


def _sigmoid(x):
    return 1.0 / (1.0 + jnp.exp(-x))


def _dot(a, b, dims):
    return lax.dot_general(a, b, (dims, ((), ())), preferred_element_type=F32)


NN = ((1,), (0,))
NT = ((1,), (1,))
TN = ((0,), (0,))


def _row_tile(rows, most):
    best = None
    for t in range(16, min(rows, most) + 1, 16):
        if rows % t == 0:
            best = t
    return best if best is not None else rows


def _matmul(a, b, *, mode, out_dtype, name, tm=512, tn=512, tk=512, add=None, n=None, b_off=0,
            out_rows=None, o_off=0, into=None, norm_g=None, carry=None):
    if mode == "tn":
        kdim, m = a.shape
    else:
        m, kdim = a.shape
    if n is None:
        n = b.shape[0] if mode == "nt" else b.shape[1]
    tm, tn, tk = min(tm, m), min(tn, n), min(tk, kdim)
    assert m % tm == 0 and n % tn == 0 and kdim % tk == 0, (name, m, n, kdim, tm, tn, tk)
    nk = kdim // tk
    dims = {"nn": NN, "nt": NT, "tn": TN}[mode]

    n_in = 2 + (add is not None) + (into is not None) + (norm_g is not None)
    assert norm_g is None or tn == n, "the RMS norm needs whole rows"

    def finish(r, refs):
        if add is not None:
            r = refs[2][...].astype(F32) + r
        refs[n_in][...] = r.astype(out_dtype)
        if norm_g is not None:
            rs = lax.rsqrt(jnp.mean(r * r, axis=-1, keepdims=True) + EPS)
            refs[n_in + 1][...] = ((r * rs) * refs[n_in - 1][...]).astype(BF16)

    def body(*refs):
        a_ref, b_ref = refs[:2]
        part = _dot(a_ref[...].astype(BF16), b_ref[...].astype(BF16), dims)
        if nk == 1:
            finish(part, refs)
            return
        acc_ref = refs[-1]
        k = pl.program_id(2)

        @pl.when(k == 0)
        def _():
            acc_ref[...] = part

        @pl.when((k > 0) & (k < nk - 1))
        def _():
            acc_ref[...] += part

        @pl.when(k == nk - 1)
        def _():
            finish(acc_ref[...] + part, refs)

    a_spec = pl.BlockSpec((tk, tm), lambda i, j, k: (k, i)) if mode == "tn" else pl.BlockSpec((tm, tk), lambda i, j, k: (i, k))
    if mode == "nt":
        b_spec = pl.BlockSpec((tn, tk), lambda i, j, k: (j + b_off, k))
    else:
        b_spec = pl.BlockSpec((tk, tn), lambda i, j, k: (k + b_off, j))
    o_spec = pl.BlockSpec((tm, tn), lambda i, j, k: (i + o_off, j))
    in_specs = [a_spec, b_spec] + ([pl.BlockSpec((tm, tn), lambda i, j, k: (i, j))] if add is not None else [])
    args = (a, b) + ((add,) if add is not None else ())
    aliases = {}
    if into is not None:
        aliases = {len(args): 0}
        in_specs.append(pl.BlockSpec(memory_space=pl.ANY))
        args += (into,)
    out_specs = [o_spec]
    out_shape = [jax.ShapeDtypeStruct((m if out_rows is None else out_rows, n), out_dtype)]
    if norm_g is not None:
        in_specs.append(pl.BlockSpec((1, n), lambda i, j, k: (0, 0)))
        args += (norm_g,)
        out_specs.append(pl.BlockSpec((tm, tn), lambda i, j, k: (i, j)))
        out_shape.append(jax.ShapeDtypeStruct((m, n), BF16))
    outs, carried = _carry_call(
        body, carry, name=name, grid=(m // tm, n // tn, nk), in_specs=in_specs, out_specs=out_specs,
        out_shape=out_shape, scratch_shapes=[pltpu.VMEM((tm, tn), F32)] if nk > 1 else [], args=args,
        own_aliases=aliases)
    out = outs[0] if norm_g is None else tuple(outs)
    return out if carry is None else (out, carried)


def _row_spec(tr, width, col_block=0):
    return pl.BlockSpec((tr, width), lambda i: (i, col_block))


def _full_spec(shape):
    return pl.BlockSpec(shape, lambda i: tuple(0 for _ in shape))


def _rmsnorm_fwd(x, g, *, name, tr=256):
    s, d = x.shape

    def body(x_ref, g_ref, o_ref):
        xv = x_ref[...]
        r = lax.rsqrt(jnp.mean(xv * xv, axis=-1, keepdims=True) + EPS)
        o_ref[...] = ((xv * r) * g_ref[...]).astype(BF16)

    return pl.pallas_call(
        body, name=name, grid=(s // tr,),
        in_specs=[_row_spec(tr, d), _full_spec((1, d))], out_specs=_row_spec(tr, d),
        out_shape=jax.ShapeDtypeStruct((s, d), BF16), compiler_params=_cparams(("parallel",)),
    )(x, g)


def _matmul_rmsnorm_bwd(a_parts, b, dres, x, g, *, mode, tk, name, extra=None, tm=512, carry=None, lead=False,
                        resident=False):
    s, d = x.shape
    n_row = s // tm
    spans, lo = [], 0
    for a in a_parts:
        spans.append((lo, lo + a.shape[1] // tk))
        lo = spans[-1][1]
    n_main, total = lo, lo + (extra is not None)
    n_parts = len(a_parts)

    def body(*refs):
        a_refs, b_ref = refs[:n_parts], refs[n_parts]
        k0 = n_parts + 1
        ax_ref, bx_ref = (refs[k0], refs[k0 + 1]) if extra is not None else (None, None)
        k0 += 2 * (extra is not None)
        dres_ref, x_ref, g_ref, dx_ref, dxb_ref, dg_ref, acc_all = refs[k0:k0 + 7]
        if resident:
            kk, i = pl.program_id(0), pl.program_id(1)
            acc_ref = acc_all.at[pl.ds(pl.multiple_of(i * tm, tm), tm)]
        else:
            i, kk = pl.program_id(0), pl.program_id(1)
            acc_ref = acc_all

        def accumulate(part, first):
            if first:
                @pl.when(kk == 0)
                def _():
                    acc_ref[...] = part

                @pl.when(kk > 0)
                def _():
                    acc_ref[...] += part
            else:
                acc_ref[...] += part

        for p, (a_ref, (lo_p, hi_p)) in enumerate(zip(a_refs, spans)):
            @pl.when((kk >= lo_p) & (kk < hi_p))
            def _(a_ref=a_ref, lo_p=lo_p):
                accumulate(_dot(a_ref[...].astype(BF16), b_ref[...].astype(BF16), NN if mode == "nn" else NT), lo_p == 0)

        if extra is not None:
            @pl.when(kk == n_main)
            def _():
                accumulate(_dot(ax_ref[...].astype(BF16), bx_ref[...].astype(BF16), NN), False)

        @pl.when(kk == total - 1)
        def _():
            dhv = acc_ref[...]
            xv = x_ref[...]
            r = lax.rsqrt(jnp.mean(xv * xv, axis=-1, keepdims=True) + EPS)
            xhat = xv * r
            dxhat = dhv * g_ref[...]
            dx = dres_ref[...] + r * (dxhat - xhat * jnp.mean(dxhat * xhat, axis=-1, keepdims=True))
            dx_ref[...] = dx
            dxb_ref[...] = dx.astype(BF16)
            dgp = jnp.sum(dhv * xhat, axis=0, keepdims=True)

            @pl.when(i == 0)
            def _():
                dg_ref[...] = dgp

            @pl.when(i > 0)
            def _():
                dg_ref[...] += dgp

    def spec(shape, index):
        return pl.BlockSpec(shape, (lambda kk, i: index(i, kk)) if resident else index)

    def row(i, kk, lo_p, hi_p):
        if not resident:
            return i
        return jnp.where(kk < lo_p, 0, jnp.where(kk >= hi_p, n_row - 1, i))

    a_specs = [spec((tm, tk), lambda i, kk, lo_p=lo_p, hi_p=hi_p: (row(i, kk, lo_p, hi_p),
                                                                    jnp.clip(kk - lo_p, 0, hi_p - lo_p - 1)))
               for lo_p, hi_p in spans]
    step = lambda kk: jnp.minimum(kk, n_main - 1)
    b_spec = (spec((tk, d), lambda i, kk: (step(kk), 0)) if mode == "nn"
              else spec((d, tk), lambda i, kk: (0, step(kk))))
    rows = spec((tm, d), lambda i, kk: (row(i, kk, total - 1, total), 0))
    one = spec((1, d), lambda i, kk: (0, 0))
    dx_spec, dx_shape = rows, jax.ShapeDtypeStruct((s, d), F32)
    if lead:
        dx_spec = spec((None, tm, d), lambda i, kk: (0, row(i, kk, total - 1, total), 0))
        dx_shape = jax.ShapeDtypeStruct((1, s, d), F32)
    x_specs, x_args = [], []
    if extra is not None:
        kx = extra[0].shape[1]
        x_specs = [spec((tm, kx), lambda i, kk: (row(i, kk, n_main, total), 0)), spec((kx, d), lambda i, kk: (0, 0))]
        x_args = list(extra)
    (dx, dxb, dg), carried = _carry_call(
        body, carry, name=name, grid=(total, n_row) if resident else (n_row, total),
        in_specs=a_specs + [b_spec] + x_specs + [rows, rows, one], out_specs=[dx_spec, rows, one],
        out_shape=[dx_shape, jax.ShapeDtypeStruct((s, d), BF16), jax.ShapeDtypeStruct((1, d), F32)],
        scratch_shapes=[pltpu.VMEM((s if resident else tm, d), F32)], args=list(a_parts) + [b] + x_args + [dres, x, g])
    return (dx, dxb, dg), carried


def _grad_w_parts(a_parts, b, *, name, tm=512, carry=None):
    s, width = a_parts[0].shape
    per, n = width // tm, b.shape[1]

    def body(*refs):
        a_refs, b_ref, o_ref = refs[:len(a_parts)], refs[len(a_parts)], refs[len(a_parts) + 1]
        i = pl.program_id(0)
        for p, a_ref in enumerate(a_refs):
            @pl.when(i // per == p)
            def _(a_ref=a_ref):
                o_ref[...] = _dot(a_ref[...].astype(BF16), b_ref[...].astype(BF16), TN).astype(BF16)

    a_specs = [pl.BlockSpec((s, tm), lambda i, p=p: (0, jnp.clip(i - p * per, 0, per - 1))) for p in range(len(a_parts))]
    (out,), carried = _carry_call(
        body, carry, name=name, grid=(len(a_parts) * per,),
        in_specs=a_specs + [pl.BlockSpec((s, n), lambda i: (0, 0))], out_specs=[pl.BlockSpec((tm, n), lambda i: (i, 0))],
        out_shape=[jax.ShapeDtypeStruct((len(a_parts) * width, n), BF16)], scratch_shapes=[], args=list(a_parts) + [b])
    return out, carried


def _ple_loss(p, wt_ple, h3, w_pg, x2, target, g, *, name, tm=256):
    s, d = x2.shape
    kp = p.shape[1]

    def body(p_ref, wp_ref, h_ref, wg_ref, x_ref, t_ref, g_ref, loss_ref, dx_ref, dple_ref, dgp_ref, dg_ref):
        i = pl.program_id(0)
        ple = _dot(p_ref[...].astype(BF16), wp_ref[...], NT)
        sg = _sigmoid(_dot(h_ref[...], wg_ref[...], NN))
        xv = x_ref[...] + ple * sg
        r = lax.rsqrt(jnp.mean(xv * xv, axis=-1, keepdims=True) + EPS)
        xhat = xv * r
        diff = xhat * g_ref[...] - t_ref[...]
        lp = jnp.zeros((1, 128), F32) + (0.5 / d) * jnp.sum(diff * diff)
        dy = diff * (1.0 / d)
        dxhat = dy * g_ref[...]
        dx = r * (dxhat - xhat * jnp.mean(dxhat * xhat, axis=-1, keepdims=True))
        dx_ref[...] = dx
        dple_ref[...] = (dx * sg).astype(BF16)
        dgp_ref[...] = (dx * ple * (sg * (1.0 - sg))).astype(BF16)
        dgp = jnp.sum(dy * xhat, axis=0, keepdims=True)

        @pl.when(i == 0)
        def _():
            dg_ref[...] = dgp
            loss_ref[...] = lp

        @pl.when(i > 0)
        def _():
            dg_ref[...] += dgp
            loss_ref[...] += lp

    rows = _row_spec(tm, d)
    return pl.pallas_call(
        body, name=name, grid=(s // tm,),
        in_specs=[_row_spec(tm, kp), _full_spec((d, kp)), rows, _full_spec((d, d)), rows, rows, _full_spec((1, d))],
        out_specs=[_full_spec((1, 128)), rows, rows, rows, _full_spec((1, d))],
        out_shape=[jax.ShapeDtypeStruct((1, 128), F32), jax.ShapeDtypeStruct((s, d), F32),
                   jax.ShapeDtypeStruct((s, d), BF16), jax.ShapeDtypeStruct((s, d), BF16),
                   jax.ShapeDtypeStruct((1, d), F32)],
        compiler_params=_cparams(("arbitrary",)),
    )(p, wt_ple, h3, w_pg, x2, target, g)


def _branches_merge(a, b, w_a, w_b, zuvg, *, name, tm=512):
    s, d = a.shape

    def body(a_ref, b_ref, wa_ref, wb_ref, ga_ref, gb_ref, ya_ref, yb_ref, o_ref):
        ya = _dot(a_ref[...], wa_ref[...], NN)
        yb = _dot(b_ref[...], wb_ref[...], NN)
        ya_ref[...] = ya
        yb_ref[...] = yb
        o_ref[...] = (_sigmoid(ga_ref[...]) * ya + _sigmoid(gb_ref[...]) * yb).astype(BF16)

    rows = _row_spec(tm, d)
    return pl.pallas_call(
        body, name=name, grid=(s // tm,),
        in_specs=[rows, rows, _full_spec((d, d)), _full_spec((d, d)), _row_spec(tm, d, 2), _row_spec(tm, d, 3)],
        out_specs=[rows, rows, rows],
        out_shape=[jax.ShapeDtypeStruct((s, d), F32), jax.ShapeDtypeStruct((s, d), F32), jax.ShapeDtypeStruct((s, d), BF16)],
        compiler_params=_cparams(("parallel",)),
    )(a, b, w_a, w_b, zuvg, zuvg)


def _merge_bwd(dx1b, w_out, ya, yb, zuvg, *, name, tm=512):
    s, d = ya.shape

    def body(dx_ref, w_ref, ya_ref, yb_ref, ga_ref, gb_ref, dya_ref, dyb_ref, dga_ref, dgb_ref):
        dmv = _dot(dx_ref[...], w_ref[...], NT)
        sa = _sigmoid(ga_ref[...])
        sb = _sigmoid(gb_ref[...])
        dya_ref[...] = (dmv * sa).astype(BF16)
        dyb_ref[...] = (dmv * sb).astype(BF16)
        dga_ref[...] = (dmv * ya_ref[...] * (sa * (1.0 - sa))).astype(BF16)
        dgb_ref[...] = (dmv * yb_ref[...] * (sb * (1.0 - sb))).astype(BF16)

    rows = _row_spec(tm, d)
    o = jax.ShapeDtypeStruct((s, d), BF16)
    return pl.pallas_call(
        body, name=name, grid=(s // tm,),
        in_specs=[rows, _full_spec((d, d)), rows, rows, _row_spec(tm, d, 2), _row_spec(tm, d, 3)],
        out_specs=[rows] * 4, out_shape=[o, o, o, o], compiler_params=_cparams(("parallel",)),
    )(dx1b, w_out, ya, yb, zuvg, zuvg)


def _masked_ws(ws_ref, g):
    row = lax.broadcasted_iota(I32, (GBLOCK, GBLOCK), 0)
    col = lax.broadcasted_iota(I32, (GBLOCK, GBLOCK), 1)
    keep = (col // CHUNK) <= (row // CHUNK)
    return jnp.where(keep, ws_ref[g], 0.0), keep


def _layernorm_parts(zv):
    mu = jnp.mean(zv, axis=-1, keepdims=True)
    xc = zv - mu
    rs = lax.rsqrt(jnp.mean(xc * xc, axis=-1, keepdims=True) + EPS)
    return xc * rs, rs


def _gmlp_fwd(zuvg, ln_g, ln_b, w_s, bs_t, *, name):
    s, w = zuvg.shape[0], GROUPS * GDIM

    def body(zu_ref, zv_ref, lng_ref, lnb_ref, ws_ref, bs_ref, a_ref):
        zu = _gelu(zu_ref[...])
        zv = _gelu(zv_ref[...])
        xhat, _ = _layernorm_parts(zv)
        vln = (xhat * lng_ref[...] + lnb_ref[...]).astype(BF16)
        for g in range(GROUPS):
            wm, _ = _masked_ws(ws_ref, g)
            mixed = _dot(wm.astype(BF16), vln[:, g * GDIM:(g + 1) * GDIM], NN) + bs_ref[:, g:g + 1]
            a_ref[:, g * GDIM:(g + 1) * GDIM] = (zu[:, g * GDIM:(g + 1) * GDIM] * mixed).astype(BF16)

    return pl.pallas_call(
        body, name=name, grid=(s // GBLOCK,),
        in_specs=[_row_spec(GBLOCK, w, 0), _row_spec(GBLOCK, w, 1), _full_spec((1, w)), _full_spec((1, w)),
                  _full_spec((GROUPS, GBLOCK, GBLOCK)), _full_spec((GBLOCK, 128))],
        out_specs=_row_spec(GBLOCK, w),
        out_shape=jax.ShapeDtypeStruct((s, w), BF16), compiler_params=_cparams(("parallel",)),
    )(zuvg, zuvg, ln_g, ln_b, w_s, bs_t)


def _gmlp_bwd(da, zuvg, ln_g, ln_b, w_s, bs_t, carry=None, *, name):
    s, w = zuvg.shape[0], GROUPS * GDIM

    def body(da_ref, zu_ref, zv_ref, lng_ref, lnb_ref, ws_ref, bs_ref,
             dzu_ref, dzv_ref, dws_ref, dbs_ref, dlng_ref, dlnb_ref, dvln_ref):
        i = pl.program_id(0)
        zu, dzu_g = _gelu_and_grad(zu_ref[...])
        zv, dzv_g = _gelu_and_grad(zv_ref[...])
        xhat, rs = _layernorm_parts(zv)
        vln = (xhat * lng_ref[...] + lnb_ref[...]).astype(BF16)
        dav = da_ref[...].astype(F32)
        lane = lax.broadcasted_iota(I32, (GBLOCK, 128), 1)
        dbs = jnp.zeros((GBLOCK, 128), F32)

        @pl.when(i == 0)
        def _():
            dws_ref[...] = jnp.zeros_like(dws_ref)

        for g in range(GROUPS):
            sl = slice(g * GDIM, (g + 1) * GDIM)
            wm, keep = _masked_ws(ws_ref, g)
            wmb = wm.astype(BF16)
            vg = vln[:, sl]
            mixed = _dot(wmb, vg, NN) + bs_ref[:, g:g + 1]
            dag = dav[:, sl]
            dzu_ref[:, sl] = (dag * mixed * dzu_g[:, sl]).astype(BF16)
            dmix = dag * zu[:, sl]
            dmb = dmix.astype(BF16)
            dws_ref[g] += jnp.where(keep, _dot(dmb, vg, NT), 0.0)
            dbs = jnp.where(lane == g, jnp.sum(dmix, axis=1, keepdims=True), dbs)
            dvln_ref[:, sl] = _dot(wmb, dmb, TN)
        dvln = dvln_ref[...]
        dxhat = dvln * lng_ref[...]
        dzv = rs * (dxhat - jnp.mean(dxhat, axis=-1, keepdims=True)
                    - xhat * jnp.mean(dxhat * xhat, axis=-1, keepdims=True))
        dzv_ref[...] = (dzv * dzv_g).astype(BF16)
        dlng = jnp.sum(dvln * xhat, axis=0, keepdims=True)
        dlnb = jnp.sum(dvln, axis=0, keepdims=True)

        @pl.when(i == 0)
        def _():
            dbs_ref[...] = dbs
            dlng_ref[...] = dlng
            dlnb_ref[...] = dlnb

        @pl.when(i > 0)
        def _():
            dbs_ref[...] += dbs
            dlng_ref[...] += dlng
            dlnb_ref[...] += dlnb

    return _carry_call(
        body, carry, name=name, grid=(s // GBLOCK,),
        in_specs=[_row_spec(GBLOCK, w), _row_spec(GBLOCK, w, 0), _row_spec(GBLOCK, w, 1), _full_spec((1, w)),
                  _full_spec((1, w)), _full_spec((GROUPS, GBLOCK, GBLOCK)), _full_spec((GBLOCK, 128))],
        out_specs=[_row_spec(GBLOCK, w), _row_spec(GBLOCK, w), _full_spec((GROUPS, GBLOCK, GBLOCK)),
                   _full_spec((GBLOCK, 128)), _full_spec((1, w)), _full_spec((1, w))],
        out_shape=[jax.ShapeDtypeStruct((s, w), BF16), jax.ShapeDtypeStruct((s, w), BF16),
                   jax.ShapeDtypeStruct((GROUPS, GBLOCK, GBLOCK), F32), jax.ShapeDtypeStruct((GBLOCK, 128), F32),
                   jax.ShapeDtypeStruct((1, w), F32), jax.ShapeDtypeStruct((1, w), F32)],
        scratch_shapes=[pltpu.VMEM((GBLOCK, w), F32)], args=[da, zuvg, zuvg, ln_g, ln_b, w_s, bs_t])


def _shift_down(u, k):
    row = lax.broadcasted_iota(I32, u.shape, 0)
    return jnp.where(row >= k, pltpu.roll(u, k, 0), 0.0)


def _shift_up(u, k):
    s = u.shape[0]
    row = lax.broadcasted_iota(I32, u.shape, 0)
    return jnp.where(row < s - k, pltpu.roll(u, s - k, 0), 0.0)


def _conv(u, w_ref, b_ref):
    return b_ref[...] + w_ref[0:1, :] * _shift_down(u, 2) + w_ref[1:2, :] * _shift_down(u, 1) + w_ref[2:3, :] * u


def _conv_specs(s, f, tc):
    nc = f // tc
    half = lambda rows: [pl.BlockSpec((rows, tc), lambda j: (0, j)), pl.BlockSpec((rows, tc), lambda j: (0, nc + j))]
    return half(s), half(3), half(1)


def _up_convglu(h2, wt_up, conv_w, conv_b, *, name, tc=256):
    s, d = h2.shape
    f = wt_up.shape[0] // 2
    nc = f // tc
    _, w_specs, b_specs = _conv_specs(s, f, tc)

    def body(h_ref, ta_ref, tg_ref, wa_ref, wg_ref, ba_ref, bg_ref, ua_ref, ug_ref, o_ref):
        ua = _dot(h_ref[...], ta_ref[...], NT)
        ua_ref[...] = ua
        ga = _gelu(_conv(ua, wa_ref, ba_ref))
        ug = _dot(h_ref[...], tg_ref[...], NT)
        ug_ref[...] = ug
        o_ref[...] = (ga * _conv(ug, wg_ref, bg_ref)).astype(BF16)

    col = pl.BlockSpec((s, tc), lambda j: (0, j))
    return pl.pallas_call(
        body, name=name, grid=(nc,),
        in_specs=[_full_spec((s, d)), pl.BlockSpec((tc, d), lambda j: (j, 0)), pl.BlockSpec((tc, d), lambda j: (nc + j, 0))]
        + w_specs + b_specs,
        out_specs=[col, col, col],
        out_shape=[jax.ShapeDtypeStruct((s, f), F32), jax.ShapeDtypeStruct((s, f), F32), jax.ShapeDtypeStruct((s, f), BF16)],
        compiler_params=_cparams(("parallel",)),
    )(h2, wt_up, wt_up, conv_w, conv_w, conv_b, conv_b)


def _convglu_bwd(dact, up_a, up_g, conv_w, conv_b, *, name, tc=256):
    s, f = up_a.shape
    _, w_specs, b_specs = _conv_specs(s, f, tc)
    up_specs = [pl.BlockSpec((s, tc), lambda j: (0, j))] * 2

    def half(dc, taps, w_ref, du_ref, dw_ref, db_ref):
        db_ref[...] = jnp.sum(dc, axis=0, keepdims=True)
        for k in range(3):
            dw_ref[k:k + 1, :] = jnp.sum(dc * taps[k], axis=0, keepdims=True)
        du = w_ref[2:3, :] * dc + w_ref[1:2, :] * _shift_up(dc, 1) + w_ref[0:1, :] * _shift_up(dc, 2)
        du_ref[...] = du.astype(BF16)

    def body(d_ref, ua_ref, ug_ref, wa_ref, wg_ref, ba_ref, bg_ref,
             dua_ref, dug_ref, dwa_ref, dwg_ref, dba_ref, dbg_ref):
        taps_a = (_shift_down(ua_ref[...], 2), _shift_down(ua_ref[...], 1), ua_ref[...])
        taps_g = (_shift_down(ug_ref[...], 2), _shift_down(ug_ref[...], 1), ug_ref[...])
        conv = lambda taps, w_ref, b_ref: b_ref[...] + w_ref[0:1, :] * taps[0] + w_ref[1:2, :] * taps[1] + w_ref[2:3, :] * taps[2]
        ca = conv(taps_a, wa_ref, ba_ref)
        cg = conv(taps_g, wg_ref, bg_ref)
        ga, dga = _gelu_and_grad(ca)
        dv = d_ref[...].astype(F32)
        half(dv * cg * dga, taps_a, wa_ref, dua_ref, dwa_ref, dba_ref)
        half(dv * ga, taps_g, wg_ref, dug_ref, dwg_ref, dbg_ref)

    col, w3, b1 = up_specs[0], w_specs[0], b_specs[0]
    return pl.pallas_call(
        body, name=name, grid=(f // tc,),
        in_specs=[col] + up_specs + w_specs + b_specs, out_specs=[col, col, w3, w3, b1, b1],
        out_shape=[jax.ShapeDtypeStruct((s, f), BF16), jax.ShapeDtypeStruct((s, f), BF16),
                   jax.ShapeDtypeStruct((3, f), F32), jax.ShapeDtypeStruct((3, f), F32),
                   jax.ShapeDtypeStruct((1, f), F32), jax.ShapeDtypeStruct((1, f), F32)],
        compiler_params=_cparams(("parallel",)),
    )(dact, up_a, up_g, conv_w, conv_w, conv_b, conv_b)


def _tri_dot(tri, x):
    b0 = x.astype(BF16)
    r1 = x - b0.astype(F32)
    b1 = r1.astype(BF16)
    b2 = (r1 - b1.astype(F32)).astype(BF16)
    return _dot(tri, b0, NN) + _dot(tri, b1, NN) + _dot(tri, b2, NN)


def _log_sigmoid(x):
    return jnp.minimum(x, 0.0) - jnp.log(1.0 + jnp.exp(-jnp.abs(x)))


def _expand_heads(col16, rows):
    src = lax.broadcasted_iota(I32, (128, HEADS * HEAD_DIM), 0)
    dst = lax.broadcasted_iota(I32, (128, HEADS * HEAD_DIM), 1) // HEAD_DIM
    spread = (src == dst).astype(BF16)
    p0, p1, p2 = _bf16_pieces(col16)
    return (_dot(p0.astype(BF16), spread, NN) + _dot(p1.astype(BF16), spread, NN)) + _dot(p2.astype(BF16), spread, NN)


def _forget_cumsum(f_logit, b_f, *, name):
    s = f_logit.shape[0]
    nb = s // 128

    def body(f_ref, b_ref, cqe_ref):
        row = lax.broadcasted_iota(I32, (128, 128), 0)
        col = lax.broadcasted_iota(I32, (128, 128), 1)
        tri = (col <= row).astype(BF16)

        def step(n, carry):
            r0 = pl.multiple_of(n * 128, 128)
            lf = _log_sigmoid(f_ref[pl.ds(r0, 128), :] + b_ref[...])
            cum = _tri_dot(tri, lf) + carry
            cqe_ref[pl.ds(r0, 128), :] = _expand_heads(cum, 128)
            return cum[127:128, :]

        lax.fori_loop(0, nb, step, jnp.zeros((1, 128), F32))

    return pl.pallas_call(
        body, name=name, grid=(1,),
        in_specs=[_full_spec((s, 128)), _full_spec((1, 128))],
        out_specs=_full_spec((s, HEADS * HEAD_DIM)),
        out_shape=jax.ShapeDtypeStruct((s, HEADS * HEAD_DIM), F32),
        compiler_params=_cparams(("arbitrary",)),
    )(f_logit, b_f)


def _forget_bwd(dcq16, sum_q16, f_logit, b_f, *, name):
    s = f_logit.shape[0]
    nb = s // 128

    def body(a_ref, k_ref, f_ref, b_ref, df_ref, db_ref):
        row = lax.broadcasted_iota(I32, (128, 128), 0)
        col = lax.broadcasted_iota(I32, (128, 128), 1)
        tri_rev = (col >= row).astype(BF16)

        def step(m, carry):
            suffix, dbsum = carry
            n = nb - 1 - m
            r0 = pl.multiple_of(n * 128, 128)
            dcum = a_ref[pl.ds(r0, 128), :] - k_ref[pl.ds(r0, 128), :]
            dlf = _tri_dot(tri_rev, dcum) + suffix
            df = dlf * _sigmoid(-(f_ref[pl.ds(r0, 128), :] + b_ref[...]))
            df_ref[pl.ds(r0, 128), :] = df.astype(BF16)
            return dlf[0:1, :], dbsum + jnp.sum(df, axis=0, keepdims=True)

        _, dbsum = lax.fori_loop(0, nb, step, (jnp.zeros((1, 128), F32), jnp.zeros((1, 128), F32)))
        db_ref[...] = dbsum

    return pl.pallas_call(
        body, name=name, grid=(1,),
        in_specs=[_full_spec((s, 128))] * 3 + [_full_spec((1, 128))],
        out_specs=[_full_spec((s, 128)), _full_spec((1, 128))],
        out_shape=[jax.ShapeDtypeStruct((s, 128), BF16), jax.ShapeDtypeStruct((1, 128), F32)],
        compiler_params=_cparams(("arbitrary",)),
    )(dcq16, sum_q16, f_logit, b_f)


ATT_T = 256


def _head_lanes(rows):
    return lax.broadcasted_iota(I32, (rows, 128), 1) < HEAD_DIM


def _bf16_pieces(c):
    p0 = c.astype(BF16).astype(F32)
    r = c - p0
    p1 = r.astype(BF16).astype(F32)
    p2 = (r - p1).astype(BF16).astype(F32)
    return p0, p1, p2


def _col_reduce(x, op):
    rows = x.shape[0]
    while rows > 8:
        rows //= 2
        x = op(x[:rows], x[rows:])
    return jnp.max(x, axis=0, keepdims=True) if op is jnp.maximum else jnp.sum(x, axis=0, keepdims=True)


def _attn_prep(qkv, cqe, carry=None, *, name):
    s = qkv.shape[0]
    npair = HEADS // 2

    def body(q_ref, k_ref, v_ref, c_ref, qa_ref, ka_ref, vt_ref):
        rows = 128
        lane = lax.broadcasted_iota(I32, (rows, 128), 1)

        def chunk(n, _):
            r0 = pl.multiple_of(n * rows, rows)
            sl = pl.ds(r0, rows)
            qv = q_ref[sl, :].astype(F32) * ATT_SCALE
            kv = k_ref[sl, :].astype(F32)
            p0, p1, p2 = _bf16_pieces(pltpu.roll(c_ref[sl, :], HEAD_DIM, 1))
            for e in range(2):
                mine = (lane < HEAD_DIM) if e == 0 else (lane >= HEAD_DIM)
                base = HEAD_DIM * (1 - e)
                ones_hi = jnp.where((lane >= base + 3) & (lane < base + 6), 1.0, 0.0)
                ones_lo = jnp.where((lane >= base) & (lane < base + 3), 1.0, 0.0)
                qa = jnp.where(mine, qv, jnp.where(lane == base, p0, jnp.where(lane == base + 1, p1,
                               jnp.where(lane == base + 2, p2, ones_hi))))
                ka = jnp.where(mine, kv, jnp.where(lane == base + 3, -p0, jnp.where(lane == base + 4, -p1,
                               jnp.where(lane == base + 5, -p2, ones_lo))))
                qa_ref[e, sl, :] = qa.astype(BF16)
                ka_ref[e, sl, :] = ka.astype(BF16)
            vt_ref[0, :, sl] = v_ref[sl, :].astype(F32).T.astype(BF16)
            return 0

        lax.fori_loop(0, s // rows, chunk, 0)

    pair = pl.BlockSpec((2, s, 128), lambda hp: (hp, 0, 0))
    return _carry_call(
        body, carry, name=name, grid=(npair,),
        in_specs=[pl.BlockSpec((s, 128), lambda hp: (0, hp)), pl.BlockSpec((s, 128), lambda hp: (0, npair + hp)),
                  pl.BlockSpec((s, 128), lambda hp: (0, 2 * npair + hp)), pl.BlockSpec((s, 128), lambda hp: (0, hp))],
        out_specs=[pair, pair, pl.BlockSpec((1, 128, s), lambda hp: (hp, 0, 0))],
        out_shape=[jax.ShapeDtypeStruct((HEADS, s, 128), BF16), jax.ShapeDtypeStruct((HEADS, s, 128), BF16),
                   jax.ShapeDtypeStruct((npair, 128, s), BF16)],
        scratch_shapes=[], args=[qkv, qkv, qkv, cqe])


def _attn_fwd(qa, ka, vt, carry=None, *, name):
    s = qa.shape[1]
    t = 2 * ATT_T
    nq = s // t
    npair = HEADS // 2

    def body(qa_ref, ka_ref, vt_ref, o_ref, lse_ref):
        i = pl.program_id(1)
        krow = lax.broadcasted_iota(I32, (t, t), 0)
        qcol = lax.broadcasted_iota(I32, (t, t), 1)
        sub = lax.broadcasted_iota(I32, (128, t), 0)
        row8 = lax.broadcasted_iota(I32, (8, t), 0)
        qbs = (qa_ref[0], qa_ref[1])
        tk = t

        def step(j, carry, diag):
            c0 = pl.multiple_of(j * tk, tk)
            vtb = vt_ref[0, :, pl.ds(c0, tk)]
            sts = [_dot(ka_ref[e, pl.ds(c0, tk), :], qbs[e], NT) for e in range(2)]
            if diag:
                sts = [jnp.where(krow <= qcol, st, NEG) for st in sts]
            pts, stats = [], []
            for e in range(2):
                m, l, _ = carry[e]
                m_new = jnp.maximum(m, _col_reduce(sts[e], jnp.maximum))
                alpha = jnp.exp(m - m_new)
                pt = jnp.exp(sts[e] - m_new)
                stats.append((m_new, alpha, alpha * l + _col_reduce(pt, jnp.add)))
                pts.append(pt.astype(BF16))
            pvs = [_dot(vtb, pts[e], NN) for e in range(2)]
            return tuple((stats[e][0], stats[e][2], stats[e][1] * carry[e][2] + pvs[e]) for e in range(2))

        init = (jnp.full((1, t), NEG, F32), jnp.zeros((1, t), F32), jnp.zeros((128, t), F32))
        carry = lax.fori_loop(0, i, functools.partial(step, diag=False), (init, init))
        (m0, l0, acc0), (m1, l1, acc1) = step(i, carry, True)
        o_pair = jnp.where(sub < HEAD_DIM, acc0 / l0, acc1 / l1)
        o_ref[...] = o_pair.T.astype(BF16)
        lse_ref[0] = jnp.where(row8 == 0, m0 + jnp.log(l0), jnp.where(row8 == 1, m1 + jnp.log(l1), 0.0))

    return _carry_call(
        body, carry, name=name, grid=(npair, nq),
        in_specs=[pl.BlockSpec((2, t, 128), lambda hp, i: (hp, i, 0)), pl.BlockSpec((2, s, 128), lambda hp, i: (hp, 0, 0)),
                  pl.BlockSpec((1, 128, s), lambda hp, i: (hp, 0, 0))],
        out_specs=[pl.BlockSpec((t, 128), lambda hp, i: (i, hp)), pl.BlockSpec((1, 8, t), lambda hp, i: (hp, 0, i))],
        out_shape=[jax.ShapeDtypeStruct((s, HEADS * HEAD_DIM), BF16), jax.ShapeDtypeStruct((npair, 8, s), F32)],
        scratch_shapes=[], args=[qa, ka, vt])


def _attn_delta(do, o, carry=None, *, name):
    s = do.shape[0]

    def body(do_ref, o_ref, d_ref):
        prod = do_ref[...].astype(F32) * o_ref[...].astype(F32)
        row = lax.broadcasted_iota(I32, (8, 128), 0)
        lane = lax.broadcasted_iota(I32, (8, 128), 1)
        sel = ((row == 0) & (lane < HEAD_DIM) | (row == 1) & (lane >= HEAD_DIM)).astype(BF16)
        p0, p1, p2 = _bf16_pieces(prod)
        d_ref[0] = (_dot(sel, p0.astype(BF16), NT) + _dot(sel, p1.astype(BF16), NT)) + _dot(sel, p2.astype(BF16), NT)

    pair = pl.BlockSpec((s, 128), lambda hp: (0, hp))
    (delta3,), carried = _carry_call(
        body, carry, name=name, grid=(HEADS // 2,), in_specs=[pair, pair],
        out_specs=[pl.BlockSpec((1, 8, s), lambda hp: (hp, 0, 0))],
        out_shape=[jax.ShapeDtypeStruct((HEADS // 2, 8, s), F32)], scratch_shapes=[], args=[do, o])
    return delta3, carried


def _attn_bwd(qa, ka, qkv, do, lse3, delta3, carry=None, *, name):
    s = qa.shape[1]
    t = 2 * ATT_T
    nb = s // t
    npair = HEADS // 2

    def body(qa_ref, ka_ref, v_ref, do_ref, lse_ref, delta_ref, dq_ref, dk_ref, dv_ref, aux_ref, dcq_ref, dqt):
        hp = pl.program_id(0)
        first = _head_lanes(t)
        lane = lax.broadcasted_iota(I32, (t, 128), 1)
        dqt[...] = jnp.zeros_like(dqt)

        @pl.when(hp == 0)
        def _():
            aux_ref[...] = jnp.zeros_like(aux_ref)

        krow = lax.broadcasted_iota(I32, (t, t), 0)
        qcol = lax.broadcasted_iota(I32, (t, t), 1)

        def key_block(j, _):
            c0 = pl.multiple_of(j * t, t)
            vb = v_ref[pl.ds(c0, t), :]
            kbs = (ka_ref[0, pl.ds(c0, t), :], ka_ref[1, pl.ds(c0, t), :])
            kbts = tuple(kb.astype(F32).T.astype(BF16) for kb in kbs)
            vhs = (jnp.where(first, vb, jnp.zeros_like(vb)), jnp.where(first, jnp.zeros_like(vb), vb))

            def query_block(i, carry, diag):
                r0 = pl.multiple_of(i * t, t)
                dob = do_ref[pl.ds(r0, t), :]
                sts = [_dot(kbs[e], qa_ref[e, pl.ds(r0, t), :], NT) for e in range(2)]
                dpts = [_dot(vhs[e], dob, NT) for e in range(2)]
                ptbs, dsbs = [], []
                for e in range(2):
                    st = jnp.where(krow <= qcol, sts[e], NEG) if diag else sts[e]
                    pt = jnp.exp(st - lse_ref[0, e:e + 1, pl.ds(r0, t)])
                    dsbs.append((pt * (dpts[e] - delta_ref[0, e:e + 1, pl.ds(r0, t)])).astype(BF16))
                    ptbs.append(pt.astype(BF16))
                out = []
                for e in range(2):
                    dk_a, dv_a = carry[e]
                    dv_a = dv_a + _dot(ptbs[e], dob, NN)
                    dk_a = dk_a + _dot(dsbs[e], qa_ref[e, pl.ds(r0, t), :], NN)
                    dqt[e, :, pl.ds(r0, t)] += _dot(kbts[e], dsbs[e], NN)
                    out.append((dk_a, dv_a))
                return tuple(out)

            zero = jnp.zeros((t, 128), F32)
            carry = query_block(j, ((zero, zero), (zero, zero)), True)
            (dk0, dv0), (dk1, dv1) = lax.fori_loop(j + 1, nb, functools.partial(query_block, diag=False), carry)
            dk_ref[pl.ds(c0, t), :] = jnp.where(first, dk0, dk1).astype(BF16)
            dv_ref[pl.ds(c0, t), :] = jnp.where(first, dv0, dv1).astype(BF16)
            sum_q = jnp.where(lane == 2 * hp, dk0[:, HEAD_DIM + 3:HEAD_DIM + 4],
                              jnp.where(lane == 2 * hp + 1, dk1[:, 3:4], aux_ref[pl.ds(c0, t), :]))
            aux_ref[pl.ds(c0, t), :] = sum_q
            return 0

        lax.fori_loop(0, nb, key_block, 0)
        sub = lax.broadcasted_iota(I32, (128, s), 0)
        row8 = lax.broadcasted_iota(I32, (8, s), 0)
        dq_ref[...] = (jnp.where(sub < HEAD_DIM, dqt[0], dqt[1]) * ATT_SCALE).T.astype(BF16)
        dcq_ref[0] = jnp.where(row8 == 0, dqt[0, HEAD_DIM:HEAD_DIM + 1, :], jnp.where(row8 == 1, dqt[1, 0:1, :], 0.0))

    def pair_cols(off):
        return pl.BlockSpec((s, 128), lambda hp: (0, off + hp))

    heads = pl.BlockSpec((2, s, 128), lambda hp: (hp, 0, 0))
    rows = pl.BlockSpec((1, 8, s), lambda hp: (hp, 0, 0))
    wide = jax.ShapeDtypeStruct((s, HEADS * HEAD_DIM), BF16)
    return _carry_call(
        body, carry, name=name, grid=(npair,),
        in_specs=[heads, heads, pair_cols(2 * npair), pair_cols(0), rows, rows],
        out_specs=[pair_cols(0), pair_cols(0), pair_cols(0), pl.BlockSpec((s, 128), lambda hp: (0, 0)), rows],
        out_shape=[wide, wide, wide, jax.ShapeDtypeStruct((s, 128), F32), jax.ShapeDtypeStruct((npair, 8, s), F32)],
        scratch_shapes=[pltpu.VMEM((2, 128, s), F32)], args=[qa, ka, qkv, do, lse3, delta3])


def _adam_math(w, g, m, v):
    m = ADAM_B1 * m + (1.0 - ADAM_B1) * g
    v = ADAM_B2 * v + (1.0 - ADAM_B2) * (g * g)
    m_hat = m / (1.0 - ADAM_B1 ** ADAM_STEP)
    v_hat = v / (1.0 - ADAM_B2 ** ADAM_STEP)
    delta = -ADAM_LR * (m_hat / (jnp.sqrt(v_hat) + ADAM_EPS) + ADAM_WD * w)
    return delta, m, v


def _sum_pairs(keep, recv, pos, *, name):
    _, r, c = recv.shape
    tr = _row_tile(r, 1024)

    def body(pos_ref, a_ref, b_ref, o32_ref, o16_ref):
        tot = a_ref[...].astype(F32) + b_ref[...].astype(F32)
        o16_ref[...] = tot.astype(BF16)

        @pl.when(pl.program_id(1) == 2 * pos_ref[0] + pos_ref[1])
        def _():
            o32_ref[...] = tot

    out = pl.BlockSpec((1, tr, c), lambda i, q, pos: (q, i, 0))
    grid_spec = pltpu.PrefetchScalarGridSpec(
        num_scalar_prefetch=1, grid=(r // tr, 4),
        in_specs=[pl.BlockSpec((1, tr, c), lambda i, q, pos: (2 * q + pos[2], i, 0)), out],
        out_specs=[pl.BlockSpec((1, tr, c), lambda i, q, pos: (0, i, 0)), out])
    return pl.pallas_call(
        body, name=name, grid_spec=grid_spec,
        out_shape=[jax.ShapeDtypeStruct((1, r, c), F32), jax.ShapeDtypeStruct((4, r, c), BF16)],
        compiler_params=_cparams(("arbitrary", "arbitrary")),
    )(pos, keep, recv)


def _adam_sharded(psum, recv, w, m, v, pos, *, name):
    r, c = w.shape
    rg = psum.shape[1]

    def body(pos_ref, p_ref, r_ref, w_ref, m_ref, v_ref, g_ref, d_ref, mo_ref, vo_ref):
        part = lambda ref, q: ref[q] if rg == r else ref[q, :r, :]
        g = part(p_ref, 0) + part(r_ref, 0).astype(F32) + part(r_ref, 1).astype(F32) + part(r_ref, 2).astype(F32)
        delta, mn, vn = _adam_math(w_ref[...], g, m_ref[...], v_ref[...])
        g_ref[...] = g
        d_ref[...] = delta
        mo_ref[...] = mn
        vo_ref[...] = vn

    if rg == r:
        tr = _row_tile(r, 320)
        grid = (r // tr,)
        row = pl.BlockSpec((tr, c), lambda i, pos: (i, 0))
        sums = lambda n: pl.BlockSpec((n, tr, c), lambda i, pos: (0, i, 0))
    else:
        tc = 256
        grid = (c // tc,)
        row = pl.BlockSpec((r, tc), lambda i, pos: (0, i))
        sums = lambda n: pl.BlockSpec((n, rg, tc), lambda i, pos: (0, 0, i))
    grid_spec = pltpu.PrefetchScalarGridSpec(
        num_scalar_prefetch=1, grid=grid, in_specs=[sums(1), sums(3), row, row, row], out_specs=[row, row, row, row])
    o = jax.ShapeDtypeStruct((r, c), F32)
    return pl.pallas_call(
        body, name=name, grid_spec=grid_spec, out_shape=[o, o, o, o],
        compiler_params=_cparams(("parallel",)),
    )(pos, psum, recv, w, m, v)


def _adam_replicated(chip_sums, last, w, m, v, *, name):
    r = w.shape[0]

    def body(s_ref, l_ref, w_ref, m_ref, v_ref, g_ref, d_ref, mo_ref, vo_ref):
        g = (((s_ref[0] + s_ref[1]) + s_ref[2]) + s_ref[3]) + l_ref[...]
        delta, mn, vn = _adam_math(w_ref[...], g, m_ref[...], v_ref[...])
        g_ref[...] = g
        d_ref[...] = delta
        mo_ref[...] = mn
        vo_ref[...] = vn

    o = jax.ShapeDtypeStruct((r, 1024), F32)
    full = _full_spec((r, 1024))
    return pl.pallas_call(
        body, name=name, grid=(1,),
        in_specs=[_full_spec((4, r, 1024)), full, full, full, full], out_specs=[full] * 4, out_shape=[o] * 4,
        compiler_params=_cparams(("arbitrary",)),
    )(chip_sums, last, w, m, v)


ASM_OUT = 512
ASM_SRC = 304


def _w_in_row(r):
    return r if r < 2048 else (r + O_G - 2048 if r < 4096 else r - 2048)


def _assemble_wt_main(g, *, name):
    win_a = ASM_OUT + 16
    table = []
    for blk in range(MAIN_COLS // ASM_OUT):
        j, l0 = divmod(_w_in_row(blk * ASM_OUT), IN_SHARD)
        start = min(l0 // 16 * 16, IN_SHARD_PAD - win_a)
        n_a = min(ASM_OUT, IN_SHARD - l0)
        table.append((j, start, l0 - start, n_a, int(n_a < ASM_OUT)))
    steps = len(table)

    def body(tab_ref, g_ref, o_ref, buf_a, buf_b, sems):
        blk = pl.program_id(0)
        slot = blk % 3

        def copy_a(step, sl):
            first = pl.multiple_of(tab_ref[step, 1], 16)
            return pltpu.make_async_copy(g_ref.at[tab_ref[step, 0], pl.ds(first, win_a)], buf_a.at[sl], sems.at[0, sl])

        def copy_b(step, sl):
            return pltpu.make_async_copy(g_ref.at[tab_ref[step, 0] + 1, pl.ds(0, ASM_OUT)], buf_b.at[sl], sems.at[1, sl])

        def fetch(step, sl):
            copy_a(step, sl).start()

            @pl.when(tab_ref[step, 4] == 1)
            def _():
                copy_b(step, sl).start()

        @pl.when(blk == 0)
        def _():
            fetch(0, 0)
            fetch(1, 1)

        @pl.when(blk + 2 < steps)
        def _():
            fetch(blk + 2, (blk + 2) % 3)

        off, n_a = tab_ref[blk, 2], tab_ref[blk, 3]
        copy_a(blk, slot).wait()
        r = lax.broadcasted_iota(I32, (ASM_OUT, win_a), 0)
        k = lax.broadcasted_iota(I32, (ASM_OUT, win_a), 1)
        sel_a = ((k == r + off) & (r < n_a)).astype(BF16)
        o_ref[...] = _dot(sel_a, buf_a[slot], NN).astype(BF16)

        @pl.when(tab_ref[blk, 4] == 1)
        def _():
            copy_b(blk, slot).wait()
            rb = lax.broadcasted_iota(I32, (ASM_OUT, ASM_OUT), 0)
            kb = lax.broadcasted_iota(I32, (ASM_OUT, ASM_OUT), 1)
            sel_b = ((kb == rb - n_a) & (rb >= n_a)).astype(BF16)
            o_ref[...] += _dot(sel_b, buf_b[slot], NN).astype(BF16)

    grid_spec = pltpu.PrefetchScalarGridSpec(
        num_scalar_prefetch=1, grid=(steps,), in_specs=[ANY],
        out_specs=pl.BlockSpec((ASM_OUT, D_MODEL), lambda blk, tab: (blk, 0)),
        scratch_shapes=[pltpu.VMEM((3, win_a, D_MODEL), BF16), pltpu.VMEM((3, ASM_OUT, D_MODEL), BF16),
                        pltpu.SemaphoreType.DMA((2, 3))])
    return pl.pallas_call(
        body, name=name, grid_spec=grid_spec, out_shape=jax.ShapeDtypeStruct((MAIN_COLS, D_MODEL), BF16),
        compiler_params=_cparams(("arbitrary",)),
    )(jnp.asarray(table, I32), g)


def _pair_sum_small(mine, theirs, *, name):
    def body(a_ref, b_ref, o_ref):
        o_ref[...] = a_ref[...] + b_ref[...]

    full = _full_spec(mine.shape)
    return pl.pallas_call(
        body, name=name, grid=(1,), in_specs=[full, full], out_specs=full,
        out_shape=jax.ShapeDtypeStruct(mine.shape, F32), compiler_params=_cparams(("arbitrary",)),
    )(mine, theirs)


ANY = pl.BlockSpec(memory_space=pl.ANY)
OTHER_CHIPS = ((1, 0), (0, 1), (1, 1))


class _Carry:
    def __init__(self, inputs, out_shapes, scratch, start, wait, aliases=None, middle=None):
        self.inputs, self.out_shapes, self.scratch = list(inputs), list(out_shapes), list(scratch)
        self.start, self.wait, self.aliases, self.middle = start, wait, dict(aliases or {}), middle


def _carry_join(*carries):
    n_in = [len(c.inputs) for c in carries]
    n_out = [len(c.out_shapes) for c in carries]
    n_scr = [len(c.scratch) for c in carries]

    def split(refs, counts):
        out, k = [], 0
        for n in counts:
            out.append(refs[k:k + n])
            k += n
        return out

    def start(ins, outs, scr):
        for c, i, o, s in zip(carries, split(ins, n_in), split(outs, n_out), split(scr, n_scr)):
            c.start(i, o, s)

    def wait(ins, outs, scr):
        for c, i, o, s in zip(carries, split(ins, n_in), split(outs, n_out), split(scr, n_scr)):
            c.wait(i, o, s)

    def middle(ins, outs, scr):
        for c, i, o, s in zip(carries, split(ins, n_in), split(outs, n_out), split(scr, n_scr)):
            if c.middle is not None:
                c.middle(i, o, s)

    aliases = {}
    for k, c in enumerate(carries):
        aliases.update({sum(n_in[:k]) + i: sum(n_out[:k]) + o for i, o in c.aliases.items()})
    joined = _Carry(sum((c.inputs for c in carries), []), sum((c.out_shapes for c in carries), []),
                    sum((c.scratch for c in carries), []), start, wait, aliases,
                    middle if any(c.middle is not None for c in carries) else None)
    joined.counts = n_out
    joined.split = lambda results: split(results, n_out)
    return joined


def _carried(body, carry, n_in, n_out, grid):
    if carry is None:
        return body
    ci, co, cs = len(carry.inputs), len(carry.out_shapes), len(carry.scratch)

    def wrapped(*refs):
        ins, cins = refs[:n_in], refs[n_in:n_in + ci]
        outs, couts = refs[n_in + ci:n_in + ci + n_out], refs[n_in + ci + n_out:n_in + ci + n_out + co]
        rest = refs[n_in + ci + n_out + co:]
        scratch, cscr = rest[:len(rest) - cs], rest[len(rest) - cs:]
        first, last, step, steps = None, None, 0, 1
        for axis, size in enumerate(grid):
            f, l = pl.program_id(axis) == 0, pl.program_id(axis) == size - 1
            first = f if first is None else first & f
            last = l if last is None else last & l
            step, steps = step * size + pl.program_id(axis), steps * size

        @pl.when(first)
        def _():
            carry.start(cins, couts, cscr)

        if carry.middle is not None:
            @pl.when(step == steps // 2)
            def _():
                carry.middle(cins, couts, cscr)

        body(*ins, *outs, *scratch)

        @pl.when(last)
        def _():
            carry.wait(cins, couts, cscr)

    return wrapped


def _carry_call(body, carry, *, name, grid, in_specs, out_specs, out_shape, scratch_shapes, args, vmem=True,
                own_aliases=None):
    n_in, n_out = len(in_specs), len(out_specs)
    extra_in = [ANY] * len(carry.inputs) if carry else []
    extra_out = [ANY] * len(carry.out_shapes) if carry else []
    aliases = dict(own_aliases or {})
    if carry:
        aliases.update({n_in + i: n_out + o for i, o in carry.aliases.items()})
    out = pl.pallas_call(
        _carried(body, carry, n_in, n_out, grid), name=name, grid=grid,
        in_specs=list(in_specs) + extra_in, out_specs=list(out_specs) + extra_out,
        out_shape=list(out_shape) + (carry.out_shapes if carry else []),
        scratch_shapes=list(scratch_shapes) + (carry.scratch if carry else []),
        input_output_aliases=aliases,
        compiler_params=_cparams(("arbitrary",) * len(grid)) if vmem else None,
    )(*args, *(carry.inputs if carry else []))
    return list(out[:n_out]), list(out[n_out:])


def _run_carry(carry, *, name):
    return _carry_call(lambda: None, carry, name=name, grid=(1,), in_specs=[], out_specs=[], out_shape=[],
                       scratch_shapes=[], args=[], vmem=False)[1]


def _sems(n):
    return [pltpu.SemaphoreType.DMA((n,)), pltpu.SemaphoreType.DMA((n,))]


def _carry_gather1(shards):
    n = len(shards)
    per = 7

    def plan(x_refs, out_refs, scr):
        send_sems, recv_sems, local_sems = scr
        x, y, c = lax.axis_index("x"), lax.axis_index("y"), lax.axis_index("c")
        me, sibling = (x, y, c), (x, y, 1 - c)
        near_x, near_y, across = (1 - x, y, c), (x, 1 - y, c), (1 - x, 1 - y, c)

        def rows(ref, t, half):
            r = shards[t].shape[0]
            h = r if r < 32 else -(-(r // 2) // 16) * 16
            if half is None or h == r:
                return ref
            return ref.at[pl.ds(0, h)] if half == 0 else ref.at[pl.ds(h, r - h)]

        def copy(t, k, block, half, to, from_input=False):
            px, py, pc = block
            slab = rows(out_refs[t].at[4 * px + 2 * py + pc], t, half)
            return pltpu.make_async_remote_copy(
                src_ref=rows(x_refs[t], t, half) if from_input else slab, dst_ref=slab,
                send_sem=send_sems.at[per * t + k], recv_sem=recv_sems.at[per * t + k], device_id=to,
                device_id_type=MESH)

        two = [shards[t].shape[0] >= 32 for t in range(n)]
        local = lambda t: pltpu.make_async_copy(x_refs[t], out_refs[t].at[4 * x + 2 * y + c], local_sems.at[t])
        first = lambda t: ([(0, me, None, sibling), (1, me, 0, near_x)]
                           + ([(2, me, 1, near_y), (3, me, 1, near_x)] if two[t] else []) + [(4, me, 0, near_y)])
        passed = lambda t: [(5, near_x, 0, near_y)] + ([(6, near_y, 1, near_x)] if two[t] else [])
        early = lambda t: [(1, near_x, 0, me)] + ([(2, near_y, 1, me)] if two[t] else [])
        late = lambda t: ([(0, sibling, None, me), (4, near_y, 0, me), (5, across, 0, me)]
                          + ([(3, near_x, 1, me), (6, across, 1, me)] if two[t] else []))
        return copy, local, first, passed, early, late

    def start(x_refs, out_refs, scr):
        copy, local, first, _, _, _ = plan(x_refs, out_refs, scr)
        for urgent in (True, False):
            for t in range(n):
                if not urgent:
                    local(t).start()
                for k, block, half, to in first(t):
                    if (k in (1, 2)) == urgent:
                        copy(t, k, block, half, to, from_input=True).start()

    def middle(x_refs, out_refs, scr):
        copy, _, _, passed, early, _ = plan(x_refs, out_refs, scr)
        for t in range(n):
            for (k, block, half, to), fwd in zip(early(t), passed(t)):
                copy(t, k, block, half, to).wait_recv()
                copy(t, *fwd).start()

    def wait(x_refs, out_refs, scr):
        copy, local, first, passed, _, late = plan(x_refs, out_refs, scr)
        for t in range(n):
            for k, block, half, to in late(t):
                copy(t, k, block, half, to).wait_recv()
        for t in range(n):
            for k, block, half, to in first(t):
                copy(t, k, block, half, to, from_input=True).wait_send()
            for k, block, half, to in passed(t):
                copy(t, k, block, half, to).wait_send()
            local(t).wait()

    return _Carry(shards, [jax.ShapeDtypeStruct((N_DEV,) + a.shape, a.dtype) for a in shards],
                  _sems(per * n) + [pltpu.SemaphoreType.DMA((n,))], start, wait, middle=middle)


def _carry_gather2(gathered):
    n = len(gathered)

    def copies(in_refs, g_refs, scr, with_arrivals):
        send_sems, recv_sems = scr
        x, y, c = lax.axis_index("x"), lax.axis_index("y"), lax.axis_index("c")
        sends, arrivals = [], []
        for t in range(n):
            for j, (fx, fy) in enumerate(OTHER_CHIPS):
                px, py = x ^ fx, y ^ fy
                sems = dict(send_sem=send_sems.at[3 * t + j], recv_sem=recv_sems.at[3 * t + j],
                            device_id=(x, y, 1 - c), device_id_type=MESH)
                mine, theirs = 4 * px + 2 * py + c, 4 * px + 2 * py + (1 - c)
                sends.append(pltpu.make_async_remote_copy(src_ref=in_refs[t].at[mine], dst_ref=g_refs[t].at[mine], **sems))
                if with_arrivals:
                    arrivals.append(pltpu.make_async_remote_copy(
                        src_ref=in_refs[t].at[mine], dst_ref=g_refs[t].at[theirs], **sems))
        return sends, arrivals

    def start(in_refs, g_refs, scr):
        for cp in copies(in_refs, g_refs, scr, False)[0]:
            cp.start()

    def wait(in_refs, g_refs, scr):
        sends, arrivals = copies(in_refs, g_refs, scr, True)
        for cp in arrivals:
            cp.wait_recv()
        for cp in sends:
            cp.wait_send()

    return _Carry(gathered, [jax.ShapeDtypeStruct(a.shape, a.dtype) for a in gathered], _sems(3 * n), start, wait,
                  aliases={t: t for t in range(n)})


def _allreduce_rows(x, *, name):
    def body(x_ref, o_ref, sib_ref, mine_ref, tab_ref, send_sems, recv_sems):
        x, y, c = lax.axis_index("x"), lax.axis_index("y"), lax.axis_index("c")
        swap = pltpu.make_async_remote_copy(src_ref=x_ref, dst_ref=sib_ref, send_sem=send_sems.at[0],
                                            recv_sem=recv_sems.at[0], device_id=(x, y, 1 - c), device_id_type=MESH)
        swap.start()
        swap.wait()
        mine_ref[...] = x_ref[...] + sib_ref[...]
        tab_ref[pl.ds(2 * x + y, 1)] = mine_ref[...][None]

        def copy(k, slot):
            fx, fy = OTHER_CHIPS[k]
            return pltpu.make_async_remote_copy(
                src_ref=mine_ref, dst_ref=tab_ref.at[slot], send_sem=send_sems.at[1 + k], recv_sem=recv_sems.at[1 + k],
                device_id=(x ^ fx, y ^ fy, c), device_id_type=MESH)

        for k in range(3):
            copy(k, 2 * x + y).start()
        for k, (fx, fy) in enumerate(OTHER_CHIPS):
            copy(k, 2 * (x ^ fx) + (y ^ fy)).wait()
        o_ref[...] = ((tab_ref[0] + tab_ref[1]) + tab_ref[2]) + tab_ref[3]

    vmem = pl.BlockSpec(memory_space=pltpu.VMEM)
    return pl.pallas_call(
        body, name=name, out_shape=jax.ShapeDtypeStruct(x.shape, F32), in_specs=[vmem], out_specs=vmem,
        scratch_shapes=[pltpu.VMEM(x.shape, F32), pltpu.VMEM(x.shape, F32), pltpu.VMEM((4,) + x.shape, F32)] + _sems(4),
    )(x)


def _allgather(shards, *, name):
    n = len(shards)
    per = 10

    def body(*refs):
        x_refs, out_refs = refs[:n], refs[n:2 * n]
        send_sems, recv_sems, local_sems = refs[2 * n:]
        x, y, c = lax.axis_index("x"), lax.axis_index("y"), lax.axis_index("c")
        me, sibling = (x, y, c), (x, y, 1 - c)
        near_x, near_y, across = (1 - x, y), (x, 1 - y), (1 - x, 1 - y)

        def rows(ref, t, half):
            r = shards[t].shape[0]
            h = -(-(r // 2) // 16) * 16
            if half is None:
                return ref
            return ref.at[pl.ds(0, h)] if half == 0 else ref.at[pl.ds(h, r - h)]

        def copy(t, k, block, half, to, from_input=False):
            px, py, pc = block
            slab = rows(out_refs[t].at[4 * px + 2 * py + pc], t, half)
            return pltpu.make_async_remote_copy(
                src_ref=rows(x_refs[t], t, half) if from_input else slab, dst_ref=slab,
                send_sem=send_sems.at[per * t + k], recv_sem=recv_sems.at[per * t + k], device_id=to,
                device_id_type=MESH)

        mine = [pltpu.make_async_copy(x_refs[t], out_refs[t].at[4 * x + 2 * y + c], local_sems.at[t]) for t in range(n)]
        for cp in mine:
            cp.start()
        sent = []

        def send(cp):
            cp.start()
            sent.append(cp)

        for t in range(n):
            send(copy(t, 0, me, None, sibling, from_input=True))
            send(copy(t, 1, me, 0, (*near_x, c), from_input=True))
            send(copy(t, 2, me, 1, (*near_y, c), from_input=True))
            send(copy(t, 3, me, 1, (*near_x, c), from_input=True))
            send(copy(t, 4, me, 0, (*near_y, c), from_input=True))
        for t in range(n):
            copy(t, 1, (*near_x, c), 0, me).wait_recv()
            send(copy(t, 5, (*near_x, c), 0, (*near_y, c)))
            copy(t, 2, (*near_y, c), 1, me).wait_recv()
            send(copy(t, 6, (*near_y, c), 1, (*near_x, c)))
        for t in range(n):
            copy(t, 3, (*near_x, c), 1, me).wait_recv()
            send(copy(t, 7, (*near_x, c), None, sibling))
            copy(t, 4, (*near_y, c), 0, me).wait_recv()
            send(copy(t, 8, (*near_y, c), None, sibling))
            copy(t, 5, (*across, c), 0, me).wait_recv()
            copy(t, 6, (*across, c), 1, me).wait_recv()
            send(copy(t, 9, (*across, c), None, sibling))
        for t in range(n):
            copy(t, 0, sibling, None, me).wait_recv()
            for k, chip in ((7, near_x), (8, near_y), (9, across)):
                copy(t, k, (*chip, 1 - c), None, me).wait_recv()
        for cp in sent:
            cp.wait_send()
        for cp in mine:
            cp.wait()

    return pl.pallas_call(
        body, name=name, out_shape=[jax.ShapeDtypeStruct((N_DEV,) + a.shape, a.dtype) for a in shards],
        in_specs=[ANY] * n, out_specs=[ANY] * n,
        scratch_shapes=[pltpu.SemaphoreType.DMA((per * n,)), pltpu.SemaphoreType.DMA((per * n,)),
                        pltpu.SemaphoreType.DMA((n,))],
    )(*shards)


def _carry_sibling(slabs, small=None):
    n = len(slabs)
    extra = [] if small is None else [small]

    def copies(in_refs, out_refs, scr):
        send_sems, recv_sems = scr
        x, y, c = lax.axis_index("x"), lax.axis_index("y"), lax.axis_index("c")
        sibling = (x, y, 1 - c)
        out = []
        for t in range(n):
            for q in range(4):
                out.append(pltpu.make_async_remote_copy(
                    src_ref=in_refs[t].at[2 * q + (1 - c)], dst_ref=out_refs[t].at[q],
                    send_sem=send_sems.at[4 * t + q], recv_sem=recv_sems.at[4 * t + q],
                    device_id=sibling, device_id_type=MESH))
        if extra:
            out.append(pltpu.make_async_remote_copy(
                src_ref=in_refs[n], dst_ref=out_refs[n], send_sem=send_sems.at[4 * n], recv_sem=recv_sems.at[4 * n],
                device_id=sibling, device_id_type=MESH))
        return out

    def start(*refs):
        for cp in copies(*refs):
            cp.start()

    def wait(*refs):
        for cp in copies(*refs):
            cp.wait()

    return _Carry(list(slabs) + extra,
                  [jax.ShapeDtypeStruct((4,) + a.shape[1:], a.dtype) for a in slabs]
                  + [jax.ShapeDtypeStruct(a.shape, a.dtype) for a in extra], _sems(4 * n + 1), start, wait)


def _carry_chips(psums, small_sum=None):
    n = len(psums)
    table = small_sum is not None

    def copies(in_refs, out_refs, scr, arrivals):
        send_sems, recv_sems = scr[0], scr[1]
        x, y, c = lax.axis_index("x"), lax.axis_index("y"), lax.axis_index("c")
        out = []
        for k, (fx, fy) in enumerate(OTHER_CHIPS):
            px, py = x ^ fx, y ^ fy
            for t in range(n):
                out.append(pltpu.make_async_remote_copy(
                    src_ref=in_refs[t].at[2 * px + py], dst_ref=out_refs[t].at[k],
                    send_sem=send_sems.at[3 * t + k], recv_sem=recv_sems.at[3 * t + k],
                    device_id=(px, py, c), device_id_type=MESH))
            if table:
                slot = 2 * px + py if arrivals else 2 * x + y
                out.append(pltpu.make_async_remote_copy(
                    src_ref=in_refs[n], dst_ref=out_refs[n].at[slot], send_sem=send_sems.at[3 * n + k],
                    recv_sem=recv_sems.at[3 * n + k], device_id=(px, py, c), device_id_type=MESH))
        return out

    def own(in_refs, out_refs, scr):
        x, y = lax.axis_index("x"), lax.axis_index("y")
        return pltpu.make_async_copy(in_refs[n], out_refs[n].at[2 * x + y], scr[2])

    def start(in_refs, out_refs, scr):
        if table:
            own(in_refs, out_refs, scr).start()
        for cp in copies(in_refs, out_refs, scr, False):
            cp.start()

    def wait(in_refs, out_refs, scr):
        for cp in copies(in_refs, out_refs, scr, True):
            cp.wait()
        if table:
            own(in_refs, out_refs, scr).wait()

    out_shapes = [jax.ShapeDtypeStruct((3,) + a.shape[1:], a.dtype) for a in psums]
    if table:
        out_shapes.append(jax.ShapeDtypeStruct((4,) + small_sum.shape, F32))
    return _Carry(list(psums) + ([small_sum] if table else []), out_shapes,
                  _sems(3 * n + 3) + ([pltpu.SemaphoreType.DMA] if table else []), start, wait)


def _to_comm(name, kind, block, dtype=BF16):
    a = block[0]
    if kind == "cols":
        a = a.T
        if name == "w_in" and dtype == BF16:
            a = jnp.pad(a, ((0, IN_SHARD_PAD - IN_SHARD), (0, 0)))
    return a if kind == "f32" else a.astype(dtype)


def _from_comm(name, kind, a):
    if kind == "cols":
        if name == "w_in" and a.shape[0] != IN_SHARD:
            a = a[:IN_SHARD]
        a = a.T
    return a[None]


def _assemble_weights(g):
    out = {}
    if "w_in" in g:
        out["wt_main"] = _assemble_wt_main(g["w_in"], name="assemble_w_in")
        j, l0 = divmod(O_F, IN_SHARD)
        out["wt_f"] = jnp.pad(g["w_in"][j, l0:l0 + HEADS], ((0, 128 - HEADS), (0, 0)))
    square = dict(w_branch_a="w_a", w_branch_b="w_b", w_out="w_out", w_ple_gate="w_pg")
    for long, short in square.items():
        if long in g:
            out[short] = g[long].reshape(D_MODEL, D_MODEL)
    if "w_up" in g:
        out["wt_up"] = g["w_up"].reshape(2 * D_FF, D_MODEL)
    if "conv_w" in g:
        out["conv_w"] = g["conv_w"].transpose(1, 0, 2).reshape(3, 2 * D_FF)
    if "w_down" in g:
        out["w_down"] = g["w_down"].reshape(D_FF, D_MODEL)
    if "w_ple" in g:
        out["wt_ple"] = g["w_ple"].reshape(D_MODEL, PLE_DIM)
    return out


def _grad_slabs(gr):
    out = {}
    if "wt_main" in gr:
        gm, gf = gr["wt_main"], gr["wt_f"]
        segments = ((0, 2048, gm, 0), (2048, O_F, gm, 2048), (O_F, O_G, gf, -O_F), (O_G, IN_COLS, gm, 2048 - O_G))
        slabs = []
        for j in range(N_DEV):
            lo, hi = j * IN_SHARD, (j + 1) * IN_SHARD
            pieces = [src[max(lo, a) + shift:min(hi, b) + shift] for a, b, src, shift in segments if max(lo, a) < min(hi, b)]
            pieces.append(jnp.zeros((IN_SHARD_PAD - IN_SHARD, D_MODEL), gm.dtype))
            slabs.append(jnp.concatenate(pieces, axis=0))
        out["w_in"] = jnp.stack(slabs)
    rows = dict(w_a="w_branch_a", w_b="w_branch_b", w_out="w_out", wt_up="w_up", w_down="w_down", w_pg="w_ple_gate")
    for short, long in rows.items():
        if short in gr:
            out[long] = gr[short].reshape(N_DEV, -1, D_MODEL)
    if "conv_w" in gr:
        out["conv_w"] = gr["conv_w"].reshape(3, N_DEV, -1).transpose(1, 0, 2)
    if "wt_ple" in gr:
        out["w_ple"] = gr["wt_ple"].reshape(N_DEV, -1, PLE_DIM)
    return {k: v.astype(BF16) for k, v in out.items()}


def _rows(a, rows):
    flat = a.reshape(-1)
    return jnp.pad(flat, (0, rows * 1024 - flat.shape[0])).reshape(rows, 1024)


def _pack_small(parts):
    return jnp.concatenate([_rows(parts[n].astype(F32), r) for n, r in SMALL], axis=0)


def _small(packed, name, shape):
    off, r = SMALL_OFF[name]
    n = math.prod(shape)
    return packed[off:off + r].reshape(-1)[:n].reshape(shape)


class _Exchanges:
    W_S_ROWS = SMALL_OFF["gmlp_w_s"]

    def __init__(self, later, shards, pos):
        self.later, self.shards, self.pos = later, dict(zip(later, shards)), pos
        self.level1, self.slabs, self.from_sib, self.sums32, self.reduced, self.tables = {}, {}, {}, {}, {}, {}

    def gather1(self, names):
        carry = _carry_gather1([self.shards[n] for n in names])
        carry.names = names
        return carry

    def gather1_done(self, carry, results):
        self.level1.update(zip(carry.names, results))

    def gather2(self):
        return _carry_gather2([self.level1[n] for n in self.later])

    def weights(self, full):
        return _assemble_weights(dict(zip(self.later, full)))

    def sibling(self, grads):
        slabs = _grad_slabs(grads)
        self.slabs.update(slabs)
        carry = _carry_sibling(list(slabs.values()))
        carry.names = list(slabs)
        return carry

    def sibling_done(self, carry, results):
        self.from_sib.update(zip(carry.names, results))

    def chips(self, names, table=None):
        sums = {n: _sum_pairs(self.slabs[n], self.from_sib[n], self.pos, name="sum_sibling_" + n) for n in names}
        self.sums32.update({n: s32 for n, (s32, _) in sums.items()})
        carry = _carry_chips([s16 for _, s16 in sums.values()], None if table is None else self.table_part(table))
        carry.names, carry.table = list(names), table
        return carry

    def chips_done(self, carry, results):
        if carry.table is not None:
            *results, self.tables[carry.table] = results
        self.reduced.update({n: (self.sums32[n], r) for n, r in zip(carry.names, results)})

    def sibling_small(self, small_g):
        self.small_g = small_g
        return _carry_sibling([], small_g)

    def sibling_small_done(self, small_sib):
        self.small_chip = _pair_sum_small(self.small_g, small_sib, name="sum_sibling_small")

    def table_part(self, which):
        off, rows = self.W_S_ROWS
        if which == "w_s":
            return self.small_chip[off:off + rows]
        return jnp.concatenate([self.small_chip[:off], self.small_chip[off + rows:]], axis=0)

    def table(self):
        off = self.W_S_ROWS[0]
        rest = self.tables["rest"]
        return jnp.concatenate([rest[:, :off], self.tables["w_s"], rest[:, off:]], axis=1)


def _local_step(x, p, target, w, sm, ex=None):
    s = x.shape[0]
    mm = _matmul
    wt_main = w["wt_main"]
    conv_b = sm["conv_b"]
    bs_t = jnp.pad(sm["gmlp_b_s"].T, ((0, 0), (0, 128 - GROUPS)))
    b_f = jnp.pad(sm["b_f"], ((0, 0), (0, 128 - HEADS)))
    big = dict(tm=1024, tn=1024, tk=1024)
    whole_s = dict(tn=1024, tk=s)

    h = _rmsnorm_fwd(x, sm["norm_mix_g"], name="norm_mix")
    tall = dict(tm=s, tn=512, tk=1024)
    qkv_args = dict(mode="nt", out_dtype=BF16, name="in_qkv", n=3072, b_off=8, **tall)
    f_logit = mm(h, w["wt_f"], mode="nt", out_dtype=F32, name="in_f", tm=1024, tk=1024)
    cqe = _forget_cumsum(f_logit, b_f, name="forget_cumsum")
    uvg = dict(mode="nt", out_dtype=F32, name="in_uvg", n=4096, **tall)
    if ex is None:
        qkv = mm(h, wt_main, **qkv_args)
        (qa, ka, vt), _ = _attn_prep(qkv, cqe, name="attn_prep")
        (b, lse3), _ = _attn_fwd(qa, ka, vt, name="attn_fwd")
        zuvg = mm(h, wt_main, **uvg)
    else:
        groups = (["w_branch_a"], ["w_branch_b"], [n for n in ex.later if n not in ("w_branch_a", "w_branch_b")])
        carries = [ex.gather1(names) for names in groups]
        qkv, got0 = mm(h, wt_main, carry=carries[0], **qkv_args)
        (qa, ka, vt), got1 = _attn_prep(qkv, cqe, carries[1], name="attn_prep")
        (b, lse3), got2 = _attn_fwd(qa, ka, vt, carries[2], name="attn_fwd")
        for carry, got in zip(carries, (got0, got1, got2)):
            ex.gather1_done(carry, got)
        zuvg, full = mm(h, wt_main, carry=ex.gather2(), **uvg)
        w = {**w, **ex.weights(full)}
    a = _gmlp_fwd(zuvg, sm["gmlp_ln_g"], sm["gmlp_ln_b"], sm["gmlp_w_s"], bs_t, name="gmlp_fwd")
    wt_up, conv_w = w["wt_up"], w["conv_w"]
    ya, yb, merged = _branches_merge(a, b, w["w_a"], w["w_b"], zuvg, name="branches_merge")
    x1, h2 = mm(merged, w["w_out"], mode="nn", out_dtype=F32, name="out_proj", add=x, norm_g=sm["norm_ffn_g"], **big)
    up_a, up_g, act = _up_convglu(h2, wt_up, conv_w, conv_b, name="up_convglu")
    x2, h3 = mm(act, w["w_down"], mode="nn", out_dtype=F32, name="down", tm=1024, tn=1024, tk=1408, add=x1,
                norm_g=sm["norm_ple_g"])

    loss, dx3, dple, dgp, d_norm_final = _ple_loss(p, w["wt_ple"], h3, w["w_pg"], x2, target, sm["norm_final_g"],
                                                   name="ple_loss")
    g_wt_ple = mm(dple, p, mode="tn", out_dtype=BF16, name="d_w_ple", tm=512, tn=256, tk=s)
    g_w_pg = mm(h3, dgp, mode="tn", out_dtype=BF16, name="d_w_pg", tm=512, **whole_s)
    (dx2, dx2b, d_norm_ple), _ = _matmul_rmsnorm_bwd([dgp], w["w_pg"], dx3, x2, sm["norm_ple_g"], mode="nt", tk=1024,
                                                     name="d_h3_norm_ple_bwd")
    g_w_down = mm(act, dx2b, mode="tn", out_dtype=BF16, name="d_w_down", tm=1408, **whole_s)
    dact_args = dict(mode="nt", out_dtype=BF16, name="d_act", tm=s, tn=256, tk=1024)
    if ex is None:
        dact = mm(dx2b, w["w_down"], **dact_args)
    else:
        early = ex.sibling(dict(w_pg=g_w_pg, wt_ple=g_wt_ple))
        dact, got = mm(dx2b, w["w_down"], carry=early, **dact_args)
        ex.sibling_done(early, got)
    dup_a, dup_g, dcw_a, dcw_g, dcb_a, dcb_g = _convglu_bwd(dact, up_a, up_g, conv_w, conv_b, name="convglu_bwd")
    g_wt_up = mm(dup_a, h2, mode="tn", out_dtype=BF16, name="d_w_up_a", tm=1408, out_rows=2 * D_FF, **whole_s)
    g_wt_up = mm(dup_g, h2, mode="tn", out_dtype=BF16, name="d_w_up_g", tm=1408, out_rows=2 * D_FF,
                 o_off=D_FF // 1408, into=g_wt_up, **whole_s)
    (dx1, dx1b, d_norm_ffn), _ = _matmul_rmsnorm_bwd([dup_a, dup_g], wt_up, dx2, x1, sm["norm_ffn_g"], mode="nn",
                                                     tk=1408, name="d_h2_norm_ffn_bwd", resident=True)
    g_w_out = mm(merged, dx1b, mode="tn", out_dtype=BF16, name="d_w_out", tm=512, **whole_s)
    dya, dyb, dga, dgb = _merge_bwd(dx1b, w["w_out"], ya, yb, zuvg, name="merge_bwd")
    g_w_a = mm(a, dya, mode="tn", out_dtype=BF16, name="d_w_a", tm=512, **whole_s)
    g_w_b = mm(b, dyb, mode="tn", out_dtype=BF16, name="d_w_b", tm=512, **whole_s)
    da = mm(dya, w["w_a"], mode="nt", out_dtype=BF16, name="d_a", **big)
    db = mm(dyb, w["w_b"], mode="nt", out_dtype=BF16, name="d_b", **big)
    grads = dict(w_a=g_w_a, w_b=g_w_b, w_out=g_w_out, wt_up=g_wt_up, conv_w=jnp.concatenate([dcw_a, dcw_g], axis=1),
                 w_down=g_w_down, wt_ple=g_wt_ple, w_pg=g_w_pg)
    gmlp_args = (da, zuvg, sm["gmlp_ln_g"], sm["gmlp_ln_b"], sm["gmlp_w_s"], bs_t)
    if ex is None:
        (dzu, dzv, d_w_s, d_bs_t, d_ln_g, d_ln_b), _ = _gmlp_bwd(*gmlp_args, name="gmlp_bwd")
    else:
        rest = ex.sibling({k: v for k, v in grads.items() if k not in ("w_pg", "wt_ple")})
        early_chips = ex.chips(early.names)
        both = _carry_join(rest, early_chips)
        (dzu, dzv, d_w_s, d_bs_t, d_ln_g, d_ln_b), got = _gmlp_bwd(*gmlp_args, both, name="gmlp_bwd")
        got_rest, got_early = both.split(got)
        ex.sibling_done(rest, got_rest)
        ex.chips_done(early_chips, got_early)
    small = dict(norm_mix_g=jnp.zeros((1, D_MODEL), F32), b_f=jnp.zeros((1, HEADS), F32), gmlp_ln_g=d_ln_g,
                 gmlp_ln_b=d_ln_b, gmlp_w_s=d_w_s, gmlp_b_s=d_bs_t[:, :GROUPS].T, norm_ffn_g=d_norm_ffn,
                 conv_b=jnp.concatenate([dcb_a, dcb_g], axis=1), norm_ple_g=d_norm_ple, norm_final_g=d_norm_final)
    if ex is None:
        delta3, _ = _attn_delta(db, b, name="attn_delta")
        (dq, dk, dv, aux, dcq3), _ = _attn_bwd(qa, ka, qkv, db, lse3, delta3, name="attn_bwd")
    else:
        delta3, (small_sib,) = _attn_delta(db, b, ex.sibling_small(_pack_small(small)), name="attn_delta")
        ex.sibling_small_done(small_sib)
        main_chips = ex.chips(rest.names, table="rest")
        (dq, dk, dv, aux, dcq3), got = _attn_bwd(qa, ka, qkv, db, lse3, delta3, main_chips, name="attn_bwd")
        ex.chips_done(main_chips, got)
    dcq16 = jnp.pad(dcq3[:, :2, :].reshape(HEADS, s).T, ((0, 0), (0, 128 - HEADS)))
    dzf, d_b_f = _forget_bwd(dcq16, aux, f_logit, b_f, name="forget_bwd")
    dz_parts = [dzu, dzv, dga, dgb, dq, dk, dv]
    w_s_chips = None if ex is None else ex.chips([], table="w_s")
    g_wt_main, got = _grad_w_parts(dz_parts, h, name="d_w_main", tm=512, carry=w_s_chips)
    if ex is not None:
        ex.chips_done(w_s_chips, got)
    g_wt_f = mm(dzf, h, mode="tn", out_dtype=BF16, name="d_w_f", **whole_s)
    grads = dict(grads, wt_main=g_wt_main, wt_f=g_wt_f)
    w_in_chips = None
    if ex is not None:
        w_in_sib = ex.sibling(dict(wt_main=g_wt_main, wt_f=g_wt_f))
        ex.sibling_done(w_in_sib, _run_carry(w_in_sib, name="exchange_sibling_w_in"))
        w_in_chips = ex.chips(w_in_sib.names)
    (dx0, _, d_norm_mix), got = _matmul_rmsnorm_bwd(dz_parts, wt_main, dx1, x, sm["norm_mix_g"], mode="nn", tk=1024,
                                                    extra=(dzf, w["wt_f"]), name="d_h_norm_mix_bwd", carry=w_in_chips,
                                                    lead=True)
    if ex is not None:
        ex.chips_done(w_in_chips, got)
    return loss, dx0, grads, dict(small, norm_mix_g=d_norm_mix, b_f=d_b_f[:, :HEADS])


def kernel(x, p, norm_mix_g, w_in, b_f, gmlp_ln_g, gmlp_ln_b, gmlp_w_s, gmlp_b_s, w_branch_a, w_branch_b, w_out, norm_ffn_g, w_up, conv_w, conv_b, w_down, norm_ple_g, w_ple, w_ple_gate, norm_final_g, loss_target, m_norm_mix_g, m_w_in, m_b_f, m_gmlp_ln_g, m_gmlp_ln_b, m_gmlp_w_s, m_gmlp_b_s, m_w_branch_a, m_w_branch_b, m_w_out, m_norm_ffn_g, m_w_up, m_conv_w, m_conv_b, m_w_down, m_norm_ple_g, m_w_ple, m_w_ple_gate, m_norm_final_g, v_norm_mix_g, v_w_in, v_b_f, v_gmlp_ln_g, v_gmlp_ln_b, v_gmlp_w_s, v_gmlp_b_s, v_w_branch_a, v_w_branch_b, v_w_out, v_norm_ffn_g, v_w_up, v_conv_w, v_conv_b, v_w_down, v_norm_ple_g, v_w_ple, v_w_ple_gate, v_norm_final_g):
    given = dict(locals())
    weights = {n: given[n] for n in WEIGHT_ORDER}
    mom_m = {n: given["m_" + n] for n in WEIGHT_ORDER}
    mom_v = {n: given["v_" + n] for n in WEIGHT_ORDER}
    pos = jnp.stack([lax.axis_index("x"), lax.axis_index("y"), lax.axis_index("c")]).astype(I32)
    names = [n for n, _ in SHARDED]
    kinds = dict(SHARDED)

    later = [n for n in names if n != "w_in"]

    first = _allgather([_to_comm("w_in", kinds["w_in"], weights["w_in"])], name="allgather_w_in")
    ex = _Exchanges(later, [_to_comm(n, kinds[n], weights[n]) for n in later], pos)

    sm = dict(norm_mix_g=norm_mix_g, b_f=b_f, gmlp_ln_g=gmlp_ln_g, gmlp_ln_b=gmlp_ln_b, gmlp_w_s=gmlp_w_s[0],
              gmlp_b_s=gmlp_b_s[0], norm_ffn_g=norm_ffn_g, conv_b=conv_b, norm_ple_g=norm_ple_g,
              norm_final_g=norm_final_g.reshape(1, D_MODEL))
    loss_part, dx0, grads, small = _local_step(
        x[0], p[0, 0], loss_target[0], _assemble_weights({"w_in": first[0]}), sm, ex)

    b_f_and_loss = jnp.concatenate([small["b_f"].reshape(-1), loss_part[0, :1]])
    last = _allreduce_rows(jnp.concatenate([_rows(small["norm_mix_g"], 8), _rows(b_f_and_loss, 8)], axis=0),
                           name="allreduce_last")
    loss = last[8, HEADS]
    small_last = jnp.pad(last, ((0, SMALL_ROWS - 16), (0, 0)))

    grad, delta, new_m, new_v = {}, {}, {}, {}
    for n in names:
        s32, r = ex.reduced[n]
        outs = _adam_sharded(s32, r, *[_to_comm(n, kinds[n], src[n], F32) for src in (weights, mom_m, mom_v)], pos,
                             name="adam_" + n)
        grad[n], delta[n], new_m[n], new_v[n] = [_from_comm(n, kinds[n], o) for o in outs]
    replicated = [n for n, _ in SMALL]
    rep = lambda src: _pack_small({n: src[n] for n in replicated})
    packed = _adam_replicated(ex.table(), small_last, rep(weights), rep(mom_m), rep(mom_v), name="adam_replicated")
    for out, pk in zip((grad, delta, new_m, new_v), packed):
        for n in replicated:
            out[n] = _small(pk, n, weights[n].shape)

    return (loss, dx0, *[grad[n] for n in WEIGHT_ORDER], *[delta[n] for n in WEIGHT_ORDER],
            *[new_m[n] for n in WEIGHT_ORDER], *[new_v[n] for n in WEIGHT_ORDER])
```
